```python
import jax, jax.numpy as jnp
from jax import lax
import numpy as np

D_MODEL = 1024
BATCH = 8
SEQ = 2048
DEPTH = 2

CHUNK = 64
Q_BLOCK = 128
N_A = DEPTH // 2
N_B = DEPTH - N_A
CONV_W = 3
N_HEADS = 8
QK_NOPE = 128
QK_ROPE = 64
V_HEAD = 128
Q_LORA = 384
KV_LORA = 256
D_FF = 2816
ROPE_THETA = 10000.0
EPS = 1e-6
NEG_INF = -1e30
MAX_POS_OFFSET = 8192

kernel_name = 'yoco_shortconv_mla_convffn'


def rms_norm(x, g):
    xf = x.astype(jnp.float32)
    y = xf * lax.rsqrt(jnp.mean(xf * xf, axis=-1, keepdims=True) + EPS)
    return (y * g.astype(jnp.float32)).astype(x.dtype)


def causal_dwconv(x, w):
    s = x.shape[1]
    xp = jnp.pad(x, ((0, 0), (CONV_W - 1, 0), (0, 0)))
    y = xp[:, 0:s, :] * w[0]
    for j in range(1, CONV_W):
        y = y + xp[:, j:j + s, :] * w[j]
    return y


def rope(x, positions):
    half = QK_ROPE // 2
    inv_freq = 1.0 / (ROPE_THETA ** (jnp.arange(half, dtype=jnp.float32) / half))
    ang = positions.astype(jnp.float32)[..., None] * inv_freq
    cos, sin = jnp.cos(ang), jnp.sin(ang)
    if x.ndim == 4:
        cos, sin = cos[:, :, None, :], sin[:, :, None, :]
    x1 = x[..., :half].astype(jnp.float32)
    x2 = x[..., half:].astype(jnp.float32)
    out = jnp.concatenate([x1 * cos - x2 * sin, x2 * cos + x1 * sin], axis=-1)
    return out.astype(x.dtype)


def short_conv_mixer(h, w_in, conv_w, w_out):
    b_gate, c_gate, u = jnp.split(h @ w_in, 3, axis=-1)
    return (b_gate * causal_dwconv(c_gate * u, conv_w)) @ w_out


def conv_ffn(h, w_up, conv_w, conv_b, w_down):
    g, v = jnp.split(h @ w_up, 2, axis=-1)
    g = causal_dwconv(g, conv_w) + conv_b
    return (jax.nn.silu(g) * v) @ w_down


def shared_kv(h, kv_in_norm, w_dkv, kv_latent_norm, w_kr, w_uk, w_uv, positions):
    b, s, _ = h.shape
    hn = rms_norm(h, kv_in_norm)
    c_kv = rms_norm(hn @ w_dkv, kv_latent_norm)
    k_rope = rope(hn @ w_kr, positions)
    k_nope = (c_kv @ w_uk).reshape(b, s, N_HEADS, QK_NOPE)
    v = (c_kv @ w_uv).reshape(b, s, N_HEADS, V_HEAD)
    return k_nope, k_rope, v


def mla_attention(h, w_dq, q_latent_norm, w_uq, w_o, k_nope, k_rope, v, positions):
    b, s, _ = h.shape
    c_q = rms_norm(h @ w_dq, q_latent_norm)
    q = (c_q @ w_uq).reshape(b, s, N_HEADS, QK_NOPE + QK_ROPE)
    q_nope = q[..., :QK_NOPE]
    q_rope = rope(q[..., QK_NOPE:], positions)
    scale = (QK_NOPE + QK_ROPE) ** -0.5
    nb = s // Q_BLOCK
    qn_blocks = q_nope.reshape(b, nb, Q_BLOCK, N_HEADS, QK_NOPE).transpose(1, 0, 2, 3, 4)
    qr_blocks = q_rope.reshape(b, nb, Q_BLOCK, N_HEADS, QK_ROPE).transpose(1, 0, 2, 3, 4)
    key_chunk = jnp.arange(s) // CHUNK

    def attend_block(args):
        qn, qr, blk = args
        sc = (jnp.einsum('bqhd,bkhd->bhqk', qn, k_nope)
              + jnp.einsum('bqhr,bkr->bhqk', qr, k_rope)).astype(jnp.float32) * scale
        q_chunk = (blk * Q_BLOCK + jnp.arange(Q_BLOCK)) // CHUNK
        mask = key_chunk[None, :] <= q_chunk[:, None]
        sc = jnp.where(mask[None, None], sc, NEG_INF)
        p = jax.nn.softmax(sc, axis=-1).astype(v.dtype)
        return jnp.einsum('bhqk,bkhd->bqhd', p, v)

    o = lax.map(attend_block, (qn_blocks, qr_blocks, jnp.arange(nb)))
    o = o.transpose(1, 0, 2, 3, 4).reshape(b, s, N_HEADS * V_HEAD)
    return o @ w_o


def _fwd_setup_inputs(seed: int = 0) -> dict:
    key = jax.random.key(seed)
    ks = jax.random.split(key, 32)
    f32 = jnp.float32
    resid = (2 * DEPTH) ** -0.5

    def w(k, shape, fan_in, extra=1.0):
        return jax.random.normal(k, shape, f32) * (fan_in ** -0.5) * extra

    def gain(k, shape):
        return 1.0 + 0.02 * jax.random.normal(k, shape, f32)

    x = jax.random.normal(ks[0], (BATCH, SEQ, D_MODEL), f32)
    offsets = jax.random.randint(ks[1], (BATCH, 1), 0, MAX_POS_OFFSET, dtype=jnp.int32)
    positions = offsets + jnp.arange(SEQ, dtype=jnp.int32)[None, :]
    return {
        'x': x,
        'positions': positions,
        'attn_norm': gain(ks[2], (DEPTH, D_MODEL)),
        'ffn_norm': gain(ks[3], (DEPTH, D_MODEL)),
        'final_norm': gain(ks[4], (D_MODEL,)),
        'sc_w_in': w(ks[5], (N_A, D_MODEL, 3 * D_MODEL), D_MODEL),
        'sc_conv_w': w(ks[6], (N_A, CONV_W, D_MODEL), CONV_W),
        'sc_w_out': w(ks[7], (N_A, D_MODEL, D_MODEL), D_MODEL, resid),
        'kv_in_norm': gain(ks[8], (D_MODEL,)),
        'w_dkv': w(ks[9], (D_MODEL, KV_LORA), D_MODEL),
        'kv_latent_norm': gain(ks[10], (KV_LORA,)),
        'w_kr': w(ks[11], (D_MODEL, QK_ROPE), D_MODEL),
        'w_uk': w(ks[12], (KV_LORA, N_HEADS * QK_NOPE), KV_LORA),
        'w_uv': w(ks[13], (KV_LORA, N_HEADS * V_HEAD), KV_LORA),
        'w_dq': w(ks[14], (N_B, D_MODEL, Q_LORA), D_MODEL),
        'q_latent_norm': gain(ks[15], (N_B, Q_LORA)),
        'w_uq': w(ks[16], (N_B, Q_LORA, N_HEADS * (QK_NOPE + QK_ROPE)), Q_LORA),
        'w_o': w(ks[17], (N_B, N_HEADS * V_HEAD, D_MODEL), N_HEADS * V_HEAD, resid),
        'ffn_w_up': w(ks[18], (DEPTH, D_MODEL, 2 * D_FF), D_MODEL),
        'ffn_conv_w': w(ks[19], (DEPTH, CONV_W, D_FF), CONV_W),
        'ffn_conv_b': 0.02 * jax.random.normal(ks[20], (DEPTH, D_FF), f32),
        'ffn_w_down': w(ks[21], (DEPTH, D_FF, D_MODEL), D_FF, resid),
    }


def _fwd_reference(x, positions, attn_norm, ffn_norm, final_norm, sc_w_in, sc_conv_w, sc_w_out,
              kv_in_norm, w_dkv, kv_latent_norm, w_kr, w_uk, w_uv,
              w_dq, q_latent_norm, w_uq, w_o,
              ffn_w_up, ffn_conv_w, ffn_conv_b, ffn_w_down):
    h = x
    kv = None
    for layer in range(DEPTH):
        hn = rms_norm(h, attn_norm[layer])
        if layer < N_A:
            h = h + short_conv_mixer(hn, sc_w_in[layer], sc_conv_w[layer], sc_w_out[layer])
        else:
            i = layer - N_A
            k_nope, k_rope, v = kv
            h = h + mla_attention(hn, w_dq[i], q_latent_norm[i], w_uq[i], w_o[i],
                                  k_nope, k_rope, v, positions)
        h = h + conv_ffn(rms_norm(h, ffn_norm[layer]), ffn_w_up[layer], ffn_conv_w[layer],
                         ffn_conv_b[layer], ffn_w_down[layer])
        if layer == N_A - 1:
            kv = shared_kv(h, kv_in_norm, w_dkv, kv_latent_norm, w_kr, w_uk, w_uv, positions)
    return rms_norm(h, final_norm)


import jax as _jax
import jax.numpy as _jnp

TWIN_FORMAT = 'train_step'
FWD_PARAMS = ['x', 'positions', 'attn_norm', 'ffn_norm', 'final_norm', 'sc_w_in', 'sc_conv_w', 'sc_w_out', 'kv_in_norm', 'w_dkv', 'kv_latent_norm', 'w_kr', 'w_uk', 'w_uv', 'w_dq', 'q_latent_norm', 'w_uq', 'w_o', 'ffn_w_up', 'ffn_conv_w', 'ffn_conv_b', 'ffn_w_down']
TWIN_WEIGHTS = ['attn_norm', 'ffn_norm', 'final_norm', 'sc_w_in', 'sc_conv_w', 'sc_w_out', 'kv_in_norm', 'w_dkv', 'kv_latent_norm', 'w_kr', 'w_uk', 'w_uv', 'w_dq', 'q_latent_norm', 'w_uq', 'w_o', 'ffn_w_up', 'ffn_conv_w', 'ffn_conv_b', 'ffn_w_down']
TWIN_DIFF_INPUT = 'x'
TWIN_INPUTS = ['x', 'positions', 'attn_norm', 'ffn_norm', 'final_norm', 'sc_w_in', 'sc_conv_w', 'sc_w_out', 'kv_in_norm', 'w_dkv', 'kv_latent_norm', 'w_kr', 'w_uk', 'w_uv', 'w_dq', 'q_latent_norm', 'w_uq', 'w_o', 'ffn_w_up', 'ffn_conv_w', 'ffn_conv_b', 'ffn_w_down', 'loss_target', 'm_attn_norm', 'm_ffn_norm', 'm_final_norm', 'm_sc_w_in', 'm_sc_conv_w', 'm_sc_w_out', 'm_kv_in_norm', 'm_w_dkv', 'm_kv_latent_norm', 'm_w_kr', 'm_w_uk', 'm_w_uv', 'm_w_dq', 'm_q_latent_norm', 'm_w_uq', 'm_w_o', 'm_ffn_w_up', 'm_ffn_conv_w', 'm_ffn_conv_b', 'm_ffn_w_down', 'v_attn_norm', 'v_ffn_norm', 'v_final_norm', 'v_sc_w_in', 'v_sc_conv_w', 'v_sc_w_out', 'v_kv_in_norm', 'v_w_dkv', 'v_kv_latent_norm', 'v_w_kr', 'v_w_uk', 'v_w_uv', 'v_w_dq', 'v_q_latent_norm', 'v_w_uq', 'v_w_o', 'v_ffn_w_up', 'v_ffn_conv_w', 'v_ffn_conv_b', 'v_ffn_w_down']
TWIN_OUTPUTS = ['loss', 'grad_x', 'grad_attn_norm', 'grad_ffn_norm', 'grad_final_norm', 'grad_sc_w_in', 'grad_sc_conv_w', 'grad_sc_w_out', 'grad_kv_in_norm', 'grad_w_dkv', 'grad_kv_latent_norm', 'grad_w_kr', 'grad_w_uk', 'grad_w_uv', 'grad_w_dq', 'grad_q_latent_norm', 'grad_w_uq', 'grad_w_o', 'grad_ffn_w_up', 'grad_ffn_conv_w', 'grad_ffn_conv_b', 'grad_ffn_w_down', 'delta_attn_norm', 'delta_ffn_norm', 'delta_final_norm', 'delta_sc_w_in', 'delta_sc_conv_w', 'delta_sc_w_out', 'delta_kv_in_norm', 'delta_w_dkv', 'delta_kv_latent_norm', 'delta_w_kr', 'delta_w_uk', 'delta_w_uv', 'delta_w_dq', 'delta_q_latent_norm', 'delta_w_uq', 'delta_w_o', 'delta_ffn_w_up', 'delta_ffn_conv_w', 'delta_ffn_conv_b', 'delta_ffn_w_down', 'new_m_attn_norm', 'new_m_ffn_norm', 'new_m_final_norm', 'new_m_sc_w_in', 'new_m_sc_conv_w', 'new_m_sc_w_out', 'new_m_kv_in_norm', 'new_m_w_dkv', 'new_m_kv_latent_norm', 'new_m_w_kr', 'new_m_w_uk', 'new_m_w_uv', 'new_m_w_dq', 'new_m_q_latent_norm', 'new_m_w_uq', 'new_m_w_o', 'new_m_ffn_w_up', 'new_m_ffn_conv_w', 'new_m_ffn_conv_b', 'new_m_ffn_w_down', 'new_v_attn_norm', 'new_v_ffn_norm', 'new_v_final_norm', 'new_v_sc_w_in', 'new_v_sc_conv_w', 'new_v_sc_w_out', 'new_v_kv_in_norm', 'new_v_w_dkv', 'new_v_kv_latent_norm', 'new_v_w_kr', 'new_v_w_uk', 'new_v_w_uv', 'new_v_w_dq', 'new_v_q_latent_norm', 'new_v_w_uq', 'new_v_w_o', 'new_v_ffn_w_up', 'new_v_ffn_conv_w', 'new_v_ffn_conv_b', 'new_v_ffn_w_down']
TWIN_LEAF_KINDS = {'loss': 'loss', 'grad_x': 'grad_x', 'grad_attn_norm': 'grad_w', 'grad_ffn_norm': 'grad_w', 'grad_final_norm': 'grad_w', 'grad_sc_w_in': 'grad_w', 'grad_sc_conv_w': 'grad_w', 'grad_sc_w_out': 'grad_w', 'grad_kv_in_norm': 'grad_w', 'grad_w_dkv': 'grad_w', 'grad_kv_latent_norm': 'grad_w', 'grad_w_kr': 'grad_w', 'grad_w_uk': 'grad_w', 'grad_w_uv': 'grad_w', 'grad_w_dq': 'grad_w', 'grad_q_latent_norm': 'grad_w', 'grad_w_uq': 'grad_w', 'grad_w_o': 'grad_w', 'grad_ffn_w_up': 'grad_w', 'grad_ffn_conv_w': 'grad_w', 'grad_ffn_conv_b': 'grad_w', 'grad_ffn_w_down': 'grad_w', 'delta_attn_norm': 'delta_w', 'delta_ffn_norm': 'delta_w', 'delta_final_norm': 'delta_w', 'delta_sc_w_in': 'delta_w', 'delta_sc_conv_w': 'delta_w', 'delta_sc_w_out': 'delta_w', 'delta_kv_in_norm': 'delta_w', 'delta_w_dkv': 'delta_w', 'delta_kv_latent_norm': 'delta_w', 'delta_w_kr': 'delta_w', 'delta_w_uk': 'delta_w', 'delta_w_uv': 'delta_w', 'delta_w_dq': 'delta_w', 'delta_q_latent_norm': 'delta_w', 'delta_w_uq': 'delta_w', 'delta_w_o': 'delta_w', 'delta_ffn_w_up': 'delta_w', 'delta_ffn_conv_w': 'delta_w', 'delta_ffn_conv_b': 'delta_w', 'delta_ffn_w_down': 'delta_w', 'new_m_attn_norm': 'new_m', 'new_m_ffn_norm': 'new_m', 'new_m_final_norm': 'new_m', 'new_m_sc_w_in': 'new_m', 'new_m_sc_conv_w': 'new_m', 'new_m_sc_w_out': 'new_m', 'new_m_kv_in_norm': 'new_m', 'new_m_w_dkv': 'new_m', 'new_m_kv_latent_norm': 'new_m', 'new_m_w_kr': 'new_m', 'new_m_w_uk': 'new_m', 'new_m_w_uv': 'new_m', 'new_m_w_dq': 'new_m', 'new_m_q_latent_norm': 'new_m', 'new_m_w_uq': 'new_m', 'new_m_w_o': 'new_m', 'new_m_ffn_w_up': 'new_m', 'new_m_ffn_conv_w': 'new_m', 'new_m_ffn_conv_b': 'new_m', 'new_m_ffn_w_down': 'new_m', 'new_v_attn_norm': 'new_v', 'new_v_ffn_norm': 'new_v', 'new_v_final_norm': 'new_v', 'new_v_sc_w_in': 'new_v', 'new_v_sc_conv_w': 'new_v', 'new_v_sc_w_out': 'new_v', 'new_v_kv_in_norm': 'new_v', 'new_v_w_dkv': 'new_v', 'new_v_kv_latent_norm': 'new_v', 'new_v_w_kr': 'new_v', 'new_v_w_uk': 'new_v', 'new_v_w_uv': 'new_v', 'new_v_w_dq': 'new_v', 'new_v_q_latent_norm': 'new_v', 'new_v_w_uq': 'new_v', 'new_v_w_o': 'new_v', 'new_v_ffn_w_up': 'new_v', 'new_v_ffn_conv_w': 'new_v', 'new_v_ffn_conv_b': 'new_v', 'new_v_ffn_w_down': 'new_v'}


def _forward(args):
    return _fwd_reference(*[args[k] for k in FWD_PARAMS])


def _output_shape():
    out = _jax.eval_shape(lambda: _forward(_fwd_setup_inputs(0)))
    return out.shape, out.dtype

N_MICROBATCH = 1
ADAM_LR = 0.001
ADAM_B1 = 0.9
ADAM_B2 = 0.999
ADAM_EPS = 1e-08
ADAM_WD = 0.01
ADAM_STEP = 10
PER_EXAMPLE_BATCH_AXIS = {'x': 0, 'positions': 0, 'loss_target': 0}
SHARED_INPUTS = []
_WEIGHT_DTYPES = {'attn_norm': _jnp.float32, 'ffn_norm': _jnp.float32, 'final_norm': _jnp.float32, 'sc_w_in': _jnp.float32, 'sc_conv_w': _jnp.float32, 'sc_w_out': _jnp.float32, 'kv_in_norm': _jnp.float32, 'w_dkv': _jnp.float32, 'kv_latent_norm': _jnp.float32, 'w_kr': _jnp.float32, 'w_uk': _jnp.float32, 'w_uv': _jnp.float32, 'w_dq': _jnp.float32, 'q_latent_norm': _jnp.float32, 'w_uq': _jnp.float32, 'w_o': _jnp.float32, 'ffn_w_up': _jnp.float32, 'ffn_conv_w': _jnp.float32, 'ffn_conv_b': _jnp.float32, 'ffn_w_down': _jnp.float32}
MOMENT_SCALE = {'attn_norm': 7.461600e-02, 'ffn_norm': 4.770554e-02, 'final_norm': 1.598464e+01, 'sc_w_in': 5.966827e-02, 'sc_conv_w': 6.065669e-02, 'sc_w_out': 1.192020e-01, 'kv_in_norm': 1.147131e-02, 'w_dkv': 2.106404e-02, 'kv_latent_norm': 2.388956e-02, 'w_kr': 1.813419e-02, 'w_uk': 7.001924e-03, 'w_uv': 8.545273e-03, 'w_dq': 1.362261e-02, 'q_latent_norm': 1.286016e-02, 'w_uq': 6.818112e-03, 'w_o': 1.723829e-02, 'ffn_w_up': 2.011391e-02, 'ffn_conv_w': 2.070358e-02, 'ffn_conv_b': 1.984990e-02, 'ffn_w_down': 6.598664e-02}


def _to_microbatches(a, axis):
    t = _jnp.moveaxis(a, axis, 0)
    t = t.reshape((N_MICROBATCH, t.shape[0] // N_MICROBATCH) + t.shape[1:])
    return _jnp.moveaxis(t, 1, axis + 1)


def setup_inputs(seed: int = 0) -> dict:
    inp = _fwd_setup_inputs(seed)
    key = _jax.random.fold_in(_jax.random.key(seed), 7919)
    shape, _ = _output_shape()
    out = dict(inp)
    out["loss_target"] = _jax.random.normal(_jax.random.fold_in(key, 0), shape, _jnp.float32)
    for i, name in enumerate(TWIN_WEIGHTS):
        w = inp[name].astype(_jnp.float32)
        if MOMENT_SCALE is None:
            s = _jnp.sqrt(_jnp.mean(_jnp.square(w)) + 1e-30)
        else:
            s = MOMENT_SCALE[name]
        km, kv = _jax.random.split(_jax.random.fold_in(key, i + 1))
        out[name] = w
        out["m_" + name] = s * _jax.random.normal(km, w.shape, _jnp.float32)
        out["v_" + name] = (s * s) * _jax.random.uniform(kv, w.shape, _jnp.float32, 0.5, 1.5)
    if N_MICROBATCH > 1:
        for name, axis in PER_EXAMPLE_BATCH_AXIS.items():
            out[name] = _to_microbatches(out[name], axis)
    return {'x': out['x'], 'positions': out['positions'], 'attn_norm': out['attn_norm'], 'ffn_norm': out['ffn_norm'], 'final_norm': out['final_norm'], 'sc_w_in': out['sc_w_in'], 'sc_conv_w': out['sc_conv_w'], 'sc_w_out': out['sc_w_out'], 'kv_in_norm': out['kv_in_norm'], 'w_dkv': out['w_dkv'], 'kv_latent_norm': out['kv_latent_norm'], 'w_kr': out['w_kr'], 'w_uk': out['w_uk'], 'w_uv': out['w_uv'], 'w_dq': out['w_dq'], 'q_latent_norm': out['q_latent_norm'], 'w_uq': out['w_uq'], 'w_o': out['w_o'], 'ffn_w_up': out['ffn_w_up'], 'ffn_conv_w': out['ffn_conv_w'], 'ffn_conv_b': out['ffn_conv_b'], 'ffn_w_down': out['ffn_w_down'], 'loss_target': out['loss_target'], 'm_attn_norm': out['m_attn_norm'], 'm_ffn_norm': out['m_ffn_norm'], 'm_final_norm': out['m_final_norm'], 'm_sc_w_in': out['m_sc_w_in'], 'm_sc_conv_w': out['m_sc_conv_w'], 'm_sc_w_out': out['m_sc_w_out'], 'm_kv_in_norm': out['m_kv_in_norm'], 'm_w_dkv': out['m_w_dkv'], 'm_kv_latent_norm': out['m_kv_latent_norm'], 'm_w_kr': out['m_w_kr'], 'm_w_uk': out['m_w_uk'], 'm_w_uv': out['m_w_uv'], 'm_w_dq': out['m_w_dq'], 'm_q_latent_norm': out['m_q_latent_norm'], 'm_w_uq': out['m_w_uq'], 'm_w_o': out['m_w_o'], 'm_ffn_w_up': out['m_ffn_w_up'], 'm_ffn_conv_w': out['m_ffn_conv_w'], 'm_ffn_conv_b': out['m_ffn_conv_b'], 'm_ffn_w_down': out['m_ffn_w_down'], 'v_attn_norm': out['v_attn_norm'], 'v_ffn_norm': out['v_ffn_norm'], 'v_final_norm': out['v_final_norm'], 'v_sc_w_in': out['v_sc_w_in'], 'v_sc_conv_w': out['v_sc_conv_w'], 'v_sc_w_out': out['v_sc_w_out'], 'v_kv_in_norm': out['v_kv_in_norm'], 'v_w_dkv': out['v_w_dkv'], 'v_kv_latent_norm': out['v_kv_latent_norm'], 'v_w_kr': out['v_w_kr'], 'v_w_uk': out['v_w_uk'], 'v_w_uv': out['v_w_uv'], 'v_w_dq': out['v_w_dq'], 'v_q_latent_norm': out['v_q_latent_norm'], 'v_w_uq': out['v_w_uq'], 'v_w_o': out['v_w_o'], 'v_ffn_w_up': out['v_ffn_w_up'], 'v_ffn_conv_w': out['v_ffn_conv_w'], 'v_ffn_conv_b': out['v_ffn_conv_b'], 'v_ffn_w_down': out['v_ffn_w_down']}


def _loss(weights, diff, rest, loss_target):
    with _jax.named_scope("forward"):
        args = {**rest, TWIN_DIFF_INPUT: diff, **{k: w.astype(_WEIGHT_DTYPES[k]) for k, w in weights.items()}}
        y = _forward(args)
    with _jax.named_scope("loss_head"):
        err = _jnp.square(y.astype(_jnp.float32) - loss_target)
        return 0.5 * _jnp.sum(_jnp.mean(err, axis=-1)) if err.ndim else 0.5 * err


def _adamw(w, g, m, v):
    m = ADAM_B1 * m + (1.0 - ADAM_B1) * g
    v = ADAM_B2 * v + (1.0 - ADAM_B2) * _jnp.square(g)
    m_hat = m / (1.0 - ADAM_B1 ** ADAM_STEP)
    v_hat = v / (1.0 - ADAM_B2 ** ADAM_STEP)
    delta = -ADAM_LR * (m_hat / (_jnp.sqrt(v_hat) + ADAM_EPS) + ADAM_WD * w)
    return delta, m, v


def reference(x, positions, attn_norm, ffn_norm, final_norm, sc_w_in, sc_conv_w, sc_w_out, kv_in_norm, w_dkv, kv_latent_norm, w_kr, w_uk, w_uv, w_dq, q_latent_norm, w_uq, w_o, ffn_w_up, ffn_conv_w, ffn_conv_b, ffn_w_down, loss_target, m_attn_norm, m_ffn_norm, m_final_norm, m_sc_w_in, m_sc_conv_w, m_sc_w_out, m_kv_in_norm, m_w_dkv, m_kv_latent_norm, m_w_kr, m_w_uk, m_w_uv, m_w_dq, m_q_latent_norm, m_w_uq, m_w_o, m_ffn_w_up, m_ffn_conv_w, m_ffn_conv_b, m_ffn_w_down, v_attn_norm, v_ffn_norm, v_final_norm, v_sc_w_in, v_sc_conv_w, v_sc_w_out, v_kv_in_norm, v_w_dkv, v_kv_latent_norm, v_w_kr, v_w_uk, v_w_uv, v_w_dq, v_q_latent_norm, v_w_uq, v_w_o, v_ffn_w_up, v_ffn_conv_w, v_ffn_conv_b, v_ffn_w_down):
    given = dict(x=x, positions=positions, attn_norm=attn_norm, ffn_norm=ffn_norm, final_norm=final_norm, sc_w_in=sc_w_in, sc_conv_w=sc_conv_w, sc_w_out=sc_w_out, kv_in_norm=kv_in_norm, w_dkv=w_dkv, kv_latent_norm=kv_latent_norm, w_kr=w_kr, w_uk=w_uk, w_uv=w_uv, w_dq=w_dq, q_latent_norm=q_latent_norm, w_uq=w_uq, w_o=w_o, ffn_w_up=ffn_w_up, ffn_conv_w=ffn_conv_w, ffn_conv_b=ffn_conv_b, ffn_w_down=ffn_w_down, loss_target=loss_target, m_attn_norm=m_attn_norm, m_ffn_norm=m_ffn_norm, m_final_norm=m_final_norm, m_sc_w_in=m_sc_w_in, m_sc_conv_w=m_sc_conv_w, m_sc_w_out=m_sc_w_out, m_kv_in_norm=m_kv_in_norm, m_w_dkv=m_w_dkv, m_kv_latent_norm=m_kv_latent_norm, m_w_kr=m_w_kr, m_w_uk=m_w_uk, m_w_uv=m_w_uv, m_w_dq=m_w_dq, m_q_latent_norm=m_q_latent_norm, m_w_uq=m_w_uq, m_w_o=m_w_o, m_ffn_w_up=m_ffn_w_up, m_ffn_conv_w=m_ffn_conv_w, m_ffn_conv_b=m_ffn_conv_b, m_ffn_w_down=m_ffn_w_down, v_attn_norm=v_attn_norm, v_ffn_norm=v_ffn_norm, v_final_norm=v_final_norm, v_sc_w_in=v_sc_w_in, v_sc_conv_w=v_sc_conv_w, v_sc_w_out=v_sc_w_out, v_kv_in_norm=v_kv_in_norm, v_w_dkv=v_w_dkv, v_kv_latent_norm=v_kv_latent_norm, v_w_kr=v_w_kr, v_w_uk=v_w_uk, v_w_uv=v_w_uv, v_w_dq=v_w_dq, v_q_latent_norm=v_q_latent_norm, v_w_uq=v_w_uq, v_w_o=v_w_o, v_ffn_w_up=v_ffn_w_up, v_ffn_conv_w=v_ffn_conv_w, v_ffn_conv_b=v_ffn_conv_b, v_ffn_w_down=v_ffn_w_down)
    weights = {n: given[n] for n in TWIN_WEIGHTS}
    shared = {n: given[n] for n in SHARED_INPUTS}
    per_example = {n: given[n] for n in ['x', 'positions']}
    grad_fn = _jax.value_and_grad(_loss, argnums=(0, 1))

    def one_microbatch(ex, loss_target):
        ex = dict(ex)
        diff = ex.pop(TWIN_DIFF_INPUT)
        return grad_fn(weights, diff, {**shared, **ex}, loss_target)

    if N_MICROBATCH == 1:
        loss, (grad_w, grad_x) = one_microbatch(per_example, given["loss_target"])
    else:
        def body(carry, xs):
            loss_sum, grad_sum = carry
            l_k, (gw_k, gx_k) = one_microbatch(xs[0], xs[1])
            with _jax.named_scope("update"):
                return (loss_sum + l_k, _jax.tree.map(_jnp.add, grad_sum, gw_k)), gx_k

        init = (_jnp.zeros((), _jnp.float32), _jax.tree.map(_jnp.zeros_like, weights))
        (loss, grad_w), grad_x = _jax.lax.scan(body, init, (per_example, given["loss_target"]))
    with _jax.named_scope("update"):
        delta_w, new_m, new_v = {}, {}, {}
        for n in TWIN_WEIGHTS:
            delta_w[n], new_m[n], new_v[n] = _adamw(weights[n], grad_w[n], given["m_" + n], given["v_" + n])
    return (loss, grad_x, *[grad_w[n] for n in TWIN_WEIGHTS], *[delta_w[n] for n in TWIN_WEIGHTS],
            *[new_m[n] for n in TWIN_WEIGHTS], *[new_v[n] for n in TWIN_WEIGHTS])
```

```python
import functools

import jax
import jax.numpy as jnp
from jax import lax
from jax.experimental import pallas as pl
from jax.experimental.pallas import tpu as pltpu

F32 = jnp.float32
BF16 = jnp.bfloat16

T = 2048
D = 1024
N_HEADS = 8
QK_NOPE = 128
QK_ROPE = 64
V_HEAD = 128
Q_LORA = 384
KV_LORA = 256
D_FF = 2816
CHUNK = 64
ROPE_THETA = 10000.0
EPS = 1e-6
NEG_INF = -1e30
ADAM_LR = 0.001
ADAM_B1 = 0.9
ADAM_B2 = 0.999
ADAM_EPS = 1e-08
ADAM_WD = 0.01
ADAM_STEP = 10

N_DEV = 8
N_CHIP = 4
FF_BLK = D_FF * 2 // N_DEV
N_FF_BLK = D_FF // FF_BLK
QK_PAD = 256
HALO = 16

TM = 512
TR = 256
TQ = 256
VMEM_LIMIT = 56 * 1024 * 1024

NN = (((1,), (0,)), ((), ()))
NT = (((1,), (1,)), ((), ()))
TN = (((0,), (0,)), ((), ()))
MESH = pl.DeviceIdType.MESH


def _params(sem):
    return pltpu.CompilerParams(dimension_semantics=sem, vmem_limit_bytes=VMEM_LIMIT)


def _mm(name, a, b, *, grid, a_spec, b_spec, o_spec, o_shape, o_dtype, dims, k_axis=None, acc_shape=None,
        add=None, add_spec=None, alias=None):
    nk = grid[k_axis] if k_axis is not None else 1
    has_add = add is not None
    has_alias = alias is not None

    def body(*refs):
        a_ref, b_ref = refs[0], refs[1]
        p = 2
        add_ref = None
        if has_add:
            add_ref = refs[p]
            p += 1
        if has_alias:
            p += 1
        o_ref = refs[p]
        p += 1
        r = lax.dot_general(a_ref[...].astype(BF16), b_ref[...].astype(BF16), dims, preferred_element_type=F32)
        if k_axis is None:
            if has_add:
                r = r + add_ref[...].astype(F32)
            o_ref[...] = r.astype(o_dtype)
        else:
            acc = refs[p]
            k = pl.program_id(k_axis)

            @pl.when(k == 0)
            def _():
                acc[...] = r

            @pl.when(k > 0)
            def _():
                acc[...] += r

            @pl.when(k == nk - 1)
            def _():
                t = acc[...]
                if has_add:
                    t = t + add_ref[...].astype(F32)
                o_ref[...] = t.astype(o_dtype)

    in_specs = [a_spec, b_spec]
    args = [a, b]
    if has_add:
        in_specs.append(add_spec if add_spec is not None else o_spec)
        args.append(add)
    aliases = {}
    if has_alias:
        in_specs.append(pl.BlockSpec(memory_space=pl.ANY))
        aliases = {len(args): 0}
        args.append(alias)
    sem = tuple("arbitrary" if ax == k_axis else "parallel" for ax in range(len(grid)))
    scratch = [pltpu.VMEM(acc_shape, F32)] if k_axis is not None else []
    return pl.pallas_call(
        body, name=name, grid=grid, in_specs=in_specs, out_specs=o_spec,
        out_shape=jax.ShapeDtypeStruct(o_shape, o_dtype), scratch_shapes=scratch,
        input_output_aliases=aliases, compiler_params=_params(sem))(*args)


def _mm_rows(name, a, b, dims, o_dtype, n_out, *, tn=None, add=None):
    k = a.shape[1]
    tn = n_out if tn is None else tn
    if dims == NN:
        b_spec = pl.BlockSpec((k, tn), lambda n, i: (0, n))
    else:
        b_spec = pl.BlockSpec((tn, k), lambda n, i: (n, 0))
    return _mm(name, a, b, grid=(n_out // tn, T // TM),
               a_spec=pl.BlockSpec((TM, k), lambda n, i: (i, 0)), b_spec=b_spec,
               o_spec=pl.BlockSpec((TM, tn), lambda n, i: (i, n)), o_shape=(T, n_out), o_dtype=o_dtype,
               dims=dims, add=add)


def _mm_wgrad(name, a, b, *, tn=512):
    k, n = a.shape[1], b.shape[1]
    tn = min(tn, n)
    return _mm(name, a, b, grid=(n // tn,),
               a_spec=pl.BlockSpec((T, k), lambda j: (0, 0)), b_spec=pl.BlockSpec((T, tn), lambda j: (0, j)),
               o_spec=pl.BlockSpec((k, tn), lambda j: (0, j)), o_shape=(k, n), o_dtype=BF16, dims=TN)


def _rms_fwd(name, x, g):
    d = x.shape[1]

    def body(x_ref, g_ref, o_ref):
        xv = x_ref[...]
        r = lax.rsqrt(jnp.mean(xv * xv, axis=-1, keepdims=True) + EPS)
        o_ref[...] = ((xv * r) * g_ref[...]).astype(BF16)

    return pl.pallas_call(
        body, name=name, grid=(T // TM,),
        in_specs=[pl.BlockSpec((TM, d), lambda i: (i, 0)), pl.BlockSpec((1, d), lambda i: (0, 0))],
        out_specs=pl.BlockSpec((TM, d), lambda i: (i, 0)),
        out_shape=jax.ShapeDtypeStruct((T, d), BF16), compiler_params=_params(("parallel",)))(x, g)


def _rms_bwd(name, x, g, dy, dres=None):
    d = x.shape[1]
    has_res = dres is not None

    def body(*refs):
        if has_res:
            x_ref, g_ref, dy_ref, res_ref, dx_ref, dxb_ref, dg_ref = refs
        else:
            x_ref, g_ref, dy_ref, dx_ref, dxb_ref, dg_ref = refs
        xv = x_ref[...]
        r = lax.rsqrt(jnp.mean(xv * xv, axis=-1, keepdims=True) + EPS)
        xn = xv * r
        dyv = dy_ref[...].astype(F32)
        gdy = dyv * g_ref[...]
        dx = r * (gdy - xn * jnp.mean(gdy * xn, axis=-1, keepdims=True))
        if has_res:
            dx = dx + res_ref[...]
        dx_ref[...] = dx
        dxb_ref[...] = dx.astype(BF16)
        part = jnp.sum(dyv * xn, axis=0, keepdims=True)

        @pl.when(pl.program_id(0) == 0)
        def _():
            dg_ref[...] = part

        @pl.when(pl.program_id(0) > 0)
        def _():
            dg_ref[...] += part

    row = pl.BlockSpec((TR, d), lambda i: (i, 0))
    vec = pl.BlockSpec((1, d), lambda i: (0, 0))
    args = [x, g, dy] + ([dres] if has_res else [])
    in_specs = [row, vec, row] + ([row] if has_res else [])
    return pl.pallas_call(
        body, name=name, grid=(T // TR,), in_specs=in_specs, out_specs=[row, row, vec],
        out_shape=[jax.ShapeDtypeStruct((T, d), F32), jax.ShapeDtypeStruct((T, d), BF16),
                   jax.ShapeDtypeStruct((1, d), F32)],
        compiler_params=_params(("arbitrary",)))(*args)


def _final(h, g, tgt):
    def body(h_ref, g_ref, t_ref, loss_ref, dh_ref, dhb_ref, dg_ref):
        hv = h_ref[...]
        r = lax.rsqrt(jnp.mean(hv * hv, axis=-1, keepdims=True) + EPS)
        xn = hv * r
        gv = g_ref[...]
        err = xn * gv - t_ref[...]
        part_loss = 0.5 * jnp.sum(jnp.mean(err * err, axis=-1, keepdims=True), axis=0, keepdims=True)
        dy = err * (1.0 / D)
        gdy = dy * gv
        dh = r * (gdy - xn * jnp.mean(gdy * xn, axis=-1, keepdims=True))
        dh_ref[...] = dh
        dhb_ref[...] = dh.astype(BF16)
        part = jnp.sum(dy * xn, axis=0, keepdims=True)
        first = pl.program_id(0) == 0

        @pl.when(first)
        def _():
            dg_ref[...] = part
            loss_ref[...] = jnp.broadcast_to(part_loss, (1, 128))

        @pl.when(jnp.logical_not(first))
        def _():
            dg_ref[...] += part
            loss_ref[...] += jnp.broadcast_to(part_loss, (1, 128))

    row = pl.BlockSpec((TR, D), lambda i: (i, 0))
    vec = pl.BlockSpec((1, D), lambda i: (0, 0))
    return pl.pallas_call(
        body, name="final_loss", grid=(T // TR,), in_specs=[row, vec, row],
        out_specs=[pl.BlockSpec((1, 128), lambda i: (0, 0)), row, row, vec],
        out_shape=[jax.ShapeDtypeStruct((1, 128), F32), jax.ShapeDtypeStruct((T, D), F32),
                   jax.ShapeDtypeStruct((T, D), BF16), jax.ShapeDtypeStruct((1, D), F32)],
        compiler_params=_params(("arbitrary",)))(h, g, tgt)


def _prev_idx(i):
    return jnp.maximum(i * (TR // HALO) - 1, 0)


def _next_idx(i):
    return jnp.minimum((i + 1) * (TR // HALO), T // HALO - 1)


def _causal_taps(ext):
    return pltpu.roll(ext, 2, 0)[HALO:], pltpu.roll(ext, 1, 0)[HALO:], ext[HALO:]


def _anticausal_taps(ext, n):
    rows = ext.shape[0]
    return pltpu.roll(ext, rows - 1, 0)[:n], pltpu.roll(ext, rows - 2, 0)[:n]


def _sc_fwd(z, w):
    def body(b_ref, c_ref, ch_ref, u_ref, uh_ref, w_ref, y_ref):
        i = pl.program_id(0)
        cu = c_ref[...].astype(F32) * u_ref[...].astype(F32)
        cuh = ch_ref[...].astype(F32) * uh_ref[...].astype(F32)
        cuh = jnp.where(i > 0, cuh, 0.0)
        x2, x1, x0 = _causal_taps(jnp.concatenate([cuh, cu], axis=0))
        wv = w_ref[...]
        cv = (x2 * wv[0:1] + x1 * wv[1:2]) + x0 * wv[2:3]
        y_ref[...] = (b_ref[...].astype(F32) * cv).astype(BF16)

    def main(part):
        return pl.BlockSpec((TR, D), lambda i: (i, part))

    def halo(part):
        return pl.BlockSpec((HALO, D), lambda i: (_prev_idx(i), part))

    return pl.pallas_call(
        body, name="sc_fwd", grid=(T // TR,),
        in_specs=[main(0), main(1), halo(1), main(2), halo(2), pl.BlockSpec((3, D), lambda i: (0, 0))],
        out_specs=pl.BlockSpec((TR, D), lambda i: (i, 0)),
        out_shape=jax.ShapeDtypeStruct((T, D), BF16), compiler_params=_params(("parallel",)))(z, z, z, z, z, w)


def _sc_bwd(z, dy, w):
    last = T // TR - 1

    def body(b_ref, bn_ref, c_ref, ch_ref, u_ref, uh_ref, dy_ref, dyn_ref, w_ref, dz_ref, dw_ref):
        i = pl.program_id(0)
        cv_ = c_ref[...].astype(F32)
        uv = u_ref[...].astype(F32)
        cu = cv_ * uv
        cuh = jnp.where(i > 0, ch_ref[...].astype(F32) * uh_ref[...].astype(F32), 0.0)
        x2, x1, x0 = _causal_taps(jnp.concatenate([cuh, cu], axis=0))
        wv = w_ref[...]
        conv = (x2 * wv[0:1] + x1 * wv[1:2]) + x0 * wv[2:3]
        dyv = dy_ref[...]
        dz_ref[:, 0:D] = (dyv * conv).astype(BF16)
        dconv = dyv * b_ref[...].astype(F32)
        dconv_n = jnp.where(i < last, dyn_ref[...] * bn_ref[...].astype(F32), 0.0)
        n1, n2 = _anticausal_taps(jnp.concatenate([dconv, dconv_n], axis=0), TR)
        dcu = (dconv * wv[2:3] + n1 * wv[1:2]) + n2 * wv[0:1]
        dz_ref[:, D:2 * D] = (dcu * uv).astype(BF16)
        dz_ref[:, 2 * D:3 * D] = (dcu * cv_).astype(BF16)
        part = jnp.concatenate([jnp.sum(dconv * x2, axis=0, keepdims=True),
                                jnp.sum(dconv * x1, axis=0, keepdims=True),
                                jnp.sum(dconv * x0, axis=0, keepdims=True)], axis=0)

        @pl.when(i == 0)
        def _():
            dw_ref[...] = part

        @pl.when(i > 0)
        def _():
            dw_ref[...] += part

    def main(part):
        return pl.BlockSpec((TR, D), lambda i: (i, part))

    def prev(part):
        return pl.BlockSpec((HALO, D), lambda i: (_prev_idx(i), part))

    def nxt(part):
        return pl.BlockSpec((HALO, D), lambda i: (_next_idx(i), part))

    wspec = pl.BlockSpec((3, D), lambda i: (0, 0))
    return pl.pallas_call(
        body, name="sc_bwd", grid=(T // TR,),
        in_specs=[main(0), nxt(0), main(1), prev(1), main(2), prev(2), main(0), nxt(0), wspec],
        out_specs=[pl.BlockSpec((TR, 3 * D), lambda i: (i, 0)), wspec],
        out_shape=[jax.ShapeDtypeStruct((T, 3 * D), BF16), jax.ShapeDtypeStruct((3, D), F32)],
        compiler_params=_params(("arbitrary",)))(z, z, z, z, z, z, dy, dy, w)


def _sigmoid(x):
    return 1.0 / (1.0 + jnp.exp(-x))


def _ffn_fwd(name, gv, w, b):
    def body(g_ref, gh_ref, v_ref, w_ref, b_ref, a_ref):
        i = pl.program_id(1)
        g = g_ref[...].astype(F32)
        gh = jnp.where(i > 0, gh_ref[...].astype(F32), 0.0)
        x2, x1, x0 = _causal_taps(jnp.concatenate([gh, g], axis=0))
        wv = w_ref[...]
        gc = ((x2 * wv[0:1] + x1 * wv[1:2]) + x0 * wv[2:3]) + b_ref[...]
        a_ref[...] = ((gc * _sigmoid(gc)) * v_ref[...].astype(F32)).astype(BF16)

    blk = (None, TR, FF_BLK)
    return pl.pallas_call(
        body, name=name, grid=(N_FF_BLK, T // TR),
        in_specs=[pl.BlockSpec(blk, lambda j, i: (j, i, 0)),
                  pl.BlockSpec((None, HALO, FF_BLK), lambda j, i: (j, _prev_idx(i), 0)),
                  pl.BlockSpec(blk, lambda j, i: (j + N_FF_BLK, i, 0)),
                  pl.BlockSpec((None, 3, FF_BLK), lambda j, i: (j, 0, 0)),
                  pl.BlockSpec((None, 1, FF_BLK), lambda j, i: (j, 0, 0))],
        out_specs=pl.BlockSpec(blk, lambda j, i: (j, i, 0)),
        out_shape=jax.ShapeDtypeStruct((N_FF_BLK, T, FF_BLK), BF16),
        compiler_params=_params(("parallel", "parallel")))(gv, gv, gv, w, b)


def _ffn_bwd(name, gv, dact, w, b):
    last = T // TR - 1

    def body(g_ref, gp_ref, gn_ref, v_ref, vn_ref, da_ref, dan_ref, w_ref, b_ref, dg_ref, dv_ref, dw_ref, db_ref):
        i = pl.program_id(1)
        gp = jnp.where(i > 0, gp_ref[...].astype(F32), 0.0)
        ext = jnp.concatenate([gp, g_ref[...].astype(F32), gn_ref[...].astype(F32)], axis=0)
        x2, x1, x0 = _causal_taps(ext)
        wv = w_ref[...]
        gc = ((x2 * wv[0:1] + x1 * wv[1:2]) + x0 * wv[2:3]) + b_ref[...]
        sg = _sigmoid(gc)
        da = jnp.concatenate([da_ref[...].astype(F32), jnp.where(i < last, dan_ref[...].astype(F32), 0.0)], axis=0)
        vv = jnp.concatenate([v_ref[...].astype(F32), vn_ref[...].astype(F32)], axis=0)
        dv_ref[...] = (da[:TR] * (gc[:TR] * sg[:TR])).astype(BF16)
        dgc = (da * vv) * (sg * (1.0 + gc * (1.0 - sg)))
        n1, n2 = _anticausal_taps(dgc, TR)
        d0 = dgc[:TR]
        dg_ref[...] = ((d0 * wv[2:3] + n1 * wv[1:2]) + n2 * wv[0:1]).astype(BF16)
        part_w = jnp.concatenate([jnp.sum(d0 * x2[:TR], axis=0, keepdims=True),
                                  jnp.sum(d0 * x1[:TR], axis=0, keepdims=True),
                                  jnp.sum(d0 * x0[:TR], axis=0, keepdims=True)], axis=0)
        part_b = jnp.sum(d0, axis=0, keepdims=True)

        @pl.when(i == 0)
        def _():
            dw_ref[...] = part_w
            db_ref[...] = part_b

        @pl.when(i > 0)
        def _():
            dw_ref[...] += part_w
            db_ref[...] += part_b

    blk = (None, TR, FF_BLK)
    hblk = (None, HALO, FF_BLK)
    wspec = pl.BlockSpec((None, 3, FF_BLK), lambda j, i: (j, 0, 0))
    bspec = pl.BlockSpec((None, 1, FF_BLK), lambda j, i: (j, 0, 0))
    return pl.pallas_call(
        body, name=name, grid=(N_FF_BLK, T // TR),
        in_specs=[pl.BlockSpec(blk, lambda j, i: (j, i, 0)),
                  pl.BlockSpec(hblk, lambda j, i: (j, _prev_idx(i), 0)),
                  pl.BlockSpec(hblk, lambda j, i: (j, _next_idx(i), 0)),
                  pl.BlockSpec(blk, lambda j, i: (j + N_FF_BLK, i, 0)),
                  pl.BlockSpec(hblk, lambda j, i: (j + N_FF_BLK, _next_idx(i), 0)),
                  pl.BlockSpec(blk, lambda j, i: (j, i, 0)),
                  pl.BlockSpec(hblk, lambda j, i: (j, _next_idx(i), 0)),
                  wspec, bspec],
        out_specs=[pl.BlockSpec(blk, lambda j, i: (j, i, 0)), pl.BlockSpec(blk, lambda j, i: (j, i, 0)), wspec, bspec],
        out_shape=[jax.ShapeDtypeStruct((N_FF_BLK, T, FF_BLK), BF16), jax.ShapeDtypeStruct((N_FF_BLK, T, FF_BLK), BF16),
                   jax.ShapeDtypeStruct((N_FF_BLK, 3, FF_BLK), F32), jax.ShapeDtypeStruct((N_FF_BLK, 1, FF_BLK), F32)],
        compiler_params=_params(("parallel", "arbitrary")))(gv, gv, gv, gv, gv, dact, dact, w, b)


def _rope_tables(pos, inv_freq):
    half = QK_ROPE // 2

    def body(p_ref, f_ref, c_ref, sa_ref, sb_ref):
        ang = p_ref[...].astype(F32) * f_ref[...]
        lane = lax.broadcasted_iota(jnp.int32, (T, 128), 1)
        c = jnp.cos(ang)
        s = jnp.sin(ang)
        c_ref[...] = jnp.where(lane < 2 * half, c, 0.0)
        sa_ref[...] = jnp.where(lane < half, -s, 0.0)
        sb_ref[...] = jnp.where(jnp.logical_and(lane >= half, lane < 2 * half), s, 0.0)

    return pl.pallas_call(
        body, name="rope_tables", out_shape=[jax.ShapeDtypeStruct((T, 128), F32)] * 3,
        compiler_params=pltpu.CompilerParams(vmem_limit_bytes=VMEM_LIMIT))(pos, inv_freq)


def _rope(name, x, tables, sign, out_dtype, reduce_groups=False):
    g, _, w = x.shape
    cos, sa, sb = tables

    def body(x_ref, c_ref, sa_ref, sb_ref, o_ref):
        xv = x_ref[...].astype(F32)
        if reduce_groups:
            acc = xv[0]
            for k in range(1, g):
                acc = acc + xv[k]
            xv = acc
        r = xv[:, w - 128:]
        out = r * c_ref[...] + sign * (pltpu.roll(r, 96, 1) * sa_ref[...] + pltpu.roll(r, 32, 1) * sb_ref[...])
        if w > 128:
            o_ref[:, :w - 128] = xv[:, :w - 128].astype(out_dtype)
        o_ref[:, w - 128:] = out.astype(out_dtype)

    tab = pl.BlockSpec((TM, 128), lambda h, i: (i, 0))
    if reduce_groups:
        x_spec = pl.BlockSpec((g, TM, w), lambda h, i: (0, i, 0))
        groups = 1
    else:
        x_spec = pl.BlockSpec((None, TM, w), lambda h, i: (h, i, 0))
        groups = g
    return pl.pallas_call(
        body, name=name, grid=(groups, T // TM), in_specs=[x_spec, tab, tab, tab],
        out_specs=pl.BlockSpec((None, TM, w), lambda h, i: (h, i, 0)),
        out_shape=jax.ShapeDtypeStruct((groups, T, w), out_dtype),
        compiler_params=_params(("parallel", "parallel")))(x, cos, sa, sb)


SCALE = (QK_NOPE + QK_ROPE) ** -0.5


def _chunk_mask(q0, k0, transposed):
    a = lax.broadcasted_iota(jnp.int32, (TQ, TQ), 0)
    b = lax.broadcasted_iota(jnp.int32, (TQ, TQ), 1)
    qpos, kpos = (q0 + b, k0 + a) if transposed else (q0 + a, k0 + b)
    return (kpos // CHUNK) <= (qpos // CHUNK)


def _attn_fwd(q, kn, kr, v):
    def body(q_ref, kn_ref, kr_ref, v_ref, o_ref, lse_ref):
        i = pl.program_id(1)
        qv = q_ref[...]

        def step(j, carry):
            m, l, acc = carry
            off = pl.multiple_of(j * TQ, TQ)
            kk = jnp.concatenate([kn_ref[pl.ds(off, TQ), :], kr_ref[pl.ds(off, TQ), :]], axis=1)
            s = lax.dot_general(qv, kk, NT, preferred_element_type=F32) * SCALE
            s = jnp.where(_chunk_mask(i * TQ, j * TQ, False), s, NEG_INF)
            m_new = jnp.maximum(m, jnp.max(s, axis=-1, keepdims=True))
            p = jnp.exp(s - m_new)
            alpha = jnp.exp(m - m_new)
            l = alpha * l + jnp.sum(p, axis=-1, keepdims=True)
            acc = alpha * acc + lax.dot_general(p.astype(BF16), v_ref[pl.ds(off, TQ), :], NN, preferred_element_type=F32)
            return m_new, l, acc

        init = (jnp.full((TQ, 1), NEG_INF, F32), jnp.zeros((TQ, 1), F32), jnp.zeros((TQ, V_HEAD), F32))
        m, l, acc = lax.fori_loop(0, i + 1, step, init)
        o_ref[...] = (acc / l).astype(BF16)
        lse_ref[...] = m + jnp.log(l)

    return pl.pallas_call(
        body, name="attn_fwd", grid=(N_HEADS, T // TQ),
        in_specs=[pl.BlockSpec((None, TQ, QK_PAD), lambda h, i: (h, i, 0)),
                  pl.BlockSpec((T, QK_NOPE), lambda h, i: (0, h)),
                  pl.BlockSpec((T, 128), lambda h, i: (0, 0)),
                  pl.BlockSpec((T, V_HEAD), lambda h, i: (0, h))],
        out_specs=[pl.BlockSpec((TQ, V_HEAD), lambda h, i: (i, h)), pl.BlockSpec((None, TQ, 1), lambda h, i: (h, i, 0))],
        out_shape=[jax.ShapeDtypeStruct((T, N_HEADS * V_HEAD), BF16), jax.ShapeDtypeStruct((N_HEADS, T, 1), F32)],
        compiler_params=_params(("parallel", "parallel")))(q, kn, kr, v)


def _attn_bwd_dq(q, kn, kr, v, o, do, lse):
    def body(q_ref, kn_ref, kr_ref, v_ref, o_ref, do_ref, lse_ref, dq_ref, dl_ref):
        i = pl.program_id(1)
        qv = q_ref[...]
        dov = do_ref[...]
        lse = lse_ref[...]
        delta = jnp.sum(dov.astype(F32) * o_ref[...].astype(F32), axis=-1, keepdims=True)
        dl_ref[...] = delta

        def step(j, dq):
            off = pl.multiple_of(j * TQ, TQ)
            kk = jnp.concatenate([kn_ref[pl.ds(off, TQ), :], kr_ref[pl.ds(off, TQ), :]], axis=1)
            s = lax.dot_general(qv, kk, NT, preferred_element_type=F32) * SCALE
            s = jnp.where(_chunk_mask(i * TQ, j * TQ, False), s, NEG_INF)
            p = jnp.exp(s - lse)
            dp = lax.dot_general(dov, v_ref[pl.ds(off, TQ), :], NT, preferred_element_type=F32)
            ds = (p * (dp - delta)) * SCALE
            return dq + lax.dot_general(ds.astype(BF16), kk, NN, preferred_element_type=F32)

        dq_ref[...] = lax.fori_loop(0, i + 1, step, jnp.zeros((TQ, QK_PAD), F32))

    col = pl.BlockSpec((None, TQ, 1), lambda h, i: (h, i, 0))
    head = pl.BlockSpec((TQ, V_HEAD), lambda h, i: (i, h))
    return pl.pallas_call(
        body, name="attn_bwd_dq", grid=(N_HEADS, T // TQ),
        in_specs=[pl.BlockSpec((None, TQ, QK_PAD), lambda h, i: (h, i, 0)),
                  pl.BlockSpec((T, QK_NOPE), lambda h, i: (0, h)),
                  pl.BlockSpec((T, 128), lambda h, i: (0, 0)),
                  pl.BlockSpec((T, V_HEAD), lambda h, i: (0, h)),
                  head, head, col],
        out_specs=[pl.BlockSpec((None, TQ, QK_PAD), lambda h, i: (h, i, 0)), col],
        out_shape=[jax.ShapeDtypeStruct((N_HEADS, T, QK_PAD), F32), jax.ShapeDtypeStruct((N_HEADS, T, 1), F32)],
        compiler_params=_params(("parallel", "parallel")))(q, kn, kr, v, o, do, lse)


def _attn_bwd_dkv(q, kn, kr, v, do, lse_row, delta_row):
    nq = T // TQ

    def body(q_ref, kn_ref, kr_ref, v_ref, do_ref, lse_ref, dl_ref, dkn_ref, dkr_ref, dv_ref):
        j = pl.program_id(1)
        kk = jnp.concatenate([kn_ref[...], kr_ref[...]], axis=1)
        vv = v_ref[...]

        def step(i, carry):
            dk, dv = carry
            off = pl.multiple_of(i * TQ, TQ)
            qi = q_ref[pl.ds(off, TQ), :]
            doi = do_ref[pl.ds(off, TQ), :]
            st = lax.dot_general(kk, qi, NT, preferred_element_type=F32) * SCALE
            st = jnp.where(_chunk_mask(i * TQ, j * TQ, True), st, NEG_INF)
            pt = jnp.exp(st - lse_ref[:, pl.ds(off, TQ)])
            dv = dv + lax.dot_general(pt.astype(BF16), doi, NN, preferred_element_type=F32)
            dpt = lax.dot_general(vv, doi, NT, preferred_element_type=F32)
            dst = (pt * (dpt - dl_ref[:, pl.ds(off, TQ)])) * SCALE
            dk = dk + lax.dot_general(dst.astype(BF16), qi, NN, preferred_element_type=F32)
            return dk, dv

        dk, dv = lax.fori_loop(j, nq, step, (jnp.zeros((TQ, QK_PAD), F32), jnp.zeros((TQ, V_HEAD), F32)))
        dkn_ref[...] = dk[:, :QK_NOPE].astype(BF16)
        dkr_ref[...] = dk[:, QK_NOPE:]
        dv_ref[...] = dv.astype(BF16)

    row = pl.BlockSpec((None, 1, T), lambda h, j: (h, 0, 0))
    head = pl.BlockSpec((TQ, 128), lambda h, j: (j, h))
    return pl.pallas_call(
        body, name="attn_bwd_dkv", grid=(N_HEADS, nq),
        in_specs=[pl.BlockSpec((None, T, QK_PAD), lambda h, j: (h, 0, 0)),
                  head, pl.BlockSpec((TQ, 128), lambda h, j: (j, 0)), head,
                  pl.BlockSpec((T, V_HEAD), lambda h, j: (0, h)), row, row],
        out_specs=[head, pl.BlockSpec((None, TQ, 128), lambda h, j: (h, j, 0)), head],
        out_shape=[jax.ShapeDtypeStruct((T, N_HEADS * QK_NOPE), BF16), jax.ShapeDtypeStruct((N_HEADS, T, 128), F32),
                   jax.ShapeDtypeStruct((T, N_HEADS * V_HEAD), BF16)],
        compiler_params=_params(("parallel", "parallel")))(q, kn, kr, v, do, lse_row, delta_row)


def _ffn_layer_fwd(tag, l, h, gain, w_up, w_down4, cw, cb):
    hf = _rms_fwd(f"{tag}_norm", h, gain)
    gv = _mm(f"{tag}_up", hf, w_up, grid=(N_DEV, T // TM),
             a_spec=pl.BlockSpec((TM, D), lambda j, i: (i, 0)),
             b_spec=pl.BlockSpec((None, None, D, FF_BLK), lambda j, i: (l, j, 0, 0)),
             o_spec=pl.BlockSpec((None, TM, FF_BLK), lambda j, i: (j, i, 0)),
             o_shape=(N_DEV, T, FF_BLK), o_dtype=BF16, dims=NN)
    act = _ffn_fwd(f"{tag}_act", gv, cw, cb)
    tn = 512
    out = _mm(f"{tag}_down", act, w_down4, grid=(D // tn, T // TM, N_FF_BLK),
              a_spec=pl.BlockSpec((None, TM, FF_BLK), lambda n, i, k: (k, i, 0)),
              b_spec=pl.BlockSpec((None, None, FF_BLK, tn), lambda n, i, k: (l, k, 0, n)),
              o_spec=pl.BlockSpec((TM, tn), lambda n, i, k: (i, n)), o_shape=(T, D), o_dtype=F32,
              dims=NN, k_axis=2, acc_shape=(TM, tn), add=h, add_spec=pl.BlockSpec((TM, tn), lambda n, i, k: (i, n)))
    return out, (hf, gv, act)


def _ffn_layer_bwd(tag, l, h, gain, w_up, w_down4, cw, cb, saved, dh, dh_bf, g_up, g_down):
    hf, gv, act = saved
    dact = _mm(f"{tag}_dact", dh_bf, w_down4, grid=(N_FF_BLK, T // TM),
               a_spec=pl.BlockSpec((TM, D), lambda j, i: (i, 0)),
               b_spec=pl.BlockSpec((None, None, FF_BLK, D), lambda j, i: (l, j, 0, 0)),
               o_spec=pl.BlockSpec((None, TM, FF_BLK), lambda j, i: (j, i, 0)),
               o_shape=(N_FF_BLK, T, FF_BLK), o_dtype=BF16, dims=NT)
    tn = 512
    g_down = _mm(f"{tag}_gdown", act, dh_bf, grid=(N_FF_BLK, D // tn),
                 a_spec=pl.BlockSpec((None, T, FF_BLK), lambda j, n: (j, 0, 0)),
                 b_spec=pl.BlockSpec((T, tn), lambda j, n: (0, n)),
                 o_spec=pl.BlockSpec((None, FF_BLK, tn), lambda j, n: (l, j, n)),
                 o_shape=(2, D_FF, D), o_dtype=BF16, dims=TN, alias=g_down)
    dg, dv, dcw, dcb = _ffn_bwd(f"{tag}_dact_ew", gv, dact, cw, cb)
    dhf = None
    for half, (name, dpart) in enumerate((("g", dg), ("v", dv))):
        dhf = _mm(f"{tag}_dhf_{name}", dpart, w_up, grid=(T // TM, N_FF_BLK),
                  a_spec=pl.BlockSpec((None, TM, FF_BLK), lambda i, k: (k, i, 0)),
                  b_spec=pl.BlockSpec((None, None, D, FF_BLK), lambda i, k, half=half: (l, k + half * N_FF_BLK, 0, 0)),
                  o_spec=pl.BlockSpec((TM, D), lambda i, k: (i, 0)), o_shape=(T, D), o_dtype=F32,
                  dims=NT, k_axis=1, acc_shape=(TM, D), add=dhf)
        g_up = _mm(f"{tag}_gup_{name}", hf, dpart, grid=(N_FF_BLK,),
                   a_spec=pl.BlockSpec((T, D), lambda j: (0, 0)),
                   b_spec=pl.BlockSpec((None, T, FF_BLK), lambda j: (j, 0, 0)),
                   o_spec=pl.BlockSpec((None, None, D, FF_BLK), lambda j, half=half: (l, j + half * N_FF_BLK, 0, 0)),
                   o_shape=(2, N_DEV, D, FF_BLK), o_dtype=BF16, dims=TN, alias=g_up)
    dh_in, dh_in_bf, dgain = _rms_bwd(f"{tag}_dnorm", h, gain, dhf, dres=dh)
    return dh_in, dh_in_bf, dgain, dcw, dcb, g_up, g_down


def _local_step(x, pos, tgt, rep, w):
    attn_norm, ffn_norm, final_norm = rep["attn_norm"], rep["ffn_norm"], rep["final_norm"]
    half = QK_ROPE // 2
    inv = 1.0 / (ROPE_THETA ** (jnp.arange(half, dtype=F32) / half))
    inv_freq = jnp.concatenate([inv, inv, jnp.zeros((128 - 2 * half,), F32)]).reshape(1, 128)
    tables = _rope_tables(pos, inv_freq)

    hn0 = _rms_fwd("l0_norm", x, attn_norm[0:1])
    z = _mm_rows("l0_in", hn0, w["sc_w_in"], NN, BF16, 3 * D, tn=512)
    y = _sc_fwd(z, w["sc_conv_w"])
    h1 = _mm_rows("l0_out", y, w["sc_w_out"], NN, F32, D, tn=512, add=x)
    h2, ffn0 = _ffn_layer_fwd("f0", 0, h1, ffn_norm[0:1], w["ffn_w_up"], w["ffn_w_down"], w["ffn_cw"][0], w["ffn_cb"][0])

    hk = _rms_fwd("kv_norm", h2, rep["kv_in_norm"])
    ckv_raw = _mm_rows("kv_down", hk, w["w_dkv"], NN, F32, KV_LORA)
    kr_raw = _mm_rows("kv_rope", hk, w["w_kr"], NN, F32, 128)
    ckv = _rms_fwd("kv_lnorm", ckv_raw, rep["kv_latent_norm"])
    kn = _mm_rows("kv_uk", ckv, w["w_uk"], NN, BF16, N_HEADS * QK_NOPE)
    vv = _mm_rows("kv_uv", ckv, w["w_uv"], NN, BF16, N_HEADS * V_HEAD)
    kr = _rope("k_rope", kr_raw.reshape(1, T, 128), tables, 1.0, BF16).reshape(T, 128)

    hn1 = _rms_fwd("l1_norm", h2, attn_norm[1:2])
    cq_raw = _mm_rows("q_down", hn1, w["w_dq"], NN, F32, Q_LORA)
    cq = _rms_fwd("q_lnorm", cq_raw, rep["q_latent_norm"])
    q_raw = _mm("q_up", cq, w["w_uq"], grid=(N_HEADS, T // TM),
                a_spec=pl.BlockSpec((TM, Q_LORA), lambda h, i: (i, 0)),
                b_spec=pl.BlockSpec((None, Q_LORA, QK_PAD), lambda h, i: (h, 0, 0)),
                o_spec=pl.BlockSpec((None, TM, QK_PAD), lambda h, i: (h, i, 0)),
                o_shape=(N_HEADS, T, QK_PAD), o_dtype=F32, dims=NN)
    q = _rope("q_rope", q_raw, tables, 1.0, BF16)
    o, lse = _attn_fwd(q, kn, kr, vv)
    h3 = _mm_rows("attn_out", o, w["w_o"], NN, F32, D, tn=512, add=h2)
    h4, ffn1 = _ffn_layer_fwd("f1", 1, h3, ffn_norm[1:2], w["ffn_w_up"], w["ffn_w_down"], w["ffn_cw"][1], w["ffn_cb"][1])

    loss, dh4, dh4_bf, d_final = _final(h4, final_norm.reshape(1, D), tgt)

    g = {}
    dh3, dh3_bf, d_fn1, dcw1, dcb1, g_up, g_down = _ffn_layer_bwd(
        "f1", 1, h3, ffn_norm[1:2], w["ffn_w_up"], w["ffn_w_down"], w["ffn_cw"][1], w["ffn_cb"][1], ffn1, dh4, dh4_bf,
        None, None)

    do = _mm_rows("d_attn_out", dh3_bf, w["w_o"], NT, BF16, N_HEADS * V_HEAD)
    g["w_o"] = _mm_wgrad("g_w_o", o, dh3_bf)
    dq, delta = _attn_bwd_dq(q, kn, kr, vv, o, do, lse)
    dkn, dkr, dvv = _attn_bwd_dkv(q, kn, kr, vv, do, lse.reshape(N_HEADS, 1, T), delta.reshape(N_HEADS, 1, T))
    dq_pre = _rope("dq_rope", dq, tables, -1.0, BF16)
    dcq = _mm("d_q_up", dq_pre, w["w_uq"], grid=(T // TM, N_HEADS),
              a_spec=pl.BlockSpec((None, TM, QK_PAD), lambda i, h: (h, i, 0)),
              b_spec=pl.BlockSpec((None, Q_LORA, QK_PAD), lambda i, h: (h, 0, 0)),
              o_spec=pl.BlockSpec((TM, Q_LORA), lambda i, h: (i, 0)), o_shape=(T, Q_LORA), o_dtype=F32,
              dims=NT, k_axis=1, acc_shape=(TM, Q_LORA))
    g["w_uq"] = _mm("g_w_uq", cq, dq_pre, grid=(N_HEADS,),
                    a_spec=pl.BlockSpec((T, Q_LORA), lambda h: (0, 0)),
                    b_spec=pl.BlockSpec((None, T, QK_PAD), lambda h: (h, 0, 0)),
                    o_spec=pl.BlockSpec((None, Q_LORA, QK_PAD), lambda h: (h, 0, 0)),
                    o_shape=(N_HEADS, Q_LORA, QK_PAD), o_dtype=BF16, dims=TN)
    _, dcq_raw_bf, d_qln = _rms_bwd("d_q_lnorm", cq_raw, rep["q_latent_norm"], dcq)
    dhn1 = _mm_rows("d_q_down", dcq_raw_bf, w["w_dq"], NT, F32, D)
    g["w_dq"] = _mm_wgrad("g_w_dq", hn1, dcq_raw_bf)
    dh2_a, _, d_an1 = _rms_bwd("d_l1_norm", h2, attn_norm[1:2], dhn1, dres=dh3)

    dckv = _mm_rows("d_kv_uk", dkn, w["w_uk"], NT, F32, KV_LORA)
    dckv = _mm_rows("d_kv_uv", dvv, w["w_uv"], NT, F32, KV_LORA, add=dckv)
    g["w_uk"] = _mm_wgrad("g_w_uk", ckv, dkn)
    g["w_uv"] = _mm_wgrad("g_w_uv", ckv, dvv)
    _, dckv_raw_bf, d_kvln = _rms_bwd("d_kv_lnorm", ckv_raw, rep["kv_latent_norm"], dckv)
    dkr_raw_bf = _rope("dk_rope", dkr, tables, -1.0, BF16, reduce_groups=True).reshape(T, 128)
    dhk = _mm_rows("d_kv_down", dckv_raw_bf, w["w_dkv"], NT, F32, D)
    dhk = _mm_rows("d_kv_rope", dkr_raw_bf, w["w_kr"], NT, F32, D, add=dhk)
    g["w_dkv"] = _mm_wgrad("g_w_dkv", hk, dckv_raw_bf)
    g["w_kr"] = _mm_wgrad("g_w_kr", hk, dkr_raw_bf)
    dh2, dh2_bf, d_kvin = _rms_bwd("d_kv_norm", h2, rep["kv_in_norm"], dhk, dres=dh2_a)

    dh1, dh1_bf, d_fn0, dcw0, dcb0, g_up, g_down = _ffn_layer_bwd(
        "f0", 0, h1, ffn_norm[0:1], w["ffn_w_up"], w["ffn_w_down"], w["ffn_cw"][0], w["ffn_cb"][0], ffn0, dh2, dh2_bf,
        g_up, g_down)
    g["ffn_w_up"] = g_up
    g["ffn_w_down"] = g_down

    dy = _mm_rows("d_l0_out", dh1_bf, w["sc_w_out"], NT, F32, D)
    g["sc_w_out"] = _mm_wgrad("g_sc_w_out", y, dh1_bf)
    dz, d_scw = _sc_bwd(z, dy, w["sc_conv_w"])
    dhn0 = _mm_rows("d_l0_in", dz, w["sc_w_in"], NT, F32, D)
    g["sc_w_in"] = _mm_wgrad("g_sc_w_in", hn0, dz)
    grad_x, _, d_an0 = _rms_bwd("d_l0_norm", x, attn_norm[0:1], dhn0, dres=dh1)

    small = {
        "attn_norm": jnp.concatenate([d_an0, d_an1], axis=0),
        "ffn_norm": jnp.concatenate([d_fn0, d_fn1], axis=0),
        "final_norm": d_final.reshape(D),
        "kv_in_norm": d_kvin.reshape(D),
        "kv_latent_norm": d_kvln.reshape(KV_LORA),
        "q_latent_norm": d_qln,
        "ffn_conv_b": jnp.stack([dcb0, dcb1]).transpose(0, 2, 1, 3).reshape(2, D_FF),
        "sc_conv_w": d_scw,
        "ffn_conv_w": jnp.stack([dcw0, dcw1]).transpose(0, 2, 1, 3).reshape(2, 3, D_FF),
    }
    return loss, grad_x, g, small


def _place():
    return lax.axis_index("x"), lax.axis_index("y"), lax.axis_index("c")


def _window(ref, kind, dev):
    if kind == "blocked":
        return ref.at[:, dev]
    width = ref.shape[-1] // N_DEV
    return ref.at[:, pl.ds(pl.multiple_of(dev * width, 128), width)]


def _all_gather(name, items):
    n = len(items)
    out_shapes = []
    for shard, kind in items:
        if kind == "blocked":
            shape = (shard.shape[0], N_DEV) + shard.shape[1:]
        else:
            shape = (shard.shape[0], N_DEV * shard.shape[1])
        out_shapes.append(jax.ShapeDtypeStruct(shape, shard.dtype))

    def body(*refs):
        srcs, outs = refs[:n], refs[n:2 * n]
        send_sems, recv_sems, local_sems = refs[2 * n:]
        x, y, c = _place()
        me = 4 * x + 2 * y + c
        sibling = (x, y, 1 - c)
        chips = [(1 - x, y), (x, 1 - y), (1 - x, 1 - y)]

        def num(px, py, pc):
            return 4 * px + 2 * py + pc

        def copy(t, k, dev, to, from_src):
            kind = items[t][1]
            dst = _window(outs[t], kind, dev)
            return pltpu.make_async_remote_copy(
                src_ref=srcs[t] if from_src else dst, dst_ref=dst,
                send_sem=send_sems.at[t, k], recv_sem=recv_sems.at[t, k], device_id=to, device_id_type=MESH)

        mine = [pltpu.make_async_copy(srcs[t], _window(outs[t], items[t][1], me), local_sems.at[t]) for t in range(n)]
        for cp in mine:
            cp.start()
        first = []
        for t in range(n):
            first.append(copy(t, 0, me, sibling, True))
            for j, chip in enumerate(chips):
                first.append(copy(t, 1 + j, me, (*chip, c), True))
        for cp in first:
            cp.start()
        passed = []
        for j, chip in enumerate(chips):
            for t in range(n):
                copy(t, 1 + j, num(*chip, c), (x, y, c), False).wait_recv()
                fwd = copy(t, 4 + j, num(*chip, c), sibling, False)
                fwd.start()
                passed.append(fwd)
        for t in range(n):
            copy(t, 0, num(x, y, 1 - c), (x, y, c), False).wait_recv()
            for j, chip in enumerate(chips):
                copy(t, 4 + j, num(*chip, 1 - c), (x, y, c), False).wait_recv()
        for cp in first + passed:
            cp.wait_send()
        for cp in mine:
            cp.wait()

    any_spec = pl.BlockSpec(memory_space=pl.ANY)
    return pl.pallas_call(
        body, name=name, in_specs=[any_spec] * n, out_specs=[any_spec] * n, out_shape=out_shapes,
        scratch_shapes=[pltpu.SemaphoreType.DMA((n, 7)), pltpu.SemaphoreType.DMA((n, 7)), pltpu.SemaphoreType.DMA((n,))],
    )(*[s for s, _ in items])


def _sibling_exchange(items):
    n = len(items)

    def shard_shape(gr, kind):
        if kind == "blocked":
            return (gr.shape[0],) + gr.shape[2:]
        return (gr.shape[0], gr.shape[1] // N_DEV)

    out_shapes = [jax.ShapeDtypeStruct((N_CHIP,) + shard_shape(gr, kind), gr.dtype) for gr, kind in items]

    def body(*refs):
        srcs, outs = refs[:n], refs[n:2 * n]
        send_sems, recv_sems = refs[2 * n:]
        x, y, c = _place()
        sibling = (x, y, 1 - c)
        copies = []
        for t in range(n):
            for k in range(N_CHIP):
                copies.append(pltpu.make_async_remote_copy(
                    src_ref=_window(srcs[t], items[t][1], 2 * k + (1 - c)), dst_ref=outs[t].at[k],
                    send_sem=send_sems.at[t, k], recv_sem=recv_sems.at[t, k], device_id=sibling, device_id_type=MESH))
        for cp in copies:
            cp.start()
        for cp in copies:
            cp.wait()

    any_spec = pl.BlockSpec(memory_space=pl.ANY)
    return pl.pallas_call(
        body, name="rs_sibling", in_specs=[any_spec] * n, out_specs=[any_spec] * n, out_shape=out_shapes,
        scratch_shapes=[pltpu.SemaphoreType.DMA((n, N_CHIP)), pltpu.SemaphoreType.DMA((n, N_CHIP))],
    )(*[gr for gr, _ in items])


def _chip_sum(t, gr, kind, recv, c):
    if kind == "blocked":
        nl, _, r, w = gr.shape
        rows = nl * r
        tr = _row_tile(r)
        per = r // tr
        g_spec = pl.BlockSpec((None, None, tr, w), lambda k, i, cref: (i // per, 2 * k + cref[0], i % per, 0))
    else:
        rows, w = gr.shape[0], gr.shape[1] // N_DEV
        tr = _row_tile(rows)
        g_spec = pl.BlockSpec((tr, w), lambda k, i, cref: (i, 2 * k + cref[0]))
    recv = recv.reshape(N_CHIP, rows, w)

    def body(c_ref, g_ref, r_ref, o_ref):
        del c_ref
        o_ref[...] = (g_ref[...].astype(F32) + r_ref[...].astype(F32)).astype(BF16)

    blk = pl.BlockSpec((None, tr, w), lambda k, i, cref: (k, i, 0))
    return pl.pallas_call(
        body, name=f"rs_sum_{t}",
        grid_spec=pltpu.PrefetchScalarGridSpec(num_scalar_prefetch=1, grid=(N_CHIP, rows // tr),
                                               in_specs=[g_spec, blk], out_specs=blk),
        out_shape=jax.ShapeDtypeStruct((N_CHIP, rows, w), BF16),
        compiler_params=_params(("parallel", "parallel")))(c, gr, recv)


def _row_tile(rows):
    for tr in (512, 384, 352, 256, 128, 64, 32, 16):
        if rows % tr == 0:
            return tr
    raise ValueError(rows)


def _chip_exchange(sums):
    n = len(sums)

    def body(*refs):
        srcs, outs = refs[:n], refs[n:2 * n]
        send_sems, recv_sems = refs[2 * n:]
        x, y, c = _place()
        my_chip = 2 * x + y
        others = [(1 - x, y), (x, 1 - y), (1 - x, 1 - y)]
        sends, recvs = [], []
        for t in range(n):
            for j, (px, py) in enumerate(others):
                sends.append(pltpu.make_async_remote_copy(
                    src_ref=srcs[t].at[2 * px + py], dst_ref=outs[t].at[my_chip],
                    send_sem=send_sems.at[t, j], recv_sem=recv_sems.at[t, j], device_id=(px, py, c), device_id_type=MESH))
                recvs.append(pltpu.make_async_remote_copy(
                    src_ref=srcs[t].at[my_chip], dst_ref=outs[t].at[2 * px + py],
                    send_sem=send_sems.at[t, j], recv_sem=recv_sems.at[t, j], device_id=(px, py, c), device_id_type=MESH))
        for cp in sends:
            cp.start()
        for cp in recvs:
            cp.wait_recv()
        for cp in sends:
            cp.wait_send()

    any_spec = pl.BlockSpec(memory_space=pl.ANY)
    return pl.pallas_call(
        body, name="rs_chips", in_specs=[any_spec] * n, out_specs=[any_spec] * n,
        out_shape=[jax.ShapeDtypeStruct(s.shape, s.dtype) for s in sums],
        scratch_shapes=[pltpu.SemaphoreType.DMA((n, 3)), pltpu.SemaphoreType.DMA((n, 3))],
    )(*sums)


def _adamw_math(g, wv, mv, vv):
    m = ADAM_B1 * mv + (1.0 - ADAM_B1) * g
    v = ADAM_B2 * vv + (1.0 - ADAM_B2) * (g * g)
    m_hat = m / (1.0 - ADAM_B1 ** ADAM_STEP)
    v_hat = v / (1.0 - ADAM_B2 ** ADAM_STEP)
    delta = -ADAM_LR * (m_hat / (jnp.sqrt(v_hat) + ADAM_EPS) + ADAM_WD * wv)
    return delta, m, v


def _adamw_sharded(t, own, recv, chip_ids, wv, mv, vv):
    shape = wv.shape
    rows, w = own.shape[1], own.shape[2]
    tr = _row_tile(rows)
    w2, m2, v2 = (a.reshape(rows, w) for a in (wv, mv, vv))

    def body(ids_ref, own_ref, r1_ref, r2_ref, r3_ref, w_ref, m_ref, v_ref, g_ref, d_ref, nm_ref, nv_ref):
        del ids_ref
        g = ((own_ref[...].astype(F32) + r1_ref[...].astype(F32)) + r2_ref[...].astype(F32)) + r3_ref[...].astype(F32)
        g_ref[...] = g
        d_ref[...], nm_ref[...], nv_ref[...] = _adamw_math(g, w_ref[...], m_ref[...], v_ref[...])

    def pick(slot):
        return pl.BlockSpec((None, tr, w), lambda i, ids: (ids[slot], i, 0))

    flat = pl.BlockSpec((tr, w), lambda i, ids: (i, 0))
    outs = pl.pallas_call(
        body, name=f"adamw_{t}",
        grid_spec=pltpu.PrefetchScalarGridSpec(
            num_scalar_prefetch=1, grid=(rows // tr,),
            in_specs=[pick(0), pick(1), pick(2), pick(3), flat, flat, flat], out_specs=[flat] * 4),
        out_shape=[jax.ShapeDtypeStruct((rows, w), F32)] * 4,
        compiler_params=_params(("parallel",)))(chip_ids, own, recv, recv, recv, w2, m2, v2)
    return [o.reshape(shape) for o in outs]


def _adamw_small(parts, wv, mv, vv):
    r = wv.shape[0]

    def body(p_ref, w_ref, m_ref, v_ref, g_ref, d_ref, nm_ref, nv_ref):
        g = p_ref[0]
        for k in range(1, N_DEV):
            g = g + p_ref[k]
        g_ref[...] = g
        d_ref[...], nm_ref[...], nv_ref[...] = _adamw_math(g, w_ref[...], m_ref[...], v_ref[...])

    return pl.pallas_call(
        body, name="adamw_small",
        out_shape=[jax.ShapeDtypeStruct((r, 128), F32)] * 4,
        compiler_params=pltpu.CompilerParams(vmem_limit_bytes=VMEM_LIMIT))(parts, wv, mv, vv)


def _pack(arrays, rows):
    flat = jnp.concatenate([a.reshape(-1).astype(F32) for a in arrays])
    return jnp.pad(flat, (0, rows * 128 - flat.shape[0])).reshape(rows, 128)


def _unpack(packed, shapes):
    flat = packed.reshape(-1)
    out, off = [], 0
    for shape in shapes:
        size = 1
        for s in shape:
            size *= s
        out.append(flat[off:off + size].reshape(shape))
        off += size
    return out


REPLICATED = ("attn_norm", "ffn_norm", "final_norm", "kv_in_norm", "kv_latent_norm", "q_latent_norm", "ffn_conv_b")
WEIGHTS = ("attn_norm", "ffn_norm", "final_norm", "sc_w_in", "sc_conv_w", "sc_w_out", "kv_in_norm", "w_dkv",
           "kv_latent_norm", "w_kr", "w_uk", "w_uv", "w_dq", "q_latent_norm", "w_uq", "w_o", "ffn_w_up", "ffn_conv_w",
           "ffn_conv_b", "ffn_w_down")
BIG = {"sc_w_in": "cols", "sc_w_out": "blocked", "w_dkv": "blocked", "w_kr": "blocked", "w_uk": "cols", "w_uv": "cols",
       "w_dq": "blocked", "w_uq": "blocked", "w_o": "blocked", "ffn_w_up": "blocked", "ffn_w_down": "blocked"}
SMALL_W_ROWS = 24
SMALL_G_ROWS = 256


def kernel(x, positions, attn_norm, ffn_norm, final_norm, sc_w_in, sc_conv_w, sc_w_out, kv_in_norm, w_dkv, kv_latent_norm, w_kr, w_uk, w_uv, w_dq, q_latent_norm, w_uq, w_o, ffn_w_up, ffn_conv_w, ffn_conv_b, ffn_w_down, loss_target, m_attn_norm, m_ffn_norm, m_final_norm, m_sc_w_in, m_sc_conv_w, m_sc_w_out, m_kv_in_norm, m_w_dkv, m_kv_latent_norm, m_w_kr, m_w_uk, m_w_uv, m_w_dq, m_q_latent_norm, m_w_uq, m_w_o, m_ffn_w_up, m_ffn_conv_w, m_ffn_conv_b, m_ffn_w_down, v_attn_norm, v_ffn_norm, v_final_norm, v_sc_w_in, v_sc_conv_w, v_sc_w_out, v_kv_in_norm, v_w_dkv, v_kv_latent_norm, v_w_kr, v_w_uk, v_w_uv, v_w_dq, v_q_latent_norm, v_w_uq, v_w_o, v_ffn_w_up, v_ffn_conv_w, v_ffn_conv_b, v_ffn_w_down):
    wts = dict(attn_norm=attn_norm, ffn_norm=ffn_norm, final_norm=final_norm, sc_w_in=sc_w_in, sc_conv_w=sc_conv_w,
               sc_w_out=sc_w_out, kv_in_norm=kv_in_norm, w_dkv=w_dkv, kv_latent_norm=kv_latent_norm, w_kr=w_kr,
               w_uk=w_uk, w_uv=w_uv, w_dq=w_dq, q_latent_norm=q_latent_norm, w_uq=w_uq, w_o=w_o, ffn_w_up=ffn_w_up,
               ffn_conv_w=ffn_conv_w, ffn_conv_b=ffn_conv_b, ffn_w_down=ffn_w_down)
    mom = dict(attn_norm=m_attn_norm, ffn_norm=m_ffn_norm, final_norm=m_final_norm, sc_w_in=m_sc_w_in,
               sc_conv_w=m_sc_conv_w, sc_w_out=m_sc_w_out, kv_in_norm=m_kv_in_norm, w_dkv=m_w_dkv,
               kv_latent_norm=m_kv_latent_norm, w_kr=m_w_kr, w_uk=m_w_uk, w_uv=m_w_uv, w_dq=m_w_dq,
               q_latent_norm=m_q_latent_norm, w_uq=m_w_uq, w_o=m_w_o, ffn_w_up=m_ffn_w_up, ffn_conv_w=m_ffn_conv_w,
               ffn_conv_b=m_ffn_conv_b, ffn_w_down=m_ffn_w_down)
    var = dict(attn_norm=v_attn_norm, ffn_norm=v_ffn_norm, final_norm=v_final_norm, sc_w_in=v_sc_w_in,
               sc_conv_w=v_sc_conv_w, sc_w_out=v_sc_w_out, kv_in_norm=v_kv_in_norm, w_dkv=v_w_dkv,
               kv_latent_norm=v_kv_latent_norm, w_kr=v_w_kr, w_uk=v_w_uk, w_uv=v_w_uv, w_dq=v_w_dq,
               q_latent_norm=v_q_latent_norm, w_uq=v_w_uq, w_o=v_w_o, ffn_w_up=v_ffn_w_up, ffn_conv_w=v_ffn_conv_w,
               ffn_conv_b=v_ffn_conv_b, ffn_w_down=v_ffn_w_down)
    xi, yi, ci = _place()
    me = 4 * xi + 2 * yi + ci

    def shard_view(name, a):
        if BIG[name] == "cols":
            return a.reshape(a.shape[-2], a.shape[-1])
        return a.reshape((-1,) + a.shape[-2:])

    names = list(BIG)
    items = [(shard_view(nm, wts[nm]).astype(BF16), BIG[nm]) for nm in names]
    items.append((_pack([sc_conv_w, ffn_conv_w], SMALL_W_ROWS).reshape(1, SMALL_W_ROWS, 128), "blocked"))
    gathered = _all_gather("ag_weights", items)
    full = dict(zip(names, gathered[:-1]))
    conv_all = gathered[-1].reshape(N_DEV, SMALL_W_ROWS * 128)
    scw = conv_all[:, :3 * 128].reshape(N_DEV, 3, 128).transpose(1, 0, 2).reshape(3, D)
    fcw = conv_all[:, 3 * 128:3 * 128 + 6 * 352].reshape(N_DEV, 2, 3, 352).transpose(1, 2, 0, 3).reshape(2, 3, D_FF)
    w = {
        "sc_w_in": full["sc_w_in"],
        "sc_conv_w": scw,
        "sc_w_out": full["sc_w_out"].reshape(D, D),
        "w_dkv": full["w_dkv"].reshape(D, KV_LORA),
        "w_kr": jnp.pad(full["w_kr"].reshape(D, QK_ROPE), ((0, 0), (0, 128 - QK_ROPE))),
        "w_uk": full["w_uk"],
        "w_uv": full["w_uv"],
        "w_dq": full["w_dq"].reshape(D, Q_LORA),
        "w_uq": jnp.pad(full["w_uq"].reshape(N_HEADS, Q_LORA, QK_NOPE + QK_ROPE),
                        ((0, 0), (0, 0), (0, QK_PAD - QK_NOPE - QK_ROPE))),
        "w_o": full["w_o"].reshape(D, D),
        "ffn_w_up": full["ffn_w_up"],
        "ffn_w_down": full["ffn_w_down"].reshape(2, N_FF_BLK, FF_BLK, D),
        "ffn_cw": fcw.reshape(2, 3, N_FF_BLK, FF_BLK).transpose(0, 2, 1, 3),
        "ffn_cb": ffn_conv_b.reshape(2, N_FF_BLK, 1, FF_BLK),
    }
    rep = {
        "attn_norm": attn_norm, "ffn_norm": ffn_norm, "final_norm": final_norm,
        "kv_in_norm": kv_in_norm.reshape(1, D), "kv_latent_norm": kv_latent_norm.reshape(1, KV_LORA),
        "q_latent_norm": q_latent_norm.reshape(1, Q_LORA),
    }

    loss, grad_x, g, small = _local_step(x.reshape(T, D), positions.reshape(T, 1), loss_target.reshape(T, D), rep, w)

    g["w_kr"] = g["w_kr"][:, :QK_ROPE]
    g["w_uq"] = g["w_uq"][:, :, :QK_NOPE + QK_ROPE]

    def grad_view(name, a):
        if BIG[name] == "cols":
            return a
        shard = shard_view(name, wts[name])
        return a.reshape((shard.shape[0], N_DEV) + shard.shape[1:])

    g_items = [(grad_view(nm, g[nm]), BIG[nm]) for nm in names]
    from_sibling = _sibling_exchange(g_items)
    c_arr = jnp.reshape(ci, (1,)).astype(jnp.int32)
    sums = [_chip_sum(t, gr, kind, rv, c_arr) for t, ((gr, kind), rv) in enumerate(zip(g_items, from_sibling))]
    from_chips = _chip_exchange(sums)
    my_chip = 2 * xi + yi
    chip_ids = jnp.stack([my_chip, my_chip ^ 1, my_chip ^ 2, my_chip ^ 3]).astype(jnp.int32)
    results = {}
    for t, nm in enumerate(names):
        results[nm] = _adamw_sharded(t, sums[t], from_chips[t], chip_ids, wts[nm], mom[nm], var[nm])

    small_order = list(REPLICATED) + ["sc_conv_w", "ffn_conv_w"]
    packed_g = _pack([loss[0, 0:1]] + [small[nm] for nm in small_order], SMALL_G_ROWS)
    parts = _all_gather("ag_small_grads", [(packed_g.reshape(1, SMALL_G_ROWS, 128), "blocked")])[0]
    parts = parts.reshape(N_DEV, SMALL_G_ROWS, 128)

    def full_params(src):
        scw_full = jnp.zeros((3, D), F32)
        scw_full = lax.dynamic_update_slice(scw_full, src["sc_conv_w"].reshape(3, 128), (0, me * 128))
        fcw_full = jnp.zeros((2, 3, D_FF), F32)
        fcw_full = lax.dynamic_update_slice(fcw_full, src["ffn_conv_w"], (0, 0, me * 352))
        return _pack([jnp.zeros((1,), F32)] + [src[nm] for nm in REPLICATED] + [scw_full, fcw_full], SMALL_G_ROWS)

    small_out = _adamw_small(parts, full_params(wts), full_params(mom), full_params(var))
    shapes = [(1,)] + [wts[nm].shape for nm in REPLICATED] + [(3, D), (2, 3, D_FF)]
    unpacked = [_unpack(o, shapes) for o in small_out]
    loss_total = unpacked[0][0].reshape(())
    for slot, nm in enumerate(small_order):
        vals = [u[slot + 1] for u in unpacked]
        if nm == "sc_conv_w":
            vals = [lax.dynamic_slice(a, (0, me * 128), (3, 128)).reshape(1, 3, 128) for a in vals]
        elif nm == "ffn_conv_w":
            vals = [lax.dynamic_slice(a, (0, 0, me * 352), (2, 3, 352)) for a in vals]
        results[nm] = vals

    outs = [loss_total, grad_x.reshape(1, T, D)]
    for slot in range(4):
        outs.extend(results[nm][slot] for nm in WEIGHTS)
    return tuple(outs)
```

```python
import functools

import jax
import jax.numpy as jnp
from jax import lax
from jax.experimental import pallas as pl
from jax.experimental.pallas import tpu as pltpu

F32 = jnp.float32
BF16 = jnp.bfloat16

T = 2048
D = 1024
N_HEADS = 8
QK_NOPE = 128
QK_ROPE = 64
V_HEAD = 128
Q_LORA = 384
KV_LORA = 256
D_FF = 2816
CHUNK = 64
ROPE_THETA = 10000.0
EPS = 1e-6
NEG_INF = -1e30
ADAM_LR = 0.001
ADAM_B1 = 0.9
ADAM_B2 = 0.999
ADAM_EPS = 1e-08
ADAM_WD = 0.01
ADAM_STEP = 10

N_DEV = 8
N_CHIP = 4
FF_BLK = D_FF * 2 // N_DEV
N_FF_BLK = D_FF // FF_BLK
QK_PAD = 256
HALO = 16

TM = 512
TR = 256
TQ = 256
VMEM_LIMIT = 56 * 1024 * 1024

NN = (((1,), (0,)), ((), ()))
NT = (((1,), (1,)), ((), ()))
TN = (((0,), (0,)), ((), ()))
MESH = pl.DeviceIdType.MESH


def _params(sem):
    return pltpu.CompilerParams(dimension_semantics=sem, vmem_limit_bytes=VMEM_LIMIT)


def _mm(name, a, b, *, grid, a_spec, b_spec, o_spec, o_shape, o_dtype, dims, k_axis=None, acc_shape=None,
        add=None, add_spec=None, alias=None):
    nk = grid[k_axis] if k_axis is not None else 1
    has_add = add is not None
    has_alias = alias is not None

    def body(*refs):
        a_ref, b_ref = refs[0], refs[1]
        p = 2
        add_ref = None
        if has_add:
            add_ref = refs[p]
            p += 1
        if has_alias:
            p += 1
        o_ref = refs[p]
        p += 1
        r = lax.dot_general(a_ref[...].astype(BF16), b_ref[...].astype(BF16), dims, preferred_element_type=F32)
        if k_axis is None:
            if has_add:
                r = r + add_ref[...].astype(F32)
            o_ref[...] = r.astype(o_dtype)
        else:
            acc = refs[p]
            k = pl.program_id(k_axis)

            @pl.when(k == 0)
            def _():
                acc[...] = r

            @pl.when(k > 0)
            def _():
                acc[...] += r

            @pl.when(k == nk - 1)
            def _():
                t = acc[...]
                if has_add:
                    t = t + add_ref[...].astype(F32)
                o_ref[...] = t.astype(o_dtype)

    in_specs = [a_spec, b_spec]
    args = [a, b]
    if has_add:
        in_specs.append(add_spec if add_spec is not None else o_spec)
        args.append(add)
    aliases = {}
    if has_alias:
        in_specs.append(pl.BlockSpec(memory_space=pl.ANY))
        aliases = {len(args): 0}
        args.append(alias)
    sem = tuple("arbitrary" if ax == k_axis else "parallel" for ax in range(len(grid)))
    scratch = [pltpu.VMEM(acc_shape, F32)] if k_axis is not None else []
    return pl.pallas_call(
        body, name=name, grid=grid, in_specs=in_specs, out_specs=o_spec,
        out_shape=jax.ShapeDtypeStruct(o_shape, o_dtype), scratch_shapes=scratch,
        input_output_aliases=aliases, compiler_params=_params(sem))(*args)


def _mm_rows(name, a, b, dims, o_dtype, n_out, *, tn=None, add=None):
    k = a.shape[1]
    tn = n_out if tn is None else tn
    if dims == NN:
        b_spec = pl.BlockSpec((k, tn), lambda n, i: (0, n))
    else:
        b_spec = pl.BlockSpec((tn, k), lambda n, i: (n, 0))
    return _mm(name, a, b, grid=(n_out // tn, T // TM),
               a_spec=pl.BlockSpec((TM, k), lambda n, i: (i, 0)), b_spec=b_spec,
               o_spec=pl.BlockSpec((TM, tn), lambda n, i: (i, n)), o_shape=(T, n_out), o_dtype=o_dtype,
               dims=dims, add=add)


def _mm_wgrad(name, a, b, *, tn=512):
    k, n = a.shape[1], b.shape[1]
    tn = min(tn, n)
    return _mm(name, a, b, grid=(n // tn,),
               a_spec=pl.BlockSpec((T, k), lambda j: (0, 0)), b_spec=pl.BlockSpec((T, tn), lambda j: (0, j)),
               o_spec=pl.BlockSpec((k, tn), lambda j: (0, j)), o_shape=(k, n), o_dtype=BF16, dims=TN)


def _rms_fwd(name, x, g):
    d = x.shape[1]

    def body(x_ref, g_ref, o_ref):
        xv = x_ref[...]
        r = lax.rsqrt(jnp.mean(xv * xv, axis=-1, keepdims=True) + EPS)
        o_ref[...] = ((xv * r) * g_ref[...]).astype(BF16)

    return pl.pallas_call(
        body, name=name, grid=(T // TM,),
        in_specs=[pl.BlockSpec((TM, d), lambda i: (i, 0)), pl.BlockSpec((1, d), lambda i: (0, 0))],
        out_specs=pl.BlockSpec((TM, d), lambda i: (i, 0)),
        out_shape=jax.ShapeDtypeStruct((T, d), BF16), compiler_params=_params(("parallel",)))(x, g)


def _rms_bwd(name, x, g, dy, dres=None):
    d = x.shape[1]
    has_res = dres is not None

    def body(*refs):
        if has_res:
            x_ref, g_ref, dy_ref, res_ref, dx_ref, dxb_ref, dg_ref = refs
        else:
            x_ref, g_ref, dy_ref, dx_ref, dxb_ref, dg_ref = refs
        xv = x_ref[...]
        r = lax.rsqrt(jnp.mean(xv * xv, axis=-1, keepdims=True) + EPS)
        xn = xv * r
        dyv = dy_ref[...].astype(F32)
        gdy = dyv * g_ref[...]
        dx = r * (gdy - xn * jnp.mean(gdy * xn, axis=-1, keepdims=True))
        if has_res:
            dx = dx + res_ref[...]
        dx_ref[...] = dx
        dxb_ref[...] = dx.astype(BF16)
        part = jnp.sum(dyv * xn, axis=0, keepdims=True)

        @pl.when(pl.program_id(0) == 0)
        def _():
            dg_ref[...] = part

        @pl.when(pl.program_id(0) > 0)
        def _():
            dg_ref[...] += part

    row = pl.BlockSpec((TR, d), lambda i: (i, 0))
    vec = pl.BlockSpec((1, d), lambda i: (0, 0))
    args = [x, g, dy] + ([dres] if has_res else [])
    in_specs = [row, vec, row] + ([row] if has_res else [])
    return pl.pallas_call(
        body, name=name, grid=(T // TR,), in_specs=in_specs, out_specs=[row, row, vec],
        out_shape=[jax.ShapeDtypeStruct((T, d), F32), jax.ShapeDtypeStruct((T, d), BF16),
                   jax.ShapeDtypeStruct((1, d), F32)],
        compiler_params=_params(("arbitrary",)))(*args)


def _final(h, g, tgt):
    def body(h_ref, g_ref, t_ref, loss_ref, dh_ref, dhb_ref, dg_ref):
        hv = h_ref[...]
        r = lax.rsqrt(jnp.mean(hv * hv, axis=-1, keepdims=True) + EPS)
        xn = hv * r
        gv = g_ref[...]
        err = xn * gv - t_ref[...]
        part_loss = 0.5 * jnp.sum(jnp.mean(err * err, axis=-1, keepdims=True), axis=0, keepdims=True)
        dy = err * (1.0 / D)
        gdy = dy * gv
        dh = r * (gdy - xn * jnp.mean(gdy * xn, axis=-1, keepdims=True))
        dh_ref[...] = dh
        dhb_ref[...] = dh.astype(BF16)
        part = jnp.sum(dy * xn, axis=0, keepdims=True)
        first = pl.program_id(0) == 0

        @pl.when(first)
        def _():
            dg_ref[...] = part
            loss_ref[...] = jnp.broadcast_to(part_loss, (1, 128))

        @pl.when(jnp.logical_not(first))
        def _():
            dg_ref[...] += part
            loss_ref[...] += jnp.broadcast_to(part_loss, (1, 128))

    row = pl.BlockSpec((TR, D), lambda i: (i, 0))
    vec = pl.BlockSpec((1, D), lambda i: (0, 0))
    return pl.pallas_call(
        body, name="final_loss", grid=(T // TR,), in_specs=[row, vec, row],
        out_specs=[pl.BlockSpec((1, 128), lambda i: (0, 0)), row, row, vec],
        out_shape=[jax.ShapeDtypeStruct((1, 128), F32), jax.ShapeDtypeStruct((T, D), F32),
                   jax.ShapeDtypeStruct((T, D), BF16), jax.ShapeDtypeStruct((1, D), F32)],
        compiler_params=_params(("arbitrary",)))(h, g, tgt)


def _prev_idx(i):
    return jnp.maximum(i * (TR // HALO) - 1, 0)


def _next_idx(i):
    return jnp.minimum((i + 1) * (TR // HALO), T // HALO - 1)


def _causal_taps(ext):
    return pltpu.roll(ext, 2, 0)[HALO:], pltpu.roll(ext, 1, 0)[HALO:], ext[HALO:]


def _anticausal_taps(ext, n):
    rows = ext.shape[0]
    return pltpu.roll(ext, rows - 1, 0)[:n], pltpu.roll(ext, rows - 2, 0)[:n]


def _sc_fwd(z, w):
    def body(b_ref, c_ref, ch_ref, u_ref, uh_ref, w_ref, y_ref):
        i = pl.program_id(0)
        cu = c_ref[...].astype(F32) * u_ref[...].astype(F32)
        cuh = ch_ref[...].astype(F32) * uh_ref[...].astype(F32)
        cuh = jnp.where(i > 0, cuh, 0.0)
        x2, x1, x0 = _causal_taps(jnp.concatenate([cuh, cu], axis=0))
        wv = w_ref[...]
        cv = (x2 * wv[0:1] + x1 * wv[1:2]) + x0 * wv[2:3]
        y_ref[...] = (b_ref[...].astype(F32) * cv).astype(BF16)

    def main(part):
        return pl.BlockSpec((TR, D), lambda i: (i, part))

    def halo(part):
        return pl.BlockSpec((HALO, D), lambda i: (_prev_idx(i), part))

    return pl.pallas_call(
        body, name="sc_fwd", grid=(T // TR,),
        in_specs=[main(0), main(1), halo(1), main(2), halo(2), pl.BlockSpec((3, D), lambda i: (0, 0))],
        out_specs=pl.BlockSpec((TR, D), lambda i: (i, 0)),
        out_shape=jax.ShapeDtypeStruct((T, D), BF16), compiler_params=_params(("parallel",)))(z, z, z, z, z, w)


def _sc_bwd(z, dy, w):
    last = T // TR - 1

    def body(b_ref, bn_ref, c_ref, ch_ref, u_ref, uh_ref, dy_ref, dyn_ref, w_ref, dz_ref, dw_ref):
        i = pl.program_id(0)
        cv_ = c_ref[...].astype(F32)
        uv = u_ref[...].astype(F32)
        cu = cv_ * uv
        cuh = jnp.where(i > 0, ch_ref[...].astype(F32) * uh_ref[...].astype(F32), 0.0)
        x2, x1, x0 = _causal_taps(jnp.concatenate([cuh, cu], axis=0))
        wv = w_ref[...]
        conv = (x2 * wv[0:1] + x1 * wv[1:2]) + x0 * wv[2:3]
        dyv = dy_ref[...]
        dz_ref[:, 0:D] = (dyv * conv).astype(BF16)
        dconv = dyv * b_ref[...].astype(F32)
        dconv_n = jnp.where(i < last, dyn_ref[...] * bn_ref[...].astype(F32), 0.0)
        n1, n2 = _anticausal_taps(jnp.concatenate([dconv, dconv_n], axis=0), TR)
        dcu = (dconv * wv[2:3] + n1 * wv[1:2]) + n2 * wv[0:1]
        dz_ref[:, D:2 * D] = (dcu * uv).astype(BF16)
        dz_ref[:, 2 * D:3 * D] = (dcu * cv_).astype(BF16)
        part = jnp.concatenate([jnp.sum(dconv * x2, axis=0, keepdims=True),
                                jnp.sum(dconv * x1, axis=0, keepdims=True),
                                jnp.sum(dconv * x0, axis=0, keepdims=True)], axis=0)

        @pl.when(i == 0)
        def _():
            dw_ref[...] = part

        @pl.when(i > 0)
        def _():
            dw_ref[...] += part

    def main(part):
        return pl.BlockSpec((TR, D), lambda i: (i, part))

    def prev(part):
        return pl.BlockSpec((HALO, D), lambda i: (_prev_idx(i), part))

    def nxt(part):
        return pl.BlockSpec((HALO, D), lambda i: (_next_idx(i), part))

    wspec = pl.BlockSpec((3, D), lambda i: (0, 0))
    return pl.pallas_call(
        body, name="sc_bwd", grid=(T // TR,),
        in_specs=[main(0), nxt(0), main(1), prev(1), main(2), prev(2), main(0), nxt(0), wspec],
        out_specs=[pl.BlockSpec((TR, 3 * D), lambda i: (i, 0)), wspec],
        out_shape=[jax.ShapeDtypeStruct((T, 3 * D), BF16), jax.ShapeDtypeStruct((3, D), F32)],
        compiler_params=_params(("arbitrary",)))(z, z, z, z, z, z, dy, dy, w)


def _sigmoid(x):
    return 1.0 / (1.0 + jnp.exp(-x))


def _ffn_fwd(name, gv, w, b):
    def body(g_ref, gh_ref, v_ref, w_ref, b_ref, a_ref):
        i = pl.program_id(1)
        g = g_ref[...].astype(F32)
        gh = jnp.where(i > 0, gh_ref[...].astype(F32), 0.0)
        x2, x1, x0 = _causal_taps(jnp.concatenate([gh, g], axis=0))
        wv = w_ref[...]
        gc = ((x2 * wv[0:1] + x1 * wv[1:2]) + x0 * wv[2:3]) + b_ref[...]
        a_ref[...] = ((gc * _sigmoid(gc)) * v_ref[...].astype(F32)).astype(BF16)

    blk = (None, TR, FF_BLK)
    return pl.pallas_call(
        body, name=name, grid=(N_FF_BLK, T // TR),
        in_specs=[pl.BlockSpec(blk, lambda j, i: (j, i, 0)),
                  pl.BlockSpec((None, HALO, FF_BLK), lambda j, i: (j, _prev_idx(i), 0)),
                  pl.BlockSpec(blk, lambda j, i: (j + N_FF_BLK, i, 0)),
                  pl.BlockSpec((None, 3, FF_BLK), lambda j, i: (j, 0, 0)),
                  pl.BlockSpec((None, 1, FF_BLK), lambda j, i: (j, 0, 0))],
        out_specs=pl.BlockSpec(blk, lambda j, i: (j, i, 0)),
        out_shape=jax.ShapeDtypeStruct((N_FF_BLK, T, FF_BLK), BF16),
        compiler_params=_params(("parallel", "parallel")))(gv, gv, gv, w, b)


def _ffn_bwd(name, gv, dact, w, b):
    last = T // TR - 1

    def body(g_ref, gp_ref, gn_ref, v_ref, vn_ref, da_ref, dan_ref, w_ref, b_ref, dg_ref, dv_ref, dw_ref, db_ref):
        i = pl.program_id(1)
        gp = jnp.where(i > 0, gp_ref[...].astype(F32), 0.0)
        ext = jnp.concatenate([gp, g_ref[...].astype(F32), gn_ref[...].astype(F32)], axis=0)
        x2, x1, x0 = _causal_taps(ext)
        wv = w_ref[...]
        gc = ((x2 * wv[0:1] + x1 * wv[1:2]) + x0 * wv[2:3]) + b_ref[...]
        sg = _sigmoid(gc)
        da = jnp.concatenate([da_ref[...].astype(F32), jnp.where(i < last, dan_ref[...].astype(F32), 0.0)], axis=0)
        vv = jnp.concatenate([v_ref[...].astype(F32), vn_ref[...].astype(F32)], axis=0)
        dv_ref[...] = (da[:TR] * (gc[:TR] * sg[:TR])).astype(BF16)
        dgc = (da * vv) * (sg * (1.0 + gc * (1.0 - sg)))
        n1, n2 = _anticausal_taps(dgc, TR)
        d0 = dgc[:TR]
        dg_ref[...] = ((d0 * wv[2:3] + n1 * wv[1:2]) + n2 * wv[0:1]).astype(BF16)
        part_w = jnp.concatenate([jnp.sum(d0 * x2[:TR], axis=0, keepdims=True),
                                  jnp.sum(d0 * x1[:TR], axis=0, keepdims=True),
                                  jnp.sum(d0 * x0[:TR], axis=0, keepdims=True)], axis=0)
        part_b = jnp.sum(d0, axis=0, keepdims=True)

        @pl.when(i == 0)
        def _():
            dw_ref[...] = part_w
            db_ref[...] = part_b

        @pl.when(i > 0)
        def _():
            dw_ref[...] += part_w
            db_ref[...] += part_b

    blk = (None, TR, FF_BLK)
    hblk = (None, HALO, FF_BLK)
    wspec = pl.BlockSpec((None, 3, FF_BLK), lambda j, i: (j, 0, 0))
    bspec = pl.BlockSpec((None, 1, FF_BLK), lambda j, i: (j, 0, 0))
    return pl.pallas_call(
        body, name=name, grid=(N_FF_BLK, T // TR),
        in_specs=[pl.BlockSpec(blk, lambda j, i: (j, i, 0)),
                  pl.BlockSpec(hblk, lambda j, i: (j, _prev_idx(i), 0)),
                  pl.BlockSpec(hblk, lambda j, i: (j, _next_idx(i), 0)),
                  pl.BlockSpec(blk, lambda j, i: (j + N_FF_BLK, i, 0)),
                  pl.BlockSpec(hblk, lambda j, i: (j + N_FF_BLK, _next_idx(i), 0)),
                  pl.BlockSpec(blk, lambda j, i: (j, i, 0)),
                  pl.BlockSpec(hblk, lambda j, i: (j, _next_idx(i), 0)),
                  wspec, bspec],
        out_specs=[pl.BlockSpec(blk, lambda j, i: (j, i, 0)), pl.BlockSpec(blk, lambda j, i: (j, i, 0)), wspec, bspec],
        out_shape=[jax.ShapeDtypeStruct((N_FF_BLK, T, FF_BLK), BF16), jax.ShapeDtypeStruct((N_FF_BLK, T, FF_BLK), BF16),
                   jax.ShapeDtypeStruct((N_FF_BLK, 3, FF_BLK), F32), jax.ShapeDtypeStruct((N_FF_BLK, 1, FF_BLK), F32)],
        compiler_params=_params(("parallel", "arbitrary")))(gv, gv, gv, gv, gv, dact, dact, w, b)


def _rope_tables(pos, inv_freq):
    half = QK_ROPE // 2

    def body(p_ref, f_ref, c_ref, sa_ref, sb_ref):
        ang = p_ref[...].astype(F32) * f_ref[...]
        lane = lax.broadcasted_iota(jnp.int32, (T, 128), 1)
        c = jnp.cos(ang)
        s = jnp.sin(ang)
        c_ref[...] = jnp.where(lane < 2 * half, c, 0.0)
        sa_ref[...] = jnp.where(lane < half, -s, 0.0)
        sb_ref[...] = jnp.where(jnp.logical_and(lane >= half, lane < 2 * half), s, 0.0)

    return pl.pallas_call(
        body, name="rope_tables", out_shape=[jax.ShapeDtypeStruct((T, 128), F32)] * 3,
        compiler_params=pltpu.CompilerParams(vmem_limit_bytes=VMEM_LIMIT))(pos, inv_freq)


def _rope(name, x, tables, sign, out_dtype, reduce_groups=False):
    g, _, w = x.shape
    cos, sa, sb = tables

    def body(x_ref, c_ref, sa_ref, sb_ref, o_ref):
        xv = x_ref[...].astype(F32)
        if reduce_groups:
            acc = xv[0]
            for k in range(1, g):
                acc = acc + xv[k]
            xv = acc
        r = xv[:, w - 128:]
        out = r * c_ref[...] + sign * (pltpu.roll(r, 96, 1) * sa_ref[...] + pltpu.roll(r, 32, 1) * sb_ref[...])
        if w > 128:
            o_ref[:, :w - 128] = xv[:, :w - 128].astype(out_dtype)
        o_ref[:, w - 128:] = out.astype(out_dtype)

    tab = pl.BlockSpec((TM, 128), lambda h, i: (i, 0))
    if reduce_groups:
        x_spec = pl.BlockSpec((g, TM, w), lambda h, i: (0, i, 0))
        groups = 1
    else:
        x_spec = pl.BlockSpec((None, TM, w), lambda h, i: (h, i, 0))
        groups = g
    return pl.pallas_call(
        body, name=name, grid=(groups, T // TM), in_specs=[x_spec, tab, tab, tab],
        out_specs=pl.BlockSpec((None, TM, w), lambda h, i: (h, i, 0)),
        out_shape=jax.ShapeDtypeStruct((groups, T, w), out_dtype),
        compiler_params=_params(("parallel", "parallel")))(x, cos, sa, sb)


SCALE = (QK_NOPE + QK_ROPE) ** -0.5


def _chunk_mask(q0, k0, transposed):
    a = lax.broadcasted_iota(jnp.int32, (TQ, TQ), 0)
    b = lax.broadcasted_iota(jnp.int32, (TQ, TQ), 1)
    qpos, kpos = (q0 + b, k0 + a) if transposed else (q0 + a, k0 + b)
    return (kpos // CHUNK) <= (qpos // CHUNK)


def _attn_fwd(q, kn, kr, v):
    def body(q_ref, kn_ref, kr_ref, v_ref, o_ref, lse_ref):
        i = pl.program_id(1)
        qv = q_ref[...]

        def step(j, carry):
            m, l, acc = carry
            off = pl.multiple_of(j * TQ, TQ)
            kk = jnp.concatenate([kn_ref[pl.ds(off, TQ), :], kr_ref[pl.ds(off, TQ), :]], axis=1)
            s = lax.dot_general(qv, kk, NT, preferred_element_type=F32) * SCALE
            s = jnp.where(_chunk_mask(i * TQ, j * TQ, False), s, NEG_INF)
            m_new = jnp.maximum(m, jnp.max(s, axis=-1, keepdims=True))
            p = jnp.exp(s - m_new)
            alpha = jnp.exp(m - m_new)
            l = alpha * l + jnp.sum(p, axis=-1, keepdims=True)
            acc = alpha * acc + lax.dot_general(p.astype(BF16), v_ref[pl.ds(off, TQ), :], NN, preferred_element_type=F32)
            return m_new, l, acc

        init = (jnp.full((TQ, 1), NEG_INF, F32), jnp.zeros((TQ, 1), F32), jnp.zeros((TQ, V_HEAD), F32))
        m, l, acc = lax.fori_loop(0, i + 1, step, init)
        o_ref[...] = (acc / l).astype(BF16)
        lse_ref[...] = m + jnp.log(l)

    return pl.pallas_call(
        body, name="attn_fwd", grid=(N_HEADS, T // TQ),
        in_specs=[pl.BlockSpec((None, TQ, QK_PAD), lambda h, i: (h, i, 0)),
                  pl.BlockSpec((T, QK_NOPE), lambda h, i: (0, h)),
                  pl.BlockSpec((T, 128), lambda h, i: (0, 0)),
                  pl.BlockSpec((T, V_HEAD), lambda h, i: (0, h))],
        out_specs=[pl.BlockSpec((TQ, V_HEAD), lambda h, i: (i, h)), pl.BlockSpec((None, TQ, 1), lambda h, i: (h, i, 0))],
        out_shape=[jax.ShapeDtypeStruct((T, N_HEADS * V_HEAD), BF16), jax.ShapeDtypeStruct((N_HEADS, T, 1), F32)],
        compiler_params=_params(("parallel", "parallel")))(q, kn, kr, v)


def _attn_bwd_dq(q, kn, kr, v, o, do, lse):
    def body(q_ref, kn_ref, kr_ref, v_ref, o_ref, do_ref, lse_ref, dq_ref, dl_ref):
        i = pl.program_id(1)
        qv = q_ref[...]
        dov = do_ref[...]
        lse = lse_ref[...]
        delta = jnp.sum(dov.astype(F32) * o_ref[...].astype(F32), axis=-1, keepdims=True)
        dl_ref[...] = delta

        def step(j, dq):
            off = pl.multiple_of(j * TQ, TQ)
            kk = jnp.concatenate([kn_ref[pl.ds(off, TQ), :], kr_ref[pl.ds(off, TQ), :]], axis=1)
            s = lax.dot_general(qv, kk, NT, preferred_element_type=F32) * SCALE
            s = jnp.where(_chunk_mask(i * TQ, j * TQ, False), s, NEG_INF)
            p = jnp.exp(s - lse)
            dp = lax.dot_general(dov, v_ref[pl.ds(off, TQ), :], NT, preferred_element_type=F32)
            ds = (p * (dp - delta)) * SCALE
            return dq + lax.dot_general(ds.astype(BF16), kk, NN, preferred_element_type=F32)

        dq_ref[...] = lax.fori_loop(0, i + 1, step, jnp.zeros((TQ, QK_PAD), F32))

    col = pl.BlockSpec((None, TQ, 1), lambda h, i: (h, i, 0))
    head = pl.BlockSpec((TQ, V_HEAD), lambda h, i: (i, h))
    return pl.pallas_call(
        body, name="attn_bwd_dq", grid=(N_HEADS, T // TQ),
        in_specs=[pl.BlockSpec((None, TQ, QK_PAD), lambda h, i: (h, i, 0)),
                  pl.BlockSpec((T, QK_NOPE), lambda h, i: (0, h)),
                  pl.BlockSpec((T, 128), lambda h, i: (0, 0)),
                  pl.BlockSpec((T, V_HEAD), lambda h, i: (0, h)),
                  head, head, col],
        out_specs=[pl.BlockSpec((None, TQ, QK_PAD), lambda h, i: (h, i, 0)), col],
        out_shape=[jax.ShapeDtypeStruct((N_HEADS, T, QK_PAD), F32), jax.ShapeDtypeStruct((N_HEADS, T, 1), F32)],
        compiler_params=_params(("parallel", "parallel")))(q, kn, kr, v, o, do, lse)


def _attn_bwd_dkv(q, kn, kr, v, do, lse_row, delta_row):
    nq = T // TQ

    def body(q_ref, kn_ref, kr_ref, v_ref, do_ref, lse_ref, dl_ref, dkn_ref, dkr_ref, dv_ref):
        j = pl.program_id(1)
        kk = jnp.concatenate([kn_ref[...], kr_ref[...]], axis=1)
        vv = v_ref[...]

        def step(i, carry):
            dk, dv = carry
            off = pl.multiple_of(i * TQ, TQ)
            qi = q_ref[pl.ds(off, TQ), :]
            doi = do_ref[pl.ds(off, TQ), :]
            st = lax.dot_general(kk, qi, NT, preferred_element_type=F32) * SCALE
            st = jnp.where(_chunk_mask(i * TQ, j * TQ, True), st, NEG_INF)
            pt = jnp.exp(st - lse_ref[:, pl.ds(off, TQ)])
            dv = dv + lax.dot_general(pt.astype(BF16), doi, NN, preferred_element_type=F32)
            dpt = lax.dot_general(vv, doi, NT, preferred_element_type=F32)
            dst = (pt * (dpt - dl_ref[:, pl.ds(off, TQ)])) * SCALE
            dk = dk + lax.dot_general(dst.astype(BF16), qi, NN, preferred_element_type=F32)
            return dk, dv

        dk, dv = lax.fori_loop(j, nq, step, (jnp.zeros((TQ, QK_PAD), F32), jnp.zeros((TQ, V_HEAD), F32)))
        dkn_ref[...] = dk[:, :QK_NOPE].astype(BF16)
        dkr_ref[...] = dk[:, QK_NOPE:]
        dv_ref[...] = dv.astype(BF16)

    row = pl.BlockSpec((None, 1, T), lambda h, j: (h, 0, 0))
    head = pl.BlockSpec((TQ, 128), lambda h, j: (j, h))
    return pl.pallas_call(
        body, name="attn_bwd_dkv", grid=(N_HEADS, nq),
        in_specs=[pl.BlockSpec((None, T, QK_PAD), lambda h, j: (h, 0, 0)),
                  head, pl.BlockSpec((TQ, 128), lambda h, j: (j, 0)), head,
                  pl.BlockSpec((T, V_HEAD), lambda h, j: (0, h)), row, row],
        out_specs=[head, pl.BlockSpec((None, TQ, 128), lambda h, j: (h, j, 0)), head],
        out_shape=[jax.ShapeDtypeStruct((T, N_HEADS * QK_NOPE), BF16), jax.ShapeDtypeStruct((N_HEADS, T, 128), F32),
                   jax.ShapeDtypeStruct((T, N_HEADS * V_HEAD), BF16)],
        compiler_params=_params(("parallel", "parallel")))(q, kn, kr, v, do, lse_row, delta_row)


def _ffn_layer_fwd(tag, h, gain, w_up, w_down4, cw, cb, mid=None):
    hf = _rms_fwd(f"{tag}_norm", h, gain)
    gv = _mm(f"{tag}_up", hf, w_up, grid=(N_DEV, T // TM),
             a_spec=pl.BlockSpec((TM, D), lambda j, i: (i, 0)),
             b_spec=pl.BlockSpec((None, None, D, FF_BLK), lambda j, i: (0, j, 0, 0)),
             o_spec=pl.BlockSpec((None, TM, FF_BLK), lambda j, i: (j, i, 0)),
             o_shape=(N_DEV, T, FF_BLK), o_dtype=BF16, dims=NN)
    act = _ffn_fwd(f"{tag}_act", gv, cw, cb)
    if mid is not None:
        mid(act)
    tn = 512
    out = _mm(f"{tag}_down", act, w_down4, grid=(D // tn, T // TM, N_FF_BLK),
              a_spec=pl.BlockSpec((None, TM, FF_BLK), lambda n, i, k: (k, i, 0)),
              b_spec=pl.BlockSpec((None, None, FF_BLK, tn), lambda n, i, k: (0, k, 0, n)),
              o_spec=pl.BlockSpec((TM, tn), lambda n, i, k: (i, n)), o_shape=(T, D), o_dtype=F32,
              dims=NN, k_axis=2, acc_shape=(TM, tn), add=h, add_spec=pl.BlockSpec((TM, tn), lambda n, i, k: (i, n)))
    return out, (hf, gv, act)


def _ffn_layer_bwd(tag, l, h, gain, w_up, w_down4, cw, cb, saved, dh, dh_bf, g_up, g_down):
    hf, gv, act = saved
    dact = _mm(f"{tag}_dact", dh_bf, w_down4, grid=(N_FF_BLK, T // TM),
               a_spec=pl.BlockSpec((TM, D), lambda j, i: (i, 0)),
               b_spec=pl.BlockSpec((None, None, FF_BLK, D), lambda j, i: (0, j, 0, 0)),
               o_spec=pl.BlockSpec((None, TM, FF_BLK), lambda j, i: (j, i, 0)),
               o_shape=(N_FF_BLK, T, FF_BLK), o_dtype=BF16, dims=NT)
    tn = 512
    g_down = _mm(f"{tag}_gdown", act, dh_bf, grid=(N_FF_BLK, D // tn),
                 a_spec=pl.BlockSpec((None, T, FF_BLK), lambda j, n: (j, 0, 0)),
                 b_spec=pl.BlockSpec((T, tn), lambda j, n: (0, n)),
                 o_spec=pl.BlockSpec((None, FF_BLK, tn), lambda j, n: (l, j, n)),
                 o_shape=(2, D_FF, D), o_dtype=BF16, dims=TN, alias=g_down)
    dg, dv, dcw, dcb = _ffn_bwd(f"{tag}_dact_ew", gv, dact, cw, cb)
    dhf = None
    for half, (name, dpart) in enumerate((("g", dg), ("v", dv))):
        dhf = _mm(f"{tag}_dhf_{name}", dpart, w_up, grid=(T // TM, N_FF_BLK),
                  a_spec=pl.BlockSpec((None, TM, FF_BLK), lambda i, k: (k, i, 0)),
                  b_spec=pl.BlockSpec((None, None, D, FF_BLK), lambda i, k, half=half: (0, k + half * N_FF_BLK, 0, 0)),
                  o_spec=pl.BlockSpec((TM, D), lambda i, k: (i, 0)), o_shape=(T, D), o_dtype=F32,
                  dims=NT, k_axis=1, acc_shape=(TM, D), add=dhf)
        g_up = _mm(f"{tag}_gup_{name}", hf, dpart, grid=(N_FF_BLK,),
                   a_spec=pl.BlockSpec((T, D), lambda j: (0, 0)),
                   b_spec=pl.BlockSpec((None, T, FF_BLK), lambda j: (j, 0, 0)),
                   o_spec=pl.BlockSpec((None, None, D, FF_BLK), lambda j, half=half: (l, j + half * N_FF_BLK, 0, 0)),
                   o_shape=(2, N_DEV, D, FF_BLK), o_dtype=BF16, dims=TN, alias=g_up)
    dh_in, dh_in_bf, dgain = _rms_bwd(f"{tag}_dnorm", h, gain, dhf, dres=dh)
    return dh_in, dh_in_bf, dgain, dcw, dcb, g_up, g_down


def _local_step(x, pos, tgt, rep, w, late=None):
    w = dict(w)
    mid, finish = late if late is not None else (None, None)
    attn_norm, ffn_norm, final_norm = rep["attn_norm"], rep["ffn_norm"], rep["final_norm"]
    half = QK_ROPE // 2
    inv = 1.0 / (ROPE_THETA ** (jnp.arange(half, dtype=F32) / half))
    inv_freq = jnp.concatenate([inv, inv, jnp.zeros((128 - 2 * half,), F32)]).reshape(1, 128)
    tables = _rope_tables(pos, inv_freq)

    hn0 = _rms_fwd("l0_norm", x, attn_norm[0:1])
    z = _mm_rows("l0_in", hn0, w["sc_w_in"], NN, BF16, 3 * D, tn=512)
    y = _sc_fwd(z, w["sc_conv_w"])
    h1 = _mm_rows("l0_out", y, w["sc_w_out"], NN, F32, D, tn=512, add=x)
    h2, ffn0 = _ffn_layer_fwd("f0", h1, ffn_norm[0:1], w["ffn_w_up0"], w["ffn_w_down0"], w["ffn_cw"][0], w["ffn_cb"][0],
                              mid=mid)
    if finish is not None:
        w.update(finish(h2))

    hk = _rms_fwd("kv_norm", h2, rep["kv_in_norm"])
    ckv_raw = _mm_rows("kv_down", hk, w["w_dkv"], NN, F32, KV_LORA)
    kr_raw = _mm_rows("kv_rope", hk, w["w_kr"], NN, F32, 128)
    ckv = _rms_fwd("kv_lnorm", ckv_raw, rep["kv_latent_norm"])
    kn = _mm_rows("kv_uk", ckv, w["w_uk"], NN, BF16, N_HEADS * QK_NOPE)
    vv = _mm_rows("kv_uv", ckv, w["w_uv"], NN, BF16, N_HEADS * V_HEAD)
    kr = _rope("k_rope", kr_raw.reshape(1, T, 128), tables, 1.0, BF16).reshape(T, 128)

    hn1 = _rms_fwd("l1_norm", h2, attn_norm[1:2])
    cq_raw = _mm_rows("q_down", hn1, w["w_dq"], NN, F32, Q_LORA)
    cq = _rms_fwd("q_lnorm", cq_raw, rep["q_latent_norm"])
    q_raw = _mm("q_up", cq, w["w_uq"], grid=(N_HEADS, T // TM),
                a_spec=pl.BlockSpec((TM, Q_LORA), lambda h, i: (i, 0)),
                b_spec=pl.BlockSpec((None, Q_LORA, QK_PAD), lambda h, i: (h, 0, 0)),
                o_spec=pl.BlockSpec((None, TM, QK_PAD), lambda h, i: (h, i, 0)),
                o_shape=(N_HEADS, T, QK_PAD), o_dtype=F32, dims=NN)
    q = _rope("q_rope", q_raw, tables, 1.0, BF16)
    o, lse = _attn_fwd(q, kn, kr, vv)
    h3 = _mm_rows("attn_out", o, w["w_o"], NN, F32, D, tn=512, add=h2)
    h4, ffn1 = _ffn_layer_fwd("f1", h3, ffn_norm[1:2], w["ffn_w_up1"], w["ffn_w_down1"], w["ffn_cw"][1], w["ffn_cb"][1])

    loss, dh4, dh4_bf, d_final = _final(h4, final_norm.reshape(1, D), tgt)

    g = {}
    dh3, dh3_bf, d_fn1, dcw1, dcb1, g_up, g_down = _ffn_layer_bwd(
        "f1", 1, h3, ffn_norm[1:2], w["ffn_w_up1"], w["ffn_w_down1"], w["ffn_cw"][1], w["ffn_cb"][1], ffn1, dh4, dh4_bf,
        None, None)

    do = _mm_rows("d_attn_out", dh3_bf, w["w_o"], NT, BF16, N_HEADS * V_HEAD)
    g["w_o"] = _mm_wgrad("g_w_o", o, dh3_bf)
    dq, delta = _attn_bwd_dq(q, kn, kr, vv, o, do, lse)
    dkn, dkr, dvv = _attn_bwd_dkv(q, kn, kr, vv, do, lse.reshape(N_HEADS, 1, T), delta.reshape(N_HEADS, 1, T))
    dq_pre = _rope("dq_rope", dq, tables, -1.0, BF16)
    dcq = _mm("d_q_up", dq_pre, w["w_uq"], grid=(T // TM, N_HEADS),
              a_spec=pl.BlockSpec((None, TM, QK_PAD), lambda i, h: (h, i, 0)),
              b_spec=pl.BlockSpec((None, Q_LORA, QK_PAD), lambda i, h: (h, 0, 0)),
              o_spec=pl.BlockSpec((TM, Q_LORA), lambda i, h: (i, 0)), o_shape=(T, Q_LORA), o_dtype=F32,
              dims=NT, k_axis=1, acc_shape=(TM, Q_LORA))
    g["w_uq"] = _mm("g_w_uq", cq, dq_pre, grid=(N_HEADS,),
                    a_spec=pl.BlockSpec((T, Q_LORA), lambda h: (0, 0)),
                    b_spec=pl.BlockSpec((None, T, QK_PAD), lambda h: (h, 0, 0)),
                    o_spec=pl.BlockSpec((None, Q_LORA, QK_PAD), lambda h: (h, 0, 0)),
                    o_shape=(N_HEADS, Q_LORA, QK_PAD), o_dtype=BF16, dims=TN)
    _, dcq_raw_bf, d_qln = _rms_bwd("d_q_lnorm", cq_raw, rep["q_latent_norm"], dcq)
    dhn1 = _mm_rows("d_q_down", dcq_raw_bf, w["w_dq"], NT, F32, D)
    g["w_dq"] = _mm_wgrad("g_w_dq", hn1, dcq_raw_bf)
    dh2_a, _, d_an1 = _rms_bwd("d_l1_norm", h2, attn_norm[1:2], dhn1, dres=dh3)

    dckv = _mm_rows("d_kv_uk", dkn, w["w_uk"], NT, F32, KV_LORA)
    dckv = _mm_rows("d_kv_uv", dvv, w["w_uv"], NT, F32, KV_LORA, add=dckv)
    g["w_uk"] = _mm_wgrad("g_w_uk", ckv, dkn)
    g["w_uv"] = _mm_wgrad("g_w_uv", ckv, dvv)
    _, dckv_raw_bf, d_kvln = _rms_bwd("d_kv_lnorm", ckv_raw, rep["kv_latent_norm"], dckv)
    dkr_raw_bf = _rope("dk_rope", dkr, tables, -1.0, BF16, reduce_groups=True).reshape(T, 128)
    dhk = _mm_rows("d_kv_down", dckv_raw_bf, w["w_dkv"], NT, F32, D)
    dhk = _mm_rows("d_kv_rope", dkr_raw_bf, w["w_kr"], NT, F32, D, add=dhk)
    g["w_dkv"] = _mm_wgrad("g_w_dkv", hk, dckv_raw_bf)
    g["w_kr"] = _mm_wgrad("g_w_kr", hk, dkr_raw_bf)
    dh2, dh2_bf, d_kvin = _rms_bwd("d_kv_norm", h2, rep["kv_in_norm"], dhk, dres=dh2_a)

    dh1, dh1_bf, d_fn0, dcw0, dcb0, g_up, g_down = _ffn_layer_bwd(
        "f0", 0, h1, ffn_norm[0:1], w["ffn_w_up0"], w["ffn_w_down0"], w["ffn_cw"][0], w["ffn_cb"][0], ffn0, dh2, dh2_bf,
        g_up, g_down)
    g["ffn_w_up"] = g_up
    g["ffn_w_down"] = g_down

    dy = _mm_rows("d_l0_out", dh1_bf, w["sc_w_out"], NT, F32, D)
    g["sc_w_out"] = _mm_wgrad("g_sc_w_out", y, dh1_bf)
    dz, d_scw = _sc_bwd(z, dy, w["sc_conv_w"])
    dhn0 = _mm_rows("d_l0_in", dz, w["sc_w_in"], NT, F32, D)
    g["sc_w_in"] = _mm_wgrad("g_sc_w_in", hn0, dz)
    grad_x, _, d_an0 = _rms_bwd("d_l0_norm", x, attn_norm[0:1], dhn0, dres=dh1)

    small = {
        "attn_norm": jnp.concatenate([d_an0, d_an1], axis=0),
        "ffn_norm": jnp.concatenate([d_fn0, d_fn1], axis=0),
        "final_norm": d_final.reshape(D),
        "kv_in_norm": d_kvin.reshape(D),
        "kv_latent_norm": d_kvln.reshape(KV_LORA),
        "q_latent_norm": d_qln,
        "ffn_conv_b": jnp.stack([dcb0, dcb1]).transpose(0, 2, 1, 3).reshape(2, D_FF),
        "sc_conv_w": d_scw,
        "ffn_conv_w": jnp.stack([dcw0, dcw1]).transpose(0, 2, 1, 3).reshape(2, 3, D_FF),
    }
    return loss, grad_x, g, small


def _place():
    return lax.axis_index("x"), lax.axis_index("y"), lax.axis_index("c")


def _window(ref, kind, dev):
    if kind == "blocked":
        return ref.at[:, dev]
    width = ref.shape[-1] // N_DEV
    return ref.at[:, pl.ds(pl.multiple_of(dev * width, 128), width)]


def _all_gather(name, items):
    n = len(items)
    out_shapes = []
    for shard, kind in items:
        if kind == "blocked":
            shape = (shard.shape[0], N_DEV) + shard.shape[1:]
        else:
            shape = (shard.shape[0], N_DEV * shard.shape[1])
        out_shapes.append(jax.ShapeDtypeStruct(shape, shard.dtype))

    def body(*refs):
        srcs, outs = refs[:n], refs[n:2 * n]
        send_sems, recv_sems, local_sems = refs[2 * n:]
        x, y, c = _place()
        me = 4 * x + 2 * y + c
        sibling = (x, y, 1 - c)
        chips = [(1 - x, y), (x, 1 - y), (1 - x, 1 - y)]

        def num(px, py, pc):
            return 4 * px + 2 * py + pc

        def copy(t, k, dev, to, from_src):
            kind = items[t][1]
            dst = _window(outs[t], kind, dev)
            return pltpu.make_async_remote_copy(
                src_ref=srcs[t] if from_src else dst, dst_ref=dst,
                send_sem=send_sems.at[t, k], recv_sem=recv_sems.at[t, k], device_id=to, device_id_type=MESH)

        mine = [pltpu.make_async_copy(srcs[t], _window(outs[t], items[t][1], me), local_sems.at[t]) for t in range(n)]
        for cp in mine:
            cp.start()
        first = []
        for t in range(n):
            first.append(copy(t, 0, me, sibling, True))
            for j, chip in enumerate(chips):
                first.append(copy(t, 1 + j, me, (*chip, c), True))
        for cp in first:
            cp.start()
        passed = []
        for j, chip in enumerate(chips):
            for t in range(n):
                copy(t, 1 + j, num(*chip, c), (x, y, c), False).wait_recv()
                fwd = copy(t, 4 + j, num(*chip, c), sibling, False)
                fwd.start()
                passed.append(fwd)
        for t in range(n):
            copy(t, 0, num(x, y, 1 - c), (x, y, c), False).wait_recv()
            for j, chip in enumerate(chips):
                copy(t, 4 + j, num(*chip, 1 - c), (x, y, c), False).wait_recv()
        for cp in first + passed:
            cp.wait_send()
        for cp in mine:
            cp.wait()

    any_spec = pl.BlockSpec(memory_space=pl.ANY)
    return pl.pallas_call(
        body, name=name, in_specs=[any_spec] * n, out_specs=[any_spec] * n, out_shape=out_shapes,
        scratch_shapes=[pltpu.SemaphoreType.DMA((n, 7)), pltpu.SemaphoreType.DMA((n, 7)), pltpu.SemaphoreType.DMA((n,))],
    )(*[s for s, _ in items])


HBM_SPEC = pl.BlockSpec(memory_space=pltpu.HBM)
SEM_SPEC = pl.BlockSpec(memory_space=pltpu.SEMAPHORE)
EFFECT = pltpu.SideEffectType.DATAFLOW_SIDE_EFFECTING


def _hbm(a):
    return pltpu.with_memory_space_constraint(a, pltpu.HBM)


def _landing(shard, kind, me):
    if kind == "blocked":
        land = lax.empty((shard.shape[0], N_DEV) + shard.shape[1:], shard.dtype)
        return lax.dynamic_update_slice(land, shard[:, None], (0, me) + (0,) * (shard.ndim - 1))
    land = lax.empty((shard.shape[0], N_DEV * shard.shape[1]), shard.dtype)
    return lax.dynamic_update_slice(land, shard, (0, me * shard.shape[1]))


def _peers():
    x, y, c = _place()
    return (x, y, 1 - c), [(1 - x, y), (x, 1 - y), (1 - x, 1 - y)]


def _ag_start(name, groups):
    flat = [it for grp in groups for it in grp]
    n, ng = len(flat), len(groups)

    def body(*refs):
        srcs, lands, sems = refs[:n], refs[n:2 * n], refs[2 * n:2 * n + 2 * ng]
        x, y, c = _place()
        sibling, chips = _peers()
        t = 0
        for gi, grp in enumerate(groups):
            for li, (_, kind, _) in enumerate(grp):
                dst = _window(lands[t], kind, 4 * x + 2 * y + c)
                for k, to in enumerate([sibling] + [(*chip, c) for chip in chips]):
                    pltpu.make_async_remote_copy(
                        src_ref=srcs[t], dst_ref=dst, send_sem=sems[2 * gi].at[4 * li + k], recv_sem=sems[2 * gi + 1].at[4 * li + k],
                        device_id=to, device_id_type=MESH).start()
                t += 1

    sem_shapes = []
    for grp in groups:
        sem_shapes += [pltpu.SemaphoreType.DMA((4 * len(grp),))] * 2
    arrays = [s for s, _, _ in flat] + [ld for _, _, ld in flat]
    outs = pl.pallas_call(
        body, name=name, in_specs=[HBM_SPEC] * (2 * n), out_specs=[SEM_SPEC] * (2 * ng) + [HBM_SPEC] * (2 * n),
        out_shape=sem_shapes + [pltpu.HBM(a.shape, a.dtype) for a in arrays],
        input_output_aliases={i: 2 * ng + i for i in range(2 * n)},
        compiler_params=pltpu.CompilerParams(has_side_effects=EFFECT))(*[_hbm(a) for a in arrays])
    result, t = [], 0
    for gi, grp in enumerate(groups):
        m = len(grp)
        result.append((outs[2 * gi], outs[2 * gi + 1], outs[2 * ng + t:2 * ng + t + m],
                       outs[2 * ng + n + t:2 * ng + n + t + m]))
        t += m
    return result


def _ag_forward(name, lands, kinds):
    n = len(lands)

    def body(*refs):
        land_refs, send, recv = refs[:n], refs[n], refs[n + 1]
        _, _, c = _place()
        sibling, chips = _peers()
        for t in range(n):
            for j, (px, py) in enumerate(chips):
                w = _window(land_refs[t], kinds[t], 4 * px + 2 * py + c)
                pltpu.make_async_remote_copy(src_ref=w, dst_ref=w, send_sem=send.at[3 * t + j], recv_sem=recv.at[3 * t + j],
                                             device_id=sibling, device_id_type=MESH).start()

    outs = pl.pallas_call(
        body, name=name, in_specs=[HBM_SPEC] * n, out_specs=[SEM_SPEC] * 2 + [HBM_SPEC] * n,
        out_shape=[pltpu.SemaphoreType.DMA((3 * n,))] * 2 + [pltpu.HBM(a.shape, a.dtype) for a in lands],
        input_output_aliases={i: 2 + i for i in range(n)},
        compiler_params=pltpu.CompilerParams(has_side_effects=EFFECT))(*lands)
    return outs[0], outs[1], outs[2:]


def _ag_wait(name, phase, send, recv, srcs, lands, kinds, after):
    n, ns = len(lands), len(srcs)

    def body(*refs):
        src_refs, land_refs = refs[:ns], refs[ns:ns + n]
        send_ref, recv_ref = refs[ns + n], refs[ns + n + 1]
        x, y, c = _place()
        sibling, chips = _peers()
        for t in range(n):
            if phase == 1:
                froms = [4 * x + 2 * y + (1 - c)] + [4 * px + 2 * py + c for px, py in chips]
                tos = [sibling] + [(px, py, c) for px, py in chips]
                sent = [src_refs[t]] * 4
            else:
                froms = [4 * px + 2 * py + (1 - c) for px, py in chips]
                tos = [sibling] * 3
                sent = [_window(land_refs[t], kinds[t], 4 * px + 2 * py + c) for px, py in chips]
            for k, (frm, to, s) in enumerate(zip(froms, tos, sent)):
                slot = len(froms) * t + k
                cp = pltpu.make_async_remote_copy(
                    src_ref=s, dst_ref=_window(land_refs[t], kinds[t], frm), send_sem=send_ref.at[slot],
                    recv_sem=recv_ref.at[slot], device_id=to, device_id_type=MESH)
                cp.wait_send()
                cp.wait_recv()

    arrays = list(srcs) + list(lands)
    outs = pl.pallas_call(
        body, name=name, in_specs=[HBM_SPEC] * (ns + n) + [SEM_SPEC] * 2 + [pl.BlockSpec(memory_space=pl.ANY)],
        out_specs=[HBM_SPEC] * (ns + n), out_shape=[pltpu.HBM(a.shape, a.dtype) for a in arrays],
        input_output_aliases={i: i for i in range(ns + n)},
        compiler_params=pltpu.CompilerParams(has_side_effects=EFFECT))(*arrays, send, recv, after)
    return outs[ns:]


def _sibling_exchange(items):
    n = len(items)

    def shard_shape(gr, kind):
        if kind == "blocked":
            return (gr.shape[0],) + gr.shape[2:]
        return (gr.shape[0], gr.shape[1] // N_DEV)

    out_shapes = [jax.ShapeDtypeStruct((N_CHIP,) + shard_shape(gr, kind), gr.dtype) for gr, kind in items]

    def body(*refs):
        srcs, outs = refs[:n], refs[n:2 * n]
        send_sems, recv_sems = refs[2 * n:]
        x, y, c = _place()
        sibling = (x, y, 1 - c)
        copies = []
        for t in range(n):
            for k in range(N_CHIP):
                copies.append(pltpu.make_async_remote_copy(
                    src_ref=_window(srcs[t], items[t][1], 2 * k + (1 - c)), dst_ref=outs[t].at[k],
                    send_sem=send_sems.at[t, k], recv_sem=recv_sems.at[t, k], device_id=sibling, device_id_type=MESH))
        for cp in copies:
            cp.start()
        for cp in copies:
            cp.wait()

    any_spec = pl.BlockSpec(memory_space=pl.ANY)
    return pl.pallas_call(
        body, name="rs_sibling", in_specs=[any_spec] * n, out_specs=[any_spec] * n, out_shape=out_shapes,
        scratch_shapes=[pltpu.SemaphoreType.DMA((n, N_CHIP)), pltpu.SemaphoreType.DMA((n, N_CHIP))],
    )(*[gr for gr, _ in items])


def _chip_sum(t, gr, kind, recv, c):
    if kind == "blocked":
        nl, _, r, w = gr.shape
        rows = nl * r
        tr = _row_tile(r)
        per = r // tr
        g_spec = pl.BlockSpec((None, None, tr, w), lambda k, i, cref: (i // per, 2 * k + cref[0], i % per, 0))
    else:
        rows, w = gr.shape[0], gr.shape[1] // N_DEV
        tr = _row_tile(rows)
        g_spec = pl.BlockSpec((tr, w), lambda k, i, cref: (i, 2 * k + cref[0]))
    recv = recv.reshape(N_CHIP, rows, w)

    def body(c_ref, g_ref, r_ref, o_ref):
        del c_ref
        o_ref[...] = (g_ref[...].astype(F32) + r_ref[...].astype(F32)).astype(BF16)

    blk = pl.BlockSpec((None, tr, w), lambda k, i, cref: (k, i, 0))
    return pl.pallas_call(
        body, name=f"rs_sum_{t}",
        grid_spec=pltpu.PrefetchScalarGridSpec(num_scalar_prefetch=1, grid=(N_CHIP, rows // tr),
                                               in_specs=[g_spec, blk], out_specs=blk),
        out_shape=jax.ShapeDtypeStruct((N_CHIP, rows, w), BF16),
        compiler_params=_params(("parallel", "parallel")))(c, gr, recv)


def _row_tile(rows):
    for tr in (512, 384, 352, 256, 128, 64, 32, 16):
        if rows % tr == 0:
            return tr
    raise ValueError(rows)


def _chip_exchange(sums):
    n = len(sums)

    def body(*refs):
        srcs, outs = refs[:n], refs[n:2 * n]
        send_sems, recv_sems = refs[2 * n:]
        x, y, c = _place()
        my_chip = 2 * x + y
        others = [(1 - x, y), (x, 1 - y), (1 - x, 1 - y)]
        sends, recvs = [], []
        for t in range(n):
            for j, (px, py) in enumerate(others):
                sends.append(pltpu.make_async_remote_copy(
                    src_ref=srcs[t].at[2 * px + py], dst_ref=outs[t].at[my_chip],
                    send_sem=send_sems.at[t, j], recv_sem=recv_sems.at[t, j], device_id=(px, py, c), device_id_type=MESH))
                recvs.append(pltpu.make_async_remote_copy(
                    src_ref=srcs[t].at[my_chip], dst_ref=outs[t].at[2 * px + py],
                    send_sem=send_sems.at[t, j], recv_sem=recv_sems.at[t, j], device_id=(px, py, c), device_id_type=MESH))
        for cp in sends:
            cp.start()
        for cp in recvs:
            cp.wait_recv()
        for cp in sends:
            cp.wait_send()

    any_spec = pl.BlockSpec(memory_space=pl.ANY)
    return pl.pallas_call(
        body, name="rs_chips", in_specs=[any_spec] * n, out_specs=[any_spec] * n,
        out_shape=[jax.ShapeDtypeStruct(s.shape, s.dtype) for s in sums],
        scratch_shapes=[pltpu.SemaphoreType.DMA((n, 3)), pltpu.SemaphoreType.DMA((n, 3))],
    )(*sums)


def _adamw_math(g, wv, mv, vv):
    m = ADAM_B1 * mv + (1.0 - ADAM_B1) * g
    v = ADAM_B2 * vv + (1.0 - ADAM_B2) * (g * g)
    m_hat = m / (1.0 - ADAM_B1 ** ADAM_STEP)
    v_hat = v / (1.0 - ADAM_B2 ** ADAM_STEP)
    delta = -ADAM_LR * (m_hat / (jnp.sqrt(v_hat) + ADAM_EPS) + ADAM_WD * wv)
    return delta, m, v


def _adamw_sharded(t, own, recv, chip_ids, wv, mv, vv):
    shape = wv.shape
    rows, w = own.shape[1], own.shape[2]
    tr = _row_tile(rows)
    w2, m2, v2 = (a.reshape(rows, w) for a in (wv, mv, vv))

    def body(ids_ref, own_ref, r1_ref, r2_ref, r3_ref, w_ref, m_ref, v_ref, g_ref, d_ref, nm_ref, nv_ref):
        del ids_ref
        g = ((own_ref[...].astype(F32) + r1_ref[...].astype(F32)) + r2_ref[...].astype(F32)) + r3_ref[...].astype(F32)
        g_ref[...] = g
        d_ref[...], nm_ref[...], nv_ref[...] = _adamw_math(g, w_ref[...], m_ref[...], v_ref[...])

    def pick(slot):
        return pl.BlockSpec((None, tr, w), lambda i, ids: (ids[slot], i, 0))

    flat = pl.BlockSpec((tr, w), lambda i, ids: (i, 0))
    outs = pl.pallas_call(
        body, name=f"adamw_{t}",
        grid_spec=pltpu.PrefetchScalarGridSpec(
            num_scalar_prefetch=1, grid=(rows // tr,),
            in_specs=[pick(0), pick(1), pick(2), pick(3), flat, flat, flat], out_specs=[flat] * 4),
        out_shape=[jax.ShapeDtypeStruct((rows, w), F32)] * 4,
        compiler_params=_params(("parallel",)))(chip_ids, own, recv, recv, recv, w2, m2, v2)
    return [o.reshape(shape) for o in outs]


def _adamw_small(parts, wv, mv, vv):
    r = wv.shape[0]

    def body(p_ref, w_ref, m_ref, v_ref, g_ref, d_ref, nm_ref, nv_ref):
        g = p_ref[0]
        for k in range(1, N_DEV):
            g = g + p_ref[k]
        g_ref[...] = g
        d_ref[...], nm_ref[...], nv_ref[...] = _adamw_math(g, w_ref[...], m_ref[...], v_ref[...])

    return pl.pallas_call(
        body, name="adamw_small",
        out_shape=[jax.ShapeDtypeStruct((r, 128), F32)] * 4,
        compiler_params=pltpu.CompilerParams(vmem_limit_bytes=VMEM_LIMIT))(parts, wv, mv, vv)


def _pack(arrays, rows):
    flat = jnp.concatenate([a.reshape(-1).astype(F32) for a in arrays])
    return jnp.pad(flat, (0, rows * 128 - flat.shape[0])).reshape(rows, 128)


def _unpack(packed, shapes):
    flat = packed.reshape(-1)
    out, off = [], 0
    for shape in shapes:
        size = 1
        for s in shape:
            size *= s
        out.append(flat[off:off + size].reshape(shape))
        off += size
    return out


REPLICATED = ("attn_norm", "ffn_norm", "final_norm", "kv_in_norm", "kv_latent_norm", "q_latent_norm", "ffn_conv_b")
WEIGHTS = ("attn_norm", "ffn_norm", "final_norm", "sc_w_in", "sc_conv_w", "sc_w_out", "kv_in_norm", "w_dkv",
           "kv_latent_norm", "w_kr", "w_uk", "w_uv", "w_dq", "q_latent_norm", "w_uq", "w_o", "ffn_w_up", "ffn_conv_w",
           "ffn_conv_b", "ffn_w_down")
BIG = {"sc_w_in": "cols", "sc_w_out": "blocked", "w_dkv": "blocked", "w_kr": "blocked", "w_uk": "cols", "w_uv": "cols",
       "w_dq": "blocked", "w_uq": "blocked", "w_o": "blocked", "ffn_w_up": "blocked", "ffn_w_down": "blocked"}
SMALL_W_ROWS = 24
SMALL_G_ROWS = 256


def kernel(x, positions, attn_norm, ffn_norm, final_norm, sc_w_in, sc_conv_w, sc_w_out, kv_in_norm, w_dkv, kv_latent_norm, w_kr, w_uk, w_uv, w_dq, q_latent_norm, w_uq, w_o, ffn_w_up, ffn_conv_w, ffn_conv_b, ffn_w_down, loss_target, m_attn_norm, m_ffn_norm, m_final_norm, m_sc_w_in, m_sc_conv_w, m_sc_w_out, m_kv_in_norm, m_w_dkv, m_kv_latent_norm, m_w_kr, m_w_uk, m_w_uv, m_w_dq, m_q_latent_norm, m_w_uq, m_w_o, m_ffn_w_up, m_ffn_conv_w, m_ffn_conv_b, m_ffn_w_down, v_attn_norm, v_ffn_norm, v_final_norm, v_sc_w_in, v_sc_conv_w, v_sc_w_out, v_kv_in_norm, v_w_dkv, v_kv_latent_norm, v_w_kr, v_w_uk, v_w_uv, v_w_dq, v_q_latent_norm, v_w_uq, v_w_o, v_ffn_w_up, v_ffn_conv_w, v_ffn_conv_b, v_ffn_w_down):
    wts = dict(attn_norm=attn_norm, ffn_norm=ffn_norm, final_norm=final_norm, sc_w_in=sc_w_in, sc_conv_w=sc_conv_w,
               sc_w_out=sc_w_out, kv_in_norm=kv_in_norm, w_dkv=w_dkv, kv_latent_norm=kv_latent_norm, w_kr=w_kr,
               w_uk=w_uk, w_uv=w_uv, w_dq=w_dq, q_latent_norm=q_latent_norm, w_uq=w_uq, w_o=w_o, ffn_w_up=ffn_w_up,
               ffn_conv_w=ffn_conv_w, ffn_conv_b=ffn_conv_b, ffn_w_down=ffn_w_down)
    mom = dict(attn_norm=m_attn_norm, ffn_norm=m_ffn_norm, final_norm=m_final_norm, sc_w_in=m_sc_w_in,
               sc_conv_w=m_sc_conv_w, sc_w_out=m_sc_w_out, kv_in_norm=m_kv_in_norm, w_dkv=m_w_dkv,
               kv_latent_norm=m_kv_latent_norm, w_kr=m_w_kr, w_uk=m_w_uk, w_uv=m_w_uv, w_dq=m_w_dq,
               q_latent_norm=m_q_latent_norm, w_uq=m_w_uq, w_o=m_w_o, ffn_w_up=m_ffn_w_up, ffn_conv_w=m_ffn_conv_w,
               ffn_conv_b=m_ffn_conv_b, ffn_w_down=m_ffn_w_down)
    var = dict(attn_norm=v_attn_norm, ffn_norm=v_ffn_norm, final_norm=v_final_norm, sc_w_in=v_sc_w_in,
               sc_conv_w=v_sc_conv_w, sc_w_out=v_sc_w_out, kv_in_norm=v_kv_in_norm, w_dkv=v_w_dkv,
               kv_latent_norm=v_kv_latent_norm, w_kr=v_w_kr, w_uk=v_w_uk, w_uv=v_w_uv, w_dq=v_w_dq,
               q_latent_norm=v_q_latent_norm, w_uq=v_w_uq, w_o=v_w_o, ffn_w_up=v_ffn_w_up, ffn_conv_w=v_ffn_conv_w,
               ffn_conv_b=v_ffn_conv_b, ffn_w_down=v_ffn_w_down)
    xi, yi, ci = _place()
    me = 4 * xi + 2 * yi + ci

    def shard_view(name, a):
        if BIG[name] == "cols":
            return a.reshape(a.shape[-2], a.shape[-1])
        return a.reshape((-1,) + a.shape[-2:])

    names = list(BIG)
    up, down = ffn_w_up.astype(BF16), ffn_w_down.astype(BF16)

    def big(nm):
        return shard_view(nm, wts[nm]).astype(BF16), BIG[nm]

    early = {"sc_w_in": big("sc_w_in"), "sc_w_out": big("sc_w_out"),
             "conv": (_pack([sc_conv_w, ffn_conv_w], SMALL_W_ROWS).reshape(1, SMALL_W_ROWS, 128), "blocked"),
             "ffn_w_up0": (up[0:1], "blocked"), "ffn_w_down0": (down[0:1], "blocked")}
    later = {nm: big(nm) for nm in ("w_dkv", "w_kr", "w_uk", "w_uv", "w_dq", "w_uq", "w_o")}
    later["ffn_w_up1"] = (up[1:2], "blocked")
    later["ffn_w_down1"] = (down[1:2], "blocked")
    kinds0 = [kind for _, kind in early.values()]
    kinds1 = [kind for _, kind in later.values()]
    started = _ag_start("ag_start", [[(s, kind, _landing(s, kind, me)) for s, kind in grp.values()]
                                     for grp in (early, later)])
    send0, recv0, src0, land0 = started[0]
    land0 = _ag_wait("ag_wait_chips_0", 1, send0, recv0, src0, land0, kinds0, x)
    send0, recv0, land0 = _ag_forward("ag_forward_0", land0, kinds0)
    land0 = _ag_wait("ag_wait_sibling_0", 2, send0, recv0, [], land0, kinds0, x)
    full = dict(zip(early, land0))
    conv_all = full["conv"].reshape(N_DEV, SMALL_W_ROWS * 128)
    scw = conv_all[:, :3 * 128].reshape(N_DEV, 3, 128).transpose(1, 0, 2).reshape(3, D)
    fcw = conv_all[:, 3 * 128:3 * 128 + 6 * 352].reshape(N_DEV, 2, 3, 352).transpose(1, 2, 0, 3).reshape(2, 3, D_FF)
    w = {
        "sc_w_in": full["sc_w_in"],
        "sc_conv_w": scw,
        "sc_w_out": full["sc_w_out"].reshape(D, D),
        "ffn_w_up0": full["ffn_w_up0"],
        "ffn_w_down0": full["ffn_w_down0"].reshape(1, N_FF_BLK, FF_BLK, D),
        "ffn_cw": fcw.reshape(2, 3, N_FF_BLK, FF_BLK).transpose(0, 2, 1, 3),
        "ffn_cb": ffn_conv_b.reshape(2, N_FF_BLK, 1, FF_BLK),
    }
    rep = {
        "attn_norm": attn_norm, "ffn_norm": ffn_norm, "final_norm": final_norm,
        "kv_in_norm": kv_in_norm.reshape(1, D), "kv_latent_norm": kv_latent_norm.reshape(1, KV_LORA),
        "q_latent_norm": q_latent_norm.reshape(1, Q_LORA),
    }
    in_flight = {}

    def mid(act):
        send1, recv1, src1, land1 = started[1]
        land1 = _ag_wait("ag_wait_chips_1", 1, send1, recv1, src1, land1, kinds1, act)
        in_flight["forward"] = _ag_forward("ag_forward_1", land1, kinds1)

    def finish(h2):
        send1, recv1, land1 = in_flight["forward"]
        land1 = _ag_wait("ag_wait_sibling_1", 2, send1, recv1, [], land1, kinds1, h2)
        got = dict(zip(later, land1))
        return {
            "w_dkv": got["w_dkv"].reshape(D, KV_LORA),
            "w_kr": jnp.pad(got["w_kr"].reshape(D, QK_ROPE), ((0, 0), (0, 128 - QK_ROPE))),
            "w_uk": got["w_uk"],
            "w_uv": got["w_uv"],
            "w_dq": got["w_dq"].reshape(D, Q_LORA),
            "w_uq": jnp.pad(got["w_uq"].reshape(N_HEADS, Q_LORA, QK_NOPE + QK_ROPE),
                            ((0, 0), (0, 0), (0, QK_PAD - QK_NOPE - QK_ROPE))),
            "w_o": got["w_o"].reshape(D, D),
            "ffn_w_up1": got["ffn_w_up1"],
            "ffn_w_down1": got["ffn_w_down1"].reshape(1, N_FF_BLK, FF_BLK, D),
        }

    loss, grad_x, g, small = _local_step(x.reshape(T, D), positions.reshape(T, 1), loss_target.reshape(T, D), rep, w,
                                         late=(mid, finish))

    g["w_kr"] = g["w_kr"][:, :QK_ROPE]
    g["w_uq"] = g["w_uq"][:, :, :QK_NOPE + QK_ROPE]

    def grad_view(name, a):
        if BIG[name] == "cols":
            return a
        shard = shard_view(name, wts[name])
        return a.reshape((shard.shape[0], N_DEV) + shard.shape[1:])

    g_items = [(grad_view(nm, g[nm]), BIG[nm]) for nm in names]
    from_sibling = _sibling_exchange(g_items)
    c_arr = jnp.reshape(ci, (1,)).astype(jnp.int32)
    sums = [_chip_sum(t, gr, kind, rv, c_arr) for t, ((gr, kind), rv) in enumerate(zip(g_items, from_sibling))]
    from_chips = _chip_exchange(sums)
    my_chip = 2 * xi + yi
    chip_ids = jnp.stack([my_chip, my_chip ^ 1, my_chip ^ 2, my_chip ^ 3]).astype(jnp.int32)
    results = {}
    for t, nm in enumerate(names):
        results[nm] = _adamw_sharded(t, sums[t], from_chips[t], chip_ids, wts[nm], mom[nm], var[nm])

    small_order = list(REPLICATED) + ["sc_conv_w", "ffn_conv_w"]
    packed_g = _pack([loss[0, 0:1]] + [small[nm] for nm in small_order], SMALL_G_ROWS)
    parts = _all_gather("ag_small_grads", [(packed_g.reshape(1, SMALL_G_ROWS, 128), "blocked")])[0]
    parts = parts.reshape(N_DEV, SMALL_G_ROWS, 128)

    def full_params(src):
        scw_full = jnp.zeros((3, D), F32)
        scw_full = lax.dynamic_update_slice(scw_full, src["sc_conv_w"].reshape(3, 128), (0, me * 128))
        fcw_full = jnp.zeros((2, 3, D_FF), F32)
        fcw_full = lax.dynamic_update_slice(fcw_full, src["ffn_conv_w"], (0, 0, me * 352))
        return _pack([jnp.zeros((1,), F32)] + [src[nm] for nm in REPLICATED] + [scw_full, fcw_full], SMALL_G_ROWS)

    small_out = _adamw_small(parts, full_params(wts), full_params(mom), full_params(var))
    shapes = [(1,)] + [wts[nm].shape for nm in REPLICATED] + [(3, D), (2, 3, D_FF)]
    unpacked = [_unpack(o, shapes) for o in small_out]
    loss_total = unpacked[0][0].reshape(())
    for slot, nm in enumerate(small_order):
        vals = [u[slot + 1] for u in unpacked]
        if nm == "sc_conv_w":
            vals = [lax.dynamic_slice(a, (0, me * 128), (3, 128)).reshape(1, 3, 128) for a in vals]
        elif nm == "ffn_conv_w":
            vals = [lax.dynamic_slice(a, (0, 0, me * 352), (2, 3, 352)) for a in vals]
        results[nm] = vals

    outs = [loss_total, grad_x.reshape(1, T, D)]
    for slot in range(4):
        outs.extend(results[nm][slot] for nm in WEIGHTS)
    return tuple(outs)
```

```python
import jax
import jax.numpy as jnp
from jax import lax
from jax.experimental import pallas as pl
from jax.experimental.pallas import tpu as pltpu

F32 = jnp.float32
BF16 = jnp.bfloat16

T = 2048
D = 1024
N_HEADS = 8
QK_NOPE = 128
QK_ROPE = 64
V_HEAD = 128
Q_LORA = 384
KV_LORA = 256
D_FF = 2816
CHUNK = 64
ROPE_THETA = 10000.0
EPS = 1e-6
NEG_INF = -1e30
ADAM_LR = 0.001
ADAM_B1 = 0.9
ADAM_B2 = 0.999
ADAM_EPS = 1e-08
ADAM_WD = 0.01
ADAM_STEP = 10

N_DEV = 8
N_CHIP = 4
FF_BLK = D_FF * 2 // N_DEV
N_FF_BLK = D_FF // FF_BLK
QK_PAD = 256
HALO = 16

TM = 512
TR = 256
TQ = 256
VMEM_LIMIT = 56 * 1024 * 1024

NN = (((1,), (0,)), ((), ()))
NT = (((1,), (1,)), ((), ()))
TN = (((0,), (0,)), ((), ()))
MESH = pl.DeviceIdType.MESH


def _params(sem):
    return pltpu.CompilerParams(dimension_semantics=sem, vmem_limit_bytes=VMEM_LIMIT)


def _mm(name, a, b, *, grid, a_spec, b_spec, o_spec, o_shape, o_dtype, dims, k_axis=None, acc_shape=None,
        add=None, add_spec=None):
    nk = grid[k_axis] if k_axis is not None else 1
    has_add = add is not None

    def body(*refs):
        a_ref, b_ref = refs[0], refs[1]
        p = 2
        add_ref = None
        if has_add:
            add_ref = refs[p]
            p += 1
        o_ref = refs[p]
        p += 1
        r = lax.dot_general(a_ref[...].astype(BF16), b_ref[...].astype(BF16), dims, preferred_element_type=F32)
        if k_axis is None:
            if has_add:
                r = r + add_ref[...].astype(F32)
            o_ref[...] = r.astype(o_dtype)
        else:
            acc = refs[p]
            k = pl.program_id(k_axis)

            @pl.when(k == 0)
            def _():
                acc[...] = r

            @pl.when(k > 0)
            def _():
                acc[...] += r

            @pl.when(k == nk - 1)
            def _():
                t = acc[...]
                if has_add:
                    t = t + add_ref[...].astype(F32)
                o_ref[...] = t.astype(o_dtype)

    in_specs = [a_spec, b_spec]
    args = [a, b]
    if has_add:
        in_specs.append(add_spec if add_spec is not None else o_spec)
        args.append(add)
    sem = tuple("arbitrary" if ax == k_axis else "parallel" for ax in range(len(grid)))
    scratch = [pltpu.VMEM(acc_shape, F32)] if k_axis is not None else []
    return pl.pallas_call(
        body, name=name, grid=grid, in_specs=in_specs, out_specs=o_spec,
        out_shape=jax.ShapeDtypeStruct(o_shape, o_dtype), scratch_shapes=scratch,
        compiler_params=_params(sem))(*args)


def _mm_rows(name, a, b, dims, o_dtype, n_out, *, tn=None, add=None):
    k = a.shape[1]
    tn = n_out if tn is None else tn
    if dims == NN:
        b_spec = pl.BlockSpec((k, tn), lambda n, i: (0, n))
    else:
        b_spec = pl.BlockSpec((tn, k), lambda n, i: (n, 0))
    return _mm(name, a, b, grid=(n_out // tn, T // TM),
               a_spec=pl.BlockSpec((TM, k), lambda n, i: (i, 0)), b_spec=b_spec,
               o_spec=pl.BlockSpec((TM, tn), lambda n, i: (i, n)), o_shape=(T, n_out), o_dtype=o_dtype,
               dims=dims, add=add)


def _mm_wgrad(name, a, b, *, tn=512):
    k, n = a.shape[1], b.shape[1]
    tn = min(tn, n)
    return _mm(name, a, b, grid=(n // tn,),
               a_spec=pl.BlockSpec((T, k), lambda j: (0, 0)), b_spec=pl.BlockSpec((T, tn), lambda j: (0, j)),
               o_spec=pl.BlockSpec((k, tn), lambda j: (0, j)), o_shape=(k, n), o_dtype=BF16, dims=TN)


def _rms_fwd(name, x, g):
    d = x.shape[1]

    def body(x_ref, g_ref, o_ref):
        xv = x_ref[...]
        r = lax.rsqrt(jnp.mean(xv * xv, axis=-1, keepdims=True) + EPS)
        o_ref[...] = ((xv * r) * g_ref[...]).astype(BF16)

    return pl.pallas_call(
        body, name=name, grid=(T // TM,),
        in_specs=[pl.BlockSpec((TM, d), lambda i: (i, 0)), pl.BlockSpec((1, d), lambda i: (0, 0))],
        out_specs=pl.BlockSpec((TM, d), lambda i: (i, 0)),
        out_shape=jax.ShapeDtypeStruct((T, d), BF16), compiler_params=_params(("parallel",)))(x, g)


def _rms_bwd(name, x, g, dy, dres=None):
    d = x.shape[1]
    has_res = dres is not None

    def body(*refs):
        if has_res:
            x_ref, g_ref, dy_ref, res_ref, dx_ref, dxb_ref, dg_ref = refs
        else:
            x_ref, g_ref, dy_ref, dx_ref, dxb_ref, dg_ref = refs
        xv = x_ref[...]
        r = lax.rsqrt(jnp.mean(xv * xv, axis=-1, keepdims=True) + EPS)
        xn = xv * r
        dyv = dy_ref[...].astype(F32)
        gdy = dyv * g_ref[...]
        dx = r * (gdy - xn * jnp.mean(gdy * xn, axis=-1, keepdims=True))
        if has_res:
            dx = dx + res_ref[...]
        dx_ref[...] = dx
        dxb_ref[...] = dx.astype(BF16)
        part = jnp.sum(dyv * xn, axis=0, keepdims=True)

        @pl.when(pl.program_id(0) == 0)
        def _():
            dg_ref[...] = part

        @pl.when(pl.program_id(0) > 0)
        def _():
            dg_ref[...] += part

    row = pl.BlockSpec((TR, d), lambda i: (i, 0))
    vec = pl.BlockSpec((1, d), lambda i: (0, 0))
    args = [x, g, dy] + ([dres] if has_res else [])
    in_specs = [row, vec, row] + ([row] if has_res else [])
    return pl.pallas_call(
        body, name=name, grid=(T // TR,), in_specs=in_specs, out_specs=[row, row, vec],
        out_shape=[jax.ShapeDtypeStruct((T, d), F32), jax.ShapeDtypeStruct((T, d), BF16),
                   jax.ShapeDtypeStruct((1, d), F32)],
        compiler_params=_params(("arbitrary",)))(*args)


def _final(h, g, tgt):
    def body(h_ref, g_ref, t_ref, loss_ref, dh_ref, dhb_ref, dg_ref):
        hv = h_ref[...]
        r = lax.rsqrt(jnp.mean(hv * hv, axis=-1, keepdims=True) + EPS)
        xn = hv * r
        gv = g_ref[...]
        err = xn * gv - t_ref[...]
        part_loss = 0.5 * jnp.sum(jnp.mean(err * err, axis=-1, keepdims=True), axis=0, keepdims=True)
        dy = err * (1.0 / D)
        gdy = dy * gv
        dh = r * (gdy - xn * jnp.mean(gdy * xn, axis=-1, keepdims=True))
        dh_ref[...] = dh
        dhb_ref[...] = dh.astype(BF16)
        part = jnp.sum(dy * xn, axis=0, keepdims=True)
        first = pl.program_id(0) == 0

        @pl.when(first)
        def _():
            dg_ref[...] = part
            loss_ref[...] = jnp.broadcast_to(part_loss, (1, 128))

        @pl.when(jnp.logical_not(first))
        def _():
            dg_ref[...] += part
            loss_ref[...] += jnp.broadcast_to(part_loss, (1, 128))

    row = pl.BlockSpec((TR, D), lambda i: (i, 0))
    vec = pl.BlockSpec((1, D), lambda i: (0, 0))
    return pl.pallas_call(
        body, name="final_loss", grid=(T // TR,), in_specs=[row, vec, row],
        out_specs=[pl.BlockSpec((1, 128), lambda i: (0, 0)), row, row, vec],
        out_shape=[jax.ShapeDtypeStruct((1, 128), F32), jax.ShapeDtypeStruct((T, D), F32),
                   jax.ShapeDtypeStruct((T, D), BF16), jax.ShapeDtypeStruct((1, D), F32)],
        compiler_params=_params(("arbitrary",)))(h, g, tgt)


def _prev_idx(i):
    return jnp.maximum(i * (TR // HALO) - 1, 0)


def _next_idx(i):
    return jnp.minimum((i + 1) * (TR // HALO), T // HALO - 1)


def _causal_taps(ext):
    return pltpu.roll(ext, 2, 0)[HALO:], pltpu.roll(ext, 1, 0)[HALO:], ext[HALO:]


def _anticausal_taps(ext, n):
    rows = ext.shape[0]
    return pltpu.roll(ext, rows - 1, 0)[:n], pltpu.roll(ext, rows - 2, 0)[:n]


def _sc_fwd(z, w):
    def body(b_ref, c_ref, ch_ref, u_ref, uh_ref, w_ref, y_ref):
        i = pl.program_id(0)
        cu = c_ref[...].astype(F32) * u_ref[...].astype(F32)
        cuh = ch_ref[...].astype(F32) * uh_ref[...].astype(F32)
        cuh = jnp.where(i > 0, cuh, 0.0)
        x2, x1, x0 = _causal_taps(jnp.concatenate([cuh, cu], axis=0))
        wv = w_ref[...]
        cv = (x2 * wv[0:1] + x1 * wv[1:2]) + x0 * wv[2:3]
        y_ref[...] = (b_ref[...].astype(F32) * cv).astype(BF16)

    def main(part):
        return pl.BlockSpec((TR, D), lambda i: (i, part))

    def halo(part):
        return pl.BlockSpec((HALO, D), lambda i: (_prev_idx(i), part))

    return pl.pallas_call(
        body, name="sc_fwd", grid=(T // TR,),
        in_specs=[main(0), main(1), halo(1), main(2), halo(2), pl.BlockSpec((3, D), lambda i: (0, 0))],
        out_specs=pl.BlockSpec((TR, D), lambda i: (i, 0)),
        out_shape=jax.ShapeDtypeStruct((T, D), BF16), compiler_params=_params(("parallel",)))(z, z, z, z, z, w)


def _sc_bwd(z, dy, w):
    last = T // TR - 1

    def body(b_ref, bn_ref, c_ref, ch_ref, u_ref, uh_ref, dy_ref, dyn_ref, w_ref, dz_ref, dw_ref):
        i = pl.program_id(0)
        cv_ = c_ref[...].astype(F32)
        uv = u_ref[...].astype(F32)
        cu = cv_ * uv
        cuh = jnp.where(i > 0, ch_ref[...].astype(F32) * uh_ref[...].astype(F32), 0.0)
        x2, x1, x0 = _causal_taps(jnp.concatenate([cuh, cu], axis=0))
        wv = w_ref[...]
        conv = (x2 * wv[0:1] + x1 * wv[1:2]) + x0 * wv[2:3]
        dyv = dy_ref[...]
        dz_ref[:, 0:D] = (dyv * conv).astype(BF16)
        dconv = dyv * b_ref[...].astype(F32)
        dconv_n = jnp.where(i < last, dyn_ref[...] * bn_ref[...].astype(F32), 0.0)
        n1, n2 = _anticausal_taps(jnp.concatenate([dconv, dconv_n], axis=0), TR)
        dcu = (dconv * wv[2:3] + n1 * wv[1:2]) + n2 * wv[0:1]
        dz_ref[:, D:2 * D] = (dcu * uv).astype(BF16)
        dz_ref[:, 2 * D:3 * D] = (dcu * cv_).astype(BF16)
        part = jnp.concatenate([jnp.sum(dconv * x2, axis=0, keepdims=True),
                                jnp.sum(dconv * x1, axis=0, keepdims=True),
                                jnp.sum(dconv * x0, axis=0, keepdims=True)], axis=0)

        @pl.when(i == 0)
        def _():
            dw_ref[...] = part

        @pl.when(i > 0)
        def _():
            dw_ref[...] += part

    def main(part):
        return pl.BlockSpec((TR, D), lambda i: (i, part))

    def prev(part):
        return pl.BlockSpec((HALO, D), lambda i: (_prev_idx(i), part))

    def nxt(part):
        return pl.BlockSpec((HALO, D), lambda i: (_next_idx(i), part))

    wspec = pl.BlockSpec((3, D), lambda i: (0, 0))
    return pl.pallas_call(
        body, name="sc_bwd", grid=(T // TR,),
        in_specs=[main(0), nxt(0), main(1), prev(1), main(2), prev(2), main(0), nxt(0), wspec],
        out_specs=[pl.BlockSpec((TR, 3 * D), lambda i: (i, 0)), wspec],
        out_shape=[jax.ShapeDtypeStruct((T, 3 * D), BF16), jax.ShapeDtypeStruct((3, D), F32)],
        compiler_params=_params(("arbitrary",)))(z, z, z, z, z, z, dy, dy, w)


def _sigmoid(x):
    return 1.0 / (1.0 + jnp.exp(-x))


def _ffn_fwd(name, gv, w, b):
    def body(g_ref, gh_ref, v_ref, w_ref, b_ref, a_ref):
        i = pl.program_id(1)
        g = g_ref[...].astype(F32)
        gh = jnp.where(i > 0, gh_ref[...].astype(F32), 0.0)
        x2, x1, x0 = _causal_taps(jnp.concatenate([gh, g], axis=0))
        wv = w_ref[...]
        gc = ((x2 * wv[0:1] + x1 * wv[1:2]) + x0 * wv[2:3]) + b_ref[...]
        a_ref[...] = ((gc * _sigmoid(gc)) * v_ref[...].astype(F32)).astype(BF16)

    blk = (None, TR, FF_BLK)
    return pl.pallas_call(
        body, name=name, grid=(N_FF_BLK, T // TR),
        in_specs=[pl.BlockSpec(blk, lambda j, i: (j, i, 0)),
                  pl.BlockSpec((None, HALO, FF_BLK), lambda j, i: (j, _prev_idx(i), 0)),
                  pl.BlockSpec(blk, lambda j, i: (j + N_FF_BLK, i, 0)),
                  pl.BlockSpec((None, 3, FF_BLK), lambda j, i: (j, 0, 0)),
                  pl.BlockSpec((None, 1, FF_BLK), lambda j, i: (j, 0, 0))],
        out_specs=pl.BlockSpec(blk, lambda j, i: (j, i, 0)),
        out_shape=jax.ShapeDtypeStruct((N_FF_BLK, T, FF_BLK), BF16),
        compiler_params=_params(("parallel", "parallel")))(gv, gv, gv, w, b)


def _ffn_bwd(name, gv, dact, w, b):
    last = T // TR - 1

    def body(g_ref, gp_ref, gn_ref, v_ref, vn_ref, da_ref, dan_ref, w_ref, b_ref, dg_ref, dv_ref, dw_ref, db_ref):
        i = pl.program_id(1)
        gp = jnp.where(i > 0, gp_ref[...].astype(F32), 0.0)
        ext = jnp.concatenate([gp, g_ref[...].astype(F32), gn_ref[...].astype(F32)], axis=0)
        x2, x1, x0 = _causal_taps(ext)
        wv = w_ref[...]
        gc = ((x2 * wv[0:1] + x1 * wv[1:2]) + x0 * wv[2:3]) + b_ref[...]
        sg = _sigmoid(gc)
        da = jnp.concatenate([da_ref[...].astype(F32), jnp.where(i < last, dan_ref[...].astype(F32), 0.0)], axis=0)
        vv = jnp.concatenate([v_ref[...].astype(F32), vn_ref[...].astype(F32)], axis=0)
        dv_ref[...] = (da[:TR] * (gc[:TR] * sg[:TR])).astype(BF16)
        dgc = (da * vv) * (sg * (1.0 + gc * (1.0 - sg)))
        n1, n2 = _anticausal_taps(dgc, TR)
        d0 = dgc[:TR]
        dg_ref[...] = ((d0 * wv[2:3] + n1 * wv[1:2]) + n2 * wv[0:1]).astype(BF16)
        part_w = jnp.concatenate([jnp.sum(d0 * x2[:TR], axis=0, keepdims=True),
                                  jnp.sum(d0 * x1[:TR], axis=0, keepdims=True),
                                  jnp.sum(d0 * x0[:TR], axis=0, keepdims=True)], axis=0)
        part_b = jnp.sum(d0, axis=0, keepdims=True)

        @pl.when(i == 0)
        def _():
            dw_ref[...] = part_w
            db_ref[...] = part_b

        @pl.when(i > 0)
        def _():
            dw_ref[...] += part_w
            db_ref[...] += part_b

    blk = (None, TR, FF_BLK)
    hblk = (None, HALO, FF_BLK)
    wspec = pl.BlockSpec((None, 3, FF_BLK), lambda j, i: (j, 0, 0))
    bspec = pl.BlockSpec((None, 1, FF_BLK), lambda j, i: (j, 0, 0))
    return pl.pallas_call(
        body, name=name, grid=(N_FF_BLK, T // TR),
        in_specs=[pl.BlockSpec(blk, lambda j, i: (j, i, 0)),
                  pl.BlockSpec(hblk, lambda j, i: (j, _prev_idx(i), 0)),
                  pl.BlockSpec(hblk, lambda j, i: (j, _next_idx(i), 0)),
                  pl.BlockSpec(blk, lambda j, i: (j + N_FF_BLK, i, 0)),
                  pl.BlockSpec(hblk, lambda j, i: (j + N_FF_BLK, _next_idx(i), 0)),
                  pl.BlockSpec(blk, lambda j, i: (j, i, 0)),
                  pl.BlockSpec(hblk, lambda j, i: (j, _next_idx(i), 0)),
                  wspec, bspec],
        out_specs=[pl.BlockSpec(blk, lambda j, i: (j, i, 0)), pl.BlockSpec(blk, lambda j, i: (j, i, 0)), wspec, bspec],
        out_shape=[jax.ShapeDtypeStruct((N_FF_BLK, T, FF_BLK), BF16), jax.ShapeDtypeStruct((N_FF_BLK, T, FF_BLK), BF16),
                   jax.ShapeDtypeStruct((N_FF_BLK, 3, FF_BLK), F32), jax.ShapeDtypeStruct((N_FF_BLK, 1, FF_BLK), F32)],
        compiler_params=_params(("parallel", "arbitrary")))(gv, gv, gv, gv, gv, dact, dact, w, b)


def _rope_tables(pos, inv_freq):
    half = QK_ROPE // 2

    def body(p_ref, f_ref, c_ref, sa_ref, sb_ref):
        ang = p_ref[...].astype(F32) * f_ref[...]
        lane = lax.broadcasted_iota(jnp.int32, (T, 128), 1)
        c = jnp.cos(ang)
        s = jnp.sin(ang)
        c_ref[...] = jnp.where(lane < 2 * half, c, 0.0)
        sa_ref[...] = jnp.where(lane < half, -s, 0.0)
        sb_ref[...] = jnp.where(jnp.logical_and(lane >= half, lane < 2 * half), s, 0.0)

    return pl.pallas_call(
        body, name="rope_tables", out_shape=[jax.ShapeDtypeStruct((T, 128), F32)] * 3,
        compiler_params=pltpu.CompilerParams(vmem_limit_bytes=VMEM_LIMIT))(pos, inv_freq)


def _rope(name, x, tables, sign, out_dtype, reduce_groups=False):
    g, _, w = x.shape
    cos, sa, sb = tables

    def body(x_ref, c_ref, sa_ref, sb_ref, o_ref):
        xv = x_ref[...].astype(F32)
        if reduce_groups:
            acc = xv[0]
            for k in range(1, g):
                acc = acc + xv[k]
            xv = acc
        r = xv[:, w - 128:]
        out = r * c_ref[...] + sign * (pltpu.roll(r, 96, 1) * sa_ref[...] + pltpu.roll(r, 32, 1) * sb_ref[...])
        if w > 128:
            o_ref[:, :w - 128] = xv[:, :w - 128].astype(out_dtype)
        o_ref[:, w - 128:] = out.astype(out_dtype)

    tab = pl.BlockSpec((TM, 128), lambda h, i: (i, 0))
    if reduce_groups:
        x_spec = pl.BlockSpec((g, TM, w), lambda h, i: (0, i, 0))
        groups = 1
    else:
        x_spec = pl.BlockSpec((None, TM, w), lambda h, i: (h, i, 0))
        groups = g
    return pl.pallas_call(
        body, name=name, grid=(groups, T // TM), in_specs=[x_spec, tab, tab, tab],
        out_specs=pl.BlockSpec((None, TM, w), lambda h, i: (h, i, 0)),
        out_shape=jax.ShapeDtypeStruct((groups, T, w), out_dtype),
        compiler_params=_params(("parallel", "parallel")))(x, cos, sa, sb)


SCALE = (QK_NOPE + QK_ROPE) ** -0.5
LOG2E = 1.4426950408889634
SCALE2 = SCALE * LOG2E


def _diag_mask(transposed):
    shift = CHUNK.bit_length() - 1
    a = lax.broadcasted_iota(jnp.int32, (TQ, TQ), 0) >> shift
    b = lax.broadcasted_iota(jnp.int32, (TQ, TQ), 1) >> shift
    return (a <= b) if transposed else (b <= a)


def _keys(kn_ref, kr_ref, off):
    return jnp.concatenate([kn_ref[pl.ds(off, TQ), :], kr_ref[pl.ds(off, TQ), :]], axis=1)


def _attn_fwd(q, kn, kr, v):
    def body(q_ref, kn_ref, kr_ref, v_ref, o_ref, lse_ref):
        i = pl.program_id(1)
        qv = q_ref[...]

        def step(j, carry, masked):
            m, l, acc = carry
            off = pl.multiple_of(j * TQ, TQ)
            s = lax.dot_general(qv, _keys(kn_ref, kr_ref, off), NT, preferred_element_type=F32) * SCALE2
            if masked:
                s = jnp.where(_diag_mask(False), s, NEG_INF)
            m_new = jnp.maximum(m, jnp.max(s, axis=-1, keepdims=True))
            p = jnp.exp2(s - m_new)
            alpha = jnp.exp2(m - m_new)
            l = alpha * l + jnp.sum(p, axis=-1, keepdims=True)
            acc = alpha * acc + lax.dot_general(p.astype(BF16), v_ref[pl.ds(off, TQ), :], NN, preferred_element_type=F32)
            return m_new, l, acc

        init = (jnp.full((TQ, 1), NEG_INF, F32), jnp.zeros((TQ, 1), F32), jnp.zeros((TQ, V_HEAD), F32))
        carry = lax.fori_loop(0, i, lambda j, cr: step(j, cr, False), init)
        m, l, acc = step(i, carry, True)
        o_ref[...] = (acc / l).astype(BF16)
        lse_ref[...] = m + jnp.log(l) * LOG2E

    return pl.pallas_call(
        body, name="attn_fwd", grid=(N_HEADS, T // TQ),
        in_specs=[pl.BlockSpec((None, TQ, QK_PAD), lambda h, i: (h, i, 0)),
                  pl.BlockSpec((T, QK_NOPE), lambda h, i: (0, h)),
                  pl.BlockSpec((T, 128), lambda h, i: (0, 0)),
                  pl.BlockSpec((T, V_HEAD), lambda h, i: (0, h))],
        out_specs=[pl.BlockSpec((TQ, V_HEAD), lambda h, i: (i, h)), pl.BlockSpec((None, TQ, 1), lambda h, i: (h, i, 0))],
        out_shape=[jax.ShapeDtypeStruct((T, N_HEADS * V_HEAD), BF16), jax.ShapeDtypeStruct((N_HEADS, T, 1), F32)],
        compiler_params=_params(("parallel", "parallel")))(q, kn, kr, v)


def _attn_bwd_dq(q, kn, kr, v, o, do, lse):
    def body(q_ref, kn_ref, kr_ref, v_ref, o_ref, do_ref, lse_ref, dq_ref, dl_ref):
        i = pl.program_id(1)
        qv = q_ref[...]
        dov = do_ref[...]
        lse = lse_ref[...]
        delta = jnp.sum(dov.astype(F32) * o_ref[...].astype(F32), axis=-1, keepdims=True)
        dl_ref[...] = delta

        def step(j, dq, masked):
            off = pl.multiple_of(j * TQ, TQ)
            kk = _keys(kn_ref, kr_ref, off)
            s = lax.dot_general(qv, kk, NT, preferred_element_type=F32) * SCALE2
            if masked:
                s = jnp.where(_diag_mask(False), s, NEG_INF)
            p = jnp.exp2(s - lse)
            dp = lax.dot_general(dov, v_ref[pl.ds(off, TQ), :], NT, preferred_element_type=F32)
            ds = (p * (dp - delta)) * SCALE
            return dq + lax.dot_general(ds.astype(BF16), kk, NN, preferred_element_type=F32)

        dq = lax.fori_loop(0, i, lambda j, acc: step(j, acc, False), jnp.zeros((TQ, QK_PAD), F32))
        dq_ref[...] = step(i, dq, True)

    col = pl.BlockSpec((None, TQ, 1), lambda h, i: (h, i, 0))
    head = pl.BlockSpec((TQ, V_HEAD), lambda h, i: (i, h))
    return pl.pallas_call(
        body, name="attn_bwd_dq", grid=(N_HEADS, T // TQ),
        in_specs=[pl.BlockSpec((None, TQ, QK_PAD), lambda h, i: (h, i, 0)),
                  pl.BlockSpec((T, QK_NOPE), lambda h, i: (0, h)),
                  pl.BlockSpec((T, 128), lambda h, i: (0, 0)),
                  pl.BlockSpec((T, V_HEAD), lambda h, i: (0, h)),
                  head, head, col],
        out_specs=[pl.BlockSpec((None, TQ, QK_PAD), lambda h, i: (h, i, 0)), col],
        out_shape=[jax.ShapeDtypeStruct((N_HEADS, T, QK_PAD), F32), jax.ShapeDtypeStruct((N_HEADS, T, 1), F32)],
        compiler_params=_params(("parallel", "parallel")))(q, kn, kr, v, o, do, lse)


def _attn_bwd_dkv(q, kn, kr, v, do, lse_row, delta_row):
    nq = T // TQ

    def body(q_ref, kn_ref, kr_ref, v_ref, do_ref, lse_ref, dl_ref, dkn_ref, dkr_ref, dv_ref):
        j = pl.program_id(1)
        kk = jnp.concatenate([kn_ref[...], kr_ref[...]], axis=1)
        vv = v_ref[...]

        def step(i, carry, masked):
            dk, dv = carry
            off = pl.multiple_of(i * TQ, TQ)
            qi = q_ref[pl.ds(off, TQ), :]
            doi = do_ref[pl.ds(off, TQ), :]
            st = lax.dot_general(kk, qi, NT, preferred_element_type=F32) * SCALE2
            if masked:
                st = jnp.where(_diag_mask(True), st, NEG_INF)
            pt = jnp.exp2(st - lse_ref[:, pl.ds(off, TQ)])
            dv = dv + lax.dot_general(pt.astype(BF16), doi, NN, preferred_element_type=F32)
            dpt = lax.dot_general(vv, doi, NT, preferred_element_type=F32)
            dst = (pt * (dpt - dl_ref[:, pl.ds(off, TQ)])) * SCALE
            dk = dk + lax.dot_general(dst.astype(BF16), qi, NN, preferred_element_type=F32)
            return dk, dv

        carry = step(j, (jnp.zeros((TQ, QK_PAD), F32), jnp.zeros((TQ, V_HEAD), F32)), True)
        dk, dv = lax.fori_loop(j + 1, nq, lambda i, cr: step(i, cr, False), carry)
        dkn_ref[...] = dk[:, :QK_NOPE].astype(BF16)
        dkr_ref[...] = dk[:, QK_NOPE:]
        dv_ref[...] = dv.astype(BF16)

    row = pl.BlockSpec((None, 1, T), lambda h, j: (h, 0, 0))
    head = pl.BlockSpec((TQ, 128), lambda h, j: (j, h))
    return pl.pallas_call(
        body, name="attn_bwd_dkv", grid=(N_HEADS, nq),
        in_specs=[pl.BlockSpec((None, T, QK_PAD), lambda h, j: (h, 0, 0)),
                  head, pl.BlockSpec((TQ, 128), lambda h, j: (j, 0)), head,
                  pl.BlockSpec((T, V_HEAD), lambda h, j: (0, h)), row, row],
        out_specs=[head, pl.BlockSpec((None, TQ, 128), lambda h, j: (h, j, 0)), head],
        out_shape=[jax.ShapeDtypeStruct((T, N_HEADS * QK_NOPE), BF16), jax.ShapeDtypeStruct((N_HEADS, T, 128), F32),
                   jax.ShapeDtypeStruct((T, N_HEADS * V_HEAD), BF16)],
        compiler_params=_params(("parallel", "parallel")))(q, kn, kr, v, do, lse_row, delta_row)


def _ffn_layer_fwd(tag, h, gain, ex):
    hf = _rms_fwd(f"{tag}_norm", h, gain)
    gv = _mm(f"{tag}_up", hf, ex.need(f"ffn_w_up{tag[1]}", hf), grid=(N_DEV, T // TM),
             a_spec=pl.BlockSpec((TM, D), lambda j, i: (i, 0)),
             b_spec=pl.BlockSpec((None, None, D, FF_BLK), lambda j, i: (0, j, 0, 0)),
             o_spec=pl.BlockSpec((None, TM, FF_BLK), lambda j, i: (j, i, 0)),
             o_shape=(N_DEV, T, FF_BLK), o_dtype=BF16, dims=NN)
    act = _ffn_fwd(f"{tag}_act", gv, ex.need(f"ffn_cw{tag[1]}", gv), ex.need(f"ffn_cb{tag[1]}", gv))
    ex.at(f"{tag}_act", act)
    tn = 512
    out = _mm(f"{tag}_down", act, ex.need(f"ffn_w_down{tag[1]}", act), grid=(D // tn, T // TM, N_FF_BLK),
              a_spec=pl.BlockSpec((None, TM, FF_BLK), lambda n, i, k: (k, i, 0)),
              b_spec=pl.BlockSpec((None, None, FF_BLK, tn), lambda n, i, k: (0, k, 0, n)),
              o_spec=pl.BlockSpec((TM, tn), lambda n, i, k: (i, n)), o_shape=(T, D), o_dtype=F32,
              dims=NN, k_axis=2, acc_shape=(TM, tn), add=h, add_spec=pl.BlockSpec((TM, tn), lambda n, i, k: (i, n)))
    return out, (hf, gv, act)


def _ffn_layer_bwd(tag, h, gain, ex, saved, dh, dh_bf):
    hf, gv, act = saved
    layer = tag[1]
    w_up, w_down4 = ex.need(f"ffn_w_up{layer}", dh_bf), ex.need(f"ffn_w_down{layer}", dh_bf)
    dact = _mm(f"{tag}_dact", dh_bf, w_down4, grid=(N_FF_BLK, T // TM),
               a_spec=pl.BlockSpec((TM, D), lambda j, i: (i, 0)),
               b_spec=pl.BlockSpec((None, None, FF_BLK, D), lambda j, i: (0, j, 0, 0)),
               o_spec=pl.BlockSpec((None, TM, FF_BLK), lambda j, i: (j, i, 0)),
               o_shape=(N_FF_BLK, T, FF_BLK), o_dtype=BF16, dims=NT)
    tn = 512
    g_down = _mm(f"{tag}_gdown", act, dh_bf, grid=(N_FF_BLK, D // tn),
                 a_spec=pl.BlockSpec((None, T, FF_BLK), lambda j, n: (j, 0, 0)),
                 b_spec=pl.BlockSpec((T, tn), lambda j, n: (0, n)),
                 o_spec=pl.BlockSpec((FF_BLK, tn), lambda j, n: (j, n)),
                 o_shape=(D_FF, D), o_dtype=BF16, dims=TN)
    dg, dv, dcw, dcb = _ffn_bwd(f"{tag}_dact_ew", gv, dact, ex.need(f"ffn_cw{layer}", dact), ex.need(f"ffn_cb{layer}", dact))
    ex.at(f"{tag}_dact_ew", dg)
    dhf = None
    g_up = []
    for half, (name, dpart) in enumerate((("g", dg), ("v", dv))):
        dhf = _mm(f"{tag}_dhf_{name}", dpart, w_up, grid=(T // TM, N_FF_BLK),
                  a_spec=pl.BlockSpec((None, TM, FF_BLK), lambda i, k: (k, i, 0)),
                  b_spec=pl.BlockSpec((None, None, D, FF_BLK), lambda i, k, half=half: (0, k + half * N_FF_BLK, 0, 0)),
                  o_spec=pl.BlockSpec((TM, D), lambda i, k: (i, 0)), o_shape=(T, D), o_dtype=F32,
                  dims=NT, k_axis=1, acc_shape=(TM, D), add=dhf)
        g_up.append(_mm(f"{tag}_gup_{name}", hf, dpart, grid=(N_FF_BLK,),
                        a_spec=pl.BlockSpec((T, D), lambda j: (0, 0)),
                        b_spec=pl.BlockSpec((None, T, FF_BLK), lambda j: (j, 0, 0)),
                        o_spec=pl.BlockSpec((None, D, FF_BLK), lambda j: (j, 0, 0)),
                        o_shape=(N_FF_BLK, D, FF_BLK), o_dtype=BF16, dims=TN))
    ex.grad("ffn_w_up", int(layer), jnp.concatenate(g_up, axis=0).reshape(1, N_DEV, D, FF_BLK))
    ex.grad("ffn_w_down", int(layer), g_down.reshape(1, N_DEV, D_FF // N_DEV, D))
    dh_in, dh_in_bf, dgain = _rms_bwd(f"{tag}_dnorm", h, gain, dhf, dres=dh)
    return dh_in, dh_in_bf, dgain, dcw, dcb


def _local_step(x, pos, tgt, rep, ex):
    attn_norm, ffn_norm, final_norm = rep["attn_norm"], rep["ffn_norm"], rep["final_norm"]
    half = QK_ROPE // 2
    inv = 1.0 / (ROPE_THETA ** (jnp.arange(half, dtype=F32) / half))
    inv_freq = jnp.concatenate([inv, inv, jnp.zeros((128 - 2 * half,), F32)]).reshape(1, 128)
    tables = _rope_tables(pos, inv_freq)

    hn0 = _rms_fwd("l0_norm", x, attn_norm[0:1])
    z = _mm_rows("l0_in", hn0, ex.need("sc_w_in", hn0), NN, BF16, 3 * D, tn=512)
    ex.at("l0_in", z)
    y = _sc_fwd(z, ex.need("sc_conv_w", z))
    h1 = _mm_rows("l0_out", y, ex.need("sc_w_out", y), NN, F32, D, tn=512, add=x)
    h2, ffn0 = _ffn_layer_fwd("f0", h1, ffn_norm[0:1], ex)

    hk = _rms_fwd("kv_norm", h2, rep["kv_in_norm"])
    ckv_raw = _mm_rows("kv_down", hk, ex.need("w_dkv", hk), NN, F32, KV_LORA)
    kr_raw = _mm_rows("kv_rope", hk, ex.need("w_kr", hk), NN, F32, 128)
    ckv = _rms_fwd("kv_lnorm", ckv_raw, rep["kv_latent_norm"])
    kn = _mm_rows("kv_uk", ckv, ex.need("w_uk", ckv), NN, BF16, N_HEADS * QK_NOPE)
    vv = _mm_rows("kv_uv", ckv, ex.need("w_uv", ckv), NN, BF16, N_HEADS * V_HEAD)
    kr = _rope("k_rope", kr_raw.reshape(1, T, 128), tables, 1.0, BF16).reshape(T, 128)

    hn1 = _rms_fwd("l1_norm", h2, attn_norm[1:2])
    cq_raw = _mm_rows("q_down", hn1, ex.need("w_dq", hn1), NN, F32, Q_LORA)
    cq = _rms_fwd("q_lnorm", cq_raw, rep["q_latent_norm"])
    w_uq = ex.need("w_uq", cq)
    q_raw = _mm("q_up", cq, w_uq, grid=(N_HEADS, T // TM),
                a_spec=pl.BlockSpec((TM, Q_LORA), lambda h, i: (i, 0)),
                b_spec=pl.BlockSpec((None, Q_LORA, QK_PAD), lambda h, i: (h, 0, 0)),
                o_spec=pl.BlockSpec((None, TM, QK_PAD), lambda h, i: (h, i, 0)),
                o_shape=(N_HEADS, T, QK_PAD), o_dtype=F32, dims=NN)
    q = _rope("q_rope", q_raw, tables, 1.0, BF16)
    o, lse = _attn_fwd(q, kn, kr, vv)
    w_o = ex.need("w_o", o)
    h3 = _mm_rows("attn_out", o, w_o, NN, F32, D, tn=512, add=h2)
    h4, ffn1 = _ffn_layer_fwd("f1", h3, ffn_norm[1:2], ex)

    loss, dh4, dh4_bf, d_final = _final(h4, final_norm.reshape(1, D), tgt)

    dh3, dh3_bf, d_fn1, dcw1, dcb1 = _ffn_layer_bwd("f1", h3, ffn_norm[1:2], ex, ffn1, dh4, dh4_bf)
    ex.at("f1_bwd", dh3)

    do = _mm_rows("d_attn_out", dh3_bf, w_o, NT, BF16, N_HEADS * V_HEAD)
    ex.grad("w_o", None, _mm_wgrad("g_w_o", o, dh3_bf).reshape(1, N_DEV, D // N_DEV, D))
    dq, delta = _attn_bwd_dq(q, kn, kr, vv, o, do, lse)
    ex.at("attn_dq", dq)
    dkn, dkr, dvv = _attn_bwd_dkv(q, kn, kr, vv, do, lse.reshape(N_HEADS, 1, T), delta.reshape(N_HEADS, 1, T))
    dq_pre = _rope("dq_rope", dq, tables, -1.0, BF16)
    dcq = _mm("d_q_up", dq_pre, w_uq, grid=(T // TM, N_HEADS),
              a_spec=pl.BlockSpec((None, TM, QK_PAD), lambda i, h: (h, i, 0)),
              b_spec=pl.BlockSpec((None, Q_LORA, QK_PAD), lambda i, h: (h, 0, 0)),
              o_spec=pl.BlockSpec((TM, Q_LORA), lambda i, h: (i, 0)), o_shape=(T, Q_LORA), o_dtype=F32,
              dims=NT, k_axis=1, acc_shape=(TM, Q_LORA))
    g_uq = _mm("g_w_uq", cq, dq_pre, grid=(N_HEADS,),
               a_spec=pl.BlockSpec((T, Q_LORA), lambda h: (0, 0)),
               b_spec=pl.BlockSpec((None, T, QK_PAD), lambda h: (h, 0, 0)),
               o_spec=pl.BlockSpec((None, Q_LORA, QK_PAD), lambda h: (h, 0, 0)),
               o_shape=(N_HEADS, Q_LORA, QK_PAD), o_dtype=BF16, dims=TN)
    ex.grad("w_uq", None, g_uq[:, :, :QK_NOPE + QK_ROPE].reshape(1, N_DEV, Q_LORA, QK_NOPE + QK_ROPE))
    _, dcq_raw_bf, d_qln = _rms_bwd("d_q_lnorm", cq_raw, rep["q_latent_norm"], dcq)
    dhn1 = _mm_rows("d_q_down", dcq_raw_bf, ex.need("w_dq", dcq_raw_bf), NT, F32, D)
    ex.grad("w_dq", None, _mm_wgrad("g_w_dq", hn1, dcq_raw_bf).reshape(1, N_DEV, D // N_DEV, Q_LORA))
    dh2_a, _, d_an1 = _rms_bwd("d_l1_norm", h2, attn_norm[1:2], dhn1, dres=dh3)

    dckv = _mm_rows("d_kv_uk", dkn, ex.need("w_uk", dkn), NT, F32, KV_LORA)
    dckv = _mm_rows("d_kv_uv", dvv, ex.need("w_uv", dvv), NT, F32, KV_LORA, add=dckv)
    ex.grad("w_uk", None, _mm_wgrad("g_w_uk", ckv, dkn))
    ex.grad("w_uv", None, _mm_wgrad("g_w_uv", ckv, dvv))
    _, dckv_raw_bf, d_kvln = _rms_bwd("d_kv_lnorm", ckv_raw, rep["kv_latent_norm"], dckv)
    dkr_raw_bf = _rope("dk_rope", dkr, tables, -1.0, BF16, reduce_groups=True).reshape(T, 128)
    dhk = _mm_rows("d_kv_down", dckv_raw_bf, ex.need("w_dkv", dckv_raw_bf), NT, F32, D)
    dhk = _mm_rows("d_kv_rope", dkr_raw_bf, ex.need("w_kr", dkr_raw_bf), NT, F32, D, add=dhk)
    ex.grad("w_dkv", None, _mm_wgrad("g_w_dkv", hk, dckv_raw_bf).reshape(1, N_DEV, D // N_DEV, KV_LORA))
    ex.grad("w_kr", None, _mm_wgrad("g_w_kr", hk, dkr_raw_bf)[:, :QK_ROPE].reshape(1, N_DEV, D // N_DEV, QK_ROPE))
    dh2, dh2_bf, d_kvin = _rms_bwd("d_kv_norm", h2, rep["kv_in_norm"], dhk, dres=dh2_a)
    ex.at("kv_bwd", dh2)

    dh1, dh1_bf, d_fn0, dcw0, dcb0 = _ffn_layer_bwd("f0", h1, ffn_norm[0:1], ex, ffn0, dh2, dh2_bf)
    ex.at("f0_bwd", dh1)

    dy = _mm_rows("d_l0_out", dh1_bf, ex.need("sc_w_out", dh1_bf), NT, F32, D)
    ex.grad("sc_w_out", None, _mm_wgrad("g_sc_w_out", y, dh1_bf).reshape(1, N_DEV, D // N_DEV, D))
    dz, d_scw = _sc_bwd(z, dy, ex.need("sc_conv_w", dy))
    ex.at("sc_bwd", dz)
    dhn0 = _mm_rows("d_l0_in", dz, ex.need("sc_w_in", dz), NT, F32, D)
    ex.grad("sc_w_in", None, _mm_wgrad("g_sc_w_in", hn0, dz))
    grad_x, _, d_an0 = _rms_bwd("d_l0_norm", x, attn_norm[0:1], dhn0, dres=dh1)

    small = {
        "attn_norm": jnp.concatenate([d_an0, d_an1], axis=0),
        "ffn_norm": jnp.concatenate([d_fn0, d_fn1], axis=0),
        "final_norm": d_final.reshape(D),
        "kv_in_norm": d_kvin.reshape(D),
        "kv_latent_norm": d_kvln.reshape(KV_LORA),
        "q_latent_norm": d_qln,
        "ffn_conv_b": jnp.stack([dcb0, dcb1]).transpose(0, 2, 1, 3).reshape(2, D_FF),
        "sc_conv_w": d_scw,
        "ffn_conv_w": jnp.stack([dcw0, dcw1]).transpose(0, 2, 1, 3).reshape(2, 3, D_FF),
    }
    return loss, grad_x, small


def _place():
    return lax.axis_index("x"), lax.axis_index("y"), lax.axis_index("c")


def _peers():
    x, y, c = _place()
    return (x, y, 1 - c), [(1 - x, y), (x, 1 - y), (1 - x, 1 - y)]


def _window(ref, kind, dev):
    if kind == "blocked":
        return ref.at[:, dev]
    width = ref.shape[-1] // N_DEV
    return ref.at[:, pl.ds(pl.multiple_of(dev * width, 128), width)]


def _all_gather(name, items):
    n = len(items)
    out_shapes = []
    for shard, kind in items:
        if kind == "blocked":
            shape = (shard.shape[0], N_DEV) + shard.shape[1:]
        else:
            shape = (shard.shape[0], N_DEV * shard.shape[1])
        out_shapes.append(jax.ShapeDtypeStruct(shape, shard.dtype))

    def body(*refs):
        srcs, outs = refs[:n], refs[n:2 * n]
        send_sems, recv_sems, local_sems = refs[2 * n:]
        x, y, c = _place()
        me = 4 * x + 2 * y + c
        sibling, chips = _peers()

        def num(px, py, pc):
            return 4 * px + 2 * py + pc

        def copy(t, k, dev, to, from_src):
            kind = items[t][1]
            dst = _window(outs[t], kind, dev)
            return pltpu.make_async_remote_copy(
                src_ref=srcs[t] if from_src else dst, dst_ref=dst,
                send_sem=send_sems.at[t, k], recv_sem=recv_sems.at[t, k], device_id=to, device_id_type=MESH)

        mine = [pltpu.make_async_copy(srcs[t], _window(outs[t], items[t][1], me), local_sems.at[t]) for t in range(n)]
        for cp in mine:
            cp.start()
        first = []
        for t in range(n):
            first.append(copy(t, 0, me, sibling, True))
            for j, chip in enumerate(chips):
                first.append(copy(t, 1 + j, me, (*chip, c), True))
        for cp in first:
            cp.start()
        passed = []
        for j, chip in enumerate(chips):
            for t in range(n):
                copy(t, 1 + j, num(*chip, c), (x, y, c), False).wait_recv()
                fwd = copy(t, 4 + j, num(*chip, c), sibling, False)
                fwd.start()
                passed.append(fwd)
        for t in range(n):
            copy(t, 0, num(x, y, 1 - c), (x, y, c), False).wait_recv()
            for j, chip in enumerate(chips):
                copy(t, 4 + j, num(*chip, 1 - c), (x, y, c), False).wait_recv()
        for cp in first + passed:
            cp.wait_send()
        for cp in mine:
            cp.wait()

    any_spec = pl.BlockSpec(memory_space=pl.ANY)
    return pl.pallas_call(
        body, name=name, in_specs=[any_spec] * n, out_specs=[any_spec] * n, out_shape=out_shapes,
        scratch_shapes=[pltpu.SemaphoreType.DMA((n, 7)), pltpu.SemaphoreType.DMA((n, 7)), pltpu.SemaphoreType.DMA((n,))],
    )(*[s for s, _ in items])


HBM_SPEC = pl.BlockSpec(memory_space=pltpu.HBM)
SEM_SPEC = pl.BlockSpec(memory_space=pltpu.SEMAPHORE)
EFFECT = pltpu.SideEffectType.DATAFLOW_SIDE_EFFECTING


def _hbm(a):
    return pltpu.with_memory_space_constraint(a, pltpu.HBM)


def _copies_start(name, srcs, lands, ncopy, plan):
    ns, nl = len(srcs), len(lands)

    def body(*refs):
        send, recv = refs[ns + nl], refs[ns + nl + 1]
        copies = plan(refs[:ns], refs[ns:ns + nl])
        assert len(copies) == ncopy
        for k, (sent, dst, to, _) in enumerate(copies):
            pltpu.make_async_remote_copy(src_ref=sent, dst_ref=dst, send_sem=send.at[k], recv_sem=recv.at[k],
                                         device_id=to, device_id_type=MESH).start()

    arrays = list(srcs) + list(lands)
    outs = pl.pallas_call(
        body, name=name, in_specs=[HBM_SPEC] * (ns + nl), out_specs=[SEM_SPEC] * 2 + [HBM_SPEC] * (ns + nl),
        out_shape=[pltpu.SemaphoreType.DMA((ncopy,))] * 2 + [pltpu.HBM(a.shape, a.dtype) for a in arrays],
        input_output_aliases={i: 2 + i for i in range(ns + nl)},
        compiler_params=pltpu.CompilerParams(has_side_effects=EFFECT))(*[_hbm(a) for a in arrays])
    return outs[0], outs[1], list(outs[2:2 + ns]), list(outs[2 + ns:])


def _copies_wait(name, started, ncopy, plan, after):
    send, recv, srcs, lands = started
    ns, nl = len(srcs), len(lands)

    def body(*refs):
        send_ref, recv_ref = refs[ns + nl], refs[ns + nl + 1]
        copies = plan(refs[:ns], refs[ns:ns + nl])
        assert len(copies) == ncopy
        for k, (sent, _, to, landed) in enumerate(copies):
            cp = pltpu.make_async_remote_copy(src_ref=sent, dst_ref=landed, send_sem=send_ref.at[k],
                                              recv_sem=recv_ref.at[k], device_id=to, device_id_type=MESH)
            cp.wait_send()
            cp.wait_recv()

    arrays = list(srcs) + list(lands)
    outs = pl.pallas_call(
        body, name=name, in_specs=[HBM_SPEC] * (ns + nl) + [SEM_SPEC] * 2 + [pl.BlockSpec(memory_space=pl.ANY)],
        out_specs=[HBM_SPEC] * (ns + nl), out_shape=[pltpu.HBM(a.shape, a.dtype) for a in arrays],
        input_output_aliases={i: i for i in range(ns + nl)},
        compiler_params=pltpu.CompilerParams(has_side_effects=EFFECT))(*arrays, send, recv, after)
    return list(outs[:ns]), list(outs[ns:])


def _plan_gather_chips(kinds):
    def plan(srcs, lands):
        x, y, c = _place()
        sibling, chips = _peers()
        out = []
        for t, kind in enumerate(kinds):
            mine = _window(lands[t], kind, 4 * x + 2 * y + c)
            out.append((srcs[t], mine, sibling, _window(lands[t], kind, 4 * x + 2 * y + 1 - c)))
            for px, py in chips:
                out.append((srcs[t], mine, (px, py, c), _window(lands[t], kind, 4 * px + 2 * py + c)))
        return out
    return plan, 4 * len(kinds)


def _plan_gather_sibling(kinds):
    def plan(srcs, lands):
        _, _, c = _place()
        sibling, chips = _peers()
        out = []
        for t, kind in enumerate(kinds):
            for px, py in chips:
                w = _window(lands[t], kind, 4 * px + 2 * py + c)
                out.append((w, w, sibling, _window(lands[t], kind, 4 * px + 2 * py + 1 - c)))
        return out
    return plan, 3 * len(kinds)


def _plan_scatter_sibling(kinds):
    def plan(srcs, lands):
        _, _, c = _place()
        sibling, _ = _peers()
        out = []
        for t, kind in enumerate(kinds):
            for k in range(N_CHIP):
                out.append((_window(srcs[t], kind, 2 * k + 1 - c), lands[t].at[k], sibling, lands[t].at[k]))
        return out
    return plan, N_CHIP * len(kinds)


def _plan_scatter_chips(n):
    def plan(srcs, lands):
        x, y, c = _place()
        _, chips = _peers()
        out = []
        for t in range(n):
            for px, py in chips:
                out.append((srcs[t].at[2 * px + py], lands[t].at[2 * x + y], (px, py, c), lands[t].at[2 * px + py]))
        return out
    return plan, 3 * n


def _landing(shard, kind, me):
    if kind == "blocked":
        land = lax.empty((shard.shape[0], N_DEV) + shard.shape[1:], shard.dtype)
        return lax.dynamic_update_slice(land, shard[:, None], (0, me) + (0,) * (shard.ndim - 1))
    land = lax.empty((shard.shape[0], N_DEV * shard.shape[1]), shard.dtype)
    return lax.dynamic_update_slice(land, shard, (0, me * shard.shape[1]))


def _row_tile(rows):
    for tr in (512, 384, 352, 256, 128, 64, 32, 16):
        if rows % tr == 0:
            return tr
    raise ValueError(rows)


def _chip_sum(name, gr, kind, recv, c):
    if kind == "blocked":
        nl, _, r, w = gr.shape
        rows = nl * r
        tr = _row_tile(r)
        per = r // tr
        g_spec = pl.BlockSpec((None, None, tr, w), lambda k, i, cref: (i // per, 2 * k + cref[0], i % per, 0))
    else:
        rows, w = gr.shape[0], gr.shape[1] // N_DEV
        tr = _row_tile(rows)
        g_spec = pl.BlockSpec((tr, w), lambda k, i, cref: (i, 2 * k + cref[0]))
    recv = recv.reshape(N_CHIP, rows, w)

    def body(c_ref, g_ref, r_ref, o_ref):
        del c_ref
        o_ref[...] = (g_ref[...].astype(F32) + r_ref[...].astype(F32)).astype(BF16)

    blk = pl.BlockSpec((None, tr, w), lambda k, i, cref: (k, i, 0))
    return pl.pallas_call(
        body, name=name,
        grid_spec=pltpu.PrefetchScalarGridSpec(num_scalar_prefetch=1, grid=(N_CHIP, rows // tr),
                                               in_specs=[g_spec, blk], out_specs=blk),
        out_shape=jax.ShapeDtypeStruct((N_CHIP, rows, w), BF16),
        compiler_params=_params(("parallel", "parallel")))(c, gr, recv)


def _adamw_math(g, wv, mv, vv):
    m = ADAM_B1 * mv + (1.0 - ADAM_B1) * g
    v = ADAM_B2 * vv + (1.0 - ADAM_B2) * (g * g)
    m_hat = m / (1.0 - ADAM_B1 ** ADAM_STEP)
    v_hat = v / (1.0 - ADAM_B2 ** ADAM_STEP)
    delta = -ADAM_LR * (m_hat / (jnp.sqrt(v_hat) + ADAM_EPS) + ADAM_WD * wv)
    return delta, m, v


def _adamw_sharded(name, own, recv, chip_ids, w3, m3, v3, layer, prev):
    nl, rows, w = w3.shape
    tr = _row_tile(rows)
    has_prev = prev is not None

    def body(*refs):
        own_ref, r1_ref, r2_ref, r3_ref, w_ref, m_ref, v_ref = refs[1:8]
        g_ref, d_ref, nm_ref, nv_ref = refs[-4:]
        g = ((own_ref[...].astype(F32) + r1_ref[...].astype(F32)) + r2_ref[...].astype(F32)) + r3_ref[...].astype(F32)
        g_ref[...] = g
        d_ref[...], nm_ref[...], nv_ref[...] = _adamw_math(g, w_ref[...], m_ref[...], v_ref[...])

    def pick(slot):
        return pl.BlockSpec((None, tr, w), lambda i, ids: (ids[slot], i, 0))

    slab = pl.BlockSpec((None, tr, w), lambda i, ids: (layer, i, 0))
    in_specs = [pick(0), pick(1), pick(2), pick(3), slab, slab, slab]
    args = [chip_ids, own, recv, recv, recv, w3, m3, v3]
    aliases = {}
    if has_prev:
        in_specs += [pl.BlockSpec(memory_space=pl.ANY)] * 4
        aliases = {len(args) + k: k for k in range(4)}
        args += list(prev)
    return pl.pallas_call(
        body, name=name,
        grid_spec=pltpu.PrefetchScalarGridSpec(num_scalar_prefetch=1, grid=(rows // tr,),
                                               in_specs=in_specs, out_specs=[slab] * 4),
        out_shape=[jax.ShapeDtypeStruct((nl, rows, w), F32)] * 4, input_output_aliases=aliases,
        compiler_params=_params(("parallel",)))(*args)


def _adamw_small(parts, wv, mv, vv):
    r = wv.shape[0]

    def body(p_ref, w_ref, m_ref, v_ref, g_ref, d_ref, nm_ref, nv_ref):
        g = p_ref[0]
        for k in range(1, N_DEV):
            g = g + p_ref[k]
        g_ref[...] = g
        d_ref[...], nm_ref[...], nv_ref[...] = _adamw_math(g, w_ref[...], m_ref[...], v_ref[...])

    return pl.pallas_call(
        body, name="adamw_small",
        out_shape=[jax.ShapeDtypeStruct((r, 128), F32)] * 4,
        compiler_params=pltpu.CompilerParams(vmem_limit_bytes=VMEM_LIMIT))(parts, wv, mv, vv)


KIND = {"sc_w_in": "cols", "sc_w_out": "blocked", "w_dkv": "blocked", "w_kr": "blocked", "w_uk": "cols", "w_uv": "cols",
        "w_dq": "blocked", "w_uq": "blocked", "w_o": "blocked", "ffn_w_up": "blocked", "ffn_w_down": "blocked",
        "conv": "blocked"}
GATHER_GROUPS = (("mixer", ("sc_w_in", "sc_w_out", "conv")),
                 ("ffn0", ("ffn_w_up0", "ffn_w_down0")),
                 ("layer1", ("w_dkv", "w_kr", "w_uk", "w_uv", "w_dq", "w_uq", "w_o", "ffn_w_up1", "ffn_w_down1")))
SCATTER_GROUPS = (("ffn1", (("ffn_w_up", 1), ("ffn_w_down", 1))),
                  ("attn", (("w_o", None), ("w_uq", None), ("w_dq", None), ("w_uk", None), ("w_uv", None),
                            ("w_dkv", None), ("w_kr", None))),
                  ("ffn0", (("ffn_w_up", 0), ("ffn_w_down", 0))),
                  ("mixer", (("sc_w_out", None), ("sc_w_in", None))))
SCHEDULE = {
    "l0_in": (("gather_forward", "ffn0"),),
    "f0_act": (("gather_forward", "layer1"),),
    "f1_bwd": (("scatter_sibling", "ffn1"),),
    "attn_dq": (("scatter_chips", "ffn1"),),
    "kv_bwd": (("scatter_sibling", "attn"), ("scatter_done", "ffn1")),
    "f0_dact_ew": (("scatter_chips", "attn"),),
    "f0_bwd": (("scatter_sibling", "ffn0"), ("scatter_done", "attn")),
    "sc_bwd": (("scatter_chips", "ffn0"),),
}
FINISH = (("scatter_sibling", "mixer"), ("scatter_chips", "mixer"), ("scatter_done", "ffn0"), ("scatter_done", "mixer"))
STAGES = {"gather_forward": 2, "gather_done": 3, "scatter_sibling": 1, "scatter_chips": 2, "scatter_done": 3}
SMALL_W_ROWS = 24
SMALL_G_ROWS = 256


def _pack(arrays, rows):
    flat = jnp.concatenate([a.reshape(-1).astype(F32) for a in arrays])
    return jnp.pad(flat, (0, rows * 128 - flat.shape[0])).reshape(rows, 128)


def _unpack(packed, shapes):
    flat = packed.reshape(-1)
    out, off = [], 0
    for shape in shapes:
        size = 1
        for s in shape:
            size *= s
        out.append(flat[off:off + size].reshape(shape))
        off += size
    return out


def _base(name):
    if name.startswith("ffn_w_") and name[-1] in "01":
        return name[:-1], int(name[-1])
    return name, None


class _Exchange:
    def __init__(self, wts, mom, var, ffn_conv_b):
        self.wts, self.mom, self.var = wts, mom, var
        x, y, c = _place()
        self.me = 4 * x + 2 * y + c
        self.c_arr = jnp.reshape(c, (1,)).astype(jnp.int32)
        chip = 2 * x + y
        self.chip_ids = jnp.stack([chip, chip ^ 1, chip ^ 2, chip ^ 3]).astype(jnp.int32)
        self.ready = {"ffn_cb0": ffn_conv_b.reshape(2, N_FF_BLK, 1, FF_BLK)[0],
                      "ffn_cb1": ffn_conv_b.reshape(2, N_FF_BLK, 1, FF_BLK)[1]}
        self.gathers, self.group_of = {}, {}
        self.grads, self.scatters, self.results = {}, {}, {}
        for gname, names in GATHER_GROUPS:
            shards = [self._shard(nm) for nm in names]
            kinds = [KIND[_base(nm)[0]] for nm in names]
            lands = [_landing(s, kind, self.me) for s, kind in zip(shards, kinds)]
            plan, ncopy = _plan_gather_chips(kinds)
            self.gathers[gname] = dict(stage=1, names=names, kinds=kinds,
                                       flight=_copies_start(f"ag_{gname}_chips", shards, lands, ncopy, plan))
            for nm in names:
                self.group_of[nm] = gname
        for nm in ("sc_conv_w", "ffn_cw0", "ffn_cw1"):
            self.group_of[nm] = "mixer"

    def _shard(self, name):
        if name == "conv":
            return _pack([self.wts["sc_conv_w"], self.wts["ffn_conv_w"]], SMALL_W_ROWS).reshape(1, SMALL_W_ROWS, 128)
        base, layer = _base(name)
        a = self.wts[base]
        if layer is not None:
            a = a[layer:layer + 1]
        if KIND[base] == "cols":
            return a.reshape(a.shape[-2], a.shape[-1]).astype(BF16)
        return a.reshape((-1,) + a.shape[-2:]).astype(BF16)

    def _gather_to(self, gname, stage, after):
        st = self.gathers[gname]
        if st["stage"] < 2 <= stage:
            plan, ncopy = _plan_gather_chips(st["kinds"])
            _, lands = _copies_wait(f"ag_{gname}_chips_wait", st["flight"], ncopy, plan, after)
            plan, ncopy = _plan_gather_sibling(st["kinds"])
            st["flight"] = _copies_start(f"ag_{gname}_sibling", [], lands, ncopy, plan)
            st["stage"] = 2
        if st["stage"] < 3 <= stage:
            plan, ncopy = _plan_gather_sibling(st["kinds"])
            _, lands = _copies_wait(f"ag_{gname}_sibling_wait", st["flight"], ncopy, plan, after)
            for nm, land in zip(st["names"], lands):
                self._arrived(nm, land)
            st["stage"] = 3

    def _arrived(self, name, land):
        if name == "conv":
            conv = land.reshape(N_DEV, SMALL_W_ROWS * 128)
            self.ready["sc_conv_w"] = conv[:, :3 * 128].reshape(N_DEV, 3, 128).transpose(1, 0, 2).reshape(3, D)
            fcw = conv[:, 3 * 128:3 * 128 + 6 * 352].reshape(N_DEV, 2, 3, 352).transpose(1, 2, 0, 3)
            fcw = fcw.reshape(2, 3, N_FF_BLK, FF_BLK).transpose(0, 2, 1, 3)
            self.ready["ffn_cw0"], self.ready["ffn_cw1"] = fcw[0], fcw[1]
        elif name in ("sc_w_in", "w_uk", "w_uv") or name.startswith("ffn_w_up"):
            self.ready[name] = land
        elif name.startswith("ffn_w_down"):
            self.ready[name] = land.reshape(1, N_FF_BLK, FF_BLK, D)
        elif name == "w_kr":
            self.ready[name] = jnp.pad(land.reshape(D, QK_ROPE), ((0, 0), (0, 128 - QK_ROPE)))
        elif name == "w_uq":
            self.ready[name] = jnp.pad(land.reshape(N_HEADS, Q_LORA, QK_NOPE + QK_ROPE),
                                       ((0, 0), (0, 0), (0, QK_PAD - QK_NOPE - QK_ROPE)))
        else:
            self.ready[name] = land.reshape(D, land.shape[-1])

    def need(self, name, after):
        if name not in self.ready:
            self._gather_to(self.group_of[name], 3, after)
        return self.ready[name]

    def grad(self, name, layer, array):
        self.grads[(name, layer)] = array

    def _scatter_to(self, gname, stage, after):
        keys = dict(SCATTER_GROUPS)[gname]
        st = self.scatters.setdefault(gname, dict(stage=0))
        kinds = [KIND[nm] for nm, _ in keys]
        if st["stage"] < 1 <= stage:
            grads = [self.grads[key] for key in keys]
            lands = []
            for gr, kind in zip(grads, kinds):
                shard = (gr.shape[0],) + gr.shape[2:] if kind == "blocked" else (gr.shape[0], gr.shape[1] // N_DEV)
                lands.append(lax.empty((N_CHIP,) + shard, BF16))
            plan, ncopy = _plan_scatter_sibling(kinds)
            st["flight"] = _copies_start(f"rs_{gname}_sibling", grads, lands, ncopy, plan)
            st["stage"] = 1
        if st["stage"] < 2 <= stage:
            plan, ncopy = _plan_scatter_sibling(kinds)
            grads, recvs = _copies_wait(f"rs_{gname}_sibling_wait", st["flight"], ncopy, plan, after)
            sums = [_chip_sum(f"rs_{gname}_sum{t}", gr, kind, rv, self.c_arr)
                    for t, (gr, kind, rv) in enumerate(zip(grads, kinds, recvs))]
            lands = [lax.empty(s.shape, BF16) for s in sums]
            plan, ncopy = _plan_scatter_chips(len(sums))
            st["flight"] = _copies_start(f"rs_{gname}_chips", sums, lands, ncopy, plan)
            st["stage"] = 2
        if st["stage"] < 3 <= stage:
            plan, ncopy = _plan_scatter_chips(len(keys))
            sums, recvs = _copies_wait(f"rs_{gname}_chips_wait", st["flight"], ncopy, plan, after)
            for t, ((nm, layer), own, rv) in enumerate(zip(keys, sums, recvs)):
                nl = 1 if layer is None else 2
                rows, w = own.shape[1], own.shape[2]
                w3, m3, v3 = (src[nm].reshape(nl, rows, w) for src in (self.wts, self.mom, self.var))
                self.results[nm] = _adamw_sharded(f"adamw_{gname}_{t}", own, rv, self.chip_ids, w3, m3, v3,
                                                  0 if layer is None else layer, self.results.get(nm))
            st["stage"] = 3

    def at(self, place, after):
        for action, gname in SCHEDULE.get(place, ()):
            self._advance(action, gname, after)

    def _advance(self, action, gname, after):
        if action.startswith("gather"):
            self._gather_to(gname, STAGES[action], after)
        else:
            self._scatter_to(gname, STAGES[action], after)

    def finish(self, after):
        for action, gname in FINISH:
            self._advance(action, gname, after)
        for gname, _ in SCATTER_GROUPS:
            self._scatter_to(gname, 3, after)
        return {nm: [o.reshape(self.wts[nm].shape) for o in outs] for nm, outs in self.results.items()}


REPLICATED = ("attn_norm", "ffn_norm", "final_norm", "kv_in_norm", "kv_latent_norm", "q_latent_norm", "ffn_conv_b")
WEIGHTS = ("attn_norm", "ffn_norm", "final_norm", "sc_w_in", "sc_conv_w", "sc_w_out", "kv_in_norm", "w_dkv",
           "kv_latent_norm", "w_kr", "w_uk", "w_uv", "w_dq", "q_latent_norm", "w_uq", "w_o", "ffn_w_up", "ffn_conv_w",
           "ffn_conv_b", "ffn_w_down")


def kernel(x, positions, attn_norm, ffn_norm, final_norm, sc_w_in, sc_conv_w, sc_w_out, kv_in_norm, w_dkv, kv_latent_norm, w_kr, w_uk, w_uv, w_dq, q_latent_norm, w_uq, w_o, ffn_w_up, ffn_conv_w, ffn_conv_b, ffn_w_down, loss_target, m_attn_norm, m_ffn_norm, m_final_norm, m_sc_w_in, m_sc_conv_w, m_sc_w_out, m_kv_in_norm, m_w_dkv, m_kv_latent_norm, m_w_kr, m_w_uk, m_w_uv, m_w_dq, m_q_latent_norm, m_w_uq, m_w_o, m_ffn_w_up, m_ffn_conv_w, m_ffn_conv_b, m_ffn_w_down, v_attn_norm, v_ffn_norm, v_final_norm, v_sc_w_in, v_sc_conv_w, v_sc_w_out, v_kv_in_norm, v_w_dkv, v_kv_latent_norm, v_w_kr, v_w_uk, v_w_uv, v_w_dq, v_q_latent_norm, v_w_uq, v_w_o, v_ffn_w_up, v_ffn_conv_w, v_ffn_conv_b, v_ffn_w_down):
    wts = dict(attn_norm=attn_norm, ffn_norm=ffn_norm, final_norm=final_norm, sc_w_in=sc_w_in, sc_conv_w=sc_conv_w,
               sc_w_out=sc_w_out, kv_in_norm=kv_in_norm, w_dkv=w_dkv, kv_latent_norm=kv_latent_norm, w_kr=w_kr,
               w_uk=w_uk, w_uv=w_uv, w_dq=w_dq, q_latent_norm=q_latent_norm, w_uq=w_uq, w_o=w_o, ffn_w_up=ffn_w_up,
               ffn_conv_w=ffn_conv_w, ffn_conv_b=ffn_conv_b, ffn_w_down=ffn_w_down)
    mom = dict(attn_norm=m_attn_norm, ffn_norm=m_ffn_norm, final_norm=m_final_norm, sc_w_in=m_sc_w_in,
               sc_conv_w=m_sc_conv_w, sc_w_out=m_sc_w_out, kv_in_norm=m_kv_in_norm, w_dkv=m_w_dkv,
               kv_latent_norm=m_kv_latent_norm, w_kr=m_w_kr, w_uk=m_w_uk, w_uv=m_w_uv, w_dq=m_w_dq,
               q_latent_norm=m_q_latent_norm, w_uq=m_w_uq, w_o=m_w_o, ffn_w_up=m_ffn_w_up, ffn_conv_w=m_ffn_conv_w,
               ffn_conv_b=m_ffn_conv_b, ffn_w_down=m_ffn_w_down)
    var = dict(attn_norm=v_attn_norm, ffn_norm=v_ffn_norm, final_norm=v_final_norm, sc_w_in=v_sc_w_in,
               sc_conv_w=v_sc_conv_w, sc_w_out=v_sc_w_out, kv_in_norm=v_kv_in_norm, w_dkv=v_w_dkv,
               kv_latent_norm=v_kv_latent_norm, w_kr=v_w_kr, w_uk=v_w_uk, w_uv=v_w_uv, w_dq=v_w_dq,
               q_latent_norm=v_q_latent_norm, w_uq=v_w_uq, w_o=v_w_o, ffn_w_up=v_ffn_w_up, ffn_conv_w=v_ffn_conv_w,
               ffn_conv_b=v_ffn_conv_b, ffn_w_down=v_ffn_w_down)
    xi, yi, ci = _place()
    me = 4 * xi + 2 * yi + ci

    ex = _Exchange(wts, mom, var, ffn_conv_b)
    rep = {
        "attn_norm": attn_norm, "ffn_norm": ffn_norm, "final_norm": final_norm,
        "kv_in_norm": kv_in_norm.reshape(1, D), "kv_latent_norm": kv_latent_norm.reshape(1, KV_LORA),
        "q_latent_norm": q_latent_norm.reshape(1, Q_LORA),
    }
    loss, grad_x, small = _local_step(x.reshape(T, D), positions.reshape(T, 1), loss_target.reshape(T, D), rep, ex)
    results = ex.finish(grad_x)

    small_order = list(REPLICATED) + ["sc_conv_w", "ffn_conv_w"]
    packed_g = _pack([loss[0, 0:1]] + [small[nm] for nm in small_order], SMALL_G_ROWS)
    parts = _all_gather("ag_small_grads", [(packed_g.reshape(1, SMALL_G_ROWS, 128), "blocked")])[0]
    parts = parts.reshape(N_DEV, SMALL_G_ROWS, 128)

    def full_params(src):
        scw_full = jnp.zeros((3, D), F32)
        scw_full = lax.dynamic_update_slice(scw_full, src["sc_conv_w"].reshape(3, 128), (0, me * 128))
        fcw_full = jnp.zeros((2, 3, D_FF), F32)
        fcw_full = lax.dynamic_update_slice(fcw_full, src["ffn_conv_w"], (0, 0, me * 352))
        return _pack([jnp.zeros((1,), F32)] + [src[nm] for nm in REPLICATED] + [scw_full, fcw_full], SMALL_G_ROWS)

    small_out = _adamw_small(parts, full_params(wts), full_params(mom), full_params(var))
    shapes = [(1,)] + [wts[nm].shape for nm in REPLICATED] + [(3, D), (2, 3, D_FF)]
    unpacked = [_unpack(o, shapes) for o in small_out]
    loss_total = unpacked[0][0].reshape(())
    for slot, nm in enumerate(small_order):
        vals = [u[slot + 1] for u in unpacked]
        if nm == "sc_conv_w":
            vals = [lax.dynamic_slice(a, (0, me * 128), (3, 128)).reshape(1, 3, 128) for a in vals]
        elif nm == "ffn_conv_w":
            vals = [lax.dynamic_slice(a, (0, 0, me * 352), (2, 3, 352)) for a in vals]
        results[nm] = vals

    outs = [loss_total, grad_x.reshape(1, T, D)]
    for slot in range(4):
        outs.extend(results[nm][slot] for nm in WEIGHTS)
    return tuple(outs)
```

```python
import jax
import jax.numpy as jnp
from jax import lax
from jax.experimental import pallas as pl
from jax.experimental.pallas import tpu as pltpu

F32 = jnp.float32
BF16 = jnp.bfloat16

T = 2048
D = 1024
N_HEADS = 8
QK_NOPE = 128
QK_ROPE = 64
V_HEAD = 128
Q_LORA = 384
KV_LORA = 256
D_FF = 2816
CHUNK = 64
ROPE_THETA = 10000.0
EPS = 1e-6
NEG_INF = -1e30
ADAM_LR = 0.001
ADAM_B1 = 0.9
ADAM_B2 = 0.999
ADAM_EPS = 1e-08
ADAM_WD = 0.01
ADAM_STEP = 10

N_DEV = 8
N_CHIP = 4
FF_BLK = D_FF * 2 // N_DEV
N_FF_BLK = D_FF // FF_BLK
QK_PAD = 256
HALO = 16

TM = 512
TR = 256
TQ = 256
VMEM_LIMIT = 56 * 1024 * 1024

NN = (((1,), (0,)), ((), ()))
NT = (((1,), (1,)), ((), ()))
TN = (((0,), (0,)), ((), ()))
MESH = pl.DeviceIdType.MESH


def _params(sem):
    return pltpu.CompilerParams(dimension_semantics=sem, vmem_limit_bytes=VMEM_LIMIT)


ANY_SPEC = pl.BlockSpec(memory_space=pl.ANY)
VMEM_SPEC = pl.BlockSpec(memory_space=pltpu.VMEM)


class _Chain:
    last = None


def _pallas(body, *, name, in_specs, out_specs, out_shape, grid=(), scratch_shapes=(), n_prefetch=0, aliases=None,
            params=None):
    def run(*args):
        after = _Chain.last
        n_lead = len(args)
        specs, operands, fn = list(in_specs), list(args), body
        if after is not None:
            def fn(*refs):
                return body(*refs[:n_lead], *refs[n_lead + 1:])
            specs.append(ANY_SPEC)
            operands.append(after)
        kw = dict(name=name, out_shape=out_shape, input_output_aliases=aliases or {})
        if params is not None:
            kw["compiler_params"] = params
        if n_prefetch:
            kw["grid_spec"] = pltpu.PrefetchScalarGridSpec(
                num_scalar_prefetch=n_prefetch, grid=grid, in_specs=specs, out_specs=out_specs,
                scratch_shapes=scratch_shapes)
        else:
            kw.update(grid=grid, in_specs=specs, out_specs=out_specs, scratch_shapes=scratch_shapes)
        outs = pl.pallas_call(fn, **kw)(*operands)
        _Chain.last = outs[0] if isinstance(outs, (list, tuple)) else outs
        return outs
    return run


def _mm(name, a, b, *, grid, a_spec, b_spec, o_spec, o_shape, o_dtype, dims, k_axis=None, acc_shape=None,
        add=None, add_spec=None):
    nk = grid[k_axis] if k_axis is not None else 1
    has_add = add is not None

    def body(*refs):
        a_ref, b_ref = refs[0], refs[1]
        p = 2
        add_ref = None
        if has_add:
            add_ref = refs[p]
            p += 1
        o_ref = refs[p]
        p += 1
        r = lax.dot_general(a_ref[...].astype(BF16), b_ref[...].astype(BF16), dims, preferred_element_type=F32)
        if k_axis is None:
            if has_add:
                r = r + add_ref[...].astype(F32)
            o_ref[...] = r.astype(o_dtype)
        else:
            acc = refs[p]
            k = pl.program_id(k_axis)

            @pl.when(k == 0)
            def _():
                acc[...] = r

            @pl.when(k > 0)
            def _():
                acc[...] += r

            @pl.when(k == nk - 1)
            def _():
                t = acc[...]
                if has_add:
                    t = t + add_ref[...].astype(F32)
                o_ref[...] = t.astype(o_dtype)

    in_specs = [a_spec, b_spec]
    args = [a, b]
    if has_add:
        in_specs.append(add_spec if add_spec is not None else o_spec)
        args.append(add)
    sem = tuple("arbitrary" if ax == k_axis else "parallel" for ax in range(len(grid)))
    scratch = [pltpu.VMEM(acc_shape, F32)] if k_axis is not None else []
    return _pallas(body, name=name, grid=grid, in_specs=in_specs, out_specs=o_spec,
                   out_shape=jax.ShapeDtypeStruct(o_shape, o_dtype), scratch_shapes=scratch, params=_params(sem))(*args)


def _mm_rows(name, a, b, dims, o_dtype, n_out, *, tn=None, add=None):
    k = a.shape[1]
    tn = n_out if tn is None else tn
    if dims == NN:
        b_spec = pl.BlockSpec((k, tn), lambda n, i: (0, n))
    else:
        b_spec = pl.BlockSpec((tn, k), lambda n, i: (n, 0))
    return _mm(name, a, b, grid=(n_out // tn, T // TM),
               a_spec=pl.BlockSpec((TM, k), lambda n, i: (i, 0)), b_spec=b_spec,
               o_spec=pl.BlockSpec((TM, tn), lambda n, i: (i, n)), o_shape=(T, n_out), o_dtype=o_dtype,
               dims=dims, add=add)


def _mm_wgrad(name, a, b, *, tn=512):
    k, n = a.shape[1], b.shape[1]
    tn = min(tn, n)
    return _mm(name, a, b, grid=(n // tn,),
               a_spec=pl.BlockSpec((T, k), lambda j: (0, 0)), b_spec=pl.BlockSpec((T, tn), lambda j: (0, j)),
               o_spec=pl.BlockSpec((k, tn), lambda j: (0, j)), o_shape=(k, n), o_dtype=BF16, dims=TN)


def _rms_fwd(name, x, g):
    d = x.shape[1]

    def body(x_ref, g_ref, o_ref):
        xv = x_ref[...]
        r = lax.rsqrt(jnp.mean(xv * xv, axis=-1, keepdims=True) + EPS)
        o_ref[...] = ((xv * r) * g_ref[...]).astype(BF16)

    return _pallas(
        body, name=name, grid=(T // TM,),
        in_specs=[pl.BlockSpec((TM, d), lambda i: (i, 0)), pl.BlockSpec((1, d), lambda i: (0, 0))],
        out_specs=pl.BlockSpec((TM, d), lambda i: (i, 0)),
        out_shape=jax.ShapeDtypeStruct((T, d), BF16), params=_params(("parallel",)))(x, g)


def _rms_bwd(name, x, g, dy, dres=None):
    d = x.shape[1]
    has_res = dres is not None

    def body(*refs):
        if has_res:
            x_ref, g_ref, dy_ref, res_ref, dx_ref, dxb_ref, dg_ref = refs
        else:
            x_ref, g_ref, dy_ref, dx_ref, dxb_ref, dg_ref = refs
        xv = x_ref[...]
        r = lax.rsqrt(jnp.mean(xv * xv, axis=-1, keepdims=True) + EPS)
        xn = xv * r
        dyv = dy_ref[...].astype(F32)
        gdy = dyv * g_ref[...]
        dx = r * (gdy - xn * jnp.mean(gdy * xn, axis=-1, keepdims=True))
        if has_res:
            dx = dx + res_ref[...]
        dx_ref[...] = dx
        dxb_ref[...] = dx.astype(BF16)
        part = jnp.sum(dyv * xn, axis=0, keepdims=True)

        @pl.when(pl.program_id(0) == 0)
        def _():
            dg_ref[...] = part

        @pl.when(pl.program_id(0) > 0)
        def _():
            dg_ref[...] += part

    row = pl.BlockSpec((TR, d), lambda i: (i, 0))
    vec = pl.BlockSpec((1, d), lambda i: (0, 0))
    args = [x, g, dy] + ([dres] if has_res else [])
    in_specs = [row, vec, row] + ([row] if has_res else [])
    return _pallas(
        body, name=name, grid=(T // TR,), in_specs=in_specs, out_specs=[row, row, vec],
        out_shape=[jax.ShapeDtypeStruct((T, d), F32), jax.ShapeDtypeStruct((T, d), BF16),
                   jax.ShapeDtypeStruct((1, d), F32)],
        params=_params(("arbitrary",)))(*args)


def _final(h, g, tgt):
    def body(h_ref, g_ref, t_ref, loss_ref, dh_ref, dhb_ref, dg_ref):
        hv = h_ref[...]
        r = lax.rsqrt(jnp.mean(hv * hv, axis=-1, keepdims=True) + EPS)
        xn = hv * r
        gv = g_ref[...]
        err = xn * gv - t_ref[...]
        part_loss = 0.5 * jnp.sum(jnp.mean(err * err, axis=-1, keepdims=True), axis=0, keepdims=True)
        dy = err * (1.0 / D)
        gdy = dy * gv
        dh = r * (gdy - xn * jnp.mean(gdy * xn, axis=-1, keepdims=True))
        dh_ref[...] = dh
        dhb_ref[...] = dh.astype(BF16)
        part = jnp.sum(dy * xn, axis=0, keepdims=True)
        first = pl.program_id(0) == 0

        @pl.when(first)
        def _():
            dg_ref[...] = part
            loss_ref[...] = jnp.broadcast_to(part_loss, (1, 128))

        @pl.when(jnp.logical_not(first))
        def _():
            dg_ref[...] += part
            loss_ref[...] += jnp.broadcast_to(part_loss, (1, 128))

    row = pl.BlockSpec((TR, D), lambda i: (i, 0))
    vec = pl.BlockSpec((1, D), lambda i: (0, 0))
    return _pallas(
        body, name="final_loss", grid=(T // TR,), in_specs=[row, vec, row],
        out_specs=[pl.BlockSpec((1, 128), lambda i: (0, 0)), row, row, vec],
        out_shape=[jax.ShapeDtypeStruct((1, 128), F32), jax.ShapeDtypeStruct((T, D), F32),
                   jax.ShapeDtypeStruct((T, D), BF16), jax.ShapeDtypeStruct((1, D), F32)],
        params=_params(("arbitrary",)))(h, g, tgt)


def _prev_idx(i):
    return jnp.maximum(i * (TR // HALO) - 1, 0)


def _next_idx(i):
    return jnp.minimum((i + 1) * (TR // HALO), T // HALO - 1)


def _causal_taps(ext):
    return pltpu.roll(ext, 2, 0)[HALO:], pltpu.roll(ext, 1, 0)[HALO:], ext[HALO:]


def _anticausal_taps(ext, n):
    rows = ext.shape[0]
    return pltpu.roll(ext, rows - 1, 0)[:n], pltpu.roll(ext, rows - 2, 0)[:n]


def _sc_fwd(z, w):
    def body(b_ref, c_ref, ch_ref, u_ref, uh_ref, w_ref, y_ref):
        i = pl.program_id(0)
        cu = c_ref[...].astype(F32) * u_ref[...].astype(F32)
        cuh = ch_ref[...].astype(F32) * uh_ref[...].astype(F32)
        cuh = jnp.where(i > 0, cuh, 0.0)
        x2, x1, x0 = _causal_taps(jnp.concatenate([cuh, cu], axis=0))
        wv = w_ref[...]
        cv = (x2 * wv[0:1] + x1 * wv[1:2]) + x0 * wv[2:3]
        y_ref[...] = (b_ref[...].astype(F32) * cv).astype(BF16)

    def main(part):
        return pl.BlockSpec((TR, D), lambda i: (i, part))

    def halo(part):
        return pl.BlockSpec((HALO, D), lambda i: (_prev_idx(i), part))

    return _pallas(
        body, name="sc_fwd", grid=(T // TR,),
        in_specs=[main(0), main(1), halo(1), main(2), halo(2), pl.BlockSpec((3, D), lambda i: (0, 0))],
        out_specs=pl.BlockSpec((TR, D), lambda i: (i, 0)),
        out_shape=jax.ShapeDtypeStruct((T, D), BF16), params=_params(("parallel",)))(z, z, z, z, z, w)


def _sc_bwd(z, dy, w):
    last = T // TR - 1

    def body(b_ref, bn_ref, c_ref, ch_ref, u_ref, uh_ref, dy_ref, dyn_ref, w_ref, dz_ref, dw_ref):
        i = pl.program_id(0)
        cv_ = c_ref[...].astype(F32)
        uv = u_ref[...].astype(F32)
        cu = cv_ * uv
        cuh = jnp.where(i > 0, ch_ref[...].astype(F32) * uh_ref[...].astype(F32), 0.0)
        x2, x1, x0 = _causal_taps(jnp.concatenate([cuh, cu], axis=0))
        wv = w_ref[...]
        conv = (x2 * wv[0:1] + x1 * wv[1:2]) + x0 * wv[2:3]
        dyv = dy_ref[...]
        dz_ref[:, 0:D] = (dyv * conv).astype(BF16)
        dconv = dyv * b_ref[...].astype(F32)
        dconv_n = jnp.where(i < last, dyn_ref[...] * bn_ref[...].astype(F32), 0.0)
        n1, n2 = _anticausal_taps(jnp.concatenate([dconv, dconv_n], axis=0), TR)
        dcu = (dconv * wv[2:3] + n1 * wv[1:2]) + n2 * wv[0:1]
        dz_ref[:, D:2 * D] = (dcu * uv).astype(BF16)
        dz_ref[:, 2 * D:3 * D] = (dcu * cv_).astype(BF16)
        part = jnp.concatenate([jnp.sum(dconv * x2, axis=0, keepdims=True),
                                jnp.sum(dconv * x1, axis=0, keepdims=True),
                                jnp.sum(dconv * x0, axis=0, keepdims=True)], axis=0)

        @pl.when(i == 0)
        def _():
            dw_ref[...] = part

        @pl.when(i > 0)
        def _():
            dw_ref[...] += part

    def main(part):
        return pl.BlockSpec((TR, D), lambda i: (i, part))

    def prev(part):
        return pl.BlockSpec((HALO, D), lambda i: (_prev_idx(i), part))

    def nxt(part):
        return pl.BlockSpec((HALO, D), lambda i: (_next_idx(i), part))

    wspec = pl.BlockSpec((3, D), lambda i: (0, 0))
    return _pallas(
        body, name="sc_bwd", grid=(T // TR,),
        in_specs=[main(0), nxt(0), main(1), prev(1), main(2), prev(2), main(0), nxt(0), wspec],
        out_specs=[pl.BlockSpec((TR, 3 * D), lambda i: (i, 0)), wspec],
        out_shape=[jax.ShapeDtypeStruct((T, 3 * D), BF16), jax.ShapeDtypeStruct((3, D), F32)],
        params=_params(("arbitrary",)))(z, z, z, z, z, z, dy, dy, w)


def _sigmoid(x):
    return 1.0 / (1.0 + jnp.exp(-x))


def _ffn_fwd(name, gv, w, b):
    def body(g_ref, gh_ref, v_ref, w_ref, b_ref, a_ref):
        i = pl.program_id(1)
        g = g_ref[...].astype(F32)
        gh = jnp.where(i > 0, gh_ref[...].astype(F32), 0.0)
        x2, x1, x0 = _causal_taps(jnp.concatenate([gh, g], axis=0))
        wv = w_ref[...]
        gc = ((x2 * wv[0:1] + x1 * wv[1:2]) + x0 * wv[2:3]) + b_ref[...]
        a_ref[...] = ((gc * _sigmoid(gc)) * v_ref[...].astype(F32)).astype(BF16)

    blk = (None, TR, FF_BLK)
    return _pallas(
        body, name=name, grid=(N_FF_BLK, T // TR),
        in_specs=[pl.BlockSpec(blk, lambda j, i: (j, i, 0)),
                  pl.BlockSpec((None, HALO, FF_BLK), lambda j, i: (j, _prev_idx(i), 0)),
                  pl.BlockSpec(blk, lambda j, i: (j + N_FF_BLK, i, 0)),
                  pl.BlockSpec((None, 3, FF_BLK), lambda j, i: (j, 0, 0)),
                  pl.BlockSpec((None, 1, FF_BLK), lambda j, i: (j, 0, 0))],
        out_specs=pl.BlockSpec(blk, lambda j, i: (j, i, 0)),
        out_shape=jax.ShapeDtypeStruct((N_FF_BLK, T, FF_BLK), BF16),
        params=_params(("parallel", "parallel")))(gv, gv, gv, w, b)


def _ffn_bwd(name, gv, dact, w, b):
    last = T // TR - 1

    def body(g_ref, gp_ref, gn_ref, v_ref, vn_ref, da_ref, dan_ref, w_ref, b_ref, dg_ref, dv_ref, dw_ref, db_ref):
        i = pl.program_id(1)
        gp = jnp.where(i > 0, gp_ref[...].astype(F32), 0.0)
        ext = jnp.concatenate([gp, g_ref[...].astype(F32), gn_ref[...].astype(F32)], axis=0)
        x2, x1, x0 = _causal_taps(ext)
        wv = w_ref[...]
        gc = ((x2 * wv[0:1] + x1 * wv[1:2]) + x0 * wv[2:3]) + b_ref[...]
        sg = _sigmoid(gc)
        da = jnp.concatenate([da_ref[...].astype(F32), jnp.where(i < last, dan_ref[...].astype(F32), 0.0)], axis=0)
        vv = jnp.concatenate([v_ref[...].astype(F32), vn_ref[...].astype(F32)], axis=0)
        dv_ref[...] = (da[:TR] * (gc[:TR] * sg[:TR])).astype(BF16)
        dgc = (da * vv) * (sg * (1.0 + gc * (1.0 - sg)))
        n1, n2 = _anticausal_taps(dgc, TR)
        d0 = dgc[:TR]
        dg_ref[...] = ((d0 * wv[2:3] + n1 * wv[1:2]) + n2 * wv[0:1]).astype(BF16)
        part_w = jnp.concatenate([jnp.sum(d0 * x2[:TR], axis=0, keepdims=True),
                                  jnp.sum(d0 * x1[:TR], axis=0, keepdims=True),
                                  jnp.sum(d0 * x0[:TR], axis=0, keepdims=True)], axis=0)
        part_b = jnp.sum(d0, axis=0, keepdims=True)

        @pl.when(i == 0)
        def _():
            dw_ref[...] = part_w
            db_ref[...] = part_b

        @pl.when(i > 0)
        def _():
            dw_ref[...] += part_w
            db_ref[...] += part_b

    blk = (None, TR, FF_BLK)
    hblk = (None, HALO, FF_BLK)
    wspec = pl.BlockSpec((None, 3, FF_BLK), lambda j, i: (j, 0, 0))
    bspec = pl.BlockSpec((None, 1, FF_BLK), lambda j, i: (j, 0, 0))
    return _pallas(
        body, name=name, grid=(N_FF_BLK, T // TR),
        in_specs=[pl.BlockSpec(blk, lambda j, i: (j, i, 0)),
                  pl.BlockSpec(hblk, lambda j, i: (j, _prev_idx(i), 0)),
                  pl.BlockSpec(hblk, lambda j, i: (j, _next_idx(i), 0)),
                  pl.BlockSpec(blk, lambda j, i: (j + N_FF_BLK, i, 0)),
                  pl.BlockSpec(hblk, lambda j, i: (j + N_FF_BLK, _next_idx(i), 0)),
                  pl.BlockSpec(blk, lambda j, i: (j, i, 0)),
                  pl.BlockSpec(hblk, lambda j, i: (j, _next_idx(i), 0)),
                  wspec, bspec],
        out_specs=[pl.BlockSpec(blk, lambda j, i: (j, i, 0)), pl.BlockSpec(blk, lambda j, i: (j, i, 0)), wspec, bspec],
        out_shape=[jax.ShapeDtypeStruct((N_FF_BLK, T, FF_BLK), BF16), jax.ShapeDtypeStruct((N_FF_BLK, T, FF_BLK), BF16),
                   jax.ShapeDtypeStruct((N_FF_BLK, 3, FF_BLK), F32), jax.ShapeDtypeStruct((N_FF_BLK, 1, FF_BLK), F32)],
        params=_params(("parallel", "arbitrary")))(gv, gv, gv, gv, gv, dact, dact, w, b)


def _rope_tables(pos, inv_freq):
    half = QK_ROPE // 2

    def body(p_ref, f_ref, c_ref, sa_ref, sb_ref):
        ang = p_ref[...].astype(F32) * f_ref[...]
        lane = lax.broadcasted_iota(jnp.int32, (T, 128), 1)
        c = jnp.cos(ang)
        s = jnp.sin(ang)
        c_ref[...] = jnp.where(lane < 2 * half, c, 0.0)
        sa_ref[...] = jnp.where(lane < half, -s, 0.0)
        sb_ref[...] = jnp.where(jnp.logical_and(lane >= half, lane < 2 * half), s, 0.0)

    return _pallas(
        body, name="rope_tables", in_specs=[VMEM_SPEC] * 2, out_specs=[VMEM_SPEC] * 3,
        out_shape=[jax.ShapeDtypeStruct((T, 128), F32)] * 3,
        params=pltpu.CompilerParams(vmem_limit_bytes=VMEM_LIMIT))(pos, inv_freq)


def _rope(name, x, tables, sign, out_dtype, reduce_groups=False):
    g, _, w = x.shape
    cos, sa, sb = tables

    def body(x_ref, c_ref, sa_ref, sb_ref, o_ref):
        xv = x_ref[...].astype(F32)
        if reduce_groups:
            acc = xv[0]
            for k in range(1, g):
                acc = acc + xv[k]
            xv = acc
        r = xv[:, w - 128:]
        out = r * c_ref[...] + sign * (pltpu.roll(r, 96, 1) * sa_ref[...] + pltpu.roll(r, 32, 1) * sb_ref[...])
        if w > 128:
            o_ref[:, :w - 128] = xv[:, :w - 128].astype(out_dtype)
        o_ref[:, w - 128:] = out.astype(out_dtype)

    tab = pl.BlockSpec((TM, 128), lambda h, i: (i, 0))
    if reduce_groups:
        x_spec = pl.BlockSpec((g, TM, w), lambda h, i: (0, i, 0))
        groups = 1
    else:
        x_spec = pl.BlockSpec((None, TM, w), lambda h, i: (h, i, 0))
        groups = g
    return _pallas(
        body, name=name, grid=(groups, T // TM), in_specs=[x_spec, tab, tab, tab],
        out_specs=pl.BlockSpec((None, TM, w), lambda h, i: (h, i, 0)),
        out_shape=jax.ShapeDtypeStruct((groups, T, w), out_dtype),
        params=_params(("parallel", "parallel")))(x, cos, sa, sb)


SCALE = (QK_NOPE + QK_ROPE) ** -0.5
LOG2E = 1.4426950408889634
SCALE2 = SCALE * LOG2E


def _diag_mask(transposed):
    shift = CHUNK.bit_length() - 1
    a = lax.broadcasted_iota(jnp.int32, (TQ, TQ), 0) >> shift
    b = lax.broadcasted_iota(jnp.int32, (TQ, TQ), 1) >> shift
    return (a <= b) if transposed else (b <= a)


def _keys(kn_ref, kr_ref, off):
    return jnp.concatenate([kn_ref[pl.ds(off, TQ), :], kr_ref[pl.ds(off, TQ), :]], axis=1)


def _attn_fwd(q, kn, kr, v):
    def body(q_ref, kn_ref, kr_ref, v_ref, o_ref, lse_ref):
        i = pl.program_id(1)
        qv = q_ref[...]

        def step(j, carry, masked):
            m, l, acc = carry
            off = pl.multiple_of(j * TQ, TQ)
            s = lax.dot_general(qv, _keys(kn_ref, kr_ref, off), NT, preferred_element_type=F32) * SCALE2
            if masked:
                s = jnp.where(_diag_mask(False), s, NEG_INF)
            m_new = jnp.maximum(m, jnp.max(s, axis=-1, keepdims=True))
            p = jnp.exp2(s - m_new)
            alpha = jnp.exp2(m - m_new)
            l = alpha * l + jnp.sum(p, axis=-1, keepdims=True)
            acc = alpha * acc + lax.dot_general(p.astype(BF16), v_ref[pl.ds(off, TQ), :], NN, preferred_element_type=F32)
            return m_new, l, acc

        init = (jnp.full((TQ, 1), NEG_INF, F32), jnp.zeros((TQ, 1), F32), jnp.zeros((TQ, V_HEAD), F32))
        carry = lax.fori_loop(0, i, lambda j, cr: step(j, cr, False), init)
        m, l, acc = step(i, carry, True)
        o_ref[...] = (acc / l).astype(BF16)
        lse_ref[...] = m + jnp.log(l) * LOG2E

    return _pallas(
        body, name="attn_fwd", grid=(N_HEADS, T // TQ),
        in_specs=[pl.BlockSpec((None, TQ, QK_PAD), lambda h, i: (h, i, 0)),
                  pl.BlockSpec((T, QK_NOPE), lambda h, i: (0, h)),
                  pl.BlockSpec((T, 128), lambda h, i: (0, 0)),
                  pl.BlockSpec((T, V_HEAD), lambda h, i: (0, h))],
        out_specs=[pl.BlockSpec((TQ, V_HEAD), lambda h, i: (i, h)), pl.BlockSpec((None, TQ, 1), lambda h, i: (h, i, 0))],
        out_shape=[jax.ShapeDtypeStruct((T, N_HEADS * V_HEAD), BF16), jax.ShapeDtypeStruct((N_HEADS, T, 1), F32)],
        params=_params(("parallel", "parallel")))(q, kn, kr, v)


def _attn_bwd_dq(q, kn, kr, v, o, do, lse):
    def body(q_ref, kn_ref, kr_ref, v_ref, o_ref, do_ref, lse_ref, dq_ref, dl_ref):
        i = pl.program_id(1)
        qv = q_ref[...]
        dov = do_ref[...]
        lse = lse_ref[...]
        delta = jnp.sum(dov.astype(F32) * o_ref[...].astype(F32), axis=-1, keepdims=True)
        dl_ref[...] = delta

        def step(j, dq, masked):
            off = pl.multiple_of(j * TQ, TQ)
            kk = _keys(kn_ref, kr_ref, off)
            s = lax.dot_general(qv, kk, NT, preferred_element_type=F32) * SCALE2
            if masked:
                s = jnp.where(_diag_mask(False), s, NEG_INF)
            p = jnp.exp2(s - lse)
            dp = lax.dot_general(dov, v_ref[pl.ds(off, TQ), :], NT, preferred_element_type=F32)
            ds = (p * (dp - delta)) * SCALE
            return dq + lax.dot_general(ds.astype(BF16), kk, NN, preferred_element_type=F32)

        dq = lax.fori_loop(0, i, lambda j, acc: step(j, acc, False), jnp.zeros((TQ, QK_PAD), F32))
        dq_ref[...] = step(i, dq, True)

    col = pl.BlockSpec((None, TQ, 1), lambda h, i: (h, i, 0))
    head = pl.BlockSpec((TQ, V_HEAD), lambda h, i: (i, h))
    return _pallas(
        body, name="attn_bwd_dq", grid=(N_HEADS, T // TQ),
        in_specs=[pl.BlockSpec((None, TQ, QK_PAD), lambda h, i: (h, i, 0)),
                  pl.BlockSpec((T, QK_NOPE), lambda h, i: (0, h)),
                  pl.BlockSpec((T, 128), lambda h, i: (0, 0)),
                  pl.BlockSpec((T, V_HEAD), lambda h, i: (0, h)),
                  head, head, col],
        out_specs=[pl.BlockSpec((None, TQ, QK_PAD), lambda h, i: (h, i, 0)), col],
        out_shape=[jax.ShapeDtypeStruct((N_HEADS, T, QK_PAD), F32), jax.ShapeDtypeStruct((N_HEADS, T, 1), F32)],
        params=_params(("parallel", "parallel")))(q, kn, kr, v, o, do, lse)


def _attn_bwd_dkv(q, kn, kr, v, do, lse_row, delta_row):
    nq = T // TQ

    def body(q_ref, kn_ref, kr_ref, v_ref, do_ref, lse_ref, dl_ref, dkn_ref, dkr_ref, dv_ref):
        j = pl.program_id(1)
        kk = jnp.concatenate([kn_ref[...], kr_ref[...]], axis=1)
        vv = v_ref[...]

        def step(i, carry, masked):
            dk, dv = carry
            off = pl.multiple_of(i * TQ, TQ)
            qi = q_ref[pl.ds(off, TQ), :]
            doi = do_ref[pl.ds(off, TQ), :]
            st = lax.dot_general(kk, qi, NT, preferred_element_type=F32) * SCALE2
            if masked:
                st = jnp.where(_diag_mask(True), st, NEG_INF)
            pt = jnp.exp2(st - lse_ref[:, pl.ds(off, TQ)])
            dv = dv + lax.dot_general(pt.astype(BF16), doi, NN, preferred_element_type=F32)
            dpt = lax.dot_general(vv, doi, NT, preferred_element_type=F32)
            dst = (pt * (dpt - dl_ref[:, pl.ds(off, TQ)])) * SCALE
            dk = dk + lax.dot_general(dst.astype(BF16), qi, NN, preferred_element_type=F32)
            return dk, dv

        carry = step(j, (jnp.zeros((TQ, QK_PAD), F32), jnp.zeros((TQ, V_HEAD), F32)), True)
        dk, dv = lax.fori_loop(j + 1, nq, lambda i, cr: step(i, cr, False), carry)
        dkn_ref[...] = dk[:, :QK_NOPE].astype(BF16)
        dkr_ref[...] = dk[:, QK_NOPE:]
        dv_ref[...] = dv.astype(BF16)

    row = pl.BlockSpec((None, 1, T), lambda h, j: (h, 0, 0))
    head = pl.BlockSpec((TQ, 128), lambda h, j: (j, h))
    return _pallas(
        body, name="attn_bwd_dkv", grid=(N_HEADS, nq),
        in_specs=[pl.BlockSpec((None, T, QK_PAD), lambda h, j: (h, 0, 0)),
                  head, pl.BlockSpec((TQ, 128), lambda h, j: (j, 0)), head,
                  pl.BlockSpec((T, V_HEAD), lambda h, j: (0, h)), row, row],
        out_specs=[head, pl.BlockSpec((None, TQ, 128), lambda h, j: (h, j, 0)), head],
        out_shape=[jax.ShapeDtypeStruct((T, N_HEADS * QK_NOPE), BF16), jax.ShapeDtypeStruct((N_HEADS, T, 128), F32),
                   jax.ShapeDtypeStruct((T, N_HEADS * V_HEAD), BF16)],
        params=_params(("parallel", "parallel")))(q, kn, kr, v, do, lse_row, delta_row)


def _ffn_layer_fwd(tag, h, gain, ex):
    hf = _rms_fwd(f"{tag}_norm", h, gain)
    gv = _mm(f"{tag}_up", hf, ex.need(f"ffn_w_up{tag[1]}", hf), grid=(N_DEV, T // TM),
             a_spec=pl.BlockSpec((TM, D), lambda j, i: (i, 0)),
             b_spec=pl.BlockSpec((None, None, D, FF_BLK), lambda j, i: (0, j, 0, 0)),
             o_spec=pl.BlockSpec((None, TM, FF_BLK), lambda j, i: (j, i, 0)),
             o_shape=(N_DEV, T, FF_BLK), o_dtype=BF16, dims=NN)
    act = _ffn_fwd(f"{tag}_act", gv, ex.need(f"ffn_cw{tag[1]}", gv), ex.need(f"ffn_cb{tag[1]}", gv))
    ex.at(f"{tag}_act", act)
    tn = 512
    out = _mm(f"{tag}_down", act, ex.need(f"ffn_w_down{tag[1]}", act), grid=(D // tn, T // TM, N_FF_BLK),
              a_spec=pl.BlockSpec((None, TM, FF_BLK), lambda n, i, k: (k, i, 0)),
              b_spec=pl.BlockSpec((None, None, FF_BLK, tn), lambda n, i, k: (0, k, 0, n)),
              o_spec=pl.BlockSpec((TM, tn), lambda n, i, k: (i, n)), o_shape=(T, D), o_dtype=F32,
              dims=NN, k_axis=2, acc_shape=(TM, tn), add=h, add_spec=pl.BlockSpec((TM, tn), lambda n, i, k: (i, n)))
    return out, (hf, gv, act)


def _ffn_layer_bwd(tag, h, gain, ex, saved, dh, dh_bf):
    hf, gv, act = saved
    layer = tag[1]
    w_up, w_down4 = ex.need(f"ffn_w_up{layer}", dh_bf), ex.need(f"ffn_w_down{layer}", dh_bf)
    dact = _mm(f"{tag}_dact", dh_bf, w_down4, grid=(N_FF_BLK, T // TM),
               a_spec=pl.BlockSpec((TM, D), lambda j, i: (i, 0)),
               b_spec=pl.BlockSpec((None, None, FF_BLK, D), lambda j, i: (0, j, 0, 0)),
               o_spec=pl.BlockSpec((None, TM, FF_BLK), lambda j, i: (j, i, 0)),
               o_shape=(N_FF_BLK, T, FF_BLK), o_dtype=BF16, dims=NT)
    tn = 512
    g_down = _mm(f"{tag}_gdown", act, dh_bf, grid=(N_FF_BLK, D // tn),
                 a_spec=pl.BlockSpec((None, T, FF_BLK), lambda j, n: (j, 0, 0)),
                 b_spec=pl.BlockSpec((T, tn), lambda j, n: (0, n)),
                 o_spec=pl.BlockSpec((FF_BLK, tn), lambda j, n: (j, n)),
                 o_shape=(D_FF, D), o_dtype=BF16, dims=TN)
    dg, dv, dcw, dcb = _ffn_bwd(f"{tag}_dact_ew", gv, dact, ex.need(f"ffn_cw{layer}", dact), ex.need(f"ffn_cb{layer}", dact))
    ex.at(f"{tag}_dact_ew", dg)
    dhf = None
    g_up = []
    for half, (name, dpart) in enumerate((("g", dg), ("v", dv))):
        dhf = _mm(f"{tag}_dhf_{name}", dpart, w_up, grid=(T // TM, N_FF_BLK),
                  a_spec=pl.BlockSpec((None, TM, FF_BLK), lambda i, k: (k, i, 0)),
                  b_spec=pl.BlockSpec((None, None, D, FF_BLK), lambda i, k, half=half: (0, k + half * N_FF_BLK, 0, 0)),
                  o_spec=pl.BlockSpec((TM, D), lambda i, k: (i, 0)), o_shape=(T, D), o_dtype=F32,
                  dims=NT, k_axis=1, acc_shape=(TM, D), add=dhf)
        g_up.append(_mm(f"{tag}_gup_{name}", hf, dpart, grid=(N_FF_BLK,),
                        a_spec=pl.BlockSpec((T, D), lambda j: (0, 0)),
                        b_spec=pl.BlockSpec((None, T, FF_BLK), lambda j: (j, 0, 0)),
                        o_spec=pl.BlockSpec((None, D, FF_BLK), lambda j: (j, 0, 0)),
                        o_shape=(N_FF_BLK, D, FF_BLK), o_dtype=BF16, dims=TN))
    ex.grad("ffn_w_up", int(layer), jnp.concatenate(g_up, axis=0).reshape(1, N_DEV, D, FF_BLK))
    ex.grad("ffn_w_down", int(layer), g_down.reshape(1, N_DEV, D_FF // N_DEV, D))
    dh_in, dh_in_bf, dgain = _rms_bwd(f"{tag}_dnorm", h, gain, dhf, dres=dh)
    return dh_in, dh_in_bf, dgain, dcw, dcb


def _local_step(x, pos, tgt, rep, ex):
    attn_norm, ffn_norm, final_norm = rep["attn_norm"], rep["ffn_norm"], rep["final_norm"]
    half = QK_ROPE // 2
    inv = 1.0 / (ROPE_THETA ** (jnp.arange(half, dtype=F32) / half))
    inv_freq = jnp.concatenate([inv, inv, jnp.zeros((128 - 2 * half,), F32)]).reshape(1, 128)
    tables = _rope_tables(pos, inv_freq)

    hn0 = _rms_fwd("l0_norm", x, attn_norm[0:1])
    z = _mm_rows("l0_in", hn0, ex.need("sc_w_in", hn0), NN, BF16, 3 * D, tn=512)
    ex.at("l0_in", z)
    y = _sc_fwd(z, ex.need("sc_conv_w", z))
    h1 = _mm_rows("l0_out", y, ex.need("sc_w_out", y), NN, F32, D, tn=512, add=x)
    h2, ffn0 = _ffn_layer_fwd("f0", h1, ffn_norm[0:1], ex)

    hk = _rms_fwd("kv_norm", h2, rep["kv_in_norm"])
    ckv_raw = _mm_rows("kv_down", hk, ex.need("w_dkv", hk), NN, F32, KV_LORA)
    kr_raw = _mm_rows("kv_rope", hk, ex.need("w_kr", hk), NN, F32, 128)
    ckv = _rms_fwd("kv_lnorm", ckv_raw, rep["kv_latent_norm"])
    kn = _mm_rows("kv_uk", ckv, ex.need("w_uk", ckv), NN, BF16, N_HEADS * QK_NOPE)
    vv = _mm_rows("kv_uv", ckv, ex.need("w_uv", ckv), NN, BF16, N_HEADS * V_HEAD)
    kr = _rope("k_rope", kr_raw.reshape(1, T, 128), tables, 1.0, BF16).reshape(T, 128)

    hn1 = _rms_fwd("l1_norm", h2, attn_norm[1:2])
    cq_raw = _mm_rows("q_down", hn1, ex.need("w_dq", hn1), NN, F32, Q_LORA)
    cq = _rms_fwd("q_lnorm", cq_raw, rep["q_latent_norm"])
    w_uq = ex.need("w_uq", cq)
    q_raw = _mm("q_up", cq, w_uq, grid=(N_HEADS, T // TM),
                a_spec=pl.BlockSpec((TM, Q_LORA), lambda h, i: (i, 0)),
                b_spec=pl.BlockSpec((None, Q_LORA, QK_PAD), lambda h, i: (h, 0, 0)),
                o_spec=pl.BlockSpec((None, TM, QK_PAD), lambda h, i: (h, i, 0)),
                o_shape=(N_HEADS, T, QK_PAD), o_dtype=F32, dims=NN)
    q = _rope("q_rope", q_raw, tables, 1.0, BF16)
    o, lse = _attn_fwd(q, kn, kr, vv)
    w_o = ex.need("w_o", o)
    h3 = _mm_rows("attn_out", o, w_o, NN, F32, D, tn=512, add=h2)
    h4, ffn1 = _ffn_layer_fwd("f1", h3, ffn_norm[1:2], ex)

    loss, dh4, dh4_bf, d_final = _final(h4, final_norm.reshape(1, D), tgt)

    dh3, dh3_bf, d_fn1, dcw1, dcb1 = _ffn_layer_bwd("f1", h3, ffn_norm[1:2], ex, ffn1, dh4, dh4_bf)
    ex.at("f1_bwd", dh3)

    do = _mm_rows("d_attn_out", dh3_bf, w_o, NT, BF16, N_HEADS * V_HEAD)
    ex.grad("w_o", None, _mm_wgrad("g_w_o", o, dh3_bf).reshape(1, N_DEV, D // N_DEV, D))
    dq, delta = _attn_bwd_dq(q, kn, kr, vv, o, do, lse)
    ex.at("attn_dq", dq)
    dkn, dkr, dvv = _attn_bwd_dkv(q, kn, kr, vv, do, lse.reshape(N_HEADS, 1, T), delta.reshape(N_HEADS, 1, T))
    dq_pre = _rope("dq_rope", dq, tables, -1.0, BF16)
    dcq = _mm("d_q_up", dq_pre, w_uq, grid=(T // TM, N_HEADS),
              a_spec=pl.BlockSpec((None, TM, QK_PAD), lambda i, h: (h, i, 0)),
              b_spec=pl.BlockSpec((None, Q_LORA, QK_PAD), lambda i, h: (h, 0, 0)),
              o_spec=pl.BlockSpec((TM, Q_LORA), lambda i, h: (i, 0)), o_shape=(T, Q_LORA), o_dtype=F32,
              dims=NT, k_axis=1, acc_shape=(TM, Q_LORA))
    g_uq = _mm("g_w_uq", cq, dq_pre, grid=(N_HEADS,),
               a_spec=pl.BlockSpec((T, Q_LORA), lambda h: (0, 0)),
               b_spec=pl.BlockSpec((None, T, QK_PAD), lambda h: (h, 0, 0)),
               o_spec=pl.BlockSpec((None, Q_LORA, QK_PAD), lambda h: (h, 0, 0)),
               o_shape=(N_HEADS, Q_LORA, QK_PAD), o_dtype=BF16, dims=TN)
    ex.grad("w_uq", None, g_uq[:, :, :QK_NOPE + QK_ROPE].reshape(1, N_DEV, Q_LORA, QK_NOPE + QK_ROPE))
    _, dcq_raw_bf, d_qln = _rms_bwd("d_q_lnorm", cq_raw, rep["q_latent_norm"], dcq)
    dhn1 = _mm_rows("d_q_down", dcq_raw_bf, ex.need("w_dq", dcq_raw_bf), NT, F32, D)
    ex.grad("w_dq", None, _mm_wgrad("g_w_dq", hn1, dcq_raw_bf).reshape(1, N_DEV, D // N_DEV, Q_LORA))
    dh2_a, _, d_an1 = _rms_bwd("d_l1_norm", h2, attn_norm[1:2], dhn1, dres=dh3)

    dckv = _mm_rows("d_kv_uk", dkn, ex.need("w_uk", dkn), NT, F32, KV_LORA)
    dckv = _mm_rows("d_kv_uv", dvv, ex.need("w_uv", dvv), NT, F32, KV_LORA, add=dckv)
    ex.grad("w_uk", None, _mm_wgrad("g_w_uk", ckv, dkn))
    ex.grad("w_uv", None, _mm_wgrad("g_w_uv", ckv, dvv))
    _, dckv_raw_bf, d_kvln = _rms_bwd("d_kv_lnorm", ckv_raw, rep["kv_latent_norm"], dckv)
    dkr_raw_bf = _rope("dk_rope", dkr, tables, -1.0, BF16, reduce_groups=True).reshape(T, 128)
    dhk = _mm_rows("d_kv_down", dckv_raw_bf, ex.need("w_dkv", dckv_raw_bf), NT, F32, D)
    dhk = _mm_rows("d_kv_rope", dkr_raw_bf, ex.need("w_kr", dkr_raw_bf), NT, F32, D, add=dhk)
    ex.grad("w_dkv", None, _mm_wgrad("g_w_dkv", hk, dckv_raw_bf).reshape(1, N_DEV, D // N_DEV, KV_LORA))
    ex.grad("w_kr", None, _mm_wgrad("g_w_kr", hk, dkr_raw_bf)[:, :QK_ROPE].reshape(1, N_DEV, D // N_DEV, QK_ROPE))
    dh2, dh2_bf, d_kvin = _rms_bwd("d_kv_norm", h2, rep["kv_in_norm"], dhk, dres=dh2_a)
    ex.at("kv_bwd", dh2)

    dh1, dh1_bf, d_fn0, dcw0, dcb0 = _ffn_layer_bwd("f0", h1, ffn_norm[0:1], ex, ffn0, dh2, dh2_bf)
    ex.at("f0_bwd", dh1)

    dy = _mm_rows("d_l0_out", dh1_bf, ex.need("sc_w_out", dh1_bf), NT, F32, D)
    ex.grad("sc_w_out", None, _mm_wgrad("g_sc_w_out", y, dh1_bf).reshape(1, N_DEV, D // N_DEV, D))
    dz, d_scw = _sc_bwd(z, dy, ex.need("sc_conv_w", dy))
    ex.at("sc_bwd", dz)
    dhn0 = _mm_rows("d_l0_in", dz, ex.need("sc_w_in", dz), NT, F32, D)
    ex.grad("sc_w_in", None, _mm_wgrad("g_sc_w_in", hn0, dz))
    grad_x, _, d_an0 = _rms_bwd("d_l0_norm", x, attn_norm[0:1], dhn0, dres=dh1)

    small = {
        "attn_norm": jnp.concatenate([d_an0, d_an1], axis=0),
        "ffn_norm": jnp.concatenate([d_fn0, d_fn1], axis=0),
        "final_norm": d_final.reshape(D),
        "kv_in_norm": d_kvin.reshape(D),
        "kv_latent_norm": d_kvln.reshape(KV_LORA),
        "q_latent_norm": d_qln,
        "ffn_conv_b": jnp.stack([dcb0, dcb1]).transpose(0, 2, 1, 3).reshape(2, D_FF),
        "sc_conv_w": d_scw,
        "ffn_conv_w": jnp.stack([dcw0, dcw1]).transpose(0, 2, 1, 3).reshape(2, 3, D_FF),
    }
    return loss, grad_x, small


def _place():
    return lax.axis_index("x"), lax.axis_index("y"), lax.axis_index("c")


def _peers():
    x, y, c = _place()
    return (x, y, 1 - c), [(1 - x, y), (x, 1 - y), (1 - x, 1 - y)]


def _window(ref, kind, dev):
    if kind == "blocked":
        return ref.at[:, dev]
    width = ref.shape[-1] // N_DEV
    return ref.at[:, pl.ds(pl.multiple_of(dev * width, 128), width)]


def _all_gather(name, items):
    n = len(items)
    out_shapes = []
    for shard, kind in items:
        if kind == "blocked":
            shape = (shard.shape[0], N_DEV) + shard.shape[1:]
        else:
            shape = (shard.shape[0], N_DEV * shard.shape[1])
        out_shapes.append(jax.ShapeDtypeStruct(shape, shard.dtype))

    def body(*refs):
        srcs, outs = refs[:n], refs[n:2 * n]
        send_sems, recv_sems, local_sems = refs[2 * n:]
        x, y, c = _place()
        me = 4 * x + 2 * y + c
        sibling, chips = _peers()

        def num(px, py, pc):
            return 4 * px + 2 * py + pc

        def copy(t, k, dev, to, from_src):
            kind = items[t][1]
            dst = _window(outs[t], kind, dev)
            return pltpu.make_async_remote_copy(
                src_ref=srcs[t] if from_src else dst, dst_ref=dst,
                send_sem=send_sems.at[t, k], recv_sem=recv_sems.at[t, k], device_id=to, device_id_type=MESH)

        mine = [pltpu.make_async_copy(srcs[t], _window(outs[t], items[t][1], me), local_sems.at[t]) for t in range(n)]
        for cp in mine:
            cp.start()
        first = []
        for t in range(n):
            first.append(copy(t, 0, me, sibling, True))
            for j, chip in enumerate(chips):
                first.append(copy(t, 1 + j, me, (*chip, c), True))
        for cp in first:
            cp.start()
        passed = []
        for j, chip in enumerate(chips):
            for t in range(n):
                copy(t, 1 + j, num(*chip, c), (x, y, c), False).wait_recv()
                fwd = copy(t, 4 + j, num(*chip, c), sibling, False)
                fwd.start()
                passed.append(fwd)
        for t in range(n):
            copy(t, 0, num(x, y, 1 - c), (x, y, c), False).wait_recv()
            for j, chip in enumerate(chips):
                copy(t, 4 + j, num(*chip, 1 - c), (x, y, c), False).wait_recv()
        for cp in first + passed:
            cp.wait_send()
        for cp in mine:
            cp.wait()

    return _pallas(
        body, name=name, in_specs=[ANY_SPEC] * n, out_specs=[ANY_SPEC] * n, out_shape=out_shapes,
        scratch_shapes=[pltpu.SemaphoreType.DMA((n, 7)), pltpu.SemaphoreType.DMA((n, 7)), pltpu.SemaphoreType.DMA((n,))],
    )(*[s for s, _ in items])


HBM_SPEC = pl.BlockSpec(memory_space=pltpu.HBM)
SEM_SPEC = pl.BlockSpec(memory_space=pltpu.SEMAPHORE)
EFFECT = pltpu.SideEffectType.DATAFLOW_SIDE_EFFECTING
TOKEN = jax.ShapeDtypeStruct((8, 128), F32)


def _hbm(a):
    return pltpu.with_memory_space_constraint(a, pltpu.HBM)


def _copies_start(name, srcs, lands, ncopy, plan):
    ns, nl = len(srcs), len(lands)

    def body(*refs):
        send, recv, token = refs[ns + nl], refs[ns + nl + 1], refs[-1]
        copies = plan(refs[:ns], refs[ns:ns + nl])
        assert len(copies) == ncopy
        for k, (sent, dst, to, _) in enumerate(copies):
            pltpu.make_async_remote_copy(src_ref=sent, dst_ref=dst, send_sem=send.at[k], recv_sem=recv.at[k],
                                         device_id=to, device_id_type=MESH).start()
        token[...] = jnp.zeros_like(token)

    arrays = list(srcs) + list(lands)
    outs = pl.pallas_call(
        body, name=name, in_specs=[HBM_SPEC] * (ns + nl),
        out_specs=[SEM_SPEC] * 2 + [HBM_SPEC] * (ns + nl) + [VMEM_SPEC],
        out_shape=[pltpu.SemaphoreType.DMA((ncopy,))] * 2 + [pltpu.HBM(a.shape, a.dtype) for a in arrays] + [TOKEN],
        input_output_aliases={i: 2 + i for i in range(ns + nl)},
        compiler_params=pltpu.CompilerParams(has_side_effects=EFFECT))(*[_hbm(a) for a in arrays])
    _Chain.last = outs[-1]
    return outs[0], outs[1], list(outs[2:2 + ns]), list(outs[2 + ns:-1])


def _copies_wait(name, started, ncopy, plan):
    send, recv, srcs, lands = started
    ns, nl = len(srcs), len(lands)

    def body(*refs):
        send_ref, recv_ref, token = refs[ns + nl], refs[ns + nl + 1], refs[-1]
        copies = plan(refs[:ns], refs[ns:ns + nl])
        assert len(copies) == ncopy
        for k, (sent, _, to, landed) in enumerate(copies):
            cp = pltpu.make_async_remote_copy(src_ref=sent, dst_ref=landed, send_sem=send_ref.at[k],
                                              recv_sem=recv_ref.at[k], device_id=to, device_id_type=MESH)
            cp.wait_send()
            cp.wait_recv()
        token[...] = jnp.zeros_like(token)

    arrays = list(srcs) + list(lands)
    outs = pl.pallas_call(
        body, name=name, in_specs=[HBM_SPEC] * (ns + nl) + [SEM_SPEC] * 2 + [ANY_SPEC],
        out_specs=[HBM_SPEC] * (ns + nl) + [VMEM_SPEC], out_shape=[pltpu.HBM(a.shape, a.dtype) for a in arrays] + [TOKEN],
        input_output_aliases={i: i for i in range(ns + nl)},
        compiler_params=pltpu.CompilerParams(has_side_effects=EFFECT))(*arrays, send, recv, _Chain.last)
    _Chain.last = outs[-1]
    return list(outs[:ns]), list(outs[ns:-1])


def _plan_gather_chips(kinds):
    def plan(srcs, lands):
        x, y, c = _place()
        sibling, chips = _peers()
        out = []
        for t, kind in enumerate(kinds):
            mine = _window(lands[t], kind, 4 * x + 2 * y + c)
            out.append((srcs[t], mine, sibling, _window(lands[t], kind, 4 * x + 2 * y + 1 - c)))
            for px, py in chips:
                out.append((srcs[t], mine, (px, py, c), _window(lands[t], kind, 4 * px + 2 * py + c)))
        return out
    return plan, 4 * len(kinds)


def _plan_gather_sibling(kinds):
    def plan(srcs, lands):
        _, _, c = _place()
        sibling, chips = _peers()
        out = []
        for t, kind in enumerate(kinds):
            for px, py in chips:
                w = _window(lands[t], kind, 4 * px + 2 * py + c)
                out.append((w, w, sibling, _window(lands[t], kind, 4 * px + 2 * py + 1 - c)))
        return out
    return plan, 3 * len(kinds)


def _plan_scatter_sibling(kinds):
    def plan(srcs, lands):
        _, _, c = _place()
        sibling, _ = _peers()
        out = []
        for t, kind in enumerate(kinds):
            for k in range(N_CHIP):
                out.append((_window(srcs[t], kind, 2 * k + 1 - c), lands[t].at[k], sibling, lands[t].at[k]))
        return out
    return plan, N_CHIP * len(kinds)


def _plan_scatter_chips(n):
    def plan(srcs, lands):
        x, y, c = _place()
        _, chips = _peers()
        out = []
        for t in range(n):
            for px, py in chips:
                out.append((srcs[t].at[2 * px + py], lands[t].at[2 * x + y], (px, py, c), lands[t].at[2 * px + py]))
        return out
    return plan, 3 * n


def _landing(shard, kind, me):
    if kind == "blocked":
        land = lax.empty((shard.shape[0], N_DEV) + shard.shape[1:], shard.dtype)
        return lax.dynamic_update_slice(land, shard[:, None], (0, me) + (0,) * (shard.ndim - 1))
    land = lax.empty((shard.shape[0], N_DEV * shard.shape[1]), shard.dtype)
    return lax.dynamic_update_slice(land, shard, (0, me * shard.shape[1]))


def _row_tile(rows):
    for tr in (512, 384, 352, 256, 128, 64, 32, 16):
        if rows % tr == 0:
            return tr
    raise ValueError(rows)


def _chip_sum(name, gr, kind, recv, c):
    if kind == "blocked":
        nl, _, r, w = gr.shape
        rows = nl * r
        tr = _row_tile(r)
        per = r // tr
        g_spec = pl.BlockSpec((None, None, tr, w), lambda k, i, cref: (i // per, 2 * k + cref[0], i % per, 0))
    else:
        rows, w = gr.shape[0], gr.shape[1] // N_DEV
        tr = _row_tile(rows)
        g_spec = pl.BlockSpec((tr, w), lambda k, i, cref: (i, 2 * k + cref[0]))
    recv = recv.reshape(N_CHIP, rows, w)

    def body(c_ref, g_ref, r_ref, o_ref):
        del c_ref
        o_ref[...] = (g_ref[...].astype(F32) + r_ref[...].astype(F32)).astype(BF16)

    blk = pl.BlockSpec((None, tr, w), lambda k, i, cref: (k, i, 0))
    return _pallas(
        body, name=name, n_prefetch=1, grid=(N_CHIP, rows // tr), in_specs=[g_spec, blk], out_specs=blk,
        out_shape=jax.ShapeDtypeStruct((N_CHIP, rows, w), BF16),
        params=_params(("parallel", "parallel")))(c, gr, recv)


def _adamw_math(g, wv, mv, vv):
    m = ADAM_B1 * mv + (1.0 - ADAM_B1) * g
    v = ADAM_B2 * vv + (1.0 - ADAM_B2) * (g * g)
    m_hat = m / (1.0 - ADAM_B1 ** ADAM_STEP)
    v_hat = v / (1.0 - ADAM_B2 ** ADAM_STEP)
    delta = -ADAM_LR * (m_hat / (jnp.sqrt(v_hat) + ADAM_EPS) + ADAM_WD * wv)
    return delta, m, v


def _adamw_sharded(name, own, recv, chip_ids, w3, m3, v3, layer, prev):
    nl, rows, w = w3.shape
    tr = _row_tile(rows)
    has_prev = prev is not None

    def body(*refs):
        own_ref, r1_ref, r2_ref, r3_ref, w_ref, m_ref, v_ref = refs[1:8]
        g_ref, d_ref, nm_ref, nv_ref = refs[-4:]
        g = ((own_ref[...].astype(F32) + r1_ref[...].astype(F32)) + r2_ref[...].astype(F32)) + r3_ref[...].astype(F32)
        g_ref[...] = g
        d_ref[...], nm_ref[...], nv_ref[...] = _adamw_math(g, w_ref[...], m_ref[...], v_ref[...])

    def pick(slot):
        return pl.BlockSpec((None, tr, w), lambda i, ids: (ids[slot], i, 0))

    slab = pl.BlockSpec((None, tr, w), lambda i, ids: (layer, i, 0))
    in_specs = [pick(0), pick(1), pick(2), pick(3), slab, slab, slab]
    args = [chip_ids, own, recv, recv, recv, w3, m3, v3]
    aliases = {}
    if has_prev:
        in_specs += [ANY_SPEC] * 4
        aliases = {len(args) + k: k for k in range(4)}
        args += list(prev)
    return _pallas(
        body, name=name, n_prefetch=1, grid=(rows // tr,), in_specs=in_specs, out_specs=[slab] * 4,
        out_shape=[jax.ShapeDtypeStruct((nl, rows, w), F32)] * 4, aliases=aliases,
        params=_params(("parallel",)))(*args)


def _adamw_small(parts, wv, mv, vv):
    r = wv.shape[0]

    def body(p_ref, w_ref, m_ref, v_ref, g_ref, d_ref, nm_ref, nv_ref):
        g = p_ref[0]
        for k in range(1, N_DEV):
            g = g + p_ref[k]
        g_ref[...] = g
        d_ref[...], nm_ref[...], nv_ref[...] = _adamw_math(g, w_ref[...], m_ref[...], v_ref[...])

    return _pallas(
        body, name="adamw_small", in_specs=[VMEM_SPEC] * 4, out_specs=[VMEM_SPEC] * 4,
        out_shape=[jax.ShapeDtypeStruct((r, 128), F32)] * 4,
        params=pltpu.CompilerParams(vmem_limit_bytes=VMEM_LIMIT))(parts, wv, mv, vv)


KIND = {"sc_w_in": "cols", "sc_w_out": "blocked", "w_dkv": "blocked", "w_kr": "blocked", "w_uk": "cols", "w_uv": "cols",
        "w_dq": "blocked", "w_uq": "blocked", "w_o": "blocked", "ffn_w_up": "blocked", "ffn_w_down": "blocked",
        "conv": "blocked"}
GATHER_GROUPS = (("mixer", ("sc_w_in", "sc_w_out", "conv")),
                 ("ffn0", ("ffn_w_up0", "ffn_w_down0")),
                 ("layer1", ("w_dkv", "w_kr", "w_uk", "w_uv", "w_dq", "w_uq", "w_o", "ffn_w_up1", "ffn_w_down1")))
SCATTER_GROUPS = (("ffn1", (("ffn_w_up", 1), ("ffn_w_down", 1))),
                  ("attn", (("w_o", None), ("w_uq", None), ("w_dq", None), ("w_uk", None), ("w_uv", None),
                            ("w_dkv", None), ("w_kr", None))),
                  ("ffn0", (("ffn_w_up", 0), ("ffn_w_down", 0))),
                  ("mixer", (("sc_w_out", None), ("sc_w_in", None))))
SCHEDULE = {
    "l0_in": (("gather_forward", "ffn0"),),
    "f0_act": (("gather_forward", "layer1"),),
    "f1_bwd": (("scatter_sibling", "ffn1"),),
    "attn_dq": (("scatter_chips", "ffn1"),),
    "kv_bwd": (("scatter_sibling", "attn"), ("scatter_done", "ffn1")),
    "f0_dact_ew": (("scatter_chips", "attn"),),
    "f0_bwd": (("scatter_sibling", "ffn0"), ("scatter_done", "attn")),
    "sc_bwd": (("scatter_chips", "ffn0"),),
}
FINISH = (("scatter_sibling", "mixer"), ("scatter_chips", "mixer"), ("scatter_done", "ffn0"), ("scatter_done", "mixer"))
STAGES = {"gather_forward": 2, "gather_done": 3, "scatter_sibling": 1, "scatter_chips": 2, "scatter_done": 3}
SMALL_W_ROWS = 24
SMALL_G_ROWS = 256


def _pack(arrays, rows):
    flat = jnp.concatenate([a.reshape(-1).astype(F32) for a in arrays])
    return jnp.pad(flat, (0, rows * 128 - flat.shape[0])).reshape(rows, 128)


def _unpack(packed, shapes):
    flat = packed.reshape(-1)
    out, off = [], 0
    for shape in shapes:
        size = 1
        for s in shape:
            size *= s
        out.append(flat[off:off + size].reshape(shape))
        off += size
    return out


def _base(name):
    if name.startswith("ffn_w_") and name[-1] in "01":
        return name[:-1], int(name[-1])
    return name, None


class _Exchange:
    def __init__(self, wts, mom, var, ffn_conv_b):
        self.wts, self.mom, self.var = wts, mom, var
        x, y, c = _place()
        self.me = 4 * x + 2 * y + c
        self.c_arr = jnp.reshape(c, (1,)).astype(jnp.int32)
        chip = 2 * x + y
        self.chip_ids = jnp.stack([chip, chip ^ 1, chip ^ 2, chip ^ 3]).astype(jnp.int32)
        self.ready = {"ffn_cb0": ffn_conv_b.reshape(2, N_FF_BLK, 1, FF_BLK)[0],
                      "ffn_cb1": ffn_conv_b.reshape(2, N_FF_BLK, 1, FF_BLK)[1]}
        self.gathers, self.group_of = {}, {}
        self.grads, self.scatters, self.results = {}, {}, {}
        for gname, names in GATHER_GROUPS:
            shards = [self._shard(nm) for nm in names]
            kinds = [KIND[_base(nm)[0]] for nm in names]
            lands = [_landing(s, kind, self.me) for s, kind in zip(shards, kinds)]
            plan, ncopy = _plan_gather_chips(kinds)
            self.gathers[gname] = dict(stage=1, names=names, kinds=kinds,
                                       flight=_copies_start(f"ag_{gname}_chips", shards, lands, ncopy, plan))
            for nm in names:
                self.group_of[nm] = gname
        for nm in ("sc_conv_w", "ffn_cw0", "ffn_cw1"):
            self.group_of[nm] = "mixer"

    def _shard(self, name):
        if name == "conv":
            return _pack([self.wts["sc_conv_w"], self.wts["ffn_conv_w"]], SMALL_W_ROWS).reshape(1, SMALL_W_ROWS, 128)
        base, layer = _base(name)
        a = self.wts[base]
        if layer is not None:
            a = a[layer:layer + 1]
        if KIND[base] == "cols":
            return a.reshape(a.shape[-2], a.shape[-1]).astype(BF16)
        return a.reshape((-1,) + a.shape[-2:]).astype(BF16)

    def _gather_to(self, gname, stage, after):
        st = self.gathers[gname]
        if st["stage"] < 2 <= stage:
            plan, ncopy = _plan_gather_chips(st["kinds"])
            _, lands = _copies_wait(f"ag_{gname}_chips_wait", st["flight"], ncopy, plan)
            plan, ncopy = _plan_gather_sibling(st["kinds"])
            st["flight"] = _copies_start(f"ag_{gname}_sibling", [], lands, ncopy, plan)
            st["stage"] = 2
        if st["stage"] < 3 <= stage:
            plan, ncopy = _plan_gather_sibling(st["kinds"])
            _, lands = _copies_wait(f"ag_{gname}_sibling_wait", st["flight"], ncopy, plan)
            for nm, land in zip(st["names"], lands):
                self._arrived(nm, land)
            st["stage"] = 3

    def _arrived(self, name, land):
        if name == "conv":
            conv = land.reshape(N_DEV, SMALL_W_ROWS * 128)
            self.ready["sc_conv_w"] = conv[:, :3 * 128].reshape(N_DEV, 3, 128).transpose(1, 0, 2).reshape(3, D)
            fcw = conv[:, 3 * 128:3 * 128 + 6 * 352].reshape(N_DEV, 2, 3, 352).transpose(1, 2, 0, 3)
            fcw = fcw.reshape(2, 3, N_FF_BLK, FF_BLK).transpose(0, 2, 1, 3)
            self.ready["ffn_cw0"], self.ready["ffn_cw1"] = fcw[0], fcw[1]
        elif name in ("sc_w_in", "w_uk", "w_uv") or name.startswith("ffn_w_up"):
            self.ready[name] = land
        elif name.startswith("ffn_w_down"):
            self.ready[name] = land.reshape(1, N_FF_BLK, FF_BLK, D)
        elif name == "w_kr":
            self.ready[name] = jnp.pad(land.reshape(D, QK_ROPE), ((0, 0), (0, 128 - QK_ROPE)))
        elif name == "w_uq":
            self.ready[name] = jnp.pad(land.reshape(N_HEADS, Q_LORA, QK_NOPE + QK_ROPE),
                                       ((0, 0), (0, 0), (0, QK_PAD - QK_NOPE - QK_ROPE)))
        else:
            self.ready[name] = land.reshape(D, land.shape[-1])

    def need(self, name, after):
        if name not in self.ready:
            self._gather_to(self.group_of[name], 3, after)
        return self.ready[name]

    def grad(self, name, layer, array):
        self.grads[(name, layer)] = array

    def _scatter_to(self, gname, stage, after):
        keys = dict(SCATTER_GROUPS)[gname]
        st = self.scatters.setdefault(gname, dict(stage=0))
        kinds = [KIND[nm] for nm, _ in keys]
        if st["stage"] < 1 <= stage:
            grads = [self.grads[key] for key in keys]
            lands = []
            for gr, kind in zip(grads, kinds):
                shard = (gr.shape[0],) + gr.shape[2:] if kind == "blocked" else (gr.shape[0], gr.shape[1] // N_DEV)
                lands.append(lax.empty((N_CHIP,) + shard, BF16))
            plan, ncopy = _plan_scatter_sibling(kinds)
            st["flight"] = _copies_start(f"rs_{gname}_sibling", grads, lands, ncopy, plan)
            st["stage"] = 1
        if st["stage"] < 2 <= stage:
            plan, ncopy = _plan_scatter_sibling(kinds)
            grads, recvs = _copies_wait(f"rs_{gname}_sibling_wait", st["flight"], ncopy, plan)
            sums = [_chip_sum(f"rs_{gname}_sum{t}", gr, kind, rv, self.c_arr)
                    for t, (gr, kind, rv) in enumerate(zip(grads, kinds, recvs))]
            lands = [lax.empty(s.shape, BF16) for s in sums]
            plan, ncopy = _plan_scatter_chips(len(sums))
            st["flight"] = _copies_start(f"rs_{gname}_chips", sums, lands, ncopy, plan)
            st["stage"] = 2
        if st["stage"] < 3 <= stage:
            plan, ncopy = _plan_scatter_chips(len(keys))
            sums, recvs = _copies_wait(f"rs_{gname}_chips_wait", st["flight"], ncopy, plan)
            for t, ((nm, layer), own, rv) in enumerate(zip(keys, sums, recvs)):
                nl = 1 if layer is None else 2
                rows, w = own.shape[1], own.shape[2]
                w3, m3, v3 = (src[nm].reshape(nl, rows, w) for src in (self.wts, self.mom, self.var))
                self.results[nm] = _adamw_sharded(f"adamw_{gname}_{t}", own, rv, self.chip_ids, w3, m3, v3,
                                                  0 if layer is None else layer, self.results.get(nm))
            st["stage"] = 3

    def at(self, place, after):
        for action, gname in SCHEDULE.get(place, ()):
            self._advance(action, gname, after)

    def _advance(self, action, gname, after):
        if action.startswith("gather"):
            self._gather_to(gname, STAGES[action], after)
        else:
            self._scatter_to(gname, STAGES[action], after)

    def finish(self, after):
        for action, gname in FINISH:
            self._advance(action, gname, after)
        for gname, _ in SCATTER_GROUPS:
            self._scatter_to(gname, 3, after)
        return {nm: [o.reshape(self.wts[nm].shape) for o in outs] for nm, outs in self.results.items()}


REPLICATED = ("attn_norm", "ffn_norm", "final_norm", "kv_in_norm", "kv_latent_norm", "q_latent_norm", "ffn_conv_b")
WEIGHTS = ("attn_norm", "ffn_norm", "final_norm", "sc_w_in", "sc_conv_w", "sc_w_out", "kv_in_norm", "w_dkv",
           "kv_latent_norm", "w_kr", "w_uk", "w_uv", "w_dq", "q_latent_norm", "w_uq", "w_o", "ffn_w_up", "ffn_conv_w",
           "ffn_conv_b", "ffn_w_down")


def kernel(x, positions, attn_norm, ffn_norm, final_norm, sc_w_in, sc_conv_w, sc_w_out, kv_in_norm, w_dkv, kv_latent_norm, w_kr, w_uk, w_uv, w_dq, q_latent_norm, w_uq, w_o, ffn_w_up, ffn_conv_w, ffn_conv_b, ffn_w_down, loss_target, m_attn_norm, m_ffn_norm, m_final_norm, m_sc_w_in, m_sc_conv_w, m_sc_w_out, m_kv_in_norm, m_w_dkv, m_kv_latent_norm, m_w_kr, m_w_uk, m_w_uv, m_w_dq, m_q_latent_norm, m_w_uq, m_w_o, m_ffn_w_up, m_ffn_conv_w, m_ffn_conv_b, m_ffn_w_down, v_attn_norm, v_ffn_norm, v_final_norm, v_sc_w_in, v_sc_conv_w, v_sc_w_out, v_kv_in_norm, v_w_dkv, v_kv_latent_norm, v_w_kr, v_w_uk, v_w_uv, v_w_dq, v_q_latent_norm, v_w_uq, v_w_o, v_ffn_w_up, v_ffn_conv_w, v_ffn_conv_b, v_ffn_w_down):
    wts = dict(attn_norm=attn_norm, ffn_norm=ffn_norm, final_norm=final_norm, sc_w_in=sc_w_in, sc_conv_w=sc_conv_w,
               sc_w_out=sc_w_out, kv_in_norm=kv_in_norm, w_dkv=w_dkv, kv_latent_norm=kv_latent_norm, w_kr=w_kr,
               w_uk=w_uk, w_uv=w_uv, w_dq=w_dq, q_latent_norm=q_latent_norm, w_uq=w_uq, w_o=w_o, ffn_w_up=ffn_w_up,
               ffn_conv_w=ffn_conv_w, ffn_conv_b=ffn_conv_b, ffn_w_down=ffn_w_down)
    mom = dict(attn_norm=m_attn_norm, ffn_norm=m_ffn_norm, final_norm=m_final_norm, sc_w_in=m_sc_w_in,
               sc_conv_w=m_sc_conv_w, sc_w_out=m_sc_w_out, kv_in_norm=m_kv_in_norm, w_dkv=m_w_dkv,
               kv_latent_norm=m_kv_latent_norm, w_kr=m_w_kr, w_uk=m_w_uk, w_uv=m_w_uv, w_dq=m_w_dq,
               q_latent_norm=m_q_latent_norm, w_uq=m_w_uq, w_o=m_w_o, ffn_w_up=m_ffn_w_up, ffn_conv_w=m_ffn_conv_w,
               ffn_conv_b=m_ffn_conv_b, ffn_w_down=m_ffn_w_down)
    var = dict(attn_norm=v_attn_norm, ffn_norm=v_ffn_norm, final_norm=v_final_norm, sc_w_in=v_sc_w_in,
               sc_conv_w=v_sc_conv_w, sc_w_out=v_sc_w_out, kv_in_norm=v_kv_in_norm, w_dkv=v_w_dkv,
               kv_latent_norm=v_kv_latent_norm, w_kr=v_w_kr, w_uk=v_w_uk, w_uv=v_w_uv, w_dq=v_w_dq,
               q_latent_norm=v_q_latent_norm, w_uq=v_w_uq, w_o=v_w_o, ffn_w_up=v_ffn_w_up, ffn_conv_w=v_ffn_conv_w,
               ffn_conv_b=v_ffn_conv_b, ffn_w_down=v_ffn_w_down)
    xi, yi, ci = _place()
    me = 4 * xi + 2 * yi + ci
    _Chain.last = None

    ex = _Exchange(wts, mom, var, ffn_conv_b)
    rep = {
        "attn_norm": attn_norm, "ffn_norm": ffn_norm, "final_norm": final_norm,
        "kv_in_norm": kv_in_norm.reshape(1, D), "kv_latent_norm": kv_latent_norm.reshape(1, KV_LORA),
        "q_latent_norm": q_latent_norm.reshape(1, Q_LORA),
    }
    loss, grad_x, small = _local_step(x.reshape(T, D), positions.reshape(T, 1), loss_target.reshape(T, D), rep, ex)
    results = ex.finish(grad_x)

    small_order = list(REPLICATED) + ["sc_conv_w", "ffn_conv_w"]
    packed_g = _pack([loss[0, 0:1]] + [small[nm] for nm in small_order], SMALL_G_ROWS)
    parts = _all_gather("ag_small_grads", [(packed_g.reshape(1, SMALL_G_ROWS, 128), "blocked")])[0]
    parts = parts.reshape(N_DEV, SMALL_G_ROWS, 128)

    def full_params(src):
        scw_full = jnp.zeros((3, D), F32)
        scw_full = lax.dynamic_update_slice(scw_full, src["sc_conv_w"].reshape(3, 128), (0, me * 128))
        fcw_full = jnp.zeros((2, 3, D_FF), F32)
        fcw_full = lax.dynamic_update_slice(fcw_full, src["ffn_conv_w"], (0, 0, me * 352))
        return _pack([jnp.zeros((1,), F32)] + [src[nm] for nm in REPLICATED] + [scw_full, fcw_full], SMALL_G_ROWS)

    small_out = _adamw_small(parts, full_params(wts), full_params(mom), full_params(var))
    shapes = [(1,)] + [wts[nm].shape for nm in REPLICATED] + [(3, D), (2, 3, D_FF)]
    unpacked = [_unpack(o, shapes) for o in small_out]
    loss_total = unpacked[0][0].reshape(())
    for slot, nm in enumerate(small_order):
        vals = [u[slot + 1] for u in unpacked]
        if nm == "sc_conv_w":
            vals = [lax.dynamic_slice(a, (0, me * 128), (3, 128)).reshape(1, 3, 128) for a in vals]
        elif nm == "ffn_conv_w":
            vals = [lax.dynamic_slice(a, (0, 0, me * 352), (2, 3, 352)) for a in vals]
        results[nm] = vals

    outs = [loss_total, grad_x.reshape(1, T, D)]
    for slot in range(4):
        outs.extend(results[nm][slot] for nm in WEIGHTS)
    return tuple(outs)
```

```python
import jax
import jax.numpy as jnp
from jax import lax
from jax.experimental import pallas as pl
from jax.experimental.pallas import tpu as pltpu

F32 = jnp.float32
BF16 = jnp.bfloat16

T = 2048
D = 1024
N_HEADS = 8
QK_NOPE = 128
QK_ROPE = 64
V_HEAD = 128
Q_LORA = 384
KV_LORA = 256
D_FF = 2816
CHUNK = 64
ROPE_THETA = 10000.0
EPS = 1e-6
NEG_INF = -1e30
ADAM_LR = 0.001
ADAM_B1 = 0.9
ADAM_B2 = 0.999
ADAM_EPS = 1e-08
ADAM_WD = 0.01
ADAM_STEP = 10

N_DEV = 8
N_CHIP = 4
FF_BLK = D_FF * 2 // N_DEV
N_FF_BLK = D_FF // FF_BLK
QK_PAD = 256
HALO = 16

TM = 512
TR = 256
TQ = 256
VMEM_LIMIT = 56 * 1024 * 1024

NN = (((1,), (0,)), ((), ()))
NT = (((1,), (1,)), ((), ()))
TN = (((0,), (0,)), ((), ()))
MESH = pl.DeviceIdType.MESH


def _params(sem):
    return pltpu.CompilerParams(dimension_semantics=sem, vmem_limit_bytes=VMEM_LIMIT)


ANY_SPEC = pl.BlockSpec(memory_space=pl.ANY)
VMEM_SPEC = pl.BlockSpec(memory_space=pltpu.VMEM)


class _Chain:
    last = None


def _pallas(body, *, name, in_specs, out_specs, out_shape, grid=(), scratch_shapes=(), n_prefetch=0, aliases=None,
            params=None):
    def run(*args):
        after = _Chain.last
        n_lead = len(args)
        specs, operands, fn = list(in_specs), list(args), body
        if after is not None:
            def fn(*refs):
                return body(*refs[:n_lead], *refs[n_lead + 1:])
            specs.append(ANY_SPEC)
            operands.append(after)
        kw = dict(name=name, out_shape=out_shape, input_output_aliases=aliases or {})
        if params is not None:
            kw["compiler_params"] = params
        if n_prefetch:
            kw["grid_spec"] = pltpu.PrefetchScalarGridSpec(
                num_scalar_prefetch=n_prefetch, grid=grid, in_specs=specs, out_specs=out_specs,
                scratch_shapes=scratch_shapes)
        else:
            kw.update(grid=grid, in_specs=specs, out_specs=out_specs, scratch_shapes=scratch_shapes)
        outs = pl.pallas_call(fn, **kw)(*operands)
        _Chain.last = outs[0] if isinstance(outs, (list, tuple)) else outs
        return outs
    return run


def _mm(name, a, b, *, grid, a_spec, b_spec, o_spec, o_shape, o_dtype, dims, k_axis=None, acc_shape=None,
        add=None, add_spec=None):
    nk = grid[k_axis] if k_axis is not None else 1
    has_add = add is not None

    def body(*refs):
        a_ref, b_ref = refs[0], refs[1]
        p = 2
        add_ref = None
        if has_add:
            add_ref = refs[p]
            p += 1
        o_ref = refs[p]
        p += 1
        r = lax.dot_general(a_ref[...].astype(BF16), b_ref[...].astype(BF16), dims, preferred_element_type=F32)
        if k_axis is None:
            if has_add:
                r = r + add_ref[...].astype(F32)
            o_ref[...] = r.astype(o_dtype)
        else:
            acc = refs[p]
            k = pl.program_id(k_axis)

            @pl.when(k == 0)
            def _():
                acc[...] = r

            @pl.when(k > 0)
            def _():
                acc[...] += r

            @pl.when(k == nk - 1)
            def _():
                t = acc[...]
                if has_add:
                    t = t + add_ref[...].astype(F32)
                o_ref[...] = t.astype(o_dtype)

    in_specs = [a_spec, b_spec]
    args = [a, b]
    if has_add:
        in_specs.append(add_spec if add_spec is not None else o_spec)
        args.append(add)
    sem = tuple("arbitrary" if ax == k_axis else "parallel" for ax in range(len(grid)))
    scratch = [pltpu.VMEM(acc_shape, F32)] if k_axis is not None else []
    return _pallas(body, name=name, grid=grid, in_specs=in_specs, out_specs=o_spec,
                   out_shape=jax.ShapeDtypeStruct(o_shape, o_dtype), scratch_shapes=scratch, params=_params(sem))(*args)


def _mm_rows(name, a, b, dims, o_dtype, n_out, *, tn=None, add=None):
    k = a.shape[1]
    tn = n_out if tn is None else tn
    if dims == NN:
        b_spec = pl.BlockSpec((k, tn), lambda n, i: (0, n))
    else:
        b_spec = pl.BlockSpec((tn, k), lambda n, i: (n, 0))
    return _mm(name, a, b, grid=(n_out // tn, T // TM),
               a_spec=pl.BlockSpec((TM, k), lambda n, i: (i, 0)), b_spec=b_spec,
               o_spec=pl.BlockSpec((TM, tn), lambda n, i: (i, n)), o_shape=(T, n_out), o_dtype=o_dtype,
               dims=dims, add=add)


def _mm_wgrad(name, a, b, *, tn=512):
    k, n = a.shape[1], b.shape[1]
    tn = min(tn, n)
    return _mm(name, a, b, grid=(n // tn,),
               a_spec=pl.BlockSpec((T, k), lambda j: (0, 0)), b_spec=pl.BlockSpec((T, tn), lambda j: (0, j)),
               o_spec=pl.BlockSpec((k, tn), lambda j: (0, j)), o_shape=(k, n), o_dtype=BF16, dims=TN)


def _rms_fwd(name, x, g):
    d = x.shape[1]

    def body(x_ref, g_ref, o_ref):
        xv = x_ref[...]
        r = lax.rsqrt(jnp.mean(xv * xv, axis=-1, keepdims=True) + EPS)
        o_ref[...] = ((xv * r) * g_ref[...]).astype(BF16)

    return _pallas(
        body, name=name, grid=(T // TM,),
        in_specs=[pl.BlockSpec((TM, d), lambda i: (i, 0)), pl.BlockSpec((1, d), lambda i: (0, 0))],
        out_specs=pl.BlockSpec((TM, d), lambda i: (i, 0)),
        out_shape=jax.ShapeDtypeStruct((T, d), BF16), params=_params(("parallel",)))(x, g)


def _rms_bwd(name, x, g, dy, dres=None):
    d = x.shape[1]
    has_res = dres is not None

    def body(*refs):
        if has_res:
            x_ref, g_ref, dy_ref, res_ref, dx_ref, dxb_ref, dg_ref = refs
        else:
            x_ref, g_ref, dy_ref, dx_ref, dxb_ref, dg_ref = refs
        xv = x_ref[...]
        r = lax.rsqrt(jnp.mean(xv * xv, axis=-1, keepdims=True) + EPS)
        xn = xv * r
        dyv = dy_ref[...].astype(F32)
        gdy = dyv * g_ref[...]
        dx = r * (gdy - xn * jnp.mean(gdy * xn, axis=-1, keepdims=True))
        if has_res:
            dx = dx + res_ref[...]
        dx_ref[...] = dx
        dxb_ref[...] = dx.astype(BF16)
        part = jnp.sum(dyv * xn, axis=0, keepdims=True)

        @pl.when(pl.program_id(0) == 0)
        def _():
            dg_ref[...] = part

        @pl.when(pl.program_id(0) > 0)
        def _():
            dg_ref[...] += part

    row = pl.BlockSpec((TR, d), lambda i: (i, 0))
    vec = pl.BlockSpec((1, d), lambda i: (0, 0))
    args = [x, g, dy] + ([dres] if has_res else [])
    in_specs = [row, vec, row] + ([row] if has_res else [])
    return _pallas(
        body, name=name, grid=(T // TR,), in_specs=in_specs, out_specs=[row, row, vec],
        out_shape=[jax.ShapeDtypeStruct((T, d), F32), jax.ShapeDtypeStruct((T, d), BF16),
                   jax.ShapeDtypeStruct((1, d), F32)],
        params=_params(("arbitrary",)))(*args)


def _final(h, g, tgt):
    def body(h_ref, g_ref, t_ref, loss_ref, dh_ref, dhb_ref, dg_ref):
        hv = h_ref[...]
        r = lax.rsqrt(jnp.mean(hv * hv, axis=-1, keepdims=True) + EPS)
        xn = hv * r
        gv = g_ref[...]
        err = xn * gv - t_ref[...]
        part_loss = 0.5 * jnp.sum(jnp.mean(err * err, axis=-1, keepdims=True), axis=0, keepdims=True)
        dy = err * (1.0 / D)
        gdy = dy * gv
        dh = r * (gdy - xn * jnp.mean(gdy * xn, axis=-1, keepdims=True))
        dh_ref[...] = dh
        dhb_ref[...] = dh.astype(BF16)
        part = jnp.sum(dy * xn, axis=0, keepdims=True)
        first = pl.program_id(0) == 0

        @pl.when(first)
        def _():
            dg_ref[...] = part
            loss_ref[...] = jnp.broadcast_to(part_loss, (1, 128))

        @pl.when(jnp.logical_not(first))
        def _():
            dg_ref[...] += part
            loss_ref[...] += jnp.broadcast_to(part_loss, (1, 128))

    row = pl.BlockSpec((TR, D), lambda i: (i, 0))
    vec = pl.BlockSpec((1, D), lambda i: (0, 0))
    return _pallas(
        body, name="final_loss", grid=(T // TR,), in_specs=[row, vec, row],
        out_specs=[pl.BlockSpec((1, 128), lambda i: (0, 0)), row, row, vec],
        out_shape=[jax.ShapeDtypeStruct((1, 128), F32), jax.ShapeDtypeStruct((T, D), F32),
                   jax.ShapeDtypeStruct((T, D), BF16), jax.ShapeDtypeStruct((1, D), F32)],
        params=_params(("arbitrary",)))(h, g, tgt)


def _prev_idx(i):
    return jnp.maximum(i * (TR // HALO) - 1, 0)


def _next_idx(i):
    return jnp.minimum((i + 1) * (TR // HALO), T // HALO - 1)


def _causal_taps(ext):
    return pltpu.roll(ext, 2, 0)[HALO:], pltpu.roll(ext, 1, 0)[HALO:], ext[HALO:]


def _anticausal_taps(ext, n):
    rows = ext.shape[0]
    return pltpu.roll(ext, rows - 1, 0)[:n], pltpu.roll(ext, rows - 2, 0)[:n]


def _sc_fwd(z, w):
    def body(b_ref, c_ref, ch_ref, u_ref, uh_ref, w_ref, y_ref):
        i = pl.program_id(0)
        cu = c_ref[...].astype(F32) * u_ref[...].astype(F32)
        cuh = ch_ref[...].astype(F32) * uh_ref[...].astype(F32)
        cuh = jnp.where(i > 0, cuh, 0.0)
        x2, x1, x0 = _causal_taps(jnp.concatenate([cuh, cu], axis=0))
        wv = w_ref[...]
        cv = (x2 * wv[0:1] + x1 * wv[1:2]) + x0 * wv[2:3]
        y_ref[...] = (b_ref[...].astype(F32) * cv).astype(BF16)

    def main(part):
        return pl.BlockSpec((TR, D), lambda i: (i, part))

    def halo(part):
        return pl.BlockSpec((HALO, D), lambda i: (_prev_idx(i), part))

    return _pallas(
        body, name="sc_fwd", grid=(T // TR,),
        in_specs=[main(0), main(1), halo(1), main(2), halo(2), pl.BlockSpec((3, D), lambda i: (0, 0))],
        out_specs=pl.BlockSpec((TR, D), lambda i: (i, 0)),
        out_shape=jax.ShapeDtypeStruct((T, D), BF16), params=_params(("parallel",)))(z, z, z, z, z, w)


def _sc_bwd(z, dy, w):
    last = T // TR - 1

    def body(b_ref, bn_ref, c_ref, ch_ref, u_ref, uh_ref, dy_ref, dyn_ref, w_ref, dz_ref, dw_ref):
        i = pl.program_id(0)
        cv_ = c_ref[...].astype(F32)
        uv = u_ref[...].astype(F32)
        cu = cv_ * uv
        cuh = jnp.where(i > 0, ch_ref[...].astype(F32) * uh_ref[...].astype(F32), 0.0)
        x2, x1, x0 = _causal_taps(jnp.concatenate([cuh, cu], axis=0))
        wv = w_ref[...]
        conv = (x2 * wv[0:1] + x1 * wv[1:2]) + x0 * wv[2:3]
        dyv = dy_ref[...]
        dz_ref[:, 0:D] = (dyv * conv).astype(BF16)
        dconv = dyv * b_ref[...].astype(F32)
        dconv_n = jnp.where(i < last, dyn_ref[...] * bn_ref[...].astype(F32), 0.0)
        n1, n2 = _anticausal_taps(jnp.concatenate([dconv, dconv_n], axis=0), TR)
        dcu = (dconv * wv[2:3] + n1 * wv[1:2]) + n2 * wv[0:1]
        dz_ref[:, D:2 * D] = (dcu * uv).astype(BF16)
        dz_ref[:, 2 * D:3 * D] = (dcu * cv_).astype(BF16)
        part = jnp.concatenate([jnp.sum(dconv * x2, axis=0, keepdims=True),
                                jnp.sum(dconv * x1, axis=0, keepdims=True),
                                jnp.sum(dconv * x0, axis=0, keepdims=True)], axis=0)

        @pl.when(i == 0)
        def _():
            dw_ref[...] = part

        @pl.when(i > 0)
        def _():
            dw_ref[...] += part

    def main(part):
        return pl.BlockSpec((TR, D), lambda i: (i, part))

    def prev(part):
        return pl.BlockSpec((HALO, D), lambda i: (_prev_idx(i), part))

    def nxt(part):
        return pl.BlockSpec((HALO, D), lambda i: (_next_idx(i), part))

    wspec = pl.BlockSpec((3, D), lambda i: (0, 0))
    return _pallas(
        body, name="sc_bwd", grid=(T // TR,),
        in_specs=[main(0), nxt(0), main(1), prev(1), main(2), prev(2), main(0), nxt(0), wspec],
        out_specs=[pl.BlockSpec((TR, 3 * D), lambda i: (i, 0)), wspec],
        out_shape=[jax.ShapeDtypeStruct((T, 3 * D), BF16), jax.ShapeDtypeStruct((3, D), F32)],
        params=_params(("arbitrary",)))(z, z, z, z, z, z, dy, dy, w)


def _sigmoid(x):
    return 1.0 / (1.0 + jnp.exp(-x))


def _ffn_fwd(name, gv, w, b):
    def body(g_ref, gh_ref, v_ref, w_ref, b_ref, a_ref):
        i = pl.program_id(1)
        g = g_ref[...].astype(F32)
        gh = jnp.where(i > 0, gh_ref[...].astype(F32), 0.0)
        x2, x1, x0 = _causal_taps(jnp.concatenate([gh, g], axis=0))
        wv = w_ref[...]
        gc = ((x2 * wv[0:1] + x1 * wv[1:2]) + x0 * wv[2:3]) + b_ref[...]
        a_ref[...] = ((gc * _sigmoid(gc)) * v_ref[...].astype(F32)).astype(BF16)

    blk = (None, TR, FF_BLK)
    return _pallas(
        body, name=name, grid=(N_FF_BLK, T // TR),
        in_specs=[pl.BlockSpec(blk, lambda j, i: (j, i, 0)),
                  pl.BlockSpec((None, HALO, FF_BLK), lambda j, i: (j, _prev_idx(i), 0)),
                  pl.BlockSpec(blk, lambda j, i: (j + N_FF_BLK, i, 0)),
                  pl.BlockSpec((None, 3, FF_BLK), lambda j, i: (j, 0, 0)),
                  pl.BlockSpec((None, 1, FF_BLK), lambda j, i: (j, 0, 0))],
        out_specs=pl.BlockSpec(blk, lambda j, i: (j, i, 0)),
        out_shape=jax.ShapeDtypeStruct((N_FF_BLK, T, FF_BLK), BF16),
        params=_params(("parallel", "parallel")))(gv, gv, gv, w, b)


def _ffn_bwd(name, gv, dact, w, b):
    last = T // TR - 1

    def body(g_ref, gp_ref, gn_ref, v_ref, vn_ref, da_ref, dan_ref, w_ref, b_ref, dg_ref, dv_ref, dw_ref, db_ref):
        i = pl.program_id(1)
        gp = jnp.where(i > 0, gp_ref[...].astype(F32), 0.0)
        ext = jnp.concatenate([gp, g_ref[...].astype(F32), gn_ref[...].astype(F32)], axis=0)
        x2, x1, x0 = _causal_taps(ext)
        wv = w_ref[...]
        gc = ((x2 * wv[0:1] + x1 * wv[1:2]) + x0 * wv[2:3]) + b_ref[...]
        sg = _sigmoid(gc)
        da = jnp.concatenate([da_ref[...].astype(F32), jnp.where(i < last, dan_ref[...].astype(F32), 0.0)], axis=0)
        vv = jnp.concatenate([v_ref[...].astype(F32), vn_ref[...].astype(F32)], axis=0)
        dv_ref[...] = (da[:TR] * (gc[:TR] * sg[:TR])).astype(BF16)
        dgc = (da * vv) * (sg * (1.0 + gc * (1.0 - sg)))
        n1, n2 = _anticausal_taps(dgc, TR)
        d0 = dgc[:TR]
        dg_ref[...] = ((d0 * wv[2:3] + n1 * wv[1:2]) + n2 * wv[0:1]).astype(BF16)
        part_w = jnp.concatenate([jnp.sum(d0 * x2[:TR], axis=0, keepdims=True),
                                  jnp.sum(d0 * x1[:TR], axis=0, keepdims=True),
                                  jnp.sum(d0 * x0[:TR], axis=0, keepdims=True)], axis=0)
        part_b = jnp.sum(d0, axis=0, keepdims=True)

        @pl.when(i == 0)
        def _():
            dw_ref[...] = part_w
            db_ref[...] = part_b

        @pl.when(i > 0)
        def _():
            dw_ref[...] += part_w
            db_ref[...] += part_b

    blk = (None, TR, FF_BLK)
    hblk = (None, HALO, FF_BLK)
    wspec = pl.BlockSpec((None, 3, FF_BLK), lambda j, i: (j, 0, 0))
    bspec = pl.BlockSpec((None, 1, FF_BLK), lambda j, i: (j, 0, 0))
    return _pallas(
        body, name=name, grid=(N_FF_BLK, T // TR),
        in_specs=[pl.BlockSpec(blk, lambda j, i: (j, i, 0)),
                  pl.BlockSpec(hblk, lambda j, i: (j, _prev_idx(i), 0)),
                  pl.BlockSpec(hblk, lambda j, i: (j, _next_idx(i), 0)),
                  pl.BlockSpec(blk, lambda j, i: (j + N_FF_BLK, i, 0)),
                  pl.BlockSpec(hblk, lambda j, i: (j + N_FF_BLK, _next_idx(i), 0)),
                  pl.BlockSpec(blk, lambda j, i: (j, i, 0)),
                  pl.BlockSpec(hblk, lambda j, i: (j, _next_idx(i), 0)),
                  wspec, bspec],
        out_specs=[pl.BlockSpec(blk, lambda j, i: (j, i, 0)), pl.BlockSpec(blk, lambda j, i: (j, i, 0)), wspec, bspec],
        out_shape=[jax.ShapeDtypeStruct((N_FF_BLK, T, FF_BLK), BF16), jax.ShapeDtypeStruct((N_FF_BLK, T, FF_BLK), BF16),
                   jax.ShapeDtypeStruct((N_FF_BLK, 3, FF_BLK), F32), jax.ShapeDtypeStruct((N_FF_BLK, 1, FF_BLK), F32)],
        params=_params(("parallel", "arbitrary")))(gv, gv, gv, gv, gv, dact, dact, w, b)


def _rope_tables(pos, inv_freq):
    half = QK_ROPE // 2

    def body(p_ref, f_ref, c_ref, sa_ref, sb_ref):
        ang = p_ref[...].astype(F32) * f_ref[...]
        lane = lax.broadcasted_iota(jnp.int32, (T, 128), 1)
        c = jnp.cos(ang)
        s = jnp.sin(ang)
        c_ref[...] = jnp.where(lane < 2 * half, c, 0.0)
        sa_ref[...] = jnp.where(lane < half, -s, 0.0)
        sb_ref[...] = jnp.where(jnp.logical_and(lane >= half, lane < 2 * half), s, 0.0)

    return _pallas(
        body, name="rope_tables", in_specs=[VMEM_SPEC] * 2, out_specs=[VMEM_SPEC] * 3,
        out_shape=[jax.ShapeDtypeStruct((T, 128), F32)] * 3,
        params=pltpu.CompilerParams(vmem_limit_bytes=VMEM_LIMIT))(pos, inv_freq)


def _rope(name, x, tables, sign, out_dtype, reduce_groups=False):
    g, _, w = x.shape
    cos, sa, sb = tables

    def body(x_ref, c_ref, sa_ref, sb_ref, o_ref):
        xv = x_ref[...].astype(F32)
        if reduce_groups:
            acc = xv[0]
            for k in range(1, g):
                acc = acc + xv[k]
            xv = acc
        r = xv[:, w - 128:]
        out = r * c_ref[...] + sign * (pltpu.roll(r, 96, 1) * sa_ref[...] + pltpu.roll(r, 32, 1) * sb_ref[...])
        if w > 128:
            o_ref[:, :w - 128] = xv[:, :w - 128].astype(out_dtype)
        o_ref[:, w - 128:] = out.astype(out_dtype)

    tab = pl.BlockSpec((TM, 128), lambda h, i: (i, 0))
    if reduce_groups:
        x_spec = pl.BlockSpec((g, TM, w), lambda h, i: (0, i, 0))
        groups = 1
    else:
        x_spec = pl.BlockSpec((None, TM, w), lambda h, i: (h, i, 0))
        groups = g
    return _pallas(
        body, name=name, grid=(groups, T // TM), in_specs=[x_spec, tab, tab, tab],
        out_specs=pl.BlockSpec((None, TM, w), lambda h, i: (h, i, 0)),
        out_shape=jax.ShapeDtypeStruct((groups, T, w), out_dtype),
        params=_params(("parallel", "parallel")))(x, cos, sa, sb)


SCALE = (QK_NOPE + QK_ROPE) ** -0.5
LOG2E = 1.4426950408889634
SCALE2 = SCALE * LOG2E


def _diag_mask(transposed):
    shift = CHUNK.bit_length() - 1
    a = lax.broadcasted_iota(jnp.int32, (TQ, TQ), 0) >> shift
    b = lax.broadcasted_iota(jnp.int32, (TQ, TQ), 1) >> shift
    return (a <= b) if transposed else (b <= a)


def _keys(kn_ref, kr_ref, off):
    return jnp.concatenate([kn_ref[pl.ds(off, TQ), :], kr_ref[pl.ds(off, TQ), :]], axis=1)


def _attn_fwd(q, kn, kr, v):
    def body(q_ref, kn_ref, kr_ref, v_ref, o_ref, lse_ref):
        i = pl.program_id(1)
        qv = q_ref[...]

        def step(j, carry, masked):
            m, l, acc = carry
            off = pl.multiple_of(j * TQ, TQ)
            s = lax.dot_general(qv, _keys(kn_ref, kr_ref, off), NT, preferred_element_type=F32) * SCALE2
            if masked:
                s = jnp.where(_diag_mask(False), s, NEG_INF)
            m_new = jnp.maximum(m, jnp.max(s, axis=-1, keepdims=True))
            p = jnp.exp2(s - m_new)
            alpha = jnp.exp2(m - m_new)
            l = alpha * l + jnp.sum(p, axis=-1, keepdims=True)
            acc = alpha * acc + lax.dot_general(p.astype(BF16), v_ref[pl.ds(off, TQ), :], NN, preferred_element_type=F32)
            return m_new, l, acc

        init = (jnp.full((TQ, 1), NEG_INF, F32), jnp.zeros((TQ, 1), F32), jnp.zeros((TQ, V_HEAD), F32))
        carry = lax.fori_loop(0, i, lambda j, cr: step(j, cr, False), init)
        m, l, acc = step(i, carry, True)
        o_ref[...] = (acc / l).astype(BF16)
        lse_ref[...] = m + jnp.log(l) * LOG2E

    return _pallas(
        body, name="attn_fwd", grid=(N_HEADS, T // TQ),
        in_specs=[pl.BlockSpec((None, TQ, QK_PAD), lambda h, i: (h, i, 0)),
                  pl.BlockSpec((T, QK_NOPE), lambda h, i: (0, h)),
                  pl.BlockSpec((T, 128), lambda h, i: (0, 0)),
                  pl.BlockSpec((T, V_HEAD), lambda h, i: (0, h))],
        out_specs=[pl.BlockSpec((TQ, V_HEAD), lambda h, i: (i, h)), pl.BlockSpec((None, TQ, 1), lambda h, i: (h, i, 0))],
        out_shape=[jax.ShapeDtypeStruct((T, N_HEADS * V_HEAD), BF16), jax.ShapeDtypeStruct((N_HEADS, T, 1), F32)],
        params=_params(("parallel", "parallel")))(q, kn, kr, v)


def _attn_bwd_dq(q, kn, kr, v, o, do, lse):
    def body(q_ref, kn_ref, kr_ref, v_ref, o_ref, do_ref, lse_ref, dq_ref, dl_ref):
        i = pl.program_id(1)
        qv = q_ref[...]
        dov = do_ref[...]
        lse = lse_ref[...]
        delta = jnp.sum(dov.astype(F32) * o_ref[...].astype(F32), axis=-1, keepdims=True)
        dl_ref[...] = delta

        def step(j, dq, masked):
            off = pl.multiple_of(j * TQ, TQ)
            kk = _keys(kn_ref, kr_ref, off)
            s = lax.dot_general(qv, kk, NT, preferred_element_type=F32) * SCALE2
            if masked:
                s = jnp.where(_diag_mask(False), s, NEG_INF)
            p = jnp.exp2(s - lse)
            dp = lax.dot_general(dov, v_ref[pl.ds(off, TQ), :], NT, preferred_element_type=F32)
            ds = (p * (dp - delta)) * SCALE
            return dq + lax.dot_general(ds.astype(BF16), kk, NN, preferred_element_type=F32)

        dq = lax.fori_loop(0, i, lambda j, acc: step(j, acc, False), jnp.zeros((TQ, QK_PAD), F32))
        dq_ref[...] = step(i, dq, True)

    col = pl.BlockSpec((None, TQ, 1), lambda h, i: (h, i, 0))
    head = pl.BlockSpec((TQ, V_HEAD), lambda h, i: (i, h))
    return _pallas(
        body, name="attn_bwd_dq", grid=(N_HEADS, T // TQ),
        in_specs=[pl.BlockSpec((None, TQ, QK_PAD), lambda h, i: (h, i, 0)),
                  pl.BlockSpec((T, QK_NOPE), lambda h, i: (0, h)),
                  pl.BlockSpec((T, 128), lambda h, i: (0, 0)),
                  pl.BlockSpec((T, V_HEAD), lambda h, i: (0, h)),
                  head, head, col],
        out_specs=[pl.BlockSpec((None, TQ, QK_PAD), lambda h, i: (h, i, 0)), col],
        out_shape=[jax.ShapeDtypeStruct((N_HEADS, T, QK_PAD), F32), jax.ShapeDtypeStruct((N_HEADS, T, 1), F32)],
        params=_params(("parallel", "parallel")))(q, kn, kr, v, o, do, lse)


def _attn_bwd_dkv(q, kn, kr, v, do, lse_row, delta_row):
    nq = T // TQ

    def body(q_ref, kn_ref, kr_ref, v_ref, do_ref, lse_ref, dl_ref, dkn_ref, dkr_ref, dv_ref):
        j = pl.program_id(1)
        kk = jnp.concatenate([kn_ref[...], kr_ref[...]], axis=1)
        vv = v_ref[...]

        def step(i, carry, masked):
            dk, dv = carry
            off = pl.multiple_of(i * TQ, TQ)
            qi = q_ref[pl.ds(off, TQ), :]
            doi = do_ref[pl.ds(off, TQ), :]
            st = lax.dot_general(kk, qi, NT, preferred_element_type=F32) * SCALE2
            if masked:
                st = jnp.where(_diag_mask(True), st, NEG_INF)
            pt = jnp.exp2(st - lse_ref[:, pl.ds(off, TQ)])
            dv = dv + lax.dot_general(pt.astype(BF16), doi, NN, preferred_element_type=F32)
            dpt = lax.dot_general(vv, doi, NT, preferred_element_type=F32)
            dst = (pt * (dpt - dl_ref[:, pl.ds(off, TQ)])) * SCALE
            dk = dk + lax.dot_general(dst.astype(BF16), qi, NN, preferred_element_type=F32)
            return dk, dv

        carry = step(j, (jnp.zeros((TQ, QK_PAD), F32), jnp.zeros((TQ, V_HEAD), F32)), True)
        dk, dv = lax.fori_loop(j + 1, nq, lambda i, cr: step(i, cr, False), carry)
        dkn_ref[...] = dk[:, :QK_NOPE].astype(BF16)
        dkr_ref[...] = dk[:, QK_NOPE:]
        dv_ref[...] = dv.astype(BF16)

    row = pl.BlockSpec((None, 1, T), lambda h, j: (h, 0, 0))
    head = pl.BlockSpec((TQ, 128), lambda h, j: (j, h))
    return _pallas(
        body, name="attn_bwd_dkv", grid=(N_HEADS, nq),
        in_specs=[pl.BlockSpec((None, T, QK_PAD), lambda h, j: (h, 0, 0)),
                  head, pl.BlockSpec((TQ, 128), lambda h, j: (j, 0)), head,
                  pl.BlockSpec((T, V_HEAD), lambda h, j: (0, h)), row, row],
        out_specs=[head, pl.BlockSpec((None, TQ, 128), lambda h, j: (h, j, 0)), head],
        out_shape=[jax.ShapeDtypeStruct((T, N_HEADS * QK_NOPE), BF16), jax.ShapeDtypeStruct((N_HEADS, T, 128), F32),
                   jax.ShapeDtypeStruct((T, N_HEADS * V_HEAD), BF16)],
        params=_params(("parallel", "parallel")))(q, kn, kr, v, do, lse_row, delta_row)


def _ffn_layer_fwd(tag, h, gain, ex):
    hf = _rms_fwd(f"{tag}_norm", h, gain)
    gv = _mm(f"{tag}_up", hf, ex.need(f"ffn_w_up{tag[1]}", hf), grid=(N_DEV, T // TM),
             a_spec=pl.BlockSpec((TM, D), lambda j, i: (i, 0)),
             b_spec=pl.BlockSpec((None, None, D, FF_BLK), lambda j, i: (0, j, 0, 0)),
             o_spec=pl.BlockSpec((None, TM, FF_BLK), lambda j, i: (j, i, 0)),
             o_shape=(N_DEV, T, FF_BLK), o_dtype=BF16, dims=NN)
    ex.at(f"{tag}_up", gv)
    act = _ffn_fwd(f"{tag}_act", gv, ex.need(f"ffn_cw{tag[1]}", gv), ex.need(f"ffn_cb{tag[1]}", gv))
    ex.at(f"{tag}_act", act)
    tn = 512
    out = _mm(f"{tag}_down", act, ex.need(f"ffn_w_down{tag[1]}", act), grid=(D // tn, T // TM, N_FF_BLK),
              a_spec=pl.BlockSpec((None, TM, FF_BLK), lambda n, i, k: (k, i, 0)),
              b_spec=pl.BlockSpec((None, None, FF_BLK, tn), lambda n, i, k: (0, k, 0, n)),
              o_spec=pl.BlockSpec((TM, tn), lambda n, i, k: (i, n)), o_shape=(T, D), o_dtype=F32,
              dims=NN, k_axis=2, acc_shape=(TM, tn), add=h, add_spec=pl.BlockSpec((TM, tn), lambda n, i, k: (i, n)))
    ex.at(f"{tag}_down", out)
    return out, (hf, gv, act)


def _ffn_layer_bwd(tag, h, gain, ex, saved, dh, dh_bf):
    hf, gv, act = saved
    layer = tag[1]
    w_up, w_down4 = ex.need(f"ffn_w_up{layer}", dh_bf), ex.need(f"ffn_w_down{layer}", dh_bf)
    dact = _mm(f"{tag}_dact", dh_bf, w_down4, grid=(N_FF_BLK, T // TM),
               a_spec=pl.BlockSpec((TM, D), lambda j, i: (i, 0)),
               b_spec=pl.BlockSpec((None, None, FF_BLK, D), lambda j, i: (0, j, 0, 0)),
               o_spec=pl.BlockSpec((None, TM, FF_BLK), lambda j, i: (j, i, 0)),
               o_shape=(N_FF_BLK, T, FF_BLK), o_dtype=BF16, dims=NT)
    tn = 512
    g_down = _mm(f"{tag}_gdown", act, dh_bf, grid=(N_FF_BLK, D // tn),
                 a_spec=pl.BlockSpec((None, T, FF_BLK), lambda j, n: (j, 0, 0)),
                 b_spec=pl.BlockSpec((T, tn), lambda j, n: (0, n)),
                 o_spec=pl.BlockSpec((FF_BLK, tn), lambda j, n: (j, n)),
                 o_shape=(D_FF, D), o_dtype=BF16, dims=TN)
    dg, dv, dcw, dcb = _ffn_bwd(f"{tag}_dact_ew", gv, dact, ex.need(f"ffn_cw{layer}", dact), ex.need(f"ffn_cb{layer}", dact))
    ex.at(f"{tag}_dact_ew", dg)
    dhf = None
    g_up = []
    for half, (name, dpart) in enumerate((("g", dg), ("v", dv))):
        dhf = _mm(f"{tag}_dhf_{name}", dpart, w_up, grid=(T // TM, N_FF_BLK),
                  a_spec=pl.BlockSpec((None, TM, FF_BLK), lambda i, k: (k, i, 0)),
                  b_spec=pl.BlockSpec((None, None, D, FF_BLK), lambda i, k, half=half: (0, k + half * N_FF_BLK, 0, 0)),
                  o_spec=pl.BlockSpec((TM, D), lambda i, k: (i, 0)), o_shape=(T, D), o_dtype=F32,
                  dims=NT, k_axis=1, acc_shape=(TM, D), add=dhf)
        g_up.append(_mm(f"{tag}_gup_{name}", hf, dpart, grid=(N_FF_BLK,),
                        a_spec=pl.BlockSpec((T, D), lambda j: (0, 0)),
                        b_spec=pl.BlockSpec((None, T, FF_BLK), lambda j: (j, 0, 0)),
                        o_spec=pl.BlockSpec((None, D, FF_BLK), lambda j: (j, 0, 0)),
                        o_shape=(N_FF_BLK, D, FF_BLK), o_dtype=BF16, dims=TN))
    ex.grad("ffn_w_up", int(layer), jnp.concatenate(g_up, axis=0).reshape(1, N_DEV, D, FF_BLK))
    ex.grad("ffn_w_down", int(layer), g_down.reshape(1, N_DEV, D_FF // N_DEV, D))
    dh_in, dh_in_bf, dgain = _rms_bwd(f"{tag}_dnorm", h, gain, dhf, dres=dh)
    return dh_in, dh_in_bf, dgain, dcw, dcb


def _local_step(x, pos, tgt, rep, ex):
    attn_norm, ffn_norm, final_norm = rep["attn_norm"], rep["ffn_norm"], rep["final_norm"]
    half = QK_ROPE // 2
    inv = 1.0 / (ROPE_THETA ** (jnp.arange(half, dtype=F32) / half))
    inv_freq = jnp.concatenate([inv, inv, jnp.zeros((128 - 2 * half,), F32)]).reshape(1, 128)
    tables = _rope_tables(pos, inv_freq)

    hn0 = _rms_fwd("l0_norm", x, attn_norm[0:1])
    w_in = ex.need("sc_w_in", hn0)
    ex.at("mixer_ready", hn0)
    z = _mm_rows("l0_in", hn0, w_in, NN, BF16, 3 * D, tn=512)
    ex.at("l0_in", z)
    y = _sc_fwd(z, ex.need("sc_conv_w", z))
    h1 = _mm_rows("l0_out", y, ex.need("sc_w_out", y), NN, F32, D, tn=512, add=x)
    ex.at("l0_out", h1)
    h2, ffn0 = _ffn_layer_fwd("f0", h1, ffn_norm[0:1], ex)

    hk = _rms_fwd("kv_norm", h2, rep["kv_in_norm"])
    ckv_raw = _mm_rows("kv_down", hk, ex.need("w_dkv", hk), NN, F32, KV_LORA)
    kr_raw = _mm_rows("kv_rope", hk, ex.need("w_kr", hk), NN, F32, 128)
    ckv = _rms_fwd("kv_lnorm", ckv_raw, rep["kv_latent_norm"])
    kn = _mm_rows("kv_uk", ckv, ex.need("w_uk", ckv), NN, BF16, N_HEADS * QK_NOPE)
    vv = _mm_rows("kv_uv", ckv, ex.need("w_uv", ckv), NN, BF16, N_HEADS * V_HEAD)
    ex.at("kv_uv", vv)
    kr = _rope("k_rope", kr_raw.reshape(1, T, 128), tables, 1.0, BF16).reshape(T, 128)

    hn1 = _rms_fwd("l1_norm", h2, attn_norm[1:2])
    cq_raw = _mm_rows("q_down", hn1, ex.need("w_dq", hn1), NN, F32, Q_LORA)
    cq = _rms_fwd("q_lnorm", cq_raw, rep["q_latent_norm"])
    w_uq = ex.need("w_uq", cq)
    q_raw = _mm("q_up", cq, w_uq, grid=(N_HEADS, T // TM),
                a_spec=pl.BlockSpec((TM, Q_LORA), lambda h, i: (i, 0)),
                b_spec=pl.BlockSpec((None, Q_LORA, QK_PAD), lambda h, i: (h, 0, 0)),
                o_spec=pl.BlockSpec((None, TM, QK_PAD), lambda h, i: (h, i, 0)),
                o_shape=(N_HEADS, T, QK_PAD), o_dtype=F32, dims=NN)
    q = _rope("q_rope", q_raw, tables, 1.0, BF16)
    o, lse = _attn_fwd(q, kn, kr, vv)
    w_o = ex.need("w_o", o)
    h3 = _mm_rows("attn_out", o, w_o, NN, F32, D, tn=512, add=h2)
    h4, ffn1 = _ffn_layer_fwd("f1", h3, ffn_norm[1:2], ex)

    loss, dh4, dh4_bf, d_final = _final(h4, final_norm.reshape(1, D), tgt)

    dh3, dh3_bf, d_fn1, dcw1, dcb1 = _ffn_layer_bwd("f1", h3, ffn_norm[1:2], ex, ffn1, dh4, dh4_bf)
    ex.at("f1_bwd", dh3)

    do = _mm_rows("d_attn_out", dh3_bf, w_o, NT, BF16, N_HEADS * V_HEAD)
    ex.grad("w_o", None, _mm_wgrad("g_w_o", o, dh3_bf).reshape(1, N_DEV, D // N_DEV, D))
    dq, delta = _attn_bwd_dq(q, kn, kr, vv, o, do, lse)
    ex.at("attn_dq", dq)
    dkn, dkr, dvv = _attn_bwd_dkv(q, kn, kr, vv, do, lse.reshape(N_HEADS, 1, T), delta.reshape(N_HEADS, 1, T))
    dq_pre = _rope("dq_rope", dq, tables, -1.0, BF16)
    dcq = _mm("d_q_up", dq_pre, w_uq, grid=(T // TM, N_HEADS),
              a_spec=pl.BlockSpec((None, TM, QK_PAD), lambda i, h: (h, i, 0)),
              b_spec=pl.BlockSpec((None, Q_LORA, QK_PAD), lambda i, h: (h, 0, 0)),
              o_spec=pl.BlockSpec((TM, Q_LORA), lambda i, h: (i, 0)), o_shape=(T, Q_LORA), o_dtype=F32,
              dims=NT, k_axis=1, acc_shape=(TM, Q_LORA))
    g_uq = _mm("g_w_uq", cq, dq_pre, grid=(N_HEADS,),
               a_spec=pl.BlockSpec((T, Q_LORA), lambda h: (0, 0)),
               b_spec=pl.BlockSpec((None, T, QK_PAD), lambda h: (h, 0, 0)),
               o_spec=pl.BlockSpec((None, Q_LORA, QK_PAD), lambda h: (h, 0, 0)),
               o_shape=(N_HEADS, Q_LORA, QK_PAD), o_dtype=BF16, dims=TN)
    ex.grad("w_uq", None, g_uq[:, :, :QK_NOPE + QK_ROPE].reshape(1, N_DEV, Q_LORA, QK_NOPE + QK_ROPE))
    _, dcq_raw_bf, d_qln = _rms_bwd("d_q_lnorm", cq_raw, rep["q_latent_norm"], dcq)
    dhn1 = _mm_rows("d_q_down", dcq_raw_bf, ex.need("w_dq", dcq_raw_bf), NT, F32, D)
    ex.grad("w_dq", None, _mm_wgrad("g_w_dq", hn1, dcq_raw_bf).reshape(1, N_DEV, D // N_DEV, Q_LORA))
    dh2_a, _, d_an1 = _rms_bwd("d_l1_norm", h2, attn_norm[1:2], dhn1, dres=dh3)

    dckv = _mm_rows("d_kv_uk", dkn, ex.need("w_uk", dkn), NT, F32, KV_LORA)
    dckv = _mm_rows("d_kv_uv", dvv, ex.need("w_uv", dvv), NT, F32, KV_LORA, add=dckv)
    ex.grad("w_uk", None, _mm_wgrad("g_w_uk", ckv, dkn))
    ex.grad("w_uv", None, _mm_wgrad("g_w_uv", ckv, dvv))
    _, dckv_raw_bf, d_kvln = _rms_bwd("d_kv_lnorm", ckv_raw, rep["kv_latent_norm"], dckv)
    dkr_raw_bf = _rope("dk_rope", dkr, tables, -1.0, BF16, reduce_groups=True).reshape(T, 128)
    dhk = _mm_rows("d_kv_down", dckv_raw_bf, ex.need("w_dkv", dckv_raw_bf), NT, F32, D)
    dhk = _mm_rows("d_kv_rope", dkr_raw_bf, ex.need("w_kr", dkr_raw_bf), NT, F32, D, add=dhk)
    ex.grad("w_dkv", None, _mm_wgrad("g_w_dkv", hk, dckv_raw_bf).reshape(1, N_DEV, D // N_DEV, KV_LORA))
    ex.grad("w_kr", None, _mm_wgrad("g_w_kr", hk, dkr_raw_bf)[:, :QK_ROPE].reshape(1, N_DEV, D // N_DEV, QK_ROPE))
    dh2, dh2_bf, d_kvin = _rms_bwd("d_kv_norm", h2, rep["kv_in_norm"], dhk, dres=dh2_a)
    ex.at("kv_bwd", dh2)

    dh1, dh1_bf, d_fn0, dcw0, dcb0 = _ffn_layer_bwd("f0", h1, ffn_norm[0:1], ex, ffn0, dh2, dh2_bf)
    ex.at("f0_bwd", dh1)

    dy = _mm_rows("d_l0_out", dh1_bf, ex.need("sc_w_out", dh1_bf), NT, F32, D)
    ex.grad("sc_w_out", None, _mm_wgrad("g_sc_w_out", y, dh1_bf).reshape(1, N_DEV, D // N_DEV, D))
    dz, d_scw = _sc_bwd(z, dy, ex.need("sc_conv_w", dy))
    ex.at("sc_bwd", dz)
    dhn0 = _mm_rows("d_l0_in", dz, ex.need("sc_w_in", dz), NT, F32, D)
    ex.grad("sc_w_in", None, _mm_wgrad("g_sc_w_in", hn0, dz))
    grad_x, _, d_an0 = _rms_bwd("d_l0_norm", x, attn_norm[0:1], dhn0, dres=dh1)

    small = {
        "attn_norm": jnp.concatenate([d_an0, d_an1], axis=0),
        "ffn_norm": jnp.concatenate([d_fn0, d_fn1], axis=0),
        "final_norm": d_final.reshape(D),
        "kv_in_norm": d_kvin.reshape(D),
        "kv_latent_norm": d_kvln.reshape(KV_LORA),
        "q_latent_norm": d_qln,
        "ffn_conv_b": jnp.stack([dcb0, dcb1]).transpose(0, 2, 1, 3).reshape(2, D_FF),
        "sc_conv_w": d_scw,
        "ffn_conv_w": jnp.stack([dcw0, dcw1]).transpose(0, 2, 1, 3).reshape(2, 3, D_FF),
    }
    return loss, grad_x, small


def _place():
    return lax.axis_index("x"), lax.axis_index("y"), lax.axis_index("c")


def _peers():
    x, y, c = _place()
    return (x, y, 1 - c), [(1 - x, y), (x, 1 - y), (1 - x, 1 - y)]


def _window(ref, kind, dev):
    if kind == "blocked":
        return ref.at[:, dev]
    width = ref.shape[-1] // N_DEV
    return ref.at[:, pl.ds(pl.multiple_of(dev * width, 128), width)]


def _all_gather(name, items):
    n = len(items)
    out_shapes = []
    for shard, kind in items:
        if kind == "blocked":
            shape = (shard.shape[0], N_DEV) + shard.shape[1:]
        else:
            shape = (shard.shape[0], N_DEV * shard.shape[1])
        out_shapes.append(jax.ShapeDtypeStruct(shape, shard.dtype))

    def body(*refs):
        srcs, outs = refs[:n], refs[n:2 * n]
        send_sems, recv_sems, local_sems = refs[2 * n:]
        x, y, c = _place()
        me = 4 * x + 2 * y + c
        sibling, chips = _peers()

        def num(px, py, pc):
            return 4 * px + 2 * py + pc

        def copy(t, k, dev, to, from_src):
            kind = items[t][1]
            dst = _window(outs[t], kind, dev)
            return pltpu.make_async_remote_copy(
                src_ref=srcs[t] if from_src else dst, dst_ref=dst,
                send_sem=send_sems.at[t, k], recv_sem=recv_sems.at[t, k], device_id=to, device_id_type=MESH)

        mine = [pltpu.make_async_copy(srcs[t], _window(outs[t], items[t][1], me), local_sems.at[t]) for t in range(n)]
        for cp in mine:
            cp.start()
        first = []
        for t in range(n):
            first.append(copy(t, 0, me, sibling, True))
            for j, chip in enumerate(chips):
                first.append(copy(t, 1 + j, me, (*chip, c), True))
        for cp in first:
            cp.start()
        passed = []
        for j, chip in enumerate(chips):
            for t in range(n):
                copy(t, 1 + j, num(*chip, c), (x, y, c), False).wait_recv()
                fwd = copy(t, 4 + j, num(*chip, c), sibling, False)
                fwd.start()
                passed.append(fwd)
        for t in range(n):
            copy(t, 0, num(x, y, 1 - c), (x, y, c), False).wait_recv()
            for j, chip in enumerate(chips):
                copy(t, 4 + j, num(*chip, 1 - c), (x, y, c), False).wait_recv()
        for cp in first + passed:
            cp.wait_send()
        for cp in mine:
            cp.wait()

    return _pallas(
        body, name=name, in_specs=[ANY_SPEC] * n, out_specs=[ANY_SPEC] * n, out_shape=out_shapes,
        scratch_shapes=[pltpu.SemaphoreType.DMA((n, 7)), pltpu.SemaphoreType.DMA((n, 7)), pltpu.SemaphoreType.DMA((n,))],
    )(*[s for s, _ in items])


HBM_SPEC = pl.BlockSpec(memory_space=pltpu.HBM)
SEM_SPEC = pl.BlockSpec(memory_space=pltpu.SEMAPHORE)
EFFECT = pltpu.SideEffectType.DATAFLOW_SIDE_EFFECTING
TOKEN = jax.ShapeDtypeStruct((8, 128), F32)


def _hbm(a):
    return pltpu.with_memory_space_constraint(a, pltpu.HBM)


def _copies_start(name, srcs, lands, ncopy, plan):
    ns, nl = len(srcs), len(lands)

    def body(*refs):
        send, recv, token = refs[ns + nl], refs[ns + nl + 1], refs[-1]
        copies = plan(refs[:ns], refs[ns:ns + nl])
        assert len(copies) == ncopy
        for k, (sent, dst, to, _) in enumerate(copies):
            pltpu.make_async_remote_copy(src_ref=sent, dst_ref=dst, send_sem=send.at[k], recv_sem=recv.at[k],
                                         device_id=to, device_id_type=MESH).start()
        token[...] = jnp.zeros_like(token)

    arrays = list(srcs) + list(lands)
    outs = pl.pallas_call(
        body, name=name, in_specs=[HBM_SPEC] * (ns + nl),
        out_specs=[SEM_SPEC] * 2 + [HBM_SPEC] * (ns + nl) + [VMEM_SPEC],
        out_shape=[pltpu.SemaphoreType.DMA((ncopy,))] * 2 + [pltpu.HBM(a.shape, a.dtype) for a in arrays] + [TOKEN],
        input_output_aliases={i: 2 + i for i in range(ns + nl)},
        compiler_params=pltpu.CompilerParams(has_side_effects=EFFECT))(*[_hbm(a) for a in arrays])
    _Chain.last = outs[-1]
    return outs[0], outs[1], list(outs[2:2 + ns]), list(outs[2 + ns:-1])


def _copies_wait(name, started, ncopy, plan):
    send, recv, srcs, lands = started
    ns, nl = len(srcs), len(lands)

    def body(*refs):
        send_ref, recv_ref, token = refs[ns + nl], refs[ns + nl + 1], refs[-1]
        copies = plan(refs[:ns], refs[ns:ns + nl])
        assert len(copies) == ncopy
        for k, (sent, _, to, landed) in enumerate(copies):
            cp = pltpu.make_async_remote_copy(src_ref=sent, dst_ref=landed, send_sem=send_ref.at[k],
                                              recv_sem=recv_ref.at[k], device_id=to, device_id_type=MESH)
            cp.wait_send()
            cp.wait_recv()
        token[...] = jnp.zeros_like(token)

    arrays = list(srcs) + list(lands)
    outs = pl.pallas_call(
        body, name=name, in_specs=[HBM_SPEC] * (ns + nl) + [SEM_SPEC] * 2 + [ANY_SPEC],
        out_specs=[HBM_SPEC] * (ns + nl) + [VMEM_SPEC], out_shape=[pltpu.HBM(a.shape, a.dtype) for a in arrays] + [TOKEN],
        input_output_aliases={i: i for i in range(ns + nl)},
        compiler_params=pltpu.CompilerParams(has_side_effects=EFFECT))(*arrays, send, recv, _Chain.last)
    _Chain.last = outs[-1]
    return list(outs[:ns]), list(outs[ns:-1])


def _plan_gather_chips(kinds):
    def plan(srcs, lands):
        x, y, c = _place()
        sibling, chips = _peers()
        out = []
        for t, kind in enumerate(kinds):
            mine = _window(lands[t], kind, 4 * x + 2 * y + c)
            out.append((srcs[t], mine, sibling, _window(lands[t], kind, 4 * x + 2 * y + 1 - c)))
            for px, py in chips:
                out.append((srcs[t], mine, (px, py, c), _window(lands[t], kind, 4 * px + 2 * py + c)))
        return out
    return plan, 4 * len(kinds)


def _plan_gather_sibling(kinds):
    def plan(srcs, lands):
        _, _, c = _place()
        sibling, chips = _peers()
        out = []
        for t, kind in enumerate(kinds):
            for px, py in chips:
                w = _window(lands[t], kind, 4 * px + 2 * py + c)
                out.append((w, w, sibling, _window(lands[t], kind, 4 * px + 2 * py + 1 - c)))
        return out
    return plan, 3 * len(kinds)


def _plan_scatter_sibling(kinds):
    def plan(srcs, lands):
        _, _, c = _place()
        sibling, _ = _peers()
        out = []
        for t, kind in enumerate(kinds):
            for k in range(N_CHIP):
                out.append((_window(srcs[t], kind, 2 * k + 1 - c), lands[t].at[k], sibling, lands[t].at[k]))
        return out
    return plan, N_CHIP * len(kinds)


def _plan_scatter_chips(n):
    def plan(srcs, lands):
        x, y, c = _place()
        _, chips = _peers()
        out = []
        for t in range(n):
            for px, py in chips:
                out.append((srcs[t].at[2 * px + py], lands[t].at[2 * x + y], (px, py, c), lands[t].at[2 * px + py]))
        return out
    return plan, 3 * n


def _landing(shard, kind, me):
    if kind == "blocked":
        land = lax.empty((shard.shape[0], N_DEV) + shard.shape[1:], shard.dtype)
        return lax.dynamic_update_slice(land, shard[:, None], (0, me) + (0,) * (shard.ndim - 1))
    land = lax.empty((shard.shape[0], N_DEV * shard.shape[1]), shard.dtype)
    return lax.dynamic_update_slice(land, shard, (0, me * shard.shape[1]))


def _row_tile(rows):
    for tr in (512, 384, 352, 256, 128, 64, 32, 16):
        if rows % tr == 0:
            return tr
    raise ValueError(rows)


def _chip_sum(name, gr, kind, recv, c):
    if kind == "blocked":
        nl, _, r, w = gr.shape
        rows = nl * r
        tr = _row_tile(r)
        per = r // tr
        g_spec = pl.BlockSpec((None, None, tr, w), lambda k, i, cref: (i // per, 2 * k + cref[0], i % per, 0))
    else:
        rows, w = gr.shape[0], gr.shape[1] // N_DEV
        tr = _row_tile(rows)
        g_spec = pl.BlockSpec((tr, w), lambda k, i, cref: (i, 2 * k + cref[0]))
    recv = recv.reshape(N_CHIP, rows, w)

    def body(c_ref, g_ref, r_ref, o_ref):
        del c_ref
        o_ref[...] = (g_ref[...].astype(F32) + r_ref[...].astype(F32)).astype(BF16)

    blk = pl.BlockSpec((None, tr, w), lambda k, i, cref: (k, i, 0))
    return _pallas(
        body, name=name, n_prefetch=1, grid=(N_CHIP, rows // tr), in_specs=[g_spec, blk], out_specs=blk,
        out_shape=jax.ShapeDtypeStruct((N_CHIP, rows, w), BF16),
        params=_params(("parallel", "parallel")))(c, gr, recv)


def _adamw_math(g, wv, mv, vv):
    m = ADAM_B1 * mv + (1.0 - ADAM_B1) * g
    v = ADAM_B2 * vv + (1.0 - ADAM_B2) * (g * g)
    m_hat = m / (1.0 - ADAM_B1 ** ADAM_STEP)
    v_hat = v / (1.0 - ADAM_B2 ** ADAM_STEP)
    delta = -ADAM_LR * (m_hat / (jnp.sqrt(v_hat) + ADAM_EPS) + ADAM_WD * wv)
    return delta, m, v


def _adamw_sharded(name, own, recv, chip_ids, w3, m3, v3, layer, prev):
    nl, rows, w = w3.shape
    tr = _row_tile(rows)
    has_prev = prev is not None

    def body(*refs):
        own_ref, r1_ref, r2_ref, r3_ref, w_ref, m_ref, v_ref = refs[1:8]
        g_ref, d_ref, nm_ref, nv_ref = refs[-4:]
        g = ((own_ref[...].astype(F32) + r1_ref[...].astype(F32)) + r2_ref[...].astype(F32)) + r3_ref[...].astype(F32)
        g_ref[...] = g
        d_ref[...], nm_ref[...], nv_ref[...] = _adamw_math(g, w_ref[...], m_ref[...], v_ref[...])

    def pick(slot):
        return pl.BlockSpec((None, tr, w), lambda i, ids: (ids[slot], i, 0))

    slab = pl.BlockSpec((None, tr, w), lambda i, ids: (layer, i, 0))
    in_specs = [pick(0), pick(1), pick(2), pick(3), slab, slab, slab]
    args = [chip_ids, own, recv, recv, recv, w3, m3, v3]
    aliases = {}
    if has_prev:
        in_specs += [ANY_SPEC] * 4
        aliases = {len(args) + k: k for k in range(4)}
        args += list(prev)
    return _pallas(
        body, name=name, n_prefetch=1, grid=(rows // tr,), in_specs=in_specs, out_specs=[slab] * 4,
        out_shape=[jax.ShapeDtypeStruct((nl, rows, w), F32)] * 4, aliases=aliases,
        params=_params(("parallel",)))(*args)


def _adamw_small(parts, wv, mv, vv):
    r = wv.shape[0]

    def body(p_ref, w_ref, m_ref, v_ref, g_ref, d_ref, nm_ref, nv_ref):
        g = p_ref[0]
        for k in range(1, N_DEV):
            g = g + p_ref[k]
        g_ref[...] = g
        d_ref[...], nm_ref[...], nv_ref[...] = _adamw_math(g, w_ref[...], m_ref[...], v_ref[...])

    return _pallas(
        body, name="adamw_small", in_specs=[VMEM_SPEC] * 4, out_specs=[VMEM_SPEC] * 4,
        out_shape=[jax.ShapeDtypeStruct((r, 128), F32)] * 4,
        params=pltpu.CompilerParams(vmem_limit_bytes=VMEM_LIMIT))(parts, wv, mv, vv)


KIND = {"sc_w_in": "cols", "sc_w_out": "blocked", "w_dkv": "blocked", "w_kr": "blocked", "w_uk": "cols", "w_uv": "cols",
        "w_dq": "blocked", "w_uq": "blocked", "w_o": "blocked", "ffn_w_up": "blocked", "ffn_w_down": "blocked",
        "conv": "blocked"}
GATHER_GROUPS = (("mixer", ("sc_w_in", "sc_w_out", "conv")),
                 ("up0", ("ffn_w_up0",)),
                 ("down0", ("ffn_w_down0",)),
                 ("attn", ("w_dkv", "w_kr", "w_uk", "w_uv", "w_dq", "w_uq", "w_o")),
                 ("ffn1", ("ffn_w_up1", "ffn_w_down1")))
SCATTER_GROUPS = (("ffn1", (("ffn_w_up", 1), ("ffn_w_down", 1))),
                  ("attn", (("w_o", None), ("w_uq", None), ("w_dq", None), ("w_uk", None), ("w_uv", None),
                            ("w_dkv", None), ("w_kr", None))),
                  ("ffn0", (("ffn_w_up", 0), ("ffn_w_down", 0))),
                  ("mixer", (("sc_w_out", None), ("sc_w_in", None))))
SCHEDULE = {
    "begin": (("gather_start", "mixer"),),
    "mixer_ready": (("gather_start", "up0"),),
    "l0_out": (("gather_forward", "up0"), ("gather_start", "down0")),
    "f0_up": (("gather_forward", "down0"), ("gather_start", "attn")),
    "f0_act": (("gather_start", "ffn1"),),
    "f0_down": (("gather_forward", "attn"),),
    "kv_uv": (("gather_forward", "ffn1"),),
    "f1_bwd": (("scatter_sibling", "ffn1"),),
    "attn_dq": (("scatter_chips", "ffn1"),),
    "kv_bwd": (("scatter_sibling", "attn"), ("scatter_done", "ffn1")),
    "f0_dact_ew": (("scatter_chips", "attn"),),
    "f0_bwd": (("scatter_sibling", "ffn0"), ("scatter_done", "attn")),
    "sc_bwd": (("scatter_chips", "ffn0"),),
}
FINISH = (("scatter_sibling", "mixer"), ("scatter_chips", "mixer"), ("scatter_done", "ffn0"), ("scatter_done", "mixer"))
STAGES = {"gather_start": 1, "gather_forward": 2, "gather_done": 3,
          "scatter_sibling": 1, "scatter_chips": 2, "scatter_done": 3}
SMALL_W_ROWS = 24
SMALL_G_ROWS = 256


def _pack(arrays, rows):
    flat = jnp.concatenate([a.reshape(-1).astype(F32) for a in arrays])
    return jnp.pad(flat, (0, rows * 128 - flat.shape[0])).reshape(rows, 128)


def _unpack(packed, shapes):
    flat = packed.reshape(-1)
    out, off = [], 0
    for shape in shapes:
        size = 1
        for s in shape:
            size *= s
        out.append(flat[off:off + size].reshape(shape))
        off += size
    return out


def _base(name):
    if name.startswith("ffn_w_") and name[-1] in "01":
        return name[:-1], int(name[-1])
    return name, None


class _Exchange:
    def __init__(self, wts, mom, var, ffn_conv_b):
        self.wts, self.mom, self.var = wts, mom, var
        x, y, c = _place()
        self.me = 4 * x + 2 * y + c
        self.c_arr = jnp.reshape(c, (1,)).astype(jnp.int32)
        chip = 2 * x + y
        self.chip_ids = jnp.stack([chip, chip ^ 1, chip ^ 2, chip ^ 3]).astype(jnp.int32)
        self.ready = {"ffn_cb0": ffn_conv_b.reshape(2, N_FF_BLK, 1, FF_BLK)[0],
                      "ffn_cb1": ffn_conv_b.reshape(2, N_FF_BLK, 1, FF_BLK)[1]}
        self.gathers, self.group_of = {}, {}
        self.grads, self.scatters, self.results = {}, {}, {}
        for gname, names in GATHER_GROUPS:
            self.gathers[gname] = dict(stage=0, names=names, kinds=[KIND[_base(nm)[0]] for nm in names])
            for nm in names:
                self.group_of[nm] = gname
        for nm in ("sc_conv_w", "ffn_cw0", "ffn_cw1"):
            self.group_of[nm] = "mixer"
        self.at("begin", None)

    def _shard(self, name):
        if name == "conv":
            return _pack([self.wts["sc_conv_w"], self.wts["ffn_conv_w"]], SMALL_W_ROWS).reshape(1, SMALL_W_ROWS, 128)
        base, layer = _base(name)
        a = self.wts[base]
        if layer is not None:
            a = a[layer:layer + 1]
        if KIND[base] == "cols":
            return a.reshape(a.shape[-2], a.shape[-1]).astype(BF16)
        return a.reshape((-1,) + a.shape[-2:]).astype(BF16)

    def _gather_to(self, gname, stage, after):
        st = self.gathers[gname]
        if st["stage"] < 1 <= stage:
            shards = [self._shard(nm) for nm in st["names"]]
            lands = [_landing(s, kind, self.me) for s, kind in zip(shards, st["kinds"])]
            plan, ncopy = _plan_gather_chips(st["kinds"])
            st["flight"] = _copies_start(f"ag_{gname}_chips", shards, lands, ncopy, plan)
            st["stage"] = 1
        if st["stage"] < 2 <= stage:
            plan, ncopy = _plan_gather_chips(st["kinds"])
            _, lands = _copies_wait(f"ag_{gname}_chips_wait", st["flight"], ncopy, plan)
            plan, ncopy = _plan_gather_sibling(st["kinds"])
            st["flight"] = _copies_start(f"ag_{gname}_sibling", [], lands, ncopy, plan)
            st["stage"] = 2
        if st["stage"] < 3 <= stage:
            plan, ncopy = _plan_gather_sibling(st["kinds"])
            _, lands = _copies_wait(f"ag_{gname}_sibling_wait", st["flight"], ncopy, plan)
            for nm, land in zip(st["names"], lands):
                self._arrived(nm, land)
            st["stage"] = 3

    def _arrived(self, name, land):
        if name == "conv":
            conv = land.reshape(N_DEV, SMALL_W_ROWS * 128)
            self.ready["sc_conv_w"] = conv[:, :3 * 128].reshape(N_DEV, 3, 128).transpose(1, 0, 2).reshape(3, D)
            fcw = conv[:, 3 * 128:3 * 128 + 6 * 352].reshape(N_DEV, 2, 3, 352).transpose(1, 2, 0, 3)
            fcw = fcw.reshape(2, 3, N_FF_BLK, FF_BLK).transpose(0, 2, 1, 3)
            self.ready["ffn_cw0"], self.ready["ffn_cw1"] = fcw[0], fcw[1]
        elif name in ("sc_w_in", "w_uk", "w_uv") or name.startswith("ffn_w_up"):
            self.ready[name] = land
        elif name.startswith("ffn_w_down"):
            self.ready[name] = land.reshape(1, N_FF_BLK, FF_BLK, D)
        elif name == "w_kr":
            self.ready[name] = jnp.pad(land.reshape(D, QK_ROPE), ((0, 0), (0, 128 - QK_ROPE)))
        elif name == "w_uq":
            self.ready[name] = jnp.pad(land.reshape(N_HEADS, Q_LORA, QK_NOPE + QK_ROPE),
                                       ((0, 0), (0, 0), (0, QK_PAD - QK_NOPE - QK_ROPE)))
        else:
            self.ready[name] = land.reshape(D, land.shape[-1])

    def need(self, name, after):
        if name not in self.ready:
            self._gather_to(self.group_of[name], 3, after)
        return self.ready[name]

    def grad(self, name, layer, array):
        self.grads[(name, layer)] = array

    def _scatter_to(self, gname, stage, after):
        keys = dict(SCATTER_GROUPS)[gname]
        st = self.scatters.setdefault(gname, dict(stage=0))
        kinds = [KIND[nm] for nm, _ in keys]
        if st["stage"] < 1 <= stage:
            grads = [self.grads[key] for key in keys]
            lands = []
            for gr, kind in zip(grads, kinds):
                shard = (gr.shape[0],) + gr.shape[2:] if kind == "blocked" else (gr.shape[0], gr.shape[1] // N_DEV)
                lands.append(lax.empty((N_CHIP,) + shard, BF16))
            plan, ncopy = _plan_scatter_sibling(kinds)
            st["flight"] = _copies_start(f"rs_{gname}_sibling", grads, lands, ncopy, plan)
            st["stage"] = 1
        if st["stage"] < 2 <= stage:
            plan, ncopy = _plan_scatter_sibling(kinds)
            grads, recvs = _copies_wait(f"rs_{gname}_sibling_wait", st["flight"], ncopy, plan)
            sums = [_chip_sum(f"rs_{gname}_sum{t}", gr, kind, rv, self.c_arr)
                    for t, (gr, kind, rv) in enumerate(zip(grads, kinds, recvs))]
            lands = [lax.empty(s.shape, BF16) for s in sums]
            plan, ncopy = _plan_scatter_chips(len(sums))
            st["flight"] = _copies_start(f"rs_{gname}_chips", sums, lands, ncopy, plan)
            st["stage"] = 2
        if st["stage"] < 3 <= stage:
            plan, ncopy = _plan_scatter_chips(len(keys))
            sums, recvs = _copies_wait(f"rs_{gname}_chips_wait", st["flight"], ncopy, plan)
            for t, ((nm, layer), own, rv) in enumerate(zip(keys, sums, recvs)):
                nl = 1 if layer is None else 2
                rows, w = own.shape[1], own.shape[2]
                w3, m3, v3 = (src[nm].reshape(nl, rows, w) for src in (self.wts, self.mom, self.var))
                self.results[nm] = _adamw_sharded(f"adamw_{gname}_{t}", own, rv, self.chip_ids, w3, m3, v3,
                                                  0 if layer is None else layer, self.results.get(nm))
            st["stage"] = 3

    def at(self, place, after):
        for action, gname in SCHEDULE.get(place, ()):
            self._advance(action, gname, after)

    def _advance(self, action, gname, after):
        if action.startswith("gather"):
            self._gather_to(gname, STAGES[action], after)
        else:
            self._scatter_to(gname, STAGES[action], after)

    def finish(self, after):
        for action, gname in FINISH:
            self._advance(action, gname, after)
        for gname, _ in SCATTER_GROUPS:
            self._scatter_to(gname, 3, after)
        return {nm: [o.reshape(self.wts[nm].shape) for o in outs] for nm, outs in self.results.items()}


REPLICATED = ("attn_norm", "ffn_norm", "final_norm", "kv_in_norm", "kv_latent_norm", "q_latent_norm", "ffn_conv_b")
WEIGHTS = ("attn_norm", "ffn_norm", "final_norm", "sc_w_in", "sc_conv_w", "sc_w_out", "kv_in_norm", "w_dkv",
           "kv_latent_norm", "w_kr", "w_uk", "w_uv", "w_dq", "q_latent_norm", "w_uq", "w_o", "ffn_w_up", "ffn_conv_w",
           "ffn_conv_b", "ffn_w_down")


def kernel(x, positions, attn_norm, ffn_norm, final_norm, sc_w_in, sc_conv_w, sc_w_out, kv_in_norm, w_dkv, kv_latent_norm, w_kr, w_uk, w_uv, w_dq, q_latent_norm, w_uq, w_o, ffn_w_up, ffn_conv_w, ffn_conv_b, ffn_w_down, loss_target, m_attn_norm, m_ffn_norm, m_final_norm, m_sc_w_in, m_sc_conv_w, m_sc_w_out, m_kv_in_norm, m_w_dkv, m_kv_latent_norm, m_w_kr, m_w_uk, m_w_uv, m_w_dq, m_q_latent_norm, m_w_uq, m_w_o, m_ffn_w_up, m_ffn_conv_w, m_ffn_conv_b, m_ffn_w_down, v_attn_norm, v_ffn_norm, v_final_norm, v_sc_w_in, v_sc_conv_w, v_sc_w_out, v_kv_in_norm, v_w_dkv, v_kv_latent_norm, v_w_kr, v_w_uk, v_w_uv, v_w_dq, v_q_latent_norm, v_w_uq, v_w_o, v_ffn_w_up, v_ffn_conv_w, v_ffn_conv_b, v_ffn_w_down):
    wts = dict(attn_norm=attn_norm, ffn_norm=ffn_norm, final_norm=final_norm, sc_w_in=sc_w_in, sc_conv_w=sc_conv_w,
               sc_w_out=sc_w_out, kv_in_norm=kv_in_norm, w_dkv=w_dkv, kv_latent_norm=kv_latent_norm, w_kr=w_kr,
               w_uk=w_uk, w_uv=w_uv, w_dq=w_dq, q_latent_norm=q_latent_norm, w_uq=w_uq, w_o=w_o, ffn_w_up=ffn_w_up,
               ffn_conv_w=ffn_conv_w, ffn_conv_b=ffn_conv_b, ffn_w_down=ffn_w_down)
    mom = dict(attn_norm=m_attn_norm, ffn_norm=m_ffn_norm, final_norm=m_final_norm, sc_w_in=m_sc_w_in,
               sc_conv_w=m_sc_conv_w, sc_w_out=m_sc_w_out, kv_in_norm=m_kv_in_norm, w_dkv=m_w_dkv,
               kv_latent_norm=m_kv_latent_norm, w_kr=m_w_kr, w_uk=m_w_uk, w_uv=m_w_uv, w_dq=m_w_dq,
               q_latent_norm=m_q_latent_norm, w_uq=m_w_uq, w_o=m_w_o, ffn_w_up=m_ffn_w_up, ffn_conv_w=m_ffn_conv_w,
               ffn_conv_b=m_ffn_conv_b, ffn_w_down=m_ffn_w_down)
    var = dict(attn_norm=v_attn_norm, ffn_norm=v_ffn_norm, final_norm=v_final_norm, sc_w_in=v_sc_w_in,
               sc_conv_w=v_sc_conv_w, sc_w_out=v_sc_w_out, kv_in_norm=v_kv_in_norm, w_dkv=v_w_dkv,
               kv_latent_norm=v_kv_latent_norm, w_kr=v_w_kr, w_uk=v_w_uk, w_uv=v_w_uv, w_dq=v_w_dq,
               q_latent_norm=v_q_latent_norm, w_uq=v_w_uq, w_o=v_w_o, ffn_w_up=v_ffn_w_up, ffn_conv_w=v_ffn_conv_w,
               ffn_conv_b=v_ffn_conv_b, ffn_w_down=v_ffn_w_down)
    xi, yi, ci = _place()
    me = 4 * xi + 2 * yi + ci
    _Chain.last = None

    ex = _Exchange(wts, mom, var, ffn_conv_b)
    rep = {
        "attn_norm": attn_norm, "ffn_norm": ffn_norm, "final_norm": final_norm,
        "kv_in_norm": kv_in_norm.reshape(1, D), "kv_latent_norm": kv_latent_norm.reshape(1, KV_LORA),
        "q_latent_norm": q_latent_norm.reshape(1, Q_LORA),
    }
    loss, grad_x, small = _local_step(x.reshape(T, D), positions.reshape(T, 1), loss_target.reshape(T, D), rep, ex)
    results = ex.finish(grad_x)

    small_order = list(REPLICATED) + ["sc_conv_w", "ffn_conv_w"]
    packed_g = _pack([loss[0, 0:1]] + [small[nm] for nm in small_order], SMALL_G_ROWS)
    parts = _all_gather("ag_small_grads", [(packed_g.reshape(1, SMALL_G_ROWS, 128), "blocked")])[0]
    parts = parts.reshape(N_DEV, SMALL_G_ROWS, 128)

    def full_params(src):
        scw_full = jnp.zeros((3, D), F32)
        scw_full = lax.dynamic_update_slice(scw_full, src["sc_conv_w"].reshape(3, 128), (0, me * 128))
        fcw_full = jnp.zeros((2, 3, D_FF), F32)
        fcw_full = lax.dynamic_update_slice(fcw_full, src["ffn_conv_w"], (0, 0, me * 352))
        return _pack([jnp.zeros((1,), F32)] + [src[nm] for nm in REPLICATED] + [scw_full, fcw_full], SMALL_G_ROWS)

    small_out = _adamw_small(parts, full_params(wts), full_params(mom), full_params(var))
    shapes = [(1,)] + [wts[nm].shape for nm in REPLICATED] + [(3, D), (2, 3, D_FF)]
    unpacked = [_unpack(o, shapes) for o in small_out]
    loss_total = unpacked[0][0].reshape(())
    for slot, nm in enumerate(small_order):
        vals = [u[slot + 1] for u in unpacked]
        if nm == "sc_conv_w":
            vals = [lax.dynamic_slice(a, (0, me * 128), (3, 128)).reshape(1, 3, 128) for a in vals]
        elif nm == "ffn_conv_w":
            vals = [lax.dynamic_slice(a, (0, 0, me * 352), (2, 3, 352)) for a in vals]
        results[nm] = vals

    outs = [loss_total, grad_x.reshape(1, T, D)]
    for slot in range(4):
        outs.extend(results[nm][slot] for nm in WEIGHTS)
    return tuple(outs)
```

```python
import jax
import jax.numpy as jnp
from jax import lax
from jax.experimental import pallas as pl
from jax.experimental.pallas import tpu as pltpu

F32 = jnp.float32
BF16 = jnp.bfloat16

T = 2048
D = 1024
N_HEADS = 8
QK_NOPE = 128
QK_ROPE = 64
V_HEAD = 128
Q_LORA = 384
KV_LORA = 256
D_FF = 2816
CHUNK = 64
ROPE_THETA = 10000.0
EPS = 1e-6
NEG_INF = -1e30
ADAM_LR = 0.001
ADAM_B1 = 0.9
ADAM_B2 = 0.999
ADAM_EPS = 1e-08
ADAM_WD = 0.01
ADAM_STEP = 10

N_DEV = 8
N_CHIP = 4
FF_BLK = D_FF * 2 // N_DEV
N_FF_BLK = D_FF // FF_BLK
QK_PAD = 256
HALO = 16

TM = 512
TR = 256
TQ = 512
VMEM_LIMIT = 56 * 1024 * 1024

NN = (((1,), (0,)), ((), ()))
NT = (((1,), (1,)), ((), ()))
TN = (((0,), (0,)), ((), ()))
MESH = pl.DeviceIdType.MESH


def _params(sem):
    return pltpu.CompilerParams(dimension_semantics=sem, vmem_limit_bytes=VMEM_LIMIT)


ANY_SPEC = pl.BlockSpec(memory_space=pl.ANY)
VMEM_SPEC = pl.BlockSpec(memory_space=pltpu.VMEM)


class _Chain:
    last = None


def _pallas(body, *, name, in_specs, out_specs, out_shape, grid=(), scratch_shapes=(), n_prefetch=0, aliases=None,
            params=None):
    def run(*args):
        after = _Chain.last
        n_lead = len(args)
        specs, operands, fn = list(in_specs), list(args), body
        if after is not None:
            def fn(*refs):
                return body(*refs[:n_lead], *refs[n_lead + 1:])
            specs.append(ANY_SPEC)
            operands.append(after)
        kw = dict(name=name, out_shape=out_shape, input_output_aliases=aliases or {})
        if params is not None:
            kw["compiler_params"] = params
        if n_prefetch:
            kw["grid_spec"] = pltpu.PrefetchScalarGridSpec(
                num_scalar_prefetch=n_prefetch, grid=grid, in_specs=specs, out_specs=out_specs,
                scratch_shapes=scratch_shapes)
        else:
            kw.update(grid=grid, in_specs=specs, out_specs=out_specs, scratch_shapes=scratch_shapes)
        outs = pl.pallas_call(fn, **kw)(*operands)
        _Chain.last = outs[0] if isinstance(outs, (list, tuple)) else outs
        return outs
    return run


def _mm(name, a, b, *, grid, a_spec, b_spec, o_spec, o_shape, o_dtype, dims, k_axis=None, acc_shape=None,
        add=None, add_spec=None):
    nk = grid[k_axis] if k_axis is not None else 1
    has_add = add is not None

    def body(*refs):
        a_ref, b_ref = refs[0], refs[1]
        p = 2
        add_ref = None
        if has_add:
            add_ref = refs[p]
            p += 1
        o_ref = refs[p]
        p += 1
        r = lax.dot_general(a_ref[...].astype(BF16), b_ref[...].astype(BF16), dims, preferred_element_type=F32)
        if k_axis is None:
            if has_add:
                r = r + add_ref[...].astype(F32)
            o_ref[...] = r.astype(o_dtype)
        else:
            acc = refs[p]
            k = pl.program_id(k_axis)

            @pl.when(k == 0)
            def _():
                acc[...] = r

            @pl.when(k > 0)
            def _():
                acc[...] += r

            @pl.when(k == nk - 1)
            def _():
                t = acc[...]
                if has_add:
                    t = t + add_ref[...].astype(F32)
                o_ref[...] = t.astype(o_dtype)

    in_specs = [a_spec, b_spec]
    args = [a, b]
    if has_add:
        in_specs.append(add_spec if add_spec is not None else o_spec)
        args.append(add)
    sem = tuple("arbitrary" if ax == k_axis else "parallel" for ax in range(len(grid)))
    scratch = [pltpu.VMEM(acc_shape, F32)] if k_axis is not None else []
    return _pallas(body, name=name, grid=grid, in_specs=in_specs, out_specs=o_spec,
                   out_shape=jax.ShapeDtypeStruct(o_shape, o_dtype), scratch_shapes=scratch, params=_params(sem))(*args)


def _mm_rows(name, a, b, dims, o_dtype, n_out, *, tn=None, add=None):
    k = a.shape[1]
    tn = n_out if tn is None else tn
    if dims == NN:
        b_spec = pl.BlockSpec((k, tn), lambda n, i: (0, n))
    else:
        b_spec = pl.BlockSpec((tn, k), lambda n, i: (n, 0))
    return _mm(name, a, b, grid=(n_out // tn, T // TM),
               a_spec=pl.BlockSpec((TM, k), lambda n, i: (i, 0)), b_spec=b_spec,
               o_spec=pl.BlockSpec((TM, tn), lambda n, i: (i, n)), o_shape=(T, n_out), o_dtype=o_dtype,
               dims=dims, add=add)


def _mm_wgrad(name, a, b, *, tn=512):
    k, n = a.shape[1], b.shape[1]
    tn = min(tn, n)
    return _mm(name, a, b, grid=(n // tn,),
               a_spec=pl.BlockSpec((T, k), lambda j: (0, 0)), b_spec=pl.BlockSpec((T, tn), lambda j: (0, j)),
               o_spec=pl.BlockSpec((k, tn), lambda j: (0, j)), o_shape=(k, n), o_dtype=BF16, dims=TN)


def _rms_fwd(name, x, g):
    d = x.shape[1]

    def body(x_ref, g_ref, o_ref):
        xv = x_ref[...]
        r = lax.rsqrt(jnp.mean(xv * xv, axis=-1, keepdims=True) + EPS)
        o_ref[...] = ((xv * r) * g_ref[...]).astype(BF16)

    return _pallas(
        body, name=name, grid=(T // TM,),
        in_specs=[pl.BlockSpec((TM, d), lambda i: (i, 0)), pl.BlockSpec((1, d), lambda i: (0, 0))],
        out_specs=pl.BlockSpec((TM, d), lambda i: (i, 0)),
        out_shape=jax.ShapeDtypeStruct((T, d), BF16), params=_params(("parallel",)))(x, g)


def _rms_bwd(name, x, g, dy, dres=None):
    d = x.shape[1]
    has_res = dres is not None

    def body(*refs):
        if has_res:
            x_ref, g_ref, dy_ref, res_ref, dx_ref, dxb_ref, dg_ref = refs
        else:
            x_ref, g_ref, dy_ref, dx_ref, dxb_ref, dg_ref = refs
        xv = x_ref[...]
        r = lax.rsqrt(jnp.mean(xv * xv, axis=-1, keepdims=True) + EPS)
        xn = xv * r
        dyv = dy_ref[...].astype(F32)
        gdy = dyv * g_ref[...]
        dx = r * (gdy - xn * jnp.mean(gdy * xn, axis=-1, keepdims=True))
        if has_res:
            dx = dx + res_ref[...]
        dx_ref[...] = dx
        dxb_ref[...] = dx.astype(BF16)
        part = jnp.sum(dyv * xn, axis=0, keepdims=True)

        @pl.when(pl.program_id(0) == 0)
        def _():
            dg_ref[...] = part

        @pl.when(pl.program_id(0) > 0)
        def _():
            dg_ref[...] += part

    row = pl.BlockSpec((TR, d), lambda i: (i, 0))
    vec = pl.BlockSpec((1, d), lambda i: (0, 0))
    args = [x, g, dy] + ([dres] if has_res else [])
    in_specs = [row, vec, row] + ([row] if has_res else [])
    return _pallas(
        body, name=name, grid=(T // TR,), in_specs=in_specs, out_specs=[row, row, vec],
        out_shape=[jax.ShapeDtypeStruct((T, d), F32), jax.ShapeDtypeStruct((T, d), BF16),
                   jax.ShapeDtypeStruct((1, d), F32)],
        params=_params(("arbitrary",)))(*args)


def _final(h, g, tgt):
    def body(h_ref, g_ref, t_ref, loss_ref, dh_ref, dhb_ref, dg_ref):
        hv = h_ref[...]
        r = lax.rsqrt(jnp.mean(hv * hv, axis=-1, keepdims=True) + EPS)
        xn = hv * r
        gv = g_ref[...]
        err = xn * gv - t_ref[...]
        part_loss = 0.5 * jnp.sum(jnp.mean(err * err, axis=-1, keepdims=True), axis=0, keepdims=True)
        dy = err * (1.0 / D)
        gdy = dy * gv
        dh = r * (gdy - xn * jnp.mean(gdy * xn, axis=-1, keepdims=True))
        dh_ref[...] = dh
        dhb_ref[...] = dh.astype(BF16)
        part = jnp.sum(dy * xn, axis=0, keepdims=True)
        first = pl.program_id(0) == 0

        @pl.when(first)
        def _():
            dg_ref[...] = part
            loss_ref[...] = jnp.broadcast_to(part_loss, (1, 128))

        @pl.when(jnp.logical_not(first))
        def _():
            dg_ref[...] += part
            loss_ref[...] += jnp.broadcast_to(part_loss, (1, 128))

    row = pl.BlockSpec((TR, D), lambda i: (i, 0))
    vec = pl.BlockSpec((1, D), lambda i: (0, 0))
    return _pallas(
        body, name="final_loss", grid=(T // TR,), in_specs=[row, vec, row],
        out_specs=[pl.BlockSpec((1, 128), lambda i: (0, 0)), row, row, vec],
        out_shape=[jax.ShapeDtypeStruct((1, 128), F32), jax.ShapeDtypeStruct((T, D), F32),
                   jax.ShapeDtypeStruct((T, D), BF16), jax.ShapeDtypeStruct((1, D), F32)],
        params=_params(("arbitrary",)))(h, g, tgt)


def _prev_idx(i):
    return jnp.maximum(i * (TR // HALO) - 1, 0)


def _next_idx(i):
    return jnp.minimum((i + 1) * (TR // HALO), T // HALO - 1)


def _causal_taps(ext):
    return pltpu.roll(ext, 2, 0)[HALO:], pltpu.roll(ext, 1, 0)[HALO:], ext[HALO:]


def _anticausal_taps(ext, n):
    rows = ext.shape[0]
    return pltpu.roll(ext, rows - 1, 0)[:n], pltpu.roll(ext, rows - 2, 0)[:n]


def _sc_fwd(z, w):
    def body(b_ref, c_ref, ch_ref, u_ref, uh_ref, w_ref, y_ref):
        i = pl.program_id(0)
        cu = c_ref[...].astype(F32) * u_ref[...].astype(F32)
        cuh = ch_ref[...].astype(F32) * uh_ref[...].astype(F32)
        cuh = jnp.where(i > 0, cuh, 0.0)
        x2, x1, x0 = _causal_taps(jnp.concatenate([cuh, cu], axis=0))
        wv = w_ref[...]
        cv = (x2 * wv[0:1] + x1 * wv[1:2]) + x0 * wv[2:3]
        y_ref[...] = (b_ref[...].astype(F32) * cv).astype(BF16)

    def main(part):
        return pl.BlockSpec((TR, D), lambda i: (i, part))

    def halo(part):
        return pl.BlockSpec((HALO, D), lambda i: (_prev_idx(i), part))

    return _pallas(
        body, name="sc_fwd", grid=(T // TR,),
        in_specs=[main(0), main(1), halo(1), main(2), halo(2), pl.BlockSpec((3, D), lambda i: (0, 0))],
        out_specs=pl.BlockSpec((TR, D), lambda i: (i, 0)),
        out_shape=jax.ShapeDtypeStruct((T, D), BF16), params=_params(("parallel",)))(z, z, z, z, z, w)


def _sc_bwd(z, dy, w):
    last = T // TR - 1

    def body(b_ref, bn_ref, c_ref, ch_ref, u_ref, uh_ref, dy_ref, dyn_ref, w_ref, dz_ref, dw_ref):
        i = pl.program_id(0)
        cv_ = c_ref[...].astype(F32)
        uv = u_ref[...].astype(F32)
        cu = cv_ * uv
        cuh = jnp.where(i > 0, ch_ref[...].astype(F32) * uh_ref[...].astype(F32), 0.0)
        x2, x1, x0 = _causal_taps(jnp.concatenate([cuh, cu], axis=0))
        wv = w_ref[...]
        conv = (x2 * wv[0:1] + x1 * wv[1:2]) + x0 * wv[2:3]
        dyv = dy_ref[...]
        dz_ref[:, 0:D] = (dyv * conv).astype(BF16)
        dconv = dyv * b_ref[...].astype(F32)
        dconv_n = jnp.where(i < last, dyn_ref[...] * bn_ref[...].astype(F32), 0.0)
        n1, n2 = _anticausal_taps(jnp.concatenate([dconv, dconv_n], axis=0), TR)
        dcu = (dconv * wv[2:3] + n1 * wv[1:2]) + n2 * wv[0:1]
        dz_ref[:, D:2 * D] = (dcu * uv).astype(BF16)
        dz_ref[:, 2 * D:3 * D] = (dcu * cv_).astype(BF16)
        part = jnp.concatenate([jnp.sum(dconv * x2, axis=0, keepdims=True),
                                jnp.sum(dconv * x1, axis=0, keepdims=True),
                                jnp.sum(dconv * x0, axis=0, keepdims=True)], axis=0)

        @pl.when(i == 0)
        def _():
            dw_ref[...] = part

        @pl.when(i > 0)
        def _():
            dw_ref[...] += part

    def main(part):
        return pl.BlockSpec((TR, D), lambda i: (i, part))

    def prev(part):
        return pl.BlockSpec((HALO, D), lambda i: (_prev_idx(i), part))

    def nxt(part):
        return pl.BlockSpec((HALO, D), lambda i: (_next_idx(i), part))

    wspec = pl.BlockSpec((3, D), lambda i: (0, 0))
    return _pallas(
        body, name="sc_bwd", grid=(T // TR,),
        in_specs=[main(0), nxt(0), main(1), prev(1), main(2), prev(2), main(0), nxt(0), wspec],
        out_specs=[pl.BlockSpec((TR, 3 * D), lambda i: (i, 0)), wspec],
        out_shape=[jax.ShapeDtypeStruct((T, 3 * D), BF16), jax.ShapeDtypeStruct((3, D), F32)],
        params=_params(("arbitrary",)))(z, z, z, z, z, z, dy, dy, w)


def _sigmoid(x):
    return 1.0 / (1.0 + jnp.exp(-x))


def _ffn_fwd(name, gv, w, b):
    def body(g_ref, gh_ref, v_ref, w_ref, b_ref, a_ref):
        i = pl.program_id(1)
        g = g_ref[...].astype(F32)
        gh = jnp.where(i > 0, gh_ref[...].astype(F32), 0.0)
        x2, x1, x0 = _causal_taps(jnp.concatenate([gh, g], axis=0))
        wv = w_ref[...]
        gc = ((x2 * wv[0:1] + x1 * wv[1:2]) + x0 * wv[2:3]) + b_ref[...]
        a_ref[...] = ((gc * _sigmoid(gc)) * v_ref[...].astype(F32)).astype(BF16)

    blk = (None, TR, FF_BLK)
    return _pallas(
        body, name=name, grid=(N_FF_BLK, T // TR),
        in_specs=[pl.BlockSpec(blk, lambda j, i: (j, i, 0)),
                  pl.BlockSpec((None, HALO, FF_BLK), lambda j, i: (j, _prev_idx(i), 0)),
                  pl.BlockSpec(blk, lambda j, i: (j + N_FF_BLK, i, 0)),
                  pl.BlockSpec((None, 3, FF_BLK), lambda j, i: (j, 0, 0)),
                  pl.BlockSpec((None, 1, FF_BLK), lambda j, i: (j, 0, 0))],
        out_specs=pl.BlockSpec(blk, lambda j, i: (j, i, 0)),
        out_shape=jax.ShapeDtypeStruct((N_FF_BLK, T, FF_BLK), BF16),
        params=_params(("parallel", "parallel")))(gv, gv, gv, w, b)


def _ffn_bwd(name, gv, dact, w, b):
    last = T // TR - 1

    def body(g_ref, gp_ref, gn_ref, v_ref, vn_ref, da_ref, dan_ref, w_ref, b_ref, dg_ref, dv_ref, dw_ref, db_ref):
        i = pl.program_id(1)
        gp = jnp.where(i > 0, gp_ref[...].astype(F32), 0.0)
        ext = jnp.concatenate([gp, g_ref[...].astype(F32), gn_ref[...].astype(F32)], axis=0)
        x2, x1, x0 = _causal_taps(ext)
        wv = w_ref[...]
        gc = ((x2 * wv[0:1] + x1 * wv[1:2]) + x0 * wv[2:3]) + b_ref[...]
        sg = _sigmoid(gc)
        da = jnp.concatenate([da_ref[...].astype(F32), jnp.where(i < last, dan_ref[...].astype(F32), 0.0)], axis=0)
        vv = jnp.concatenate([v_ref[...].astype(F32), vn_ref[...].astype(F32)], axis=0)
        dv_ref[...] = (da[:TR] * (gc[:TR] * sg[:TR])).astype(BF16)
        dgc = (da * vv) * (sg * (1.0 + gc * (1.0 - sg)))
        n1, n2 = _anticausal_taps(dgc, TR)
        d0 = dgc[:TR]
        dg_ref[...] = ((d0 * wv[2:3] + n1 * wv[1:2]) + n2 * wv[0:1]).astype(BF16)
        part_w = jnp.concatenate([jnp.sum(d0 * x2[:TR], axis=0, keepdims=True),
                                  jnp.sum(d0 * x1[:TR], axis=0, keepdims=True),
                                  jnp.sum(d0 * x0[:TR], axis=0, keepdims=True)], axis=0)
        part_b = jnp.sum(d0, axis=0, keepdims=True)

        @pl.when(i == 0)
        def _():
            dw_ref[...] = part_w
            db_ref[...] = part_b

        @pl.when(i > 0)
        def _():
            dw_ref[...] += part_w
            db_ref[...] += part_b

    blk = (None, TR, FF_BLK)
    hblk = (None, HALO, FF_BLK)
    wspec = pl.BlockSpec((None, 3, FF_BLK), lambda j, i: (j, 0, 0))
    bspec = pl.BlockSpec((None, 1, FF_BLK), lambda j, i: (j, 0, 0))
    return _pallas(
        body, name=name, grid=(N_FF_BLK, T // TR),
        in_specs=[pl.BlockSpec(blk, lambda j, i: (j, i, 0)),
                  pl.BlockSpec(hblk, lambda j, i: (j, _prev_idx(i), 0)),
                  pl.BlockSpec(hblk, lambda j, i: (j, _next_idx(i), 0)),
                  pl.BlockSpec(blk, lambda j, i: (j + N_FF_BLK, i, 0)),
                  pl.BlockSpec(hblk, lambda j, i: (j + N_FF_BLK, _next_idx(i), 0)),
                  pl.BlockSpec(blk, lambda j, i: (j, i, 0)),
                  pl.BlockSpec(hblk, lambda j, i: (j, _next_idx(i), 0)),
                  wspec, bspec],
        out_specs=[pl.BlockSpec(blk, lambda j, i: (j, i, 0)), pl.BlockSpec(blk, lambda j, i: (j, i, 0)), wspec, bspec],
        out_shape=[jax.ShapeDtypeStruct((N_FF_BLK, T, FF_BLK), BF16), jax.ShapeDtypeStruct((N_FF_BLK, T, FF_BLK), BF16),
                   jax.ShapeDtypeStruct((N_FF_BLK, 3, FF_BLK), F32), jax.ShapeDtypeStruct((N_FF_BLK, 1, FF_BLK), F32)],
        params=_params(("parallel", "arbitrary")))(gv, gv, gv, gv, gv, dact, dact, w, b)


def _rope_tables(pos, inv_freq):
    half = QK_ROPE // 2

    def body(p_ref, f_ref, c_ref, sa_ref, sb_ref):
        ang = p_ref[...].astype(F32) * f_ref[...]
        lane = lax.broadcasted_iota(jnp.int32, (T, 128), 1)
        c = jnp.cos(ang)
        s = jnp.sin(ang)
        c_ref[...] = jnp.where(lane < 2 * half, c, 0.0)
        sa_ref[...] = jnp.where(lane < half, -s, 0.0)
        sb_ref[...] = jnp.where(jnp.logical_and(lane >= half, lane < 2 * half), s, 0.0)

    return _pallas(
        body, name="rope_tables", in_specs=[VMEM_SPEC] * 2, out_specs=[VMEM_SPEC] * 3,
        out_shape=[jax.ShapeDtypeStruct((T, 128), F32)] * 3,
        params=pltpu.CompilerParams(vmem_limit_bytes=VMEM_LIMIT))(pos, inv_freq)


def _rope(name, x, tables, sign, out_dtype, reduce_groups=False):
    g, _, w = x.shape
    cos, sa, sb = tables

    def body(x_ref, c_ref, sa_ref, sb_ref, o_ref):
        xv = x_ref[...].astype(F32)
        if reduce_groups:
            acc = xv[0]
            for k in range(1, g):
                acc = acc + xv[k]
            xv = acc
        r = xv[:, w - 128:]
        out = r * c_ref[...] + sign * (pltpu.roll(r, 96, 1) * sa_ref[...] + pltpu.roll(r, 32, 1) * sb_ref[...])
        if w > 128:
            o_ref[:, :w - 128] = xv[:, :w - 128].astype(out_dtype)
        o_ref[:, w - 128:] = out.astype(out_dtype)

    tab = pl.BlockSpec((TM, 128), lambda h, i: (i, 0))
    if reduce_groups:
        x_spec = pl.BlockSpec((g, TM, w), lambda h, i: (0, i, 0))
        groups = 1
    else:
        x_spec = pl.BlockSpec((None, TM, w), lambda h, i: (h, i, 0))
        groups = g
    return _pallas(
        body, name=name, grid=(groups, T // TM), in_specs=[x_spec, tab, tab, tab],
        out_specs=pl.BlockSpec((None, TM, w), lambda h, i: (h, i, 0)),
        out_shape=jax.ShapeDtypeStruct((groups, T, w), out_dtype),
        params=_params(("parallel", "parallel")))(x, cos, sa, sb)


SCALE = (QK_NOPE + QK_ROPE) ** -0.5
LOG2E = 1.4426950408889634
SCALE2 = SCALE * LOG2E


def _diag_mask(transposed):
    shift = CHUNK.bit_length() - 1
    a = lax.broadcasted_iota(jnp.int32, (TQ, TQ), 0) >> shift
    b = lax.broadcasted_iota(jnp.int32, (TQ, TQ), 1) >> shift
    return (a <= b) if transposed else (b <= a)


def _keys(kn_ref, kr_ref, off):
    return jnp.concatenate([kn_ref[pl.ds(off, TQ), :], kr_ref[pl.ds(off, TQ), :]], axis=1)


def _attn_fwd(q, kn, kr, v):
    def body(q_ref, kn_ref, kr_ref, v_ref, o_ref, lse_ref):
        i = pl.program_id(1)
        qv = q_ref[...]

        def step(j, carry, masked):
            m, l, acc = carry
            off = pl.multiple_of(j * TQ, TQ)
            s = lax.dot_general(qv, _keys(kn_ref, kr_ref, off), NT, preferred_element_type=F32) * SCALE2
            if masked:
                s = jnp.where(_diag_mask(False), s, NEG_INF)
            m_new = jnp.maximum(m, jnp.max(s, axis=-1, keepdims=True))
            p = jnp.exp2(s - m_new)
            alpha = jnp.exp2(m - m_new)
            l = alpha * l + jnp.sum(p, axis=-1, keepdims=True)
            acc = alpha * acc + lax.dot_general(p.astype(BF16), v_ref[pl.ds(off, TQ), :], NN, preferred_element_type=F32)
            return m_new, l, acc

        init = (jnp.full((TQ, 1), NEG_INF, F32), jnp.zeros((TQ, 1), F32), jnp.zeros((TQ, V_HEAD), F32))
        carry = lax.fori_loop(0, i, lambda j, cr: step(j, cr, False), init)
        m, l, acc = step(i, carry, True)
        o_ref[...] = (acc / l).astype(BF16)
        lse_ref[...] = m + jnp.log(l) * LOG2E

    return _pallas(
        body, name="attn_fwd", grid=(N_HEADS, T // TQ),
        in_specs=[pl.BlockSpec((None, TQ, QK_PAD), lambda h, i: (h, i, 0)),
                  pl.BlockSpec((T, QK_NOPE), lambda h, i: (0, h)),
                  pl.BlockSpec((T, 128), lambda h, i: (0, 0)),
                  pl.BlockSpec((T, V_HEAD), lambda h, i: (0, h))],
        out_specs=[pl.BlockSpec((TQ, V_HEAD), lambda h, i: (i, h)), pl.BlockSpec((None, TQ, 1), lambda h, i: (h, i, 0))],
        out_shape=[jax.ShapeDtypeStruct((T, N_HEADS * V_HEAD), BF16), jax.ShapeDtypeStruct((N_HEADS, T, 1), F32)],
        params=_params(("parallel", "parallel")))(q, kn, kr, v)


def _attn_bwd_dq(q, kn, kr, v, o, do, lse):
    def body(q_ref, kn_ref, kr_ref, v_ref, o_ref, do_ref, lse_ref, dq_ref, dl_ref):
        i = pl.program_id(1)
        qv = q_ref[...]
        dov = do_ref[...]
        lse = lse_ref[...]
        delta = jnp.sum(dov.astype(F32) * o_ref[...].astype(F32), axis=-1, keepdims=True)
        dl_ref[...] = delta

        def step(j, dq, masked):
            off = pl.multiple_of(j * TQ, TQ)
            kk = _keys(kn_ref, kr_ref, off)
            s = lax.dot_general(qv, kk, NT, preferred_element_type=F32) * SCALE2
            if masked:
                s = jnp.where(_diag_mask(False), s, NEG_INF)
            p = jnp.exp2(s - lse)
            dp = lax.dot_general(dov, v_ref[pl.ds(off, TQ), :], NT, preferred_element_type=F32)
            ds = (p * (dp - delta)) * SCALE
            return dq + lax.dot_general(ds.astype(BF16), kk, NN, preferred_element_type=F32)

        dq = lax.fori_loop(0, i, lambda j, acc: step(j, acc, False), jnp.zeros((TQ, QK_PAD), F32))
        dq_ref[...] = step(i, dq, True)

    col = pl.BlockSpec((None, TQ, 1), lambda h, i: (h, i, 0))
    head = pl.BlockSpec((TQ, V_HEAD), lambda h, i: (i, h))
    return _pallas(
        body, name="attn_bwd_dq", grid=(N_HEADS, T // TQ),
        in_specs=[pl.BlockSpec((None, TQ, QK_PAD), lambda h, i: (h, i, 0)),
                  pl.BlockSpec((T, QK_NOPE), lambda h, i: (0, h)),
                  pl.BlockSpec((T, 128), lambda h, i: (0, 0)),
                  pl.BlockSpec((T, V_HEAD), lambda h, i: (0, h)),
                  head, head, col],
        out_specs=[pl.BlockSpec((None, TQ, QK_PAD), lambda h, i: (h, i, 0)), col],
        out_shape=[jax.ShapeDtypeStruct((N_HEADS, T, QK_PAD), F32), jax.ShapeDtypeStruct((N_HEADS, T, 1), F32)],
        params=_params(("parallel", "parallel")))(q, kn, kr, v, o, do, lse)


def _attn_bwd_dkv(q, kn, kr, v, do, lse_row, delta_row):
    nq = T // TQ

    def body(q_ref, kn_ref, kr_ref, v_ref, do_ref, lse_ref, dl_ref, dkn_ref, dkr_ref, dv_ref):
        j = pl.program_id(1)
        kk = jnp.concatenate([kn_ref[...], kr_ref[...]], axis=1)
        vv = v_ref[...]

        def step(i, carry, masked):
            dk, dv = carry
            off = pl.multiple_of(i * TQ, TQ)
            qi = q_ref[pl.ds(off, TQ), :]
            doi = do_ref[pl.ds(off, TQ), :]
            st = lax.dot_general(kk, qi, NT, preferred_element_type=F32) * SCALE2
            if masked:
                st = jnp.where(_diag_mask(True), st, NEG_INF)
            pt = jnp.exp2(st - lse_ref[:, pl.ds(off, TQ)])
            dv = dv + lax.dot_general(pt.astype(BF16), doi, NN, preferred_element_type=F32)
            dpt = lax.dot_general(vv, doi, NT, preferred_element_type=F32)
            dst = (pt * (dpt - dl_ref[:, pl.ds(off, TQ)])) * SCALE
            dk = dk + lax.dot_general(dst.astype(BF16), qi, NN, preferred_element_type=F32)
            return dk, dv

        carry = step(j, (jnp.zeros((TQ, QK_PAD), F32), jnp.zeros((TQ, V_HEAD), F32)), True)
        dk, dv = lax.fori_loop(j + 1, nq, lambda i, cr: step(i, cr, False), carry)
        dkn_ref[...] = dk[:, :QK_NOPE].astype(BF16)
        dkr_ref[...] = dk[:, QK_NOPE:]
        dv_ref[...] = dv.astype(BF16)

    row = pl.BlockSpec((None, 1, T), lambda h, j: (h, 0, 0))
    head = pl.BlockSpec((TQ, 128), lambda h, j: (j, h))
    return _pallas(
        body, name="attn_bwd_dkv", grid=(N_HEADS, nq),
        in_specs=[pl.BlockSpec((None, T, QK_PAD), lambda h, j: (h, 0, 0)),
                  head, pl.BlockSpec((TQ, 128), lambda h, j: (j, 0)), head,
                  pl.BlockSpec((T, V_HEAD), lambda h, j: (0, h)), row, row],
        out_specs=[head, pl.BlockSpec((None, TQ, 128), lambda h, j: (h, j, 0)), head],
        out_shape=[jax.ShapeDtypeStruct((T, N_HEADS * QK_NOPE), BF16), jax.ShapeDtypeStruct((N_HEADS, T, 128), F32),
                   jax.ShapeDtypeStruct((T, N_HEADS * V_HEAD), BF16)],
        params=_params(("parallel", "parallel")))(q, kn, kr, v, do, lse_row, delta_row)


def _ffn_layer_fwd(tag, h, gain, ex):
    hf = _rms_fwd(f"{tag}_norm", h, gain)
    gv = _mm(f"{tag}_up", hf, ex.need(f"ffn_w_up{tag[1]}", hf), grid=(N_DEV, T // TM),
             a_spec=pl.BlockSpec((TM, D), lambda j, i: (i, 0)),
             b_spec=pl.BlockSpec((None, None, D, FF_BLK), lambda j, i: (0, j, 0, 0)),
             o_spec=pl.BlockSpec((None, TM, FF_BLK), lambda j, i: (j, i, 0)),
             o_shape=(N_DEV, T, FF_BLK), o_dtype=BF16, dims=NN)
    ex.at(f"{tag}_up", gv)
    act = _ffn_fwd(f"{tag}_act", gv, ex.need(f"ffn_cw{tag[1]}", gv), ex.need(f"ffn_cb{tag[1]}", gv))
    ex.at(f"{tag}_act", act)
    tn = 512
    out = _mm(f"{tag}_down", act, ex.need(f"ffn_w_down{tag[1]}", act), grid=(D // tn, T // TM, N_FF_BLK),
              a_spec=pl.BlockSpec((None, TM, FF_BLK), lambda n, i, k: (k, i, 0)),
              b_spec=pl.BlockSpec((None, None, FF_BLK, tn), lambda n, i, k: (0, k, 0, n)),
              o_spec=pl.BlockSpec((TM, tn), lambda n, i, k: (i, n)), o_shape=(T, D), o_dtype=F32,
              dims=NN, k_axis=2, acc_shape=(TM, tn), add=h, add_spec=pl.BlockSpec((TM, tn), lambda n, i, k: (i, n)))
    ex.at(f"{tag}_down", out)
    return out, (hf, gv, act)


def _ffn_layer_bwd(tag, h, gain, ex, saved, dh, dh_bf):
    hf, gv, act = saved
    layer = tag[1]
    w_up, w_down4 = ex.need(f"ffn_w_up{layer}", dh_bf), ex.need(f"ffn_w_down{layer}", dh_bf)
    dact = _mm(f"{tag}_dact", dh_bf, w_down4, grid=(N_FF_BLK, T // TM),
               a_spec=pl.BlockSpec((TM, D), lambda j, i: (i, 0)),
               b_spec=pl.BlockSpec((None, None, FF_BLK, D), lambda j, i: (0, j, 0, 0)),
               o_spec=pl.BlockSpec((None, TM, FF_BLK), lambda j, i: (j, i, 0)),
               o_shape=(N_FF_BLK, T, FF_BLK), o_dtype=BF16, dims=NT)
    tn = 512
    g_down = _mm(f"{tag}_gdown", act, dh_bf, grid=(N_FF_BLK, D // tn),
                 a_spec=pl.BlockSpec((None, T, FF_BLK), lambda j, n: (j, 0, 0)),
                 b_spec=pl.BlockSpec((T, tn), lambda j, n: (0, n)),
                 o_spec=pl.BlockSpec((FF_BLK, tn), lambda j, n: (j, n)),
                 o_shape=(D_FF, D), o_dtype=BF16, dims=TN)
    dg, dv, dcw, dcb = _ffn_bwd(f"{tag}_dact_ew", gv, dact, ex.need(f"ffn_cw{layer}", dact), ex.need(f"ffn_cb{layer}", dact))
    ex.at(f"{tag}_dact_ew", dg)
    dhf = None
    g_up = []
    for half, (name, dpart) in enumerate((("g", dg), ("v", dv))):
        dhf = _mm(f"{tag}_dhf_{name}", dpart, w_up, grid=(T // TM, N_FF_BLK),
                  a_spec=pl.BlockSpec((None, TM, FF_BLK), lambda i, k: (k, i, 0)),
                  b_spec=pl.BlockSpec((None, None, D, FF_BLK), lambda i, k, half=half: (0, k + half * N_FF_BLK, 0, 0)),
                  o_spec=pl.BlockSpec((TM, D), lambda i, k: (i, 0)), o_shape=(T, D), o_dtype=F32,
                  dims=NT, k_axis=1, acc_shape=(TM, D), add=dhf)
        g_up.append(_mm(f"{tag}_gup_{name}", hf, dpart, grid=(N_FF_BLK,),
                        a_spec=pl.BlockSpec((T, D), lambda j: (0, 0)),
                        b_spec=pl.BlockSpec((None, T, FF_BLK), lambda j: (j, 0, 0)),
                        o_spec=pl.BlockSpec((None, D, FF_BLK), lambda j: (j, 0, 0)),
                        o_shape=(N_FF_BLK, D, FF_BLK), o_dtype=BF16, dims=TN))
    ex.grad("ffn_w_up", int(layer), jnp.concatenate(g_up, axis=0).reshape(1, N_DEV, D, FF_BLK))
    ex.grad("ffn_w_down", int(layer), g_down.reshape(1, N_DEV, D_FF // N_DEV, D))
    dh_in, dh_in_bf, dgain = _rms_bwd(f"{tag}_dnorm", h, gain, dhf, dres=dh)
    return dh_in, dh_in_bf, dgain, dcw, dcb


def _local_step(x, pos, tgt, rep, ex):
    attn_norm, ffn_norm, final_norm = rep["attn_norm"], rep["ffn_norm"], rep["final_norm"]
    half = QK_ROPE // 2
    inv = 1.0 / (ROPE_THETA ** (jnp.arange(half, dtype=F32) / half))
    inv_freq = jnp.concatenate([inv, inv, jnp.zeros((128 - 2 * half,), F32)]).reshape(1, 128)
    tables = _rope_tables(pos, inv_freq)

    hn0 = _rms_fwd("l0_norm", x, attn_norm[0:1])
    w_in = ex.need("sc_w_in", hn0)
    ex.at("mixer_ready", hn0)
    z = _mm_rows("l0_in", hn0, w_in, NN, BF16, 3 * D, tn=512)
    ex.at("l0_in", z)
    y = _sc_fwd(z, ex.need("sc_conv_w", z))
    h1 = _mm_rows("l0_out", y, ex.need("sc_w_out", y), NN, F32, D, tn=512, add=x)
    ex.at("l0_out", h1)
    h2, ffn0 = _ffn_layer_fwd("f0", h1, ffn_norm[0:1], ex)

    hk = _rms_fwd("kv_norm", h2, rep["kv_in_norm"])
    ckv_raw = _mm_rows("kv_down", hk, ex.need("w_dkv", hk), NN, F32, KV_LORA)
    kr_raw = _mm_rows("kv_rope", hk, ex.need("w_kr", hk), NN, F32, 128)
    ckv = _rms_fwd("kv_lnorm", ckv_raw, rep["kv_latent_norm"])
    kn = _mm_rows("kv_uk", ckv, ex.need("w_uk", ckv), NN, BF16, N_HEADS * QK_NOPE)
    vv = _mm_rows("kv_uv", ckv, ex.need("w_uv", ckv), NN, BF16, N_HEADS * V_HEAD)
    ex.at("kv_uv", vv)
    kr = _rope("k_rope", kr_raw.reshape(1, T, 128), tables, 1.0, BF16).reshape(T, 128)

    hn1 = _rms_fwd("l1_norm", h2, attn_norm[1:2])
    cq_raw = _mm_rows("q_down", hn1, ex.need("w_dq", hn1), NN, F32, Q_LORA)
    cq = _rms_fwd("q_lnorm", cq_raw, rep["q_latent_norm"])
    w_uq = ex.need("w_uq", cq)
    q_raw = _mm("q_up", cq, w_uq, grid=(N_HEADS, T // TM),
                a_spec=pl.BlockSpec((TM, Q_LORA), lambda h, i: (i, 0)),
                b_spec=pl.BlockSpec((None, Q_LORA, QK_PAD), lambda h, i: (h, 0, 0)),
                o_spec=pl.BlockSpec((None, TM, QK_PAD), lambda h, i: (h, i, 0)),
                o_shape=(N_HEADS, T, QK_PAD), o_dtype=F32, dims=NN)
    q = _rope("q_rope", q_raw, tables, 1.0, BF16)
    o, lse = _attn_fwd(q, kn, kr, vv)
    w_o = ex.need("w_o", o)
    h3 = _mm_rows("attn_out", o, w_o, NN, F32, D, tn=512, add=h2)
    h4, ffn1 = _ffn_layer_fwd("f1", h3, ffn_norm[1:2], ex)

    loss, dh4, dh4_bf, d_final = _final(h4, final_norm.reshape(1, D), tgt)

    dh3, dh3_bf, d_fn1, dcw1, dcb1 = _ffn_layer_bwd("f1", h3, ffn_norm[1:2], ex, ffn1, dh4, dh4_bf)
    ex.at("f1_bwd", dh3)

    do = _mm_rows("d_attn_out", dh3_bf, w_o, NT, BF16, N_HEADS * V_HEAD)
    ex.grad("w_o", None, _mm_wgrad("g_w_o", o, dh3_bf).reshape(1, N_DEV, D // N_DEV, D))
    dq, delta = _attn_bwd_dq(q, kn, kr, vv, o, do, lse)
    ex.at("attn_dq", dq)
    dkn, dkr, dvv = _attn_bwd_dkv(q, kn, kr, vv, do, lse.reshape(N_HEADS, 1, T), delta.reshape(N_HEADS, 1, T))
    dq_pre = _rope("dq_rope", dq, tables, -1.0, BF16)
    dcq = _mm("d_q_up", dq_pre, w_uq, grid=(T // TM, N_HEADS),
              a_spec=pl.BlockSpec((None, TM, QK_PAD), lambda i, h: (h, i, 0)),
              b_spec=pl.BlockSpec((None, Q_LORA, QK_PAD), lambda i, h: (h, 0, 0)),
              o_spec=pl.BlockSpec((TM, Q_LORA), lambda i, h: (i, 0)), o_shape=(T, Q_LORA), o_dtype=F32,
              dims=NT, k_axis=1, acc_shape=(TM, Q_LORA))
    g_uq = _mm("g_w_uq", cq, dq_pre, grid=(N_HEADS,),
               a_spec=pl.BlockSpec((T, Q_LORA), lambda h: (0, 0)),
               b_spec=pl.BlockSpec((None, T, QK_PAD), lambda h: (h, 0, 0)),
               o_spec=pl.BlockSpec((None, Q_LORA, QK_PAD), lambda h: (h, 0, 0)),
               o_shape=(N_HEADS, Q_LORA, QK_PAD), o_dtype=BF16, dims=TN)
    ex.grad("w_uq", None, g_uq[:, :, :QK_NOPE + QK_ROPE].reshape(1, N_DEV, Q_LORA, QK_NOPE + QK_ROPE))
    _, dcq_raw_bf, d_qln = _rms_bwd("d_q_lnorm", cq_raw, rep["q_latent_norm"], dcq)
    dhn1 = _mm_rows("d_q_down", dcq_raw_bf, ex.need("w_dq", dcq_raw_bf), NT, F32, D)
    ex.grad("w_dq", None, _mm_wgrad("g_w_dq", hn1, dcq_raw_bf).reshape(1, N_DEV, D // N_DEV, Q_LORA))
    dh2_a, _, d_an1 = _rms_bwd("d_l1_norm", h2, attn_norm[1:2], dhn1, dres=dh3)

    dckv = _mm_rows("d_kv_uk", dkn, ex.need("w_uk", dkn), NT, F32, KV_LORA)
    dckv = _mm_rows("d_kv_uv", dvv, ex.need("w_uv", dvv), NT, F32, KV_LORA, add=dckv)
    ex.grad("w_uk", None, _mm_wgrad("g_w_uk", ckv, dkn))
    ex.grad("w_uv", None, _mm_wgrad("g_w_uv", ckv, dvv))
    _, dckv_raw_bf, d_kvln = _rms_bwd("d_kv_lnorm", ckv_raw, rep["kv_latent_norm"], dckv)
    dkr_raw_bf = _rope("dk_rope", dkr, tables, -1.0, BF16, reduce_groups=True).reshape(T, 128)
    dhk = _mm_rows("d_kv_down", dckv_raw_bf, ex.need("w_dkv", dckv_raw_bf), NT, F32, D)
    dhk = _mm_rows("d_kv_rope", dkr_raw_bf, ex.need("w_kr", dkr_raw_bf), NT, F32, D, add=dhk)
    ex.grad("w_dkv", None, _mm_wgrad("g_w_dkv", hk, dckv_raw_bf).reshape(1, N_DEV, D // N_DEV, KV_LORA))
    ex.grad("w_kr", None, _mm_wgrad("g_w_kr", hk, dkr_raw_bf)[:, :QK_ROPE].reshape(1, N_DEV, D // N_DEV, QK_ROPE))
    dh2, dh2_bf, d_kvin = _rms_bwd("d_kv_norm", h2, rep["kv_in_norm"], dhk, dres=dh2_a)
    ex.at("kv_bwd", dh2)

    dh1, dh1_bf, d_fn0, dcw0, dcb0 = _ffn_layer_bwd("f0", h1, ffn_norm[0:1], ex, ffn0, dh2, dh2_bf)
    ex.at("f0_bwd", dh1)

    dy = _mm_rows("d_l0_out", dh1_bf, ex.need("sc_w_out", dh1_bf), NT, F32, D)
    ex.grad("sc_w_out", None, _mm_wgrad("g_sc_w_out", y, dh1_bf).reshape(1, N_DEV, D // N_DEV, D))
    dz, d_scw = _sc_bwd(z, dy, ex.need("sc_conv_w", dy))
    ex.at("sc_bwd", dz)
    dhn0 = _mm_rows("d_l0_in", dz, ex.need("sc_w_in", dz), NT, F32, D)
    ex.grad("sc_w_in", None, _mm_wgrad("g_sc_w_in", hn0, dz))
    grad_x, _, d_an0 = _rms_bwd("d_l0_norm", x, attn_norm[0:1], dhn0, dres=dh1)

    small = {
        "attn_norm": jnp.concatenate([d_an0, d_an1], axis=0),
        "ffn_norm": jnp.concatenate([d_fn0, d_fn1], axis=0),
        "final_norm": d_final.reshape(D),
        "kv_in_norm": d_kvin.reshape(D),
        "kv_latent_norm": d_kvln.reshape(KV_LORA),
        "q_latent_norm": d_qln,
        "ffn_conv_b": jnp.stack([dcb0, dcb1]).transpose(0, 2, 1, 3).reshape(2, D_FF),
        "sc_conv_w": d_scw,
        "ffn_conv_w": jnp.stack([dcw0, dcw1]).transpose(0, 2, 1, 3).reshape(2, 3, D_FF),
    }
    return loss, grad_x, small


def _place():
    return lax.axis_index("x"), lax.axis_index("y"), lax.axis_index("c")


def _peers():
    x, y, c = _place()
    return (x, y, 1 - c), [(1 - x, y), (x, 1 - y), (1 - x, 1 - y)]


def _window(ref, kind, dev):
    if kind == "blocked":
        return ref.at[:, dev]
    width = ref.shape[-1] // N_DEV
    return ref.at[:, pl.ds(pl.multiple_of(dev * width, 128), width)]


def _all_gather(name, items):
    n = len(items)
    out_shapes = []
    for shard, kind in items:
        if kind == "blocked":
            shape = (shard.shape[0], N_DEV) + shard.shape[1:]
        else:
            shape = (shard.shape[0], N_DEV * shard.shape[1])
        out_shapes.append(jax.ShapeDtypeStruct(shape, shard.dtype))

    def body(*refs):
        srcs, outs = refs[:n], refs[n:2 * n]
        send_sems, recv_sems, local_sems = refs[2 * n:]
        x, y, c = _place()
        me = 4 * x + 2 * y + c
        sibling, chips = _peers()

        def num(px, py, pc):
            return 4 * px + 2 * py + pc

        def copy(t, k, dev, to, from_src):
            kind = items[t][1]
            dst = _window(outs[t], kind, dev)
            return pltpu.make_async_remote_copy(
                src_ref=srcs[t] if from_src else dst, dst_ref=dst,
                send_sem=send_sems.at[t, k], recv_sem=recv_sems.at[t, k], device_id=to, device_id_type=MESH)

        mine = [pltpu.make_async_copy(srcs[t], _window(outs[t], items[t][1], me), local_sems.at[t]) for t in range(n)]
        for cp in mine:
            cp.start()
        first = []
        for t in range(n):
            first.append(copy(t, 0, me, sibling, True))
            for j, chip in enumerate(chips):
                first.append(copy(t, 1 + j, me, (*chip, c), True))
        for cp in first:
            cp.start()
        passed = []
        for j, chip in enumerate(chips):
            for t in range(n):
                copy(t, 1 + j, num(*chip, c), (x, y, c), False).wait_recv()
                fwd = copy(t, 4 + j, num(*chip, c), sibling, False)
                fwd.start()
                passed.append(fwd)
        for t in range(n):
            copy(t, 0, num(x, y, 1 - c), (x, y, c), False).wait_recv()
            for j, chip in enumerate(chips):
                copy(t, 4 + j, num(*chip, 1 - c), (x, y, c), False).wait_recv()
        for cp in first + passed:
            cp.wait_send()
        for cp in mine:
            cp.wait()

    return _pallas(
        body, name=name, in_specs=[ANY_SPEC] * n, out_specs=[ANY_SPEC] * n, out_shape=out_shapes,
        scratch_shapes=[pltpu.SemaphoreType.DMA((n, 7)), pltpu.SemaphoreType.DMA((n, 7)), pltpu.SemaphoreType.DMA((n,))],
    )(*[s for s, _ in items])


HBM_SPEC = pl.BlockSpec(memory_space=pltpu.HBM)
SEM_SPEC = pl.BlockSpec(memory_space=pltpu.SEMAPHORE)
EFFECT = pltpu.SideEffectType.DATAFLOW_SIDE_EFFECTING
TOKEN = jax.ShapeDtypeStruct((8, 128), F32)


def _hbm(a):
    return pltpu.with_memory_space_constraint(a, pltpu.HBM)


def _copies_start(name, srcs, lands, ncopy, plan):
    ns, nl = len(srcs), len(lands)

    def body(*refs):
        send, recv, token = refs[ns + nl], refs[ns + nl + 1], refs[-1]
        copies = plan(refs[:ns], refs[ns:ns + nl])
        assert len(copies) == ncopy
        for k, (sent, dst, to, _) in enumerate(copies):
            pltpu.make_async_remote_copy(src_ref=sent, dst_ref=dst, send_sem=send.at[k], recv_sem=recv.at[k],
                                         device_id=to, device_id_type=MESH).start()
        token[...] = jnp.zeros_like(token)

    arrays = list(srcs) + list(lands)
    outs = pl.pallas_call(
        body, name=name, in_specs=[HBM_SPEC] * (ns + nl),
        out_specs=[SEM_SPEC] * 2 + [HBM_SPEC] * (ns + nl) + [VMEM_SPEC],
        out_shape=[pltpu.SemaphoreType.DMA((ncopy,))] * 2 + [pltpu.HBM(a.shape, a.dtype) for a in arrays] + [TOKEN],
        input_output_aliases={i: 2 + i for i in range(ns + nl)},
        compiler_params=pltpu.CompilerParams(has_side_effects=EFFECT))(*[_hbm(a) for a in arrays])
    _Chain.last = outs[-1]
    return outs[0], outs[1], list(outs[2:2 + ns]), list(outs[2 + ns:-1])


def _copies_wait(name, started, ncopy, plan):
    send, recv, srcs, lands = started
    ns, nl = len(srcs), len(lands)

    def body(*refs):
        send_ref, recv_ref, token = refs[ns + nl], refs[ns + nl + 1], refs[-1]
        copies = plan(refs[:ns], refs[ns:ns + nl])
        assert len(copies) == ncopy
        for k, (sent, _, to, landed) in enumerate(copies):
            cp = pltpu.make_async_remote_copy(src_ref=sent, dst_ref=landed, send_sem=send_ref.at[k],
                                              recv_sem=recv_ref.at[k], device_id=to, device_id_type=MESH)
            cp.wait_send()
            cp.wait_recv()
        token[...] = jnp.zeros_like(token)

    arrays = list(srcs) + list(lands)
    outs = pl.pallas_call(
        body, name=name, in_specs=[HBM_SPEC] * (ns + nl) + [SEM_SPEC] * 2 + [ANY_SPEC],
        out_specs=[HBM_SPEC] * (ns + nl) + [VMEM_SPEC], out_shape=[pltpu.HBM(a.shape, a.dtype) for a in arrays] + [TOKEN],
        input_output_aliases={i: i for i in range(ns + nl)},
        compiler_params=pltpu.CompilerParams(has_side_effects=EFFECT))(*arrays, send, recv, _Chain.last)
    _Chain.last = outs[-1]
    return list(outs[:ns]), list(outs[ns:-1])


def _plan_gather_chips(kinds):
    def plan(srcs, lands):
        x, y, c = _place()
        sibling, chips = _peers()
        out = []
        for t, kind in enumerate(kinds):
            mine = _window(lands[t], kind, 4 * x + 2 * y + c)
            out.append((srcs[t], mine, sibling, _window(lands[t], kind, 4 * x + 2 * y + 1 - c)))
            for px, py in chips:
                out.append((srcs[t], mine, (px, py, c), _window(lands[t], kind, 4 * px + 2 * py + c)))
        return out
    return plan, 4 * len(kinds)


def _plan_gather_sibling(kinds):
    def plan(srcs, lands):
        _, _, c = _place()
        sibling, chips = _peers()
        out = []
        for t, kind in enumerate(kinds):
            for px, py in chips:
                w = _window(lands[t], kind, 4 * px + 2 * py + c)
                out.append((w, w, sibling, _window(lands[t], kind, 4 * px + 2 * py + 1 - c)))
        return out
    return plan, 3 * len(kinds)


def _plan_scatter_sibling(kinds):
    def plan(srcs, lands):
        _, _, c = _place()
        sibling, _ = _peers()
        out = []
        for t, kind in enumerate(kinds):
            for k in range(N_CHIP):
                out.append((_window(srcs[t], kind, 2 * k + 1 - c), lands[t].at[k], sibling, lands[t].at[k]))
        return out
    return plan, N_CHIP * len(kinds)


def _plan_scatter_chips(n):
    def plan(srcs, lands):
        x, y, c = _place()
        _, chips = _peers()
        out = []
        for t in range(n):
            for px, py in chips:
                out.append((srcs[t].at[2 * px + py], lands[t].at[2 * x + y], (px, py, c), lands[t].at[2 * px + py]))
        return out
    return plan, 3 * n


def _landing(shard, kind, me):
    if kind == "blocked":
        land = lax.empty((shard.shape[0], N_DEV) + shard.shape[1:], shard.dtype)
        return lax.dynamic_update_slice(land, shard[:, None], (0, me) + (0,) * (shard.ndim - 1))
    land = lax.empty((shard.shape[0], N_DEV * shard.shape[1]), shard.dtype)
    return lax.dynamic_update_slice(land, shard, (0, me * shard.shape[1]))


def _row_tile(rows):
    for tr in (512, 384, 352, 256, 128, 64, 32, 16):
        if rows % tr == 0:
            return tr
    raise ValueError(rows)


def _chip_sum(name, gr, kind, recv, c):
    if kind == "blocked":
        nl, _, r, w = gr.shape
        rows = nl * r
        tr = _row_tile(r)
        per = r // tr
        g_spec = pl.BlockSpec((None, None, tr, w), lambda k, i, cref: (i // per, 2 * k + cref[0], i % per, 0))
    else:
        rows, w = gr.shape[0], gr.shape[1] // N_DEV
        tr = _row_tile(rows)
        g_spec = pl.BlockSpec((tr, w), lambda k, i, cref: (i, 2 * k + cref[0]))
    recv = recv.reshape(N_CHIP, rows, w)

    def body(c_ref, g_ref, r_ref, o_ref):
        del c_ref
        o_ref[...] = (g_ref[...].astype(F32) + r_ref[...].astype(F32)).astype(BF16)

    blk = pl.BlockSpec((None, tr, w), lambda k, i, cref: (k, i, 0))
    return _pallas(
        body, name=name, n_prefetch=1, grid=(N_CHIP, rows // tr), in_specs=[g_spec, blk], out_specs=blk,
        out_shape=jax.ShapeDtypeStruct((N_CHIP, rows, w), BF16),
        params=_params(("parallel", "parallel")))(c, gr, recv)


def _adamw_math(g, wv, mv, vv):
    m = ADAM_B1 * mv + (1.0 - ADAM_B1) * g
    v = ADAM_B2 * vv + (1.0 - ADAM_B2) * (g * g)
    m_hat = m / (1.0 - ADAM_B1 ** ADAM_STEP)
    v_hat = v / (1.0 - ADAM_B2 ** ADAM_STEP)
    delta = -ADAM_LR * (m_hat / (jnp.sqrt(v_hat) + ADAM_EPS) + ADAM_WD * wv)
    return delta, m, v


def _adamw_sharded(name, own, recv, chip_ids, w3, m3, v3, layer, prev):
    nl, rows, w = w3.shape
    tr = _row_tile(rows)
    has_prev = prev is not None

    def body(*refs):
        own_ref, r1_ref, r2_ref, r3_ref, w_ref, m_ref, v_ref = refs[1:8]
        g_ref, d_ref, nm_ref, nv_ref = refs[-4:]
        g = ((own_ref[...].astype(F32) + r1_ref[...].astype(F32)) + r2_ref[...].astype(F32)) + r3_ref[...].astype(F32)
        g_ref[...] = g
        d_ref[...], nm_ref[...], nv_ref[...] = _adamw_math(g, w_ref[...], m_ref[...], v_ref[...])

    def pick(slot):
        return pl.BlockSpec((None, tr, w), lambda i, ids: (ids[slot], i, 0))

    slab = pl.BlockSpec((None, tr, w), lambda i, ids: (layer, i, 0))
    in_specs = [pick(0), pick(1), pick(2), pick(3), slab, slab, slab]
    args = [chip_ids, own, recv, recv, recv, w3, m3, v3]
    aliases = {}
    if has_prev:
        in_specs += [ANY_SPEC] * 4
        aliases = {len(args) + k: k for k in range(4)}
        args += list(prev)
    return _pallas(
        body, name=name, n_prefetch=1, grid=(rows // tr,), in_specs=in_specs, out_specs=[slab] * 4,
        out_shape=[jax.ShapeDtypeStruct((nl, rows, w), F32)] * 4, aliases=aliases,
        params=_params(("parallel",)))(*args)


def _adamw_small(parts, wv, mv, vv):
    r = wv.shape[0]

    def body(p_ref, w_ref, m_ref, v_ref, g_ref, d_ref, nm_ref, nv_ref):
        g = p_ref[0]
        for k in range(1, N_DEV):
            g = g + p_ref[k]
        g_ref[...] = g
        d_ref[...], nm_ref[...], nv_ref[...] = _adamw_math(g, w_ref[...], m_ref[...], v_ref[...])

    return _pallas(
        body, name="adamw_small", in_specs=[VMEM_SPEC] * 4, out_specs=[VMEM_SPEC] * 4,
        out_shape=[jax.ShapeDtypeStruct((r, 128), F32)] * 4,
        params=pltpu.CompilerParams(vmem_limit_bytes=VMEM_LIMIT))(parts, wv, mv, vv)


KIND = {"sc_w_in": "cols", "sc_w_out": "blocked", "w_dkv": "blocked", "w_kr": "blocked", "w_uk": "cols", "w_uv": "cols",
        "w_dq": "blocked", "w_uq": "blocked", "w_o": "blocked", "ffn_w_up": "blocked", "ffn_w_down": "blocked",
        "conv": "blocked"}
GATHER_GROUPS = (("mixer", ("sc_w_in", "sc_w_out", "conv")),
                 ("up0", ("ffn_w_up0",)),
                 ("down0", ("ffn_w_down0",)),
                 ("attn", ("w_dkv", "w_kr", "w_uk", "w_uv", "w_dq", "w_uq", "w_o")),
                 ("ffn1", ("ffn_w_up1", "ffn_w_down1")))
SCATTER_GROUPS = (("ffn1", (("ffn_w_up", 1), ("ffn_w_down", 1))),
                  ("attn", (("w_o", None), ("w_uq", None), ("w_dq", None), ("w_uk", None), ("w_uv", None),
                            ("w_dkv", None), ("w_kr", None))),
                  ("ffn0", (("ffn_w_up", 0), ("ffn_w_down", 0))),
                  ("mixer", (("sc_w_out", None), ("sc_w_in", None))))
SCHEDULE = {
    "begin": (("gather_start", "mixer"),),
    "mixer_ready": (("gather_start", "up0"),),
    "l0_out": (("gather_forward", "up0"), ("gather_start", "down0")),
    "f0_up": (("gather_forward", "down0"), ("gather_start", "attn")),
    "f0_down": (("gather_forward", "attn"), ("gather_start", "ffn1")),
    "kv_uv": (("gather_forward", "ffn1"),),
    "f1_bwd": (("scatter_sibling", "ffn1"),),
    "attn_dq": (("scatter_chips", "ffn1"),),
    "kv_bwd": (("scatter_sibling", "attn"), ("scatter_done", "ffn1")),
    "f0_dact_ew": (("scatter_chips", "attn"),),
    "f0_bwd": (("scatter_sibling", "ffn0"), ("scatter_done", "attn")),
    "sc_bwd": (("scatter_chips", "ffn0"),),
}
FINISH = (("scatter_sibling", "mixer"), ("scatter_chips", "mixer"), ("scatter_done", "ffn0"), ("scatter_done", "mixer"))
STAGES = {"gather_start": 1, "gather_forward": 2, "gather_done": 3,
          "scatter_sibling": 1, "scatter_chips": 2, "scatter_done": 3}
SMALL_W_ROWS = 24
SMALL_G_ROWS = 256


def _pack(arrays, rows):
    flat = jnp.concatenate([a.reshape(-1).astype(F32) for a in arrays])
    return jnp.pad(flat, (0, rows * 128 - flat.shape[0])).reshape(rows, 128)


def _unpack(packed, shapes):
    flat = packed.reshape(-1)
    out, off = [], 0
    for shape in shapes:
        size = 1
        for s in shape:
            size *= s
        out.append(flat[off:off + size].reshape(shape))
        off += size
    return out


def _base(name):
    if name.startswith("ffn_w_") and name[-1] in "01":
        return name[:-1], int(name[-1])
    return name, None


class _Exchange:
    def __init__(self, wts, mom, var, ffn_conv_b):
        self.wts, self.mom, self.var = wts, mom, var
        x, y, c = _place()
        self.me = 4 * x + 2 * y + c
        self.c_arr = jnp.reshape(c, (1,)).astype(jnp.int32)
        chip = 2 * x + y
        self.chip_ids = jnp.stack([chip, chip ^ 1, chip ^ 2, chip ^ 3]).astype(jnp.int32)
        self.ready = {"ffn_cb0": ffn_conv_b.reshape(2, N_FF_BLK, 1, FF_BLK)[0],
                      "ffn_cb1": ffn_conv_b.reshape(2, N_FF_BLK, 1, FF_BLK)[1]}
        self.gathers, self.group_of = {}, {}
        self.grads, self.scatters, self.results = {}, {}, {}
        for gname, names in GATHER_GROUPS:
            self.gathers[gname] = dict(stage=0, names=names, kinds=[KIND[_base(nm)[0]] for nm in names])
            for nm in names:
                self.group_of[nm] = gname
        for nm in ("sc_conv_w", "ffn_cw0", "ffn_cw1"):
            self.group_of[nm] = "mixer"
        self.at("begin", None)

    def _shard(self, name):
        if name == "conv":
            return _pack([self.wts["sc_conv_w"], self.wts["ffn_conv_w"]], SMALL_W_ROWS).reshape(1, SMALL_W_ROWS, 128)
        base, layer = _base(name)
        a = self.wts[base]
        if layer is not None:
            a = a[layer:layer + 1]
        if KIND[base] == "cols":
            return a.reshape(a.shape[-2], a.shape[-1]).astype(BF16)
        return a.reshape((-1,) + a.shape[-2:]).astype(BF16)

    def _gather_to(self, gname, stage, after):
        st = self.gathers[gname]
        if st["stage"] < 1 <= stage:
            shards = [self._shard(nm) for nm in st["names"]]
            lands = [_landing(s, kind, self.me) for s, kind in zip(shards, st["kinds"])]
            plan, ncopy = _plan_gather_chips(st["kinds"])
            st["flight"] = _copies_start(f"ag_{gname}_chips", shards, lands, ncopy, plan)
            st["stage"] = 1
        if st["stage"] < 2 <= stage:
            plan, ncopy = _plan_gather_chips(st["kinds"])
            _, lands = _copies_wait(f"ag_{gname}_chips_wait", st["flight"], ncopy, plan)
            plan, ncopy = _plan_gather_sibling(st["kinds"])
            st["flight"] = _copies_start(f"ag_{gname}_sibling", [], lands, ncopy, plan)
            st["stage"] = 2
        if st["stage"] < 3 <= stage:
            plan, ncopy = _plan_gather_sibling(st["kinds"])
            _, lands = _copies_wait(f"ag_{gname}_sibling_wait", st["flight"], ncopy, plan)
            for nm, land in zip(st["names"], lands):
                self._arrived(nm, land)
            st["stage"] = 3

    def _arrived(self, name, land):
        if name == "conv":
            conv = land.reshape(N_DEV, SMALL_W_ROWS * 128)
            self.ready["sc_conv_w"] = conv[:, :3 * 128].reshape(N_DEV, 3, 128).transpose(1, 0, 2).reshape(3, D)
            fcw = conv[:, 3 * 128:3 * 128 + 6 * 352].reshape(N_DEV, 2, 3, 352).transpose(1, 2, 0, 3)
            fcw = fcw.reshape(2, 3, N_FF_BLK, FF_BLK).transpose(0, 2, 1, 3)
            self.ready["ffn_cw0"], self.ready["ffn_cw1"] = fcw[0], fcw[1]
        elif name in ("sc_w_in", "w_uk", "w_uv") or name.startswith("ffn_w_up"):
            self.ready[name] = land
        elif name.startswith("ffn_w_down"):
            self.ready[name] = land.reshape(1, N_FF_BLK, FF_BLK, D)
        elif name == "w_kr":
            self.ready[name] = jnp.pad(land.reshape(D, QK_ROPE), ((0, 0), (0, 128 - QK_ROPE)))
        elif name == "w_uq":
            self.ready[name] = jnp.pad(land.reshape(N_HEADS, Q_LORA, QK_NOPE + QK_ROPE),
                                       ((0, 0), (0, 0), (0, QK_PAD - QK_NOPE - QK_ROPE)))
        else:
            self.ready[name] = land.reshape(D, land.shape[-1])

    def need(self, name, after):
        if name not in self.ready:
            self._gather_to(self.group_of[name], 3, after)
        return self.ready[name]

    def grad(self, name, layer, array):
        self.grads[(name, layer)] = array

    def _scatter_to(self, gname, stage, after):
        keys = dict(SCATTER_GROUPS)[gname]
        st = self.scatters.setdefault(gname, dict(stage=0))
        kinds = [KIND[nm] for nm, _ in keys]
        if st["stage"] < 1 <= stage:
            grads = [self.grads[key] for key in keys]
            lands = []
            for gr, kind in zip(grads, kinds):
                shard = (gr.shape[0],) + gr.shape[2:] if kind == "blocked" else (gr.shape[0], gr.shape[1] // N_DEV)
                lands.append(lax.empty((N_CHIP,) + shard, BF16))
            plan, ncopy = _plan_scatter_sibling(kinds)
            st["flight"] = _copies_start(f"rs_{gname}_sibling", grads, lands, ncopy, plan)
            st["stage"] = 1
        if st["stage"] < 2 <= stage:
            plan, ncopy = _plan_scatter_sibling(kinds)
            grads, recvs = _copies_wait(f"rs_{gname}_sibling_wait", st["flight"], ncopy, plan)
            sums = [_chip_sum(f"rs_{gname}_sum{t}", gr, kind, rv, self.c_arr)
                    for t, (gr, kind, rv) in enumerate(zip(grads, kinds, recvs))]
            lands = [lax.empty(s.shape, BF16) for s in sums]
            plan, ncopy = _plan_scatter_chips(len(sums))
            st["flight"] = _copies_start(f"rs_{gname}_chips", sums, lands, ncopy, plan)
            st["stage"] = 2
        if st["stage"] < 3 <= stage:
            plan, ncopy = _plan_scatter_chips(len(keys))
            sums, recvs = _copies_wait(f"rs_{gname}_chips_wait", st["flight"], ncopy, plan)
            for t, ((nm, layer), own, rv) in enumerate(zip(keys, sums, recvs)):
                nl = 1 if layer is None else 2
                rows, w = own.shape[1], own.shape[2]
                w3, m3, v3 = (src[nm].reshape(nl, rows, w) for src in (self.wts, self.mom, self.var))
                self.results[nm] = _adamw_sharded(f"adamw_{gname}_{t}", own, rv, self.chip_ids, w3, m3, v3,
                                                  0 if layer is None else layer, self.results.get(nm))
            st["stage"] = 3

    def at(self, place, after):
        for action, gname in SCHEDULE.get(place, ()):
            self._advance(action, gname, after)

    def _advance(self, action, gname, after):
        if action.startswith("gather"):
            self._gather_to(gname, STAGES[action], after)
        else:
            self._scatter_to(gname, STAGES[action], after)

    def finish(self, after):
        for action, gname in FINISH:
            self._advance(action, gname, after)
        for gname, _ in SCATTER_GROUPS:
            self._scatter_to(gname, 3, after)
        return {nm: [o.reshape(self.wts[nm].shape) for o in outs] for nm, outs in self.results.items()}


REPLICATED = ("attn_norm", "ffn_norm", "final_norm", "kv_in_norm", "kv_latent_norm", "q_latent_norm", "ffn_conv_b")
WEIGHTS = ("attn_norm", "ffn_norm", "final_norm", "sc_w_in", "sc_conv_w", "sc_w_out", "kv_in_norm", "w_dkv",
           "kv_latent_norm", "w_kr", "w_uk", "w_uv", "w_dq", "q_latent_norm", "w_uq", "w_o", "ffn_w_up", "ffn_conv_w",
           "ffn_conv_b", "ffn_w_down")


def kernel(x, positions, attn_norm, ffn_norm, final_norm, sc_w_in, sc_conv_w, sc_w_out, kv_in_norm, w_dkv, kv_latent_norm, w_kr, w_uk, w_uv, w_dq, q_latent_norm, w_uq, w_o, ffn_w_up, ffn_conv_w, ffn_conv_b, ffn_w_down, loss_target, m_attn_norm, m_ffn_norm, m_final_norm, m_sc_w_in, m_sc_conv_w, m_sc_w_out, m_kv_in_norm, m_w_dkv, m_kv_latent_norm, m_w_kr, m_w_uk, m_w_uv, m_w_dq, m_q_latent_norm, m_w_uq, m_w_o, m_ffn_w_up, m_ffn_conv_w, m_ffn_conv_b, m_ffn_w_down, v_attn_norm, v_ffn_norm, v_final_norm, v_sc_w_in, v_sc_conv_w, v_sc_w_out, v_kv_in_norm, v_w_dkv, v_kv_latent_norm, v_w_kr, v_w_uk, v_w_uv, v_w_dq, v_q_latent_norm, v_w_uq, v_w_o, v_ffn_w_up, v_ffn_conv_w, v_ffn_conv_b, v_ffn_w_down):
    wts = dict(attn_norm=attn_norm, ffn_norm=ffn_norm, final_norm=final_norm, sc_w_in=sc_w_in, sc_conv_w=sc_conv_w,
               sc_w_out=sc_w_out, kv_in_norm=kv_in_norm, w_dkv=w_dkv, kv_latent_norm=kv_latent_norm, w_kr=w_kr,
               w_uk=w_uk, w_uv=w_uv, w_dq=w_dq, q_latent_norm=q_latent_norm, w_uq=w_uq, w_o=w_o, ffn_w_up=ffn_w_up,
               ffn_conv_w=ffn_conv_w, ffn_conv_b=ffn_conv_b, ffn_w_down=ffn_w_down)
    mom = dict(attn_norm=m_attn_norm, ffn_norm=m_ffn_norm, final_norm=m_final_norm, sc_w_in=m_sc_w_in,
               sc_conv_w=m_sc_conv_w, sc_w_out=m_sc_w_out, kv_in_norm=m_kv_in_norm, w_dkv=m_w_dkv,
               kv_latent_norm=m_kv_latent_norm, w_kr=m_w_kr, w_uk=m_w_uk, w_uv=m_w_uv, w_dq=m_w_dq,
               q_latent_norm=m_q_latent_norm, w_uq=m_w_uq, w_o=m_w_o, ffn_w_up=m_ffn_w_up, ffn_conv_w=m_ffn_conv_w,
               ffn_conv_b=m_ffn_conv_b, ffn_w_down=m_ffn_w_down)
    var = dict(attn_norm=v_attn_norm, ffn_norm=v_ffn_norm, final_norm=v_final_norm, sc_w_in=v_sc_w_in,
               sc_conv_w=v_sc_conv_w, sc_w_out=v_sc_w_out, kv_in_norm=v_kv_in_norm, w_dkv=v_w_dkv,
               kv_latent_norm=v_kv_latent_norm, w_kr=v_w_kr, w_uk=v_w_uk, w_uv=v_w_uv, w_dq=v_w_dq,
               q_latent_norm=v_q_latent_norm, w_uq=v_w_uq, w_o=v_w_o, ffn_w_up=v_ffn_w_up, ffn_conv_w=v_ffn_conv_w,
               ffn_conv_b=v_ffn_conv_b, ffn_w_down=v_ffn_w_down)
    xi, yi, ci = _place()
    me = 4 * xi + 2 * yi + ci
    _Chain.last = None

    ex = _Exchange(wts, mom, var, ffn_conv_b)
    rep = {
        "attn_norm": attn_norm, "ffn_norm": ffn_norm, "final_norm": final_norm,
        "kv_in_norm": kv_in_norm.reshape(1, D), "kv_latent_norm": kv_latent_norm.reshape(1, KV_LORA),
        "q_latent_norm": q_latent_norm.reshape(1, Q_LORA),
    }
    loss, grad_x, small = _local_step(x.reshape(T, D), positions.reshape(T, 1), loss_target.reshape(T, D), rep, ex)
    results = ex.finish(grad_x)

    small_order = list(REPLICATED) + ["sc_conv_w", "ffn_conv_w"]
    packed_g = _pack([loss[0, 0:1]] + [small[nm] for nm in small_order], SMALL_G_ROWS)
    parts = _all_gather("ag_small_grads", [(packed_g.reshape(1, SMALL_G_ROWS, 128), "blocked")])[0]
    parts = parts.reshape(N_DEV, SMALL_G_ROWS, 128)

    def full_params(src):
        scw_full = jnp.zeros((3, D), F32)
        scw_full = lax.dynamic_update_slice(scw_full, src["sc_conv_w"].reshape(3, 128), (0, me * 128))
        fcw_full = jnp.zeros((2, 3, D_FF), F32)
        fcw_full = lax.dynamic_update_slice(fcw_full, src["ffn_conv_w"], (0, 0, me * 352))
        return _pack([jnp.zeros((1,), F32)] + [src[nm] for nm in REPLICATED] + [scw_full, fcw_full], SMALL_G_ROWS)

    small_out = _adamw_small(parts, full_params(wts), full_params(mom), full_params(var))
    shapes = [(1,)] + [wts[nm].shape for nm in REPLICATED] + [(3, D), (2, 3, D_FF)]
    unpacked = [_unpack(o, shapes) for o in small_out]
    loss_total = unpacked[0][0].reshape(())
    for slot, nm in enumerate(small_order):
        vals = [u[slot + 1] for u in unpacked]
        if nm == "sc_conv_w":
            vals = [lax.dynamic_slice(a, (0, me * 128), (3, 128)).reshape(1, 3, 128) for a in vals]
        elif nm == "ffn_conv_w":
            vals = [lax.dynamic_slice(a, (0, 0, me * 352), (2, 3, 352)) for a in vals]
        results[nm] = vals

    outs = [loss_total, grad_x.reshape(1, T, D)]
    for slot in range(4):
        outs.extend(results[nm][slot] for nm in WEIGHTS)
    return tuple(outs)
```

```python
import jax
import jax.numpy as jnp
from jax import lax
from jax.experimental import pallas as pl
from jax.experimental.pallas import tpu as pltpu

F32 = jnp.float32
BF16 = jnp.bfloat16

T = 2048
D = 1024
N_HEADS = 8
QK_NOPE = 128
QK_ROPE = 64
V_HEAD = 128
Q_LORA = 384
KV_LORA = 256
D_FF = 2816
CHUNK = 64
ROPE_THETA = 10000.0
EPS = 1e-6
NEG_INF = -1e30
ADAM_LR = 0.001
ADAM_B1 = 0.9
ADAM_B2 = 0.999
ADAM_EPS = 1e-08
ADAM_WD = 0.01
ADAM_STEP = 10

N_DEV = 8
N_CHIP = 4
FF_BLK = D_FF * 2 // N_DEV
N_FF_BLK = D_FF // FF_BLK
QK_PAD = 256
HALO = 16

TM = 1024
TS = 512
TR = 256
TQ = 512
VMEM_LIMIT = 56 * 1024 * 1024

NN = (((1,), (0,)), ((), ()))
NT = (((1,), (1,)), ((), ()))
TN = (((0,), (0,)), ((), ()))
MESH = pl.DeviceIdType.MESH


def _params(sem):
    return pltpu.CompilerParams(dimension_semantics=sem, vmem_limit_bytes=VMEM_LIMIT)


ANY_SPEC = pl.BlockSpec(memory_space=pl.ANY)
VMEM_SPEC = pl.BlockSpec(memory_space=pltpu.VMEM)


class _Chain:
    last = None


def _pallas(body, *, name, in_specs, out_specs, out_shape, grid=(), scratch_shapes=(), n_prefetch=0, aliases=None,
            params=None):
    def run(*args):
        after = _Chain.last
        n_lead = len(args)
        specs, operands, fn = list(in_specs), list(args), body
        if after is not None:
            def fn(*refs):
                return body(*refs[:n_lead], *refs[n_lead + 1:])
            specs.append(ANY_SPEC)
            operands.append(after)
        kw = dict(name=name, out_shape=out_shape, input_output_aliases=aliases or {})
        if params is not None:
            kw["compiler_params"] = params
        if n_prefetch:
            kw["grid_spec"] = pltpu.PrefetchScalarGridSpec(
                num_scalar_prefetch=n_prefetch, grid=grid, in_specs=specs, out_specs=out_specs,
                scratch_shapes=scratch_shapes)
        else:
            kw.update(grid=grid, in_specs=specs, out_specs=out_specs, scratch_shapes=scratch_shapes)
        outs = pl.pallas_call(fn, **kw)(*operands)
        _Chain.last = outs[0] if isinstance(outs, (list, tuple)) else outs
        return outs
    return run


def _mm(name, a, b, *, grid, a_spec, b_spec, o_spec, o_shape, o_dtype, dims, k_axis=None, acc_shape=None,
        add=None, add_spec=None):
    nk = grid[k_axis] if k_axis is not None else 1
    has_add = add is not None

    def body(*refs):
        a_ref, b_ref = refs[0], refs[1]
        p = 2
        add_ref = None
        if has_add:
            add_ref = refs[p]
            p += 1
        o_ref = refs[p]
        p += 1
        r = lax.dot_general(a_ref[...].astype(BF16), b_ref[...].astype(BF16), dims, preferred_element_type=F32)
        if k_axis is None:
            if has_add:
                r = r + add_ref[...].astype(F32)
            o_ref[...] = r.astype(o_dtype)
        else:
            acc = refs[p]
            k = pl.program_id(k_axis)

            @pl.when(k == 0)
            def _():
                acc[...] = r

            @pl.when(k > 0)
            def _():
                acc[...] += r

            @pl.when(k == nk - 1)
            def _():
                t = acc[...]
                if has_add:
                    t = t + add_ref[...].astype(F32)
                o_ref[...] = t.astype(o_dtype)

    in_specs = [a_spec, b_spec]
    args = [a, b]
    if has_add:
        in_specs.append(add_spec if add_spec is not None else o_spec)
        args.append(add)
    sem = tuple("arbitrary" if ax == k_axis else "parallel" for ax in range(len(grid)))
    scratch = [pltpu.VMEM(acc_shape, F32)] if k_axis is not None else []
    return _pallas(body, name=name, grid=grid, in_specs=in_specs, out_specs=o_spec,
                   out_shape=jax.ShapeDtypeStruct(o_shape, o_dtype), scratch_shapes=scratch, params=_params(sem))(*args)


def _mm_sum(name, parts, *, grid, o_spec, o_shape, o_dtype, add=None):
    has_add = add is not None

    def body(*refs):
        o_ref = refs[-1]
        acc = None
        for p, (_, _, _, _, dims) in enumerate(parts):
            a_ref, b_ref = refs[2 * p], refs[2 * p + 1]
            for k in range(a_ref.shape[0]):
                r = lax.dot_general(a_ref[k], b_ref[k], dims, preferred_element_type=F32)
                acc = r if acc is None else acc + r
        if has_add:
            acc = acc + refs[2 * len(parts)][...]
        o_ref[...] = acc.astype(o_dtype)

    in_specs, args = [], []
    for a, a_spec, b, b_spec, _ in parts:
        in_specs += [a_spec, b_spec]
        args += [a, b]
    if has_add:
        in_specs.append(o_spec)
        args.append(add)
    return _pallas(body, name=name, grid=grid, in_specs=in_specs, out_specs=o_spec,
                   out_shape=jax.ShapeDtypeStruct(o_shape, o_dtype),
                   params=_params(("parallel",) * len(grid)))(*args)


def _mm_rows(name, a, b, dims, o_dtype, n_out, *, tn=None, add=None):
    k = a.shape[1]
    tn = n_out if tn is None else tn
    if dims == NN:
        b_spec = pl.BlockSpec((k, tn), lambda n, i: (0, n))
    else:
        b_spec = pl.BlockSpec((tn, k), lambda n, i: (n, 0))
    return _mm(name, a, b, grid=(n_out // tn, T // TM),
               a_spec=pl.BlockSpec((TM, k), lambda n, i: (i, 0)), b_spec=b_spec,
               o_spec=pl.BlockSpec((TM, tn), lambda n, i: (i, n)), o_shape=(T, n_out), o_dtype=o_dtype,
               dims=dims, add=add)


def _mm_wgrad(name, a, b, *, tn=512):
    k, n = a.shape[1], b.shape[1]
    tn = min(tn, n)
    return _mm(name, a, b, grid=(n // tn,),
               a_spec=pl.BlockSpec((T, k), lambda j: (0, 0)), b_spec=pl.BlockSpec((T, tn), lambda j: (0, j)),
               o_spec=pl.BlockSpec((k, tn), lambda j: (0, j)), o_shape=(k, n), o_dtype=BF16, dims=TN)


def _rms_fwd(name, x, g):
    d = x.shape[1]

    def body(x_ref, g_ref, o_ref):
        xv = x_ref[...]
        r = lax.rsqrt(jnp.mean(xv * xv, axis=-1, keepdims=True) + EPS)
        o_ref[...] = ((xv * r) * g_ref[...]).astype(BF16)

    return _pallas(
        body, name=name, grid=(T // TM,),
        in_specs=[pl.BlockSpec((TM, d), lambda i: (i, 0)), pl.BlockSpec((1, d), lambda i: (0, 0))],
        out_specs=pl.BlockSpec((TM, d), lambda i: (i, 0)),
        out_shape=jax.ShapeDtypeStruct((T, d), BF16), params=_params(("parallel",)))(x, g)


def _rms_bwd(name, x, g, dy, dres=None):
    d = x.shape[1]
    has_res = dres is not None

    def body(*refs):
        if has_res:
            x_ref, g_ref, dy_ref, res_ref, dx_ref, dxb_ref, dg_ref = refs
        else:
            x_ref, g_ref, dy_ref, dx_ref, dxb_ref, dg_ref = refs
        xv = x_ref[...]
        r = lax.rsqrt(jnp.mean(xv * xv, axis=-1, keepdims=True) + EPS)
        xn = xv * r
        dyv = dy_ref[...].astype(F32)
        gdy = dyv * g_ref[...]
        dx = r * (gdy - xn * jnp.mean(gdy * xn, axis=-1, keepdims=True))
        if has_res:
            dx = dx + res_ref[...]
        dx_ref[...] = dx
        dxb_ref[...] = dx.astype(BF16)
        part = jnp.sum(dyv * xn, axis=0, keepdims=True)

        @pl.when(pl.program_id(0) == 0)
        def _():
            dg_ref[...] = part

        @pl.when(pl.program_id(0) > 0)
        def _():
            dg_ref[...] += part

    row = pl.BlockSpec((TR, d), lambda i: (i, 0))
    vec = pl.BlockSpec((1, d), lambda i: (0, 0))
    args = [x, g, dy] + ([dres] if has_res else [])
    in_specs = [row, vec, row] + ([row] if has_res else [])
    return _pallas(
        body, name=name, grid=(T // TR,), in_specs=in_specs, out_specs=[row, row, vec],
        out_shape=[jax.ShapeDtypeStruct((T, d), F32), jax.ShapeDtypeStruct((T, d), BF16),
                   jax.ShapeDtypeStruct((1, d), F32)],
        params=_params(("arbitrary",)))(*args)


def _final(h, g, tgt):
    def body(h_ref, g_ref, t_ref, loss_ref, dh_ref, dhb_ref, dg_ref):
        hv = h_ref[...]
        r = lax.rsqrt(jnp.mean(hv * hv, axis=-1, keepdims=True) + EPS)
        xn = hv * r
        gv = g_ref[...]
        err = xn * gv - t_ref[...]
        part_loss = 0.5 * jnp.sum(jnp.mean(err * err, axis=-1, keepdims=True), axis=0, keepdims=True)
        dy = err * (1.0 / D)
        gdy = dy * gv
        dh = r * (gdy - xn * jnp.mean(gdy * xn, axis=-1, keepdims=True))
        dh_ref[...] = dh
        dhb_ref[...] = dh.astype(BF16)
        part = jnp.sum(dy * xn, axis=0, keepdims=True)
        first = pl.program_id(0) == 0

        @pl.when(first)
        def _():
            dg_ref[...] = part
            loss_ref[...] = jnp.broadcast_to(part_loss, (1, 128))

        @pl.when(jnp.logical_not(first))
        def _():
            dg_ref[...] += part
            loss_ref[...] += jnp.broadcast_to(part_loss, (1, 128))

    row = pl.BlockSpec((TR, D), lambda i: (i, 0))
    vec = pl.BlockSpec((1, D), lambda i: (0, 0))
    return _pallas(
        body, name="final_loss", grid=(T // TR,), in_specs=[row, vec, row],
        out_specs=[pl.BlockSpec((1, 128), lambda i: (0, 0)), row, row, vec],
        out_shape=[jax.ShapeDtypeStruct((1, 128), F32), jax.ShapeDtypeStruct((T, D), F32),
                   jax.ShapeDtypeStruct((T, D), BF16), jax.ShapeDtypeStruct((1, D), F32)],
        params=_params(("arbitrary",)))(h, g, tgt)


def _prev_idx(i):
    return jnp.maximum(i * (TR // HALO) - 1, 0)


def _next_idx(i):
    return jnp.minimum((i + 1) * (TR // HALO), T // HALO - 1)


def _causal_taps(ext):
    return pltpu.roll(ext, 2, 0)[HALO:], pltpu.roll(ext, 1, 0)[HALO:], ext[HALO:]


def _anticausal_taps(ext, n):
    rows = ext.shape[0]
    return pltpu.roll(ext, rows - 1, 0)[:n], pltpu.roll(ext, rows - 2, 0)[:n]


def _sc_fwd(z, w):
    def body(b_ref, c_ref, ch_ref, u_ref, uh_ref, w_ref, y_ref):
        i = pl.program_id(0)
        cu = c_ref[...].astype(F32) * u_ref[...].astype(F32)
        cuh = ch_ref[...].astype(F32) * uh_ref[...].astype(F32)
        cuh = jnp.where(i > 0, cuh, 0.0)
        x2, x1, x0 = _causal_taps(jnp.concatenate([cuh, cu], axis=0))
        wv = w_ref[...]
        cv = (x2 * wv[0:1] + x1 * wv[1:2]) + x0 * wv[2:3]
        y_ref[...] = (b_ref[...].astype(F32) * cv).astype(BF16)

    def main(part):
        return pl.BlockSpec((TR, D), lambda i: (i, part))

    def halo(part):
        return pl.BlockSpec((HALO, D), lambda i: (_prev_idx(i), part))

    return _pallas(
        body, name="sc_fwd", grid=(T // TR,),
        in_specs=[main(0), main(1), halo(1), main(2), halo(2), pl.BlockSpec((3, D), lambda i: (0, 0))],
        out_specs=pl.BlockSpec((TR, D), lambda i: (i, 0)),
        out_shape=jax.ShapeDtypeStruct((T, D), BF16), params=_params(("parallel",)))(z, z, z, z, z, w)


def _sc_bwd(z, dy, w):
    last = T // TR - 1

    def body(b_ref, bn_ref, c_ref, ch_ref, u_ref, uh_ref, dy_ref, dyn_ref, w_ref, dz_ref, dw_ref):
        i = pl.program_id(0)
        cv_ = c_ref[...].astype(F32)
        uv = u_ref[...].astype(F32)
        cu = cv_ * uv
        cuh = jnp.where(i > 0, ch_ref[...].astype(F32) * uh_ref[...].astype(F32), 0.0)
        x2, x1, x0 = _causal_taps(jnp.concatenate([cuh, cu], axis=0))
        wv = w_ref[...]
        conv = (x2 * wv[0:1] + x1 * wv[1:2]) + x0 * wv[2:3]
        dyv = dy_ref[...]
        dz_ref[:, 0:D] = (dyv * conv).astype(BF16)
        dconv = dyv * b_ref[...].astype(F32)
        dconv_n = jnp.where(i < last, dyn_ref[...] * bn_ref[...].astype(F32), 0.0)
        n1, n2 = _anticausal_taps(jnp.concatenate([dconv, dconv_n], axis=0), TR)
        dcu = (dconv * wv[2:3] + n1 * wv[1:2]) + n2 * wv[0:1]
        dz_ref[:, D:2 * D] = (dcu * uv).astype(BF16)
        dz_ref[:, 2 * D:3 * D] = (dcu * cv_).astype(BF16)
        part = jnp.concatenate([jnp.sum(dconv * x2, axis=0, keepdims=True),
                                jnp.sum(dconv * x1, axis=0, keepdims=True),
                                jnp.sum(dconv * x0, axis=0, keepdims=True)], axis=0)

        @pl.when(i == 0)
        def _():
            dw_ref[...] = part

        @pl.when(i > 0)
        def _():
            dw_ref[...] += part

    def main(part):
        return pl.BlockSpec((TR, D), lambda i: (i, part))

    def prev(part):
        return pl.BlockSpec((HALO, D), lambda i: (_prev_idx(i), part))

    def nxt(part):
        return pl.BlockSpec((HALO, D), lambda i: (_next_idx(i), part))

    wspec = pl.BlockSpec((3, D), lambda i: (0, 0))
    return _pallas(
        body, name="sc_bwd", grid=(T // TR,),
        in_specs=[main(0), nxt(0), main(1), prev(1), main(2), prev(2), main(0), nxt(0), wspec],
        out_specs=[pl.BlockSpec((TR, 3 * D), lambda i: (i, 0)), wspec],
        out_shape=[jax.ShapeDtypeStruct((T, 3 * D), BF16), jax.ShapeDtypeStruct((3, D), F32)],
        params=_params(("arbitrary",)))(z, z, z, z, z, z, dy, dy, w)


def _sigmoid(x):
    return 1.0 / (1.0 + jnp.exp(-x))


def _ffn_fwd(name, gv, w, b):
    def body(g_ref, gh_ref, v_ref, w_ref, b_ref, a_ref):
        i = pl.program_id(1)
        g = g_ref[...].astype(F32)
        gh = jnp.where(i > 0, gh_ref[...].astype(F32), 0.0)
        x2, x1, x0 = _causal_taps(jnp.concatenate([gh, g], axis=0))
        wv = w_ref[...]
        gc = ((x2 * wv[0:1] + x1 * wv[1:2]) + x0 * wv[2:3]) + b_ref[...]
        a_ref[...] = ((gc * _sigmoid(gc)) * v_ref[...].astype(F32)).astype(BF16)

    blk = (None, TR, FF_BLK)
    return _pallas(
        body, name=name, grid=(N_FF_BLK, T // TR),
        in_specs=[pl.BlockSpec(blk, lambda j, i: (j, i, 0)),
                  pl.BlockSpec((None, HALO, FF_BLK), lambda j, i: (j, _prev_idx(i), 0)),
                  pl.BlockSpec(blk, lambda j, i: (j + N_FF_BLK, i, 0)),
                  pl.BlockSpec((None, 3, FF_BLK), lambda j, i: (j, 0, 0)),
                  pl.BlockSpec((None, 1, FF_BLK), lambda j, i: (j, 0, 0))],
        out_specs=pl.BlockSpec(blk, lambda j, i: (j, i, 0)),
        out_shape=jax.ShapeDtypeStruct((N_FF_BLK, T, FF_BLK), BF16),
        params=_params(("parallel", "parallel")))(gv, gv, gv, w, b)


def _ffn_bwd(name, gv, dact, w, b):
    last = T // TR - 1

    def body(g_ref, gp_ref, gn_ref, v_ref, vn_ref, da_ref, dan_ref, w_ref, b_ref, dg_ref, dv_ref, dw_ref, db_ref):
        i = pl.program_id(1)
        gp = jnp.where(i > 0, gp_ref[...].astype(F32), 0.0)
        ext = jnp.concatenate([gp, g_ref[...].astype(F32), gn_ref[...].astype(F32)], axis=0)
        x2, x1, x0 = _causal_taps(ext)
        wv = w_ref[...]
        gc = ((x2 * wv[0:1] + x1 * wv[1:2]) + x0 * wv[2:3]) + b_ref[...]
        sg = _sigmoid(gc)
        da = jnp.concatenate([da_ref[...].astype(F32), jnp.where(i < last, dan_ref[...].astype(F32), 0.0)], axis=0)
        vv = jnp.concatenate([v_ref[...].astype(F32), vn_ref[...].astype(F32)], axis=0)
        dv_ref[...] = (da[:TR] * (gc[:TR] * sg[:TR])).astype(BF16)
        dgc = (da * vv) * (sg * (1.0 + gc * (1.0 - sg)))
        n1, n2 = _anticausal_taps(dgc, TR)
        d0 = dgc[:TR]
        dg_ref[...] = ((d0 * wv[2:3] + n1 * wv[1:2]) + n2 * wv[0:1]).astype(BF16)
        part_w = jnp.concatenate([jnp.sum(d0 * x2[:TR], axis=0, keepdims=True),
                                  jnp.sum(d0 * x1[:TR], axis=0, keepdims=True),
                                  jnp.sum(d0 * x0[:TR], axis=0, keepdims=True)], axis=0)
        part_b = jnp.sum(d0, axis=0, keepdims=True)

        @pl.when(i == 0)
        def _():
            dw_ref[...] = part_w
            db_ref[...] = part_b

        @pl.when(i > 0)
        def _():
            dw_ref[...] += part_w
            db_ref[...] += part_b

    blk = (None, TR, FF_BLK)
    hblk = (None, HALO, FF_BLK)
    wspec = pl.BlockSpec((None, 3, FF_BLK), lambda j, i: (j, 0, 0))
    bspec = pl.BlockSpec((None, 1, FF_BLK), lambda j, i: (j, 0, 0))
    return _pallas(
        body, name=name, grid=(N_FF_BLK, T // TR),
        in_specs=[pl.BlockSpec(blk, lambda j, i: (j, i, 0)),
                  pl.BlockSpec(hblk, lambda j, i: (j, _prev_idx(i), 0)),
                  pl.BlockSpec(hblk, lambda j, i: (j, _next_idx(i), 0)),
                  pl.BlockSpec(blk, lambda j, i: (j + N_FF_BLK, i, 0)),
                  pl.BlockSpec(hblk, lambda j, i: (j + N_FF_BLK, _next_idx(i), 0)),
                  pl.BlockSpec(blk, lambda j, i: (j, i, 0)),
                  pl.BlockSpec(hblk, lambda j, i: (j, _next_idx(i), 0)),
                  wspec, bspec],
        out_specs=[pl.BlockSpec(blk, lambda j, i: (j, i, 0)), pl.BlockSpec(blk, lambda j, i: (j, i, 0)), wspec, bspec],
        out_shape=[jax.ShapeDtypeStruct((N_FF_BLK, T, FF_BLK), BF16), jax.ShapeDtypeStruct((N_FF_BLK, T, FF_BLK), BF16),
                   jax.ShapeDtypeStruct((N_FF_BLK, 3, FF_BLK), F32), jax.ShapeDtypeStruct((N_FF_BLK, 1, FF_BLK), F32)],
        params=_params(("parallel", "arbitrary")))(gv, gv, gv, gv, gv, dact, dact, w, b)


def _rope_tables(pos, inv_freq):
    half = QK_ROPE // 2

    def body(p_ref, f_ref, c_ref, sa_ref, sb_ref):
        ang = p_ref[...].astype(F32) * f_ref[...]
        lane = lax.broadcasted_iota(jnp.int32, (T, 128), 1)
        c = jnp.cos(ang)
        s = jnp.sin(ang)
        c_ref[...] = jnp.where(lane < 2 * half, c, 0.0)
        sa_ref[...] = jnp.where(lane < half, -s, 0.0)
        sb_ref[...] = jnp.where(jnp.logical_and(lane >= half, lane < 2 * half), s, 0.0)

    return _pallas(
        body, name="rope_tables", in_specs=[VMEM_SPEC] * 2, out_specs=[VMEM_SPEC] * 3,
        out_shape=[jax.ShapeDtypeStruct((T, 128), F32)] * 3,
        params=pltpu.CompilerParams(vmem_limit_bytes=VMEM_LIMIT))(pos, inv_freq)


def _rope(name, x, tables, sign, out_dtype, reduce_groups=False):
    g, _, w = x.shape
    cos, sa, sb = tables

    def body(x_ref, c_ref, sa_ref, sb_ref, o_ref):
        xv = x_ref[...].astype(F32)
        if reduce_groups:
            acc = xv[0]
            for k in range(1, g):
                acc = acc + xv[k]
            xv = acc
        r = xv[:, w - 128:]
        out = r * c_ref[...] + sign * (pltpu.roll(r, 96, 1) * sa_ref[...] + pltpu.roll(r, 32, 1) * sb_ref[...])
        if w > 128:
            o_ref[:, :w - 128] = xv[:, :w - 128].astype(out_dtype)
        o_ref[:, w - 128:] = out.astype(out_dtype)

    tab = pl.BlockSpec((TM, 128), lambda h, i: (i, 0))
    if reduce_groups:
        x_spec = pl.BlockSpec((g, TM, w), lambda h, i: (0, i, 0))
        groups = 1
    else:
        x_spec = pl.BlockSpec((None, TM, w), lambda h, i: (h, i, 0))
        groups = g
    return _pallas(
        body, name=name, grid=(groups, T // TM), in_specs=[x_spec, tab, tab, tab],
        out_specs=pl.BlockSpec((None, TM, w), lambda h, i: (h, i, 0)),
        out_shape=jax.ShapeDtypeStruct((groups, T, w), out_dtype),
        params=_params(("parallel", "parallel")))(x, cos, sa, sb)


SCALE = (QK_NOPE + QK_ROPE) ** -0.5
LOG2E = 1.4426950408889634
SCALE2 = SCALE * LOG2E


def _diag_mask(transposed):
    shift = CHUNK.bit_length() - 1
    a = lax.broadcasted_iota(jnp.int32, (TQ, TQ), 0) >> shift
    b = lax.broadcasted_iota(jnp.int32, (TQ, TQ), 1) >> shift
    return (a <= b) if transposed else (b <= a)


def _keys(kn_ref, kr_ref, off):
    return jnp.concatenate([kn_ref[pl.ds(off, TQ), :], kr_ref[pl.ds(off, TQ), :]], axis=1)


def _attn_fwd(q, kn, kr, v):
    def body(q_ref, kn_ref, kr_ref, v_ref, o_ref, lse_ref):
        i = pl.program_id(1)
        qv = q_ref[...]

        def step(j, carry, masked):
            m, l, acc = carry
            off = pl.multiple_of(j * TQ, TQ)
            s = lax.dot_general(qv, _keys(kn_ref, kr_ref, off), NT, preferred_element_type=F32) * SCALE2
            if masked:
                s = jnp.where(_diag_mask(False), s, NEG_INF)
            m_new = jnp.maximum(m, jnp.max(s, axis=-1, keepdims=True))
            p = jnp.exp2(s - m_new)
            alpha = jnp.exp2(m - m_new)
            l = alpha * l + jnp.sum(p, axis=-1, keepdims=True)
            acc = alpha * acc + lax.dot_general(p.astype(BF16), v_ref[pl.ds(off, TQ), :], NN, preferred_element_type=F32)
            return m_new, l, acc

        init = (jnp.full((TQ, 1), NEG_INF, F32), jnp.zeros((TQ, 1), F32), jnp.zeros((TQ, V_HEAD), F32))
        carry = lax.fori_loop(0, i, lambda j, cr: step(j, cr, False), init)
        m, l, acc = step(i, carry, True)
        o_ref[...] = (acc / l).astype(BF16)
        lse_ref[...] = m + jnp.log(l) * LOG2E

    return _pallas(
        body, name="attn_fwd", grid=(N_HEADS, T // TQ),
        in_specs=[pl.BlockSpec((None, TQ, QK_PAD), lambda h, i: (h, i, 0)),
                  pl.BlockSpec((T, QK_NOPE), lambda h, i: (0, h)),
                  pl.BlockSpec((T, 128), lambda h, i: (0, 0)),
                  pl.BlockSpec((T, V_HEAD), lambda h, i: (0, h))],
        out_specs=[pl.BlockSpec((TQ, V_HEAD), lambda h, i: (i, h)), pl.BlockSpec((None, TQ, 1), lambda h, i: (h, i, 0))],
        out_shape=[jax.ShapeDtypeStruct((T, N_HEADS * V_HEAD), BF16), jax.ShapeDtypeStruct((N_HEADS, T, 1), F32)],
        params=_params(("parallel", "parallel")))(q, kn, kr, v)


def _attn_bwd_dq(q, kn, kr, v, o, do, lse):
    def body(q_ref, kn_ref, kr_ref, v_ref, o_ref, do_ref, lse_ref, dq_ref, dl_ref):
        i = pl.program_id(1)
        qv = q_ref[...]
        dov = do_ref[...]
        lse = lse_ref[...]
        delta = jnp.sum(dov.astype(F32) * o_ref[...].astype(F32), axis=-1, keepdims=True)
        dl_ref[...] = delta

        def step(j, dq, masked):
            off = pl.multiple_of(j * TQ, TQ)
            kk = _keys(kn_ref, kr_ref, off)
            s = lax.dot_general(qv, kk, NT, preferred_element_type=F32) * SCALE2
            if masked:
                s = jnp.where(_diag_mask(False), s, NEG_INF)
            p = jnp.exp2(s - lse)
            dp = lax.dot_general(dov, v_ref[pl.ds(off, TQ), :], NT, preferred_element_type=F32)
            ds = (p * (dp - delta)) * SCALE
            return dq + lax.dot_general(ds.astype(BF16), kk, NN, preferred_element_type=F32)

        dq = lax.fori_loop(0, i, lambda j, acc: step(j, acc, False), jnp.zeros((TQ, QK_PAD), F32))
        dq_ref[...] = step(i, dq, True)

    col = pl.BlockSpec((None, TQ, 1), lambda h, i: (h, i, 0))
    head = pl.BlockSpec((TQ, V_HEAD), lambda h, i: (i, h))
    return _pallas(
        body, name="attn_bwd_dq", grid=(N_HEADS, T // TQ),
        in_specs=[pl.BlockSpec((None, TQ, QK_PAD), lambda h, i: (h, i, 0)),
                  pl.BlockSpec((T, QK_NOPE), lambda h, i: (0, h)),
                  pl.BlockSpec((T, 128), lambda h, i: (0, 0)),
                  pl.BlockSpec((T, V_HEAD), lambda h, i: (0, h)),
                  head, head, col],
        out_specs=[pl.BlockSpec((None, TQ, QK_PAD), lambda h, i: (h, i, 0)), col],
        out_shape=[jax.ShapeDtypeStruct((N_HEADS, T, QK_PAD), F32), jax.ShapeDtypeStruct((N_HEADS, T, 1), F32)],
        params=_params(("parallel", "parallel")))(q, kn, kr, v, o, do, lse)


def _attn_bwd_dkv(q, kn, kr, v, do, lse_row, delta_row):
    nq = T // TQ

    def body(q_ref, kn_ref, kr_ref, v_ref, do_ref, lse_ref, dl_ref, dkn_ref, dkr_ref, dv_ref):
        j = pl.program_id(1)
        kk = jnp.concatenate([kn_ref[...], kr_ref[...]], axis=1)
        vv = v_ref[...]

        def step(i, carry, masked):
            dk, dv = carry
            off = pl.multiple_of(i * TQ, TQ)
            qi = q_ref[pl.ds(off, TQ), :]
            doi = do_ref[pl.ds(off, TQ), :]
            st = lax.dot_general(kk, qi, NT, preferred_element_type=F32) * SCALE2
            if masked:
                st = jnp.where(_diag_mask(True), st, NEG_INF)
            pt = jnp.exp2(st - lse_ref[:, pl.ds(off, TQ)])
            dv = dv + lax.dot_general(pt.astype(BF16), doi, NN, preferred_element_type=F32)
            dpt = lax.dot_general(vv, doi, NT, preferred_element_type=F32)
            dst = (pt * (dpt - dl_ref[:, pl.ds(off, TQ)])) * SCALE
            dk = dk + lax.dot_general(dst.astype(BF16), qi, NN, preferred_element_type=F32)
            return dk, dv

        carry = step(j, (jnp.zeros((TQ, QK_PAD), F32), jnp.zeros((TQ, V_HEAD), F32)), True)
        dk, dv = lax.fori_loop(j + 1, nq, lambda i, cr: step(i, cr, False), carry)
        dkn_ref[...] = dk[:, :QK_NOPE].astype(BF16)
        dkr_ref[...] = dk[:, QK_NOPE:]
        dv_ref[...] = dv.astype(BF16)

    row = pl.BlockSpec((None, 1, T), lambda h, j: (h, 0, 0))
    head = pl.BlockSpec((TQ, 128), lambda h, j: (j, h))
    return _pallas(
        body, name="attn_bwd_dkv", grid=(N_HEADS, nq),
        in_specs=[pl.BlockSpec((None, T, QK_PAD), lambda h, j: (h, 0, 0)),
                  head, pl.BlockSpec((TQ, 128), lambda h, j: (j, 0)), head,
                  pl.BlockSpec((T, V_HEAD), lambda h, j: (0, h)), row, row],
        out_specs=[head, pl.BlockSpec((None, TQ, 128), lambda h, j: (h, j, 0)), head],
        out_shape=[jax.ShapeDtypeStruct((T, N_HEADS * QK_NOPE), BF16), jax.ShapeDtypeStruct((N_HEADS, T, 128), F32),
                   jax.ShapeDtypeStruct((T, N_HEADS * V_HEAD), BF16)],
        params=_params(("parallel", "parallel")))(q, kn, kr, v, do, lse_row, delta_row)


def _ffn_gup(name, dg, dv, hf):
    def body(dg_ref, dv_ref, hf_ref, o_ref):
        j = pl.program_id(0)

        @pl.when(j < N_FF_BLK)
        def _():
            o_ref[...] = lax.dot_general(dg_ref[...], hf_ref[...], TN, preferred_element_type=F32).astype(BF16)

        @pl.when(j >= N_FF_BLK)
        def _():
            o_ref[...] = lax.dot_general(dv_ref[...], hf_ref[...], TN, preferred_element_type=F32).astype(BF16)

    return _pallas(
        body, name=name, grid=(N_DEV,),
        in_specs=[pl.BlockSpec((None, T, FF_BLK), lambda j: (jnp.minimum(j, N_FF_BLK - 1), 0, 0)),
                  pl.BlockSpec((None, T, FF_BLK), lambda j: (jnp.maximum(j - N_FF_BLK, 0), 0, 0)),
                  pl.BlockSpec((T, D), lambda j: (0, 0))],
        out_specs=pl.BlockSpec((None, FF_BLK, D), lambda j: (j, 0, 0)),
        out_shape=jax.ShapeDtypeStruct((N_DEV, FF_BLK, D), BF16), params=_params(("parallel",)))(dg, dv, hf)


def _ffn_layer_fwd(tag, h, gain, ex):
    hf = _rms_fwd(f"{tag}_norm", h, gain)
    gv = _mm(f"{tag}_up", hf, ex.need(f"ffn_w_up{tag[1]}", hf), grid=(N_DEV, T // TM),
             a_spec=pl.BlockSpec((TM, D), lambda j, i: (i, 0)),
             b_spec=pl.BlockSpec((None, None, FF_BLK, D), lambda j, i: (0, j, 0, 0)),
             o_spec=pl.BlockSpec((None, TM, FF_BLK), lambda j, i: (j, i, 0)),
             o_shape=(N_DEV, T, FF_BLK), o_dtype=BF16, dims=NT)
    ex.at(f"{tag}_up", gv)
    act = _ffn_fwd(f"{tag}_act", gv, ex.need(f"ffn_cw{tag[1]}", gv), ex.need(f"ffn_cb{tag[1]}", gv))
    ex.at(f"{tag}_act", act)
    rows = pl.BlockSpec((TS, D), lambda i: (i, 0))
    out = _mm_sum(f"{tag}_down",
                  [(act, pl.BlockSpec((N_FF_BLK, TS, FF_BLK), lambda i: (0, i, 0)), ex.need(f"ffn_w_down{tag[1]}", act),
                    pl.BlockSpec((None, N_FF_BLK, FF_BLK, D), lambda i: (0, 0, 0, 0)), NN)],
                  grid=(T // TS,), o_spec=rows, o_shape=(T, D), o_dtype=F32, add=h)
    ex.at(f"{tag}_down", out)
    return out, (hf, gv, act)


def _ffn_layer_bwd(tag, h, gain, ex, saved, dh, dh_bf):
    hf, gv, act = saved
    layer = tag[1]
    w_up, w_down4 = ex.need(f"ffn_w_up{layer}", dh_bf), ex.need(f"ffn_w_down{layer}", dh_bf)
    dact = _mm(f"{tag}_dact", dh_bf, w_down4, grid=(N_FF_BLK, T // TM),
               a_spec=pl.BlockSpec((TM, D), lambda j, i: (i, 0)),
               b_spec=pl.BlockSpec((None, None, FF_BLK, D), lambda j, i: (0, j, 0, 0)),
               o_spec=pl.BlockSpec((None, TM, FF_BLK), lambda j, i: (j, i, 0)),
               o_shape=(N_FF_BLK, T, FF_BLK), o_dtype=BF16, dims=NT)
    tn = 512
    g_down = _mm(f"{tag}_gdown", act, dh_bf, grid=(N_FF_BLK, D // tn),
                 a_spec=pl.BlockSpec((None, T, FF_BLK), lambda j, n: (j, 0, 0)),
                 b_spec=pl.BlockSpec((T, tn), lambda j, n: (0, n)),
                 o_spec=pl.BlockSpec((FF_BLK, tn), lambda j, n: (j, n)),
                 o_shape=(D_FF, D), o_dtype=BF16, dims=TN)
    dg, dv, dcw, dcb = _ffn_bwd(f"{tag}_dact_ew", gv, dact, ex.need(f"ffn_cw{layer}", dact), ex.need(f"ffn_cb{layer}", dact))
    ex.at(f"{tag}_dact_ew", dg)
    part = pl.BlockSpec((N_FF_BLK, TS, FF_BLK), lambda i: (0, i, 0))
    dhf = _mm_sum(f"{tag}_dhf",
                  [(dg, part, w_up, pl.BlockSpec((None, N_FF_BLK, FF_BLK, D), lambda i: (0, 0, 0, 0)), NN),
                   (dv, part, w_up, pl.BlockSpec((None, N_FF_BLK, FF_BLK, D), lambda i: (0, 1, 0, 0)), NN)],
                  grid=(T // TS,), o_spec=pl.BlockSpec((TS, D), lambda i: (i, 0)), o_shape=(T, D), o_dtype=F32)
    ex.grad("ffn_w_up", int(layer), _ffn_gup(f"{tag}_gup", dg, dv, hf).reshape(1, N_DEV, FF_BLK, D))
    ex.grad("ffn_w_down", int(layer), g_down.reshape(1, N_DEV, D_FF // N_DEV, D))
    dh_in, dh_in_bf, dgain = _rms_bwd(f"{tag}_dnorm", h, gain, dhf, dres=dh)
    return dh_in, dh_in_bf, dgain, dcw, dcb


def _local_step(x, pos, tgt, rep, ex):
    attn_norm, ffn_norm, final_norm = rep["attn_norm"], rep["ffn_norm"], rep["final_norm"]
    half = QK_ROPE // 2
    inv = 1.0 / (ROPE_THETA ** (jnp.arange(half, dtype=F32) / half))
    inv_freq = jnp.concatenate([inv, inv, jnp.zeros((128 - 2 * half,), F32)]).reshape(1, 128)
    tables = _rope_tables(pos, inv_freq)

    hn0 = _rms_fwd("l0_norm", x, attn_norm[0:1])
    w_in = ex.need("sc_w_in", hn0)
    ex.at("mixer_ready", hn0)
    z = _mm_rows("l0_in", hn0, w_in, NN, BF16, 3 * D, tn=512)
    ex.at("l0_in", z)
    y = _sc_fwd(z, ex.need("sc_conv_w", z))
    h1 = _mm_rows("l0_out", y, ex.need("sc_w_out", y), NN, F32, D, tn=512, add=x)
    ex.at("l0_out", h1)
    h2, ffn0 = _ffn_layer_fwd("f0", h1, ffn_norm[0:1], ex)

    hk = _rms_fwd("kv_norm", h2, rep["kv_in_norm"])
    ckv_raw = _mm_rows("kv_down", hk, ex.need("w_dkv", hk), NN, F32, KV_LORA)
    kr_raw = _mm_rows("kv_rope", hk, ex.need("w_kr", hk), NN, F32, 128)
    ckv = _rms_fwd("kv_lnorm", ckv_raw, rep["kv_latent_norm"])
    kn = _mm_rows("kv_uk", ckv, ex.need("w_uk", ckv), NN, BF16, N_HEADS * QK_NOPE)
    vv = _mm_rows("kv_uv", ckv, ex.need("w_uv", ckv), NN, BF16, N_HEADS * V_HEAD)
    kr = _rope("k_rope", kr_raw.reshape(1, T, 128), tables, 1.0, BF16).reshape(T, 128)

    hn1 = _rms_fwd("l1_norm", h2, attn_norm[1:2])
    cq_raw = _mm_rows("q_down", hn1, ex.need("w_dq", hn1), NN, F32, Q_LORA)
    cq = _rms_fwd("q_lnorm", cq_raw, rep["q_latent_norm"])
    w_uq = ex.need("w_uq", cq)
    q_raw = _mm("q_up", cq, w_uq, grid=(N_HEADS, T // TM),
                a_spec=pl.BlockSpec((TM, Q_LORA), lambda h, i: (i, 0)),
                b_spec=pl.BlockSpec((None, Q_LORA, QK_PAD), lambda h, i: (h, 0, 0)),
                o_spec=pl.BlockSpec((None, TM, QK_PAD), lambda h, i: (h, i, 0)),
                o_shape=(N_HEADS, T, QK_PAD), o_dtype=F32, dims=NN)
    q = _rope("q_rope", q_raw, tables, 1.0, BF16)
    o, lse = _attn_fwd(q, kn, kr, vv)
    ex.at("attn_fwd", o)
    w_o = ex.need("w_o", o)
    h3 = _mm_rows("attn_out", o, w_o, NN, F32, D, tn=512, add=h2)
    h4, ffn1 = _ffn_layer_fwd("f1", h3, ffn_norm[1:2], ex)

    loss, dh4, dh4_bf, d_final = _final(h4, final_norm.reshape(1, D), tgt)

    dh3, dh3_bf, d_fn1, dcw1, dcb1 = _ffn_layer_bwd("f1", h3, ffn_norm[1:2], ex, ffn1, dh4, dh4_bf)
    ex.at("f1_bwd", dh3)

    do = _mm_rows("d_attn_out", dh3_bf, w_o, NT, BF16, N_HEADS * V_HEAD)
    ex.grad("w_o", None, _mm_wgrad("g_w_o", o, dh3_bf).reshape(1, N_DEV, D // N_DEV, D))
    dq, delta = _attn_bwd_dq(q, kn, kr, vv, o, do, lse)
    ex.at("attn_dq", dq)
    dkn, dkr, dvv = _attn_bwd_dkv(q, kn, kr, vv, do, lse.reshape(N_HEADS, 1, T), delta.reshape(N_HEADS, 1, T))
    dq_pre = _rope("dq_rope", dq, tables, -1.0, BF16)
    dcq = _mm("d_q_up", dq_pre, w_uq, grid=(T // TM, N_HEADS),
              a_spec=pl.BlockSpec((None, TM, QK_PAD), lambda i, h: (h, i, 0)),
              b_spec=pl.BlockSpec((None, Q_LORA, QK_PAD), lambda i, h: (h, 0, 0)),
              o_spec=pl.BlockSpec((TM, Q_LORA), lambda i, h: (i, 0)), o_shape=(T, Q_LORA), o_dtype=F32,
              dims=NT, k_axis=1, acc_shape=(TM, Q_LORA))
    g_uq = _mm("g_w_uq", cq, dq_pre, grid=(N_HEADS,),
               a_spec=pl.BlockSpec((T, Q_LORA), lambda h: (0, 0)),
               b_spec=pl.BlockSpec((None, T, QK_PAD), lambda h: (h, 0, 0)),
               o_spec=pl.BlockSpec((None, Q_LORA, QK_PAD), lambda h: (h, 0, 0)),
               o_shape=(N_HEADS, Q_LORA, QK_PAD), o_dtype=BF16, dims=TN)
    ex.grad("w_uq", None, g_uq[:, :, :QK_NOPE + QK_ROPE].reshape(1, N_DEV, Q_LORA, QK_NOPE + QK_ROPE))
    _, dcq_raw_bf, d_qln = _rms_bwd("d_q_lnorm", cq_raw, rep["q_latent_norm"], dcq)
    dhn1 = _mm_rows("d_q_down", dcq_raw_bf, ex.need("w_dq", dcq_raw_bf), NT, F32, D)
    ex.grad("w_dq", None, _mm_wgrad("g_w_dq", hn1, dcq_raw_bf).reshape(1, N_DEV, D // N_DEV, Q_LORA))
    dh2_a, _, d_an1 = _rms_bwd("d_l1_norm", h2, attn_norm[1:2], dhn1, dres=dh3)

    dckv = _mm_rows("d_kv_uk", dkn, ex.need("w_uk", dkn), NT, F32, KV_LORA)
    dckv = _mm_rows("d_kv_uv", dvv, ex.need("w_uv", dvv), NT, F32, KV_LORA, add=dckv)
    ex.grad("w_uk", None, _mm_wgrad("g_w_uk", ckv, dkn))
    ex.grad("w_uv", None, _mm_wgrad("g_w_uv", ckv, dvv))
    _, dckv_raw_bf, d_kvln = _rms_bwd("d_kv_lnorm", ckv_raw, rep["kv_latent_norm"], dckv)
    dkr_raw_bf = _rope("dk_rope", dkr, tables, -1.0, BF16, reduce_groups=True).reshape(T, 128)
    dhk = _mm_rows("d_kv_down", dckv_raw_bf, ex.need("w_dkv", dckv_raw_bf), NT, F32, D)
    dhk = _mm_rows("d_kv_rope", dkr_raw_bf, ex.need("w_kr", dkr_raw_bf), NT, F32, D, add=dhk)
    ex.grad("w_dkv", None, _mm_wgrad("g_w_dkv", hk, dckv_raw_bf).reshape(1, N_DEV, D // N_DEV, KV_LORA))
    ex.grad("w_kr", None, _mm_wgrad("g_w_kr", hk, dkr_raw_bf)[:, :QK_ROPE].reshape(1, N_DEV, D // N_DEV, QK_ROPE))
    dh2, dh2_bf, d_kvin = _rms_bwd("d_kv_norm", h2, rep["kv_in_norm"], dhk, dres=dh2_a)
    ex.at("kv_bwd", dh2)

    dh1, dh1_bf, d_fn0, dcw0, dcb0 = _ffn_layer_bwd("f0", h1, ffn_norm[0:1], ex, ffn0, dh2, dh2_bf)
    ex.at("f0_bwd", dh1)

    dy = _mm_rows("d_l0_out", dh1_bf, ex.need("sc_w_out", dh1_bf), NT, F32, D)
    ex.grad("sc_w_out", None, _mm_wgrad("g_sc_w_out", y, dh1_bf).reshape(1, N_DEV, D // N_DEV, D))
    dz, d_scw = _sc_bwd(z, dy, ex.need("sc_conv_w", dy))
    ex.at("sc_bwd", dz)
    dhn0 = _mm_rows("d_l0_in", dz, ex.need("sc_w_in", dz), NT, F32, D)
    ex.grad("sc_w_in", None, _mm_wgrad("g_sc_w_in", hn0, dz))
    grad_x, _, d_an0 = _rms_bwd("d_l0_norm", x, attn_norm[0:1], dhn0, dres=dh1)

    small = {
        "attn_norm": jnp.concatenate([d_an0, d_an1], axis=0),
        "ffn_norm": jnp.concatenate([d_fn0, d_fn1], axis=0),
        "final_norm": d_final.reshape(D),
        "kv_in_norm": d_kvin.reshape(D),
        "kv_latent_norm": d_kvln.reshape(KV_LORA),
        "q_latent_norm": d_qln,
        "ffn_conv_b": jnp.stack([dcb0, dcb1]).transpose(0, 2, 1, 3).reshape(2, D_FF),
        "sc_conv_w": d_scw,
        "ffn_conv_w": jnp.stack([dcw0, dcw1]).transpose(0, 2, 1, 3).reshape(2, 3, D_FF),
    }
    return loss, grad_x, small


def _place():
    return lax.axis_index("x"), lax.axis_index("y"), lax.axis_index("c")


def _peers():
    x, y, c = _place()
    return (x, y, 1 - c), [(1 - x, y), (x, 1 - y), (1 - x, 1 - y)]


def _window(ref, kind, dev):
    if kind == "blocked":
        return ref.at[:, dev]
    width = ref.shape[-1] // N_DEV
    return ref.at[:, pl.ds(pl.multiple_of(dev * width, 128), width)]


def _all_gather(name, items):
    n = len(items)
    out_shapes = []
    for shard, kind in items:
        if kind == "blocked":
            shape = (shard.shape[0], N_DEV) + shard.shape[1:]
        else:
            shape = (shard.shape[0], N_DEV * shard.shape[1])
        out_shapes.append(jax.ShapeDtypeStruct(shape, shard.dtype))

    def body(*refs):
        srcs, outs = refs[:n], refs[n:2 * n]
        send_sems, recv_sems, local_sems = refs[2 * n:]
        x, y, c = _place()
        me = 4 * x + 2 * y + c
        sibling, chips = _peers()

        def num(px, py, pc):
            return 4 * px + 2 * py + pc

        def copy(t, k, dev, to, from_src):
            kind = items[t][1]
            dst = _window(outs[t], kind, dev)
            return pltpu.make_async_remote_copy(
                src_ref=srcs[t] if from_src else dst, dst_ref=dst,
                send_sem=send_sems.at[t, k], recv_sem=recv_sems.at[t, k], device_id=to, device_id_type=MESH)

        mine = [pltpu.make_async_copy(srcs[t], _window(outs[t], items[t][1], me), local_sems.at[t]) for t in range(n)]
        for cp in mine:
            cp.start()
        first = []
        for t in range(n):
            first.append(copy(t, 0, me, sibling, True))
            for j, chip in enumerate(chips):
                first.append(copy(t, 1 + j, me, (*chip, c), True))
        for cp in first:
            cp.start()
        passed = []
        for j, chip in enumerate(chips):
            for t in range(n):
                copy(t, 1 + j, num(*chip, c), (x, y, c), False).wait_recv()
                fwd = copy(t, 4 + j, num(*chip, c), sibling, False)
                fwd.start()
                passed.append(fwd)
        for t in range(n):
            copy(t, 0, num(x, y, 1 - c), (x, y, c), False).wait_recv()
            for j, chip in enumerate(chips):
                copy(t, 4 + j, num(*chip, 1 - c), (x, y, c), False).wait_recv()
        for cp in first + passed:
            cp.wait_send()
        for cp in mine:
            cp.wait()

    return _pallas(
        body, name=name, in_specs=[ANY_SPEC] * n, out_specs=[ANY_SPEC] * n, out_shape=out_shapes,
        scratch_shapes=[pltpu.SemaphoreType.DMA((n, 7)), pltpu.SemaphoreType.DMA((n, 7)), pltpu.SemaphoreType.DMA((n,))],
    )(*[s for s, _ in items])


HBM_SPEC = pl.BlockSpec(memory_space=pltpu.HBM)
SEM_SPEC = pl.BlockSpec(memory_space=pltpu.SEMAPHORE)
EFFECT = pltpu.SideEffectType.DATAFLOW_SIDE_EFFECTING
TOKEN = jax.ShapeDtypeStruct((8, 128), F32)


def _hbm(a):
    return pltpu.with_memory_space_constraint(a, pltpu.HBM)


def _copies_start(name, srcs, lands, ncopy, plan):
    ns, nl = len(srcs), len(lands)

    def body(*refs):
        send, recv, token = refs[ns + nl], refs[ns + nl + 1], refs[-1]
        copies = plan(refs[:ns], refs[ns:ns + nl])
        assert len(copies) == ncopy
        for k, (sent, dst, to, _) in enumerate(copies):
            pltpu.make_async_remote_copy(src_ref=sent, dst_ref=dst, send_sem=send.at[k], recv_sem=recv.at[k],
                                         device_id=to, device_id_type=MESH).start()
        token[...] = jnp.zeros_like(token)

    arrays = list(srcs) + list(lands)
    outs = pl.pallas_call(
        body, name=name, in_specs=[HBM_SPEC] * (ns + nl),
        out_specs=[SEM_SPEC] * 2 + [HBM_SPEC] * (ns + nl) + [VMEM_SPEC],
        out_shape=[pltpu.SemaphoreType.DMA((ncopy,))] * 2 + [pltpu.HBM(a.shape, a.dtype) for a in arrays] + [TOKEN],
        input_output_aliases={i: 2 + i for i in range(ns + nl)},
        compiler_params=pltpu.CompilerParams(has_side_effects=EFFECT))(*[_hbm(a) for a in arrays])
    _Chain.last = outs[-1]
    return outs[0], outs[1], list(outs[2:2 + ns]), list(outs[2 + ns:-1])


def _copies_wait(name, started, ncopy, plan):
    send, recv, srcs, lands = started
    ns, nl = len(srcs), len(lands)

    def body(*refs):
        send_ref, recv_ref, token = refs[ns + nl], refs[ns + nl + 1], refs[-1]
        copies = plan(refs[:ns], refs[ns:ns + nl])
        assert len(copies) == ncopy
        for k, (sent, _, to, landed) in enumerate(copies):
            cp = pltpu.make_async_remote_copy(src_ref=sent, dst_ref=landed, send_sem=send_ref.at[k],
                                              recv_sem=recv_ref.at[k], device_id=to, device_id_type=MESH)
            cp.wait_send()
            cp.wait_recv()
        token[...] = jnp.zeros_like(token)

    arrays = list(srcs) + list(lands)
    outs = pl.pallas_call(
        body, name=name, in_specs=[HBM_SPEC] * (ns + nl) + [SEM_SPEC] * 2 + [ANY_SPEC],
        out_specs=[HBM_SPEC] * (ns + nl) + [VMEM_SPEC], out_shape=[pltpu.HBM(a.shape, a.dtype) for a in arrays] + [TOKEN],
        input_output_aliases={i: i for i in range(ns + nl)},
        compiler_params=pltpu.CompilerParams(has_side_effects=EFFECT))(*arrays, send, recv, _Chain.last)
    _Chain.last = outs[-1]
    return list(outs[:ns]), list(outs[ns:-1])


def _plan_gather_chips(kinds):
    def plan(srcs, lands):
        x, y, c = _place()
        sibling, chips = _peers()
        out = []
        for t, kind in enumerate(kinds):
            mine = _window(lands[t], kind, 4 * x + 2 * y + c)
            out.append((srcs[t], mine, sibling, _window(lands[t], kind, 4 * x + 2 * y + 1 - c)))
            for px, py in chips:
                out.append((srcs[t], mine, (px, py, c), _window(lands[t], kind, 4 * px + 2 * py + c)))
        return out
    return plan, 4 * len(kinds)


def _plan_gather_sibling(kinds):
    def plan(srcs, lands):
        _, _, c = _place()
        sibling, chips = _peers()
        out = []
        for t, kind in enumerate(kinds):
            for px, py in chips:
                w = _window(lands[t], kind, 4 * px + 2 * py + c)
                out.append((w, w, sibling, _window(lands[t], kind, 4 * px + 2 * py + 1 - c)))
        return out
    return plan, 3 * len(kinds)


def _plan_scatter_sibling(kinds):
    def plan(srcs, lands):
        _, _, c = _place()
        sibling, _ = _peers()
        out = []
        for t, kind in enumerate(kinds):
            for k in range(N_CHIP):
                out.append((_window(srcs[t], kind, 2 * k + 1 - c), lands[t].at[k], sibling, lands[t].at[k]))
        return out
    return plan, N_CHIP * len(kinds)


def _plan_scatter_chips(n):
    def plan(srcs, lands):
        x, y, c = _place()
        _, chips = _peers()
        out = []
        for t in range(n):
            for px, py in chips:
                out.append((srcs[t].at[2 * px + py], lands[t].at[2 * x + y], (px, py, c), lands[t].at[2 * px + py]))
        return out
    return plan, 3 * n


def _landing(shard, kind, me):
    if kind == "blocked":
        land = lax.empty((shard.shape[0], N_DEV) + shard.shape[1:], shard.dtype)
        return lax.dynamic_update_slice(land, shard[:, None], (0, me) + (0,) * (shard.ndim - 1))
    land = lax.empty((shard.shape[0], N_DEV * shard.shape[1]), shard.dtype)
    return lax.dynamic_update_slice(land, shard, (0, me * shard.shape[1]))


def _row_tile(rows):
    for tr in (512, 384, 352, 256, 128, 64, 32, 16):
        if rows % tr == 0:
            return tr
    raise ValueError(rows)


def _chip_sum(name, gr, kind, recv, c):
    if kind == "blocked":
        nl, _, r, w = gr.shape
        rows = nl * r
        tr = _row_tile(r)
        per = r // tr
        g_spec = pl.BlockSpec((None, None, tr, w), lambda k, i, cref: (i // per, 2 * k + cref[0], i % per, 0))
    else:
        rows, w = gr.shape[0], gr.shape[1] // N_DEV
        tr = _row_tile(rows)
        g_spec = pl.BlockSpec((tr, w), lambda k, i, cref: (i, 2 * k + cref[0]))
    recv = recv.reshape(N_CHIP, rows, w)

    def body(c_ref, g_ref, r_ref, o_ref):
        del c_ref
        o_ref[...] = (g_ref[...].astype(F32) + r_ref[...].astype(F32)).astype(BF16)

    blk = pl.BlockSpec((None, tr, w), lambda k, i, cref: (k, i, 0))
    return _pallas(
        body, name=name, n_prefetch=1, grid=(N_CHIP, rows // tr), in_specs=[g_spec, blk], out_specs=blk,
        out_shape=jax.ShapeDtypeStruct((N_CHIP, rows, w), BF16),
        params=_params(("parallel", "parallel")))(c, gr, recv)


def _adamw_math(g, wv, mv, vv):
    m = ADAM_B1 * mv + (1.0 - ADAM_B1) * g
    v = ADAM_B2 * vv + (1.0 - ADAM_B2) * (g * g)
    m_hat = m / (1.0 - ADAM_B1 ** ADAM_STEP)
    v_hat = v / (1.0 - ADAM_B2 ** ADAM_STEP)
    delta = -ADAM_LR * (m_hat / (jnp.sqrt(v_hat) + ADAM_EPS) + ADAM_WD * wv)
    return delta, m, v


def _adamw_sharded(name, own, recv, chip_ids, w3, m3, v3, layer, prev):
    nl, rows, w = w3.shape
    tr = _row_tile(rows)
    has_prev = prev is not None

    def body(*refs):
        own_ref, r1_ref, r2_ref, r3_ref, w_ref, m_ref, v_ref = refs[1:8]
        g_ref, d_ref, nm_ref, nv_ref = refs[-4:]
        g = ((own_ref[...].astype(F32) + r1_ref[...].astype(F32)) + r2_ref[...].astype(F32)) + r3_ref[...].astype(F32)
        g_ref[...] = g
        d_ref[...], nm_ref[...], nv_ref[...] = _adamw_math(g, w_ref[...], m_ref[...], v_ref[...])

    def pick(slot):
        return pl.BlockSpec((None, tr, w), lambda i, ids: (ids[slot], i, 0))

    slab = pl.BlockSpec((None, tr, w), lambda i, ids: (layer, i, 0))
    in_specs = [pick(0), pick(1), pick(2), pick(3), slab, slab, slab]
    args = [chip_ids, own, recv, recv, recv, w3, m3, v3]
    aliases = {}
    if has_prev:
        in_specs += [ANY_SPEC] * 4
        aliases = {len(args) + k: k for k in range(4)}
        args += list(prev)
    return _pallas(
        body, name=name, n_prefetch=1, grid=(rows // tr,), in_specs=in_specs, out_specs=[slab] * 4,
        out_shape=[jax.ShapeDtypeStruct((nl, rows, w), F32)] * 4, aliases=aliases,
        params=_params(("parallel",)))(*args)


def _adamw_small(parts, wv, mv, vv):
    r = wv.shape[0]

    def body(p_ref, w_ref, m_ref, v_ref, g_ref, d_ref, nm_ref, nv_ref):
        g = p_ref[0]
        for k in range(1, N_DEV):
            g = g + p_ref[k]
        g_ref[...] = g
        d_ref[...], nm_ref[...], nv_ref[...] = _adamw_math(g, w_ref[...], m_ref[...], v_ref[...])

    return _pallas(
        body, name="adamw_small", in_specs=[VMEM_SPEC] * 4, out_specs=[VMEM_SPEC] * 4,
        out_shape=[jax.ShapeDtypeStruct((r, 128), F32)] * 4,
        params=pltpu.CompilerParams(vmem_limit_bytes=VMEM_LIMIT))(parts, wv, mv, vv)


KIND = {"sc_w_in": "cols", "sc_w_out": "blocked", "w_dkv": "blocked", "w_kr": "blocked", "w_uk": "cols", "w_uv": "cols",
        "w_dq": "blocked", "w_uq": "blocked", "w_o": "blocked", "ffn_w_up": "blocked", "ffn_w_down": "blocked",
        "conv": "blocked"}
GATHER_GROUPS = (("mixer", ("sc_w_in", "sc_w_out", "conv")),
                 ("up0", ("ffn_w_up0",)),
                 ("down0", ("ffn_w_down0",)),
                 ("attn", ("w_dkv", "w_kr", "w_uk", "w_uv", "w_dq", "w_uq", "w_o")),
                 ("ffn1", ("ffn_w_up1", "ffn_w_down1")))
SCATTER_GROUPS = (("ffn1", (("ffn_w_up", 1), ("ffn_w_down", 1))),
                  ("attn", (("w_o", None), ("w_uq", None), ("w_dq", None), ("w_uk", None), ("w_uv", None),
                            ("w_dkv", None), ("w_kr", None))),
                  ("ffn0", (("ffn_w_up", 0), ("ffn_w_down", 0))),
                  ("mixer", (("sc_w_out", None), ("sc_w_in", None))))
SCHEDULE = {
    "begin": (("gather_start", "mixer"),),
    "mixer_ready": (("gather_start", "up0"),),
    "l0_out": (("gather_forward", "up0"), ("gather_start", "down0")),
    "f0_up": (("gather_forward", "down0"), ("gather_start", "attn")),
    "f0_down": (("gather_forward", "attn"), ("gather_start", "ffn1")),
    "attn_fwd": (("gather_forward", "ffn1"),),
    "f1_bwd": (("scatter_sibling", "ffn1"),),
    "attn_dq": (("scatter_chips", "ffn1"),),
    "kv_bwd": (("scatter_sibling", "attn"), ("scatter_done", "ffn1")),
    "f0_dact_ew": (("scatter_chips", "attn"),),
    "f0_bwd": (("scatter_sibling", "ffn0"), ("scatter_done", "attn")),
    "sc_bwd": (("scatter_chips", "ffn0"),),
}
FINISH = (("scatter_sibling", "mixer"), ("scatter_chips", "mixer"), ("scatter_done", "ffn0"), ("scatter_done", "mixer"))
STAGES = {"gather_start": 1, "gather_forward": 2, "gather_done": 3,
          "scatter_sibling": 1, "scatter_chips": 2, "scatter_done": 3}
SMALL_W_ROWS = 24
SMALL_G_ROWS = 256


def _pack(arrays, rows):
    flat = jnp.concatenate([a.reshape(-1).astype(F32) for a in arrays])
    return jnp.pad(flat, (0, rows * 128 - flat.shape[0])).reshape(rows, 128)


def _unpack(packed, shapes):
    flat = packed.reshape(-1)
    out, off = [], 0
    for shape in shapes:
        size = 1
        for s in shape:
            size *= s
        out.append(flat[off:off + size].reshape(shape))
        off += size
    return out


def _stored(name, a):
    return jnp.swapaxes(a, -1, -2) if name == "ffn_w_up" else a


def _base(name):
    if name.startswith("ffn_w_") and name[-1] in "01":
        return name[:-1], int(name[-1])
    return name, None


class _Exchange:
    def __init__(self, wts, mom, var, ffn_conv_b):
        self.wts, self.mom, self.var = wts, mom, var
        x, y, c = _place()
        self.me = 4 * x + 2 * y + c
        self.c_arr = jnp.reshape(c, (1,)).astype(jnp.int32)
        chip = 2 * x + y
        self.chip_ids = jnp.stack([chip, chip ^ 1, chip ^ 2, chip ^ 3]).astype(jnp.int32)
        self.ready = {"ffn_cb0": ffn_conv_b.reshape(2, N_FF_BLK, 1, FF_BLK)[0],
                      "ffn_cb1": ffn_conv_b.reshape(2, N_FF_BLK, 1, FF_BLK)[1]}
        self.gathers, self.group_of = {}, {}
        self.grads, self.scatters, self.results = {}, {}, {}
        for gname, names in GATHER_GROUPS:
            self.gathers[gname] = dict(stage=0, names=names, kinds=[KIND[_base(nm)[0]] for nm in names])
            for nm in names:
                self.group_of[nm] = gname
        for nm in ("sc_conv_w", "ffn_cw0", "ffn_cw1"):
            self.group_of[nm] = "mixer"
        self.at("begin", None)

    def _shard(self, name):
        if name == "conv":
            return _pack([self.wts["sc_conv_w"], self.wts["ffn_conv_w"]], SMALL_W_ROWS).reshape(1, SMALL_W_ROWS, 128)
        base, layer = _base(name)
        a = _stored(base, self.wts[base])
        if layer is not None:
            a = a[layer:layer + 1]
        if KIND[base] == "cols":
            return a.reshape(a.shape[-2], a.shape[-1]).astype(BF16)
        return a.reshape((-1,) + a.shape[-2:]).astype(BF16)

    def _gather_to(self, gname, stage, after):
        st = self.gathers[gname]
        if st["stage"] < 1 <= stage:
            shards = [self._shard(nm) for nm in st["names"]]
            lands = [_landing(s, kind, self.me) for s, kind in zip(shards, st["kinds"])]
            plan, ncopy = _plan_gather_chips(st["kinds"])
            st["flight"] = _copies_start(f"ag_{gname}_chips", shards, lands, ncopy, plan)
            st["stage"] = 1
        if st["stage"] < 2 <= stage:
            plan, ncopy = _plan_gather_chips(st["kinds"])
            _, lands = _copies_wait(f"ag_{gname}_chips_wait", st["flight"], ncopy, plan)
            plan, ncopy = _plan_gather_sibling(st["kinds"])
            st["flight"] = _copies_start(f"ag_{gname}_sibling", [], lands, ncopy, plan)
            st["stage"] = 2
        if st["stage"] < 3 <= stage:
            plan, ncopy = _plan_gather_sibling(st["kinds"])
            _, lands = _copies_wait(f"ag_{gname}_sibling_wait", st["flight"], ncopy, plan)
            for nm, land in zip(st["names"], lands):
                self._arrived(nm, land)
            st["stage"] = 3

    def _arrived(self, name, land):
        if name == "conv":
            conv = land.reshape(N_DEV, SMALL_W_ROWS * 128)
            self.ready["sc_conv_w"] = conv[:, :3 * 128].reshape(N_DEV, 3, 128).transpose(1, 0, 2).reshape(3, D)
            fcw = conv[:, 3 * 128:3 * 128 + 6 * 352].reshape(N_DEV, 2, 3, 352).transpose(1, 2, 0, 3)
            fcw = fcw.reshape(2, 3, N_FF_BLK, FF_BLK).transpose(0, 2, 1, 3)
            self.ready["ffn_cw0"], self.ready["ffn_cw1"] = fcw[0], fcw[1]
        elif name in ("sc_w_in", "w_uk", "w_uv") or name.startswith("ffn_w_up"):
            self.ready[name] = land
        elif name.startswith("ffn_w_down"):
            self.ready[name] = land.reshape(1, N_FF_BLK, FF_BLK, D)
        elif name == "w_kr":
            self.ready[name] = jnp.pad(land.reshape(D, QK_ROPE), ((0, 0), (0, 128 - QK_ROPE)))
        elif name == "w_uq":
            self.ready[name] = jnp.pad(land.reshape(N_HEADS, Q_LORA, QK_NOPE + QK_ROPE),
                                       ((0, 0), (0, 0), (0, QK_PAD - QK_NOPE - QK_ROPE)))
        else:
            self.ready[name] = land.reshape(D, land.shape[-1])

    def need(self, name, after):
        if name not in self.ready:
            self._gather_to(self.group_of[name], 3, after)
        return self.ready[name]

    def grad(self, name, layer, array):
        self.grads[(name, layer)] = array

    def _scatter_to(self, gname, stage, after):
        keys = dict(SCATTER_GROUPS)[gname]
        st = self.scatters.setdefault(gname, dict(stage=0))
        kinds = [KIND[nm] for nm, _ in keys]
        if st["stage"] < 1 <= stage:
            grads = [self.grads[key] for key in keys]
            lands = []
            for gr, kind in zip(grads, kinds):
                shard = (gr.shape[0],) + gr.shape[2:] if kind == "blocked" else (gr.shape[0], gr.shape[1] // N_DEV)
                lands.append(lax.empty((N_CHIP,) + shard, BF16))
            plan, ncopy = _plan_scatter_sibling(kinds)
            st["flight"] = _copies_start(f"rs_{gname}_sibling", grads, lands, ncopy, plan)
            st["stage"] = 1
        if st["stage"] < 2 <= stage:
            plan, ncopy = _plan_scatter_sibling(kinds)
            grads, recvs = _copies_wait(f"rs_{gname}_sibling_wait", st["flight"], ncopy, plan)
            sums = [_chip_sum(f"rs_{gname}_sum{t}", gr, kind, rv, self.c_arr)
                    for t, (gr, kind, rv) in enumerate(zip(grads, kinds, recvs))]
            lands = [lax.empty(s.shape, BF16) for s in sums]
            plan, ncopy = _plan_scatter_chips(len(sums))
            st["flight"] = _copies_start(f"rs_{gname}_chips", sums, lands, ncopy, plan)
            st["stage"] = 2
        if st["stage"] < 3 <= stage:
            plan, ncopy = _plan_scatter_chips(len(keys))
            sums, recvs = _copies_wait(f"rs_{gname}_chips_wait", st["flight"], ncopy, plan)
            for t, ((nm, layer), own, rv) in enumerate(zip(keys, sums, recvs)):
                nl = 1 if layer is None else 2
                rows, w = own.shape[1], own.shape[2]
                w3, m3, v3 = (_stored(nm, src[nm]).reshape(nl, rows, w) for src in (self.wts, self.mom, self.var))
                self.results[nm] = _adamw_sharded(f"adamw_{gname}_{t}", own, rv, self.chip_ids, w3, m3, v3,
                                                  0 if layer is None else layer, self.results.get(nm))
            st["stage"] = 3

    def at(self, place, after):
        for action, gname in SCHEDULE.get(place, ()):
            self._advance(action, gname, after)

    def _advance(self, action, gname, after):
        if action.startswith("gather"):
            self._gather_to(gname, STAGES[action], after)
        else:
            self._scatter_to(gname, STAGES[action], after)

    def finish(self, after):
        for action, gname in FINISH:
            self._advance(action, gname, after)
        for gname, _ in SCATTER_GROUPS:
            self._scatter_to(gname, 3, after)
        return {nm: [_stored(nm, o.reshape(_stored(nm, self.wts[nm]).shape)) for o in outs]
                for nm, outs in self.results.items()}


REPLICATED = ("attn_norm", "ffn_norm", "final_norm", "kv_in_norm", "kv_latent_norm", "q_latent_norm", "ffn_conv_b")
WEIGHTS = ("attn_norm", "ffn_norm", "final_norm", "sc_w_in", "sc_conv_w", "sc_w_out", "kv_in_norm", "w_dkv",
           "kv_latent_norm", "w_kr", "w_uk", "w_uv", "w_dq", "q_latent_norm", "w_uq", "w_o", "ffn_w_up", "ffn_conv_w",
           "ffn_conv_b", "ffn_w_down")


def kernel(x, positions, attn_norm, ffn_norm, final_norm, sc_w_in, sc_conv_w, sc_w_out, kv_in_norm, w_dkv, kv_latent_norm, w_kr, w_uk, w_uv, w_dq, q_latent_norm, w_uq, w_o, ffn_w_up, ffn_conv_w, ffn_conv_b, ffn_w_down, loss_target, m_attn_norm, m_ffn_norm, m_final_norm, m_sc_w_in, m_sc_conv_w, m_sc_w_out, m_kv_in_norm, m_w_dkv, m_kv_latent_norm, m_w_kr, m_w_uk, m_w_uv, m_w_dq, m_q_latent_norm, m_w_uq, m_w_o, m_ffn_w_up, m_ffn_conv_w, m_ffn_conv_b, m_ffn_w_down, v_attn_norm, v_ffn_norm, v_final_norm, v_sc_w_in, v_sc_conv_w, v_sc_w_out, v_kv_in_norm, v_w_dkv, v_kv_latent_norm, v_w_kr, v_w_uk, v_w_uv, v_w_dq, v_q_latent_norm, v_w_uq, v_w_o, v_ffn_w_up, v_ffn_conv_w, v_ffn_conv_b, v_ffn_w_down):
    wts = dict(attn_norm=attn_norm, ffn_norm=ffn_norm, final_norm=final_norm, sc_w_in=sc_w_in, sc_conv_w=sc_conv_w,
               sc_w_out=sc_w_out, kv_in_norm=kv_in_norm, w_dkv=w_dkv, kv_latent_norm=kv_latent_norm, w_kr=w_kr,
               w_uk=w_uk, w_uv=w_uv, w_dq=w_dq, q_latent_norm=q_latent_norm, w_uq=w_uq, w_o=w_o, ffn_w_up=ffn_w_up,
               ffn_conv_w=ffn_conv_w, ffn_conv_b=ffn_conv_b, ffn_w_down=ffn_w_down)
    mom = dict(attn_norm=m_attn_norm, ffn_norm=m_ffn_norm, final_norm=m_final_norm, sc_w_in=m_sc_w_in,
               sc_conv_w=m_sc_conv_w, sc_w_out=m_sc_w_out, kv_in_norm=m_kv_in_norm, w_dkv=m_w_dkv,
               kv_latent_norm=m_kv_latent_norm, w_kr=m_w_kr, w_uk=m_w_uk, w_uv=m_w_uv, w_dq=m_w_dq,
               q_latent_norm=m_q_latent_norm, w_uq=m_w_uq, w_o=m_w_o, ffn_w_up=m_ffn_w_up, ffn_conv_w=m_ffn_conv_w,
               ffn_conv_b=m_ffn_conv_b, ffn_w_down=m_ffn_w_down)
    var = dict(attn_norm=v_attn_norm, ffn_norm=v_ffn_norm, final_norm=v_final_norm, sc_w_in=v_sc_w_in,
               sc_conv_w=v_sc_conv_w, sc_w_out=v_sc_w_out, kv_in_norm=v_kv_in_norm, w_dkv=v_w_dkv,
               kv_latent_norm=v_kv_latent_norm, w_kr=v_w_kr, w_uk=v_w_uk, w_uv=v_w_uv, w_dq=v_w_dq,
               q_latent_norm=v_q_latent_norm, w_uq=v_w_uq, w_o=v_w_o, ffn_w_up=v_ffn_w_up, ffn_conv_w=v_ffn_conv_w,
               ffn_conv_b=v_ffn_conv_b, ffn_w_down=v_ffn_w_down)
    xi, yi, ci = _place()
    me = 4 * xi + 2 * yi + ci
    _Chain.last = None

    ex = _Exchange(wts, mom, var, ffn_conv_b)
    rep = {
        "attn_norm": attn_norm, "ffn_norm": ffn_norm, "final_norm": final_norm,
        "kv_in_norm": kv_in_norm.reshape(1, D), "kv_latent_norm": kv_latent_norm.reshape(1, KV_LORA),
        "q_latent_norm": q_latent_norm.reshape(1, Q_LORA),
    }
    loss, grad_x, small = _local_step(x.reshape(T, D), positions.reshape(T, 1), loss_target.reshape(T, D), rep, ex)
    results = ex.finish(grad_x)

    small_order = list(REPLICATED) + ["sc_conv_w", "ffn_conv_w"]
    packed_g = _pack([loss[0, 0:1]] + [small[nm] for nm in small_order], SMALL_G_ROWS)
    parts = _all_gather("ag_small_grads", [(packed_g.reshape(1, SMALL_G_ROWS, 128), "blocked")])[0]
    parts = parts.reshape(N_DEV, SMALL_G_ROWS, 128)

    def full_params(src):
        scw_full = jnp.zeros((3, D), F32)
        scw_full = lax.dynamic_update_slice(scw_full, src["sc_conv_w"].reshape(3, 128), (0, me * 128))
        fcw_full = jnp.zeros((2, 3, D_FF), F32)
        fcw_full = lax.dynamic_update_slice(fcw_full, src["ffn_conv_w"], (0, 0, me * 352))
        return _pack([jnp.zeros((1,), F32)] + [src[nm] for nm in REPLICATED] + [scw_full, fcw_full], SMALL_G_ROWS)

    small_out = _adamw_small(parts, full_params(wts), full_params(mom), full_params(var))
    shapes = [(1,)] + [wts[nm].shape for nm in REPLICATED] + [(3, D), (2, 3, D_FF)]
    unpacked = [_unpack(o, shapes) for o in small_out]
    loss_total = unpacked[0][0].reshape(())
    for slot, nm in enumerate(small_order):
        vals = [u[slot + 1] for u in unpacked]
        if nm == "sc_conv_w":
            vals = [lax.dynamic_slice(a, (0, me * 128), (3, 128)).reshape(1, 3, 128) for a in vals]
        elif nm == "ffn_conv_w":
            vals = [lax.dynamic_slice(a, (0, 0, me * 352), (2, 3, 352)) for a in vals]
        results[nm] = vals

    outs = [loss_total, grad_x.reshape(1, T, D)]
    for slot in range(4):
        outs.extend(results[nm][slot] for nm in WEIGHTS)
    return tuple(outs)
```

```python
import jax
import jax.numpy as jnp
from jax import lax
from jax.experimental import pallas as pl
from jax.experimental.pallas import tpu as pltpu

F32 = jnp.float32
BF16 = jnp.bfloat16

T = 2048
D = 1024
N_HEADS = 8
QK_NOPE = 128
QK_ROPE = 64
V_HEAD = 128
Q_LORA = 384
KV_LORA = 256
D_FF = 2816
CHUNK = 64
ROPE_THETA = 10000.0
EPS = 1e-6
NEG_INF = -1e30
ADAM_LR = 0.001
ADAM_B1 = 0.9
ADAM_B2 = 0.999
ADAM_EPS = 1e-08
ADAM_WD = 0.01
ADAM_STEP = 10

N_DEV = 8
N_CHIP = 4
FF_BLK = D_FF * 2 // N_DEV
N_FF_BLK = D_FF // FF_BLK
QK_PAD = 256
HALO = 16

TM = 1024
TS = 512
TR = 256
TQ = 512
VMEM_LIMIT = 56 * 1024 * 1024

NN = (((1,), (0,)), ((), ()))
NT = (((1,), (1,)), ((), ()))
TN = (((0,), (0,)), ((), ()))
MESH = pl.DeviceIdType.MESH


def _params(sem):
    return pltpu.CompilerParams(dimension_semantics=sem, vmem_limit_bytes=VMEM_LIMIT)


ANY_SPEC = pl.BlockSpec(memory_space=pl.ANY)
VMEM_SPEC = pl.BlockSpec(memory_space=pltpu.VMEM)


class _Chain:
    last = None


def _pallas(body, *, name, in_specs, out_specs, out_shape, grid=(), scratch_shapes=(), n_prefetch=0, aliases=None,
            params=None):
    def run(*args):
        after = _Chain.last
        n_lead = len(args)
        specs, operands, fn = list(in_specs), list(args), body
        if after is not None:
            def fn(*refs):
                return body(*refs[:n_lead], *refs[n_lead + 1:])
            specs.append(ANY_SPEC)
            operands.append(after)
        kw = dict(name=name, out_shape=out_shape, input_output_aliases=aliases or {})
        if params is not None:
            kw["compiler_params"] = params
        if n_prefetch:
            kw["grid_spec"] = pltpu.PrefetchScalarGridSpec(
                num_scalar_prefetch=n_prefetch, grid=grid, in_specs=specs, out_specs=out_specs,
                scratch_shapes=scratch_shapes)
        else:
            kw.update(grid=grid, in_specs=specs, out_specs=out_specs, scratch_shapes=scratch_shapes)
        outs = pl.pallas_call(fn, **kw)(*operands)
        _Chain.last = outs[0] if isinstance(outs, (list, tuple)) else outs
        return outs
    return run


def _mm(name, a, b, *, grid, a_spec, b_spec, o_spec, o_shape, o_dtype, dims, k_axis=None, acc_shape=None,
        add=None, add_spec=None):
    nk = grid[k_axis] if k_axis is not None else 1
    has_add = add is not None

    def body(*refs):
        a_ref, b_ref = refs[0], refs[1]
        p = 2
        add_ref = None
        if has_add:
            add_ref = refs[p]
            p += 1
        o_ref = refs[p]
        p += 1
        r = lax.dot_general(a_ref[...].astype(BF16), b_ref[...].astype(BF16), dims, preferred_element_type=F32)
        if k_axis is None:
            if has_add:
                r = r + add_ref[...].astype(F32)
            o_ref[...] = r.astype(o_dtype)
        else:
            acc = refs[p]
            k = pl.program_id(k_axis)

            @pl.when(k == 0)
            def _():
                acc[...] = r

            @pl.when(k > 0)
            def _():
                acc[...] += r

            @pl.when(k == nk - 1)
            def _():
                t = acc[...]
                if has_add:
                    t = t + add_ref[...].astype(F32)
                o_ref[...] = t.astype(o_dtype)

    in_specs = [a_spec, b_spec]
    args = [a, b]
    if has_add:
        in_specs.append(add_spec if add_spec is not None else o_spec)
        args.append(add)
    sem = tuple("arbitrary" if ax == k_axis else "parallel" for ax in range(len(grid)))
    scratch = [pltpu.VMEM(acc_shape, F32)] if k_axis is not None else []
    return _pallas(body, name=name, grid=grid, in_specs=in_specs, out_specs=o_spec,
                   out_shape=jax.ShapeDtypeStruct(o_shape, o_dtype), scratch_shapes=scratch, params=_params(sem))(*args)


def _mm_sum(name, parts, *, grid, o_spec, o_shape, o_dtype, add=None):
    has_add = add is not None

    def body(*refs):
        o_ref = refs[-1]
        acc = None
        for p, (_, _, _, _, dims) in enumerate(parts):
            a_ref, b_ref = refs[2 * p], refs[2 * p + 1]
            for k in range(a_ref.shape[0]):
                r = lax.dot_general(a_ref[k], b_ref[k], dims, preferred_element_type=F32)
                acc = r if acc is None else acc + r
        if has_add:
            acc = acc + refs[2 * len(parts)][...]
        o_ref[...] = acc.astype(o_dtype)

    in_specs, args = [], []
    for a, a_spec, b, b_spec, _ in parts:
        in_specs += [a_spec, b_spec]
        args += [a, b]
    if has_add:
        in_specs.append(o_spec)
        args.append(add)
    return _pallas(body, name=name, grid=grid, in_specs=in_specs, out_specs=o_spec,
                   out_shape=jax.ShapeDtypeStruct(o_shape, o_dtype),
                   params=_params(("parallel",) * len(grid)))(*args)


def _mm_rows(name, a, b, dims, o_dtype, n_out, *, tn=None, add=None):
    k = a.shape[1]
    tn = n_out if tn is None else tn
    if dims == NN:
        b_spec = pl.BlockSpec((k, tn), lambda n, i: (0, n))
    else:
        b_spec = pl.BlockSpec((tn, k), lambda n, i: (n, 0))
    return _mm(name, a, b, grid=(n_out // tn, T // TM),
               a_spec=pl.BlockSpec((TM, k), lambda n, i: (i, 0)), b_spec=b_spec,
               o_spec=pl.BlockSpec((TM, tn), lambda n, i: (i, n)), o_shape=(T, n_out), o_dtype=o_dtype,
               dims=dims, add=add)


def _mm_wgrad(name, a, b, *, tn=512):
    k, n = a.shape[1], b.shape[1]
    tn = min(tn, n)
    return _mm(name, a, b, grid=(n // tn,),
               a_spec=pl.BlockSpec((T, k), lambda j: (0, 0)), b_spec=pl.BlockSpec((T, tn), lambda j: (0, j)),
               o_spec=pl.BlockSpec((k, tn), lambda j: (0, j)), o_shape=(k, n), o_dtype=BF16, dims=TN)


def _rms_fwd(name, x, g):
    d = x.shape[1]

    def body(x_ref, g_ref, o_ref):
        xv = x_ref[...]
        r = lax.rsqrt(jnp.mean(xv * xv, axis=-1, keepdims=True) + EPS)
        o_ref[...] = ((xv * r) * g_ref[...]).astype(BF16)

    return _pallas(
        body, name=name, grid=(T // TM,),
        in_specs=[pl.BlockSpec((TM, d), lambda i: (i, 0)), pl.BlockSpec((1, d), lambda i: (0, 0))],
        out_specs=pl.BlockSpec((TM, d), lambda i: (i, 0)),
        out_shape=jax.ShapeDtypeStruct((T, d), BF16), params=_params(("parallel",)))(x, g)


def _rms_bwd(name, x, gains, dys, dres=None):
    d = x.shape[1]
    n = len(gains)
    has_res = dres is not None

    def body(*refs):
        x_ref, g_refs, dy_refs = refs[0], refs[1:1 + n], refs[1 + n:1 + 2 * n]
        dx_ref, dxb_ref = refs[-2 - n], refs[-1 - n]
        dg_refs = refs[-n:]
        xv = x_ref[...]
        r = lax.rsqrt(jnp.mean(xv * xv, axis=-1, keepdims=True) + EPS)
        xn = xv * r
        dx = refs[1 + 2 * n][...] if has_res else None
        parts = []
        for g_ref, dy_ref in zip(g_refs, dy_refs):
            dyv = dy_ref[...].astype(F32)
            gdy = dyv * g_ref[...]
            t = r * (gdy - xn * jnp.mean(gdy * xn, axis=-1, keepdims=True))
            dx = t if dx is None else dx + t
            parts.append(jnp.sum(dyv * xn, axis=0, keepdims=True))
        dx_ref[...] = dx
        dxb_ref[...] = dx.astype(BF16)

        @pl.when(pl.program_id(0) == 0)
        def _():
            for dg_ref, part in zip(dg_refs, parts):
                dg_ref[...] = part

        @pl.when(pl.program_id(0) > 0)
        def _():
            for dg_ref, part in zip(dg_refs, parts):
                dg_ref[...] += part

    row = pl.BlockSpec((TR, d), lambda i: (i, 0))
    vec = pl.BlockSpec((1, d), lambda i: (0, 0))
    args = [x] + list(gains) + list(dys) + ([dres] if has_res else [])
    in_specs = [row] + [vec] * n + [row] * n + ([row] if has_res else [])
    outs = _pallas(
        body, name=name, grid=(T // TR,), in_specs=in_specs, out_specs=[row, row] + [vec] * n,
        out_shape=[jax.ShapeDtypeStruct((T, d), F32), jax.ShapeDtypeStruct((T, d), BF16)]
        + [jax.ShapeDtypeStruct((1, d), F32)] * n,
        params=_params(("arbitrary",)))(*args)
    return outs[0], outs[1], list(outs[2:])


def _final(h, g, tgt):
    def body(h_ref, g_ref, t_ref, loss_ref, dh_ref, dhb_ref, dg_ref):
        hv = h_ref[...]
        r = lax.rsqrt(jnp.mean(hv * hv, axis=-1, keepdims=True) + EPS)
        xn = hv * r
        gv = g_ref[...]
        err = xn * gv - t_ref[...]
        part_loss = 0.5 * jnp.sum(jnp.mean(err * err, axis=-1, keepdims=True), axis=0, keepdims=True)
        dy = err * (1.0 / D)
        gdy = dy * gv
        dh = r * (gdy - xn * jnp.mean(gdy * xn, axis=-1, keepdims=True))
        dh_ref[...] = dh
        dhb_ref[...] = dh.astype(BF16)
        part = jnp.sum(dy * xn, axis=0, keepdims=True)
        first = pl.program_id(0) == 0

        @pl.when(first)
        def _():
            dg_ref[...] = part
            loss_ref[...] = jnp.broadcast_to(part_loss, (1, 128))

        @pl.when(jnp.logical_not(first))
        def _():
            dg_ref[...] += part
            loss_ref[...] += jnp.broadcast_to(part_loss, (1, 128))

    row = pl.BlockSpec((TR, D), lambda i: (i, 0))
    vec = pl.BlockSpec((1, D), lambda i: (0, 0))
    return _pallas(
        body, name="final_loss", grid=(T // TR,), in_specs=[row, vec, row],
        out_specs=[pl.BlockSpec((1, 128), lambda i: (0, 0)), row, row, vec],
        out_shape=[jax.ShapeDtypeStruct((1, 128), F32), jax.ShapeDtypeStruct((T, D), F32),
                   jax.ShapeDtypeStruct((T, D), BF16), jax.ShapeDtypeStruct((1, D), F32)],
        params=_params(("arbitrary",)))(h, g, tgt)


def _prev_idx(i):
    return jnp.maximum(i * (TR // HALO) - 1, 0)


def _next_idx(i):
    return jnp.minimum((i + 1) * (TR // HALO), T // HALO - 1)


def _causal_taps(ext):
    return pltpu.roll(ext, 2, 0)[HALO:], pltpu.roll(ext, 1, 0)[HALO:], ext[HALO:]


def _anticausal_taps(ext, n):
    rows = ext.shape[0]
    return pltpu.roll(ext, rows - 1, 0)[:n], pltpu.roll(ext, rows - 2, 0)[:n]


def _sc_fwd(z, w):
    def body(b_ref, c_ref, ch_ref, u_ref, uh_ref, w_ref, y_ref):
        i = pl.program_id(0)
        cu = c_ref[...].astype(F32) * u_ref[...].astype(F32)
        cuh = ch_ref[...].astype(F32) * uh_ref[...].astype(F32)
        cuh = jnp.where(i > 0, cuh, 0.0)
        x2, x1, x0 = _causal_taps(jnp.concatenate([cuh, cu], axis=0))
        wv = w_ref[...]
        cv = (x2 * wv[0:1] + x1 * wv[1:2]) + x0 * wv[2:3]
        y_ref[...] = (b_ref[...].astype(F32) * cv).astype(BF16)

    def main(part):
        return pl.BlockSpec((TR, D), lambda i: (i, part))

    def halo(part):
        return pl.BlockSpec((HALO, D), lambda i: (_prev_idx(i), part))

    return _pallas(
        body, name="sc_fwd", grid=(T // TR,),
        in_specs=[main(0), main(1), halo(1), main(2), halo(2), pl.BlockSpec((3, D), lambda i: (0, 0))],
        out_specs=pl.BlockSpec((TR, D), lambda i: (i, 0)),
        out_shape=jax.ShapeDtypeStruct((T, D), BF16), params=_params(("parallel",)))(z, z, z, z, z, w)


def _sc_bwd(z, dy, w):
    last = T // TR - 1

    def body(b_ref, bn_ref, c_ref, ch_ref, u_ref, uh_ref, dy_ref, dyn_ref, w_ref, dz_ref, dw_ref):
        i = pl.program_id(0)
        cv_ = c_ref[...].astype(F32)
        uv = u_ref[...].astype(F32)
        cu = cv_ * uv
        cuh = jnp.where(i > 0, ch_ref[...].astype(F32) * uh_ref[...].astype(F32), 0.0)
        x2, x1, x0 = _causal_taps(jnp.concatenate([cuh, cu], axis=0))
        wv = w_ref[...]
        conv = (x2 * wv[0:1] + x1 * wv[1:2]) + x0 * wv[2:3]
        dyv = dy_ref[...]
        dz_ref[:, 0:D] = (dyv * conv).astype(BF16)
        dconv = dyv * b_ref[...].astype(F32)
        dconv_n = jnp.where(i < last, dyn_ref[...] * bn_ref[...].astype(F32), 0.0)
        n1, n2 = _anticausal_taps(jnp.concatenate([dconv, dconv_n], axis=0), TR)
        dcu = (dconv * wv[2:3] + n1 * wv[1:2]) + n2 * wv[0:1]
        dz_ref[:, D:2 * D] = (dcu * uv).astype(BF16)
        dz_ref[:, 2 * D:3 * D] = (dcu * cv_).astype(BF16)
        part = jnp.concatenate([jnp.sum(dconv * x2, axis=0, keepdims=True),
                                jnp.sum(dconv * x1, axis=0, keepdims=True),
                                jnp.sum(dconv * x0, axis=0, keepdims=True)], axis=0)

        @pl.when(i == 0)
        def _():
            dw_ref[...] = part

        @pl.when(i > 0)
        def _():
            dw_ref[...] += part

    def main(part):
        return pl.BlockSpec((TR, D), lambda i: (i, part))

    def prev(part):
        return pl.BlockSpec((HALO, D), lambda i: (_prev_idx(i), part))

    def nxt(part):
        return pl.BlockSpec((HALO, D), lambda i: (_next_idx(i), part))

    wspec = pl.BlockSpec((3, D), lambda i: (0, 0))
    return _pallas(
        body, name="sc_bwd", grid=(T // TR,),
        in_specs=[main(0), nxt(0), main(1), prev(1), main(2), prev(2), main(0), nxt(0), wspec],
        out_specs=[pl.BlockSpec((TR, 3 * D), lambda i: (i, 0)), wspec],
        out_shape=[jax.ShapeDtypeStruct((T, 3 * D), BF16), jax.ShapeDtypeStruct((3, D), F32)],
        params=_params(("arbitrary",)))(z, z, z, z, z, z, dy, dy, w)


def _sigmoid(x):
    return 1.0 / (1.0 + jnp.exp(-x))


def _ffn_fwd(name, gv, w, b):
    def body(g_ref, gh_ref, v_ref, w_ref, b_ref, a_ref):
        i = pl.program_id(1)
        g = g_ref[...].astype(F32)
        gh = jnp.where(i > 0, gh_ref[...].astype(F32), 0.0)
        x2, x1, x0 = _causal_taps(jnp.concatenate([gh, g], axis=0))
        wv = w_ref[...]
        gc = ((x2 * wv[0:1] + x1 * wv[1:2]) + x0 * wv[2:3]) + b_ref[...]
        a_ref[...] = ((gc * _sigmoid(gc)) * v_ref[...].astype(F32)).astype(BF16)

    blk = (None, TR, FF_BLK)
    return _pallas(
        body, name=name, grid=(N_FF_BLK, T // TR),
        in_specs=[pl.BlockSpec(blk, lambda j, i: (j, i, 0)),
                  pl.BlockSpec((None, HALO, FF_BLK), lambda j, i: (j, _prev_idx(i), 0)),
                  pl.BlockSpec(blk, lambda j, i: (j + N_FF_BLK, i, 0)),
                  pl.BlockSpec((None, 3, FF_BLK), lambda j, i: (j, 0, 0)),
                  pl.BlockSpec((None, 1, FF_BLK), lambda j, i: (j, 0, 0))],
        out_specs=pl.BlockSpec(blk, lambda j, i: (j, i, 0)),
        out_shape=jax.ShapeDtypeStruct((N_FF_BLK, T, FF_BLK), BF16),
        params=_params(("parallel", "parallel")))(gv, gv, gv, w, b)


def _ffn_bwd(name, gv, dact, w, b):
    last = T // TR - 1

    def body(g_ref, gp_ref, gn_ref, v_ref, vn_ref, da_ref, dan_ref, w_ref, b_ref, dg_ref, dv_ref, dw_ref, db_ref):
        i = pl.program_id(1)
        gp = jnp.where(i > 0, gp_ref[...].astype(F32), 0.0)
        ext = jnp.concatenate([gp, g_ref[...].astype(F32), gn_ref[...].astype(F32)], axis=0)
        x2, x1, x0 = _causal_taps(ext)
        wv = w_ref[...]
        gc = ((x2 * wv[0:1] + x1 * wv[1:2]) + x0 * wv[2:3]) + b_ref[...]
        sg = _sigmoid(gc)
        da = jnp.concatenate([da_ref[...].astype(F32), jnp.where(i < last, dan_ref[...].astype(F32), 0.0)], axis=0)
        vv = jnp.concatenate([v_ref[...].astype(F32), vn_ref[...].astype(F32)], axis=0)
        dv_ref[...] = (da[:TR] * (gc[:TR] * sg[:TR])).astype(BF16)
        dgc = (da * vv) * (sg * (1.0 + gc * (1.0 - sg)))
        n1, n2 = _anticausal_taps(dgc, TR)
        d0 = dgc[:TR]
        dg_ref[...] = ((d0 * wv[2:3] + n1 * wv[1:2]) + n2 * wv[0:1]).astype(BF16)
        part_w = jnp.concatenate([jnp.sum(d0 * x2[:TR], axis=0, keepdims=True),
                                  jnp.sum(d0 * x1[:TR], axis=0, keepdims=True),
                                  jnp.sum(d0 * x0[:TR], axis=0, keepdims=True)], axis=0)
        part_b = jnp.sum(d0, axis=0, keepdims=True)

        @pl.when(i == 0)
        def _():
            dw_ref[...] = part_w
            db_ref[...] = part_b

        @pl.when(i > 0)
        def _():
            dw_ref[...] += part_w
            db_ref[...] += part_b

    blk = (None, TR, FF_BLK)
    hblk = (None, HALO, FF_BLK)
    wspec = pl.BlockSpec((None, 3, FF_BLK), lambda j, i: (j, 0, 0))
    bspec = pl.BlockSpec((None, 1, FF_BLK), lambda j, i: (j, 0, 0))
    return _pallas(
        body, name=name, grid=(N_FF_BLK, T // TR),
        in_specs=[pl.BlockSpec(blk, lambda j, i: (j, i, 0)),
                  pl.BlockSpec(hblk, lambda j, i: (j, _prev_idx(i), 0)),
                  pl.BlockSpec(hblk, lambda j, i: (j, _next_idx(i), 0)),
                  pl.BlockSpec(blk, lambda j, i: (j + N_FF_BLK, i, 0)),
                  pl.BlockSpec(hblk, lambda j, i: (j + N_FF_BLK, _next_idx(i), 0)),
                  pl.BlockSpec(blk, lambda j, i: (j, i, 0)),
                  pl.BlockSpec(hblk, lambda j, i: (j, _next_idx(i), 0)),
                  wspec, bspec],
        out_specs=[pl.BlockSpec(blk, lambda j, i: (j, i, 0)), pl.BlockSpec(blk, lambda j, i: (j, i, 0)), wspec, bspec],
        out_shape=[jax.ShapeDtypeStruct((N_FF_BLK, T, FF_BLK), BF16), jax.ShapeDtypeStruct((N_FF_BLK, T, FF_BLK), BF16),
                   jax.ShapeDtypeStruct((N_FF_BLK, 3, FF_BLK), F32), jax.ShapeDtypeStruct((N_FF_BLK, 1, FF_BLK), F32)],
        params=_params(("parallel", "arbitrary")))(gv, gv, gv, gv, gv, dact, dact, w, b)


def _rope_tables(pos, inv_freq):
    half = QK_ROPE // 2

    def body(p_ref, f_ref, c_ref, sa_ref, sb_ref):
        ang = p_ref[...].astype(F32) * f_ref[...]
        lane = lax.broadcasted_iota(jnp.int32, (T, 128), 1)
        c = jnp.cos(ang)
        s = jnp.sin(ang)
        c_ref[...] = jnp.where(lane < 2 * half, c, 0.0)
        sa_ref[...] = jnp.where(lane < half, -s, 0.0)
        sb_ref[...] = jnp.where(jnp.logical_and(lane >= half, lane < 2 * half), s, 0.0)

    return _pallas(
        body, name="rope_tables", in_specs=[VMEM_SPEC] * 2, out_specs=[VMEM_SPEC] * 3,
        out_shape=[jax.ShapeDtypeStruct((T, 128), F32)] * 3,
        params=pltpu.CompilerParams(vmem_limit_bytes=VMEM_LIMIT))(pos, inv_freq)


def _rotate(r, c, sa, sb, sign):
    return r * c + sign * (pltpu.roll(r, 96, 1) * sa + pltpu.roll(r, 32, 1) * sb)


def _q_up(cq, w_uq, tables):
    cos, sa, sb = tables

    def body(a_ref, b_ref, c_ref, sa_ref, sb_ref, o_ref):
        r = lax.dot_general(a_ref[...], b_ref[...], NN, preferred_element_type=F32)
        o_ref[:, :QK_NOPE] = r[:, :QK_NOPE].astype(BF16)
        o_ref[:, QK_NOPE:] = _rotate(r[:, QK_NOPE:], c_ref[...], sa_ref[...], sb_ref[...], 1.0).astype(BF16)

    tab = pl.BlockSpec((TM, 128), lambda h, i: (i, 0))
    return _pallas(
        body, name="q_up", grid=(N_HEADS, T // TM),
        in_specs=[pl.BlockSpec((TM, Q_LORA), lambda h, i: (i, 0)),
                  pl.BlockSpec((None, Q_LORA, QK_PAD), lambda h, i: (h, 0, 0)), tab, tab, tab],
        out_specs=pl.BlockSpec((None, TM, QK_PAD), lambda h, i: (h, i, 0)),
        out_shape=jax.ShapeDtypeStruct((N_HEADS, T, QK_PAD), BF16),
        params=_params(("parallel", "parallel")))(cq, w_uq, cos, sa, sb)


def _rope(name, x, tables, sign, out_dtype, reduce_groups=False):
    g, _, w = x.shape
    cos, sa, sb = tables

    def body(x_ref, c_ref, sa_ref, sb_ref, o_ref):
        xv = x_ref[...].astype(F32)
        if reduce_groups:
            acc = xv[0]
            for k in range(1, g):
                acc = acc + xv[k]
            xv = acc
        out = _rotate(xv[:, w - 128:], c_ref[...], sa_ref[...], sb_ref[...], sign)
        if w > 128:
            o_ref[:, :w - 128] = xv[:, :w - 128].astype(out_dtype)
        o_ref[:, w - 128:] = out.astype(out_dtype)

    tab = pl.BlockSpec((TM, 128), lambda h, i: (i, 0))
    if reduce_groups:
        x_spec = pl.BlockSpec((g, TM, w), lambda h, i: (0, i, 0))
        groups = 1
    else:
        x_spec = pl.BlockSpec((None, TM, w), lambda h, i: (h, i, 0))
        groups = g
    return _pallas(
        body, name=name, grid=(groups, T // TM), in_specs=[x_spec, tab, tab, tab],
        out_specs=pl.BlockSpec((None, TM, w), lambda h, i: (h, i, 0)),
        out_shape=jax.ShapeDtypeStruct((groups, T, w), out_dtype),
        params=_params(("parallel", "parallel")))(x, cos, sa, sb)


SCALE = (QK_NOPE + QK_ROPE) ** -0.5
LOG2E = 1.4426950408889634
SCALE2 = SCALE * LOG2E


def _diag_mask(transposed):
    shift = CHUNK.bit_length() - 1
    a = lax.broadcasted_iota(jnp.int32, (TQ, TQ), 0) >> shift
    b = lax.broadcasted_iota(jnp.int32, (TQ, TQ), 1) >> shift
    return (a <= b) if transposed else (b <= a)


def _keys(kn_ref, kr_ref, off):
    return jnp.concatenate([kn_ref[pl.ds(off, TQ), :], kr_ref[pl.ds(off, TQ), :]], axis=1)


def _attn_fwd(q, kn, kr, v):
    def body(q_ref, kn_ref, kr_ref, v_ref, o_ref, lse_ref):
        i = pl.program_id(1)
        qv = q_ref[...]

        def step(j, carry, masked):
            m, l, acc = carry
            off = pl.multiple_of(j * TQ, TQ)
            s = lax.dot_general(qv, _keys(kn_ref, kr_ref, off), NT, preferred_element_type=F32) * SCALE2
            if masked:
                s = jnp.where(_diag_mask(False), s, NEG_INF)
            m_new = jnp.maximum(m, jnp.max(s, axis=-1, keepdims=True))
            p = jnp.exp2(s - m_new)
            alpha = jnp.exp2(m - m_new)
            l = alpha * l + jnp.sum(p, axis=-1, keepdims=True)
            acc = alpha * acc + lax.dot_general(p.astype(BF16), v_ref[pl.ds(off, TQ), :], NN, preferred_element_type=F32)
            return m_new, l, acc

        init = (jnp.full((TQ, 1), NEG_INF, F32), jnp.zeros((TQ, 1), F32), jnp.zeros((TQ, V_HEAD), F32))
        carry = lax.fori_loop(0, i, lambda j, cr: step(j, cr, False), init)
        m, l, acc = step(i, carry, True)
        o_ref[...] = (acc / l).astype(BF16)
        lse_ref[...] = m + jnp.log(l) * LOG2E

    return _pallas(
        body, name="attn_fwd", grid=(N_HEADS, T // TQ),
        in_specs=[pl.BlockSpec((None, TQ, QK_PAD), lambda h, i: (h, i, 0)),
                  pl.BlockSpec((T, QK_NOPE), lambda h, i: (0, h)),
                  pl.BlockSpec((T, 128), lambda h, i: (0, 0)),
                  pl.BlockSpec((T, V_HEAD), lambda h, i: (0, h))],
        out_specs=[pl.BlockSpec((TQ, V_HEAD), lambda h, i: (i, h)), pl.BlockSpec((None, TQ, 1), lambda h, i: (h, i, 0))],
        out_shape=[jax.ShapeDtypeStruct((T, N_HEADS * V_HEAD), BF16), jax.ShapeDtypeStruct((N_HEADS, T, 1), F32)],
        params=_params(("parallel", "parallel")))(q, kn, kr, v)


def _attn_bwd_dq(q, kn, kr, v, o, do, lse, tables):
    cos, sa, sb = tables

    def body(q_ref, kn_ref, kr_ref, v_ref, o_ref, do_ref, lse_ref, c_ref, sa_ref, sb_ref, dq_ref, dl_ref):
        i = pl.program_id(1)
        qv = q_ref[...]
        dov = do_ref[...]
        lse = lse_ref[...]
        delta = jnp.sum(dov.astype(F32) * o_ref[...].astype(F32), axis=-1, keepdims=True)
        dl_ref[...] = delta

        def step(j, dq, masked):
            off = pl.multiple_of(j * TQ, TQ)
            kk = _keys(kn_ref, kr_ref, off)
            s = lax.dot_general(qv, kk, NT, preferred_element_type=F32) * SCALE2
            if masked:
                s = jnp.where(_diag_mask(False), s, NEG_INF)
            p = jnp.exp2(s - lse)
            dp = lax.dot_general(dov, v_ref[pl.ds(off, TQ), :], NT, preferred_element_type=F32)
            ds = (p * (dp - delta)) * SCALE
            return dq + lax.dot_general(ds.astype(BF16), kk, NN, preferred_element_type=F32)

        dq = lax.fori_loop(0, i, lambda j, acc: step(j, acc, False), jnp.zeros((TQ, QK_PAD), F32))
        dq = step(i, dq, True)
        dq_ref[:, :QK_NOPE] = dq[:, :QK_NOPE].astype(BF16)
        dq_ref[:, QK_NOPE:] = _rotate(dq[:, QK_NOPE:], c_ref[...], sa_ref[...], sb_ref[...], -1.0).astype(BF16)

    col = pl.BlockSpec((None, TQ, 1), lambda h, i: (h, i, 0))
    head = pl.BlockSpec((TQ, V_HEAD), lambda h, i: (i, h))
    tab = pl.BlockSpec((TQ, 128), lambda h, i: (i, 0))
    return _pallas(
        body, name="attn_bwd_dq", grid=(N_HEADS, T // TQ),
        in_specs=[pl.BlockSpec((None, TQ, QK_PAD), lambda h, i: (h, i, 0)),
                  pl.BlockSpec((T, QK_NOPE), lambda h, i: (0, h)),
                  pl.BlockSpec((T, 128), lambda h, i: (0, 0)),
                  pl.BlockSpec((T, V_HEAD), lambda h, i: (0, h)),
                  head, head, col, tab, tab, tab],
        out_specs=[pl.BlockSpec((None, TQ, QK_PAD), lambda h, i: (h, i, 0)), col],
        out_shape=[jax.ShapeDtypeStruct((N_HEADS, T, QK_PAD), BF16), jax.ShapeDtypeStruct((N_HEADS, T, 1), F32)],
        params=_params(("parallel", "parallel")))(q, kn, kr, v, o, do, lse, cos, sa, sb)


def _attn_bwd_dkv(q, kn, kr, v, do, lse_row, delta_row):
    nq = T // TQ

    def body(q_ref, kn_ref, kr_ref, v_ref, do_ref, lse_ref, dl_ref, dkn_ref, dkr_ref, dv_ref):
        j = pl.program_id(1)
        kk = jnp.concatenate([kn_ref[...], kr_ref[...]], axis=1)
        vv = v_ref[...]

        def step(i, carry, masked):
            dk, dv = carry
            off = pl.multiple_of(i * TQ, TQ)
            qi = q_ref[pl.ds(off, TQ), :]
            doi = do_ref[pl.ds(off, TQ), :]
            st = lax.dot_general(kk, qi, NT, preferred_element_type=F32) * SCALE2
            if masked:
                st = jnp.where(_diag_mask(True), st, NEG_INF)
            pt = jnp.exp2(st - lse_ref[:, pl.ds(off, TQ)])
            dv = dv + lax.dot_general(pt.astype(BF16), doi, NN, preferred_element_type=F32)
            dpt = lax.dot_general(vv, doi, NT, preferred_element_type=F32)
            dst = (pt * (dpt - dl_ref[:, pl.ds(off, TQ)])) * SCALE
            dk = dk + lax.dot_general(dst.astype(BF16), qi, NN, preferred_element_type=F32)
            return dk, dv

        carry = step(j, (jnp.zeros((TQ, QK_PAD), F32), jnp.zeros((TQ, V_HEAD), F32)), True)
        dk, dv = lax.fori_loop(j + 1, nq, lambda i, cr: step(i, cr, False), carry)
        dkn_ref[...] = dk[:, :QK_NOPE].astype(BF16)
        dkr_ref[...] = dk[:, QK_NOPE:]
        dv_ref[...] = dv.astype(BF16)

    row = pl.BlockSpec((None, 1, T), lambda h, j: (h, 0, 0))
    head = pl.BlockSpec((TQ, 128), lambda h, j: (j, h))
    return _pallas(
        body, name="attn_bwd_dkv", grid=(N_HEADS, nq),
        in_specs=[pl.BlockSpec((None, T, QK_PAD), lambda h, j: (h, 0, 0)),
                  head, pl.BlockSpec((TQ, 128), lambda h, j: (j, 0)), head,
                  pl.BlockSpec((T, V_HEAD), lambda h, j: (0, h)), row, row],
        out_specs=[head, pl.BlockSpec((None, TQ, 128), lambda h, j: (h, j, 0)), head],
        out_shape=[jax.ShapeDtypeStruct((T, N_HEADS * QK_NOPE), BF16), jax.ShapeDtypeStruct((N_HEADS, T, 128), F32),
                   jax.ShapeDtypeStruct((T, N_HEADS * V_HEAD), BF16)],
        params=_params(("parallel", "parallel")))(q, kn, kr, v, do, lse_row, delta_row)


def _ffn_gup(name, dg, dv, hf):
    def body(dg_ref, dv_ref, hf_ref, o_ref):
        j = pl.program_id(0)

        @pl.when(j < N_FF_BLK)
        def _():
            o_ref[...] = lax.dot_general(dg_ref[...], hf_ref[...], TN, preferred_element_type=F32).astype(BF16)

        @pl.when(j >= N_FF_BLK)
        def _():
            o_ref[...] = lax.dot_general(dv_ref[...], hf_ref[...], TN, preferred_element_type=F32).astype(BF16)

    return _pallas(
        body, name=name, grid=(N_DEV,),
        in_specs=[pl.BlockSpec((None, T, FF_BLK), lambda j: (jnp.minimum(j, N_FF_BLK - 1), 0, 0)),
                  pl.BlockSpec((None, T, FF_BLK), lambda j: (jnp.maximum(j - N_FF_BLK, 0), 0, 0)),
                  pl.BlockSpec((T, D), lambda j: (0, 0))],
        out_specs=pl.BlockSpec((None, FF_BLK, D), lambda j: (j, 0, 0)),
        out_shape=jax.ShapeDtypeStruct((N_DEV, FF_BLK, D), BF16), params=_params(("parallel",)))(dg, dv, hf)


def _ffn_layer_fwd(tag, h, gain, ex):
    hf = _rms_fwd(f"{tag}_norm", h, gain)
    gv = _mm(f"{tag}_up", hf, ex.need(f"ffn_w_up{tag[1]}", hf), grid=(N_DEV, T // TM),
             a_spec=pl.BlockSpec((TM, D), lambda j, i: (i, 0)),
             b_spec=pl.BlockSpec((None, None, FF_BLK, D), lambda j, i: (0, j, 0, 0)),
             o_spec=pl.BlockSpec((None, TM, FF_BLK), lambda j, i: (j, i, 0)),
             o_shape=(N_DEV, T, FF_BLK), o_dtype=BF16, dims=NT)
    ex.at(f"{tag}_up", gv)
    act = _ffn_fwd(f"{tag}_act", gv, ex.need(f"ffn_cw{tag[1]}", gv), ex.need(f"ffn_cb{tag[1]}", gv))
    ex.at(f"{tag}_act", act)
    rows = pl.BlockSpec((TS, D), lambda i: (i, 0))
    out = _mm_sum(f"{tag}_down",
                  [(act, pl.BlockSpec((N_FF_BLK, TS, FF_BLK), lambda i: (0, i, 0)), ex.need(f"ffn_w_down{tag[1]}", act),
                    pl.BlockSpec((None, N_FF_BLK, FF_BLK, D), lambda i: (0, 0, 0, 0)), NN)],
                  grid=(T // TS,), o_spec=rows, o_shape=(T, D), o_dtype=F32, add=h)
    ex.at(f"{tag}_down", out)
    return out, (hf, gv, act)


def _ffn_layer_bwd(tag, h, gain, ex, saved, dh, dh_bf):
    hf, gv, act = saved
    layer = tag[1]
    w_up, w_down4 = ex.need(f"ffn_w_up{layer}", dh_bf), ex.need(f"ffn_w_down{layer}", dh_bf)
    dact = _mm(f"{tag}_dact", dh_bf, w_down4, grid=(N_FF_BLK, T // TM),
               a_spec=pl.BlockSpec((TM, D), lambda j, i: (i, 0)),
               b_spec=pl.BlockSpec((None, None, FF_BLK, D), lambda j, i: (0, j, 0, 0)),
               o_spec=pl.BlockSpec((None, TM, FF_BLK), lambda j, i: (j, i, 0)),
               o_shape=(N_FF_BLK, T, FF_BLK), o_dtype=BF16, dims=NT)
    tn = 512
    g_down = _mm(f"{tag}_gdown", act, dh_bf, grid=(N_FF_BLK, D // tn),
                 a_spec=pl.BlockSpec((None, T, FF_BLK), lambda j, n: (j, 0, 0)),
                 b_spec=pl.BlockSpec((T, tn), lambda j, n: (0, n)),
                 o_spec=pl.BlockSpec((FF_BLK, tn), lambda j, n: (j, n)),
                 o_shape=(D_FF, D), o_dtype=BF16, dims=TN)
    dg, dv, dcw, dcb = _ffn_bwd(f"{tag}_dact_ew", gv, dact, ex.need(f"ffn_cw{layer}", dact), ex.need(f"ffn_cb{layer}", dact))
    ex.at(f"{tag}_dact_ew", dg)
    g_up = _ffn_gup(f"{tag}_gup", dg, dv, hf)
    ex.grad("ffn_w_up", int(layer), g_up.reshape(1, N_DEV, FF_BLK, D))
    ex.grad("ffn_w_down", int(layer), g_down.reshape(1, N_DEV, D_FF // N_DEV, D))
    ex.at(f"{tag}_gup", g_up)
    part = pl.BlockSpec((N_FF_BLK, TS, FF_BLK), lambda i: (0, i, 0))
    dhf = _mm_sum(f"{tag}_dhf",
                  [(dg, part, w_up, pl.BlockSpec((None, N_FF_BLK, FF_BLK, D), lambda i: (0, 0, 0, 0)), NN),
                   (dv, part, w_up, pl.BlockSpec((None, N_FF_BLK, FF_BLK, D), lambda i: (0, 1, 0, 0)), NN)],
                  grid=(T // TS,), o_spec=pl.BlockSpec((TS, D), lambda i: (i, 0)), o_shape=(T, D), o_dtype=F32)
    ex.at(f"{tag}_dhf", dhf)
    dh_in, dh_in_bf, dgain = _rms_bwd(f"{tag}_dnorm", h, [gain], [dhf], dres=dh)
    return dh_in, dh_in_bf, dgain[0], dcw, dcb


def _local_step(x, pos, tgt, rep, ex):
    attn_norm, ffn_norm, final_norm = rep["attn_norm"], rep["ffn_norm"], rep["final_norm"]
    half = QK_ROPE // 2
    inv = 1.0 / (ROPE_THETA ** (jnp.arange(half, dtype=F32) / half))
    inv_freq = jnp.concatenate([inv, inv, jnp.zeros((128 - 2 * half,), F32)]).reshape(1, 128)
    tables = _rope_tables(pos, inv_freq)

    hn0 = _rms_fwd("l0_norm", x, attn_norm[0:1])
    w_in = ex.need("sc_w_in", hn0)
    ex.at("mixer_ready", hn0)
    z = _mm_rows("l0_in", hn0, w_in, NN, BF16, 3 * D, tn=512)
    ex.at("l0_in", z)
    y = _sc_fwd(z, ex.need("sc_conv_w", z))
    h1 = _mm_rows("l0_out", y, ex.need("sc_w_out", y), NN, F32, D, tn=512, add=x)
    ex.at("l0_out", h1)
    h2, ffn0 = _ffn_layer_fwd("f0", h1, ffn_norm[0:1], ex)

    hk = _rms_fwd("kv_norm", h2, rep["kv_in_norm"])
    ckv_raw = _mm_rows("kv_down", hk, ex.need("w_dkv", hk), NN, F32, KV_LORA)
    kr_raw = _mm_rows("kv_rope", hk, ex.need("w_kr", hk), NN, F32, 128)
    ckv = _rms_fwd("kv_lnorm", ckv_raw, rep["kv_latent_norm"])
    kn = _mm_rows("kv_uk", ckv, ex.need("w_uk", ckv), NN, BF16, N_HEADS * QK_NOPE)
    vv = _mm_rows("kv_uv", ckv, ex.need("w_uv", ckv), NN, BF16, N_HEADS * V_HEAD)
    kr = _rope("k_rope", kr_raw.reshape(1, T, 128), tables, 1.0, BF16).reshape(T, 128)

    hn1 = _rms_fwd("l1_norm", h2, attn_norm[1:2])
    cq_raw = _mm_rows("q_down", hn1, ex.need("w_dq", hn1), NN, F32, Q_LORA)
    cq = _rms_fwd("q_lnorm", cq_raw, rep["q_latent_norm"])
    w_uq = ex.need("w_uq", cq)
    q = _q_up(cq, w_uq, tables)
    o, lse = _attn_fwd(q, kn, kr, vv)
    ex.at("attn_fwd", o)
    w_o = ex.need("w_o", o)
    h3 = _mm_rows("attn_out", o, w_o, NN, F32, D, tn=512, add=h2)
    h4, ffn1 = _ffn_layer_fwd("f1", h3, ffn_norm[1:2], ex)

    loss, dh4, dh4_bf, d_final = _final(h4, final_norm.reshape(1, D), tgt)

    dh3, dh3_bf, d_fn1, dcw1, dcb1 = _ffn_layer_bwd("f1", h3, ffn_norm[1:2], ex, ffn1, dh4, dh4_bf)
    ex.at("f1_bwd", dh3)

    do = _mm_rows("d_attn_out", dh3_bf, w_o, NT, BF16, N_HEADS * V_HEAD)
    ex.grad("w_o", None, _mm_wgrad("g_w_o", o, dh3_bf).reshape(1, N_DEV, D // N_DEV, D))
    dq_pre, delta = _attn_bwd_dq(q, kn, kr, vv, o, do, lse, tables)
    ex.at("attn_dq", dq_pre)
    dkn, dkr, dvv = _attn_bwd_dkv(q, kn, kr, vv, do, lse.reshape(N_HEADS, 1, T), delta.reshape(N_HEADS, 1, T))
    dcq = _mm("d_q_up", dq_pre, w_uq, grid=(T // TM, N_HEADS),
              a_spec=pl.BlockSpec((None, TM, QK_PAD), lambda i, h: (h, i, 0)),
              b_spec=pl.BlockSpec((None, Q_LORA, QK_PAD), lambda i, h: (h, 0, 0)),
              o_spec=pl.BlockSpec((TM, Q_LORA), lambda i, h: (i, 0)), o_shape=(T, Q_LORA), o_dtype=F32,
              dims=NT, k_axis=1, acc_shape=(TM, Q_LORA))
    g_uq = _mm("g_w_uq", cq, dq_pre, grid=(N_HEADS,),
               a_spec=pl.BlockSpec((T, Q_LORA), lambda h: (0, 0)),
               b_spec=pl.BlockSpec((None, T, QK_PAD), lambda h: (h, 0, 0)),
               o_spec=pl.BlockSpec((None, Q_LORA, QK_PAD), lambda h: (h, 0, 0)),
               o_shape=(N_HEADS, Q_LORA, QK_PAD), o_dtype=BF16, dims=TN)
    ex.grad("w_uq", None, g_uq[:, :, :QK_NOPE + QK_ROPE].reshape(1, N_DEV, Q_LORA, QK_NOPE + QK_ROPE))
    _, dcq_raw_bf, (d_qln,) = _rms_bwd("d_q_lnorm", cq_raw, [rep["q_latent_norm"]], [dcq])
    dhn1 = _mm_rows("d_q_down", dcq_raw_bf, ex.need("w_dq", dcq_raw_bf), NT, F32, D)
    ex.grad("w_dq", None, _mm_wgrad("g_w_dq", hn1, dcq_raw_bf).reshape(1, N_DEV, D // N_DEV, Q_LORA))

    dckv = _mm_rows("d_kv_uk", dkn, ex.need("w_uk", dkn), NT, F32, KV_LORA)
    dckv = _mm_rows("d_kv_uv", dvv, ex.need("w_uv", dvv), NT, F32, KV_LORA, add=dckv)
    ex.grad("w_uk", None, _mm_wgrad("g_w_uk", ckv, dkn))
    ex.grad("w_uv", None, _mm_wgrad("g_w_uv", ckv, dvv))
    _, dckv_raw_bf, (d_kvln,) = _rms_bwd("d_kv_lnorm", ckv_raw, [rep["kv_latent_norm"]], [dckv])
    dkr_raw_bf = _rope("dk_rope", dkr, tables, -1.0, BF16, reduce_groups=True).reshape(T, 128)
    dhk = _mm_rows("d_kv_down", dckv_raw_bf, ex.need("w_dkv", dckv_raw_bf), NT, F32, D)
    dhk = _mm_rows("d_kv_rope", dkr_raw_bf, ex.need("w_kr", dkr_raw_bf), NT, F32, D, add=dhk)
    ex.grad("w_dkv", None, _mm_wgrad("g_w_dkv", hk, dckv_raw_bf).reshape(1, N_DEV, D // N_DEV, KV_LORA))
    ex.grad("w_kr", None, _mm_wgrad("g_w_kr", hk, dkr_raw_bf)[:, :QK_ROPE].reshape(1, N_DEV, D // N_DEV, QK_ROPE))
    dh2, dh2_bf, (d_an1, d_kvin) = _rms_bwd("d_h2_norms", h2, [attn_norm[1:2], rep["kv_in_norm"]], [dhn1, dhk], dres=dh3)
    ex.at("kv_bwd", dh2)

    dh1, dh1_bf, d_fn0, dcw0, dcb0 = _ffn_layer_bwd("f0", h1, ffn_norm[0:1], ex, ffn0, dh2, dh2_bf)
    ex.at("f0_bwd", dh1)

    dy = _mm_rows("d_l0_out", dh1_bf, ex.need("sc_w_out", dh1_bf), NT, F32, D)
    ex.grad("sc_w_out", None, _mm_wgrad("g_sc_w_out", y, dh1_bf).reshape(1, N_DEV, D // N_DEV, D))
    dz, d_scw = _sc_bwd(z, dy, ex.need("sc_conv_w", dy))
    g_in = _mm_wgrad("g_sc_w_in", hn0, dz)
    ex.grad("sc_w_in", None, g_in)
    ex.at("sc_bwd", g_in)
    dhn0 = _mm_rows("d_l0_in", dz, ex.need("sc_w_in", dz), NT, F32, D)
    ex.at("d_l0_in", dhn0)
    grad_x, _, (d_an0,) = _rms_bwd("d_l0_norm", x, [attn_norm[0:1]], [dhn0], dres=dh1)

    small = {
        "attn_norm": jnp.concatenate([d_an0, d_an1], axis=0),
        "ffn_norm": jnp.concatenate([d_fn0, d_fn1], axis=0),
        "final_norm": d_final.reshape(D),
        "kv_in_norm": d_kvin.reshape(D),
        "kv_latent_norm": d_kvln.reshape(KV_LORA),
        "q_latent_norm": d_qln,
        "ffn_conv_b": jnp.stack([dcb0, dcb1]).transpose(0, 2, 1, 3).reshape(2, D_FF),
        "sc_conv_w": d_scw,
        "ffn_conv_w": jnp.stack([dcw0, dcw1]).transpose(0, 2, 1, 3).reshape(2, 3, D_FF),
    }
    return loss, grad_x, small


def _place():
    return lax.axis_index("x"), lax.axis_index("y"), lax.axis_index("c")


def _peers():
    x, y, c = _place()
    return (x, y, 1 - c), [(1 - x, y), (x, 1 - y), (1 - x, 1 - y)]


def _window(ref, kind, dev):
    if kind == "blocked":
        return ref.at[:, dev]
    width = ref.shape[-1] // N_DEV
    return ref.at[:, pl.ds(pl.multiple_of(dev * width, 128), width)]


def _all_gather(name, items):
    n = len(items)
    out_shapes = []
    for shard, kind in items:
        if kind == "blocked":
            shape = (shard.shape[0], N_DEV) + shard.shape[1:]
        else:
            shape = (shard.shape[0], N_DEV * shard.shape[1])
        out_shapes.append(jax.ShapeDtypeStruct(shape, shard.dtype))

    def body(*refs):
        srcs, outs = refs[:n], refs[n:2 * n]
        send_sems, recv_sems, local_sems = refs[2 * n:]
        x, y, c = _place()
        me = 4 * x + 2 * y + c
        sibling, chips = _peers()

        def num(px, py, pc):
            return 4 * px + 2 * py + pc

        def copy(t, k, dev, to, from_src):
            kind = items[t][1]
            dst = _window(outs[t], kind, dev)
            return pltpu.make_async_remote_copy(
                src_ref=srcs[t] if from_src else dst, dst_ref=dst,
                send_sem=send_sems.at[t, k], recv_sem=recv_sems.at[t, k], device_id=to, device_id_type=MESH)

        mine = [pltpu.make_async_copy(srcs[t], _window(outs[t], items[t][1], me), local_sems.at[t]) for t in range(n)]
        for cp in mine:
            cp.start()
        first = []
        for t in range(n):
            first.append(copy(t, 0, me, sibling, True))
            for j, chip in enumerate(chips):
                first.append(copy(t, 1 + j, me, (*chip, c), True))
        for cp in first:
            cp.start()
        passed = []
        for j, chip in enumerate(chips):
            for t in range(n):
                copy(t, 1 + j, num(*chip, c), (x, y, c), False).wait_recv()
                fwd = copy(t, 4 + j, num(*chip, c), sibling, False)
                fwd.start()
                passed.append(fwd)
        for t in range(n):
            copy(t, 0, num(x, y, 1 - c), (x, y, c), False).wait_recv()
            for j, chip in enumerate(chips):
                copy(t, 4 + j, num(*chip, 1 - c), (x, y, c), False).wait_recv()
        for cp in first + passed:
            cp.wait_send()
        for cp in mine:
            cp.wait()

    return _pallas(
        body, name=name, in_specs=[ANY_SPEC] * n, out_specs=[ANY_SPEC] * n, out_shape=out_shapes,
        scratch_shapes=[pltpu.SemaphoreType.DMA((n, 7)), pltpu.SemaphoreType.DMA((n, 7)), pltpu.SemaphoreType.DMA((n,))],
    )(*[s for s, _ in items])


HBM_SPEC = pl.BlockSpec(memory_space=pltpu.HBM)
SEM_SPEC = pl.BlockSpec(memory_space=pltpu.SEMAPHORE)
EFFECT = pltpu.SideEffectType.DATAFLOW_SIDE_EFFECTING
TOKEN = jax.ShapeDtypeStruct((8, 128), F32)


def _hbm(a):
    return pltpu.with_memory_space_constraint(a, pltpu.HBM)


def _copies_start(name, srcs, lands, ncopy, plan):
    ns, nl = len(srcs), len(lands)

    def body(*refs):
        send, recv, token = refs[ns + nl], refs[ns + nl + 1], refs[-1]
        copies = plan(refs[:ns], refs[ns:ns + nl])
        assert len(copies) == ncopy
        for k, (sent, dst, to, _) in enumerate(copies):
            pltpu.make_async_remote_copy(src_ref=sent, dst_ref=dst, send_sem=send.at[k], recv_sem=recv.at[k],
                                         device_id=to, device_id_type=MESH).start()
        token[...] = jnp.zeros_like(token)

    arrays = list(srcs) + list(lands)
    outs = pl.pallas_call(
        body, name=name, in_specs=[HBM_SPEC] * (ns + nl),
        out_specs=[SEM_SPEC] * 2 + [HBM_SPEC] * (ns + nl) + [VMEM_SPEC],
        out_shape=[pltpu.SemaphoreType.DMA((ncopy,))] * 2 + [pltpu.HBM(a.shape, a.dtype) for a in arrays] + [TOKEN],
        input_output_aliases={i: 2 + i for i in range(ns + nl)},
        compiler_params=pltpu.CompilerParams(has_side_effects=EFFECT))(*[_hbm(a) for a in arrays])
    _Chain.last = outs[-1]
    return outs[0], outs[1], list(outs[2:2 + ns]), list(outs[2 + ns:-1])


def _copies_wait(name, started, ncopy, plan):
    send, recv, srcs, lands = started
    ns, nl = len(srcs), len(lands)

    def body(*refs):
        send_ref, recv_ref, token = refs[ns + nl], refs[ns + nl + 1], refs[-1]
        copies = plan(refs[:ns], refs[ns:ns + nl])
        assert len(copies) == ncopy
        for k, (sent, _, to, landed) in enumerate(copies):
            cp = pltpu.make_async_remote_copy(src_ref=sent, dst_ref=landed, send_sem=send_ref.at[k],
                                              recv_sem=recv_ref.at[k], device_id=to, device_id_type=MESH)
            cp.wait_send()
            cp.wait_recv()
        token[...] = jnp.zeros_like(token)

    arrays = list(srcs) + list(lands)
    outs = pl.pallas_call(
        body, name=name, in_specs=[HBM_SPEC] * (ns + nl) + [SEM_SPEC] * 2 + [ANY_SPEC],
        out_specs=[HBM_SPEC] * (ns + nl) + [VMEM_SPEC], out_shape=[pltpu.HBM(a.shape, a.dtype) for a in arrays] + [TOKEN],
        input_output_aliases={i: i for i in range(ns + nl)},
        compiler_params=pltpu.CompilerParams(has_side_effects=EFFECT))(*arrays, send, recv, _Chain.last)
    _Chain.last = outs[-1]
    return list(outs[:ns]), list(outs[ns:-1])


def _plan_gather_chips(kinds):
    def plan(srcs, lands):
        x, y, c = _place()
        sibling, chips = _peers()
        out = []
        for t, kind in enumerate(kinds):
            mine = _window(lands[t], kind, 4 * x + 2 * y + c)
            out.append((srcs[t], mine, sibling, _window(lands[t], kind, 4 * x + 2 * y + 1 - c)))
            for px, py in chips:
                out.append((srcs[t], mine, (px, py, c), _window(lands[t], kind, 4 * px + 2 * py + c)))
        return out
    return plan, 4 * len(kinds)


def _plan_gather_sibling(kinds):
    def plan(srcs, lands):
        _, _, c = _place()
        sibling, chips = _peers()
        out = []
        for t, kind in enumerate(kinds):
            for px, py in chips:
                w = _window(lands[t], kind, 4 * px + 2 * py + c)
                out.append((w, w, sibling, _window(lands[t], kind, 4 * px + 2 * py + 1 - c)))
        return out
    return plan, 3 * len(kinds)


def _plan_scatter_sibling(kinds):
    def plan(srcs, lands):
        _, _, c = _place()
        sibling, _ = _peers()
        out = []
        for t, kind in enumerate(kinds):
            for k in range(N_CHIP):
                out.append((_window(srcs[t], kind, 2 * k + 1 - c), lands[t].at[k], sibling, lands[t].at[k]))
        return out
    return plan, N_CHIP * len(kinds)


def _plan_scatter_chips(n):
    def plan(srcs, lands):
        x, y, c = _place()
        _, chips = _peers()
        out = []
        for t in range(n):
            for px, py in chips:
                out.append((srcs[t].at[2 * px + py], lands[t].at[2 * x + y], (px, py, c), lands[t].at[2 * px + py]))
        return out
    return plan, 3 * n


def _landing(shard, kind, me):
    if kind == "blocked":
        land = lax.empty((shard.shape[0], N_DEV) + shard.shape[1:], shard.dtype)
        return lax.dynamic_update_slice(land, shard[:, None], (0, me) + (0,) * (shard.ndim - 1))
    land = lax.empty((shard.shape[0], N_DEV * shard.shape[1]), shard.dtype)
    return lax.dynamic_update_slice(land, shard, (0, me * shard.shape[1]))


def _row_tile(rows):
    for tr in (512, 384, 352, 256, 128, 64, 32, 16):
        if rows % tr == 0:
            return tr
    raise ValueError(rows)


def _chip_sum(name, gr, kind, recv, c):
    if kind == "blocked":
        nl, _, r, w = gr.shape
        rows = nl * r
        tr = _row_tile(r)
        per = r // tr
        g_spec = pl.BlockSpec((None, None, tr, w), lambda k, i, cref: (i // per, 2 * k + cref[0], i % per, 0))
    else:
        rows, w = gr.shape[0], gr.shape[1] // N_DEV
        tr = _row_tile(rows)
        g_spec = pl.BlockSpec((tr, w), lambda k, i, cref: (i, 2 * k + cref[0]))
    recv = recv.reshape(N_CHIP, rows, w)

    def body(c_ref, g_ref, r_ref, o_ref):
        del c_ref
        o_ref[...] = (g_ref[...].astype(F32) + r_ref[...].astype(F32)).astype(BF16)

    blk = pl.BlockSpec((None, tr, w), lambda k, i, cref: (k, i, 0))
    return _pallas(
        body, name=name, n_prefetch=1, grid=(N_CHIP, rows // tr), in_specs=[g_spec, blk], out_specs=blk,
        out_shape=jax.ShapeDtypeStruct((N_CHIP, rows, w), BF16),
        params=_params(("parallel", "parallel")))(c, gr, recv)


def _adamw_math(g, wv, mv, vv):
    m = ADAM_B1 * mv + (1.0 - ADAM_B1) * g
    v = ADAM_B2 * vv + (1.0 - ADAM_B2) * (g * g)
    m_hat = m / (1.0 - ADAM_B1 ** ADAM_STEP)
    v_hat = v / (1.0 - ADAM_B2 ** ADAM_STEP)
    delta = -ADAM_LR * (m_hat / (jnp.sqrt(v_hat) + ADAM_EPS) + ADAM_WD * wv)
    return delta, m, v


def _adamw_sharded(name, own, recv, chip_ids, w3, m3, v3, layer, prev):
    nl, rows, w = w3.shape
    tr = _row_tile(rows)
    has_prev = prev is not None

    def body(*refs):
        own_ref, r1_ref, r2_ref, r3_ref, w_ref, m_ref, v_ref = refs[1:8]
        g_ref, d_ref, nm_ref, nv_ref = refs[-4:]
        g = ((own_ref[...].astype(F32) + r1_ref[...].astype(F32)) + r2_ref[...].astype(F32)) + r3_ref[...].astype(F32)
        g_ref[...] = g
        d_ref[...], nm_ref[...], nv_ref[...] = _adamw_math(g, w_ref[...], m_ref[...], v_ref[...])

    def pick(slot):
        return pl.BlockSpec((None, tr, w), lambda i, ids: (ids[slot], i, 0))

    slab = pl.BlockSpec((None, tr, w), lambda i, ids: (layer, i, 0))
    in_specs = [pick(0), pick(1), pick(2), pick(3), slab, slab, slab]
    args = [chip_ids, own, recv, recv, recv, w3, m3, v3]
    aliases = {}
    if has_prev:
        in_specs += [ANY_SPEC] * 4
        aliases = {len(args) + k: k for k in range(4)}
        args += list(prev)
    return _pallas(
        body, name=name, n_prefetch=1, grid=(rows // tr,), in_specs=in_specs, out_specs=[slab] * 4,
        out_shape=[jax.ShapeDtypeStruct((nl, rows, w), F32)] * 4, aliases=aliases,
        params=_params(("parallel",)))(*args)


def _adamw_small(parts, wv, mv, vv):
    r = wv.shape[0]

    def body(p_ref, w_ref, m_ref, v_ref, g_ref, d_ref, nm_ref, nv_ref):
        g = p_ref[0]
        for k in range(1, N_DEV):
            g = g + p_ref[k]
        g_ref[...] = g
        d_ref[...], nm_ref[...], nv_ref[...] = _adamw_math(g, w_ref[...], m_ref[...], v_ref[...])

    return _pallas(
        body, name="adamw_small", in_specs=[VMEM_SPEC] * 4, out_specs=[VMEM_SPEC] * 4,
        out_shape=[jax.ShapeDtypeStruct((r, 128), F32)] * 4,
        params=pltpu.CompilerParams(vmem_limit_bytes=VMEM_LIMIT))(parts, wv, mv, vv)


KIND = {"sc_w_in": "cols", "sc_w_out": "blocked", "w_dkv": "blocked", "w_kr": "blocked", "w_uk": "cols", "w_uv": "cols",
        "w_dq": "blocked", "w_uq": "blocked", "w_o": "blocked", "ffn_w_up": "blocked", "ffn_w_down": "blocked",
        "conv": "blocked"}
GATHER_GROUPS = (("mixer", ("sc_w_in", "sc_w_out", "conv")),
                 ("up0", ("ffn_w_up0",)),
                 ("down0", ("ffn_w_down0",)),
                 ("attn", ("w_dkv", "w_kr", "w_uk", "w_uv", "w_dq", "w_uq", "w_o")),
                 ("ffn1", ("ffn_w_up1", "ffn_w_down1")))
SCATTER_GROUPS = (("ffn1", (("ffn_w_up", 1), ("ffn_w_down", 1))),
                  ("attn", (("w_o", None), ("w_uq", None), ("w_dq", None), ("w_uk", None), ("w_uv", None),
                            ("w_dkv", None), ("w_kr", None))),
                  ("ffn0", (("ffn_w_up", 0), ("ffn_w_down", 0))),
                  ("mixer", (("sc_w_out", None), ("sc_w_in", None))))
SCHEDULE = {
    "begin": (("gather_start", "mixer"),),
    "mixer_ready": (("gather_start", "up0"),),
    "l0_out": (("gather_forward", "up0"), ("gather_start", "down0")),
    "f0_up": (("gather_forward", "down0"), ("gather_start", "attn")),
    "f0_down": (("gather_forward", "attn"), ("gather_start", "ffn1")),
    "attn_fwd": (("gather_forward", "ffn1"),),
    "f1_gup": (("scatter_sibling", "ffn1"),),
    "f1_dhf": (("scatter_chips", "ffn1"),),
    "kv_bwd": (("scatter_sibling", "attn"), ("scatter_done", "ffn1")),
    "f0_dact_ew": (("scatter_chips", "attn"),),
    "f0_gup": (("scatter_sibling", "ffn0"),),
    "f0_dhf": (("scatter_chips", "ffn0"),),
    "f0_bwd": (("scatter_done", "attn"),),
    "sc_bwd": (("scatter_sibling", "mixer"),),
    "d_l0_in": (("scatter_chips", "mixer"),),
}
FINISH = (("scatter_done", "ffn0"), ("scatter_done", "mixer"))
STAGES = {"gather_start": 1, "gather_forward": 2, "gather_done": 3,
          "scatter_sibling": 1, "scatter_chips": 2, "scatter_done": 3}
SMALL_W_ROWS = 24
SMALL_G_ROWS = 256


def _pack(arrays, rows):
    flat = jnp.concatenate([a.reshape(-1).astype(F32) for a in arrays])
    return jnp.pad(flat, (0, rows * 128 - flat.shape[0])).reshape(rows, 128)


def _unpack(packed, shapes):
    flat = packed.reshape(-1)
    out, off = [], 0
    for shape in shapes:
        size = 1
        for s in shape:
            size *= s
        out.append(flat[off:off + size].reshape(shape))
        off += size
    return out


def _stored(name, a):
    return jnp.swapaxes(a, -1, -2) if name == "ffn_w_up" else a


def _base(name):
    if name.startswith("ffn_w_") and name[-1] in "01":
        return name[:-1], int(name[-1])
    return name, None


class _Exchange:
    def __init__(self, wts, mom, var, ffn_conv_b):
        self.wts, self.mom, self.var = wts, mom, var
        x, y, c = _place()
        self.me = 4 * x + 2 * y + c
        self.c_arr = jnp.reshape(c, (1,)).astype(jnp.int32)
        chip = 2 * x + y
        self.chip_ids = jnp.stack([chip, chip ^ 1, chip ^ 2, chip ^ 3]).astype(jnp.int32)
        self.ready = {"ffn_cb0": ffn_conv_b.reshape(2, N_FF_BLK, 1, FF_BLK)[0],
                      "ffn_cb1": ffn_conv_b.reshape(2, N_FF_BLK, 1, FF_BLK)[1]}
        self.gathers, self.group_of = {}, {}
        self.grads, self.scatters, self.results = {}, {}, {}
        for gname, names in GATHER_GROUPS:
            self.gathers[gname] = dict(stage=0, names=names, kinds=[KIND[_base(nm)[0]] for nm in names])
            for nm in names:
                self.group_of[nm] = gname
        for nm in ("sc_conv_w", "ffn_cw0", "ffn_cw1"):
            self.group_of[nm] = "mixer"
        self.at("begin", None)

    def _shard(self, name):
        if name == "conv":
            return _pack([self.wts["sc_conv_w"], self.wts["ffn_conv_w"]], SMALL_W_ROWS).reshape(1, SMALL_W_ROWS, 128)
        base, layer = _base(name)
        a = _stored(base, self.wts[base])
        if layer is not None:
            a = a[layer:layer + 1]
        if KIND[base] == "cols":
            return a.reshape(a.shape[-2], a.shape[-1]).astype(BF16)
        return a.reshape((-1,) + a.shape[-2:]).astype(BF16)

    def _gather_to(self, gname, stage, after):
        st = self.gathers[gname]
        if st["stage"] < 1 <= stage:
            shards = [self._shard(nm) for nm in st["names"]]
            lands = [_landing(s, kind, self.me) for s, kind in zip(shards, st["kinds"])]
            plan, ncopy = _plan_gather_chips(st["kinds"])
            st["flight"] = _copies_start(f"ag_{gname}_chips", shards, lands, ncopy, plan)
            st["stage"] = 1
        if st["stage"] < 2 <= stage:
            plan, ncopy = _plan_gather_chips(st["kinds"])
            _, lands = _copies_wait(f"ag_{gname}_chips_wait", st["flight"], ncopy, plan)
            plan, ncopy = _plan_gather_sibling(st["kinds"])
            st["flight"] = _copies_start(f"ag_{gname}_sibling", [], lands, ncopy, plan)
            st["stage"] = 2
        if st["stage"] < 3 <= stage:
            plan, ncopy = _plan_gather_sibling(st["kinds"])
            _, lands = _copies_wait(f"ag_{gname}_sibling_wait", st["flight"], ncopy, plan)
            for nm, land in zip(st["names"], lands):
                self._arrived(nm, land)
            st["stage"] = 3

    def _arrived(self, name, land):
        if name == "conv":
            conv = land.reshape(N_DEV, SMALL_W_ROWS * 128)
            self.ready["sc_conv_w"] = conv[:, :3 * 128].reshape(N_DEV, 3, 128).transpose(1, 0, 2).reshape(3, D)
            fcw = conv[:, 3 * 128:3 * 128 + 6 * 352].reshape(N_DEV, 2, 3, 352).transpose(1, 2, 0, 3)
            fcw = fcw.reshape(2, 3, N_FF_BLK, FF_BLK).transpose(0, 2, 1, 3)
            self.ready["ffn_cw0"], self.ready["ffn_cw1"] = fcw[0], fcw[1]
        elif name in ("sc_w_in", "w_uk", "w_uv") or name.startswith("ffn_w_up"):
            self.ready[name] = land
        elif name.startswith("ffn_w_down"):
            self.ready[name] = land.reshape(1, N_FF_BLK, FF_BLK, D)
        elif name == "w_kr":
            self.ready[name] = jnp.pad(land.reshape(D, QK_ROPE), ((0, 0), (0, 128 - QK_ROPE)))
        elif name == "w_uq":
            self.ready[name] = jnp.pad(land.reshape(N_HEADS, Q_LORA, QK_NOPE + QK_ROPE),
                                       ((0, 0), (0, 0), (0, QK_PAD - QK_NOPE - QK_ROPE)))
        else:
            self.ready[name] = land.reshape(D, land.shape[-1])

    def need(self, name, after):
        if name not in self.ready:
            self._gather_to(self.group_of[name], 3, after)
        return self.ready[name]

    def grad(self, name, layer, array):
        self.grads[(name, layer)] = array

    def _scatter_to(self, gname, stage, after):
        keys = dict(SCATTER_GROUPS)[gname]
        st = self.scatters.setdefault(gname, dict(stage=0))
        kinds = [KIND[nm] for nm, _ in keys]
        if st["stage"] < 1 <= stage:
            grads = [self.grads[key] for key in keys]
            lands = []
            for gr, kind in zip(grads, kinds):
                shard = (gr.shape[0],) + gr.shape[2:] if kind == "blocked" else (gr.shape[0], gr.shape[1] // N_DEV)
                lands.append(lax.empty((N_CHIP,) + shard, BF16))
            plan, ncopy = _plan_scatter_sibling(kinds)
            st["flight"] = _copies_start(f"rs_{gname}_sibling", grads, lands, ncopy, plan)
            st["stage"] = 1
        if st["stage"] < 2 <= stage:
            plan, ncopy = _plan_scatter_sibling(kinds)
            grads, recvs = _copies_wait(f"rs_{gname}_sibling_wait", st["flight"], ncopy, plan)
            sums = [_chip_sum(f"rs_{gname}_sum{t}", gr, kind, rv, self.c_arr)
                    for t, (gr, kind, rv) in enumerate(zip(grads, kinds, recvs))]
            lands = [lax.empty(s.shape, BF16) for s in sums]
            plan, ncopy = _plan_scatter_chips(len(sums))
            st["flight"] = _copies_start(f"rs_{gname}_chips", sums, lands, ncopy, plan)
            st["stage"] = 2
        if st["stage"] < 3 <= stage:
            plan, ncopy = _plan_scatter_chips(len(keys))
            sums, recvs = _copies_wait(f"rs_{gname}_chips_wait", st["flight"], ncopy, plan)
            for t, ((nm, layer), own, rv) in enumerate(zip(keys, sums, recvs)):
                nl = 1 if layer is None else 2
                rows, w = own.shape[1], own.shape[2]
                w3, m3, v3 = (_stored(nm, src[nm]).reshape(nl, rows, w) for src in (self.wts, self.mom, self.var))
                self.results[nm] = _adamw_sharded(f"adamw_{gname}_{t}", own, rv, self.chip_ids, w3, m3, v3,
                                                  0 if layer is None else layer, self.results.get(nm))
            st["stage"] = 3

    def at(self, place, after):
        for action, gname in SCHEDULE.get(place, ()):
            self._advance(action, gname, after)

    def _advance(self, action, gname, after):
        if action.startswith("gather"):
            self._gather_to(gname, STAGES[action], after)
        else:
            self._scatter_to(gname, STAGES[action], after)

    def finish(self, after):
        for action, gname in FINISH:
            self._advance(action, gname, after)
        for gname, _ in SCATTER_GROUPS:
            self._scatter_to(gname, 3, after)
        return {nm: [_stored(nm, o.reshape(_stored(nm, self.wts[nm]).shape)) for o in outs]
                for nm, outs in self.results.items()}


REPLICATED = ("attn_norm", "ffn_norm", "final_norm", "kv_in_norm", "kv_latent_norm", "q_latent_norm", "ffn_conv_b")
WEIGHTS = ("attn_norm", "ffn_norm", "final_norm", "sc_w_in", "sc_conv_w", "sc_w_out", "kv_in_norm", "w_dkv",
           "kv_latent_norm", "w_kr", "w_uk", "w_uv", "w_dq", "q_latent_norm", "w_uq", "w_o", "ffn_w_up", "ffn_conv_w",
           "ffn_conv_b", "ffn_w_down")


def kernel(x, positions, attn_norm, ffn_norm, final_norm, sc_w_in, sc_conv_w, sc_w_out, kv_in_norm, w_dkv, kv_latent_norm, w_kr, w_uk, w_uv, w_dq, q_latent_norm, w_uq, w_o, ffn_w_up, ffn_conv_w, ffn_conv_b, ffn_w_down, loss_target, m_attn_norm, m_ffn_norm, m_final_norm, m_sc_w_in, m_sc_conv_w, m_sc_w_out, m_kv_in_norm, m_w_dkv, m_kv_latent_norm, m_w_kr, m_w_uk, m_w_uv, m_w_dq, m_q_latent_norm, m_w_uq, m_w_o, m_ffn_w_up, m_ffn_conv_w, m_ffn_conv_b, m_ffn_w_down, v_attn_norm, v_ffn_norm, v_final_norm, v_sc_w_in, v_sc_conv_w, v_sc_w_out, v_kv_in_norm, v_w_dkv, v_kv_latent_norm, v_w_kr, v_w_uk, v_w_uv, v_w_dq, v_q_latent_norm, v_w_uq, v_w_o, v_ffn_w_up, v_ffn_conv_w, v_ffn_conv_b, v_ffn_w_down):
    wts = dict(attn_norm=attn_norm, ffn_norm=ffn_norm, final_norm=final_norm, sc_w_in=sc_w_in, sc_conv_w=sc_conv_w,
               sc_w_out=sc_w_out, kv_in_norm=kv_in_norm, w_dkv=w_dkv, kv_latent_norm=kv_latent_norm, w_kr=w_kr,
               w_uk=w_uk, w_uv=w_uv, w_dq=w_dq, q_latent_norm=q_latent_norm, w_uq=w_uq, w_o=w_o, ffn_w_up=ffn_w_up,
               ffn_conv_w=ffn_conv_w, ffn_conv_b=ffn_conv_b, ffn_w_down=ffn_w_down)
    mom = dict(attn_norm=m_attn_norm, ffn_norm=m_ffn_norm, final_norm=m_final_norm, sc_w_in=m_sc_w_in,
               sc_conv_w=m_sc_conv_w, sc_w_out=m_sc_w_out, kv_in_norm=m_kv_in_norm, w_dkv=m_w_dkv,
               kv_latent_norm=m_kv_latent_norm, w_kr=m_w_kr, w_uk=m_w_uk, w_uv=m_w_uv, w_dq=m_w_dq,
               q_latent_norm=m_q_latent_norm, w_uq=m_w_uq, w_o=m_w_o, ffn_w_up=m_ffn_w_up, ffn_conv_w=m_ffn_conv_w,
               ffn_conv_b=m_ffn_conv_b, ffn_w_down=m_ffn_w_down)
    var = dict(attn_norm=v_attn_norm, ffn_norm=v_ffn_norm, final_norm=v_final_norm, sc_w_in=v_sc_w_in,
               sc_conv_w=v_sc_conv_w, sc_w_out=v_sc_w_out, kv_in_norm=v_kv_in_norm, w_dkv=v_w_dkv,
               kv_latent_norm=v_kv_latent_norm, w_kr=v_w_kr, w_uk=v_w_uk, w_uv=v_w_uv, w_dq=v_w_dq,
               q_latent_norm=v_q_latent_norm, w_uq=v_w_uq, w_o=v_w_o, ffn_w_up=v_ffn_w_up, ffn_conv_w=v_ffn_conv_w,
               ffn_conv_b=v_ffn_conv_b, ffn_w_down=v_ffn_w_down)
    xi, yi, ci = _place()
    me = 4 * xi + 2 * yi + ci
    _Chain.last = None

    ex = _Exchange(wts, mom, var, ffn_conv_b)
    rep = {
        "attn_norm": attn_norm, "ffn_norm": ffn_norm, "final_norm": final_norm,
        "kv_in_norm": kv_in_norm.reshape(1, D), "kv_latent_norm": kv_latent_norm.reshape(1, KV_LORA),
        "q_latent_norm": q_latent_norm.reshape(1, Q_LORA),
    }
    loss, grad_x, small = _local_step(x.reshape(T, D), positions.reshape(T, 1), loss_target.reshape(T, D), rep, ex)
    results = ex.finish(grad_x)

    small_order = list(REPLICATED) + ["sc_conv_w", "ffn_conv_w"]
    packed_g = _pack([loss[0, 0:1]] + [small[nm] for nm in small_order], SMALL_G_ROWS)
    parts = _all_gather("ag_small_grads", [(packed_g.reshape(1, SMALL_G_ROWS, 128), "blocked")])[0]
    parts = parts.reshape(N_DEV, SMALL_G_ROWS, 128)

    def full_params(src):
        scw_full = jnp.zeros((3, D), F32)
        scw_full = lax.dynamic_update_slice(scw_full, src["sc_conv_w"].reshape(3, 128), (0, me * 128))
        fcw_full = jnp.zeros((2, 3, D_FF), F32)
        fcw_full = lax.dynamic_update_slice(fcw_full, src["ffn_conv_w"], (0, 0, me * 352))
        return _pack([jnp.zeros((1,), F32)] + [src[nm] for nm in REPLICATED] + [scw_full, fcw_full], SMALL_G_ROWS)

    small_out = _adamw_small(parts, full_params(wts), full_params(mom), full_params(var))
    shapes = [(1,)] + [wts[nm].shape for nm in REPLICATED] + [(3, D), (2, 3, D_FF)]
    unpacked = [_unpack(o, shapes) for o in small_out]
    loss_total = unpacked[0][0].reshape(())
    for slot, nm in enumerate(small_order):
        vals = [u[slot + 1] for u in unpacked]
        if nm == "sc_conv_w":
            vals = [lax.dynamic_slice(a, (0, me * 128), (3, 128)).reshape(1, 3, 128) for a in vals]
        elif nm == "ffn_conv_w":
            vals = [lax.dynamic_slice(a, (0, 0, me * 352), (2, 3, 352)) for a in vals]
        results[nm] = vals

    outs = [loss_total, grad_x.reshape(1, T, D)]
    for slot in range(4):
        outs.extend(results[nm][slot] for nm in WEIGHTS)
    return tuple(outs)
```

```python
import jax
import jax.numpy as jnp
from jax import lax
from jax.experimental import pallas as pl
from jax.experimental.pallas import tpu as pltpu

F32 = jnp.float32
BF16 = jnp.bfloat16

T = 2048
D = 1024
N_HEADS = 8
QK_NOPE = 128
QK_ROPE = 64
V_HEAD = 128
Q_LORA = 384
KV_LORA = 256
D_FF = 2816
CHUNK = 64
ROPE_THETA = 10000.0
EPS = 1e-6
NEG_INF = -1e30
ADAM_LR = 0.001
ADAM_B1 = 0.9
ADAM_B2 = 0.999
ADAM_EPS = 1e-08
ADAM_WD = 0.01
ADAM_STEP = 10

N_DEV = 8
N_CHIP = 4
FF_BLK = D_FF * 2 // N_DEV
N_FF_BLK = D_FF // FF_BLK
QK_PAD = 256
HALO = 16

TM = 1024
TS = 512
TR = 256
TQ = 512
VMEM_LIMIT = 56 * 1024 * 1024

NN = (((1,), (0,)), ((), ()))
NT = (((1,), (1,)), ((), ()))
TN = (((0,), (0,)), ((), ()))
MESH = pl.DeviceIdType.MESH


def _params(sem):
    return pltpu.CompilerParams(dimension_semantics=sem, vmem_limit_bytes=VMEM_LIMIT)


ANY_SPEC = pl.BlockSpec(memory_space=pl.ANY)
VMEM_SPEC = pl.BlockSpec(memory_space=pltpu.VMEM)


class _Chain:
    last = None


def _pallas(body, *, name, in_specs, out_specs, out_shape, grid=(), scratch_shapes=(), n_prefetch=0, aliases=None,
            params=None):
    def run(*args):
        after = _Chain.last
        n_lead = len(args)
        specs, operands, fn = list(in_specs), list(args), body
        if after is not None:
            def fn(*refs):
                return body(*refs[:n_lead], *refs[n_lead + 1:])
            specs.append(ANY_SPEC)
            operands.append(after)
        kw = dict(name=name, out_shape=out_shape, input_output_aliases=aliases or {})
        if params is not None:
            kw["compiler_params"] = params
        if n_prefetch:
            kw["grid_spec"] = pltpu.PrefetchScalarGridSpec(
                num_scalar_prefetch=n_prefetch, grid=grid, in_specs=specs, out_specs=out_specs,
                scratch_shapes=scratch_shapes)
        else:
            kw.update(grid=grid, in_specs=specs, out_specs=out_specs, scratch_shapes=scratch_shapes)
        outs = pl.pallas_call(fn, **kw)(*operands)
        _Chain.last = outs[0] if isinstance(outs, (list, tuple)) else outs
        return outs
    return run


def _mm(name, a, b, *, grid, a_spec, b_spec, o_spec, o_shape, o_dtype, dims, k_axis=None, acc_shape=None,
        add=None, add_spec=None):
    nk = grid[k_axis] if k_axis is not None else 1
    has_add = add is not None

    def body(*refs):
        a_ref, b_ref = refs[0], refs[1]
        p = 2
        add_ref = None
        if has_add:
            add_ref = refs[p]
            p += 1
        o_ref = refs[p]
        p += 1
        r = lax.dot_general(a_ref[...].astype(BF16), b_ref[...].astype(BF16), dims, preferred_element_type=F32)
        if k_axis is None:
            if has_add:
                r = r + add_ref[...].astype(F32)
            o_ref[...] = r.astype(o_dtype)
        else:
            acc = refs[p]
            k = pl.program_id(k_axis)

            @pl.when(k == 0)
            def _():
                acc[...] = r

            @pl.when(k > 0)
            def _():
                acc[...] += r

            @pl.when(k == nk - 1)
            def _():
                t = acc[...]
                if has_add:
                    t = t + add_ref[...].astype(F32)
                o_ref[...] = t.astype(o_dtype)

    in_specs = [a_spec, b_spec]
    args = [a, b]
    if has_add:
        in_specs.append(add_spec if add_spec is not None else o_spec)
        args.append(add)
    sem = tuple("arbitrary" if ax == k_axis else "parallel" for ax in range(len(grid)))
    scratch = [pltpu.VMEM(acc_shape, F32)] if k_axis is not None else []
    return _pallas(body, name=name, grid=grid, in_specs=in_specs, out_specs=o_spec,
                   out_shape=jax.ShapeDtypeStruct(o_shape, o_dtype), scratch_shapes=scratch, params=_params(sem))(*args)


def _mm_sum(name, parts, *, grid, o_spec, o_shape, o_dtype, add=None):
    has_add = add is not None

    def body(*refs):
        o_ref = refs[-1]
        acc = None
        for p, (_, _, _, _, dims) in enumerate(parts):
            a_ref, b_ref = refs[2 * p], refs[2 * p + 1]
            for k in range(a_ref.shape[0]):
                r = lax.dot_general(a_ref[k], b_ref[k], dims, preferred_element_type=F32)
                acc = r if acc is None else acc + r
        if has_add:
            acc = acc + refs[2 * len(parts)][...]
        o_ref[...] = acc.astype(o_dtype)

    in_specs, args = [], []
    for a, a_spec, b, b_spec, _ in parts:
        in_specs += [a_spec, b_spec]
        args += [a, b]
    if has_add:
        in_specs.append(o_spec)
        args.append(add)
    return _pallas(body, name=name, grid=grid, in_specs=in_specs, out_specs=o_spec,
                   out_shape=jax.ShapeDtypeStruct(o_shape, o_dtype),
                   params=_params(("parallel",) * len(grid)))(*args)


def _mm_rows(name, a, b, dims, o_dtype, n_out, *, tn=None, add=None):
    k = a.shape[1]
    tn = n_out if tn is None else tn
    if dims == NN:
        b_spec = pl.BlockSpec((k, tn), lambda n, i: (0, n))
    else:
        b_spec = pl.BlockSpec((tn, k), lambda n, i: (n, 0))
    return _mm(name, a, b, grid=(n_out // tn, T // TM),
               a_spec=pl.BlockSpec((TM, k), lambda n, i: (i, 0)), b_spec=b_spec,
               o_spec=pl.BlockSpec((TM, tn), lambda n, i: (i, n)), o_shape=(T, n_out), o_dtype=o_dtype,
               dims=dims, add=add)


def _mm_wgrad(name, a, b, *, tn=512):
    k, n = a.shape[1], b.shape[1]
    tn = min(tn, n)
    return _mm(name, a, b, grid=(n // tn,),
               a_spec=pl.BlockSpec((T, k), lambda j: (0, 0)), b_spec=pl.BlockSpec((T, tn), lambda j: (0, j)),
               o_spec=pl.BlockSpec((k, tn), lambda j: (0, j)), o_shape=(k, n), o_dtype=BF16, dims=TN)


def _rms_fwd(name, x, g):
    d = x.shape[1]

    def body(x_ref, g_ref, o_ref):
        xv = x_ref[...]
        r = lax.rsqrt(jnp.mean(xv * xv, axis=-1, keepdims=True) + EPS)
        o_ref[...] = ((xv * r) * g_ref[...]).astype(BF16)

    return _pallas(
        body, name=name, grid=(T // TM,),
        in_specs=[pl.BlockSpec((TM, d), lambda i: (i, 0)), pl.BlockSpec((1, d), lambda i: (0, 0))],
        out_specs=pl.BlockSpec((TM, d), lambda i: (i, 0)),
        out_shape=jax.ShapeDtypeStruct((T, d), BF16), params=_params(("parallel",)))(x, g)


def _rms_bwd(name, x, gains, dys, dres=None):
    d = x.shape[1]
    n = len(gains)
    has_res = dres is not None

    def body(*refs):
        x_ref, g_refs, dy_refs = refs[0], refs[1:1 + n], refs[1 + n:1 + 2 * n]
        dx_ref, dxb_ref = refs[-2 - n], refs[-1 - n]
        dg_refs = refs[-n:]
        xv = x_ref[...]
        r = lax.rsqrt(jnp.mean(xv * xv, axis=-1, keepdims=True) + EPS)
        xn = xv * r
        dx = refs[1 + 2 * n][...] if has_res else None
        parts = []
        for g_ref, dy_ref in zip(g_refs, dy_refs):
            dyv = dy_ref[...].astype(F32)
            gdy = dyv * g_ref[...]
            t = r * (gdy - xn * jnp.mean(gdy * xn, axis=-1, keepdims=True))
            dx = t if dx is None else dx + t
            parts.append(jnp.sum(dyv * xn, axis=0, keepdims=True))
        dx_ref[...] = dx
        dxb_ref[...] = dx.astype(BF16)

        @pl.when(pl.program_id(0) == 0)
        def _():
            for dg_ref, part in zip(dg_refs, parts):
                dg_ref[...] = part

        @pl.when(pl.program_id(0) > 0)
        def _():
            for dg_ref, part in zip(dg_refs, parts):
                dg_ref[...] += part

    row = pl.BlockSpec((TR, d), lambda i: (i, 0))
    vec = pl.BlockSpec((1, d), lambda i: (0, 0))
    args = [x] + list(gains) + list(dys) + ([dres] if has_res else [])
    in_specs = [row] + [vec] * n + [row] * n + ([row] if has_res else [])
    outs = _pallas(
        body, name=name, grid=(T // TR,), in_specs=in_specs, out_specs=[row, row] + [vec] * n,
        out_shape=[jax.ShapeDtypeStruct((T, d), F32), jax.ShapeDtypeStruct((T, d), BF16)]
        + [jax.ShapeDtypeStruct((1, d), F32)] * n,
        params=_params(("arbitrary",)))(*args)
    return outs[0], outs[1], list(outs[2:])


def _final(h, g, tgt):
    def body(h_ref, g_ref, t_ref, loss_ref, dh_ref, dhb_ref, dg_ref):
        hv = h_ref[...]
        r = lax.rsqrt(jnp.mean(hv * hv, axis=-1, keepdims=True) + EPS)
        xn = hv * r
        gv = g_ref[...]
        err = xn * gv - t_ref[...]
        part_loss = 0.5 * jnp.sum(jnp.mean(err * err, axis=-1, keepdims=True), axis=0, keepdims=True)
        dy = err * (1.0 / D)
        gdy = dy * gv
        dh = r * (gdy - xn * jnp.mean(gdy * xn, axis=-1, keepdims=True))
        dh_ref[...] = dh
        dhb_ref[...] = dh.astype(BF16)
        part = jnp.sum(dy * xn, axis=0, keepdims=True)
        first = pl.program_id(0) == 0

        @pl.when(first)
        def _():
            dg_ref[...] = part
            loss_ref[...] = jnp.broadcast_to(part_loss, (1, 128))

        @pl.when(jnp.logical_not(first))
        def _():
            dg_ref[...] += part
            loss_ref[...] += jnp.broadcast_to(part_loss, (1, 128))

    row = pl.BlockSpec((TR, D), lambda i: (i, 0))
    vec = pl.BlockSpec((1, D), lambda i: (0, 0))
    return _pallas(
        body, name="final_loss", grid=(T // TR,), in_specs=[row, vec, row],
        out_specs=[pl.BlockSpec((1, 128), lambda i: (0, 0)), row, row, vec],
        out_shape=[jax.ShapeDtypeStruct((1, 128), F32), jax.ShapeDtypeStruct((T, D), F32),
                   jax.ShapeDtypeStruct((T, D), BF16), jax.ShapeDtypeStruct((1, D), F32)],
        params=_params(("arbitrary",)))(h, g, tgt)


def _prev_idx(i):
    return jnp.maximum(i * (TR // HALO) - 1, 0)


def _next_idx(i):
    return jnp.minimum((i + 1) * (TR // HALO), T // HALO - 1)


def _causal_taps(ext):
    return pltpu.roll(ext, 2, 0)[HALO:], pltpu.roll(ext, 1, 0)[HALO:], ext[HALO:]


def _anticausal_taps(ext, n):
    rows = ext.shape[0]
    return pltpu.roll(ext, rows - 1, 0)[:n], pltpu.roll(ext, rows - 2, 0)[:n]


def _sc_fwd(z, w):
    def body(b_ref, c_ref, ch_ref, u_ref, uh_ref, w_ref, y_ref):
        i = pl.program_id(0)
        cu = c_ref[...].astype(F32) * u_ref[...].astype(F32)
        cuh = ch_ref[...].astype(F32) * uh_ref[...].astype(F32)
        cuh = jnp.where(i > 0, cuh, 0.0)
        x2, x1, x0 = _causal_taps(jnp.concatenate([cuh, cu], axis=0))
        wv = w_ref[...]
        cv = (x2 * wv[0:1] + x1 * wv[1:2]) + x0 * wv[2:3]
        y_ref[...] = (b_ref[...].astype(F32) * cv).astype(BF16)

    def main(part):
        return pl.BlockSpec((TR, D), lambda i: (i, part))

    def halo(part):
        return pl.BlockSpec((HALO, D), lambda i: (_prev_idx(i), part))

    return _pallas(
        body, name="sc_fwd", grid=(T // TR,),
        in_specs=[main(0), main(1), halo(1), main(2), halo(2), pl.BlockSpec((3, D), lambda i: (0, 0))],
        out_specs=pl.BlockSpec((TR, D), lambda i: (i, 0)),
        out_shape=jax.ShapeDtypeStruct((T, D), BF16), params=_params(("parallel",)))(z, z, z, z, z, w)


def _sc_bwd(z, dy, w):
    last = T // TR - 1

    def body(b_ref, bn_ref, c_ref, ch_ref, u_ref, uh_ref, dy_ref, dyn_ref, w_ref, dz_ref, dw_ref):
        i = pl.program_id(0)
        cv_ = c_ref[...].astype(F32)
        uv = u_ref[...].astype(F32)
        cu = cv_ * uv
        cuh = jnp.where(i > 0, ch_ref[...].astype(F32) * uh_ref[...].astype(F32), 0.0)
        x2, x1, x0 = _causal_taps(jnp.concatenate([cuh, cu], axis=0))
        wv = w_ref[...]
        conv = (x2 * wv[0:1] + x1 * wv[1:2]) + x0 * wv[2:3]
        dyv = dy_ref[...]
        dz_ref[:, 0:D] = (dyv * conv).astype(BF16)
        dconv = dyv * b_ref[...].astype(F32)
        dconv_n = jnp.where(i < last, dyn_ref[...] * bn_ref[...].astype(F32), 0.0)
        n1, n2 = _anticausal_taps(jnp.concatenate([dconv, dconv_n], axis=0), TR)
        dcu = (dconv * wv[2:3] + n1 * wv[1:2]) + n2 * wv[0:1]
        dz_ref[:, D:2 * D] = (dcu * uv).astype(BF16)
        dz_ref[:, 2 * D:3 * D] = (dcu * cv_).astype(BF16)
        part = jnp.concatenate([jnp.sum(dconv * x2, axis=0, keepdims=True),
                                jnp.sum(dconv * x1, axis=0, keepdims=True),
                                jnp.sum(dconv * x0, axis=0, keepdims=True)], axis=0)

        @pl.when(i == 0)
        def _():
            dw_ref[...] = part

        @pl.when(i > 0)
        def _():
            dw_ref[...] += part

    def main(part):
        return pl.BlockSpec((TR, D), lambda i: (i, part))

    def prev(part):
        return pl.BlockSpec((HALO, D), lambda i: (_prev_idx(i), part))

    def nxt(part):
        return pl.BlockSpec((HALO, D), lambda i: (_next_idx(i), part))

    wspec = pl.BlockSpec((3, D), lambda i: (0, 0))
    return _pallas(
        body, name="sc_bwd", grid=(T // TR,),
        in_specs=[main(0), nxt(0), main(1), prev(1), main(2), prev(2), main(0), nxt(0), wspec],
        out_specs=[pl.BlockSpec((TR, 3 * D), lambda i: (i, 0)), wspec],
        out_shape=[jax.ShapeDtypeStruct((T, 3 * D), BF16), jax.ShapeDtypeStruct((3, D), F32)],
        params=_params(("arbitrary",)))(z, z, z, z, z, z, dy, dy, w)


def _sigmoid(x):
    return 1.0 / (1.0 + jnp.exp(-x))


def _ffn_fwd(name, gv, w, b):
    def body(g_ref, gh_ref, v_ref, w_ref, b_ref, a_ref):
        i = pl.program_id(1)
        g = g_ref[...].astype(F32)
        gh = jnp.where(i > 0, gh_ref[...].astype(F32), 0.0)
        x2, x1, x0 = _causal_taps(jnp.concatenate([gh, g], axis=0))
        wv = w_ref[...]
        gc = ((x2 * wv[0:1] + x1 * wv[1:2]) + x0 * wv[2:3]) + b_ref[...]
        a_ref[...] = ((gc * _sigmoid(gc)) * v_ref[...].astype(F32)).astype(BF16)

    blk = (None, TR, FF_BLK)
    return _pallas(
        body, name=name, grid=(N_FF_BLK, T // TR),
        in_specs=[pl.BlockSpec(blk, lambda j, i: (j, i, 0)),
                  pl.BlockSpec((None, HALO, FF_BLK), lambda j, i: (j, _prev_idx(i), 0)),
                  pl.BlockSpec(blk, lambda j, i: (j + N_FF_BLK, i, 0)),
                  pl.BlockSpec((None, 3, FF_BLK), lambda j, i: (j, 0, 0)),
                  pl.BlockSpec((None, 1, FF_BLK), lambda j, i: (j, 0, 0))],
        out_specs=pl.BlockSpec(blk, lambda j, i: (j, i, 0)),
        out_shape=jax.ShapeDtypeStruct((N_FF_BLK, T, FF_BLK), BF16),
        params=_params(("parallel", "parallel")))(gv, gv, gv, w, b)


def _ffn_bwd(name, gv, dact, w, b):
    last = T // TR - 1

    def body(g_ref, gp_ref, gn_ref, v_ref, vn_ref, da_ref, dan_ref, w_ref, b_ref, dg_ref, dv_ref, dw_ref, db_ref):
        i = pl.program_id(1)
        gp = jnp.where(i > 0, gp_ref[...].astype(F32), 0.0)
        ext = jnp.concatenate([gp, g_ref[...].astype(F32), gn_ref[...].astype(F32)], axis=0)
        x2, x1, x0 = _causal_taps(ext)
        wv = w_ref[...]
        gc = ((x2 * wv[0:1] + x1 * wv[1:2]) + x0 * wv[2:3]) + b_ref[...]
        sg = _sigmoid(gc)
        da = jnp.concatenate([da_ref[...].astype(F32), jnp.where(i < last, dan_ref[...].astype(F32), 0.0)], axis=0)
        vv = jnp.concatenate([v_ref[...].astype(F32), vn_ref[...].astype(F32)], axis=0)
        dv_ref[...] = (da[:TR] * (gc[:TR] * sg[:TR])).astype(BF16)
        dgc = (da * vv) * (sg * (1.0 + gc * (1.0 - sg)))
        n1, n2 = _anticausal_taps(dgc, TR)
        d0 = dgc[:TR]
        dg_ref[...] = ((d0 * wv[2:3] + n1 * wv[1:2]) + n2 * wv[0:1]).astype(BF16)
        part_w = jnp.concatenate([jnp.sum(d0 * x2[:TR], axis=0, keepdims=True),
                                  jnp.sum(d0 * x1[:TR], axis=0, keepdims=True),
                                  jnp.sum(d0 * x0[:TR], axis=0, keepdims=True)], axis=0)
        part_b = jnp.sum(d0, axis=0, keepdims=True)

        @pl.when(i == 0)
        def _():
            dw_ref[...] = part_w
            db_ref[...] = part_b

        @pl.when(i > 0)
        def _():
            dw_ref[...] += part_w
            db_ref[...] += part_b

    blk = (None, TR, FF_BLK)
    hblk = (None, HALO, FF_BLK)
    wspec = pl.BlockSpec((None, 3, FF_BLK), lambda j, i: (j, 0, 0))
    bspec = pl.BlockSpec((None, 1, FF_BLK), lambda j, i: (j, 0, 0))
    return _pallas(
        body, name=name, grid=(N_FF_BLK, T // TR),
        in_specs=[pl.BlockSpec(blk, lambda j, i: (j, i, 0)),
                  pl.BlockSpec(hblk, lambda j, i: (j, _prev_idx(i), 0)),
                  pl.BlockSpec(hblk, lambda j, i: (j, _next_idx(i), 0)),
                  pl.BlockSpec(blk, lambda j, i: (j + N_FF_BLK, i, 0)),
                  pl.BlockSpec(hblk, lambda j, i: (j + N_FF_BLK, _next_idx(i), 0)),
                  pl.BlockSpec(blk, lambda j, i: (j, i, 0)),
                  pl.BlockSpec(hblk, lambda j, i: (j, _next_idx(i), 0)),
                  wspec, bspec],
        out_specs=[pl.BlockSpec(blk, lambda j, i: (j, i, 0)), pl.BlockSpec(blk, lambda j, i: (j, i, 0)), wspec, bspec],
        out_shape=[jax.ShapeDtypeStruct((N_FF_BLK, T, FF_BLK), BF16), jax.ShapeDtypeStruct((N_FF_BLK, T, FF_BLK), BF16),
                   jax.ShapeDtypeStruct((N_FF_BLK, 3, FF_BLK), F32), jax.ShapeDtypeStruct((N_FF_BLK, 1, FF_BLK), F32)],
        params=_params(("parallel", "arbitrary")))(gv, gv, gv, gv, gv, dact, dact, w, b)


def _rope_tables(pos, inv_freq):
    half = QK_ROPE // 2

    def body(p_ref, f_ref, c_ref, sa_ref, sb_ref):
        ang = p_ref[...].astype(F32) * f_ref[...]
        lane = lax.broadcasted_iota(jnp.int32, (T, 128), 1)
        c = jnp.cos(ang)
        s = jnp.sin(ang)
        c_ref[...] = jnp.where(lane < 2 * half, c, 0.0)
        sa_ref[...] = jnp.where(lane < half, -s, 0.0)
        sb_ref[...] = jnp.where(jnp.logical_and(lane >= half, lane < 2 * half), s, 0.0)

    return _pallas(
        body, name="rope_tables", in_specs=[VMEM_SPEC] * 2, out_specs=[VMEM_SPEC] * 3,
        out_shape=[jax.ShapeDtypeStruct((T, 128), F32)] * 3,
        params=pltpu.CompilerParams(vmem_limit_bytes=VMEM_LIMIT))(pos, inv_freq)


def _rotate(r, c, sa, sb, sign):
    return r * c + sign * (pltpu.roll(r, 96, 1) * sa + pltpu.roll(r, 32, 1) * sb)


def _q_up(cq, w_uq, tables):
    cos, sa, sb = tables

    def body(a_ref, b_ref, c_ref, sa_ref, sb_ref, o_ref):
        r = lax.dot_general(a_ref[...], b_ref[...], NN, preferred_element_type=F32)
        o_ref[:, :QK_NOPE] = r[:, :QK_NOPE].astype(BF16)
        o_ref[:, QK_NOPE:] = _rotate(r[:, QK_NOPE:], c_ref[...], sa_ref[...], sb_ref[...], 1.0).astype(BF16)

    tab = pl.BlockSpec((TM, 128), lambda h, i: (i, 0))
    return _pallas(
        body, name="q_up", grid=(N_HEADS, T // TM),
        in_specs=[pl.BlockSpec((TM, Q_LORA), lambda h, i: (i, 0)),
                  pl.BlockSpec((None, Q_LORA, QK_PAD), lambda h, i: (h, 0, 0)), tab, tab, tab],
        out_specs=pl.BlockSpec((None, TM, QK_PAD), lambda h, i: (h, i, 0)),
        out_shape=jax.ShapeDtypeStruct((N_HEADS, T, QK_PAD), BF16),
        params=_params(("parallel", "parallel")))(cq, w_uq, cos, sa, sb)


def _rope(name, x, tables, sign, out_dtype, reduce_groups=False):
    g, _, w = x.shape
    cos, sa, sb = tables

    def body(x_ref, c_ref, sa_ref, sb_ref, o_ref):
        xv = x_ref[...].astype(F32)
        if reduce_groups:
            acc = xv[0]
            for k in range(1, g):
                acc = acc + xv[k]
            xv = acc
        out = _rotate(xv[:, w - 128:], c_ref[...], sa_ref[...], sb_ref[...], sign)
        if w > 128:
            o_ref[:, :w - 128] = xv[:, :w - 128].astype(out_dtype)
        o_ref[:, w - 128:] = out.astype(out_dtype)

    tab = pl.BlockSpec((TM, 128), lambda h, i: (i, 0))
    if reduce_groups:
        x_spec = pl.BlockSpec((g, TM, w), lambda h, i: (0, i, 0))
        groups = 1
    else:
        x_spec = pl.BlockSpec((None, TM, w), lambda h, i: (h, i, 0))
        groups = g
    return _pallas(
        body, name=name, grid=(groups, T // TM), in_specs=[x_spec, tab, tab, tab],
        out_specs=pl.BlockSpec((None, TM, w), lambda h, i: (h, i, 0)),
        out_shape=jax.ShapeDtypeStruct((groups, T, w), out_dtype),
        params=_params(("parallel", "parallel")))(x, cos, sa, sb)


SCALE = (QK_NOPE + QK_ROPE) ** -0.5
LOG2E = 1.4426950408889634
SCALE2 = SCALE * LOG2E


def _diag_mask(transposed):
    shift = CHUNK.bit_length() - 1
    a = lax.broadcasted_iota(jnp.int32, (TQ, TQ), 0) >> shift
    b = lax.broadcasted_iota(jnp.int32, (TQ, TQ), 1) >> shift
    return (a <= b) if transposed else (b <= a)


def _keys(kn_ref, kr_ref, off):
    return jnp.concatenate([kn_ref[pl.ds(off, TQ), :], kr_ref[pl.ds(off, TQ), :]], axis=1)


def _attn_fwd(q, kn, kr, v):
    def body(q_ref, kn_ref, kr_ref, v_ref, o_ref, lse_ref):
        i = pl.program_id(1)
        qv = q_ref[...]

        def step(j, carry, masked):
            m, l, acc = carry
            off = pl.multiple_of(j * TQ, TQ)
            s = lax.dot_general(qv, _keys(kn_ref, kr_ref, off), NT, preferred_element_type=F32) * SCALE2
            if masked:
                s = jnp.where(_diag_mask(False), s, NEG_INF)
            m_new = jnp.maximum(m, jnp.max(s, axis=-1, keepdims=True))
            p = jnp.exp2(s - m_new)
            alpha = jnp.exp2(m - m_new)
            l = alpha * l + jnp.sum(p, axis=-1, keepdims=True)
            acc = alpha * acc + lax.dot_general(p.astype(BF16), v_ref[pl.ds(off, TQ), :], NN, preferred_element_type=F32)
            return m_new, l, acc

        init = (jnp.full((TQ, 1), NEG_INF, F32), jnp.zeros((TQ, 1), F32), jnp.zeros((TQ, V_HEAD), F32))
        carry = lax.fori_loop(0, i, lambda j, cr: step(j, cr, False), init)
        m, l, acc = step(i, carry, True)
        o_ref[...] = (acc / l).astype(BF16)
        lse_ref[...] = m + jnp.log(l) * LOG2E

    return _pallas(
        body, name="attn_fwd", grid=(N_HEADS, T // TQ),
        in_specs=[pl.BlockSpec((None, TQ, QK_PAD), lambda h, i: (h, i, 0)),
                  pl.BlockSpec((T, QK_NOPE), lambda h, i: (0, h)),
                  pl.BlockSpec((T, 128), lambda h, i: (0, 0)),
                  pl.BlockSpec((T, V_HEAD), lambda h, i: (0, h))],
        out_specs=[pl.BlockSpec((TQ, V_HEAD), lambda h, i: (i, h)), pl.BlockSpec((None, TQ, 1), lambda h, i: (h, i, 0))],
        out_shape=[jax.ShapeDtypeStruct((T, N_HEADS * V_HEAD), BF16), jax.ShapeDtypeStruct((N_HEADS, T, 1), F32)],
        params=_params(("parallel", "parallel")))(q, kn, kr, v)


def _attn_bwd_dq(q, kn, kr, v, o, do, lse, tables):
    cos, sa, sb = tables

    def body(q_ref, kn_ref, kr_ref, v_ref, o_ref, do_ref, lse_ref, c_ref, sa_ref, sb_ref, dq_ref, dl_ref):
        i = pl.program_id(1)
        qv = q_ref[...]
        dov = do_ref[...]
        lse = lse_ref[...]
        delta = jnp.sum(dov.astype(F32) * o_ref[...].astype(F32), axis=-1, keepdims=True)
        dl_ref[...] = delta

        def step(j, dq, masked):
            off = pl.multiple_of(j * TQ, TQ)
            kk = _keys(kn_ref, kr_ref, off)
            s = lax.dot_general(qv, kk, NT, preferred_element_type=F32) * SCALE2
            if masked:
                s = jnp.where(_diag_mask(False), s, NEG_INF)
            p = jnp.exp2(s - lse)
            dp = lax.dot_general(dov, v_ref[pl.ds(off, TQ), :], NT, preferred_element_type=F32)
            ds = (p * (dp - delta)) * SCALE
            return dq + lax.dot_general(ds.astype(BF16), kk, NN, preferred_element_type=F32)

        dq = lax.fori_loop(0, i, lambda j, acc: step(j, acc, False), jnp.zeros((TQ, QK_PAD), F32))
        dq = step(i, dq, True)
        dq_ref[:, :QK_NOPE] = dq[:, :QK_NOPE].astype(BF16)
        dq_ref[:, QK_NOPE:] = _rotate(dq[:, QK_NOPE:], c_ref[...], sa_ref[...], sb_ref[...], -1.0).astype(BF16)

    col = pl.BlockSpec((None, TQ, 1), lambda h, i: (h, i, 0))
    head = pl.BlockSpec((TQ, V_HEAD), lambda h, i: (i, h))
    tab = pl.BlockSpec((TQ, 128), lambda h, i: (i, 0))
    return _pallas(
        body, name="attn_bwd_dq", grid=(N_HEADS, T // TQ),
        in_specs=[pl.BlockSpec((None, TQ, QK_PAD), lambda h, i: (h, i, 0)),
                  pl.BlockSpec((T, QK_NOPE), lambda h, i: (0, h)),
                  pl.BlockSpec((T, 128), lambda h, i: (0, 0)),
                  pl.BlockSpec((T, V_HEAD), lambda h, i: (0, h)),
                  head, head, col, tab, tab, tab],
        out_specs=[pl.BlockSpec((None, TQ, QK_PAD), lambda h, i: (h, i, 0)), col],
        out_shape=[jax.ShapeDtypeStruct((N_HEADS, T, QK_PAD), BF16), jax.ShapeDtypeStruct((N_HEADS, T, 1), F32)],
        params=_params(("parallel", "parallel")))(q, kn, kr, v, o, do, lse, cos, sa, sb)


def _attn_bwd_dkv(q, kn, kr, v, do, lse_row, delta_row):
    nq = T // TQ

    def body(q_ref, kn_ref, kr_ref, v_ref, do_ref, lse_ref, dl_ref, dkn_ref, dkr_ref, dv_ref):
        j = pl.program_id(1)
        kk = jnp.concatenate([kn_ref[...], kr_ref[...]], axis=1)
        vv = v_ref[...]

        def step(i, carry, masked):
            dk, dv = carry
            off = pl.multiple_of(i * TQ, TQ)
            qi = q_ref[pl.ds(off, TQ), :]
            doi = do_ref[pl.ds(off, TQ), :]
            st = lax.dot_general(kk, qi, NT, preferred_element_type=F32) * SCALE2
            if masked:
                st = jnp.where(_diag_mask(True), st, NEG_INF)
            pt = jnp.exp2(st - lse_ref[:, pl.ds(off, TQ)])
            dv = dv + lax.dot_general(pt.astype(BF16), doi, NN, preferred_element_type=F32)
            dpt = lax.dot_general(vv, doi, NT, preferred_element_type=F32)
            dst = (pt * (dpt - dl_ref[:, pl.ds(off, TQ)])) * SCALE
            dk = dk + lax.dot_general(dst.astype(BF16), qi, NN, preferred_element_type=F32)
            return dk, dv

        carry = step(j, (jnp.zeros((TQ, QK_PAD), F32), jnp.zeros((TQ, V_HEAD), F32)), True)
        dk, dv = lax.fori_loop(j + 1, nq, lambda i, cr: step(i, cr, False), carry)
        dkn_ref[...] = dk[:, :QK_NOPE].astype(BF16)
        dkr_ref[...] = dk[:, QK_NOPE:]
        dv_ref[...] = dv.astype(BF16)

    row = pl.BlockSpec((None, 1, T), lambda h, j: (h, 0, 0))
    head = pl.BlockSpec((TQ, 128), lambda h, j: (j, h))
    return _pallas(
        body, name="attn_bwd_dkv", grid=(N_HEADS, nq),
        in_specs=[pl.BlockSpec((None, T, QK_PAD), lambda h, j: (h, 0, 0)),
                  head, pl.BlockSpec((TQ, 128), lambda h, j: (j, 0)), head,
                  pl.BlockSpec((T, V_HEAD), lambda h, j: (0, h)), row, row],
        out_specs=[head, pl.BlockSpec((None, TQ, 128), lambda h, j: (h, j, 0)), head],
        out_shape=[jax.ShapeDtypeStruct((T, N_HEADS * QK_NOPE), BF16), jax.ShapeDtypeStruct((N_HEADS, T, 128), F32),
                   jax.ShapeDtypeStruct((T, N_HEADS * V_HEAD), BF16)],
        params=_params(("parallel", "parallel")))(q, kn, kr, v, do, lse_row, delta_row)


def _ffn_gup(name, dg, dv, hf):
    def body(dg_ref, dv_ref, hf_ref, o_ref):
        j = pl.program_id(0)

        @pl.when(j < N_FF_BLK)
        def _():
            o_ref[...] = lax.dot_general(dg_ref[...], hf_ref[...], TN, preferred_element_type=F32).astype(BF16)

        @pl.when(j >= N_FF_BLK)
        def _():
            o_ref[...] = lax.dot_general(dv_ref[...], hf_ref[...], TN, preferred_element_type=F32).astype(BF16)

    return _pallas(
        body, name=name, grid=(N_DEV,),
        in_specs=[pl.BlockSpec((None, T, FF_BLK), lambda j: (jnp.minimum(j, N_FF_BLK - 1), 0, 0)),
                  pl.BlockSpec((None, T, FF_BLK), lambda j: (jnp.maximum(j - N_FF_BLK, 0), 0, 0)),
                  pl.BlockSpec((T, D), lambda j: (0, 0))],
        out_specs=pl.BlockSpec((None, FF_BLK, D), lambda j: (j, 0, 0)),
        out_shape=jax.ShapeDtypeStruct((N_DEV, FF_BLK, D), BF16), params=_params(("parallel",)))(dg, dv, hf)


def _ffn_layer_fwd(tag, h, gain, ex):
    hf = _rms_fwd(f"{tag}_norm", h, gain)
    gv = _mm(f"{tag}_up", hf, ex.need(f"ffn_w_up{tag[1]}", hf), grid=(N_DEV, T // TM),
             a_spec=pl.BlockSpec((TM, D), lambda j, i: (i, 0)),
             b_spec=pl.BlockSpec((None, None, FF_BLK, D), lambda j, i: (0, j, 0, 0)),
             o_spec=pl.BlockSpec((None, TM, FF_BLK), lambda j, i: (j, i, 0)),
             o_shape=(N_DEV, T, FF_BLK), o_dtype=BF16, dims=NT)
    ex.at(f"{tag}_up", gv)
    act = _ffn_fwd(f"{tag}_act", gv, ex.need(f"ffn_cw{tag[1]}", gv), ex.need(f"ffn_cb{tag[1]}", gv))
    ex.at(f"{tag}_act", act)
    rows = pl.BlockSpec((TS, D), lambda i: (i, 0))
    out = _mm_sum(f"{tag}_down",
                  [(act, pl.BlockSpec((N_FF_BLK, TS, FF_BLK), lambda i: (0, i, 0)), ex.need(f"ffn_w_down{tag[1]}", act),
                    pl.BlockSpec((None, N_FF_BLK, FF_BLK, D), lambda i: (0, 0, 0, 0)), NN)],
                  grid=(T // TS,), o_spec=rows, o_shape=(T, D), o_dtype=F32, add=h)
    ex.at(f"{tag}_down", out)
    return out, (hf, gv, act)


def _ffn_layer_bwd(tag, h, gain, ex, saved, dh, dh_bf):
    hf, gv, act = saved
    layer = tag[1]
    w_up, w_down4 = ex.need(f"ffn_w_up{layer}", dh_bf), ex.need(f"ffn_w_down{layer}", dh_bf)
    dact = _mm(f"{tag}_dact", dh_bf, w_down4, grid=(N_FF_BLK, T // TM),
               a_spec=pl.BlockSpec((TM, D), lambda j, i: (i, 0)),
               b_spec=pl.BlockSpec((None, None, FF_BLK, D), lambda j, i: (0, j, 0, 0)),
               o_spec=pl.BlockSpec((None, TM, FF_BLK), lambda j, i: (j, i, 0)),
               o_shape=(N_FF_BLK, T, FF_BLK), o_dtype=BF16, dims=NT)
    tn = 512
    g_down = _mm(f"{tag}_gdown", act, dh_bf, grid=(N_FF_BLK, D // tn),
                 a_spec=pl.BlockSpec((None, T, FF_BLK), lambda j, n: (j, 0, 0)),
                 b_spec=pl.BlockSpec((T, tn), lambda j, n: (0, n)),
                 o_spec=pl.BlockSpec((FF_BLK, tn), lambda j, n: (j, n)),
                 o_shape=(D_FF, D), o_dtype=BF16, dims=TN)
    dg, dv, dcw, dcb = _ffn_bwd(f"{tag}_dact_ew", gv, dact, ex.need(f"ffn_cw{layer}", dact), ex.need(f"ffn_cb{layer}", dact))
    ex.at(f"{tag}_dact_ew", dg)
    g_up = _ffn_gup(f"{tag}_gup", dg, dv, hf)
    ex.grad("ffn_w_up", int(layer), g_up.reshape(1, N_DEV, FF_BLK, D))
    ex.grad("ffn_w_down", int(layer), g_down.reshape(1, N_DEV, D_FF // N_DEV, D))
    ex.at(f"{tag}_gup", g_up)
    part = pl.BlockSpec((N_FF_BLK, TS, FF_BLK), lambda i: (0, i, 0))
    dhf = _mm_sum(f"{tag}_dhf",
                  [(dg, part, w_up, pl.BlockSpec((None, N_FF_BLK, FF_BLK, D), lambda i: (0, 0, 0, 0)), NN),
                   (dv, part, w_up, pl.BlockSpec((None, N_FF_BLK, FF_BLK, D), lambda i: (0, 1, 0, 0)), NN)],
                  grid=(T // TS,), o_spec=pl.BlockSpec((TS, D), lambda i: (i, 0)), o_shape=(T, D), o_dtype=F32)
    ex.at(f"{tag}_dhf", dhf)
    dh_in, dh_in_bf, dgain = _rms_bwd(f"{tag}_dnorm", h, [gain], [dhf], dres=dh)
    return dh_in, dh_in_bf, dgain[0], dcw, dcb


def _local_step(x, pos, tgt, rep, ex):
    attn_norm, ffn_norm, final_norm = rep["attn_norm"], rep["ffn_norm"], rep["final_norm"]
    half = QK_ROPE // 2
    inv = 1.0 / (ROPE_THETA ** (jnp.arange(half, dtype=F32) / half))
    inv_freq = jnp.concatenate([inv, inv, jnp.zeros((128 - 2 * half,), F32)]).reshape(1, 128)
    tables = _rope_tables(pos, inv_freq)

    hn0 = _rms_fwd("l0_norm", x, attn_norm[0:1])
    w_in = ex.need("sc_w_in", hn0)
    ex.at("mixer_ready", hn0)
    z = _mm_rows("l0_in", hn0, w_in, NN, BF16, 3 * D, tn=512)
    ex.at("l0_in", z)
    y = _sc_fwd(z, ex.need("sc_conv_w", z))
    h1 = _mm_rows("l0_out", y, ex.need("sc_w_out", y), NN, F32, D, tn=512, add=x)
    ex.at("l0_out", h1)
    h2, ffn0 = _ffn_layer_fwd("f0", h1, ffn_norm[0:1], ex)

    hk = _rms_fwd("kv_norm", h2, rep["kv_in_norm"])
    ckv_raw = _mm_rows("kv_down", hk, ex.need("w_dkv", hk), NN, F32, KV_LORA)
    kr_raw = _mm_rows("kv_rope", hk, ex.need("w_kr", hk), NN, F32, 128)
    ckv = _rms_fwd("kv_lnorm", ckv_raw, rep["kv_latent_norm"])
    kn = _mm_rows("kv_uk", ckv, ex.need("w_uk", ckv), NN, BF16, N_HEADS * QK_NOPE)
    vv = _mm_rows("kv_uv", ckv, ex.need("w_uv", ckv), NN, BF16, N_HEADS * V_HEAD)
    kr = _rope("k_rope", kr_raw.reshape(1, T, 128), tables, 1.0, BF16).reshape(T, 128)

    hn1 = _rms_fwd("l1_norm", h2, attn_norm[1:2])
    cq_raw = _mm_rows("q_down", hn1, ex.need("w_dq", hn1), NN, F32, Q_LORA)
    cq = _rms_fwd("q_lnorm", cq_raw, rep["q_latent_norm"])
    w_uq = ex.need("w_uq", cq)
    q = _q_up(cq, w_uq, tables)
    o, lse = _attn_fwd(q, kn, kr, vv)
    ex.at("attn_fwd", o)
    w_o = ex.need("w_o", o)
    h3 = _mm_rows("attn_out", o, w_o, NN, F32, D, tn=512, add=h2)
    h4, ffn1 = _ffn_layer_fwd("f1", h3, ffn_norm[1:2], ex)

    loss, dh4, dh4_bf, d_final = _final(h4, final_norm.reshape(1, D), tgt)

    dh3, dh3_bf, d_fn1, dcw1, dcb1 = _ffn_layer_bwd("f1", h3, ffn_norm[1:2], ex, ffn1, dh4, dh4_bf)
    ex.at("f1_bwd", dh3)

    do = _mm_rows("d_attn_out", dh3_bf, w_o, NT, BF16, N_HEADS * V_HEAD)
    ex.grad("w_o", None, _mm_wgrad("g_w_o", o, dh3_bf).reshape(1, N_DEV, D // N_DEV, D))
    dq_pre, delta = _attn_bwd_dq(q, kn, kr, vv, o, do, lse, tables)
    ex.at("attn_dq", dq_pre)
    dkn, dkr, dvv = _attn_bwd_dkv(q, kn, kr, vv, do, lse.reshape(N_HEADS, 1, T), delta.reshape(N_HEADS, 1, T))
    dcq = _mm("d_q_up", dq_pre, w_uq, grid=(T // TM, N_HEADS),
              a_spec=pl.BlockSpec((None, TM, QK_PAD), lambda i, h: (h, i, 0)),
              b_spec=pl.BlockSpec((None, Q_LORA, QK_PAD), lambda i, h: (h, 0, 0)),
              o_spec=pl.BlockSpec((TM, Q_LORA), lambda i, h: (i, 0)), o_shape=(T, Q_LORA), o_dtype=F32,
              dims=NT, k_axis=1, acc_shape=(TM, Q_LORA))
    g_uq = _mm("g_w_uq", cq, dq_pre, grid=(N_HEADS,),
               a_spec=pl.BlockSpec((T, Q_LORA), lambda h: (0, 0)),
               b_spec=pl.BlockSpec((None, T, QK_PAD), lambda h: (h, 0, 0)),
               o_spec=pl.BlockSpec((None, Q_LORA, QK_PAD), lambda h: (h, 0, 0)),
               o_shape=(N_HEADS, Q_LORA, QK_PAD), o_dtype=BF16, dims=TN)
    ex.grad("w_uq", None, g_uq[:, :, :QK_NOPE + QK_ROPE].reshape(1, N_DEV, Q_LORA, QK_NOPE + QK_ROPE))
    _, dcq_raw_bf, (d_qln,) = _rms_bwd("d_q_lnorm", cq_raw, [rep["q_latent_norm"]], [dcq])
    dhn1 = _mm_rows("d_q_down", dcq_raw_bf, ex.need("w_dq", dcq_raw_bf), NT, F32, D)
    ex.grad("w_dq", None, _mm_wgrad("g_w_dq", hn1, dcq_raw_bf).reshape(1, N_DEV, D // N_DEV, Q_LORA))

    dckv = _mm_rows("d_kv_uk", dkn, ex.need("w_uk", dkn), NT, F32, KV_LORA)
    dckv = _mm_rows("d_kv_uv", dvv, ex.need("w_uv", dvv), NT, F32, KV_LORA, add=dckv)
    ex.grad("w_uk", None, _mm_wgrad("g_w_uk", ckv, dkn))
    ex.grad("w_uv", None, _mm_wgrad("g_w_uv", ckv, dvv))
    _, dckv_raw_bf, (d_kvln,) = _rms_bwd("d_kv_lnorm", ckv_raw, [rep["kv_latent_norm"]], [dckv])
    dkr_raw_bf = _rope("dk_rope", dkr, tables, -1.0, BF16, reduce_groups=True).reshape(T, 128)
    dhk = _mm_rows("d_kv_down", dckv_raw_bf, ex.need("w_dkv", dckv_raw_bf), NT, F32, D)
    dhk = _mm_rows("d_kv_rope", dkr_raw_bf, ex.need("w_kr", dkr_raw_bf), NT, F32, D, add=dhk)
    ex.grad("w_dkv", None, _mm_wgrad("g_w_dkv", hk, dckv_raw_bf).reshape(1, N_DEV, D // N_DEV, KV_LORA))
    ex.grad("w_kr", None, _mm_wgrad("g_w_kr", hk, dkr_raw_bf)[:, :QK_ROPE].reshape(1, N_DEV, D // N_DEV, QK_ROPE))
    dh2, dh2_bf, (d_an1, d_kvin) = _rms_bwd("d_h2_norms", h2, [attn_norm[1:2], rep["kv_in_norm"]], [dhn1, dhk], dres=dh3)
    ex.at("kv_bwd", dh2)

    dh1, dh1_bf, d_fn0, dcw0, dcb0 = _ffn_layer_bwd("f0", h1, ffn_norm[0:1], ex, ffn0, dh2, dh2_bf)
    ex.at("f0_bwd", dh1)

    dy = _mm_rows("d_l0_out", dh1_bf, ex.need("sc_w_out", dh1_bf), NT, F32, D)
    ex.grad("sc_w_out", None, _mm_wgrad("g_sc_w_out", y, dh1_bf).reshape(1, N_DEV, D // N_DEV, D))
    dz, d_scw = _sc_bwd(z, dy, ex.need("sc_conv_w", dy))
    g_in = _mm_wgrad("g_sc_w_in", hn0, dz)
    ex.grad("sc_w_in", None, g_in)
    ex.at("sc_bwd", g_in)
    dhn0 = _mm_rows("d_l0_in", dz, ex.need("sc_w_in", dz), NT, F32, D)
    ex.at("d_l0_in", dhn0)
    grad_x, _, (d_an0,) = _rms_bwd("d_l0_norm", x, [attn_norm[0:1]], [dhn0], dres=dh1)

    small = {
        "attn_norm": jnp.concatenate([d_an0, d_an1], axis=0),
        "ffn_norm": jnp.concatenate([d_fn0, d_fn1], axis=0),
        "final_norm": d_final.reshape(D),
        "kv_in_norm": d_kvin.reshape(D),
        "kv_latent_norm": d_kvln.reshape(KV_LORA),
        "q_latent_norm": d_qln,
        "ffn_conv_b": jnp.stack([dcb0, dcb1]).transpose(0, 2, 1, 3).reshape(2, D_FF),
        "sc_conv_w": d_scw,
        "ffn_conv_w": jnp.stack([dcw0, dcw1]).transpose(0, 2, 1, 3).reshape(2, 3, D_FF),
    }
    return loss, grad_x, small


def _place():
    return lax.axis_index("x"), lax.axis_index("y"), lax.axis_index("c")


def _peers():
    x, y, c = _place()
    return (x, y, 1 - c), [(1 - x, y), (x, 1 - y), (1 - x, 1 - y)]


def _window(ref, kind, dev):
    if kind == "blocked":
        return ref.at[:, dev]
    width = ref.shape[-1] // N_DEV
    return ref.at[:, pl.ds(pl.multiple_of(dev * width, 128), width)]


def _all_gather(name, items):
    n = len(items)
    out_shapes = []
    for shard, kind in items:
        if kind == "blocked":
            shape = (shard.shape[0], N_DEV) + shard.shape[1:]
        else:
            shape = (shard.shape[0], N_DEV * shard.shape[1])
        out_shapes.append(jax.ShapeDtypeStruct(shape, shard.dtype))

    def body(*refs):
        srcs, outs = refs[:n], refs[n:2 * n]
        send_sems, recv_sems, local_sems = refs[2 * n:]
        x, y, c = _place()
        me = 4 * x + 2 * y + c
        sibling, chips = _peers()

        def num(px, py, pc):
            return 4 * px + 2 * py + pc

        def copy(t, k, dev, to, from_src):
            kind = items[t][1]
            dst = _window(outs[t], kind, dev)
            return pltpu.make_async_remote_copy(
                src_ref=srcs[t] if from_src else dst, dst_ref=dst,
                send_sem=send_sems.at[t, k], recv_sem=recv_sems.at[t, k], device_id=to, device_id_type=MESH)

        mine = [pltpu.make_async_copy(srcs[t], _window(outs[t], items[t][1], me), local_sems.at[t]) for t in range(n)]
        for cp in mine:
            cp.start()
        first = []
        for t in range(n):
            first.append(copy(t, 0, me, sibling, True))
            for j, chip in enumerate(chips):
                first.append(copy(t, 1 + j, me, (*chip, c), True))
        for cp in first:
            cp.start()
        passed = []
        for j, chip in enumerate(chips):
            for t in range(n):
                copy(t, 1 + j, num(*chip, c), (x, y, c), False).wait_recv()
                fwd = copy(t, 4 + j, num(*chip, c), sibling, False)
                fwd.start()
                passed.append(fwd)
        for t in range(n):
            copy(t, 0, num(x, y, 1 - c), (x, y, c), False).wait_recv()
            for j, chip in enumerate(chips):
                copy(t, 4 + j, num(*chip, 1 - c), (x, y, c), False).wait_recv()
        for cp in first + passed:
            cp.wait_send()
        for cp in mine:
            cp.wait()

    return _pallas(
        body, name=name, in_specs=[ANY_SPEC] * n, out_specs=[ANY_SPEC] * n, out_shape=out_shapes,
        scratch_shapes=[pltpu.SemaphoreType.DMA((n, 7)), pltpu.SemaphoreType.DMA((n, 7)), pltpu.SemaphoreType.DMA((n,))],
    )(*[s for s, _ in items])


HBM_SPEC = pl.BlockSpec(memory_space=pltpu.HBM)
SEM_SPEC = pl.BlockSpec(memory_space=pltpu.SEMAPHORE)
EFFECT = pltpu.SideEffectType.DATAFLOW_SIDE_EFFECTING
TOKEN = jax.ShapeDtypeStruct((8, 128), F32)


def _hbm(a):
    return pltpu.with_memory_space_constraint(a, pltpu.HBM)


def _copies_start(name, srcs, lands, ncopy, plan):
    ns, nl = len(srcs), len(lands)

    def body(*refs):
        send, recv, token = refs[ns + nl], refs[ns + nl + 1], refs[-1]
        copies = plan(refs[:ns], refs[ns:ns + nl])
        assert len(copies) == ncopy
        for k, (sent, dst, to, _) in enumerate(copies):
            pltpu.make_async_remote_copy(src_ref=sent, dst_ref=dst, send_sem=send.at[k], recv_sem=recv.at[k],
                                         device_id=to, device_id_type=MESH).start()
        token[...] = jnp.zeros_like(token)

    arrays = list(srcs) + list(lands)
    outs = pl.pallas_call(
        body, name=name, in_specs=[HBM_SPEC] * (ns + nl),
        out_specs=[SEM_SPEC] * 2 + [HBM_SPEC] * (ns + nl) + [VMEM_SPEC],
        out_shape=[pltpu.SemaphoreType.DMA((ncopy,))] * 2 + [pltpu.HBM(a.shape, a.dtype) for a in arrays] + [TOKEN],
        input_output_aliases={i: 2 + i for i in range(ns + nl)},
        compiler_params=pltpu.CompilerParams(has_side_effects=EFFECT))(*[_hbm(a) for a in arrays])
    _Chain.last = outs[-1]
    return outs[0], outs[1], list(outs[2:2 + ns]), list(outs[2 + ns:-1])


def _copies_wait(name, started, ncopy, plan):
    send, recv, srcs, lands = started
    ns, nl = len(srcs), len(lands)

    def body(*refs):
        send_ref, recv_ref, token = refs[ns + nl], refs[ns + nl + 1], refs[-1]
        copies = plan(refs[:ns], refs[ns:ns + nl])
        assert len(copies) == ncopy
        for k, (sent, _, to, landed) in enumerate(copies):
            cp = pltpu.make_async_remote_copy(src_ref=sent, dst_ref=landed, send_sem=send_ref.at[k],
                                              recv_sem=recv_ref.at[k], device_id=to, device_id_type=MESH)
            cp.wait_send()
            cp.wait_recv()
        token[...] = jnp.zeros_like(token)

    arrays = list(srcs) + list(lands)
    outs = pl.pallas_call(
        body, name=name, in_specs=[HBM_SPEC] * (ns + nl) + [SEM_SPEC] * 2 + [ANY_SPEC],
        out_specs=[HBM_SPEC] * (ns + nl) + [VMEM_SPEC], out_shape=[pltpu.HBM(a.shape, a.dtype) for a in arrays] + [TOKEN],
        input_output_aliases={i: i for i in range(ns + nl)},
        compiler_params=pltpu.CompilerParams(has_side_effects=EFFECT))(*arrays, send, recv, _Chain.last)
    _Chain.last = outs[-1]
    return list(outs[:ns]), list(outs[ns:-1])


def _plan_gather_chips(kinds):
    def plan(srcs, lands):
        x, y, c = _place()
        sibling, chips = _peers()
        out = []
        for t, kind in enumerate(kinds):
            mine = _window(lands[t], kind, 4 * x + 2 * y + c)
            out.append((srcs[t], mine, sibling, _window(lands[t], kind, 4 * x + 2 * y + 1 - c)))
            for px, py in chips:
                out.append((srcs[t], mine, (px, py, c), _window(lands[t], kind, 4 * px + 2 * py + c)))
        return out
    return plan, 4 * len(kinds)


def _plan_gather_sibling(kinds):
    def plan(srcs, lands):
        _, _, c = _place()
        sibling, chips = _peers()
        out = []
        for t, kind in enumerate(kinds):
            for px, py in chips:
                w = _window(lands[t], kind, 4 * px + 2 * py + c)
                out.append((w, w, sibling, _window(lands[t], kind, 4 * px + 2 * py + 1 - c)))
        return out
    return plan, 3 * len(kinds)


def _plan_scatter_sibling(kinds):
    def plan(srcs, lands):
        _, _, c = _place()
        sibling, _ = _peers()
        out = []
        for t, kind in enumerate(kinds):
            for k in range(N_CHIP):
                out.append((_window(srcs[t], kind, 2 * k + 1 - c), lands[t].at[k], sibling, lands[t].at[k]))
        return out
    return plan, N_CHIP * len(kinds)


def _plan_scatter_chips(n):
    def plan(srcs, lands):
        x, y, c = _place()
        _, chips = _peers()
        out = []
        for t in range(n):
            for px, py in chips:
                out.append((srcs[t].at[2 * px + py], lands[t].at[2 * x + y], (px, py, c), lands[t].at[2 * px + py]))
        return out
    return plan, 3 * n


def _landing(shard, kind, me):
    if kind == "blocked":
        land = lax.empty((shard.shape[0], N_DEV) + shard.shape[1:], shard.dtype)
        return lax.dynamic_update_slice(land, shard[:, None], (0, me) + (0,) * (shard.ndim - 1))
    land = lax.empty((shard.shape[0], N_DEV * shard.shape[1]), shard.dtype)
    return lax.dynamic_update_slice(land, shard, (0, me * shard.shape[1]))


def _chip_sums(name, grads, kinds, recvs, c):
    n = len(grads)
    in_specs, out_specs, out_shape, args = [], [], [], []
    for gr, kind, rv in zip(grads, kinds, recvs):
        if kind == "blocked":
            rows, w = gr.shape[2], gr.shape[3]
            in_specs.append(pl.BlockSpec((None, None, rows, w), lambda k, cref: (0, 2 * k + cref[0], 0, 0)))
        else:
            rows, w = gr.shape[0], gr.shape[1] // N_DEV
            in_specs.append(pl.BlockSpec((rows, w), lambda k, cref: (0, 2 * k + cref[0])))
        blk = pl.BlockSpec((None, rows, w), lambda k, cref: (k, 0, 0))
        in_specs.append(blk)
        out_specs.append(blk)
        out_shape.append(jax.ShapeDtypeStruct((N_CHIP, rows, w), BF16))
        args += [gr, rv.reshape(N_CHIP, rows, w)]

    def body(*refs):
        for t in range(n):
            g_ref, r_ref, o_ref = refs[1 + 2 * t], refs[2 + 2 * t], refs[1 + 2 * n + t]
            o_ref[...] = (g_ref[...].astype(F32) + r_ref[...].astype(F32)).astype(BF16)

    return _pallas(body, name=name, n_prefetch=1, grid=(N_CHIP,), in_specs=in_specs, out_specs=out_specs,
                   out_shape=out_shape, params=_params(("parallel",)))(c, *args)


def _adamw_math(g, wv, mv, vv):
    m = ADAM_B1 * mv + (1.0 - ADAM_B1) * g
    v = ADAM_B2 * vv + (1.0 - ADAM_B2) * (g * g)
    m_hat = m / (1.0 - ADAM_B1 ** ADAM_STEP)
    v_hat = v / (1.0 - ADAM_B2 ** ADAM_STEP)
    delta = -ADAM_LR * (m_hat / (jnp.sqrt(v_hat) + ADAM_EPS) + ADAM_WD * wv)
    return delta, m, v


ADAM_STEPS = 2


def _adamw_group(name, items, chip_ids):
    n = len(items)
    in_specs, out_specs, out_shape, args, prevs = [], [], [], [chip_ids], []
    for own, recv, w3, m3, v3, layer, _ in items:
        nl, rows, w = w3.shape
        tr = rows // ADAM_STEPS
        assert tr % 16 == 0, (name, rows)
        in_specs += [pl.BlockSpec((None, tr, w), lambda i, ids, slot=slot: (ids[slot], i, 0)) for slot in range(4)]
        slab = pl.BlockSpec((None, tr, w), lambda i, ids, layer=layer: (layer, i, 0))
        in_specs += [slab] * 3
        out_specs += [slab] * 4
        out_shape += [jax.ShapeDtypeStruct((nl, rows, w), F32)] * 4
        args += [own, recv, recv, recv, w3, m3, v3]
    aliases = {}
    for t, item in enumerate(items):
        if item[6] is not None:
            for k in range(4):
                aliases[len(args) + k] = 4 * t + k
            in_specs += [ANY_SPEC] * 4
            args += list(item[6])
            prevs.append(t)
    n_in = 1 + 7 * n + 4 * len(prevs)

    def body(*refs):
        for t in range(n):
            own_ref, r1_ref, r2_ref, r3_ref, w_ref, m_ref, v_ref = refs[1 + 7 * t:8 + 7 * t]
            g_ref, d_ref, nm_ref, nv_ref = refs[n_in + 4 * t:n_in + 4 * t + 4]
            g = ((own_ref[...].astype(F32) + r1_ref[...].astype(F32)) + r2_ref[...].astype(F32)) + r3_ref[...].astype(F32)
            g_ref[...] = g
            d_ref[...], nm_ref[...], nv_ref[...] = _adamw_math(g, w_ref[...], m_ref[...], v_ref[...])

    outs = _pallas(body, name=name, n_prefetch=1, grid=(ADAM_STEPS,), in_specs=in_specs, out_specs=out_specs,
                   out_shape=out_shape, aliases=aliases, params=_params(("parallel",)))(*args)
    return [list(outs[4 * t:4 * t + 4]) for t in range(n)]


def _adamw_small(gathered, ws, ms, vs):
    n = len(gathered)
    full = [w is not None for w in ws]
    args = list(gathered)
    out_shape = []
    for t in range(n):
        shape = jax.ShapeDtypeStruct(gathered[t].shape[2:], F32)
        if full[t]:
            args += [ws[t], ms[t], vs[t]]
            out_shape += [shape] * 4
        else:
            out_shape += [shape]

    def body(*refs):
        i_in, i_out = n, len(args)
        for t in range(n):
            p_ref = refs[t]
            g = p_ref[0, 0]
            for k in range(1, N_DEV):
                g = g + p_ref[0, k]
            refs[i_out][...] = g
            if full[t]:
                w_ref, m_ref, v_ref = refs[i_in:i_in + 3]
                refs[i_out + 1][...], refs[i_out + 2][...], refs[i_out + 3][...] = _adamw_math(
                    g, w_ref[...], m_ref[...], v_ref[...])
                i_in += 3
                i_out += 4
            else:
                i_out += 1

    outs = _pallas(body, name="adamw_small", in_specs=[VMEM_SPEC] * len(args), out_specs=[VMEM_SPEC] * len(out_shape),
                   out_shape=out_shape, params=pltpu.CompilerParams(vmem_limit_bytes=VMEM_LIMIT))(*args)
    result, i = [], 0
    for t in range(n):
        k = 4 if full[t] else 1
        result.append(list(outs[i:i + k]))
        i += k
    return result


def _adamw_plain(name, gs, ws, ms, vs):
    n = len(gs)

    def body(*refs):
        for t in range(n):
            g_ref, w_ref, m_ref, v_ref = refs[4 * t:4 * t + 4]
            outs = refs[4 * n + 3 * t:4 * n + 3 * t + 3]
            outs[0][...], outs[1][...], outs[2][...] = _adamw_math(g_ref[...], w_ref[...], m_ref[...], v_ref[...])

    args, out_shape = [], []
    for g, w, m, v in zip(gs, ws, ms, vs):
        args += [g, w, m, v]
        out_shape += [jax.ShapeDtypeStruct(w.shape, F32)] * 3
    outs = _pallas(body, name=name, in_specs=[VMEM_SPEC] * len(args), out_specs=[VMEM_SPEC] * len(out_shape),
                   out_shape=out_shape, params=pltpu.CompilerParams(vmem_limit_bytes=VMEM_LIMIT))(*args)
    return [list(outs[3 * t:3 * t + 3]) for t in range(n)]


KIND = {"sc_w_in": "cols", "sc_w_out": "blocked", "w_dkv": "blocked", "w_kr": "blocked", "w_uk": "cols", "w_uv": "cols",
        "w_dq": "blocked", "w_uq": "blocked", "w_o": "blocked", "ffn_w_up": "blocked", "ffn_w_down": "blocked",
        "conv": "blocked"}
GATHER_GROUPS = (("mixer", ("sc_w_in", "sc_w_out", "conv")),
                 ("up0", ("ffn_w_up0",)),
                 ("down0", ("ffn_w_down0",)),
                 ("attn", ("w_dkv", "w_kr", "w_uk", "w_uv", "w_dq", "w_uq", "w_o")),
                 ("ffn1", ("ffn_w_up1", "ffn_w_down1")))
SCATTER_GROUPS = (("ffn1", (("ffn_w_up", 1), ("ffn_w_down", 1))),
                  ("attn", (("w_o", None), ("w_uq", None), ("w_dq", None), ("w_uk", None), ("w_uv", None),
                            ("w_dkv", None), ("w_kr", None))),
                  ("ffn0", (("ffn_w_up", 0), ("ffn_w_down", 0))),
                  ("mixer", (("sc_w_out", None), ("sc_w_in", None))))
SCHEDULE = {
    "begin": (("gather_start", "mixer"),),
    "mixer_ready": (("gather_start", "up0"),),
    "l0_out": (("gather_forward", "up0"), ("gather_start", "down0")),
    "f0_up": (("gather_forward", "down0"), ("gather_start", "attn")),
    "f0_down": (("gather_forward", "attn"), ("gather_start", "ffn1")),
    "attn_fwd": (("gather_forward", "ffn1"),),
    "f1_gup": (("scatter_sibling", "ffn1"),),
    "f1_dhf": (("scatter_chips", "ffn1"),),
    "kv_bwd": (("scatter_sibling", "attn"), ("scatter_done", "ffn1")),
    "f0_dact_ew": (("scatter_chips", "attn"),),
    "f0_gup": (("scatter_sibling", "ffn0"),),
    "f0_dhf": (("scatter_chips", "ffn0"),),
    "f0_bwd": (("scatter_done", "attn"),),
    "sc_bwd": (("scatter_sibling", "mixer"),),
    "d_l0_in": (("scatter_chips", "mixer"),),
}
FINISH = (("scatter_done", "ffn0"), ("scatter_done", "mixer"))
STAGES = {"gather_start": 1, "gather_forward": 2, "gather_done": 3,
          "scatter_sibling": 1, "scatter_chips": 2, "scatter_done": 3}
SMALL_W_ROWS = 24


def _pack(arrays, rows):
    flat = jnp.concatenate([a.reshape(-1).astype(F32) for a in arrays])
    return jnp.pad(flat, (0, rows * 128 - flat.shape[0])).reshape(rows, 128)


def _stored(name, a):
    return jnp.swapaxes(a, -1, -2) if name == "ffn_w_up" else a


def _base(name):
    if name.startswith("ffn_w_") and name[-1] in "01":
        return name[:-1], int(name[-1])
    return name, None


class _Exchange:
    def __init__(self, wts, mom, var, ffn_conv_b):
        self.wts, self.mom, self.var = wts, mom, var
        x, y, c = _place()
        self.me = 4 * x + 2 * y + c
        self.c_arr = jnp.reshape(c, (1,)).astype(jnp.int32)
        chip = 2 * x + y
        self.chip_ids = jnp.stack([chip, chip ^ 1, chip ^ 2, chip ^ 3]).astype(jnp.int32)
        self.ready = {"ffn_cb0": ffn_conv_b.reshape(2, N_FF_BLK, 1, FF_BLK)[0],
                      "ffn_cb1": ffn_conv_b.reshape(2, N_FF_BLK, 1, FF_BLK)[1]}
        self.gathers, self.group_of = {}, {}
        self.grads, self.scatters, self.results = {}, {}, {}
        for gname, names in GATHER_GROUPS:
            self.gathers[gname] = dict(stage=0, names=names, kinds=[KIND[_base(nm)[0]] for nm in names])
            for nm in names:
                self.group_of[nm] = gname
        for nm in ("sc_conv_w", "ffn_cw0", "ffn_cw1"):
            self.group_of[nm] = "mixer"
        self.at("begin", None)

    def _shard(self, name):
        if name == "conv":
            return _pack([self.wts["sc_conv_w"], self.wts["ffn_conv_w"]], SMALL_W_ROWS).reshape(1, SMALL_W_ROWS, 128)
        base, layer = _base(name)
        a = _stored(base, self.wts[base])
        if layer is not None:
            a = a[layer:layer + 1]
        if KIND[base] == "cols":
            return a.reshape(a.shape[-2], a.shape[-1]).astype(BF16)
        return a.reshape((-1,) + a.shape[-2:]).astype(BF16)

    def _gather_to(self, gname, stage, after):
        st = self.gathers[gname]
        if st["stage"] < 1 <= stage:
            shards = [self._shard(nm) for nm in st["names"]]
            lands = [_landing(s, kind, self.me) for s, kind in zip(shards, st["kinds"])]
            plan, ncopy = _plan_gather_chips(st["kinds"])
            st["flight"] = _copies_start(f"ag_{gname}_chips", shards, lands, ncopy, plan)
            st["stage"] = 1
        if st["stage"] < 2 <= stage:
            plan, ncopy = _plan_gather_chips(st["kinds"])
            _, lands = _copies_wait(f"ag_{gname}_chips_wait", st["flight"], ncopy, plan)
            plan, ncopy = _plan_gather_sibling(st["kinds"])
            st["flight"] = _copies_start(f"ag_{gname}_sibling", [], lands, ncopy, plan)
            st["stage"] = 2
        if st["stage"] < 3 <= stage:
            plan, ncopy = _plan_gather_sibling(st["kinds"])
            _, lands = _copies_wait(f"ag_{gname}_sibling_wait", st["flight"], ncopy, plan)
            for nm, land in zip(st["names"], lands):
                self._arrived(nm, land)
            st["stage"] = 3

    def _arrived(self, name, land):
        if name == "conv":
            conv = land.reshape(N_DEV, SMALL_W_ROWS * 128)
            self.ready["sc_conv_w"] = conv[:, :3 * 128].reshape(N_DEV, 3, 128).transpose(1, 0, 2).reshape(3, D)
            fcw = conv[:, 3 * 128:3 * 128 + 6 * 352].reshape(N_DEV, 2, 3, 352).transpose(1, 2, 0, 3)
            fcw = fcw.reshape(2, 3, N_FF_BLK, FF_BLK).transpose(0, 2, 1, 3)
            self.ready["ffn_cw0"], self.ready["ffn_cw1"] = fcw[0], fcw[1]
        elif name in ("sc_w_in", "w_uk", "w_uv") or name.startswith("ffn_w_up"):
            self.ready[name] = land
        elif name.startswith("ffn_w_down"):
            self.ready[name] = land.reshape(1, N_FF_BLK, FF_BLK, D)
        elif name == "w_kr":
            self.ready[name] = jnp.pad(land.reshape(D, QK_ROPE), ((0, 0), (0, 128 - QK_ROPE)))
        elif name == "w_uq":
            self.ready[name] = jnp.pad(land.reshape(N_HEADS, Q_LORA, QK_NOPE + QK_ROPE),
                                       ((0, 0), (0, 0), (0, QK_PAD - QK_NOPE - QK_ROPE)))
        else:
            self.ready[name] = land.reshape(D, land.shape[-1])

    def need(self, name, after):
        if name not in self.ready:
            self._gather_to(self.group_of[name], 3, after)
        return self.ready[name]

    def grad(self, name, layer, array):
        self.grads[(name, layer)] = array

    def _scatter_to(self, gname, stage, after):
        keys = dict(SCATTER_GROUPS)[gname]
        st = self.scatters.setdefault(gname, dict(stage=0))
        kinds = [KIND[nm] for nm, _ in keys]
        if st["stage"] < 1 <= stage:
            grads = [self.grads[key] for key in keys]
            lands = []
            for gr, kind in zip(grads, kinds):
                shard = (gr.shape[0],) + gr.shape[2:] if kind == "blocked" else (gr.shape[0], gr.shape[1] // N_DEV)
                lands.append(lax.empty((N_CHIP,) + shard, BF16))
            plan, ncopy = _plan_scatter_sibling(kinds)
            st["flight"] = _copies_start(f"rs_{gname}_sibling", grads, lands, ncopy, plan)
            st["stage"] = 1
        if st["stage"] < 2 <= stage:
            plan, ncopy = _plan_scatter_sibling(kinds)
            grads, recvs = _copies_wait(f"rs_{gname}_sibling_wait", st["flight"], ncopy, plan)
            sums = _chip_sums(f"rs_{gname}_sums", grads, kinds, recvs, self.c_arr)
            lands = [lax.empty(s.shape, BF16) for s in sums]
            plan, ncopy = _plan_scatter_chips(len(sums))
            st["flight"] = _copies_start(f"rs_{gname}_chips", sums, lands, ncopy, plan)
            st["stage"] = 2
        if st["stage"] < 3 <= stage:
            plan, ncopy = _plan_scatter_chips(len(keys))
            sums, recvs = _copies_wait(f"rs_{gname}_chips_wait", st["flight"], ncopy, plan)
            items = []
            for (nm, layer), own, rv in zip(keys, sums, recvs):
                nl = 1 if layer is None else 2
                rows, w = own.shape[1], own.shape[2]
                w3, m3, v3 = (_stored(nm, src[nm]).reshape(nl, rows, w) for src in (self.wts, self.mom, self.var))
                items.append((own, rv, w3, m3, v3, 0 if layer is None else layer, self.results.get(nm)))
            outs = _adamw_group(f"adamw_{gname}", items, self.chip_ids)
            for (nm, _), out in zip(keys, outs):
                self.results[nm] = out
            st["stage"] = 3

    def at(self, place, after):
        for action, gname in SCHEDULE.get(place, ()):
            self._advance(action, gname, after)

    def _advance(self, action, gname, after):
        if action.startswith("gather"):
            self._gather_to(gname, STAGES[action], after)
        else:
            self._scatter_to(gname, STAGES[action], after)

    def finish(self, after):
        for action, gname in FINISH:
            self._advance(action, gname, after)
        for gname, _ in SCATTER_GROUPS:
            self._scatter_to(gname, 3, after)
        return {nm: [_stored(nm, o.reshape(_stored(nm, self.wts[nm]).shape)) for o in outs]
                for nm, outs in self.results.items()}


REPLICATED = ("attn_norm", "ffn_norm", "final_norm", "kv_in_norm", "kv_latent_norm", "q_latent_norm", "ffn_conv_b")
WEIGHTS = ("attn_norm", "ffn_norm", "final_norm", "sc_w_in", "sc_conv_w", "sc_w_out", "kv_in_norm", "w_dkv",
           "kv_latent_norm", "w_kr", "w_uk", "w_uv", "w_dq", "q_latent_norm", "w_uq", "w_o", "ffn_w_up", "ffn_conv_w",
           "ffn_conv_b", "ffn_w_down")


def kernel(x, positions, attn_norm, ffn_norm, final_norm, sc_w_in, sc_conv_w, sc_w_out, kv_in_norm, w_dkv, kv_latent_norm, w_kr, w_uk, w_uv, w_dq, q_latent_norm, w_uq, w_o, ffn_w_up, ffn_conv_w, ffn_conv_b, ffn_w_down, loss_target, m_attn_norm, m_ffn_norm, m_final_norm, m_sc_w_in, m_sc_conv_w, m_sc_w_out, m_kv_in_norm, m_w_dkv, m_kv_latent_norm, m_w_kr, m_w_uk, m_w_uv, m_w_dq, m_q_latent_norm, m_w_uq, m_w_o, m_ffn_w_up, m_ffn_conv_w, m_ffn_conv_b, m_ffn_w_down, v_attn_norm, v_ffn_norm, v_final_norm, v_sc_w_in, v_sc_conv_w, v_sc_w_out, v_kv_in_norm, v_w_dkv, v_kv_latent_norm, v_w_kr, v_w_uk, v_w_uv, v_w_dq, v_q_latent_norm, v_w_uq, v_w_o, v_ffn_w_up, v_ffn_conv_w, v_ffn_conv_b, v_ffn_w_down):
    wts = dict(attn_norm=attn_norm, ffn_norm=ffn_norm, final_norm=final_norm, sc_w_in=sc_w_in, sc_conv_w=sc_conv_w,
               sc_w_out=sc_w_out, kv_in_norm=kv_in_norm, w_dkv=w_dkv, kv_latent_norm=kv_latent_norm, w_kr=w_kr,
               w_uk=w_uk, w_uv=w_uv, w_dq=w_dq, q_latent_norm=q_latent_norm, w_uq=w_uq, w_o=w_o, ffn_w_up=ffn_w_up,
               ffn_conv_w=ffn_conv_w, ffn_conv_b=ffn_conv_b, ffn_w_down=ffn_w_down)
    mom = dict(attn_norm=m_attn_norm, ffn_norm=m_ffn_norm, final_norm=m_final_norm, sc_w_in=m_sc_w_in,
               sc_conv_w=m_sc_conv_w, sc_w_out=m_sc_w_out, kv_in_norm=m_kv_in_norm, w_dkv=m_w_dkv,
               kv_latent_norm=m_kv_latent_norm, w_kr=m_w_kr, w_uk=m_w_uk, w_uv=m_w_uv, w_dq=m_w_dq,
               q_latent_norm=m_q_latent_norm, w_uq=m_w_uq, w_o=m_w_o, ffn_w_up=m_ffn_w_up, ffn_conv_w=m_ffn_conv_w,
               ffn_conv_b=m_ffn_conv_b, ffn_w_down=m_ffn_w_down)
    var = dict(attn_norm=v_attn_norm, ffn_norm=v_ffn_norm, final_norm=v_final_norm, sc_w_in=v_sc_w_in,
               sc_conv_w=v_sc_conv_w, sc_w_out=v_sc_w_out, kv_in_norm=v_kv_in_norm, w_dkv=v_w_dkv,
               kv_latent_norm=v_kv_latent_norm, w_kr=v_w_kr, w_uk=v_w_uk, w_uv=v_w_uv, w_dq=v_w_dq,
               q_latent_norm=v_q_latent_norm, w_uq=v_w_uq, w_o=v_w_o, ffn_w_up=v_ffn_w_up, ffn_conv_w=v_ffn_conv_w,
               ffn_conv_b=v_ffn_conv_b, ffn_w_down=v_ffn_w_down)
    xi, yi, ci = _place()
    me = 4 * xi + 2 * yi + ci
    _Chain.last = None

    ex = _Exchange(wts, mom, var, ffn_conv_b)
    rep = {
        "attn_norm": attn_norm, "ffn_norm": ffn_norm, "final_norm": final_norm,
        "kv_in_norm": kv_in_norm.reshape(1, D), "kv_latent_norm": kv_latent_norm.reshape(1, KV_LORA),
        "q_latent_norm": q_latent_norm.reshape(1, Q_LORA),
    }
    loss, grad_x, small = _local_step(x.reshape(T, D), positions.reshape(T, 1), loss_target.reshape(T, D), rep, ex)
    results = ex.finish(grad_x)

    def rows_of(a):
        return a.reshape(-1, a.shape[-1])

    small_order = list(REPLICATED) + ["sc_conv_w", "ffn_conv_w"]
    shards = [loss.reshape(1, 1, 128)] + [rows_of(small[nm])[None] for nm in small_order]
    gathered = _all_gather("ag_small_grads", [(s, "blocked") for s in shards])
    params = [[None] + [rows_of(src[nm]) for nm in REPLICATED] + [None, None] for src in (wts, mom, var)]
    summed = _adamw_small(gathered, *params)
    loss_total = summed[0][0][0, 0]
    for nm, vals in zip(REPLICATED, summed[1:1 + len(REPLICATED)]):
        results[nm] = [a.reshape(wts[nm].shape) for a in vals]
    g_scw = lax.dynamic_slice(summed[-2][0], (0, me * 128), (3, 128))
    g_fcw = lax.dynamic_slice(summed[-1][0], (0, me * 352), (6, 352))
    conv = _adamw_plain("adamw_conv", [g_scw, g_fcw], *[[rows_of(src["sc_conv_w"]), rows_of(src["ffn_conv_w"])]
                                                        for src in (wts, mom, var)])
    for nm, g_own, vals in zip(("sc_conv_w", "ffn_conv_w"), (g_scw, g_fcw), conv):
        results[nm] = [a.reshape(wts[nm].shape) for a in [g_own] + vals]

    outs = [loss_total, grad_x.reshape(1, T, D)]
    for slot in range(4):
        outs.extend(results[nm][slot] for nm in WEIGHTS)
    return tuple(outs)
```

```python
import jax
import jax.numpy as jnp
from jax import lax
from jax.experimental import pallas as pl
from jax.experimental.pallas import tpu as pltpu

F32 = jnp.float32
BF16 = jnp.bfloat16

T = 2048
D = 1024
N_HEADS = 8
QK_NOPE = 128
QK_ROPE = 64
V_HEAD = 128
Q_LORA = 384
KV_LORA = 256
D_FF = 2816
CHUNK = 64
ROPE_THETA = 10000.0
EPS = 1e-6
NEG_INF = -1e30
ADAM_LR = 0.001
ADAM_B1 = 0.9
ADAM_B2 = 0.999
ADAM_EPS = 1e-08
ADAM_WD = 0.01
ADAM_STEP = 10

N_DEV = 8
N_CHIP = 4
FF_BLK = D_FF * 2 // N_DEV
N_FF_BLK = D_FF // FF_BLK
QK_PAD = 256
HALO = 16

TM = 1024
TS = 512
TR = 256
TQ = 512
VMEM_LIMIT = 56 * 1024 * 1024

NN = (((1,), (0,)), ((), ()))
NT = (((1,), (1,)), ((), ()))
TN = (((0,), (0,)), ((), ()))
MESH = pl.DeviceIdType.MESH


def _params(sem):
    return pltpu.CompilerParams(dimension_semantics=sem, vmem_limit_bytes=VMEM_LIMIT)


ANY_SPEC = pl.BlockSpec(memory_space=pl.ANY)
VMEM_SPEC = pl.BlockSpec(memory_space=pltpu.VMEM)


class _Chain:
    last = None


def _pallas(body, *, name, in_specs, out_specs, out_shape, grid=(), scratch_shapes=(), n_prefetch=0, aliases=None,
            params=None):
    def run(*args):
        after = _Chain.last
        n_lead = len(args)
        specs, operands, fn = list(in_specs), list(args), body
        if after is not None:
            def fn(*refs):
                return body(*refs[:n_lead], *refs[n_lead + 1:])
            specs.append(ANY_SPEC)
            operands.append(after)
        kw = dict(name=name, out_shape=out_shape, input_output_aliases=aliases or {})
        if params is not None:
            kw["compiler_params"] = params
        if n_prefetch:
            kw["grid_spec"] = pltpu.PrefetchScalarGridSpec(
                num_scalar_prefetch=n_prefetch, grid=grid, in_specs=specs, out_specs=out_specs,
                scratch_shapes=scratch_shapes)
        else:
            kw.update(grid=grid, in_specs=specs, out_specs=out_specs, scratch_shapes=scratch_shapes)
        outs = pl.pallas_call(fn, **kw)(*operands)
        _Chain.last = outs[0] if isinstance(outs, (list, tuple)) else outs
        return outs
    return run


def _mm(name, a, b, *, grid, a_spec, b_spec, o_spec, o_shape, o_dtype, dims, k_axis=None, acc_shape=None,
        add=None, add_spec=None):
    nk = grid[k_axis] if k_axis is not None else 1
    has_add = add is not None

    def body(*refs):
        a_ref, b_ref = refs[0], refs[1]
        p = 2
        add_ref = None
        if has_add:
            add_ref = refs[p]
            p += 1
        o_ref = refs[p]
        p += 1
        r = lax.dot_general(a_ref[...].astype(BF16), b_ref[...].astype(BF16), dims, preferred_element_type=F32)
        if k_axis is None:
            if has_add:
                r = r + add_ref[...].astype(F32)
            o_ref[...] = r.astype(o_dtype)
        else:
            acc = refs[p]
            k = pl.program_id(k_axis)

            @pl.when(k == 0)
            def _():
                acc[...] = r

            @pl.when(k > 0)
            def _():
                acc[...] += r

            @pl.when(k == nk - 1)
            def _():
                t = acc[...]
                if has_add:
                    t = t + add_ref[...].astype(F32)
                o_ref[...] = t.astype(o_dtype)

    in_specs = [a_spec, b_spec]
    args = [a, b]
    if has_add:
        in_specs.append(add_spec if add_spec is not None else o_spec)
        args.append(add)
    sem = tuple("arbitrary" if ax == k_axis else "parallel" for ax in range(len(grid)))
    scratch = [pltpu.VMEM(acc_shape, F32)] if k_axis is not None else []
    return _pallas(body, name=name, grid=grid, in_specs=in_specs, out_specs=o_spec,
                   out_shape=jax.ShapeDtypeStruct(o_shape, o_dtype), scratch_shapes=scratch, params=_params(sem))(*args)


def _mm_sum(name, parts, *, grid, o_spec, o_shape, o_dtype, add=None):
    has_add = add is not None

    def body(*refs):
        o_ref = refs[-1]
        acc = None
        for p, (_, _, _, _, dims) in enumerate(parts):
            a_ref, b_ref = refs[2 * p], refs[2 * p + 1]
            for k in range(a_ref.shape[0]):
                r = lax.dot_general(a_ref[k], b_ref[k], dims, preferred_element_type=F32)
                acc = r if acc is None else acc + r
        if has_add:
            acc = acc + refs[2 * len(parts)][...]
        o_ref[...] = acc.astype(o_dtype)

    in_specs, args = [], []
    for a, a_spec, b, b_spec, _ in parts:
        in_specs += [a_spec, b_spec]
        args += [a, b]
    if has_add:
        in_specs.append(o_spec)
        args.append(add)
    return _pallas(body, name=name, grid=grid, in_specs=in_specs, out_specs=o_spec,
                   out_shape=jax.ShapeDtypeStruct(o_shape, o_dtype),
                   params=_params(("parallel",) * len(grid)))(*args)


def _mm_rows(name, a, b, dims, o_dtype, n_out, *, tn=None, add=None):
    k = a.shape[1]
    tn = n_out if tn is None else tn
    if dims == NN:
        b_spec = pl.BlockSpec((k, tn), lambda n, i: (0, n))
    else:
        b_spec = pl.BlockSpec((tn, k), lambda n, i: (n, 0))
    return _mm(name, a, b, grid=(n_out // tn, T // TM),
               a_spec=pl.BlockSpec((TM, k), lambda n, i: (i, 0)), b_spec=b_spec,
               o_spec=pl.BlockSpec((TM, tn), lambda n, i: (i, n)), o_shape=(T, n_out), o_dtype=o_dtype,
               dims=dims, add=add)


def _mm_wgrad(name, a, b, *, tn=512):
    k, n = a.shape[1], b.shape[1]
    tn = min(tn, n)
    return _mm(name, a, b, grid=(n // tn,),
               a_spec=pl.BlockSpec((T, k), lambda j: (0, 0)), b_spec=pl.BlockSpec((T, tn), lambda j: (0, j)),
               o_spec=pl.BlockSpec((k, tn), lambda j: (0, j)), o_shape=(k, n), o_dtype=BF16, dims=TN)


def _rms_fwd(name, x, g):
    d = x.shape[1]

    def body(x_ref, g_ref, o_ref):
        xv = x_ref[...]
        r = lax.rsqrt(jnp.mean(xv * xv, axis=-1, keepdims=True) + EPS)
        o_ref[...] = ((xv * r) * g_ref[...]).astype(BF16)

    return _pallas(
        body, name=name, grid=(T // TM,),
        in_specs=[pl.BlockSpec((TM, d), lambda i: (i, 0)), pl.BlockSpec((1, d), lambda i: (0, 0))],
        out_specs=pl.BlockSpec((TM, d), lambda i: (i, 0)),
        out_shape=jax.ShapeDtypeStruct((T, d), BF16), params=_params(("parallel",)))(x, g)


def _rms_bwd(name, x, gains, dys, dres=None):
    d = x.shape[1]
    n = len(gains)
    has_res = dres is not None

    def body(*refs):
        x_ref, g_refs, dy_refs = refs[0], refs[1:1 + n], refs[1 + n:1 + 2 * n]
        dx_ref, dxb_ref = refs[-2 - n], refs[-1 - n]
        dg_refs = refs[-n:]
        xv = x_ref[...]
        r = lax.rsqrt(jnp.mean(xv * xv, axis=-1, keepdims=True) + EPS)
        xn = xv * r
        dx = refs[1 + 2 * n][...] if has_res else None
        parts = []
        for g_ref, dy_ref in zip(g_refs, dy_refs):
            dyv = dy_ref[...].astype(F32)
            gdy = dyv * g_ref[...]
            t = r * (gdy - xn * jnp.mean(gdy * xn, axis=-1, keepdims=True))
            dx = t if dx is None else dx + t
            parts.append(jnp.sum(dyv * xn, axis=0, keepdims=True))
        dx_ref[...] = dx
        dxb_ref[...] = dx.astype(BF16)

        @pl.when(pl.program_id(0) == 0)
        def _():
            for dg_ref, part in zip(dg_refs, parts):
                dg_ref[...] = part

        @pl.when(pl.program_id(0) > 0)
        def _():
            for dg_ref, part in zip(dg_refs, parts):
                dg_ref[...] += part

    row = pl.BlockSpec((TR, d), lambda i: (i, 0))
    vec = pl.BlockSpec((1, d), lambda i: (0, 0))
    args = [x] + list(gains) + list(dys) + ([dres] if has_res else [])
    in_specs = [row] + [vec] * n + [row] * n + ([row] if has_res else [])
    outs = _pallas(
        body, name=name, grid=(T // TR,), in_specs=in_specs, out_specs=[row, row] + [vec] * n,
        out_shape=[jax.ShapeDtypeStruct((T, d), F32), jax.ShapeDtypeStruct((T, d), BF16)]
        + [jax.ShapeDtypeStruct((1, d), F32)] * n,
        params=_params(("arbitrary",)))(*args)
    return outs[0], outs[1], list(outs[2:])


def _final(h, g, tgt):
    def body(h_ref, g_ref, t_ref, loss_ref, dh_ref, dhb_ref, dg_ref):
        hv = h_ref[...]
        r = lax.rsqrt(jnp.mean(hv * hv, axis=-1, keepdims=True) + EPS)
        xn = hv * r
        gv = g_ref[...]
        err = xn * gv - t_ref[...]
        part_loss = 0.5 * jnp.sum(jnp.mean(err * err, axis=-1, keepdims=True), axis=0, keepdims=True)
        dy = err * (1.0 / D)
        gdy = dy * gv
        dh = r * (gdy - xn * jnp.mean(gdy * xn, axis=-1, keepdims=True))
        dh_ref[...] = dh
        dhb_ref[...] = dh.astype(BF16)
        part = jnp.sum(dy * xn, axis=0, keepdims=True)
        first = pl.program_id(0) == 0

        @pl.when(first)
        def _():
            dg_ref[...] = part
            loss_ref[...] = jnp.broadcast_to(part_loss, (1, 128))

        @pl.when(jnp.logical_not(first))
        def _():
            dg_ref[...] += part
            loss_ref[...] += jnp.broadcast_to(part_loss, (1, 128))

    row = pl.BlockSpec((TR, D), lambda i: (i, 0))
    vec = pl.BlockSpec((1, D), lambda i: (0, 0))
    return _pallas(
        body, name="final_loss", grid=(T // TR,), in_specs=[row, vec, row],
        out_specs=[pl.BlockSpec((1, 128), lambda i: (0, 0)), row, row, vec],
        out_shape=[jax.ShapeDtypeStruct((1, 128), F32), jax.ShapeDtypeStruct((T, D), F32),
                   jax.ShapeDtypeStruct((T, D), BF16), jax.ShapeDtypeStruct((1, D), F32)],
        params=_params(("arbitrary",)))(h, g, tgt)


def _prev_idx(i, rows=TR):
    return jnp.maximum(i * (rows // HALO) - 1, 0)


def _next_idx(i, rows=TR):
    return jnp.minimum((i + 1) * (rows // HALO), T // HALO - 1)


def _causal_taps(ext):
    return pltpu.roll(ext, 2, 0)[HALO:], pltpu.roll(ext, 1, 0)[HALO:], ext[HALO:]


def _anticausal_taps(ext, n):
    rows = ext.shape[0]
    return pltpu.roll(ext, rows - 1, 0)[:n], pltpu.roll(ext, rows - 2, 0)[:n]


def _sc_fwd(z, w):
    def body(b_ref, c_ref, ch_ref, u_ref, uh_ref, w_ref, y_ref):
        i = pl.program_id(0)
        cu = c_ref[...].astype(F32) * u_ref[...].astype(F32)
        cuh = ch_ref[...].astype(F32) * uh_ref[...].astype(F32)
        cuh = jnp.where(i > 0, cuh, 0.0)
        x2, x1, x0 = _causal_taps(jnp.concatenate([cuh, cu], axis=0))
        wv = w_ref[...]
        cv = (x2 * wv[0:1] + x1 * wv[1:2]) + x0 * wv[2:3]
        y_ref[...] = (b_ref[...].astype(F32) * cv).astype(BF16)

    def main(part):
        return pl.BlockSpec((TR, D), lambda i: (i, part))

    def halo(part):
        return pl.BlockSpec((HALO, D), lambda i: (_prev_idx(i), part))

    return _pallas(
        body, name="sc_fwd", grid=(T // TR,),
        in_specs=[main(0), main(1), halo(1), main(2), halo(2), pl.BlockSpec((3, D), lambda i: (0, 0))],
        out_specs=pl.BlockSpec((TR, D), lambda i: (i, 0)),
        out_shape=jax.ShapeDtypeStruct((T, D), BF16), params=_params(("parallel",)))(z, z, z, z, z, w)


def _sc_bwd(z, dy, w):
    last = T // TR - 1

    def body(b_ref, bn_ref, c_ref, ch_ref, u_ref, uh_ref, dy_ref, dyn_ref, w_ref, dz_ref, dw_ref):
        i = pl.program_id(0)
        cv_ = c_ref[...].astype(F32)
        uv = u_ref[...].astype(F32)
        cu = cv_ * uv
        cuh = jnp.where(i > 0, ch_ref[...].astype(F32) * uh_ref[...].astype(F32), 0.0)
        x2, x1, x0 = _causal_taps(jnp.concatenate([cuh, cu], axis=0))
        wv = w_ref[...]
        conv = (x2 * wv[0:1] + x1 * wv[1:2]) + x0 * wv[2:3]
        dyv = dy_ref[...]
        dz_ref[:, 0:D] = (dyv * conv).astype(BF16)
        dconv = dyv * b_ref[...].astype(F32)
        dconv_n = jnp.where(i < last, dyn_ref[...] * bn_ref[...].astype(F32), 0.0)
        n1, n2 = _anticausal_taps(jnp.concatenate([dconv, dconv_n], axis=0), TR)
        dcu = (dconv * wv[2:3] + n1 * wv[1:2]) + n2 * wv[0:1]
        dz_ref[:, D:2 * D] = (dcu * uv).astype(BF16)
        dz_ref[:, 2 * D:3 * D] = (dcu * cv_).astype(BF16)
        part = jnp.concatenate([jnp.sum(dconv * x2, axis=0, keepdims=True),
                                jnp.sum(dconv * x1, axis=0, keepdims=True),
                                jnp.sum(dconv * x0, axis=0, keepdims=True)], axis=0)

        @pl.when(i == 0)
        def _():
            dw_ref[...] = part

        @pl.when(i > 0)
        def _():
            dw_ref[...] += part

    def main(part):
        return pl.BlockSpec((TR, D), lambda i: (i, part))

    def prev(part):
        return pl.BlockSpec((HALO, D), lambda i: (_prev_idx(i), part))

    def nxt(part):
        return pl.BlockSpec((HALO, D), lambda i: (_next_idx(i), part))

    wspec = pl.BlockSpec((3, D), lambda i: (0, 0))
    return _pallas(
        body, name="sc_bwd", grid=(T // TR,),
        in_specs=[main(0), nxt(0), main(1), prev(1), main(2), prev(2), main(0), nxt(0), wspec],
        out_specs=[pl.BlockSpec((TR, 3 * D), lambda i: (i, 0)), wspec],
        out_shape=[jax.ShapeDtypeStruct((T, 3 * D), BF16), jax.ShapeDtypeStruct((3, D), F32)],
        params=_params(("arbitrary",)))(z, z, z, z, z, z, dy, dy, w)


def _sigmoid(x):
    return 1.0 / (1.0 + jnp.exp(-x))


def _ffn_up_act(name, hf, w_up, w, b):
    def body(h_ref, hh_ref, wg_ref, wv_ref, w_ref, b_ref, g_ref, v_ref, a_ref):
        i = pl.program_id(1)
        wg = wg_ref[...]
        g = lax.dot_general(h_ref[...], wg, NT, preferred_element_type=F32).astype(BF16)
        gh = lax.dot_general(hh_ref[...], wg, NT, preferred_element_type=F32).astype(BF16)
        v = lax.dot_general(h_ref[...], wv_ref[...], NT, preferred_element_type=F32).astype(BF16)
        g_ref[...] = g
        v_ref[...] = v
        gh = jnp.where(i > 0, gh.astype(F32), 0.0)
        x2, x1, x0 = _causal_taps(jnp.concatenate([gh, g.astype(F32)], axis=0))
        wv = w_ref[...]
        gc = ((x2 * wv[0:1] + x1 * wv[1:2]) + x0 * wv[2:3]) + b_ref[...]
        a_ref[...] = ((gc * _sigmoid(gc)) * v.astype(F32)).astype(BF16)

    blk = pl.BlockSpec((None, TS, FF_BLK), lambda j, i: (j, i, 0))
    out = jax.ShapeDtypeStruct((N_FF_BLK, T, FF_BLK), BF16)
    return _pallas(
        body, name=name, grid=(N_FF_BLK, T // TS),
        in_specs=[pl.BlockSpec((TS, D), lambda j, i: (i, 0)),
                  pl.BlockSpec((HALO, D), lambda j, i: (_prev_idx(i, TS), 0)),
                  pl.BlockSpec((None, None, FF_BLK, D), lambda j, i: (0, j, 0, 0)),
                  pl.BlockSpec((None, None, FF_BLK, D), lambda j, i: (0, j + N_FF_BLK, 0, 0)),
                  pl.BlockSpec((None, 3, FF_BLK), lambda j, i: (j, 0, 0)),
                  pl.BlockSpec((None, 1, FF_BLK), lambda j, i: (j, 0, 0))],
        out_specs=[blk, blk, blk], out_shape=[out, out, out],
        params=_params(("parallel", "parallel")))(hf, hf, w_up, w_up, w, b)


def _ffn_dact(name, dh, w_down4, g, v, w, b):
    last = T // TS - 1

    def body(dh_ref, dhn_ref, wd_ref, g_ref, gp_ref, gn_ref, v_ref, vn_ref, w_ref, b_ref, dg_ref, dv_ref, dw_ref, db_ref):
        i = pl.program_id(1)
        wd = wd_ref[...]
        da = lax.dot_general(dh_ref[...], wd, NT, preferred_element_type=F32)
        dan = lax.dot_general(dhn_ref[...], wd, NT, preferred_element_type=F32)
        da = jnp.concatenate([da, jnp.where(i < last, dan, 0.0)], axis=0)
        gp = jnp.where(i > 0, gp_ref[...].astype(F32), 0.0)
        ext = jnp.concatenate([gp, g_ref[...].astype(F32), gn_ref[...].astype(F32)], axis=0)
        x2, x1, x0 = _causal_taps(ext)
        wv = w_ref[...]
        gc = ((x2 * wv[0:1] + x1 * wv[1:2]) + x0 * wv[2:3]) + b_ref[...]
        sg = _sigmoid(gc)
        vv = jnp.concatenate([v_ref[...].astype(F32), vn_ref[...].astype(F32)], axis=0)
        dv_ref[...] = (da[:TS] * (gc[:TS] * sg[:TS])).astype(BF16)
        dgc = (da * vv) * (sg * (1.0 + gc * (1.0 - sg)))
        n1, n2 = _anticausal_taps(dgc, TS)
        d0 = dgc[:TS]
        dg_ref[...] = ((d0 * wv[2:3] + n1 * wv[1:2]) + n2 * wv[0:1]).astype(BF16)
        part_w = jnp.concatenate([jnp.sum(d0 * x2[:TS], axis=0, keepdims=True),
                                  jnp.sum(d0 * x1[:TS], axis=0, keepdims=True),
                                  jnp.sum(d0 * x0[:TS], axis=0, keepdims=True)], axis=0)
        part_b = jnp.sum(d0, axis=0, keepdims=True)

        @pl.when(i == 0)
        def _():
            dw_ref[...] = part_w
            db_ref[...] = part_b

        @pl.when(i > 0)
        def _():
            dw_ref[...] += part_w
            db_ref[...] += part_b

    blk = pl.BlockSpec((None, TS, FF_BLK), lambda j, i: (j, i, 0))
    prev = pl.BlockSpec((None, HALO, FF_BLK), lambda j, i: (j, _prev_idx(i, TS), 0))
    nxt = pl.BlockSpec((None, HALO, FF_BLK), lambda j, i: (j, _next_idx(i, TS), 0))
    wspec = pl.BlockSpec((None, 3, FF_BLK), lambda j, i: (j, 0, 0))
    bspec = pl.BlockSpec((None, 1, FF_BLK), lambda j, i: (j, 0, 0))
    return _pallas(
        body, name=name, grid=(N_FF_BLK, T // TS),
        in_specs=[pl.BlockSpec((TS, D), lambda j, i: (i, 0)),
                  pl.BlockSpec((HALO, D), lambda j, i: (_next_idx(i, TS), 0)),
                  pl.BlockSpec((None, None, FF_BLK, D), lambda j, i: (0, j, 0, 0)),
                  blk, prev, nxt, blk, nxt, wspec, bspec],
        out_specs=[blk, blk, wspec, bspec],
        out_shape=[jax.ShapeDtypeStruct((N_FF_BLK, T, FF_BLK), BF16), jax.ShapeDtypeStruct((N_FF_BLK, T, FF_BLK), BF16),
                   jax.ShapeDtypeStruct((N_FF_BLK, 3, FF_BLK), F32), jax.ShapeDtypeStruct((N_FF_BLK, 1, FF_BLK), F32)],
        params=_params(("parallel", "arbitrary")))(dh, dh, w_down4, g, g, g, v, v, w, b)


def _rope_tables(pos, inv_freq):
    half = QK_ROPE // 2

    def body(p_ref, f_ref, c_ref, sa_ref, sb_ref):
        ang = p_ref[...].astype(F32) * f_ref[...]
        lane = lax.broadcasted_iota(jnp.int32, (T, 128), 1)
        c = jnp.cos(ang)
        s = jnp.sin(ang)
        c_ref[...] = jnp.where(lane < 2 * half, c, 0.0)
        sa_ref[...] = jnp.where(lane < half, -s, 0.0)
        sb_ref[...] = jnp.where(jnp.logical_and(lane >= half, lane < 2 * half), s, 0.0)

    return _pallas(
        body, name="rope_tables", in_specs=[VMEM_SPEC] * 2, out_specs=[VMEM_SPEC] * 3,
        out_shape=[jax.ShapeDtypeStruct((T, 128), F32)] * 3,
        params=pltpu.CompilerParams(vmem_limit_bytes=VMEM_LIMIT))(pos, inv_freq)


def _rotate(r, c, sa, sb, sign):
    return r * c + sign * (pltpu.roll(r, 96, 1) * sa + pltpu.roll(r, 32, 1) * sb)


def _q_up(cq, w_uq, tables):
    cos, sa, sb = tables

    def body(a_ref, b_ref, c_ref, sa_ref, sb_ref, o_ref):
        r = lax.dot_general(a_ref[...], b_ref[...], NN, preferred_element_type=F32)
        o_ref[:, :QK_NOPE] = r[:, :QK_NOPE].astype(BF16)
        o_ref[:, QK_NOPE:] = _rotate(r[:, QK_NOPE:], c_ref[...], sa_ref[...], sb_ref[...], 1.0).astype(BF16)

    tab = pl.BlockSpec((TM, 128), lambda h, i: (i, 0))
    return _pallas(
        body, name="q_up", grid=(N_HEADS, T // TM),
        in_specs=[pl.BlockSpec((TM, Q_LORA), lambda h, i: (i, 0)),
                  pl.BlockSpec((None, Q_LORA, QK_PAD), lambda h, i: (h, 0, 0)), tab, tab, tab],
        out_specs=pl.BlockSpec((None, TM, QK_PAD), lambda h, i: (h, i, 0)),
        out_shape=jax.ShapeDtypeStruct((N_HEADS, T, QK_PAD), BF16),
        params=_params(("parallel", "parallel")))(cq, w_uq, cos, sa, sb)


def _rope(name, x, tables, sign, out_dtype, reduce_groups=False):
    g, _, w = x.shape
    cos, sa, sb = tables

    def body(x_ref, c_ref, sa_ref, sb_ref, o_ref):
        xv = x_ref[...].astype(F32)
        if reduce_groups:
            acc = xv[0]
            for k in range(1, g):
                acc = acc + xv[k]
            xv = acc
        out = _rotate(xv[:, w - 128:], c_ref[...], sa_ref[...], sb_ref[...], sign)
        if w > 128:
            o_ref[:, :w - 128] = xv[:, :w - 128].astype(out_dtype)
        o_ref[:, w - 128:] = out.astype(out_dtype)

    tab = pl.BlockSpec((TM, 128), lambda h, i: (i, 0))
    if reduce_groups:
        x_spec = pl.BlockSpec((g, TM, w), lambda h, i: (0, i, 0))
        groups = 1
    else:
        x_spec = pl.BlockSpec((None, TM, w), lambda h, i: (h, i, 0))
        groups = g
    return _pallas(
        body, name=name, grid=(groups, T // TM), in_specs=[x_spec, tab, tab, tab],
        out_specs=pl.BlockSpec((None, TM, w), lambda h, i: (h, i, 0)),
        out_shape=jax.ShapeDtypeStruct((groups, T, w), out_dtype),
        params=_params(("parallel", "parallel")))(x, cos, sa, sb)


SCALE = (QK_NOPE + QK_ROPE) ** -0.5
LOG2E = 1.4426950408889634
SCALE2 = SCALE * LOG2E


def _diag_mask(transposed):
    shift = CHUNK.bit_length() - 1
    a = lax.broadcasted_iota(jnp.int32, (TQ, TQ), 0) >> shift
    b = lax.broadcasted_iota(jnp.int32, (TQ, TQ), 1) >> shift
    return (a <= b) if transposed else (b <= a)


def _keys(kn_ref, kr_ref, off):
    return jnp.concatenate([kn_ref[pl.ds(off, TQ), :], kr_ref[pl.ds(off, TQ), :]], axis=1)


def _attn_fwd(q, kn, kr, v):
    def body(q_ref, kn_ref, kr_ref, v_ref, o_ref, lse_ref):
        i = pl.program_id(1)
        qv = q_ref[...]

        def step(j, carry, masked):
            m, l, acc = carry
            off = pl.multiple_of(j * TQ, TQ)
            s = lax.dot_general(qv, _keys(kn_ref, kr_ref, off), NT, preferred_element_type=F32) * SCALE2
            if masked:
                s = jnp.where(_diag_mask(False), s, NEG_INF)
            m_new = jnp.maximum(m, jnp.max(s, axis=-1, keepdims=True))
            p = jnp.exp2(s - m_new)
            alpha = jnp.exp2(m - m_new)
            l = alpha * l + jnp.sum(p, axis=-1, keepdims=True)
            acc = alpha * acc + lax.dot_general(p.astype(BF16), v_ref[pl.ds(off, TQ), :], NN, preferred_element_type=F32)
            return m_new, l, acc

        init = (jnp.full((TQ, 1), NEG_INF, F32), jnp.zeros((TQ, 1), F32), jnp.zeros((TQ, V_HEAD), F32))
        carry = lax.fori_loop(0, i, lambda j, cr: step(j, cr, False), init)
        m, l, acc = step(i, carry, True)
        o_ref[...] = (acc / l).astype(BF16)
        lse_ref[...] = m + jnp.log(l) * LOG2E

    return _pallas(
        body, name="attn_fwd", grid=(N_HEADS, T // TQ),
        in_specs=[pl.BlockSpec((None, TQ, QK_PAD), lambda h, i: (h, i, 0)),
                  pl.BlockSpec((T, QK_NOPE), lambda h, i: (0, h)),
                  pl.BlockSpec((T, 128), lambda h, i: (0, 0)),
                  pl.BlockSpec((T, V_HEAD), lambda h, i: (0, h))],
        out_specs=[pl.BlockSpec((TQ, V_HEAD), lambda h, i: (i, h)), pl.BlockSpec((None, TQ, 1), lambda h, i: (h, i, 0))],
        out_shape=[jax.ShapeDtypeStruct((T, N_HEADS * V_HEAD), BF16), jax.ShapeDtypeStruct((N_HEADS, T, 1), F32)],
        params=_params(("parallel", "parallel")))(q, kn, kr, v)


def _attn_bwd_dq(q, kn, kr, v, o, do, lse, tables):
    cos, sa, sb = tables

    def body(q_ref, kn_ref, kr_ref, v_ref, o_ref, do_ref, lse_ref, c_ref, sa_ref, sb_ref, dq_ref, dl_ref):
        i = pl.program_id(1)
        qv = q_ref[...]
        dov = do_ref[...]
        lse = lse_ref[...]
        delta = jnp.sum(dov.astype(F32) * o_ref[...].astype(F32), axis=-1, keepdims=True)
        dl_ref[...] = delta

        def step(j, dq, masked):
            off = pl.multiple_of(j * TQ, TQ)
            kk = _keys(kn_ref, kr_ref, off)
            s = lax.dot_general(qv, kk, NT, preferred_element_type=F32) * SCALE2
            if masked:
                s = jnp.where(_diag_mask(False), s, NEG_INF)
            p = jnp.exp2(s - lse)
            dp = lax.dot_general(dov, v_ref[pl.ds(off, TQ), :], NT, preferred_element_type=F32)
            ds = (p * (dp - delta)) * SCALE
            return dq + lax.dot_general(ds.astype(BF16), kk, NN, preferred_element_type=F32)

        dq = lax.fori_loop(0, i, lambda j, acc: step(j, acc, False), jnp.zeros((TQ, QK_PAD), F32))
        dq = step(i, dq, True)
        dq_ref[:, :QK_NOPE] = dq[:, :QK_NOPE].astype(BF16)
        dq_ref[:, QK_NOPE:] = _rotate(dq[:, QK_NOPE:], c_ref[...], sa_ref[...], sb_ref[...], -1.0).astype(BF16)

    col = pl.BlockSpec((None, TQ, 1), lambda h, i: (h, i, 0))
    head = pl.BlockSpec((TQ, V_HEAD), lambda h, i: (i, h))
    tab = pl.BlockSpec((TQ, 128), lambda h, i: (i, 0))
    return _pallas(
        body, name="attn_bwd_dq", grid=(N_HEADS, T // TQ),
        in_specs=[pl.BlockSpec((None, TQ, QK_PAD), lambda h, i: (h, i, 0)),
                  pl.BlockSpec((T, QK_NOPE), lambda h, i: (0, h)),
                  pl.BlockSpec((T, 128), lambda h, i: (0, 0)),
                  pl.BlockSpec((T, V_HEAD), lambda h, i: (0, h)),
                  head, head, col, tab, tab, tab],
        out_specs=[pl.BlockSpec((None, TQ, QK_PAD), lambda h, i: (h, i, 0)), col],
        out_shape=[jax.ShapeDtypeStruct((N_HEADS, T, QK_PAD), BF16), jax.ShapeDtypeStruct((N_HEADS, T, 1), F32)],
        params=_params(("parallel", "parallel")))(q, kn, kr, v, o, do, lse, cos, sa, sb)


def _attn_bwd_dkv(q, kn, kr, v, do, lse_row, delta_row):
    nq = T // TQ

    def body(q_ref, kn_ref, kr_ref, v_ref, do_ref, lse_ref, dl_ref, dkn_ref, dkr_ref, dv_ref):
        j = pl.program_id(1)
        kk = jnp.concatenate([kn_ref[...], kr_ref[...]], axis=1)
        vv = v_ref[...]

        def step(i, carry, masked):
            dk, dv = carry
            off = pl.multiple_of(i * TQ, TQ)
            qi = q_ref[pl.ds(off, TQ), :]
            doi = do_ref[pl.ds(off, TQ), :]
            st = lax.dot_general(kk, qi, NT, preferred_element_type=F32) * SCALE2
            if masked:
                st = jnp.where(_diag_mask(True), st, NEG_INF)
            pt = jnp.exp2(st - lse_ref[:, pl.ds(off, TQ)])
            dv = dv + lax.dot_general(pt.astype(BF16), doi, NN, preferred_element_type=F32)
            dpt = lax.dot_general(vv, doi, NT, preferred_element_type=F32)
            dst = (pt * (dpt - dl_ref[:, pl.ds(off, TQ)])) * SCALE
            dk = dk + lax.dot_general(dst.astype(BF16), qi, NN, preferred_element_type=F32)
            return dk, dv

        carry = step(j, (jnp.zeros((TQ, QK_PAD), F32), jnp.zeros((TQ, V_HEAD), F32)), True)
        dk, dv = lax.fori_loop(j + 1, nq, lambda i, cr: step(i, cr, False), carry)
        dkn_ref[...] = dk[:, :QK_NOPE].astype(BF16)
        dkr_ref[...] = dk[:, QK_NOPE:]
        dv_ref[...] = dv.astype(BF16)

    row = pl.BlockSpec((None, 1, T), lambda h, j: (h, 0, 0))
    head = pl.BlockSpec((TQ, 128), lambda h, j: (j, h))
    return _pallas(
        body, name="attn_bwd_dkv", grid=(N_HEADS, nq),
        in_specs=[pl.BlockSpec((None, T, QK_PAD), lambda h, j: (h, 0, 0)),
                  head, pl.BlockSpec((TQ, 128), lambda h, j: (j, 0)), head,
                  pl.BlockSpec((T, V_HEAD), lambda h, j: (0, h)), row, row],
        out_specs=[head, pl.BlockSpec((None, TQ, 128), lambda h, j: (h, j, 0)), head],
        out_shape=[jax.ShapeDtypeStruct((T, N_HEADS * QK_NOPE), BF16), jax.ShapeDtypeStruct((N_HEADS, T, 128), F32),
                   jax.ShapeDtypeStruct((T, N_HEADS * V_HEAD), BF16)],
        params=_params(("parallel", "parallel")))(q, kn, kr, v, do, lse_row, delta_row)


def _ffn_gup(name, dg, dv, hf):
    def body(dg_ref, dv_ref, hf_ref, o_ref):
        j = pl.program_id(0)

        @pl.when(j < N_FF_BLK)
        def _():
            o_ref[...] = lax.dot_general(dg_ref[...], hf_ref[...], TN, preferred_element_type=F32).astype(BF16)

        @pl.when(j >= N_FF_BLK)
        def _():
            o_ref[...] = lax.dot_general(dv_ref[...], hf_ref[...], TN, preferred_element_type=F32).astype(BF16)

    return _pallas(
        body, name=name, grid=(N_DEV,),
        in_specs=[pl.BlockSpec((None, T, FF_BLK), lambda j: (jnp.minimum(j, N_FF_BLK - 1), 0, 0)),
                  pl.BlockSpec((None, T, FF_BLK), lambda j: (jnp.maximum(j - N_FF_BLK, 0), 0, 0)),
                  pl.BlockSpec((T, D), lambda j: (0, 0))],
        out_specs=pl.BlockSpec((None, FF_BLK, D), lambda j: (j, 0, 0)),
        out_shape=jax.ShapeDtypeStruct((N_DEV, FF_BLK, D), BF16), params=_params(("parallel",)))(dg, dv, hf)


def _ffn_layer_fwd(tag, h, gain, ex):
    hf = _rms_fwd(f"{tag}_norm", h, gain)
    g, v, act = _ffn_up_act(f"{tag}_up", hf, ex.need(f"ffn_w_up{tag[1]}", hf), ex.need(f"ffn_cw{tag[1]}", hf),
                            ex.need(f"ffn_cb{tag[1]}", hf))
    ex.at(f"{tag}_up", act)
    rows = pl.BlockSpec((TS, D), lambda i: (i, 0))
    out = _mm_sum(f"{tag}_down",
                  [(act, pl.BlockSpec((N_FF_BLK, TS, FF_BLK), lambda i: (0, i, 0)), ex.need(f"ffn_w_down{tag[1]}", act),
                    pl.BlockSpec((None, N_FF_BLK, FF_BLK, D), lambda i: (0, 0, 0, 0)), NN)],
                  grid=(T // TS,), o_spec=rows, o_shape=(T, D), o_dtype=F32, add=h)
    ex.at(f"{tag}_down", out)
    return out, (hf, g, v, act)


def _ffn_layer_bwd(tag, h, gain, ex, saved, dh, dh_bf):
    hf, g, v, act = saved
    layer = tag[1]
    w_up, w_down4 = ex.need(f"ffn_w_up{layer}", dh_bf), ex.need(f"ffn_w_down{layer}", dh_bf)
    dg, dv, dcw, dcb = _ffn_dact(f"{tag}_dact", dh_bf, w_down4, g, v, ex.need(f"ffn_cw{layer}", dh_bf),
                                 ex.need(f"ffn_cb{layer}", dh_bf))
    ex.at(f"{tag}_dact", dg)
    tn = 512
    g_down = _mm(f"{tag}_gdown", act, dh_bf, grid=(N_FF_BLK, D // tn),
                 a_spec=pl.BlockSpec((None, T, FF_BLK), lambda j, n: (j, 0, 0)),
                 b_spec=pl.BlockSpec((T, tn), lambda j, n: (0, n)),
                 o_spec=pl.BlockSpec((FF_BLK, tn), lambda j, n: (j, n)),
                 o_shape=(D_FF, D), o_dtype=BF16, dims=TN)
    g_up = _ffn_gup(f"{tag}_gup", dg, dv, hf)
    ex.grad("ffn_w_up", int(layer), g_up.reshape(1, N_DEV, FF_BLK, D))
    ex.grad("ffn_w_down", int(layer), g_down.reshape(1, N_DEV, D_FF // N_DEV, D))
    ex.at(f"{tag}_gup", g_up)
    part = pl.BlockSpec((N_FF_BLK, TS, FF_BLK), lambda i: (0, i, 0))
    dhf = _mm_sum(f"{tag}_dhf",
                  [(dg, part, w_up, pl.BlockSpec((None, N_FF_BLK, FF_BLK, D), lambda i: (0, 0, 0, 0)), NN),
                   (dv, part, w_up, pl.BlockSpec((None, N_FF_BLK, FF_BLK, D), lambda i: (0, 1, 0, 0)), NN)],
                  grid=(T // TS,), o_spec=pl.BlockSpec((TS, D), lambda i: (i, 0)), o_shape=(T, D), o_dtype=F32)
    ex.at(f"{tag}_dhf", dhf)
    dh_in, dh_in_bf, dgain = _rms_bwd(f"{tag}_dnorm", h, [gain], [dhf], dres=dh)
    return dh_in, dh_in_bf, dgain[0], dcw, dcb


def _local_step(x, pos, tgt, rep, ex):
    attn_norm, ffn_norm, final_norm = rep["attn_norm"], rep["ffn_norm"], rep["final_norm"]
    half = QK_ROPE // 2
    inv = 1.0 / (ROPE_THETA ** (jnp.arange(half, dtype=F32) / half))
    inv_freq = jnp.concatenate([inv, inv, jnp.zeros((128 - 2 * half,), F32)]).reshape(1, 128)
    tables = _rope_tables(pos, inv_freq)

    hn0 = _rms_fwd("l0_norm", x, attn_norm[0:1])
    w_in = ex.need("sc_w_in", hn0)
    ex.at("mixer_ready", hn0)
    z = _mm_rows("l0_in", hn0, w_in, NN, BF16, 3 * D, tn=512)
    ex.at("l0_in", z)
    y = _sc_fwd(z, ex.need("sc_conv_w", z))
    h1 = _mm_rows("l0_out", y, ex.need("sc_w_out", y), NN, F32, D, tn=512, add=x)
    ex.at("l0_out", h1)
    h2, ffn0 = _ffn_layer_fwd("f0", h1, ffn_norm[0:1], ex)

    hk = _rms_fwd("kv_norm", h2, rep["kv_in_norm"])
    ckv_raw = _mm_rows("kv_down", hk, ex.need("w_dkv", hk), NN, F32, KV_LORA)
    kr_raw = _mm_rows("kv_rope", hk, ex.need("w_kr", hk), NN, F32, 128)
    ckv = _rms_fwd("kv_lnorm", ckv_raw, rep["kv_latent_norm"])
    kn = _mm_rows("kv_uk", ckv, ex.need("w_uk", ckv), NN, BF16, N_HEADS * QK_NOPE)
    vv = _mm_rows("kv_uv", ckv, ex.need("w_uv", ckv), NN, BF16, N_HEADS * V_HEAD)
    kr = _rope("k_rope", kr_raw.reshape(1, T, 128), tables, 1.0, BF16).reshape(T, 128)

    hn1 = _rms_fwd("l1_norm", h2, attn_norm[1:2])
    cq_raw = _mm_rows("q_down", hn1, ex.need("w_dq", hn1), NN, F32, Q_LORA)
    cq = _rms_fwd("q_lnorm", cq_raw, rep["q_latent_norm"])
    w_uq = ex.need("w_uq", cq)
    q = _q_up(cq, w_uq, tables)
    o, lse = _attn_fwd(q, kn, kr, vv)
    ex.at("attn_fwd", o)
    w_o = ex.need("w_o", o)
    h3 = _mm_rows("attn_out", o, w_o, NN, F32, D, tn=512, add=h2)
    h4, ffn1 = _ffn_layer_fwd("f1", h3, ffn_norm[1:2], ex)

    loss, dh4, dh4_bf, d_final = _final(h4, final_norm.reshape(1, D), tgt)

    dh3, dh3_bf, d_fn1, dcw1, dcb1 = _ffn_layer_bwd("f1", h3, ffn_norm[1:2], ex, ffn1, dh4, dh4_bf)
    ex.at("f1_bwd", dh3)

    do = _mm_rows("d_attn_out", dh3_bf, w_o, NT, BF16, N_HEADS * V_HEAD)
    ex.grad("w_o", None, _mm_wgrad("g_w_o", o, dh3_bf).reshape(1, N_DEV, D // N_DEV, D))
    dq_pre, delta = _attn_bwd_dq(q, kn, kr, vv, o, do, lse, tables)
    ex.at("attn_dq", dq_pre)
    dkn, dkr, dvv = _attn_bwd_dkv(q, kn, kr, vv, do, lse.reshape(N_HEADS, 1, T), delta.reshape(N_HEADS, 1, T))
    dcq = _mm("d_q_up", dq_pre, w_uq, grid=(T // TM, N_HEADS),
              a_spec=pl.BlockSpec((None, TM, QK_PAD), lambda i, h: (h, i, 0)),
              b_spec=pl.BlockSpec((None, Q_LORA, QK_PAD), lambda i, h: (h, 0, 0)),
              o_spec=pl.BlockSpec((TM, Q_LORA), lambda i, h: (i, 0)), o_shape=(T, Q_LORA), o_dtype=F32,
              dims=NT, k_axis=1, acc_shape=(TM, Q_LORA))
    g_uq = _mm("g_w_uq", cq, dq_pre, grid=(N_HEADS,),
               a_spec=pl.BlockSpec((T, Q_LORA), lambda h: (0, 0)),
               b_spec=pl.BlockSpec((None, T, QK_PAD), lambda h: (h, 0, 0)),
               o_spec=pl.BlockSpec((None, Q_LORA, QK_PAD), lambda h: (h, 0, 0)),
               o_shape=(N_HEADS, Q_LORA, QK_PAD), o_dtype=BF16, dims=TN)
    ex.grad("w_uq", None, g_uq[:, :, :QK_NOPE + QK_ROPE].reshape(1, N_DEV, Q_LORA, QK_NOPE + QK_ROPE))
    _, dcq_raw_bf, (d_qln,) = _rms_bwd("d_q_lnorm", cq_raw, [rep["q_latent_norm"]], [dcq])
    dhn1 = _mm_rows("d_q_down", dcq_raw_bf, ex.need("w_dq", dcq_raw_bf), NT, F32, D)
    ex.grad("w_dq", None, _mm_wgrad("g_w_dq", hn1, dcq_raw_bf).reshape(1, N_DEV, D // N_DEV, Q_LORA))

    dckv = _mm_rows("d_kv_uk", dkn, ex.need("w_uk", dkn), NT, F32, KV_LORA)
    dckv = _mm_rows("d_kv_uv", dvv, ex.need("w_uv", dvv), NT, F32, KV_LORA, add=dckv)
    ex.grad("w_uk", None, _mm_wgrad("g_w_uk", ckv, dkn))
    ex.grad("w_uv", None, _mm_wgrad("g_w_uv", ckv, dvv))
    _, dckv_raw_bf, (d_kvln,) = _rms_bwd("d_kv_lnorm", ckv_raw, [rep["kv_latent_norm"]], [dckv])
    dkr_raw_bf = _rope("dk_rope", dkr, tables, -1.0, BF16, reduce_groups=True).reshape(T, 128)
    dhk = _mm_rows("d_kv_down", dckv_raw_bf, ex.need("w_dkv", dckv_raw_bf), NT, F32, D)
    dhk = _mm_rows("d_kv_rope", dkr_raw_bf, ex.need("w_kr", dkr_raw_bf), NT, F32, D, add=dhk)
    ex.grad("w_dkv", None, _mm_wgrad("g_w_dkv", hk, dckv_raw_bf).reshape(1, N_DEV, D // N_DEV, KV_LORA))
    ex.grad("w_kr", None, _mm_wgrad("g_w_kr", hk, dkr_raw_bf)[:, :QK_ROPE].reshape(1, N_DEV, D // N_DEV, QK_ROPE))
    dh2, dh2_bf, (d_an1, d_kvin) = _rms_bwd("d_h2_norms", h2, [attn_norm[1:2], rep["kv_in_norm"]], [dhn1, dhk], dres=dh3)
    ex.at("kv_bwd", dh2)

    dh1, dh1_bf, d_fn0, dcw0, dcb0 = _ffn_layer_bwd("f0", h1, ffn_norm[0:1], ex, ffn0, dh2, dh2_bf)
    ex.at("f0_bwd", dh1)

    dy = _mm_rows("d_l0_out", dh1_bf, ex.need("sc_w_out", dh1_bf), NT, F32, D)
    ex.grad("sc_w_out", None, _mm_wgrad("g_sc_w_out", y, dh1_bf).reshape(1, N_DEV, D // N_DEV, D))
    dz, d_scw = _sc_bwd(z, dy, ex.need("sc_conv_w", dy))
    g_in = _mm_wgrad("g_sc_w_in", hn0, dz)
    ex.grad("sc_w_in", None, g_in)
    ex.at("sc_bwd", g_in)
    dhn0 = _mm_rows("d_l0_in", dz, ex.need("sc_w_in", dz), NT, F32, D)
    ex.at("d_l0_in", dhn0)
    grad_x, _, (d_an0,) = _rms_bwd("d_l0_norm", x, [attn_norm[0:1]], [dhn0], dres=dh1)

    small = {
        "attn_norm": jnp.concatenate([d_an0, d_an1], axis=0),
        "ffn_norm": jnp.concatenate([d_fn0, d_fn1], axis=0),
        "final_norm": d_final.reshape(D),
        "kv_in_norm": d_kvin.reshape(D),
        "kv_latent_norm": d_kvln.reshape(KV_LORA),
        "q_latent_norm": d_qln,
        "ffn_conv_b": jnp.stack([dcb0, dcb1]).transpose(0, 2, 1, 3).reshape(2, D_FF),
        "sc_conv_w": d_scw,
        "ffn_conv_w": jnp.stack([dcw0, dcw1]).transpose(0, 2, 1, 3).reshape(2, 3, D_FF),
    }
    return loss, grad_x, small


def _place():
    return lax.axis_index("x"), lax.axis_index("y"), lax.axis_index("c")


def _peers():
    x, y, c = _place()
    return (x, y, 1 - c), [(1 - x, y), (x, 1 - y), (1 - x, 1 - y)]


def _window(ref, kind, dev):
    if kind == "blocked":
        return ref.at[:, dev]
    width = ref.shape[-1] // N_DEV
    return ref.at[:, pl.ds(pl.multiple_of(dev * width, 128), width)]


def _all_gather(name, items):
    n = len(items)
    out_shapes = []
    for shard, kind in items:
        if kind == "blocked":
            shape = (shard.shape[0], N_DEV) + shard.shape[1:]
        else:
            shape = (shard.shape[0], N_DEV * shard.shape[1])
        out_shapes.append(jax.ShapeDtypeStruct(shape, shard.dtype))

    def body(*refs):
        srcs, outs = refs[:n], refs[n:2 * n]
        send_sems, recv_sems, local_sems = refs[2 * n:]
        x, y, c = _place()
        me = 4 * x + 2 * y + c
        sibling, chips = _peers()

        def num(px, py, pc):
            return 4 * px + 2 * py + pc

        def copy(t, k, dev, to, from_src):
            kind = items[t][1]
            dst = _window(outs[t], kind, dev)
            return pltpu.make_async_remote_copy(
                src_ref=srcs[t] if from_src else dst, dst_ref=dst,
                send_sem=send_sems.at[t, k], recv_sem=recv_sems.at[t, k], device_id=to, device_id_type=MESH)

        mine = [pltpu.make_async_copy(srcs[t], _window(outs[t], items[t][1], me), local_sems.at[t]) for t in range(n)]
        for cp in mine:
            cp.start()
        first = []
        for t in range(n):
            first.append(copy(t, 0, me, sibling, True))
            for j, chip in enumerate(chips):
                first.append(copy(t, 1 + j, me, (*chip, c), True))
        for cp in first:
            cp.start()
        passed = []
        for j, chip in enumerate(chips):
            for t in range(n):
                copy(t, 1 + j, num(*chip, c), (x, y, c), False).wait_recv()
                fwd = copy(t, 4 + j, num(*chip, c), sibling, False)
                fwd.start()
                passed.append(fwd)
        for t in range(n):
            copy(t, 0, num(x, y, 1 - c), (x, y, c), False).wait_recv()
            for j, chip in enumerate(chips):
                copy(t, 4 + j, num(*chip, 1 - c), (x, y, c), False).wait_recv()
        for cp in first + passed:
            cp.wait_send()
        for cp in mine:
            cp.wait()

    return _pallas(
        body, name=name, in_specs=[ANY_SPEC] * n, out_specs=[ANY_SPEC] * n, out_shape=out_shapes,
        scratch_shapes=[pltpu.SemaphoreType.DMA((n, 7)), pltpu.SemaphoreType.DMA((n, 7)), pltpu.SemaphoreType.DMA((n,))],
    )(*[s for s, _ in items])


HBM_SPEC = pl.BlockSpec(memory_space=pltpu.HBM)
SEM_SPEC = pl.BlockSpec(memory_space=pltpu.SEMAPHORE)
EFFECT = pltpu.SideEffectType.DATAFLOW_SIDE_EFFECTING
TOKEN = jax.ShapeDtypeStruct((8, 128), F32)


def _hbm(a):
    return pltpu.with_memory_space_constraint(a, pltpu.HBM)


def _copies_start(name, srcs, lands, ncopy, plan):
    ns, nl = len(srcs), len(lands)

    def body(*refs):
        send, recv, token = refs[ns + nl], refs[ns + nl + 1], refs[-1]
        copies = plan(refs[:ns], refs[ns:ns + nl])
        assert len(copies) == ncopy
        for k, (sent, dst, to, _) in enumerate(copies):
            pltpu.make_async_remote_copy(src_ref=sent, dst_ref=dst, send_sem=send.at[k], recv_sem=recv.at[k],
                                         device_id=to, device_id_type=MESH).start()
        token[...] = jnp.zeros_like(token)

    arrays = list(srcs) + list(lands)
    outs = pl.pallas_call(
        body, name=name, in_specs=[HBM_SPEC] * (ns + nl),
        out_specs=[SEM_SPEC] * 2 + [HBM_SPEC] * (ns + nl) + [VMEM_SPEC],
        out_shape=[pltpu.SemaphoreType.DMA((ncopy,))] * 2 + [pltpu.HBM(a.shape, a.dtype) for a in arrays] + [TOKEN],
        input_output_aliases={i: 2 + i for i in range(ns + nl)},
        compiler_params=pltpu.CompilerParams(has_side_effects=EFFECT))(*[_hbm(a) for a in arrays])
    _Chain.last = outs[-1]
    return outs[0], outs[1], list(outs[2:2 + ns]), list(outs[2 + ns:-1])


def _copies_wait(name, started, ncopy, plan):
    send, recv, srcs, lands = started
    ns, nl = len(srcs), len(lands)

    def body(*refs):
        send_ref, recv_ref, token = refs[ns + nl], refs[ns + nl + 1], refs[-1]
        copies = plan(refs[:ns], refs[ns:ns + nl])
        assert len(copies) == ncopy
        for k, (sent, _, to, landed) in enumerate(copies):
            cp = pltpu.make_async_remote_copy(src_ref=sent, dst_ref=landed, send_sem=send_ref.at[k],
                                              recv_sem=recv_ref.at[k], device_id=to, device_id_type=MESH)
            cp.wait_send()
            cp.wait_recv()
        token[...] = jnp.zeros_like(token)

    arrays = list(srcs) + list(lands)
    outs = pl.pallas_call(
        body, name=name, in_specs=[HBM_SPEC] * (ns + nl) + [SEM_SPEC] * 2 + [ANY_SPEC],
        out_specs=[HBM_SPEC] * (ns + nl) + [VMEM_SPEC], out_shape=[pltpu.HBM(a.shape, a.dtype) for a in arrays] + [TOKEN],
        input_output_aliases={i: i for i in range(ns + nl)},
        compiler_params=pltpu.CompilerParams(has_side_effects=EFFECT))(*arrays, send, recv, _Chain.last)
    _Chain.last = outs[-1]
    return list(outs[:ns]), list(outs[ns:-1])


def _plan_gather_chips(kinds):
    def plan(srcs, lands):
        x, y, c = _place()
        sibling, chips = _peers()
        out = []
        for t, kind in enumerate(kinds):
            mine = _window(lands[t], kind, 4 * x + 2 * y + c)
            out.append((srcs[t], mine, sibling, _window(lands[t], kind, 4 * x + 2 * y + 1 - c)))
            for px, py in chips:
                out.append((srcs[t], mine, (px, py, c), _window(lands[t], kind, 4 * px + 2 * py + c)))
        return out
    return plan, 4 * len(kinds)


def _plan_gather_sibling(kinds):
    def plan(srcs, lands):
        _, _, c = _place()
        sibling, chips = _peers()
        out = []
        for t, kind in enumerate(kinds):
            for px, py in chips:
                w = _window(lands[t], kind, 4 * px + 2 * py + c)
                out.append((w, w, sibling, _window(lands[t], kind, 4 * px + 2 * py + 1 - c)))
        return out
    return plan, 3 * len(kinds)


def _plan_scatter_sibling(kinds):
    def plan(srcs, lands):
        _, _, c = _place()
        sibling, _ = _peers()
        out = []
        for t, kind in enumerate(kinds):
            for k in range(N_CHIP):
                out.append((_window(srcs[t], kind, 2 * k + 1 - c), lands[t].at[k], sibling, lands[t].at[k]))
        return out
    return plan, N_CHIP * len(kinds)


def _plan_scatter_chips(n):
    def plan(srcs, lands):
        x, y, c = _place()
        _, chips = _peers()
        out = []
        for t in range(n):
            for px, py in chips:
                out.append((srcs[t].at[2 * px + py], lands[t].at[2 * x + y], (px, py, c), lands[t].at[2 * px + py]))
        return out
    return plan, 3 * n


def _landing(shard, kind, me):
    if kind == "blocked":
        land = lax.empty((shard.shape[0], N_DEV) + shard.shape[1:], shard.dtype)
        return lax.dynamic_update_slice(land, shard[:, None], (0, me) + (0,) * (shard.ndim - 1))
    land = lax.empty((shard.shape[0], N_DEV * shard.shape[1]), shard.dtype)
    return lax.dynamic_update_slice(land, shard, (0, me * shard.shape[1]))


def _chip_sums(name, grads, kinds, recvs, c):
    n = len(grads)
    in_specs, out_specs, out_shape, args = [], [], [], []
    for gr, kind, rv in zip(grads, kinds, recvs):
        if kind == "blocked":
            rows, w = gr.shape[2], gr.shape[3]
            in_specs.append(pl.BlockSpec((None, None, rows, w), lambda k, cref: (0, 2 * k + cref[0], 0, 0)))
        else:
            rows, w = gr.shape[0], gr.shape[1] // N_DEV
            in_specs.append(pl.BlockSpec((rows, w), lambda k, cref: (0, 2 * k + cref[0])))
        blk = pl.BlockSpec((None, rows, w), lambda k, cref: (k, 0, 0))
        in_specs.append(blk)
        out_specs.append(blk)
        out_shape.append(jax.ShapeDtypeStruct((N_CHIP, rows, w), BF16))
        args += [gr, rv.reshape(N_CHIP, rows, w)]

    def body(*refs):
        for t in range(n):
            g_ref, r_ref, o_ref = refs[1 + 2 * t], refs[2 + 2 * t], refs[1 + 2 * n + t]
            o_ref[...] = (g_ref[...].astype(F32) + r_ref[...].astype(F32)).astype(BF16)

    return _pallas(body, name=name, n_prefetch=1, grid=(N_CHIP,), in_specs=in_specs, out_specs=out_specs,
                   out_shape=out_shape, params=_params(("parallel",)))(c, *args)


def _adamw_math(g, wv, mv, vv):
    m = ADAM_B1 * mv + (1.0 - ADAM_B1) * g
    v = ADAM_B2 * vv + (1.0 - ADAM_B2) * (g * g)
    m_hat = m / (1.0 - ADAM_B1 ** ADAM_STEP)
    v_hat = v / (1.0 - ADAM_B2 ** ADAM_STEP)
    delta = -ADAM_LR * (m_hat / (jnp.sqrt(v_hat) + ADAM_EPS) + ADAM_WD * wv)
    return delta, m, v


ADAM_STEPS = 2


def _adamw_group(name, items, chip_ids):
    n = len(items)
    in_specs, out_specs, out_shape, args, prevs = [], [], [], [chip_ids], []
    for own, recv, w3, m3, v3, layer, _ in items:
        nl, rows, w = w3.shape
        tr = rows // ADAM_STEPS
        assert tr % 16 == 0, (name, rows)
        in_specs += [pl.BlockSpec((None, tr, w), lambda i, ids, slot=slot: (ids[slot], i, 0)) for slot in range(4)]
        slab = pl.BlockSpec((None, tr, w), lambda i, ids, layer=layer: (layer, i, 0))
        in_specs += [slab] * 3
        out_specs += [slab] * 4
        out_shape += [jax.ShapeDtypeStruct((nl, rows, w), F32)] * 4
        args += [own, recv, recv, recv, w3, m3, v3]
    aliases = {}
    for t, item in enumerate(items):
        if item[6] is not None:
            for k in range(4):
                aliases[len(args) + k] = 4 * t + k
            in_specs += [ANY_SPEC] * 4
            args += list(item[6])
            prevs.append(t)
    n_in = 1 + 7 * n + 4 * len(prevs)

    def body(*refs):
        for t in range(n):
            own_ref, r1_ref, r2_ref, r3_ref, w_ref, m_ref, v_ref = refs[1 + 7 * t:8 + 7 * t]
            g_ref, d_ref, nm_ref, nv_ref = refs[n_in + 4 * t:n_in + 4 * t + 4]
            g = ((own_ref[...].astype(F32) + r1_ref[...].astype(F32)) + r2_ref[...].astype(F32)) + r3_ref[...].astype(F32)
            g_ref[...] = g
            d_ref[...], nm_ref[...], nv_ref[...] = _adamw_math(g, w_ref[...], m_ref[...], v_ref[...])

    outs = _pallas(body, name=name, n_prefetch=1, grid=(ADAM_STEPS,), in_specs=in_specs, out_specs=out_specs,
                   out_shape=out_shape, aliases=aliases, params=_params(("parallel",)))(*args)
    return [list(outs[4 * t:4 * t + 4]) for t in range(n)]


def _adamw_small(gathered, ws, ms, vs):
    n = len(gathered)
    full = [w is not None for w in ws]
    args = list(gathered)
    out_shape = []
    for t in range(n):
        shape = jax.ShapeDtypeStruct(gathered[t].shape[2:], F32)
        if full[t]:
            args += [ws[t], ms[t], vs[t]]
            out_shape += [shape] * 4
        else:
            out_shape += [shape]

    def body(*refs):
        i_in, i_out = n, len(args)
        for t in range(n):
            p_ref = refs[t]
            g = p_ref[0, 0]
            for k in range(1, N_DEV):
                g = g + p_ref[0, k]
            refs[i_out][...] = g
            if full[t]:
                w_ref, m_ref, v_ref = refs[i_in:i_in + 3]
                refs[i_out + 1][...], refs[i_out + 2][...], refs[i_out + 3][...] = _adamw_math(
                    g, w_ref[...], m_ref[...], v_ref[...])
                i_in += 3
                i_out += 4
            else:
                i_out += 1

    outs = _pallas(body, name="adamw_small", in_specs=[VMEM_SPEC] * len(args), out_specs=[VMEM_SPEC] * len(out_shape),
                   out_shape=out_shape, params=pltpu.CompilerParams(vmem_limit_bytes=VMEM_LIMIT))(*args)
    result, i = [], 0
    for t in range(n):
        k = 4 if full[t] else 1
        result.append(list(outs[i:i + k]))
        i += k
    return result


def _adamw_plain(name, gs, ws, ms, vs):
    n = len(gs)

    def body(*refs):
        for t in range(n):
            g_ref, w_ref, m_ref, v_ref = refs[4 * t:4 * t + 4]
            outs = refs[4 * n + 3 * t:4 * n + 3 * t + 3]
            outs[0][...], outs[1][...], outs[2][...] = _adamw_math(g_ref[...], w_ref[...], m_ref[...], v_ref[...])

    args, out_shape = [], []
    for g, w, m, v in zip(gs, ws, ms, vs):
        args += [g, w, m, v]
        out_shape += [jax.ShapeDtypeStruct(w.shape, F32)] * 3
    outs = _pallas(body, name=name, in_specs=[VMEM_SPEC] * len(args), out_specs=[VMEM_SPEC] * len(out_shape),
                   out_shape=out_shape, params=pltpu.CompilerParams(vmem_limit_bytes=VMEM_LIMIT))(*args)
    return [list(outs[3 * t:3 * t + 3]) for t in range(n)]


KIND = {"sc_w_in": "cols", "sc_w_out": "blocked", "w_dkv": "blocked", "w_kr": "blocked", "w_uk": "cols", "w_uv": "cols",
        "w_dq": "blocked", "w_uq": "blocked", "w_o": "blocked", "ffn_w_up": "blocked", "ffn_w_down": "blocked",
        "conv": "blocked"}
GATHER_GROUPS = (("mixer", ("sc_w_in", "sc_w_out", "conv")),
                 ("up0", ("ffn_w_up0",)),
                 ("down0", ("ffn_w_down0",)),
                 ("attn", ("w_dkv", "w_kr", "w_uk", "w_uv", "w_dq", "w_uq", "w_o")),
                 ("ffn1", ("ffn_w_up1", "ffn_w_down1")))
SCATTER_GROUPS = (("ffn1", (("ffn_w_up", 1), ("ffn_w_down", 1))),
                  ("attn", (("w_o", None), ("w_uq", None), ("w_dq", None), ("w_uk", None), ("w_uv", None),
                            ("w_dkv", None), ("w_kr", None))),
                  ("ffn0", (("ffn_w_up", 0), ("ffn_w_down", 0))),
                  ("mixer", (("sc_w_out", None), ("sc_w_in", None))))
SCHEDULE = {
    "begin": (("gather_start", "mixer"),),
    "mixer_ready": (("gather_start", "up0"),),
    "l0_out": (("gather_forward", "up0"), ("gather_start", "down0")),
    "f0_up": (("gather_forward", "down0"), ("gather_start", "attn")),
    "f0_down": (("gather_forward", "attn"), ("gather_start", "ffn1")),
    "attn_fwd": (("gather_forward", "ffn1"),),
    "f1_gup": (("scatter_sibling", "ffn1"),),
    "f1_dhf": (("scatter_chips", "ffn1"),),
    "kv_bwd": (("scatter_sibling", "attn"), ("scatter_done", "ffn1")),
    "f0_dact": (("scatter_chips", "attn"),),
    "f0_gup": (("scatter_sibling", "ffn0"),),
    "f0_dhf": (("scatter_chips", "ffn0"),),
    "f0_bwd": (("scatter_done", "attn"),),
    "sc_bwd": (("scatter_sibling", "mixer"),),
    "d_l0_in": (("scatter_chips", "mixer"),),
}
FINISH = (("scatter_done", "ffn0"), ("scatter_done", "mixer"))
STAGES = {"gather_start": 1, "gather_forward": 2, "gather_done": 3,
          "scatter_sibling": 1, "scatter_chips": 2, "scatter_done": 3}
SMALL_W_ROWS = 24


def _pack(arrays, rows):
    flat = jnp.concatenate([a.reshape(-1).astype(F32) for a in arrays])
    return jnp.pad(flat, (0, rows * 128 - flat.shape[0])).reshape(rows, 128)


def _stored(name, a):
    return jnp.swapaxes(a, -1, -2) if name == "ffn_w_up" else a


def _base(name):
    if name.startswith("ffn_w_") and name[-1] in "01":
        return name[:-1], int(name[-1])
    return name, None


class _Exchange:
    def __init__(self, wts, mom, var, ffn_conv_b):
        self.wts, self.mom, self.var = wts, mom, var
        x, y, c = _place()
        self.me = 4 * x + 2 * y + c
        self.c_arr = jnp.reshape(c, (1,)).astype(jnp.int32)
        chip = 2 * x + y
        self.chip_ids = jnp.stack([chip, chip ^ 1, chip ^ 2, chip ^ 3]).astype(jnp.int32)
        self.ready = {"ffn_cb0": ffn_conv_b.reshape(2, N_FF_BLK, 1, FF_BLK)[0],
                      "ffn_cb1": ffn_conv_b.reshape(2, N_FF_BLK, 1, FF_BLK)[1]}
        self.gathers, self.group_of = {}, {}
        self.grads, self.scatters, self.results = {}, {}, {}
        for gname, names in GATHER_GROUPS:
            self.gathers[gname] = dict(stage=0, names=names, kinds=[KIND[_base(nm)[0]] for nm in names])
            for nm in names:
                self.group_of[nm] = gname
        for nm in ("sc_conv_w", "ffn_cw0", "ffn_cw1"):
            self.group_of[nm] = "mixer"
        self.at("begin", None)

    def _shard(self, name):
        if name == "conv":
            return _pack([self.wts["sc_conv_w"], self.wts["ffn_conv_w"]], SMALL_W_ROWS).reshape(1, SMALL_W_ROWS, 128)
        base, layer = _base(name)
        a = _stored(base, self.wts[base])
        if layer is not None:
            a = a[layer:layer + 1]
        if KIND[base] == "cols":
            return a.reshape(a.shape[-2], a.shape[-1]).astype(BF16)
        return a.reshape((-1,) + a.shape[-2:]).astype(BF16)

    def _gather_to(self, gname, stage, after):
        st = self.gathers[gname]
        if st["stage"] < 1 <= stage:
            shards = [self._shard(nm) for nm in st["names"]]
            lands = [_landing(s, kind, self.me) for s, kind in zip(shards, st["kinds"])]
            plan, ncopy = _plan_gather_chips(st["kinds"])
            st["flight"] = _copies_start(f"ag_{gname}_chips", shards, lands, ncopy, plan)
            st["stage"] = 1
        if st["stage"] < 2 <= stage:
            plan, ncopy = _plan_gather_chips(st["kinds"])
            _, lands = _copies_wait(f"ag_{gname}_chips_wait", st["flight"], ncopy, plan)
            plan, ncopy = _plan_gather_sibling(st["kinds"])
            st["flight"] = _copies_start(f"ag_{gname}_sibling", [], lands, ncopy, plan)
            st["stage"] = 2
        if st["stage"] < 3 <= stage:
            plan, ncopy = _plan_gather_sibling(st["kinds"])
            _, lands = _copies_wait(f"ag_{gname}_sibling_wait", st["flight"], ncopy, plan)
            for nm, land in zip(st["names"], lands):
                self._arrived(nm, land)
            st["stage"] = 3

    def _arrived(self, name, land):
        if name == "conv":
            conv = land.reshape(N_DEV, SMALL_W_ROWS * 128)
            self.ready["sc_conv_w"] = conv[:, :3 * 128].reshape(N_DEV, 3, 128).transpose(1, 0, 2).reshape(3, D)
            fcw = conv[:, 3 * 128:3 * 128 + 6 * 352].reshape(N_DEV, 2, 3, 352).transpose(1, 2, 0, 3)
            fcw = fcw.reshape(2, 3, N_FF_BLK, FF_BLK).transpose(0, 2, 1, 3)
            self.ready["ffn_cw0"], self.ready["ffn_cw1"] = fcw[0], fcw[1]
        elif name in ("sc_w_in", "w_uk", "w_uv") or name.startswith("ffn_w_up"):
            self.ready[name] = land
        elif name.startswith("ffn_w_down"):
            self.ready[name] = land.reshape(1, N_FF_BLK, FF_BLK, D)
        elif name == "w_kr":
            self.ready[name] = jnp.pad(land.reshape(D, QK_ROPE), ((0, 0), (0, 128 - QK_ROPE)))
        elif name == "w_uq":
            self.ready[name] = jnp.pad(land.reshape(N_HEADS, Q_LORA, QK_NOPE + QK_ROPE),
                                       ((0, 0), (0, 0), (0, QK_PAD - QK_NOPE - QK_ROPE)))
        else:
            self.ready[name] = land.reshape(D, land.shape[-1])

    def need(self, name, after):
        if name not in self.ready:
            self._gather_to(self.group_of[name], 3, after)
        return self.ready[name]

    def grad(self, name, layer, array):
        self.grads[(name, layer)] = array

    def _scatter_to(self, gname, stage, after):
        keys = dict(SCATTER_GROUPS)[gname]
        st = self.scatters.setdefault(gname, dict(stage=0))
        kinds = [KIND[nm] for nm, _ in keys]
        if st["stage"] < 1 <= stage:
            grads = [self.grads[key] for key in keys]
            lands = []
            for gr, kind in zip(grads, kinds):
                shard = (gr.shape[0],) + gr.shape[2:] if kind == "blocked" else (gr.shape[0], gr.shape[1] // N_DEV)
                lands.append(lax.empty((N_CHIP,) + shard, BF16))
            plan, ncopy = _plan_scatter_sibling(kinds)
            st["flight"] = _copies_start(f"rs_{gname}_sibling", grads, lands, ncopy, plan)
            st["stage"] = 1
        if st["stage"] < 2 <= stage:
            plan, ncopy = _plan_scatter_sibling(kinds)
            grads, recvs = _copies_wait(f"rs_{gname}_sibling_wait", st["flight"], ncopy, plan)
            sums = _chip_sums(f"rs_{gname}_sums", grads, kinds, recvs, self.c_arr)
            lands = [lax.empty(s.shape, BF16) for s in sums]
            plan, ncopy = _plan_scatter_chips(len(sums))
            st["flight"] = _copies_start(f"rs_{gname}_chips", sums, lands, ncopy, plan)
            st["stage"] = 2
        if st["stage"] < 3 <= stage:
            plan, ncopy = _plan_scatter_chips(len(keys))
            sums, recvs = _copies_wait(f"rs_{gname}_chips_wait", st["flight"], ncopy, plan)
            items = []
            for (nm, layer), own, rv in zip(keys, sums, recvs):
                nl = 1 if layer is None else 2
                rows, w = own.shape[1], own.shape[2]
                w3, m3, v3 = (_stored(nm, src[nm]).reshape(nl, rows, w) for src in (self.wts, self.mom, self.var))
                items.append((own, rv, w3, m3, v3, 0 if layer is None else layer, self.results.get(nm)))
            outs = _adamw_group(f"adamw_{gname}", items, self.chip_ids)
            for (nm, _), out in zip(keys, outs):
                self.results[nm] = out
            st["stage"] = 3

    def at(self, place, after):
        for action, gname in SCHEDULE.get(place, ()):
            self._advance(action, gname, after)

    def _advance(self, action, gname, after):
        if action.startswith("gather"):
            self._gather_to(gname, STAGES[action], after)
        else:
            self._scatter_to(gname, STAGES[action], after)

    def finish(self, after):
        for action, gname in FINISH:
            self._advance(action, gname, after)
        for gname, _ in SCATTER_GROUPS:
            self._scatter_to(gname, 3, after)
        return {nm: [_stored(nm, o.reshape(_stored(nm, self.wts[nm]).shape)) for o in outs]
                for nm, outs in self.results.items()}


REPLICATED = ("attn_norm", "ffn_norm", "final_norm", "kv_in_norm", "kv_latent_norm", "q_latent_norm", "ffn_conv_b")
WEIGHTS = ("attn_norm", "ffn_norm", "final_norm", "sc_w_in", "sc_conv_w", "sc_w_out", "kv_in_norm", "w_dkv",
           "kv_latent_norm", "w_kr", "w_uk", "w_uv", "w_dq", "q_latent_norm", "w_uq", "w_o", "ffn_w_up", "ffn_conv_w",
           "ffn_conv_b", "ffn_w_down")


def kernel(x, positions, attn_norm, ffn_norm, final_norm, sc_w_in, sc_conv_w, sc_w_out, kv_in_norm, w_dkv, kv_latent_norm, w_kr, w_uk, w_uv, w_dq, q_latent_norm, w_uq, w_o, ffn_w_up, ffn_conv_w, ffn_conv_b, ffn_w_down, loss_target, m_attn_norm, m_ffn_norm, m_final_norm, m_sc_w_in, m_sc_conv_w, m_sc_w_out, m_kv_in_norm, m_w_dkv, m_kv_latent_norm, m_w_kr, m_w_uk, m_w_uv, m_w_dq, m_q_latent_norm, m_w_uq, m_w_o, m_ffn_w_up, m_ffn_conv_w, m_ffn_conv_b, m_ffn_w_down, v_attn_norm, v_ffn_norm, v_final_norm, v_sc_w_in, v_sc_conv_w, v_sc_w_out, v_kv_in_norm, v_w_dkv, v_kv_latent_norm, v_w_kr, v_w_uk, v_w_uv, v_w_dq, v_q_latent_norm, v_w_uq, v_w_o, v_ffn_w_up, v_ffn_conv_w, v_ffn_conv_b, v_ffn_w_down):
    wts = dict(attn_norm=attn_norm, ffn_norm=ffn_norm, final_norm=final_norm, sc_w_in=sc_w_in, sc_conv_w=sc_conv_w,
               sc_w_out=sc_w_out, kv_in_norm=kv_in_norm, w_dkv=w_dkv, kv_latent_norm=kv_latent_norm, w_kr=w_kr,
               w_uk=w_uk, w_uv=w_uv, w_dq=w_dq, q_latent_norm=q_latent_norm, w_uq=w_uq, w_o=w_o, ffn_w_up=ffn_w_up,
               ffn_conv_w=ffn_conv_w, ffn_conv_b=ffn_conv_b, ffn_w_down=ffn_w_down)
    mom = dict(attn_norm=m_attn_norm, ffn_norm=m_ffn_norm, final_norm=m_final_norm, sc_w_in=m_sc_w_in,
               sc_conv_w=m_sc_conv_w, sc_w_out=m_sc_w_out, kv_in_norm=m_kv_in_norm, w_dkv=m_w_dkv,
               kv_latent_norm=m_kv_latent_norm, w_kr=m_w_kr, w_uk=m_w_uk, w_uv=m_w_uv, w_dq=m_w_dq,
               q_latent_norm=m_q_latent_norm, w_uq=m_w_uq, w_o=m_w_o, ffn_w_up=m_ffn_w_up, ffn_conv_w=m_ffn_conv_w,
               ffn_conv_b=m_ffn_conv_b, ffn_w_down=m_ffn_w_down)
    var = dict(attn_norm=v_attn_norm, ffn_norm=v_ffn_norm, final_norm=v_final_norm, sc_w_in=v_sc_w_in,
               sc_conv_w=v_sc_conv_w, sc_w_out=v_sc_w_out, kv_in_norm=v_kv_in_norm, w_dkv=v_w_dkv,
               kv_latent_norm=v_kv_latent_norm, w_kr=v_w_kr, w_uk=v_w_uk, w_uv=v_w_uv, w_dq=v_w_dq,
               q_latent_norm=v_q_latent_norm, w_uq=v_w_uq, w_o=v_w_o, ffn_w_up=v_ffn_w_up, ffn_conv_w=v_ffn_conv_w,
               ffn_conv_b=v_ffn_conv_b, ffn_w_down=v_ffn_w_down)
    xi, yi, ci = _place()
    me = 4 * xi + 2 * yi + ci
    _Chain.last = None

    ex = _Exchange(wts, mom, var, ffn_conv_b)
    rep = {
        "attn_norm": attn_norm, "ffn_norm": ffn_norm, "final_norm": final_norm,
        "kv_in_norm": kv_in_norm.reshape(1, D), "kv_latent_norm": kv_latent_norm.reshape(1, KV_LORA),
        "q_latent_norm": q_latent_norm.reshape(1, Q_LORA),
    }
    loss, grad_x, small = _local_step(x.reshape(T, D), positions.reshape(T, 1), loss_target.reshape(T, D), rep, ex)
    results = ex.finish(grad_x)

    def rows_of(a):
        return a.reshape(-1, a.shape[-1])

    small_order = list(REPLICATED) + ["sc_conv_w", "ffn_conv_w"]
    shards = [loss.reshape(1, 1, 128)] + [rows_of(small[nm])[None] for nm in small_order]
    gathered = _all_gather("ag_small_grads", [(s, "blocked") for s in shards])
    params = [[None] + [rows_of(src[nm]) for nm in REPLICATED] + [None, None] for src in (wts, mom, var)]
    summed = _adamw_small(gathered, *params)
    loss_total = summed[0][0][0, 0]
    for nm, vals in zip(REPLICATED, summed[1:1 + len(REPLICATED)]):
        results[nm] = [a.reshape(wts[nm].shape) for a in vals]
    g_scw = lax.dynamic_slice(summed[-2][0], (0, me * 128), (3, 128))
    g_fcw = lax.dynamic_slice(summed[-1][0], (0, me * 352), (6, 352))
    conv = _adamw_plain("adamw_conv", [g_scw, g_fcw], *[[rows_of(src["sc_conv_w"]), rows_of(src["ffn_conv_w"])]
                                                        for src in (wts, mom, var)])
    for nm, g_own, vals in zip(("sc_conv_w", "ffn_conv_w"), (g_scw, g_fcw), conv):
        results[nm] = [a.reshape(wts[nm].shape) for a in [g_own] + vals]

    outs = [loss_total, grad_x.reshape(1, T, D)]
    for slot in range(4):
        outs.extend(results[nm][slot] for nm in WEIGHTS)
    return tuple(outs)
```

```python
import jax
import jax.numpy as jnp
from jax import lax
from jax.experimental import pallas as pl
from jax.experimental.pallas import tpu as pltpu

F32 = jnp.float32
BF16 = jnp.bfloat16

T = 2048
D = 1024
N_HEADS = 8
QK_NOPE = 128
QK_ROPE = 64
V_HEAD = 128
Q_LORA = 384
KV_LORA = 256
D_FF = 2816
CHUNK = 64
ROPE_THETA = 10000.0
EPS = 1e-6
NEG_INF = -1e30
ADAM_LR = 0.001
ADAM_B1 = 0.9
ADAM_B2 = 0.999
ADAM_EPS = 1e-08
ADAM_WD = 0.01
ADAM_STEP = 10

N_DEV = 8
N_CHIP = 4
FF_BLK = D_FF * 2 // N_DEV
N_FF_BLK = D_FF // FF_BLK
QK_PAD = 256
HALO = 16

TM = 1024
TS = 512
TR = 256
TQ = 512
VMEM_LIMIT = 56 * 1024 * 1024

NN = (((1,), (0,)), ((), ()))
NT = (((1,), (1,)), ((), ()))
TN = (((0,), (0,)), ((), ()))
MESH = pl.DeviceIdType.MESH


def _params(sem):
    return pltpu.CompilerParams(dimension_semantics=sem, vmem_limit_bytes=VMEM_LIMIT)


ANY_SPEC = pl.BlockSpec(memory_space=pl.ANY)
VMEM_SPEC = pl.BlockSpec(memory_space=pltpu.VMEM)


class _Chain:
    last = None


def _pallas(body, *, name, in_specs, out_specs, out_shape, grid=(), scratch_shapes=(), n_prefetch=0, aliases=None,
            params=None):
    def run(*args):
        after = _Chain.last
        n_lead = len(args)
        specs, operands, fn = list(in_specs), list(args), body
        if after is not None:
            def fn(*refs):
                return body(*refs[:n_lead], *refs[n_lead + 1:])
            specs.append(ANY_SPEC)
            operands.append(after)
        kw = dict(name=name, out_shape=out_shape, input_output_aliases=aliases or {})
        if params is not None:
            kw["compiler_params"] = params
        if n_prefetch:
            kw["grid_spec"] = pltpu.PrefetchScalarGridSpec(
                num_scalar_prefetch=n_prefetch, grid=grid, in_specs=specs, out_specs=out_specs,
                scratch_shapes=scratch_shapes)
        else:
            kw.update(grid=grid, in_specs=specs, out_specs=out_specs, scratch_shapes=scratch_shapes)
        outs = pl.pallas_call(fn, **kw)(*operands)
        _Chain.last = outs[0] if isinstance(outs, (list, tuple)) else outs
        return outs
    return run


def _mm(name, a, b, *, grid, a_spec, b_spec, o_spec, o_shape, o_dtype, dims, k_axis=None, acc_shape=None,
        add=None, add_spec=None):
    nk = grid[k_axis] if k_axis is not None else 1
    has_add = add is not None

    def body(*refs):
        a_ref, b_ref = refs[0], refs[1]
        p = 2
        add_ref = None
        if has_add:
            add_ref = refs[p]
            p += 1
        o_ref = refs[p]
        p += 1
        r = lax.dot_general(a_ref[...].astype(BF16), b_ref[...].astype(BF16), dims, preferred_element_type=F32)
        if k_axis is None:
            if has_add:
                r = r + add_ref[...].astype(F32)
            o_ref[...] = r.astype(o_dtype)
        else:
            acc = refs[p]
            k = pl.program_id(k_axis)

            @pl.when(k == 0)
            def _():
                acc[...] = r

            @pl.when(k > 0)
            def _():
                acc[...] += r

            @pl.when(k == nk - 1)
            def _():
                t = acc[...]
                if has_add:
                    t = t + add_ref[...].astype(F32)
                o_ref[...] = t.astype(o_dtype)

    in_specs = [a_spec, b_spec]
    args = [a, b]
    if has_add:
        in_specs.append(add_spec if add_spec is not None else o_spec)
        args.append(add)
    sem = tuple("arbitrary" if ax == k_axis else "parallel" for ax in range(len(grid)))
    scratch = [pltpu.VMEM(acc_shape, F32)] if k_axis is not None else []
    return _pallas(body, name=name, grid=grid, in_specs=in_specs, out_specs=o_spec,
                   out_shape=jax.ShapeDtypeStruct(o_shape, o_dtype), scratch_shapes=scratch, params=_params(sem))(*args)


def _mm_sum(name, parts, *, grid, o_spec, o_shape, o_dtype, add=None, norm_bwd=None):
    has_add = add is not None
    np_ = len(parts)

    def body(*refs):
        acc = None
        for p, (_, _, _, _, dims) in enumerate(parts):
            a_ref, b_ref = refs[2 * p], refs[2 * p + 1]
            for k in range(a_ref.shape[0]):
                r = lax.dot_general(a_ref[k], b_ref[k], dims, preferred_element_type=F32)
                acc = r if acc is None else acc + r
        if norm_bwd is None:
            if has_add:
                acc = acc + refs[2 * np_][...]
            refs[-1][...] = acc.astype(o_dtype)
            return
        x_ref, g_ref, res_ref = refs[2 * np_:2 * np_ + 3]
        dx_ref, dxb_ref, dg_ref = refs[-3:]
        xv = x_ref[...]
        r = lax.rsqrt(jnp.mean(xv * xv, axis=-1, keepdims=True) + EPS)
        xn = xv * r
        gdy = acc * g_ref[...]
        dx = r * (gdy - xn * jnp.mean(gdy * xn, axis=-1, keepdims=True)) + res_ref[...]
        dx_ref[...] = dx
        dxb_ref[...] = dx.astype(BF16)
        part = jnp.sum(acc * xn, axis=0, keepdims=True)

        @pl.when(pl.program_id(0) == 0)
        def _():
            dg_ref[...] = part

        @pl.when(pl.program_id(0) > 0)
        def _():
            dg_ref[...] += part

    in_specs, args = [], []
    for a, a_spec, b, b_spec, _ in parts:
        in_specs += [a_spec, b_spec]
        args += [a, b]
    if norm_bwd is None:
        if has_add:
            in_specs.append(o_spec)
            args.append(add)
        return _pallas(body, name=name, grid=grid, in_specs=in_specs, out_specs=o_spec,
                       out_shape=jax.ShapeDtypeStruct(o_shape, o_dtype),
                       params=_params(("parallel",) * len(grid)))(*args)
    x, gain, dres = norm_bwd
    vec = pl.BlockSpec((1, o_shape[1]), lambda i: (0, 0))
    in_specs += [o_spec, vec, o_spec]
    args += [x, gain, dres]
    return _pallas(body, name=name, grid=grid, in_specs=in_specs, out_specs=[o_spec, o_spec, vec],
                   out_shape=[jax.ShapeDtypeStruct(o_shape, F32), jax.ShapeDtypeStruct(o_shape, BF16),
                              jax.ShapeDtypeStruct((1, o_shape[1]), F32)],
                   params=_params(("arbitrary",)))(*args)


def _mm_rows(name, a, b, dims, o_dtype, n_out, *, tn=None, add=None):
    k = a.shape[1]
    tn = n_out if tn is None else tn
    if dims == NN:
        b_spec = pl.BlockSpec((k, tn), lambda n, i: (0, n))
    else:
        b_spec = pl.BlockSpec((tn, k), lambda n, i: (n, 0))
    return _mm(name, a, b, grid=(n_out // tn, T // TM),
               a_spec=pl.BlockSpec((TM, k), lambda n, i: (i, 0)), b_spec=b_spec,
               o_spec=pl.BlockSpec((TM, tn), lambda n, i: (i, n)), o_shape=(T, n_out), o_dtype=o_dtype,
               dims=dims, add=add)


def _mm_wgrad(name, a, b, *, tn=512):
    k, n = a.shape[1], b.shape[1]
    tn = min(tn, n)
    return _mm(name, a, b, grid=(n // tn,),
               a_spec=pl.BlockSpec((T, k), lambda j: (0, 0)), b_spec=pl.BlockSpec((T, tn), lambda j: (0, j)),
               o_spec=pl.BlockSpec((k, tn), lambda j: (0, j)), o_shape=(k, n), o_dtype=BF16, dims=TN)


def _rms_fwd(name, x, g):
    d = x.shape[1]

    def body(x_ref, g_ref, o_ref):
        xv = x_ref[...]
        r = lax.rsqrt(jnp.mean(xv * xv, axis=-1, keepdims=True) + EPS)
        o_ref[...] = ((xv * r) * g_ref[...]).astype(BF16)

    return _pallas(
        body, name=name, grid=(T // TM,),
        in_specs=[pl.BlockSpec((TM, d), lambda i: (i, 0)), pl.BlockSpec((1, d), lambda i: (0, 0))],
        out_specs=pl.BlockSpec((TM, d), lambda i: (i, 0)),
        out_shape=jax.ShapeDtypeStruct((T, d), BF16), params=_params(("parallel",)))(x, g)


def _rms_bwd(name, x, gains, dys, dres=None):
    d = x.shape[1]
    n = len(gains)
    has_res = dres is not None

    def body(*refs):
        x_ref, g_refs, dy_refs = refs[0], refs[1:1 + n], refs[1 + n:1 + 2 * n]
        dx_ref, dxb_ref = refs[-2 - n], refs[-1 - n]
        dg_refs = refs[-n:]
        xv = x_ref[...]
        r = lax.rsqrt(jnp.mean(xv * xv, axis=-1, keepdims=True) + EPS)
        xn = xv * r
        dx = refs[1 + 2 * n][...] if has_res else None
        parts = []
        for g_ref, dy_ref in zip(g_refs, dy_refs):
            dyv = dy_ref[...].astype(F32)
            gdy = dyv * g_ref[...]
            t = r * (gdy - xn * jnp.mean(gdy * xn, axis=-1, keepdims=True))
            dx = t if dx is None else dx + t
            parts.append(jnp.sum(dyv * xn, axis=0, keepdims=True))
        dx_ref[...] = dx
        dxb_ref[...] = dx.astype(BF16)

        @pl.when(pl.program_id(0) == 0)
        def _():
            for dg_ref, part in zip(dg_refs, parts):
                dg_ref[...] = part

        @pl.when(pl.program_id(0) > 0)
        def _():
            for dg_ref, part in zip(dg_refs, parts):
                dg_ref[...] += part

    row = pl.BlockSpec((TR, d), lambda i: (i, 0))
    vec = pl.BlockSpec((1, d), lambda i: (0, 0))
    args = [x] + list(gains) + list(dys) + ([dres] if has_res else [])
    in_specs = [row] + [vec] * n + [row] * n + ([row] if has_res else [])
    outs = _pallas(
        body, name=name, grid=(T // TR,), in_specs=in_specs, out_specs=[row, row] + [vec] * n,
        out_shape=[jax.ShapeDtypeStruct((T, d), F32), jax.ShapeDtypeStruct((T, d), BF16)]
        + [jax.ShapeDtypeStruct((1, d), F32)] * n,
        params=_params(("arbitrary",)))(*args)
    return outs[0], outs[1], list(outs[2:])


def _final(h, g, tgt):
    def body(h_ref, g_ref, t_ref, loss_ref, dh_ref, dhb_ref, dg_ref):
        hv = h_ref[...]
        r = lax.rsqrt(jnp.mean(hv * hv, axis=-1, keepdims=True) + EPS)
        xn = hv * r
        gv = g_ref[...]
        err = xn * gv - t_ref[...]
        part_loss = 0.5 * jnp.sum(jnp.mean(err * err, axis=-1, keepdims=True), axis=0, keepdims=True)
        dy = err * (1.0 / D)
        gdy = dy * gv
        dh = r * (gdy - xn * jnp.mean(gdy * xn, axis=-1, keepdims=True))
        dh_ref[...] = dh
        dhb_ref[...] = dh.astype(BF16)
        part = jnp.sum(dy * xn, axis=0, keepdims=True)
        first = pl.program_id(0) == 0

        @pl.when(first)
        def _():
            dg_ref[...] = part
            loss_ref[...] = jnp.broadcast_to(part_loss, (1, 128))

        @pl.when(jnp.logical_not(first))
        def _():
            dg_ref[...] += part
            loss_ref[...] += jnp.broadcast_to(part_loss, (1, 128))

    row = pl.BlockSpec((TR, D), lambda i: (i, 0))
    vec = pl.BlockSpec((1, D), lambda i: (0, 0))
    return _pallas(
        body, name="final_loss", grid=(T // TR,), in_specs=[row, vec, row],
        out_specs=[pl.BlockSpec((1, 128), lambda i: (0, 0)), row, row, vec],
        out_shape=[jax.ShapeDtypeStruct((1, 128), F32), jax.ShapeDtypeStruct((T, D), F32),
                   jax.ShapeDtypeStruct((T, D), BF16), jax.ShapeDtypeStruct((1, D), F32)],
        params=_params(("arbitrary",)))(h, g, tgt)


def _prev_idx(i, rows=TR):
    return jnp.maximum(i * (rows // HALO) - 1, 0)


def _next_idx(i, rows=TR):
    return jnp.minimum((i + 1) * (rows // HALO), T // HALO - 1)


def _causal_taps(ext):
    return pltpu.roll(ext, 2, 0)[HALO:], pltpu.roll(ext, 1, 0)[HALO:], ext[HALO:]


def _anticausal_taps(ext, n):
    rows = ext.shape[0]
    return pltpu.roll(ext, rows - 1, 0)[:n], pltpu.roll(ext, rows - 2, 0)[:n]


def _sc_fwd(z, w):
    def body(b_ref, c_ref, ch_ref, u_ref, uh_ref, w_ref, y_ref):
        i = pl.program_id(0)
        cu = c_ref[...].astype(F32) * u_ref[...].astype(F32)
        cuh = ch_ref[...].astype(F32) * uh_ref[...].astype(F32)
        cuh = jnp.where(i > 0, cuh, 0.0)
        x2, x1, x0 = _causal_taps(jnp.concatenate([cuh, cu], axis=0))
        wv = w_ref[...]
        cv = (x2 * wv[0:1] + x1 * wv[1:2]) + x0 * wv[2:3]
        y_ref[...] = (b_ref[...].astype(F32) * cv).astype(BF16)

    def main(part):
        return pl.BlockSpec((TR, D), lambda i: (i, part))

    def halo(part):
        return pl.BlockSpec((HALO, D), lambda i: (_prev_idx(i), part))

    return _pallas(
        body, name="sc_fwd", grid=(T // TR,),
        in_specs=[main(0), main(1), halo(1), main(2), halo(2), pl.BlockSpec((3, D), lambda i: (0, 0))],
        out_specs=pl.BlockSpec((TR, D), lambda i: (i, 0)),
        out_shape=jax.ShapeDtypeStruct((T, D), BF16), params=_params(("parallel",)))(z, z, z, z, z, w)


def _sc_bwd(z, dy, w):
    last = T // TR - 1

    def body(b_ref, bn_ref, c_ref, ch_ref, u_ref, uh_ref, dy_ref, dyn_ref, w_ref, dz_ref, dw_ref):
        i = pl.program_id(0)
        cv_ = c_ref[...].astype(F32)
        uv = u_ref[...].astype(F32)
        cu = cv_ * uv
        cuh = jnp.where(i > 0, ch_ref[...].astype(F32) * uh_ref[...].astype(F32), 0.0)
        x2, x1, x0 = _causal_taps(jnp.concatenate([cuh, cu], axis=0))
        wv = w_ref[...]
        conv = (x2 * wv[0:1] + x1 * wv[1:2]) + x0 * wv[2:3]
        dyv = dy_ref[...]
        dz_ref[:, 0:D] = (dyv * conv).astype(BF16)
        dconv = dyv * b_ref[...].astype(F32)
        dconv_n = jnp.where(i < last, dyn_ref[...] * bn_ref[...].astype(F32), 0.0)
        n1, n2 = _anticausal_taps(jnp.concatenate([dconv, dconv_n], axis=0), TR)
        dcu = (dconv * wv[2:3] + n1 * wv[1:2]) + n2 * wv[0:1]
        dz_ref[:, D:2 * D] = (dcu * uv).astype(BF16)
        dz_ref[:, 2 * D:3 * D] = (dcu * cv_).astype(BF16)
        part = jnp.concatenate([jnp.sum(dconv * x2, axis=0, keepdims=True),
                                jnp.sum(dconv * x1, axis=0, keepdims=True),
                                jnp.sum(dconv * x0, axis=0, keepdims=True)], axis=0)

        @pl.when(i == 0)
        def _():
            dw_ref[...] = part

        @pl.when(i > 0)
        def _():
            dw_ref[...] += part

    def main(part):
        return pl.BlockSpec((TR, D), lambda i: (i, part))

    def prev(part):
        return pl.BlockSpec((HALO, D), lambda i: (_prev_idx(i), part))

    def nxt(part):
        return pl.BlockSpec((HALO, D), lambda i: (_next_idx(i), part))

    wspec = pl.BlockSpec((3, D), lambda i: (0, 0))
    return _pallas(
        body, name="sc_bwd", grid=(T // TR,),
        in_specs=[main(0), nxt(0), main(1), prev(1), main(2), prev(2), main(0), nxt(0), wspec],
        out_specs=[pl.BlockSpec((TR, 3 * D), lambda i: (i, 0)), wspec],
        out_shape=[jax.ShapeDtypeStruct((T, 3 * D), BF16), jax.ShapeDtypeStruct((3, D), F32)],
        params=_params(("arbitrary",)))(z, z, z, z, z, z, dy, dy, w)


def _sigmoid(x):
    return 1.0 / (1.0 + jnp.exp(-x))


def _ffn_up_act(name, hf, w_up, w, b):
    def body(h_ref, hh_ref, wg_ref, wv_ref, w_ref, b_ref, g_ref, v_ref, a_ref):
        i = pl.program_id(1)
        wg = wg_ref[...]
        g = lax.dot_general(h_ref[...], wg, NT, preferred_element_type=F32).astype(BF16)
        gh = lax.dot_general(hh_ref[...], wg, NT, preferred_element_type=F32).astype(BF16)
        v = lax.dot_general(h_ref[...], wv_ref[...], NT, preferred_element_type=F32).astype(BF16)
        g_ref[...] = g
        v_ref[...] = v
        gh = jnp.where(i > 0, gh.astype(F32), 0.0)
        x2, x1, x0 = _causal_taps(jnp.concatenate([gh, g.astype(F32)], axis=0))
        wv = w_ref[...]
        gc = ((x2 * wv[0:1] + x1 * wv[1:2]) + x0 * wv[2:3]) + b_ref[...]
        a_ref[...] = ((gc * _sigmoid(gc)) * v.astype(F32)).astype(BF16)

    blk = pl.BlockSpec((None, TS, FF_BLK), lambda j, i: (j, i, 0))
    out = jax.ShapeDtypeStruct((N_FF_BLK, T, FF_BLK), BF16)
    return _pallas(
        body, name=name, grid=(N_FF_BLK, T // TS),
        in_specs=[pl.BlockSpec((TS, D), lambda j, i: (i, 0)),
                  pl.BlockSpec((HALO, D), lambda j, i: (_prev_idx(i, TS), 0)),
                  pl.BlockSpec((None, None, FF_BLK, D), lambda j, i: (0, j, 0, 0)),
                  pl.BlockSpec((None, None, FF_BLK, D), lambda j, i: (0, j + N_FF_BLK, 0, 0)),
                  pl.BlockSpec((None, 3, FF_BLK), lambda j, i: (j, 0, 0)),
                  pl.BlockSpec((None, 1, FF_BLK), lambda j, i: (j, 0, 0))],
        out_specs=[blk, blk, blk], out_shape=[out, out, out],
        params=_params(("parallel", "parallel")))(hf, hf, w_up, w_up, w, b)


def _ffn_dact(name, dh, w_down4, g, v, w, b):
    last = T // TS - 1

    def body(dh_ref, dhn_ref, wd_ref, g_ref, gp_ref, gn_ref, v_ref, vn_ref, w_ref, b_ref, dg_ref, dv_ref, dw_ref, db_ref):
        i = pl.program_id(1)
        wd = wd_ref[...]
        da = lax.dot_general(dh_ref[...], wd, NT, preferred_element_type=F32)
        dan = lax.dot_general(dhn_ref[...], wd, NT, preferred_element_type=F32)
        da = jnp.concatenate([da, jnp.where(i < last, dan, 0.0)], axis=0)
        gp = jnp.where(i > 0, gp_ref[...].astype(F32), 0.0)
        ext = jnp.concatenate([gp, g_ref[...].astype(F32), gn_ref[...].astype(F32)], axis=0)
        x2, x1, x0 = _causal_taps(ext)
        wv = w_ref[...]
        gc = ((x2 * wv[0:1] + x1 * wv[1:2]) + x0 * wv[2:3]) + b_ref[...]
        sg = _sigmoid(gc)
        vv = jnp.concatenate([v_ref[...].astype(F32), vn_ref[...].astype(F32)], axis=0)
        dv_ref[...] = (da[:TS] * (gc[:TS] * sg[:TS])).astype(BF16)
        dgc = (da * vv) * (sg * (1.0 + gc * (1.0 - sg)))
        n1, n2 = _anticausal_taps(dgc, TS)
        d0 = dgc[:TS]
        dg_ref[...] = ((d0 * wv[2:3] + n1 * wv[1:2]) + n2 * wv[0:1]).astype(BF16)
        part_w = jnp.concatenate([jnp.sum(d0 * x2[:TS], axis=0, keepdims=True),
                                  jnp.sum(d0 * x1[:TS], axis=0, keepdims=True),
                                  jnp.sum(d0 * x0[:TS], axis=0, keepdims=True)], axis=0)
        part_b = jnp.sum(d0, axis=0, keepdims=True)

        @pl.when(i == 0)
        def _():
            dw_ref[...] = part_w
            db_ref[...] = part_b

        @pl.when(i > 0)
        def _():
            dw_ref[...] += part_w
            db_ref[...] += part_b

    blk = pl.BlockSpec((None, TS, FF_BLK), lambda j, i: (j, i, 0))
    prev = pl.BlockSpec((None, HALO, FF_BLK), lambda j, i: (j, _prev_idx(i, TS), 0))
    nxt = pl.BlockSpec((None, HALO, FF_BLK), lambda j, i: (j, _next_idx(i, TS), 0))
    wspec = pl.BlockSpec((None, 3, FF_BLK), lambda j, i: (j, 0, 0))
    bspec = pl.BlockSpec((None, 1, FF_BLK), lambda j, i: (j, 0, 0))
    return _pallas(
        body, name=name, grid=(N_FF_BLK, T // TS),
        in_specs=[pl.BlockSpec((TS, D), lambda j, i: (i, 0)),
                  pl.BlockSpec((HALO, D), lambda j, i: (_next_idx(i, TS), 0)),
                  pl.BlockSpec((None, None, FF_BLK, D), lambda j, i: (0, j, 0, 0)),
                  blk, prev, nxt, blk, nxt, wspec, bspec],
        out_specs=[blk, blk, wspec, bspec],
        out_shape=[jax.ShapeDtypeStruct((N_FF_BLK, T, FF_BLK), BF16), jax.ShapeDtypeStruct((N_FF_BLK, T, FF_BLK), BF16),
                   jax.ShapeDtypeStruct((N_FF_BLK, 3, FF_BLK), F32), jax.ShapeDtypeStruct((N_FF_BLK, 1, FF_BLK), F32)],
        params=_params(("parallel", "arbitrary")))(dh, dh, w_down4, g, g, g, v, v, w, b)


def _rope_tables(pos, inv_freq):
    half = QK_ROPE // 2

    def body(p_ref, f_ref, c_ref, sa_ref, sb_ref):
        ang = p_ref[...].astype(F32) * f_ref[...]
        lane = lax.broadcasted_iota(jnp.int32, (T, 128), 1)
        c = jnp.cos(ang)
        s = jnp.sin(ang)
        c_ref[...] = jnp.where(lane < 2 * half, c, 0.0)
        sa_ref[...] = jnp.where(lane < half, -s, 0.0)
        sb_ref[...] = jnp.where(jnp.logical_and(lane >= half, lane < 2 * half), s, 0.0)

    return _pallas(
        body, name="rope_tables", in_specs=[VMEM_SPEC] * 2, out_specs=[VMEM_SPEC] * 3,
        out_shape=[jax.ShapeDtypeStruct((T, 128), F32)] * 3,
        params=pltpu.CompilerParams(vmem_limit_bytes=VMEM_LIMIT))(pos, inv_freq)


def _rotate(r, c, sa, sb, sign):
    return r * c + sign * (pltpu.roll(r, 96, 1) * sa + pltpu.roll(r, 32, 1) * sb)


def _q_up(cq, w_uq, tables):
    cos, sa, sb = tables

    def body(a_ref, b_ref, c_ref, sa_ref, sb_ref, o_ref):
        r = lax.dot_general(a_ref[...], b_ref[...], NN, preferred_element_type=F32)
        o_ref[:, :QK_NOPE] = r[:, :QK_NOPE].astype(BF16)
        o_ref[:, QK_NOPE:] = _rotate(r[:, QK_NOPE:], c_ref[...], sa_ref[...], sb_ref[...], 1.0).astype(BF16)

    tab = pl.BlockSpec((TM, 128), lambda h, i: (i, 0))
    return _pallas(
        body, name="q_up", grid=(N_HEADS, T // TM),
        in_specs=[pl.BlockSpec((TM, Q_LORA), lambda h, i: (i, 0)),
                  pl.BlockSpec((None, Q_LORA, QK_PAD), lambda h, i: (h, 0, 0)), tab, tab, tab],
        out_specs=pl.BlockSpec((None, TM, QK_PAD), lambda h, i: (h, i, 0)),
        out_shape=jax.ShapeDtypeStruct((N_HEADS, T, QK_PAD), BF16),
        params=_params(("parallel", "parallel")))(cq, w_uq, cos, sa, sb)


def _rope(name, x, tables, sign, out_dtype, reduce_groups=False):
    g, _, w = x.shape
    cos, sa, sb = tables

    def body(x_ref, c_ref, sa_ref, sb_ref, o_ref):
        xv = x_ref[...].astype(F32)
        if reduce_groups:
            acc = xv[0]
            for k in range(1, g):
                acc = acc + xv[k]
            xv = acc
        out = _rotate(xv[:, w - 128:], c_ref[...], sa_ref[...], sb_ref[...], sign)
        if w > 128:
            o_ref[:, :w - 128] = xv[:, :w - 128].astype(out_dtype)
        o_ref[:, w - 128:] = out.astype(out_dtype)

    tab = pl.BlockSpec((TM, 128), lambda h, i: (i, 0))
    if reduce_groups:
        x_spec = pl.BlockSpec((g, TM, w), lambda h, i: (0, i, 0))
        groups = 1
    else:
        x_spec = pl.BlockSpec((None, TM, w), lambda h, i: (h, i, 0))
        groups = g
    return _pallas(
        body, name=name, grid=(groups, T // TM), in_specs=[x_spec, tab, tab, tab],
        out_specs=pl.BlockSpec((None, TM, w), lambda h, i: (h, i, 0)),
        out_shape=jax.ShapeDtypeStruct((groups, T, w), out_dtype),
        params=_params(("parallel", "parallel")))(x, cos, sa, sb)


SCALE = (QK_NOPE + QK_ROPE) ** -0.5
LOG2E = 1.4426950408889634
SCALE2 = SCALE * LOG2E


def _diag_mask(transposed):
    shift = CHUNK.bit_length() - 1
    a = lax.broadcasted_iota(jnp.int32, (TQ, TQ), 0) >> shift
    b = lax.broadcasted_iota(jnp.int32, (TQ, TQ), 1) >> shift
    return (a <= b) if transposed else (b <= a)


def _keys(kn_ref, kr_ref, off):
    return jnp.concatenate([kn_ref[pl.ds(off, TQ), :], kr_ref[pl.ds(off, TQ), :]], axis=1)


def _attn_fwd(q, kn, kr, v):
    def body(q_ref, kn_ref, kr_ref, v_ref, o_ref, lse_ref):
        i = pl.program_id(1)
        qv = q_ref[...]

        def step(j, carry, masked):
            m, l, acc = carry
            off = pl.multiple_of(j * TQ, TQ)
            s = lax.dot_general(qv, _keys(kn_ref, kr_ref, off), NT, preferred_element_type=F32) * SCALE2
            if masked:
                s = jnp.where(_diag_mask(False), s, NEG_INF)
            m_new = jnp.maximum(m, jnp.max(s, axis=-1, keepdims=True))
            p = jnp.exp2(s - m_new)
            alpha = jnp.exp2(m - m_new)
            l = alpha * l + jnp.sum(p, axis=-1, keepdims=True)
            acc = alpha * acc + lax.dot_general(p.astype(BF16), v_ref[pl.ds(off, TQ), :], NN, preferred_element_type=F32)
            return m_new, l, acc

        init = (jnp.full((TQ, 1), NEG_INF, F32), jnp.zeros((TQ, 1), F32), jnp.zeros((TQ, V_HEAD), F32))
        carry = lax.fori_loop(0, i, lambda j, cr: step(j, cr, False), init)
        m, l, acc = step(i, carry, True)
        o_ref[...] = (acc / l).astype(BF16)
        lse_ref[...] = m + jnp.log(l) * LOG2E

    return _pallas(
        body, name="attn_fwd", grid=(N_HEADS, T // TQ),
        in_specs=[pl.BlockSpec((None, TQ, QK_PAD), lambda h, i: (h, i, 0)),
                  pl.BlockSpec((T, QK_NOPE), lambda h, i: (0, h)),
                  pl.BlockSpec((T, 128), lambda h, i: (0, 0)),
                  pl.BlockSpec((T, V_HEAD), lambda h, i: (0, h))],
        out_specs=[pl.BlockSpec((TQ, V_HEAD), lambda h, i: (i, h)), pl.BlockSpec((None, TQ, 1), lambda h, i: (h, i, 0))],
        out_shape=[jax.ShapeDtypeStruct((T, N_HEADS * V_HEAD), BF16), jax.ShapeDtypeStruct((N_HEADS, T, 1), F32)],
        params=_params(("parallel", "parallel")))(q, kn, kr, v)


def _attn_delta(o, do):
    def body(o_ref, do_ref, dl_ref):
        dl_ref[...] = jnp.sum(do_ref[...].astype(F32) * o_ref[...].astype(F32), axis=-1, keepdims=True)

    head = pl.BlockSpec((TM, V_HEAD), lambda h, i: (i, h))
    return _pallas(
        body, name="attn_delta", grid=(N_HEADS, T // TM), in_specs=[head, head],
        out_specs=pl.BlockSpec((None, TM, 1), lambda h, i: (h, i, 0)),
        out_shape=jax.ShapeDtypeStruct((N_HEADS, T, 1), F32), params=_params(("parallel", "parallel")))(o, do)


def _attn_bwd(q, kn, kr, v, do, lse_row, delta_row, tables):
    nq = T // TQ
    cos, sa, sb = tables

    def body(q_ref, kn_ref, kr_ref, v_ref, do_ref, lse_ref, dl_ref, c_ref, sa_ref, sb_ref,
             dq_ref, dkn_ref, dkr_ref, dv_ref, dq_acc):
        j = pl.program_id(1)

        @pl.when(j == 0)
        def _():
            dq_acc[...] = jnp.zeros_like(dq_acc)

        kk = jnp.concatenate([kn_ref[...], kr_ref[...]], axis=1)
        vv = v_ref[...]

        def step(i, carry, masked):
            dk, dv = carry
            off = pl.multiple_of(i * TQ, TQ)
            qi = q_ref[pl.ds(off, TQ), :]
            doi = do_ref[pl.ds(off, TQ), :]
            st = lax.dot_general(kk, qi, NT, preferred_element_type=F32) * SCALE2
            if masked:
                st = jnp.where(_diag_mask(True), st, NEG_INF)
            pt = jnp.exp2(st - lse_ref[:, pl.ds(off, TQ)])
            dv = dv + lax.dot_general(pt.astype(BF16), doi, NN, preferred_element_type=F32)
            dpt = lax.dot_general(vv, doi, NT, preferred_element_type=F32)
            dst = ((pt * (dpt - dl_ref[:, pl.ds(off, TQ)])) * SCALE).astype(BF16)
            dk = dk + lax.dot_general(dst, qi, NN, preferred_element_type=F32)
            dq_acc[pl.ds(off, TQ), :] += lax.dot_general(dst, kk, TN, preferred_element_type=F32)
            return dk, dv

        carry = step(j, (jnp.zeros((TQ, QK_PAD), F32), jnp.zeros((TQ, V_HEAD), F32)), True)
        dk, dv = lax.fori_loop(j + 1, nq, lambda i, cr: step(i, cr, False), carry)
        dkn_ref[...] = dk[:, :QK_NOPE].astype(BF16)
        dkr_ref[...] = dk[:, QK_NOPE:]
        dv_ref[...] = dv.astype(BF16)

        @pl.when(j == nq - 1)
        def _():
            dq = dq_acc[...]
            dq_ref[:, :QK_NOPE] = dq[:, :QK_NOPE].astype(BF16)
            dq_ref[:, QK_NOPE:] = _rotate(dq[:, QK_NOPE:], c_ref[...], sa_ref[...], sb_ref[...], -1.0).astype(BF16)

    row = pl.BlockSpec((None, 1, T), lambda h, j: (h, 0, 0))
    head = pl.BlockSpec((TQ, 128), lambda h, j: (j, h))
    whole = pl.BlockSpec((None, T, QK_PAD), lambda h, j: (h, 0, 0))
    tab = pl.BlockSpec((T, 128), lambda h, j: (0, 0))
    return _pallas(
        body, name="attn_bwd", grid=(N_HEADS, nq),
        in_specs=[whole, head, pl.BlockSpec((TQ, 128), lambda h, j: (j, 0)), head,
                  pl.BlockSpec((T, V_HEAD), lambda h, j: (0, h)), row, row, tab, tab, tab],
        out_specs=[whole, head, pl.BlockSpec((None, TQ, 128), lambda h, j: (h, j, 0)), head],
        out_shape=[jax.ShapeDtypeStruct((N_HEADS, T, QK_PAD), BF16), jax.ShapeDtypeStruct((T, N_HEADS * QK_NOPE), BF16),
                   jax.ShapeDtypeStruct((N_HEADS, T, 128), F32), jax.ShapeDtypeStruct((T, N_HEADS * V_HEAD), BF16)],
        scratch_shapes=[pltpu.VMEM((T, QK_PAD), F32)],
        params=_params(("parallel", "arbitrary")))(q, kn, kr, v, do, lse_row, delta_row, cos, sa, sb)


def _ffn_gup(name, dg, dv, hf):
    def body(dg_ref, dv_ref, hf_ref, o_ref):
        j = pl.program_id(0)

        @pl.when(j < N_FF_BLK)
        def _():
            o_ref[...] = lax.dot_general(dg_ref[...], hf_ref[...], TN, preferred_element_type=F32).astype(BF16)

        @pl.when(j >= N_FF_BLK)
        def _():
            o_ref[...] = lax.dot_general(dv_ref[...], hf_ref[...], TN, preferred_element_type=F32).astype(BF16)

    return _pallas(
        body, name=name, grid=(N_DEV,),
        in_specs=[pl.BlockSpec((None, T, FF_BLK), lambda j: (jnp.minimum(j, N_FF_BLK - 1), 0, 0)),
                  pl.BlockSpec((None, T, FF_BLK), lambda j: (jnp.maximum(j - N_FF_BLK, 0), 0, 0)),
                  pl.BlockSpec((T, D), lambda j: (0, 0))],
        out_specs=pl.BlockSpec((None, FF_BLK, D), lambda j: (j, 0, 0)),
        out_shape=jax.ShapeDtypeStruct((N_DEV, FF_BLK, D), BF16), params=_params(("parallel",)))(dg, dv, hf)


def _ffn_layer_fwd(tag, h, gain, ex):
    hf = _rms_fwd(f"{tag}_norm", h, gain)
    g, v, act = _ffn_up_act(f"{tag}_up", hf, ex.need(f"ffn_w_up{tag[1]}", hf), ex.need(f"ffn_cw{tag[1]}", hf),
                            ex.need(f"ffn_cb{tag[1]}", hf))
    ex.at(f"{tag}_up", act)
    rows = pl.BlockSpec((TS, D), lambda i: (i, 0))
    out = _mm_sum(f"{tag}_down",
                  [(act, pl.BlockSpec((N_FF_BLK, TS, FF_BLK), lambda i: (0, i, 0)), ex.need(f"ffn_w_down{tag[1]}", act),
                    pl.BlockSpec((None, N_FF_BLK, FF_BLK, D), lambda i: (0, 0, 0, 0)), NN)],
                  grid=(T // TS,), o_spec=rows, o_shape=(T, D), o_dtype=F32, add=h)
    ex.at(f"{tag}_down", out)
    return out, (hf, g, v, act)


def _ffn_layer_bwd(tag, h, gain, ex, saved, dh, dh_bf):
    hf, g, v, act = saved
    layer = tag[1]
    w_up, w_down4 = ex.need(f"ffn_w_up{layer}", dh_bf), ex.need(f"ffn_w_down{layer}", dh_bf)
    dg, dv, dcw, dcb = _ffn_dact(f"{tag}_dact", dh_bf, w_down4, g, v, ex.need(f"ffn_cw{layer}", dh_bf),
                                 ex.need(f"ffn_cb{layer}", dh_bf))
    ex.at(f"{tag}_dact", dg)
    tn = 512
    g_down = _mm(f"{tag}_gdown", act, dh_bf, grid=(N_FF_BLK, D // tn),
                 a_spec=pl.BlockSpec((None, T, FF_BLK), lambda j, n: (j, 0, 0)),
                 b_spec=pl.BlockSpec((T, tn), lambda j, n: (0, n)),
                 o_spec=pl.BlockSpec((FF_BLK, tn), lambda j, n: (j, n)),
                 o_shape=(D_FF, D), o_dtype=BF16, dims=TN)
    g_up = _ffn_gup(f"{tag}_gup", dg, dv, hf)
    ex.grad("ffn_w_up", int(layer), g_up.reshape(1, N_DEV, FF_BLK, D))
    ex.grad("ffn_w_down", int(layer), g_down.reshape(1, N_DEV, D_FF // N_DEV, D))
    ex.at(f"{tag}_gup", g_up)
    part = pl.BlockSpec((N_FF_BLK, TR, FF_BLK), lambda i: (0, i, 0))
    dh_in, dh_in_bf, dgain = _mm_sum(
        f"{tag}_dhf",
        [(dg, part, w_up, pl.BlockSpec((None, N_FF_BLK, FF_BLK, D), lambda i: (0, 0, 0, 0)), NN),
         (dv, part, w_up, pl.BlockSpec((None, N_FF_BLK, FF_BLK, D), lambda i: (0, 1, 0, 0)), NN)],
        grid=(T // TR,), o_spec=pl.BlockSpec((TR, D), lambda i: (i, 0)), o_shape=(T, D), o_dtype=F32,
        norm_bwd=(h, gain, dh))
    ex.at(f"{tag}_dhf", dh_in)
    return dh_in, dh_in_bf, dgain, dcw, dcb


def _local_step(x, pos, tgt, rep, ex):
    attn_norm, ffn_norm, final_norm = rep["attn_norm"], rep["ffn_norm"], rep["final_norm"]
    half = QK_ROPE // 2
    inv = 1.0 / (ROPE_THETA ** (jnp.arange(half, dtype=F32) / half))
    inv_freq = jnp.concatenate([inv, inv, jnp.zeros((128 - 2 * half,), F32)]).reshape(1, 128)
    tables = _rope_tables(pos, inv_freq)

    hn0 = _rms_fwd("l0_norm", x, attn_norm[0:1])
    w_in = ex.need("sc_w_in", hn0)
    ex.at("mixer_ready", hn0)
    z = _mm_rows("l0_in", hn0, w_in, NN, BF16, 3 * D, tn=512)
    ex.at("l0_in", z)
    y = _sc_fwd(z, ex.need("sc_conv_w", z))
    h1 = _mm_rows("l0_out", y, ex.need("sc_w_out", y), NN, F32, D, tn=512, add=x)
    ex.at("l0_out", h1)
    h2, ffn0 = _ffn_layer_fwd("f0", h1, ffn_norm[0:1], ex)

    hk = _rms_fwd("kv_norm", h2, rep["kv_in_norm"])
    ckv_raw = _mm_rows("kv_down", hk, ex.need("w_dkv", hk), NN, F32, KV_LORA)
    kr_raw = _mm_rows("kv_rope", hk, ex.need("w_kr", hk), NN, F32, 128)
    ckv = _rms_fwd("kv_lnorm", ckv_raw, rep["kv_latent_norm"])
    kn = _mm_rows("kv_uk", ckv, ex.need("w_uk", ckv), NN, BF16, N_HEADS * QK_NOPE)
    vv = _mm_rows("kv_uv", ckv, ex.need("w_uv", ckv), NN, BF16, N_HEADS * V_HEAD)
    kr = _rope("k_rope", kr_raw.reshape(1, T, 128), tables, 1.0, BF16).reshape(T, 128)

    hn1 = _rms_fwd("l1_norm", h2, attn_norm[1:2])
    cq_raw = _mm_rows("q_down", hn1, ex.need("w_dq", hn1), NN, F32, Q_LORA)
    cq = _rms_fwd("q_lnorm", cq_raw, rep["q_latent_norm"])
    w_uq = ex.need("w_uq", cq)
    q = _q_up(cq, w_uq, tables)
    o, lse = _attn_fwd(q, kn, kr, vv)
    ex.at("attn_fwd", o)
    w_o = ex.need("w_o", o)
    h3 = _mm_rows("attn_out", o, w_o, NN, F32, D, tn=512, add=h2)
    h4, ffn1 = _ffn_layer_fwd("f1", h3, ffn_norm[1:2], ex)

    loss, dh4, dh4_bf, d_final = _final(h4, final_norm.reshape(1, D), tgt)

    dh3, dh3_bf, d_fn1, dcw1, dcb1 = _ffn_layer_bwd("f1", h3, ffn_norm[1:2], ex, ffn1, dh4, dh4_bf)
    ex.at("f1_bwd", dh3)

    do = _mm_rows("d_attn_out", dh3_bf, w_o, NT, BF16, N_HEADS * V_HEAD)
    ex.grad("w_o", None, _mm_wgrad("g_w_o", o, dh3_bf).reshape(1, N_DEV, D // N_DEV, D))
    delta = _attn_delta(o, do)
    dq_pre, dkn, dkr, dvv = _attn_bwd(q, kn, kr, vv, do, lse.reshape(N_HEADS, 1, T), delta.reshape(N_HEADS, 1, T), tables)
    dcq = _mm("d_q_up", dq_pre, w_uq, grid=(T // TM, N_HEADS),
              a_spec=pl.BlockSpec((None, TM, QK_PAD), lambda i, h: (h, i, 0)),
              b_spec=pl.BlockSpec((None, Q_LORA, QK_PAD), lambda i, h: (h, 0, 0)),
              o_spec=pl.BlockSpec((TM, Q_LORA), lambda i, h: (i, 0)), o_shape=(T, Q_LORA), o_dtype=F32,
              dims=NT, k_axis=1, acc_shape=(TM, Q_LORA))
    g_uq = _mm("g_w_uq", cq, dq_pre, grid=(N_HEADS,),
               a_spec=pl.BlockSpec((T, Q_LORA), lambda h: (0, 0)),
               b_spec=pl.BlockSpec((None, T, QK_PAD), lambda h: (h, 0, 0)),
               o_spec=pl.BlockSpec((None, Q_LORA, QK_PAD), lambda h: (h, 0, 0)),
               o_shape=(N_HEADS, Q_LORA, QK_PAD), o_dtype=BF16, dims=TN)
    ex.grad("w_uq", None, g_uq[:, :, :QK_NOPE + QK_ROPE].reshape(1, N_DEV, Q_LORA, QK_NOPE + QK_ROPE))
    _, dcq_raw_bf, (d_qln,) = _rms_bwd("d_q_lnorm", cq_raw, [rep["q_latent_norm"]], [dcq])
    dhn1 = _mm_rows("d_q_down", dcq_raw_bf, ex.need("w_dq", dcq_raw_bf), NT, F32, D)
    ex.grad("w_dq", None, _mm_wgrad("g_w_dq", hn1, dcq_raw_bf).reshape(1, N_DEV, D // N_DEV, Q_LORA))

    dckv = _mm_rows("d_kv_uk", dkn, ex.need("w_uk", dkn), NT, F32, KV_LORA)
    dckv = _mm_rows("d_kv_uv", dvv, ex.need("w_uv", dvv), NT, F32, KV_LORA, add=dckv)
    ex.grad("w_uk", None, _mm_wgrad("g_w_uk", ckv, dkn))
    ex.grad("w_uv", None, _mm_wgrad("g_w_uv", ckv, dvv))
    _, dckv_raw_bf, (d_kvln,) = _rms_bwd("d_kv_lnorm", ckv_raw, [rep["kv_latent_norm"]], [dckv])
    dkr_raw_bf = _rope("dk_rope", dkr, tables, -1.0, BF16, reduce_groups=True).reshape(T, 128)
    dhk = _mm_rows("d_kv_down", dckv_raw_bf, ex.need("w_dkv", dckv_raw_bf), NT, F32, D)
    dhk = _mm_rows("d_kv_rope", dkr_raw_bf, ex.need("w_kr", dkr_raw_bf), NT, F32, D, add=dhk)
    ex.grad("w_dkv", None, _mm_wgrad("g_w_dkv", hk, dckv_raw_bf).reshape(1, N_DEV, D // N_DEV, KV_LORA))
    ex.grad("w_kr", None, _mm_wgrad("g_w_kr", hk, dkr_raw_bf)[:, :QK_ROPE].reshape(1, N_DEV, D // N_DEV, QK_ROPE))
    dh2, dh2_bf, (d_an1, d_kvin) = _rms_bwd("d_h2_norms", h2, [attn_norm[1:2], rep["kv_in_norm"]], [dhn1, dhk], dres=dh3)
    ex.at("kv_bwd", dh2)

    dh1, dh1_bf, d_fn0, dcw0, dcb0 = _ffn_layer_bwd("f0", h1, ffn_norm[0:1], ex, ffn0, dh2, dh2_bf)
    ex.at("f0_bwd", dh1)

    dy = _mm_rows("d_l0_out", dh1_bf, ex.need("sc_w_out", dh1_bf), NT, F32, D)
    ex.grad("sc_w_out", None, _mm_wgrad("g_sc_w_out", y, dh1_bf).reshape(1, N_DEV, D // N_DEV, D))
    dz, d_scw = _sc_bwd(z, dy, ex.need("sc_conv_w", dy))
    g_in = _mm_wgrad("g_sc_w_in", hn0, dz)
    ex.grad("sc_w_in", None, g_in)
    ex.at("sc_bwd", g_in)
    dhn0 = _mm_rows("d_l0_in", dz, ex.need("sc_w_in", dz), NT, F32, D)
    ex.at("d_l0_in", dhn0)
    grad_x, _, (d_an0,) = _rms_bwd("d_l0_norm", x, [attn_norm[0:1]], [dhn0], dres=dh1)

    small = {
        "attn_norm": jnp.concatenate([d_an0, d_an1], axis=0),
        "ffn_norm": jnp.concatenate([d_fn0, d_fn1], axis=0),
        "final_norm": d_final.reshape(D),
        "kv_in_norm": d_kvin.reshape(D),
        "kv_latent_norm": d_kvln.reshape(KV_LORA),
        "q_latent_norm": d_qln,
        "ffn_conv_b": jnp.stack([dcb0, dcb1]).transpose(0, 2, 1, 3).reshape(2, D_FF),
        "sc_conv_w": d_scw,
        "ffn_conv_w": jnp.stack([dcw0, dcw1]).transpose(0, 2, 1, 3).reshape(2, 3, D_FF),
    }
    return loss, grad_x, small


def _place():
    return lax.axis_index("x"), lax.axis_index("y"), lax.axis_index("c")


def _peers():
    x, y, c = _place()
    return (x, y, 1 - c), [(1 - x, y), (x, 1 - y), (1 - x, 1 - y)]


def _window(ref, kind, dev):
    if kind == "blocked":
        return ref.at[:, dev]
    width = ref.shape[-1] // N_DEV
    return ref.at[:, pl.ds(pl.multiple_of(dev * width, 128), width)]


def _all_gather(name, items):
    n = len(items)
    out_shapes = []
    for shard, kind in items:
        if kind == "blocked":
            shape = (shard.shape[0], N_DEV) + shard.shape[1:]
        else:
            shape = (shard.shape[0], N_DEV * shard.shape[1])
        out_shapes.append(jax.ShapeDtypeStruct(shape, shard.dtype))

    def body(*refs):
        srcs, outs = refs[:n], refs[n:2 * n]
        send_sems, recv_sems, local_sems = refs[2 * n:]
        x, y, c = _place()
        me = 4 * x + 2 * y + c
        sibling, chips = _peers()

        def num(px, py, pc):
            return 4 * px + 2 * py + pc

        def copy(t, k, dev, to, from_src):
            kind = items[t][1]
            dst = _window(outs[t], kind, dev)
            return pltpu.make_async_remote_copy(
                src_ref=srcs[t] if from_src else dst, dst_ref=dst,
                send_sem=send_sems.at[t, k], recv_sem=recv_sems.at[t, k], device_id=to, device_id_type=MESH)

        mine = [pltpu.make_async_copy(srcs[t], _window(outs[t], items[t][1], me), local_sems.at[t]) for t in range(n)]
        for cp in mine:
            cp.start()
        first = []
        for t in range(n):
            first.append(copy(t, 0, me, sibling, True))
            for j, chip in enumerate(chips):
                first.append(copy(t, 1 + j, me, (*chip, c), True))
        for cp in first:
            cp.start()
        passed = []
        for j, chip in enumerate(chips):
            for t in range(n):
                copy(t, 1 + j, num(*chip, c), (x, y, c), False).wait_recv()
                fwd = copy(t, 4 + j, num(*chip, c), sibling, False)
                fwd.start()
                passed.append(fwd)
        for t in range(n):
            copy(t, 0, num(x, y, 1 - c), (x, y, c), False).wait_recv()
            for j, chip in enumerate(chips):
                copy(t, 4 + j, num(*chip, 1 - c), (x, y, c), False).wait_recv()
        for cp in first + passed:
            cp.wait_send()
        for cp in mine:
            cp.wait()

    return _pallas(
        body, name=name, in_specs=[ANY_SPEC] * n, out_specs=[ANY_SPEC] * n, out_shape=out_shapes,
        scratch_shapes=[pltpu.SemaphoreType.DMA((n, 7)), pltpu.SemaphoreType.DMA((n, 7)), pltpu.SemaphoreType.DMA((n,))],
    )(*[s for s, _ in items])


HBM_SPEC = pl.BlockSpec(memory_space=pltpu.HBM)
SEM_SPEC = pl.BlockSpec(memory_space=pltpu.SEMAPHORE)
EFFECT = pltpu.SideEffectType.DATAFLOW_SIDE_EFFECTING
TOKEN = jax.ShapeDtypeStruct((8, 128), F32)


def _hbm(a):
    return pltpu.with_memory_space_constraint(a, pltpu.HBM)


def _copies_start(name, srcs, lands, ncopy, plan):
    ns, nl = len(srcs), len(lands)

    def body(*refs):
        send, recv, token = refs[ns + nl], refs[ns + nl + 1], refs[-1]
        copies = plan(refs[:ns], refs[ns:ns + nl])
        assert len(copies) == ncopy
        for k, (sent, dst, to, _) in enumerate(copies):
            pltpu.make_async_remote_copy(src_ref=sent, dst_ref=dst, send_sem=send.at[k], recv_sem=recv.at[k],
                                         device_id=to, device_id_type=MESH).start()
        token[...] = jnp.zeros_like(token)

    arrays = list(srcs) + list(lands)
    outs = pl.pallas_call(
        body, name=name, in_specs=[HBM_SPEC] * (ns + nl),
        out_specs=[SEM_SPEC] * 2 + [HBM_SPEC] * (ns + nl) + [VMEM_SPEC],
        out_shape=[pltpu.SemaphoreType.DMA((ncopy,))] * 2 + [pltpu.HBM(a.shape, a.dtype) for a in arrays] + [TOKEN],
        input_output_aliases={i: 2 + i for i in range(ns + nl)},
        compiler_params=pltpu.CompilerParams(has_side_effects=EFFECT))(*[_hbm(a) for a in arrays])
    _Chain.last = outs[-1]
    return outs[0], outs[1], list(outs[2:2 + ns]), list(outs[2 + ns:-1])


def _copies_wait(name, started, ncopy, plan):
    send, recv, srcs, lands = started
    ns, nl = len(srcs), len(lands)

    def body(*refs):
        send_ref, recv_ref, token = refs[ns + nl], refs[ns + nl + 1], refs[-1]
        copies = plan(refs[:ns], refs[ns:ns + nl])
        assert len(copies) == ncopy
        for k, (sent, _, to, landed) in enumerate(copies):
            cp = pltpu.make_async_remote_copy(src_ref=sent, dst_ref=landed, send_sem=send_ref.at[k],
                                              recv_sem=recv_ref.at[k], device_id=to, device_id_type=MESH)
            cp.wait_send()
            cp.wait_recv()
        token[...] = jnp.zeros_like(token)

    arrays = list(srcs) + list(lands)
    outs = pl.pallas_call(
        body, name=name, in_specs=[HBM_SPEC] * (ns + nl) + [SEM_SPEC] * 2 + [ANY_SPEC],
        out_specs=[HBM_SPEC] * (ns + nl) + [VMEM_SPEC], out_shape=[pltpu.HBM(a.shape, a.dtype) for a in arrays] + [TOKEN],
        input_output_aliases={i: i for i in range(ns + nl)},
        compiler_params=pltpu.CompilerParams(has_side_effects=EFFECT))(*arrays, send, recv, _Chain.last)
    _Chain.last = outs[-1]
    return list(outs[:ns]), list(outs[ns:-1])


def _plan_gather_chips(kinds):
    def plan(srcs, lands):
        x, y, c = _place()
        sibling, chips = _peers()
        out = []
        for t, kind in enumerate(kinds):
            mine = _window(lands[t], kind, 4 * x + 2 * y + c)
            out.append((srcs[t], mine, sibling, _window(lands[t], kind, 4 * x + 2 * y + 1 - c)))
            for px, py in chips:
                out.append((srcs[t], mine, (px, py, c), _window(lands[t], kind, 4 * px + 2 * py + c)))
        return out
    return plan, 4 * len(kinds)


def _plan_gather_sibling(kinds):
    def plan(srcs, lands):
        _, _, c = _place()
        sibling, chips = _peers()
        out = []
        for t, kind in enumerate(kinds):
            for px, py in chips:
                w = _window(lands[t], kind, 4 * px + 2 * py + c)
                out.append((w, w, sibling, _window(lands[t], kind, 4 * px + 2 * py + 1 - c)))
        return out
    return plan, 3 * len(kinds)


def _plan_scatter_sibling(kinds):
    def plan(srcs, lands):
        _, _, c = _place()
        sibling, _ = _peers()
        out = []
        for t, kind in enumerate(kinds):
            for k in range(N_CHIP):
                out.append((_window(srcs[t], kind, 2 * k + 1 - c), lands[t].at[k], sibling, lands[t].at[k]))
        return out
    return plan, N_CHIP * len(kinds)


def _plan_scatter_chips(n):
    def plan(srcs, lands):
        x, y, c = _place()
        _, chips = _peers()
        out = []
        for t in range(n):
            for px, py in chips:
                out.append((srcs[t].at[2 * px + py], lands[t].at[2 * x + y], (px, py, c), lands[t].at[2 * px + py]))
        return out
    return plan, 3 * n


def _landing(shard, kind, me):
    if kind == "blocked":
        land = lax.empty((shard.shape[0], N_DEV) + shard.shape[1:], shard.dtype)
        return lax.dynamic_update_slice(land, shard[:, None], (0, me) + (0,) * (shard.ndim - 1))
    land = lax.empty((shard.shape[0], N_DEV * shard.shape[1]), shard.dtype)
    return lax.dynamic_update_slice(land, shard, (0, me * shard.shape[1]))


def _chip_sums(name, grads, kinds, recvs, c):
    n = len(grads)
    in_specs, out_specs, out_shape, args = [], [], [], []
    for gr, kind, rv in zip(grads, kinds, recvs):
        if kind == "blocked":
            rows, w = gr.shape[2], gr.shape[3]
            in_specs.append(pl.BlockSpec((None, None, rows, w), lambda k, cref: (0, 2 * k + cref[0], 0, 0)))
        else:
            rows, w = gr.shape[0], gr.shape[1] // N_DEV
            in_specs.append(pl.BlockSpec((rows, w), lambda k, cref: (0, 2 * k + cref[0])))
        blk = pl.BlockSpec((None, rows, w), lambda k, cref: (k, 0, 0))
        in_specs.append(blk)
        out_specs.append(blk)
        out_shape.append(jax.ShapeDtypeStruct((N_CHIP, rows, w), BF16))
        args += [gr, rv.reshape(N_CHIP, rows, w)]

    def body(*refs):
        for t in range(n):
            g_ref, r_ref, o_ref = refs[1 + 2 * t], refs[2 + 2 * t], refs[1 + 2 * n + t]
            o_ref[...] = (g_ref[...].astype(F32) + r_ref[...].astype(F32)).astype(BF16)

    return _pallas(body, name=name, n_prefetch=1, grid=(N_CHIP,), in_specs=in_specs, out_specs=out_specs,
                   out_shape=out_shape, params=_params(("parallel",)))(c, *args)


def _adamw_math(g, wv, mv, vv):
    m = ADAM_B1 * mv + (1.0 - ADAM_B1) * g
    v = ADAM_B2 * vv + (1.0 - ADAM_B2) * (g * g)
    m_hat = m / (1.0 - ADAM_B1 ** ADAM_STEP)
    v_hat = v / (1.0 - ADAM_B2 ** ADAM_STEP)
    delta = -ADAM_LR * (m_hat / (jnp.sqrt(v_hat) + ADAM_EPS) + ADAM_WD * wv)
    return delta, m, v


ADAM_STEPS = 2


def _adamw_group(name, items, chip_ids):
    n = len(items)
    in_specs, out_specs, out_shape, args, prevs = [], [], [], [chip_ids], []
    for own, recv, w3, m3, v3, layer, _ in items:
        nl, rows, w = w3.shape
        tr = rows // ADAM_STEPS
        assert tr % 16 == 0, (name, rows)
        in_specs += [pl.BlockSpec((None, tr, w), lambda i, ids, slot=slot: (ids[slot], i, 0)) for slot in range(4)]
        slab = pl.BlockSpec((None, tr, w), lambda i, ids, layer=layer: (layer, i, 0))
        in_specs += [slab] * 3
        out_specs += [slab] * 4
        out_shape += [jax.ShapeDtypeStruct((nl, rows, w), F32)] * 4
        args += [own, recv, recv, recv, w3, m3, v3]
    aliases = {}
    for t, item in enumerate(items):
        if item[6] is not None:
            for k in range(4):
                aliases[len(args) + k] = 4 * t + k
            in_specs += [ANY_SPEC] * 4
            args += list(item[6])
            prevs.append(t)
    n_in = 1 + 7 * n + 4 * len(prevs)

    def body(*refs):
        for t in range(n):
            own_ref, r1_ref, r2_ref, r3_ref, w_ref, m_ref, v_ref = refs[1 + 7 * t:8 + 7 * t]
            g_ref, d_ref, nm_ref, nv_ref = refs[n_in + 4 * t:n_in + 4 * t + 4]
            g = ((own_ref[...].astype(F32) + r1_ref[...].astype(F32)) + r2_ref[...].astype(F32)) + r3_ref[...].astype(F32)
            g_ref[...] = g
            d_ref[...], nm_ref[...], nv_ref[...] = _adamw_math(g, w_ref[...], m_ref[...], v_ref[...])

    outs = _pallas(body, name=name, n_prefetch=1, grid=(ADAM_STEPS,), in_specs=in_specs, out_specs=out_specs,
                   out_shape=out_shape, aliases=aliases, params=_params(("parallel",)))(*args)
    return [list(outs[4 * t:4 * t + 4]) for t in range(n)]


def _adamw_small(gathered, ws, ms, vs):
    n = len(gathered)
    full = [w is not None for w in ws]
    args = list(gathered)
    out_shape = []
    for t in range(n):
        shape = jax.ShapeDtypeStruct(gathered[t].shape[2:], F32)
        if full[t]:
            args += [ws[t], ms[t], vs[t]]
            out_shape += [shape] * 4
        else:
            out_shape += [shape]

    def body(*refs):
        i_in, i_out = n, len(args)
        for t in range(n):
            p_ref = refs[t]
            g = p_ref[0, 0]
            for k in range(1, N_DEV):
                g = g + p_ref[0, k]
            refs[i_out][...] = g
            if full[t]:
                w_ref, m_ref, v_ref = refs[i_in:i_in + 3]
                refs[i_out + 1][...], refs[i_out + 2][...], refs[i_out + 3][...] = _adamw_math(
                    g, w_ref[...], m_ref[...], v_ref[...])
                i_in += 3
                i_out += 4
            else:
                i_out += 1

    outs = _pallas(body, name="adamw_small", in_specs=[VMEM_SPEC] * len(args), out_specs=[VMEM_SPEC] * len(out_shape),
                   out_shape=out_shape, params=pltpu.CompilerParams(vmem_limit_bytes=VMEM_LIMIT))(*args)
    result, i = [], 0
    for t in range(n):
        k = 4 if full[t] else 1
        result.append(list(outs[i:i + k]))
        i += k
    return result


def _adamw_plain(name, gs, ws, ms, vs):
    n = len(gs)

    def body(*refs):
        for t in range(n):
            g_ref, w_ref, m_ref, v_ref = refs[4 * t:4 * t + 4]
            outs = refs[4 * n + 3 * t:4 * n + 3 * t + 3]
            outs[0][...], outs[1][...], outs[2][...] = _adamw_math(g_ref[...], w_ref[...], m_ref[...], v_ref[...])

    args, out_shape = [], []
    for g, w, m, v in zip(gs, ws, ms, vs):
        args += [g, w, m, v]
        out_shape += [jax.ShapeDtypeStruct(w.shape, F32)] * 3
    outs = _pallas(body, name=name, in_specs=[VMEM_SPEC] * len(args), out_specs=[VMEM_SPEC] * len(out_shape),
                   out_shape=out_shape, params=pltpu.CompilerParams(vmem_limit_bytes=VMEM_LIMIT))(*args)
    return [list(outs[3 * t:3 * t + 3]) for t in range(n)]


KIND = {"sc_w_in": "cols", "sc_w_out": "blocked", "w_dkv": "blocked", "w_kr": "blocked", "w_uk": "cols", "w_uv": "cols",
        "w_dq": "blocked", "w_uq": "blocked", "w_o": "blocked", "ffn_w_up": "blocked", "ffn_w_down": "blocked",
        "conv": "blocked"}
GATHER_GROUPS = (("mixer", ("sc_w_in", "sc_w_out", "conv")),
                 ("up0", ("ffn_w_up0",)),
                 ("down0", ("ffn_w_down0",)),
                 ("attn", ("w_dkv", "w_kr", "w_uk", "w_uv", "w_dq", "w_uq", "w_o")),
                 ("ffn1", ("ffn_w_up1", "ffn_w_down1")))
SCATTER_GROUPS = (("ffn1", (("ffn_w_up", 1), ("ffn_w_down", 1))),
                  ("attn", (("w_o", None), ("w_uq", None), ("w_dq", None), ("w_uk", None), ("w_uv", None),
                            ("w_dkv", None), ("w_kr", None))),
                  ("ffn0", (("ffn_w_up", 0), ("ffn_w_down", 0))),
                  ("mixer", (("sc_w_out", None), ("sc_w_in", None))))
SCHEDULE = {
    "begin": (("gather_start", "mixer"),),
    "mixer_ready": (("gather_start", "up0"),),
    "l0_out": (("gather_forward", "up0"), ("gather_start", "down0")),
    "f0_up": (("gather_forward", "down0"), ("gather_start", "attn")),
    "f0_down": (("gather_forward", "attn"), ("gather_start", "ffn1")),
    "attn_fwd": (("gather_forward", "ffn1"),),
    "f1_gup": (("scatter_sibling", "ffn1"),),
    "f1_dhf": (("scatter_chips", "ffn1"),),
    "kv_bwd": (("scatter_sibling", "attn"), ("scatter_done", "ffn1")),
    "f0_dact": (("scatter_chips", "attn"),),
    "f0_gup": (("scatter_sibling", "ffn0"),),
    "f0_dhf": (("scatter_chips", "ffn0"),),
    "f0_bwd": (("scatter_done", "attn"),),
    "sc_bwd": (("scatter_sibling", "mixer"),),
    "d_l0_in": (("scatter_chips", "mixer"),),
}
FINISH = (("scatter_done", "ffn0"), ("scatter_done", "mixer"))
STAGES = {"gather_start": 1, "gather_forward": 2, "gather_done": 3,
          "scatter_sibling": 1, "scatter_chips": 2, "scatter_done": 3}
SMALL_W_ROWS = 24


def _pack(arrays, rows):
    flat = jnp.concatenate([a.reshape(-1).astype(F32) for a in arrays])
    return jnp.pad(flat, (0, rows * 128 - flat.shape[0])).reshape(rows, 128)


def _stored(name, a):
    return jnp.swapaxes(a, -1, -2) if name == "ffn_w_up" else a


def _base(name):
    if name.startswith("ffn_w_") and name[-1] in "01":
        return name[:-1], int(name[-1])
    return name, None


class _Exchange:
    def __init__(self, wts, mom, var, ffn_conv_b):
        self.wts, self.mom, self.var = wts, mom, var
        x, y, c = _place()
        self.me = 4 * x + 2 * y + c
        self.c_arr = jnp.reshape(c, (1,)).astype(jnp.int32)
        chip = 2 * x + y
        self.chip_ids = jnp.stack([chip, chip ^ 1, chip ^ 2, chip ^ 3]).astype(jnp.int32)
        self.ready = {"ffn_cb0": ffn_conv_b.reshape(2, N_FF_BLK, 1, FF_BLK)[0],
                      "ffn_cb1": ffn_conv_b.reshape(2, N_FF_BLK, 1, FF_BLK)[1]}
        self.gathers, self.group_of = {}, {}
        self.grads, self.scatters, self.results = {}, {}, {}
        for gname, names in GATHER_GROUPS:
            self.gathers[gname] = dict(stage=0, names=names, kinds=[KIND[_base(nm)[0]] for nm in names])
            for nm in names:
                self.group_of[nm] = gname
        for nm in ("sc_conv_w", "ffn_cw0", "ffn_cw1"):
            self.group_of[nm] = "mixer"
        self.at("begin", None)

    def _shard(self, name):
        if name == "conv":
            return _pack([self.wts["sc_conv_w"], self.wts["ffn_conv_w"]], SMALL_W_ROWS).reshape(1, SMALL_W_ROWS, 128)
        base, layer = _base(name)
        a = _stored(base, self.wts[base])
        if layer is not None:
            a = a[layer:layer + 1]
        if KIND[base] == "cols":
            return a.reshape(a.shape[-2], a.shape[-1]).astype(BF16)
        return a.reshape((-1,) + a.shape[-2:]).astype(BF16)

    def _gather_to(self, gname, stage, after):
        st = self.gathers[gname]
        if st["stage"] < 1 <= stage:
            shards = [self._shard(nm) for nm in st["names"]]
            lands = [_landing(s, kind, self.me) for s, kind in zip(shards, st["kinds"])]
            plan, ncopy = _plan_gather_chips(st["kinds"])
            st["flight"] = _copies_start(f"ag_{gname}_chips", shards, lands, ncopy, plan)
            st["stage"] = 1
        if st["stage"] < 2 <= stage:
            plan, ncopy = _plan_gather_chips(st["kinds"])
            _, lands = _copies_wait(f"ag_{gname}_chips_wait", st["flight"], ncopy, plan)
            plan, ncopy = _plan_gather_sibling(st["kinds"])
            st["flight"] = _copies_start(f"ag_{gname}_sibling", [], lands, ncopy, plan)
            st["stage"] = 2
        if st["stage"] < 3 <= stage:
            plan, ncopy = _plan_gather_sibling(st["kinds"])
            _, lands = _copies_wait(f"ag_{gname}_sibling_wait", st["flight"], ncopy, plan)
            for nm, land in zip(st["names"], lands):
                self._arrived(nm, land)
            st["stage"] = 3

    def _arrived(self, name, land):
        if name == "conv":
            conv = land.reshape(N_DEV, SMALL_W_ROWS * 128)
            self.ready["sc_conv_w"] = conv[:, :3 * 128].reshape(N_DEV, 3, 128).transpose(1, 0, 2).reshape(3, D)
            fcw = conv[:, 3 * 128:3 * 128 + 6 * 352].reshape(N_DEV, 2, 3, 352).transpose(1, 2, 0, 3)
            fcw = fcw.reshape(2, 3, N_FF_BLK, FF_BLK).transpose(0, 2, 1, 3)
            self.ready["ffn_cw0"], self.ready["ffn_cw1"] = fcw[0], fcw[1]
        elif name in ("sc_w_in", "w_uk", "w_uv") or name.startswith("ffn_w_up"):
            self.ready[name] = land
        elif name.startswith("ffn_w_down"):
            self.ready[name] = land.reshape(1, N_FF_BLK, FF_BLK, D)
        elif name == "w_kr":
            self.ready[name] = jnp.pad(land.reshape(D, QK_ROPE), ((0, 0), (0, 128 - QK_ROPE)))
        elif name == "w_uq":
            self.ready[name] = jnp.pad(land.reshape(N_HEADS, Q_LORA, QK_NOPE + QK_ROPE),
                                       ((0, 0), (0, 0), (0, QK_PAD - QK_NOPE - QK_ROPE)))
        else:
            self.ready[name] = land.reshape(D, land.shape[-1])

    def need(self, name, after):
        if name not in self.ready:
            self._gather_to(self.group_of[name], 3, after)
        return self.ready[name]

    def grad(self, name, layer, array):
        self.grads[(name, layer)] = array

    def _scatter_to(self, gname, stage, after):
        keys = dict(SCATTER_GROUPS)[gname]
        st = self.scatters.setdefault(gname, dict(stage=0))
        kinds = [KIND[nm] for nm, _ in keys]
        if st["stage"] < 1 <= stage:
            grads = [self.grads[key] for key in keys]
            lands = []
            for gr, kind in zip(grads, kinds):
                shard = (gr.shape[0],) + gr.shape[2:] if kind == "blocked" else (gr.shape[0], gr.shape[1] // N_DEV)
                lands.append(lax.empty((N_CHIP,) + shard, BF16))
            plan, ncopy = _plan_scatter_sibling(kinds)
            st["flight"] = _copies_start(f"rs_{gname}_sibling", grads, lands, ncopy, plan)
            st["stage"] = 1
        if st["stage"] < 2 <= stage:
            plan, ncopy = _plan_scatter_sibling(kinds)
            grads, recvs = _copies_wait(f"rs_{gname}_sibling_wait", st["flight"], ncopy, plan)
            sums = _chip_sums(f"rs_{gname}_sums", grads, kinds, recvs, self.c_arr)
            lands = [lax.empty(s.shape, BF16) for s in sums]
            plan, ncopy = _plan_scatter_chips(len(sums))
            st["flight"] = _copies_start(f"rs_{gname}_chips", sums, lands, ncopy, plan)
            st["stage"] = 2
        if st["stage"] < 3 <= stage:
            plan, ncopy = _plan_scatter_chips(len(keys))
            sums, recvs = _copies_wait(f"rs_{gname}_chips_wait", st["flight"], ncopy, plan)
            items = []
            for (nm, layer), own, rv in zip(keys, sums, recvs):
                nl = 1 if layer is None else 2
                rows, w = own.shape[1], own.shape[2]
                w3, m3, v3 = (_stored(nm, src[nm]).reshape(nl, rows, w) for src in (self.wts, self.mom, self.var))
                items.append((own, rv, w3, m3, v3, 0 if layer is None else layer, self.results.get(nm)))
            outs = _adamw_group(f"adamw_{gname}", items, self.chip_ids)
            for (nm, _), out in zip(keys, outs):
                self.results[nm] = out
            st["stage"] = 3

    def at(self, place, after):
        for action, gname in SCHEDULE.get(place, ()):
            self._advance(action, gname, after)

    def _advance(self, action, gname, after):
        if action.startswith("gather"):
            self._gather_to(gname, STAGES[action], after)
        else:
            self._scatter_to(gname, STAGES[action], after)

    def finish(self, after):
        for action, gname in FINISH:
            self._advance(action, gname, after)
        for gname, _ in SCATTER_GROUPS:
            self._scatter_to(gname, 3, after)
        return {nm: [_stored(nm, o.reshape(_stored(nm, self.wts[nm]).shape)) for o in outs]
                for nm, outs in self.results.items()}


REPLICATED = ("attn_norm", "ffn_norm", "final_norm", "kv_in_norm", "kv_latent_norm", "q_latent_norm", "ffn_conv_b")
WEIGHTS = ("attn_norm", "ffn_norm", "final_norm", "sc_w_in", "sc_conv_w", "sc_w_out", "kv_in_norm", "w_dkv",
           "kv_latent_norm", "w_kr", "w_uk", "w_uv", "w_dq", "q_latent_norm", "w_uq", "w_o", "ffn_w_up", "ffn_conv_w",
           "ffn_conv_b", "ffn_w_down")


def kernel(x, positions, attn_norm, ffn_norm, final_norm, sc_w_in, sc_conv_w, sc_w_out, kv_in_norm, w_dkv, kv_latent_norm, w_kr, w_uk, w_uv, w_dq, q_latent_norm, w_uq, w_o, ffn_w_up, ffn_conv_w, ffn_conv_b, ffn_w_down, loss_target, m_attn_norm, m_ffn_norm, m_final_norm, m_sc_w_in, m_sc_conv_w, m_sc_w_out, m_kv_in_norm, m_w_dkv, m_kv_latent_norm, m_w_kr, m_w_uk, m_w_uv, m_w_dq, m_q_latent_norm, m_w_uq, m_w_o, m_ffn_w_up, m_ffn_conv_w, m_ffn_conv_b, m_ffn_w_down, v_attn_norm, v_ffn_norm, v_final_norm, v_sc_w_in, v_sc_conv_w, v_sc_w_out, v_kv_in_norm, v_w_dkv, v_kv_latent_norm, v_w_kr, v_w_uk, v_w_uv, v_w_dq, v_q_latent_norm, v_w_uq, v_w_o, v_ffn_w_up, v_ffn_conv_w, v_ffn_conv_b, v_ffn_w_down):
    wts = dict(attn_norm=attn_norm, ffn_norm=ffn_norm, final_norm=final_norm, sc_w_in=sc_w_in, sc_conv_w=sc_conv_w,
               sc_w_out=sc_w_out, kv_in_norm=kv_in_norm, w_dkv=w_dkv, kv_latent_norm=kv_latent_norm, w_kr=w_kr,
               w_uk=w_uk, w_uv=w_uv, w_dq=w_dq, q_latent_norm=q_latent_norm, w_uq=w_uq, w_o=w_o, ffn_w_up=ffn_w_up,
               ffn_conv_w=ffn_conv_w, ffn_conv_b=ffn_conv_b, ffn_w_down=ffn_w_down)
    mom = dict(attn_norm=m_attn_norm, ffn_norm=m_ffn_norm, final_norm=m_final_norm, sc_w_in=m_sc_w_in,
               sc_conv_w=m_sc_conv_w, sc_w_out=m_sc_w_out, kv_in_norm=m_kv_in_norm, w_dkv=m_w_dkv,
               kv_latent_norm=m_kv_latent_norm, w_kr=m_w_kr, w_uk=m_w_uk, w_uv=m_w_uv, w_dq=m_w_dq,
               q_latent_norm=m_q_latent_norm, w_uq=m_w_uq, w_o=m_w_o, ffn_w_up=m_ffn_w_up, ffn_conv_w=m_ffn_conv_w,
               ffn_conv_b=m_ffn_conv_b, ffn_w_down=m_ffn_w_down)
    var = dict(attn_norm=v_attn_norm, ffn_norm=v_ffn_norm, final_norm=v_final_norm, sc_w_in=v_sc_w_in,
               sc_conv_w=v_sc_conv_w, sc_w_out=v_sc_w_out, kv_in_norm=v_kv_in_norm, w_dkv=v_w_dkv,
               kv_latent_norm=v_kv_latent_norm, w_kr=v_w_kr, w_uk=v_w_uk, w_uv=v_w_uv, w_dq=v_w_dq,
               q_latent_norm=v_q_latent_norm, w_uq=v_w_uq, w_o=v_w_o, ffn_w_up=v_ffn_w_up, ffn_conv_w=v_ffn_conv_w,
               ffn_conv_b=v_ffn_conv_b, ffn_w_down=v_ffn_w_down)
    xi, yi, ci = _place()
    me = 4 * xi + 2 * yi + ci
    _Chain.last = None

    ex = _Exchange(wts, mom, var, ffn_conv_b)
    rep = {
        "attn_norm": attn_norm, "ffn_norm": ffn_norm, "final_norm": final_norm,
        "kv_in_norm": kv_in_norm.reshape(1, D), "kv_latent_norm": kv_latent_norm.reshape(1, KV_LORA),
        "q_latent_norm": q_latent_norm.reshape(1, Q_LORA),
    }
    loss, grad_x, small = _local_step(x.reshape(T, D), positions.reshape(T, 1), loss_target.reshape(T, D), rep, ex)
    results = ex.finish(grad_x)

    def rows_of(a):
        return a.reshape(-1, a.shape[-1])

    small_order = list(REPLICATED) + ["sc_conv_w", "ffn_conv_w"]
    shards = [loss.reshape(1, 1, 128)] + [rows_of(small[nm])[None] for nm in small_order]
    gathered = _all_gather("ag_small_grads", [(s, "blocked") for s in shards])
    params = [[None] + [rows_of(src[nm]) for nm in REPLICATED] + [None, None] for src in (wts, mom, var)]
    summed = _adamw_small(gathered, *params)
    loss_total = summed[0][0][0, 0]
    for nm, vals in zip(REPLICATED, summed[1:1 + len(REPLICATED)]):
        results[nm] = [a.reshape(wts[nm].shape) for a in vals]
    g_scw = lax.dynamic_slice(summed[-2][0], (0, me * 128), (3, 128))
    g_fcw = lax.dynamic_slice(summed[-1][0], (0, me * 352), (6, 352))
    conv = _adamw_plain("adamw_conv", [g_scw, g_fcw], *[[rows_of(src["sc_conv_w"]), rows_of(src["ffn_conv_w"])]
                                                        for src in (wts, mom, var)])
    for nm, g_own, vals in zip(("sc_conv_w", "ffn_conv_w"), (g_scw, g_fcw), conv):
        results[nm] = [a.reshape(wts[nm].shape) for a in [g_own] + vals]

    outs = [loss_total, grad_x.reshape(1, T, D)]
    for slot in range(4):
        outs.extend(results[nm][slot] for nm in WEIGHTS)
    return tuple(outs)
```

```python
import jax
import jax.numpy as jnp
from jax import lax
from jax.experimental import pallas as pl
from jax.experimental.pallas import tpu as pltpu

F32 = jnp.float32
BF16 = jnp.bfloat16

T = 2048
D = 1024
N_HEADS = 8
QK_NOPE = 128
QK_ROPE = 64
V_HEAD = 128
Q_LORA = 384
KV_LORA = 256
D_FF = 2816
CHUNK = 64
ROPE_THETA = 10000.0
EPS = 1e-6
NEG_INF = -1e30
ADAM_LR = 0.001
ADAM_B1 = 0.9
ADAM_B2 = 0.999
ADAM_EPS = 1e-08
ADAM_WD = 0.01
ADAM_STEP = 10

N_DEV = 8
N_CHIP = 4
FF_BLK = D_FF * 2 // N_DEV
N_FF_BLK = D_FF // FF_BLK
QK_PAD = 256
HALO = 16

TM = 1024
TS = 512
TR = 256
TQ = 512
VMEM_LIMIT = 56 * 1024 * 1024

NN = (((1,), (0,)), ((), ()))
NT = (((1,), (1,)), ((), ()))
TN = (((0,), (0,)), ((), ()))
MESH = pl.DeviceIdType.MESH


def _params(sem):
    return pltpu.CompilerParams(dimension_semantics=sem, vmem_limit_bytes=VMEM_LIMIT)


ANY_SPEC = pl.BlockSpec(memory_space=pl.ANY)
VMEM_SPEC = pl.BlockSpec(memory_space=pltpu.VMEM)


class _Chain:
    last = None


def _pallas(body, *, name, in_specs, out_specs, out_shape, grid=(), scratch_shapes=(), n_prefetch=0, aliases=None,
            params=None):
    def run(*args):
        after = _Chain.last
        n_lead = len(args)
        specs, operands, fn = list(in_specs), list(args), body
        if after is not None:
            def fn(*refs):
                return body(*refs[:n_lead], *refs[n_lead + 1:])
            specs.append(ANY_SPEC)
            operands.append(after)
        kw = dict(name=name, out_shape=out_shape, input_output_aliases=aliases or {})
        if params is not None:
            kw["compiler_params"] = params
        if n_prefetch:
            kw["grid_spec"] = pltpu.PrefetchScalarGridSpec(
                num_scalar_prefetch=n_prefetch, grid=grid, in_specs=specs, out_specs=out_specs,
                scratch_shapes=scratch_shapes)
        else:
            kw.update(grid=grid, in_specs=specs, out_specs=out_specs, scratch_shapes=scratch_shapes)
        outs = pl.pallas_call(fn, **kw)(*operands)
        _Chain.last = outs[0] if isinstance(outs, (list, tuple)) else outs
        return outs
    return run


def _mm(name, a, b, *, grid, a_spec, b_spec, o_spec, o_shape, o_dtype, dims, k_axis=None, acc_shape=None,
        add=None, add_spec=None):
    nk = grid[k_axis] if k_axis is not None else 1
    has_add = add is not None

    def body(*refs):
        a_ref, b_ref = refs[0], refs[1]
        p = 2
        add_ref = None
        if has_add:
            add_ref = refs[p]
            p += 1
        o_ref = refs[p]
        p += 1
        r = lax.dot_general(a_ref[...].astype(BF16), b_ref[...].astype(BF16), dims, preferred_element_type=F32)
        if k_axis is None:
            if has_add:
                r = r + add_ref[...].astype(F32)
            o_ref[...] = r.astype(o_dtype)
        else:
            acc = refs[p]
            k = pl.program_id(k_axis)

            @pl.when(k == 0)
            def _():
                acc[...] = r

            @pl.when(k > 0)
            def _():
                acc[...] += r

            @pl.when(k == nk - 1)
            def _():
                t = acc[...]
                if has_add:
                    t = t + add_ref[...].astype(F32)
                o_ref[...] = t.astype(o_dtype)

    in_specs = [a_spec, b_spec]
    args = [a, b]
    if has_add:
        in_specs.append(add_spec if add_spec is not None else o_spec)
        args.append(add)
    sem = tuple("arbitrary" if ax == k_axis else "parallel" for ax in range(len(grid)))
    scratch = [pltpu.VMEM(acc_shape, F32)] if k_axis is not None else []
    return _pallas(body, name=name, grid=grid, in_specs=in_specs, out_specs=o_spec,
                   out_shape=jax.ShapeDtypeStruct(o_shape, o_dtype), scratch_shapes=scratch, params=_params(sem))(*args)


def _mm_sum(name, parts, *, grid, o_spec, o_shape, o_dtype, add=None, norm_bwd=None):
    has_add = add is not None
    np_ = len(parts)
    nn = 1 if norm_bwd is None else len(norm_bwd[1])

    def body(*refs):
        accs = [None] * nn
        for p, (_, _, _, _, dims, n) in enumerate(parts):
            a_ref, b_ref = refs[2 * p], refs[2 * p + 1]
            for k in range(a_ref.shape[0]):
                r = lax.dot_general(a_ref[k], b_ref[k], dims, preferred_element_type=F32)
                accs[n] = r if accs[n] is None else accs[n] + r
        if norm_bwd is None:
            acc = accs[0]
            if has_add:
                acc = acc + refs[2 * np_][...]
            refs[-1][...] = acc.astype(o_dtype)
            return
        x_ref, g_refs, res_ref = refs[2 * np_], refs[2 * np_ + 1:2 * np_ + 1 + nn], refs[2 * np_ + 1 + nn]
        dx_ref, dxb_ref, dg_refs = refs[-2 - nn], refs[-1 - nn], refs[-nn:]
        xv = x_ref[...]
        r = lax.rsqrt(jnp.mean(xv * xv, axis=-1, keepdims=True) + EPS)
        xn = xv * r
        dx = res_ref[...]
        sums = []
        for acc, g_ref in zip(accs, g_refs):
            gdy = acc * g_ref[...]
            dx = dx + r * (gdy - xn * jnp.mean(gdy * xn, axis=-1, keepdims=True))
            sums.append(jnp.sum(acc * xn, axis=0, keepdims=True))
        dx_ref[...] = dx
        dxb_ref[...] = dx.astype(BF16)

        @pl.when(pl.program_id(0) == 0)
        def _():
            for dg_ref, part in zip(dg_refs, sums):
                dg_ref[...] = part

        @pl.when(pl.program_id(0) > 0)
        def _():
            for dg_ref, part in zip(dg_refs, sums):
                dg_ref[...] += part

    in_specs, args = [], []
    for a, a_spec, b, b_spec, _, _ in parts:
        in_specs += [a_spec, b_spec]
        args += [a, b]
    if norm_bwd is None:
        if has_add:
            in_specs.append(o_spec)
            args.append(add)
        return _pallas(body, name=name, grid=grid, in_specs=in_specs, out_specs=o_spec,
                       out_shape=jax.ShapeDtypeStruct(o_shape, o_dtype),
                       params=_params(("parallel",) * len(grid)))(*args)
    x, gains, dres = norm_bwd
    vec = pl.BlockSpec((1, o_shape[1]), lambda i: (0, 0))
    in_specs += [o_spec] + [vec] * nn + [o_spec]
    args += [x] + list(gains) + [dres]
    outs = _pallas(body, name=name, grid=grid, in_specs=in_specs, out_specs=[o_spec, o_spec] + [vec] * nn,
                   out_shape=[jax.ShapeDtypeStruct(o_shape, F32), jax.ShapeDtypeStruct(o_shape, BF16)]
                   + [jax.ShapeDtypeStruct((1, o_shape[1]), F32)] * nn,
                   params=_params(("arbitrary",)))(*args)
    return outs[0], outs[1], list(outs[2:])


def _mm_rows(name, a, b, dims, o_dtype, n_out, *, tn=None, add=None):
    k = a.shape[1]
    tn = n_out if tn is None else tn
    if dims == NN:
        b_spec = pl.BlockSpec((k, tn), lambda n, i: (0, n))
    else:
        b_spec = pl.BlockSpec((tn, k), lambda n, i: (n, 0))
    return _mm(name, a, b, grid=(n_out // tn, T // TM),
               a_spec=pl.BlockSpec((TM, k), lambda n, i: (i, 0)), b_spec=b_spec,
               o_spec=pl.BlockSpec((TM, tn), lambda n, i: (i, n)), o_shape=(T, n_out), o_dtype=o_dtype,
               dims=dims, add=add)


def _mm_wgrad(name, a, b, *, tn=512):
    k, n = a.shape[1], b.shape[1]
    tn = min(tn, n)
    return _mm(name, a, b, grid=(n // tn,),
               a_spec=pl.BlockSpec((T, k), lambda j: (0, 0)), b_spec=pl.BlockSpec((T, tn), lambda j: (0, j)),
               o_spec=pl.BlockSpec((k, tn), lambda j: (0, j)), o_shape=(k, n), o_dtype=BF16, dims=TN)


def _rms_fwd(name, x, g):
    d = x.shape[1]

    def body(x_ref, g_ref, o_ref):
        xv = x_ref[...]
        r = lax.rsqrt(jnp.mean(xv * xv, axis=-1, keepdims=True) + EPS)
        o_ref[...] = ((xv * r) * g_ref[...]).astype(BF16)

    return _pallas(
        body, name=name, grid=(T // TM,),
        in_specs=[pl.BlockSpec((TM, d), lambda i: (i, 0)), pl.BlockSpec((1, d), lambda i: (0, 0))],
        out_specs=pl.BlockSpec((TM, d), lambda i: (i, 0)),
        out_shape=jax.ShapeDtypeStruct((T, d), BF16), params=_params(("parallel",)))(x, g)


def _rms_bwd(name, x, gains, dys, dres=None):
    d = x.shape[1]
    n = len(gains)
    has_res = dres is not None

    def body(*refs):
        x_ref, g_refs, dy_refs = refs[0], refs[1:1 + n], refs[1 + n:1 + 2 * n]
        dx_ref, dxb_ref = refs[-2 - n], refs[-1 - n]
        dg_refs = refs[-n:]
        xv = x_ref[...]
        r = lax.rsqrt(jnp.mean(xv * xv, axis=-1, keepdims=True) + EPS)
        xn = xv * r
        dx = refs[1 + 2 * n][...] if has_res else None
        parts = []
        for g_ref, dy_ref in zip(g_refs, dy_refs):
            dyv = dy_ref[...].astype(F32)
            gdy = dyv * g_ref[...]
            t = r * (gdy - xn * jnp.mean(gdy * xn, axis=-1, keepdims=True))
            dx = t if dx is None else dx + t
            parts.append(jnp.sum(dyv * xn, axis=0, keepdims=True))
        dx_ref[...] = dx
        dxb_ref[...] = dx.astype(BF16)

        @pl.when(pl.program_id(0) == 0)
        def _():
            for dg_ref, part in zip(dg_refs, parts):
                dg_ref[...] = part

        @pl.when(pl.program_id(0) > 0)
        def _():
            for dg_ref, part in zip(dg_refs, parts):
                dg_ref[...] += part

    row = pl.BlockSpec((TR, d), lambda i: (i, 0))
    vec = pl.BlockSpec((1, d), lambda i: (0, 0))
    args = [x] + list(gains) + list(dys) + ([dres] if has_res else [])
    in_specs = [row] + [vec] * n + [row] * n + ([row] if has_res else [])
    outs = _pallas(
        body, name=name, grid=(T // TR,), in_specs=in_specs, out_specs=[row, row] + [vec] * n,
        out_shape=[jax.ShapeDtypeStruct((T, d), F32), jax.ShapeDtypeStruct((T, d), BF16)]
        + [jax.ShapeDtypeStruct((1, d), F32)] * n,
        params=_params(("arbitrary",)))(*args)
    return outs[0], outs[1], list(outs[2:])


def _final(h, g, tgt):
    def body(h_ref, g_ref, t_ref, loss_ref, dh_ref, dhb_ref, dg_ref):
        hv = h_ref[...]
        r = lax.rsqrt(jnp.mean(hv * hv, axis=-1, keepdims=True) + EPS)
        xn = hv * r
        gv = g_ref[...]
        err = xn * gv - t_ref[...]
        part_loss = 0.5 * jnp.sum(jnp.mean(err * err, axis=-1, keepdims=True), axis=0, keepdims=True)
        dy = err * (1.0 / D)
        gdy = dy * gv
        dh = r * (gdy - xn * jnp.mean(gdy * xn, axis=-1, keepdims=True))
        dh_ref[...] = dh
        dhb_ref[...] = dh.astype(BF16)
        part = jnp.sum(dy * xn, axis=0, keepdims=True)
        first = pl.program_id(0) == 0

        @pl.when(first)
        def _():
            dg_ref[...] = part
            loss_ref[...] = jnp.broadcast_to(part_loss, (1, 128))

        @pl.when(jnp.logical_not(first))
        def _():
            dg_ref[...] += part
            loss_ref[...] += jnp.broadcast_to(part_loss, (1, 128))

    row = pl.BlockSpec((TR, D), lambda i: (i, 0))
    vec = pl.BlockSpec((1, D), lambda i: (0, 0))
    return _pallas(
        body, name="final_loss", grid=(T // TR,), in_specs=[row, vec, row],
        out_specs=[pl.BlockSpec((1, 128), lambda i: (0, 0)), row, row, vec],
        out_shape=[jax.ShapeDtypeStruct((1, 128), F32), jax.ShapeDtypeStruct((T, D), F32),
                   jax.ShapeDtypeStruct((T, D), BF16), jax.ShapeDtypeStruct((1, D), F32)],
        params=_params(("arbitrary",)))(h, g, tgt)


def _prev_idx(i, rows=TR):
    return jnp.maximum(i * (rows // HALO) - 1, 0)


def _next_idx(i, rows=TR):
    return jnp.minimum((i + 1) * (rows // HALO), T // HALO - 1)


def _causal_taps(ext):
    return pltpu.roll(ext, 2, 0)[HALO:], pltpu.roll(ext, 1, 0)[HALO:], ext[HALO:]


def _anticausal_taps(ext, n):
    rows = ext.shape[0]
    return pltpu.roll(ext, rows - 1, 0)[:n], pltpu.roll(ext, rows - 2, 0)[:n]


def _sc_fwd(z, w):
    def body(b_ref, c_ref, ch_ref, u_ref, uh_ref, w_ref, y_ref):
        i = pl.program_id(0)
        cu = c_ref[...].astype(F32) * u_ref[...].astype(F32)
        cuh = ch_ref[...].astype(F32) * uh_ref[...].astype(F32)
        cuh = jnp.where(i > 0, cuh, 0.0)
        x2, x1, x0 = _causal_taps(jnp.concatenate([cuh, cu], axis=0))
        wv = w_ref[...]
        cv = (x2 * wv[0:1] + x1 * wv[1:2]) + x0 * wv[2:3]
        y_ref[...] = (b_ref[...].astype(F32) * cv).astype(BF16)

    def main(part):
        return pl.BlockSpec((TR, D), lambda i: (i, part))

    def halo(part):
        return pl.BlockSpec((HALO, D), lambda i: (_prev_idx(i), part))

    return _pallas(
        body, name="sc_fwd", grid=(T // TR,),
        in_specs=[main(0), main(1), halo(1), main(2), halo(2), pl.BlockSpec((3, D), lambda i: (0, 0))],
        out_specs=pl.BlockSpec((TR, D), lambda i: (i, 0)),
        out_shape=jax.ShapeDtypeStruct((T, D), BF16), params=_params(("parallel",)))(z, z, z, z, z, w)


def _sc_bwd(z, dy, w):
    last = T // TR - 1

    def body(b_ref, bn_ref, c_ref, ch_ref, u_ref, uh_ref, dy_ref, dyn_ref, w_ref, dz_ref, dw_ref):
        i = pl.program_id(0)
        cv_ = c_ref[...].astype(F32)
        uv = u_ref[...].astype(F32)
        cu = cv_ * uv
        cuh = jnp.where(i > 0, ch_ref[...].astype(F32) * uh_ref[...].astype(F32), 0.0)
        x2, x1, x0 = _causal_taps(jnp.concatenate([cuh, cu], axis=0))
        wv = w_ref[...]
        conv = (x2 * wv[0:1] + x1 * wv[1:2]) + x0 * wv[2:3]
        dyv = dy_ref[...]
        dz_ref[:, 0:D] = (dyv * conv).astype(BF16)
        dconv = dyv * b_ref[...].astype(F32)
        dconv_n = jnp.where(i < last, dyn_ref[...] * bn_ref[...].astype(F32), 0.0)
        n1, n2 = _anticausal_taps(jnp.concatenate([dconv, dconv_n], axis=0), TR)
        dcu = (dconv * wv[2:3] + n1 * wv[1:2]) + n2 * wv[0:1]
        dz_ref[:, D:2 * D] = (dcu * uv).astype(BF16)
        dz_ref[:, 2 * D:3 * D] = (dcu * cv_).astype(BF16)
        part = jnp.concatenate([jnp.sum(dconv * x2, axis=0, keepdims=True),
                                jnp.sum(dconv * x1, axis=0, keepdims=True),
                                jnp.sum(dconv * x0, axis=0, keepdims=True)], axis=0)

        @pl.when(i == 0)
        def _():
            dw_ref[...] = part

        @pl.when(i > 0)
        def _():
            dw_ref[...] += part

    def main(part):
        return pl.BlockSpec((TR, D), lambda i: (i, part))

    def prev(part):
        return pl.BlockSpec((HALO, D), lambda i: (_prev_idx(i), part))

    def nxt(part):
        return pl.BlockSpec((HALO, D), lambda i: (_next_idx(i), part))

    wspec = pl.BlockSpec((3, D), lambda i: (0, 0))
    return _pallas(
        body, name="sc_bwd", grid=(T // TR,),
        in_specs=[main(0), nxt(0), main(1), prev(1), main(2), prev(2), main(0), nxt(0), wspec],
        out_specs=[pl.BlockSpec((TR, 3 * D), lambda i: (i, 0)), wspec],
        out_shape=[jax.ShapeDtypeStruct((T, 3 * D), BF16), jax.ShapeDtypeStruct((3, D), F32)],
        params=_params(("arbitrary",)))(z, z, z, z, z, z, dy, dy, w)


def _sigmoid(x):
    return 1.0 / (1.0 + jnp.exp(-x))


def _ffn_up_act(name, hf, w_up, w, b):
    def body(h_ref, hh_ref, wg_ref, wv_ref, w_ref, b_ref, g_ref, v_ref, a_ref):
        i = pl.program_id(1)
        wg = wg_ref[...]
        g = lax.dot_general(h_ref[...], wg, NT, preferred_element_type=F32).astype(BF16)
        gh = lax.dot_general(hh_ref[...], wg, NT, preferred_element_type=F32).astype(BF16)
        v = lax.dot_general(h_ref[...], wv_ref[...], NT, preferred_element_type=F32).astype(BF16)
        g_ref[...] = g
        v_ref[...] = v
        gh = jnp.where(i > 0, gh.astype(F32), 0.0)
        x2, x1, x0 = _causal_taps(jnp.concatenate([gh, g.astype(F32)], axis=0))
        wv = w_ref[...]
        gc = ((x2 * wv[0:1] + x1 * wv[1:2]) + x0 * wv[2:3]) + b_ref[...]
        a_ref[...] = ((gc * _sigmoid(gc)) * v.astype(F32)).astype(BF16)

    blk = pl.BlockSpec((None, TS, FF_BLK), lambda j, i: (j, i, 0))
    out = jax.ShapeDtypeStruct((N_FF_BLK, T, FF_BLK), BF16)
    return _pallas(
        body, name=name, grid=(N_FF_BLK, T // TS),
        in_specs=[pl.BlockSpec((TS, D), lambda j, i: (i, 0)),
                  pl.BlockSpec((HALO, D), lambda j, i: (_prev_idx(i, TS), 0)),
                  pl.BlockSpec((None, None, FF_BLK, D), lambda j, i: (0, j, 0, 0)),
                  pl.BlockSpec((None, None, FF_BLK, D), lambda j, i: (0, j + N_FF_BLK, 0, 0)),
                  pl.BlockSpec((None, 3, FF_BLK), lambda j, i: (j, 0, 0)),
                  pl.BlockSpec((None, 1, FF_BLK), lambda j, i: (j, 0, 0))],
        out_specs=[blk, blk, blk], out_shape=[out, out, out],
        params=_params(("parallel", "parallel")))(hf, hf, w_up, w_up, w, b)


def _ffn_dact(name, dh, w_down4, g, v, w, b):
    last = T // TS - 1

    def body(dh_ref, dhn_ref, wd_ref, g_ref, gp_ref, gn_ref, v_ref, vn_ref, w_ref, b_ref, dg_ref, dv_ref, dw_ref, db_ref):
        i = pl.program_id(1)
        wd = wd_ref[...]
        da = lax.dot_general(dh_ref[...], wd, NT, preferred_element_type=F32)
        dan = lax.dot_general(dhn_ref[...], wd, NT, preferred_element_type=F32)
        da = jnp.concatenate([da, jnp.where(i < last, dan, 0.0)], axis=0)
        gp = jnp.where(i > 0, gp_ref[...].astype(F32), 0.0)
        ext = jnp.concatenate([gp, g_ref[...].astype(F32), gn_ref[...].astype(F32)], axis=0)
        x2, x1, x0 = _causal_taps(ext)
        wv = w_ref[...]
        gc = ((x2 * wv[0:1] + x1 * wv[1:2]) + x0 * wv[2:3]) + b_ref[...]
        sg = _sigmoid(gc)
        vv = jnp.concatenate([v_ref[...].astype(F32), vn_ref[...].astype(F32)], axis=0)
        dv_ref[...] = (da[:TS] * (gc[:TS] * sg[:TS])).astype(BF16)
        dgc = (da * vv) * (sg * (1.0 + gc * (1.0 - sg)))
        n1, n2 = _anticausal_taps(dgc, TS)
        d0 = dgc[:TS]
        dg_ref[...] = ((d0 * wv[2:3] + n1 * wv[1:2]) + n2 * wv[0:1]).astype(BF16)
        part_w = jnp.concatenate([jnp.sum(d0 * x2[:TS], axis=0, keepdims=True),
                                  jnp.sum(d0 * x1[:TS], axis=0, keepdims=True),
                                  jnp.sum(d0 * x0[:TS], axis=0, keepdims=True)], axis=0)
        part_b = jnp.sum(d0, axis=0, keepdims=True)

        @pl.when(i == 0)
        def _():
            dw_ref[...] = part_w
            db_ref[...] = part_b

        @pl.when(i > 0)
        def _():
            dw_ref[...] += part_w
            db_ref[...] += part_b

    blk = pl.BlockSpec((None, TS, FF_BLK), lambda j, i: (j, i, 0))
    prev = pl.BlockSpec((None, HALO, FF_BLK), lambda j, i: (j, _prev_idx(i, TS), 0))
    nxt = pl.BlockSpec((None, HALO, FF_BLK), lambda j, i: (j, _next_idx(i, TS), 0))
    wspec = pl.BlockSpec((None, 3, FF_BLK), lambda j, i: (j, 0, 0))
    bspec = pl.BlockSpec((None, 1, FF_BLK), lambda j, i: (j, 0, 0))
    return _pallas(
        body, name=name, grid=(N_FF_BLK, T // TS),
        in_specs=[pl.BlockSpec((TS, D), lambda j, i: (i, 0)),
                  pl.BlockSpec((HALO, D), lambda j, i: (_next_idx(i, TS), 0)),
                  pl.BlockSpec((None, None, FF_BLK, D), lambda j, i: (0, j, 0, 0)),
                  blk, prev, nxt, blk, nxt, wspec, bspec],
        out_specs=[blk, blk, wspec, bspec],
        out_shape=[jax.ShapeDtypeStruct((N_FF_BLK, T, FF_BLK), BF16), jax.ShapeDtypeStruct((N_FF_BLK, T, FF_BLK), BF16),
                   jax.ShapeDtypeStruct((N_FF_BLK, 3, FF_BLK), F32), jax.ShapeDtypeStruct((N_FF_BLK, 1, FF_BLK), F32)],
        params=_params(("parallel", "arbitrary")))(dh, dh, w_down4, g, g, g, v, v, w, b)


def _rope_tables(pos, inv_freq):
    half = QK_ROPE // 2

    def body(p_ref, f_ref, c_ref, sa_ref, sb_ref):
        ang = p_ref[...].astype(F32) * f_ref[...]
        lane = lax.broadcasted_iota(jnp.int32, (T, 128), 1)
        c = jnp.cos(ang)
        s = jnp.sin(ang)
        c_ref[...] = jnp.where(lane < 2 * half, c, 0.0)
        sa_ref[...] = jnp.where(lane < half, -s, 0.0)
        sb_ref[...] = jnp.where(jnp.logical_and(lane >= half, lane < 2 * half), s, 0.0)

    return _pallas(
        body, name="rope_tables", in_specs=[VMEM_SPEC] * 2, out_specs=[VMEM_SPEC] * 3,
        out_shape=[jax.ShapeDtypeStruct((T, 128), F32)] * 3,
        params=pltpu.CompilerParams(vmem_limit_bytes=VMEM_LIMIT))(pos, inv_freq)


def _rotate(r, c, sa, sb, sign):
    return r * c + sign * (pltpu.roll(r, 96, 1) * sa + pltpu.roll(r, 32, 1) * sb)


def _q_up(cq, w_uq, tables):
    cos, sa, sb = tables

    def body(a_ref, b_ref, c_ref, sa_ref, sb_ref, o_ref):
        r = lax.dot_general(a_ref[...], b_ref[...], NN, preferred_element_type=F32)
        o_ref[:, :QK_NOPE] = r[:, :QK_NOPE].astype(BF16)
        o_ref[:, QK_NOPE:] = _rotate(r[:, QK_NOPE:], c_ref[...], sa_ref[...], sb_ref[...], 1.0).astype(BF16)

    tab = pl.BlockSpec((TM, 128), lambda h, i: (i, 0))
    return _pallas(
        body, name="q_up", grid=(N_HEADS, T // TM),
        in_specs=[pl.BlockSpec((TM, Q_LORA), lambda h, i: (i, 0)),
                  pl.BlockSpec((None, Q_LORA, QK_PAD), lambda h, i: (h, 0, 0)), tab, tab, tab],
        out_specs=pl.BlockSpec((None, TM, QK_PAD), lambda h, i: (h, i, 0)),
        out_shape=jax.ShapeDtypeStruct((N_HEADS, T, QK_PAD), BF16),
        params=_params(("parallel", "parallel")))(cq, w_uq, cos, sa, sb)


def _rope(name, x, tables, sign, out_dtype, reduce_groups=False):
    g, _, w = x.shape
    cos, sa, sb = tables

    def body(x_ref, c_ref, sa_ref, sb_ref, o_ref):
        xv = x_ref[...].astype(F32)
        if reduce_groups:
            acc = xv[0]
            for k in range(1, g):
                acc = acc + xv[k]
            xv = acc
        out = _rotate(xv[:, w - 128:], c_ref[...], sa_ref[...], sb_ref[...], sign)
        if w > 128:
            o_ref[:, :w - 128] = xv[:, :w - 128].astype(out_dtype)
        o_ref[:, w - 128:] = out.astype(out_dtype)

    tab = pl.BlockSpec((TM, 128), lambda h, i: (i, 0))
    if reduce_groups:
        x_spec = pl.BlockSpec((g, TM, w), lambda h, i: (0, i, 0))
        groups = 1
    else:
        x_spec = pl.BlockSpec((None, TM, w), lambda h, i: (h, i, 0))
        groups = g
    return _pallas(
        body, name=name, grid=(groups, T // TM), in_specs=[x_spec, tab, tab, tab],
        out_specs=pl.BlockSpec((None, TM, w), lambda h, i: (h, i, 0)),
        out_shape=jax.ShapeDtypeStruct((groups, T, w), out_dtype),
        params=_params(("parallel", "parallel")))(x, cos, sa, sb)


SCALE = (QK_NOPE + QK_ROPE) ** -0.5
LOG2E = 1.4426950408889634
SCALE2 = SCALE * LOG2E


def _diag_mask(transposed):
    shift = CHUNK.bit_length() - 1
    a = lax.broadcasted_iota(jnp.int32, (TQ, TQ), 0) >> shift
    b = lax.broadcasted_iota(jnp.int32, (TQ, TQ), 1) >> shift
    return (a <= b) if transposed else (b <= a)


def _as_row(col):
    return jnp.transpose(jnp.broadcast_to(col, (col.shape[0], 128)), (1, 0))[0:1]


def _keys(kn_ref, kr_ref, off):
    return jnp.concatenate([kn_ref[pl.ds(off, TQ), :], kr_ref[pl.ds(off, TQ), :]], axis=1)


def _attn_fwd(q, kn, kr, v):
    def body(q_ref, kn_ref, kr_ref, v_ref, o_ref, lse_ref):
        i = pl.program_id(1)
        qv = q_ref[...]

        def step(j, carry, masked):
            m, l, acc = carry
            off = pl.multiple_of(j * TQ, TQ)
            s = lax.dot_general(qv, _keys(kn_ref, kr_ref, off), NT, preferred_element_type=F32) * SCALE2
            if masked:
                s = jnp.where(_diag_mask(False), s, NEG_INF)
            m_new = jnp.maximum(m, jnp.max(s, axis=-1, keepdims=True))
            p = jnp.exp2(s - m_new)
            alpha = jnp.exp2(m - m_new)
            l = alpha * l + jnp.sum(p, axis=-1, keepdims=True)
            acc = alpha * acc + lax.dot_general(p.astype(BF16), v_ref[pl.ds(off, TQ), :], NN, preferred_element_type=F32)
            return m_new, l, acc

        init = (jnp.full((TQ, 1), NEG_INF, F32), jnp.zeros((TQ, 1), F32), jnp.zeros((TQ, V_HEAD), F32))
        carry = lax.fori_loop(0, i, lambda j, cr: step(j, cr, False), init)
        m, l, acc = step(i, carry, True)
        o_ref[...] = (acc / l).astype(BF16)
        lse_ref[...] = _as_row(m + jnp.log(l) * LOG2E)

    return _pallas(
        body, name="attn_fwd", grid=(N_HEADS, T // TQ),
        in_specs=[pl.BlockSpec((None, TQ, QK_PAD), lambda h, i: (h, i, 0)),
                  pl.BlockSpec((T, QK_NOPE), lambda h, i: (0, h)),
                  pl.BlockSpec((T, 128), lambda h, i: (0, 0)),
                  pl.BlockSpec((T, V_HEAD), lambda h, i: (0, h))],
        out_specs=[pl.BlockSpec((TQ, V_HEAD), lambda h, i: (i, h)), pl.BlockSpec((None, 1, TQ), lambda h, i: (h, 0, i))],
        out_shape=[jax.ShapeDtypeStruct((T, N_HEADS * V_HEAD), BF16), jax.ShapeDtypeStruct((N_HEADS, 1, T), F32)],
        params=_params(("parallel", "parallel")))(q, kn, kr, v)


def _attn_bwd(q, kn, kr, v, o, do, lse_row, tables):
    nq = T // TQ
    cos, sa, sb = tables

    def body(q_ref, kn_ref, kr_ref, v_ref, o_ref, do_ref, lse_ref, c_ref, sa_ref, sb_ref,
             dq_ref, dkn_ref, dkr_ref, dv_ref, dq_acc, dl_ref):
        j = pl.program_id(1)

        @pl.when(j == 0)
        def _():
            dq_acc[...] = jnp.zeros_like(dq_acc)
            for i in range(nq):
                rows = pl.ds(i * TQ, TQ)
                prod = do_ref[rows, :].astype(F32) * o_ref[rows, :].astype(F32)
                dl_ref[:, rows] = _as_row(jnp.sum(prod, axis=-1, keepdims=True))

        kk = jnp.concatenate([kn_ref[...], kr_ref[...]], axis=1)
        vv = v_ref[...]

        def step(i, carry, masked):
            dk, dv = carry
            off = pl.multiple_of(i * TQ, TQ)
            qi = q_ref[pl.ds(off, TQ), :]
            doi = do_ref[pl.ds(off, TQ), :]
            st = lax.dot_general(kk, qi, NT, preferred_element_type=F32) * SCALE2
            if masked:
                st = jnp.where(_diag_mask(True), st, NEG_INF)
            pt = jnp.exp2(st - lse_ref[:, pl.ds(off, TQ)])
            dv = dv + lax.dot_general(pt.astype(BF16), doi, NN, preferred_element_type=F32)
            dpt = lax.dot_general(vv, doi, NT, preferred_element_type=F32)
            dst = ((pt * (dpt - dl_ref[:, pl.ds(off, TQ)])) * SCALE).astype(BF16)
            dk = dk + lax.dot_general(dst, qi, NN, preferred_element_type=F32)
            dq_acc[pl.ds(off, TQ), :] += lax.dot_general(dst, kk, TN, preferred_element_type=F32)
            return dk, dv

        carry = step(j, (jnp.zeros((TQ, QK_PAD), F32), jnp.zeros((TQ, V_HEAD), F32)), True)
        dk, dv = lax.fori_loop(j + 1, nq, lambda i, cr: step(i, cr, False), carry)
        dkn_ref[...] = dk[:, :QK_NOPE].astype(BF16)
        dkr_ref[...] = dk[:, QK_NOPE:]
        dv_ref[...] = dv.astype(BF16)

        @pl.when(j == nq - 1)
        def _():
            dq = dq_acc[...]
            dq_ref[:, :QK_NOPE] = dq[:, :QK_NOPE].astype(BF16)
            dq_ref[:, QK_NOPE:] = _rotate(dq[:, QK_NOPE:], c_ref[...], sa_ref[...], sb_ref[...], -1.0).astype(BF16)

    row = pl.BlockSpec((None, 1, T), lambda h, j: (h, 0, 0))
    head = pl.BlockSpec((TQ, 128), lambda h, j: (j, h))
    whole = pl.BlockSpec((None, T, QK_PAD), lambda h, j: (h, 0, 0))
    tab = pl.BlockSpec((T, 128), lambda h, j: (0, 0))
    heads = pl.BlockSpec((T, V_HEAD), lambda h, j: (0, h))
    return _pallas(
        body, name="attn_bwd", grid=(N_HEADS, nq),
        in_specs=[whole, head, pl.BlockSpec((TQ, 128), lambda h, j: (j, 0)), head, heads, heads, row, tab, tab, tab],
        out_specs=[whole, head, pl.BlockSpec((None, TQ, 128), lambda h, j: (h, j, 0)), head],
        out_shape=[jax.ShapeDtypeStruct((N_HEADS, T, QK_PAD), BF16), jax.ShapeDtypeStruct((T, N_HEADS * QK_NOPE), BF16),
                   jax.ShapeDtypeStruct((N_HEADS, T, 128), F32), jax.ShapeDtypeStruct((T, N_HEADS * V_HEAD), BF16)],
        scratch_shapes=[pltpu.VMEM((T, QK_PAD), F32), pltpu.VMEM((1, T), F32)],
        params=_params(("parallel", "arbitrary")))(q, kn, kr, v, o, do, lse_row, cos, sa, sb)


def _ffn_gup(name, dg, dv, hf):
    def body(dg_ref, dv_ref, hf_ref, o_ref):
        j = pl.program_id(0)

        @pl.when(j < N_FF_BLK)
        def _():
            o_ref[...] = lax.dot_general(dg_ref[...], hf_ref[...], TN, preferred_element_type=F32).astype(BF16)

        @pl.when(j >= N_FF_BLK)
        def _():
            o_ref[...] = lax.dot_general(dv_ref[...], hf_ref[...], TN, preferred_element_type=F32).astype(BF16)

    return _pallas(
        body, name=name, grid=(N_DEV,),
        in_specs=[pl.BlockSpec((None, T, FF_BLK), lambda j: (jnp.minimum(j, N_FF_BLK - 1), 0, 0)),
                  pl.BlockSpec((None, T, FF_BLK), lambda j: (jnp.maximum(j - N_FF_BLK, 0), 0, 0)),
                  pl.BlockSpec((T, D), lambda j: (0, 0))],
        out_specs=pl.BlockSpec((None, FF_BLK, D), lambda j: (j, 0, 0)),
        out_shape=jax.ShapeDtypeStruct((N_DEV, FF_BLK, D), BF16), params=_params(("parallel",)))(dg, dv, hf)


def _ffn_layer_fwd(tag, h, gain, ex):
    hf = _rms_fwd(f"{tag}_norm", h, gain)
    g, v, act = _ffn_up_act(f"{tag}_up", hf, ex.need(f"ffn_w_up{tag[1]}", hf), ex.need(f"ffn_cw{tag[1]}", hf),
                            ex.need(f"ffn_cb{tag[1]}", hf))
    ex.at(f"{tag}_up", act)
    rows = pl.BlockSpec((TS, D), lambda i: (i, 0))
    out = _mm_sum(f"{tag}_down",
                  [(act, pl.BlockSpec((N_FF_BLK, TS, FF_BLK), lambda i: (0, i, 0)), ex.need(f"ffn_w_down{tag[1]}", act),
                    pl.BlockSpec((None, N_FF_BLK, FF_BLK, D), lambda i: (0, 0, 0, 0)), NN, 0)],
                  grid=(T // TS,), o_spec=rows, o_shape=(T, D), o_dtype=F32, add=h)
    ex.at(f"{tag}_down", out)
    return out, (hf, g, v, act)


def _ffn_layer_bwd(tag, h, gain, ex, saved, dh, dh_bf):
    hf, g, v, act = saved
    layer = tag[1]
    w_up, w_down4 = ex.need(f"ffn_w_up{layer}", dh_bf), ex.need(f"ffn_w_down{layer}", dh_bf)
    dg, dv, dcw, dcb = _ffn_dact(f"{tag}_dact", dh_bf, w_down4, g, v, ex.need(f"ffn_cw{layer}", dh_bf),
                                 ex.need(f"ffn_cb{layer}", dh_bf))
    ex.at(f"{tag}_dact", dg)
    tn = 512
    g_down = _mm(f"{tag}_gdown", act, dh_bf, grid=(N_FF_BLK, D // tn),
                 a_spec=pl.BlockSpec((None, T, FF_BLK), lambda j, n: (j, 0, 0)),
                 b_spec=pl.BlockSpec((T, tn), lambda j, n: (0, n)),
                 o_spec=pl.BlockSpec((FF_BLK, tn), lambda j, n: (j, n)),
                 o_shape=(D_FF, D), o_dtype=BF16, dims=TN)
    g_up = _ffn_gup(f"{tag}_gup", dg, dv, hf)
    ex.grad("ffn_w_up", int(layer), g_up.reshape(1, N_DEV, FF_BLK, D))
    ex.grad("ffn_w_down", int(layer), g_down.reshape(1, N_DEV, D_FF // N_DEV, D))
    ex.at(f"{tag}_gup", g_up)
    part = pl.BlockSpec((N_FF_BLK, TR, FF_BLK), lambda i: (0, i, 0))
    dh_in, dh_in_bf, dgain = _mm_sum(
        f"{tag}_dhf",
        [(dg, part, w_up, pl.BlockSpec((None, N_FF_BLK, FF_BLK, D), lambda i: (0, 0, 0, 0)), NN, 0),
         (dv, part, w_up, pl.BlockSpec((None, N_FF_BLK, FF_BLK, D), lambda i: (0, 1, 0, 0)), NN, 0)],
        grid=(T // TR,), o_spec=pl.BlockSpec((TR, D), lambda i: (i, 0)), o_shape=(T, D), o_dtype=F32,
        norm_bwd=(h, [gain], dh))
    ex.at(f"{tag}_dhf", dh_in)
    return dh_in, dh_in_bf, dgain[0], dcw, dcb


def _local_step(x, pos, tgt, rep, ex):
    attn_norm, ffn_norm, final_norm = rep["attn_norm"], rep["ffn_norm"], rep["final_norm"]
    half = QK_ROPE // 2
    inv = 1.0 / (ROPE_THETA ** (jnp.arange(half, dtype=F32) / half))
    inv_freq = jnp.concatenate([inv, inv, jnp.zeros((128 - 2 * half,), F32)]).reshape(1, 128)
    tables = _rope_tables(pos, inv_freq)

    hn0 = _rms_fwd("l0_norm", x, attn_norm[0:1])
    w_in = ex.need("sc_w_in", hn0)
    ex.at("mixer_ready", hn0)
    z = _mm_rows("l0_in", hn0, w_in, NN, BF16, 3 * D, tn=512)
    ex.at("l0_in", z)
    y = _sc_fwd(z, ex.need("sc_conv_w", z))
    h1 = _mm_rows("l0_out", y, ex.need("sc_w_out", y), NN, F32, D, tn=512, add=x)
    ex.at("l0_out", h1)
    h2, ffn0 = _ffn_layer_fwd("f0", h1, ffn_norm[0:1], ex)

    hk = _rms_fwd("kv_norm", h2, rep["kv_in_norm"])
    ckv_raw = _mm_rows("kv_down", hk, ex.need("w_dkv", hk), NN, F32, KV_LORA)
    kr_raw = _mm_rows("kv_rope", hk, ex.need("w_kr", hk), NN, F32, 128)
    ckv = _rms_fwd("kv_lnorm", ckv_raw, rep["kv_latent_norm"])
    kn = _mm_rows("kv_uk", ckv, ex.need("w_uk", ckv), NN, BF16, N_HEADS * QK_NOPE)
    vv = _mm_rows("kv_uv", ckv, ex.need("w_uv", ckv), NN, BF16, N_HEADS * V_HEAD)
    kr = _rope("k_rope", kr_raw.reshape(1, T, 128), tables, 1.0, BF16).reshape(T, 128)

    hn1 = _rms_fwd("l1_norm", h2, attn_norm[1:2])
    cq_raw = _mm_rows("q_down", hn1, ex.need("w_dq", hn1), NN, F32, Q_LORA)
    cq = _rms_fwd("q_lnorm", cq_raw, rep["q_latent_norm"])
    w_uq = ex.need("w_uq", cq)
    q = _q_up(cq, w_uq, tables)
    o, lse = _attn_fwd(q, kn, kr, vv)
    ex.at("attn_fwd", o)
    w_o = ex.need("w_o", o)
    h3 = _mm_rows("attn_out", o, w_o, NN, F32, D, tn=512, add=h2)
    h4, ffn1 = _ffn_layer_fwd("f1", h3, ffn_norm[1:2], ex)

    loss, dh4, dh4_bf, d_final = _final(h4, final_norm.reshape(1, D), tgt)

    dh3, dh3_bf, d_fn1, dcw1, dcb1 = _ffn_layer_bwd("f1", h3, ffn_norm[1:2], ex, ffn1, dh4, dh4_bf)
    ex.at("f1_bwd", dh3)

    do = _mm_rows("d_attn_out", dh3_bf, w_o, NT, BF16, N_HEADS * V_HEAD)
    ex.grad("w_o", None, _mm_wgrad("g_w_o", o, dh3_bf).reshape(1, N_DEV, D // N_DEV, D))
    dq_pre, dkn, dkr, dvv = _attn_bwd(q, kn, kr, vv, o, do, lse, tables)
    dcq = _mm("d_q_up", dq_pre, w_uq, grid=(T // TM, N_HEADS),
              a_spec=pl.BlockSpec((None, TM, QK_PAD), lambda i, h: (h, i, 0)),
              b_spec=pl.BlockSpec((None, Q_LORA, QK_PAD), lambda i, h: (h, 0, 0)),
              o_spec=pl.BlockSpec((TM, Q_LORA), lambda i, h: (i, 0)), o_shape=(T, Q_LORA), o_dtype=F32,
              dims=NT, k_axis=1, acc_shape=(TM, Q_LORA))
    g_uq = _mm("g_w_uq", cq, dq_pre, grid=(N_HEADS,),
               a_spec=pl.BlockSpec((T, Q_LORA), lambda h: (0, 0)),
               b_spec=pl.BlockSpec((None, T, QK_PAD), lambda h: (h, 0, 0)),
               o_spec=pl.BlockSpec((None, Q_LORA, QK_PAD), lambda h: (h, 0, 0)),
               o_shape=(N_HEADS, Q_LORA, QK_PAD), o_dtype=BF16, dims=TN)
    ex.grad("w_uq", None, g_uq[:, :, :QK_NOPE + QK_ROPE].reshape(1, N_DEV, Q_LORA, QK_NOPE + QK_ROPE))
    _, dcq_raw_bf, (d_qln,) = _rms_bwd("d_q_lnorm", cq_raw, [rep["q_latent_norm"]], [dcq])
    ex.grad("w_dq", None, _mm_wgrad("g_w_dq", hn1, dcq_raw_bf).reshape(1, N_DEV, D // N_DEV, Q_LORA))

    dckv = _mm_rows("d_kv_uk", dkn, ex.need("w_uk", dkn), NT, F32, KV_LORA)
    dckv = _mm_rows("d_kv_uv", dvv, ex.need("w_uv", dvv), NT, F32, KV_LORA, add=dckv)
    ex.grad("w_uk", None, _mm_wgrad("g_w_uk", ckv, dkn))
    ex.grad("w_uv", None, _mm_wgrad("g_w_uv", ckv, dvv))
    _, dckv_raw_bf, (d_kvln,) = _rms_bwd("d_kv_lnorm", ckv_raw, [rep["kv_latent_norm"]], [dckv])
    dkr_raw_bf = _rope("dk_rope", dkr, tables, -1.0, BF16, reduce_groups=True).reshape(T, 128)
    ex.grad("w_dkv", None, _mm_wgrad("g_w_dkv", hk, dckv_raw_bf).reshape(1, N_DEV, D // N_DEV, KV_LORA))
    ex.grad("w_kr", None, _mm_wgrad("g_w_kr", hk, dkr_raw_bf)[:, :QK_ROPE].reshape(1, N_DEV, D // N_DEV, QK_ROPE))

    def rows_of(a):
        return a[None], pl.BlockSpec((1, TS, a.shape[1]), lambda i: (0, i, 0))

    def whole(wt):
        return wt[None], pl.BlockSpec((1,) + wt.shape, lambda i: (0, 0, 0))

    dh2, dh2_bf, (d_an1, d_kvin) = _mm_sum(
        "d_h2", [(*rows_of(dcq_raw_bf), *whole(ex.need("w_dq", dcq_raw_bf)), NT, 0),
                 (*rows_of(dckv_raw_bf), *whole(ex.need("w_dkv", dckv_raw_bf)), NT, 1),
                 (*rows_of(dkr_raw_bf), *whole(ex.need("w_kr", dkr_raw_bf)), NT, 1)],
        grid=(T // TS,), o_spec=pl.BlockSpec((TS, D), lambda i: (i, 0)), o_shape=(T, D), o_dtype=F32,
        norm_bwd=(h2, [attn_norm[1:2], rep["kv_in_norm"]], dh3))
    ex.at("kv_bwd", dh2)

    dh1, dh1_bf, d_fn0, dcw0, dcb0 = _ffn_layer_bwd("f0", h1, ffn_norm[0:1], ex, ffn0, dh2, dh2_bf)
    ex.at("f0_bwd", dh1)

    dy = _mm_rows("d_l0_out", dh1_bf, ex.need("sc_w_out", dh1_bf), NT, F32, D)
    ex.grad("sc_w_out", None, _mm_wgrad("g_sc_w_out", y, dh1_bf).reshape(1, N_DEV, D // N_DEV, D))
    dz, d_scw = _sc_bwd(z, dy, ex.need("sc_conv_w", dy))
    g_in = _mm_wgrad("g_sc_w_in", hn0, dz)
    ex.grad("sc_w_in", None, g_in)
    ex.at("sc_bwd", g_in)
    ex.at("d_l0_in", g_in)
    grad_x, _, (d_an0,) = _mm_sum(
        "d_l0_in", [(*rows_of(dz), *whole(ex.need("sc_w_in", dz)), NT, 0)],
        grid=(T // TS,), o_spec=pl.BlockSpec((TS, D), lambda i: (i, 0)), o_shape=(T, D), o_dtype=F32,
        norm_bwd=(x, [attn_norm[0:1]], dh1))

    small = {
        "attn_norm": jnp.concatenate([d_an0, d_an1], axis=0),
        "ffn_norm": jnp.concatenate([d_fn0, d_fn1], axis=0),
        "final_norm": d_final.reshape(D),
        "kv_in_norm": d_kvin.reshape(D),
        "kv_latent_norm": d_kvln.reshape(KV_LORA),
        "q_latent_norm": d_qln,
        "ffn_conv_b": jnp.stack([dcb0, dcb1]).transpose(0, 2, 1, 3).reshape(2, D_FF),
        "sc_conv_w": d_scw,
        "ffn_conv_w": jnp.stack([dcw0, dcw1]).transpose(0, 2, 1, 3).reshape(2, 3, D_FF),
    }
    return loss, grad_x, small


def _place():
    return lax.axis_index("x"), lax.axis_index("y"), lax.axis_index("c")


def _peers():
    x, y, c = _place()
    return (x, y, 1 - c), [(1 - x, y), (x, 1 - y), (1 - x, 1 - y)]


def _window(ref, kind, dev):
    if kind == "blocked":
        return ref.at[:, dev]
    width = ref.shape[-1] // N_DEV
    return ref.at[:, pl.ds(pl.multiple_of(dev * width, 128), width)]


def _all_gather(name, items):
    n = len(items)
    out_shapes = []
    for shard, kind in items:
        if kind == "blocked":
            shape = (shard.shape[0], N_DEV) + shard.shape[1:]
        else:
            shape = (shard.shape[0], N_DEV * shard.shape[1])
        out_shapes.append(jax.ShapeDtypeStruct(shape, shard.dtype))

    def body(*refs):
        srcs, outs = refs[:n], refs[n:2 * n]
        send_sems, recv_sems, local_sems = refs[2 * n:]
        x, y, c = _place()
        me = 4 * x + 2 * y + c
        sibling, chips = _peers()

        def num(px, py, pc):
            return 4 * px + 2 * py + pc

        def copy(t, k, dev, to, from_src):
            kind = items[t][1]
            dst = _window(outs[t], kind, dev)
            return pltpu.make_async_remote_copy(
                src_ref=srcs[t] if from_src else dst, dst_ref=dst,
                send_sem=send_sems.at[t, k], recv_sem=recv_sems.at[t, k], device_id=to, device_id_type=MESH)

        mine = [pltpu.make_async_copy(srcs[t], _window(outs[t], items[t][1], me), local_sems.at[t]) for t in range(n)]
        for cp in mine:
            cp.start()
        first = []
        for t in range(n):
            first.append(copy(t, 0, me, sibling, True))
            for j, chip in enumerate(chips):
                first.append(copy(t, 1 + j, me, (*chip, c), True))
        for cp in first:
            cp.start()
        passed = []
        for j, chip in enumerate(chips):
            for t in range(n):
                copy(t, 1 + j, num(*chip, c), (x, y, c), False).wait_recv()
                fwd = copy(t, 4 + j, num(*chip, c), sibling, False)
                fwd.start()
                passed.append(fwd)
        for t in range(n):
            copy(t, 0, num(x, y, 1 - c), (x, y, c), False).wait_recv()
            for j, chip in enumerate(chips):
                copy(t, 4 + j, num(*chip, 1 - c), (x, y, c), False).wait_recv()
        for cp in first + passed:
            cp.wait_send()
        for cp in mine:
            cp.wait()

    return _pallas(
        body, name=name, in_specs=[ANY_SPEC] * n, out_specs=[ANY_SPEC] * n, out_shape=out_shapes,
        scratch_shapes=[pltpu.SemaphoreType.DMA((n, 7)), pltpu.SemaphoreType.DMA((n, 7)), pltpu.SemaphoreType.DMA((n,))],
    )(*[s for s, _ in items])


HBM_SPEC = pl.BlockSpec(memory_space=pltpu.HBM)
SEM_SPEC = pl.BlockSpec(memory_space=pltpu.SEMAPHORE)
EFFECT = pltpu.SideEffectType.DATAFLOW_SIDE_EFFECTING
TOKEN = jax.ShapeDtypeStruct((8, 128), F32)


def _hbm(a):
    return pltpu.with_memory_space_constraint(a, pltpu.HBM)


def _copies_start(name, srcs, lands, ncopy, plan):
    ns, nl = len(srcs), len(lands)

    def body(*refs):
        send, recv, token = refs[ns + nl], refs[ns + nl + 1], refs[-1]
        copies = plan(refs[:ns], refs[ns:ns + nl])
        assert len(copies) == ncopy
        for k, (sent, dst, to, _) in enumerate(copies):
            pltpu.make_async_remote_copy(src_ref=sent, dst_ref=dst, send_sem=send.at[k], recv_sem=recv.at[k],
                                         device_id=to, device_id_type=MESH).start()
        token[...] = jnp.zeros_like(token)

    arrays = list(srcs) + list(lands)
    outs = pl.pallas_call(
        body, name=name, in_specs=[HBM_SPEC] * (ns + nl),
        out_specs=[SEM_SPEC] * 2 + [HBM_SPEC] * (ns + nl) + [VMEM_SPEC],
        out_shape=[pltpu.SemaphoreType.DMA((ncopy,))] * 2 + [pltpu.HBM(a.shape, a.dtype) for a in arrays] + [TOKEN],
        input_output_aliases={i: 2 + i for i in range(ns + nl)},
        compiler_params=pltpu.CompilerParams(has_side_effects=EFFECT))(*[_hbm(a) for a in arrays])
    _Chain.last = outs[-1]
    return outs[0], outs[1], list(outs[2:2 + ns]), list(outs[2 + ns:-1])


def _copies_wait(name, started, ncopy, plan):
    send, recv, srcs, lands = started
    ns, nl = len(srcs), len(lands)

    def body(*refs):
        send_ref, recv_ref, token = refs[ns + nl], refs[ns + nl + 1], refs[-1]
        copies = plan(refs[:ns], refs[ns:ns + nl])
        assert len(copies) == ncopy
        for k, (sent, _, to, landed) in enumerate(copies):
            cp = pltpu.make_async_remote_copy(src_ref=sent, dst_ref=landed, send_sem=send_ref.at[k],
                                              recv_sem=recv_ref.at[k], device_id=to, device_id_type=MESH)
            cp.wait_send()
            cp.wait_recv()
        token[...] = jnp.zeros_like(token)

    arrays = list(srcs) + list(lands)
    outs = pl.pallas_call(
        body, name=name, in_specs=[HBM_SPEC] * (ns + nl) + [SEM_SPEC] * 2 + [ANY_SPEC],
        out_specs=[HBM_SPEC] * (ns + nl) + [VMEM_SPEC], out_shape=[pltpu.HBM(a.shape, a.dtype) for a in arrays] + [TOKEN],
        input_output_aliases={i: i for i in range(ns + nl)},
        compiler_params=pltpu.CompilerParams(has_side_effects=EFFECT))(*arrays, send, recv, _Chain.last)
    _Chain.last = outs[-1]
    return list(outs[:ns]), list(outs[ns:-1])


def _plan_gather_chips(kinds):
    def plan(srcs, lands):
        x, y, c = _place()
        sibling, chips = _peers()
        out = []
        for t, kind in enumerate(kinds):
            mine = _window(lands[t], kind, 4 * x + 2 * y + c)
            out.append((srcs[t], mine, sibling, _window(lands[t], kind, 4 * x + 2 * y + 1 - c)))
            for px, py in chips:
                out.append((srcs[t], mine, (px, py, c), _window(lands[t], kind, 4 * px + 2 * py + c)))
        return out
    return plan, 4 * len(kinds)


def _plan_gather_sibling(kinds):
    def plan(srcs, lands):
        _, _, c = _place()
        sibling, chips = _peers()
        out = []
        for t, kind in enumerate(kinds):
            for px, py in chips:
                w = _window(lands[t], kind, 4 * px + 2 * py + c)
                out.append((w, w, sibling, _window(lands[t], kind, 4 * px + 2 * py + 1 - c)))
        return out
    return plan, 3 * len(kinds)


def _plan_scatter_sibling(kinds):
    def plan(srcs, lands):
        _, _, c = _place()
        sibling, _ = _peers()
        out = []
        for t, kind in enumerate(kinds):
            for k in range(N_CHIP):
                out.append((_window(srcs[t], kind, 2 * k + 1 - c), lands[t].at[k], sibling, lands[t].at[k]))
        return out
    return plan, N_CHIP * len(kinds)


def _plan_scatter_chips(n):
    def plan(srcs, lands):
        x, y, c = _place()
        _, chips = _peers()
        out = []
        for t in range(n):
            for px, py in chips:
                out.append((srcs[t].at[2 * px + py], lands[t].at[2 * x + y], (px, py, c), lands[t].at[2 * px + py]))
        return out
    return plan, 3 * n


def _landing(shard, kind, me):
    if kind == "blocked":
        land = lax.empty((shard.shape[0], N_DEV) + shard.shape[1:], shard.dtype)
        return lax.dynamic_update_slice(land, shard[:, None], (0, me) + (0,) * (shard.ndim - 1))
    land = lax.empty((shard.shape[0], N_DEV * shard.shape[1]), shard.dtype)
    return lax.dynamic_update_slice(land, shard, (0, me * shard.shape[1]))


def _chip_sums(name, grads, kinds, recvs, c):
    n = len(grads)
    in_specs, out_specs, out_shape, args = [], [], [], []
    for gr, kind, rv in zip(grads, kinds, recvs):
        if kind == "blocked":
            rows, w = gr.shape[2], gr.shape[3]
            in_specs.append(pl.BlockSpec((None, None, rows, w), lambda k, cref: (0, 2 * k + cref[0], 0, 0)))
        else:
            rows, w = gr.shape[0], gr.shape[1] // N_DEV
            in_specs.append(pl.BlockSpec((rows, w), lambda k, cref: (0, 2 * k + cref[0])))
        blk = pl.BlockSpec((None, rows, w), lambda k, cref: (k, 0, 0))
        in_specs.append(blk)
        out_specs.append(blk)
        out_shape.append(jax.ShapeDtypeStruct((N_CHIP, rows, w), BF16))
        args += [gr, rv.reshape(N_CHIP, rows, w)]

    def body(*refs):
        for t in range(n):
            g_ref, r_ref, o_ref = refs[1 + 2 * t], refs[2 + 2 * t], refs[1 + 2 * n + t]
            o_ref[...] = (g_ref[...].astype(F32) + r_ref[...].astype(F32)).astype(BF16)

    return _pallas(body, name=name, n_prefetch=1, grid=(N_CHIP,), in_specs=in_specs, out_specs=out_specs,
                   out_shape=out_shape, params=_params(("parallel",)))(c, *args)


def _adamw_math(g, wv, mv, vv):
    m = ADAM_B1 * mv + (1.0 - ADAM_B1) * g
    v = ADAM_B2 * vv + (1.0 - ADAM_B2) * (g * g)
    m_hat = m / (1.0 - ADAM_B1 ** ADAM_STEP)
    v_hat = v / (1.0 - ADAM_B2 ** ADAM_STEP)
    delta = -ADAM_LR * (m_hat / (jnp.sqrt(v_hat) + ADAM_EPS) + ADAM_WD * wv)
    return delta, m, v


ADAM_STEPS = 2


def _adamw_group(name, items, chip_ids):
    n = len(items)
    in_specs, out_specs, out_shape, args, prevs = [], [], [], [chip_ids], []
    for own, recv, w3, m3, v3, layer, _ in items:
        nl, rows, w = w3.shape
        tr = rows // ADAM_STEPS
        assert tr % 16 == 0, (name, rows)
        in_specs += [pl.BlockSpec((None, tr, w), lambda i, ids, slot=slot: (ids[slot], i, 0)) for slot in range(4)]
        slab = pl.BlockSpec((None, tr, w), lambda i, ids, layer=layer: (layer, i, 0))
        in_specs += [slab] * 3
        out_specs += [slab] * 4
        out_shape += [jax.ShapeDtypeStruct((nl, rows, w), F32)] * 4
        args += [own, recv, recv, recv, w3, m3, v3]
    aliases = {}
    for t, item in enumerate(items):
        if item[6] is not None:
            for k in range(4):
                aliases[len(args) + k] = 4 * t + k
            in_specs += [ANY_SPEC] * 4
            args += list(item[6])
            prevs.append(t)
    n_in = 1 + 7 * n + 4 * len(prevs)

    def body(*refs):
        for t in range(n):
            own_ref, r1_ref, r2_ref, r3_ref, w_ref, m_ref, v_ref = refs[1 + 7 * t:8 + 7 * t]
            g_ref, d_ref, nm_ref, nv_ref = refs[n_in + 4 * t:n_in + 4 * t + 4]
            g = ((own_ref[...].astype(F32) + r1_ref[...].astype(F32)) + r2_ref[...].astype(F32)) + r3_ref[...].astype(F32)
            g_ref[...] = g
            d_ref[...], nm_ref[...], nv_ref[...] = _adamw_math(g, w_ref[...], m_ref[...], v_ref[...])

    outs = _pallas(body, name=name, n_prefetch=1, grid=(ADAM_STEPS,), in_specs=in_specs, out_specs=out_specs,
                   out_shape=out_shape, aliases=aliases, params=_params(("parallel",)))(*args)
    return [list(outs[4 * t:4 * t + 4]) for t in range(n)]


def _adamw_small(gathered, ws, ms, vs):
    n = len(gathered)
    full = [w is not None for w in ws]
    args = list(gathered)
    out_shape = []
    for t in range(n):
        shape = jax.ShapeDtypeStruct(gathered[t].shape[2:], F32)
        if full[t]:
            args += [ws[t], ms[t], vs[t]]
            out_shape += [shape] * 4
        else:
            out_shape += [shape]

    def body(*refs):
        i_in, i_out = n, len(args)
        for t in range(n):
            p_ref = refs[t]
            g = p_ref[0, 0]
            for k in range(1, N_DEV):
                g = g + p_ref[0, k]
            refs[i_out][...] = g
            if full[t]:
                w_ref, m_ref, v_ref = refs[i_in:i_in + 3]
                refs[i_out + 1][...], refs[i_out + 2][...], refs[i_out + 3][...] = _adamw_math(
                    g, w_ref[...], m_ref[...], v_ref[...])
                i_in += 3
                i_out += 4
            else:
                i_out += 1

    outs = _pallas(body, name="adamw_small", in_specs=[VMEM_SPEC] * len(args), out_specs=[VMEM_SPEC] * len(out_shape),
                   out_shape=out_shape, params=pltpu.CompilerParams(vmem_limit_bytes=VMEM_LIMIT))(*args)
    result, i = [], 0
    for t in range(n):
        k = 4 if full[t] else 1
        result.append(list(outs[i:i + k]))
        i += k
    return result


def _adamw_plain(name, gs, ws, ms, vs):
    n = len(gs)

    def body(*refs):
        for t in range(n):
            g_ref, w_ref, m_ref, v_ref = refs[4 * t:4 * t + 4]
            outs = refs[4 * n + 3 * t:4 * n + 3 * t + 3]
            outs[0][...], outs[1][...], outs[2][...] = _adamw_math(g_ref[...], w_ref[...], m_ref[...], v_ref[...])

    args, out_shape = [], []
    for g, w, m, v in zip(gs, ws, ms, vs):
        args += [g, w, m, v]
        out_shape += [jax.ShapeDtypeStruct(w.shape, F32)] * 3
    outs = _pallas(body, name=name, in_specs=[VMEM_SPEC] * len(args), out_specs=[VMEM_SPEC] * len(out_shape),
                   out_shape=out_shape, params=pltpu.CompilerParams(vmem_limit_bytes=VMEM_LIMIT))(*args)
    return [list(outs[3 * t:3 * t + 3]) for t in range(n)]


KIND = {"sc_w_in": "cols", "sc_w_out": "blocked", "w_dkv": "blocked", "w_kr": "blocked", "w_uk": "cols", "w_uv": "cols",
        "w_dq": "blocked", "w_uq": "blocked", "w_o": "blocked", "ffn_w_up": "blocked", "ffn_w_down": "blocked",
        "conv": "blocked"}
GATHER_GROUPS = (("mixer", ("sc_w_in", "sc_w_out", "conv")),
                 ("up0", ("ffn_w_up0",)),
                 ("down0", ("ffn_w_down0",)),
                 ("attn", ("w_dkv", "w_kr", "w_uk", "w_uv", "w_dq", "w_uq", "w_o")),
                 ("ffn1", ("ffn_w_up1", "ffn_w_down1")))
SCATTER_GROUPS = (("ffn1", (("ffn_w_up", 1), ("ffn_w_down", 1))),
                  ("attn", (("w_o", None), ("w_uq", None), ("w_dq", None), ("w_uk", None), ("w_uv", None),
                            ("w_dkv", None), ("w_kr", None))),
                  ("ffn0", (("ffn_w_up", 0), ("ffn_w_down", 0))),
                  ("mixer", (("sc_w_out", None), ("sc_w_in", None))))
SCHEDULE = {
    "begin": (("gather_start", "mixer"),),
    "mixer_ready": (("gather_start", "up0"),),
    "l0_out": (("gather_forward", "up0"), ("gather_start", "down0")),
    "f0_up": (("gather_forward", "down0"), ("gather_start", "attn")),
    "f0_down": (("gather_forward", "attn"), ("gather_start", "ffn1")),
    "attn_fwd": (("gather_forward", "ffn1"),),
    "f1_gup": (("scatter_sibling", "ffn1"),),
    "f1_dhf": (("scatter_chips", "ffn1"),),
    "kv_bwd": (("scatter_sibling", "attn"), ("scatter_done", "ffn1")),
    "f0_dact": (("scatter_chips", "attn"),),
    "f0_gup": (("scatter_sibling", "ffn0"),),
    "f0_dhf": (("scatter_chips", "ffn0"),),
    "f0_bwd": (("scatter_done", "attn"),),
    "sc_bwd": (("scatter_sibling", "mixer"),),
    "d_l0_in": (("scatter_chips", "mixer"),),
}
FINISH = (("scatter_done", "ffn0"), ("scatter_done", "mixer"))
STAGES = {"gather_start": 1, "gather_forward": 2, "gather_done": 3,
          "scatter_sibling": 1, "scatter_chips": 2, "scatter_done": 3}
SMALL_W_ROWS = 24


def _pack(arrays, rows):
    flat = jnp.concatenate([a.reshape(-1).astype(F32) for a in arrays])
    return jnp.pad(flat, (0, rows * 128 - flat.shape[0])).reshape(rows, 128)


def _stored(name, a):
    return jnp.swapaxes(a, -1, -2) if name == "ffn_w_up" else a


def _base(name):
    if name.startswith("ffn_w_") and name[-1] in "01":
        return name[:-1], int(name[-1])
    return name, None


class _Exchange:
    def __init__(self, wts, mom, var, ffn_conv_b):
        self.wts, self.mom, self.var = wts, mom, var
        x, y, c = _place()
        self.me = 4 * x + 2 * y + c
        self.c_arr = jnp.reshape(c, (1,)).astype(jnp.int32)
        chip = 2 * x + y
        self.chip_ids = jnp.stack([chip, chip ^ 1, chip ^ 2, chip ^ 3]).astype(jnp.int32)
        self.ready = {"ffn_cb0": ffn_conv_b.reshape(2, N_FF_BLK, 1, FF_BLK)[0],
                      "ffn_cb1": ffn_conv_b.reshape(2, N_FF_BLK, 1, FF_BLK)[1]}
        self.gathers, self.group_of = {}, {}
        self.grads, self.scatters, self.results = {}, {}, {}
        for gname, names in GATHER_GROUPS:
            self.gathers[gname] = dict(stage=0, names=names, kinds=[KIND[_base(nm)[0]] for nm in names])
            for nm in names:
                self.group_of[nm] = gname
        for nm in ("sc_conv_w", "ffn_cw0", "ffn_cw1"):
            self.group_of[nm] = "mixer"
        self.at("begin", None)

    def _shard(self, name):
        if name == "conv":
            return _pack([self.wts["sc_conv_w"], self.wts["ffn_conv_w"]], SMALL_W_ROWS).reshape(1, SMALL_W_ROWS, 128)
        base, layer = _base(name)
        a = _stored(base, self.wts[base])
        if layer is not None:
            a = a[layer:layer + 1]
        if KIND[base] == "cols":
            return a.reshape(a.shape[-2], a.shape[-1]).astype(BF16)
        return a.reshape((-1,) + a.shape[-2:]).astype(BF16)

    def _gather_to(self, gname, stage, after):
        st = self.gathers[gname]
        if st["stage"] < 1 <= stage:
            shards = [self._shard(nm) for nm in st["names"]]
            lands = [_landing(s, kind, self.me) for s, kind in zip(shards, st["kinds"])]
            plan, ncopy = _plan_gather_chips(st["kinds"])
            st["flight"] = _copies_start(f"ag_{gname}_chips", shards, lands, ncopy, plan)
            st["stage"] = 1
        if st["stage"] < 2 <= stage:
            plan, ncopy = _plan_gather_chips(st["kinds"])
            _, lands = _copies_wait(f"ag_{gname}_chips_wait", st["flight"], ncopy, plan)
            plan, ncopy = _plan_gather_sibling(st["kinds"])
            st["flight"] = _copies_start(f"ag_{gname}_sibling", [], lands, ncopy, plan)
            st["stage"] = 2
        if st["stage"] < 3 <= stage:
            plan, ncopy = _plan_gather_sibling(st["kinds"])
            _, lands = _copies_wait(f"ag_{gname}_sibling_wait", st["flight"], ncopy, plan)
            for nm, land in zip(st["names"], lands):
                self._arrived(nm, land)
            st["stage"] = 3

    def _arrived(self, name, land):
        if name == "conv":
            conv = land.reshape(N_DEV, SMALL_W_ROWS * 128)
            self.ready["sc_conv_w"] = conv[:, :3 * 128].reshape(N_DEV, 3, 128).transpose(1, 0, 2).reshape(3, D)
            fcw = conv[:, 3 * 128:3 * 128 + 6 * 352].reshape(N_DEV, 2, 3, 352).transpose(1, 2, 0, 3)
            fcw = fcw.reshape(2, 3, N_FF_BLK, FF_BLK).transpose(0, 2, 1, 3)
            self.ready["ffn_cw0"], self.ready["ffn_cw1"] = fcw[0], fcw[1]
        elif name in ("sc_w_in", "w_uk", "w_uv") or name.startswith("ffn_w_up"):
            self.ready[name] = land
        elif name.startswith("ffn_w_down"):
            self.ready[name] = land.reshape(1, N_FF_BLK, FF_BLK, D)
        elif name == "w_kr":
            self.ready[name] = jnp.pad(land.reshape(D, QK_ROPE), ((0, 0), (0, 128 - QK_ROPE)))
        elif name == "w_uq":
            self.ready[name] = jnp.pad(land.reshape(N_HEADS, Q_LORA, QK_NOPE + QK_ROPE),
                                       ((0, 0), (0, 0), (0, QK_PAD - QK_NOPE - QK_ROPE)))
        else:
            self.ready[name] = land.reshape(D, land.shape[-1])

    def need(self, name, after):
        if name not in self.ready:
            self._gather_to(self.group_of[name], 3, after)
        return self.ready[name]

    def grad(self, name, layer, array):
        self.grads[(name, layer)] = array

    def _scatter_to(self, gname, stage, after):
        keys = dict(SCATTER_GROUPS)[gname]
        st = self.scatters.setdefault(gname, dict(stage=0))
        kinds = [KIND[nm] for nm, _ in keys]
        if st["stage"] < 1 <= stage:
            grads = [self.grads[key] for key in keys]
            lands = []
            for gr, kind in zip(grads, kinds):
                shard = (gr.shape[0],) + gr.shape[2:] if kind == "blocked" else (gr.shape[0], gr.shape[1] // N_DEV)
                lands.append(lax.empty((N_CHIP,) + shard, BF16))
            plan, ncopy = _plan_scatter_sibling(kinds)
            st["flight"] = _copies_start(f"rs_{gname}_sibling", grads, lands, ncopy, plan)
            st["stage"] = 1
        if st["stage"] < 2 <= stage:
            plan, ncopy = _plan_scatter_sibling(kinds)
            grads, recvs = _copies_wait(f"rs_{gname}_sibling_wait", st["flight"], ncopy, plan)
            sums = _chip_sums(f"rs_{gname}_sums", grads, kinds, recvs, self.c_arr)
            lands = [lax.empty(s.shape, BF16) for s in sums]
            plan, ncopy = _plan_scatter_chips(len(sums))
            st["flight"] = _copies_start(f"rs_{gname}_chips", sums, lands, ncopy, plan)
            st["stage"] = 2
        if st["stage"] < 3 <= stage:
            plan, ncopy = _plan_scatter_chips(len(keys))
            sums, recvs = _copies_wait(f"rs_{gname}_chips_wait", st["flight"], ncopy, plan)
            items = []
            for (nm, layer), own, rv in zip(keys, sums, recvs):
                nl = 1 if layer is None else 2
                rows, w = own.shape[1], own.shape[2]
                w3, m3, v3 = (_stored(nm, src[nm]).reshape(nl, rows, w) for src in (self.wts, self.mom, self.var))
                items.append((own, rv, w3, m3, v3, 0 if layer is None else layer, self.results.get(nm)))
            outs = _adamw_group(f"adamw_{gname}", items, self.chip_ids)
            for (nm, _), out in zip(keys, outs):
                self.results[nm] = out
            st["stage"] = 3

    def at(self, place, after):
        for action, gname in SCHEDULE.get(place, ()):
            self._advance(action, gname, after)

    def _advance(self, action, gname, after):
        if action.startswith("gather"):
            self._gather_to(gname, STAGES[action], after)
        else:
            self._scatter_to(gname, STAGES[action], after)

    def finish(self, after):
        for action, gname in FINISH:
            self._advance(action, gname, after)
        for gname, _ in SCATTER_GROUPS:
            self._scatter_to(gname, 3, after)
        return {nm: [_stored(nm, o.reshape(_stored(nm, self.wts[nm]).shape)) for o in outs]
                for nm, outs in self.results.items()}


REPLICATED = ("attn_norm", "ffn_norm", "final_norm", "kv_in_norm", "kv_latent_norm", "q_latent_norm", "ffn_conv_b")
WEIGHTS = ("attn_norm", "ffn_norm", "final_norm", "sc_w_in", "sc_conv_w", "sc_w_out", "kv_in_norm", "w_dkv",
           "kv_latent_norm", "w_kr", "w_uk", "w_uv", "w_dq", "q_latent_norm", "w_uq", "w_o", "ffn_w_up", "ffn_conv_w",
           "ffn_conv_b", "ffn_w_down")


def kernel(x, positions, attn_norm, ffn_norm, final_norm, sc_w_in, sc_conv_w, sc_w_out, kv_in_norm, w_dkv, kv_latent_norm, w_kr, w_uk, w_uv, w_dq, q_latent_norm, w_uq, w_o, ffn_w_up, ffn_conv_w, ffn_conv_b, ffn_w_down, loss_target, m_attn_norm, m_ffn_norm, m_final_norm, m_sc_w_in, m_sc_conv_w, m_sc_w_out, m_kv_in_norm, m_w_dkv, m_kv_latent_norm, m_w_kr, m_w_uk, m_w_uv, m_w_dq, m_q_latent_norm, m_w_uq, m_w_o, m_ffn_w_up, m_ffn_conv_w, m_ffn_conv_b, m_ffn_w_down, v_attn_norm, v_ffn_norm, v_final_norm, v_sc_w_in, v_sc_conv_w, v_sc_w_out, v_kv_in_norm, v_w_dkv, v_kv_latent_norm, v_w_kr, v_w_uk, v_w_uv, v_w_dq, v_q_latent_norm, v_w_uq, v_w_o, v_ffn_w_up, v_ffn_conv_w, v_ffn_conv_b, v_ffn_w_down):
    wts = dict(attn_norm=attn_norm, ffn_norm=ffn_norm, final_norm=final_norm, sc_w_in=sc_w_in, sc_conv_w=sc_conv_w,
               sc_w_out=sc_w_out, kv_in_norm=kv_in_norm, w_dkv=w_dkv, kv_latent_norm=kv_latent_norm, w_kr=w_kr,
               w_uk=w_uk, w_uv=w_uv, w_dq=w_dq, q_latent_norm=q_latent_norm, w_uq=w_uq, w_o=w_o, ffn_w_up=ffn_w_up,
               ffn_conv_w=ffn_conv_w, ffn_conv_b=ffn_conv_b, ffn_w_down=ffn_w_down)
    mom = dict(attn_norm=m_attn_norm, ffn_norm=m_ffn_norm, final_norm=m_final_norm, sc_w_in=m_sc_w_in,
               sc_conv_w=m_sc_conv_w, sc_w_out=m_sc_w_out, kv_in_norm=m_kv_in_norm, w_dkv=m_w_dkv,
               kv_latent_norm=m_kv_latent_norm, w_kr=m_w_kr, w_uk=m_w_uk, w_uv=m_w_uv, w_dq=m_w_dq,
               q_latent_norm=m_q_latent_norm, w_uq=m_w_uq, w_o=m_w_o, ffn_w_up=m_ffn_w_up, ffn_conv_w=m_ffn_conv_w,
               ffn_conv_b=m_ffn_conv_b, ffn_w_down=m_ffn_w_down)
    var = dict(attn_norm=v_attn_norm, ffn_norm=v_ffn_norm, final_norm=v_final_norm, sc_w_in=v_sc_w_in,
               sc_conv_w=v_sc_conv_w, sc_w_out=v_sc_w_out, kv_in_norm=v_kv_in_norm, w_dkv=v_w_dkv,
               kv_latent_norm=v_kv_latent_norm, w_kr=v_w_kr, w_uk=v_w_uk, w_uv=v_w_uv, w_dq=v_w_dq,
               q_latent_norm=v_q_latent_norm, w_uq=v_w_uq, w_o=v_w_o, ffn_w_up=v_ffn_w_up, ffn_conv_w=v_ffn_conv_w,
               ffn_conv_b=v_ffn_conv_b, ffn_w_down=v_ffn_w_down)
    xi, yi, ci = _place()
    me = 4 * xi + 2 * yi + ci
    _Chain.last = None

    ex = _Exchange(wts, mom, var, ffn_conv_b)
    rep = {
        "attn_norm": attn_norm, "ffn_norm": ffn_norm, "final_norm": final_norm,
        "kv_in_norm": kv_in_norm.reshape(1, D), "kv_latent_norm": kv_latent_norm.reshape(1, KV_LORA),
        "q_latent_norm": q_latent_norm.reshape(1, Q_LORA),
    }
    loss, grad_x, small = _local_step(x.reshape(T, D), positions.reshape(T, 1), loss_target.reshape(T, D), rep, ex)
    results = ex.finish(grad_x)

    def rows_of(a):
        return a.reshape(-1, a.shape[-1])

    small_order = list(REPLICATED) + ["sc_conv_w", "ffn_conv_w"]
    shards = [loss.reshape(1, 1, 128)] + [rows_of(small[nm])[None] for nm in small_order]
    gathered = _all_gather("ag_small_grads", [(s, "blocked") for s in shards])
    params = [[None] + [rows_of(src[nm]) for nm in REPLICATED] + [None, None] for src in (wts, mom, var)]
    summed = _adamw_small(gathered, *params)
    loss_total = summed[0][0][0, 0]
    for nm, vals in zip(REPLICATED, summed[1:1 + len(REPLICATED)]):
        results[nm] = [a.reshape(wts[nm].shape) for a in vals]
    g_scw = lax.dynamic_slice(summed[-2][0], (0, me * 128), (3, 128))
    g_fcw = lax.dynamic_slice(summed[-1][0], (0, me * 352), (6, 352))
    conv = _adamw_plain("adamw_conv", [g_scw, g_fcw], *[[rows_of(src["sc_conv_w"]), rows_of(src["ffn_conv_w"])]
                                                        for src in (wts, mom, var)])
    for nm, g_own, vals in zip(("sc_conv_w", "ffn_conv_w"), (g_scw, g_fcw), conv):
        results[nm] = [a.reshape(wts[nm].shape) for a in [g_own] + vals]

    outs = [loss_total, grad_x.reshape(1, T, D)]
    for slot in range(4):
        outs.extend(results[nm][slot] for nm in WEIGHTS)
    return tuple(outs)
```

```python
import jax
import jax.numpy as jnp
from jax import lax
from jax.experimental import pallas as pl
from jax.experimental.pallas import tpu as pltpu

F32 = jnp.float32
BF16 = jnp.bfloat16

T = 2048
D = 1024
N_HEADS = 8
QK_NOPE = 128
QK_ROPE = 64
V_HEAD = 128
Q_LORA = 384
KV_LORA = 256
D_FF = 2816
CHUNK = 64
ROPE_THETA = 10000.0
EPS = 1e-6
NEG_INF = -1e30
ADAM_LR = 0.001
ADAM_B1 = 0.9
ADAM_B2 = 0.999
ADAM_EPS = 1e-08
ADAM_WD = 0.01
ADAM_STEP = 10

N_DEV = 8
N_CHIP = 4
FF_BLK = D_FF * 2 // N_DEV
N_FF_BLK = D_FF // FF_BLK
QK_PAD = 256
HALO = 16

TM = 1024
TS = 512
TR = 256
TQ = 512
VMEM_LIMIT = 56 * 1024 * 1024

NN = (((1,), (0,)), ((), ()))
NT = (((1,), (1,)), ((), ()))
TN = (((0,), (0,)), ((), ()))
MESH = pl.DeviceIdType.MESH


def _params(sem):
    return pltpu.CompilerParams(dimension_semantics=sem, vmem_limit_bytes=VMEM_LIMIT)


ANY_SPEC = pl.BlockSpec(memory_space=pl.ANY)
VMEM_SPEC = pl.BlockSpec(memory_space=pltpu.VMEM)


class _Chain:
    last = None


def _pallas(body, *, name, in_specs, out_specs, out_shape, grid=(), scratch_shapes=(), n_prefetch=0, aliases=None,
            params=None):
    def run(*args):
        after = _Chain.last
        n_lead = len(args)
        specs, operands, fn = list(in_specs), list(args), body
        if after is not None:
            def fn(*refs):
                return body(*refs[:n_lead], *refs[n_lead + 1:])
            specs.append(ANY_SPEC)
            operands.append(after)
        kw = dict(name=name, out_shape=out_shape, input_output_aliases=aliases or {})
        if params is not None:
            kw["compiler_params"] = params
        if n_prefetch:
            kw["grid_spec"] = pltpu.PrefetchScalarGridSpec(
                num_scalar_prefetch=n_prefetch, grid=grid, in_specs=specs, out_specs=out_specs,
                scratch_shapes=scratch_shapes)
        else:
            kw.update(grid=grid, in_specs=specs, out_specs=out_specs, scratch_shapes=scratch_shapes)
        outs = pl.pallas_call(fn, **kw)(*operands)
        _Chain.last = outs[0] if isinstance(outs, (list, tuple)) else outs
        return outs
    return run


def _mm(name, a, b, *, grid, a_spec, b_spec, o_spec, o_shape, o_dtype, dims, k_axis=None, acc_shape=None,
        add=None, add_spec=None):
    nk = grid[k_axis] if k_axis is not None else 1
    has_add = add is not None

    def body(*refs):
        a_ref, b_ref = refs[0], refs[1]
        p = 2
        add_ref = None
        if has_add:
            add_ref = refs[p]
            p += 1
        o_ref = refs[p]
        p += 1
        r = lax.dot_general(a_ref[...].astype(BF16), b_ref[...].astype(BF16), dims, preferred_element_type=F32)
        if k_axis is None:
            if has_add:
                r = r + add_ref[...].astype(F32)
            o_ref[...] = r.astype(o_dtype)
        else:
            acc = refs[p]
            k = pl.program_id(k_axis)

            @pl.when(k == 0)
            def _():
                acc[...] = r

            @pl.when(k > 0)
            def _():
                acc[...] += r

            @pl.when(k == nk - 1)
            def _():
                t = acc[...]
                if has_add:
                    t = t + add_ref[...].astype(F32)
                o_ref[...] = t.astype(o_dtype)

    in_specs = [a_spec, b_spec]
    args = [a, b]
    if has_add:
        in_specs.append(add_spec if add_spec is not None else o_spec)
        args.append(add)
    sem = tuple("arbitrary" if ax == k_axis else "parallel" for ax in range(len(grid)))
    scratch = [pltpu.VMEM(acc_shape, F32)] if k_axis is not None else []
    return _pallas(body, name=name, grid=grid, in_specs=in_specs, out_specs=o_spec,
                   out_shape=jax.ShapeDtypeStruct(o_shape, o_dtype), scratch_shapes=scratch, params=_params(sem))(*args)


def _mm_sum(name, parts, *, grid, o_spec, o_shape, o_dtype, add=None, norm_bwd=None):
    has_add = add is not None
    np_ = len(parts)
    nn = 1 if norm_bwd is None else len(norm_bwd[1])
    has_res = norm_bwd is not None and norm_bwd[2] is not None

    def body(*refs):
        accs = [None] * nn
        for p, (_, _, _, _, dims, n) in enumerate(parts):
            a_ref, b_ref = refs[2 * p], refs[2 * p + 1]
            for k in range(a_ref.shape[0]):
                r = lax.dot_general(a_ref[k], b_ref[k], dims, preferred_element_type=F32)
                accs[n] = r if accs[n] is None else accs[n] + r
        if norm_bwd is None:
            acc = accs[0]
            if has_add:
                acc = acc + refs[2 * np_][...]
            refs[-1][...] = acc.astype(o_dtype)
            return
        x_ref, g_refs = refs[2 * np_], refs[2 * np_ + 1:2 * np_ + 1 + nn]
        dx_ref, dxb_ref, dg_refs = refs[-2 - nn], refs[-1 - nn], refs[-nn:]
        xv = x_ref[...]
        r = lax.rsqrt(jnp.mean(xv * xv, axis=-1, keepdims=True) + EPS)
        xn = xv * r
        dx = refs[2 * np_ + 1 + nn][...] if has_res else None
        sums = []
        for acc, g_ref in zip(accs, g_refs):
            gdy = acc * g_ref[...]
            t = r * (gdy - xn * jnp.mean(gdy * xn, axis=-1, keepdims=True))
            dx = t if dx is None else dx + t
            sums.append(jnp.sum(acc * xn, axis=0, keepdims=True))
        dx_ref[...] = dx
        dxb_ref[...] = dx.astype(BF16)

        @pl.when(pl.program_id(0) == 0)
        def _():
            for dg_ref, part in zip(dg_refs, sums):
                dg_ref[...] = part

        @pl.when(pl.program_id(0) > 0)
        def _():
            for dg_ref, part in zip(dg_refs, sums):
                dg_ref[...] += part

    in_specs, args = [], []
    for a, a_spec, b, b_spec, _, _ in parts:
        in_specs += [a_spec, b_spec]
        args += [a, b]
    if norm_bwd is None:
        if has_add:
            in_specs.append(o_spec)
            args.append(add)
        return _pallas(body, name=name, grid=grid, in_specs=in_specs, out_specs=o_spec,
                       out_shape=jax.ShapeDtypeStruct(o_shape, o_dtype),
                       params=_params(("parallel",) * len(grid)))(*args)
    x, gains, dres = norm_bwd
    vec = pl.BlockSpec((1, o_shape[1]), lambda i: (0, 0))
    in_specs += [o_spec] + [vec] * nn + ([o_spec] if has_res else [])
    args += [x] + list(gains) + ([dres] if has_res else [])
    outs = _pallas(body, name=name, grid=grid, in_specs=in_specs, out_specs=[o_spec, o_spec] + [vec] * nn,
                   out_shape=[jax.ShapeDtypeStruct(o_shape, F32), jax.ShapeDtypeStruct(o_shape, BF16)]
                   + [jax.ShapeDtypeStruct((1, o_shape[1]), F32)] * nn,
                   params=_params(("arbitrary",)))(*args)
    return outs[0], outs[1], list(outs[2:])


def _mm_rows(name, a, b, dims, o_dtype, n_out, *, tn=None, add=None):
    k = a.shape[1]
    tn = n_out if tn is None else tn
    if dims == NN:
        b_spec = pl.BlockSpec((k, tn), lambda n, i: (0, n))
    else:
        b_spec = pl.BlockSpec((tn, k), lambda n, i: (n, 0))
    return _mm(name, a, b, grid=(n_out // tn, T // TM),
               a_spec=pl.BlockSpec((TM, k), lambda n, i: (i, 0)), b_spec=b_spec,
               o_spec=pl.BlockSpec((TM, tn), lambda n, i: (i, n)), o_shape=(T, n_out), o_dtype=o_dtype,
               dims=dims, add=add)


def _mm_wgrad(name, a, b, *, tn=512):
    k, n = a.shape[1], b.shape[1]
    tn = min(tn, n)
    return _mm(name, a, b, grid=(n // tn,),
               a_spec=pl.BlockSpec((T, k), lambda j: (0, 0)), b_spec=pl.BlockSpec((T, tn), lambda j: (0, j)),
               o_spec=pl.BlockSpec((k, tn), lambda j: (0, j)), o_shape=(k, n), o_dtype=BF16, dims=TN)


def _rms_fwd(name, x, g):
    d = x.shape[1]

    def body(x_ref, g_ref, o_ref):
        xv = x_ref[...]
        r = lax.rsqrt(jnp.mean(xv * xv, axis=-1, keepdims=True) + EPS)
        o_ref[...] = ((xv * r) * g_ref[...]).astype(BF16)

    return _pallas(
        body, name=name, grid=(T // TM,),
        in_specs=[pl.BlockSpec((TM, d), lambda i: (i, 0)), pl.BlockSpec((1, d), lambda i: (0, 0))],
        out_specs=pl.BlockSpec((TM, d), lambda i: (i, 0)),
        out_shape=jax.ShapeDtypeStruct((T, d), BF16), params=_params(("parallel",)))(x, g)


def _rms_bwd(name, x, gains, dys, dres=None):
    d = x.shape[1]
    n = len(gains)
    has_res = dres is not None

    def body(*refs):
        x_ref, g_refs, dy_refs = refs[0], refs[1:1 + n], refs[1 + n:1 + 2 * n]
        dx_ref, dxb_ref = refs[-2 - n], refs[-1 - n]
        dg_refs = refs[-n:]
        xv = x_ref[...]
        r = lax.rsqrt(jnp.mean(xv * xv, axis=-1, keepdims=True) + EPS)
        xn = xv * r
        dx = refs[1 + 2 * n][...] if has_res else None
        parts = []
        for g_ref, dy_ref in zip(g_refs, dy_refs):
            dyv = dy_ref[...].astype(F32)
            gdy = dyv * g_ref[...]
            t = r * (gdy - xn * jnp.mean(gdy * xn, axis=-1, keepdims=True))
            dx = t if dx is None else dx + t
            parts.append(jnp.sum(dyv * xn, axis=0, keepdims=True))
        dx_ref[...] = dx
        dxb_ref[...] = dx.astype(BF16)

        @pl.when(pl.program_id(0) == 0)
        def _():
            for dg_ref, part in zip(dg_refs, parts):
                dg_ref[...] = part

        @pl.when(pl.program_id(0) > 0)
        def _():
            for dg_ref, part in zip(dg_refs, parts):
                dg_ref[...] += part

    row = pl.BlockSpec((TR, d), lambda i: (i, 0))
    vec = pl.BlockSpec((1, d), lambda i: (0, 0))
    args = [x] + list(gains) + list(dys) + ([dres] if has_res else [])
    in_specs = [row] + [vec] * n + [row] * n + ([row] if has_res else [])
    outs = _pallas(
        body, name=name, grid=(T // TR,), in_specs=in_specs, out_specs=[row, row] + [vec] * n,
        out_shape=[jax.ShapeDtypeStruct((T, d), F32), jax.ShapeDtypeStruct((T, d), BF16)]
        + [jax.ShapeDtypeStruct((1, d), F32)] * n,
        params=_params(("arbitrary",)))(*args)
    return outs[0], outs[1], list(outs[2:])


def _final(h, g, tgt):
    def body(h_ref, g_ref, t_ref, loss_ref, dh_ref, dhb_ref, dg_ref):
        hv = h_ref[...]
        r = lax.rsqrt(jnp.mean(hv * hv, axis=-1, keepdims=True) + EPS)
        xn = hv * r
        gv = g_ref[...]
        err = xn * gv - t_ref[...]
        part_loss = 0.5 * jnp.sum(jnp.mean(err * err, axis=-1, keepdims=True), axis=0, keepdims=True)
        dy = err * (1.0 / D)
        gdy = dy * gv
        dh = r * (gdy - xn * jnp.mean(gdy * xn, axis=-1, keepdims=True))
        dh_ref[...] = dh
        dhb_ref[...] = dh.astype(BF16)
        part = jnp.sum(dy * xn, axis=0, keepdims=True)
        first = pl.program_id(0) == 0

        @pl.when(first)
        def _():
            dg_ref[...] = part
            loss_ref[...] = jnp.broadcast_to(part_loss, (1, 128))

        @pl.when(jnp.logical_not(first))
        def _():
            dg_ref[...] += part
            loss_ref[...] += jnp.broadcast_to(part_loss, (1, 128))

    row = pl.BlockSpec((TR, D), lambda i: (i, 0))
    vec = pl.BlockSpec((1, D), lambda i: (0, 0))
    return _pallas(
        body, name="final_loss", grid=(T // TR,), in_specs=[row, vec, row],
        out_specs=[pl.BlockSpec((1, 128), lambda i: (0, 0)), row, row, vec],
        out_shape=[jax.ShapeDtypeStruct((1, 128), F32), jax.ShapeDtypeStruct((T, D), F32),
                   jax.ShapeDtypeStruct((T, D), BF16), jax.ShapeDtypeStruct((1, D), F32)],
        params=_params(("arbitrary",)))(h, g, tgt)


def _prev_idx(i, rows=TR):
    return jnp.maximum(i * (rows // HALO) - 1, 0)


def _next_idx(i, rows=TR):
    return jnp.minimum((i + 1) * (rows // HALO), T // HALO - 1)


def _causal_taps(ext):
    return pltpu.roll(ext, 2, 0)[HALO:], pltpu.roll(ext, 1, 0)[HALO:], ext[HALO:]


def _anticausal_taps(ext, n):
    rows = ext.shape[0]
    return pltpu.roll(ext, rows - 1, 0)[:n], pltpu.roll(ext, rows - 2, 0)[:n]


def _sc_fwd(z, w):
    def body(b_ref, c_ref, ch_ref, u_ref, uh_ref, w_ref, y_ref):
        i = pl.program_id(0)
        cu = c_ref[...].astype(F32) * u_ref[...].astype(F32)
        cuh = ch_ref[...].astype(F32) * uh_ref[...].astype(F32)
        cuh = jnp.where(i > 0, cuh, 0.0)
        x2, x1, x0 = _causal_taps(jnp.concatenate([cuh, cu], axis=0))
        wv = w_ref[...]
        cv = (x2 * wv[0:1] + x1 * wv[1:2]) + x0 * wv[2:3]
        y_ref[...] = (b_ref[...].astype(F32) * cv).astype(BF16)

    def main(part):
        return pl.BlockSpec((TR, D), lambda i: (i, part))

    def halo(part):
        return pl.BlockSpec((HALO, D), lambda i: (_prev_idx(i), part))

    return _pallas(
        body, name="sc_fwd", grid=(T // TR,),
        in_specs=[main(0), main(1), halo(1), main(2), halo(2), pl.BlockSpec((3, D), lambda i: (0, 0))],
        out_specs=pl.BlockSpec((TR, D), lambda i: (i, 0)),
        out_shape=jax.ShapeDtypeStruct((T, D), BF16), params=_params(("parallel",)))(z, z, z, z, z, w)


def _sc_bwd(z, dy, w):
    last = T // TR - 1

    def body(b_ref, bn_ref, c_ref, ch_ref, u_ref, uh_ref, dy_ref, dyn_ref, w_ref, dz_ref, dw_ref):
        i = pl.program_id(0)
        cv_ = c_ref[...].astype(F32)
        uv = u_ref[...].astype(F32)
        cu = cv_ * uv
        cuh = jnp.where(i > 0, ch_ref[...].astype(F32) * uh_ref[...].astype(F32), 0.0)
        x2, x1, x0 = _causal_taps(jnp.concatenate([cuh, cu], axis=0))
        wv = w_ref[...]
        conv = (x2 * wv[0:1] + x1 * wv[1:2]) + x0 * wv[2:3]
        dyv = dy_ref[...]
        dz_ref[:, 0:D] = (dyv * conv).astype(BF16)
        dconv = dyv * b_ref[...].astype(F32)
        dconv_n = jnp.where(i < last, dyn_ref[...] * bn_ref[...].astype(F32), 0.0)
        n1, n2 = _anticausal_taps(jnp.concatenate([dconv, dconv_n], axis=0), TR)
        dcu = (dconv * wv[2:3] + n1 * wv[1:2]) + n2 * wv[0:1]
        dz_ref[:, D:2 * D] = (dcu * uv).astype(BF16)
        dz_ref[:, 2 * D:3 * D] = (dcu * cv_).astype(BF16)
        part = jnp.concatenate([jnp.sum(dconv * x2, axis=0, keepdims=True),
                                jnp.sum(dconv * x1, axis=0, keepdims=True),
                                jnp.sum(dconv * x0, axis=0, keepdims=True)], axis=0)

        @pl.when(i == 0)
        def _():
            dw_ref[...] = part

        @pl.when(i > 0)
        def _():
            dw_ref[...] += part

    def main(part):
        return pl.BlockSpec((TR, D), lambda i: (i, part))

    def prev(part):
        return pl.BlockSpec((HALO, D), lambda i: (_prev_idx(i), part))

    def nxt(part):
        return pl.BlockSpec((HALO, D), lambda i: (_next_idx(i), part))

    wspec = pl.BlockSpec((3, D), lambda i: (0, 0))
    return _pallas(
        body, name="sc_bwd", grid=(T // TR,),
        in_specs=[main(0), nxt(0), main(1), prev(1), main(2), prev(2), main(0), nxt(0), wspec],
        out_specs=[pl.BlockSpec((TR, 3 * D), lambda i: (i, 0)), wspec],
        out_shape=[jax.ShapeDtypeStruct((T, 3 * D), BF16), jax.ShapeDtypeStruct((3, D), F32)],
        params=_params(("arbitrary",)))(z, z, z, z, z, z, dy, dy, w)


def _sigmoid(x):
    return 1.0 / (1.0 + jnp.exp(-x))


def _ffn_up_act(name, hf, w_up, w, b):
    def body(h_ref, hh_ref, wg_ref, wv_ref, w_ref, b_ref, g_ref, v_ref, a_ref):
        i = pl.program_id(1)
        wg = wg_ref[...]
        g = lax.dot_general(h_ref[...], wg, NT, preferred_element_type=F32).astype(BF16)
        gh = lax.dot_general(hh_ref[...], wg, NT, preferred_element_type=F32).astype(BF16)
        v = lax.dot_general(h_ref[...], wv_ref[...], NT, preferred_element_type=F32).astype(BF16)
        g_ref[...] = g
        v_ref[...] = v
        gh = jnp.where(i > 0, gh.astype(F32), 0.0)
        x2, x1, x0 = _causal_taps(jnp.concatenate([gh, g.astype(F32)], axis=0))
        wv = w_ref[...]
        gc = ((x2 * wv[0:1] + x1 * wv[1:2]) + x0 * wv[2:3]) + b_ref[...]
        a_ref[...] = ((gc * _sigmoid(gc)) * v.astype(F32)).astype(BF16)

    blk = pl.BlockSpec((None, TS, FF_BLK), lambda j, i: (j, i, 0))
    out = jax.ShapeDtypeStruct((N_FF_BLK, T, FF_BLK), BF16)
    return _pallas(
        body, name=name, grid=(N_FF_BLK, T // TS),
        in_specs=[pl.BlockSpec((TS, D), lambda j, i: (i, 0)),
                  pl.BlockSpec((HALO, D), lambda j, i: (_prev_idx(i, TS), 0)),
                  pl.BlockSpec((None, None, FF_BLK, D), lambda j, i: (0, j, 0, 0)),
                  pl.BlockSpec((None, None, FF_BLK, D), lambda j, i: (0, j + N_FF_BLK, 0, 0)),
                  pl.BlockSpec((None, 3, FF_BLK), lambda j, i: (j, 0, 0)),
                  pl.BlockSpec((None, 1, FF_BLK), lambda j, i: (j, 0, 0))],
        out_specs=[blk, blk, blk], out_shape=[out, out, out],
        params=_params(("parallel", "parallel")))(hf, hf, w_up, w_up, w, b)


def _ffn_dact(name, dh, w_down4, g, v, w, b):
    last = T // TS - 1

    def body(dh_ref, dhn_ref, wd_ref, g_ref, gp_ref, gn_ref, v_ref, vn_ref, w_ref, b_ref, dg_ref, dv_ref, dw_ref, db_ref):
        i = pl.program_id(1)
        wd = wd_ref[...]
        da = lax.dot_general(dh_ref[...], wd, NT, preferred_element_type=F32)
        dan = lax.dot_general(dhn_ref[...], wd, NT, preferred_element_type=F32)
        da = jnp.concatenate([da, jnp.where(i < last, dan, 0.0)], axis=0)
        gp = jnp.where(i > 0, gp_ref[...].astype(F32), 0.0)
        ext = jnp.concatenate([gp, g_ref[...].astype(F32), gn_ref[...].astype(F32)], axis=0)
        x2, x1, x0 = _causal_taps(ext)
        wv = w_ref[...]
        gc = ((x2 * wv[0:1] + x1 * wv[1:2]) + x0 * wv[2:3]) + b_ref[...]
        sg = _sigmoid(gc)
        vv = jnp.concatenate([v_ref[...].astype(F32), vn_ref[...].astype(F32)], axis=0)
        dv_ref[...] = (da[:TS] * (gc[:TS] * sg[:TS])).astype(BF16)
        dgc = (da * vv) * (sg * (1.0 + gc * (1.0 - sg)))
        n1, n2 = _anticausal_taps(dgc, TS)
        d0 = dgc[:TS]
        dg_ref[...] = ((d0 * wv[2:3] + n1 * wv[1:2]) + n2 * wv[0:1]).astype(BF16)
        part_w = jnp.concatenate([jnp.sum(d0 * x2[:TS], axis=0, keepdims=True),
                                  jnp.sum(d0 * x1[:TS], axis=0, keepdims=True),
                                  jnp.sum(d0 * x0[:TS], axis=0, keepdims=True)], axis=0)
        part_b = jnp.sum(d0, axis=0, keepdims=True)

        @pl.when(i == 0)
        def _():
            dw_ref[...] = part_w
            db_ref[...] = part_b

        @pl.when(i > 0)
        def _():
            dw_ref[...] += part_w
            db_ref[...] += part_b

    blk = pl.BlockSpec((None, TS, FF_BLK), lambda j, i: (j, i, 0))
    prev = pl.BlockSpec((None, HALO, FF_BLK), lambda j, i: (j, _prev_idx(i, TS), 0))
    nxt = pl.BlockSpec((None, HALO, FF_BLK), lambda j, i: (j, _next_idx(i, TS), 0))
    wspec = pl.BlockSpec((None, 3, FF_BLK), lambda j, i: (j, 0, 0))
    bspec = pl.BlockSpec((None, 1, FF_BLK), lambda j, i: (j, 0, 0))
    return _pallas(
        body, name=name, grid=(N_FF_BLK, T // TS),
        in_specs=[pl.BlockSpec((TS, D), lambda j, i: (i, 0)),
                  pl.BlockSpec((HALO, D), lambda j, i: (_next_idx(i, TS), 0)),
                  pl.BlockSpec((None, None, FF_BLK, D), lambda j, i: (0, j, 0, 0)),
                  blk, prev, nxt, blk, nxt, wspec, bspec],
        out_specs=[blk, blk, wspec, bspec],
        out_shape=[jax.ShapeDtypeStruct((N_FF_BLK, T, FF_BLK), BF16), jax.ShapeDtypeStruct((N_FF_BLK, T, FF_BLK), BF16),
                   jax.ShapeDtypeStruct((N_FF_BLK, 3, FF_BLK), F32), jax.ShapeDtypeStruct((N_FF_BLK, 1, FF_BLK), F32)],
        params=_params(("parallel", "arbitrary")))(dh, dh, w_down4, g, g, g, v, v, w, b)


def _rope_tables(pos, inv_freq):
    half = QK_ROPE // 2

    def body(p_ref, f_ref, c_ref, sa_ref, sb_ref):
        ang = p_ref[...].astype(F32) * f_ref[...]
        lane = lax.broadcasted_iota(jnp.int32, (T, 128), 1)
        c = jnp.cos(ang)
        s = jnp.sin(ang)
        c_ref[...] = jnp.where(lane < 2 * half, c, 0.0)
        sa_ref[...] = jnp.where(lane < half, -s, 0.0)
        sb_ref[...] = jnp.where(jnp.logical_and(lane >= half, lane < 2 * half), s, 0.0)

    return _pallas(
        body, name="rope_tables", in_specs=[VMEM_SPEC] * 2, out_specs=[VMEM_SPEC] * 3,
        out_shape=[jax.ShapeDtypeStruct((T, 128), F32)] * 3,
        params=pltpu.CompilerParams(vmem_limit_bytes=VMEM_LIMIT))(pos, inv_freq)


def _rotate(r, c, sa, sb, sign):
    return r * c + sign * (pltpu.roll(r, 96, 1) * sa + pltpu.roll(r, 32, 1) * sb)


def _q_up(cq, w_uq, tables):
    cos, sa, sb = tables

    def body(a_ref, b_ref, c_ref, sa_ref, sb_ref, o_ref):
        r = lax.dot_general(a_ref[...], b_ref[...], NN, preferred_element_type=F32)
        o_ref[:, :QK_NOPE] = r[:, :QK_NOPE].astype(BF16)
        o_ref[:, QK_NOPE:] = _rotate(r[:, QK_NOPE:], c_ref[...], sa_ref[...], sb_ref[...], 1.0).astype(BF16)

    tab = pl.BlockSpec((TM, 128), lambda h, i: (i, 0))
    return _pallas(
        body, name="q_up", grid=(N_HEADS, T // TM),
        in_specs=[pl.BlockSpec((TM, Q_LORA), lambda h, i: (i, 0)),
                  pl.BlockSpec((None, Q_LORA, QK_PAD), lambda h, i: (h, 0, 0)), tab, tab, tab],
        out_specs=pl.BlockSpec((None, TM, QK_PAD), lambda h, i: (h, i, 0)),
        out_shape=jax.ShapeDtypeStruct((N_HEADS, T, QK_PAD), BF16),
        params=_params(("parallel", "parallel")))(cq, w_uq, cos, sa, sb)


def _rope(name, x, tables, sign, out_dtype, reduce_groups=False):
    g, _, w = x.shape
    cos, sa, sb = tables

    def body(x_ref, c_ref, sa_ref, sb_ref, o_ref):
        xv = x_ref[...].astype(F32)
        if reduce_groups:
            acc = xv[0]
            for k in range(1, g):
                acc = acc + xv[k]
            xv = acc
        out = _rotate(xv[:, w - 128:], c_ref[...], sa_ref[...], sb_ref[...], sign)
        if w > 128:
            o_ref[:, :w - 128] = xv[:, :w - 128].astype(out_dtype)
        o_ref[:, w - 128:] = out.astype(out_dtype)

    tab = pl.BlockSpec((TM, 128), lambda h, i: (i, 0))
    if reduce_groups:
        x_spec = pl.BlockSpec((g, TM, w), lambda h, i: (0, i, 0))
        groups = 1
    else:
        x_spec = pl.BlockSpec((None, TM, w), lambda h, i: (h, i, 0))
        groups = g
    return _pallas(
        body, name=name, grid=(groups, T // TM), in_specs=[x_spec, tab, tab, tab],
        out_specs=pl.BlockSpec((None, TM, w), lambda h, i: (h, i, 0)),
        out_shape=jax.ShapeDtypeStruct((groups, T, w), out_dtype),
        params=_params(("parallel", "parallel")))(x, cos, sa, sb)


SCALE = (QK_NOPE + QK_ROPE) ** -0.5
LOG2E = 1.4426950408889634
SCALE2 = SCALE * LOG2E


def _diag_mask(transposed):
    shift = CHUNK.bit_length() - 1
    a = lax.broadcasted_iota(jnp.int32, (TQ, TQ), 0) >> shift
    b = lax.broadcasted_iota(jnp.int32, (TQ, TQ), 1) >> shift
    return (a <= b) if transposed else (b <= a)


def _as_row(col):
    return jnp.transpose(jnp.broadcast_to(col, (col.shape[0], 128)), (1, 0))[0:1]


def _keys(kn_ref, kr_ref, off):
    return jnp.concatenate([kn_ref[pl.ds(off, TQ), :], kr_ref[pl.ds(off, TQ), :]], axis=1)


def _attn_fwd(q, kn, kr, v):
    def body(q_ref, kn_ref, kr_ref, v_ref, o_ref, lse_ref):
        i = pl.program_id(1)
        qv = q_ref[...]

        def step(j, carry, masked):
            m, l, acc = carry
            off = pl.multiple_of(j * TQ, TQ)
            s = lax.dot_general(qv, _keys(kn_ref, kr_ref, off), NT, preferred_element_type=F32) * SCALE2
            if masked:
                s = jnp.where(_diag_mask(False), s, NEG_INF)
            m_new = jnp.maximum(m, jnp.max(s, axis=-1, keepdims=True))
            p = jnp.exp2(s - m_new)
            alpha = jnp.exp2(m - m_new)
            l = alpha * l + jnp.sum(p, axis=-1, keepdims=True)
            acc = alpha * acc + lax.dot_general(p.astype(BF16), v_ref[pl.ds(off, TQ), :], NN, preferred_element_type=F32)
            return m_new, l, acc

        init = (jnp.full((TQ, 1), NEG_INF, F32), jnp.zeros((TQ, 1), F32), jnp.zeros((TQ, V_HEAD), F32))
        carry = lax.fori_loop(0, i, lambda j, cr: step(j, cr, False), init)
        m, l, acc = step(i, carry, True)
        o_ref[...] = (acc / l).astype(BF16)
        lse_ref[...] = _as_row(m + jnp.log(l) * LOG2E)

    return _pallas(
        body, name="attn_fwd", grid=(N_HEADS, T // TQ),
        in_specs=[pl.BlockSpec((None, TQ, QK_PAD), lambda h, i: (h, i, 0)),
                  pl.BlockSpec((T, QK_NOPE), lambda h, i: (0, h)),
                  pl.BlockSpec((T, 128), lambda h, i: (0, 0)),
                  pl.BlockSpec((T, V_HEAD), lambda h, i: (0, h))],
        out_specs=[pl.BlockSpec((TQ, V_HEAD), lambda h, i: (i, h)), pl.BlockSpec((None, 1, TQ), lambda h, i: (h, 0, i))],
        out_shape=[jax.ShapeDtypeStruct((T, N_HEADS * V_HEAD), BF16), jax.ShapeDtypeStruct((N_HEADS, 1, T), F32)],
        params=_params(("parallel", "parallel")))(q, kn, kr, v)


def _attn_bwd(q, kn, kr, v, o, do, lse_row, tables):
    nq = T // TQ
    cos, sa, sb = tables

    def body(q_ref, kn_ref, kr_ref, v_ref, o_ref, do_ref, lse_ref, c_ref, sa_ref, sb_ref,
             dq_ref, dkn_ref, dkr_ref, dv_ref, dq_acc, dl_ref):
        j = pl.program_id(1)

        @pl.when(j == 0)
        def _():
            dq_acc[...] = jnp.zeros_like(dq_acc)
            for i in range(nq):
                rows = pl.ds(i * TQ, TQ)
                prod = do_ref[rows, :].astype(F32) * o_ref[rows, :].astype(F32)
                dl_ref[:, rows] = _as_row(jnp.sum(prod, axis=-1, keepdims=True))

        kk = jnp.concatenate([kn_ref[...], kr_ref[...]], axis=1)
        vv = v_ref[...]

        def step(i, carry, masked):
            dk, dv = carry
            off = pl.multiple_of(i * TQ, TQ)
            qi = q_ref[pl.ds(off, TQ), :]
            doi = do_ref[pl.ds(off, TQ), :]
            st = lax.dot_general(kk, qi, NT, preferred_element_type=F32) * SCALE2
            if masked:
                st = jnp.where(_diag_mask(True), st, NEG_INF)
            pt = jnp.exp2(st - lse_ref[:, pl.ds(off, TQ)])
            dv = dv + lax.dot_general(pt.astype(BF16), doi, NN, preferred_element_type=F32)
            dpt = lax.dot_general(vv, doi, NT, preferred_element_type=F32)
            dst = ((pt * (dpt - dl_ref[:, pl.ds(off, TQ)])) * SCALE).astype(BF16)
            dk = dk + lax.dot_general(dst, qi, NN, preferred_element_type=F32)
            dq_acc[pl.ds(off, TQ), :] += lax.dot_general(dst, kk, TN, preferred_element_type=F32)
            return dk, dv

        carry = step(j, (jnp.zeros((TQ, QK_PAD), F32), jnp.zeros((TQ, V_HEAD), F32)), True)
        dk, dv = lax.fori_loop(j + 1, nq, lambda i, cr: step(i, cr, False), carry)
        dkn_ref[...] = dk[:, :QK_NOPE].astype(BF16)
        dkr_ref[...] = dk[:, QK_NOPE:]
        dv_ref[...] = dv.astype(BF16)

        @pl.when(j == nq - 1)
        def _():
            dq = dq_acc[...]
            dq_ref[:, :QK_NOPE] = dq[:, :QK_NOPE].astype(BF16)
            dq_ref[:, QK_NOPE:] = _rotate(dq[:, QK_NOPE:], c_ref[...], sa_ref[...], sb_ref[...], -1.0).astype(BF16)

    row = pl.BlockSpec((None, 1, T), lambda h, j: (h, 0, 0))
    head = pl.BlockSpec((TQ, 128), lambda h, j: (j, h))
    whole = pl.BlockSpec((None, T, QK_PAD), lambda h, j: (h, 0, 0))
    tab = pl.BlockSpec((T, 128), lambda h, j: (0, 0))
    heads = pl.BlockSpec((T, V_HEAD), lambda h, j: (0, h))
    return _pallas(
        body, name="attn_bwd", grid=(N_HEADS, nq),
        in_specs=[whole, head, pl.BlockSpec((TQ, 128), lambda h, j: (j, 0)), head, heads, heads, row, tab, tab, tab],
        out_specs=[whole, head, pl.BlockSpec((None, TQ, 128), lambda h, j: (h, j, 0)), head],
        out_shape=[jax.ShapeDtypeStruct((N_HEADS, T, QK_PAD), BF16), jax.ShapeDtypeStruct((T, N_HEADS * QK_NOPE), BF16),
                   jax.ShapeDtypeStruct((N_HEADS, T, 128), F32), jax.ShapeDtypeStruct((T, N_HEADS * V_HEAD), BF16)],
        scratch_shapes=[pltpu.VMEM((T, QK_PAD), F32), pltpu.VMEM((1, T), F32)],
        params=_params(("parallel", "arbitrary")))(q, kn, kr, v, o, do, lse_row, cos, sa, sb)


def _ffn_gup(name, dg, dv, hf):
    def body(dg_ref, dv_ref, hf_ref, o_ref):
        j = pl.program_id(0)

        @pl.when(j < N_FF_BLK)
        def _():
            o_ref[...] = lax.dot_general(dg_ref[...], hf_ref[...], TN, preferred_element_type=F32).astype(BF16)

        @pl.when(j >= N_FF_BLK)
        def _():
            o_ref[...] = lax.dot_general(dv_ref[...], hf_ref[...], TN, preferred_element_type=F32).astype(BF16)

    return _pallas(
        body, name=name, grid=(N_DEV,),
        in_specs=[pl.BlockSpec((None, T, FF_BLK), lambda j: (jnp.minimum(j, N_FF_BLK - 1), 0, 0)),
                  pl.BlockSpec((None, T, FF_BLK), lambda j: (jnp.maximum(j - N_FF_BLK, 0), 0, 0)),
                  pl.BlockSpec((T, D), lambda j: (0, 0))],
        out_specs=pl.BlockSpec((None, FF_BLK, D), lambda j: (j, 0, 0)),
        out_shape=jax.ShapeDtypeStruct((N_DEV, FF_BLK, D), BF16), params=_params(("parallel",)))(dg, dv, hf)


def _ffn_layer_fwd(tag, h, gain, ex):
    hf = _rms_fwd(f"{tag}_norm", h, gain)
    g, v, act = _ffn_up_act(f"{tag}_up", hf, ex.need(f"ffn_w_up{tag[1]}", hf), ex.need(f"ffn_cw{tag[1]}", hf),
                            ex.need(f"ffn_cb{tag[1]}", hf))
    ex.at(f"{tag}_up", act)
    rows = pl.BlockSpec((TS, D), lambda i: (i, 0))
    out = _mm_sum(f"{tag}_down",
                  [(act, pl.BlockSpec((N_FF_BLK, TS, FF_BLK), lambda i: (0, i, 0)), ex.need(f"ffn_w_down{tag[1]}", act),
                    pl.BlockSpec((None, N_FF_BLK, FF_BLK, D), lambda i: (0, 0, 0, 0)), NN, 0)],
                  grid=(T // TS,), o_spec=rows, o_shape=(T, D), o_dtype=F32, add=h)
    ex.at(f"{tag}_down", out)
    return out, (hf, g, v, act)


def _ffn_layer_bwd(tag, h, gain, ex, saved, dh, dh_bf):
    hf, g, v, act = saved
    layer = tag[1]
    w_up, w_down4 = ex.need(f"ffn_w_up{layer}", dh_bf), ex.need(f"ffn_w_down{layer}", dh_bf)
    dg, dv, dcw, dcb = _ffn_dact(f"{tag}_dact", dh_bf, w_down4, g, v, ex.need(f"ffn_cw{layer}", dh_bf),
                                 ex.need(f"ffn_cb{layer}", dh_bf))
    ex.at(f"{tag}_dact", dg)
    g_down = _mm(f"{tag}_gdown", act, dh_bf, grid=(N_FF_BLK,),
                 a_spec=pl.BlockSpec((None, T, FF_BLK), lambda j: (j, 0, 0)),
                 b_spec=pl.BlockSpec((T, D), lambda j: (0, 0)),
                 o_spec=pl.BlockSpec((FF_BLK, D), lambda j: (j, 0)),
                 o_shape=(D_FF, D), o_dtype=BF16, dims=TN)
    g_up = _ffn_gup(f"{tag}_gup", dg, dv, hf)
    ex.grad("ffn_w_up", int(layer), g_up.reshape(1, N_DEV, FF_BLK, D))
    ex.grad("ffn_w_down", int(layer), g_down.reshape(1, N_DEV, D_FF // N_DEV, D))
    ex.at(f"{tag}_gup", g_up)
    part = pl.BlockSpec((N_FF_BLK, TR, FF_BLK), lambda i: (0, i, 0))
    dh_in, dh_in_bf, dgain = _mm_sum(
        f"{tag}_dhf",
        [(dg, part, w_up, pl.BlockSpec((None, N_FF_BLK, FF_BLK, D), lambda i: (0, 0, 0, 0)), NN, 0),
         (dv, part, w_up, pl.BlockSpec((None, N_FF_BLK, FF_BLK, D), lambda i: (0, 1, 0, 0)), NN, 0)],
        grid=(T // TR,), o_spec=pl.BlockSpec((TR, D), lambda i: (i, 0)), o_shape=(T, D), o_dtype=F32,
        norm_bwd=(h, [gain], dh))
    ex.at(f"{tag}_dhf", dh_in)
    return dh_in, dh_in_bf, dgain[0], dcw, dcb


def _local_step(x, pos, tgt, rep, ex):
    attn_norm, ffn_norm, final_norm = rep["attn_norm"], rep["ffn_norm"], rep["final_norm"]
    half = QK_ROPE // 2
    inv = 1.0 / (ROPE_THETA ** (jnp.arange(half, dtype=F32) / half))
    inv_freq = jnp.concatenate([inv, inv, jnp.zeros((128 - 2 * half,), F32)]).reshape(1, 128)
    tables = _rope_tables(pos, inv_freq)

    hn0 = _rms_fwd("l0_norm", x, attn_norm[0:1])
    w_in = ex.need("sc_w_in", hn0)
    ex.at("mixer_ready", hn0)
    z = _mm_rows("l0_in", hn0, w_in, NN, BF16, 3 * D, tn=512)
    ex.at("l0_in", z)
    y = _sc_fwd(z, ex.need("sc_conv_w", z))
    h1 = _mm_rows("l0_out", y, ex.need("sc_w_out", y), NN, F32, D, tn=512, add=x)
    ex.at("l0_out", h1)
    h2, ffn0 = _ffn_layer_fwd("f0", h1, ffn_norm[0:1], ex)

    hk = _rms_fwd("kv_norm", h2, rep["kv_in_norm"])
    ckv_raw = _mm_rows("kv_down", hk, ex.need("w_dkv", hk), NN, F32, KV_LORA)
    kr_raw = _mm_rows("kv_rope", hk, ex.need("w_kr", hk), NN, F32, 128)
    ckv = _rms_fwd("kv_lnorm", ckv_raw, rep["kv_latent_norm"])
    kn = _mm_rows("kv_uk", ckv, ex.need("w_uk", ckv), NN, BF16, N_HEADS * QK_NOPE)
    vv = _mm_rows("kv_uv", ckv, ex.need("w_uv", ckv), NN, BF16, N_HEADS * V_HEAD)
    kr = _rope("k_rope", kr_raw.reshape(1, T, 128), tables, 1.0, BF16).reshape(T, 128)

    hn1 = _rms_fwd("l1_norm", h2, attn_norm[1:2])
    cq_raw = _mm_rows("q_down", hn1, ex.need("w_dq", hn1), NN, F32, Q_LORA)
    cq = _rms_fwd("q_lnorm", cq_raw, rep["q_latent_norm"])
    w_uq = ex.need("w_uq", cq)
    q = _q_up(cq, w_uq, tables)
    o, lse = _attn_fwd(q, kn, kr, vv)
    ex.at("attn_fwd", o)
    w_o = ex.need("w_o", o)
    h3 = _mm_rows("attn_out", o, w_o, NN, F32, D, tn=512, add=h2)
    h4, ffn1 = _ffn_layer_fwd("f1", h3, ffn_norm[1:2], ex)

    loss, dh4, dh4_bf, d_final = _final(h4, final_norm.reshape(1, D), tgt)

    dh3, dh3_bf, d_fn1, dcw1, dcb1 = _ffn_layer_bwd("f1", h3, ffn_norm[1:2], ex, ffn1, dh4, dh4_bf)
    ex.at("f1_bwd", dh3)

    do = _mm_rows("d_attn_out", dh3_bf, w_o, NT, BF16, N_HEADS * V_HEAD)
    ex.grad("w_o", None, _mm_wgrad("g_w_o", o, dh3_bf).reshape(1, N_DEV, D // N_DEV, D))
    dq_pre, dkn, dkr, dvv = _attn_bwd(q, kn, kr, vv, o, do, lse, tables)
    def rows_of(a):
        return a[None], pl.BlockSpec((1, TS, a.shape[1]), lambda i: (0, i, 0))

    def whole(wt):
        return wt[None], pl.BlockSpec((1,) + wt.shape, lambda i: (0, 0, 0))

    def row_blocks(d):
        return dict(grid=(T // TS,), o_spec=pl.BlockSpec((TS, d), lambda i: (i, 0)), o_shape=(T, d), o_dtype=F32)

    _, dcq_raw_bf, (d_qln,) = _mm_sum(
        "d_q_up", [(dq_pre, pl.BlockSpec((N_HEADS, TS, QK_PAD), lambda i: (0, i, 0)),
                    w_uq, pl.BlockSpec((N_HEADS, Q_LORA, QK_PAD), lambda i: (0, 0, 0)), NT, 0)],
        norm_bwd=(cq_raw, [rep["q_latent_norm"]], None), **row_blocks(Q_LORA))
    g_uq = _mm("g_w_uq", cq, dq_pre, grid=(N_HEADS,),
               a_spec=pl.BlockSpec((T, Q_LORA), lambda h: (0, 0)),
               b_spec=pl.BlockSpec((None, T, QK_PAD), lambda h: (h, 0, 0)),
               o_spec=pl.BlockSpec((None, Q_LORA, QK_PAD), lambda h: (h, 0, 0)),
               o_shape=(N_HEADS, Q_LORA, QK_PAD), o_dtype=BF16, dims=TN)
    ex.grad("w_uq", None, g_uq[:, :, :QK_NOPE + QK_ROPE].reshape(1, N_DEV, Q_LORA, QK_NOPE + QK_ROPE))
    ex.grad("w_dq", None, _mm_wgrad("g_w_dq", hn1, dcq_raw_bf).reshape(1, N_DEV, D // N_DEV, Q_LORA))

    _, dckv_raw_bf, (d_kvln,) = _mm_sum(
        "d_kv_up", [(*rows_of(dkn), *whole(ex.need("w_uk", dkn)), NT, 0),
                    (*rows_of(dvv), *whole(ex.need("w_uv", dvv)), NT, 0)],
        norm_bwd=(ckv_raw, [rep["kv_latent_norm"]], None), **row_blocks(KV_LORA))
    ex.grad("w_uk", None, _mm_wgrad("g_w_uk", ckv, dkn))
    ex.grad("w_uv", None, _mm_wgrad("g_w_uv", ckv, dvv))
    dkr_raw_bf = _rope("dk_rope", dkr, tables, -1.0, BF16, reduce_groups=True).reshape(T, 128)
    ex.grad("w_dkv", None, _mm_wgrad("g_w_dkv", hk, dckv_raw_bf).reshape(1, N_DEV, D // N_DEV, KV_LORA))
    ex.grad("w_kr", None, _mm_wgrad("g_w_kr", hk, dkr_raw_bf)[:, :QK_ROPE].reshape(1, N_DEV, D // N_DEV, QK_ROPE))

    dh2, dh2_bf, (d_an1, d_kvin) = _mm_sum(
        "d_h2", [(*rows_of(dcq_raw_bf), *whole(ex.need("w_dq", dcq_raw_bf)), NT, 0),
                 (*rows_of(dckv_raw_bf), *whole(ex.need("w_dkv", dckv_raw_bf)), NT, 1),
                 (*rows_of(dkr_raw_bf), *whole(ex.need("w_kr", dkr_raw_bf)), NT, 1)],
        norm_bwd=(h2, [attn_norm[1:2], rep["kv_in_norm"]], dh3), **row_blocks(D))
    ex.at("kv_bwd", dh2)

    dh1, dh1_bf, d_fn0, dcw0, dcb0 = _ffn_layer_bwd("f0", h1, ffn_norm[0:1], ex, ffn0, dh2, dh2_bf)
    ex.at("f0_bwd", dh1)

    dy = _mm_rows("d_l0_out", dh1_bf, ex.need("sc_w_out", dh1_bf), NT, F32, D)
    ex.grad("sc_w_out", None, _mm_wgrad("g_sc_w_out", y, dh1_bf).reshape(1, N_DEV, D // N_DEV, D))
    dz, d_scw = _sc_bwd(z, dy, ex.need("sc_conv_w", dy))
    g_in = _mm_wgrad("g_sc_w_in", hn0, dz)
    ex.grad("sc_w_in", None, g_in)
    ex.at("sc_bwd", g_in)
    ex.at("d_l0_in", g_in)
    grad_x, _, (d_an0,) = _mm_sum(
        "d_l0_in", [(*rows_of(dz), *whole(ex.need("sc_w_in", dz)), NT, 0)],
        norm_bwd=(x, [attn_norm[0:1]], dh1), **row_blocks(D))

    small = {
        "attn_norm": jnp.concatenate([d_an0, d_an1], axis=0),
        "ffn_norm": jnp.concatenate([d_fn0, d_fn1], axis=0),
        "final_norm": d_final.reshape(D),
        "kv_in_norm": d_kvin.reshape(D),
        "kv_latent_norm": d_kvln.reshape(KV_LORA),
        "q_latent_norm": d_qln,
        "ffn_conv_b": jnp.stack([dcb0, dcb1]).transpose(0, 2, 1, 3).reshape(2, D_FF),
        "sc_conv_w": d_scw,
        "ffn_conv_w": jnp.stack([dcw0, dcw1]).transpose(0, 2, 1, 3).reshape(2, 3, D_FF),
    }
    return loss, grad_x, small


def _place():
    return lax.axis_index("x"), lax.axis_index("y"), lax.axis_index("c")


def _peers():
    x, y, c = _place()
    return (x, y, 1 - c), [(1 - x, y), (x, 1 - y), (1 - x, 1 - y)]


def _window(ref, kind, dev):
    if kind == "blocked":
        return ref.at[:, dev]
    width = ref.shape[-1] // N_DEV
    return ref.at[:, pl.ds(pl.multiple_of(dev * width, 128), width)]


def _all_gather(name, items):
    n = len(items)
    out_shapes = []
    for shard, kind in items:
        if kind == "blocked":
            shape = (shard.shape[0], N_DEV) + shard.shape[1:]
        else:
            shape = (shard.shape[0], N_DEV * shard.shape[1])
        out_shapes.append(jax.ShapeDtypeStruct(shape, shard.dtype))

    def body(*refs):
        srcs, outs = refs[:n], refs[n:2 * n]
        send_sems, recv_sems, local_sems = refs[2 * n:]
        x, y, c = _place()
        me = 4 * x + 2 * y + c
        sibling, chips = _peers()

        def num(px, py, pc):
            return 4 * px + 2 * py + pc

        def copy(t, k, dev, to, from_src):
            kind = items[t][1]
            dst = _window(outs[t], kind, dev)
            return pltpu.make_async_remote_copy(
                src_ref=srcs[t] if from_src else dst, dst_ref=dst,
                send_sem=send_sems.at[t, k], recv_sem=recv_sems.at[t, k], device_id=to, device_id_type=MESH)

        mine = [pltpu.make_async_copy(srcs[t], _window(outs[t], items[t][1], me), local_sems.at[t]) for t in range(n)]
        for cp in mine:
            cp.start()
        first = []
        for t in range(n):
            first.append(copy(t, 0, me, sibling, True))
            for j, chip in enumerate(chips):
                first.append(copy(t, 1 + j, me, (*chip, c), True))
        for cp in first:
            cp.start()
        passed = []
        for j, chip in enumerate(chips):
            for t in range(n):
                copy(t, 1 + j, num(*chip, c), (x, y, c), False).wait_recv()
                fwd = copy(t, 4 + j, num(*chip, c), sibling, False)
                fwd.start()
                passed.append(fwd)
        for t in range(n):
            copy(t, 0, num(x, y, 1 - c), (x, y, c), False).wait_recv()
            for j, chip in enumerate(chips):
                copy(t, 4 + j, num(*chip, 1 - c), (x, y, c), False).wait_recv()
        for cp in first + passed:
            cp.wait_send()
        for cp in mine:
            cp.wait()

    return _pallas(
        body, name=name, in_specs=[ANY_SPEC] * n, out_specs=[ANY_SPEC] * n, out_shape=out_shapes,
        scratch_shapes=[pltpu.SemaphoreType.DMA((n, 7)), pltpu.SemaphoreType.DMA((n, 7)), pltpu.SemaphoreType.DMA((n,))],
    )(*[s for s, _ in items])


HBM_SPEC = pl.BlockSpec(memory_space=pltpu.HBM)
SEM_SPEC = pl.BlockSpec(memory_space=pltpu.SEMAPHORE)
EFFECT = pltpu.SideEffectType.DATAFLOW_SIDE_EFFECTING
TOKEN = jax.ShapeDtypeStruct((8, 128), F32)


def _hbm(a):
    return pltpu.with_memory_space_constraint(a, pltpu.HBM)


def _copies_start(name, srcs, lands, ncopy, plan):
    ns, nl = len(srcs), len(lands)

    def body(*refs):
        send, recv, token = refs[ns + nl], refs[ns + nl + 1], refs[-1]
        copies = plan(refs[:ns], refs[ns:ns + nl])
        assert len(copies) == ncopy
        for k, (sent, dst, to, _) in enumerate(copies):
            pltpu.make_async_remote_copy(src_ref=sent, dst_ref=dst, send_sem=send.at[k], recv_sem=recv.at[k],
                                         device_id=to, device_id_type=MESH).start()
        token[...] = jnp.zeros_like(token)

    arrays = list(srcs) + list(lands)
    outs = pl.pallas_call(
        body, name=name, in_specs=[HBM_SPEC] * (ns + nl),
        out_specs=[SEM_SPEC] * 2 + [HBM_SPEC] * (ns + nl) + [VMEM_SPEC],
        out_shape=[pltpu.SemaphoreType.DMA((ncopy,))] * 2 + [pltpu.HBM(a.shape, a.dtype) for a in arrays] + [TOKEN],
        input_output_aliases={i: 2 + i for i in range(ns + nl)},
        compiler_params=pltpu.CompilerParams(has_side_effects=EFFECT))(*[_hbm(a) for a in arrays])
    _Chain.last = outs[-1]
    return outs[0], outs[1], list(outs[2:2 + ns]), list(outs[2 + ns:-1])


def _copies_wait(name, started, ncopy, plan):
    send, recv, srcs, lands = started
    ns, nl = len(srcs), len(lands)

    def body(*refs):
        send_ref, recv_ref, token = refs[ns + nl], refs[ns + nl + 1], refs[-1]
        copies = plan(refs[:ns], refs[ns:ns + nl])
        assert len(copies) == ncopy
        for k, (sent, _, to, landed) in enumerate(copies):
            cp = pltpu.make_async_remote_copy(src_ref=sent, dst_ref=landed, send_sem=send_ref.at[k],
                                              recv_sem=recv_ref.at[k], device_id=to, device_id_type=MESH)
            cp.wait_send()
            cp.wait_recv()
        token[...] = jnp.zeros_like(token)

    arrays = list(srcs) + list(lands)
    outs = pl.pallas_call(
        body, name=name, in_specs=[HBM_SPEC] * (ns + nl) + [SEM_SPEC] * 2 + [ANY_SPEC],
        out_specs=[HBM_SPEC] * (ns + nl) + [VMEM_SPEC], out_shape=[pltpu.HBM(a.shape, a.dtype) for a in arrays] + [TOKEN],
        input_output_aliases={i: i for i in range(ns + nl)},
        compiler_params=pltpu.CompilerParams(has_side_effects=EFFECT))(*arrays, send, recv, _Chain.last)
    _Chain.last = outs[-1]
    return list(outs[:ns]), list(outs[ns:-1])


def _plan_gather_chips(kinds):
    def plan(srcs, lands):
        x, y, c = _place()
        sibling, chips = _peers()
        out = []
        for t, kind in enumerate(kinds):
            mine = _window(lands[t], kind, 4 * x + 2 * y + c)
            out.append((srcs[t], mine, sibling, _window(lands[t], kind, 4 * x + 2 * y + 1 - c)))
            for px, py in chips:
                out.append((srcs[t], mine, (px, py, c), _window(lands[t], kind, 4 * px + 2 * py + c)))
        return out
    return plan, 4 * len(kinds)


def _plan_gather_sibling(kinds):
    def plan(srcs, lands):
        _, _, c = _place()
        sibling, chips = _peers()
        out = []
        for t, kind in enumerate(kinds):
            for px, py in chips:
                w = _window(lands[t], kind, 4 * px + 2 * py + c)
                out.append((w, w, sibling, _window(lands[t], kind, 4 * px + 2 * py + 1 - c)))
        return out
    return plan, 3 * len(kinds)


def _plan_scatter_sibling(kinds):
    def plan(srcs, lands):
        _, _, c = _place()
        sibling, _ = _peers()
        out = []
        for t, kind in enumerate(kinds):
            for k in range(N_CHIP):
                out.append((_window(srcs[t], kind, 2 * k + 1 - c), lands[t].at[k], sibling, lands[t].at[k]))
        return out
    return plan, N_CHIP * len(kinds)


def _plan_scatter_chips(n):
    def plan(srcs, lands):
        x, y, c = _place()
        _, chips = _peers()
        out = []
        for t in range(n):
            for px, py in chips:
                out.append((srcs[t].at[2 * px + py], lands[t].at[2 * x + y], (px, py, c), lands[t].at[2 * px + py]))
        return out
    return plan, 3 * n


def _landing(shard, kind, me):
    if kind == "blocked":
        land = lax.empty((shard.shape[0], N_DEV) + shard.shape[1:], shard.dtype)
        return lax.dynamic_update_slice(land, shard[:, None], (0, me) + (0,) * (shard.ndim - 1))
    land = lax.empty((shard.shape[0], N_DEV * shard.shape[1]), shard.dtype)
    return lax.dynamic_update_slice(land, shard, (0, me * shard.shape[1]))


def _chip_sums(name, grads, kinds, recvs, c):
    n = len(grads)
    in_specs, out_specs, out_shape, args = [], [], [], []
    for gr, kind, rv in zip(grads, kinds, recvs):
        if kind == "blocked":
            rows, w = gr.shape[2], gr.shape[3]
            in_specs.append(pl.BlockSpec((None, None, rows, w), lambda k, cref: (0, 2 * k + cref[0], 0, 0)))
        else:
            rows, w = gr.shape[0], gr.shape[1] // N_DEV
            in_specs.append(pl.BlockSpec((rows, w), lambda k, cref: (0, 2 * k + cref[0])))
        blk = pl.BlockSpec((None, rows, w), lambda k, cref: (k, 0, 0))
        in_specs.append(blk)
        out_specs.append(blk)
        out_shape.append(jax.ShapeDtypeStruct((N_CHIP, rows, w), BF16))
        args += [gr, rv.reshape(N_CHIP, rows, w)]

    def body(*refs):
        for t in range(n):
            g_ref, r_ref, o_ref = refs[1 + 2 * t], refs[2 + 2 * t], refs[1 + 2 * n + t]
            o_ref[...] = (g_ref[...].astype(F32) + r_ref[...].astype(F32)).astype(BF16)

    return _pallas(body, name=name, n_prefetch=1, grid=(N_CHIP,), in_specs=in_specs, out_specs=out_specs,
                   out_shape=out_shape, params=_params(("parallel",)))(c, *args)


def _adamw_math(g, wv, mv, vv):
    m = ADAM_B1 * mv + (1.0 - ADAM_B1) * g
    v = ADAM_B2 * vv + (1.0 - ADAM_B2) * (g * g)
    m_hat = m / (1.0 - ADAM_B1 ** ADAM_STEP)
    v_hat = v / (1.0 - ADAM_B2 ** ADAM_STEP)
    delta = -ADAM_LR * (m_hat / (jnp.sqrt(v_hat) + ADAM_EPS) + ADAM_WD * wv)
    return delta, m, v


ADAM_STEPS = 2


def _adamw_group(name, items, chip_ids):
    n = len(items)
    in_specs, out_specs, out_shape, args, prevs = [], [], [], [chip_ids], []
    for own, recv, w3, m3, v3, layer, _ in items:
        nl, rows, w = w3.shape
        tr = rows // ADAM_STEPS
        assert tr % 16 == 0, (name, rows)
        in_specs += [pl.BlockSpec((None, tr, w), lambda i, ids, slot=slot: (ids[slot], i, 0)) for slot in range(4)]
        slab = pl.BlockSpec((None, tr, w), lambda i, ids, layer=layer: (layer, i, 0))
        in_specs += [slab] * 3
        out_specs += [slab] * 4
        out_shape += [jax.ShapeDtypeStruct((nl, rows, w), F32)] * 4
        args += [own, recv, recv, recv, w3, m3, v3]
    aliases = {}
    for t, item in enumerate(items):
        if item[6] is not None:
            for k in range(4):
                aliases[len(args) + k] = 4 * t + k
            in_specs += [ANY_SPEC] * 4
            args += list(item[6])
            prevs.append(t)
    n_in = 1 + 7 * n + 4 * len(prevs)

    def body(*refs):
        for t in range(n):
            own_ref, r1_ref, r2_ref, r3_ref, w_ref, m_ref, v_ref = refs[1 + 7 * t:8 + 7 * t]
            g_ref, d_ref, nm_ref, nv_ref = refs[n_in + 4 * t:n_in + 4 * t + 4]
            g = ((own_ref[...].astype(F32) + r1_ref[...].astype(F32)) + r2_ref[...].astype(F32)) + r3_ref[...].astype(F32)
            g_ref[...] = g
            d_ref[...], nm_ref[...], nv_ref[...] = _adamw_math(g, w_ref[...], m_ref[...], v_ref[...])

    outs = _pallas(body, name=name, n_prefetch=1, grid=(ADAM_STEPS,), in_specs=in_specs, out_specs=out_specs,
                   out_shape=out_shape, aliases=aliases, params=_params(("parallel",)))(*args)
    return [list(outs[4 * t:4 * t + 4]) for t in range(n)]


def _adamw_small(gathered, ws, ms, vs):
    n = len(gathered)
    full = [w is not None for w in ws]
    args = list(gathered)
    out_shape = []
    for t in range(n):
        shape = jax.ShapeDtypeStruct(gathered[t].shape[2:], F32)
        if full[t]:
            args += [ws[t], ms[t], vs[t]]
            out_shape += [shape] * 4
        else:
            out_shape += [shape]

    def body(*refs):
        i_in, i_out = n, len(args)
        for t in range(n):
            p_ref = refs[t]
            g = p_ref[0, 0]
            for k in range(1, N_DEV):
                g = g + p_ref[0, k]
            refs[i_out][...] = g
            if full[t]:
                w_ref, m_ref, v_ref = refs[i_in:i_in + 3]
                refs[i_out + 1][...], refs[i_out + 2][...], refs[i_out + 3][...] = _adamw_math(
                    g, w_ref[...], m_ref[...], v_ref[...])
                i_in += 3
                i_out += 4
            else:
                i_out += 1

    outs = _pallas(body, name="adamw_small", in_specs=[VMEM_SPEC] * len(args), out_specs=[VMEM_SPEC] * len(out_shape),
                   out_shape=out_shape, params=pltpu.CompilerParams(vmem_limit_bytes=VMEM_LIMIT))(*args)
    result, i = [], 0
    for t in range(n):
        k = 4 if full[t] else 1
        result.append(list(outs[i:i + k]))
        i += k
    return result


def _adamw_plain(name, gs, ws, ms, vs):
    n = len(gs)

    def body(*refs):
        for t in range(n):
            g_ref, w_ref, m_ref, v_ref = refs[4 * t:4 * t + 4]
            outs = refs[4 * n + 3 * t:4 * n + 3 * t + 3]
            outs[0][...], outs[1][...], outs[2][...] = _adamw_math(g_ref[...], w_ref[...], m_ref[...], v_ref[...])

    args, out_shape = [], []
    for g, w, m, v in zip(gs, ws, ms, vs):
        args += [g, w, m, v]
        out_shape += [jax.ShapeDtypeStruct(w.shape, F32)] * 3
    outs = _pallas(body, name=name, in_specs=[VMEM_SPEC] * len(args), out_specs=[VMEM_SPEC] * len(out_shape),
                   out_shape=out_shape, params=pltpu.CompilerParams(vmem_limit_bytes=VMEM_LIMIT))(*args)
    return [list(outs[3 * t:3 * t + 3]) for t in range(n)]


KIND = {"sc_w_in": "cols", "sc_w_out": "blocked", "w_dkv": "blocked", "w_kr": "blocked", "w_uk": "cols", "w_uv": "cols",
        "w_dq": "blocked", "w_uq": "blocked", "w_o": "blocked", "ffn_w_up": "blocked", "ffn_w_down": "blocked",
        "conv": "blocked"}
GATHER_GROUPS = (("mixer", ("sc_w_in", "sc_w_out", "conv")),
                 ("up0", ("ffn_w_up0",)),
                 ("down0", ("ffn_w_down0",)),
                 ("attn", ("w_dkv", "w_kr", "w_uk", "w_uv", "w_dq", "w_uq", "w_o")),
                 ("ffn1", ("ffn_w_up1", "ffn_w_down1")))
SCATTER_GROUPS = (("ffn1", (("ffn_w_up", 1), ("ffn_w_down", 1))),
                  ("attn", (("w_o", None), ("w_uq", None), ("w_dq", None), ("w_uk", None), ("w_uv", None),
                            ("w_dkv", None), ("w_kr", None))),
                  ("ffn0", (("ffn_w_up", 0), ("ffn_w_down", 0))),
                  ("mixer", (("sc_w_out", None), ("sc_w_in", None))))
SCHEDULE = {
    "begin": (("gather_start", "mixer"),),
    "mixer_ready": (("gather_start", "up0"),),
    "l0_out": (("gather_forward", "up0"), ("gather_start", "down0")),
    "f0_up": (("gather_forward", "down0"), ("gather_start", "attn")),
    "f0_down": (("gather_forward", "attn"), ("gather_start", "ffn1")),
    "attn_fwd": (("gather_forward", "ffn1"),),
    "f1_gup": (("scatter_sibling", "ffn1"),),
    "f1_dhf": (("scatter_chips", "ffn1"),),
    "kv_bwd": (("scatter_sibling", "attn"), ("scatter_done", "ffn1")),
    "f0_dact": (("scatter_chips", "attn"),),
    "f0_gup": (("scatter_sibling", "ffn0"),),
    "f0_dhf": (("scatter_chips", "ffn0"),),
    "f0_bwd": (("scatter_done", "attn"),),
    "sc_bwd": (("scatter_sibling", "mixer"),),
    "d_l0_in": (("scatter_chips", "mixer"),),
}
FINISH = (("scatter_done", "ffn0"), ("scatter_done", "mixer"))
STAGES = {"gather_start": 1, "gather_forward": 2, "gather_done": 3,
          "scatter_sibling": 1, "scatter_chips": 2, "scatter_done": 3}
SMALL_W_ROWS = 24


def _pack(arrays, rows):
    flat = jnp.concatenate([a.reshape(-1).astype(F32) for a in arrays])
    return jnp.pad(flat, (0, rows * 128 - flat.shape[0])).reshape(rows, 128)


def _stored(name, a):
    return jnp.swapaxes(a, -1, -2) if name == "ffn_w_up" else a


def _base(name):
    if name.startswith("ffn_w_") and name[-1] in "01":
        return name[:-1], int(name[-1])
    return name, None


class _Exchange:
    def __init__(self, wts, mom, var, ffn_conv_b):
        self.wts, self.mom, self.var = wts, mom, var
        x, y, c = _place()
        self.me = 4 * x + 2 * y + c
        self.c_arr = jnp.reshape(c, (1,)).astype(jnp.int32)
        chip = 2 * x + y
        self.chip_ids = jnp.stack([chip, chip ^ 1, chip ^ 2, chip ^ 3]).astype(jnp.int32)
        self.ready = {"ffn_cb0": ffn_conv_b.reshape(2, N_FF_BLK, 1, FF_BLK)[0],
                      "ffn_cb1": ffn_conv_b.reshape(2, N_FF_BLK, 1, FF_BLK)[1]}
        self.gathers, self.group_of = {}, {}
        self.grads, self.scatters, self.results = {}, {}, {}
        for gname, names in GATHER_GROUPS:
            self.gathers[gname] = dict(stage=0, names=names, kinds=[KIND[_base(nm)[0]] for nm in names])
            for nm in names:
                self.group_of[nm] = gname
        for nm in ("sc_conv_w", "ffn_cw0", "ffn_cw1"):
            self.group_of[nm] = "mixer"
        self.at("begin", None)

    def _shard(self, name):
        if name == "conv":
            return _pack([self.wts["sc_conv_w"], self.wts["ffn_conv_w"]], SMALL_W_ROWS).reshape(1, SMALL_W_ROWS, 128)
        base, layer = _base(name)
        a = _stored(base, self.wts[base])
        if layer is not None:
            a = a[layer:layer + 1]
        if KIND[base] == "cols":
            return a.reshape(a.shape[-2], a.shape[-1]).astype(BF16)
        return a.reshape((-1,) + a.shape[-2:]).astype(BF16)

    def _gather_to(self, gname, stage, after):
        st = self.gathers[gname]
        if st["stage"] < 1 <= stage:
            shards = [self._shard(nm) for nm in st["names"]]
            lands = [_landing(s, kind, self.me) for s, kind in zip(shards, st["kinds"])]
            plan, ncopy = _plan_gather_chips(st["kinds"])
            st["flight"] = _copies_start(f"ag_{gname}_chips", shards, lands, ncopy, plan)
            st["stage"] = 1
        if st["stage"] < 2 <= stage:
            plan, ncopy = _plan_gather_chips(st["kinds"])
            _, lands = _copies_wait(f"ag_{gname}_chips_wait", st["flight"], ncopy, plan)
            plan, ncopy = _plan_gather_sibling(st["kinds"])
            st["flight"] = _copies_start(f"ag_{gname}_sibling", [], lands, ncopy, plan)
            st["stage"] = 2
        if st["stage"] < 3 <= stage:
            plan, ncopy = _plan_gather_sibling(st["kinds"])
            _, lands = _copies_wait(f"ag_{gname}_sibling_wait", st["flight"], ncopy, plan)
            for nm, land in zip(st["names"], lands):
                self._arrived(nm, land)
            st["stage"] = 3

    def _arrived(self, name, land):
        if name == "conv":
            conv = land.reshape(N_DEV, SMALL_W_ROWS * 128)
            self.ready["sc_conv_w"] = conv[:, :3 * 128].reshape(N_DEV, 3, 128).transpose(1, 0, 2).reshape(3, D)
            fcw = conv[:, 3 * 128:3 * 128 + 6 * 352].reshape(N_DEV, 2, 3, 352).transpose(1, 2, 0, 3)
            fcw = fcw.reshape(2, 3, N_FF_BLK, FF_BLK).transpose(0, 2, 1, 3)
            self.ready["ffn_cw0"], self.ready["ffn_cw1"] = fcw[0], fcw[1]
        elif name in ("sc_w_in", "w_uk", "w_uv") or name.startswith("ffn_w_up"):
            self.ready[name] = land
        elif name.startswith("ffn_w_down"):
            self.ready[name] = land.reshape(1, N_FF_BLK, FF_BLK, D)
        elif name == "w_kr":
            self.ready[name] = jnp.pad(land.reshape(D, QK_ROPE), ((0, 0), (0, 128 - QK_ROPE)))
        elif name == "w_uq":
            self.ready[name] = jnp.pad(land.reshape(N_HEADS, Q_LORA, QK_NOPE + QK_ROPE),
                                       ((0, 0), (0, 0), (0, QK_PAD - QK_NOPE - QK_ROPE)))
        else:
            self.ready[name] = land.reshape(D, land.shape[-1])

    def need(self, name, after):
        if name not in self.ready:
            self._gather_to(self.group_of[name], 3, after)
        return self.ready[name]

    def grad(self, name, layer, array):
        self.grads[(name, layer)] = array

    def _scatter_to(self, gname, stage, after):
        keys = dict(SCATTER_GROUPS)[gname]
        st = self.scatters.setdefault(gname, dict(stage=0))
        kinds = [KIND[nm] for nm, _ in keys]
        if st["stage"] < 1 <= stage:
            grads = [self.grads[key] for key in keys]
            lands = []
            for gr, kind in zip(grads, kinds):
                shard = (gr.shape[0],) + gr.shape[2:] if kind == "blocked" else (gr.shape[0], gr.shape[1] // N_DEV)
                lands.append(lax.empty((N_CHIP,) + shard, BF16))
            plan, ncopy = _plan_scatter_sibling(kinds)
            st["flight"] = _copies_start(f"rs_{gname}_sibling", grads, lands, ncopy, plan)
            st["stage"] = 1
        if st["stage"] < 2 <= stage:
            plan, ncopy = _plan_scatter_sibling(kinds)
            grads, recvs = _copies_wait(f"rs_{gname}_sibling_wait", st["flight"], ncopy, plan)
            sums = _chip_sums(f"rs_{gname}_sums", grads, kinds, recvs, self.c_arr)
            lands = [lax.empty(s.shape, BF16) for s in sums]
            plan, ncopy = _plan_scatter_chips(len(sums))
            st["flight"] = _copies_start(f"rs_{gname}_chips", sums, lands, ncopy, plan)
            st["stage"] = 2
        if st["stage"] < 3 <= stage:
            plan, ncopy = _plan_scatter_chips(len(keys))
            sums, recvs = _copies_wait(f"rs_{gname}_chips_wait", st["flight"], ncopy, plan)
            items = []
            for (nm, layer), own, rv in zip(keys, sums, recvs):
                nl = 1 if layer is None else 2
                rows, w = own.shape[1], own.shape[2]
                w3, m3, v3 = (_stored(nm, src[nm]).reshape(nl, rows, w) for src in (self.wts, self.mom, self.var))
                items.append((own, rv, w3, m3, v3, 0 if layer is None else layer, self.results.get(nm)))
            outs = _adamw_group(f"adamw_{gname}", items, self.chip_ids)
            for (nm, _), out in zip(keys, outs):
                self.results[nm] = out
            st["stage"] = 3

    def at(self, place, after):
        for action, gname in SCHEDULE.get(place, ()):
            self._advance(action, gname, after)

    def _advance(self, action, gname, after):
        if action.startswith("gather"):
            self._gather_to(gname, STAGES[action], after)
        else:
            self._scatter_to(gname, STAGES[action], after)

    def finish(self, after):
        for action, gname in FINISH:
            self._advance(action, gname, after)
        for gname, _ in SCATTER_GROUPS:
            self._scatter_to(gname, 3, after)
        return {nm: [_stored(nm, o.reshape(_stored(nm, self.wts[nm]).shape)) for o in outs]
                for nm, outs in self.results.items()}


REPLICATED = ("attn_norm", "ffn_norm", "final_norm", "kv_in_norm", "kv_latent_norm", "q_latent_norm", "ffn_conv_b")
WEIGHTS = ("attn_norm", "ffn_norm", "final_norm", "sc_w_in", "sc_conv_w", "sc_w_out", "kv_in_norm", "w_dkv",
           "kv_latent_norm", "w_kr", "w_uk", "w_uv", "w_dq", "q_latent_norm", "w_uq", "w_o", "ffn_w_up", "ffn_conv_w",
           "ffn_conv_b", "ffn_w_down")


def kernel(x, positions, attn_norm, ffn_norm, final_norm, sc_w_in, sc_conv_w, sc_w_out, kv_in_norm, w_dkv, kv_latent_norm, w_kr, w_uk, w_uv, w_dq, q_latent_norm, w_uq, w_o, ffn_w_up, ffn_conv_w, ffn_conv_b, ffn_w_down, loss_target, m_attn_norm, m_ffn_norm, m_final_norm, m_sc_w_in, m_sc_conv_w, m_sc_w_out, m_kv_in_norm, m_w_dkv, m_kv_latent_norm, m_w_kr, m_w_uk, m_w_uv, m_w_dq, m_q_latent_norm, m_w_uq, m_w_o, m_ffn_w_up, m_ffn_conv_w, m_ffn_conv_b, m_ffn_w_down, v_attn_norm, v_ffn_norm, v_final_norm, v_sc_w_in, v_sc_conv_w, v_sc_w_out, v_kv_in_norm, v_w_dkv, v_kv_latent_norm, v_w_kr, v_w_uk, v_w_uv, v_w_dq, v_q_latent_norm, v_w_uq, v_w_o, v_ffn_w_up, v_ffn_conv_w, v_ffn_conv_b, v_ffn_w_down):
    wts = dict(attn_norm=attn_norm, ffn_norm=ffn_norm, final_norm=final_norm, sc_w_in=sc_w_in, sc_conv_w=sc_conv_w,
               sc_w_out=sc_w_out, kv_in_norm=kv_in_norm, w_dkv=w_dkv, kv_latent_norm=kv_latent_norm, w_kr=w_kr,
               w_uk=w_uk, w_uv=w_uv, w_dq=w_dq, q_latent_norm=q_latent_norm, w_uq=w_uq, w_o=w_o, ffn_w_up=ffn_w_up,
               ffn_conv_w=ffn_conv_w, ffn_conv_b=ffn_conv_b, ffn_w_down=ffn_w_down)
    mom = dict(attn_norm=m_attn_norm, ffn_norm=m_ffn_norm, final_norm=m_final_norm, sc_w_in=m_sc_w_in,
               sc_conv_w=m_sc_conv_w, sc_w_out=m_sc_w_out, kv_in_norm=m_kv_in_norm, w_dkv=m_w_dkv,
               kv_latent_norm=m_kv_latent_norm, w_kr=m_w_kr, w_uk=m_w_uk, w_uv=m_w_uv, w_dq=m_w_dq,
               q_latent_norm=m_q_latent_norm, w_uq=m_w_uq, w_o=m_w_o, ffn_w_up=m_ffn_w_up, ffn_conv_w=m_ffn_conv_w,
               ffn_conv_b=m_ffn_conv_b, ffn_w_down=m_ffn_w_down)
    var = dict(attn_norm=v_attn_norm, ffn_norm=v_ffn_norm, final_norm=v_final_norm, sc_w_in=v_sc_w_in,
               sc_conv_w=v_sc_conv_w, sc_w_out=v_sc_w_out, kv_in_norm=v_kv_in_norm, w_dkv=v_w_dkv,
               kv_latent_norm=v_kv_latent_norm, w_kr=v_w_kr, w_uk=v_w_uk, w_uv=v_w_uv, w_dq=v_w_dq,
               q_latent_norm=v_q_latent_norm, w_uq=v_w_uq, w_o=v_w_o, ffn_w_up=v_ffn_w_up, ffn_conv_w=v_ffn_conv_w,
               ffn_conv_b=v_ffn_conv_b, ffn_w_down=v_ffn_w_down)
    xi, yi, ci = _place()
    me = 4 * xi + 2 * yi + ci
    _Chain.last = None

    ex = _Exchange(wts, mom, var, ffn_conv_b)
    rep = {
        "attn_norm": attn_norm, "ffn_norm": ffn_norm, "final_norm": final_norm,
        "kv_in_norm": kv_in_norm.reshape(1, D), "kv_latent_norm": kv_latent_norm.reshape(1, KV_LORA),
        "q_latent_norm": q_latent_norm.reshape(1, Q_LORA),
    }
    loss, grad_x, small = _local_step(x.reshape(T, D), positions.reshape(T, 1), loss_target.reshape(T, D), rep, ex)
    results = ex.finish(grad_x)

    def rows_of(a):
        return a.reshape(-1, a.shape[-1])

    small_order = list(REPLICATED) + ["sc_conv_w", "ffn_conv_w"]
    shards = [loss.reshape(1, 1, 128)] + [rows_of(small[nm])[None] for nm in small_order]
    gathered = _all_gather("ag_small_grads", [(s, "blocked") for s in shards])
    params = [[None] + [rows_of(src[nm]) for nm in REPLICATED] + [None, None] for src in (wts, mom, var)]
    summed = _adamw_small(gathered, *params)
    loss_total = summed[0][0][0, 0]
    for nm, vals in zip(REPLICATED, summed[1:1 + len(REPLICATED)]):
        results[nm] = [a.reshape(wts[nm].shape) for a in vals]
    g_scw = lax.dynamic_slice(summed[-2][0], (0, me * 128), (3, 128))
    g_fcw = lax.dynamic_slice(summed[-1][0], (0, me * 352), (6, 352))
    conv = _adamw_plain("adamw_conv", [g_scw, g_fcw], *[[rows_of(src["sc_conv_w"]), rows_of(src["ffn_conv_w"])]
                                                        for src in (wts, mom, var)])
    for nm, g_own, vals in zip(("sc_conv_w", "ffn_conv_w"), (g_scw, g_fcw), conv):
        results[nm] = [a.reshape(wts[nm].shape) for a in [g_own] + vals]

    outs = [loss_total, grad_x.reshape(1, T, D)]
    for slot in range(4):
        outs.extend(results[nm][slot] for nm in WEIGHTS)
    return tuple(outs)
```

```python
import jax
import jax.numpy as jnp
from jax import lax
from jax.experimental import pallas as pl
from jax.experimental.pallas import tpu as pltpu

F32 = jnp.float32
BF16 = jnp.bfloat16

T = 2048
D = 1024
N_HEADS = 8
QK_NOPE = 128
QK_ROPE = 64
V_HEAD = 128
Q_LORA = 384
KV_LORA = 256
D_FF = 2816
CHUNK = 64
ROPE_THETA = 10000.0
EPS = 1e-6
NEG_INF = -1e30
ADAM_LR = 0.001
ADAM_B1 = 0.9
ADAM_B2 = 0.999
ADAM_EPS = 1e-08
ADAM_WD = 0.01
ADAM_STEP = 10

N_DEV = 8
N_CHIP = 4
FF_BLK = D_FF * 2 // N_DEV
N_FF_BLK = D_FF // FF_BLK
QK_PAD = 256
HALO = 16

TM = 1024
TS = 512
TR = 256
TQ = 512
VMEM_LIMIT = 56 * 1024 * 1024

NN = (((1,), (0,)), ((), ()))
NT = (((1,), (1,)), ((), ()))
TN = (((0,), (0,)), ((), ()))
MESH = pl.DeviceIdType.MESH


def _params(sem):
    return pltpu.CompilerParams(dimension_semantics=sem, vmem_limit_bytes=VMEM_LIMIT)


ANY_SPEC = pl.BlockSpec(memory_space=pl.ANY)
VMEM_SPEC = pl.BlockSpec(memory_space=pltpu.VMEM)


class _Chain:
    last = None


def _pallas(body, *, name, in_specs, out_specs, out_shape, grid=(), scratch_shapes=(), n_prefetch=0, aliases=None,
            params=None):
    def run(*args):
        after = _Chain.last
        n_lead = len(args)
        specs, operands, fn = list(in_specs), list(args), body
        if after is not None:
            def fn(*refs):
                return body(*refs[:n_lead], *refs[n_lead + 1:])
            specs.append(ANY_SPEC)
            operands.append(after)
        kw = dict(name=name, out_shape=out_shape, input_output_aliases=aliases or {})
        if params is not None:
            kw["compiler_params"] = params
        if n_prefetch:
            kw["grid_spec"] = pltpu.PrefetchScalarGridSpec(
                num_scalar_prefetch=n_prefetch, grid=grid, in_specs=specs, out_specs=out_specs,
                scratch_shapes=scratch_shapes)
        else:
            kw.update(grid=grid, in_specs=specs, out_specs=out_specs, scratch_shapes=scratch_shapes)
        outs = pl.pallas_call(fn, **kw)(*operands)
        _Chain.last = outs[0] if isinstance(outs, (list, tuple)) else outs
        return outs
    return run


def _mm(name, a, b, *, grid, a_spec, b_spec, o_spec, o_shape, o_dtype, dims, k_axis=None, acc_shape=None,
        add=None, add_spec=None):
    nk = grid[k_axis] if k_axis is not None else 1
    has_add = add is not None

    def body(*refs):
        a_ref, b_ref = refs[0], refs[1]
        p = 2
        add_ref = None
        if has_add:
            add_ref = refs[p]
            p += 1
        o_ref = refs[p]
        p += 1
        r = lax.dot_general(a_ref[...].astype(BF16), b_ref[...].astype(BF16), dims, preferred_element_type=F32)
        if k_axis is None:
            if has_add:
                r = r + add_ref[...].astype(F32)
            o_ref[...] = r.astype(o_dtype)
        else:
            acc = refs[p]
            k = pl.program_id(k_axis)

            @pl.when(k == 0)
            def _():
                acc[...] = r

            @pl.when(k > 0)
            def _():
                acc[...] += r

            @pl.when(k == nk - 1)
            def _():
                t = acc[...]
                if has_add:
                    t = t + add_ref[...].astype(F32)
                o_ref[...] = t.astype(o_dtype)

    in_specs = [a_spec, b_spec]
    args = [a, b]
    if has_add:
        in_specs.append(add_spec if add_spec is not None else o_spec)
        args.append(add)
    sem = tuple("arbitrary" if ax == k_axis else "parallel" for ax in range(len(grid)))
    scratch = [pltpu.VMEM(acc_shape, F32)] if k_axis is not None else []
    return _pallas(body, name=name, grid=grid, in_specs=in_specs, out_specs=o_spec,
                   out_shape=jax.ShapeDtypeStruct(o_shape, o_dtype), scratch_shapes=scratch, params=_params(sem))(*args)


def _mm_sum(name, parts, *, grid, o_spec, o_shape, o_dtype, add=None, norm_bwd=None):
    has_add = add is not None
    np_ = len(parts)
    nn = 1 if norm_bwd is None else len(norm_bwd[1])
    has_res = norm_bwd is not None and norm_bwd[2] is not None

    def body(*refs):
        accs = [None] * nn
        for p, (_, _, _, _, dims, n) in enumerate(parts):
            a_ref, b_ref = refs[2 * p], refs[2 * p + 1]
            for k in range(a_ref.shape[0]):
                r = lax.dot_general(a_ref[k], b_ref[k], dims, preferred_element_type=F32)
                accs[n] = r if accs[n] is None else accs[n] + r
        if norm_bwd is None:
            acc = accs[0]
            if has_add:
                acc = acc + refs[2 * np_][...]
            refs[-1][...] = acc.astype(o_dtype)
            return
        x_ref, g_refs = refs[2 * np_], refs[2 * np_ + 1:2 * np_ + 1 + nn]
        dx_ref, dxb_ref, dg_refs = refs[-2 - nn], refs[-1 - nn], refs[-nn:]
        xv = x_ref[...]
        r = lax.rsqrt(jnp.mean(xv * xv, axis=-1, keepdims=True) + EPS)
        xn = xv * r
        dx = refs[2 * np_ + 1 + nn][...] if has_res else None
        sums = []
        for acc, g_ref in zip(accs, g_refs):
            gdy = acc * g_ref[...]
            t = r * (gdy - xn * jnp.mean(gdy * xn, axis=-1, keepdims=True))
            dx = t if dx is None else dx + t
            sums.append(jnp.sum(acc * xn, axis=0, keepdims=True))
        dx_ref[...] = dx
        dxb_ref[...] = dx.astype(BF16)

        @pl.when(pl.program_id(0) == 0)
        def _():
            for dg_ref, part in zip(dg_refs, sums):
                dg_ref[...] = part

        @pl.when(pl.program_id(0) > 0)
        def _():
            for dg_ref, part in zip(dg_refs, sums):
                dg_ref[...] += part

    in_specs, args = [], []
    for a, a_spec, b, b_spec, _, _ in parts:
        in_specs += [a_spec, b_spec]
        args += [a, b]
    if norm_bwd is None:
        if has_add:
            in_specs.append(o_spec)
            args.append(add)
        return _pallas(body, name=name, grid=grid, in_specs=in_specs, out_specs=o_spec,
                       out_shape=jax.ShapeDtypeStruct(o_shape, o_dtype),
                       params=_params(("parallel",) * len(grid)))(*args)
    x, gains, dres = norm_bwd
    vec = pl.BlockSpec((1, o_shape[1]), lambda i: (0, 0))
    in_specs += [o_spec] + [vec] * nn + ([o_spec] if has_res else [])
    args += [x] + list(gains) + ([dres] if has_res else [])
    outs = _pallas(body, name=name, grid=grid, in_specs=in_specs, out_specs=[o_spec, o_spec] + [vec] * nn,
                   out_shape=[jax.ShapeDtypeStruct(o_shape, F32), jax.ShapeDtypeStruct(o_shape, BF16)]
                   + [jax.ShapeDtypeStruct((1, o_shape[1]), F32)] * nn,
                   params=_params(("arbitrary",)))(*args)
    return outs[0], outs[1], list(outs[2:])


def _mm_rows(name, a, b, dims, o_dtype, n_out, *, tn=None, add=None):
    k = a.shape[1]
    tn = n_out if tn is None else tn
    if dims == NN:
        b_spec = pl.BlockSpec((k, tn), lambda n, i: (0, n))
    else:
        b_spec = pl.BlockSpec((tn, k), lambda n, i: (n, 0))
    return _mm(name, a, b, grid=(n_out // tn, T // TM),
               a_spec=pl.BlockSpec((TM, k), lambda n, i: (i, 0)), b_spec=b_spec,
               o_spec=pl.BlockSpec((TM, tn), lambda n, i: (i, n)), o_shape=(T, n_out), o_dtype=o_dtype,
               dims=dims, add=add)


def _mm_wgrad(name, a, b, *, tn=512):
    k, n = a.shape[1], b.shape[1]
    tn = min(tn, n)
    return _mm(name, a, b, grid=(n // tn,),
               a_spec=pl.BlockSpec((T, k), lambda j: (0, 0)), b_spec=pl.BlockSpec((T, tn), lambda j: (0, j)),
               o_spec=pl.BlockSpec((k, tn), lambda j: (0, j)), o_shape=(k, n), o_dtype=BF16, dims=TN)


def _rms_fwd(name, x, g):
    d = x.shape[1]

    def body(x_ref, g_ref, o_ref):
        xv = x_ref[...]
        r = lax.rsqrt(jnp.mean(xv * xv, axis=-1, keepdims=True) + EPS)
        o_ref[...] = ((xv * r) * g_ref[...]).astype(BF16)

    return _pallas(
        body, name=name, grid=(T // TM,),
        in_specs=[pl.BlockSpec((TM, d), lambda i: (i, 0)), pl.BlockSpec((1, d), lambda i: (0, 0))],
        out_specs=pl.BlockSpec((TM, d), lambda i: (i, 0)),
        out_shape=jax.ShapeDtypeStruct((T, d), BF16), params=_params(("parallel",)))(x, g)


def _rows_call(name, body, row_ins, whole_ins, outs):
    in_specs = [pl.BlockSpec((TM, a.shape[1]), lambda i: (i, 0)) for a in row_ins]
    in_specs += [pl.BlockSpec(a.shape, lambda i: (0, 0)) for a in whole_ins]
    return _pallas(
        body, name=name, grid=(T // TM,), in_specs=in_specs,
        out_specs=[pl.BlockSpec((TM, d), lambda i: (i, 0)) for d, _ in outs],
        out_shape=[jax.ShapeDtypeStruct((T, d), dt) for d, dt in outs],
        params=_params(("parallel",)))(*row_ins, *whole_ins)


def _rms(xv, g):
    return (xv * lax.rsqrt(jnp.mean(xv * xv, axis=-1, keepdims=True) + EPS)) * g


def _rms_fwd2(name, x, g1, g2):
    d = x.shape[1]

    def body(x_ref, g1_ref, g2_ref, o1_ref, o2_ref):
        xv = x_ref[...]
        xn = xv * lax.rsqrt(jnp.mean(xv * xv, axis=-1, keepdims=True) + EPS)
        o1_ref[...] = (xn * g1_ref[...]).astype(BF16)
        o2_ref[...] = (xn * g2_ref[...]).astype(BF16)

    return _rows_call(name, body, [x], [g1, g2], [(d, BF16), (d, BF16)])


def _down_norm(name, a, w, g):
    n = w.shape[1]

    def body(a_ref, w_ref, g_ref, raw_ref, o_ref):
        raw = lax.dot_general(a_ref[...], w_ref[...], NN, preferred_element_type=F32)
        raw_ref[...] = raw
        o_ref[...] = _rms(raw, g_ref[...]).astype(BF16)

    return _rows_call(name, body, [a], [w, g], [(n, F32), (n, BF16)])


def _kv_down(hk, w_dkv, w_kr, g, tables):
    def body(a_ref, c_ref, sa_ref, sb_ref, wd_ref, wr_ref, g_ref, raw_ref, ckv_ref, kr_ref):
        av = a_ref[...]
        raw = lax.dot_general(av, wd_ref[...], NN, preferred_element_type=F32)
        raw_ref[...] = raw
        ckv_ref[...] = _rms(raw, g_ref[...]).astype(BF16)
        kr = lax.dot_general(av, wr_ref[...], NN, preferred_element_type=F32)
        kr_ref[...] = _rotate(kr, c_ref[...], sa_ref[...], sb_ref[...], 1.0).astype(BF16)

    return _rows_call("kv_down", body, [hk, *tables], [w_dkv, w_kr, g], [(KV_LORA, F32), (KV_LORA, BF16), (128, BF16)])


def _kv_up(ckv, w_uk, w_uv):
    def body(a_ref, wk_ref, wv_ref, k_ref, v_ref):
        av = a_ref[...]
        k_ref[...] = lax.dot_general(av, wk_ref[...], NN, preferred_element_type=F32).astype(BF16)
        v_ref[...] = lax.dot_general(av, wv_ref[...], NN, preferred_element_type=F32).astype(BF16)

    return _rows_call("kv_up", body, [ckv], [w_uk, w_uv], [(N_HEADS * QK_NOPE, BF16), (N_HEADS * V_HEAD, BF16)])


def _rms_bwd(name, x, gains, dys, dres=None):
    d = x.shape[1]
    n = len(gains)
    has_res = dres is not None

    def body(*refs):
        x_ref, g_refs, dy_refs = refs[0], refs[1:1 + n], refs[1 + n:1 + 2 * n]
        dx_ref, dxb_ref = refs[-2 - n], refs[-1 - n]
        dg_refs = refs[-n:]
        xv = x_ref[...]
        r = lax.rsqrt(jnp.mean(xv * xv, axis=-1, keepdims=True) + EPS)
        xn = xv * r
        dx = refs[1 + 2 * n][...] if has_res else None
        parts = []
        for g_ref, dy_ref in zip(g_refs, dy_refs):
            dyv = dy_ref[...].astype(F32)
            gdy = dyv * g_ref[...]
            t = r * (gdy - xn * jnp.mean(gdy * xn, axis=-1, keepdims=True))
            dx = t if dx is None else dx + t
            parts.append(jnp.sum(dyv * xn, axis=0, keepdims=True))
        dx_ref[...] = dx
        dxb_ref[...] = dx.astype(BF16)

        @pl.when(pl.program_id(0) == 0)
        def _():
            for dg_ref, part in zip(dg_refs, parts):
                dg_ref[...] = part

        @pl.when(pl.program_id(0) > 0)
        def _():
            for dg_ref, part in zip(dg_refs, parts):
                dg_ref[...] += part

    row = pl.BlockSpec((TR, d), lambda i: (i, 0))
    vec = pl.BlockSpec((1, d), lambda i: (0, 0))
    args = [x] + list(gains) + list(dys) + ([dres] if has_res else [])
    in_specs = [row] + [vec] * n + [row] * n + ([row] if has_res else [])
    outs = _pallas(
        body, name=name, grid=(T // TR,), in_specs=in_specs, out_specs=[row, row] + [vec] * n,
        out_shape=[jax.ShapeDtypeStruct((T, d), F32), jax.ShapeDtypeStruct((T, d), BF16)]
        + [jax.ShapeDtypeStruct((1, d), F32)] * n,
        params=_params(("arbitrary",)))(*args)
    return outs[0], outs[1], list(outs[2:])


def _final(h, g, tgt):
    def body(h_ref, g_ref, t_ref, loss_ref, dh_ref, dhb_ref, dg_ref):
        hv = h_ref[...]
        r = lax.rsqrt(jnp.mean(hv * hv, axis=-1, keepdims=True) + EPS)
        xn = hv * r
        gv = g_ref[...]
        err = xn * gv - t_ref[...]
        part_loss = 0.5 * jnp.sum(jnp.mean(err * err, axis=-1, keepdims=True), axis=0, keepdims=True)
        dy = err * (1.0 / D)
        gdy = dy * gv
        dh = r * (gdy - xn * jnp.mean(gdy * xn, axis=-1, keepdims=True))
        dh_ref[...] = dh
        dhb_ref[...] = dh.astype(BF16)
        part = jnp.sum(dy * xn, axis=0, keepdims=True)
        first = pl.program_id(0) == 0

        @pl.when(first)
        def _():
            dg_ref[...] = part
            loss_ref[...] = jnp.broadcast_to(part_loss, (1, 128))

        @pl.when(jnp.logical_not(first))
        def _():
            dg_ref[...] += part
            loss_ref[...] += jnp.broadcast_to(part_loss, (1, 128))

    row = pl.BlockSpec((TR, D), lambda i: (i, 0))
    vec = pl.BlockSpec((1, D), lambda i: (0, 0))
    return _pallas(
        body, name="final_loss", grid=(T // TR,), in_specs=[row, vec, row],
        out_specs=[pl.BlockSpec((1, 128), lambda i: (0, 0)), row, row, vec],
        out_shape=[jax.ShapeDtypeStruct((1, 128), F32), jax.ShapeDtypeStruct((T, D), F32),
                   jax.ShapeDtypeStruct((T, D), BF16), jax.ShapeDtypeStruct((1, D), F32)],
        params=_params(("arbitrary",)))(h, g, tgt)


def _prev_idx(i, rows=TR):
    return jnp.maximum(i * (rows // HALO) - 1, 0)


def _next_idx(i, rows=TR):
    return jnp.minimum((i + 1) * (rows // HALO), T // HALO - 1)


def _causal_taps(ext):
    return pltpu.roll(ext, 2, 0)[HALO:], pltpu.roll(ext, 1, 0)[HALO:], ext[HALO:]


def _anticausal_taps(ext, n):
    rows = ext.shape[0]
    return pltpu.roll(ext, rows - 1, 0)[:n], pltpu.roll(ext, rows - 2, 0)[:n]


def _sc_fwd(z, w):
    def body(b_ref, c_ref, ch_ref, u_ref, uh_ref, w_ref, y_ref):
        i = pl.program_id(0)
        cu = c_ref[...].astype(F32) * u_ref[...].astype(F32)
        cuh = ch_ref[...].astype(F32) * uh_ref[...].astype(F32)
        cuh = jnp.where(i > 0, cuh, 0.0)
        x2, x1, x0 = _causal_taps(jnp.concatenate([cuh, cu], axis=0))
        wv = w_ref[...]
        cv = (x2 * wv[0:1] + x1 * wv[1:2]) + x0 * wv[2:3]
        y_ref[...] = (b_ref[...].astype(F32) * cv).astype(BF16)

    def main(part):
        return pl.BlockSpec((TR, D), lambda i: (i, part))

    def halo(part):
        return pl.BlockSpec((HALO, D), lambda i: (_prev_idx(i), part))

    return _pallas(
        body, name="sc_fwd", grid=(T // TR,),
        in_specs=[main(0), main(1), halo(1), main(2), halo(2), pl.BlockSpec((3, D), lambda i: (0, 0))],
        out_specs=pl.BlockSpec((TR, D), lambda i: (i, 0)),
        out_shape=jax.ShapeDtypeStruct((T, D), BF16), params=_params(("parallel",)))(z, z, z, z, z, w)


def _sc_bwd(z, dy, w):
    last = T // TR - 1

    def body(b_ref, bn_ref, c_ref, ch_ref, u_ref, uh_ref, dy_ref, dyn_ref, w_ref, dz_ref, dw_ref):
        i = pl.program_id(0)
        cv_ = c_ref[...].astype(F32)
        uv = u_ref[...].astype(F32)
        cu = cv_ * uv
        cuh = jnp.where(i > 0, ch_ref[...].astype(F32) * uh_ref[...].astype(F32), 0.0)
        x2, x1, x0 = _causal_taps(jnp.concatenate([cuh, cu], axis=0))
        wv = w_ref[...]
        conv = (x2 * wv[0:1] + x1 * wv[1:2]) + x0 * wv[2:3]
        dyv = dy_ref[...]
        dz_ref[:, 0:D] = (dyv * conv).astype(BF16)
        dconv = dyv * b_ref[...].astype(F32)
        dconv_n = jnp.where(i < last, dyn_ref[...] * bn_ref[...].astype(F32), 0.0)
        n1, n2 = _anticausal_taps(jnp.concatenate([dconv, dconv_n], axis=0), TR)
        dcu = (dconv * wv[2:3] + n1 * wv[1:2]) + n2 * wv[0:1]
        dz_ref[:, D:2 * D] = (dcu * uv).astype(BF16)
        dz_ref[:, 2 * D:3 * D] = (dcu * cv_).astype(BF16)
        part = jnp.concatenate([jnp.sum(dconv * x2, axis=0, keepdims=True),
                                jnp.sum(dconv * x1, axis=0, keepdims=True),
                                jnp.sum(dconv * x0, axis=0, keepdims=True)], axis=0)

        @pl.when(i == 0)
        def _():
            dw_ref[...] = part

        @pl.when(i > 0)
        def _():
            dw_ref[...] += part

    def main(part):
        return pl.BlockSpec((TR, D), lambda i: (i, part))

    def prev(part):
        return pl.BlockSpec((HALO, D), lambda i: (_prev_idx(i), part))

    def nxt(part):
        return pl.BlockSpec((HALO, D), lambda i: (_next_idx(i), part))

    wspec = pl.BlockSpec((3, D), lambda i: (0, 0))
    return _pallas(
        body, name="sc_bwd", grid=(T // TR,),
        in_specs=[main(0), nxt(0), main(1), prev(1), main(2), prev(2), main(0), nxt(0), wspec],
        out_specs=[pl.BlockSpec((TR, 3 * D), lambda i: (i, 0)), wspec],
        out_shape=[jax.ShapeDtypeStruct((T, 3 * D), BF16), jax.ShapeDtypeStruct((3, D), F32)],
        params=_params(("arbitrary",)))(z, z, z, z, z, z, dy, dy, w)


def _sigmoid(x):
    return 1.0 / (1.0 + jnp.exp(-x))


def _ffn_up_act(name, hf, w_up, w, b):
    def body(h_ref, hh_ref, wg_ref, wv_ref, w_ref, b_ref, g_ref, v_ref, a_ref):
        i = pl.program_id(1)
        wg = wg_ref[...]
        g = lax.dot_general(h_ref[...], wg, NT, preferred_element_type=F32).astype(BF16)
        gh = lax.dot_general(hh_ref[...], wg, NT, preferred_element_type=F32).astype(BF16)
        v = lax.dot_general(h_ref[...], wv_ref[...], NT, preferred_element_type=F32).astype(BF16)
        g_ref[...] = g
        v_ref[...] = v
        gh = jnp.where(i > 0, gh.astype(F32), 0.0)
        x2, x1, x0 = _causal_taps(jnp.concatenate([gh, g.astype(F32)], axis=0))
        wv = w_ref[...]
        gc = ((x2 * wv[0:1] + x1 * wv[1:2]) + x0 * wv[2:3]) + b_ref[...]
        a_ref[...] = ((gc * _sigmoid(gc)) * v.astype(F32)).astype(BF16)

    blk = pl.BlockSpec((None, TS, FF_BLK), lambda j, i: (j, i, 0))
    out = jax.ShapeDtypeStruct((N_FF_BLK, T, FF_BLK), BF16)
    return _pallas(
        body, name=name, grid=(N_FF_BLK, T // TS),
        in_specs=[pl.BlockSpec((TS, D), lambda j, i: (i, 0)),
                  pl.BlockSpec((HALO, D), lambda j, i: (_prev_idx(i, TS), 0)),
                  pl.BlockSpec((None, None, FF_BLK, D), lambda j, i: (0, j, 0, 0)),
                  pl.BlockSpec((None, None, FF_BLK, D), lambda j, i: (0, j + N_FF_BLK, 0, 0)),
                  pl.BlockSpec((None, 3, FF_BLK), lambda j, i: (j, 0, 0)),
                  pl.BlockSpec((None, 1, FF_BLK), lambda j, i: (j, 0, 0))],
        out_specs=[blk, blk, blk], out_shape=[out, out, out],
        params=_params(("parallel", "parallel")))(hf, hf, w_up, w_up, w, b)


def _ffn_dact(name, dh, w_down4, g, v, w, b):
    last = T // TS - 1

    def body(dh_ref, dhn_ref, wd_ref, g_ref, gp_ref, gn_ref, v_ref, vn_ref, w_ref, b_ref, dg_ref, dv_ref, dw_ref, db_ref):
        i = pl.program_id(1)
        wd = wd_ref[...]
        da = lax.dot_general(dh_ref[...], wd, NT, preferred_element_type=F32)
        dan = lax.dot_general(dhn_ref[...], wd, NT, preferred_element_type=F32)
        da = jnp.concatenate([da, jnp.where(i < last, dan, 0.0)], axis=0)
        gp = jnp.where(i > 0, gp_ref[...].astype(F32), 0.0)
        ext = jnp.concatenate([gp, g_ref[...].astype(F32), gn_ref[...].astype(F32)], axis=0)
        x2, x1, x0 = _causal_taps(ext)
        wv = w_ref[...]
        gc = ((x2 * wv[0:1] + x1 * wv[1:2]) + x0 * wv[2:3]) + b_ref[...]
        sg = _sigmoid(gc)
        vv = jnp.concatenate([v_ref[...].astype(F32), vn_ref[...].astype(F32)], axis=0)
        dv_ref[...] = (da[:TS] * (gc[:TS] * sg[:TS])).astype(BF16)
        dgc = (da * vv) * (sg * (1.0 + gc * (1.0 - sg)))
        n1, n2 = _anticausal_taps(dgc, TS)
        d0 = dgc[:TS]
        dg_ref[...] = ((d0 * wv[2:3] + n1 * wv[1:2]) + n2 * wv[0:1]).astype(BF16)
        part_w = jnp.concatenate([jnp.sum(d0 * x2[:TS], axis=0, keepdims=True),
                                  jnp.sum(d0 * x1[:TS], axis=0, keepdims=True),
                                  jnp.sum(d0 * x0[:TS], axis=0, keepdims=True)], axis=0)
        part_b = jnp.sum(d0, axis=0, keepdims=True)

        @pl.when(i == 0)
        def _():
            dw_ref[...] = part_w
            db_ref[...] = part_b

        @pl.when(i > 0)
        def _():
            dw_ref[...] += part_w
            db_ref[...] += part_b

    blk = pl.BlockSpec((None, TS, FF_BLK), lambda j, i: (j, i, 0))
    prev = pl.BlockSpec((None, HALO, FF_BLK), lambda j, i: (j, _prev_idx(i, TS), 0))
    nxt = pl.BlockSpec((None, HALO, FF_BLK), lambda j, i: (j, _next_idx(i, TS), 0))
    wspec = pl.BlockSpec((None, 3, FF_BLK), lambda j, i: (j, 0, 0))
    bspec = pl.BlockSpec((None, 1, FF_BLK), lambda j, i: (j, 0, 0))
    return _pallas(
        body, name=name, grid=(N_FF_BLK, T // TS),
        in_specs=[pl.BlockSpec((TS, D), lambda j, i: (i, 0)),
                  pl.BlockSpec((HALO, D), lambda j, i: (_next_idx(i, TS), 0)),
                  pl.BlockSpec((None, None, FF_BLK, D), lambda j, i: (0, j, 0, 0)),
                  blk, prev, nxt, blk, nxt, wspec, bspec],
        out_specs=[blk, blk, wspec, bspec],
        out_shape=[jax.ShapeDtypeStruct((N_FF_BLK, T, FF_BLK), BF16), jax.ShapeDtypeStruct((N_FF_BLK, T, FF_BLK), BF16),
                   jax.ShapeDtypeStruct((N_FF_BLK, 3, FF_BLK), F32), jax.ShapeDtypeStruct((N_FF_BLK, 1, FF_BLK), F32)],
        params=_params(("parallel", "arbitrary")))(dh, dh, w_down4, g, g, g, v, v, w, b)


def _rope_tables(pos, inv_freq):
    half = QK_ROPE // 2

    def body(p_ref, f_ref, c_ref, sa_ref, sb_ref):
        ang = p_ref[...].astype(F32) * f_ref[...]
        lane = lax.broadcasted_iota(jnp.int32, (T, 128), 1)
        c = jnp.cos(ang)
        s = jnp.sin(ang)
        c_ref[...] = jnp.where(lane < 2 * half, c, 0.0)
        sa_ref[...] = jnp.where(lane < half, -s, 0.0)
        sb_ref[...] = jnp.where(jnp.logical_and(lane >= half, lane < 2 * half), s, 0.0)

    return _pallas(
        body, name="rope_tables", in_specs=[VMEM_SPEC] * 2, out_specs=[VMEM_SPEC] * 3,
        out_shape=[jax.ShapeDtypeStruct((T, 128), F32)] * 3,
        params=pltpu.CompilerParams(vmem_limit_bytes=VMEM_LIMIT))(pos, inv_freq)


def _rotate(r, c, sa, sb, sign):
    return r * c + sign * (pltpu.roll(r, 96, 1) * sa + pltpu.roll(r, 32, 1) * sb)


def _q_up(cq, w_uq, tables):
    cos, sa, sb = tables

    def body(a_ref, b_ref, c_ref, sa_ref, sb_ref, o_ref):
        for h in range(N_HEADS):
            r = lax.dot_general(a_ref[...], b_ref[h], NN, preferred_element_type=F32)
            o_ref[h, :, :QK_NOPE] = r[:, :QK_NOPE].astype(BF16)
            o_ref[h, :, QK_NOPE:] = _rotate(r[:, QK_NOPE:], c_ref[...], sa_ref[...], sb_ref[...], 1.0).astype(BF16)

    tab = pl.BlockSpec((TS, 128), lambda i: (i, 0))
    return _pallas(
        body, name="q_up", grid=(T // TS,),
        in_specs=[pl.BlockSpec((TS, Q_LORA), lambda i: (i, 0)),
                  pl.BlockSpec((N_HEADS, Q_LORA, QK_PAD), lambda i: (0, 0, 0)), tab, tab, tab],
        out_specs=pl.BlockSpec((N_HEADS, TS, QK_PAD), lambda i: (0, i, 0)),
        out_shape=jax.ShapeDtypeStruct((N_HEADS, T, QK_PAD), BF16),
        params=_params(("parallel",)))(cq, w_uq, cos, sa, sb)


def _rope(name, x, tables, sign, out_dtype, reduce_groups=False):
    g, _, w = x.shape
    cos, sa, sb = tables

    def body(x_ref, c_ref, sa_ref, sb_ref, o_ref):
        xv = x_ref[...].astype(F32)
        if reduce_groups:
            acc = xv[0]
            for k in range(1, g):
                acc = acc + xv[k]
            xv = acc
        out = _rotate(xv[:, w - 128:], c_ref[...], sa_ref[...], sb_ref[...], sign)
        if w > 128:
            o_ref[:, :w - 128] = xv[:, :w - 128].astype(out_dtype)
        o_ref[:, w - 128:] = out.astype(out_dtype)

    tab = pl.BlockSpec((TM, 128), lambda h, i: (i, 0))
    if reduce_groups:
        x_spec = pl.BlockSpec((g, TM, w), lambda h, i: (0, i, 0))
        groups = 1
    else:
        x_spec = pl.BlockSpec((None, TM, w), lambda h, i: (h, i, 0))
        groups = g
    return _pallas(
        body, name=name, grid=(groups, T // TM), in_specs=[x_spec, tab, tab, tab],
        out_specs=pl.BlockSpec((None, TM, w), lambda h, i: (h, i, 0)),
        out_shape=jax.ShapeDtypeStruct((groups, T, w), out_dtype),
        params=_params(("parallel", "parallel")))(x, cos, sa, sb)


SCALE = (QK_NOPE + QK_ROPE) ** -0.5
LOG2E = 1.4426950408889634
SCALE2 = SCALE * LOG2E


def _diag_mask(transposed):
    shift = CHUNK.bit_length() - 1
    a = lax.broadcasted_iota(jnp.int32, (TQ, TQ), 0) >> shift
    b = lax.broadcasted_iota(jnp.int32, (TQ, TQ), 1) >> shift
    return (a <= b) if transposed else (b <= a)


def _as_row(col):
    return jnp.transpose(jnp.broadcast_to(col, (col.shape[0], 128)), (1, 0))[0:1]


def _keys(kn_ref, kr_ref, off):
    return jnp.concatenate([kn_ref[pl.ds(off, TQ), :], kr_ref[pl.ds(off, TQ), :]], axis=1)


def _attn_fwd(q, kn, kr, v):
    def body(q_ref, kn_ref, kr_ref, v_ref, o_ref, lse_ref):
        i = pl.program_id(1)
        qv = q_ref[...]

        def step(j, carry, masked):
            m, l, acc = carry
            off = pl.multiple_of(j * TQ, TQ)
            s = lax.dot_general(qv, _keys(kn_ref, kr_ref, off), NT, preferred_element_type=F32) * SCALE2
            if masked:
                s = jnp.where(_diag_mask(False), s, NEG_INF)
            m_new = jnp.maximum(m, jnp.max(s, axis=-1, keepdims=True))
            p = jnp.exp2(s - m_new)
            alpha = jnp.exp2(m - m_new)
            l = alpha * l + jnp.sum(p, axis=-1, keepdims=True)
            acc = alpha * acc + lax.dot_general(p.astype(BF16), v_ref[pl.ds(off, TQ), :], NN, preferred_element_type=F32)
            return m_new, l, acc

        init = (jnp.full((TQ, 1), NEG_INF, F32), jnp.zeros((TQ, 1), F32), jnp.zeros((TQ, V_HEAD), F32))
        carry = lax.fori_loop(0, i, lambda j, cr: step(j, cr, False), init)
        m, l, acc = step(i, carry, True)
        o_ref[...] = (acc / l).astype(BF16)
        lse_ref[...] = _as_row(m + jnp.log(l) * LOG2E)

    return _pallas(
        body, name="attn_fwd", grid=(N_HEADS, T // TQ),
        in_specs=[pl.BlockSpec((None, TQ, QK_PAD), lambda h, i: (h, i, 0)),
                  pl.BlockSpec((T, QK_NOPE), lambda h, i: (0, h)),
                  pl.BlockSpec((T, 128), lambda h, i: (0, 0)),
                  pl.BlockSpec((T, V_HEAD), lambda h, i: (0, h))],
        out_specs=[pl.BlockSpec((TQ, V_HEAD), lambda h, i: (i, h)), pl.BlockSpec((None, 1, TQ), lambda h, i: (h, 0, i))],
        out_shape=[jax.ShapeDtypeStruct((T, N_HEADS * V_HEAD), BF16), jax.ShapeDtypeStruct((N_HEADS, 1, T), F32)],
        params=_params(("parallel", "parallel")))(q, kn, kr, v)


def _attn_bwd(q, kn, kr, v, o, do, lse_row, tables):
    nq = T // TQ
    cos, sa, sb = tables

    def body(q_ref, kn_ref, kr_ref, v_ref, o_ref, do_ref, lse_ref, c_ref, sa_ref, sb_ref,
             dq_ref, dkn_ref, dkr_ref, dv_ref, dq_acc, dl_ref):
        j = pl.program_id(1)

        @pl.when(j == 0)
        def _():
            dq_acc[...] = jnp.zeros_like(dq_acc)
            for i in range(nq):
                rows = pl.ds(i * TQ, TQ)
                prod = do_ref[rows, :].astype(F32) * o_ref[rows, :].astype(F32)
                dl_ref[:, rows] = _as_row(jnp.sum(prod, axis=-1, keepdims=True))

        kk = jnp.concatenate([kn_ref[...], kr_ref[...]], axis=1)
        vv = v_ref[...]

        def step(i, carry, masked):
            dk, dv = carry
            off = pl.multiple_of(i * TQ, TQ)
            qi = q_ref[pl.ds(off, TQ), :]
            doi = do_ref[pl.ds(off, TQ), :]
            st = lax.dot_general(kk, qi, NT, preferred_element_type=F32) * SCALE2
            if masked:
                st = jnp.where(_diag_mask(True), st, NEG_INF)
            pt = jnp.exp2(st - lse_ref[:, pl.ds(off, TQ)])
            dv = dv + lax.dot_general(pt.astype(BF16), doi, NN, preferred_element_type=F32)
            dpt = lax.dot_general(vv, doi, NT, preferred_element_type=F32)
            dst = ((pt * (dpt - dl_ref[:, pl.ds(off, TQ)])) * SCALE).astype(BF16)
            dk = dk + lax.dot_general(dst, qi, NN, preferred_element_type=F32)
            dq_acc[pl.ds(off, TQ), :] += lax.dot_general(dst, kk, TN, preferred_element_type=F32)
            return dk, dv

        carry = step(j, (jnp.zeros((TQ, QK_PAD), F32), jnp.zeros((TQ, V_HEAD), F32)), True)
        dk, dv = lax.fori_loop(j + 1, nq, lambda i, cr: step(i, cr, False), carry)
        dkn_ref[...] = dk[:, :QK_NOPE].astype(BF16)
        dkr_ref[...] = dk[:, QK_NOPE:]
        dv_ref[...] = dv.astype(BF16)

        @pl.when(j == nq - 1)
        def _():
            dq = dq_acc[...]
            dq_ref[:, :QK_NOPE] = dq[:, :QK_NOPE].astype(BF16)
            dq_ref[:, QK_NOPE:] = _rotate(dq[:, QK_NOPE:], c_ref[...], sa_ref[...], sb_ref[...], -1.0).astype(BF16)

    row = pl.BlockSpec((None, 1, T), lambda h, j: (h, 0, 0))
    head = pl.BlockSpec((TQ, 128), lambda h, j: (j, h))
    whole = pl.BlockSpec((None, T, QK_PAD), lambda h, j: (h, 0, 0))
    tab = pl.BlockSpec((T, 128), lambda h, j: (0, 0))
    heads = pl.BlockSpec((T, V_HEAD), lambda h, j: (0, h))
    return _pallas(
        body, name="attn_bwd", grid=(N_HEADS, nq),
        in_specs=[whole, head, pl.BlockSpec((TQ, 128), lambda h, j: (j, 0)), head, heads, heads, row, tab, tab, tab],
        out_specs=[whole, head, pl.BlockSpec((None, TQ, 128), lambda h, j: (h, j, 0)), head],
        out_shape=[jax.ShapeDtypeStruct((N_HEADS, T, QK_PAD), BF16), jax.ShapeDtypeStruct((T, N_HEADS * QK_NOPE), BF16),
                   jax.ShapeDtypeStruct((N_HEADS, T, 128), F32), jax.ShapeDtypeStruct((T, N_HEADS * V_HEAD), BF16)],
        scratch_shapes=[pltpu.VMEM((T, QK_PAD), F32), pltpu.VMEM((1, T), F32)],
        params=_params(("parallel", "arbitrary")))(q, kn, kr, v, o, do, lse_row, cos, sa, sb)


def _ffn_gup(name, dg, dv, hf):
    def body(dg_ref, dv_ref, hf_ref, o_ref):
        j = pl.program_id(0)

        @pl.when(j < N_FF_BLK)
        def _():
            o_ref[...] = lax.dot_general(dg_ref[...], hf_ref[...], TN, preferred_element_type=F32).astype(BF16)

        @pl.when(j >= N_FF_BLK)
        def _():
            o_ref[...] = lax.dot_general(dv_ref[...], hf_ref[...], TN, preferred_element_type=F32).astype(BF16)

    return _pallas(
        body, name=name, grid=(N_DEV,),
        in_specs=[pl.BlockSpec((None, T, FF_BLK), lambda j: (jnp.minimum(j, N_FF_BLK - 1), 0, 0)),
                  pl.BlockSpec((None, T, FF_BLK), lambda j: (jnp.maximum(j - N_FF_BLK, 0), 0, 0)),
                  pl.BlockSpec((T, D), lambda j: (0, 0))],
        out_specs=pl.BlockSpec((None, FF_BLK, D), lambda j: (j, 0, 0)),
        out_shape=jax.ShapeDtypeStruct((N_DEV, FF_BLK, D), BF16), params=_params(("parallel",)))(dg, dv, hf)


def _ffn_layer_fwd(tag, h, gain, ex):
    hf = _rms_fwd(f"{tag}_norm", h, gain)
    g, v, act = _ffn_up_act(f"{tag}_up", hf, ex.need(f"ffn_w_up{tag[1]}", hf), ex.need(f"ffn_cw{tag[1]}", hf),
                            ex.need(f"ffn_cb{tag[1]}", hf))
    ex.at(f"{tag}_up", act)
    rows = pl.BlockSpec((TS, D), lambda i: (i, 0))
    out = _mm_sum(f"{tag}_down",
                  [(act, pl.BlockSpec((N_FF_BLK, TS, FF_BLK), lambda i: (0, i, 0)), ex.need(f"ffn_w_down{tag[1]}", act),
                    pl.BlockSpec((None, N_FF_BLK, FF_BLK, D), lambda i: (0, 0, 0, 0)), NN, 0)],
                  grid=(T // TS,), o_spec=rows, o_shape=(T, D), o_dtype=F32, add=h)
    ex.at(f"{tag}_down", out)
    return out, (hf, g, v, act)


def _ffn_layer_bwd(tag, h, gain, ex, saved, dh, dh_bf):
    hf, g, v, act = saved
    layer = tag[1]
    w_up, w_down4 = ex.need(f"ffn_w_up{layer}", dh_bf), ex.need(f"ffn_w_down{layer}", dh_bf)
    dg, dv, dcw, dcb = _ffn_dact(f"{tag}_dact", dh_bf, w_down4, g, v, ex.need(f"ffn_cw{layer}", dh_bf),
                                 ex.need(f"ffn_cb{layer}", dh_bf))
    ex.at(f"{tag}_dact", dg)
    g_down = _mm(f"{tag}_gdown", act, dh_bf, grid=(N_FF_BLK,),
                 a_spec=pl.BlockSpec((None, T, FF_BLK), lambda j: (j, 0, 0)),
                 b_spec=pl.BlockSpec((T, D), lambda j: (0, 0)),
                 o_spec=pl.BlockSpec((FF_BLK, D), lambda j: (j, 0)),
                 o_shape=(D_FF, D), o_dtype=BF16, dims=TN)
    g_up = _ffn_gup(f"{tag}_gup", dg, dv, hf)
    ex.grad("ffn_w_up", int(layer), g_up.reshape(1, N_DEV, FF_BLK, D))
    ex.grad("ffn_w_down", int(layer), g_down.reshape(1, N_DEV, D_FF // N_DEV, D))
    ex.at(f"{tag}_gup", g_up)
    part = pl.BlockSpec((N_FF_BLK, TR, FF_BLK), lambda i: (0, i, 0))
    dh_in, dh_in_bf, dgain = _mm_sum(
        f"{tag}_dhf",
        [(dg, part, w_up, pl.BlockSpec((None, N_FF_BLK, FF_BLK, D), lambda i: (0, 0, 0, 0)), NN, 0),
         (dv, part, w_up, pl.BlockSpec((None, N_FF_BLK, FF_BLK, D), lambda i: (0, 1, 0, 0)), NN, 0)],
        grid=(T // TR,), o_spec=pl.BlockSpec((TR, D), lambda i: (i, 0)), o_shape=(T, D), o_dtype=F32,
        norm_bwd=(h, [gain], dh))
    ex.at(f"{tag}_dhf", dh_in)
    return dh_in, dh_in_bf, dgain[0], dcw, dcb


def _local_step(x, pos, tgt, rep, ex):
    attn_norm, ffn_norm, final_norm = rep["attn_norm"], rep["ffn_norm"], rep["final_norm"]
    half = QK_ROPE // 2
    inv = 1.0 / (ROPE_THETA ** (jnp.arange(half, dtype=F32) / half))
    inv_freq = jnp.concatenate([inv, inv, jnp.zeros((128 - 2 * half,), F32)]).reshape(1, 128)
    tables = _rope_tables(pos, inv_freq)

    hn0 = _rms_fwd("l0_norm", x, attn_norm[0:1])
    w_in = ex.need("sc_w_in", hn0)
    ex.at("mixer_ready", hn0)
    z = _mm_rows("l0_in", hn0, w_in, NN, BF16, 3 * D, tn=512)
    ex.at("l0_in", z)
    y = _sc_fwd(z, ex.need("sc_conv_w", z))
    h1 = _mm_rows("l0_out", y, ex.need("sc_w_out", y), NN, F32, D, tn=512, add=x)
    ex.at("l0_out", h1)
    h2, ffn0 = _ffn_layer_fwd("f0", h1, ffn_norm[0:1], ex)

    hk, hn1 = _rms_fwd2("h2_norms", h2, rep["kv_in_norm"], attn_norm[1:2])
    ckv_raw, ckv, kr = _kv_down(hk, ex.need("w_dkv", hk), ex.need("w_kr", hk), rep["kv_latent_norm"], tables)
    kn, vv = _kv_up(ckv, ex.need("w_uk", ckv), ex.need("w_uv", ckv))

    cq_raw, cq = _down_norm("q_down", hn1, ex.need("w_dq", hn1), rep["q_latent_norm"])
    w_uq = ex.need("w_uq", cq)
    q = _q_up(cq, w_uq, tables)
    o, lse = _attn_fwd(q, kn, kr, vv)
    ex.at("attn_fwd", o)
    w_o = ex.need("w_o", o)
    h3 = _mm_rows("attn_out", o, w_o, NN, F32, D, tn=512, add=h2)
    h4, ffn1 = _ffn_layer_fwd("f1", h3, ffn_norm[1:2], ex)

    loss, dh4, dh4_bf, d_final = _final(h4, final_norm.reshape(1, D), tgt)

    dh3, dh3_bf, d_fn1, dcw1, dcb1 = _ffn_layer_bwd("f1", h3, ffn_norm[1:2], ex, ffn1, dh4, dh4_bf)
    ex.at("f1_bwd", dh3)

    do = _mm_rows("d_attn_out", dh3_bf, w_o, NT, BF16, N_HEADS * V_HEAD)
    ex.grad("w_o", None, _mm_wgrad("g_w_o", o, dh3_bf).reshape(1, N_DEV, D // N_DEV, D))
    dq_pre, dkn, dkr, dvv = _attn_bwd(q, kn, kr, vv, o, do, lse, tables)
    def rows_of(a):
        return a[None], pl.BlockSpec((1, TS, a.shape[1]), lambda i: (0, i, 0))

    def whole(wt):
        return wt[None], pl.BlockSpec((1,) + wt.shape, lambda i: (0, 0, 0))

    def row_blocks(d):
        return dict(grid=(T // TS,), o_spec=pl.BlockSpec((TS, d), lambda i: (i, 0)), o_shape=(T, d), o_dtype=F32)

    _, dcq_raw_bf, (d_qln,) = _mm_sum(
        "d_q_up", [(dq_pre, pl.BlockSpec((N_HEADS, TS, QK_PAD), lambda i: (0, i, 0)),
                    w_uq, pl.BlockSpec((N_HEADS, Q_LORA, QK_PAD), lambda i: (0, 0, 0)), NT, 0)],
        norm_bwd=(cq_raw, [rep["q_latent_norm"]], None), **row_blocks(Q_LORA))
    g_uq = _mm("g_w_uq", cq, dq_pre, grid=(N_HEADS,),
               a_spec=pl.BlockSpec((T, Q_LORA), lambda h: (0, 0)),
               b_spec=pl.BlockSpec((None, T, QK_PAD), lambda h: (h, 0, 0)),
               o_spec=pl.BlockSpec((None, Q_LORA, QK_PAD), lambda h: (h, 0, 0)),
               o_shape=(N_HEADS, Q_LORA, QK_PAD), o_dtype=BF16, dims=TN)
    ex.grad("w_uq", None, g_uq[:, :, :QK_NOPE + QK_ROPE].reshape(1, N_DEV, Q_LORA, QK_NOPE + QK_ROPE))
    ex.grad("w_dq", None, _mm_wgrad("g_w_dq", hn1, dcq_raw_bf).reshape(1, N_DEV, D // N_DEV, Q_LORA))

    _, dckv_raw_bf, (d_kvln,) = _mm_sum(
        "d_kv_up", [(*rows_of(dkn), *whole(ex.need("w_uk", dkn)), NT, 0),
                    (*rows_of(dvv), *whole(ex.need("w_uv", dvv)), NT, 0)],
        norm_bwd=(ckv_raw, [rep["kv_latent_norm"]], None), **row_blocks(KV_LORA))
    ex.grad("w_uk", None, _mm_wgrad("g_w_uk", ckv, dkn))
    ex.grad("w_uv", None, _mm_wgrad("g_w_uv", ckv, dvv))
    dkr_raw_bf = _rope("dk_rope", dkr, tables, -1.0, BF16, reduce_groups=True).reshape(T, 128)
    ex.grad("w_dkv", None, _mm_wgrad("g_w_dkv", hk, dckv_raw_bf).reshape(1, N_DEV, D // N_DEV, KV_LORA))
    ex.grad("w_kr", None, _mm_wgrad("g_w_kr", hk, dkr_raw_bf)[:, :QK_ROPE].reshape(1, N_DEV, D // N_DEV, QK_ROPE))

    dh2, dh2_bf, (d_an1, d_kvin) = _mm_sum(
        "d_h2", [(*rows_of(dcq_raw_bf), *whole(ex.need("w_dq", dcq_raw_bf)), NT, 0),
                 (*rows_of(dckv_raw_bf), *whole(ex.need("w_dkv", dckv_raw_bf)), NT, 1),
                 (*rows_of(dkr_raw_bf), *whole(ex.need("w_kr", dkr_raw_bf)), NT, 1)],
        norm_bwd=(h2, [attn_norm[1:2], rep["kv_in_norm"]], dh3), **row_blocks(D))
    ex.at("kv_bwd", dh2)

    dh1, dh1_bf, d_fn0, dcw0, dcb0 = _ffn_layer_bwd("f0", h1, ffn_norm[0:1], ex, ffn0, dh2, dh2_bf)
    ex.at("f0_bwd", dh1)

    dy = _mm_rows("d_l0_out", dh1_bf, ex.need("sc_w_out", dh1_bf), NT, F32, D)
    ex.grad("sc_w_out", None, _mm_wgrad("g_sc_w_out", y, dh1_bf).reshape(1, N_DEV, D // N_DEV, D))
    dz, d_scw = _sc_bwd(z, dy, ex.need("sc_conv_w", dy))
    g_in = _mm_wgrad("g_sc_w_in", hn0, dz)
    ex.grad("sc_w_in", None, g_in)
    ex.at("sc_bwd", g_in)
    ex.at("d_l0_in", g_in)
    grad_x, _, (d_an0,) = _mm_sum(
        "d_l0_in", [(*rows_of(dz), *whole(ex.need("sc_w_in", dz)), NT, 0)],
        norm_bwd=(x, [attn_norm[0:1]], dh1), **row_blocks(D))

    small = {
        "attn_norm": jnp.concatenate([d_an0, d_an1], axis=0),
        "ffn_norm": jnp.concatenate([d_fn0, d_fn1], axis=0),
        "final_norm": d_final.reshape(D),
        "kv_in_norm": d_kvin.reshape(D),
        "kv_latent_norm": d_kvln.reshape(KV_LORA),
        "q_latent_norm": d_qln,
        "ffn_conv_b": jnp.stack([dcb0, dcb1]).transpose(0, 2, 1, 3).reshape(2, D_FF),
        "sc_conv_w": d_scw,
        "ffn_conv_w": jnp.stack([dcw0, dcw1]).transpose(0, 2, 1, 3).reshape(2, 3, D_FF),
    }
    return loss, grad_x, small


def _place():
    return lax.axis_index("x"), lax.axis_index("y"), lax.axis_index("c")


def _peers():
    x, y, c = _place()
    return (x, y, 1 - c), [(1 - x, y), (x, 1 - y), (1 - x, 1 - y)]


def _window(ref, kind, dev):
    if kind == "blocked":
        return ref.at[:, dev]
    width = ref.shape[-1] // N_DEV
    return ref.at[:, pl.ds(pl.multiple_of(dev * width, 128), width)]


def _all_gather(name, items):
    n = len(items)
    out_shapes = []
    for shard, kind in items:
        if kind == "blocked":
            shape = (shard.shape[0], N_DEV) + shard.shape[1:]
        else:
            shape = (shard.shape[0], N_DEV * shard.shape[1])
        out_shapes.append(jax.ShapeDtypeStruct(shape, shard.dtype))

    def body(*refs):
        srcs, outs = refs[:n], refs[n:2 * n]
        send_sems, recv_sems, local_sems = refs[2 * n:]
        x, y, c = _place()
        me = 4 * x + 2 * y + c
        sibling, chips = _peers()

        def num(px, py, pc):
            return 4 * px + 2 * py + pc

        def copy(t, k, dev, to, from_src):
            kind = items[t][1]
            dst = _window(outs[t], kind, dev)
            return pltpu.make_async_remote_copy(
                src_ref=srcs[t] if from_src else dst, dst_ref=dst,
                send_sem=send_sems.at[t, k], recv_sem=recv_sems.at[t, k], device_id=to, device_id_type=MESH)

        mine = [pltpu.make_async_copy(srcs[t], _window(outs[t], items[t][1], me), local_sems.at[t]) for t in range(n)]
        for cp in mine:
            cp.start()
        first = []
        for t in range(n):
            first.append(copy(t, 0, me, sibling, True))
            for j, chip in enumerate(chips):
                first.append(copy(t, 1 + j, me, (*chip, c), True))
        for cp in first:
            cp.start()
        passed = []
        for j, chip in enumerate(chips):
            for t in range(n):
                copy(t, 1 + j, num(*chip, c), (x, y, c), False).wait_recv()
                fwd = copy(t, 4 + j, num(*chip, c), sibling, False)
                fwd.start()
                passed.append(fwd)
        for t in range(n):
            copy(t, 0, num(x, y, 1 - c), (x, y, c), False).wait_recv()
            for j, chip in enumerate(chips):
                copy(t, 4 + j, num(*chip, 1 - c), (x, y, c), False).wait_recv()
        for cp in first + passed:
            cp.wait_send()
        for cp in mine:
            cp.wait()

    return _pallas(
        body, name=name, in_specs=[ANY_SPEC] * n, out_specs=[ANY_SPEC] * n, out_shape=out_shapes,
        scratch_shapes=[pltpu.SemaphoreType.DMA((n, 7)), pltpu.SemaphoreType.DMA((n, 7)), pltpu.SemaphoreType.DMA((n,))],
    )(*[s for s, _ in items])


HBM_SPEC = pl.BlockSpec(memory_space=pltpu.HBM)
SEM_SPEC = pl.BlockSpec(memory_space=pltpu.SEMAPHORE)
EFFECT = pltpu.SideEffectType.DATAFLOW_SIDE_EFFECTING
TOKEN = jax.ShapeDtypeStruct((8, 128), F32)


def _hbm(a):
    return pltpu.with_memory_space_constraint(a, pltpu.HBM)


def _copies_start(name, jobs):
    nj = len(jobs)
    counts = [(len(srcs), len(lands)) for srcs, lands, _, _ in jobs]
    n_arr = sum(ns + nl for ns, nl in counts)

    def body(*refs):
        sems, token = refs[n_arr:n_arr + 2 * nj], refs[-1]
        at = 0
        for j, ((ns, nl), (_, _, ncopy, plan)) in enumerate(zip(counts, jobs)):
            copies = plan(refs[at:at + ns], refs[at + ns:at + ns + nl])
            assert len(copies) == ncopy
            for k, (sent, dst, to, _) in enumerate(copies):
                pltpu.make_async_remote_copy(src_ref=sent, dst_ref=dst, send_sem=sems[2 * j].at[k],
                                             recv_sem=sems[2 * j + 1].at[k], device_id=to, device_id_type=MESH).start()
            at += ns + nl
        token[...] = jnp.zeros_like(token)

    arrays = [a for srcs, lands, _, _ in jobs for a in list(srcs) + list(lands)]
    sem_shapes = [pltpu.SemaphoreType.DMA((ncopy,)) for _, _, ncopy, _ in jobs for _ in range(2)]
    outs = pl.pallas_call(
        body, name=name, in_specs=[HBM_SPEC] * n_arr,
        out_specs=[SEM_SPEC] * (2 * nj) + [HBM_SPEC] * n_arr + [VMEM_SPEC],
        out_shape=sem_shapes + [pltpu.HBM(a.shape, a.dtype) for a in arrays] + [TOKEN],
        input_output_aliases={i: 2 * nj + i for i in range(n_arr)},
        compiler_params=pltpu.CompilerParams(has_side_effects=EFFECT))(*[_hbm(a) for a in arrays])
    _Chain.last = outs[-1]
    flights, at = [], 2 * nj
    for j, (ns, nl) in enumerate(counts):
        flights.append((outs[2 * j], outs[2 * j + 1], list(outs[at:at + ns]), list(outs[at + ns:at + ns + nl])))
        at += ns + nl
    return flights


def _copies_wait(name, started, ncopy, plan):
    send, recv, srcs, lands = started
    ns, nl = len(srcs), len(lands)

    def body(*refs):
        send_ref, recv_ref, token = refs[ns + nl], refs[ns + nl + 1], refs[-1]
        copies = plan(refs[:ns], refs[ns:ns + nl])
        assert len(copies) == ncopy
        for k, (sent, _, to, landed) in enumerate(copies):
            cp = pltpu.make_async_remote_copy(src_ref=sent, dst_ref=landed, send_sem=send_ref.at[k],
                                              recv_sem=recv_ref.at[k], device_id=to, device_id_type=MESH)
            cp.wait_send()
            cp.wait_recv()
        token[...] = jnp.zeros_like(token)

    arrays = list(srcs) + list(lands)
    outs = pl.pallas_call(
        body, name=name, in_specs=[HBM_SPEC] * (ns + nl) + [SEM_SPEC] * 2 + [ANY_SPEC],
        out_specs=[HBM_SPEC] * (ns + nl) + [VMEM_SPEC], out_shape=[pltpu.HBM(a.shape, a.dtype) for a in arrays] + [TOKEN],
        input_output_aliases={i: i for i in range(ns + nl)},
        compiler_params=pltpu.CompilerParams(has_side_effects=EFFECT))(*arrays, send, recv, _Chain.last)
    _Chain.last = outs[-1]
    return list(outs[:ns]), list(outs[ns:-1])


def _plan_gather_chips(kinds):
    def plan(srcs, lands):
        x, y, c = _place()
        sibling, chips = _peers()
        out = []
        for t, kind in enumerate(kinds):
            mine = _window(lands[t], kind, 4 * x + 2 * y + c)
            out.append((srcs[t], mine, sibling, _window(lands[t], kind, 4 * x + 2 * y + 1 - c)))
            for px, py in chips:
                out.append((srcs[t], mine, (px, py, c), _window(lands[t], kind, 4 * px + 2 * py + c)))
        return out
    return plan, 4 * len(kinds)


def _plan_gather_sibling(kinds):
    def plan(srcs, lands):
        _, _, c = _place()
        sibling, chips = _peers()
        out = []
        for t, kind in enumerate(kinds):
            for px, py in chips:
                w = _window(lands[t], kind, 4 * px + 2 * py + c)
                out.append((w, w, sibling, _window(lands[t], kind, 4 * px + 2 * py + 1 - c)))
        return out
    return plan, 3 * len(kinds)


def _plan_scatter_sibling(kinds):
    def plan(srcs, lands):
        _, _, c = _place()
        sibling, _ = _peers()
        out = []
        for t, kind in enumerate(kinds):
            for k in range(N_CHIP):
                out.append((_window(srcs[t], kind, 2 * k + 1 - c), lands[t].at[k], sibling, lands[t].at[k]))
        return out
    return plan, N_CHIP * len(kinds)


def _plan_scatter_chips(n):
    def plan(srcs, lands):
        x, y, c = _place()
        _, chips = _peers()
        out = []
        for t in range(n):
            for px, py in chips:
                out.append((srcs[t].at[2 * px + py], lands[t].at[2 * x + y], (px, py, c), lands[t].at[2 * px + py]))
        return out
    return plan, 3 * n


def _landing(shard, kind, me):
    if kind == "blocked":
        land = lax.empty((shard.shape[0], N_DEV) + shard.shape[1:], shard.dtype)
        return lax.dynamic_update_slice(land, shard[:, None], (0, me) + (0,) * (shard.ndim - 1))
    land = lax.empty((shard.shape[0], N_DEV * shard.shape[1]), shard.dtype)
    return lax.dynamic_update_slice(land, shard, (0, me * shard.shape[1]))


def _chip_sums(name, grads, kinds, recvs, c):
    n = len(grads)
    in_specs, out_specs, out_shape, args = [], [], [], []
    for gr, kind, rv in zip(grads, kinds, recvs):
        if kind == "blocked":
            rows, w = gr.shape[2], gr.shape[3]
            in_specs.append(pl.BlockSpec((None, None, rows, w), lambda k, cref: (0, 2 * k + cref[0], 0, 0)))
        else:
            rows, w = gr.shape[0], gr.shape[1] // N_DEV
            in_specs.append(pl.BlockSpec((rows, w), lambda k, cref: (0, 2 * k + cref[0])))
        blk = pl.BlockSpec((None, rows, w), lambda k, cref: (k, 0, 0))
        in_specs.append(blk)
        out_specs.append(blk)
        out_shape.append(jax.ShapeDtypeStruct((N_CHIP, rows, w), BF16))
        args += [gr, rv.reshape(N_CHIP, rows, w)]

    def body(*refs):
        for t in range(n):
            g_ref, r_ref, o_ref = refs[1 + 2 * t], refs[2 + 2 * t], refs[1 + 2 * n + t]
            o_ref[...] = (g_ref[...].astype(F32) + r_ref[...].astype(F32)).astype(BF16)

    return _pallas(body, name=name, n_prefetch=1, grid=(N_CHIP,), in_specs=in_specs, out_specs=out_specs,
                   out_shape=out_shape, params=_params(("parallel",)))(c, *args)


def _adamw_math(g, wv, mv, vv):
    m = ADAM_B1 * mv + (1.0 - ADAM_B1) * g
    v = ADAM_B2 * vv + (1.0 - ADAM_B2) * (g * g)
    m_hat = m / (1.0 - ADAM_B1 ** ADAM_STEP)
    v_hat = v / (1.0 - ADAM_B2 ** ADAM_STEP)
    delta = -ADAM_LR * (m_hat / (jnp.sqrt(v_hat) + ADAM_EPS) + ADAM_WD * wv)
    return delta, m, v


ADAM_STEPS = 2


def _adamw_group(name, items, chip_ids):
    n = len(items)
    in_specs, out_specs, out_shape, args, prevs = [], [], [], [chip_ids], []
    for own, recv, w3, m3, v3, layer, _ in items:
        nl, rows, w = w3.shape
        tr = rows // ADAM_STEPS
        assert tr % 16 == 0, (name, rows)
        in_specs += [pl.BlockSpec((None, tr, w), lambda i, ids, slot=slot: (ids[slot], i, 0)) for slot in range(4)]
        slab = pl.BlockSpec((None, tr, w), lambda i, ids, layer=layer: (layer, i, 0))
        in_specs += [slab] * 3
        out_specs += [slab] * 4
        out_shape += [jax.ShapeDtypeStruct((nl, rows, w), F32)] * 4
        args += [own, recv, recv, recv, w3, m3, v3]
    aliases = {}
    for t, item in enumerate(items):
        if item[6] is not None:
            for k in range(4):
                aliases[len(args) + k] = 4 * t + k
            in_specs += [ANY_SPEC] * 4
            args += list(item[6])
            prevs.append(t)
    n_in = 1 + 7 * n + 4 * len(prevs)

    def body(*refs):
        for t in range(n):
            own_ref, r1_ref, r2_ref, r3_ref, w_ref, m_ref, v_ref = refs[1 + 7 * t:8 + 7 * t]
            g_ref, d_ref, nm_ref, nv_ref = refs[n_in + 4 * t:n_in + 4 * t + 4]
            g = ((own_ref[...].astype(F32) + r1_ref[...].astype(F32)) + r2_ref[...].astype(F32)) + r3_ref[...].astype(F32)
            g_ref[...] = g
            d_ref[...], nm_ref[...], nv_ref[...] = _adamw_math(g, w_ref[...], m_ref[...], v_ref[...])

    outs = _pallas(body, name=name, n_prefetch=1, grid=(ADAM_STEPS,), in_specs=in_specs, out_specs=out_specs,
                   out_shape=out_shape, aliases=aliases, params=_params(("parallel",)))(*args)
    return [list(outs[4 * t:4 * t + 4]) for t in range(n)]


def _adamw_small(gathered, ws, ms, vs):
    n = len(gathered)
    full = [w is not None for w in ws]
    args = list(gathered)
    out_shape = []
    for t in range(n):
        shape = jax.ShapeDtypeStruct(gathered[t].shape[2:], F32)
        if full[t]:
            args += [ws[t], ms[t], vs[t]]
            out_shape += [shape] * 4
        else:
            out_shape += [shape]

    def body(*refs):
        i_in, i_out = n, len(args)
        for t in range(n):
            p_ref = refs[t]
            g = p_ref[0, 0]
            for k in range(1, N_DEV):
                g = g + p_ref[0, k]
            refs[i_out][...] = g
            if full[t]:
                w_ref, m_ref, v_ref = refs[i_in:i_in + 3]
                refs[i_out + 1][...], refs[i_out + 2][...], refs[i_out + 3][...] = _adamw_math(
                    g, w_ref[...], m_ref[...], v_ref[...])
                i_in += 3
                i_out += 4
            else:
                i_out += 1

    outs = _pallas(body, name="adamw_small", in_specs=[VMEM_SPEC] * len(args), out_specs=[VMEM_SPEC] * len(out_shape),
                   out_shape=out_shape, params=pltpu.CompilerParams(vmem_limit_bytes=VMEM_LIMIT))(*args)
    result, i = [], 0
    for t in range(n):
        k = 4 if full[t] else 1
        result.append(list(outs[i:i + k]))
        i += k
    return result


def _adamw_plain(name, gs, ws, ms, vs):
    n = len(gs)

    def body(*refs):
        for t in range(n):
            g_ref, w_ref, m_ref, v_ref = refs[4 * t:4 * t + 4]
            outs = refs[4 * n + 3 * t:4 * n + 3 * t + 3]
            outs[0][...], outs[1][...], outs[2][...] = _adamw_math(g_ref[...], w_ref[...], m_ref[...], v_ref[...])

    args, out_shape = [], []
    for g, w, m, v in zip(gs, ws, ms, vs):
        args += [g, w, m, v]
        out_shape += [jax.ShapeDtypeStruct(w.shape, F32)] * 3
    outs = _pallas(body, name=name, in_specs=[VMEM_SPEC] * len(args), out_specs=[VMEM_SPEC] * len(out_shape),
                   out_shape=out_shape, params=pltpu.CompilerParams(vmem_limit_bytes=VMEM_LIMIT))(*args)
    return [list(outs[3 * t:3 * t + 3]) for t in range(n)]


KIND = {"sc_w_in": "cols", "sc_w_out": "blocked", "w_dkv": "blocked", "w_kr": "blocked", "w_uk": "cols", "w_uv": "cols",
        "w_dq": "blocked", "w_uq": "blocked", "w_o": "blocked", "ffn_w_up": "blocked", "ffn_w_down": "blocked",
        "conv": "blocked"}
GATHER_GROUPS = (("mixer", ("sc_w_in", "sc_w_out", "conv")),
                 ("up0", ("ffn_w_up0",)),
                 ("down0", ("ffn_w_down0",)),
                 ("attn", ("w_dkv", "w_kr", "w_uk", "w_uv", "w_dq", "w_uq", "w_o")),
                 ("ffn1", ("ffn_w_up1", "ffn_w_down1")))
SCATTER_GROUPS = (("ffn1", (("ffn_w_up", 1), ("ffn_w_down", 1))),
                  ("attn", (("w_o", None), ("w_uq", None), ("w_dq", None), ("w_uk", None), ("w_uv", None),
                            ("w_dkv", None), ("w_kr", None))),
                  ("ffn0", (("ffn_w_up", 0), ("ffn_w_down", 0))),
                  ("mixer", (("sc_w_out", None), ("sc_w_in", None))))
SCHEDULE = {
    "begin": (("gather_start", "mixer"),),
    "mixer_ready": (("gather_start", "up0"),),
    "l0_out": (("gather_forward", "up0"), ("gather_start", "down0")),
    "f0_up": (("gather_forward", "down0"), ("gather_start", "attn")),
    "f0_down": (("gather_forward", "attn"), ("gather_start", "ffn1")),
    "attn_fwd": (("gather_forward", "ffn1"),),
    "f1_gup": (("scatter_sibling", "ffn1"),),
    "f1_dhf": (("scatter_chips", "ffn1"),),
    "kv_bwd": (("scatter_sibling", "attn"), ("scatter_done", "ffn1")),
    "f0_dact": (("scatter_chips", "attn"),),
    "f0_gup": (("scatter_sibling", "ffn0"),),
    "f0_dhf": (("scatter_chips", "ffn0"),),
    "f0_bwd": (("scatter_done", "attn"),),
    "sc_bwd": (("scatter_sibling", "mixer"),),
    "d_l0_in": (("scatter_chips", "mixer"),),
}
FINISH = (("scatter_done", "ffn0"), ("scatter_done", "mixer"))
STAGES = {"gather_start": 1, "gather_forward": 2, "gather_done": 3,
          "scatter_sibling": 1, "scatter_chips": 2, "scatter_done": 3}
SMALL_W_ROWS = 24


def _pack(arrays, rows):
    flat = jnp.concatenate([a.reshape(-1).astype(F32) for a in arrays])
    return jnp.pad(flat, (0, rows * 128 - flat.shape[0])).reshape(rows, 128)


def _stored(name, a):
    return jnp.swapaxes(a, -1, -2) if name == "ffn_w_up" else a


def _base(name):
    if name.startswith("ffn_w_") and name[-1] in "01":
        return name[:-1], int(name[-1])
    return name, None


class _Exchange:
    def __init__(self, wts, mom, var, ffn_conv_b):
        self.wts, self.mom, self.var = wts, mom, var
        x, y, c = _place()
        self.me = 4 * x + 2 * y + c
        self.c_arr = jnp.reshape(c, (1,)).astype(jnp.int32)
        chip = 2 * x + y
        self.chip_ids = jnp.stack([chip, chip ^ 1, chip ^ 2, chip ^ 3]).astype(jnp.int32)
        self.ready = {"ffn_cb0": ffn_conv_b.reshape(2, N_FF_BLK, 1, FF_BLK)[0],
                      "ffn_cb1": ffn_conv_b.reshape(2, N_FF_BLK, 1, FF_BLK)[1]}
        self.gathers, self.group_of = {}, {}
        self.grads, self.scatters, self.results, self.queue = {}, {}, {}, []
        for gname, names in GATHER_GROUPS:
            self.gathers[gname] = dict(stage=0, names=names, kinds=[KIND[_base(nm)[0]] for nm in names])
            for nm in names:
                self.group_of[nm] = gname
        for nm in ("sc_conv_w", "ffn_cw0", "ffn_cw1"):
            self.group_of[nm] = "mixer"
        self.at("begin", None)

    def _shard(self, name):
        if name == "conv":
            return _pack([self.wts["sc_conv_w"], self.wts["ffn_conv_w"]], SMALL_W_ROWS).reshape(1, SMALL_W_ROWS, 128)
        base, layer = _base(name)
        a = _stored(base, self.wts[base])
        if layer is not None:
            a = a[layer:layer + 1]
        if KIND[base] == "cols":
            return a.reshape(a.shape[-2], a.shape[-1]).astype(BF16)
        return a.reshape((-1,) + a.shape[-2:]).astype(BF16)

    def _start(self, name, srcs, lands, ncopy, plan, st):
        self.queue.append((name, (srcs, lands, ncopy, plan), st))

    def _flush(self):
        if self.queue:
            flights = _copies_start("__".join(name for name, _, _ in self.queue), [job for _, job, _ in self.queue])
            for (_, _, st), flight in zip(self.queue, flights):
                st["flight"] = flight
            self.queue = []

    def _flight(self, st):
        self._flush()
        return st["flight"]

    def _gather_to(self, gname, stage, after):
        st = self.gathers[gname]
        if st["stage"] < 1 <= stage:
            shards = [self._shard(nm) for nm in st["names"]]
            lands = [_landing(s, kind, self.me) for s, kind in zip(shards, st["kinds"])]
            plan, ncopy = _plan_gather_chips(st["kinds"])
            self._start(f"ag_{gname}_chips", shards, lands, ncopy, plan, st)
            st["stage"] = 1
        if st["stage"] < 2 <= stage:
            plan, ncopy = _plan_gather_chips(st["kinds"])
            _, lands = _copies_wait(f"ag_{gname}_chips_wait", self._flight(st), ncopy, plan)
            plan, ncopy = _plan_gather_sibling(st["kinds"])
            self._start(f"ag_{gname}_sibling", [], lands, ncopy, plan, st)
            st["stage"] = 2
        if st["stage"] < 3 <= stage:
            plan, ncopy = _plan_gather_sibling(st["kinds"])
            _, lands = _copies_wait(f"ag_{gname}_sibling_wait", self._flight(st), ncopy, plan)
            for nm, land in zip(st["names"], lands):
                self._arrived(nm, land)
            st["stage"] = 3

    def _arrived(self, name, land):
        if name == "conv":
            conv = land.reshape(N_DEV, SMALL_W_ROWS * 128)
            self.ready["sc_conv_w"] = conv[:, :3 * 128].reshape(N_DEV, 3, 128).transpose(1, 0, 2).reshape(3, D)
            fcw = conv[:, 3 * 128:3 * 128 + 6 * 352].reshape(N_DEV, 2, 3, 352).transpose(1, 2, 0, 3)
            fcw = fcw.reshape(2, 3, N_FF_BLK, FF_BLK).transpose(0, 2, 1, 3)
            self.ready["ffn_cw0"], self.ready["ffn_cw1"] = fcw[0], fcw[1]
        elif name in ("sc_w_in", "w_uk", "w_uv") or name.startswith("ffn_w_up"):
            self.ready[name] = land
        elif name.startswith("ffn_w_down"):
            self.ready[name] = land.reshape(1, N_FF_BLK, FF_BLK, D)
        elif name == "w_kr":
            self.ready[name] = jnp.pad(land.reshape(D, QK_ROPE), ((0, 0), (0, 128 - QK_ROPE)))
        elif name == "w_uq":
            self.ready[name] = jnp.pad(land.reshape(N_HEADS, Q_LORA, QK_NOPE + QK_ROPE),
                                       ((0, 0), (0, 0), (0, QK_PAD - QK_NOPE - QK_ROPE)))
        else:
            self.ready[name] = land.reshape(D, land.shape[-1])

    def need(self, name, after):
        if name not in self.ready:
            self._gather_to(self.group_of[name], 3, after)
            self._flush()
        return self.ready[name]

    def grad(self, name, layer, array):
        self.grads[(name, layer)] = array

    def _scatter_to(self, gname, stage, after):
        keys = dict(SCATTER_GROUPS)[gname]
        st = self.scatters.setdefault(gname, dict(stage=0))
        kinds = [KIND[nm] for nm, _ in keys]
        if st["stage"] < 1 <= stage:
            grads = [self.grads[key] for key in keys]
            lands = []
            for gr, kind in zip(grads, kinds):
                shard = (gr.shape[0],) + gr.shape[2:] if kind == "blocked" else (gr.shape[0], gr.shape[1] // N_DEV)
                lands.append(lax.empty((N_CHIP,) + shard, BF16))
            plan, ncopy = _plan_scatter_sibling(kinds)
            self._start(f"rs_{gname}_sibling", grads, lands, ncopy, plan, st)
            st["stage"] = 1
        if st["stage"] < 2 <= stage:
            plan, ncopy = _plan_scatter_sibling(kinds)
            grads, recvs = _copies_wait(f"rs_{gname}_sibling_wait", self._flight(st), ncopy, plan)
            sums = _chip_sums(f"rs_{gname}_sums", grads, kinds, recvs, self.c_arr)
            lands = [lax.empty(s.shape, BF16) for s in sums]
            plan, ncopy = _plan_scatter_chips(len(sums))
            self._start(f"rs_{gname}_chips", sums, lands, ncopy, plan, st)
            st["stage"] = 2
        if st["stage"] < 3 <= stage:
            plan, ncopy = _plan_scatter_chips(len(keys))
            sums, recvs = _copies_wait(f"rs_{gname}_chips_wait", self._flight(st), ncopy, plan)
            items = []
            for (nm, layer), own, rv in zip(keys, sums, recvs):
                nl = 1 if layer is None else 2
                rows, w = own.shape[1], own.shape[2]
                w3, m3, v3 = (_stored(nm, src[nm]).reshape(nl, rows, w) for src in (self.wts, self.mom, self.var))
                items.append((own, rv, w3, m3, v3, 0 if layer is None else layer, self.results.get(nm)))
            outs = _adamw_group(f"adamw_{gname}", items, self.chip_ids)
            for (nm, _), out in zip(keys, outs):
                self.results[nm] = out
            st["stage"] = 3

    def at(self, place, after):
        for action, gname in SCHEDULE.get(place, ()):
            self._advance(action, gname, after)
        self._flush()

    def _advance(self, action, gname, after):
        if action.startswith("gather"):
            self._gather_to(gname, STAGES[action], after)
        else:
            self._scatter_to(gname, STAGES[action], after)

    def finish(self, after):
        for action, gname in FINISH:
            self._advance(action, gname, after)
        for gname, _ in SCATTER_GROUPS:
            self._scatter_to(gname, 3, after)
        return {nm: [_stored(nm, o.reshape(_stored(nm, self.wts[nm]).shape)) for o in outs]
                for nm, outs in self.results.items()}


REPLICATED = ("attn_norm", "ffn_norm", "final_norm", "kv_in_norm", "kv_latent_norm", "q_latent_norm", "ffn_conv_b")
WEIGHTS = ("attn_norm", "ffn_norm", "final_norm", "sc_w_in", "sc_conv_w", "sc_w_out", "kv_in_norm", "w_dkv",
           "kv_latent_norm", "w_kr", "w_uk", "w_uv", "w_dq", "q_latent_norm", "w_uq", "w_o", "ffn_w_up", "ffn_conv_w",
           "ffn_conv_b", "ffn_w_down")


def kernel(x, positions, attn_norm, ffn_norm, final_norm, sc_w_in, sc_conv_w, sc_w_out, kv_in_norm, w_dkv, kv_latent_norm, w_kr, w_uk, w_uv, w_dq, q_latent_norm, w_uq, w_o, ffn_w_up, ffn_conv_w, ffn_conv_b, ffn_w_down, loss_target, m_attn_norm, m_ffn_norm, m_final_norm, m_sc_w_in, m_sc_conv_w, m_sc_w_out, m_kv_in_norm, m_w_dkv, m_kv_latent_norm, m_w_kr, m_w_uk, m_w_uv, m_w_dq, m_q_latent_norm, m_w_uq, m_w_o, m_ffn_w_up, m_ffn_conv_w, m_ffn_conv_b, m_ffn_w_down, v_attn_norm, v_ffn_norm, v_final_norm, v_sc_w_in, v_sc_conv_w, v_sc_w_out, v_kv_in_norm, v_w_dkv, v_kv_latent_norm, v_w_kr, v_w_uk, v_w_uv, v_w_dq, v_q_latent_norm, v_w_uq, v_w_o, v_ffn_w_up, v_ffn_conv_w, v_ffn_conv_b, v_ffn_w_down):
    wts = dict(attn_norm=attn_norm, ffn_norm=ffn_norm, final_norm=final_norm, sc_w_in=sc_w_in, sc_conv_w=sc_conv_w,
               sc_w_out=sc_w_out, kv_in_norm=kv_in_norm, w_dkv=w_dkv, kv_latent_norm=kv_latent_norm, w_kr=w_kr,
               w_uk=w_uk, w_uv=w_uv, w_dq=w_dq, q_latent_norm=q_latent_norm, w_uq=w_uq, w_o=w_o, ffn_w_up=ffn_w_up,
               ffn_conv_w=ffn_conv_w, ffn_conv_b=ffn_conv_b, ffn_w_down=ffn_w_down)
    mom = dict(attn_norm=m_attn_norm, ffn_norm=m_ffn_norm, final_norm=m_final_norm, sc_w_in=m_sc_w_in,
               sc_conv_w=m_sc_conv_w, sc_w_out=m_sc_w_out, kv_in_norm=m_kv_in_norm, w_dkv=m_w_dkv,
               kv_latent_norm=m_kv_latent_norm, w_kr=m_w_kr, w_uk=m_w_uk, w_uv=m_w_uv, w_dq=m_w_dq,
               q_latent_norm=m_q_latent_norm, w_uq=m_w_uq, w_o=m_w_o, ffn_w_up=m_ffn_w_up, ffn_conv_w=m_ffn_conv_w,
               ffn_conv_b=m_ffn_conv_b, ffn_w_down=m_ffn_w_down)
    var = dict(attn_norm=v_attn_norm, ffn_norm=v_ffn_norm, final_norm=v_final_norm, sc_w_in=v_sc_w_in,
               sc_conv_w=v_sc_conv_w, sc_w_out=v_sc_w_out, kv_in_norm=v_kv_in_norm, w_dkv=v_w_dkv,
               kv_latent_norm=v_kv_latent_norm, w_kr=v_w_kr, w_uk=v_w_uk, w_uv=v_w_uv, w_dq=v_w_dq,
               q_latent_norm=v_q_latent_norm, w_uq=v_w_uq, w_o=v_w_o, ffn_w_up=v_ffn_w_up, ffn_conv_w=v_ffn_conv_w,
               ffn_conv_b=v_ffn_conv_b, ffn_w_down=v_ffn_w_down)
    xi, yi, ci = _place()
    me = 4 * xi + 2 * yi + ci
    _Chain.last = None

    ex = _Exchange(wts, mom, var, ffn_conv_b)
    rep = {
        "attn_norm": attn_norm, "ffn_norm": ffn_norm, "final_norm": final_norm,
        "kv_in_norm": kv_in_norm.reshape(1, D), "kv_latent_norm": kv_latent_norm.reshape(1, KV_LORA),
        "q_latent_norm": q_latent_norm.reshape(1, Q_LORA),
    }
    loss, grad_x, small = _local_step(x.reshape(T, D), positions.reshape(T, 1), loss_target.reshape(T, D), rep, ex)
    results = ex.finish(grad_x)

    def rows_of(a):
        return a.reshape(-1, a.shape[-1])

    small_order = list(REPLICATED) + ["sc_conv_w", "ffn_conv_w"]
    shards = [loss.reshape(1, 1, 128)] + [rows_of(small[nm])[None] for nm in small_order]
    gathered = _all_gather("ag_small_grads", [(s, "blocked") for s in shards])
    params = [[None] + [rows_of(src[nm]) for nm in REPLICATED] + [None, None] for src in (wts, mom, var)]
    summed = _adamw_small(gathered, *params)
    loss_total = summed[0][0][0, 0]
    for nm, vals in zip(REPLICATED, summed[1:1 + len(REPLICATED)]):
        results[nm] = [a.reshape(wts[nm].shape) for a in vals]
    g_scw = lax.dynamic_slice(summed[-2][0], (0, me * 128), (3, 128))
    g_fcw = lax.dynamic_slice(summed[-1][0], (0, me * 352), (6, 352))
    conv = _adamw_plain("adamw_conv", [g_scw, g_fcw], *[[rows_of(src["sc_conv_w"]), rows_of(src["ffn_conv_w"])]
                                                        for src in (wts, mom, var)])
    for nm, g_own, vals in zip(("sc_conv_w", "ffn_conv_w"), (g_scw, g_fcw), conv):
        results[nm] = [a.reshape(wts[nm].shape) for a in [g_own] + vals]

    outs = [loss_total, grad_x.reshape(1, T, D)]
    for slot in range(4):
        outs.extend(results[nm][slot] for nm in WEIGHTS)
    return tuple(outs)
```

```python
import jax
import jax.numpy as jnp
from jax import lax
from jax.experimental import pallas as pl
from jax.experimental.pallas import tpu as pltpu

F32 = jnp.float32
BF16 = jnp.bfloat16

T = 2048
D = 1024
N_HEADS = 8
QK_NOPE = 128
QK_ROPE = 64
V_HEAD = 128
Q_LORA = 384
KV_LORA = 256
D_FF = 2816
CHUNK = 64
ROPE_THETA = 10000.0
EPS = 1e-6
NEG_INF = -1e30
ADAM_LR = 0.001
ADAM_B1 = 0.9
ADAM_B2 = 0.999
ADAM_EPS = 1e-08
ADAM_WD = 0.01
ADAM_STEP = 10

N_DEV = 8
N_CHIP = 4
FF_BLK = D_FF * 2 // N_DEV
N_FF_BLK = D_FF // FF_BLK
QK_PAD = 256
HALO = 16

TM = 1024
TS = 512
TR = 256
TQ = 512
VMEM_LIMIT = 56 * 1024 * 1024

NN = (((1,), (0,)), ((), ()))
NT = (((1,), (1,)), ((), ()))
TN = (((0,), (0,)), ((), ()))
MESH = pl.DeviceIdType.MESH


def _params(sem):
    return pltpu.CompilerParams(dimension_semantics=sem, vmem_limit_bytes=VMEM_LIMIT)


ANY_SPEC = pl.BlockSpec(memory_space=pl.ANY)
VMEM_SPEC = pl.BlockSpec(memory_space=pltpu.VMEM)


class _Chain:
    last = None


def _pallas(body, *, name, in_specs, out_specs, out_shape, grid=(), scratch_shapes=(), n_prefetch=0, aliases=None,
            params=None):
    def run(*args):
        after = _Chain.last
        n_lead = len(args)
        specs, operands, fn = list(in_specs), list(args), body
        if after is not None:
            def fn(*refs):
                return body(*refs[:n_lead], *refs[n_lead + 1:])
            specs.append(ANY_SPEC)
            operands.append(after)
        kw = dict(name=name, out_shape=out_shape, input_output_aliases=aliases or {})
        if params is not None:
            kw["compiler_params"] = params
        if n_prefetch:
            kw["grid_spec"] = pltpu.PrefetchScalarGridSpec(
                num_scalar_prefetch=n_prefetch, grid=grid, in_specs=specs, out_specs=out_specs,
                scratch_shapes=scratch_shapes)
        else:
            kw.update(grid=grid, in_specs=specs, out_specs=out_specs, scratch_shapes=scratch_shapes)
        outs = pl.pallas_call(fn, **kw)(*operands)
        _Chain.last = outs[0] if isinstance(outs, (list, tuple)) else outs
        return outs
    return run


def _mm(name, a, b, *, grid, a_spec, b_spec, o_spec, o_shape, o_dtype, dims, k_axis=None, acc_shape=None,
        add=None, add_spec=None):
    nk = grid[k_axis] if k_axis is not None else 1
    has_add = add is not None

    def body(*refs):
        a_ref, b_ref = refs[0], refs[1]
        p = 2
        add_ref = None
        if has_add:
            add_ref = refs[p]
            p += 1
        o_ref = refs[p]
        p += 1
        r = lax.dot_general(a_ref[...].astype(BF16), b_ref[...].astype(BF16), dims, preferred_element_type=F32)
        if k_axis is None:
            if has_add:
                r = r + add_ref[...].astype(F32)
            o_ref[...] = r.astype(o_dtype)
        else:
            acc = refs[p]
            k = pl.program_id(k_axis)

            @pl.when(k == 0)
            def _():
                acc[...] = r

            @pl.when(k > 0)
            def _():
                acc[...] += r

            @pl.when(k == nk - 1)
            def _():
                t = acc[...]
                if has_add:
                    t = t + add_ref[...].astype(F32)
                o_ref[...] = t.astype(o_dtype)

    in_specs = [a_spec, b_spec]
    args = [a, b]
    if has_add:
        in_specs.append(add_spec if add_spec is not None else o_spec)
        args.append(add)
    sem = tuple("arbitrary" if ax == k_axis else "parallel" for ax in range(len(grid)))
    scratch = [pltpu.VMEM(acc_shape, F32)] if k_axis is not None else []
    return _pallas(body, name=name, grid=grid, in_specs=in_specs, out_specs=o_spec,
                   out_shape=jax.ShapeDtypeStruct(o_shape, o_dtype), scratch_shapes=scratch, params=_params(sem))(*args)


def _mm_sum(name, parts, *, grid, o_spec, o_shape, o_dtype, add=None, norm_bwd=None):
    has_add = add is not None
    np_ = len(parts)
    nn = 1 if norm_bwd is None else len(norm_bwd[1])
    has_res = norm_bwd is not None and norm_bwd[2] is not None

    def body(*refs):
        accs = [None] * nn
        for p, (_, _, _, _, dims, n) in enumerate(parts):
            a_ref, b_ref = refs[2 * p], refs[2 * p + 1]
            for k in range(a_ref.shape[0]):
                r = lax.dot_general(a_ref[k], b_ref[k], dims, preferred_element_type=F32)
                accs[n] = r if accs[n] is None else accs[n] + r
        if norm_bwd is None:
            acc = accs[0]
            if has_add:
                acc = acc + refs[2 * np_][...]
            refs[-1][...] = acc.astype(o_dtype)
            return
        x_ref, g_refs = refs[2 * np_], refs[2 * np_ + 1:2 * np_ + 1 + nn]
        dx_ref, dxb_ref, dg_refs = refs[-2 - nn], refs[-1 - nn], refs[-nn:]
        xv = x_ref[...]
        r = lax.rsqrt(jnp.mean(xv * xv, axis=-1, keepdims=True) + EPS)
        xn = xv * r
        dx = refs[2 * np_ + 1 + nn][...] if has_res else None
        sums = []
        for acc, g_ref in zip(accs, g_refs):
            gdy = acc * g_ref[...]
            t = r * (gdy - xn * jnp.mean(gdy * xn, axis=-1, keepdims=True))
            dx = t if dx is None else dx + t
            sums.append(jnp.sum(acc * xn, axis=0, keepdims=True))
        dx_ref[...] = dx
        dxb_ref[...] = dx.astype(BF16)

        @pl.when(pl.program_id(0) == 0)
        def _():
            for dg_ref, part in zip(dg_refs, sums):
                dg_ref[...] = part

        @pl.when(pl.program_id(0) > 0)
        def _():
            for dg_ref, part in zip(dg_refs, sums):
                dg_ref[...] += part

    in_specs, args = [], []
    for a, a_spec, b, b_spec, _, _ in parts:
        in_specs += [a_spec, b_spec]
        args += [a, b]
    if norm_bwd is None:
        if has_add:
            in_specs.append(o_spec)
            args.append(add)
        return _pallas(body, name=name, grid=grid, in_specs=in_specs, out_specs=o_spec,
                       out_shape=jax.ShapeDtypeStruct(o_shape, o_dtype),
                       params=_params(("parallel",) * len(grid)))(*args)
    x, gains, dres = norm_bwd
    vec = pl.BlockSpec((1, o_shape[1]), lambda i: (0, 0))
    in_specs += [o_spec] + [vec] * nn + ([o_spec] if has_res else [])
    args += [x] + list(gains) + ([dres] if has_res else [])
    outs = _pallas(body, name=name, grid=grid, in_specs=in_specs, out_specs=[o_spec, o_spec] + [vec] * nn,
                   out_shape=[jax.ShapeDtypeStruct(o_shape, F32), jax.ShapeDtypeStruct(o_shape, BF16)]
                   + [jax.ShapeDtypeStruct((1, o_shape[1]), F32)] * nn,
                   params=_params(("arbitrary",)))(*args)
    return outs[0], outs[1], list(outs[2:])


def _mm_rows(name, a, b, dims, o_dtype, n_out, *, tn=None, add=None):
    k = a.shape[1]
    tn = n_out if tn is None else tn
    if dims == NN:
        b_spec = pl.BlockSpec((k, tn), lambda n, i: (0, n))
    else:
        b_spec = pl.BlockSpec((tn, k), lambda n, i: (n, 0))
    return _mm(name, a, b, grid=(n_out // tn, T // TM),
               a_spec=pl.BlockSpec((TM, k), lambda n, i: (i, 0)), b_spec=b_spec,
               o_spec=pl.BlockSpec((TM, tn), lambda n, i: (i, n)), o_shape=(T, n_out), o_dtype=o_dtype,
               dims=dims, add=add)


def _mm_wgrad(name, a, b, *, tn=512):
    k, n = a.shape[1], b.shape[1]
    tn = min(tn, n)
    return _mm(name, a, b, grid=(n // tn,),
               a_spec=pl.BlockSpec((T, k), lambda j: (0, 0)), b_spec=pl.BlockSpec((T, tn), lambda j: (0, j)),
               o_spec=pl.BlockSpec((k, tn), lambda j: (0, j)), o_shape=(k, n), o_dtype=BF16, dims=TN)


def _rms_fwd(name, x, g):
    d = x.shape[1]

    def body(x_ref, g_ref, o_ref):
        xv = x_ref[...]
        r = lax.rsqrt(jnp.mean(xv * xv, axis=-1, keepdims=True) + EPS)
        o_ref[...] = ((xv * r) * g_ref[...]).astype(BF16)

    return _pallas(
        body, name=name, grid=(T // TM,),
        in_specs=[pl.BlockSpec((TM, d), lambda i: (i, 0)), pl.BlockSpec((1, d), lambda i: (0, 0))],
        out_specs=pl.BlockSpec((TM, d), lambda i: (i, 0)),
        out_shape=jax.ShapeDtypeStruct((T, d), BF16), params=_params(("parallel",)))(x, g)


def _rows_call(name, body, row_ins, whole_ins, outs):
    in_specs = [pl.BlockSpec((TM, a.shape[1]), lambda i: (i, 0)) for a in row_ins]
    in_specs += [pl.BlockSpec(a.shape, lambda i: (0, 0)) for a in whole_ins]
    return _pallas(
        body, name=name, grid=(T // TM,), in_specs=in_specs,
        out_specs=[pl.BlockSpec((TM, d), lambda i: (i, 0)) for d, _ in outs],
        out_shape=[jax.ShapeDtypeStruct((T, d), dt) for d, dt in outs],
        params=_params(("parallel",)))(*row_ins, *whole_ins)


def _rms(xv, g):
    return (xv * lax.rsqrt(jnp.mean(xv * xv, axis=-1, keepdims=True) + EPS)) * g


def _rms_fwd2(name, x, g1, g2):
    d = x.shape[1]

    def body(x_ref, g1_ref, g2_ref, o1_ref, o2_ref):
        xv = x_ref[...]
        xn = xv * lax.rsqrt(jnp.mean(xv * xv, axis=-1, keepdims=True) + EPS)
        o1_ref[...] = (xn * g1_ref[...]).astype(BF16)
        o2_ref[...] = (xn * g2_ref[...]).astype(BF16)

    return _rows_call(name, body, [x], [g1, g2], [(d, BF16), (d, BF16)])


def _down_norm(name, a, w, g):
    n = w.shape[1]

    def body(a_ref, w_ref, g_ref, raw_ref, o_ref):
        raw = lax.dot_general(a_ref[...], w_ref[...], NN, preferred_element_type=F32)
        raw_ref[...] = raw
        o_ref[...] = _rms(raw, g_ref[...]).astype(BF16)

    return _rows_call(name, body, [a], [w, g], [(n, F32), (n, BF16)])


def _kv_down(hk, w_dkv, w_kr, g, tables):
    def body(a_ref, c_ref, sa_ref, sb_ref, wd_ref, wr_ref, g_ref, raw_ref, ckv_ref, kr_ref):
        av = a_ref[...]
        raw = lax.dot_general(av, wd_ref[...], NN, preferred_element_type=F32)
        raw_ref[...] = raw
        ckv_ref[...] = _rms(raw, g_ref[...]).astype(BF16)
        kr = lax.dot_general(av, wr_ref[...], NN, preferred_element_type=F32)
        kr_ref[...] = _rotate(kr, c_ref[...], sa_ref[...], sb_ref[...], 1.0).astype(BF16)

    return _rows_call("kv_down", body, [hk, *tables], [w_dkv, w_kr, g], [(KV_LORA, F32), (KV_LORA, BF16), (128, BF16)])


def _kv_up(ckv, w_uk, w_uv):
    def body(a_ref, wk_ref, wv_ref, k_ref, v_ref):
        av = a_ref[...]
        k_ref[...] = lax.dot_general(av, wk_ref[...], NN, preferred_element_type=F32).astype(BF16)
        v_ref[...] = lax.dot_general(av, wv_ref[...], NN, preferred_element_type=F32).astype(BF16)

    return _rows_call("kv_up", body, [ckv], [w_uk, w_uv], [(N_HEADS * QK_NOPE, BF16), (N_HEADS * V_HEAD, BF16)])


def _rms_bwd(name, x, gains, dys, dres=None):
    d = x.shape[1]
    n = len(gains)
    has_res = dres is not None

    def body(*refs):
        x_ref, g_refs, dy_refs = refs[0], refs[1:1 + n], refs[1 + n:1 + 2 * n]
        dx_ref, dxb_ref = refs[-2 - n], refs[-1 - n]
        dg_refs = refs[-n:]
        xv = x_ref[...]
        r = lax.rsqrt(jnp.mean(xv * xv, axis=-1, keepdims=True) + EPS)
        xn = xv * r
        dx = refs[1 + 2 * n][...] if has_res else None
        parts = []
        for g_ref, dy_ref in zip(g_refs, dy_refs):
            dyv = dy_ref[...].astype(F32)
            gdy = dyv * g_ref[...]
            t = r * (gdy - xn * jnp.mean(gdy * xn, axis=-1, keepdims=True))
            dx = t if dx is None else dx + t
            parts.append(jnp.sum(dyv * xn, axis=0, keepdims=True))
        dx_ref[...] = dx
        dxb_ref[...] = dx.astype(BF16)

        @pl.when(pl.program_id(0) == 0)
        def _():
            for dg_ref, part in zip(dg_refs, parts):
                dg_ref[...] = part

        @pl.when(pl.program_id(0) > 0)
        def _():
            for dg_ref, part in zip(dg_refs, parts):
                dg_ref[...] += part

    row = pl.BlockSpec((TR, d), lambda i: (i, 0))
    vec = pl.BlockSpec((1, d), lambda i: (0, 0))
    args = [x] + list(gains) + list(dys) + ([dres] if has_res else [])
    in_specs = [row] + [vec] * n + [row] * n + ([row] if has_res else [])
    outs = _pallas(
        body, name=name, grid=(T // TR,), in_specs=in_specs, out_specs=[row, row] + [vec] * n,
        out_shape=[jax.ShapeDtypeStruct((T, d), F32), jax.ShapeDtypeStruct((T, d), BF16)]
        + [jax.ShapeDtypeStruct((1, d), F32)] * n,
        params=_params(("arbitrary",)))(*args)
    return outs[0], outs[1], list(outs[2:])


def _final(h, g, tgt):
    def body(h_ref, g_ref, t_ref, loss_ref, dh_ref, dhb_ref, dg_ref):
        hv = h_ref[...]
        r = lax.rsqrt(jnp.mean(hv * hv, axis=-1, keepdims=True) + EPS)
        xn = hv * r
        gv = g_ref[...]
        err = xn * gv - t_ref[...]
        part_loss = 0.5 * jnp.sum(jnp.mean(err * err, axis=-1, keepdims=True), axis=0, keepdims=True)
        dy = err * (1.0 / D)
        gdy = dy * gv
        dh = r * (gdy - xn * jnp.mean(gdy * xn, axis=-1, keepdims=True))
        dh_ref[...] = dh
        dhb_ref[...] = dh.astype(BF16)
        part = jnp.sum(dy * xn, axis=0, keepdims=True)
        first = pl.program_id(0) == 0

        @pl.when(first)
        def _():
            dg_ref[...] = part
            loss_ref[...] = jnp.broadcast_to(part_loss, (1, 128))

        @pl.when(jnp.logical_not(first))
        def _():
            dg_ref[...] += part
            loss_ref[...] += jnp.broadcast_to(part_loss, (1, 128))

    row = pl.BlockSpec((TR, D), lambda i: (i, 0))
    vec = pl.BlockSpec((1, D), lambda i: (0, 0))
    return _pallas(
        body, name="final_loss", grid=(T // TR,), in_specs=[row, vec, row],
        out_specs=[pl.BlockSpec((1, 128), lambda i: (0, 0)), row, row, vec],
        out_shape=[jax.ShapeDtypeStruct((1, 128), F32), jax.ShapeDtypeStruct((T, D), F32),
                   jax.ShapeDtypeStruct((T, D), BF16), jax.ShapeDtypeStruct((1, D), F32)],
        params=_params(("arbitrary",)))(h, g, tgt)


def _prev_idx(i, rows=TR):
    return jnp.maximum(i * (rows // HALO) - 1, 0)


def _next_idx(i, rows=TR):
    return jnp.minimum((i + 1) * (rows // HALO), T // HALO - 1)


def _causal_taps(ext):
    return pltpu.roll(ext, 2, 0)[HALO:], pltpu.roll(ext, 1, 0)[HALO:], ext[HALO:]


def _anticausal_taps(ext, n):
    rows = ext.shape[0]
    return pltpu.roll(ext, rows - 1, 0)[:n], pltpu.roll(ext, rows - 2, 0)[:n]


def _sc_fwd(z, w):
    def body(b_ref, c_ref, ch_ref, u_ref, uh_ref, w_ref, y_ref):
        i = pl.program_id(0)
        cu = c_ref[...].astype(F32) * u_ref[...].astype(F32)
        cuh = ch_ref[...].astype(F32) * uh_ref[...].astype(F32)
        cuh = jnp.where(i > 0, cuh, 0.0)
        x2, x1, x0 = _causal_taps(jnp.concatenate([cuh, cu], axis=0))
        wv = w_ref[...]
        cv = (x2 * wv[0:1] + x1 * wv[1:2]) + x0 * wv[2:3]
        y_ref[...] = (b_ref[...].astype(F32) * cv).astype(BF16)

    def main(part):
        return pl.BlockSpec((TR, D), lambda i: (i, part))

    def halo(part):
        return pl.BlockSpec((HALO, D), lambda i: (_prev_idx(i), part))

    return _pallas(
        body, name="sc_fwd", grid=(T // TR,),
        in_specs=[main(0), main(1), halo(1), main(2), halo(2), pl.BlockSpec((3, D), lambda i: (0, 0))],
        out_specs=pl.BlockSpec((TR, D), lambda i: (i, 0)),
        out_shape=jax.ShapeDtypeStruct((T, D), BF16), params=_params(("parallel",)))(z, z, z, z, z, w)


def _sc_bwd(z, dy, w):
    last = T // TR - 1

    def body(b_ref, bn_ref, c_ref, ch_ref, u_ref, uh_ref, dy_ref, dyn_ref, w_ref, dz_ref, dw_ref):
        i = pl.program_id(0)
        cv_ = c_ref[...].astype(F32)
        uv = u_ref[...].astype(F32)
        cu = cv_ * uv
        cuh = jnp.where(i > 0, ch_ref[...].astype(F32) * uh_ref[...].astype(F32), 0.0)
        x2, x1, x0 = _causal_taps(jnp.concatenate([cuh, cu], axis=0))
        wv = w_ref[...]
        conv = (x2 * wv[0:1] + x1 * wv[1:2]) + x0 * wv[2:3]
        dyv = dy_ref[...]
        dz_ref[:, 0:D] = (dyv * conv).astype(BF16)
        dconv = dyv * b_ref[...].astype(F32)
        dconv_n = jnp.where(i < last, dyn_ref[...] * bn_ref[...].astype(F32), 0.0)
        n1, n2 = _anticausal_taps(jnp.concatenate([dconv, dconv_n], axis=0), TR)
        dcu = (dconv * wv[2:3] + n1 * wv[1:2]) + n2 * wv[0:1]
        dz_ref[:, D:2 * D] = (dcu * uv).astype(BF16)
        dz_ref[:, 2 * D:3 * D] = (dcu * cv_).astype(BF16)
        part = jnp.concatenate([jnp.sum(dconv * x2, axis=0, keepdims=True),
                                jnp.sum(dconv * x1, axis=0, keepdims=True),
                                jnp.sum(dconv * x0, axis=0, keepdims=True)], axis=0)

        @pl.when(i == 0)
        def _():
            dw_ref[...] = part

        @pl.when(i > 0)
        def _():
            dw_ref[...] += part

    def main(part):
        return pl.BlockSpec((TR, D), lambda i: (i, part))

    def prev(part):
        return pl.BlockSpec((HALO, D), lambda i: (_prev_idx(i), part))

    def nxt(part):
        return pl.BlockSpec((HALO, D), lambda i: (_next_idx(i), part))

    wspec = pl.BlockSpec((3, D), lambda i: (0, 0))
    return _pallas(
        body, name="sc_bwd", grid=(T // TR,),
        in_specs=[main(0), nxt(0), main(1), prev(1), main(2), prev(2), main(0), nxt(0), wspec],
        out_specs=[pl.BlockSpec((TR, 3 * D), lambda i: (i, 0)), wspec],
        out_shape=[jax.ShapeDtypeStruct((T, 3 * D), BF16), jax.ShapeDtypeStruct((3, D), F32)],
        params=_params(("arbitrary",)))(z, z, z, z, z, z, dy, dy, w)


def _sigmoid(x):
    return 1.0 / (1.0 + jnp.exp(-x))


def _ffn_up_act(name, hf, w_up, w, b):
    def body(h_ref, hh_ref, wg_ref, wv_ref, w_ref, b_ref, g_ref, v_ref, a_ref):
        i = pl.program_id(1)
        hv = h_ref[...]
        ge = lax.dot_general(jnp.concatenate([hh_ref[...], hv], axis=0), wg_ref[...], NT,
                             preferred_element_type=F32).astype(BF16)
        v = lax.dot_general(hv, wv_ref[...], NT, preferred_element_type=F32).astype(BF16)
        g_ref[...] = ge[HALO:]
        v_ref[...] = v
        ext = ge.astype(F32)
        row = lax.broadcasted_iota(jnp.int32, (HALO + TS, 1), 0)
        ext = jnp.where(jnp.logical_or(i > 0, row >= HALO), ext, 0.0)
        x2, x1, x0 = _causal_taps(ext)
        wv = w_ref[...]
        gc = ((x2 * wv[0:1] + x1 * wv[1:2]) + x0 * wv[2:3]) + b_ref[...]
        a_ref[...] = ((gc * _sigmoid(gc)) * v.astype(F32)).astype(BF16)

    blk = pl.BlockSpec((None, TS, FF_BLK), lambda j, i: (j, i, 0))
    out = jax.ShapeDtypeStruct((N_FF_BLK, T, FF_BLK), BF16)
    return _pallas(
        body, name=name, grid=(N_FF_BLK, T // TS),
        in_specs=[pl.BlockSpec((TS, D), lambda j, i: (i, 0)),
                  pl.BlockSpec((HALO, D), lambda j, i: (_prev_idx(i, TS), 0)),
                  pl.BlockSpec((None, None, FF_BLK, D), lambda j, i: (0, j, 0, 0)),
                  pl.BlockSpec((None, None, FF_BLK, D), lambda j, i: (0, j + N_FF_BLK, 0, 0)),
                  pl.BlockSpec((None, 3, FF_BLK), lambda j, i: (j, 0, 0)),
                  pl.BlockSpec((None, 1, FF_BLK), lambda j, i: (j, 0, 0))],
        out_specs=[blk, blk, blk], out_shape=[out, out, out],
        params=_params(("parallel", "parallel")))(hf, hf, w_up, w_up, w, b)


def _ffn_dact(name, dh, w_down4, g, v, w, b):
    last = T // TS - 1

    def body(dh_ref, dhn_ref, wd_ref, g_ref, gp_ref, gn_ref, v_ref, vn_ref, w_ref, b_ref, dg_ref, dv_ref, dw_ref, db_ref):
        i = pl.program_id(1)
        da = lax.dot_general(jnp.concatenate([dh_ref[...], dhn_ref[...]], axis=0), wd_ref[...], NT,
                             preferred_element_type=F32)
        row = lax.broadcasted_iota(jnp.int32, (TS + HALO, 1), 0)
        da = jnp.where(jnp.logical_or(i < last, row < TS), da, 0.0)
        gp = jnp.where(i > 0, gp_ref[...].astype(F32), 0.0)
        ext = jnp.concatenate([gp, g_ref[...].astype(F32), gn_ref[...].astype(F32)], axis=0)
        x2, x1, x0 = _causal_taps(ext)
        wv = w_ref[...]
        gc = ((x2 * wv[0:1] + x1 * wv[1:2]) + x0 * wv[2:3]) + b_ref[...]
        sg = _sigmoid(gc)
        vv = jnp.concatenate([v_ref[...].astype(F32), vn_ref[...].astype(F32)], axis=0)
        dv_ref[...] = (da[:TS] * (gc[:TS] * sg[:TS])).astype(BF16)
        dgc = (da * vv) * (sg * (1.0 + gc * (1.0 - sg)))
        n1, n2 = _anticausal_taps(dgc, TS)
        d0 = dgc[:TS]
        dg_ref[...] = ((d0 * wv[2:3] + n1 * wv[1:2]) + n2 * wv[0:1]).astype(BF16)
        part_w = jnp.concatenate([jnp.sum(d0 * x2[:TS], axis=0, keepdims=True),
                                  jnp.sum(d0 * x1[:TS], axis=0, keepdims=True),
                                  jnp.sum(d0 * x0[:TS], axis=0, keepdims=True)], axis=0)
        part_b = jnp.sum(d0, axis=0, keepdims=True)

        @pl.when(i == 0)
        def _():
            dw_ref[...] = part_w
            db_ref[...] = part_b

        @pl.when(i > 0)
        def _():
            dw_ref[...] += part_w
            db_ref[...] += part_b

    blk = pl.BlockSpec((None, TS, FF_BLK), lambda j, i: (j, i, 0))
    prev = pl.BlockSpec((None, HALO, FF_BLK), lambda j, i: (j, _prev_idx(i, TS), 0))
    nxt = pl.BlockSpec((None, HALO, FF_BLK), lambda j, i: (j, _next_idx(i, TS), 0))
    wspec = pl.BlockSpec((None, 3, FF_BLK), lambda j, i: (j, 0, 0))
    bspec = pl.BlockSpec((None, 1, FF_BLK), lambda j, i: (j, 0, 0))
    return _pallas(
        body, name=name, grid=(N_FF_BLK, T // TS),
        in_specs=[pl.BlockSpec((TS, D), lambda j, i: (i, 0)),
                  pl.BlockSpec((HALO, D), lambda j, i: (_next_idx(i, TS), 0)),
                  pl.BlockSpec((None, None, FF_BLK, D), lambda j, i: (0, j, 0, 0)),
                  blk, prev, nxt, blk, nxt, wspec, bspec],
        out_specs=[blk, blk, wspec, bspec],
        out_shape=[jax.ShapeDtypeStruct((N_FF_BLK, T, FF_BLK), BF16), jax.ShapeDtypeStruct((N_FF_BLK, T, FF_BLK), BF16),
                   jax.ShapeDtypeStruct((N_FF_BLK, 3, FF_BLK), F32), jax.ShapeDtypeStruct((N_FF_BLK, 1, FF_BLK), F32)],
        params=_params(("parallel", "arbitrary")))(dh, dh, w_down4, g, g, g, v, v, w, b)


def _rope_tables(pos, inv_freq):
    half = QK_ROPE // 2

    def body(p_ref, f_ref, c_ref, sa_ref, sb_ref):
        ang = p_ref[...].astype(F32) * f_ref[...]
        lane = lax.broadcasted_iota(jnp.int32, (T, 128), 1)
        c = jnp.cos(ang)
        s = jnp.sin(ang)
        c_ref[...] = jnp.where(lane < 2 * half, c, 0.0)
        sa_ref[...] = jnp.where(lane < half, -s, 0.0)
        sb_ref[...] = jnp.where(jnp.logical_and(lane >= half, lane < 2 * half), s, 0.0)

    return _pallas(
        body, name="rope_tables", in_specs=[VMEM_SPEC] * 2, out_specs=[VMEM_SPEC] * 3,
        out_shape=[jax.ShapeDtypeStruct((T, 128), F32)] * 3,
        params=pltpu.CompilerParams(vmem_limit_bytes=VMEM_LIMIT))(pos, inv_freq)


def _rotate(r, c, sa, sb, sign):
    return r * c + sign * (pltpu.roll(r, 96, 1) * sa + pltpu.roll(r, 32, 1) * sb)


def _q_up(cq, w_uq, tables):
    cos, sa, sb = tables

    def body(a_ref, b_ref, c_ref, sa_ref, sb_ref, o_ref):
        for h in range(N_HEADS):
            r = lax.dot_general(a_ref[...], b_ref[h], NN, preferred_element_type=F32)
            o_ref[h, :, :QK_NOPE] = r[:, :QK_NOPE].astype(BF16)
            o_ref[h, :, QK_NOPE:] = _rotate(r[:, QK_NOPE:], c_ref[...], sa_ref[...], sb_ref[...], 1.0).astype(BF16)

    tab = pl.BlockSpec((TS, 128), lambda i: (i, 0))
    return _pallas(
        body, name="q_up", grid=(T // TS,),
        in_specs=[pl.BlockSpec((TS, Q_LORA), lambda i: (i, 0)),
                  pl.BlockSpec((N_HEADS, Q_LORA, QK_PAD), lambda i: (0, 0, 0)), tab, tab, tab],
        out_specs=pl.BlockSpec((N_HEADS, TS, QK_PAD), lambda i: (0, i, 0)),
        out_shape=jax.ShapeDtypeStruct((N_HEADS, T, QK_PAD), BF16),
        params=_params(("parallel",)))(cq, w_uq, cos, sa, sb)


def _rope(name, x, tables, sign, out_dtype, reduce_groups=False):
    g, _, w = x.shape
    cos, sa, sb = tables

    def body(x_ref, c_ref, sa_ref, sb_ref, o_ref):
        xv = x_ref[...].astype(F32)
        if reduce_groups:
            acc = xv[0]
            for k in range(1, g):
                acc = acc + xv[k]
            xv = acc
        out = _rotate(xv[:, w - 128:], c_ref[...], sa_ref[...], sb_ref[...], sign)
        if w > 128:
            o_ref[:, :w - 128] = xv[:, :w - 128].astype(out_dtype)
        o_ref[:, w - 128:] = out.astype(out_dtype)

    tab = pl.BlockSpec((TM, 128), lambda h, i: (i, 0))
    if reduce_groups:
        x_spec = pl.BlockSpec((g, TM, w), lambda h, i: (0, i, 0))
        groups = 1
    else:
        x_spec = pl.BlockSpec((None, TM, w), lambda h, i: (h, i, 0))
        groups = g
    return _pallas(
        body, name=name, grid=(groups, T // TM), in_specs=[x_spec, tab, tab, tab],
        out_specs=pl.BlockSpec((None, TM, w), lambda h, i: (h, i, 0)),
        out_shape=jax.ShapeDtypeStruct((groups, T, w), out_dtype),
        params=_params(("parallel", "parallel")))(x, cos, sa, sb)


SCALE = (QK_NOPE + QK_ROPE) ** -0.5
LOG2E = 1.4426950408889634
SCALE2 = SCALE * LOG2E


def _diag_mask(transposed):
    shift = CHUNK.bit_length() - 1
    a = lax.broadcasted_iota(jnp.int32, (TQ, TQ), 0) >> shift
    b = lax.broadcasted_iota(jnp.int32, (TQ, TQ), 1) >> shift
    return (a <= b) if transposed else (b <= a)


def _as_row(col):
    return jnp.transpose(jnp.broadcast_to(col, (col.shape[0], 128)), (1, 0))[0:1]


def _keys(kn_ref, kr_ref, off):
    return jnp.concatenate([kn_ref[pl.ds(off, TQ), :], kr_ref[pl.ds(off, TQ), :]], axis=1)


def _attn_fwd(q, kn, kr, v):
    def body(q_ref, kn_ref, kr_ref, v_ref, o_ref, lse_ref):
        i = pl.program_id(1)
        qv = q_ref[...]

        def step(j, carry, masked):
            m, l, acc = carry
            off = pl.multiple_of(j * TQ, TQ)
            s = lax.dot_general(qv, _keys(kn_ref, kr_ref, off), NT, preferred_element_type=F32) * SCALE2
            if masked:
                s = jnp.where(_diag_mask(False), s, NEG_INF)
            m_new = jnp.maximum(m, jnp.max(s, axis=-1, keepdims=True))
            p = jnp.exp2(s - m_new)
            alpha = jnp.exp2(m - m_new)
            l = alpha * l + jnp.sum(p, axis=-1, keepdims=True)
            acc = alpha * acc + lax.dot_general(p.astype(BF16), v_ref[pl.ds(off, TQ), :], NN, preferred_element_type=F32)
            return m_new, l, acc

        init = (jnp.full((TQ, 1), NEG_INF, F32), jnp.zeros((TQ, 1), F32), jnp.zeros((TQ, V_HEAD), F32))
        carry = lax.fori_loop(0, i, lambda j, cr: step(j, cr, False), init)
        m, l, acc = step(i, carry, True)
        o_ref[...] = (acc / l).astype(BF16)
        lse_ref[...] = _as_row(m + jnp.log(l) * LOG2E)

    return _pallas(
        body, name="attn_fwd", grid=(N_HEADS, T // TQ),
        in_specs=[pl.BlockSpec((None, TQ, QK_PAD), lambda h, i: (h, i, 0)),
                  pl.BlockSpec((T, QK_NOPE), lambda h, i: (0, h)),
                  pl.BlockSpec((T, 128), lambda h, i: (0, 0)),
                  pl.BlockSpec((T, V_HEAD), lambda h, i: (0, h))],
        out_specs=[pl.BlockSpec((TQ, V_HEAD), lambda h, i: (i, h)), pl.BlockSpec((None, 1, TQ), lambda h, i: (h, 0, i))],
        out_shape=[jax.ShapeDtypeStruct((T, N_HEADS * V_HEAD), BF16), jax.ShapeDtypeStruct((N_HEADS, 1, T), F32)],
        params=_params(("parallel", "parallel")))(q, kn, kr, v)


def _attn_bwd(q, kn, kr, v, o, do, lse_row, tables):
    nq = T // TQ
    cos, sa, sb = tables

    def body(q_ref, kn_ref, kr_ref, v_ref, o_ref, do_ref, lse_ref, c_ref, sa_ref, sb_ref,
             dq_ref, dkn_ref, dkr_ref, dv_ref, dq_acc, dl_ref):
        j = pl.program_id(1)

        @pl.when(j == 0)
        def _():
            dq_acc[...] = jnp.zeros_like(dq_acc)
            for i in range(nq):
                rows = pl.ds(i * TQ, TQ)
                prod = do_ref[rows, :].astype(F32) * o_ref[rows, :].astype(F32)
                dl_ref[:, rows] = _as_row(jnp.sum(prod, axis=-1, keepdims=True))

        kk = jnp.concatenate([kn_ref[...], kr_ref[...]], axis=1)
        vv = v_ref[...]

        def step(i, carry, masked):
            dk, dv = carry
            off = pl.multiple_of(i * TQ, TQ)
            qi = q_ref[pl.ds(off, TQ), :]
            doi = do_ref[pl.ds(off, TQ), :]
            st = lax.dot_general(kk, qi, NT, preferred_element_type=F32) * SCALE2
            if masked:
                st = jnp.where(_diag_mask(True), st, NEG_INF)
            pt = jnp.exp2(st - lse_ref[:, pl.ds(off, TQ)])
            dv = dv + lax.dot_general(pt.astype(BF16), doi, NN, preferred_element_type=F32)
            dpt = lax.dot_general(vv, doi, NT, preferred_element_type=F32)
            dst = ((pt * (dpt - dl_ref[:, pl.ds(off, TQ)])) * SCALE).astype(BF16)
            dk = dk + lax.dot_general(dst, qi, NN, preferred_element_type=F32)
            dq_acc[pl.ds(off, TQ), :] += lax.dot_general(dst, kk, TN, preferred_element_type=F32)
            return dk, dv

        carry = step(j, (jnp.zeros((TQ, QK_PAD), F32), jnp.zeros((TQ, V_HEAD), F32)), True)
        dk, dv = lax.fori_loop(j + 1, nq, lambda i, cr: step(i, cr, False), carry)
        dkn_ref[...] = dk[:, :QK_NOPE].astype(BF16)
        dkr_ref[...] = dk[:, QK_NOPE:]
        dv_ref[...] = dv.astype(BF16)

        @pl.when(j == nq - 1)
        def _():
            dq = dq_acc[...]
            dq_ref[:, :QK_NOPE] = dq[:, :QK_NOPE].astype(BF16)
            dq_ref[:, QK_NOPE:] = _rotate(dq[:, QK_NOPE:], c_ref[...], sa_ref[...], sb_ref[...], -1.0).astype(BF16)

    row = pl.BlockSpec((None, 1, T), lambda h, j: (h, 0, 0))
    head = pl.BlockSpec((TQ, 128), lambda h, j: (j, h))
    whole = pl.BlockSpec((None, T, QK_PAD), lambda h, j: (h, 0, 0))
    tab = pl.BlockSpec((T, 128), lambda h, j: (0, 0))
    heads = pl.BlockSpec((T, V_HEAD), lambda h, j: (0, h))
    return _pallas(
        body, name="attn_bwd", grid=(N_HEADS, nq),
        in_specs=[whole, head, pl.BlockSpec((TQ, 128), lambda h, j: (j, 0)), head, heads, heads, row, tab, tab, tab],
        out_specs=[whole, head, pl.BlockSpec((None, TQ, 128), lambda h, j: (h, j, 0)), head],
        out_shape=[jax.ShapeDtypeStruct((N_HEADS, T, QK_PAD), BF16), jax.ShapeDtypeStruct((T, N_HEADS * QK_NOPE), BF16),
                   jax.ShapeDtypeStruct((N_HEADS, T, 128), F32), jax.ShapeDtypeStruct((T, N_HEADS * V_HEAD), BF16)],
        scratch_shapes=[pltpu.VMEM((T, QK_PAD), F32), pltpu.VMEM((1, T), F32)],
        params=_params(("parallel", "arbitrary")))(q, kn, kr, v, o, do, lse_row, cos, sa, sb)


def _ffn_gup(name, dg, dv, hf):
    def body(dg_ref, dv_ref, hf_ref, o_ref):
        j = pl.program_id(0)

        @pl.when(j < N_FF_BLK)
        def _():
            o_ref[...] = lax.dot_general(dg_ref[...], hf_ref[...], TN, preferred_element_type=F32).astype(BF16)

        @pl.when(j >= N_FF_BLK)
        def _():
            o_ref[...] = lax.dot_general(dv_ref[...], hf_ref[...], TN, preferred_element_type=F32).astype(BF16)

    return _pallas(
        body, name=name, grid=(N_DEV,),
        in_specs=[pl.BlockSpec((None, T, FF_BLK), lambda j: (jnp.minimum(j, N_FF_BLK - 1), 0, 0)),
                  pl.BlockSpec((None, T, FF_BLK), lambda j: (jnp.maximum(j - N_FF_BLK, 0), 0, 0)),
                  pl.BlockSpec((T, D), lambda j: (0, 0))],
        out_specs=pl.BlockSpec((None, FF_BLK, D), lambda j: (j, 0, 0)),
        out_shape=jax.ShapeDtypeStruct((N_DEV, FF_BLK, D), BF16), params=_params(("parallel",)))(dg, dv, hf)


def _ffn_layer_fwd(tag, h, gain, ex):
    hf = _rms_fwd(f"{tag}_norm", h, gain)
    g, v, act = _ffn_up_act(f"{tag}_up", hf, ex.need(f"ffn_w_up{tag[1]}", hf), ex.need(f"ffn_cw{tag[1]}", hf),
                            ex.need(f"ffn_cb{tag[1]}", hf))
    ex.at(f"{tag}_up", act)
    rows = pl.BlockSpec((TS, D), lambda i: (i, 0))
    out = _mm_sum(f"{tag}_down",
                  [(act, pl.BlockSpec((N_FF_BLK, TS, FF_BLK), lambda i: (0, i, 0)), ex.need(f"ffn_w_down{tag[1]}", act),
                    pl.BlockSpec((None, N_FF_BLK, FF_BLK, D), lambda i: (0, 0, 0, 0)), NN, 0)],
                  grid=(T // TS,), o_spec=rows, o_shape=(T, D), o_dtype=F32, add=h)
    ex.at(f"{tag}_down", out)
    return out, (hf, g, v, act)


def _ffn_layer_bwd(tag, h, gain, ex, saved, dh, dh_bf):
    hf, g, v, act = saved
    layer = tag[1]
    w_up, w_down4 = ex.need(f"ffn_w_up{layer}", dh_bf), ex.need(f"ffn_w_down{layer}", dh_bf)
    dg, dv, dcw, dcb = _ffn_dact(f"{tag}_dact", dh_bf, w_down4, g, v, ex.need(f"ffn_cw{layer}", dh_bf),
                                 ex.need(f"ffn_cb{layer}", dh_bf))
    ex.at(f"{tag}_dact", dg)
    g_down = _mm(f"{tag}_gdown", act, dh_bf, grid=(N_FF_BLK,),
                 a_spec=pl.BlockSpec((None, T, FF_BLK), lambda j: (j, 0, 0)),
                 b_spec=pl.BlockSpec((T, D), lambda j: (0, 0)),
                 o_spec=pl.BlockSpec((FF_BLK, D), lambda j: (j, 0)),
                 o_shape=(D_FF, D), o_dtype=BF16, dims=TN)
    g_up = _ffn_gup(f"{tag}_gup", dg, dv, hf)
    ex.grad("ffn_w_up", int(layer), g_up.reshape(1, N_DEV, FF_BLK, D))
    ex.grad("ffn_w_down", int(layer), g_down.reshape(1, N_DEV, D_FF // N_DEV, D))
    ex.at(f"{tag}_gup", g_up)
    part = pl.BlockSpec((N_FF_BLK, TR, FF_BLK), lambda i: (0, i, 0))
    dh_in, dh_in_bf, dgain = _mm_sum(
        f"{tag}_dhf",
        [(dg, part, w_up, pl.BlockSpec((None, N_FF_BLK, FF_BLK, D), lambda i: (0, 0, 0, 0)), NN, 0),
         (dv, part, w_up, pl.BlockSpec((None, N_FF_BLK, FF_BLK, D), lambda i: (0, 1, 0, 0)), NN, 0)],
        grid=(T // TR,), o_spec=pl.BlockSpec((TR, D), lambda i: (i, 0)), o_shape=(T, D), o_dtype=F32,
        norm_bwd=(h, [gain], dh))
    ex.at(f"{tag}_dhf", dh_in)
    return dh_in, dh_in_bf, dgain[0], dcw, dcb


def _local_step(x, pos, tgt, rep, ex):
    attn_norm, ffn_norm, final_norm = rep["attn_norm"], rep["ffn_norm"], rep["final_norm"]
    half = QK_ROPE // 2
    inv = 1.0 / (ROPE_THETA ** (jnp.arange(half, dtype=F32) / half))
    inv_freq = jnp.concatenate([inv, inv, jnp.zeros((128 - 2 * half,), F32)]).reshape(1, 128)
    tables = _rope_tables(pos, inv_freq)

    hn0 = _rms_fwd("l0_norm", x, attn_norm[0:1])
    w_in = ex.need("sc_w_in", hn0)
    ex.at("mixer_ready", hn0)
    z = _mm_rows("l0_in", hn0, w_in, NN, BF16, 3 * D, tn=512)
    ex.at("l0_in", z)
    y = _sc_fwd(z, ex.need("sc_conv_w", z))
    h1 = _mm_rows("l0_out", y, ex.need("sc_w_out", y), NN, F32, D, tn=512, add=x)
    ex.at("l0_out", h1)
    h2, ffn0 = _ffn_layer_fwd("f0", h1, ffn_norm[0:1], ex)

    hk, hn1 = _rms_fwd2("h2_norms", h2, rep["kv_in_norm"], attn_norm[1:2])
    ckv_raw, ckv, kr = _kv_down(hk, ex.need("w_dkv", hk), ex.need("w_kr", hk), rep["kv_latent_norm"], tables)
    kn, vv = _kv_up(ckv, ex.need("w_uk", ckv), ex.need("w_uv", ckv))

    cq_raw, cq = _down_norm("q_down", hn1, ex.need("w_dq", hn1), rep["q_latent_norm"])
    w_uq = ex.need("w_uq", cq)
    q = _q_up(cq, w_uq, tables)
    o, lse = _attn_fwd(q, kn, kr, vv)
    ex.at("attn_fwd", o)
    w_o = ex.need("w_o", o)
    h3 = _mm_rows("attn_out", o, w_o, NN, F32, D, tn=512, add=h2)
    h4, ffn1 = _ffn_layer_fwd("f1", h3, ffn_norm[1:2], ex)

    loss, dh4, dh4_bf, d_final = _final(h4, final_norm.reshape(1, D), tgt)

    dh3, dh3_bf, d_fn1, dcw1, dcb1 = _ffn_layer_bwd("f1", h3, ffn_norm[1:2], ex, ffn1, dh4, dh4_bf)
    ex.at("f1_bwd", dh3)

    do = _mm_rows("d_attn_out", dh3_bf, w_o, NT, BF16, N_HEADS * V_HEAD)
    ex.grad("w_o", None, _mm_wgrad("g_w_o", o, dh3_bf).reshape(1, N_DEV, D // N_DEV, D))
    dq_pre, dkn, dkr, dvv = _attn_bwd(q, kn, kr, vv, o, do, lse, tables)
    def rows_of(a):
        return a[None], pl.BlockSpec((1, TS, a.shape[1]), lambda i: (0, i, 0))

    def whole(wt):
        return wt[None], pl.BlockSpec((1,) + wt.shape, lambda i: (0, 0, 0))

    def row_blocks(d):
        return dict(grid=(T // TS,), o_spec=pl.BlockSpec((TS, d), lambda i: (i, 0)), o_shape=(T, d), o_dtype=F32)

    _, dcq_raw_bf, (d_qln,) = _mm_sum(
        "d_q_up", [(dq_pre, pl.BlockSpec((N_HEADS, TS, QK_PAD), lambda i: (0, i, 0)),
                    w_uq, pl.BlockSpec((N_HEADS, Q_LORA, QK_PAD), lambda i: (0, 0, 0)), NT, 0)],
        norm_bwd=(cq_raw, [rep["q_latent_norm"]], None), **row_blocks(Q_LORA))
    g_uq = _mm("g_w_uq", cq, dq_pre, grid=(N_HEADS,),
               a_spec=pl.BlockSpec((T, Q_LORA), lambda h: (0, 0)),
               b_spec=pl.BlockSpec((None, T, QK_PAD), lambda h: (h, 0, 0)),
               o_spec=pl.BlockSpec((None, Q_LORA, QK_PAD), lambda h: (h, 0, 0)),
               o_shape=(N_HEADS, Q_LORA, QK_PAD), o_dtype=BF16, dims=TN)
    ex.grad("w_uq", None, g_uq[:, :, :QK_NOPE + QK_ROPE].reshape(1, N_DEV, Q_LORA, QK_NOPE + QK_ROPE))
    ex.grad("w_dq", None, _mm_wgrad("g_w_dq", hn1, dcq_raw_bf).reshape(1, N_DEV, D // N_DEV, Q_LORA))

    _, dckv_raw_bf, (d_kvln,) = _mm_sum(
        "d_kv_up", [(*rows_of(dkn), *whole(ex.need("w_uk", dkn)), NT, 0),
                    (*rows_of(dvv), *whole(ex.need("w_uv", dvv)), NT, 0)],
        norm_bwd=(ckv_raw, [rep["kv_latent_norm"]], None), **row_blocks(KV_LORA))
    ex.grad("w_uk", None, _mm_wgrad("g_w_uk", ckv, dkn))
    ex.grad("w_uv", None, _mm_wgrad("g_w_uv", ckv, dvv))
    dkr_raw_bf = _rope("dk_rope", dkr, tables, -1.0, BF16, reduce_groups=True).reshape(T, 128)
    ex.grad("w_dkv", None, _mm_wgrad("g_w_dkv", hk, dckv_raw_bf).reshape(1, N_DEV, D // N_DEV, KV_LORA))
    ex.grad("w_kr", None, _mm_wgrad("g_w_kr", hk, dkr_raw_bf)[:, :QK_ROPE].reshape(1, N_DEV, D // N_DEV, QK_ROPE))

    dh2, dh2_bf, (d_an1, d_kvin) = _mm_sum(
        "d_h2", [(*rows_of(dcq_raw_bf), *whole(ex.need("w_dq", dcq_raw_bf)), NT, 0),
                 (*rows_of(dckv_raw_bf), *whole(ex.need("w_dkv", dckv_raw_bf)), NT, 1),
                 (*rows_of(dkr_raw_bf), *whole(ex.need("w_kr", dkr_raw_bf)), NT, 1)],
        norm_bwd=(h2, [attn_norm[1:2], rep["kv_in_norm"]], dh3), **row_blocks(D))
    ex.at("kv_bwd", dh2)

    dh1, dh1_bf, d_fn0, dcw0, dcb0 = _ffn_layer_bwd("f0", h1, ffn_norm[0:1], ex, ffn0, dh2, dh2_bf)
    ex.at("f0_bwd", dh1)

    dy = _mm_rows("d_l0_out", dh1_bf, ex.need("sc_w_out", dh1_bf), NT, F32, D)
    ex.grad("sc_w_out", None, _mm_wgrad("g_sc_w_out", y, dh1_bf).reshape(1, N_DEV, D // N_DEV, D))
    dz, d_scw = _sc_bwd(z, dy, ex.need("sc_conv_w", dy))
    g_in = _mm_wgrad("g_sc_w_in", hn0, dz)
    ex.grad("sc_w_in", None, g_in)
    ex.at("sc_bwd", g_in)
    ex.at("d_l0_in", g_in)
    grad_x, _, (d_an0,) = _mm_sum(
        "d_l0_in", [(*rows_of(dz), *whole(ex.need("sc_w_in", dz)), NT, 0)],
        norm_bwd=(x, [attn_norm[0:1]], dh1), **row_blocks(D))

    small = {
        "attn_norm": jnp.concatenate([d_an0, d_an1], axis=0),
        "ffn_norm": jnp.concatenate([d_fn0, d_fn1], axis=0),
        "final_norm": d_final.reshape(D),
        "kv_in_norm": d_kvin.reshape(D),
        "kv_latent_norm": d_kvln.reshape(KV_LORA),
        "q_latent_norm": d_qln,
        "ffn_conv_b": jnp.stack([dcb0, dcb1]).transpose(0, 2, 1, 3).reshape(2, D_FF),
        "sc_conv_w": d_scw,
        "ffn_conv_w": jnp.stack([dcw0, dcw1]).transpose(0, 2, 1, 3).reshape(2, 3, D_FF),
    }
    return loss, grad_x, small


def _place():
    return lax.axis_index("x"), lax.axis_index("y"), lax.axis_index("c")


def _peers():
    x, y, c = _place()
    return (x, y, 1 - c), [(1 - x, y), (x, 1 - y), (1 - x, 1 - y)]


def _window(ref, kind, dev):
    if kind == "blocked":
        return ref.at[:, dev]
    width = ref.shape[-1] // N_DEV
    return ref.at[:, pl.ds(pl.multiple_of(dev * width, 128), width)]


def _all_gather(name, items):
    n = len(items)
    out_shapes = []
    for shard, kind in items:
        if kind == "blocked":
            shape = (shard.shape[0], N_DEV) + shard.shape[1:]
        else:
            shape = (shard.shape[0], N_DEV * shard.shape[1])
        out_shapes.append(jax.ShapeDtypeStruct(shape, shard.dtype))

    def body(*refs):
        srcs, outs = refs[:n], refs[n:2 * n]
        send_sems, recv_sems, local_sems = refs[2 * n:]
        x, y, c = _place()
        me = 4 * x + 2 * y + c
        sibling, chips = _peers()

        def num(px, py, pc):
            return 4 * px + 2 * py + pc

        def copy(t, k, dev, to, from_src):
            kind = items[t][1]
            dst = _window(outs[t], kind, dev)
            return pltpu.make_async_remote_copy(
                src_ref=srcs[t] if from_src else dst, dst_ref=dst,
                send_sem=send_sems.at[t, k], recv_sem=recv_sems.at[t, k], device_id=to, device_id_type=MESH)

        mine = [pltpu.make_async_copy(srcs[t], _window(outs[t], items[t][1], me), local_sems.at[t]) for t in range(n)]
        for cp in mine:
            cp.start()
        first = []
        for t in range(n):
            first.append(copy(t, 0, me, sibling, True))
            for j, chip in enumerate(chips):
                first.append(copy(t, 1 + j, me, (*chip, c), True))
        for cp in first:
            cp.start()
        passed = []
        for j, chip in enumerate(chips):
            for t in range(n):
                copy(t, 1 + j, num(*chip, c), (x, y, c), False).wait_recv()
                fwd = copy(t, 4 + j, num(*chip, c), sibling, False)
                fwd.start()
                passed.append(fwd)
        for t in range(n):
            copy(t, 0, num(x, y, 1 - c), (x, y, c), False).wait_recv()
            for j, chip in enumerate(chips):
                copy(t, 4 + j, num(*chip, 1 - c), (x, y, c), False).wait_recv()
        for cp in first + passed:
            cp.wait_send()
        for cp in mine:
            cp.wait()

    return _pallas(
        body, name=name, in_specs=[ANY_SPEC] * n, out_specs=[ANY_SPEC] * n, out_shape=out_shapes,
        scratch_shapes=[pltpu.SemaphoreType.DMA((n, 7)), pltpu.SemaphoreType.DMA((n, 7)), pltpu.SemaphoreType.DMA((n,))],
    )(*[s for s, _ in items])


HBM_SPEC = pl.BlockSpec(memory_space=pltpu.HBM)
SEM_SPEC = pl.BlockSpec(memory_space=pltpu.SEMAPHORE)
EFFECT = pltpu.SideEffectType.DATAFLOW_SIDE_EFFECTING
TOKEN = jax.ShapeDtypeStruct((8, 128), F32)


def _hbm(a):
    return pltpu.with_memory_space_constraint(a, pltpu.HBM)


def _copies_start(name, jobs):
    nj = len(jobs)
    counts = [(len(srcs), len(lands)) for srcs, lands, _, _ in jobs]
    n_arr = sum(ns + nl for ns, nl in counts)

    def body(*refs):
        sems, token = refs[n_arr:n_arr + 2 * nj], refs[-1]
        at = 0
        for j, ((ns, nl), (_, _, ncopy, plan)) in enumerate(zip(counts, jobs)):
            copies = plan(refs[at:at + ns], refs[at + ns:at + ns + nl])
            assert len(copies) == ncopy
            for k, (sent, dst, to, _) in enumerate(copies):
                pltpu.make_async_remote_copy(src_ref=sent, dst_ref=dst, send_sem=sems[2 * j].at[k],
                                             recv_sem=sems[2 * j + 1].at[k], device_id=to, device_id_type=MESH).start()
            at += ns + nl
        token[...] = jnp.zeros_like(token)

    arrays = [a for srcs, lands, _, _ in jobs for a in list(srcs) + list(lands)]
    sem_shapes = [pltpu.SemaphoreType.DMA((ncopy,)) for _, _, ncopy, _ in jobs for _ in range(2)]
    outs = pl.pallas_call(
        body, name=name, in_specs=[HBM_SPEC] * n_arr,
        out_specs=[SEM_SPEC] * (2 * nj) + [HBM_SPEC] * n_arr + [VMEM_SPEC],
        out_shape=sem_shapes + [pltpu.HBM(a.shape, a.dtype) for a in arrays] + [TOKEN],
        input_output_aliases={i: 2 * nj + i for i in range(n_arr)},
        compiler_params=pltpu.CompilerParams(has_side_effects=EFFECT))(*[_hbm(a) for a in arrays])
    _Chain.last = outs[-1]
    flights, at = [], 2 * nj
    for j, (ns, nl) in enumerate(counts):
        flights.append((outs[2 * j], outs[2 * j + 1], list(outs[at:at + ns]), list(outs[at + ns:at + ns + nl])))
        at += ns + nl
    return flights


def _copies_wait(name, started, ncopy, plan):
    send, recv, srcs, lands = started
    ns, nl = len(srcs), len(lands)

    def body(*refs):
        send_ref, recv_ref, token = refs[ns + nl], refs[ns + nl + 1], refs[-1]
        copies = plan(refs[:ns], refs[ns:ns + nl])
        assert len(copies) == ncopy
        for k, (sent, _, to, landed) in enumerate(copies):
            cp = pltpu.make_async_remote_copy(src_ref=sent, dst_ref=landed, send_sem=send_ref.at[k],
                                              recv_sem=recv_ref.at[k], device_id=to, device_id_type=MESH)
            cp.wait_send()
            cp.wait_recv()
        token[...] = jnp.zeros_like(token)

    arrays = list(srcs) + list(lands)
    outs = pl.pallas_call(
        body, name=name, in_specs=[HBM_SPEC] * (ns + nl) + [SEM_SPEC] * 2 + [ANY_SPEC],
        out_specs=[HBM_SPEC] * (ns + nl) + [VMEM_SPEC], out_shape=[pltpu.HBM(a.shape, a.dtype) for a in arrays] + [TOKEN],
        input_output_aliases={i: i for i in range(ns + nl)},
        compiler_params=pltpu.CompilerParams(has_side_effects=EFFECT))(*arrays, send, recv, _Chain.last)
    _Chain.last = outs[-1]
    return list(outs[:ns]), list(outs[ns:-1])


def _plan_gather_chips(kinds):
    def plan(srcs, lands):
        x, y, c = _place()
        sibling, chips = _peers()
        out = []
        for t, kind in enumerate(kinds):
            mine = _window(lands[t], kind, 4 * x + 2 * y + c)
            out.append((srcs[t], mine, (x, y, c), mine))
            out.append((srcs[t], mine, sibling, _window(lands[t], kind, 4 * x + 2 * y + 1 - c)))
            for px, py in chips:
                out.append((srcs[t], mine, (px, py, c), _window(lands[t], kind, 4 * px + 2 * py + c)))
        return out
    return plan, 5 * len(kinds)


def _plan_gather_sibling(kinds):
    def plan(srcs, lands):
        _, _, c = _place()
        sibling, chips = _peers()
        out = []
        for t, kind in enumerate(kinds):
            for px, py in chips:
                w = _window(lands[t], kind, 4 * px + 2 * py + c)
                out.append((w, w, sibling, _window(lands[t], kind, 4 * px + 2 * py + 1 - c)))
        return out
    return plan, 3 * len(kinds)


def _plan_scatter_sibling(kinds):
    def plan(srcs, lands):
        _, _, c = _place()
        sibling, _ = _peers()
        out = []
        for t, kind in enumerate(kinds):
            for k in range(N_CHIP):
                out.append((_window(srcs[t], kind, 2 * k + 1 - c), lands[t].at[k], sibling, lands[t].at[k]))
        return out
    return plan, N_CHIP * len(kinds)


def _plan_scatter_chips(n):
    def plan(srcs, lands):
        x, y, c = _place()
        _, chips = _peers()
        out = []
        for t in range(n):
            for px, py in chips:
                out.append((srcs[t].at[2 * px + py], lands[t].at[2 * x + y], (px, py, c), lands[t].at[2 * px + py]))
        return out
    return plan, 3 * n


def _landing(shard, kind):
    if kind == "blocked":
        return lax.empty((shard.shape[0], N_DEV) + shard.shape[1:], shard.dtype)
    return lax.empty((shard.shape[0], N_DEV * shard.shape[1]), shard.dtype)


def _chip_sums(name, grads, kinds, recvs, c):
    n = len(grads)
    in_specs, out_specs, out_shape, args = [], [], [], []
    for gr, kind, rv in zip(grads, kinds, recvs):
        if kind == "blocked":
            rows, w = gr.shape[2], gr.shape[3]
            in_specs.append(pl.BlockSpec((None, None, rows, w), lambda k, cref: (0, 2 * k + cref[0], 0, 0)))
        else:
            rows, w = gr.shape[0], gr.shape[1] // N_DEV
            in_specs.append(pl.BlockSpec((rows, w), lambda k, cref: (0, 2 * k + cref[0])))
        blk = pl.BlockSpec((None, rows, w), lambda k, cref: (k, 0, 0))
        in_specs.append(blk)
        out_specs.append(blk)
        out_shape.append(jax.ShapeDtypeStruct((N_CHIP, rows, w), BF16))
        args += [gr, rv.reshape(N_CHIP, rows, w)]

    def body(*refs):
        for t in range(n):
            g_ref, r_ref, o_ref = refs[1 + 2 * t], refs[2 + 2 * t], refs[1 + 2 * n + t]
            o_ref[...] = (g_ref[...].astype(F32) + r_ref[...].astype(F32)).astype(BF16)

    return _pallas(body, name=name, n_prefetch=1, grid=(N_CHIP,), in_specs=in_specs, out_specs=out_specs,
                   out_shape=out_shape, params=_params(("parallel",)))(c, *args)


def _adamw_math(g, wv, mv, vv):
    m = ADAM_B1 * mv + (1.0 - ADAM_B1) * g
    v = ADAM_B2 * vv + (1.0 - ADAM_B2) * (g * g)
    m_hat = m / (1.0 - ADAM_B1 ** ADAM_STEP)
    v_hat = v / (1.0 - ADAM_B2 ** ADAM_STEP)
    delta = -ADAM_LR * (m_hat / (jnp.sqrt(v_hat) + ADAM_EPS) + ADAM_WD * wv)
    return delta, m, v


ADAM_STEPS = 2


def _adamw_group(name, items, chip_ids):
    n = len(items)
    in_specs, out_specs, out_shape, args, prevs = [], [], [], [chip_ids], []
    for own, recv, w3, m3, v3, layer, _ in items:
        nl, rows, w = w3.shape
        tr = rows // ADAM_STEPS
        assert tr % 16 == 0, (name, rows)
        in_specs += [pl.BlockSpec((None, tr, w), lambda i, ids, slot=slot: (ids[slot], i, 0)) for slot in range(4)]
        slab = pl.BlockSpec((None, tr, w), lambda i, ids, layer=layer: (layer, i, 0))
        in_specs += [slab] * 3
        out_specs += [slab] * 4
        out_shape += [jax.ShapeDtypeStruct((nl, rows, w), F32)] * 4
        args += [own, recv, recv, recv, w3, m3, v3]
    aliases = {}
    for t, item in enumerate(items):
        if item[6] is not None:
            for k in range(4):
                aliases[len(args) + k] = 4 * t + k
            in_specs += [ANY_SPEC] * 4
            args += list(item[6])
            prevs.append(t)
    n_in = 1 + 7 * n + 4 * len(prevs)

    def body(*refs):
        for t in range(n):
            own_ref, r1_ref, r2_ref, r3_ref, w_ref, m_ref, v_ref = refs[1 + 7 * t:8 + 7 * t]
            g_ref, d_ref, nm_ref, nv_ref = refs[n_in + 4 * t:n_in + 4 * t + 4]
            g = ((own_ref[...].astype(F32) + r1_ref[...].astype(F32)) + r2_ref[...].astype(F32)) + r3_ref[...].astype(F32)
            g_ref[...] = g
            d_ref[...], nm_ref[...], nv_ref[...] = _adamw_math(g, w_ref[...], m_ref[...], v_ref[...])

    outs = _pallas(body, name=name, n_prefetch=1, grid=(ADAM_STEPS,), in_specs=in_specs, out_specs=out_specs,
                   out_shape=out_shape, aliases=aliases, params=_params(("parallel",)))(*args)
    return [list(outs[4 * t:4 * t + 4]) for t in range(n)]


def _adamw_small(gathered, ws, ms, vs):
    n = len(gathered)
    full = [w is not None for w in ws]
    args = list(gathered)
    out_shape = []
    for t in range(n):
        shape = jax.ShapeDtypeStruct(gathered[t].shape[2:], F32)
        if full[t]:
            args += [ws[t], ms[t], vs[t]]
            out_shape += [shape] * 4
        else:
            out_shape += [shape]

    def body(*refs):
        i_in, i_out = n, len(args)
        for t in range(n):
            p_ref = refs[t]
            g = p_ref[0, 0]
            for k in range(1, N_DEV):
                g = g + p_ref[0, k]
            refs[i_out][...] = g
            if full[t]:
                w_ref, m_ref, v_ref = refs[i_in:i_in + 3]
                refs[i_out + 1][...], refs[i_out + 2][...], refs[i_out + 3][...] = _adamw_math(
                    g, w_ref[...], m_ref[...], v_ref[...])
                i_in += 3
                i_out += 4
            else:
                i_out += 1

    outs = _pallas(body, name="adamw_small", in_specs=[VMEM_SPEC] * len(args), out_specs=[VMEM_SPEC] * len(out_shape),
                   out_shape=out_shape, params=pltpu.CompilerParams(vmem_limit_bytes=VMEM_LIMIT))(*args)
    result, i = [], 0
    for t in range(n):
        k = 4 if full[t] else 1
        result.append(list(outs[i:i + k]))
        i += k
    return result


def _adamw_plain(name, gs, ws, ms, vs):
    n = len(gs)

    def body(*refs):
        for t in range(n):
            g_ref, w_ref, m_ref, v_ref = refs[4 * t:4 * t + 4]
            outs = refs[4 * n + 3 * t:4 * n + 3 * t + 3]
            outs[0][...], outs[1][...], outs[2][...] = _adamw_math(g_ref[...], w_ref[...], m_ref[...], v_ref[...])

    args, out_shape = [], []
    for g, w, m, v in zip(gs, ws, ms, vs):
        args += [g, w, m, v]
        out_shape += [jax.ShapeDtypeStruct(w.shape, F32)] * 3
    outs = _pallas(body, name=name, in_specs=[VMEM_SPEC] * len(args), out_specs=[VMEM_SPEC] * len(out_shape),
                   out_shape=out_shape, params=pltpu.CompilerParams(vmem_limit_bytes=VMEM_LIMIT))(*args)
    return [list(outs[3 * t:3 * t + 3]) for t in range(n)]


KIND = {"sc_w_in": "cols", "sc_w_out": "blocked", "w_dkv": "blocked", "w_kr": "blocked", "w_uk": "cols", "w_uv": "cols",
        "w_dq": "blocked", "w_uq": "blocked", "w_o": "blocked", "ffn_w_up": "blocked", "ffn_w_down": "blocked",
        "conv": "blocked"}
GATHER_GROUPS = (("mixer", ("sc_w_in", "sc_w_out", "conv")),
                 ("up0", ("ffn_w_up0",)),
                 ("down0", ("ffn_w_down0",)),
                 ("attn", ("w_dkv", "w_kr", "w_uk", "w_uv", "w_dq", "w_uq", "w_o")),
                 ("ffn1", ("ffn_w_up1", "ffn_w_down1")))
SCATTER_GROUPS = (("ffn1", (("ffn_w_up", 1), ("ffn_w_down", 1))),
                  ("attn", (("w_o", None), ("w_uq", None), ("w_dq", None), ("w_uk", None), ("w_uv", None),
                            ("w_dkv", None), ("w_kr", None))),
                  ("ffn0", (("ffn_w_up", 0), ("ffn_w_down", 0))),
                  ("mixer", (("sc_w_out", None), ("sc_w_in", None))))
SCHEDULE = {
    "begin": (("gather_start", "mixer"),),
    "mixer_ready": (("gather_start", "up0"),),
    "l0_out": (("gather_forward", "up0"), ("gather_start", "down0")),
    "f0_up": (("gather_forward", "down0"), ("gather_start", "attn")),
    "f0_down": (("gather_forward", "attn"), ("gather_start", "ffn1")),
    "attn_fwd": (("gather_forward", "ffn1"),),
    "f1_gup": (("scatter_sibling", "ffn1"),),
    "f1_dhf": (("scatter_chips", "ffn1"),),
    "kv_bwd": (("scatter_sibling", "attn"), ("scatter_done", "ffn1")),
    "f0_dact": (("scatter_chips", "attn"),),
    "f0_gup": (("scatter_sibling", "ffn0"),),
    "f0_dhf": (("scatter_chips", "ffn0"),),
    "f0_bwd": (("scatter_done", "attn"),),
    "sc_bwd": (("scatter_sibling", "mixer"),),
    "d_l0_in": (("scatter_chips", "mixer"),),
}
FINISH = (("scatter_done", "ffn0"), ("scatter_done", "mixer"))
STAGES = {"gather_start": 1, "gather_forward": 2, "gather_done": 3,
          "scatter_sibling": 1, "scatter_chips": 2, "scatter_done": 3}
SMALL_W_ROWS = 24


def _pack(arrays, rows):
    flat = jnp.concatenate([a.reshape(-1).astype(F32) for a in arrays])
    return jnp.pad(flat, (0, rows * 128 - flat.shape[0])).reshape(rows, 128)


def _stored(name, a):
    return jnp.swapaxes(a, -1, -2) if name == "ffn_w_up" else a


def _base(name):
    if name.startswith("ffn_w_") and name[-1] in "01":
        return name[:-1], int(name[-1])
    return name, None


class _Exchange:
    def __init__(self, wts, mom, var, ffn_conv_b):
        self.wts, self.mom, self.var = wts, mom, var
        x, y, c = _place()
        self.me = 4 * x + 2 * y + c
        self.c_arr = jnp.reshape(c, (1,)).astype(jnp.int32)
        chip = 2 * x + y
        self.chip_ids = jnp.stack([chip, chip ^ 1, chip ^ 2, chip ^ 3]).astype(jnp.int32)
        self.ready = {"ffn_cb0": ffn_conv_b.reshape(2, N_FF_BLK, 1, FF_BLK)[0],
                      "ffn_cb1": ffn_conv_b.reshape(2, N_FF_BLK, 1, FF_BLK)[1]}
        self.gathers, self.group_of = {}, {}
        self.grads, self.scatters, self.results, self.queue = {}, {}, {}, []
        for gname, names in GATHER_GROUPS:
            self.gathers[gname] = dict(stage=0, names=names, kinds=[KIND[_base(nm)[0]] for nm in names])
            for nm in names:
                self.group_of[nm] = gname
        for nm in ("sc_conv_w", "ffn_cw0", "ffn_cw1"):
            self.group_of[nm] = "mixer"
        self.at("begin", None)

    def _shard(self, name):
        if name == "conv":
            return _pack([self.wts["sc_conv_w"], self.wts["ffn_conv_w"]], SMALL_W_ROWS).reshape(1, SMALL_W_ROWS, 128)
        base, layer = _base(name)
        a = _stored(base, self.wts[base])
        if layer is not None:
            a = a[layer:layer + 1]
        if KIND[base] == "cols":
            return a.reshape(a.shape[-2], a.shape[-1]).astype(BF16)
        return a.reshape((-1,) + a.shape[-2:]).astype(BF16)

    def _start(self, name, srcs, lands, ncopy, plan, st):
        self.queue.append((name, (srcs, lands, ncopy, plan), st))

    def _flush(self):
        if self.queue:
            flights = _copies_start("__".join(name for name, _, _ in self.queue), [job for _, job, _ in self.queue])
            for (_, _, st), flight in zip(self.queue, flights):
                st["flight"] = flight
            self.queue = []

    def _flight(self, st):
        self._flush()
        return st["flight"]

    def _gather_to(self, gname, stage, after):
        st = self.gathers[gname]
        if st["stage"] < 1 <= stage:
            shards = [self._shard(nm) for nm in st["names"]]
            lands = [_landing(s, kind) for s, kind in zip(shards, st["kinds"])]
            plan, ncopy = _plan_gather_chips(st["kinds"])
            self._start(f"ag_{gname}_chips", shards, lands, ncopy, plan, st)
            st["stage"] = 1
        if st["stage"] < 2 <= stage:
            plan, ncopy = _plan_gather_chips(st["kinds"])
            _, lands = _copies_wait(f"ag_{gname}_chips_wait", self._flight(st), ncopy, plan)
            plan, ncopy = _plan_gather_sibling(st["kinds"])
            self._start(f"ag_{gname}_sibling", [], lands, ncopy, plan, st)
            st["stage"] = 2
        if st["stage"] < 3 <= stage:
            plan, ncopy = _plan_gather_sibling(st["kinds"])
            _, lands = _copies_wait(f"ag_{gname}_sibling_wait", self._flight(st), ncopy, plan)
            for nm, land in zip(st["names"], lands):
                self._arrived(nm, land)
            st["stage"] = 3

    def _arrived(self, name, land):
        if name == "conv":
            conv = land.reshape(N_DEV, SMALL_W_ROWS * 128)
            self.ready["sc_conv_w"] = conv[:, :3 * 128].reshape(N_DEV, 3, 128).transpose(1, 0, 2).reshape(3, D)
            fcw = conv[:, 3 * 128:3 * 128 + 6 * 352].reshape(N_DEV, 2, 3, 352).transpose(1, 2, 0, 3)
            fcw = fcw.reshape(2, 3, N_FF_BLK, FF_BLK).transpose(0, 2, 1, 3)
            self.ready["ffn_cw0"], self.ready["ffn_cw1"] = fcw[0], fcw[1]
        elif name in ("sc_w_in", "w_uk", "w_uv") or name.startswith("ffn_w_up"):
            self.ready[name] = land
        elif name.startswith("ffn_w_down"):
            self.ready[name] = land.reshape(1, N_FF_BLK, FF_BLK, D)
        elif name == "w_kr":
            self.ready[name] = jnp.pad(land.reshape(D, QK_ROPE), ((0, 0), (0, 128 - QK_ROPE)))
        elif name == "w_uq":
            self.ready[name] = jnp.pad(land.reshape(N_HEADS, Q_LORA, QK_NOPE + QK_ROPE),
                                       ((0, 0), (0, 0), (0, QK_PAD - QK_NOPE - QK_ROPE)))
        else:
            self.ready[name] = land.reshape(D, land.shape[-1])

    def need(self, name, after):
        if name not in self.ready:
            self._gather_to(self.group_of[name], 3, after)
            self._flush()
        return self.ready[name]

    def grad(self, name, layer, array):
        self.grads[(name, layer)] = array

    def _scatter_to(self, gname, stage, after):
        keys = dict(SCATTER_GROUPS)[gname]
        st = self.scatters.setdefault(gname, dict(stage=0))
        kinds = [KIND[nm] for nm, _ in keys]
        if st["stage"] < 1 <= stage:
            grads = [self.grads[key] for key in keys]
            lands = []
            for gr, kind in zip(grads, kinds):
                shard = (gr.shape[0],) + gr.shape[2:] if kind == "blocked" else (gr.shape[0], gr.shape[1] // N_DEV)
                lands.append(lax.empty((N_CHIP,) + shard, BF16))
            plan, ncopy = _plan_scatter_sibling(kinds)
            self._start(f"rs_{gname}_sibling", grads, lands, ncopy, plan, st)
            st["stage"] = 1
        if st["stage"] < 2 <= stage:
            plan, ncopy = _plan_scatter_sibling(kinds)
            grads, recvs = _copies_wait(f"rs_{gname}_sibling_wait", self._flight(st), ncopy, plan)
            sums = _chip_sums(f"rs_{gname}_sums", grads, kinds, recvs, self.c_arr)
            lands = [lax.empty(s.shape, BF16) for s in sums]
            plan, ncopy = _plan_scatter_chips(len(sums))
            self._start(f"rs_{gname}_chips", sums, lands, ncopy, plan, st)
            st["stage"] = 2
        if st["stage"] < 3 <= stage:
            plan, ncopy = _plan_scatter_chips(len(keys))
            sums, recvs = _copies_wait(f"rs_{gname}_chips_wait", self._flight(st), ncopy, plan)
            items = []
            for (nm, layer), own, rv in zip(keys, sums, recvs):
                nl = 1 if layer is None else 2
                rows, w = own.shape[1], own.shape[2]
                w3, m3, v3 = (_stored(nm, src[nm]).reshape(nl, rows, w) for src in (self.wts, self.mom, self.var))
                items.append((own, rv, w3, m3, v3, 0 if layer is None else layer, self.results.get(nm)))
            outs = _adamw_group(f"adamw_{gname}", items, self.chip_ids)
            for (nm, _), out in zip(keys, outs):
                self.results[nm] = out
            st["stage"] = 3

    def at(self, place, after):
        for action, gname in SCHEDULE.get(place, ()):
            self._advance(action, gname, after)
        self._flush()

    def _advance(self, action, gname, after):
        if action.startswith("gather"):
            self._gather_to(gname, STAGES[action], after)
        else:
            self._scatter_to(gname, STAGES[action], after)

    def finish(self, after):
        for action, gname in FINISH:
            self._advance(action, gname, after)
        for gname, _ in SCATTER_GROUPS:
            self._scatter_to(gname, 3, after)
        return {nm: [_stored(nm, o.reshape(_stored(nm, self.wts[nm]).shape)) for o in outs]
                for nm, outs in self.results.items()}


REPLICATED = ("attn_norm", "ffn_norm", "final_norm", "kv_in_norm", "kv_latent_norm", "q_latent_norm", "ffn_conv_b")
WEIGHTS = ("attn_norm", "ffn_norm", "final_norm", "sc_w_in", "sc_conv_w", "sc_w_out", "kv_in_norm", "w_dkv",
           "kv_latent_norm", "w_kr", "w_uk", "w_uv", "w_dq", "q_latent_norm", "w_uq", "w_o", "ffn_w_up", "ffn_conv_w",
           "ffn_conv_b", "ffn_w_down")


def kernel(x, positions, attn_norm, ffn_norm, final_norm, sc_w_in, sc_conv_w, sc_w_out, kv_in_norm, w_dkv, kv_latent_norm, w_kr, w_uk, w_uv, w_dq, q_latent_norm, w_uq, w_o, ffn_w_up, ffn_conv_w, ffn_conv_b, ffn_w_down, loss_target, m_attn_norm, m_ffn_norm, m_final_norm, m_sc_w_in, m_sc_conv_w, m_sc_w_out, m_kv_in_norm, m_w_dkv, m_kv_latent_norm, m_w_kr, m_w_uk, m_w_uv, m_w_dq, m_q_latent_norm, m_w_uq, m_w_o, m_ffn_w_up, m_ffn_conv_w, m_ffn_conv_b, m_ffn_w_down, v_attn_norm, v_ffn_norm, v_final_norm, v_sc_w_in, v_sc_conv_w, v_sc_w_out, v_kv_in_norm, v_w_dkv, v_kv_latent_norm, v_w_kr, v_w_uk, v_w_uv, v_w_dq, v_q_latent_norm, v_w_uq, v_w_o, v_ffn_w_up, v_ffn_conv_w, v_ffn_conv_b, v_ffn_w_down):
    wts = dict(attn_norm=attn_norm, ffn_norm=ffn_norm, final_norm=final_norm, sc_w_in=sc_w_in, sc_conv_w=sc_conv_w,
               sc_w_out=sc_w_out, kv_in_norm=kv_in_norm, w_dkv=w_dkv, kv_latent_norm=kv_latent_norm, w_kr=w_kr,
               w_uk=w_uk, w_uv=w_uv, w_dq=w_dq, q_latent_norm=q_latent_norm, w_uq=w_uq, w_o=w_o, ffn_w_up=ffn_w_up,
               ffn_conv_w=ffn_conv_w, ffn_conv_b=ffn_conv_b, ffn_w_down=ffn_w_down)
    mom = dict(attn_norm=m_attn_norm, ffn_norm=m_ffn_norm, final_norm=m_final_norm, sc_w_in=m_sc_w_in,
               sc_conv_w=m_sc_conv_w, sc_w_out=m_sc_w_out, kv_in_norm=m_kv_in_norm, w_dkv=m_w_dkv,
               kv_latent_norm=m_kv_latent_norm, w_kr=m_w_kr, w_uk=m_w_uk, w_uv=m_w_uv, w_dq=m_w_dq,
               q_latent_norm=m_q_latent_norm, w_uq=m_w_uq, w_o=m_w_o, ffn_w_up=m_ffn_w_up, ffn_conv_w=m_ffn_conv_w,
               ffn_conv_b=m_ffn_conv_b, ffn_w_down=m_ffn_w_down)
    var = dict(attn_norm=v_attn_norm, ffn_norm=v_ffn_norm, final_norm=v_final_norm, sc_w_in=v_sc_w_in,
               sc_conv_w=v_sc_conv_w, sc_w_out=v_sc_w_out, kv_in_norm=v_kv_in_norm, w_dkv=v_w_dkv,
               kv_latent_norm=v_kv_latent_norm, w_kr=v_w_kr, w_uk=v_w_uk, w_uv=v_w_uv, w_dq=v_w_dq,
               q_latent_norm=v_q_latent_norm, w_uq=v_w_uq, w_o=v_w_o, ffn_w_up=v_ffn_w_up, ffn_conv_w=v_ffn_conv_w,
               ffn_conv_b=v_ffn_conv_b, ffn_w_down=v_ffn_w_down)
    xi, yi, ci = _place()
    me = 4 * xi + 2 * yi + ci
    _Chain.last = None

    ex = _Exchange(wts, mom, var, ffn_conv_b)
    rep = {
        "attn_norm": attn_norm, "ffn_norm": ffn_norm, "final_norm": final_norm,
        "kv_in_norm": kv_in_norm.reshape(1, D), "kv_latent_norm": kv_latent_norm.reshape(1, KV_LORA),
        "q_latent_norm": q_latent_norm.reshape(1, Q_LORA),
    }
    loss, grad_x, small = _local_step(x.reshape(T, D), positions.reshape(T, 1), loss_target.reshape(T, D), rep, ex)
    results = ex.finish(grad_x)

    def rows_of(a):
        return a.reshape(-1, a.shape[-1])

    small_order = list(REPLICATED) + ["sc_conv_w", "ffn_conv_w"]
    shards = [loss.reshape(1, 1, 128)] + [rows_of(small[nm])[None] for nm in small_order]
    gathered = _all_gather("ag_small_grads", [(s, "blocked") for s in shards])
    params = [[None] + [rows_of(src[nm]) for nm in REPLICATED] + [None, None] for src in (wts, mom, var)]
    summed = _adamw_small(gathered, *params)
    loss_total = summed[0][0][0, 0]
    for nm, vals in zip(REPLICATED, summed[1:1 + len(REPLICATED)]):
        results[nm] = [a.reshape(wts[nm].shape) for a in vals]
    g_scw = lax.dynamic_slice(summed[-2][0], (0, me * 128), (3, 128))
    g_fcw = lax.dynamic_slice(summed[-1][0], (0, me * 352), (6, 352))
    conv = _adamw_plain("adamw_conv", [g_scw, g_fcw], *[[rows_of(src["sc_conv_w"]), rows_of(src["ffn_conv_w"])]
                                                        for src in (wts, mom, var)])
    for nm, g_own, vals in zip(("sc_conv_w", "ffn_conv_w"), (g_scw, g_fcw), conv):
        results[nm] = [a.reshape(wts[nm].shape) for a in [g_own] + vals]

    outs = [loss_total, grad_x.reshape(1, T, D)]
    for slot in range(4):
        outs.extend(results[nm][slot] for nm in WEIGHTS)
    return tuple(outs)
```

```python
import jax
import jax.numpy as jnp
from jax import lax
from jax.experimental import pallas as pl
from jax.experimental.pallas import tpu as pltpu

F32 = jnp.float32
BF16 = jnp.bfloat16

T = 2048
D = 1024
N_HEADS = 8
QK_NOPE = 128
QK_ROPE = 64
V_HEAD = 128
Q_LORA = 384
KV_LORA = 256
D_FF = 2816
CHUNK = 64
ROPE_THETA = 10000.0
EPS = 1e-6
NEG_INF = -1e30
ADAM_LR = 0.001
ADAM_B1 = 0.9
ADAM_B2 = 0.999
ADAM_EPS = 1e-08
ADAM_WD = 0.01
ADAM_STEP = 10

N_DEV = 8
N_CHIP = 4
FF_BLK = D_FF * 2 // N_DEV
N_FF_BLK = D_FF // FF_BLK
QK_PAD = 256
HALO = 16

TM = 1024
TS = 512
TR = 256
TQ = 512
VMEM_LIMIT = 56 * 1024 * 1024

NN = (((1,), (0,)), ((), ()))
NT = (((1,), (1,)), ((), ()))
TN = (((0,), (0,)), ((), ()))
MESH = pl.DeviceIdType.MESH


def _params(sem):
    return pltpu.CompilerParams(dimension_semantics=sem, vmem_limit_bytes=VMEM_LIMIT)


ANY_SPEC = pl.BlockSpec(memory_space=pl.ANY)
VMEM_SPEC = pl.BlockSpec(memory_space=pltpu.VMEM)


class _Chain:
    last = None


def _pallas(body, *, name, in_specs, out_specs, out_shape, grid=(), scratch_shapes=(), n_prefetch=0, aliases=None,
            params=None):
    def run(*args):
        after = _Chain.last
        n_lead = len(args)
        specs, operands, fn = list(in_specs), list(args), body
        if after is not None:
            def fn(*refs):
                return body(*refs[:n_lead], *refs[n_lead + 1:])
            specs.append(ANY_SPEC)
            operands.append(after)
        kw = dict(name=name, out_shape=out_shape, input_output_aliases=aliases or {})
        if params is not None:
            kw["compiler_params"] = params
        if n_prefetch:
            kw["grid_spec"] = pltpu.PrefetchScalarGridSpec(
                num_scalar_prefetch=n_prefetch, grid=grid, in_specs=specs, out_specs=out_specs,
                scratch_shapes=scratch_shapes)
        else:
            kw.update(grid=grid, in_specs=specs, out_specs=out_specs, scratch_shapes=scratch_shapes)
        outs = pl.pallas_call(fn, **kw)(*operands)
        _Chain.last = outs[0] if isinstance(outs, (list, tuple)) else outs
        return outs
    return run


def _mm(name, a, b, *, grid, a_spec, b_spec, o_spec, o_shape, o_dtype, dims, k_axis=None, acc_shape=None,
        add=None, add_spec=None):
    nk = grid[k_axis] if k_axis is not None else 1
    has_add = add is not None

    def body(*refs):
        a_ref, b_ref = refs[0], refs[1]
        p = 2
        add_ref = None
        if has_add:
            add_ref = refs[p]
            p += 1
        o_ref = refs[p]
        p += 1
        r = lax.dot_general(a_ref[...].astype(BF16), b_ref[...].astype(BF16), dims, preferred_element_type=F32)
        if k_axis is None:
            if has_add:
                r = r + add_ref[...].astype(F32)
            o_ref[...] = r.astype(o_dtype)
        else:
            acc = refs[p]
            k = pl.program_id(k_axis)

            @pl.when(k == 0)
            def _():
                acc[...] = r

            @pl.when(k > 0)
            def _():
                acc[...] += r

            @pl.when(k == nk - 1)
            def _():
                t = acc[...]
                if has_add:
                    t = t + add_ref[...].astype(F32)
                o_ref[...] = t.astype(o_dtype)

    in_specs = [a_spec, b_spec]
    args = [a, b]
    if has_add:
        in_specs.append(add_spec if add_spec is not None else o_spec)
        args.append(add)
    sem = tuple("arbitrary" if ax == k_axis else "parallel" for ax in range(len(grid)))
    scratch = [pltpu.VMEM(acc_shape, F32)] if k_axis is not None else []
    return _pallas(body, name=name, grid=grid, in_specs=in_specs, out_specs=o_spec,
                   out_shape=jax.ShapeDtypeStruct(o_shape, o_dtype), scratch_shapes=scratch, params=_params(sem))(*args)


def _mm_sum(name, parts, *, grid, o_spec, o_shape, o_dtype, add=None, norm_bwd=None):
    has_add = add is not None
    np_ = len(parts)
    nn = 1 if norm_bwd is None else len(norm_bwd[1])
    has_res = norm_bwd is not None and norm_bwd[2] is not None

    def body(*refs):
        accs = [None] * nn
        for p, (_, _, _, _, dims, n) in enumerate(parts):
            a_ref, b_ref = refs[2 * p], refs[2 * p + 1]
            for k in range(a_ref.shape[0]):
                r = lax.dot_general(a_ref[k], b_ref[k], dims, preferred_element_type=F32)
                accs[n] = r if accs[n] is None else accs[n] + r
        if norm_bwd is None:
            acc = accs[0]
            if has_add:
                acc = acc + refs[2 * np_][...]
            refs[-1][...] = acc.astype(o_dtype)
            return
        x_ref, g_refs = refs[2 * np_], refs[2 * np_ + 1:2 * np_ + 1 + nn]
        dx_ref, dxb_ref, dg_refs = refs[-2 - nn], refs[-1 - nn], refs[-nn:]
        xv = x_ref[...]
        r = lax.rsqrt(jnp.mean(xv * xv, axis=-1, keepdims=True) + EPS)
        xn = xv * r
        dx = refs[2 * np_ + 1 + nn][...] if has_res else None
        sums = []
        for acc, g_ref in zip(accs, g_refs):
            gdy = acc * g_ref[...]
            t = r * (gdy - xn * jnp.mean(gdy * xn, axis=-1, keepdims=True))
            dx = t if dx is None else dx + t
            sums.append(jnp.sum(acc * xn, axis=0, keepdims=True))
        dx_ref[...] = dx
        dxb_ref[...] = dx.astype(BF16)

        @pl.when(pl.program_id(0) == 0)
        def _():
            for dg_ref, part in zip(dg_refs, sums):
                dg_ref[...] = part

        @pl.when(pl.program_id(0) > 0)
        def _():
            for dg_ref, part in zip(dg_refs, sums):
                dg_ref[...] += part

    in_specs, args = [], []
    for a, a_spec, b, b_spec, _, _ in parts:
        in_specs += [a_spec, b_spec]
        args += [a, b]
    if norm_bwd is None:
        if has_add:
            in_specs.append(o_spec)
            args.append(add)
        return _pallas(body, name=name, grid=grid, in_specs=in_specs, out_specs=o_spec,
                       out_shape=jax.ShapeDtypeStruct(o_shape, o_dtype),
                       params=_params(("parallel",) * len(grid)))(*args)
    x, gains, dres = norm_bwd
    vec = pl.BlockSpec((1, o_shape[1]), lambda i: (0, 0))
    in_specs += [o_spec] + [vec] * nn + ([o_spec] if has_res else [])
    args += [x] + list(gains) + ([dres] if has_res else [])
    outs = _pallas(body, name=name, grid=grid, in_specs=in_specs, out_specs=[o_spec, o_spec] + [vec] * nn,
                   out_shape=[jax.ShapeDtypeStruct(o_shape, F32), jax.ShapeDtypeStruct(o_shape, BF16)]
                   + [jax.ShapeDtypeStruct((1, o_shape[1]), F32)] * nn,
                   params=_params(("arbitrary",)))(*args)
    return outs[0], outs[1], list(outs[2:])


def _mm_rows(name, a, b, dims, o_dtype, n_out, *, tn=None, add=None):
    k = a.shape[1]
    tn = n_out if tn is None else tn
    if dims == NN:
        b_spec = pl.BlockSpec((k, tn), lambda n, i: (0, n))
    else:
        b_spec = pl.BlockSpec((tn, k), lambda n, i: (n, 0))
    return _mm(name, a, b, grid=(n_out // tn, T // TM),
               a_spec=pl.BlockSpec((TM, k), lambda n, i: (i, 0)), b_spec=b_spec,
               o_spec=pl.BlockSpec((TM, tn), lambda n, i: (i, n)), o_shape=(T, n_out), o_dtype=o_dtype,
               dims=dims, add=add)


def _mm_wgrad(name, a, b, *, tn=512):
    k, n = a.shape[1], b.shape[1]
    tn = min(tn, n)
    return _mm(name, a, b, grid=(n // tn,),
               a_spec=pl.BlockSpec((T, k), lambda j: (0, 0)), b_spec=pl.BlockSpec((T, tn), lambda j: (0, j)),
               o_spec=pl.BlockSpec((k, tn), lambda j: (0, j)), o_shape=(k, n), o_dtype=BF16, dims=TN)


def _rms_fwd(name, x, g):
    d = x.shape[1]

    def body(x_ref, g_ref, o_ref):
        xv = x_ref[...]
        r = lax.rsqrt(jnp.mean(xv * xv, axis=-1, keepdims=True) + EPS)
        o_ref[...] = ((xv * r) * g_ref[...]).astype(BF16)

    return _pallas(
        body, name=name, grid=(T // TM,),
        in_specs=[pl.BlockSpec((TM, d), lambda i: (i, 0)), pl.BlockSpec((1, d), lambda i: (0, 0))],
        out_specs=pl.BlockSpec((TM, d), lambda i: (i, 0)),
        out_shape=jax.ShapeDtypeStruct((T, d), BF16), params=_params(("parallel",)))(x, g)


def _rows_call(name, body, row_ins, whole_ins, outs):
    in_specs = [pl.BlockSpec((TM, a.shape[1]), lambda i: (i, 0)) for a in row_ins]
    in_specs += [pl.BlockSpec(a.shape, lambda i: (0, 0)) for a in whole_ins]
    return _pallas(
        body, name=name, grid=(T // TM,), in_specs=in_specs,
        out_specs=[pl.BlockSpec((TM, d), lambda i: (i, 0)) for d, _ in outs],
        out_shape=[jax.ShapeDtypeStruct((T, d), dt) for d, dt in outs],
        params=_params(("parallel",)))(*row_ins, *whole_ins)


def _rms(xv, g):
    return (xv * lax.rsqrt(jnp.mean(xv * xv, axis=-1, keepdims=True) + EPS)) * g


def _rms_fwd2(name, x, g1, g2):
    d = x.shape[1]

    def body(x_ref, g1_ref, g2_ref, o1_ref, o2_ref):
        xv = x_ref[...]
        xn = xv * lax.rsqrt(jnp.mean(xv * xv, axis=-1, keepdims=True) + EPS)
        o1_ref[...] = (xn * g1_ref[...]).astype(BF16)
        o2_ref[...] = (xn * g2_ref[...]).astype(BF16)

    return _rows_call(name, body, [x], [g1, g2], [(d, BF16), (d, BF16)])


def _down_norm(name, a, w, g):
    n = w.shape[1]

    def body(a_ref, w_ref, g_ref, raw_ref, o_ref):
        raw = lax.dot_general(a_ref[...], w_ref[...], NN, preferred_element_type=F32)
        raw_ref[...] = raw
        o_ref[...] = _rms(raw, g_ref[...]).astype(BF16)

    return _rows_call(name, body, [a], [w, g], [(n, F32), (n, BF16)])


def _kv_down(hk, w_dkv, w_kr, g, tables):
    def body(a_ref, c_ref, sa_ref, sb_ref, wd_ref, wr_ref, g_ref, raw_ref, ckv_ref, kr_ref):
        av = a_ref[...]
        raw = lax.dot_general(av, wd_ref[...], NN, preferred_element_type=F32)
        raw_ref[...] = raw
        ckv_ref[...] = _rms(raw, g_ref[...]).astype(BF16)
        kr = lax.dot_general(av, wr_ref[...], NN, preferred_element_type=F32)
        kr_ref[...] = _rotate(kr, c_ref[...], sa_ref[...], sb_ref[...], 1.0).astype(BF16)

    return _rows_call("kv_down", body, [hk, *tables], [w_dkv, w_kr, g], [(KV_LORA, F32), (KV_LORA, BF16), (128, BF16)])


def _kv_up(ckv, w_uk, w_uv):
    def body(a_ref, wk_ref, wv_ref, k_ref, v_ref):
        av = a_ref[...]
        k_ref[...] = lax.dot_general(av, wk_ref[...], NN, preferred_element_type=F32).astype(BF16)
        v_ref[...] = lax.dot_general(av, wv_ref[...], NN, preferred_element_type=F32).astype(BF16)

    return _rows_call("kv_up", body, [ckv], [w_uk, w_uv], [(N_HEADS * QK_NOPE, BF16), (N_HEADS * V_HEAD, BF16)])


def _rms_bwd(name, x, gains, dys, dres=None):
    d = x.shape[1]
    n = len(gains)
    has_res = dres is not None

    def body(*refs):
        x_ref, g_refs, dy_refs = refs[0], refs[1:1 + n], refs[1 + n:1 + 2 * n]
        dx_ref, dxb_ref = refs[-2 - n], refs[-1 - n]
        dg_refs = refs[-n:]
        xv = x_ref[...]
        r = lax.rsqrt(jnp.mean(xv * xv, axis=-1, keepdims=True) + EPS)
        xn = xv * r
        dx = refs[1 + 2 * n][...] if has_res else None
        parts = []
        for g_ref, dy_ref in zip(g_refs, dy_refs):
            dyv = dy_ref[...].astype(F32)
            gdy = dyv * g_ref[...]
            t = r * (gdy - xn * jnp.mean(gdy * xn, axis=-1, keepdims=True))
            dx = t if dx is None else dx + t
            parts.append(jnp.sum(dyv * xn, axis=0, keepdims=True))
        dx_ref[...] = dx
        dxb_ref[...] = dx.astype(BF16)

        @pl.when(pl.program_id(0) == 0)
        def _():
            for dg_ref, part in zip(dg_refs, parts):
                dg_ref[...] = part

        @pl.when(pl.program_id(0) > 0)
        def _():
            for dg_ref, part in zip(dg_refs, parts):
                dg_ref[...] += part

    row = pl.BlockSpec((TR, d), lambda i: (i, 0))
    vec = pl.BlockSpec((1, d), lambda i: (0, 0))
    args = [x] + list(gains) + list(dys) + ([dres] if has_res else [])
    in_specs = [row] + [vec] * n + [row] * n + ([row] if has_res else [])
    outs = _pallas(
        body, name=name, grid=(T // TR,), in_specs=in_specs, out_specs=[row, row] + [vec] * n,
        out_shape=[jax.ShapeDtypeStruct((T, d), F32), jax.ShapeDtypeStruct((T, d), BF16)]
        + [jax.ShapeDtypeStruct((1, d), F32)] * n,
        params=_params(("arbitrary",)))(*args)
    return outs[0], outs[1], list(outs[2:])


def _final(h, g, tgt):
    def body(h_ref, g_ref, t_ref, loss_ref, dh_ref, dhb_ref, dg_ref):
        hv = h_ref[...]
        r = lax.rsqrt(jnp.mean(hv * hv, axis=-1, keepdims=True) + EPS)
        xn = hv * r
        gv = g_ref[...]
        err = xn * gv - t_ref[...]
        part_loss = 0.5 * jnp.sum(jnp.mean(err * err, axis=-1, keepdims=True), axis=0, keepdims=True)
        dy = err * (1.0 / D)
        gdy = dy * gv
        dh = r * (gdy - xn * jnp.mean(gdy * xn, axis=-1, keepdims=True))
        dh_ref[...] = dh
        dhb_ref[...] = dh.astype(BF16)
        part = jnp.sum(dy * xn, axis=0, keepdims=True)
        first = pl.program_id(0) == 0

        @pl.when(first)
        def _():
            dg_ref[...] = part
            loss_ref[...] = jnp.broadcast_to(part_loss, (1, 128))

        @pl.when(jnp.logical_not(first))
        def _():
            dg_ref[...] += part
            loss_ref[...] += jnp.broadcast_to(part_loss, (1, 128))

    row = pl.BlockSpec((TR, D), lambda i: (i, 0))
    vec = pl.BlockSpec((1, D), lambda i: (0, 0))
    return _pallas(
        body, name="final_loss", grid=(T // TR,), in_specs=[row, vec, row],
        out_specs=[pl.BlockSpec((1, 128), lambda i: (0, 0)), row, row, vec],
        out_shape=[jax.ShapeDtypeStruct((1, 128), F32), jax.ShapeDtypeStruct((T, D), F32),
                   jax.ShapeDtypeStruct((T, D), BF16), jax.ShapeDtypeStruct((1, D), F32)],
        params=_params(("arbitrary",)))(h, g, tgt)


def _prev_idx(i, rows=TR):
    return jnp.maximum(i * (rows // HALO) - 1, 0)


def _next_idx(i, rows=TR):
    return jnp.minimum((i + 1) * (rows // HALO), T // HALO - 1)


def _causal_taps(ext):
    return pltpu.roll(ext, 2, 0)[HALO:], pltpu.roll(ext, 1, 0)[HALO:], ext[HALO:]


def _anticausal_taps(ext, n):
    rows = ext.shape[0]
    return pltpu.roll(ext, rows - 1, 0)[:n], pltpu.roll(ext, rows - 2, 0)[:n]


MIX_COLS = 512


def _mixer_in(hn, w_in, w):
    nc = D // MIX_COLS

    def body(h_ref, hh_ref, wb_ref, wc_ref, wu_ref, w_ref, b_ref, c_ref, u_ref, y_ref):
        i = pl.program_id(1)
        hv = h_ref[...]
        he = jnp.concatenate([hh_ref[...], hv], axis=0)
        ce = lax.dot_general(he, wc_ref[...], NN, preferred_element_type=F32).astype(BF16)
        ue = lax.dot_general(he, wu_ref[...], NN, preferred_element_type=F32).astype(BF16)
        bv = lax.dot_general(hv, wb_ref[...], NN, preferred_element_type=F32).astype(BF16)
        b_ref[...] = bv
        c_ref[...] = ce[HALO:]
        u_ref[...] = ue[HALO:]
        row = lax.broadcasted_iota(jnp.int32, (HALO + TS, 1), 0)
        cu = jnp.where(jnp.logical_or(i > 0, row >= HALO), ce.astype(F32) * ue.astype(F32), 0.0)
        x2, x1, x0 = _causal_taps(cu)
        wv = w_ref[...]
        cv = (x2 * wv[0:1] + x1 * wv[1:2]) + x0 * wv[2:3]
        y_ref[...] = (bv.astype(F32) * cv).astype(BF16)

    def cols(part):
        return pl.BlockSpec((D, MIX_COLS), lambda j, i: (0, part * nc + j))

    blk = pl.BlockSpec((TS, MIX_COLS), lambda j, i: (i, j))
    out = jax.ShapeDtypeStruct((T, D), BF16)
    return _pallas(
        body, name="l0_in", grid=(nc, T // TS),
        in_specs=[pl.BlockSpec((TS, D), lambda j, i: (i, 0)), pl.BlockSpec((HALO, D), lambda j, i: (_prev_idx(i, TS), 0)),
                  cols(0), cols(1), cols(2), pl.BlockSpec((3, MIX_COLS), lambda j, i: (0, j))],
        out_specs=[blk] * 4, out_shape=[out] * 4,
        params=_params(("parallel", "parallel")))(hn, hn, w_in, w_in, w_in, w)


def _mixer_out_bwd(dh, w_out, zb, zc, zu, w):
    last = T // TR - 1

    def body(dh_ref, dhn_ref, wo_ref, b_ref, bn_ref, c_ref, ch_ref, u_ref, uh_ref, w_ref, dz_ref, dw_ref):
        i = pl.program_id(0)
        dye = lax.dot_general(jnp.concatenate([dh_ref[...], dhn_ref[...]], axis=0), wo_ref[...], NT,
                              preferred_element_type=F32)
        cv_ = c_ref[...].astype(F32)
        uv = u_ref[...].astype(F32)
        cu = cv_ * uv
        cuh = jnp.where(i > 0, ch_ref[...].astype(F32) * uh_ref[...].astype(F32), 0.0)
        x2, x1, x0 = _causal_taps(jnp.concatenate([cuh, cu], axis=0))
        wv = w_ref[...]
        conv = (x2 * wv[0:1] + x1 * wv[1:2]) + x0 * wv[2:3]
        dyv = dye[:TR]
        dz_ref[:, 0:D] = (dyv * conv).astype(BF16)
        dconv = dyv * b_ref[...].astype(F32)
        dconv_n = jnp.where(i < last, dye[TR:] * bn_ref[...].astype(F32), 0.0)
        n1, n2 = _anticausal_taps(jnp.concatenate([dconv, dconv_n], axis=0), TR)
        dcu = (dconv * wv[2:3] + n1 * wv[1:2]) + n2 * wv[0:1]
        dz_ref[:, D:2 * D] = (dcu * uv).astype(BF16)
        dz_ref[:, 2 * D:3 * D] = (dcu * cv_).astype(BF16)
        part = jnp.concatenate([jnp.sum(dconv * x2, axis=0, keepdims=True),
                                jnp.sum(dconv * x1, axis=0, keepdims=True),
                                jnp.sum(dconv * x0, axis=0, keepdims=True)], axis=0)

        @pl.when(i == 0)
        def _():
            dw_ref[...] = part

        @pl.when(i > 0)
        def _():
            dw_ref[...] += part

    main = pl.BlockSpec((TR, D), lambda i: (i, 0))
    prev = pl.BlockSpec((HALO, D), lambda i: (_prev_idx(i), 0))
    nxt = pl.BlockSpec((HALO, D), lambda i: (_next_idx(i), 0))
    wspec = pl.BlockSpec((3, D), lambda i: (0, 0))
    return _pallas(
        body, name="d_l0_out", grid=(T // TR,),
        in_specs=[main, nxt, pl.BlockSpec((D, D), lambda i: (0, 0)), main, nxt, main, prev, main, prev, wspec],
        out_specs=[pl.BlockSpec((TR, 3 * D), lambda i: (i, 0)), wspec],
        out_shape=[jax.ShapeDtypeStruct((T, 3 * D), BF16), jax.ShapeDtypeStruct((3, D), F32)],
        params=_params(("arbitrary",)))(dh, dh, w_out, zb, zb, zc, zc, zu, zu, w)


def _sigmoid(x):
    return 1.0 / (1.0 + jnp.exp(-x))


def _ffn_up_act(name, hf, w_up, w, b):
    def body(h_ref, hh_ref, wg_ref, wv_ref, w_ref, b_ref, g_ref, v_ref, a_ref):
        i = pl.program_id(1)
        hv = h_ref[...]
        ge = lax.dot_general(jnp.concatenate([hh_ref[...], hv], axis=0), wg_ref[...], NT,
                             preferred_element_type=F32).astype(BF16)
        v = lax.dot_general(hv, wv_ref[...], NT, preferred_element_type=F32).astype(BF16)
        g_ref[...] = ge[HALO:]
        v_ref[...] = v
        ext = ge.astype(F32)
        row = lax.broadcasted_iota(jnp.int32, (HALO + TS, 1), 0)
        ext = jnp.where(jnp.logical_or(i > 0, row >= HALO), ext, 0.0)
        x2, x1, x0 = _causal_taps(ext)
        wv = w_ref[...]
        gc = ((x2 * wv[0:1] + x1 * wv[1:2]) + x0 * wv[2:3]) + b_ref[...]
        a_ref[...] = ((gc * _sigmoid(gc)) * v.astype(F32)).astype(BF16)

    blk = pl.BlockSpec((None, TS, FF_BLK), lambda j, i: (j, i, 0))
    out = jax.ShapeDtypeStruct((N_FF_BLK, T, FF_BLK), BF16)
    return _pallas(
        body, name=name, grid=(N_FF_BLK, T // TS),
        in_specs=[pl.BlockSpec((TS, D), lambda j, i: (i, 0)),
                  pl.BlockSpec((HALO, D), lambda j, i: (_prev_idx(i, TS), 0)),
                  pl.BlockSpec((None, None, FF_BLK, D), lambda j, i: (0, j, 0, 0)),
                  pl.BlockSpec((None, None, FF_BLK, D), lambda j, i: (0, j + N_FF_BLK, 0, 0)),
                  pl.BlockSpec((None, 3, FF_BLK), lambda j, i: (j, 0, 0)),
                  pl.BlockSpec((None, 1, FF_BLK), lambda j, i: (j, 0, 0))],
        out_specs=[blk, blk, blk], out_shape=[out, out, out],
        params=_params(("parallel", "parallel")))(hf, hf, w_up, w_up, w, b)


def _ffn_dact(name, dh, w_down4, g, v, w, b):
    last = T // TS - 1

    def body(dh_ref, dhn_ref, wd_ref, g_ref, gp_ref, gn_ref, v_ref, vn_ref, w_ref, b_ref, dg_ref, dv_ref, dw_ref, db_ref):
        i = pl.program_id(1)
        da = lax.dot_general(jnp.concatenate([dh_ref[...], dhn_ref[...]], axis=0), wd_ref[...], NT,
                             preferred_element_type=F32)
        row = lax.broadcasted_iota(jnp.int32, (TS + HALO, 1), 0)
        da = jnp.where(jnp.logical_or(i < last, row < TS), da, 0.0)
        gp = jnp.where(i > 0, gp_ref[...].astype(F32), 0.0)
        ext = jnp.concatenate([gp, g_ref[...].astype(F32), gn_ref[...].astype(F32)], axis=0)
        x2, x1, x0 = _causal_taps(ext)
        wv = w_ref[...]
        gc = ((x2 * wv[0:1] + x1 * wv[1:2]) + x0 * wv[2:3]) + b_ref[...]
        sg = _sigmoid(gc)
        vv = jnp.concatenate([v_ref[...].astype(F32), vn_ref[...].astype(F32)], axis=0)
        dv_ref[...] = (da[:TS] * (gc[:TS] * sg[:TS])).astype(BF16)
        dgc = (da * vv) * (sg * (1.0 + gc * (1.0 - sg)))
        n1, n2 = _anticausal_taps(dgc, TS)
        d0 = dgc[:TS]
        dg_ref[...] = ((d0 * wv[2:3] + n1 * wv[1:2]) + n2 * wv[0:1]).astype(BF16)
        part_w = jnp.concatenate([jnp.sum(d0 * x2[:TS], axis=0, keepdims=True),
                                  jnp.sum(d0 * x1[:TS], axis=0, keepdims=True),
                                  jnp.sum(d0 * x0[:TS], axis=0, keepdims=True)], axis=0)
        part_b = jnp.sum(d0, axis=0, keepdims=True)

        @pl.when(i == 0)
        def _():
            dw_ref[...] = part_w
            db_ref[...] = part_b

        @pl.when(i > 0)
        def _():
            dw_ref[...] += part_w
            db_ref[...] += part_b

    blk = pl.BlockSpec((None, TS, FF_BLK), lambda j, i: (j, i, 0))
    prev = pl.BlockSpec((None, HALO, FF_BLK), lambda j, i: (j, _prev_idx(i, TS), 0))
    nxt = pl.BlockSpec((None, HALO, FF_BLK), lambda j, i: (j, _next_idx(i, TS), 0))
    wspec = pl.BlockSpec((None, 3, FF_BLK), lambda j, i: (j, 0, 0))
    bspec = pl.BlockSpec((None, 1, FF_BLK), lambda j, i: (j, 0, 0))
    return _pallas(
        body, name=name, grid=(N_FF_BLK, T // TS),
        in_specs=[pl.BlockSpec((TS, D), lambda j, i: (i, 0)),
                  pl.BlockSpec((HALO, D), lambda j, i: (_next_idx(i, TS), 0)),
                  pl.BlockSpec((None, None, FF_BLK, D), lambda j, i: (0, j, 0, 0)),
                  blk, prev, nxt, blk, nxt, wspec, bspec],
        out_specs=[blk, blk, wspec, bspec],
        out_shape=[jax.ShapeDtypeStruct((N_FF_BLK, T, FF_BLK), BF16), jax.ShapeDtypeStruct((N_FF_BLK, T, FF_BLK), BF16),
                   jax.ShapeDtypeStruct((N_FF_BLK, 3, FF_BLK), F32), jax.ShapeDtypeStruct((N_FF_BLK, 1, FF_BLK), F32)],
        params=_params(("parallel", "arbitrary")))(dh, dh, w_down4, g, g, g, v, v, w, b)


def _rope_tables(pos, inv_freq):
    half = QK_ROPE // 2

    def body(p_ref, f_ref, c_ref, sa_ref, sb_ref):
        ang = p_ref[...].astype(F32) * f_ref[...]
        lane = lax.broadcasted_iota(jnp.int32, (T, 128), 1)
        c = jnp.cos(ang)
        s = jnp.sin(ang)
        c_ref[...] = jnp.where(lane < 2 * half, c, 0.0)
        sa_ref[...] = jnp.where(lane < half, -s, 0.0)
        sb_ref[...] = jnp.where(jnp.logical_and(lane >= half, lane < 2 * half), s, 0.0)

    return _pallas(
        body, name="rope_tables", in_specs=[VMEM_SPEC] * 2, out_specs=[VMEM_SPEC] * 3,
        out_shape=[jax.ShapeDtypeStruct((T, 128), F32)] * 3,
        params=pltpu.CompilerParams(vmem_limit_bytes=VMEM_LIMIT))(pos, inv_freq)


def _rotate(r, c, sa, sb, sign):
    return r * c + sign * (pltpu.roll(r, 96, 1) * sa + pltpu.roll(r, 32, 1) * sb)


def _q_up(cq, w_uq, tables):
    cos, sa, sb = tables

    def body(a_ref, b_ref, c_ref, sa_ref, sb_ref, o_ref):
        for h in range(N_HEADS):
            r = lax.dot_general(a_ref[...], b_ref[h], NN, preferred_element_type=F32)
            o_ref[h, :, :QK_NOPE] = r[:, :QK_NOPE].astype(BF16)
            o_ref[h, :, QK_NOPE:] = _rotate(r[:, QK_NOPE:], c_ref[...], sa_ref[...], sb_ref[...], 1.0).astype(BF16)

    tab = pl.BlockSpec((TS, 128), lambda i: (i, 0))
    return _pallas(
        body, name="q_up", grid=(T // TS,),
        in_specs=[pl.BlockSpec((TS, Q_LORA), lambda i: (i, 0)),
                  pl.BlockSpec((N_HEADS, Q_LORA, QK_PAD), lambda i: (0, 0, 0)), tab, tab, tab],
        out_specs=pl.BlockSpec((N_HEADS, TS, QK_PAD), lambda i: (0, i, 0)),
        out_shape=jax.ShapeDtypeStruct((N_HEADS, T, QK_PAD), BF16),
        params=_params(("parallel",)))(cq, w_uq, cos, sa, sb)


def _rope(name, x, tables, sign, out_dtype, reduce_groups=False):
    g, _, w = x.shape
    cos, sa, sb = tables

    def body(x_ref, c_ref, sa_ref, sb_ref, o_ref):
        xv = x_ref[...].astype(F32)
        if reduce_groups:
            acc = xv[0]
            for k in range(1, g):
                acc = acc + xv[k]
            xv = acc
        out = _rotate(xv[:, w - 128:], c_ref[...], sa_ref[...], sb_ref[...], sign)
        if w > 128:
            o_ref[:, :w - 128] = xv[:, :w - 128].astype(out_dtype)
        o_ref[:, w - 128:] = out.astype(out_dtype)

    tab = pl.BlockSpec((TM, 128), lambda h, i: (i, 0))
    if reduce_groups:
        x_spec = pl.BlockSpec((g, TM, w), lambda h, i: (0, i, 0))
        groups = 1
    else:
        x_spec = pl.BlockSpec((None, TM, w), lambda h, i: (h, i, 0))
        groups = g
    return _pallas(
        body, name=name, grid=(groups, T // TM), in_specs=[x_spec, tab, tab, tab],
        out_specs=pl.BlockSpec((None, TM, w), lambda h, i: (h, i, 0)),
        out_shape=jax.ShapeDtypeStruct((groups, T, w), out_dtype),
        params=_params(("parallel", "parallel")))(x, cos, sa, sb)


SCALE = (QK_NOPE + QK_ROPE) ** -0.5
LOG2E = 1.4426950408889634
SCALE2 = SCALE * LOG2E


def _diag_mask(transposed):
    shift = CHUNK.bit_length() - 1
    a = lax.broadcasted_iota(jnp.int32, (TQ, TQ), 0) >> shift
    b = lax.broadcasted_iota(jnp.int32, (TQ, TQ), 1) >> shift
    return (a <= b) if transposed else (b <= a)


def _as_row(col):
    return jnp.transpose(jnp.broadcast_to(col, (col.shape[0], 128)), (1, 0))[0:1]


def _keys(kn_ref, kr_ref, off):
    return jnp.concatenate([kn_ref[pl.ds(off, TQ), :], kr_ref[pl.ds(off, TQ), :]], axis=1)


def _attn_fwd(q, kn, kr, v):
    def body(q_ref, kn_ref, kr_ref, v_ref, o_ref, lse_ref):
        i = pl.program_id(1)
        qv = q_ref[...]

        def step(j, carry, masked):
            m, l, acc = carry
            off = pl.multiple_of(j * TQ, TQ)
            s = lax.dot_general(qv, _keys(kn_ref, kr_ref, off), NT, preferred_element_type=F32) * SCALE2
            if masked:
                s = jnp.where(_diag_mask(False), s, NEG_INF)
            m_new = jnp.maximum(m, jnp.max(s, axis=-1, keepdims=True))
            p = jnp.exp2(s - m_new)
            alpha = jnp.exp2(m - m_new)
            l = alpha * l + jnp.sum(p, axis=-1, keepdims=True)
            acc = alpha * acc + lax.dot_general(p.astype(BF16), v_ref[pl.ds(off, TQ), :], NN, preferred_element_type=F32)
            return m_new, l, acc

        init = (jnp.full((TQ, 1), NEG_INF, F32), jnp.zeros((TQ, 1), F32), jnp.zeros((TQ, V_HEAD), F32))
        carry = lax.fori_loop(0, i, lambda j, cr: step(j, cr, False), init)
        m, l, acc = step(i, carry, True)
        o_ref[...] = (acc / l).astype(BF16)
        lse_ref[...] = _as_row(m + jnp.log(l) * LOG2E)

    return _pallas(
        body, name="attn_fwd", grid=(N_HEADS, T // TQ),
        in_specs=[pl.BlockSpec((None, TQ, QK_PAD), lambda h, i: (h, i, 0)),
                  pl.BlockSpec((T, QK_NOPE), lambda h, i: (0, h)),
                  pl.BlockSpec((T, 128), lambda h, i: (0, 0)),
                  pl.BlockSpec((T, V_HEAD), lambda h, i: (0, h))],
        out_specs=[pl.BlockSpec((TQ, V_HEAD), lambda h, i: (i, h)), pl.BlockSpec((None, 1, TQ), lambda h, i: (h, 0, i))],
        out_shape=[jax.ShapeDtypeStruct((T, N_HEADS * V_HEAD), BF16), jax.ShapeDtypeStruct((N_HEADS, 1, T), F32)],
        params=_params(("parallel", "parallel")))(q, kn, kr, v)


def _attn_bwd(q, kn, kr, v, o, do, lse_row, tables):
    nq = T // TQ
    cos, sa, sb = tables

    def body(q_ref, kn_ref, kr_ref, v_ref, o_ref, do_ref, lse_ref, c_ref, sa_ref, sb_ref,
             dq_ref, dkn_ref, dkr_ref, dv_ref, dq_acc, dl_ref):
        j = pl.program_id(1)

        @pl.when(j == 0)
        def _():
            dq_acc[...] = jnp.zeros_like(dq_acc)
            for i in range(nq):
                rows = pl.ds(i * TQ, TQ)
                prod = do_ref[rows, :].astype(F32) * o_ref[rows, :].astype(F32)
                dl_ref[:, rows] = _as_row(jnp.sum(prod, axis=-1, keepdims=True))

        kk = jnp.concatenate([kn_ref[...], kr_ref[...]], axis=1)
        vv = v_ref[...]

        def step(i, carry, masked):
            dk, dv = carry
            off = pl.multiple_of(i * TQ, TQ)
            qi = q_ref[pl.ds(off, TQ), :]
            doi = do_ref[pl.ds(off, TQ), :]
            st = lax.dot_general(kk, qi, NT, preferred_element_type=F32) * SCALE2
            if masked:
                st = jnp.where(_diag_mask(True), st, NEG_INF)
            pt = jnp.exp2(st - lse_ref[:, pl.ds(off, TQ)])
            dv = dv + lax.dot_general(pt.astype(BF16), doi, NN, preferred_element_type=F32)
            dpt = lax.dot_general(vv, doi, NT, preferred_element_type=F32)
            dst = ((pt * (dpt - dl_ref[:, pl.ds(off, TQ)])) * SCALE).astype(BF16)
            dk = dk + lax.dot_general(dst, qi, NN, preferred_element_type=F32)
            dq_acc[pl.ds(off, TQ), :] += lax.dot_general(dst, kk, TN, preferred_element_type=F32)
            return dk, dv

        carry = step(j, (jnp.zeros((TQ, QK_PAD), F32), jnp.zeros((TQ, V_HEAD), F32)), True)
        dk, dv = lax.fori_loop(j + 1, nq, lambda i, cr: step(i, cr, False), carry)
        dkn_ref[...] = dk[:, :QK_NOPE].astype(BF16)
        dkr_ref[...] = dk[:, QK_NOPE:]
        dv_ref[...] = dv.astype(BF16)

        @pl.when(j == nq - 1)
        def _():
            dq = dq_acc[...]
            dq_ref[:, :QK_NOPE] = dq[:, :QK_NOPE].astype(BF16)
            dq_ref[:, QK_NOPE:] = _rotate(dq[:, QK_NOPE:], c_ref[...], sa_ref[...], sb_ref[...], -1.0).astype(BF16)

    row = pl.BlockSpec((None, 1, T), lambda h, j: (h, 0, 0))
    head = pl.BlockSpec((TQ, 128), lambda h, j: (j, h))
    whole = pl.BlockSpec((None, T, QK_PAD), lambda h, j: (h, 0, 0))
    tab = pl.BlockSpec((T, 128), lambda h, j: (0, 0))
    heads = pl.BlockSpec((T, V_HEAD), lambda h, j: (0, h))
    return _pallas(
        body, name="attn_bwd", grid=(N_HEADS, nq),
        in_specs=[whole, head, pl.BlockSpec((TQ, 128), lambda h, j: (j, 0)), head, heads, heads, row, tab, tab, tab],
        out_specs=[whole, head, pl.BlockSpec((None, TQ, 128), lambda h, j: (h, j, 0)), head],
        out_shape=[jax.ShapeDtypeStruct((N_HEADS, T, QK_PAD), BF16), jax.ShapeDtypeStruct((T, N_HEADS * QK_NOPE), BF16),
                   jax.ShapeDtypeStruct((N_HEADS, T, 128), F32), jax.ShapeDtypeStruct((T, N_HEADS * V_HEAD), BF16)],
        scratch_shapes=[pltpu.VMEM((T, QK_PAD), F32), pltpu.VMEM((1, T), F32)],
        params=_params(("parallel", "arbitrary")))(q, kn, kr, v, o, do, lse_row, cos, sa, sb)


def _ffn_gup(name, dg, dv, hf):
    def body(dg_ref, dv_ref, hf_ref, o_ref):
        j = pl.program_id(0)

        @pl.when(j < N_FF_BLK)
        def _():
            o_ref[...] = lax.dot_general(dg_ref[...], hf_ref[...], TN, preferred_element_type=F32).astype(BF16)

        @pl.when(j >= N_FF_BLK)
        def _():
            o_ref[...] = lax.dot_general(dv_ref[...], hf_ref[...], TN, preferred_element_type=F32).astype(BF16)

    return _pallas(
        body, name=name, grid=(N_DEV,),
        in_specs=[pl.BlockSpec((None, T, FF_BLK), lambda j: (jnp.minimum(j, N_FF_BLK - 1), 0, 0)),
                  pl.BlockSpec((None, T, FF_BLK), lambda j: (jnp.maximum(j - N_FF_BLK, 0), 0, 0)),
                  pl.BlockSpec((T, D), lambda j: (0, 0))],
        out_specs=pl.BlockSpec((None, FF_BLK, D), lambda j: (j, 0, 0)),
        out_shape=jax.ShapeDtypeStruct((N_DEV, FF_BLK, D), BF16), params=_params(("parallel",)))(dg, dv, hf)


def _ffn_layer_fwd(tag, h, gain, ex):
    hf = _rms_fwd(f"{tag}_norm", h, gain)
    g, v, act = _ffn_up_act(f"{tag}_up", hf, ex.need(f"ffn_w_up{tag[1]}", hf), ex.need(f"ffn_cw{tag[1]}", hf),
                            ex.need(f"ffn_cb{tag[1]}", hf))
    ex.at(f"{tag}_up", act)
    rows = pl.BlockSpec((TS, D), lambda i: (i, 0))
    out = _mm_sum(f"{tag}_down",
                  [(act, pl.BlockSpec((N_FF_BLK, TS, FF_BLK), lambda i: (0, i, 0)), ex.need(f"ffn_w_down{tag[1]}", act),
                    pl.BlockSpec((None, N_FF_BLK, FF_BLK, D), lambda i: (0, 0, 0, 0)), NN, 0)],
                  grid=(T // TS,), o_spec=rows, o_shape=(T, D), o_dtype=F32, add=h)
    ex.at(f"{tag}_down", out)
    return out, (hf, g, v, act)


def _ffn_layer_bwd(tag, h, gain, ex, saved, dh, dh_bf):
    hf, g, v, act = saved
    layer = tag[1]
    w_up, w_down4 = ex.need(f"ffn_w_up{layer}", dh_bf), ex.need(f"ffn_w_down{layer}", dh_bf)
    dg, dv, dcw, dcb = _ffn_dact(f"{tag}_dact", dh_bf, w_down4, g, v, ex.need(f"ffn_cw{layer}", dh_bf),
                                 ex.need(f"ffn_cb{layer}", dh_bf))
    ex.at(f"{tag}_dact", dg)
    g_down = _mm(f"{tag}_gdown", act, dh_bf, grid=(N_FF_BLK,),
                 a_spec=pl.BlockSpec((None, T, FF_BLK), lambda j: (j, 0, 0)),
                 b_spec=pl.BlockSpec((T, D), lambda j: (0, 0)),
                 o_spec=pl.BlockSpec((FF_BLK, D), lambda j: (j, 0)),
                 o_shape=(D_FF, D), o_dtype=BF16, dims=TN)
    g_up = _ffn_gup(f"{tag}_gup", dg, dv, hf)
    ex.grad("ffn_w_up", int(layer), g_up.reshape(1, N_DEV, FF_BLK, D))
    ex.grad("ffn_w_down", int(layer), g_down.reshape(1, N_DEV, D_FF // N_DEV, D))
    ex.at(f"{tag}_gup", g_up)
    part = pl.BlockSpec((N_FF_BLK, TR, FF_BLK), lambda i: (0, i, 0))
    dh_in, dh_in_bf, dgain = _mm_sum(
        f"{tag}_dhf",
        [(dg, part, w_up, pl.BlockSpec((None, N_FF_BLK, FF_BLK, D), lambda i: (0, 0, 0, 0)), NN, 0),
         (dv, part, w_up, pl.BlockSpec((None, N_FF_BLK, FF_BLK, D), lambda i: (0, 1, 0, 0)), NN, 0)],
        grid=(T // TR,), o_spec=pl.BlockSpec((TR, D), lambda i: (i, 0)), o_shape=(T, D), o_dtype=F32,
        norm_bwd=(h, [gain], dh))
    ex.at(f"{tag}_dhf", dh_in)
    return dh_in, dh_in_bf, dgain[0], dcw, dcb


def _local_step(x, pos, tgt, rep, ex):
    attn_norm, ffn_norm, final_norm = rep["attn_norm"], rep["ffn_norm"], rep["final_norm"]
    half = QK_ROPE // 2
    inv = 1.0 / (ROPE_THETA ** (jnp.arange(half, dtype=F32) / half))
    inv_freq = jnp.concatenate([inv, inv, jnp.zeros((128 - 2 * half,), F32)]).reshape(1, 128)
    tables = _rope_tables(pos, inv_freq)

    hn0 = _rms_fwd("l0_norm", x, attn_norm[0:1])
    w_in = ex.need("sc_w_in", hn0)
    ex.at("mixer_ready", hn0)
    zb, zc, zu, y = _mixer_in(hn0, w_in, ex.need("sc_conv_w", hn0))
    ex.at("l0_in", y)
    h1 = _mm_rows("l0_out", y, ex.need("sc_w_out", y), NN, F32, D, tn=512, add=x)
    ex.at("l0_out", h1)
    h2, ffn0 = _ffn_layer_fwd("f0", h1, ffn_norm[0:1], ex)

    hk, hn1 = _rms_fwd2("h2_norms", h2, rep["kv_in_norm"], attn_norm[1:2])
    ckv_raw, ckv, kr = _kv_down(hk, ex.need("w_dkv", hk), ex.need("w_kr", hk), rep["kv_latent_norm"], tables)
    kn, vv = _kv_up(ckv, ex.need("w_uk", ckv), ex.need("w_uv", ckv))

    cq_raw, cq = _down_norm("q_down", hn1, ex.need("w_dq", hn1), rep["q_latent_norm"])
    w_uq = ex.need("w_uq", cq)
    q = _q_up(cq, w_uq, tables)
    o, lse = _attn_fwd(q, kn, kr, vv)
    ex.at("attn_fwd", o)
    w_o = ex.need("w_o", o)
    h3 = _mm_rows("attn_out", o, w_o, NN, F32, D, tn=512, add=h2)
    h4, ffn1 = _ffn_layer_fwd("f1", h3, ffn_norm[1:2], ex)

    loss, dh4, dh4_bf, d_final = _final(h4, final_norm.reshape(1, D), tgt)

    dh3, dh3_bf, d_fn1, dcw1, dcb1 = _ffn_layer_bwd("f1", h3, ffn_norm[1:2], ex, ffn1, dh4, dh4_bf)
    ex.at("f1_bwd", dh3)

    do = _mm_rows("d_attn_out", dh3_bf, w_o, NT, BF16, N_HEADS * V_HEAD)
    ex.grad("w_o", None, _mm_wgrad("g_w_o", o, dh3_bf).reshape(1, N_DEV, D // N_DEV, D))
    dq_pre, dkn, dkr, dvv = _attn_bwd(q, kn, kr, vv, o, do, lse, tables)
    def rows_of(a):
        return a[None], pl.BlockSpec((1, TS, a.shape[1]), lambda i: (0, i, 0))

    def whole(wt):
        return wt[None], pl.BlockSpec((1,) + wt.shape, lambda i: (0, 0, 0))

    def row_blocks(d):
        return dict(grid=(T // TS,), o_spec=pl.BlockSpec((TS, d), lambda i: (i, 0)), o_shape=(T, d), o_dtype=F32)

    _, dcq_raw_bf, (d_qln,) = _mm_sum(
        "d_q_up", [(dq_pre, pl.BlockSpec((N_HEADS, TS, QK_PAD), lambda i: (0, i, 0)),
                    w_uq, pl.BlockSpec((N_HEADS, Q_LORA, QK_PAD), lambda i: (0, 0, 0)), NT, 0)],
        norm_bwd=(cq_raw, [rep["q_latent_norm"]], None), **row_blocks(Q_LORA))
    g_uq = _mm("g_w_uq", cq, dq_pre, grid=(N_HEADS,),
               a_spec=pl.BlockSpec((T, Q_LORA), lambda h: (0, 0)),
               b_spec=pl.BlockSpec((None, T, QK_PAD), lambda h: (h, 0, 0)),
               o_spec=pl.BlockSpec((None, Q_LORA, QK_PAD), lambda h: (h, 0, 0)),
               o_shape=(N_HEADS, Q_LORA, QK_PAD), o_dtype=BF16, dims=TN)
    ex.grad("w_uq", None, g_uq[:, :, :QK_NOPE + QK_ROPE].reshape(1, N_DEV, Q_LORA, QK_NOPE + QK_ROPE))
    ex.grad("w_dq", None, _mm_wgrad("g_w_dq", hn1, dcq_raw_bf).reshape(1, N_DEV, D // N_DEV, Q_LORA))

    _, dckv_raw_bf, (d_kvln,) = _mm_sum(
        "d_kv_up", [(*rows_of(dkn), *whole(ex.need("w_uk", dkn)), NT, 0),
                    (*rows_of(dvv), *whole(ex.need("w_uv", dvv)), NT, 0)],
        norm_bwd=(ckv_raw, [rep["kv_latent_norm"]], None), **row_blocks(KV_LORA))
    ex.grad("w_uk", None, _mm_wgrad("g_w_uk", ckv, dkn))
    ex.grad("w_uv", None, _mm_wgrad("g_w_uv", ckv, dvv))
    dkr_raw_bf = _rope("dk_rope", dkr, tables, -1.0, BF16, reduce_groups=True).reshape(T, 128)
    ex.grad("w_dkv", None, _mm_wgrad("g_w_dkv", hk, dckv_raw_bf).reshape(1, N_DEV, D // N_DEV, KV_LORA))
    ex.grad("w_kr", None, _mm_wgrad("g_w_kr", hk, dkr_raw_bf)[:, :QK_ROPE].reshape(1, N_DEV, D // N_DEV, QK_ROPE))

    dh2, dh2_bf, (d_an1, d_kvin) = _mm_sum(
        "d_h2", [(*rows_of(dcq_raw_bf), *whole(ex.need("w_dq", dcq_raw_bf)), NT, 0),
                 (*rows_of(dckv_raw_bf), *whole(ex.need("w_dkv", dckv_raw_bf)), NT, 1),
                 (*rows_of(dkr_raw_bf), *whole(ex.need("w_kr", dkr_raw_bf)), NT, 1)],
        norm_bwd=(h2, [attn_norm[1:2], rep["kv_in_norm"]], dh3), **row_blocks(D))
    ex.at("kv_bwd", dh2)

    dh1, dh1_bf, d_fn0, dcw0, dcb0 = _ffn_layer_bwd("f0", h1, ffn_norm[0:1], ex, ffn0, dh2, dh2_bf)
    ex.at("f0_bwd", dh1)

    ex.grad("sc_w_out", None, _mm_wgrad("g_sc_w_out", y, dh1_bf).reshape(1, N_DEV, D // N_DEV, D))
    dz, d_scw = _mixer_out_bwd(dh1_bf, ex.need("sc_w_out", dh1_bf), zb, zc, zu, ex.need("sc_conv_w", dh1_bf))
    g_in = _mm_wgrad("g_sc_w_in", hn0, dz)
    ex.grad("sc_w_in", None, g_in)
    ex.at("sc_bwd", g_in)
    ex.at("d_l0_in", g_in)
    grad_x, _, (d_an0,) = _mm_sum(
        "d_l0_in", [(*rows_of(dz), *whole(ex.need("sc_w_in", dz)), NT, 0)],
        norm_bwd=(x, [attn_norm[0:1]], dh1), **row_blocks(D))

    small = {
        "attn_norm": jnp.concatenate([d_an0, d_an1], axis=0),
        "ffn_norm": jnp.concatenate([d_fn0, d_fn1], axis=0),
        "final_norm": d_final.reshape(D),
        "kv_in_norm": d_kvin.reshape(D),
        "kv_latent_norm": d_kvln.reshape(KV_LORA),
        "q_latent_norm": d_qln,
        "ffn_conv_b": jnp.stack([dcb0, dcb1]).transpose(0, 2, 1, 3).reshape(2, D_FF),
        "sc_conv_w": d_scw,
        "ffn_conv_w": jnp.stack([dcw0, dcw1]).transpose(0, 2, 1, 3).reshape(2, 3, D_FF),
    }
    return loss, grad_x, small


def _place():
    return lax.axis_index("x"), lax.axis_index("y"), lax.axis_index("c")


def _peers():
    x, y, c = _place()
    return (x, y, 1 - c), [(1 - x, y), (x, 1 - y), (1 - x, 1 - y)]


def _window(ref, kind, dev):
    if kind == "blocked":
        return ref.at[:, dev]
    width = ref.shape[-1] // N_DEV
    return ref.at[:, pl.ds(pl.multiple_of(dev * width, 128), width)]


def _all_gather(name, items):
    n = len(items)
    out_shapes = []
    for shard, kind in items:
        if kind == "blocked":
            shape = (shard.shape[0], N_DEV) + shard.shape[1:]
        else:
            shape = (shard.shape[0], N_DEV * shard.shape[1])
        out_shapes.append(jax.ShapeDtypeStruct(shape, shard.dtype))

    def body(*refs):
        srcs, outs = refs[:n], refs[n:2 * n]
        send_sems, recv_sems, local_sems = refs[2 * n:]
        x, y, c = _place()
        me = 4 * x + 2 * y + c
        sibling, chips = _peers()

        def num(px, py, pc):
            return 4 * px + 2 * py + pc

        def copy(t, k, dev, to, from_src):
            kind = items[t][1]
            dst = _window(outs[t], kind, dev)
            return pltpu.make_async_remote_copy(
                src_ref=srcs[t] if from_src else dst, dst_ref=dst,
                send_sem=send_sems.at[t, k], recv_sem=recv_sems.at[t, k], device_id=to, device_id_type=MESH)

        mine = [pltpu.make_async_copy(srcs[t], _window(outs[t], items[t][1], me), local_sems.at[t]) for t in range(n)]
        for cp in mine:
            cp.start()
        first = []
        for t in range(n):
            first.append(copy(t, 0, me, sibling, True))
            for j, chip in enumerate(chips):
                first.append(copy(t, 1 + j, me, (*chip, c), True))
        for cp in first:
            cp.start()
        passed = []
        for j, chip in enumerate(chips):
            for t in range(n):
                copy(t, 1 + j, num(*chip, c), (x, y, c), False).wait_recv()
                fwd = copy(t, 4 + j, num(*chip, c), sibling, False)
                fwd.start()
                passed.append(fwd)
        for t in range(n):
            copy(t, 0, num(x, y, 1 - c), (x, y, c), False).wait_recv()
            for j, chip in enumerate(chips):
                copy(t, 4 + j, num(*chip, 1 - c), (x, y, c), False).wait_recv()
        for cp in first + passed:
            cp.wait_send()
        for cp in mine:
            cp.wait()

    return _pallas(
        body, name=name, in_specs=[ANY_SPEC] * n, out_specs=[ANY_SPEC] * n, out_shape=out_shapes,
        scratch_shapes=[pltpu.SemaphoreType.DMA((n, 7)), pltpu.SemaphoreType.DMA((n, 7)), pltpu.SemaphoreType.DMA((n,))],
    )(*[s for s, _ in items])


HBM_SPEC = pl.BlockSpec(memory_space=pltpu.HBM)
SEM_SPEC = pl.BlockSpec(memory_space=pltpu.SEMAPHORE)
EFFECT = pltpu.SideEffectType.DATAFLOW_SIDE_EFFECTING
TOKEN = jax.ShapeDtypeStruct((8, 128), F32)


def _hbm(a):
    return pltpu.with_memory_space_constraint(a, pltpu.HBM)


def _copies_start(name, jobs):
    nj = len(jobs)
    counts = [(len(srcs), len(lands)) for srcs, lands, _, _ in jobs]
    n_arr = sum(ns + nl for ns, nl in counts)

    def body(*refs):
        sems, token = refs[n_arr:n_arr + 2 * nj], refs[-1]
        at = 0
        for j, ((ns, nl), (_, _, ncopy, plan)) in enumerate(zip(counts, jobs)):
            copies = plan(refs[at:at + ns], refs[at + ns:at + ns + nl])
            assert len(copies) == ncopy
            for k, (sent, dst, to, _) in enumerate(copies):
                pltpu.make_async_remote_copy(src_ref=sent, dst_ref=dst, send_sem=sems[2 * j].at[k],
                                             recv_sem=sems[2 * j + 1].at[k], device_id=to, device_id_type=MESH).start()
            at += ns + nl
        token[...] = jnp.zeros_like(token)

    arrays = [a for srcs, lands, _, _ in jobs for a in list(srcs) + list(lands)]
    sem_shapes = [pltpu.SemaphoreType.DMA((ncopy,)) for _, _, ncopy, _ in jobs for _ in range(2)]
    outs = pl.pallas_call(
        body, name=name, in_specs=[HBM_SPEC] * n_arr,
        out_specs=[SEM_SPEC] * (2 * nj) + [HBM_SPEC] * n_arr + [VMEM_SPEC],
        out_shape=sem_shapes + [pltpu.HBM(a.shape, a.dtype) for a in arrays] + [TOKEN],
        input_output_aliases={i: 2 * nj + i for i in range(n_arr)},
        compiler_params=pltpu.CompilerParams(has_side_effects=EFFECT))(*[_hbm(a) for a in arrays])
    _Chain.last = outs[-1]
    flights, at = [], 2 * nj
    for j, (ns, nl) in enumerate(counts):
        flights.append((outs[2 * j], outs[2 * j + 1], list(outs[at:at + ns]), list(outs[at + ns:at + ns + nl])))
        at += ns + nl
    return flights


def _copies_wait(name, started, ncopy, plan):
    send, recv, srcs, lands = started
    ns, nl = len(srcs), len(lands)

    def body(*refs):
        send_ref, recv_ref, token = refs[ns + nl], refs[ns + nl + 1], refs[-1]
        copies = plan(refs[:ns], refs[ns:ns + nl])
        assert len(copies) == ncopy
        for k, (sent, _, to, landed) in enumerate(copies):
            cp = pltpu.make_async_remote_copy(src_ref=sent, dst_ref=landed, send_sem=send_ref.at[k],
                                              recv_sem=recv_ref.at[k], device_id=to, device_id_type=MESH)
            cp.wait_send()
            cp.wait_recv()
        token[...] = jnp.zeros_like(token)

    arrays = list(srcs) + list(lands)
    outs = pl.pallas_call(
        body, name=name, in_specs=[HBM_SPEC] * (ns + nl) + [SEM_SPEC] * 2 + [ANY_SPEC],
        out_specs=[HBM_SPEC] * (ns + nl) + [VMEM_SPEC], out_shape=[pltpu.HBM(a.shape, a.dtype) for a in arrays] + [TOKEN],
        input_output_aliases={i: i for i in range(ns + nl)},
        compiler_params=pltpu.CompilerParams(has_side_effects=EFFECT))(*arrays, send, recv, _Chain.last)
    _Chain.last = outs[-1]
    return list(outs[:ns]), list(outs[ns:-1])


def _plan_gather_chips(kinds):
    def plan(srcs, lands):
        x, y, c = _place()
        sibling, chips = _peers()
        out = []
        for t, kind in enumerate(kinds):
            mine = _window(lands[t], kind, 4 * x + 2 * y + c)
            out.append((srcs[t], mine, (x, y, c), mine))
            out.append((srcs[t], mine, sibling, _window(lands[t], kind, 4 * x + 2 * y + 1 - c)))
            for px, py in chips:
                out.append((srcs[t], mine, (px, py, c), _window(lands[t], kind, 4 * px + 2 * py + c)))
        return out
    return plan, 5 * len(kinds)


def _plan_gather_sibling(kinds):
    def plan(srcs, lands):
        _, _, c = _place()
        sibling, chips = _peers()
        out = []
        for t, kind in enumerate(kinds):
            for px, py in chips:
                w = _window(lands[t], kind, 4 * px + 2 * py + c)
                out.append((w, w, sibling, _window(lands[t], kind, 4 * px + 2 * py + 1 - c)))
        return out
    return plan, 3 * len(kinds)


def _plan_scatter_sibling(kinds):
    def plan(srcs, lands):
        _, _, c = _place()
        sibling, _ = _peers()
        out = []
        for t, kind in enumerate(kinds):
            for k in range(N_CHIP):
                out.append((_window(srcs[t], kind, 2 * k + 1 - c), lands[t].at[k], sibling, lands[t].at[k]))
        return out
    return plan, N_CHIP * len(kinds)


def _plan_scatter_chips(n):
    def plan(srcs, lands):
        x, y, c = _place()
        _, chips = _peers()
        out = []
        for t in range(n):
            for px, py in chips:
                out.append((srcs[t].at[2 * px + py], lands[t].at[2 * x + y], (px, py, c), lands[t].at[2 * px + py]))
        return out
    return plan, 3 * n


def _landing(shard, kind):
    if kind == "blocked":
        return lax.empty((shard.shape[0], N_DEV) + shard.shape[1:], shard.dtype)
    return lax.empty((shard.shape[0], N_DEV * shard.shape[1]), shard.dtype)


def _chip_sums(name, grads, kinds, recvs, c):
    n = len(grads)
    in_specs, out_specs, out_shape, args = [], [], [], []
    for gr, kind, rv in zip(grads, kinds, recvs):
        if kind == "blocked":
            rows, w = gr.shape[2], gr.shape[3]
            in_specs.append(pl.BlockSpec((None, None, rows, w), lambda k, cref: (0, 2 * k + cref[0], 0, 0)))
        else:
            rows, w = gr.shape[0], gr.shape[1] // N_DEV
            in_specs.append(pl.BlockSpec((rows, w), lambda k, cref: (0, 2 * k + cref[0])))
        blk = pl.BlockSpec((None, rows, w), lambda k, cref: (k, 0, 0))
        in_specs.append(blk)
        out_specs.append(blk)
        out_shape.append(jax.ShapeDtypeStruct((N_CHIP, rows, w), BF16))
        args += [gr, rv.reshape(N_CHIP, rows, w)]

    def body(*refs):
        for t in range(n):
            g_ref, r_ref, o_ref = refs[1 + 2 * t], refs[2 + 2 * t], refs[1 + 2 * n + t]
            o_ref[...] = (g_ref[...].astype(F32) + r_ref[...].astype(F32)).astype(BF16)

    return _pallas(body, name=name, n_prefetch=1, grid=(N_CHIP,), in_specs=in_specs, out_specs=out_specs,
                   out_shape=out_shape, params=_params(("parallel",)))(c, *args)


def _adamw_math(g, wv, mv, vv):
    m = ADAM_B1 * mv + (1.0 - ADAM_B1) * g
    v = ADAM_B2 * vv + (1.0 - ADAM_B2) * (g * g)
    m_hat = m / (1.0 - ADAM_B1 ** ADAM_STEP)
    v_hat = v / (1.0 - ADAM_B2 ** ADAM_STEP)
    delta = -ADAM_LR * (m_hat / (jnp.sqrt(v_hat) + ADAM_EPS) + ADAM_WD * wv)
    return delta, m, v


ADAM_STEPS = 2


def _adamw_group(name, items, chip_ids):
    n = len(items)
    in_specs, out_specs, out_shape, args, prevs = [], [], [], [chip_ids], []
    for own, recv, w3, m3, v3, layer, _ in items:
        nl, rows, w = w3.shape
        tr = rows // ADAM_STEPS
        assert tr % 16 == 0, (name, rows)
        in_specs += [pl.BlockSpec((None, tr, w), lambda i, ids, slot=slot: (ids[slot], i, 0)) for slot in range(4)]
        slab = pl.BlockSpec((None, tr, w), lambda i, ids, layer=layer: (layer, i, 0))
        in_specs += [slab] * 3
        out_specs += [slab] * 4
        out_shape += [jax.ShapeDtypeStruct((nl, rows, w), F32)] * 4
        args += [own, recv, recv, recv, w3, m3, v3]
    aliases = {}
    for t, item in enumerate(items):
        if item[6] is not None:
            for k in range(4):
                aliases[len(args) + k] = 4 * t + k
            in_specs += [ANY_SPEC] * 4
            args += list(item[6])
            prevs.append(t)
    n_in = 1 + 7 * n + 4 * len(prevs)

    def body(*refs):
        for t in range(n):
            own_ref, r1_ref, r2_ref, r3_ref, w_ref, m_ref, v_ref = refs[1 + 7 * t:8 + 7 * t]
            g_ref, d_ref, nm_ref, nv_ref = refs[n_in + 4 * t:n_in + 4 * t + 4]
            g = ((own_ref[...].astype(F32) + r1_ref[...].astype(F32)) + r2_ref[...].astype(F32)) + r3_ref[...].astype(F32)
            g_ref[...] = g
            d_ref[...], nm_ref[...], nv_ref[...] = _adamw_math(g, w_ref[...], m_ref[...], v_ref[...])

    outs = _pallas(body, name=name, n_prefetch=1, grid=(ADAM_STEPS,), in_specs=in_specs, out_specs=out_specs,
                   out_shape=out_shape, aliases=aliases, params=_params(("parallel",)))(*args)
    return [list(outs[4 * t:4 * t + 4]) for t in range(n)]


def _adamw_small(gathered, ws, ms, vs):
    n = len(gathered)
    full = [w is not None for w in ws]
    args = list(gathered)
    out_shape = []
    for t in range(n):
        shape = jax.ShapeDtypeStruct(gathered[t].shape[2:], F32)
        if full[t]:
            args += [ws[t], ms[t], vs[t]]
            out_shape += [shape] * 4
        else:
            out_shape += [shape]

    def body(*refs):
        i_in, i_out = n, len(args)
        for t in range(n):
            p_ref = refs[t]
            g = p_ref[0, 0]
            for k in range(1, N_DEV):
                g = g + p_ref[0, k]
            refs[i_out][...] = g
            if full[t]:
                w_ref, m_ref, v_ref = refs[i_in:i_in + 3]
                refs[i_out + 1][...], refs[i_out + 2][...], refs[i_out + 3][...] = _adamw_math(
                    g, w_ref[...], m_ref[...], v_ref[...])
                i_in += 3
                i_out += 4
            else:
                i_out += 1

    outs = _pallas(body, name="adamw_small", in_specs=[VMEM_SPEC] * len(args), out_specs=[VMEM_SPEC] * len(out_shape),
                   out_shape=out_shape, params=pltpu.CompilerParams(vmem_limit_bytes=VMEM_LIMIT))(*args)
    result, i = [], 0
    for t in range(n):
        k = 4 if full[t] else 1
        result.append(list(outs[i:i + k]))
        i += k
    return result


def _adamw_plain(name, gs, ws, ms, vs):
    n = len(gs)

    def body(*refs):
        for t in range(n):
            g_ref, w_ref, m_ref, v_ref = refs[4 * t:4 * t + 4]
            outs = refs[4 * n + 3 * t:4 * n + 3 * t + 3]
            outs[0][...], outs[1][...], outs[2][...] = _adamw_math(g_ref[...], w_ref[...], m_ref[...], v_ref[...])

    args, out_shape = [], []
    for g, w, m, v in zip(gs, ws, ms, vs):
        args += [g, w, m, v]
        out_shape += [jax.ShapeDtypeStruct(w.shape, F32)] * 3
    outs = _pallas(body, name=name, in_specs=[VMEM_SPEC] * len(args), out_specs=[VMEM_SPEC] * len(out_shape),
                   out_shape=out_shape, params=pltpu.CompilerParams(vmem_limit_bytes=VMEM_LIMIT))(*args)
    return [list(outs[3 * t:3 * t + 3]) for t in range(n)]


KIND = {"sc_w_in": "cols", "sc_w_out": "blocked", "w_dkv": "blocked", "w_kr": "blocked", "w_uk": "cols", "w_uv": "cols",
        "w_dq": "blocked", "w_uq": "blocked", "w_o": "blocked", "ffn_w_up": "blocked", "ffn_w_down": "blocked",
        "conv": "blocked"}
GATHER_GROUPS = (("mixer", ("sc_w_in", "sc_w_out", "conv")),
                 ("up0", ("ffn_w_up0",)),
                 ("down0", ("ffn_w_down0",)),
                 ("attn", ("w_dkv", "w_kr", "w_uk", "w_uv", "w_dq", "w_uq", "w_o")),
                 ("ffn1", ("ffn_w_up1", "ffn_w_down1")))
SCATTER_GROUPS = (("ffn1", (("ffn_w_up", 1), ("ffn_w_down", 1))),
                  ("attn", (("w_o", None), ("w_uq", None), ("w_dq", None), ("w_uk", None), ("w_uv", None),
                            ("w_dkv", None), ("w_kr", None))),
                  ("ffn0", (("ffn_w_up", 0), ("ffn_w_down", 0))),
                  ("mixer", (("sc_w_out", None), ("sc_w_in", None))))
SCHEDULE = {
    "begin": (("gather_start", "mixer"),),
    "mixer_ready": (("gather_start", "up0"),),
    "l0_out": (("gather_forward", "up0"), ("gather_start", "down0")),
    "f0_up": (("gather_forward", "down0"), ("gather_start", "attn")),
    "f0_down": (("gather_forward", "attn"), ("gather_start", "ffn1")),
    "attn_fwd": (("gather_forward", "ffn1"),),
    "f1_gup": (("scatter_sibling", "ffn1"),),
    "f1_dhf": (("scatter_chips", "ffn1"),),
    "kv_bwd": (("scatter_sibling", "attn"), ("scatter_done", "ffn1")),
    "f0_dact": (("scatter_chips", "attn"),),
    "f0_gup": (("scatter_sibling", "ffn0"),),
    "f0_dhf": (("scatter_chips", "ffn0"),),
    "f0_bwd": (("scatter_done", "attn"),),
    "sc_bwd": (("scatter_sibling", "mixer"),),
    "d_l0_in": (("scatter_chips", "mixer"),),
}
FINISH = (("scatter_done", "ffn0"), ("scatter_done", "mixer"))
STAGES = {"gather_start": 1, "gather_forward": 2, "gather_done": 3,
          "scatter_sibling": 1, "scatter_chips": 2, "scatter_done": 3}
SMALL_W_ROWS = 24


def _pack(arrays, rows):
    flat = jnp.concatenate([a.reshape(-1).astype(F32) for a in arrays])
    return jnp.pad(flat, (0, rows * 128 - flat.shape[0])).reshape(rows, 128)


def _stored(name, a):
    return jnp.swapaxes(a, -1, -2) if name == "ffn_w_up" else a


def _base(name):
    if name.startswith("ffn_w_") and name[-1] in "01":
        return name[:-1], int(name[-1])
    return name, None


class _Exchange:
    def __init__(self, wts, mom, var, ffn_conv_b):
        self.wts, self.mom, self.var = wts, mom, var
        x, y, c = _place()
        self.me = 4 * x + 2 * y + c
        self.c_arr = jnp.reshape(c, (1,)).astype(jnp.int32)
        chip = 2 * x + y
        self.chip_ids = jnp.stack([chip, chip ^ 1, chip ^ 2, chip ^ 3]).astype(jnp.int32)
        self.ready = {"ffn_cb0": ffn_conv_b.reshape(2, N_FF_BLK, 1, FF_BLK)[0],
                      "ffn_cb1": ffn_conv_b.reshape(2, N_FF_BLK, 1, FF_BLK)[1]}
        self.gathers, self.group_of = {}, {}
        self.grads, self.scatters, self.results, self.queue = {}, {}, {}, []
        for gname, names in GATHER_GROUPS:
            self.gathers[gname] = dict(stage=0, names=names, kinds=[KIND[_base(nm)[0]] for nm in names])
            for nm in names:
                self.group_of[nm] = gname
        for nm in ("sc_conv_w", "ffn_cw0", "ffn_cw1"):
            self.group_of[nm] = "mixer"
        self.at("begin", None)

    def _shard(self, name):
        if name == "conv":
            return _pack([self.wts["sc_conv_w"], self.wts["ffn_conv_w"]], SMALL_W_ROWS).reshape(1, SMALL_W_ROWS, 128)
        base, layer = _base(name)
        a = _stored(base, self.wts[base])
        if layer is not None:
            a = a[layer:layer + 1]
        if KIND[base] == "cols":
            return a.reshape(a.shape[-2], a.shape[-1]).astype(BF16)
        return a.reshape((-1,) + a.shape[-2:]).astype(BF16)

    def _start(self, name, srcs, lands, ncopy, plan, st):
        self.queue.append((name, (srcs, lands, ncopy, plan), st))

    def _flush(self):
        if self.queue:
            flights = _copies_start("__".join(name for name, _, _ in self.queue), [job for _, job, _ in self.queue])
            for (_, _, st), flight in zip(self.queue, flights):
                st["flight"] = flight
            self.queue = []

    def _flight(self, st):
        self._flush()
        return st["flight"]

    def _gather_to(self, gname, stage, after):
        st = self.gathers[gname]
        if st["stage"] < 1 <= stage:
            shards = [self._shard(nm) for nm in st["names"]]
            lands = [_landing(s, kind) for s, kind in zip(shards, st["kinds"])]
            plan, ncopy = _plan_gather_chips(st["kinds"])
            self._start(f"ag_{gname}_chips", shards, lands, ncopy, plan, st)
            st["stage"] = 1
        if st["stage"] < 2 <= stage:
            plan, ncopy = _plan_gather_chips(st["kinds"])
            _, lands = _copies_wait(f"ag_{gname}_chips_wait", self._flight(st), ncopy, plan)
            plan, ncopy = _plan_gather_sibling(st["kinds"])
            self._start(f"ag_{gname}_sibling", [], lands, ncopy, plan, st)
            st["stage"] = 2
        if st["stage"] < 3 <= stage:
            plan, ncopy = _plan_gather_sibling(st["kinds"])
            _, lands = _copies_wait(f"ag_{gname}_sibling_wait", self._flight(st), ncopy, plan)
            for nm, land in zip(st["names"], lands):
                self._arrived(nm, land)
            st["stage"] = 3

    def _arrived(self, name, land):
        if name == "conv":
            conv = land.reshape(N_DEV, SMALL_W_ROWS * 128)
            self.ready["sc_conv_w"] = conv[:, :3 * 128].reshape(N_DEV, 3, 128).transpose(1, 0, 2).reshape(3, D)
            fcw = conv[:, 3 * 128:3 * 128 + 6 * 352].reshape(N_DEV, 2, 3, 352).transpose(1, 2, 0, 3)
            fcw = fcw.reshape(2, 3, N_FF_BLK, FF_BLK).transpose(0, 2, 1, 3)
            self.ready["ffn_cw0"], self.ready["ffn_cw1"] = fcw[0], fcw[1]
        elif name in ("sc_w_in", "w_uk", "w_uv") or name.startswith("ffn_w_up"):
            self.ready[name] = land
        elif name.startswith("ffn_w_down"):
            self.ready[name] = land.reshape(1, N_FF_BLK, FF_BLK, D)
        elif name == "w_kr":
            self.ready[name] = jnp.pad(land.reshape(D, QK_ROPE), ((0, 0), (0, 128 - QK_ROPE)))
        elif name == "w_uq":
            self.ready[name] = jnp.pad(land.reshape(N_HEADS, Q_LORA, QK_NOPE + QK_ROPE),
                                       ((0, 0), (0, 0), (0, QK_PAD - QK_NOPE - QK_ROPE)))
        else:
            self.ready[name] = land.reshape(D, land.shape[-1])

    def need(self, name, after):
        if name not in self.ready:
            self._gather_to(self.group_of[name], 3, after)
            self._flush()
        return self.ready[name]

    def grad(self, name, layer, array):
        self.grads[(name, layer)] = array

    def _scatter_to(self, gname, stage, after):
        keys = dict(SCATTER_GROUPS)[gname]
        st = self.scatters.setdefault(gname, dict(stage=0))
        kinds = [KIND[nm] for nm, _ in keys]
        if st["stage"] < 1 <= stage:
            grads = [self.grads[key] for key in keys]
            lands = []
            for gr, kind in zip(grads, kinds):
                shard = (gr.shape[0],) + gr.shape[2:] if kind == "blocked" else (gr.shape[0], gr.shape[1] // N_DEV)
                lands.append(lax.empty((N_CHIP,) + shard, BF16))
            plan, ncopy = _plan_scatter_sibling(kinds)
            self._start(f"rs_{gname}_sibling", grads, lands, ncopy, plan, st)
            st["stage"] = 1
        if st["stage"] < 2 <= stage:
            plan, ncopy = _plan_scatter_sibling(kinds)
            grads, recvs = _copies_wait(f"rs_{gname}_sibling_wait", self._flight(st), ncopy, plan)
            sums = _chip_sums(f"rs_{gname}_sums", grads, kinds, recvs, self.c_arr)
            lands = [lax.empty(s.shape, BF16) for s in sums]
            plan, ncopy = _plan_scatter_chips(len(sums))
            self._start(f"rs_{gname}_chips", sums, lands, ncopy, plan, st)
            st["stage"] = 2
        if st["stage"] < 3 <= stage:
            plan, ncopy = _plan_scatter_chips(len(keys))
            sums, recvs = _copies_wait(f"rs_{gname}_chips_wait", self._flight(st), ncopy, plan)
            items = []
            for (nm, layer), own, rv in zip(keys, sums, recvs):
                nl = 1 if layer is None else 2
                rows, w = own.shape[1], own.shape[2]
                w3, m3, v3 = (_stored(nm, src[nm]).reshape(nl, rows, w) for src in (self.wts, self.mom, self.var))
                items.append((own, rv, w3, m3, v3, 0 if layer is None else layer, self.results.get(nm)))
            outs = _adamw_group(f"adamw_{gname}", items, self.chip_ids)
            for (nm, _), out in zip(keys, outs):
                self.results[nm] = out
            st["stage"] = 3

    def at(self, place, after):
        for action, gname in SCHEDULE.get(place, ()):
            self._advance(action, gname, after)
        self._flush()

    def _advance(self, action, gname, after):
        if action.startswith("gather"):
            self._gather_to(gname, STAGES[action], after)
        else:
            self._scatter_to(gname, STAGES[action], after)

    def finish(self, after):
        for action, gname in FINISH:
            self._advance(action, gname, after)
        for gname, _ in SCATTER_GROUPS:
            self._scatter_to(gname, 3, after)
        return {nm: [_stored(nm, o.reshape(_stored(nm, self.wts[nm]).shape)) for o in outs]
                for nm, outs in self.results.items()}


REPLICATED = ("attn_norm", "ffn_norm", "final_norm", "kv_in_norm", "kv_latent_norm", "q_latent_norm", "ffn_conv_b")
WEIGHTS = ("attn_norm", "ffn_norm", "final_norm", "sc_w_in", "sc_conv_w", "sc_w_out", "kv_in_norm", "w_dkv",
           "kv_latent_norm", "w_kr", "w_uk", "w_uv", "w_dq", "q_latent_norm", "w_uq", "w_o", "ffn_w_up", "ffn_conv_w",
           "ffn_conv_b", "ffn_w_down")


def kernel(x, positions, attn_norm, ffn_norm, final_norm, sc_w_in, sc_conv_w, sc_w_out, kv_in_norm, w_dkv, kv_latent_norm, w_kr, w_uk, w_uv, w_dq, q_latent_norm, w_uq, w_o, ffn_w_up, ffn_conv_w, ffn_conv_b, ffn_w_down, loss_target, m_attn_norm, m_ffn_norm, m_final_norm, m_sc_w_in, m_sc_conv_w, m_sc_w_out, m_kv_in_norm, m_w_dkv, m_kv_latent_norm, m_w_kr, m_w_uk, m_w_uv, m_w_dq, m_q_latent_norm, m_w_uq, m_w_o, m_ffn_w_up, m_ffn_conv_w, m_ffn_conv_b, m_ffn_w_down, v_attn_norm, v_ffn_norm, v_final_norm, v_sc_w_in, v_sc_conv_w, v_sc_w_out, v_kv_in_norm, v_w_dkv, v_kv_latent_norm, v_w_kr, v_w_uk, v_w_uv, v_w_dq, v_q_latent_norm, v_w_uq, v_w_o, v_ffn_w_up, v_ffn_conv_w, v_ffn_conv_b, v_ffn_w_down):
    wts = dict(attn_norm=attn_norm, ffn_norm=ffn_norm, final_norm=final_norm, sc_w_in=sc_w_in, sc_conv_w=sc_conv_w,
               sc_w_out=sc_w_out, kv_in_norm=kv_in_norm, w_dkv=w_dkv, kv_latent_norm=kv_latent_norm, w_kr=w_kr,
               w_uk=w_uk, w_uv=w_uv, w_dq=w_dq, q_latent_norm=q_latent_norm, w_uq=w_uq, w_o=w_o, ffn_w_up=ffn_w_up,
               ffn_conv_w=ffn_conv_w, ffn_conv_b=ffn_conv_b, ffn_w_down=ffn_w_down)
    mom = dict(attn_norm=m_attn_norm, ffn_norm=m_ffn_norm, final_norm=m_final_norm, sc_w_in=m_sc_w_in,
               sc_conv_w=m_sc_conv_w, sc_w_out=m_sc_w_out, kv_in_norm=m_kv_in_norm, w_dkv=m_w_dkv,
               kv_latent_norm=m_kv_latent_norm, w_kr=m_w_kr, w_uk=m_w_uk, w_uv=m_w_uv, w_dq=m_w_dq,
               q_latent_norm=m_q_latent_norm, w_uq=m_w_uq, w_o=m_w_o, ffn_w_up=m_ffn_w_up, ffn_conv_w=m_ffn_conv_w,
               ffn_conv_b=m_ffn_conv_b, ffn_w_down=m_ffn_w_down)
    var = dict(attn_norm=v_attn_norm, ffn_norm=v_ffn_norm, final_norm=v_final_norm, sc_w_in=v_sc_w_in,
               sc_conv_w=v_sc_conv_w, sc_w_out=v_sc_w_out, kv_in_norm=v_kv_in_norm, w_dkv=v_w_dkv,
               kv_latent_norm=v_kv_latent_norm, w_kr=v_w_kr, w_uk=v_w_uk, w_uv=v_w_uv, w_dq=v_w_dq,
               q_latent_norm=v_q_latent_norm, w_uq=v_w_uq, w_o=v_w_o, ffn_w_up=v_ffn_w_up, ffn_conv_w=v_ffn_conv_w,
               ffn_conv_b=v_ffn_conv_b, ffn_w_down=v_ffn_w_down)
    xi, yi, ci = _place()
    me = 4 * xi + 2 * yi + ci
    _Chain.last = None

    ex = _Exchange(wts, mom, var, ffn_conv_b)
    rep = {
        "attn_norm": attn_norm, "ffn_norm": ffn_norm, "final_norm": final_norm,
        "kv_in_norm": kv_in_norm.reshape(1, D), "kv_latent_norm": kv_latent_norm.reshape(1, KV_LORA),
        "q_latent_norm": q_latent_norm.reshape(1, Q_LORA),
    }
    loss, grad_x, small = _local_step(x.reshape(T, D), positions.reshape(T, 1), loss_target.reshape(T, D), rep, ex)
    results = ex.finish(grad_x)

    def rows_of(a):
        return a.reshape(-1, a.shape[-1])

    small_order = list(REPLICATED) + ["sc_conv_w", "ffn_conv_w"]
    shards = [loss.reshape(1, 1, 128)] + [rows_of(small[nm])[None] for nm in small_order]
    gathered = _all_gather("ag_small_grads", [(s, "blocked") for s in shards])
    params = [[None] + [rows_of(src[nm]) for nm in REPLICATED] + [None, None] for src in (wts, mom, var)]
    summed = _adamw_small(gathered, *params)
    loss_total = summed[0][0][0, 0]
    for nm, vals in zip(REPLICATED, summed[1:1 + len(REPLICATED)]):
        results[nm] = [a.reshape(wts[nm].shape) for a in vals]
    g_scw = lax.dynamic_slice(summed[-2][0], (0, me * 128), (3, 128))
    g_fcw = lax.dynamic_slice(summed[-1][0], (0, me * 352), (6, 352))
    conv = _adamw_plain("adamw_conv", [g_scw, g_fcw], *[[rows_of(src["sc_conv_w"]), rows_of(src["ffn_conv_w"])]
                                                        for src in (wts, mom, var)])
    for nm, g_own, vals in zip(("sc_conv_w", "ffn_conv_w"), (g_scw, g_fcw), conv):
        results[nm] = [a.reshape(wts[nm].shape) for a in [g_own] + vals]

    outs = [loss_total, grad_x.reshape(1, T, D)]
    for slot in range(4):
        outs.extend(results[nm][slot] for nm in WEIGHTS)
    return tuple(outs)
```

```python
import jax
import jax.numpy as jnp
from jax import lax
from jax.experimental import pallas as pl
from jax.experimental.pallas import tpu as pltpu

F32 = jnp.float32
BF16 = jnp.bfloat16

T = 2048
D = 1024
N_HEADS = 8
QK_NOPE = 128
QK_ROPE = 64
V_HEAD = 128
Q_LORA = 384
KV_LORA = 256
D_FF = 2816
CHUNK = 64
ROPE_THETA = 10000.0
EPS = 1e-6
NEG_INF = -1e30
ADAM_LR = 0.001
ADAM_B1 = 0.9
ADAM_B2 = 0.999
ADAM_EPS = 1e-08
ADAM_WD = 0.01
ADAM_STEP = 10

N_DEV = 8
N_CHIP = 4
FF_BLK = D_FF * 2 // N_DEV
N_FF_BLK = D_FF // FF_BLK
QK_PAD = 256
HALO = 16

TM = 1024
TS = 512
TR = 256
TQ = 512
VMEM_LIMIT = 56 * 1024 * 1024

NN = (((1,), (0,)), ((), ()))
NT = (((1,), (1,)), ((), ()))
TN = (((0,), (0,)), ((), ()))
MESH = pl.DeviceIdType.MESH


def _params(sem):
    return pltpu.CompilerParams(dimension_semantics=sem, vmem_limit_bytes=VMEM_LIMIT)


ANY_SPEC = pl.BlockSpec(memory_space=pl.ANY)
VMEM_SPEC = pl.BlockSpec(memory_space=pltpu.VMEM)


class _Chain:
    last = None


def _pallas(body, *, name, in_specs, out_specs, out_shape, grid=(), scratch_shapes=(), n_prefetch=0, aliases=None,
            params=None):
    def run(*args):
        after = _Chain.last
        n_lead = len(args)
        specs, operands, fn = list(in_specs), list(args), body
        if after is not None:
            def fn(*refs):
                return body(*refs[:n_lead], *refs[n_lead + 1:])
            specs.append(ANY_SPEC)
            operands.append(after)
        kw = dict(name=name, out_shape=out_shape, input_output_aliases=aliases or {})
        if params is not None:
            kw["compiler_params"] = params
        if n_prefetch:
            kw["grid_spec"] = pltpu.PrefetchScalarGridSpec(
                num_scalar_prefetch=n_prefetch, grid=grid, in_specs=specs, out_specs=out_specs,
                scratch_shapes=scratch_shapes)
        else:
            kw.update(grid=grid, in_specs=specs, out_specs=out_specs, scratch_shapes=scratch_shapes)
        outs = pl.pallas_call(fn, **kw)(*operands)
        _Chain.last = outs[0] if isinstance(outs, (list, tuple)) else outs
        return outs
    return run


def _mm(name, a, b, *, grid, a_spec, b_spec, o_spec, o_shape, o_dtype, dims, k_axis=None, acc_shape=None,
        add=None, add_spec=None):
    nk = grid[k_axis] if k_axis is not None else 1
    has_add = add is not None

    def body(*refs):
        a_ref, b_ref = refs[0], refs[1]
        p = 2
        add_ref = None
        if has_add:
            add_ref = refs[p]
            p += 1
        o_ref = refs[p]
        p += 1
        r = lax.dot_general(a_ref[...].astype(BF16), b_ref[...].astype(BF16), dims, preferred_element_type=F32)
        if k_axis is None:
            if has_add:
                r = r + add_ref[...].astype(F32)
            o_ref[...] = r.astype(o_dtype)
        else:
            acc = refs[p]
            k = pl.program_id(k_axis)

            @pl.when(k == 0)
            def _():
                acc[...] = r

            @pl.when(k > 0)
            def _():
                acc[...] += r

            @pl.when(k == nk - 1)
            def _():
                t = acc[...]
                if has_add:
                    t = t + add_ref[...].astype(F32)
                o_ref[...] = t.astype(o_dtype)

    in_specs = [a_spec, b_spec]
    args = [a, b]
    if has_add:
        in_specs.append(add_spec if add_spec is not None else o_spec)
        args.append(add)
    sem = tuple("arbitrary" if ax == k_axis else "parallel" for ax in range(len(grid)))
    scratch = [pltpu.VMEM(acc_shape, F32)] if k_axis is not None else []
    return _pallas(body, name=name, grid=grid, in_specs=in_specs, out_specs=o_spec,
                   out_shape=jax.ShapeDtypeStruct(o_shape, o_dtype), scratch_shapes=scratch, params=_params(sem))(*args)


def _mm_sum(name, parts, *, grid, o_spec, o_shape, o_dtype, add=None, norm_bwd=None):
    has_add = add is not None
    np_ = len(parts)
    nn = 1 if norm_bwd is None else len(norm_bwd[1])
    has_res = norm_bwd is not None and norm_bwd[2] is not None

    def body(*refs):
        accs = [None] * nn
        for p, (_, _, _, _, dims, n) in enumerate(parts):
            a_ref, b_ref = refs[2 * p], refs[2 * p + 1]
            for k in range(a_ref.shape[0]):
                r = lax.dot_general(a_ref[k], b_ref[k], dims, preferred_element_type=F32)
                accs[n] = r if accs[n] is None else accs[n] + r
        if norm_bwd is None:
            acc = accs[0]
            if has_add:
                acc = acc + refs[2 * np_][...]
            refs[-1][...] = acc.astype(o_dtype)
            return
        x_ref, g_refs = refs[2 * np_], refs[2 * np_ + 1:2 * np_ + 1 + nn]
        dx_ref, dxb_ref, dg_refs = refs[-2 - nn], refs[-1 - nn], refs[-nn:]
        xv = x_ref[...]
        r = lax.rsqrt(jnp.mean(xv * xv, axis=-1, keepdims=True) + EPS)
        xn = xv * r
        dx = refs[2 * np_ + 1 + nn][...] if has_res else None
        sums = []
        for acc, g_ref in zip(accs, g_refs):
            gdy = acc * g_ref[...]
            t = r * (gdy - xn * jnp.mean(gdy * xn, axis=-1, keepdims=True))
            dx = t if dx is None else dx + t
            sums.append(jnp.sum(acc * xn, axis=0, keepdims=True))
        dx_ref[...] = dx
        dxb_ref[...] = dx.astype(BF16)

        @pl.when(pl.program_id(0) == 0)
        def _():
            for dg_ref, part in zip(dg_refs, sums):
                dg_ref[...] = part

        @pl.when(pl.program_id(0) > 0)
        def _():
            for dg_ref, part in zip(dg_refs, sums):
                dg_ref[...] += part

    in_specs, args = [], []
    for a, a_spec, b, b_spec, _, _ in parts:
        in_specs += [a_spec, b_spec]
        args += [a, b]
    if norm_bwd is None:
        if has_add:
            in_specs.append(o_spec)
            args.append(add)
        return _pallas(body, name=name, grid=grid, in_specs=in_specs, out_specs=o_spec,
                       out_shape=jax.ShapeDtypeStruct(o_shape, o_dtype),
                       params=_params(("parallel",) * len(grid)))(*args)
    x, gains, dres = norm_bwd
    vec = pl.BlockSpec((1, o_shape[1]), lambda i: (0, 0))
    in_specs += [o_spec] + [vec] * nn + ([o_spec] if has_res else [])
    args += [x] + list(gains) + ([dres] if has_res else [])
    outs = _pallas(body, name=name, grid=grid, in_specs=in_specs, out_specs=[o_spec, o_spec] + [vec] * nn,
                   out_shape=[jax.ShapeDtypeStruct(o_shape, F32), jax.ShapeDtypeStruct(o_shape, BF16)]
                   + [jax.ShapeDtypeStruct((1, o_shape[1]), F32)] * nn,
                   params=_params(("arbitrary",)))(*args)
    return outs[0], outs[1], list(outs[2:])


def _mm_rows(name, a, b, dims, o_dtype, n_out, *, tn=None, add=None):
    k = a.shape[1]
    tn = n_out if tn is None else tn
    if dims == NN:
        b_spec = pl.BlockSpec((k, tn), lambda n, i: (0, n))
    else:
        b_spec = pl.BlockSpec((tn, k), lambda n, i: (n, 0))
    return _mm(name, a, b, grid=(n_out // tn, T // TM),
               a_spec=pl.BlockSpec((TM, k), lambda n, i: (i, 0)), b_spec=b_spec,
               o_spec=pl.BlockSpec((TM, tn), lambda n, i: (i, n)), o_shape=(T, n_out), o_dtype=o_dtype,
               dims=dims, add=add)


def _mm_wgrad(name, a, b, *, tn=512):
    k, n = a.shape[1], b.shape[1]
    tn = min(tn, n)
    return _mm(name, a, b, grid=(n // tn,),
               a_spec=pl.BlockSpec((T, k), lambda j: (0, 0)), b_spec=pl.BlockSpec((T, tn), lambda j: (0, j)),
               o_spec=pl.BlockSpec((k, tn), lambda j: (0, j)), o_shape=(k, n), o_dtype=BF16, dims=TN)


def _rms_fwd(name, x, g):
    d = x.shape[1]

    def body(x_ref, g_ref, o_ref):
        xv = x_ref[...]
        r = lax.rsqrt(jnp.mean(xv * xv, axis=-1, keepdims=True) + EPS)
        o_ref[...] = ((xv * r) * g_ref[...]).astype(BF16)

    return _pallas(
        body, name=name, grid=(T // TM,),
        in_specs=[pl.BlockSpec((TM, d), lambda i: (i, 0)), pl.BlockSpec((1, d), lambda i: (0, 0))],
        out_specs=pl.BlockSpec((TM, d), lambda i: (i, 0)),
        out_shape=jax.ShapeDtypeStruct((T, d), BF16), params=_params(("parallel",)))(x, g)


def _rows_call(name, body, row_ins, whole_ins, outs):
    in_specs = [pl.BlockSpec((TM, a.shape[1]), lambda i: (i, 0)) for a in row_ins]
    in_specs += [pl.BlockSpec(a.shape, lambda i: (0, 0)) for a in whole_ins]
    return _pallas(
        body, name=name, grid=(T // TM,), in_specs=in_specs,
        out_specs=[pl.BlockSpec((TM, d), lambda i: (i, 0)) for d, _ in outs],
        out_shape=[jax.ShapeDtypeStruct((T, d), dt) for d, dt in outs],
        params=_params(("parallel",)))(*row_ins, *whole_ins)


def _rms(xv, g):
    return (xv * lax.rsqrt(jnp.mean(xv * xv, axis=-1, keepdims=True) + EPS)) * g


def _rms_fwd2(name, x, g1, g2):
    d = x.shape[1]

    def body(x_ref, g1_ref, g2_ref, o1_ref, o2_ref):
        xv = x_ref[...]
        xn = xv * lax.rsqrt(jnp.mean(xv * xv, axis=-1, keepdims=True) + EPS)
        o1_ref[...] = (xn * g1_ref[...]).astype(BF16)
        o2_ref[...] = (xn * g2_ref[...]).astype(BF16)

    return _rows_call(name, body, [x], [g1, g2], [(d, BF16), (d, BF16)])


def _down_norm(name, a, w, g):
    n = w.shape[1]

    def body(a_ref, w_ref, g_ref, raw_ref, o_ref):
        raw = lax.dot_general(a_ref[...], w_ref[...], NN, preferred_element_type=F32)
        raw_ref[...] = raw
        o_ref[...] = _rms(raw, g_ref[...]).astype(BF16)

    return _rows_call(name, body, [a], [w, g], [(n, F32), (n, BF16)])


def _kv_down(hk, w_dkv, w_kr, g, tables):
    def body(a_ref, c_ref, sa_ref, sb_ref, wd_ref, wr_ref, g_ref, raw_ref, ckv_ref, kr_ref):
        av = a_ref[...]
        raw = lax.dot_general(av, wd_ref[...], NN, preferred_element_type=F32)
        raw_ref[...] = raw
        ckv_ref[...] = _rms(raw, g_ref[...]).astype(BF16)
        kr = lax.dot_general(av, wr_ref[...], NN, preferred_element_type=F32)
        kr_ref[...] = _rotate(kr, c_ref[...], sa_ref[...], sb_ref[...], 1.0).astype(BF16)

    return _rows_call("kv_down", body, [hk, *tables], [w_dkv, w_kr, g], [(KV_LORA, F32), (KV_LORA, BF16), (128, BF16)])


def _kv_up(ckv, w_uk, w_uv):
    def body(a_ref, wk_ref, wv_ref, k_ref, v_ref):
        av = a_ref[...]
        k_ref[...] = lax.dot_general(av, wk_ref[...], NN, preferred_element_type=F32).astype(BF16)
        v_ref[...] = lax.dot_general(av, wv_ref[...], NN, preferred_element_type=F32).astype(BF16)

    return _rows_call("kv_up", body, [ckv], [w_uk, w_uv], [(N_HEADS * QK_NOPE, BF16), (N_HEADS * V_HEAD, BF16)])


def _rms_bwd(name, x, gains, dys, dres=None):
    d = x.shape[1]
    n = len(gains)
    has_res = dres is not None

    def body(*refs):
        x_ref, g_refs, dy_refs = refs[0], refs[1:1 + n], refs[1 + n:1 + 2 * n]
        dx_ref, dxb_ref = refs[-2 - n], refs[-1 - n]
        dg_refs = refs[-n:]
        xv = x_ref[...]
        r = lax.rsqrt(jnp.mean(xv * xv, axis=-1, keepdims=True) + EPS)
        xn = xv * r
        dx = refs[1 + 2 * n][...] if has_res else None
        parts = []
        for g_ref, dy_ref in zip(g_refs, dy_refs):
            dyv = dy_ref[...].astype(F32)
            gdy = dyv * g_ref[...]
            t = r * (gdy - xn * jnp.mean(gdy * xn, axis=-1, keepdims=True))
            dx = t if dx is None else dx + t
            parts.append(jnp.sum(dyv * xn, axis=0, keepdims=True))
        dx_ref[...] = dx
        dxb_ref[...] = dx.astype(BF16)

        @pl.when(pl.program_id(0) == 0)
        def _():
            for dg_ref, part in zip(dg_refs, parts):
                dg_ref[...] = part

        @pl.when(pl.program_id(0) > 0)
        def _():
            for dg_ref, part in zip(dg_refs, parts):
                dg_ref[...] += part

    row = pl.BlockSpec((TR, d), lambda i: (i, 0))
    vec = pl.BlockSpec((1, d), lambda i: (0, 0))
    args = [x] + list(gains) + list(dys) + ([dres] if has_res else [])
    in_specs = [row] + [vec] * n + [row] * n + ([row] if has_res else [])
    outs = _pallas(
        body, name=name, grid=(T // TR,), in_specs=in_specs, out_specs=[row, row] + [vec] * n,
        out_shape=[jax.ShapeDtypeStruct((T, d), F32), jax.ShapeDtypeStruct((T, d), BF16)]
        + [jax.ShapeDtypeStruct((1, d), F32)] * n,
        params=_params(("arbitrary",)))(*args)
    return outs[0], outs[1], list(outs[2:])


def _final(h, g, tgt):
    def body(h_ref, g_ref, t_ref, loss_ref, dh_ref, dhb_ref, dg_ref):
        hv = h_ref[...]
        r = lax.rsqrt(jnp.mean(hv * hv, axis=-1, keepdims=True) + EPS)
        xn = hv * r
        gv = g_ref[...]
        err = xn * gv - t_ref[...]
        part_loss = 0.5 * jnp.sum(jnp.mean(err * err, axis=-1, keepdims=True), axis=0, keepdims=True)
        dy = err * (1.0 / D)
        gdy = dy * gv
        dh = r * (gdy - xn * jnp.mean(gdy * xn, axis=-1, keepdims=True))
        dh_ref[...] = dh
        dhb_ref[...] = dh.astype(BF16)
        part = jnp.sum(dy * xn, axis=0, keepdims=True)
        first = pl.program_id(0) == 0

        @pl.when(first)
        def _():
            dg_ref[...] = part
            loss_ref[...] = jnp.broadcast_to(part_loss, (1, 128))

        @pl.when(jnp.logical_not(first))
        def _():
            dg_ref[...] += part
            loss_ref[...] += jnp.broadcast_to(part_loss, (1, 128))

    row = pl.BlockSpec((TR, D), lambda i: (i, 0))
    vec = pl.BlockSpec((1, D), lambda i: (0, 0))
    return _pallas(
        body, name="final_loss", grid=(T // TR,), in_specs=[row, vec, row],
        out_specs=[pl.BlockSpec((1, 128), lambda i: (0, 0)), row, row, vec],
        out_shape=[jax.ShapeDtypeStruct((1, 128), F32), jax.ShapeDtypeStruct((T, D), F32),
                   jax.ShapeDtypeStruct((T, D), BF16), jax.ShapeDtypeStruct((1, D), F32)],
        params=_params(("arbitrary",)))(h, g, tgt)


def _prev_idx(i, rows=TR):
    return jnp.maximum(i * (rows // HALO) - 1, 0)


def _next_idx(i, rows=TR):
    return jnp.minimum((i + 1) * (rows // HALO), T // HALO - 1)


def _causal_taps(ext):
    return pltpu.roll(ext, 2, 0)[HALO:], pltpu.roll(ext, 1, 0)[HALO:], ext[HALO:]


def _anticausal_taps(ext, n):
    rows = ext.shape[0]
    return pltpu.roll(ext, rows - 1, 0)[:n], pltpu.roll(ext, rows - 2, 0)[:n]


MIX_COLS = 512


def _mixer_in(hn, w_in, w):
    nc = D // MIX_COLS

    def body(h_ref, hh_ref, wb_ref, wc_ref, wu_ref, w_ref, b_ref, c_ref, u_ref, y_ref):
        i = pl.program_id(1)
        hv = h_ref[...]
        he = jnp.concatenate([hh_ref[...], hv], axis=0)
        ce = lax.dot_general(he, wc_ref[...], NN, preferred_element_type=F32).astype(BF16)
        ue = lax.dot_general(he, wu_ref[...], NN, preferred_element_type=F32).astype(BF16)
        bv = lax.dot_general(hv, wb_ref[...], NN, preferred_element_type=F32).astype(BF16)
        b_ref[...] = bv
        c_ref[...] = ce[HALO:]
        u_ref[...] = ue[HALO:]
        row = lax.broadcasted_iota(jnp.int32, (HALO + TS, 1), 0)
        cu = jnp.where(jnp.logical_or(i > 0, row >= HALO), ce.astype(F32) * ue.astype(F32), 0.0)
        x2, x1, x0 = _causal_taps(cu)
        wv = w_ref[...]
        cv = (x2 * wv[0:1] + x1 * wv[1:2]) + x0 * wv[2:3]
        y_ref[...] = (bv.astype(F32) * cv).astype(BF16)

    def cols(part):
        return pl.BlockSpec((D, MIX_COLS), lambda j, i: (0, part * nc + j))

    blk = pl.BlockSpec((TS, MIX_COLS), lambda j, i: (i, j))
    out = jax.ShapeDtypeStruct((T, D), BF16)
    return _pallas(
        body, name="l0_in", grid=(nc, T // TS),
        in_specs=[pl.BlockSpec((TS, D), lambda j, i: (i, 0)), pl.BlockSpec((HALO, D), lambda j, i: (_prev_idx(i, TS), 0)),
                  cols(0), cols(1), cols(2), pl.BlockSpec((3, MIX_COLS), lambda j, i: (0, j))],
        out_specs=[blk] * 4, out_shape=[out] * 4,
        params=_params(("parallel", "parallel")))(hn, hn, w_in, w_in, w_in, w)


def _mixer_out_bwd(dh, w_out, zb, zc, zu, w):
    last = T // TR - 1

    def body(dh_ref, dhn_ref, wo_ref, b_ref, bn_ref, c_ref, ch_ref, u_ref, uh_ref, w_ref, dz_ref, dw_ref):
        i = pl.program_id(0)
        dye = lax.dot_general(jnp.concatenate([dh_ref[...], dhn_ref[...]], axis=0), wo_ref[...], NT,
                              preferred_element_type=F32)
        cv_ = c_ref[...].astype(F32)
        uv = u_ref[...].astype(F32)
        cu = cv_ * uv
        cuh = jnp.where(i > 0, ch_ref[...].astype(F32) * uh_ref[...].astype(F32), 0.0)
        x2, x1, x0 = _causal_taps(jnp.concatenate([cuh, cu], axis=0))
        wv = w_ref[...]
        conv = (x2 * wv[0:1] + x1 * wv[1:2]) + x0 * wv[2:3]
        dyv = dye[:TR]
        dz_ref[:, 0:D] = (dyv * conv).astype(BF16)
        dconv = dyv * b_ref[...].astype(F32)
        dconv_n = jnp.where(i < last, dye[TR:] * bn_ref[...].astype(F32), 0.0)
        n1, n2 = _anticausal_taps(jnp.concatenate([dconv, dconv_n], axis=0), TR)
        dcu = (dconv * wv[2:3] + n1 * wv[1:2]) + n2 * wv[0:1]
        dz_ref[:, D:2 * D] = (dcu * uv).astype(BF16)
        dz_ref[:, 2 * D:3 * D] = (dcu * cv_).astype(BF16)
        part = jnp.concatenate([jnp.sum(dconv * x2, axis=0, keepdims=True),
                                jnp.sum(dconv * x1, axis=0, keepdims=True),
                                jnp.sum(dconv * x0, axis=0, keepdims=True)], axis=0)

        @pl.when(i == 0)
        def _():
            dw_ref[...] = part

        @pl.when(i > 0)
        def _():
            dw_ref[...] += part

    main = pl.BlockSpec((TR, D), lambda i: (i, 0))
    prev = pl.BlockSpec((HALO, D), lambda i: (_prev_idx(i), 0))
    nxt = pl.BlockSpec((HALO, D), lambda i: (_next_idx(i), 0))
    wspec = pl.BlockSpec((3, D), lambda i: (0, 0))
    return _pallas(
        body, name="d_l0_out", grid=(T // TR,),
        in_specs=[main, nxt, pl.BlockSpec((D, D), lambda i: (0, 0)), main, nxt, main, prev, main, prev, wspec],
        out_specs=[pl.BlockSpec((TR, 3 * D), lambda i: (i, 0)), wspec],
        out_shape=[jax.ShapeDtypeStruct((T, 3 * D), BF16), jax.ShapeDtypeStruct((3, D), F32)],
        params=_params(("arbitrary",)))(dh, dh, w_out, zb, zb, zc, zc, zu, zu, w)


def _sigmoid(x):
    return 0.5 * jnp.tanh(0.5 * x) + 0.5


def _ffn_up_act(name, hf, w_up, w, b):
    def body(h_ref, hh_ref, wg_ref, wv_ref, w_ref, b_ref, g_ref, v_ref, a_ref):
        i = pl.program_id(1)
        hv = h_ref[...]
        ge = lax.dot_general(jnp.concatenate([hh_ref[...], hv], axis=0), wg_ref[...], NT,
                             preferred_element_type=F32).astype(BF16)
        v = lax.dot_general(hv, wv_ref[...], NT, preferred_element_type=F32).astype(BF16)
        g_ref[...] = ge[HALO:]
        v_ref[...] = v
        ext = ge.astype(F32)
        row = lax.broadcasted_iota(jnp.int32, (HALO + TS, 1), 0)
        ext = jnp.where(jnp.logical_or(i > 0, row >= HALO), ext, 0.0)
        x2, x1, x0 = _causal_taps(ext)
        wv = w_ref[...]
        gc = ((x2 * wv[0:1] + x1 * wv[1:2]) + x0 * wv[2:3]) + b_ref[...]
        a_ref[...] = ((gc * _sigmoid(gc)) * v.astype(F32)).astype(BF16)

    blk = pl.BlockSpec((None, TS, FF_BLK), lambda j, i: (j, i, 0))
    out = jax.ShapeDtypeStruct((N_FF_BLK, T, FF_BLK), BF16)
    return _pallas(
        body, name=name, grid=(N_FF_BLK, T // TS),
        in_specs=[pl.BlockSpec((TS, D), lambda j, i: (i, 0)),
                  pl.BlockSpec((HALO, D), lambda j, i: (_prev_idx(i, TS), 0)),
                  pl.BlockSpec((None, None, FF_BLK, D), lambda j, i: (0, j, 0, 0)),
                  pl.BlockSpec((None, None, FF_BLK, D), lambda j, i: (0, j + N_FF_BLK, 0, 0)),
                  pl.BlockSpec((None, 3, FF_BLK), lambda j, i: (j, 0, 0)),
                  pl.BlockSpec((None, 1, FF_BLK), lambda j, i: (j, 0, 0))],
        out_specs=[blk, blk, blk], out_shape=[out, out, out],
        params=_params(("parallel", "parallel")))(hf, hf, w_up, w_up, w, b)


def _ffn_dact(name, dh, w_down4, g, v, w, b):
    last = T // TS - 1

    def body(dh_ref, dhn_ref, wd_ref, g_ref, gp_ref, gn_ref, v_ref, vn_ref, w_ref, b_ref, dg_ref, dv_ref, dw_ref, db_ref):
        i = pl.program_id(1)
        da = lax.dot_general(jnp.concatenate([dh_ref[...], dhn_ref[...]], axis=0), wd_ref[...], NT,
                             preferred_element_type=F32)
        row = lax.broadcasted_iota(jnp.int32, (TS + HALO, 1), 0)
        da = jnp.where(jnp.logical_or(i < last, row < TS), da, 0.0)
        gp = jnp.where(i > 0, gp_ref[...].astype(F32), 0.0)
        ext = jnp.concatenate([gp, g_ref[...].astype(F32), gn_ref[...].astype(F32)], axis=0)
        x2, x1, x0 = _causal_taps(ext)
        wv = w_ref[...]
        gc = ((x2 * wv[0:1] + x1 * wv[1:2]) + x0 * wv[2:3]) + b_ref[...]
        sg = _sigmoid(gc)
        vv = jnp.concatenate([v_ref[...].astype(F32), vn_ref[...].astype(F32)], axis=0)
        dv_ref[...] = (da[:TS] * (gc[:TS] * sg[:TS])).astype(BF16)
        dgc = (da * vv) * (sg * (1.0 + gc * (1.0 - sg)))
        n1, n2 = _anticausal_taps(dgc, TS)
        d0 = dgc[:TS]
        dg_ref[...] = ((d0 * wv[2:3] + n1 * wv[1:2]) + n2 * wv[0:1]).astype(BF16)
        part_w = jnp.concatenate([jnp.sum(d0 * x2[:TS], axis=0, keepdims=True),
                                  jnp.sum(d0 * x1[:TS], axis=0, keepdims=True),
                                  jnp.sum(d0 * x0[:TS], axis=0, keepdims=True)], axis=0)
        part_b = jnp.sum(d0, axis=0, keepdims=True)

        @pl.when(i == 0)
        def _():
            dw_ref[...] = part_w
            db_ref[...] = part_b

        @pl.when(i > 0)
        def _():
            dw_ref[...] += part_w
            db_ref[...] += part_b

    blk = pl.BlockSpec((None, TS, FF_BLK), lambda j, i: (j, i, 0))
    prev = pl.BlockSpec((None, HALO, FF_BLK), lambda j, i: (j, _prev_idx(i, TS), 0))
    nxt = pl.BlockSpec((None, HALO, FF_BLK), lambda j, i: (j, _next_idx(i, TS), 0))
    wspec = pl.BlockSpec((None, 3, FF_BLK), lambda j, i: (j, 0, 0))
    bspec = pl.BlockSpec((None, 1, FF_BLK), lambda j, i: (j, 0, 0))
    return _pallas(
        body, name=name, grid=(N_FF_BLK, T // TS),
        in_specs=[pl.BlockSpec((TS, D), lambda j, i: (i, 0)),
                  pl.BlockSpec((HALO, D), lambda j, i: (_next_idx(i, TS), 0)),
                  pl.BlockSpec((None, None, FF_BLK, D), lambda j, i: (0, j, 0, 0)),
                  blk, prev, nxt, blk, nxt, wspec, bspec],
        out_specs=[blk, blk, wspec, bspec],
        out_shape=[jax.ShapeDtypeStruct((N_FF_BLK, T, FF_BLK), BF16), jax.ShapeDtypeStruct((N_FF_BLK, T, FF_BLK), BF16),
                   jax.ShapeDtypeStruct((N_FF_BLK, 3, FF_BLK), F32), jax.ShapeDtypeStruct((N_FF_BLK, 1, FF_BLK), F32)],
        params=_params(("parallel", "arbitrary")))(dh, dh, w_down4, g, g, g, v, v, w, b)


def _rope_tables(pos, inv_freq):
    half = QK_ROPE // 2

    def body(p_ref, f_ref, c_ref, sa_ref, sb_ref):
        ang = p_ref[...].astype(F32) * f_ref[...]
        lane = lax.broadcasted_iota(jnp.int32, (T, 128), 1)
        c = jnp.cos(ang)
        s = jnp.sin(ang)
        c_ref[...] = jnp.where(lane < 2 * half, c, 0.0)
        sa_ref[...] = jnp.where(lane < half, -s, 0.0)
        sb_ref[...] = jnp.where(jnp.logical_and(lane >= half, lane < 2 * half), s, 0.0)

    return _pallas(
        body, name="rope_tables", in_specs=[VMEM_SPEC] * 2, out_specs=[VMEM_SPEC] * 3,
        out_shape=[jax.ShapeDtypeStruct((T, 128), F32)] * 3,
        params=pltpu.CompilerParams(vmem_limit_bytes=VMEM_LIMIT))(pos, inv_freq)


def _rotate(r, c, sa, sb, sign):
    return r * c + sign * (pltpu.roll(r, 96, 1) * sa + pltpu.roll(r, 32, 1) * sb)


def _q_up(cq, w_uq, tables):
    cos, sa, sb = tables

    def body(a_ref, b_ref, c_ref, sa_ref, sb_ref, o_ref):
        for h in range(N_HEADS):
            r = lax.dot_general(a_ref[...], b_ref[h], NN, preferred_element_type=F32)
            o_ref[h, :, :QK_NOPE] = r[:, :QK_NOPE].astype(BF16)
            o_ref[h, :, QK_NOPE:] = _rotate(r[:, QK_NOPE:], c_ref[...], sa_ref[...], sb_ref[...], 1.0).astype(BF16)

    tab = pl.BlockSpec((TS, 128), lambda i: (i, 0))
    return _pallas(
        body, name="q_up", grid=(T // TS,),
        in_specs=[pl.BlockSpec((TS, Q_LORA), lambda i: (i, 0)),
                  pl.BlockSpec((N_HEADS, Q_LORA, QK_PAD), lambda i: (0, 0, 0)), tab, tab, tab],
        out_specs=pl.BlockSpec((N_HEADS, TS, QK_PAD), lambda i: (0, i, 0)),
        out_shape=jax.ShapeDtypeStruct((N_HEADS, T, QK_PAD), BF16),
        params=_params(("parallel",)))(cq, w_uq, cos, sa, sb)


def _rope(name, x, tables, sign, out_dtype, reduce_groups=False):
    g, _, w = x.shape
    cos, sa, sb = tables

    def body(x_ref, c_ref, sa_ref, sb_ref, o_ref):
        xv = x_ref[...].astype(F32)
        if reduce_groups:
            acc = xv[0]
            for k in range(1, g):
                acc = acc + xv[k]
            xv = acc
        out = _rotate(xv[:, w - 128:], c_ref[...], sa_ref[...], sb_ref[...], sign)
        if w > 128:
            o_ref[:, :w - 128] = xv[:, :w - 128].astype(out_dtype)
        o_ref[:, w - 128:] = out.astype(out_dtype)

    tab = pl.BlockSpec((TM, 128), lambda h, i: (i, 0))
    if reduce_groups:
        x_spec = pl.BlockSpec((g, TM, w), lambda h, i: (0, i, 0))
        groups = 1
    else:
        x_spec = pl.BlockSpec((None, TM, w), lambda h, i: (h, i, 0))
        groups = g
    return _pallas(
        body, name=name, grid=(groups, T // TM), in_specs=[x_spec, tab, tab, tab],
        out_specs=pl.BlockSpec((None, TM, w), lambda h, i: (h, i, 0)),
        out_shape=jax.ShapeDtypeStruct((groups, T, w), out_dtype),
        params=_params(("parallel", "parallel")))(x, cos, sa, sb)


SCALE = (QK_NOPE + QK_ROPE) ** -0.5
LOG2E = 1.4426950408889634
SCALE2 = SCALE * LOG2E


def _diag_mask(transposed):
    shift = CHUNK.bit_length() - 1
    a = lax.broadcasted_iota(jnp.int32, (TQ, TQ), 0) >> shift
    b = lax.broadcasted_iota(jnp.int32, (TQ, TQ), 1) >> shift
    return (a <= b) if transposed else (b <= a)


def _as_row(col):
    return jnp.transpose(jnp.broadcast_to(col, (col.shape[0], 128)), (1, 0))[0:1]


def _keys(kn_ref, kr_ref, off):
    return jnp.concatenate([kn_ref[pl.ds(off, TQ), :], kr_ref[pl.ds(off, TQ), :]], axis=1)


def _attn_fwd(q, kn, kr, v):
    def body(q_ref, kn_ref, kr_ref, v_ref, o_ref, lse_ref):
        i = pl.program_id(1)
        qv = q_ref[...]

        def step(j, carry, masked):
            m, l, acc = carry
            off = pl.multiple_of(j * TQ, TQ)
            s = lax.dot_general(qv, _keys(kn_ref, kr_ref, off), NT, preferred_element_type=F32) * SCALE2
            if masked:
                s = jnp.where(_diag_mask(False), s, NEG_INF)
            m_new = jnp.maximum(m, jnp.max(s, axis=-1, keepdims=True))
            p = jnp.exp2(s - m_new)
            alpha = jnp.exp2(m - m_new)
            l = alpha * l + jnp.sum(p, axis=-1, keepdims=True)
            acc = alpha * acc + lax.dot_general(p.astype(BF16), v_ref[pl.ds(off, TQ), :], NN, preferred_element_type=F32)
            return m_new, l, acc

        init = (jnp.full((TQ, 1), NEG_INF, F32), jnp.zeros((TQ, 1), F32), jnp.zeros((TQ, V_HEAD), F32))
        carry = lax.fori_loop(0, i, lambda j, cr: step(j, cr, False), init)
        m, l, acc = step(i, carry, True)
        o_ref[...] = (acc / l).astype(BF16)
        lse_ref[...] = _as_row(m + jnp.log(l) * LOG2E)

    return _pallas(
        body, name="attn_fwd", grid=(N_HEADS, T // TQ),
        in_specs=[pl.BlockSpec((None, TQ, QK_PAD), lambda h, i: (h, i, 0)),
                  pl.BlockSpec((T, QK_NOPE), lambda h, i: (0, h)),
                  pl.BlockSpec((T, 128), lambda h, i: (0, 0)),
                  pl.BlockSpec((T, V_HEAD), lambda h, i: (0, h))],
        out_specs=[pl.BlockSpec((TQ, V_HEAD), lambda h, i: (i, h)), pl.BlockSpec((None, 1, TQ), lambda h, i: (h, 0, i))],
        out_shape=[jax.ShapeDtypeStruct((T, N_HEADS * V_HEAD), BF16), jax.ShapeDtypeStruct((N_HEADS, 1, T), F32)],
        params=_params(("parallel", "parallel")))(q, kn, kr, v)


def _attn_bwd(q, kn, kr, v, o, do, lse_row, tables):
    nq = T // TQ
    cos, sa, sb = tables

    def body(q_ref, kn_ref, kr_ref, v_ref, o_ref, do_ref, lse_ref, c_ref, sa_ref, sb_ref,
             dq_ref, dkn_ref, dkr_ref, dv_ref, dq_acc, dl_ref):
        j = pl.program_id(1)

        @pl.when(j == 0)
        def _():
            dq_acc[...] = jnp.zeros_like(dq_acc)
            for i in range(nq):
                rows = pl.ds(i * TQ, TQ)
                prod = do_ref[rows, :].astype(F32) * o_ref[rows, :].astype(F32)
                dl_ref[:, rows] = _as_row(jnp.sum(prod, axis=-1, keepdims=True))

        kk = jnp.concatenate([kn_ref[...], kr_ref[...]], axis=1)
        vv = v_ref[...]

        def step(i, carry, masked):
            dk, dv = carry
            off = pl.multiple_of(i * TQ, TQ)
            qi = q_ref[pl.ds(off, TQ), :]
            doi = do_ref[pl.ds(off, TQ), :]
            st = lax.dot_general(kk, qi, NT, preferred_element_type=F32) * SCALE2
            if masked:
                st = jnp.where(_diag_mask(True), st, NEG_INF)
            pt = jnp.exp2(st - lse_ref[:, pl.ds(off, TQ)])
            dv = dv + lax.dot_general(pt.astype(BF16), doi, NN, preferred_element_type=F32)
            dpt = lax.dot_general(vv, doi, NT, preferred_element_type=F32)
            dst = ((pt * (dpt - dl_ref[:, pl.ds(off, TQ)])) * SCALE).astype(BF16)
            dk = dk + lax.dot_general(dst, qi, NN, preferred_element_type=F32)
            dq_acc[pl.ds(off, TQ), :] += lax.dot_general(dst, kk, TN, preferred_element_type=F32)
            return dk, dv

        carry = step(j, (jnp.zeros((TQ, QK_PAD), F32), jnp.zeros((TQ, V_HEAD), F32)), True)
        dk, dv = lax.fori_loop(j + 1, nq, lambda i, cr: step(i, cr, False), carry)
        dkn_ref[...] = dk[:, :QK_NOPE].astype(BF16)
        dkr_ref[...] = dk[:, QK_NOPE:]
        dv_ref[...] = dv.astype(BF16)

        @pl.when(j == nq - 1)
        def _():
            dq = dq_acc[...]
            dq_ref[:, :QK_NOPE] = dq[:, :QK_NOPE].astype(BF16)
            dq_ref[:, QK_NOPE:] = _rotate(dq[:, QK_NOPE:], c_ref[...], sa_ref[...], sb_ref[...], -1.0).astype(BF16)

    row = pl.BlockSpec((None, 1, T), lambda h, j: (h, 0, 0))
    head = pl.BlockSpec((TQ, 128), lambda h, j: (j, h))
    whole = pl.BlockSpec((None, T, QK_PAD), lambda h, j: (h, 0, 0))
    tab = pl.BlockSpec((T, 128), lambda h, j: (0, 0))
    heads = pl.BlockSpec((T, V_HEAD), lambda h, j: (0, h))
    return _pallas(
        body, name="attn_bwd", grid=(N_HEADS, nq),
        in_specs=[whole, head, pl.BlockSpec((TQ, 128), lambda h, j: (j, 0)), head, heads, heads, row, tab, tab, tab],
        out_specs=[whole, head, pl.BlockSpec((None, TQ, 128), lambda h, j: (h, j, 0)), head],
        out_shape=[jax.ShapeDtypeStruct((N_HEADS, T, QK_PAD), BF16), jax.ShapeDtypeStruct((T, N_HEADS * QK_NOPE), BF16),
                   jax.ShapeDtypeStruct((N_HEADS, T, 128), F32), jax.ShapeDtypeStruct((T, N_HEADS * V_HEAD), BF16)],
        scratch_shapes=[pltpu.VMEM((T, QK_PAD), F32), pltpu.VMEM((1, T), F32)],
        params=_params(("parallel", "arbitrary")))(q, kn, kr, v, o, do, lse_row, cos, sa, sb)


def _ffn_gup(name, dg, dv, hf):
    def body(dg_ref, dv_ref, hf_ref, o_ref):
        j = pl.program_id(0)

        @pl.when(j < N_FF_BLK)
        def _():
            o_ref[...] = lax.dot_general(dg_ref[...], hf_ref[...], TN, preferred_element_type=F32).astype(BF16)

        @pl.when(j >= N_FF_BLK)
        def _():
            o_ref[...] = lax.dot_general(dv_ref[...], hf_ref[...], TN, preferred_element_type=F32).astype(BF16)

    return _pallas(
        body, name=name, grid=(N_DEV,),
        in_specs=[pl.BlockSpec((None, T, FF_BLK), lambda j: (jnp.minimum(j, N_FF_BLK - 1), 0, 0)),
                  pl.BlockSpec((None, T, FF_BLK), lambda j: (jnp.maximum(j - N_FF_BLK, 0), 0, 0)),
                  pl.BlockSpec((T, D), lambda j: (0, 0))],
        out_specs=pl.BlockSpec((None, FF_BLK, D), lambda j: (j, 0, 0)),
        out_shape=jax.ShapeDtypeStruct((N_DEV, FF_BLK, D), BF16), params=_params(("parallel",)))(dg, dv, hf)


def _ffn_layer_fwd(tag, h, gain, ex):
    hf = _rms_fwd(f"{tag}_norm", h, gain)
    g, v, act = _ffn_up_act(f"{tag}_up", hf, ex.need(f"ffn_w_up{tag[1]}", hf), ex.need(f"ffn_cw{tag[1]}", hf),
                            ex.need(f"ffn_cb{tag[1]}", hf))
    ex.at(f"{tag}_up", act)
    rows = pl.BlockSpec((TS, D), lambda i: (i, 0))
    out = _mm_sum(f"{tag}_down",
                  [(act, pl.BlockSpec((N_FF_BLK, TS, FF_BLK), lambda i: (0, i, 0)), ex.need(f"ffn_w_down{tag[1]}", act),
                    pl.BlockSpec((None, N_FF_BLK, FF_BLK, D), lambda i: (0, 0, 0, 0)), NN, 0)],
                  grid=(T // TS,), o_spec=rows, o_shape=(T, D), o_dtype=F32, add=h)
    ex.at(f"{tag}_down", out)
    return out, (hf, g, v, act)


def _ffn_layer_bwd(tag, h, gain, ex, saved, dh, dh_bf):
    hf, g, v, act = saved
    layer = tag[1]
    w_up, w_down4 = ex.need(f"ffn_w_up{layer}", dh_bf), ex.need(f"ffn_w_down{layer}", dh_bf)
    dg, dv, dcw, dcb = _ffn_dact(f"{tag}_dact", dh_bf, w_down4, g, v, ex.need(f"ffn_cw{layer}", dh_bf),
                                 ex.need(f"ffn_cb{layer}", dh_bf))
    ex.at(f"{tag}_dact", dg)
    g_down = _mm(f"{tag}_gdown", act, dh_bf, grid=(N_FF_BLK,),
                 a_spec=pl.BlockSpec((None, T, FF_BLK), lambda j: (j, 0, 0)),
                 b_spec=pl.BlockSpec((T, D), lambda j: (0, 0)),
                 o_spec=pl.BlockSpec((FF_BLK, D), lambda j: (j, 0)),
                 o_shape=(D_FF, D), o_dtype=BF16, dims=TN)
    g_up = _ffn_gup(f"{tag}_gup", dg, dv, hf)
    ex.grad("ffn_w_up", int(layer), g_up.reshape(1, N_DEV, FF_BLK, D))
    ex.grad("ffn_w_down", int(layer), g_down.reshape(1, N_DEV, D_FF // N_DEV, D))
    ex.at(f"{tag}_gup", g_up)
    part = pl.BlockSpec((N_FF_BLK, TR, FF_BLK), lambda i: (0, i, 0))
    dh_in, dh_in_bf, dgain = _mm_sum(
        f"{tag}_dhf",
        [(dg, part, w_up, pl.BlockSpec((None, N_FF_BLK, FF_BLK, D), lambda i: (0, 0, 0, 0)), NN, 0),
         (dv, part, w_up, pl.BlockSpec((None, N_FF_BLK, FF_BLK, D), lambda i: (0, 1, 0, 0)), NN, 0)],
        grid=(T // TR,), o_spec=pl.BlockSpec((TR, D), lambda i: (i, 0)), o_shape=(T, D), o_dtype=F32,
        norm_bwd=(h, [gain], dh))
    ex.at(f"{tag}_dhf", dh_in)
    return dh_in, dh_in_bf, dgain[0], dcw, dcb


def _local_step(x, pos, tgt, rep, ex):
    attn_norm, ffn_norm, final_norm = rep["attn_norm"], rep["ffn_norm"], rep["final_norm"]
    half = QK_ROPE // 2
    inv = 1.0 / (ROPE_THETA ** (jnp.arange(half, dtype=F32) / half))
    inv_freq = jnp.concatenate([inv, inv, jnp.zeros((128 - 2 * half,), F32)]).reshape(1, 128)
    tables = _rope_tables(pos, inv_freq)

    hn0 = _rms_fwd("l0_norm", x, attn_norm[0:1])
    w_in = ex.need("sc_w_in", hn0)
    ex.at("mixer_ready", hn0)
    zb, zc, zu, y = _mixer_in(hn0, w_in, ex.need("sc_conv_w", hn0))
    ex.at("l0_in", y)
    h1 = _mm_rows("l0_out", y, ex.need("sc_w_out", y), NN, F32, D, tn=512, add=x)
    ex.at("l0_out", h1)
    h2, ffn0 = _ffn_layer_fwd("f0", h1, ffn_norm[0:1], ex)

    hk, hn1 = _rms_fwd2("h2_norms", h2, rep["kv_in_norm"], attn_norm[1:2])
    ckv_raw, ckv, kr = _kv_down(hk, ex.need("w_dkv", hk), ex.need("w_kr", hk), rep["kv_latent_norm"], tables)
    kn, vv = _kv_up(ckv, ex.need("w_uk", ckv), ex.need("w_uv", ckv))

    cq_raw, cq = _down_norm("q_down", hn1, ex.need("w_dq", hn1), rep["q_latent_norm"])
    w_uq = ex.need("w_uq", cq)
    q = _q_up(cq, w_uq, tables)
    o, lse = _attn_fwd(q, kn, kr, vv)
    ex.at("attn_fwd", o)
    w_o = ex.need("w_o", o)
    h3 = _mm_rows("attn_out", o, w_o, NN, F32, D, tn=512, add=h2)
    h4, ffn1 = _ffn_layer_fwd("f1", h3, ffn_norm[1:2], ex)

    loss, dh4, dh4_bf, d_final = _final(h4, final_norm.reshape(1, D), tgt)

    dh3, dh3_bf, d_fn1, dcw1, dcb1 = _ffn_layer_bwd("f1", h3, ffn_norm[1:2], ex, ffn1, dh4, dh4_bf)
    ex.at("f1_bwd", dh3)

    do = _mm_rows("d_attn_out", dh3_bf, w_o, NT, BF16, N_HEADS * V_HEAD)
    ex.grad("w_o", None, _mm_wgrad("g_w_o", o, dh3_bf).reshape(1, N_DEV, D // N_DEV, D))
    dq_pre, dkn, dkr, dvv = _attn_bwd(q, kn, kr, vv, o, do, lse, tables)
    def rows_of(a):
        return a[None], pl.BlockSpec((1, TS, a.shape[1]), lambda i: (0, i, 0))

    def whole(wt):
        return wt[None], pl.BlockSpec((1,) + wt.shape, lambda i: (0, 0, 0))

    def row_blocks(d):
        return dict(grid=(T // TS,), o_spec=pl.BlockSpec((TS, d), lambda i: (i, 0)), o_shape=(T, d), o_dtype=F32)

    _, dcq_raw_bf, (d_qln,) = _mm_sum(
        "d_q_up", [(dq_pre, pl.BlockSpec((N_HEADS, TS, QK_PAD), lambda i: (0, i, 0)),
                    w_uq, pl.BlockSpec((N_HEADS, Q_LORA, QK_PAD), lambda i: (0, 0, 0)), NT, 0)],
        norm_bwd=(cq_raw, [rep["q_latent_norm"]], None), **row_blocks(Q_LORA))
    g_uq = _mm("g_w_uq", cq, dq_pre, grid=(N_HEADS,),
               a_spec=pl.BlockSpec((T, Q_LORA), lambda h: (0, 0)),
               b_spec=pl.BlockSpec((None, T, QK_PAD), lambda h: (h, 0, 0)),
               o_spec=pl.BlockSpec((None, Q_LORA, QK_PAD), lambda h: (h, 0, 0)),
               o_shape=(N_HEADS, Q_LORA, QK_PAD), o_dtype=BF16, dims=TN)
    ex.grad("w_uq", None, g_uq[:, :, :QK_NOPE + QK_ROPE].reshape(1, N_DEV, Q_LORA, QK_NOPE + QK_ROPE))
    ex.grad("w_dq", None, _mm_wgrad("g_w_dq", hn1, dcq_raw_bf).reshape(1, N_DEV, D // N_DEV, Q_LORA))

    _, dckv_raw_bf, (d_kvln,) = _mm_sum(
        "d_kv_up", [(*rows_of(dkn), *whole(ex.need("w_uk", dkn)), NT, 0),
                    (*rows_of(dvv), *whole(ex.need("w_uv", dvv)), NT, 0)],
        norm_bwd=(ckv_raw, [rep["kv_latent_norm"]], None), **row_blocks(KV_LORA))
    ex.grad("w_uk", None, _mm_wgrad("g_w_uk", ckv, dkn))
    ex.grad("w_uv", None, _mm_wgrad("g_w_uv", ckv, dvv))
    dkr_raw_bf = _rope("dk_rope", dkr, tables, -1.0, BF16, reduce_groups=True).reshape(T, 128)
    ex.grad("w_dkv", None, _mm_wgrad("g_w_dkv", hk, dckv_raw_bf).reshape(1, N_DEV, D // N_DEV, KV_LORA))
    ex.grad("w_kr", None, _mm_wgrad("g_w_kr", hk, dkr_raw_bf)[:, :QK_ROPE].reshape(1, N_DEV, D // N_DEV, QK_ROPE))

    dh2, dh2_bf, (d_an1, d_kvin) = _mm_sum(
        "d_h2", [(*rows_of(dcq_raw_bf), *whole(ex.need("w_dq", dcq_raw_bf)), NT, 0),
                 (*rows_of(dckv_raw_bf), *whole(ex.need("w_dkv", dckv_raw_bf)), NT, 1),
                 (*rows_of(dkr_raw_bf), *whole(ex.need("w_kr", dkr_raw_bf)), NT, 1)],
        norm_bwd=(h2, [attn_norm[1:2], rep["kv_in_norm"]], dh3), **row_blocks(D))
    ex.at("kv_bwd", dh2)

    dh1, dh1_bf, d_fn0, dcw0, dcb0 = _ffn_layer_bwd("f0", h1, ffn_norm[0:1], ex, ffn0, dh2, dh2_bf)
    ex.at("f0_bwd", dh1)

    ex.grad("sc_w_out", None, _mm_wgrad("g_sc_w_out", y, dh1_bf).reshape(1, N_DEV, D // N_DEV, D))
    dz, d_scw = _mixer_out_bwd(dh1_bf, ex.need("sc_w_out", dh1_bf), zb, zc, zu, ex.need("sc_conv_w", dh1_bf))
    g_in = _mm_wgrad("g_sc_w_in", hn0, dz)
    ex.grad("sc_w_in", None, g_in)
    ex.at("sc_bwd", g_in)
    ex.at("d_l0_in", g_in)
    grad_x, _, (d_an0,) = _mm_sum(
        "d_l0_in", [(*rows_of(dz), *whole(ex.need("sc_w_in", dz)), NT, 0)],
        norm_bwd=(x, [attn_norm[0:1]], dh1), **row_blocks(D))

    small = {
        "attn_norm": jnp.concatenate([d_an0, d_an1], axis=0),
        "ffn_norm": jnp.concatenate([d_fn0, d_fn1], axis=0),
        "final_norm": d_final.reshape(D),
        "kv_in_norm": d_kvin.reshape(D),
        "kv_latent_norm": d_kvln.reshape(KV_LORA),
        "q_latent_norm": d_qln,
        "ffn_conv_b": jnp.stack([dcb0, dcb1]).transpose(0, 2, 1, 3).reshape(2, D_FF),
        "sc_conv_w": d_scw,
        "ffn_conv_w": jnp.stack([dcw0, dcw1]).transpose(0, 2, 1, 3).reshape(2, 3, D_FF),
    }
    return loss, grad_x, small


def _place():
    return lax.axis_index("x"), lax.axis_index("y"), lax.axis_index("c")


def _peers():
    x, y, c = _place()
    return (x, y, 1 - c), [(1 - x, y), (x, 1 - y), (1 - x, 1 - y)]


def _window(ref, kind, dev):
    if kind == "blocked":
        return ref.at[:, dev]
    width = ref.shape[-1] // N_DEV
    return ref.at[:, pl.ds(pl.multiple_of(dev * width, 128), width)]


HBM_SPEC = pl.BlockSpec(memory_space=pltpu.HBM)
SEM_SPEC = pl.BlockSpec(memory_space=pltpu.SEMAPHORE)
EFFECT = pltpu.SideEffectType.DATAFLOW_SIDE_EFFECTING
TOKEN = jax.ShapeDtypeStruct((8, 128), F32)


def _hbm(a):
    return pltpu.with_memory_space_constraint(a, pltpu.HBM)


def _copies_start(name, jobs):
    nj = len(jobs)
    counts = [(len(srcs), len(lands)) for srcs, lands, _, _ in jobs]
    n_arr = sum(ns + nl for ns, nl in counts)

    def body(*refs):
        sems, token = refs[n_arr:n_arr + 2 * nj], refs[-1]
        at = 0
        for j, ((ns, nl), (_, _, ncopy, plan)) in enumerate(zip(counts, jobs)):
            copies = plan(refs[at:at + ns], refs[at + ns:at + ns + nl])
            assert len(copies) == ncopy
            for k, (sent, dst, to, _) in enumerate(copies):
                pltpu.make_async_remote_copy(src_ref=sent, dst_ref=dst, send_sem=sems[2 * j].at[k],
                                             recv_sem=sems[2 * j + 1].at[k], device_id=to, device_id_type=MESH).start()
            at += ns + nl
        token[...] = jnp.zeros_like(token)

    arrays = [a for srcs, lands, _, _ in jobs for a in list(srcs) + list(lands)]
    sem_shapes = [pltpu.SemaphoreType.DMA((ncopy,)) for _, _, ncopy, _ in jobs for _ in range(2)]
    outs = pl.pallas_call(
        body, name=name, in_specs=[HBM_SPEC] * n_arr,
        out_specs=[SEM_SPEC] * (2 * nj) + [HBM_SPEC] * n_arr + [VMEM_SPEC],
        out_shape=sem_shapes + [pltpu.HBM(a.shape, a.dtype) for a in arrays] + [TOKEN],
        input_output_aliases={i: 2 * nj + i for i in range(n_arr)},
        compiler_params=pltpu.CompilerParams(has_side_effects=EFFECT))(*[_hbm(a) for a in arrays])
    _Chain.last = outs[-1]
    flights, at = [], 2 * nj
    for j, (ns, nl) in enumerate(counts):
        flights.append((outs[2 * j], outs[2 * j + 1], list(outs[at:at + ns]), list(outs[at + ns:at + ns + nl])))
        at += ns + nl
    return flights


def _copies_wait(name, started, ncopy, plan):
    send, recv, srcs, lands = started
    ns, nl = len(srcs), len(lands)

    def body(*refs):
        send_ref, recv_ref, token = refs[ns + nl], refs[ns + nl + 1], refs[-1]
        copies = plan(refs[:ns], refs[ns:ns + nl])
        assert len(copies) == ncopy
        for k, (sent, _, to, landed) in enumerate(copies):
            cp = pltpu.make_async_remote_copy(src_ref=sent, dst_ref=landed, send_sem=send_ref.at[k],
                                              recv_sem=recv_ref.at[k], device_id=to, device_id_type=MESH)
            cp.wait_send()
            cp.wait_recv()
        token[...] = jnp.zeros_like(token)

    arrays = list(srcs) + list(lands)
    outs = pl.pallas_call(
        body, name=name, in_specs=[HBM_SPEC] * (ns + nl) + [SEM_SPEC] * 2 + [ANY_SPEC],
        out_specs=[HBM_SPEC] * (ns + nl) + [VMEM_SPEC], out_shape=[pltpu.HBM(a.shape, a.dtype) for a in arrays] + [TOKEN],
        input_output_aliases={i: i for i in range(ns + nl)},
        compiler_params=pltpu.CompilerParams(has_side_effects=EFFECT))(*arrays, send, recv, _Chain.last)
    _Chain.last = outs[-1]
    return list(outs[:ns]), list(outs[ns:-1])


def _plan_gather_chips(kinds):
    def plan(srcs, lands):
        x, y, c = _place()
        sibling, chips = _peers()
        out = []
        for t, kind in enumerate(kinds):
            mine = _window(lands[t], kind, 4 * x + 2 * y + c)
            out.append((srcs[t], mine, (x, y, c), mine))
            out.append((srcs[t], mine, sibling, _window(lands[t], kind, 4 * x + 2 * y + 1 - c)))
            for px, py in chips:
                out.append((srcs[t], mine, (px, py, c), _window(lands[t], kind, 4 * px + 2 * py + c)))
        return out
    return plan, 5 * len(kinds)


def _plan_gather_all(n):
    def plan(srcs, lands):
        x, y, c = _place()
        out = []
        for t in range(n):
            mine = lands[t].at[:, 4 * x + 2 * y + c]
            for m in range(N_DEV):
                px, py, pc = (1 - x if m & 4 else x), (1 - y if m & 2 else y), (1 - c if m & 1 else c)
                out.append((srcs[t], mine, (px, py, pc), lands[t].at[:, 4 * px + 2 * py + pc]))
        return out
    return plan, N_DEV * n


def _plan_gather_sibling(kinds):
    def plan(srcs, lands):
        _, _, c = _place()
        sibling, chips = _peers()
        out = []
        for t, kind in enumerate(kinds):
            for px, py in chips:
                w = _window(lands[t], kind, 4 * px + 2 * py + c)
                out.append((w, w, sibling, _window(lands[t], kind, 4 * px + 2 * py + 1 - c)))
        return out
    return plan, 3 * len(kinds)


def _plan_scatter_sibling(kinds):
    def plan(srcs, lands):
        _, _, c = _place()
        sibling, _ = _peers()
        out = []
        for t, kind in enumerate(kinds):
            for k in range(N_CHIP):
                out.append((_window(srcs[t], kind, 2 * k + 1 - c), lands[t].at[k], sibling, lands[t].at[k]))
        return out
    return plan, N_CHIP * len(kinds)


def _plan_scatter_chips(n):
    def plan(srcs, lands):
        x, y, c = _place()
        _, chips = _peers()
        out = []
        for t in range(n):
            for px, py in chips:
                out.append((srcs[t].at[2 * px + py], lands[t].at[2 * x + y], (px, py, c), lands[t].at[2 * px + py]))
        return out
    return plan, 3 * n


def _landing(shard, kind):
    if kind == "blocked":
        return lax.empty((shard.shape[0], N_DEV) + shard.shape[1:], shard.dtype)
    return lax.empty((shard.shape[0], N_DEV * shard.shape[1]), shard.dtype)


def _chip_sums(name, grads, kinds, recvs, c):
    n = len(grads)
    in_specs, out_specs, out_shape, args = [], [], [], []
    for gr, kind, rv in zip(grads, kinds, recvs):
        if kind == "blocked":
            rows, w = gr.shape[2], gr.shape[3]
            in_specs.append(pl.BlockSpec((None, None, rows, w), lambda k, cref: (0, 2 * k + cref[0], 0, 0)))
        else:
            rows, w = gr.shape[0], gr.shape[1] // N_DEV
            in_specs.append(pl.BlockSpec((rows, w), lambda k, cref: (0, 2 * k + cref[0])))
        blk = pl.BlockSpec((None, rows, w), lambda k, cref: (k, 0, 0))
        in_specs.append(blk)
        out_specs.append(blk)
        out_shape.append(jax.ShapeDtypeStruct((N_CHIP, rows, w), BF16))
        args += [gr, rv.reshape(N_CHIP, rows, w)]

    def body(*refs):
        for t in range(n):
            g_ref, r_ref, o_ref = refs[1 + 2 * t], refs[2 + 2 * t], refs[1 + 2 * n + t]
            o_ref[...] = (g_ref[...].astype(F32) + r_ref[...].astype(F32)).astype(BF16)

    return _pallas(body, name=name, n_prefetch=1, grid=(N_CHIP,), in_specs=in_specs, out_specs=out_specs,
                   out_shape=out_shape, params=_params(("parallel",)))(c, *args)


def _adamw_math(g, wv, mv, vv):
    m = ADAM_B1 * mv + (1.0 - ADAM_B1) * g
    v = ADAM_B2 * vv + (1.0 - ADAM_B2) * (g * g)
    m_hat = m / (1.0 - ADAM_B1 ** ADAM_STEP)
    v_hat = v / (1.0 - ADAM_B2 ** ADAM_STEP)
    delta = -ADAM_LR * (m_hat / (jnp.sqrt(v_hat) + ADAM_EPS) + ADAM_WD * wv)
    return delta, m, v


ADAM_STEPS = 2


def _adamw_group(name, items, chip_ids):
    n = len(items)
    in_specs, out_specs, out_shape, args, prevs = [], [], [], [chip_ids], []
    for own, recv, w3, m3, v3, layer, _ in items:
        nl, rows, w = w3.shape
        tr = rows // ADAM_STEPS
        assert tr % 16 == 0, (name, rows)
        in_specs += [pl.BlockSpec((None, tr, w), lambda i, ids, slot=slot: (ids[slot], i, 0)) for slot in range(4)]
        slab = pl.BlockSpec((None, tr, w), lambda i, ids, layer=layer: (layer, i, 0))
        in_specs += [slab] * 3
        out_specs += [slab] * 4
        out_shape += [jax.ShapeDtypeStruct((nl, rows, w), F32)] * 4
        args += [own, recv, recv, recv, w3, m3, v3]
    aliases = {}
    for t, item in enumerate(items):
        if item[6] is not None:
            for k in range(4):
                aliases[len(args) + k] = 4 * t + k
            in_specs += [ANY_SPEC] * 4
            args += list(item[6])
            prevs.append(t)
    n_in = 1 + 7 * n + 4 * len(prevs)

    def body(*refs):
        for t in range(n):
            own_ref, r1_ref, r2_ref, r3_ref, w_ref, m_ref, v_ref = refs[1 + 7 * t:8 + 7 * t]
            g_ref, d_ref, nm_ref, nv_ref = refs[n_in + 4 * t:n_in + 4 * t + 4]
            g = ((own_ref[...].astype(F32) + r1_ref[...].astype(F32)) + r2_ref[...].astype(F32)) + r3_ref[...].astype(F32)
            g_ref[...] = g
            d_ref[...], nm_ref[...], nv_ref[...] = _adamw_math(g, w_ref[...], m_ref[...], v_ref[...])

    outs = _pallas(body, name=name, n_prefetch=1, grid=(ADAM_STEPS,), in_specs=in_specs, out_specs=out_specs,
                   out_shape=out_shape, aliases=aliases, params=_params(("parallel",)))(*args)
    return [list(outs[4 * t:4 * t + 4]) for t in range(n)]


def _adamw_small(gathered, ws, ms, vs):
    n = len(gathered)
    full = [w is not None for w in ws]
    args = list(gathered)
    out_shape = []
    for t in range(n):
        shape = jax.ShapeDtypeStruct(gathered[t].shape[2:], F32)
        if full[t]:
            args += [ws[t], ms[t], vs[t]]
            out_shape += [shape] * 4
        else:
            out_shape += [shape]

    def body(*refs):
        i_in, i_out = n, len(args)
        for t in range(n):
            p_ref = refs[t]
            g = p_ref[0, 0]
            for k in range(1, N_DEV):
                g = g + p_ref[0, k]
            refs[i_out][...] = g
            if full[t]:
                w_ref, m_ref, v_ref = refs[i_in:i_in + 3]
                refs[i_out + 1][...], refs[i_out + 2][...], refs[i_out + 3][...] = _adamw_math(
                    g, w_ref[...], m_ref[...], v_ref[...])
                i_in += 3
                i_out += 4
            else:
                i_out += 1

    outs = _pallas(body, name="adamw_small", in_specs=[VMEM_SPEC] * len(args), out_specs=[VMEM_SPEC] * len(out_shape),
                   out_shape=out_shape, params=pltpu.CompilerParams(vmem_limit_bytes=VMEM_LIMIT))(*args)
    result, i = [], 0
    for t in range(n):
        k = 4 if full[t] else 1
        result.append(list(outs[i:i + k]))
        i += k
    return result


def _adamw_plain(name, gs, ws, ms, vs):
    n = len(gs)

    def body(*refs):
        for t in range(n):
            g_ref, w_ref, m_ref, v_ref = refs[4 * t:4 * t + 4]
            outs = refs[4 * n + 3 * t:4 * n + 3 * t + 3]
            outs[0][...], outs[1][...], outs[2][...] = _adamw_math(g_ref[...], w_ref[...], m_ref[...], v_ref[...])

    args, out_shape = [], []
    for g, w, m, v in zip(gs, ws, ms, vs):
        args += [g, w, m, v]
        out_shape += [jax.ShapeDtypeStruct(w.shape, F32)] * 3
    outs = _pallas(body, name=name, in_specs=[VMEM_SPEC] * len(args), out_specs=[VMEM_SPEC] * len(out_shape),
                   out_shape=out_shape, params=pltpu.CompilerParams(vmem_limit_bytes=VMEM_LIMIT))(*args)
    return [list(outs[3 * t:3 * t + 3]) for t in range(n)]


KIND = {"sc_w_in": "cols", "sc_w_out": "blocked", "w_dkv": "blocked", "w_kr": "blocked", "w_uk": "cols", "w_uv": "cols",
        "w_dq": "blocked", "w_uq": "blocked", "w_o": "blocked", "ffn_w_up": "blocked", "ffn_w_down": "blocked",
        "conv": "blocked"}
GATHER_GROUPS = (("mixer", ("sc_w_in", "sc_w_out", "conv")),
                 ("up0", ("ffn_w_up0",)),
                 ("down0", ("ffn_w_down0",)),
                 ("attn", ("w_dkv", "w_kr", "w_uk", "w_uv", "w_dq", "w_uq", "w_o")),
                 ("ffn1", ("ffn_w_up1", "ffn_w_down1")))
SCATTER_GROUPS = (("ffn1", (("ffn_w_up", 1), ("ffn_w_down", 1))),
                  ("attn", (("w_o", None), ("w_uq", None), ("w_dq", None), ("w_uk", None), ("w_uv", None),
                            ("w_dkv", None), ("w_kr", None))),
                  ("ffn0", (("ffn_w_up", 0), ("ffn_w_down", 0))),
                  ("mixer", (("sc_w_out", None), ("sc_w_in", None))))
SCHEDULE = {
    "begin": (("gather_start", "mixer"),),
    "mixer_ready": (("gather_start", "up0"),),
    "l0_out": (("gather_forward", "up0"), ("gather_start", "down0")),
    "f0_up": (("gather_forward", "down0"), ("gather_start", "attn")),
    "f0_down": (("gather_forward", "attn"), ("gather_start", "ffn1")),
    "attn_fwd": (("gather_forward", "ffn1"),),
    "f1_gup": (("scatter_sibling", "ffn1"),),
    "f1_dhf": (("scatter_chips", "ffn1"),),
    "kv_bwd": (("scatter_sibling", "attn"), ("scatter_done", "ffn1")),
    "f0_dact": (("scatter_chips", "attn"),),
    "f0_gup": (("scatter_sibling", "ffn0"),),
    "f0_dhf": (("scatter_chips", "ffn0"),),
    "f0_bwd": (("scatter_done", "attn"),),
    "sc_bwd": (("scatter_sibling", "mixer"),),
    "d_l0_in": (("scatter_chips", "mixer"),),
}
FINISH = (("scatter_done", "ffn0"), ("scatter_done", "mixer"))
STAGES = {"gather_start": 1, "gather_forward": 2, "gather_done": 3,
          "scatter_sibling": 1, "scatter_chips": 2, "scatter_done": 3}
SMALL_W_ROWS = 24


def _pack(arrays, rows):
    flat = jnp.concatenate([a.reshape(-1).astype(F32) for a in arrays])
    return jnp.pad(flat, (0, rows * 128 - flat.shape[0])).reshape(rows, 128)


def _stored(name, a):
    return jnp.swapaxes(a, -1, -2) if name == "ffn_w_up" else a


def _base(name):
    if name.startswith("ffn_w_") and name[-1] in "01":
        return name[:-1], int(name[-1])
    return name, None


class _Exchange:
    def __init__(self, wts, mom, var, ffn_conv_b):
        self.wts, self.mom, self.var = wts, mom, var
        x, y, c = _place()
        self.me = 4 * x + 2 * y + c
        self.c_arr = jnp.reshape(c, (1,)).astype(jnp.int32)
        chip = 2 * x + y
        self.chip_ids = jnp.stack([chip, chip ^ 1, chip ^ 2, chip ^ 3]).astype(jnp.int32)
        self.ready = {"ffn_cb0": ffn_conv_b.reshape(2, N_FF_BLK, 1, FF_BLK)[0],
                      "ffn_cb1": ffn_conv_b.reshape(2, N_FF_BLK, 1, FF_BLK)[1]}
        self.gathers, self.group_of = {}, {}
        self.grads, self.scatters, self.results, self.queue = {}, {}, {}, []
        for gname, names in GATHER_GROUPS:
            self.gathers[gname] = dict(stage=0, names=names, kinds=[KIND[_base(nm)[0]] for nm in names])
            for nm in names:
                self.group_of[nm] = gname
        for nm in ("sc_conv_w", "ffn_cw0", "ffn_cw1"):
            self.group_of[nm] = "mixer"
        self.at("begin", None)

    def _shard(self, name):
        if name == "conv":
            return _pack([self.wts["sc_conv_w"], self.wts["ffn_conv_w"]], SMALL_W_ROWS).reshape(1, SMALL_W_ROWS, 128)
        base, layer = _base(name)
        a = _stored(base, self.wts[base])
        if layer is not None:
            a = a[layer:layer + 1]
        if KIND[base] == "cols":
            return a.reshape(a.shape[-2], a.shape[-1]).astype(BF16)
        return a.reshape((-1,) + a.shape[-2:]).astype(BF16)

    def _start(self, name, srcs, lands, ncopy, plan, st):
        self.queue.append((name, (srcs, lands, ncopy, plan), st))

    def _flush(self):
        if self.queue:
            flights = _copies_start("__".join(name for name, _, _ in self.queue), [job for _, job, _ in self.queue])
            for (_, _, st), flight in zip(self.queue, flights):
                st["flight"] = flight
            self.queue = []

    def _flight(self, st):
        self._flush()
        return st["flight"]

    def _gather_to(self, gname, stage, after):
        st = self.gathers[gname]
        if st["stage"] < 1 <= stage:
            shards = [self._shard(nm) for nm in st["names"]]
            lands = [_landing(s, kind) for s, kind in zip(shards, st["kinds"])]
            plan, ncopy = _plan_gather_chips(st["kinds"])
            self._start(f"ag_{gname}_chips", shards, lands, ncopy, plan, st)
            st["stage"] = 1
        if st["stage"] < 2 <= stage:
            plan, ncopy = _plan_gather_chips(st["kinds"])
            _, lands = _copies_wait(f"ag_{gname}_chips_wait", self._flight(st), ncopy, plan)
            plan, ncopy = _plan_gather_sibling(st["kinds"])
            self._start(f"ag_{gname}_sibling", [], lands, ncopy, plan, st)
            st["stage"] = 2
        if st["stage"] < 3 <= stage:
            plan, ncopy = _plan_gather_sibling(st["kinds"])
            _, lands = _copies_wait(f"ag_{gname}_sibling_wait", self._flight(st), ncopy, plan)
            for nm, land in zip(st["names"], lands):
                self._arrived(nm, land)
            st["stage"] = 3

    def _arrived(self, name, land):
        if name == "conv":
            conv = land.reshape(N_DEV, SMALL_W_ROWS * 128)
            self.ready["sc_conv_w"] = conv[:, :3 * 128].reshape(N_DEV, 3, 128).transpose(1, 0, 2).reshape(3, D)
            fcw = conv[:, 3 * 128:3 * 128 + 6 * 352].reshape(N_DEV, 2, 3, 352).transpose(1, 2, 0, 3)
            fcw = fcw.reshape(2, 3, N_FF_BLK, FF_BLK).transpose(0, 2, 1, 3)
            self.ready["ffn_cw0"], self.ready["ffn_cw1"] = fcw[0], fcw[1]
        elif name in ("sc_w_in", "w_uk", "w_uv") or name.startswith("ffn_w_up"):
            self.ready[name] = land
        elif name.startswith("ffn_w_down"):
            self.ready[name] = land.reshape(1, N_FF_BLK, FF_BLK, D)
        elif name == "w_kr":
            self.ready[name] = jnp.pad(land.reshape(D, QK_ROPE), ((0, 0), (0, 128 - QK_ROPE)))
        elif name == "w_uq":
            self.ready[name] = jnp.pad(land.reshape(N_HEADS, Q_LORA, QK_NOPE + QK_ROPE),
                                       ((0, 0), (0, 0), (0, QK_PAD - QK_NOPE - QK_ROPE)))
        else:
            self.ready[name] = land.reshape(D, land.shape[-1])

    def need(self, name, after):
        if name not in self.ready:
            self._gather_to(self.group_of[name], 3, after)
            self._flush()
        return self.ready[name]

    def grad(self, name, layer, array):
        self.grads[(name, layer)] = array

    def _scatter_to(self, gname, stage, after):
        keys = dict(SCATTER_GROUPS)[gname]
        st = self.scatters.setdefault(gname, dict(stage=0))
        kinds = [KIND[nm] for nm, _ in keys]
        if st["stage"] < 1 <= stage:
            grads = [self.grads[key] for key in keys]
            lands = []
            for gr, kind in zip(grads, kinds):
                shard = (gr.shape[0],) + gr.shape[2:] if kind == "blocked" else (gr.shape[0], gr.shape[1] // N_DEV)
                lands.append(lax.empty((N_CHIP,) + shard, BF16))
            plan, ncopy = _plan_scatter_sibling(kinds)
            self._start(f"rs_{gname}_sibling", grads, lands, ncopy, plan, st)
            st["stage"] = 1
        if st["stage"] < 2 <= stage:
            plan, ncopy = _plan_scatter_sibling(kinds)
            grads, recvs = _copies_wait(f"rs_{gname}_sibling_wait", self._flight(st), ncopy, plan)
            sums = _chip_sums(f"rs_{gname}_sums", grads, kinds, recvs, self.c_arr)
            lands = [lax.empty(s.shape, BF16) for s in sums]
            plan, ncopy = _plan_scatter_chips(len(sums))
            self._start(f"rs_{gname}_chips", sums, lands, ncopy, plan, st)
            st["stage"] = 2
        if st["stage"] < 3 <= stage:
            plan, ncopy = _plan_scatter_chips(len(keys))
            sums, recvs = _copies_wait(f"rs_{gname}_chips_wait", self._flight(st), ncopy, plan)
            items = []
            for (nm, layer), own, rv in zip(keys, sums, recvs):
                nl = 1 if layer is None else 2
                rows, w = own.shape[1], own.shape[2]
                w3, m3, v3 = (_stored(nm, src[nm]).reshape(nl, rows, w) for src in (self.wts, self.mom, self.var))
                items.append((own, rv, w3, m3, v3, 0 if layer is None else layer, self.results.get(nm)))
            outs = _adamw_group(f"adamw_{gname}", items, self.chip_ids)
            for (nm, _), out in zip(keys, outs):
                self.results[nm] = out
            st["stage"] = 3

    def at(self, place, after):
        for action, gname in SCHEDULE.get(place, ()):
            self._advance(action, gname, after)
        self._flush()

    def _advance(self, action, gname, after):
        if action.startswith("gather"):
            self._gather_to(gname, STAGES[action], after)
        else:
            self._scatter_to(gname, STAGES[action], after)

    def finish(self, after):
        for action, gname in FINISH:
            self._advance(action, gname, after)
        for gname, _ in SCATTER_GROUPS:
            self._scatter_to(gname, 3, after)
        return {nm: [_stored(nm, o.reshape(_stored(nm, self.wts[nm]).shape)) for o in outs]
                for nm, outs in self.results.items()}


REPLICATED = ("attn_norm", "ffn_norm", "final_norm", "kv_in_norm", "kv_latent_norm", "q_latent_norm", "ffn_conv_b")
WEIGHTS = ("attn_norm", "ffn_norm", "final_norm", "sc_w_in", "sc_conv_w", "sc_w_out", "kv_in_norm", "w_dkv",
           "kv_latent_norm", "w_kr", "w_uk", "w_uv", "w_dq", "q_latent_norm", "w_uq", "w_o", "ffn_w_up", "ffn_conv_w",
           "ffn_conv_b", "ffn_w_down")


def kernel(x, positions, attn_norm, ffn_norm, final_norm, sc_w_in, sc_conv_w, sc_w_out, kv_in_norm, w_dkv, kv_latent_norm, w_kr, w_uk, w_uv, w_dq, q_latent_norm, w_uq, w_o, ffn_w_up, ffn_conv_w, ffn_conv_b, ffn_w_down, loss_target, m_attn_norm, m_ffn_norm, m_final_norm, m_sc_w_in, m_sc_conv_w, m_sc_w_out, m_kv_in_norm, m_w_dkv, m_kv_latent_norm, m_w_kr, m_w_uk, m_w_uv, m_w_dq, m_q_latent_norm, m_w_uq, m_w_o, m_ffn_w_up, m_ffn_conv_w, m_ffn_conv_b, m_ffn_w_down, v_attn_norm, v_ffn_norm, v_final_norm, v_sc_w_in, v_sc_conv_w, v_sc_w_out, v_kv_in_norm, v_w_dkv, v_kv_latent_norm, v_w_kr, v_w_uk, v_w_uv, v_w_dq, v_q_latent_norm, v_w_uq, v_w_o, v_ffn_w_up, v_ffn_conv_w, v_ffn_conv_b, v_ffn_w_down):
    wts = dict(attn_norm=attn_norm, ffn_norm=ffn_norm, final_norm=final_norm, sc_w_in=sc_w_in, sc_conv_w=sc_conv_w,
               sc_w_out=sc_w_out, kv_in_norm=kv_in_norm, w_dkv=w_dkv, kv_latent_norm=kv_latent_norm, w_kr=w_kr,
               w_uk=w_uk, w_uv=w_uv, w_dq=w_dq, q_latent_norm=q_latent_norm, w_uq=w_uq, w_o=w_o, ffn_w_up=ffn_w_up,
               ffn_conv_w=ffn_conv_w, ffn_conv_b=ffn_conv_b, ffn_w_down=ffn_w_down)
    mom = dict(attn_norm=m_attn_norm, ffn_norm=m_ffn_norm, final_norm=m_final_norm, sc_w_in=m_sc_w_in,
               sc_conv_w=m_sc_conv_w, sc_w_out=m_sc_w_out, kv_in_norm=m_kv_in_norm, w_dkv=m_w_dkv,
               kv_latent_norm=m_kv_latent_norm, w_kr=m_w_kr, w_uk=m_w_uk, w_uv=m_w_uv, w_dq=m_w_dq,
               q_latent_norm=m_q_latent_norm, w_uq=m_w_uq, w_o=m_w_o, ffn_w_up=m_ffn_w_up, ffn_conv_w=m_ffn_conv_w,
               ffn_conv_b=m_ffn_conv_b, ffn_w_down=m_ffn_w_down)
    var = dict(attn_norm=v_attn_norm, ffn_norm=v_ffn_norm, final_norm=v_final_norm, sc_w_in=v_sc_w_in,
               sc_conv_w=v_sc_conv_w, sc_w_out=v_sc_w_out, kv_in_norm=v_kv_in_norm, w_dkv=v_w_dkv,
               kv_latent_norm=v_kv_latent_norm, w_kr=v_w_kr, w_uk=v_w_uk, w_uv=v_w_uv, w_dq=v_w_dq,
               q_latent_norm=v_q_latent_norm, w_uq=v_w_uq, w_o=v_w_o, ffn_w_up=v_ffn_w_up, ffn_conv_w=v_ffn_conv_w,
               ffn_conv_b=v_ffn_conv_b, ffn_w_down=v_ffn_w_down)
    xi, yi, ci = _place()
    me = 4 * xi + 2 * yi + ci
    _Chain.last = None

    ex = _Exchange(wts, mom, var, ffn_conv_b)
    rep = {
        "attn_norm": attn_norm, "ffn_norm": ffn_norm, "final_norm": final_norm,
        "kv_in_norm": kv_in_norm.reshape(1, D), "kv_latent_norm": kv_latent_norm.reshape(1, KV_LORA),
        "q_latent_norm": q_latent_norm.reshape(1, Q_LORA),
    }
    loss, grad_x, small = _local_step(x.reshape(T, D), positions.reshape(T, 1), loss_target.reshape(T, D), rep, ex)

    def rows_of(a):
        return a.reshape(-1, a.shape[-1])

    small_order = list(REPLICATED) + ["sc_conv_w", "ffn_conv_w"]
    shards = [loss.reshape(1, 1, 128)] + [rows_of(small[nm])[None] for nm in small_order]
    plan, ncopy = _plan_gather_all(len(shards))
    flight, = _copies_start("ag_small", [(shards, [lax.empty((1, N_DEV) + s.shape[1:], F32) for s in shards], ncopy, plan)])
    results = ex.finish(grad_x)
    _, gathered = _copies_wait("ag_small_wait", flight, ncopy, plan)
    params = [[None] + [rows_of(src[nm]) for nm in REPLICATED] + [None, None] for src in (wts, mom, var)]
    summed = _adamw_small(gathered, *params)
    loss_total = summed[0][0][0, 0]
    for nm, vals in zip(REPLICATED, summed[1:1 + len(REPLICATED)]):
        results[nm] = [a.reshape(wts[nm].shape) for a in vals]
    g_scw = lax.dynamic_slice(summed[-2][0], (0, me * 128), (3, 128))
    g_fcw = lax.dynamic_slice(summed[-1][0], (0, me * 352), (6, 352))
    conv = _adamw_plain("adamw_conv", [g_scw, g_fcw], *[[rows_of(src["sc_conv_w"]), rows_of(src["ffn_conv_w"])]
                                                        for src in (wts, mom, var)])
    for nm, g_own, vals in zip(("sc_conv_w", "ffn_conv_w"), (g_scw, g_fcw), conv):
        results[nm] = [a.reshape(wts[nm].shape) for a in [g_own] + vals]

    outs = [loss_total, grad_x.reshape(1, T, D)]
    for slot in range(4):
        outs.extend(results[nm][slot] for nm in WEIGHTS)
    return tuple(outs)
```

```python
import jax
import jax.numpy as jnp
from jax import lax
from jax.experimental import pallas as pl
from jax.experimental.pallas import tpu as pltpu

F32 = jnp.float32
BF16 = jnp.bfloat16

T = 2048
D = 1024
N_HEADS = 8
QK_NOPE = 128
QK_ROPE = 64
V_HEAD = 128
Q_LORA = 384
KV_LORA = 256
D_FF = 2816
CHUNK = 64
ROPE_THETA = 10000.0
EPS = 1e-6
NEG_INF = -1e30
ADAM_LR = 0.001
ADAM_B1 = 0.9
ADAM_B2 = 0.999
ADAM_EPS = 1e-08
ADAM_WD = 0.01
ADAM_STEP = 10

N_DEV = 8
N_CHIP = 4
FF_BLK = D_FF * 2 // N_DEV
N_FF_BLK = D_FF // FF_BLK
QK_PAD = 256
HALO = 16

TM = 1024
TS = 512
TR = 256
TQ = 512
VMEM_LIMIT = 56 * 1024 * 1024

NN = (((1,), (0,)), ((), ()))
NT = (((1,), (1,)), ((), ()))
TN = (((0,), (0,)), ((), ()))
MESH = pl.DeviceIdType.MESH


def _params(sem):
    return pltpu.CompilerParams(dimension_semantics=sem, vmem_limit_bytes=VMEM_LIMIT)


ANY_SPEC = pl.BlockSpec(memory_space=pl.ANY)
VMEM_SPEC = pl.BlockSpec(memory_space=pltpu.VMEM)


class _Chain:
    last = None


def _pallas(body, *, name, in_specs, out_specs, out_shape, grid=(), scratch_shapes=(), n_prefetch=0, aliases=None,
            params=None):
    def run(*args):
        after = _Chain.last
        n_lead = len(args)
        specs, operands, fn = list(in_specs), list(args), body
        if after is not None:
            def fn(*refs):
                return body(*refs[:n_lead], *refs[n_lead + 1:])
            specs.append(ANY_SPEC)
            operands.append(after)
        kw = dict(name=name, out_shape=out_shape, input_output_aliases=aliases or {})
        if params is not None:
            kw["compiler_params"] = params
        if n_prefetch:
            kw["grid_spec"] = pltpu.PrefetchScalarGridSpec(
                num_scalar_prefetch=n_prefetch, grid=grid, in_specs=specs, out_specs=out_specs,
                scratch_shapes=scratch_shapes)
        else:
            kw.update(grid=grid, in_specs=specs, out_specs=out_specs, scratch_shapes=scratch_shapes)
        outs = pl.pallas_call(fn, **kw)(*operands)
        _Chain.last = outs[0] if isinstance(outs, (list, tuple)) else outs
        return outs
    return run


def _mm(name, a, b, *, grid, a_spec, b_spec, o_spec, o_shape, o_dtype, dims, k_axis=None, acc_shape=None,
        add=None, add_spec=None):
    nk = grid[k_axis] if k_axis is not None else 1
    has_add = add is not None

    def body(*refs):
        a_ref, b_ref = refs[0], refs[1]
        p = 2
        add_ref = None
        if has_add:
            add_ref = refs[p]
            p += 1
        o_ref = refs[p]
        p += 1
        r = lax.dot_general(a_ref[...].astype(BF16), b_ref[...].astype(BF16), dims, preferred_element_type=F32)
        if k_axis is None:
            if has_add:
                r = r + add_ref[...].astype(F32)
            o_ref[...] = r.astype(o_dtype)
        else:
            acc = refs[p]
            k = pl.program_id(k_axis)

            @pl.when(k == 0)
            def _():
                acc[...] = r

            @pl.when(k > 0)
            def _():
                acc[...] += r

            @pl.when(k == nk - 1)
            def _():
                t = acc[...]
                if has_add:
                    t = t + add_ref[...].astype(F32)
                o_ref[...] = t.astype(o_dtype)

    in_specs = [a_spec, b_spec]
    args = [a, b]
    if has_add:
        in_specs.append(add_spec if add_spec is not None else o_spec)
        args.append(add)
    sem = tuple("arbitrary" if ax == k_axis else "parallel" for ax in range(len(grid)))
    scratch = [pltpu.VMEM(acc_shape, F32)] if k_axis is not None else []
    return _pallas(body, name=name, grid=grid, in_specs=in_specs, out_specs=o_spec,
                   out_shape=jax.ShapeDtypeStruct(o_shape, o_dtype), scratch_shapes=scratch, params=_params(sem))(*args)


def _mm_sum(name, parts, *, grid, o_spec, o_shape, o_dtype, add=None, norm_bwd=None):
    has_add = add is not None
    np_ = len(parts)
    nn = 1 if norm_bwd is None else len(norm_bwd[1])
    has_res = norm_bwd is not None and norm_bwd[2] is not None

    def body(*refs):
        accs = [None] * nn
        for p, (_, _, _, _, dims, n) in enumerate(parts):
            a_ref, b_ref = refs[2 * p], refs[2 * p + 1]
            for k in range(a_ref.shape[0]):
                r = lax.dot_general(a_ref[k], b_ref[k], dims, preferred_element_type=F32)
                accs[n] = r if accs[n] is None else accs[n] + r
        if norm_bwd is None:
            acc = accs[0]
            if has_add:
                acc = acc + refs[2 * np_][...]
            refs[-1][...] = acc.astype(o_dtype)
            return
        x_ref, g_refs = refs[2 * np_], refs[2 * np_ + 1:2 * np_ + 1 + nn]
        dx_ref, dxb_ref, dg_refs = refs[-2 - nn], refs[-1 - nn], refs[-nn:]
        xv = x_ref[...]
        r = lax.rsqrt(jnp.mean(xv * xv, axis=-1, keepdims=True) + EPS)
        xn = xv * r
        dx = refs[2 * np_ + 1 + nn][...] if has_res else None
        sums = []
        for acc, g_ref in zip(accs, g_refs):
            gdy = acc * g_ref[...]
            t = r * (gdy - xn * jnp.mean(gdy * xn, axis=-1, keepdims=True))
            dx = t if dx is None else dx + t
            sums.append(jnp.sum(acc * xn, axis=0, keepdims=True))
        dx_ref[...] = dx
        dxb_ref[...] = dx.astype(BF16)

        @pl.when(pl.program_id(0) == 0)
        def _():
            for dg_ref, part in zip(dg_refs, sums):
                dg_ref[...] = part

        @pl.when(pl.program_id(0) > 0)
        def _():
            for dg_ref, part in zip(dg_refs, sums):
                dg_ref[...] += part

    in_specs, args = [], []
    for a, a_spec, b, b_spec, _, _ in parts:
        in_specs += [a_spec, b_spec]
        args += [a, b]
    if norm_bwd is None:
        if has_add:
            in_specs.append(o_spec)
            args.append(add)
        return _pallas(body, name=name, grid=grid, in_specs=in_specs, out_specs=o_spec,
                       out_shape=jax.ShapeDtypeStruct(o_shape, o_dtype),
                       params=_params(("parallel",) * len(grid)))(*args)
    x, gains, dres = norm_bwd
    vec = pl.BlockSpec((1, o_shape[1]), lambda i: (0, 0))
    in_specs += [o_spec] + [vec] * nn + ([o_spec] if has_res else [])
    args += [x] + list(gains) + ([dres] if has_res else [])
    outs = _pallas(body, name=name, grid=grid, in_specs=in_specs, out_specs=[o_spec, o_spec] + [vec] * nn,
                   out_shape=[jax.ShapeDtypeStruct(o_shape, F32), jax.ShapeDtypeStruct(o_shape, BF16)]
                   + [jax.ShapeDtypeStruct((1, o_shape[1]), F32)] * nn,
                   params=_params(("arbitrary",)))(*args)
    return outs[0], outs[1], list(outs[2:])


def _mm_rows(name, a, b, dims, o_dtype, n_out, *, tn=None, add=None):
    k = a.shape[1]
    tn = n_out if tn is None else tn
    if dims == NN:
        b_spec = pl.BlockSpec((k, tn), lambda n, i: (0, n))
    else:
        b_spec = pl.BlockSpec((tn, k), lambda n, i: (n, 0))
    return _mm(name, a, b, grid=(n_out // tn, T // TM),
               a_spec=pl.BlockSpec((TM, k), lambda n, i: (i, 0)), b_spec=b_spec,
               o_spec=pl.BlockSpec((TM, tn), lambda n, i: (i, n)), o_shape=(T, n_out), o_dtype=o_dtype,
               dims=dims, add=add)


def _mm_wgrad(name, a, b, *, tn=512):
    k, n = a.shape[1], b.shape[1]
    tn = min(tn, n)
    return _mm(name, a, b, grid=(n // tn,),
               a_spec=pl.BlockSpec((T, k), lambda j: (0, 0)), b_spec=pl.BlockSpec((T, tn), lambda j: (0, j)),
               o_spec=pl.BlockSpec((k, tn), lambda j: (0, j)), o_shape=(k, n), o_dtype=BF16, dims=TN)


def _rms_fwd(name, x, g):
    d = x.shape[1]

    def body(x_ref, g_ref, o_ref):
        xv = x_ref[...]
        r = lax.rsqrt(jnp.mean(xv * xv, axis=-1, keepdims=True) + EPS)
        o_ref[...] = ((xv * r) * g_ref[...]).astype(BF16)

    return _pallas(
        body, name=name, grid=(T // TM,),
        in_specs=[pl.BlockSpec((TM, d), lambda i: (i, 0)), pl.BlockSpec((1, d), lambda i: (0, 0))],
        out_specs=pl.BlockSpec((TM, d), lambda i: (i, 0)),
        out_shape=jax.ShapeDtypeStruct((T, d), BF16), params=_params(("parallel",)))(x, g)


def _rows_call(name, body, row_ins, whole_ins, outs):
    in_specs = [pl.BlockSpec((TM, a.shape[1]), lambda i: (i, 0)) for a in row_ins]
    in_specs += [pl.BlockSpec(a.shape, lambda i: (0, 0)) for a in whole_ins]
    return _pallas(
        body, name=name, grid=(T // TM,), in_specs=in_specs,
        out_specs=[pl.BlockSpec((TM, d), lambda i: (i, 0)) for d, _ in outs],
        out_shape=[jax.ShapeDtypeStruct((T, d), dt) for d, dt in outs],
        params=_params(("parallel",)))(*row_ins, *whole_ins)


def _rms(xv, g):
    return (xv * lax.rsqrt(jnp.mean(xv * xv, axis=-1, keepdims=True) + EPS)) * g


def _rms_fwd2(name, x, g1, g2):
    d = x.shape[1]

    def body(x_ref, g1_ref, g2_ref, o1_ref, o2_ref):
        xv = x_ref[...]
        xn = xv * lax.rsqrt(jnp.mean(xv * xv, axis=-1, keepdims=True) + EPS)
        o1_ref[...] = (xn * g1_ref[...]).astype(BF16)
        o2_ref[...] = (xn * g2_ref[...]).astype(BF16)

    return _rows_call(name, body, [x], [g1, g2], [(d, BF16), (d, BF16)])


def _down_norm(name, a, w, g):
    n = w.shape[1]

    def body(a_ref, w_ref, g_ref, raw_ref, o_ref):
        raw = lax.dot_general(a_ref[...], w_ref[...], NN, preferred_element_type=F32)
        raw_ref[...] = raw
        o_ref[...] = _rms(raw, g_ref[...]).astype(BF16)

    return _rows_call(name, body, [a], [w, g], [(n, F32), (n, BF16)])


def _kv_down(hk, w_dkv, w_kr, g, tables):
    def body(a_ref, c_ref, sa_ref, sb_ref, wd_ref, wr_ref, g_ref, raw_ref, ckv_ref, kr_ref):
        av = a_ref[...]
        raw = lax.dot_general(av, wd_ref[...], NN, preferred_element_type=F32)
        raw_ref[...] = raw
        ckv_ref[...] = _rms(raw, g_ref[...]).astype(BF16)
        kr = lax.dot_general(av, wr_ref[...], NN, preferred_element_type=F32)
        kr_ref[...] = _rotate(kr, c_ref[...], sa_ref[...], sb_ref[...], 1.0).astype(BF16)

    return _rows_call("kv_down", body, [hk, *tables], [w_dkv, w_kr, g], [(KV_LORA, F32), (KV_LORA, BF16), (128, BF16)])


def _kv_up(ckv, w_uk, w_uv):
    def body(a_ref, wk_ref, wv_ref, k_ref, v_ref):
        av = a_ref[...]
        k_ref[...] = lax.dot_general(av, wk_ref[...], NN, preferred_element_type=F32).astype(BF16)
        v_ref[...] = lax.dot_general(av, wv_ref[...], NN, preferred_element_type=F32).astype(BF16)

    return _rows_call("kv_up", body, [ckv], [w_uk, w_uv], [(N_HEADS * QK_NOPE, BF16), (N_HEADS * V_HEAD, BF16)])


def _rms_bwd(name, x, gains, dys, dres=None):
    d = x.shape[1]
    n = len(gains)
    has_res = dres is not None

    def body(*refs):
        x_ref, g_refs, dy_refs = refs[0], refs[1:1 + n], refs[1 + n:1 + 2 * n]
        dx_ref, dxb_ref = refs[-2 - n], refs[-1 - n]
        dg_refs = refs[-n:]
        xv = x_ref[...]
        r = lax.rsqrt(jnp.mean(xv * xv, axis=-1, keepdims=True) + EPS)
        xn = xv * r
        dx = refs[1 + 2 * n][...] if has_res else None
        parts = []
        for g_ref, dy_ref in zip(g_refs, dy_refs):
            dyv = dy_ref[...].astype(F32)
            gdy = dyv * g_ref[...]
            t = r * (gdy - xn * jnp.mean(gdy * xn, axis=-1, keepdims=True))
            dx = t if dx is None else dx + t
            parts.append(jnp.sum(dyv * xn, axis=0, keepdims=True))
        dx_ref[...] = dx
        dxb_ref[...] = dx.astype(BF16)

        @pl.when(pl.program_id(0) == 0)
        def _():
            for dg_ref, part in zip(dg_refs, parts):
                dg_ref[...] = part

        @pl.when(pl.program_id(0) > 0)
        def _():
            for dg_ref, part in zip(dg_refs, parts):
                dg_ref[...] += part

    row = pl.BlockSpec((TR, d), lambda i: (i, 0))
    vec = pl.BlockSpec((1, d), lambda i: (0, 0))
    args = [x] + list(gains) + list(dys) + ([dres] if has_res else [])
    in_specs = [row] + [vec] * n + [row] * n + ([row] if has_res else [])
    outs = _pallas(
        body, name=name, grid=(T // TR,), in_specs=in_specs, out_specs=[row, row] + [vec] * n,
        out_shape=[jax.ShapeDtypeStruct((T, d), F32), jax.ShapeDtypeStruct((T, d), BF16)]
        + [jax.ShapeDtypeStruct((1, d), F32)] * n,
        params=_params(("arbitrary",)))(*args)
    return outs[0], outs[1], list(outs[2:])


def _final(h, g, tgt):
    def body(h_ref, g_ref, t_ref, loss_ref, dh_ref, dhb_ref, dg_ref):
        hv = h_ref[...]
        r = lax.rsqrt(jnp.mean(hv * hv, axis=-1, keepdims=True) + EPS)
        xn = hv * r
        gv = g_ref[...]
        err = xn * gv - t_ref[...]
        part_loss = 0.5 * jnp.sum(jnp.mean(err * err, axis=-1, keepdims=True), axis=0, keepdims=True)
        dy = err * (1.0 / D)
        gdy = dy * gv
        dh = r * (gdy - xn * jnp.mean(gdy * xn, axis=-1, keepdims=True))
        dh_ref[...] = dh
        dhb_ref[...] = dh.astype(BF16)
        part = jnp.sum(dy * xn, axis=0, keepdims=True)
        first = pl.program_id(0) == 0

        @pl.when(first)
        def _():
            dg_ref[...] = part
            loss_ref[...] = jnp.broadcast_to(part_loss, (1, 128))

        @pl.when(jnp.logical_not(first))
        def _():
            dg_ref[...] += part
            loss_ref[...] += jnp.broadcast_to(part_loss, (1, 128))

    row = pl.BlockSpec((TR, D), lambda i: (i, 0))
    vec = pl.BlockSpec((1, D), lambda i: (0, 0))
    return _pallas(
        body, name="final_loss", grid=(T // TR,), in_specs=[row, vec, row],
        out_specs=[pl.BlockSpec((1, 128), lambda i: (0, 0)), row, row, vec],
        out_shape=[jax.ShapeDtypeStruct((1, 128), F32), jax.ShapeDtypeStruct((T, D), F32),
                   jax.ShapeDtypeStruct((T, D), BF16), jax.ShapeDtypeStruct((1, D), F32)],
        params=_params(("arbitrary",)))(h, g, tgt)


def _prev_idx(i, rows=TR):
    return jnp.maximum(i * (rows // HALO) - 1, 0)


def _next_idx(i, rows=TR):
    return jnp.minimum((i + 1) * (rows // HALO), T // HALO - 1)


def _causal_taps(ext):
    return pltpu.roll(ext, 2, 0)[HALO:], pltpu.roll(ext, 1, 0)[HALO:], ext[HALO:]


def _anticausal_taps(ext, n):
    rows = ext.shape[0]
    return pltpu.roll(ext, rows - 1, 0)[:n], pltpu.roll(ext, rows - 2, 0)[:n]


MIX_COLS = 512


def _mixer_in(hn, w_in, w):
    nc = D // MIX_COLS

    def body(h_ref, hh_ref, wb_ref, wc_ref, wu_ref, w_ref, b_ref, c_ref, u_ref, y_ref):
        i = pl.program_id(1)
        hv = h_ref[...]
        he = jnp.concatenate([hh_ref[...], hv], axis=0)
        ce = lax.dot_general(he, wc_ref[...], NN, preferred_element_type=F32).astype(BF16)
        ue = lax.dot_general(he, wu_ref[...], NN, preferred_element_type=F32).astype(BF16)
        bv = lax.dot_general(hv, wb_ref[...], NN, preferred_element_type=F32).astype(BF16)
        b_ref[...] = bv
        c_ref[...] = ce[HALO:]
        u_ref[...] = ue[HALO:]
        row = lax.broadcasted_iota(jnp.int32, (HALO + TS, 1), 0)
        cu = jnp.where(jnp.logical_or(i > 0, row >= HALO), ce.astype(F32) * ue.astype(F32), 0.0)
        x2, x1, x0 = _causal_taps(cu)
        wv = w_ref[...]
        cv = (x2 * wv[0:1] + x1 * wv[1:2]) + x0 * wv[2:3]
        y_ref[...] = (bv.astype(F32) * cv).astype(BF16)

    def cols(part):
        return pl.BlockSpec((D, MIX_COLS), lambda j, i: (0, part * nc + j))

    blk = pl.BlockSpec((TS, MIX_COLS), lambda j, i: (i, j))
    out = jax.ShapeDtypeStruct((T, D), BF16)
    return _pallas(
        body, name="l0_in", grid=(nc, T // TS),
        in_specs=[pl.BlockSpec((TS, D), lambda j, i: (i, 0)), pl.BlockSpec((HALO, D), lambda j, i: (_prev_idx(i, TS), 0)),
                  cols(0), cols(1), cols(2), pl.BlockSpec((3, MIX_COLS), lambda j, i: (0, j))],
        out_specs=[blk] * 4, out_shape=[out] * 4,
        params=_params(("parallel", "parallel")))(hn, hn, w_in, w_in, w_in, w)


def _mixer_out_bwd(dh, w_out, zb, zc, zu, w):
    last = T // TR - 1

    def body(dh_ref, dhn_ref, wo_ref, b_ref, bn_ref, c_ref, ch_ref, u_ref, uh_ref, w_ref, dz_ref, dw_ref):
        i = pl.program_id(0)
        dye = lax.dot_general(jnp.concatenate([dh_ref[...], dhn_ref[...]], axis=0), wo_ref[...], NT,
                              preferred_element_type=F32)
        cv_ = c_ref[...].astype(F32)
        uv = u_ref[...].astype(F32)
        cu = cv_ * uv
        cuh = jnp.where(i > 0, ch_ref[...].astype(F32) * uh_ref[...].astype(F32), 0.0)
        x2, x1, x0 = _causal_taps(jnp.concatenate([cuh, cu], axis=0))
        wv = w_ref[...]
        conv = (x2 * wv[0:1] + x1 * wv[1:2]) + x0 * wv[2:3]
        dyv = dye[:TR]
        dz_ref[:, 0:D] = (dyv * conv).astype(BF16)
        dconv = dyv * b_ref[...].astype(F32)
        dconv_n = jnp.where(i < last, dye[TR:] * bn_ref[...].astype(F32), 0.0)
        n1, n2 = _anticausal_taps(jnp.concatenate([dconv, dconv_n], axis=0), TR)
        dcu = (dconv * wv[2:3] + n1 * wv[1:2]) + n2 * wv[0:1]
        dz_ref[:, D:2 * D] = (dcu * uv).astype(BF16)
        dz_ref[:, 2 * D:3 * D] = (dcu * cv_).astype(BF16)
        part = jnp.concatenate([jnp.sum(dconv * x2, axis=0, keepdims=True),
                                jnp.sum(dconv * x1, axis=0, keepdims=True),
                                jnp.sum(dconv * x0, axis=0, keepdims=True)], axis=0)

        @pl.when(i == 0)
        def _():
            dw_ref[...] = part

        @pl.when(i > 0)
        def _():
            dw_ref[...] += part

    main = pl.BlockSpec((TR, D), lambda i: (i, 0))
    prev = pl.BlockSpec((HALO, D), lambda i: (_prev_idx(i), 0))
    nxt = pl.BlockSpec((HALO, D), lambda i: (_next_idx(i), 0))
    wspec = pl.BlockSpec((3, D), lambda i: (0, 0))
    return _pallas(
        body, name="d_l0_out", grid=(T // TR,),
        in_specs=[main, nxt, pl.BlockSpec((D, D), lambda i: (0, 0)), main, nxt, main, prev, main, prev, wspec],
        out_specs=[pl.BlockSpec((TR, 3 * D), lambda i: (i, 0)), wspec],
        out_shape=[jax.ShapeDtypeStruct((T, 3 * D), BF16), jax.ShapeDtypeStruct((3, D), F32)],
        params=_params(("arbitrary",)))(dh, dh, w_out, zb, zb, zc, zc, zu, zu, w)


def _sigmoid(x):
    return 0.5 * jnp.tanh(0.5 * x) + 0.5


def _ffn_up_act(name, hf, w_up, w, b):
    def body(h_ref, hh_ref, wg_ref, wv_ref, w_ref, b_ref, g_ref, v_ref, a_ref):
        i = pl.program_id(1)
        hv = h_ref[...]
        ge = lax.dot_general(jnp.concatenate([hh_ref[...], hv], axis=0), wg_ref[...], NT,
                             preferred_element_type=F32).astype(BF16)
        v = lax.dot_general(hv, wv_ref[...], NT, preferred_element_type=F32).astype(BF16)
        g_ref[...] = ge[HALO:]
        v_ref[...] = v
        ext = ge.astype(F32)
        row = lax.broadcasted_iota(jnp.int32, (HALO + TS, 1), 0)
        ext = jnp.where(jnp.logical_or(i > 0, row >= HALO), ext, 0.0)
        x2, x1, x0 = _causal_taps(ext)
        wv = w_ref[...]
        gc = ((x2 * wv[0:1] + x1 * wv[1:2]) + x0 * wv[2:3]) + b_ref[...]
        a_ref[...] = ((gc * _sigmoid(gc)) * v.astype(F32)).astype(BF16)

    blk = pl.BlockSpec((None, TS, FF_BLK), lambda j, i: (j, i, 0))
    out = jax.ShapeDtypeStruct((N_FF_BLK, T, FF_BLK), BF16)
    return _pallas(
        body, name=name, grid=(N_FF_BLK, T // TS),
        in_specs=[pl.BlockSpec((TS, D), lambda j, i: (i, 0)),
                  pl.BlockSpec((HALO, D), lambda j, i: (_prev_idx(i, TS), 0)),
                  pl.BlockSpec((None, None, FF_BLK, D), lambda j, i: (0, j, 0, 0)),
                  pl.BlockSpec((None, None, FF_BLK, D), lambda j, i: (0, j + N_FF_BLK, 0, 0)),
                  pl.BlockSpec((None, 3, FF_BLK), lambda j, i: (j, 0, 0)),
                  pl.BlockSpec((None, 1, FF_BLK), lambda j, i: (j, 0, 0))],
        out_specs=[blk, blk, blk], out_shape=[out, out, out],
        params=_params(("parallel", "parallel")))(hf, hf, w_up, w_up, w, b)


def _ffn_dact(name, dh, w_down4, g, v, act, hf, w, b):
    last = T // TS - 1

    def body(dh_ref, dhn_ref, wd_ref, g_ref, gp_ref, gn_ref, v_ref, vn_ref, a_ref, hf_ref, w_ref, b_ref,
             dg_ref, dv_ref, dw_ref, db_ref, gd_ref, gu_ref, acc_d, acc_u):
        i = pl.program_id(1)
        dhv = dh_ref[...]
        da = lax.dot_general(jnp.concatenate([dhv, dhn_ref[...]], axis=0), wd_ref[...], NT,
                             preferred_element_type=F32)
        row = lax.broadcasted_iota(jnp.int32, (TS + HALO, 1), 0)
        da = jnp.where(jnp.logical_or(i < last, row < TS), da, 0.0)
        gp = jnp.where(i > 0, gp_ref[...].astype(F32), 0.0)
        ext = jnp.concatenate([gp, g_ref[...].astype(F32), gn_ref[...].astype(F32)], axis=0)
        x2, x1, x0 = _causal_taps(ext)
        wv = w_ref[...]
        gc = ((x2 * wv[0:1] + x1 * wv[1:2]) + x0 * wv[2:3]) + b_ref[...]
        sg = _sigmoid(gc)
        vv = jnp.concatenate([v_ref[...].astype(F32), vn_ref[...].astype(F32)], axis=0)
        dvb = (da[:TS] * (gc[:TS] * sg[:TS])).astype(BF16)
        dv_ref[...] = dvb
        dgc = (da * vv) * (sg * (1.0 + gc * (1.0 - sg)))
        n1, n2 = _anticausal_taps(dgc, TS)
        d0 = dgc[:TS]
        dgb = ((d0 * wv[2:3] + n1 * wv[1:2]) + n2 * wv[0:1]).astype(BF16)
        dg_ref[...] = dgb
        part_w = jnp.concatenate([jnp.sum(d0 * x2[:TS], axis=0, keepdims=True),
                                  jnp.sum(d0 * x1[:TS], axis=0, keepdims=True),
                                  jnp.sum(d0 * x0[:TS], axis=0, keepdims=True)], axis=0)
        part_b = jnp.sum(d0, axis=0, keepdims=True)
        hfv = hf_ref[...]
        part_d = lax.dot_general(a_ref[...], dhv, TN, preferred_element_type=F32)
        part_g = lax.dot_general(dgb, hfv, TN, preferred_element_type=F32)
        part_v = lax.dot_general(dvb, hfv, TN, preferred_element_type=F32)

        @pl.when(i == 0)
        def _():
            dw_ref[...] = part_w
            db_ref[...] = part_b
            acc_d[...] = part_d
            acc_u[0] = part_g
            acc_u[1] = part_v

        @pl.when(i > 0)
        def _():
            dw_ref[...] += part_w
            db_ref[...] += part_b
            acc_d[...] += part_d
            acc_u[0] += part_g
            acc_u[1] += part_v

        @pl.when(i == last)
        def _():
            gd_ref[...] = acc_d[...].astype(BF16)
            gu_ref[...] = acc_u[...].astype(BF16)

    blk = pl.BlockSpec((None, TS, FF_BLK), lambda j, i: (j, i, 0))
    prev = pl.BlockSpec((None, HALO, FF_BLK), lambda j, i: (j, _prev_idx(i, TS), 0))
    nxt = pl.BlockSpec((None, HALO, FF_BLK), lambda j, i: (j, _next_idx(i, TS), 0))
    rows = pl.BlockSpec((TS, D), lambda j, i: (i, 0))
    wspec = pl.BlockSpec((None, 3, FF_BLK), lambda j, i: (j, 0, 0))
    bspec = pl.BlockSpec((None, 1, FF_BLK), lambda j, i: (j, 0, 0))
    return _pallas(
        body, name=name, grid=(N_FF_BLK, T // TS),
        in_specs=[rows, pl.BlockSpec((HALO, D), lambda j, i: (_next_idx(i, TS), 0)),
                  pl.BlockSpec((None, None, FF_BLK, D), lambda j, i: (0, j, 0, 0)),
                  blk, prev, nxt, blk, nxt, blk, rows, wspec, bspec],
        out_specs=[blk, blk, wspec, bspec, pl.BlockSpec((FF_BLK, D), lambda j, i: (j, 0)),
                   pl.BlockSpec((None, 2, FF_BLK, D), lambda j, i: (j, 0, 0, 0))],
        out_shape=[jax.ShapeDtypeStruct((N_FF_BLK, T, FF_BLK), BF16), jax.ShapeDtypeStruct((N_FF_BLK, T, FF_BLK), BF16),
                   jax.ShapeDtypeStruct((N_FF_BLK, 3, FF_BLK), F32), jax.ShapeDtypeStruct((N_FF_BLK, 1, FF_BLK), F32),
                   jax.ShapeDtypeStruct((D_FF, D), BF16), jax.ShapeDtypeStruct((N_FF_BLK, 2, FF_BLK, D), BF16)],
        scratch_shapes=[pltpu.VMEM((FF_BLK, D), F32), pltpu.VMEM((2, FF_BLK, D), F32)],
        params=_params(("parallel", "arbitrary")))(dh, dh, w_down4, g, g, g, v, v, act, hf, w, b)


def _rope_tables(pos, inv_freq):
    half = QK_ROPE // 2

    def body(p_ref, f_ref, c_ref, sa_ref, sb_ref):
        ang = p_ref[...].astype(F32) * f_ref[...]
        lane = lax.broadcasted_iota(jnp.int32, (T, 128), 1)
        c = jnp.cos(ang)
        s = jnp.sin(ang)
        c_ref[...] = jnp.where(lane < 2 * half, c, 0.0)
        sa_ref[...] = jnp.where(lane < half, -s, 0.0)
        sb_ref[...] = jnp.where(jnp.logical_and(lane >= half, lane < 2 * half), s, 0.0)

    return _pallas(
        body, name="rope_tables", in_specs=[VMEM_SPEC] * 2, out_specs=[VMEM_SPEC] * 3,
        out_shape=[jax.ShapeDtypeStruct((T, 128), F32)] * 3,
        params=pltpu.CompilerParams(vmem_limit_bytes=VMEM_LIMIT))(pos, inv_freq)


def _rotate(r, c, sa, sb, sign):
    return r * c + sign * (pltpu.roll(r, 96, 1) * sa + pltpu.roll(r, 32, 1) * sb)


def _q_up(cq, w_uq, tables):
    cos, sa, sb = tables

    def body(a_ref, b_ref, c_ref, sa_ref, sb_ref, o_ref):
        for h in range(N_HEADS):
            r = lax.dot_general(a_ref[...], b_ref[h], NN, preferred_element_type=F32)
            o_ref[h, :, :QK_NOPE] = r[:, :QK_NOPE].astype(BF16)
            o_ref[h, :, QK_NOPE:] = _rotate(r[:, QK_NOPE:], c_ref[...], sa_ref[...], sb_ref[...], 1.0).astype(BF16)

    tab = pl.BlockSpec((TS, 128), lambda i: (i, 0))
    return _pallas(
        body, name="q_up", grid=(T // TS,),
        in_specs=[pl.BlockSpec((TS, Q_LORA), lambda i: (i, 0)),
                  pl.BlockSpec((N_HEADS, Q_LORA, QK_PAD), lambda i: (0, 0, 0)), tab, tab, tab],
        out_specs=pl.BlockSpec((N_HEADS, TS, QK_PAD), lambda i: (0, i, 0)),
        out_shape=jax.ShapeDtypeStruct((N_HEADS, T, QK_PAD), BF16),
        params=_params(("parallel",)))(cq, w_uq, cos, sa, sb)


def _rope(name, x, tables, sign, out_dtype, reduce_groups=False):
    g, _, w = x.shape
    cos, sa, sb = tables

    def body(x_ref, c_ref, sa_ref, sb_ref, o_ref):
        xv = x_ref[...].astype(F32)
        if reduce_groups:
            acc = xv[0]
            for k in range(1, g):
                acc = acc + xv[k]
            xv = acc
        out = _rotate(xv[:, w - 128:], c_ref[...], sa_ref[...], sb_ref[...], sign)
        if w > 128:
            o_ref[:, :w - 128] = xv[:, :w - 128].astype(out_dtype)
        o_ref[:, w - 128:] = out.astype(out_dtype)

    tab = pl.BlockSpec((TM, 128), lambda h, i: (i, 0))
    if reduce_groups:
        x_spec = pl.BlockSpec((g, TM, w), lambda h, i: (0, i, 0))
        groups = 1
    else:
        x_spec = pl.BlockSpec((None, TM, w), lambda h, i: (h, i, 0))
        groups = g
    return _pallas(
        body, name=name, grid=(groups, T // TM), in_specs=[x_spec, tab, tab, tab],
        out_specs=pl.BlockSpec((None, TM, w), lambda h, i: (h, i, 0)),
        out_shape=jax.ShapeDtypeStruct((groups, T, w), out_dtype),
        params=_params(("parallel", "parallel")))(x, cos, sa, sb)


SCALE = (QK_NOPE + QK_ROPE) ** -0.5
LOG2E = 1.4426950408889634
SCALE2 = SCALE * LOG2E


def _diag_mask(transposed):
    shift = CHUNK.bit_length() - 1
    a = lax.broadcasted_iota(jnp.int32, (TQ, TQ), 0) >> shift
    b = lax.broadcasted_iota(jnp.int32, (TQ, TQ), 1) >> shift
    return (a <= b) if transposed else (b <= a)


def _as_row(col):
    return jnp.transpose(jnp.broadcast_to(col, (col.shape[0], 128)), (1, 0))[0:1]


def _keys(kn_ref, kr_ref, off):
    return jnp.concatenate([kn_ref[pl.ds(off, TQ), :], kr_ref[pl.ds(off, TQ), :]], axis=1)


def _attn_fwd(q, kn, kr, v):
    def body(q_ref, kn_ref, kr_ref, v_ref, o_ref, lse_ref):
        i = pl.program_id(1)
        qv = q_ref[...]

        def step(j, carry, masked):
            m, l, acc = carry
            off = pl.multiple_of(j * TQ, TQ)
            s = lax.dot_general(qv, _keys(kn_ref, kr_ref, off), NT, preferred_element_type=F32) * SCALE2
            if masked:
                s = jnp.where(_diag_mask(False), s, NEG_INF)
            m_new = jnp.maximum(m, jnp.max(s, axis=-1, keepdims=True))
            p = jnp.exp2(s - m_new)
            alpha = jnp.exp2(m - m_new)
            l = alpha * l + jnp.sum(p, axis=-1, keepdims=True)
            acc = alpha * acc + lax.dot_general(p.astype(BF16), v_ref[pl.ds(off, TQ), :], NN, preferred_element_type=F32)
            return m_new, l, acc

        init = (jnp.full((TQ, 1), NEG_INF, F32), jnp.zeros((TQ, 1), F32), jnp.zeros((TQ, V_HEAD), F32))
        carry = lax.fori_loop(0, i, lambda j, cr: step(j, cr, False), init)
        m, l, acc = step(i, carry, True)
        o_ref[...] = (acc / l).astype(BF16)
        lse_ref[...] = _as_row(m + jnp.log(l) * LOG2E)

    return _pallas(
        body, name="attn_fwd", grid=(N_HEADS, T // TQ),
        in_specs=[pl.BlockSpec((None, TQ, QK_PAD), lambda h, i: (h, i, 0)),
                  pl.BlockSpec((T, QK_NOPE), lambda h, i: (0, h)),
                  pl.BlockSpec((T, 128), lambda h, i: (0, 0)),
                  pl.BlockSpec((T, V_HEAD), lambda h, i: (0, h))],
        out_specs=[pl.BlockSpec((TQ, V_HEAD), lambda h, i: (i, h)), pl.BlockSpec((None, 1, TQ), lambda h, i: (h, 0, i))],
        out_shape=[jax.ShapeDtypeStruct((T, N_HEADS * V_HEAD), BF16), jax.ShapeDtypeStruct((N_HEADS, 1, T), F32)],
        params=_params(("parallel", "parallel")))(q, kn, kr, v)


def _attn_bwd(q, kn, kr, v, o, do, lse_row, tables):
    nq = T // TQ
    cos, sa, sb = tables

    def body(q_ref, kn_ref, kr_ref, v_ref, o_ref, do_ref, lse_ref, c_ref, sa_ref, sb_ref,
             dq_ref, dkn_ref, dkr_ref, dv_ref, dq_acc, dl_ref):
        j = pl.program_id(1)

        @pl.when(j == 0)
        def _():
            dq_acc[...] = jnp.zeros_like(dq_acc)
            for i in range(nq):
                rows = pl.ds(i * TQ, TQ)
                prod = do_ref[rows, :].astype(F32) * o_ref[rows, :].astype(F32)
                dl_ref[:, rows] = _as_row(jnp.sum(prod, axis=-1, keepdims=True))

        kk = jnp.concatenate([kn_ref[...], kr_ref[...]], axis=1)
        vv = v_ref[...]

        def step(i, carry, masked):
            dk, dv = carry
            off = pl.multiple_of(i * TQ, TQ)
            qi = q_ref[pl.ds(off, TQ), :]
            doi = do_ref[pl.ds(off, TQ), :]
            st = lax.dot_general(kk, qi, NT, preferred_element_type=F32) * SCALE2
            if masked:
                st = jnp.where(_diag_mask(True), st, NEG_INF)
            pt = jnp.exp2(st - lse_ref[:, pl.ds(off, TQ)])
            dv = dv + lax.dot_general(pt.astype(BF16), doi, NN, preferred_element_type=F32)
            dpt = lax.dot_general(vv, doi, NT, preferred_element_type=F32)
            dst = ((pt * (dpt - dl_ref[:, pl.ds(off, TQ)])) * SCALE).astype(BF16)
            dk = dk + lax.dot_general(dst, qi, NN, preferred_element_type=F32)
            dq_acc[pl.ds(off, TQ), :] += lax.dot_general(dst, kk, TN, preferred_element_type=F32)
            return dk, dv

        carry = step(j, (jnp.zeros((TQ, QK_PAD), F32), jnp.zeros((TQ, V_HEAD), F32)), True)
        dk, dv = lax.fori_loop(j + 1, nq, lambda i, cr: step(i, cr, False), carry)
        dkn_ref[...] = dk[:, :QK_NOPE].astype(BF16)
        dkr_ref[...] = dk[:, QK_NOPE:]
        dv_ref[...] = dv.astype(BF16)

        @pl.when(j == nq - 1)
        def _():
            dq = dq_acc[...]
            dq_ref[:, :QK_NOPE] = dq[:, :QK_NOPE].astype(BF16)
            dq_ref[:, QK_NOPE:] = _rotate(dq[:, QK_NOPE:], c_ref[...], sa_ref[...], sb_ref[...], -1.0).astype(BF16)

    row = pl.BlockSpec((None, 1, T), lambda h, j: (h, 0, 0))
    head = pl.BlockSpec((TQ, 128), lambda h, j: (j, h))
    whole = pl.BlockSpec((None, T, QK_PAD), lambda h, j: (h, 0, 0))
    tab = pl.BlockSpec((T, 128), lambda h, j: (0, 0))
    heads = pl.BlockSpec((T, V_HEAD), lambda h, j: (0, h))
    return _pallas(
        body, name="attn_bwd", grid=(N_HEADS, nq),
        in_specs=[whole, head, pl.BlockSpec((TQ, 128), lambda h, j: (j, 0)), head, heads, heads, row, tab, tab, tab],
        out_specs=[whole, head, pl.BlockSpec((None, TQ, 128), lambda h, j: (h, j, 0)), head],
        out_shape=[jax.ShapeDtypeStruct((N_HEADS, T, QK_PAD), BF16), jax.ShapeDtypeStruct((T, N_HEADS * QK_NOPE), BF16),
                   jax.ShapeDtypeStruct((N_HEADS, T, 128), F32), jax.ShapeDtypeStruct((T, N_HEADS * V_HEAD), BF16)],
        scratch_shapes=[pltpu.VMEM((T, QK_PAD), F32), pltpu.VMEM((1, T), F32)],
        params=_params(("parallel", "arbitrary")))(q, kn, kr, v, o, do, lse_row, cos, sa, sb)


def _ffn_layer_fwd(tag, h, gain, ex):
    hf = _rms_fwd(f"{tag}_norm", h, gain)
    g, v, act = _ffn_up_act(f"{tag}_up", hf, ex.need(f"ffn_w_up{tag[1]}", hf), ex.need(f"ffn_cw{tag[1]}", hf),
                            ex.need(f"ffn_cb{tag[1]}", hf))
    ex.at(f"{tag}_up", act)
    rows = pl.BlockSpec((TS, D), lambda i: (i, 0))
    out = _mm_sum(f"{tag}_down",
                  [(act, pl.BlockSpec((N_FF_BLK, TS, FF_BLK), lambda i: (0, i, 0)), ex.need(f"ffn_w_down{tag[1]}", act),
                    pl.BlockSpec((None, N_FF_BLK, FF_BLK, D), lambda i: (0, 0, 0, 0)), NN, 0)],
                  grid=(T // TS,), o_spec=rows, o_shape=(T, D), o_dtype=F32, add=h)
    ex.at(f"{tag}_down", out)
    return out, (hf, g, v, act)


def _ffn_layer_bwd(tag, h, gain, ex, saved, dh, dh_bf):
    hf, g, v, act = saved
    layer = tag[1]
    w_up, w_down4 = ex.need(f"ffn_w_up{layer}", dh_bf), ex.need(f"ffn_w_down{layer}", dh_bf)
    dg, dv, dcw, dcb, g_down, g_up = _ffn_dact(f"{tag}_dact", dh_bf, w_down4, g, v, act, hf,
                                               ex.need(f"ffn_cw{layer}", dh_bf), ex.need(f"ffn_cb{layer}", dh_bf))
    ex.grad("ffn_w_up", int(layer), g_up)
    ex.grad("ffn_w_down", int(layer), g_down.reshape(1, N_DEV, D_FF // N_DEV, D))
    ex.at(f"{tag}_dact", dg)
    ex.at(f"{tag}_gup", g_up)
    part = pl.BlockSpec((N_FF_BLK, TR, FF_BLK), lambda i: (0, i, 0))
    dh_in, dh_in_bf, dgain = _mm_sum(
        f"{tag}_dhf",
        [(dg, part, w_up, pl.BlockSpec((None, N_FF_BLK, FF_BLK, D), lambda i: (0, 0, 0, 0)), NN, 0),
         (dv, part, w_up, pl.BlockSpec((None, N_FF_BLK, FF_BLK, D), lambda i: (0, 1, 0, 0)), NN, 0)],
        grid=(T // TR,), o_spec=pl.BlockSpec((TR, D), lambda i: (i, 0)), o_shape=(T, D), o_dtype=F32,
        norm_bwd=(h, [gain], dh))
    ex.at(f"{tag}_dhf", dh_in)
    return dh_in, dh_in_bf, dgain[0], dcw, dcb


def _local_step(x, pos, tgt, rep, ex):
    attn_norm, ffn_norm, final_norm = rep["attn_norm"], rep["ffn_norm"], rep["final_norm"]
    half = QK_ROPE // 2
    inv = 1.0 / (ROPE_THETA ** (jnp.arange(half, dtype=F32) / half))
    inv_freq = jnp.concatenate([inv, inv, jnp.zeros((128 - 2 * half,), F32)]).reshape(1, 128)
    tables = _rope_tables(pos, inv_freq)

    hn0 = _rms_fwd("l0_norm", x, attn_norm[0:1])
    w_in = ex.need("sc_w_in", hn0)
    ex.at("mixer_ready", hn0)
    zb, zc, zu, y = _mixer_in(hn0, w_in, ex.need("sc_conv_w", hn0))
    ex.at("l0_in", y)
    h1 = _mm_rows("l0_out", y, ex.need("sc_w_out", y), NN, F32, D, tn=512, add=x)
    ex.at("l0_out", h1)
    h2, ffn0 = _ffn_layer_fwd("f0", h1, ffn_norm[0:1], ex)

    hk, hn1 = _rms_fwd2("h2_norms", h2, rep["kv_in_norm"], attn_norm[1:2])
    ckv_raw, ckv, kr = _kv_down(hk, ex.need("w_dkv", hk), ex.need("w_kr", hk), rep["kv_latent_norm"], tables)
    kn, vv = _kv_up(ckv, ex.need("w_uk", ckv), ex.need("w_uv", ckv))

    cq_raw, cq = _down_norm("q_down", hn1, ex.need("w_dq", hn1), rep["q_latent_norm"])
    w_uq = ex.need("w_uq", cq)
    q = _q_up(cq, w_uq, tables)
    o, lse = _attn_fwd(q, kn, kr, vv)
    ex.at("attn_fwd", o)
    w_o = ex.need("w_o", o)
    h3 = _mm_rows("attn_out", o, w_o, NN, F32, D, tn=512, add=h2)
    h4, ffn1 = _ffn_layer_fwd("f1", h3, ffn_norm[1:2], ex)

    loss, dh4, dh4_bf, d_final = _final(h4, final_norm.reshape(1, D), tgt)

    dh3, dh3_bf, d_fn1, dcw1, dcb1 = _ffn_layer_bwd("f1", h3, ffn_norm[1:2], ex, ffn1, dh4, dh4_bf)
    ex.at("f1_bwd", dh3)

    do = _mm_rows("d_attn_out", dh3_bf, w_o, NT, BF16, N_HEADS * V_HEAD)
    ex.grad("w_o", None, _mm_wgrad("g_w_o", o, dh3_bf).reshape(1, N_DEV, D // N_DEV, D))
    dq_pre, dkn, dkr, dvv = _attn_bwd(q, kn, kr, vv, o, do, lse, tables)
    def rows_of(a):
        return a[None], pl.BlockSpec((1, TS, a.shape[1]), lambda i: (0, i, 0))

    def whole(wt):
        return wt[None], pl.BlockSpec((1,) + wt.shape, lambda i: (0, 0, 0))

    def row_blocks(d):
        return dict(grid=(T // TS,), o_spec=pl.BlockSpec((TS, d), lambda i: (i, 0)), o_shape=(T, d), o_dtype=F32)

    _, dcq_raw_bf, (d_qln,) = _mm_sum(
        "d_q_up", [(dq_pre, pl.BlockSpec((N_HEADS, TS, QK_PAD), lambda i: (0, i, 0)),
                    w_uq, pl.BlockSpec((N_HEADS, Q_LORA, QK_PAD), lambda i: (0, 0, 0)), NT, 0)],
        norm_bwd=(cq_raw, [rep["q_latent_norm"]], None), **row_blocks(Q_LORA))
    g_uq = _mm("g_w_uq", cq, dq_pre, grid=(N_HEADS,),
               a_spec=pl.BlockSpec((T, Q_LORA), lambda h: (0, 0)),
               b_spec=pl.BlockSpec((None, T, QK_PAD), lambda h: (h, 0, 0)),
               o_spec=pl.BlockSpec((None, Q_LORA, QK_PAD), lambda h: (h, 0, 0)),
               o_shape=(N_HEADS, Q_LORA, QK_PAD), o_dtype=BF16, dims=TN)
    ex.grad("w_uq", None, g_uq[:, :, :QK_NOPE + QK_ROPE].reshape(1, N_DEV, Q_LORA, QK_NOPE + QK_ROPE))
    ex.grad("w_dq", None, _mm_wgrad("g_w_dq", hn1, dcq_raw_bf).reshape(1, N_DEV, D // N_DEV, Q_LORA))

    _, dckv_raw_bf, (d_kvln,) = _mm_sum(
        "d_kv_up", [(*rows_of(dkn), *whole(ex.need("w_uk", dkn)), NT, 0),
                    (*rows_of(dvv), *whole(ex.need("w_uv", dvv)), NT, 0)],
        norm_bwd=(ckv_raw, [rep["kv_latent_norm"]], None), **row_blocks(KV_LORA))
    ex.grad("w_uk", None, _mm_wgrad("g_w_uk", ckv, dkn))
    ex.grad("w_uv", None, _mm_wgrad("g_w_uv", ckv, dvv))
    dkr_raw_bf = _rope("dk_rope", dkr, tables, -1.0, BF16, reduce_groups=True).reshape(T, 128)
    ex.grad("w_dkv", None, _mm_wgrad("g_w_dkv", hk, dckv_raw_bf).reshape(1, N_DEV, D // N_DEV, KV_LORA))
    ex.grad("w_kr", None, _mm_wgrad("g_w_kr", hk, dkr_raw_bf)[:, :QK_ROPE].reshape(1, N_DEV, D // N_DEV, QK_ROPE))

    dh2, dh2_bf, (d_an1, d_kvin) = _mm_sum(
        "d_h2", [(*rows_of(dcq_raw_bf), *whole(ex.need("w_dq", dcq_raw_bf)), NT, 0),
                 (*rows_of(dckv_raw_bf), *whole(ex.need("w_dkv", dckv_raw_bf)), NT, 1),
                 (*rows_of(dkr_raw_bf), *whole(ex.need("w_kr", dkr_raw_bf)), NT, 1)],
        norm_bwd=(h2, [attn_norm[1:2], rep["kv_in_norm"]], dh3), **row_blocks(D))
    ex.at("kv_bwd", dh2)

    dh1, dh1_bf, d_fn0, dcw0, dcb0 = _ffn_layer_bwd("f0", h1, ffn_norm[0:1], ex, ffn0, dh2, dh2_bf)
    ex.at("f0_bwd", dh1)

    ex.grad("sc_w_out", None, _mm_wgrad("g_sc_w_out", y, dh1_bf).reshape(1, N_DEV, D // N_DEV, D))
    dz, d_scw = _mixer_out_bwd(dh1_bf, ex.need("sc_w_out", dh1_bf), zb, zc, zu, ex.need("sc_conv_w", dh1_bf))
    g_in = _mm_wgrad("g_sc_w_in", hn0, dz)
    ex.grad("sc_w_in", None, g_in)
    ex.at("sc_bwd", g_in)
    ex.at("d_l0_in", g_in)
    grad_x, _, (d_an0,) = _mm_sum(
        "d_l0_in", [(*rows_of(dz), *whole(ex.need("sc_w_in", dz)), NT, 0)],
        norm_bwd=(x, [attn_norm[0:1]], dh1), **row_blocks(D))

    small = {
        "attn_norm": jnp.concatenate([d_an0, d_an1], axis=0),
        "ffn_norm": jnp.concatenate([d_fn0, d_fn1], axis=0),
        "final_norm": d_final.reshape(D),
        "kv_in_norm": d_kvin.reshape(D),
        "kv_latent_norm": d_kvln.reshape(KV_LORA),
        "q_latent_norm": d_qln,
        "ffn_conv_b": jnp.stack([dcb0, dcb1]).transpose(0, 2, 1, 3).reshape(2, D_FF),
        "sc_conv_w": d_scw,
        "ffn_conv_w": jnp.stack([dcw0, dcw1]).transpose(0, 2, 1, 3).reshape(2, 3, D_FF),
    }
    return loss, grad_x, small


def _place():
    return lax.axis_index("x"), lax.axis_index("y"), lax.axis_index("c")


def _peers():
    x, y, c = _place()
    return (x, y, 1 - c), [(1 - x, y), (x, 1 - y), (1 - x, 1 - y)]


def _window(ref, kind, dev):
    if kind == "blocked":
        return ref.at[:, dev]
    if kind == "halves":
        return ref.at[lax.rem(dev, 4), dev // 4]
    width = ref.shape[-1] // N_DEV
    return ref.at[:, pl.ds(pl.multiple_of(dev * width, 128), width)]


HBM_SPEC = pl.BlockSpec(memory_space=pltpu.HBM)
SEM_SPEC = pl.BlockSpec(memory_space=pltpu.SEMAPHORE)
EFFECT = pltpu.SideEffectType.DATAFLOW_SIDE_EFFECTING
TOKEN = jax.ShapeDtypeStruct((8, 128), F32)


def _hbm(a):
    return pltpu.with_memory_space_constraint(a, pltpu.HBM)


def _copies_start(name, jobs):
    nj = len(jobs)
    counts = [(len(srcs), len(lands)) for srcs, lands, _, _ in jobs]
    n_arr = sum(ns + nl for ns, nl in counts)

    def body(*refs):
        sems, token = refs[n_arr:n_arr + 2 * nj], refs[-1]
        at = 0
        for j, ((ns, nl), (_, _, ncopy, plan)) in enumerate(zip(counts, jobs)):
            copies = plan(refs[at:at + ns], refs[at + ns:at + ns + nl])
            assert len(copies) == ncopy
            for k, (sent, dst, to, _) in enumerate(copies):
                pltpu.make_async_remote_copy(src_ref=sent, dst_ref=dst, send_sem=sems[2 * j].at[k],
                                             recv_sem=sems[2 * j + 1].at[k], device_id=to, device_id_type=MESH).start()
            at += ns + nl
        token[...] = jnp.zeros_like(token)

    arrays = [a for srcs, lands, _, _ in jobs for a in list(srcs) + list(lands)]
    sem_shapes = [pltpu.SemaphoreType.DMA((ncopy,)) for _, _, ncopy, _ in jobs for _ in range(2)]
    outs = pl.pallas_call(
        body, name=name, in_specs=[HBM_SPEC] * n_arr,
        out_specs=[SEM_SPEC] * (2 * nj) + [HBM_SPEC] * n_arr + [VMEM_SPEC],
        out_shape=sem_shapes + [pltpu.HBM(a.shape, a.dtype) for a in arrays] + [TOKEN],
        input_output_aliases={i: 2 * nj + i for i in range(n_arr)},
        compiler_params=pltpu.CompilerParams(has_side_effects=EFFECT))(*[_hbm(a) for a in arrays])
    _Chain.last = outs[-1]
    flights, at = [], 2 * nj
    for j, (ns, nl) in enumerate(counts):
        flights.append((outs[2 * j], outs[2 * j + 1], list(outs[at:at + ns]), list(outs[at + ns:at + ns + nl])))
        at += ns + nl
    return flights


def _copies_wait(name, started, ncopy, plan):
    send, recv, srcs, lands = started
    ns, nl = len(srcs), len(lands)

    def body(*refs):
        send_ref, recv_ref, token = refs[ns + nl], refs[ns + nl + 1], refs[-1]
        copies = plan(refs[:ns], refs[ns:ns + nl])
        assert len(copies) == ncopy
        for k, (sent, _, to, landed) in enumerate(copies):
            cp = pltpu.make_async_remote_copy(src_ref=sent, dst_ref=landed, send_sem=send_ref.at[k],
                                              recv_sem=recv_ref.at[k], device_id=to, device_id_type=MESH)
            cp.wait_send()
            cp.wait_recv()
        token[...] = jnp.zeros_like(token)

    arrays = list(srcs) + list(lands)
    outs = pl.pallas_call(
        body, name=name, in_specs=[HBM_SPEC] * (ns + nl) + [SEM_SPEC] * 2 + [ANY_SPEC],
        out_specs=[HBM_SPEC] * (ns + nl) + [VMEM_SPEC], out_shape=[pltpu.HBM(a.shape, a.dtype) for a in arrays] + [TOKEN],
        input_output_aliases={i: i for i in range(ns + nl)},
        compiler_params=pltpu.CompilerParams(has_side_effects=EFFECT))(*arrays, send, recv, _Chain.last)
    _Chain.last = outs[-1]
    return list(outs[:ns]), list(outs[ns:-1])


def _plan_gather_chips(kinds):
    def plan(srcs, lands):
        x, y, c = _place()
        sibling, chips = _peers()
        out = []
        for t, kind in enumerate(kinds):
            mine = _window(lands[t], kind, 4 * x + 2 * y + c)
            out.append((srcs[t], mine, (x, y, c), mine))
            out.append((srcs[t], mine, sibling, _window(lands[t], kind, 4 * x + 2 * y + 1 - c)))
            for px, py in chips:
                out.append((srcs[t], mine, (px, py, c), _window(lands[t], kind, 4 * px + 2 * py + c)))
        return out
    return plan, 5 * len(kinds)


def _plan_gather_all(n):
    def plan(srcs, lands):
        x, y, c = _place()
        out = []
        for t in range(n):
            mine = lands[t].at[:, 4 * x + 2 * y + c]
            for m in range(N_DEV):
                px, py, pc = (1 - x if m & 4 else x), (1 - y if m & 2 else y), (1 - c if m & 1 else c)
                out.append((srcs[t], mine, (px, py, pc), lands[t].at[:, 4 * px + 2 * py + pc]))
        return out
    return plan, N_DEV * n


def _plan_gather_sibling(kinds):
    def plan(srcs, lands):
        _, _, c = _place()
        sibling, chips = _peers()
        out = []
        for t, kind in enumerate(kinds):
            for px, py in chips:
                w = _window(lands[t], kind, 4 * px + 2 * py + c)
                out.append((w, w, sibling, _window(lands[t], kind, 4 * px + 2 * py + 1 - c)))
        return out
    return plan, 3 * len(kinds)


def _plan_scatter_sibling(kinds):
    def plan(srcs, lands):
        _, _, c = _place()
        sibling, _ = _peers()
        out = []
        for t, kind in enumerate(kinds):
            for k in range(N_CHIP):
                out.append((_window(srcs[t], kind, 2 * k + 1 - c), lands[t].at[k], sibling, lands[t].at[k]))
        return out
    return plan, N_CHIP * len(kinds)


def _plan_scatter_chips(n):
    def plan(srcs, lands):
        x, y, c = _place()
        _, chips = _peers()
        out = []
        for t in range(n):
            for px, py in chips:
                out.append((srcs[t].at[2 * px + py], lands[t].at[2 * x + y], (px, py, c), lands[t].at[2 * px + py]))
        return out
    return plan, 3 * n


def _landing(shard, kind):
    if kind == "blocked":
        return lax.empty((shard.shape[0], N_DEV) + shard.shape[1:], shard.dtype)
    return lax.empty((shard.shape[0], N_DEV * shard.shape[1]), shard.dtype)


def _chip_sums(name, grads, kinds, recvs, c):
    n = len(grads)
    in_specs, out_specs, out_shape, args = [], [], [], []
    for gr, kind, rv in zip(grads, kinds, recvs):
        if kind == "blocked":
            rows, w = gr.shape[2], gr.shape[3]
            in_specs.append(pl.BlockSpec((None, None, rows, w), lambda k, cref: (0, 2 * k + cref[0], 0, 0)))
        elif kind == "halves":
            rows, w = gr.shape[2], gr.shape[3]
            in_specs.append(pl.BlockSpec((None, None, rows, w),
                                         lambda k, cref: (lax.rem(2 * k + cref[0], 4), (2 * k + cref[0]) // 4, 0, 0)))
        else:
            rows, w = gr.shape[0], gr.shape[1] // N_DEV
            in_specs.append(pl.BlockSpec((rows, w), lambda k, cref: (0, 2 * k + cref[0])))
        blk = pl.BlockSpec((None, rows, w), lambda k, cref: (k, 0, 0))
        in_specs.append(blk)
        out_specs.append(blk)
        out_shape.append(jax.ShapeDtypeStruct((N_CHIP, rows, w), BF16))
        args += [gr, rv.reshape(N_CHIP, rows, w)]

    def body(*refs):
        for t in range(n):
            g_ref, r_ref, o_ref = refs[1 + 2 * t], refs[2 + 2 * t], refs[1 + 2 * n + t]
            o_ref[...] = (g_ref[...].astype(F32) + r_ref[...].astype(F32)).astype(BF16)

    return _pallas(body, name=name, n_prefetch=1, grid=(N_CHIP,), in_specs=in_specs, out_specs=out_specs,
                   out_shape=out_shape, params=_params(("parallel",)))(c, *args)


def _adamw_math(g, wv, mv, vv):
    m = ADAM_B1 * mv + (1.0 - ADAM_B1) * g
    v = ADAM_B2 * vv + (1.0 - ADAM_B2) * (g * g)
    m_hat = m / (1.0 - ADAM_B1 ** ADAM_STEP)
    v_hat = v / (1.0 - ADAM_B2 ** ADAM_STEP)
    delta = -ADAM_LR * (m_hat / (jnp.sqrt(v_hat) + ADAM_EPS) + ADAM_WD * wv)
    return delta, m, v


ADAM_STEPS = 2


def _adamw_group(name, items, chip_ids):
    n = len(items)
    in_specs, out_specs, out_shape, args, prevs = [], [], [], [chip_ids], []
    for own, recv, w3, m3, v3, layer, _ in items:
        nl, rows, w = w3.shape
        tr = rows // ADAM_STEPS
        assert tr % 16 == 0, (name, rows)
        in_specs += [pl.BlockSpec((None, tr, w), lambda i, ids, slot=slot: (ids[slot], i, 0)) for slot in range(4)]
        slab = pl.BlockSpec((None, tr, w), lambda i, ids, layer=layer: (layer, i, 0))
        in_specs += [slab] * 3
        out_specs += [slab] * 4
        out_shape += [jax.ShapeDtypeStruct((nl, rows, w), F32)] * 4
        args += [own, recv, recv, recv, w3, m3, v3]
    aliases = {}
    for t, item in enumerate(items):
        if item[6] is not None:
            for k in range(4):
                aliases[len(args) + k] = 4 * t + k
            in_specs += [ANY_SPEC] * 4
            args += list(item[6])
            prevs.append(t)
    n_in = 1 + 7 * n + 4 * len(prevs)

    def body(*refs):
        for t in range(n):
            own_ref, r1_ref, r2_ref, r3_ref, w_ref, m_ref, v_ref = refs[1 + 7 * t:8 + 7 * t]
            g_ref, d_ref, nm_ref, nv_ref = refs[n_in + 4 * t:n_in + 4 * t + 4]
            g = ((own_ref[...].astype(F32) + r1_ref[...].astype(F32)) + r2_ref[...].astype(F32)) + r3_ref[...].astype(F32)
            g_ref[...] = g
            d_ref[...], nm_ref[...], nv_ref[...] = _adamw_math(g, w_ref[...], m_ref[...], v_ref[...])

    outs = _pallas(body, name=name, n_prefetch=1, grid=(ADAM_STEPS,), in_specs=in_specs, out_specs=out_specs,
                   out_shape=out_shape, aliases=aliases, params=_params(("parallel",)))(*args)
    return [list(outs[4 * t:4 * t + 4]) for t in range(n)]


def _adamw_small(gathered, ws, ms, vs):
    n = len(gathered)
    full = [w is not None for w in ws]
    args = list(gathered)
    out_shape = []
    for t in range(n):
        shape = jax.ShapeDtypeStruct(gathered[t].shape[2:], F32)
        if full[t]:
            args += [ws[t], ms[t], vs[t]]
            out_shape += [shape] * 4
        else:
            out_shape += [shape]

    def body(*refs):
        i_in, i_out = n, len(args)
        for t in range(n):
            p_ref = refs[t]
            g = p_ref[0, 0]
            for k in range(1, N_DEV):
                g = g + p_ref[0, k]
            refs[i_out][...] = g
            if full[t]:
                w_ref, m_ref, v_ref = refs[i_in:i_in + 3]
                refs[i_out + 1][...], refs[i_out + 2][...], refs[i_out + 3][...] = _adamw_math(
                    g, w_ref[...], m_ref[...], v_ref[...])
                i_in += 3
                i_out += 4
            else:
                i_out += 1

    outs = _pallas(body, name="adamw_small", in_specs=[VMEM_SPEC] * len(args), out_specs=[VMEM_SPEC] * len(out_shape),
                   out_shape=out_shape, params=pltpu.CompilerParams(vmem_limit_bytes=VMEM_LIMIT))(*args)
    result, i = [], 0
    for t in range(n):
        k = 4 if full[t] else 1
        result.append(list(outs[i:i + k]))
        i += k
    return result


def _adamw_plain(name, gs, ws, ms, vs):
    n = len(gs)

    def body(*refs):
        for t in range(n):
            g_ref, w_ref, m_ref, v_ref = refs[4 * t:4 * t + 4]
            outs = refs[4 * n + 3 * t:4 * n + 3 * t + 3]
            outs[0][...], outs[1][...], outs[2][...] = _adamw_math(g_ref[...], w_ref[...], m_ref[...], v_ref[...])

    args, out_shape = [], []
    for g, w, m, v in zip(gs, ws, ms, vs):
        args += [g, w, m, v]
        out_shape += [jax.ShapeDtypeStruct(w.shape, F32)] * 3
    outs = _pallas(body, name=name, in_specs=[VMEM_SPEC] * len(args), out_specs=[VMEM_SPEC] * len(out_shape),
                   out_shape=out_shape, params=pltpu.CompilerParams(vmem_limit_bytes=VMEM_LIMIT))(*args)
    return [list(outs[3 * t:3 * t + 3]) for t in range(n)]


KIND = {"sc_w_in": "cols", "sc_w_out": "blocked", "w_dkv": "blocked", "w_kr": "blocked", "w_uk": "cols", "w_uv": "cols",
        "w_dq": "blocked", "w_uq": "blocked", "w_o": "blocked", "ffn_w_up": "blocked", "ffn_w_down": "blocked",
        "conv": "blocked"}
GATHER_GROUPS = (("mixer", ("sc_w_in", "sc_w_out", "conv")),
                 ("up0", ("ffn_w_up0",)),
                 ("down0", ("ffn_w_down0",)),
                 ("attn", ("w_dkv", "w_kr", "w_uk", "w_uv", "w_dq", "w_uq", "w_o")),
                 ("ffn1", ("ffn_w_up1", "ffn_w_down1")))
SCATTER_GROUPS = (("ffn1", (("ffn_w_up", 1), ("ffn_w_down", 1))),
                  ("attn", (("w_o", None), ("w_uq", None), ("w_dq", None), ("w_uk", None), ("w_uv", None),
                            ("w_dkv", None), ("w_kr", None))),
                  ("ffn0", (("ffn_w_up", 0), ("ffn_w_down", 0))),
                  ("mixer", (("sc_w_out", None), ("sc_w_in", None))))
SCHEDULE = {
    "begin": (("gather_start", "mixer"),),
    "mixer_ready": (("gather_start", "up0"),),
    "l0_out": (("gather_forward", "up0"), ("gather_start", "down0")),
    "f0_up": (("gather_forward", "down0"), ("gather_start", "attn")),
    "f0_down": (("gather_forward", "attn"), ("gather_start", "ffn1")),
    "attn_fwd": (("gather_forward", "ffn1"),),
    "f1_gup": (("scatter_sibling", "ffn1"),),
    "f1_dhf": (("scatter_chips", "ffn1"),),
    "kv_bwd": (("scatter_sibling", "attn"), ("scatter_done", "ffn1")),
    "f0_dact": (("scatter_chips", "attn"),),
    "f0_gup": (("scatter_sibling", "ffn0"),),
    "f0_dhf": (("scatter_chips", "ffn0"),),
    "f0_bwd": (("scatter_done", "attn"),),
    "sc_bwd": (("scatter_sibling", "mixer"),),
    "d_l0_in": (("scatter_chips", "mixer"),),
}
FINISH = (("scatter_done", "ffn0"), ("scatter_done", "mixer"))
STAGES = {"gather_start": 1, "gather_forward": 2, "gather_done": 3,
          "scatter_sibling": 1, "scatter_chips": 2, "scatter_done": 3}
SMALL_W_ROWS = 24


def _pack(arrays, rows):
    flat = jnp.concatenate([a.reshape(-1).astype(F32) for a in arrays])
    return jnp.pad(flat, (0, rows * 128 - flat.shape[0])).reshape(rows, 128)


def _stored(name, a):
    return jnp.swapaxes(a, -1, -2) if name == "ffn_w_up" else a


def _base(name):
    if name.startswith("ffn_w_") and name[-1] in "01":
        return name[:-1], int(name[-1])
    return name, None


class _Exchange:
    def __init__(self, wts, mom, var, ffn_conv_b):
        self.wts, self.mom, self.var = wts, mom, var
        x, y, c = _place()
        self.me = 4 * x + 2 * y + c
        self.c_arr = jnp.reshape(c, (1,)).astype(jnp.int32)
        chip = 2 * x + y
        self.chip_ids = jnp.stack([chip, chip ^ 1, chip ^ 2, chip ^ 3]).astype(jnp.int32)
        self.ready = {"ffn_cb0": ffn_conv_b.reshape(2, N_FF_BLK, 1, FF_BLK)[0],
                      "ffn_cb1": ffn_conv_b.reshape(2, N_FF_BLK, 1, FF_BLK)[1]}
        self.gathers, self.group_of = {}, {}
        self.grads, self.scatters, self.results, self.queue = {}, {}, {}, []
        for gname, names in GATHER_GROUPS:
            self.gathers[gname] = dict(stage=0, names=names, kinds=[KIND[_base(nm)[0]] for nm in names])
            for nm in names:
                self.group_of[nm] = gname
        for nm in ("sc_conv_w", "ffn_cw0", "ffn_cw1"):
            self.group_of[nm] = "mixer"
        self.at("begin", None)

    def _shard(self, name):
        if name == "conv":
            return _pack([self.wts["sc_conv_w"], self.wts["ffn_conv_w"]], SMALL_W_ROWS).reshape(1, SMALL_W_ROWS, 128)
        base, layer = _base(name)
        a = _stored(base, self.wts[base])
        if layer is not None:
            a = a[layer:layer + 1]
        if KIND[base] == "cols":
            return a.reshape(a.shape[-2], a.shape[-1]).astype(BF16)
        return a.reshape((-1,) + a.shape[-2:]).astype(BF16)

    def _start(self, name, srcs, lands, ncopy, plan, st):
        self.queue.append((name, (srcs, lands, ncopy, plan), st))

    def _flush(self):
        if self.queue:
            flights = _copies_start("__".join(name for name, _, _ in self.queue), [job for _, job, _ in self.queue])
            for (_, _, st), flight in zip(self.queue, flights):
                st["flight"] = flight
            self.queue = []

    def _flight(self, st):
        self._flush()
        return st["flight"]

    def _gather_to(self, gname, stage, after):
        st = self.gathers[gname]
        if st["stage"] < 1 <= stage:
            shards = [self._shard(nm) for nm in st["names"]]
            lands = [_landing(s, kind) for s, kind in zip(shards, st["kinds"])]
            plan, ncopy = _plan_gather_chips(st["kinds"])
            self._start(f"ag_{gname}_chips", shards, lands, ncopy, plan, st)
            st["stage"] = 1
        if st["stage"] < 2 <= stage:
            plan, ncopy = _plan_gather_chips(st["kinds"])
            _, lands = _copies_wait(f"ag_{gname}_chips_wait", self._flight(st), ncopy, plan)
            plan, ncopy = _plan_gather_sibling(st["kinds"])
            self._start(f"ag_{gname}_sibling", [], lands, ncopy, plan, st)
            st["stage"] = 2
        if st["stage"] < 3 <= stage:
            plan, ncopy = _plan_gather_sibling(st["kinds"])
            _, lands = _copies_wait(f"ag_{gname}_sibling_wait", self._flight(st), ncopy, plan)
            for nm, land in zip(st["names"], lands):
                self._arrived(nm, land)
            st["stage"] = 3

    def _arrived(self, name, land):
        if name == "conv":
            conv = land.reshape(N_DEV, SMALL_W_ROWS * 128)
            self.ready["sc_conv_w"] = conv[:, :3 * 128].reshape(N_DEV, 3, 128).transpose(1, 0, 2).reshape(3, D)
            fcw = conv[:, 3 * 128:3 * 128 + 6 * 352].reshape(N_DEV, 2, 3, 352).transpose(1, 2, 0, 3)
            fcw = fcw.reshape(2, 3, N_FF_BLK, FF_BLK).transpose(0, 2, 1, 3)
            self.ready["ffn_cw0"], self.ready["ffn_cw1"] = fcw[0], fcw[1]
        elif name in ("sc_w_in", "w_uk", "w_uv") or name.startswith("ffn_w_up"):
            self.ready[name] = land
        elif name.startswith("ffn_w_down"):
            self.ready[name] = land.reshape(1, N_FF_BLK, FF_BLK, D)
        elif name == "w_kr":
            self.ready[name] = jnp.pad(land.reshape(D, QK_ROPE), ((0, 0), (0, 128 - QK_ROPE)))
        elif name == "w_uq":
            self.ready[name] = jnp.pad(land.reshape(N_HEADS, Q_LORA, QK_NOPE + QK_ROPE),
                                       ((0, 0), (0, 0), (0, QK_PAD - QK_NOPE - QK_ROPE)))
        else:
            self.ready[name] = land.reshape(D, land.shape[-1])

    def need(self, name, after):
        if name not in self.ready:
            self._gather_to(self.group_of[name], 3, after)
            self._flush()
        return self.ready[name]

    def grad(self, name, layer, array):
        self.grads[(name, layer)] = array

    def _scatter_to(self, gname, stage, after):
        keys = dict(SCATTER_GROUPS)[gname]
        st = self.scatters.setdefault(gname, dict(stage=0))
        kinds = ["halves" if nm == "ffn_w_up" else KIND[nm] for nm, _ in keys]
        if st["stage"] < 1 <= stage:
            grads = [self.grads[key] for key in keys]
            lands = []
            for gr, kind in zip(grads, kinds):
                shard = {"blocked": (gr.shape[0],) + gr.shape[2:], "halves": gr.shape[2:],
                         "cols": (gr.shape[0], gr.shape[1] // N_DEV)}[kind]
                lands.append(lax.empty((N_CHIP,) + shard, BF16))
            plan, ncopy = _plan_scatter_sibling(kinds)
            self._start(f"rs_{gname}_sibling", grads, lands, ncopy, plan, st)
            st["stage"] = 1
        if st["stage"] < 2 <= stage:
            plan, ncopy = _plan_scatter_sibling(kinds)
            grads, recvs = _copies_wait(f"rs_{gname}_sibling_wait", self._flight(st), ncopy, plan)
            sums = _chip_sums(f"rs_{gname}_sums", grads, kinds, recvs, self.c_arr)
            lands = [lax.empty(s.shape, BF16) for s in sums]
            plan, ncopy = _plan_scatter_chips(len(sums))
            self._start(f"rs_{gname}_chips", sums, lands, ncopy, plan, st)
            st["stage"] = 2
        if st["stage"] < 3 <= stage:
            plan, ncopy = _plan_scatter_chips(len(keys))
            sums, recvs = _copies_wait(f"rs_{gname}_chips_wait", self._flight(st), ncopy, plan)
            items = []
            for (nm, layer), own, rv in zip(keys, sums, recvs):
                nl = 1 if layer is None else 2
                rows, w = own.shape[1], own.shape[2]
                w3, m3, v3 = (_stored(nm, src[nm]).reshape(nl, rows, w) for src in (self.wts, self.mom, self.var))
                items.append((own, rv, w3, m3, v3, 0 if layer is None else layer, self.results.get(nm)))
            outs = _adamw_group(f"adamw_{gname}", items, self.chip_ids)
            for (nm, _), out in zip(keys, outs):
                self.results[nm] = out
            st["stage"] = 3

    def at(self, place, after):
        for action, gname in SCHEDULE.get(place, ()):
            self._advance(action, gname, after)
        self._flush()

    def _advance(self, action, gname, after):
        if action.startswith("gather"):
            self._gather_to(gname, STAGES[action], after)
        else:
            self._scatter_to(gname, STAGES[action], after)

    def finish(self, after):
        for action, gname in FINISH:
            self._advance(action, gname, after)
        for gname, _ in SCATTER_GROUPS:
            self._scatter_to(gname, 3, after)
        return {nm: [_stored(nm, o.reshape(_stored(nm, self.wts[nm]).shape)) for o in outs]
                for nm, outs in self.results.items()}


REPLICATED = ("attn_norm", "ffn_norm", "final_norm", "kv_in_norm", "kv_latent_norm", "q_latent_norm", "ffn_conv_b")
WEIGHTS = ("attn_norm", "ffn_norm", "final_norm", "sc_w_in", "sc_conv_w", "sc_w_out", "kv_in_norm", "w_dkv",
           "kv_latent_norm", "w_kr", "w_uk", "w_uv", "w_dq", "q_latent_norm", "w_uq", "w_o", "ffn_w_up", "ffn_conv_w",
           "ffn_conv_b", "ffn_w_down")


def kernel(x, positions, attn_norm, ffn_norm, final_norm, sc_w_in, sc_conv_w, sc_w_out, kv_in_norm, w_dkv, kv_latent_norm, w_kr, w_uk, w_uv, w_dq, q_latent_norm, w_uq, w_o, ffn_w_up, ffn_conv_w, ffn_conv_b, ffn_w_down, loss_target, m_attn_norm, m_ffn_norm, m_final_norm, m_sc_w_in, m_sc_conv_w, m_sc_w_out, m_kv_in_norm, m_w_dkv, m_kv_latent_norm, m_w_kr, m_w_uk, m_w_uv, m_w_dq, m_q_latent_norm, m_w_uq, m_w_o, m_ffn_w_up, m_ffn_conv_w, m_ffn_conv_b, m_ffn_w_down, v_attn_norm, v_ffn_norm, v_final_norm, v_sc_w_in, v_sc_conv_w, v_sc_w_out, v_kv_in_norm, v_w_dkv, v_kv_latent_norm, v_w_kr, v_w_uk, v_w_uv, v_w_dq, v_q_latent_norm, v_w_uq, v_w_o, v_ffn_w_up, v_ffn_conv_w, v_ffn_conv_b, v_ffn_w_down):
    wts = dict(attn_norm=attn_norm, ffn_norm=ffn_norm, final_norm=final_norm, sc_w_in=sc_w_in, sc_conv_w=sc_conv_w,
               sc_w_out=sc_w_out, kv_in_norm=kv_in_norm, w_dkv=w_dkv, kv_latent_norm=kv_latent_norm, w_kr=w_kr,
               w_uk=w_uk, w_uv=w_uv, w_dq=w_dq, q_latent_norm=q_latent_norm, w_uq=w_uq, w_o=w_o, ffn_w_up=ffn_w_up,
               ffn_conv_w=ffn_conv_w, ffn_conv_b=ffn_conv_b, ffn_w_down=ffn_w_down)
    mom = dict(attn_norm=m_attn_norm, ffn_norm=m_ffn_norm, final_norm=m_final_norm, sc_w_in=m_sc_w_in,
               sc_conv_w=m_sc_conv_w, sc_w_out=m_sc_w_out, kv_in_norm=m_kv_in_norm, w_dkv=m_w_dkv,
               kv_latent_norm=m_kv_latent_norm, w_kr=m_w_kr, w_uk=m_w_uk, w_uv=m_w_uv, w_dq=m_w_dq,
               q_latent_norm=m_q_latent_norm, w_uq=m_w_uq, w_o=m_w_o, ffn_w_up=m_ffn_w_up, ffn_conv_w=m_ffn_conv_w,
               ffn_conv_b=m_ffn_conv_b, ffn_w_down=m_ffn_w_down)
    var = dict(attn_norm=v_attn_norm, ffn_norm=v_ffn_norm, final_norm=v_final_norm, sc_w_in=v_sc_w_in,
               sc_conv_w=v_sc_conv_w, sc_w_out=v_sc_w_out, kv_in_norm=v_kv_in_norm, w_dkv=v_w_dkv,
               kv_latent_norm=v_kv_latent_norm, w_kr=v_w_kr, w_uk=v_w_uk, w_uv=v_w_uv, w_dq=v_w_dq,
               q_latent_norm=v_q_latent_norm, w_uq=v_w_uq, w_o=v_w_o, ffn_w_up=v_ffn_w_up, ffn_conv_w=v_ffn_conv_w,
               ffn_conv_b=v_ffn_conv_b, ffn_w_down=v_ffn_w_down)
    xi, yi, ci = _place()
    me = 4 * xi + 2 * yi + ci
    _Chain.last = None

    ex = _Exchange(wts, mom, var, ffn_conv_b)
    rep = {
        "attn_norm": attn_norm, "ffn_norm": ffn_norm, "final_norm": final_norm,
        "kv_in_norm": kv_in_norm.reshape(1, D), "kv_latent_norm": kv_latent_norm.reshape(1, KV_LORA),
        "q_latent_norm": q_latent_norm.reshape(1, Q_LORA),
    }
    loss, grad_x, small = _local_step(x.reshape(T, D), positions.reshape(T, 1), loss_target.reshape(T, D), rep, ex)

    def rows_of(a):
        return a.reshape(-1, a.shape[-1])

    small_order = list(REPLICATED) + ["sc_conv_w", "ffn_conv_w"]
    shards = [loss.reshape(1, 1, 128)] + [rows_of(small[nm])[None] for nm in small_order]
    plan, ncopy = _plan_gather_all(len(shards))
    flight, = _copies_start("ag_small", [(shards, [lax.empty((1, N_DEV) + s.shape[1:], F32) for s in shards], ncopy, plan)])
    results = ex.finish(grad_x)
    _, gathered = _copies_wait("ag_small_wait", flight, ncopy, plan)
    params = [[None] + [rows_of(src[nm]) for nm in REPLICATED] + [None, None] for src in (wts, mom, var)]
    summed = _adamw_small(gathered, *params)
    loss_total = summed[0][0][0, 0]
    for nm, vals in zip(REPLICATED, summed[1:1 + len(REPLICATED)]):
        results[nm] = [a.reshape(wts[nm].shape) for a in vals]
    g_scw = lax.dynamic_slice(summed[-2][0], (0, me * 128), (3, 128))
    g_fcw = lax.dynamic_slice(summed[-1][0], (0, me * 352), (6, 352))
    conv = _adamw_plain("adamw_conv", [g_scw, g_fcw], *[[rows_of(src["sc_conv_w"]), rows_of(src["ffn_conv_w"])]
                                                        for src in (wts, mom, var)])
    for nm, g_own, vals in zip(("sc_conv_w", "ffn_conv_w"), (g_scw, g_fcw), conv):
        results[nm] = [a.reshape(wts[nm].shape) for a in [g_own] + vals]

    outs = [loss_total, grad_x.reshape(1, T, D)]
    for slot in range(4):
        outs.extend(results[nm][slot] for nm in WEIGHTS)
    return tuple(outs)
```

```python
import jax
import jax.numpy as jnp
from jax import lax
from jax.experimental import pallas as pl
from jax.experimental.pallas import tpu as pltpu

F32 = jnp.float32
BF16 = jnp.bfloat16

T = 2048
D = 1024
N_HEADS = 8
QK_NOPE = 128
QK_ROPE = 64
V_HEAD = 128
Q_LORA = 384
KV_LORA = 256
D_FF = 2816
CHUNK = 64
ROPE_THETA = 10000.0
EPS = 1e-6
NEG_INF = -1e30
ADAM_LR = 0.001
ADAM_B1 = 0.9
ADAM_B2 = 0.999
ADAM_EPS = 1e-08
ADAM_WD = 0.01
ADAM_STEP = 10

N_DEV = 8
N_CHIP = 4
FF_BLK = D_FF * 2 // N_DEV
N_FF_BLK = D_FF // FF_BLK
QK_PAD = 256
HALO = 16

TM = 1024
TS = 512
TR = 256
TQ = 512
VMEM_LIMIT = 56 * 1024 * 1024

NN = (((1,), (0,)), ((), ()))
NT = (((1,), (1,)), ((), ()))
TN = (((0,), (0,)), ((), ()))
MESH = pl.DeviceIdType.MESH


def _params(sem):
    return pltpu.CompilerParams(dimension_semantics=sem, vmem_limit_bytes=VMEM_LIMIT)


ANY_SPEC = pl.BlockSpec(memory_space=pl.ANY)
VMEM_SPEC = pl.BlockSpec(memory_space=pltpu.VMEM)


class _Chain:
    last = None


def _pallas(body, *, name, in_specs, out_specs, out_shape, grid=(), scratch_shapes=(), n_prefetch=0, aliases=None,
            params=None):
    def run(*args):
        after = _Chain.last
        n_lead = len(args)
        specs, operands, fn = list(in_specs), list(args), body
        if after is not None:
            def fn(*refs):
                return body(*refs[:n_lead], *refs[n_lead + 1:])
            specs.append(ANY_SPEC)
            operands.append(after)
        kw = dict(name=name, out_shape=out_shape, input_output_aliases=aliases or {})
        if params is not None:
            kw["compiler_params"] = params
        if n_prefetch:
            kw["grid_spec"] = pltpu.PrefetchScalarGridSpec(
                num_scalar_prefetch=n_prefetch, grid=grid, in_specs=specs, out_specs=out_specs,
                scratch_shapes=scratch_shapes)
        else:
            kw.update(grid=grid, in_specs=specs, out_specs=out_specs, scratch_shapes=scratch_shapes)
        outs = pl.pallas_call(fn, **kw)(*operands)
        _Chain.last = outs[0] if isinstance(outs, (list, tuple)) else outs
        return outs
    return run


def _mm(name, a, b, *, grid, a_spec, b_spec, o_spec, o_shape, o_dtype, dims, k_axis=None, acc_shape=None,
        add=None, add_spec=None):
    nk = grid[k_axis] if k_axis is not None else 1
    has_add = add is not None

    def body(*refs):
        a_ref, b_ref = refs[0], refs[1]
        p = 2
        add_ref = None
        if has_add:
            add_ref = refs[p]
            p += 1
        o_ref = refs[p]
        p += 1
        r = lax.dot_general(a_ref[...].astype(BF16), b_ref[...].astype(BF16), dims, preferred_element_type=F32)
        if k_axis is None:
            if has_add:
                r = r + add_ref[...].astype(F32)
            o_ref[...] = r.astype(o_dtype)
        else:
            acc = refs[p]
            k = pl.program_id(k_axis)

            @pl.when(k == 0)
            def _():
                acc[...] = r

            @pl.when(k > 0)
            def _():
                acc[...] += r

            @pl.when(k == nk - 1)
            def _():
                t = acc[...]
                if has_add:
                    t = t + add_ref[...].astype(F32)
                o_ref[...] = t.astype(o_dtype)

    in_specs = [a_spec, b_spec]
    args = [a, b]
    if has_add:
        in_specs.append(add_spec if add_spec is not None else o_spec)
        args.append(add)
    sem = tuple("arbitrary" if ax == k_axis else "parallel" for ax in range(len(grid)))
    scratch = [pltpu.VMEM(acc_shape, F32)] if k_axis is not None else []
    return _pallas(body, name=name, grid=grid, in_specs=in_specs, out_specs=o_spec,
                   out_shape=jax.ShapeDtypeStruct(o_shape, o_dtype), scratch_shapes=scratch, params=_params(sem))(*args)


def _mm_sum(name, parts, *, grid, o_spec, o_shape, o_dtype, add=None, norm_bwd=None):
    has_add = add is not None
    np_ = len(parts)
    nn = 1 if norm_bwd is None else len(norm_bwd[1])
    has_res = norm_bwd is not None and norm_bwd[2] is not None

    def body(*refs):
        accs = [None] * nn
        for p, (_, _, _, _, dims, n) in enumerate(parts):
            a_ref, b_ref = refs[2 * p], refs[2 * p + 1]
            for k in range(a_ref.shape[0]):
                r = lax.dot_general(a_ref[k], b_ref[k], dims, preferred_element_type=F32)
                accs[n] = r if accs[n] is None else accs[n] + r
        if norm_bwd is None:
            acc = accs[0]
            if has_add:
                acc = acc + refs[2 * np_][...]
            refs[-1][...] = acc.astype(o_dtype)
            return
        x_ref, g_refs = refs[2 * np_], refs[2 * np_ + 1:2 * np_ + 1 + nn]
        dx_ref, dxb_ref, dg_refs = refs[-2 - nn], refs[-1 - nn], refs[-nn:]
        xv = x_ref[...]
        r = lax.rsqrt(jnp.mean(xv * xv, axis=-1, keepdims=True) + EPS)
        xn = xv * r
        dx = refs[2 * np_ + 1 + nn][...] if has_res else None
        sums = []
        for acc, g_ref in zip(accs, g_refs):
            gdy = acc * g_ref[...]
            t = r * (gdy - xn * jnp.mean(gdy * xn, axis=-1, keepdims=True))
            dx = t if dx is None else dx + t
            sums.append(jnp.sum(acc * xn, axis=0, keepdims=True))
        dx_ref[...] = dx
        dxb_ref[...] = dx.astype(BF16)

        @pl.when(pl.program_id(0) == 0)
        def _():
            for dg_ref, part in zip(dg_refs, sums):
                dg_ref[...] = part

        @pl.when(pl.program_id(0) > 0)
        def _():
            for dg_ref, part in zip(dg_refs, sums):
                dg_ref[...] += part

    in_specs, args = [], []
    for a, a_spec, b, b_spec, _, _ in parts:
        in_specs += [a_spec, b_spec]
        args += [a, b]
    if norm_bwd is None:
        if has_add:
            in_specs.append(o_spec)
            args.append(add)
        return _pallas(body, name=name, grid=grid, in_specs=in_specs, out_specs=o_spec,
                       out_shape=jax.ShapeDtypeStruct(o_shape, o_dtype),
                       params=_params(("parallel",) * len(grid)))(*args)
    x, gains, dres = norm_bwd
    vec = pl.BlockSpec((1, o_shape[1]), lambda i: (0, 0))
    in_specs += [o_spec] + [vec] * nn + ([o_spec] if has_res else [])
    args += [x] + list(gains) + ([dres] if has_res else [])
    outs = _pallas(body, name=name, grid=grid, in_specs=in_specs, out_specs=[o_spec, o_spec] + [vec] * nn,
                   out_shape=[jax.ShapeDtypeStruct(o_shape, F32), jax.ShapeDtypeStruct(o_shape, BF16)]
                   + [jax.ShapeDtypeStruct((1, o_shape[1]), F32)] * nn,
                   params=_params(("arbitrary",)))(*args)
    return outs[0], outs[1], list(outs[2:])


def _mm_rows(name, a, b, dims, o_dtype, n_out, *, tn=None, add=None):
    k = a.shape[1]
    tn = n_out if tn is None else tn
    if dims == NN:
        b_spec = pl.BlockSpec((k, tn), lambda n, i: (0, n))
    else:
        b_spec = pl.BlockSpec((tn, k), lambda n, i: (n, 0))
    return _mm(name, a, b, grid=(n_out // tn, T // TM),
               a_spec=pl.BlockSpec((TM, k), lambda n, i: (i, 0)), b_spec=b_spec,
               o_spec=pl.BlockSpec((TM, tn), lambda n, i: (i, n)), o_shape=(T, n_out), o_dtype=o_dtype,
               dims=dims, add=add)


def _mm_wgrad(name, a, b, *, tn=512):
    k, n = a.shape[1], b.shape[1]
    tn = min(tn, n)
    return _mm(name, a, b, grid=(n // tn,),
               a_spec=pl.BlockSpec((T, k), lambda j: (0, 0)), b_spec=pl.BlockSpec((T, tn), lambda j: (0, j)),
               o_spec=pl.BlockSpec((k, tn), lambda j: (0, j)), o_shape=(k, n), o_dtype=BF16, dims=TN)


def _rms_fwd(name, x, g):
    d = x.shape[1]

    def body(x_ref, g_ref, o_ref):
        xv = x_ref[...]
        r = lax.rsqrt(jnp.mean(xv * xv, axis=-1, keepdims=True) + EPS)
        o_ref[...] = ((xv * r) * g_ref[...]).astype(BF16)

    return _pallas(
        body, name=name, grid=(T // TM,),
        in_specs=[pl.BlockSpec((TM, d), lambda i: (i, 0)), pl.BlockSpec((1, d), lambda i: (0, 0))],
        out_specs=pl.BlockSpec((TM, d), lambda i: (i, 0)),
        out_shape=jax.ShapeDtypeStruct((T, d), BF16), params=_params(("parallel",)))(x, g)


def _rows_call(name, body, row_ins, whole_ins, outs):
    in_specs = [pl.BlockSpec((TM, a.shape[1]), lambda i: (i, 0)) for a in row_ins]
    in_specs += [pl.BlockSpec(a.shape, lambda i: (0, 0)) for a in whole_ins]
    return _pallas(
        body, name=name, grid=(T // TM,), in_specs=in_specs,
        out_specs=[pl.BlockSpec((TM, d), lambda i: (i, 0)) for d, _ in outs],
        out_shape=[jax.ShapeDtypeStruct((T, d), dt) for d, dt in outs],
        params=_params(("parallel",)))(*row_ins, *whole_ins)


def _rms(xv, g):
    return (xv * lax.rsqrt(jnp.mean(xv * xv, axis=-1, keepdims=True) + EPS)) * g


def _rms_fwd2(name, x, g1, g2):
    d = x.shape[1]

    def body(x_ref, g1_ref, g2_ref, o1_ref, o2_ref):
        xv = x_ref[...]
        xn = xv * lax.rsqrt(jnp.mean(xv * xv, axis=-1, keepdims=True) + EPS)
        o1_ref[...] = (xn * g1_ref[...]).astype(BF16)
        o2_ref[...] = (xn * g2_ref[...]).astype(BF16)

    return _rows_call(name, body, [x], [g1, g2], [(d, BF16), (d, BF16)])


def _down_norm(name, a, w, g):
    n = w.shape[1]

    def body(a_ref, w_ref, g_ref, raw_ref, o_ref):
        raw = lax.dot_general(a_ref[...], w_ref[...], NN, preferred_element_type=F32)
        raw_ref[...] = raw
        o_ref[...] = _rms(raw, g_ref[...]).astype(BF16)

    return _rows_call(name, body, [a], [w, g], [(n, F32), (n, BF16)])


def _kv_down(hk, w_dkv, w_kr, g, tables):
    def body(a_ref, c_ref, sa_ref, sb_ref, wd_ref, wr_ref, g_ref, raw_ref, ckv_ref, kr_ref):
        av = a_ref[...]
        raw = lax.dot_general(av, wd_ref[...], NN, preferred_element_type=F32)
        raw_ref[...] = raw
        ckv_ref[...] = _rms(raw, g_ref[...]).astype(BF16)
        kr = lax.dot_general(av, wr_ref[...], NT, preferred_element_type=F32)
        kr_ref[...] = _rotate(kr, c_ref[...], sa_ref[...], sb_ref[...], 1.0).astype(BF16)

    return _rows_call("kv_down", body, [hk, *tables], [w_dkv, w_kr, g], [(KV_LORA, F32), (KV_LORA, BF16), (128, BF16)])


def _kv_up(ckv, w_uk, w_uv):
    def body(a_ref, wk_ref, wv_ref, k_ref, v_ref):
        av = a_ref[...]
        k_ref[...] = lax.dot_general(av, wk_ref[...], NN, preferred_element_type=F32).astype(BF16)
        v_ref[...] = lax.dot_general(av, wv_ref[...], NN, preferred_element_type=F32).astype(BF16)

    return _rows_call("kv_up", body, [ckv], [w_uk, w_uv], [(N_HEADS * QK_NOPE, BF16), (N_HEADS * V_HEAD, BF16)])


def _rms_bwd(name, x, gains, dys, dres=None):
    d = x.shape[1]
    n = len(gains)
    has_res = dres is not None

    def body(*refs):
        x_ref, g_refs, dy_refs = refs[0], refs[1:1 + n], refs[1 + n:1 + 2 * n]
        dx_ref, dxb_ref = refs[-2 - n], refs[-1 - n]
        dg_refs = refs[-n:]
        xv = x_ref[...]
        r = lax.rsqrt(jnp.mean(xv * xv, axis=-1, keepdims=True) + EPS)
        xn = xv * r
        dx = refs[1 + 2 * n][...] if has_res else None
        parts = []
        for g_ref, dy_ref in zip(g_refs, dy_refs):
            dyv = dy_ref[...].astype(F32)
            gdy = dyv * g_ref[...]
            t = r * (gdy - xn * jnp.mean(gdy * xn, axis=-1, keepdims=True))
            dx = t if dx is None else dx + t
            parts.append(jnp.sum(dyv * xn, axis=0, keepdims=True))
        dx_ref[...] = dx
        dxb_ref[...] = dx.astype(BF16)

        @pl.when(pl.program_id(0) == 0)
        def _():
            for dg_ref, part in zip(dg_refs, parts):
                dg_ref[...] = part

        @pl.when(pl.program_id(0) > 0)
        def _():
            for dg_ref, part in zip(dg_refs, parts):
                dg_ref[...] += part

    row = pl.BlockSpec((TR, d), lambda i: (i, 0))
    vec = pl.BlockSpec((1, d), lambda i: (0, 0))
    args = [x] + list(gains) + list(dys) + ([dres] if has_res else [])
    in_specs = [row] + [vec] * n + [row] * n + ([row] if has_res else [])
    outs = _pallas(
        body, name=name, grid=(T // TR,), in_specs=in_specs, out_specs=[row, row] + [vec] * n,
        out_shape=[jax.ShapeDtypeStruct((T, d), F32), jax.ShapeDtypeStruct((T, d), BF16)]
        + [jax.ShapeDtypeStruct((1, d), F32)] * n,
        params=_params(("arbitrary",)))(*args)
    return outs[0], outs[1], list(outs[2:])


def _final(h, g, tgt):
    def body(h_ref, g_ref, t_ref, loss_ref, dh_ref, dhb_ref, dg_ref):
        hv = h_ref[...]
        r = lax.rsqrt(jnp.mean(hv * hv, axis=-1, keepdims=True) + EPS)
        xn = hv * r
        gv = g_ref[...]
        err = xn * gv - t_ref[...]
        part_loss = 0.5 * jnp.sum(jnp.mean(err * err, axis=-1, keepdims=True), axis=0, keepdims=True)
        dy = err * (1.0 / D)
        gdy = dy * gv
        dh = r * (gdy - xn * jnp.mean(gdy * xn, axis=-1, keepdims=True))
        dh_ref[...] = dh
        dhb_ref[...] = dh.astype(BF16)
        part = jnp.sum(dy * xn, axis=0, keepdims=True)
        first = pl.program_id(0) == 0

        @pl.when(first)
        def _():
            dg_ref[...] = part
            loss_ref[...] = jnp.broadcast_to(part_loss, (1, 128))

        @pl.when(jnp.logical_not(first))
        def _():
            dg_ref[...] += part
            loss_ref[...] += jnp.broadcast_to(part_loss, (1, 128))

    row = pl.BlockSpec((TR, D), lambda i: (i, 0))
    vec = pl.BlockSpec((1, D), lambda i: (0, 0))
    return _pallas(
        body, name="final_loss", grid=(T // TR,), in_specs=[row, vec, row],
        out_specs=[pl.BlockSpec((1, 128), lambda i: (0, 0)), row, row, vec],
        out_shape=[jax.ShapeDtypeStruct((1, 128), F32), jax.ShapeDtypeStruct((T, D), F32),
                   jax.ShapeDtypeStruct((T, D), BF16), jax.ShapeDtypeStruct((1, D), F32)],
        params=_params(("arbitrary",)))(h, g, tgt)


def _prev_idx(i, rows=TR):
    return jnp.maximum(i * (rows // HALO) - 1, 0)


def _next_idx(i, rows=TR):
    return jnp.minimum((i + 1) * (rows // HALO), T // HALO - 1)


def _causal_taps(ext):
    return pltpu.roll(ext, 2, 0)[HALO:], pltpu.roll(ext, 1, 0)[HALO:], ext[HALO:]


def _anticausal_taps(ext, n):
    rows = ext.shape[0]
    return pltpu.roll(ext, rows - 1, 0)[:n], pltpu.roll(ext, rows - 2, 0)[:n]


MIX_COLS = 512


def _mixer_in(hn, w_in, w):
    nc = D // MIX_COLS

    def body(h_ref, hh_ref, wb_ref, wc_ref, wu_ref, w_ref, b_ref, c_ref, u_ref, y_ref):
        i = pl.program_id(1)
        hv = h_ref[...]
        he = jnp.concatenate([hh_ref[...], hv], axis=0)
        ce = lax.dot_general(he, wc_ref[...], NN, preferred_element_type=F32).astype(BF16)
        ue = lax.dot_general(he, wu_ref[...], NN, preferred_element_type=F32).astype(BF16)
        bv = lax.dot_general(hv, wb_ref[...], NN, preferred_element_type=F32).astype(BF16)
        b_ref[...] = bv
        c_ref[...] = ce[HALO:]
        u_ref[...] = ue[HALO:]
        row = lax.broadcasted_iota(jnp.int32, (HALO + TS, 1), 0)
        cu = jnp.where(jnp.logical_or(i > 0, row >= HALO), ce.astype(F32) * ue.astype(F32), 0.0)
        x2, x1, x0 = _causal_taps(cu)
        wv = w_ref[...]
        cv = (x2 * wv[0:1] + x1 * wv[1:2]) + x0 * wv[2:3]
        y_ref[...] = (bv.astype(F32) * cv).astype(BF16)

    def cols(part):
        return pl.BlockSpec((D, MIX_COLS), lambda j, i: (0, part * nc + j))

    blk = pl.BlockSpec((TS, MIX_COLS), lambda j, i: (i, j))
    out = jax.ShapeDtypeStruct((T, D), BF16)
    return _pallas(
        body, name="l0_in", grid=(nc, T // TS),
        in_specs=[pl.BlockSpec((TS, D), lambda j, i: (i, 0)), pl.BlockSpec((HALO, D), lambda j, i: (_prev_idx(i, TS), 0)),
                  cols(0), cols(1), cols(2), pl.BlockSpec((3, MIX_COLS), lambda j, i: (0, j))],
        out_specs=[blk] * 4, out_shape=[out] * 4,
        params=_params(("parallel", "parallel")))(hn, hn, w_in, w_in, w_in, w)


def _mixer_out_bwd(dh, w_out, zb, zc, zu, w):
    last = T // TR - 1

    def body(dh_ref, dhn_ref, wo_ref, b_ref, bn_ref, c_ref, ch_ref, u_ref, uh_ref, w_ref, dz_ref, dw_ref):
        i = pl.program_id(0)
        dye = lax.dot_general(jnp.concatenate([dh_ref[...], dhn_ref[...]], axis=0), wo_ref[...], NT,
                              preferred_element_type=F32)
        cv_ = c_ref[...].astype(F32)
        uv = u_ref[...].astype(F32)
        cu = cv_ * uv
        cuh = jnp.where(i > 0, ch_ref[...].astype(F32) * uh_ref[...].astype(F32), 0.0)
        x2, x1, x0 = _causal_taps(jnp.concatenate([cuh, cu], axis=0))
        wv = w_ref[...]
        conv = (x2 * wv[0:1] + x1 * wv[1:2]) + x0 * wv[2:3]
        dyv = dye[:TR]
        dz_ref[:, 0:D] = (dyv * conv).astype(BF16)
        dconv = dyv * b_ref[...].astype(F32)
        dconv_n = jnp.where(i < last, dye[TR:] * bn_ref[...].astype(F32), 0.0)
        n1, n2 = _anticausal_taps(jnp.concatenate([dconv, dconv_n], axis=0), TR)
        dcu = (dconv * wv[2:3] + n1 * wv[1:2]) + n2 * wv[0:1]
        dz_ref[:, D:2 * D] = (dcu * uv).astype(BF16)
        dz_ref[:, 2 * D:3 * D] = (dcu * cv_).astype(BF16)
        part = jnp.concatenate([jnp.sum(dconv * x2, axis=0, keepdims=True),
                                jnp.sum(dconv * x1, axis=0, keepdims=True),
                                jnp.sum(dconv * x0, axis=0, keepdims=True)], axis=0)

        @pl.when(i == 0)
        def _():
            dw_ref[...] = part

        @pl.when(i > 0)
        def _():
            dw_ref[...] += part

    main = pl.BlockSpec((TR, D), lambda i: (i, 0))
    prev = pl.BlockSpec((HALO, D), lambda i: (_prev_idx(i), 0))
    nxt = pl.BlockSpec((HALO, D), lambda i: (_next_idx(i), 0))
    wspec = pl.BlockSpec((3, D), lambda i: (0, 0))
    return _pallas(
        body, name="d_l0_out", grid=(T // TR,),
        in_specs=[main, nxt, pl.BlockSpec((D, D), lambda i: (0, 0)), main, nxt, main, prev, main, prev, wspec],
        out_specs=[pl.BlockSpec((TR, 3 * D), lambda i: (i, 0)), wspec],
        out_shape=[jax.ShapeDtypeStruct((T, 3 * D), BF16), jax.ShapeDtypeStruct((3, D), F32)],
        params=_params(("arbitrary",)))(dh, dh, w_out, zb, zb, zc, zc, zu, zu, w)


def _sigmoid(x):
    return 0.5 * jnp.tanh(0.5 * x) + 0.5


def _ffn_up_act(name, hf, w_up, w, b):
    def body(h_ref, hh_ref, wg_ref, wv_ref, w_ref, b_ref, g_ref, v_ref, a_ref):
        i = pl.program_id(1)
        hv = h_ref[...]
        ge = lax.dot_general(jnp.concatenate([hh_ref[...], hv], axis=0), wg_ref[...], NT,
                             preferred_element_type=F32).astype(BF16)
        v = lax.dot_general(hv, wv_ref[...], NT, preferred_element_type=F32).astype(BF16)
        g_ref[...] = ge[HALO:]
        v_ref[...] = v
        ext = ge.astype(F32)
        row = lax.broadcasted_iota(jnp.int32, (HALO + TS, 1), 0)
        ext = jnp.where(jnp.logical_or(i > 0, row >= HALO), ext, 0.0)
        x2, x1, x0 = _causal_taps(ext)
        wv = w_ref[...]
        gc = ((x2 * wv[0:1] + x1 * wv[1:2]) + x0 * wv[2:3]) + b_ref[...]
        a_ref[...] = ((gc * _sigmoid(gc)) * v.astype(F32)).astype(BF16)

    blk = pl.BlockSpec((None, TS, FF_BLK), lambda j, i: (j, i, 0))
    out = jax.ShapeDtypeStruct((N_FF_BLK, T, FF_BLK), BF16)
    return _pallas(
        body, name=name, grid=(N_FF_BLK, T // TS),
        in_specs=[pl.BlockSpec((TS, D), lambda j, i: (i, 0)),
                  pl.BlockSpec((HALO, D), lambda j, i: (_prev_idx(i, TS), 0)),
                  pl.BlockSpec((None, None, FF_BLK, D), lambda j, i: (0, j, 0, 0)),
                  pl.BlockSpec((None, None, FF_BLK, D), lambda j, i: (0, j + N_FF_BLK, 0, 0)),
                  pl.BlockSpec((None, 3, FF_BLK), lambda j, i: (j, 0, 0)),
                  pl.BlockSpec((None, 1, FF_BLK), lambda j, i: (j, 0, 0))],
        out_specs=[blk, blk, blk], out_shape=[out, out, out],
        params=_params(("parallel", "parallel")))(hf, hf, w_up, w_up, w, b)


def _ffn_dact(name, dh, w_down4, g, v, w, b):
    last = T // TS - 1

    def body(dh_ref, dhn_ref, wd_ref, g_ref, gp_ref, gn_ref, v_ref, vn_ref, w_ref, b_ref, dg_ref, dv_ref, dw_ref, db_ref):
        i = pl.program_id(1)
        da = lax.dot_general(jnp.concatenate([dh_ref[...], dhn_ref[...]], axis=0), wd_ref[...], NT,
                             preferred_element_type=F32)
        row = lax.broadcasted_iota(jnp.int32, (TS + HALO, 1), 0)
        da = jnp.where(jnp.logical_or(i < last, row < TS), da, 0.0)
        gp = jnp.where(i > 0, gp_ref[...].astype(F32), 0.0)
        ext = jnp.concatenate([gp, g_ref[...].astype(F32), gn_ref[...].astype(F32)], axis=0)
        x2, x1, x0 = _causal_taps(ext)
        wv = w_ref[...]
        gc = ((x2 * wv[0:1] + x1 * wv[1:2]) + x0 * wv[2:3]) + b_ref[...]
        sg = _sigmoid(gc)
        vv = jnp.concatenate([v_ref[...].astype(F32), vn_ref[...].astype(F32)], axis=0)
        dv_ref[...] = (da[:TS] * (gc[:TS] * sg[:TS])).astype(BF16)
        dgc = (da * vv) * (sg * (1.0 + gc * (1.0 - sg)))
        n1, n2 = _anticausal_taps(dgc, TS)
        d0 = dgc[:TS]
        dg_ref[...] = ((d0 * wv[2:3] + n1 * wv[1:2]) + n2 * wv[0:1]).astype(BF16)
        part_w = jnp.concatenate([jnp.sum(d0 * x2[:TS], axis=0, keepdims=True),
                                  jnp.sum(d0 * x1[:TS], axis=0, keepdims=True),
                                  jnp.sum(d0 * x0[:TS], axis=0, keepdims=True)], axis=0)
        part_b = jnp.sum(d0, axis=0, keepdims=True)

        @pl.when(i == 0)
        def _():
            dw_ref[...] = part_w
            db_ref[...] = part_b

        @pl.when(i > 0)
        def _():
            dw_ref[...] += part_w
            db_ref[...] += part_b

    blk = pl.BlockSpec((None, TS, FF_BLK), lambda j, i: (j, i, 0))
    prev = pl.BlockSpec((None, HALO, FF_BLK), lambda j, i: (j, _prev_idx(i, TS), 0))
    nxt = pl.BlockSpec((None, HALO, FF_BLK), lambda j, i: (j, _next_idx(i, TS), 0))
    wspec = pl.BlockSpec((None, 3, FF_BLK), lambda j, i: (j, 0, 0))
    bspec = pl.BlockSpec((None, 1, FF_BLK), lambda j, i: (j, 0, 0))
    return _pallas(
        body, name=name, grid=(N_FF_BLK, T // TS),
        in_specs=[pl.BlockSpec((TS, D), lambda j, i: (i, 0)),
                  pl.BlockSpec((HALO, D), lambda j, i: (_next_idx(i, TS), 0)),
                  pl.BlockSpec((None, None, FF_BLK, D), lambda j, i: (0, j, 0, 0)),
                  blk, prev, nxt, blk, nxt, wspec, bspec],
        out_specs=[blk, blk, wspec, bspec],
        out_shape=[jax.ShapeDtypeStruct((N_FF_BLK, T, FF_BLK), BF16), jax.ShapeDtypeStruct((N_FF_BLK, T, FF_BLK), BF16),
                   jax.ShapeDtypeStruct((N_FF_BLK, 3, FF_BLK), F32), jax.ShapeDtypeStruct((N_FF_BLK, 1, FF_BLK), F32)],
        params=_params(("parallel", "arbitrary")))(dh, dh, w_down4, g, g, g, v, v, w, b)


def _rope_tables(pos, inv_freq):
    half = QK_ROPE // 2

    def body(p_ref, f_ref, c_ref, sa_ref, sb_ref):
        ang = p_ref[...].astype(F32) * f_ref[...]
        lane = lax.broadcasted_iota(jnp.int32, (T, 128), 1)
        c = jnp.cos(ang)
        s = jnp.sin(ang)
        c_ref[...] = jnp.where(lane < 2 * half, c, 0.0)
        sa_ref[...] = jnp.where(lane < half, -s, 0.0)
        sb_ref[...] = jnp.where(jnp.logical_and(lane >= half, lane < 2 * half), s, 0.0)

    return _pallas(
        body, name="rope_tables", in_specs=[VMEM_SPEC] * 2, out_specs=[VMEM_SPEC] * 3,
        out_shape=[jax.ShapeDtypeStruct((T, 128), F32)] * 3,
        params=pltpu.CompilerParams(vmem_limit_bytes=VMEM_LIMIT))(pos, inv_freq)


def _rotate(r, c, sa, sb, sign):
    return r * c + sign * (pltpu.roll(r, 96, 1) * sa + pltpu.roll(r, 32, 1) * sb)


def _q_up(cq, w_uq, tables):
    cos, sa, sb = tables

    def body(a_ref, b_ref, c_ref, sa_ref, sb_ref, o_ref):
        for h in range(N_HEADS):
            r = lax.dot_general(a_ref[...], b_ref[h], NT, preferred_element_type=F32)
            o_ref[h, :, :QK_NOPE] = r[:, :QK_NOPE].astype(BF16)
            o_ref[h, :, QK_NOPE:] = _rotate(r[:, QK_NOPE:], c_ref[...], sa_ref[...], sb_ref[...], 1.0).astype(BF16)

    tab = pl.BlockSpec((TS, 128), lambda i: (i, 0))
    return _pallas(
        body, name="q_up", grid=(T // TS,),
        in_specs=[pl.BlockSpec((TS, Q_LORA), lambda i: (i, 0)),
                  pl.BlockSpec((N_HEADS, QK_PAD, Q_LORA), lambda i: (0, 0, 0)), tab, tab, tab],
        out_specs=pl.BlockSpec((N_HEADS, TS, QK_PAD), lambda i: (0, i, 0)),
        out_shape=jax.ShapeDtypeStruct((N_HEADS, T, QK_PAD), BF16),
        params=_params(("parallel",)))(cq, w_uq, cos, sa, sb)


def _rope(name, x, tables, sign, out_dtype, reduce_groups=False):
    g, _, w = x.shape
    cos, sa, sb = tables

    def body(x_ref, c_ref, sa_ref, sb_ref, o_ref):
        xv = x_ref[...].astype(F32)
        if reduce_groups:
            acc = xv[0]
            for k in range(1, g):
                acc = acc + xv[k]
            xv = acc
        out = _rotate(xv[:, w - 128:], c_ref[...], sa_ref[...], sb_ref[...], sign)
        if w > 128:
            o_ref[:, :w - 128] = xv[:, :w - 128].astype(out_dtype)
        o_ref[:, w - 128:] = out.astype(out_dtype)

    tab = pl.BlockSpec((TM, 128), lambda h, i: (i, 0))
    if reduce_groups:
        x_spec = pl.BlockSpec((g, TM, w), lambda h, i: (0, i, 0))
        groups = 1
    else:
        x_spec = pl.BlockSpec((None, TM, w), lambda h, i: (h, i, 0))
        groups = g
    return _pallas(
        body, name=name, grid=(groups, T // TM), in_specs=[x_spec, tab, tab, tab],
        out_specs=pl.BlockSpec((None, TM, w), lambda h, i: (h, i, 0)),
        out_shape=jax.ShapeDtypeStruct((groups, T, w), out_dtype),
        params=_params(("parallel", "parallel")))(x, cos, sa, sb)


SCALE = (QK_NOPE + QK_ROPE) ** -0.5
LOG2E = 1.4426950408889634
SCALE2 = SCALE * LOG2E


def _diag_mask(transposed):
    shift = CHUNK.bit_length() - 1
    a = lax.broadcasted_iota(jnp.int32, (TQ, TQ), 0) >> shift
    b = lax.broadcasted_iota(jnp.int32, (TQ, TQ), 1) >> shift
    return (a <= b) if transposed else (b <= a)


def _as_row(col):
    return jnp.transpose(jnp.broadcast_to(col, (col.shape[0], 128)), (1, 0))[0:1]


def _keys(kn_ref, kr_ref, off):
    return jnp.concatenate([kn_ref[pl.ds(off, TQ), :], kr_ref[pl.ds(off, TQ), :]], axis=1)


def _attn_fwd(q, kn, kr, v):
    def body(q_ref, kn_ref, kr_ref, v_ref, o_ref, lse_ref):
        i = pl.program_id(1)
        qv = q_ref[...]

        def step(j, carry, masked):
            m, l, acc = carry
            off = pl.multiple_of(j * TQ, TQ)
            s = lax.dot_general(qv, _keys(kn_ref, kr_ref, off), NT, preferred_element_type=F32) * SCALE2
            if masked:
                s = jnp.where(_diag_mask(False), s, NEG_INF)
            m_new = jnp.maximum(m, jnp.max(s, axis=-1, keepdims=True))
            p = jnp.exp2(s - m_new)
            alpha = jnp.exp2(m - m_new)
            l = alpha * l + jnp.sum(p, axis=-1, keepdims=True)
            acc = alpha * acc + lax.dot_general(p.astype(BF16), v_ref[pl.ds(off, TQ), :], NN, preferred_element_type=F32)
            return m_new, l, acc

        init = (jnp.full((TQ, 1), NEG_INF, F32), jnp.zeros((TQ, 1), F32), jnp.zeros((TQ, V_HEAD), F32))
        carry = lax.fori_loop(0, i, lambda j, cr: step(j, cr, False), init)
        m, l, acc = step(i, carry, True)
        o_ref[...] = (acc / l).astype(BF16)
        lse_ref[...] = _as_row(m + jnp.log(l) * LOG2E)

    return _pallas(
        body, name="attn_fwd", grid=(N_HEADS, T // TQ),
        in_specs=[pl.BlockSpec((None, TQ, QK_PAD), lambda h, i: (h, i, 0)),
                  pl.BlockSpec((T, QK_NOPE), lambda h, i: (0, h)),
                  pl.BlockSpec((T, 128), lambda h, i: (0, 0)),
                  pl.BlockSpec((T, V_HEAD), lambda h, i: (0, h))],
        out_specs=[pl.BlockSpec((TQ, V_HEAD), lambda h, i: (i, h)), pl.BlockSpec((None, 1, TQ), lambda h, i: (h, 0, i))],
        out_shape=[jax.ShapeDtypeStruct((T, N_HEADS * V_HEAD), BF16), jax.ShapeDtypeStruct((N_HEADS, 1, T), F32)],
        params=_params(("parallel", "parallel")))(q, kn, kr, v)


def _attn_bwd(q, kn, kr, v, o, do, lse_row, tables):
    nq = T // TQ
    cos, sa, sb = tables

    def body(q_ref, kn_ref, kr_ref, v_ref, o_ref, do_ref, lse_ref, c_ref, sa_ref, sb_ref,
             dq_ref, dkn_ref, dkr_ref, dv_ref, dq_acc, dl_ref):
        j = pl.program_id(1)

        @pl.when(j == 0)
        def _():
            dq_acc[...] = jnp.zeros_like(dq_acc)
            for i in range(nq):
                rows = pl.ds(i * TQ, TQ)
                prod = do_ref[rows, :].astype(F32) * o_ref[rows, :].astype(F32)
                dl_ref[:, rows] = _as_row(jnp.sum(prod, axis=-1, keepdims=True))

        kk = jnp.concatenate([kn_ref[...], kr_ref[...]], axis=1)
        vv = v_ref[...]

        def step(i, carry, masked):
            dk, dv = carry
            off = pl.multiple_of(i * TQ, TQ)
            qi = q_ref[pl.ds(off, TQ), :]
            doi = do_ref[pl.ds(off, TQ), :]
            st = lax.dot_general(kk, qi, NT, preferred_element_type=F32) * SCALE2
            if masked:
                st = jnp.where(_diag_mask(True), st, NEG_INF)
            pt = jnp.exp2(st - lse_ref[:, pl.ds(off, TQ)])
            dv = dv + lax.dot_general(pt.astype(BF16), doi, NN, preferred_element_type=F32)
            dpt = lax.dot_general(vv, doi, NT, preferred_element_type=F32)
            dst = ((pt * (dpt - dl_ref[:, pl.ds(off, TQ)])) * SCALE).astype(BF16)
            dk = dk + lax.dot_general(dst, qi, NN, preferred_element_type=F32)
            dq_acc[pl.ds(off, TQ), :] += lax.dot_general(dst, kk, TN, preferred_element_type=F32)
            return dk, dv

        carry = step(j, (jnp.zeros((TQ, QK_PAD), F32), jnp.zeros((TQ, V_HEAD), F32)), True)
        dk, dv = lax.fori_loop(j + 1, nq, lambda i, cr: step(i, cr, False), carry)
        dkn_ref[...] = dk[:, :QK_NOPE].astype(BF16)
        dkr_ref[...] = dk[:, QK_NOPE:]
        dv_ref[...] = dv.astype(BF16)

        @pl.when(j == nq - 1)
        def _():
            dq = dq_acc[...]
            dq_ref[:, :QK_NOPE] = dq[:, :QK_NOPE].astype(BF16)
            dq_ref[:, QK_NOPE:] = _rotate(dq[:, QK_NOPE:], c_ref[...], sa_ref[...], sb_ref[...], -1.0).astype(BF16)

    row = pl.BlockSpec((None, 1, T), lambda h, j: (h, 0, 0))
    head = pl.BlockSpec((TQ, 128), lambda h, j: (j, h))
    whole = pl.BlockSpec((None, T, QK_PAD), lambda h, j: (h, 0, 0))
    tab = pl.BlockSpec((T, 128), lambda h, j: (0, 0))
    heads = pl.BlockSpec((T, V_HEAD), lambda h, j: (0, h))
    return _pallas(
        body, name="attn_bwd", grid=(N_HEADS, nq),
        in_specs=[whole, head, pl.BlockSpec((TQ, 128), lambda h, j: (j, 0)), head, heads, heads, row, tab, tab, tab],
        out_specs=[whole, head, pl.BlockSpec((None, TQ, 128), lambda h, j: (h, j, 0)), head],
        out_shape=[jax.ShapeDtypeStruct((N_HEADS, T, QK_PAD), BF16), jax.ShapeDtypeStruct((T, N_HEADS * QK_NOPE), BF16),
                   jax.ShapeDtypeStruct((N_HEADS, T, 128), F32), jax.ShapeDtypeStruct((T, N_HEADS * V_HEAD), BF16)],
        scratch_shapes=[pltpu.VMEM((T, QK_PAD), F32), pltpu.VMEM((1, T), F32)],
        params=_params(("parallel", "arbitrary")))(q, kn, kr, v, o, do, lse_row, cos, sa, sb)


def _ffn_gup(name, dg, dv, hf):
    def body(dg_ref, dv_ref, hf_ref, o_ref):
        j = pl.program_id(0)

        @pl.when(j < N_FF_BLK)
        def _():
            o_ref[...] = lax.dot_general(dg_ref[...], hf_ref[...], TN, preferred_element_type=F32).astype(BF16)

        @pl.when(j >= N_FF_BLK)
        def _():
            o_ref[...] = lax.dot_general(dv_ref[...], hf_ref[...], TN, preferred_element_type=F32).astype(BF16)

    return _pallas(
        body, name=name, grid=(N_DEV,),
        in_specs=[pl.BlockSpec((None, T, FF_BLK), lambda j: (jnp.minimum(j, N_FF_BLK - 1), 0, 0)),
                  pl.BlockSpec((None, T, FF_BLK), lambda j: (jnp.maximum(j - N_FF_BLK, 0), 0, 0)),
                  pl.BlockSpec((T, D), lambda j: (0, 0))],
        out_specs=pl.BlockSpec((None, FF_BLK, D), lambda j: (j, 0, 0)),
        out_shape=jax.ShapeDtypeStruct((N_DEV, FF_BLK, D), BF16), params=_params(("parallel",)))(dg, dv, hf)


def _ffn_layer_fwd(tag, h, gain, ex):
    hf = _rms_fwd(f"{tag}_norm", h, gain)
    g, v, act = _ffn_up_act(f"{tag}_up", hf, ex.need(f"ffn_w_up{tag[1]}", hf), ex.need(f"ffn_cw{tag[1]}", hf),
                            ex.need(f"ffn_cb{tag[1]}", hf))
    ex.at(f"{tag}_up", act)
    rows = pl.BlockSpec((TS, D), lambda i: (i, 0))
    out = _mm_sum(f"{tag}_down",
                  [(act, pl.BlockSpec((N_FF_BLK, TS, FF_BLK), lambda i: (0, i, 0)), ex.need(f"ffn_w_down{tag[1]}", act),
                    pl.BlockSpec((None, N_FF_BLK, FF_BLK, D), lambda i: (0, 0, 0, 0)), NN, 0)],
                  grid=(T // TS,), o_spec=rows, o_shape=(T, D), o_dtype=F32, add=h)
    ex.at(f"{tag}_down", out)
    return out, (hf, g, v, act)


def _ffn_layer_bwd(tag, h, gain, ex, saved, dh, dh_bf):
    hf, g, v, act = saved
    layer = tag[1]
    w_up, w_down4 = ex.need(f"ffn_w_up{layer}", dh_bf), ex.need(f"ffn_w_down{layer}", dh_bf)
    dg, dv, dcw, dcb = _ffn_dact(f"{tag}_dact", dh_bf, w_down4, g, v, ex.need(f"ffn_cw{layer}", dh_bf),
                                 ex.need(f"ffn_cb{layer}", dh_bf))
    ex.at(f"{tag}_dact", dg)
    g_down = _mm(f"{tag}_gdown", act, dh_bf, grid=(N_FF_BLK,),
                 a_spec=pl.BlockSpec((None, T, FF_BLK), lambda j: (j, 0, 0)),
                 b_spec=pl.BlockSpec((T, D), lambda j: (0, 0)),
                 o_spec=pl.BlockSpec((FF_BLK, D), lambda j: (j, 0)),
                 o_shape=(D_FF, D), o_dtype=BF16, dims=TN)
    g_up = _ffn_gup(f"{tag}_gup", dg, dv, hf)
    ex.grad("ffn_w_up", int(layer), g_up.reshape(1, N_DEV, FF_BLK, D))
    ex.grad("ffn_w_down", int(layer), g_down.reshape(1, N_DEV, D_FF // N_DEV, D))
    ex.at(f"{tag}_gup", g_up)
    part = pl.BlockSpec((N_FF_BLK, TR, FF_BLK), lambda i: (0, i, 0))
    dh_in, dh_in_bf, dgain = _mm_sum(
        f"{tag}_dhf",
        [(dg, part, w_up, pl.BlockSpec((None, N_FF_BLK, FF_BLK, D), lambda i: (0, 0, 0, 0)), NN, 0),
         (dv, part, w_up, pl.BlockSpec((None, N_FF_BLK, FF_BLK, D), lambda i: (0, 1, 0, 0)), NN, 0)],
        grid=(T // TR,), o_spec=pl.BlockSpec((TR, D), lambda i: (i, 0)), o_shape=(T, D), o_dtype=F32,
        norm_bwd=(h, [gain], dh))
    ex.at(f"{tag}_dhf", dh_in)
    return dh_in, dh_in_bf, dgain[0], dcw, dcb


def _local_step(x, pos, tgt, rep, ex):
    attn_norm, ffn_norm, final_norm = rep["attn_norm"], rep["ffn_norm"], rep["final_norm"]
    half = QK_ROPE // 2
    inv = 1.0 / (ROPE_THETA ** (jnp.arange(half, dtype=F32) / half))
    inv_freq = jnp.concatenate([inv, inv, jnp.zeros((128 - 2 * half,), F32)]).reshape(1, 128)
    tables = _rope_tables(pos, inv_freq)

    hn0 = _rms_fwd("l0_norm", x, attn_norm[0:1])
    w_in = ex.need("sc_w_in", hn0)
    ex.at("mixer_ready", hn0)
    zb, zc, zu, y = _mixer_in(hn0, w_in, ex.need("sc_conv_w", hn0))
    ex.at("l0_in", y)
    h1 = _mm_rows("l0_out", y, ex.need("sc_w_out", y), NN, F32, D, tn=512, add=x)
    ex.at("l0_out", h1)
    h2, ffn0 = _ffn_layer_fwd("f0", h1, ffn_norm[0:1], ex)

    hk, hn1 = _rms_fwd2("h2_norms", h2, rep["kv_in_norm"], attn_norm[1:2])
    ckv_raw, ckv, kr = _kv_down(hk, ex.need("w_dkv", hk), ex.need("w_kr", hk), rep["kv_latent_norm"], tables)
    kn, vv = _kv_up(ckv, ex.need("w_uk", ckv), ex.need("w_uv", ckv))

    cq_raw, cq = _down_norm("q_down", hn1, ex.need("w_dq", hn1), rep["q_latent_norm"])
    w_uq = ex.need("w_uq", cq)
    q = _q_up(cq, w_uq, tables)
    o, lse = _attn_fwd(q, kn, kr, vv)
    ex.at("attn_fwd", o)
    w_o = ex.need("w_o", o)
    h3 = _mm_rows("attn_out", o, w_o, NN, F32, D, tn=512, add=h2)
    h4, ffn1 = _ffn_layer_fwd("f1", h3, ffn_norm[1:2], ex)

    loss, dh4, dh4_bf, d_final = _final(h4, final_norm.reshape(1, D), tgt)

    dh3, dh3_bf, d_fn1, dcw1, dcb1 = _ffn_layer_bwd("f1", h3, ffn_norm[1:2], ex, ffn1, dh4, dh4_bf)
    ex.at("f1_bwd", dh3)

    do = _mm_rows("d_attn_out", dh3_bf, w_o, NT, BF16, N_HEADS * V_HEAD)
    ex.grad("w_o", None, _mm_wgrad("g_w_o", o, dh3_bf).reshape(1, N_DEV, D // N_DEV, D))
    dq_pre, dkn, dkr, dvv = _attn_bwd(q, kn, kr, vv, o, do, lse, tables)
    def rows_of(a):
        return a[None], pl.BlockSpec((1, TS, a.shape[1]), lambda i: (0, i, 0))

    def whole(wt):
        return wt[None], pl.BlockSpec((1,) + wt.shape, lambda i: (0, 0, 0))

    def row_blocks(d):
        return dict(grid=(T // TS,), o_spec=pl.BlockSpec((TS, d), lambda i: (i, 0)), o_shape=(T, d), o_dtype=F32)

    _, dcq_raw_bf, (d_qln,) = _mm_sum(
        "d_q_up", [(dq_pre, pl.BlockSpec((N_HEADS, TS, QK_PAD), lambda i: (0, i, 0)),
                    w_uq, pl.BlockSpec((N_HEADS, QK_PAD, Q_LORA), lambda i: (0, 0, 0)), NN, 0)],
        norm_bwd=(cq_raw, [rep["q_latent_norm"]], None), **row_blocks(Q_LORA))
    g_uq = _mm("g_w_uq", dq_pre, cq, grid=(N_HEADS,),
               a_spec=pl.BlockSpec((None, T, QK_PAD), lambda h: (h, 0, 0)),
               b_spec=pl.BlockSpec((T, Q_LORA), lambda h: (0, 0)),
               o_spec=pl.BlockSpec((None, QK_PAD, Q_LORA), lambda h: (h, 0, 0)),
               o_shape=(N_HEADS, QK_PAD, Q_LORA), o_dtype=BF16, dims=TN)
    ex.grad("w_uq", None, g_uq[:, :QK_NOPE + QK_ROPE].reshape(1, N_DEV, QK_NOPE + QK_ROPE, Q_LORA))
    ex.grad("w_dq", None, _mm_wgrad("g_w_dq", hn1, dcq_raw_bf).reshape(1, N_DEV, D // N_DEV, Q_LORA))

    _, dckv_raw_bf, (d_kvln,) = _mm_sum(
        "d_kv_up", [(*rows_of(dkn), *whole(ex.need("w_uk", dkn)), NT, 0),
                    (*rows_of(dvv), *whole(ex.need("w_uv", dvv)), NT, 0)],
        norm_bwd=(ckv_raw, [rep["kv_latent_norm"]], None), **row_blocks(KV_LORA))
    ex.grad("w_uk", None, _mm_wgrad("g_w_uk", ckv, dkn))
    ex.grad("w_uv", None, _mm_wgrad("g_w_uv", ckv, dvv))
    dkr_raw_bf = _rope("dk_rope", dkr, tables, -1.0, BF16, reduce_groups=True).reshape(T, 128)
    ex.grad("w_dkv", None, _mm_wgrad("g_w_dkv", hk, dckv_raw_bf).reshape(1, N_DEV, D // N_DEV, KV_LORA))
    ex.grad("w_kr", None, _mm_wgrad("g_w_kr", dkr_raw_bf, hk)[:QK_ROPE])

    dh2, dh2_bf, (d_an1, d_kvin) = _mm_sum(
        "d_h2", [(*rows_of(dcq_raw_bf), *whole(ex.need("w_dq", dcq_raw_bf)), NT, 0),
                 (*rows_of(dckv_raw_bf), *whole(ex.need("w_dkv", dckv_raw_bf)), NT, 1),
                 (*rows_of(dkr_raw_bf), *whole(ex.need("w_kr", dkr_raw_bf)), NN, 1)],
        norm_bwd=(h2, [attn_norm[1:2], rep["kv_in_norm"]], dh3), **row_blocks(D))
    ex.at("kv_bwd", dh2)

    dh1, dh1_bf, d_fn0, dcw0, dcb0 = _ffn_layer_bwd("f0", h1, ffn_norm[0:1], ex, ffn0, dh2, dh2_bf)
    ex.at("f0_bwd", dh1)

    ex.grad("sc_w_out", None, _mm_wgrad("g_sc_w_out", y, dh1_bf).reshape(1, N_DEV, D // N_DEV, D))
    dz, d_scw = _mixer_out_bwd(dh1_bf, ex.need("sc_w_out", dh1_bf), zb, zc, zu, ex.need("sc_conv_w", dh1_bf))
    g_in = _mm_wgrad("g_sc_w_in", hn0, dz)
    ex.grad("sc_w_in", None, g_in)
    ex.at("sc_bwd", g_in)
    ex.at("d_l0_in", g_in)
    grad_x, _, (d_an0,) = _mm_sum(
        "d_l0_in", [(*rows_of(dz), *whole(ex.need("sc_w_in", dz)), NT, 0)],
        norm_bwd=(x, [attn_norm[0:1]], dh1), **row_blocks(D))

    small = {
        "attn_norm": jnp.concatenate([d_an0, d_an1], axis=0),
        "ffn_norm": jnp.concatenate([d_fn0, d_fn1], axis=0),
        "final_norm": d_final.reshape(D),
        "kv_in_norm": d_kvin.reshape(D),
        "kv_latent_norm": d_kvln.reshape(KV_LORA),
        "q_latent_norm": d_qln,
        "ffn_conv_b": jnp.stack([dcb0, dcb1]).transpose(0, 2, 1, 3).reshape(2, D_FF),
        "sc_conv_w": d_scw,
        "ffn_conv_w": jnp.stack([dcw0, dcw1]).transpose(0, 2, 1, 3).reshape(2, 3, D_FF),
    }
    return loss, grad_x, small


def _place():
    return lax.axis_index("x"), lax.axis_index("y"), lax.axis_index("c")


def _peers():
    x, y, c = _place()
    return (x, y, 1 - c), [(1 - x, y), (x, 1 - y), (1 - x, 1 - y)]


def _window(ref, kind, dev):
    if kind == "blocked":
        return ref.at[:, dev]
    width = ref.shape[-1] // N_DEV
    return ref.at[:, pl.ds(pl.multiple_of(dev * width, 128), width)]


HBM_SPEC = pl.BlockSpec(memory_space=pltpu.HBM)
SEM_SPEC = pl.BlockSpec(memory_space=pltpu.SEMAPHORE)
EFFECT = pltpu.SideEffectType.DATAFLOW_SIDE_EFFECTING
TOKEN = jax.ShapeDtypeStruct((8, 128), F32)


def _hbm(a):
    return pltpu.with_memory_space_constraint(a, pltpu.HBM)


def _copies_start(name, jobs):
    nj = len(jobs)
    counts = [(len(srcs), len(lands)) for srcs, lands, _, _ in jobs]
    n_arr = sum(ns + nl for ns, nl in counts)

    def body(*refs):
        sems, token = refs[n_arr:n_arr + 2 * nj], refs[-1]
        at = 0
        for j, ((ns, nl), (_, _, ncopy, plan)) in enumerate(zip(counts, jobs)):
            copies = plan(refs[at:at + ns], refs[at + ns:at + ns + nl])
            assert len(copies) == ncopy
            for k, (sent, dst, to, _) in enumerate(copies):
                pltpu.make_async_remote_copy(src_ref=sent, dst_ref=dst, send_sem=sems[2 * j].at[k],
                                             recv_sem=sems[2 * j + 1].at[k], device_id=to, device_id_type=MESH).start()
            at += ns + nl
        token[...] = jnp.zeros_like(token)

    arrays = [a for srcs, lands, _, _ in jobs for a in list(srcs) + list(lands)]
    sem_shapes = [pltpu.SemaphoreType.DMA((ncopy,)) for _, _, ncopy, _ in jobs for _ in range(2)]
    outs = pl.pallas_call(
        body, name=name, in_specs=[HBM_SPEC] * n_arr,
        out_specs=[SEM_SPEC] * (2 * nj) + [HBM_SPEC] * n_arr + [VMEM_SPEC],
        out_shape=sem_shapes + [pltpu.HBM(a.shape, a.dtype) for a in arrays] + [TOKEN],
        input_output_aliases={i: 2 * nj + i for i in range(n_arr)},
        compiler_params=pltpu.CompilerParams(has_side_effects=EFFECT))(*[_hbm(a) for a in arrays])
    _Chain.last = outs[-1]
    flights, at = [], 2 * nj
    for j, (ns, nl) in enumerate(counts):
        flights.append((outs[2 * j], outs[2 * j + 1], list(outs[at:at + ns]), list(outs[at + ns:at + ns + nl])))
        at += ns + nl
    return flights


def _copies_wait(name, started, ncopy, plan):
    send, recv, srcs, lands = started
    ns, nl = len(srcs), len(lands)

    def body(*refs):
        send_ref, recv_ref, token = refs[ns + nl], refs[ns + nl + 1], refs[-1]
        copies = plan(refs[:ns], refs[ns:ns + nl])
        assert len(copies) == ncopy
        for k, (sent, _, to, landed) in enumerate(copies):
            cp = pltpu.make_async_remote_copy(src_ref=sent, dst_ref=landed, send_sem=send_ref.at[k],
                                              recv_sem=recv_ref.at[k], device_id=to, device_id_type=MESH)
            cp.wait_send()
            cp.wait_recv()
        token[...] = jnp.zeros_like(token)

    arrays = list(srcs) + list(lands)
    outs = pl.pallas_call(
        body, name=name, in_specs=[HBM_SPEC] * (ns + nl) + [SEM_SPEC] * 2 + [ANY_SPEC],
        out_specs=[HBM_SPEC] * (ns + nl) + [VMEM_SPEC], out_shape=[pltpu.HBM(a.shape, a.dtype) for a in arrays] + [TOKEN],
        input_output_aliases={i: i for i in range(ns + nl)},
        compiler_params=pltpu.CompilerParams(has_side_effects=EFFECT))(*arrays, send, recv, _Chain.last)
    _Chain.last = outs[-1]
    return list(outs[:ns]), list(outs[ns:-1])


def _plan_gather_chips(kinds):
    def plan(srcs, lands):
        x, y, c = _place()
        sibling, chips = _peers()
        out = []
        for t, kind in enumerate(kinds):
            mine = _window(lands[t], kind, 4 * x + 2 * y + c)
            out.append((srcs[t], mine, (x, y, c), mine))
            out.append((srcs[t], mine, sibling, _window(lands[t], kind, 4 * x + 2 * y + 1 - c)))
            for px, py in chips:
                out.append((srcs[t], mine, (px, py, c), _window(lands[t], kind, 4 * px + 2 * py + c)))
        return out
    return plan, 5 * len(kinds)


def _plan_gather_all(n):
    def plan(srcs, lands):
        x, y, c = _place()
        out = []
        for t in range(n):
            mine = lands[t].at[:, 4 * x + 2 * y + c]
            for m in range(N_DEV):
                px, py, pc = (1 - x if m & 4 else x), (1 - y if m & 2 else y), (1 - c if m & 1 else c)
                out.append((srcs[t], mine, (px, py, pc), lands[t].at[:, 4 * px + 2 * py + pc]))
        return out
    return plan, N_DEV * n


def _plan_gather_sibling(kinds):
    def plan(srcs, lands):
        _, _, c = _place()
        sibling, chips = _peers()
        out = []
        for t, kind in enumerate(kinds):
            for px, py in chips:
                w = _window(lands[t], kind, 4 * px + 2 * py + c)
                out.append((w, w, sibling, _window(lands[t], kind, 4 * px + 2 * py + 1 - c)))
        return out
    return plan, 3 * len(kinds)


def _plan_scatter_sibling(kinds):
    def plan(srcs, lands):
        _, _, c = _place()
        sibling, _ = _peers()
        out = []
        for t, kind in enumerate(kinds):
            for k in range(N_CHIP):
                out.append((_window(srcs[t], kind, 2 * k + 1 - c), lands[t].at[k], sibling, lands[t].at[k]))
        return out
    return plan, N_CHIP * len(kinds)


def _plan_scatter_chips(n):
    def plan(srcs, lands):
        x, y, c = _place()
        _, chips = _peers()
        out = []
        for t in range(n):
            for px, py in chips:
                out.append((srcs[t].at[2 * px + py], lands[t].at[2 * x + y], (px, py, c), lands[t].at[2 * px + py]))
        return out
    return plan, 3 * n


def _landing(shard, kind):
    if kind == "blocked":
        return lax.empty((shard.shape[0], N_DEV) + shard.shape[1:], shard.dtype)
    return lax.empty((shard.shape[0], N_DEV * shard.shape[1]), shard.dtype)


def _chip_sums(name, grads, kinds, recvs, c):
    n = len(grads)
    in_specs, out_specs, out_shape, args = [], [], [], []
    for gr, kind, rv in zip(grads, kinds, recvs):
        if kind == "blocked":
            rows, w = gr.shape[2], gr.shape[3]
            in_specs.append(pl.BlockSpec((None, None, rows, w), lambda k, cref: (0, 2 * k + cref[0], 0, 0)))
        else:
            rows, w = gr.shape[0], gr.shape[1] // N_DEV
            in_specs.append(pl.BlockSpec((rows, w), lambda k, cref: (0, 2 * k + cref[0])))
        blk = pl.BlockSpec((None, rows, w), lambda k, cref: (k, 0, 0))
        in_specs.append(blk)
        out_specs.append(blk)
        out_shape.append(jax.ShapeDtypeStruct((N_CHIP, rows, w), BF16))
        args += [gr, rv.reshape(N_CHIP, rows, w)]

    def body(*refs):
        for t in range(n):
            g_ref, r_ref, o_ref = refs[1 + 2 * t], refs[2 + 2 * t], refs[1 + 2 * n + t]
            o_ref[...] = (g_ref[...].astype(F32) + r_ref[...].astype(F32)).astype(BF16)

    return _pallas(body, name=name, n_prefetch=1, grid=(N_CHIP,), in_specs=in_specs, out_specs=out_specs,
                   out_shape=out_shape, params=_params(("parallel",)))(c, *args)


def _adamw_math(g, wv, mv, vv):
    m = ADAM_B1 * mv + (1.0 - ADAM_B1) * g
    v = ADAM_B2 * vv + (1.0 - ADAM_B2) * (g * g)
    m_hat = m / (1.0 - ADAM_B1 ** ADAM_STEP)
    v_hat = v / (1.0 - ADAM_B2 ** ADAM_STEP)
    delta = -ADAM_LR * (m_hat / (jnp.sqrt(v_hat) + ADAM_EPS) + ADAM_WD * wv)
    return delta, m, v


ADAM_STEPS = 2


def _adamw_group(name, items, chip_ids):
    n = len(items)
    in_specs, out_specs, out_shape, args, prevs = [], [], [], [chip_ids], []
    for own, recv, w3, m3, v3, layer, _ in items:
        nl, rows, w = w3.shape
        tr = rows // ADAM_STEPS
        assert tr % 16 == 0, (name, rows)
        in_specs += [pl.BlockSpec((None, tr, w), lambda i, ids, slot=slot: (ids[slot], i, 0)) for slot in range(4)]
        slab = pl.BlockSpec((None, tr, w), lambda i, ids, layer=layer: (layer, i, 0))
        in_specs += [slab] * 3
        out_specs += [slab] * 4
        out_shape += [jax.ShapeDtypeStruct((nl, rows, w), F32)] * 4
        args += [own, recv, recv, recv, w3, m3, v3]
    aliases = {}
    for t, item in enumerate(items):
        if item[6] is not None:
            for k in range(4):
                aliases[len(args) + k] = 4 * t + k
            in_specs += [ANY_SPEC] * 4
            args += list(item[6])
            prevs.append(t)
    n_in = 1 + 7 * n + 4 * len(prevs)

    def body(*refs):
        for t in range(n):
            own_ref, r1_ref, r2_ref, r3_ref, w_ref, m_ref, v_ref = refs[1 + 7 * t:8 + 7 * t]
            g_ref, d_ref, nm_ref, nv_ref = refs[n_in + 4 * t:n_in + 4 * t + 4]
            g = ((own_ref[...].astype(F32) + r1_ref[...].astype(F32)) + r2_ref[...].astype(F32)) + r3_ref[...].astype(F32)
            g_ref[...] = g
            d_ref[...], nm_ref[...], nv_ref[...] = _adamw_math(g, w_ref[...], m_ref[...], v_ref[...])

    outs = _pallas(body, name=name, n_prefetch=1, grid=(ADAM_STEPS,), in_specs=in_specs, out_specs=out_specs,
                   out_shape=out_shape, aliases=aliases, params=_params(("parallel",)))(*args)
    return [list(outs[4 * t:4 * t + 4]) for t in range(n)]


def _adamw_small(gathered, ws, ms, vs):
    n = len(gathered)
    full = [w is not None for w in ws]
    args = list(gathered)
    out_shape = []
    for t in range(n):
        shape = jax.ShapeDtypeStruct(gathered[t].shape[2:], F32)
        if full[t]:
            args += [ws[t], ms[t], vs[t]]
            out_shape += [shape] * 4
        else:
            out_shape += [shape]

    def body(*refs):
        i_in, i_out = n, len(args)
        for t in range(n):
            p_ref = refs[t]
            g = p_ref[0, 0]
            for k in range(1, N_DEV):
                g = g + p_ref[0, k]
            refs[i_out][...] = g
            if full[t]:
                w_ref, m_ref, v_ref = refs[i_in:i_in + 3]
                refs[i_out + 1][...], refs[i_out + 2][...], refs[i_out + 3][...] = _adamw_math(
                    g, w_ref[...], m_ref[...], v_ref[...])
                i_in += 3
                i_out += 4
            else:
                i_out += 1

    outs = _pallas(body, name="adamw_small", in_specs=[VMEM_SPEC] * len(args), out_specs=[VMEM_SPEC] * len(out_shape),
                   out_shape=out_shape, params=pltpu.CompilerParams(vmem_limit_bytes=VMEM_LIMIT))(*args)
    result, i = [], 0
    for t in range(n):
        k = 4 if full[t] else 1
        result.append(list(outs[i:i + k]))
        i += k
    return result


def _adamw_plain(name, gs, ws, ms, vs):
    n = len(gs)

    def body(*refs):
        for t in range(n):
            g_ref, w_ref, m_ref, v_ref = refs[4 * t:4 * t + 4]
            outs = refs[4 * n + 3 * t:4 * n + 3 * t + 3]
            outs[0][...], outs[1][...], outs[2][...] = _adamw_math(g_ref[...], w_ref[...], m_ref[...], v_ref[...])

    args, out_shape = [], []
    for g, w, m, v in zip(gs, ws, ms, vs):
        args += [g, w, m, v]
        out_shape += [jax.ShapeDtypeStruct(w.shape, F32)] * 3
    outs = _pallas(body, name=name, in_specs=[VMEM_SPEC] * len(args), out_specs=[VMEM_SPEC] * len(out_shape),
                   out_shape=out_shape, params=pltpu.CompilerParams(vmem_limit_bytes=VMEM_LIMIT))(*args)
    return [list(outs[3 * t:3 * t + 3]) for t in range(n)]


KIND = {"sc_w_in": "cols", "sc_w_out": "blocked", "w_dkv": "blocked", "w_kr": "cols", "w_uk": "cols", "w_uv": "cols",
        "w_dq": "blocked", "w_uq": "blocked", "w_o": "blocked", "ffn_w_up": "blocked", "ffn_w_down": "blocked",
        "conv": "blocked"}
GATHER_GROUPS = (("mixer", ("sc_w_in", "sc_w_out", "conv")),
                 ("up0", ("ffn_w_up0",)),
                 ("down0", ("ffn_w_down0",)),
                 ("attn", ("w_dkv", "w_kr", "w_uk", "w_uv", "w_dq", "w_uq", "w_o")),
                 ("ffn1", ("ffn_w_up1", "ffn_w_down1")))
SCATTER_GROUPS = (("ffn1", (("ffn_w_up", 1), ("ffn_w_down", 1))),
                  ("attn", (("w_o", None), ("w_uq", None), ("w_dq", None), ("w_uk", None), ("w_uv", None),
                            ("w_dkv", None), ("w_kr", None))),
                  ("ffn0", (("ffn_w_up", 0), ("ffn_w_down", 0))),
                  ("mixer", (("sc_w_out", None), ("sc_w_in", None))))
SCHEDULE = {
    "begin": (("gather_start", "mixer"),),
    "mixer_ready": (("gather_start", "up0"),),
    "l0_out": (("gather_forward", "up0"), ("gather_start", "down0")),
    "f0_up": (("gather_forward", "down0"), ("gather_start", "attn")),
    "f0_down": (("gather_forward", "attn"), ("gather_start", "ffn1")),
    "attn_fwd": (("gather_forward", "ffn1"),),
    "f1_gup": (("scatter_sibling", "ffn1"),),
    "f1_dhf": (("scatter_chips", "ffn1"),),
    "kv_bwd": (("scatter_sibling", "attn"), ("scatter_done", "ffn1")),
    "f0_dact": (("scatter_chips", "attn"),),
    "f0_gup": (("scatter_sibling", "ffn0"),),
    "f0_dhf": (("scatter_chips", "ffn0"),),
    "f0_bwd": (("scatter_done", "attn"),),
    "sc_bwd": (("scatter_sibling", "mixer"),),
    "d_l0_in": (("scatter_chips", "mixer"),),
}
FINISH = (("scatter_done", "ffn0"), ("scatter_done", "mixer"))
STAGES = {"gather_start": 1, "gather_forward": 2, "gather_done": 3,
          "scatter_sibling": 1, "scatter_chips": 2, "scatter_done": 3}
SMALL_W_ROWS = 24


def _pack(arrays, rows):
    flat = jnp.concatenate([a.reshape(-1).astype(F32) for a in arrays])
    return jnp.pad(flat, (0, rows * 128 - flat.shape[0])).reshape(rows, 128)


STORED_TRANSPOSED = ("ffn_w_up", "w_uq", "w_kr")


def _stored(name, a):
    return jnp.swapaxes(a, -1, -2) if name in STORED_TRANSPOSED else a


def _base(name):
    if name.startswith("ffn_w_") and name[-1] in "01":
        return name[:-1], int(name[-1])
    return name, None


class _Exchange:
    def __init__(self, wts, mom, var, ffn_conv_b):
        self.wts, self.mom, self.var = wts, mom, var
        x, y, c = _place()
        self.c_arr = jnp.reshape(c, (1,)).astype(jnp.int32)
        chip = 2 * x + y
        self.chip_ids = jnp.stack([chip, chip ^ 1, chip ^ 2, chip ^ 3]).astype(jnp.int32)
        self.ready = {"ffn_cb0": ffn_conv_b.reshape(2, N_FF_BLK, 1, FF_BLK)[0],
                      "ffn_cb1": ffn_conv_b.reshape(2, N_FF_BLK, 1, FF_BLK)[1]}
        self.gathers, self.group_of = {}, {}
        self.grads, self.scatters, self.results, self.queue = {}, {}, {}, []
        for gname, names in GATHER_GROUPS:
            self.gathers[gname] = dict(stage=0, names=names, kinds=[KIND[_base(nm)[0]] for nm in names])
            for nm in names:
                self.group_of[nm] = gname
        for nm in ("sc_conv_w", "ffn_cw0", "ffn_cw1"):
            self.group_of[nm] = "mixer"
        self.at("begin", None)

    def _shard(self, name):
        if name == "conv":
            return _pack([self.wts["sc_conv_w"], self.wts["ffn_conv_w"]], SMALL_W_ROWS).reshape(1, SMALL_W_ROWS, 128)
        base, layer = _base(name)
        a = _stored(base, self.wts[base])
        if layer is not None:
            a = a[layer:layer + 1]
        if KIND[base] == "cols":
            return a.reshape(a.shape[-2], a.shape[-1]).astype(BF16)
        return a.reshape((-1,) + a.shape[-2:]).astype(BF16)

    def _start(self, name, srcs, lands, ncopy, plan, st):
        self.queue.append((name, (srcs, lands, ncopy, plan), st))

    def _flush(self):
        if self.queue:
            flights = _copies_start("__".join(name for name, _, _ in self.queue), [job for _, job, _ in self.queue])
            for (_, _, st), flight in zip(self.queue, flights):
                st["flight"] = flight
            self.queue = []

    def _flight(self, st):
        self._flush()
        return st["flight"]

    def _gather_to(self, gname, stage, after):
        st = self.gathers[gname]
        if st["stage"] < 1 <= stage:
            shards = [self._shard(nm) for nm in st["names"]]
            lands = [_landing(s, kind) for s, kind in zip(shards, st["kinds"])]
            plan, ncopy = _plan_gather_chips(st["kinds"])
            self._start(f"ag_{gname}_chips", shards, lands, ncopy, plan, st)
            st["stage"] = 1
        if st["stage"] < 2 <= stage:
            plan, ncopy = _plan_gather_chips(st["kinds"])
            _, lands = _copies_wait(f"ag_{gname}_chips_wait", self._flight(st), ncopy, plan)
            plan, ncopy = _plan_gather_sibling(st["kinds"])
            self._start(f"ag_{gname}_sibling", [], lands, ncopy, plan, st)
            st["stage"] = 2
        if st["stage"] < 3 <= stage:
            plan, ncopy = _plan_gather_sibling(st["kinds"])
            _, lands = _copies_wait(f"ag_{gname}_sibling_wait", self._flight(st), ncopy, plan)
            for nm, land in zip(st["names"], lands):
                self._arrived(nm, land)
            st["stage"] = 3

    def _arrived(self, name, land):
        if name == "conv":
            conv = land.reshape(N_DEV, SMALL_W_ROWS * 128)
            self.ready["sc_conv_w"] = conv[:, :3 * 128].reshape(N_DEV, 3, 128).transpose(1, 0, 2).reshape(3, D)
            fcw = conv[:, 3 * 128:3 * 128 + 6 * 352].reshape(N_DEV, 2, 3, 352).transpose(1, 2, 0, 3)
            fcw = fcw.reshape(2, 3, N_FF_BLK, FF_BLK).transpose(0, 2, 1, 3)
            self.ready["ffn_cw0"], self.ready["ffn_cw1"] = fcw[0], fcw[1]
        elif name in ("sc_w_in", "w_uk", "w_uv") or name.startswith("ffn_w_up"):
            self.ready[name] = land
        elif name.startswith("ffn_w_down"):
            self.ready[name] = land.reshape(1, N_FF_BLK, FF_BLK, D)
        elif name == "w_kr":
            self.ready[name] = jnp.pad(land, ((0, 128 - QK_ROPE), (0, 0)))
        elif name == "w_uq":
            self.ready[name] = jnp.pad(land.reshape(N_HEADS, QK_NOPE + QK_ROPE, Q_LORA),
                                       ((0, 0), (0, QK_PAD - QK_NOPE - QK_ROPE), (0, 0)))
        else:
            self.ready[name] = land.reshape(D, land.shape[-1])

    def need(self, name, after):
        if name not in self.ready:
            self._gather_to(self.group_of[name], 3, after)
            self._flush()
        return self.ready[name]

    def grad(self, name, layer, array):
        self.grads[(name, layer)] = array

    def _scatter_to(self, gname, stage, after):
        keys = dict(SCATTER_GROUPS)[gname]
        st = self.scatters.setdefault(gname, dict(stage=0))
        kinds = [KIND[nm] for nm, _ in keys]
        if st["stage"] < 1 <= stage:
            grads = [self.grads[key] for key in keys]
            lands = []
            for gr, kind in zip(grads, kinds):
                shard = (gr.shape[0],) + gr.shape[2:] if kind == "blocked" else (gr.shape[0], gr.shape[1] // N_DEV)
                lands.append(lax.empty((N_CHIP,) + shard, BF16))
            plan, ncopy = _plan_scatter_sibling(kinds)
            self._start(f"rs_{gname}_sibling", grads, lands, ncopy, plan, st)
            st["stage"] = 1
        if st["stage"] < 2 <= stage:
            plan, ncopy = _plan_scatter_sibling(kinds)
            grads, recvs = _copies_wait(f"rs_{gname}_sibling_wait", self._flight(st), ncopy, plan)
            sums = _chip_sums(f"rs_{gname}_sums", grads, kinds, recvs, self.c_arr)
            lands = [lax.empty(s.shape, BF16) for s in sums]
            plan, ncopy = _plan_scatter_chips(len(sums))
            self._start(f"rs_{gname}_chips", sums, lands, ncopy, plan, st)
            st["stage"] = 2
        if st["stage"] < 3 <= stage:
            plan, ncopy = _plan_scatter_chips(len(keys))
            sums, recvs = _copies_wait(f"rs_{gname}_chips_wait", self._flight(st), ncopy, plan)
            items = []
            for (nm, layer), own, rv in zip(keys, sums, recvs):
                nl = 1 if layer is None else 2
                rows, w = own.shape[1], own.shape[2]
                w3, m3, v3 = (_stored(nm, src[nm]).reshape(nl, rows, w) for src in (self.wts, self.mom, self.var))
                items.append((own, rv, w3, m3, v3, 0 if layer is None else layer, self.results.get(nm)))
            outs = _adamw_group(f"adamw_{gname}", items, self.chip_ids)
            for (nm, _), out in zip(keys, outs):
                self.results[nm] = out
            st["stage"] = 3

    def at(self, place, after):
        for action, gname in SCHEDULE.get(place, ()):
            self._advance(action, gname, after)
        self._flush()

    def _advance(self, action, gname, after):
        if action.startswith("gather"):
            self._gather_to(gname, STAGES[action], after)
        else:
            self._scatter_to(gname, STAGES[action], after)

    def finish(self, after):
        for action, gname in FINISH:
            self._advance(action, gname, after)
        for gname, _ in SCATTER_GROUPS:
            self._scatter_to(gname, 3, after)
        return {nm: [_stored(nm, o.reshape(_stored(nm, self.wts[nm]).shape)) for o in outs]
                for nm, outs in self.results.items()}


REPLICATED = ("attn_norm", "ffn_norm", "final_norm", "kv_in_norm", "kv_latent_norm", "q_latent_norm", "ffn_conv_b")
WEIGHTS = ("attn_norm", "ffn_norm", "final_norm", "sc_w_in", "sc_conv_w", "sc_w_out", "kv_in_norm", "w_dkv",
           "kv_latent_norm", "w_kr", "w_uk", "w_uv", "w_dq", "q_latent_norm", "w_uq", "w_o", "ffn_w_up", "ffn_conv_w",
           "ffn_conv_b", "ffn_w_down")


def kernel(x, positions, attn_norm, ffn_norm, final_norm, sc_w_in, sc_conv_w, sc_w_out, kv_in_norm, w_dkv, kv_latent_norm, w_kr, w_uk, w_uv, w_dq, q_latent_norm, w_uq, w_o, ffn_w_up, ffn_conv_w, ffn_conv_b, ffn_w_down, loss_target, m_attn_norm, m_ffn_norm, m_final_norm, m_sc_w_in, m_sc_conv_w, m_sc_w_out, m_kv_in_norm, m_w_dkv, m_kv_latent_norm, m_w_kr, m_w_uk, m_w_uv, m_w_dq, m_q_latent_norm, m_w_uq, m_w_o, m_ffn_w_up, m_ffn_conv_w, m_ffn_conv_b, m_ffn_w_down, v_attn_norm, v_ffn_norm, v_final_norm, v_sc_w_in, v_sc_conv_w, v_sc_w_out, v_kv_in_norm, v_w_dkv, v_kv_latent_norm, v_w_kr, v_w_uk, v_w_uv, v_w_dq, v_q_latent_norm, v_w_uq, v_w_o, v_ffn_w_up, v_ffn_conv_w, v_ffn_conv_b, v_ffn_w_down):
    wts = dict(attn_norm=attn_norm, ffn_norm=ffn_norm, final_norm=final_norm, sc_w_in=sc_w_in, sc_conv_w=sc_conv_w,
               sc_w_out=sc_w_out, kv_in_norm=kv_in_norm, w_dkv=w_dkv, kv_latent_norm=kv_latent_norm, w_kr=w_kr,
               w_uk=w_uk, w_uv=w_uv, w_dq=w_dq, q_latent_norm=q_latent_norm, w_uq=w_uq, w_o=w_o, ffn_w_up=ffn_w_up,
               ffn_conv_w=ffn_conv_w, ffn_conv_b=ffn_conv_b, ffn_w_down=ffn_w_down)
    mom = dict(attn_norm=m_attn_norm, ffn_norm=m_ffn_norm, final_norm=m_final_norm, sc_w_in=m_sc_w_in,
               sc_conv_w=m_sc_conv_w, sc_w_out=m_sc_w_out, kv_in_norm=m_kv_in_norm, w_dkv=m_w_dkv,
               kv_latent_norm=m_kv_latent_norm, w_kr=m_w_kr, w_uk=m_w_uk, w_uv=m_w_uv, w_dq=m_w_dq,
               q_latent_norm=m_q_latent_norm, w_uq=m_w_uq, w_o=m_w_o, ffn_w_up=m_ffn_w_up, ffn_conv_w=m_ffn_conv_w,
               ffn_conv_b=m_ffn_conv_b, ffn_w_down=m_ffn_w_down)
    var = dict(attn_norm=v_attn_norm, ffn_norm=v_ffn_norm, final_norm=v_final_norm, sc_w_in=v_sc_w_in,
               sc_conv_w=v_sc_conv_w, sc_w_out=v_sc_w_out, kv_in_norm=v_kv_in_norm, w_dkv=v_w_dkv,
               kv_latent_norm=v_kv_latent_norm, w_kr=v_w_kr, w_uk=v_w_uk, w_uv=v_w_uv, w_dq=v_w_dq,
               q_latent_norm=v_q_latent_norm, w_uq=v_w_uq, w_o=v_w_o, ffn_w_up=v_ffn_w_up, ffn_conv_w=v_ffn_conv_w,
               ffn_conv_b=v_ffn_conv_b, ffn_w_down=v_ffn_w_down)
    xi, yi, ci = _place()
    me = 4 * xi + 2 * yi + ci
    _Chain.last = None

    ex = _Exchange(wts, mom, var, ffn_conv_b)
    rep = {
        "attn_norm": attn_norm, "ffn_norm": ffn_norm, "final_norm": final_norm,
        "kv_in_norm": kv_in_norm.reshape(1, D), "kv_latent_norm": kv_latent_norm.reshape(1, KV_LORA),
        "q_latent_norm": q_latent_norm.reshape(1, Q_LORA),
    }
    loss, grad_x, small = _local_step(x.reshape(T, D), positions.reshape(T, 1), loss_target.reshape(T, D), rep, ex)

    def rows_of(a):
        return a.reshape(-1, a.shape[-1])

    small_order = list(REPLICATED) + ["sc_conv_w", "ffn_conv_w"]
    shards = [loss.reshape(1, 1, 128)] + [rows_of(small[nm])[None] for nm in small_order]
    plan, ncopy = _plan_gather_all(len(shards))
    flight, = _copies_start("ag_small", [(shards, [lax.empty((1, N_DEV) + s.shape[1:], F32) for s in shards], ncopy, plan)])
    results = ex.finish(grad_x)
    _, gathered = _copies_wait("ag_small_wait", flight, ncopy, plan)
    params = [[None] + [rows_of(src[nm]) for nm in REPLICATED] + [None, None] for src in (wts, mom, var)]
    summed = _adamw_small(gathered, *params)
    loss_total = summed[0][0][0, 0]
    for nm, vals in zip(REPLICATED, summed[1:1 + len(REPLICATED)]):
        results[nm] = [a.reshape(wts[nm].shape) for a in vals]
    g_scw = lax.dynamic_slice(summed[-2][0], (0, me * 128), (3, 128))
    g_fcw = lax.dynamic_slice(summed[-1][0], (0, me * 352), (6, 352))
    conv = _adamw_plain("adamw_conv", [g_scw, g_fcw], *[[rows_of(src["sc_conv_w"]), rows_of(src["ffn_conv_w"])]
                                                        for src in (wts, mom, var)])
    for nm, g_own, vals in zip(("sc_conv_w", "ffn_conv_w"), (g_scw, g_fcw), conv):
        results[nm] = [a.reshape(wts[nm].shape) for a in [g_own] + vals]

    outs = [loss_total, grad_x.reshape(1, T, D)]
    for slot in range(4):
        outs.extend(results[nm][slot] for nm in WEIGHTS)
    return tuple(outs)
```

```python
import jax
import jax.numpy as jnp
from jax import lax
from jax.experimental import pallas as pl
from jax.experimental.pallas import tpu as pltpu

F32 = jnp.float32
BF16 = jnp.bfloat16

T = 2048
D = 1024
N_HEADS = 8
QK_NOPE = 128
QK_ROPE = 64
V_HEAD = 128
Q_LORA = 384
KV_LORA = 256
D_FF = 2816
CHUNK = 64
ROPE_THETA = 10000.0
EPS = 1e-6
NEG_INF = -1e30
ADAM_LR = 0.001
ADAM_B1 = 0.9
ADAM_B2 = 0.999
ADAM_EPS = 1e-08
ADAM_WD = 0.01
ADAM_STEP = 10

N_DEV = 8
N_CHIP = 4
FF_BLK = D_FF * 2 // N_DEV
N_FF_BLK = D_FF // FF_BLK
QK_PAD = 256
HALO = 16

TM = 1024
TS = 512
TR = 256
TQ = 512
VMEM_LIMIT = 56 * 1024 * 1024

NN = (((1,), (0,)), ((), ()))
NT = (((1,), (1,)), ((), ()))
TN = (((0,), (0,)), ((), ()))
MESH = pl.DeviceIdType.MESH


def _params(sem):
    return pltpu.CompilerParams(dimension_semantics=sem, vmem_limit_bytes=VMEM_LIMIT)


ANY_SPEC = pl.BlockSpec(memory_space=pl.ANY)
VMEM_SPEC = pl.BlockSpec(memory_space=pltpu.VMEM)


class _Chain:
    last = None


def _pallas(body, *, name, in_specs, out_specs, out_shape, grid=(), scratch_shapes=(), n_prefetch=0, aliases=None,
            params=None):
    def run(*args):
        after = _Chain.last
        n_lead = len(args)
        specs, operands, fn = list(in_specs), list(args), body
        if after is not None:
            def fn(*refs):
                return body(*refs[:n_lead], *refs[n_lead + 1:])
            specs.append(ANY_SPEC)
            operands.append(after)
        kw = dict(name=name, out_shape=out_shape, input_output_aliases=aliases or {})
        if params is not None:
            kw["compiler_params"] = params
        if n_prefetch:
            kw["grid_spec"] = pltpu.PrefetchScalarGridSpec(
                num_scalar_prefetch=n_prefetch, grid=grid, in_specs=specs, out_specs=out_specs,
                scratch_shapes=scratch_shapes)
        else:
            kw.update(grid=grid, in_specs=specs, out_specs=out_specs, scratch_shapes=scratch_shapes)
        outs = pl.pallas_call(fn, **kw)(*operands)
        _Chain.last = outs[0] if isinstance(outs, (list, tuple)) else outs
        return outs
    return run


def _mm(name, a, b, *, grid, a_spec, b_spec, o_spec, o_shape, o_dtype, dims, k_axis=None, acc_shape=None,
        add=None, add_spec=None):
    nk = grid[k_axis] if k_axis is not None else 1
    has_add = add is not None

    def body(*refs):
        a_ref, b_ref = refs[0], refs[1]
        p = 2
        add_ref = None
        if has_add:
            add_ref = refs[p]
            p += 1
        o_ref = refs[p]
        p += 1
        r = lax.dot_general(a_ref[...].astype(BF16), b_ref[...].astype(BF16), dims, preferred_element_type=F32)
        if k_axis is None:
            if has_add:
                r = r + add_ref[...].astype(F32)
            o_ref[...] = r.astype(o_dtype)
        else:
            acc = refs[p]
            k = pl.program_id(k_axis)

            @pl.when(k == 0)
            def _():
                acc[...] = r

            @pl.when(k > 0)
            def _():
                acc[...] += r

            @pl.when(k == nk - 1)
            def _():
                t = acc[...]
                if has_add:
                    t = t + add_ref[...].astype(F32)
                o_ref[...] = t.astype(o_dtype)

    in_specs = [a_spec, b_spec]
    args = [a, b]
    if has_add:
        in_specs.append(add_spec if add_spec is not None else o_spec)
        args.append(add)
    sem = tuple("arbitrary" if ax == k_axis else "parallel" for ax in range(len(grid)))
    scratch = [pltpu.VMEM(acc_shape, F32)] if k_axis is not None else []
    return _pallas(body, name=name, grid=grid, in_specs=in_specs, out_specs=o_spec,
                   out_shape=jax.ShapeDtypeStruct(o_shape, o_dtype), scratch_shapes=scratch, params=_params(sem))(*args)


def _mm_sum(name, parts, *, grid, o_spec, o_shape, o_dtype, add=None, norm_bwd=None):
    has_add = add is not None
    np_ = len(parts)
    nn = 1 if norm_bwd is None else len(norm_bwd[1])
    has_res = norm_bwd is not None and norm_bwd[2] is not None

    def body(*refs):
        accs = [None] * nn
        for p, (_, _, _, _, dims, n) in enumerate(parts):
            a_ref, b_ref = refs[2 * p], refs[2 * p + 1]
            for k in range(a_ref.shape[0]):
                r = lax.dot_general(a_ref[k], b_ref[k], dims, preferred_element_type=F32)
                accs[n] = r if accs[n] is None else accs[n] + r
        if norm_bwd is None:
            acc = accs[0]
            if has_add:
                acc = acc + refs[2 * np_][...]
            refs[-1][...] = acc.astype(o_dtype)
            return
        x_ref, g_refs = refs[2 * np_], refs[2 * np_ + 1:2 * np_ + 1 + nn]
        dx_ref, dxb_ref, dg_refs = refs[-2 - nn], refs[-1 - nn], refs[-nn:]
        xv = x_ref[...]
        r = lax.rsqrt(jnp.mean(xv * xv, axis=-1, keepdims=True) + EPS)
        xn = xv * r
        dx = refs[2 * np_ + 1 + nn][...] if has_res else None
        sums = []
        for acc, g_ref in zip(accs, g_refs):
            gdy = acc * g_ref[...]
            t = r * (gdy - xn * jnp.mean(gdy * xn, axis=-1, keepdims=True))
            dx = t if dx is None else dx + t
            sums.append(jnp.sum(acc * xn, axis=0, keepdims=True))
        dx_ref[...] = dx
        dxb_ref[...] = dx.astype(BF16)

        @pl.when(pl.program_id(0) == 0)
        def _():
            for dg_ref, part in zip(dg_refs, sums):
                dg_ref[...] = part

        @pl.when(pl.program_id(0) > 0)
        def _():
            for dg_ref, part in zip(dg_refs, sums):
                dg_ref[...] += part

    in_specs, args = [], []
    for a, a_spec, b, b_spec, _, _ in parts:
        in_specs += [a_spec, b_spec]
        args += [a, b]
    if norm_bwd is None:
        if has_add:
            in_specs.append(o_spec)
            args.append(add)
        return _pallas(body, name=name, grid=grid, in_specs=in_specs, out_specs=o_spec,
                       out_shape=jax.ShapeDtypeStruct(o_shape, o_dtype),
                       params=_params(("parallel",) * len(grid)))(*args)
    x, gains, dres = norm_bwd
    vec = pl.BlockSpec((1, o_shape[1]), lambda i: (0, 0))
    in_specs += [o_spec] + [vec] * nn + ([o_spec] if has_res else [])
    args += [x] + list(gains) + ([dres] if has_res else [])
    outs = _pallas(body, name=name, grid=grid, in_specs=in_specs, out_specs=[o_spec, o_spec] + [vec] * nn,
                   out_shape=[jax.ShapeDtypeStruct(o_shape, F32), jax.ShapeDtypeStruct(o_shape, BF16)]
                   + [jax.ShapeDtypeStruct((1, o_shape[1]), F32)] * nn,
                   params=_params(("arbitrary",)))(*args)
    return outs[0], outs[1], list(outs[2:])


def _mm_rows(name, a, b, dims, o_dtype, n_out, *, tn=None, add=None):
    k = a.shape[1]
    tn = n_out if tn is None else tn
    if dims == NN:
        b_spec = pl.BlockSpec((k, tn), lambda n, i: (0, n))
    else:
        b_spec = pl.BlockSpec((tn, k), lambda n, i: (n, 0))
    return _mm(name, a, b, grid=(n_out // tn, T // TM),
               a_spec=pl.BlockSpec((TM, k), lambda n, i: (i, 0)), b_spec=b_spec,
               o_spec=pl.BlockSpec((TM, tn), lambda n, i: (i, n)), o_shape=(T, n_out), o_dtype=o_dtype,
               dims=dims, add=add)


def _mm_wgrad(name, a, b, *, tn=512):
    k, n = a.shape[1], b.shape[1]
    tn = min(tn, n)
    return _mm(name, a, b, grid=(n // tn,),
               a_spec=pl.BlockSpec((T, k), lambda j: (0, 0)), b_spec=pl.BlockSpec((T, tn), lambda j: (0, j)),
               o_spec=pl.BlockSpec((k, tn), lambda j: (0, j)), o_shape=(k, n), o_dtype=BF16, dims=TN)


def _rms_fwd(name, x, g):
    d = x.shape[1]

    def body(x_ref, g_ref, o_ref):
        xv = x_ref[...]
        r = lax.rsqrt(jnp.mean(xv * xv, axis=-1, keepdims=True) + EPS)
        o_ref[...] = ((xv * r) * g_ref[...]).astype(BF16)

    return _pallas(
        body, name=name, grid=(T // TM,),
        in_specs=[pl.BlockSpec((TM, d), lambda i: (i, 0)), pl.BlockSpec((1, d), lambda i: (0, 0))],
        out_specs=pl.BlockSpec((TM, d), lambda i: (i, 0)),
        out_shape=jax.ShapeDtypeStruct((T, d), BF16), params=_params(("parallel",)))(x, g)


def _rows_call(name, body, row_ins, whole_ins, outs):
    in_specs = [pl.BlockSpec((TM, a.shape[1]), lambda i: (i, 0)) for a in row_ins]
    in_specs += [pl.BlockSpec(a.shape, lambda i: (0, 0)) for a in whole_ins]
    return _pallas(
        body, name=name, grid=(T // TM,), in_specs=in_specs,
        out_specs=[pl.BlockSpec((TM, d), lambda i: (i, 0)) for d, _ in outs],
        out_shape=[jax.ShapeDtypeStruct((T, d), dt) for d, dt in outs],
        params=_params(("parallel",)))(*row_ins, *whole_ins)


def _rms(xv, g):
    return (xv * lax.rsqrt(jnp.mean(xv * xv, axis=-1, keepdims=True) + EPS)) * g


def _rms_fwd2(name, x, g1, g2):
    d = x.shape[1]

    def body(x_ref, g1_ref, g2_ref, o1_ref, o2_ref):
        xv = x_ref[...]
        xn = xv * lax.rsqrt(jnp.mean(xv * xv, axis=-1, keepdims=True) + EPS)
        o1_ref[...] = (xn * g1_ref[...]).astype(BF16)
        o2_ref[...] = (xn * g2_ref[...]).astype(BF16)

    return _rows_call(name, body, [x], [g1, g2], [(d, BF16), (d, BF16)])


def _down_norm(name, a, w, g):
    n = w.shape[1]

    def body(a_ref, w_ref, g_ref, raw_ref, o_ref):
        raw = lax.dot_general(a_ref[...], w_ref[...], NN, preferred_element_type=F32)
        raw_ref[...] = raw
        o_ref[...] = _rms(raw, g_ref[...]).astype(BF16)

    return _rows_call(name, body, [a], [w, g], [(n, F32), (n, BF16)])


def _kv_down(hk, w_dkv, w_kr, g, tables):
    def body(a_ref, c_ref, sa_ref, sb_ref, wd_ref, wr_ref, g_ref, raw_ref, ckv_ref, kr_ref):
        av = a_ref[...]
        raw = lax.dot_general(av, wd_ref[...], NN, preferred_element_type=F32)
        raw_ref[...] = raw
        ckv_ref[...] = _rms(raw, g_ref[...]).astype(BF16)
        kr = lax.dot_general(av, wr_ref[...], NT, preferred_element_type=F32)
        kr_ref[...] = _rotate(kr, c_ref[...], sa_ref[...], sb_ref[...], 1.0).astype(BF16)

    return _rows_call("kv_down", body, [hk, *tables], [w_dkv, w_kr, g], [(KV_LORA, F32), (KV_LORA, BF16), (128, BF16)])


def _kv_up(ckv, w_uk, w_uv):
    def body(a_ref, wk_ref, wv_ref, k_ref, v_ref):
        av = a_ref[...]
        k_ref[...] = lax.dot_general(av, wk_ref[...], NN, preferred_element_type=F32).astype(BF16)
        v_ref[...] = lax.dot_general(av, wv_ref[...], NN, preferred_element_type=F32).astype(BF16)

    return _rows_call("kv_up", body, [ckv], [w_uk, w_uv], [(N_HEADS * QK_NOPE, BF16), (N_HEADS * V_HEAD, BF16)])


def _rms_bwd(name, x, gains, dys, dres=None):
    d = x.shape[1]
    n = len(gains)
    has_res = dres is not None

    def body(*refs):
        x_ref, g_refs, dy_refs = refs[0], refs[1:1 + n], refs[1 + n:1 + 2 * n]
        dx_ref, dxb_ref = refs[-2 - n], refs[-1 - n]
        dg_refs = refs[-n:]
        xv = x_ref[...]
        r = lax.rsqrt(jnp.mean(xv * xv, axis=-1, keepdims=True) + EPS)
        xn = xv * r
        dx = refs[1 + 2 * n][...] if has_res else None
        parts = []
        for g_ref, dy_ref in zip(g_refs, dy_refs):
            dyv = dy_ref[...].astype(F32)
            gdy = dyv * g_ref[...]
            t = r * (gdy - xn * jnp.mean(gdy * xn, axis=-1, keepdims=True))
            dx = t if dx is None else dx + t
            parts.append(jnp.sum(dyv * xn, axis=0, keepdims=True))
        dx_ref[...] = dx
        dxb_ref[...] = dx.astype(BF16)

        @pl.when(pl.program_id(0) == 0)
        def _():
            for dg_ref, part in zip(dg_refs, parts):
                dg_ref[...] = part

        @pl.when(pl.program_id(0) > 0)
        def _():
            for dg_ref, part in zip(dg_refs, parts):
                dg_ref[...] += part

    row = pl.BlockSpec((TR, d), lambda i: (i, 0))
    vec = pl.BlockSpec((1, d), lambda i: (0, 0))
    args = [x] + list(gains) + list(dys) + ([dres] if has_res else [])
    in_specs = [row] + [vec] * n + [row] * n + ([row] if has_res else [])
    outs = _pallas(
        body, name=name, grid=(T // TR,), in_specs=in_specs, out_specs=[row, row] + [vec] * n,
        out_shape=[jax.ShapeDtypeStruct((T, d), F32), jax.ShapeDtypeStruct((T, d), BF16)]
        + [jax.ShapeDtypeStruct((1, d), F32)] * n,
        params=_params(("arbitrary",)))(*args)
    return outs[0], outs[1], list(outs[2:])


def _final(h, g, tgt):
    def body(h_ref, g_ref, t_ref, loss_ref, dh_ref, dhb_ref, dg_ref):
        hv = h_ref[...]
        r = lax.rsqrt(jnp.mean(hv * hv, axis=-1, keepdims=True) + EPS)
        xn = hv * r
        gv = g_ref[...]
        err = xn * gv - t_ref[...]
        part_loss = 0.5 * jnp.sum(jnp.mean(err * err, axis=-1, keepdims=True), axis=0, keepdims=True)
        dy = err * (1.0 / D)
        gdy = dy * gv
        dh = r * (gdy - xn * jnp.mean(gdy * xn, axis=-1, keepdims=True))
        dh_ref[...] = dh
        dhb_ref[...] = dh.astype(BF16)
        part = jnp.sum(dy * xn, axis=0, keepdims=True)
        first = pl.program_id(0) == 0

        @pl.when(first)
        def _():
            dg_ref[...] = part
            loss_ref[...] = jnp.broadcast_to(part_loss, (1, 128))

        @pl.when(jnp.logical_not(first))
        def _():
            dg_ref[...] += part
            loss_ref[...] += jnp.broadcast_to(part_loss, (1, 128))

    row = pl.BlockSpec((TR, D), lambda i: (i, 0))
    vec = pl.BlockSpec((1, D), lambda i: (0, 0))
    return _pallas(
        body, name="final_loss", grid=(T // TR,), in_specs=[row, vec, row],
        out_specs=[pl.BlockSpec((1, 128), lambda i: (0, 0)), row, row, vec],
        out_shape=[jax.ShapeDtypeStruct((1, 128), F32), jax.ShapeDtypeStruct((T, D), F32),
                   jax.ShapeDtypeStruct((T, D), BF16), jax.ShapeDtypeStruct((1, D), F32)],
        params=_params(("arbitrary",)))(h, g, tgt)


def _prev_idx(i, rows=TR):
    return jnp.maximum(i * (rows // HALO) - 1, 0)


def _next_idx(i, rows=TR):
    return jnp.minimum((i + 1) * (rows // HALO), T // HALO - 1)


def _causal_taps(ext):
    return pltpu.roll(ext, 2, 0)[HALO:], pltpu.roll(ext, 1, 0)[HALO:], ext[HALO:]


def _anticausal_taps(ext, n):
    rows = ext.shape[0]
    return pltpu.roll(ext, rows - 1, 0)[:n], pltpu.roll(ext, rows - 2, 0)[:n]


MIX_COLS = 512


def _mixer_in(hn, w_in, w):
    nc = D // MIX_COLS

    def body(h_ref, hh_ref, wb_ref, wc_ref, wu_ref, w_ref, b_ref, c_ref, u_ref, y_ref):
        i = pl.program_id(1)
        hv = h_ref[...]
        he = jnp.concatenate([hh_ref[...], hv], axis=0)
        ce = lax.dot_general(he, wc_ref[...], NN, preferred_element_type=F32).astype(BF16)
        ue = lax.dot_general(he, wu_ref[...], NN, preferred_element_type=F32).astype(BF16)
        bv = lax.dot_general(hv, wb_ref[...], NN, preferred_element_type=F32).astype(BF16)
        b_ref[...] = bv
        c_ref[...] = ce[HALO:]
        u_ref[...] = ue[HALO:]
        row = lax.broadcasted_iota(jnp.int32, (HALO + TS, 1), 0)
        cu = jnp.where(jnp.logical_or(i > 0, row >= HALO), ce.astype(F32) * ue.astype(F32), 0.0)
        x2, x1, x0 = _causal_taps(cu)
        wv = w_ref[...]
        cv = (x2 * wv[0:1] + x1 * wv[1:2]) + x0 * wv[2:3]
        y_ref[...] = (bv.astype(F32) * cv).astype(BF16)

    def cols(part):
        return pl.BlockSpec((D, MIX_COLS), lambda j, i: (0, part * nc + j))

    blk = pl.BlockSpec((TS, MIX_COLS), lambda j, i: (i, j))
    out = jax.ShapeDtypeStruct((T, D), BF16)
    return _pallas(
        body, name="l0_in", grid=(nc, T // TS),
        in_specs=[pl.BlockSpec((TS, D), lambda j, i: (i, 0)), pl.BlockSpec((HALO, D), lambda j, i: (_prev_idx(i, TS), 0)),
                  cols(0), cols(1), cols(2), pl.BlockSpec((3, MIX_COLS), lambda j, i: (0, j))],
        out_specs=[blk] * 4, out_shape=[out] * 4,
        params=_params(("parallel", "parallel")))(hn, hn, w_in, w_in, w_in, w)


def _mixer_out_bwd(dh, w_out, zb, zc, zu, w):
    last = T // TR - 1

    def body(dh_ref, dhn_ref, wo_ref, b_ref, bn_ref, c_ref, ch_ref, u_ref, uh_ref, w_ref, dz_ref, dw_ref):
        i = pl.program_id(0)
        dye = lax.dot_general(jnp.concatenate([dh_ref[...], dhn_ref[...]], axis=0), wo_ref[...], NT,
                              preferred_element_type=F32)
        cv_ = c_ref[...].astype(F32)
        uv = u_ref[...].astype(F32)
        cu = cv_ * uv
        cuh = jnp.where(i > 0, ch_ref[...].astype(F32) * uh_ref[...].astype(F32), 0.0)
        x2, x1, x0 = _causal_taps(jnp.concatenate([cuh, cu], axis=0))
        wv = w_ref[...]
        conv = (x2 * wv[0:1] + x1 * wv[1:2]) + x0 * wv[2:3]
        dyv = dye[:TR]
        dz_ref[:, 0:D] = (dyv * conv).astype(BF16)
        dconv = dyv * b_ref[...].astype(F32)
        dconv_n = jnp.where(i < last, dye[TR:] * bn_ref[...].astype(F32), 0.0)
        n1, n2 = _anticausal_taps(jnp.concatenate([dconv, dconv_n], axis=0), TR)
        dcu = (dconv * wv[2:3] + n1 * wv[1:2]) + n2 * wv[0:1]
        dz_ref[:, D:2 * D] = (dcu * uv).astype(BF16)
        dz_ref[:, 2 * D:3 * D] = (dcu * cv_).astype(BF16)
        part = jnp.concatenate([jnp.sum(dconv * x2, axis=0, keepdims=True),
                                jnp.sum(dconv * x1, axis=0, keepdims=True),
                                jnp.sum(dconv * x0, axis=0, keepdims=True)], axis=0)

        @pl.when(i == 0)
        def _():
            dw_ref[...] = part

        @pl.when(i > 0)
        def _():
            dw_ref[...] += part

    main = pl.BlockSpec((TR, D), lambda i: (i, 0))
    prev = pl.BlockSpec((HALO, D), lambda i: (_prev_idx(i), 0))
    nxt = pl.BlockSpec((HALO, D), lambda i: (_next_idx(i), 0))
    wspec = pl.BlockSpec((3, D), lambda i: (0, 0))
    return _pallas(
        body, name="d_l0_out", grid=(T // TR,),
        in_specs=[main, nxt, pl.BlockSpec((D, D), lambda i: (0, 0)), main, nxt, main, prev, main, prev, wspec],
        out_specs=[pl.BlockSpec((TR, 3 * D), lambda i: (i, 0)), wspec],
        out_shape=[jax.ShapeDtypeStruct((T, 3 * D), BF16), jax.ShapeDtypeStruct((3, D), F32)],
        params=_params(("arbitrary",)))(dh, dh, w_out, zb, zb, zc, zc, zu, zu, w)


def _sigmoid(x):
    return 0.5 * jnp.tanh(0.5 * x) + 0.5


def _ffn_up_act(name, hf, w_up, w, b):
    def body(h_ref, hh_ref, wg_ref, wv_ref, w_ref, b_ref, g_ref, v_ref, a_ref):
        i = pl.program_id(1)
        hv = h_ref[...]
        ge = lax.dot_general(jnp.concatenate([hh_ref[...], hv], axis=0), wg_ref[...], NT,
                             preferred_element_type=F32).astype(BF16)
        v = lax.dot_general(hv, wv_ref[...], NT, preferred_element_type=F32).astype(BF16)
        g_ref[...] = ge[HALO:]
        v_ref[...] = v
        ext = ge.astype(F32)
        row = lax.broadcasted_iota(jnp.int32, (HALO + TM, 1), 0)
        ext = jnp.where(jnp.logical_or(i > 0, row >= HALO), ext, 0.0)
        x2, x1, x0 = _causal_taps(ext)
        wv = w_ref[...]
        gc = ((x2 * wv[0:1] + x1 * wv[1:2]) + x0 * wv[2:3]) + b_ref[...]
        a_ref[...] = ((gc * _sigmoid(gc)) * v.astype(F32)).astype(BF16)

    blk = pl.BlockSpec((None, TM, FF_BLK), lambda j, i: (j, i, 0))
    out = jax.ShapeDtypeStruct((N_FF_BLK, T, FF_BLK), BF16)
    return _pallas(
        body, name=name, grid=(N_FF_BLK, T // TM),
        in_specs=[pl.BlockSpec((TM, D), lambda j, i: (i, 0)),
                  pl.BlockSpec((HALO, D), lambda j, i: (_prev_idx(i, TM), 0)),
                  pl.BlockSpec((None, None, FF_BLK, D), lambda j, i: (0, j, 0, 0)),
                  pl.BlockSpec((None, None, FF_BLK, D), lambda j, i: (0, j + N_FF_BLK, 0, 0)),
                  pl.BlockSpec((None, 3, FF_BLK), lambda j, i: (j, 0, 0)),
                  pl.BlockSpec((None, 1, FF_BLK), lambda j, i: (j, 0, 0))],
        out_specs=[blk, blk, blk], out_shape=[out, out, out],
        params=_params(("parallel", "parallel")))(hf, hf, w_up, w_up, w, b)


def _ffn_dact(name, dh, w_down4, g, v, w, b):
    last = T // TS - 1

    def body(dh_ref, dhn_ref, wd_ref, g_ref, gp_ref, gn_ref, v_ref, vn_ref, w_ref, b_ref, dg_ref, dv_ref, dw_ref, db_ref):
        i = pl.program_id(1)
        da = lax.dot_general(jnp.concatenate([dh_ref[...], dhn_ref[...]], axis=0), wd_ref[...], NT,
                             preferred_element_type=F32)
        row = lax.broadcasted_iota(jnp.int32, (TS + HALO, 1), 0)
        da = jnp.where(jnp.logical_or(i < last, row < TS), da, 0.0)
        gp = jnp.where(i > 0, gp_ref[...].astype(F32), 0.0)
        ext = jnp.concatenate([gp, g_ref[...].astype(F32), gn_ref[...].astype(F32)], axis=0)
        x2, x1, x0 = _causal_taps(ext)
        wv = w_ref[...]
        gc = ((x2 * wv[0:1] + x1 * wv[1:2]) + x0 * wv[2:3]) + b_ref[...]
        sg = _sigmoid(gc)
        vv = jnp.concatenate([v_ref[...].astype(F32), vn_ref[...].astype(F32)], axis=0)
        dv_ref[...] = (da[:TS] * (gc[:TS] * sg[:TS])).astype(BF16)
        dgc = (da * vv) * (sg * (1.0 + gc * (1.0 - sg)))
        n1, n2 = _anticausal_taps(dgc, TS)
        d0 = dgc[:TS]
        dg_ref[...] = ((d0 * wv[2:3] + n1 * wv[1:2]) + n2 * wv[0:1]).astype(BF16)
        part_w = jnp.concatenate([jnp.sum(d0 * x2[:TS], axis=0, keepdims=True),
                                  jnp.sum(d0 * x1[:TS], axis=0, keepdims=True),
                                  jnp.sum(d0 * x0[:TS], axis=0, keepdims=True)], axis=0)
        part_b = jnp.sum(d0, axis=0, keepdims=True)

        @pl.when(i == 0)
        def _():
            dw_ref[...] = part_w
            db_ref[...] = part_b

        @pl.when(i > 0)
        def _():
            dw_ref[...] += part_w
            db_ref[...] += part_b

    blk = pl.BlockSpec((None, TS, FF_BLK), lambda j, i: (j, i, 0))
    prev = pl.BlockSpec((None, HALO, FF_BLK), lambda j, i: (j, _prev_idx(i, TS), 0))
    nxt = pl.BlockSpec((None, HALO, FF_BLK), lambda j, i: (j, _next_idx(i, TS), 0))
    wspec = pl.BlockSpec((None, 3, FF_BLK), lambda j, i: (j, 0, 0))
    bspec = pl.BlockSpec((None, 1, FF_BLK), lambda j, i: (j, 0, 0))
    return _pallas(
        body, name=name, grid=(N_FF_BLK, T // TS),
        in_specs=[pl.BlockSpec((TS, D), lambda j, i: (i, 0)),
                  pl.BlockSpec((HALO, D), lambda j, i: (_next_idx(i, TS), 0)),
                  pl.BlockSpec((None, None, FF_BLK, D), lambda j, i: (0, j, 0, 0)),
                  blk, prev, nxt, blk, nxt, wspec, bspec],
        out_specs=[blk, blk, wspec, bspec],
        out_shape=[jax.ShapeDtypeStruct((N_FF_BLK, T, FF_BLK), BF16), jax.ShapeDtypeStruct((N_FF_BLK, T, FF_BLK), BF16),
                   jax.ShapeDtypeStruct((N_FF_BLK, 3, FF_BLK), F32), jax.ShapeDtypeStruct((N_FF_BLK, 1, FF_BLK), F32)],
        params=_params(("parallel", "arbitrary")))(dh, dh, w_down4, g, g, g, v, v, w, b)


def _rope_tables(pos, inv_freq):
    half = QK_ROPE // 2

    def body(p_ref, f_ref, c_ref, sa_ref, sb_ref):
        ang = p_ref[...].astype(F32) * f_ref[...]
        lane = lax.broadcasted_iota(jnp.int32, (T, 128), 1)
        c = jnp.cos(ang)
        s = jnp.sin(ang)
        c_ref[...] = jnp.where(lane < 2 * half, c, 0.0)
        sa_ref[...] = jnp.where(lane < half, -s, 0.0)
        sb_ref[...] = jnp.where(jnp.logical_and(lane >= half, lane < 2 * half), s, 0.0)

    return _pallas(
        body, name="rope_tables", in_specs=[VMEM_SPEC] * 2, out_specs=[VMEM_SPEC] * 3,
        out_shape=[jax.ShapeDtypeStruct((T, 128), F32)] * 3,
        params=pltpu.CompilerParams(vmem_limit_bytes=VMEM_LIMIT))(pos, inv_freq)


def _rotate(r, c, sa, sb, sign):
    return r * c + sign * (pltpu.roll(r, 96, 1) * sa + pltpu.roll(r, 32, 1) * sb)


def _q_up(cq, w_uq, tables):
    cos, sa, sb = tables

    def body(a_ref, b_ref, c_ref, sa_ref, sb_ref, o_ref):
        for h in range(N_HEADS):
            r = lax.dot_general(a_ref[...], b_ref[h], NT, preferred_element_type=F32)
            o_ref[h, :, :QK_NOPE] = r[:, :QK_NOPE].astype(BF16)
            o_ref[h, :, QK_NOPE:] = _rotate(r[:, QK_NOPE:], c_ref[...], sa_ref[...], sb_ref[...], 1.0).astype(BF16)

    tab = pl.BlockSpec((TS, 128), lambda i: (i, 0))
    return _pallas(
        body, name="q_up", grid=(T // TS,),
        in_specs=[pl.BlockSpec((TS, Q_LORA), lambda i: (i, 0)),
                  pl.BlockSpec((N_HEADS, QK_PAD, Q_LORA), lambda i: (0, 0, 0)), tab, tab, tab],
        out_specs=pl.BlockSpec((N_HEADS, TS, QK_PAD), lambda i: (0, i, 0)),
        out_shape=jax.ShapeDtypeStruct((N_HEADS, T, QK_PAD), BF16),
        params=_params(("parallel",)))(cq, w_uq, cos, sa, sb)


def _rope(name, x, tables, sign, out_dtype, reduce_groups=False):
    g, _, w = x.shape
    cos, sa, sb = tables

    def body(x_ref, c_ref, sa_ref, sb_ref, o_ref):
        xv = x_ref[...].astype(F32)
        if reduce_groups:
            acc = xv[0]
            for k in range(1, g):
                acc = acc + xv[k]
            xv = acc
        out = _rotate(xv[:, w - 128:], c_ref[...], sa_ref[...], sb_ref[...], sign)
        if w > 128:
            o_ref[:, :w - 128] = xv[:, :w - 128].astype(out_dtype)
        o_ref[:, w - 128:] = out.astype(out_dtype)

    tab = pl.BlockSpec((TM, 128), lambda h, i: (i, 0))
    if reduce_groups:
        x_spec = pl.BlockSpec((g, TM, w), lambda h, i: (0, i, 0))
        groups = 1
    else:
        x_spec = pl.BlockSpec((None, TM, w), lambda h, i: (h, i, 0))
        groups = g
    return _pallas(
        body, name=name, grid=(groups, T // TM), in_specs=[x_spec, tab, tab, tab],
        out_specs=pl.BlockSpec((None, TM, w), lambda h, i: (h, i, 0)),
        out_shape=jax.ShapeDtypeStruct((groups, T, w), out_dtype),
        params=_params(("parallel", "parallel")))(x, cos, sa, sb)


SCALE = (QK_NOPE + QK_ROPE) ** -0.5
LOG2E = 1.4426950408889634
SCALE2 = SCALE * LOG2E


def _diag_mask(transposed):
    shift = CHUNK.bit_length() - 1
    a = lax.broadcasted_iota(jnp.int32, (TQ, TQ), 0) >> shift
    b = lax.broadcasted_iota(jnp.int32, (TQ, TQ), 1) >> shift
    return (a <= b) if transposed else (b <= a)


def _as_row(col):
    return jnp.transpose(jnp.broadcast_to(col, (col.shape[0], 128)), (1, 0))[0:1]


def _keys(kn_ref, kr_ref, off):
    return jnp.concatenate([kn_ref[pl.ds(off, TQ), :], kr_ref[pl.ds(off, TQ), :]], axis=1)


def _attn_fwd(q, kn, kr, v):
    def body(q_ref, kn_ref, kr_ref, v_ref, o_ref, lse_ref):
        i = pl.program_id(1)
        qv = q_ref[...]

        def step(j, carry, masked):
            m, l, acc = carry
            off = pl.multiple_of(j * TQ, TQ)
            s = lax.dot_general(qv, _keys(kn_ref, kr_ref, off), NT, preferred_element_type=F32) * SCALE2
            if masked:
                s = jnp.where(_diag_mask(False), s, NEG_INF)
            m_new = jnp.maximum(m, jnp.max(s, axis=-1, keepdims=True))
            p = jnp.exp2(s - m_new)
            alpha = jnp.exp2(m - m_new)
            l = alpha * l + jnp.sum(p, axis=-1, keepdims=True)
            acc = alpha * acc + lax.dot_general(p.astype(BF16), v_ref[pl.ds(off, TQ), :], NN, preferred_element_type=F32)
            return m_new, l, acc

        init = (jnp.full((TQ, 1), NEG_INF, F32), jnp.zeros((TQ, 1), F32), jnp.zeros((TQ, V_HEAD), F32))
        carry = lax.fori_loop(0, i, lambda j, cr: step(j, cr, False), init)
        m, l, acc = step(i, carry, True)
        o_ref[...] = (acc / l).astype(BF16)
        lse_ref[...] = _as_row(m + jnp.log(l) * LOG2E)

    return _pallas(
        body, name="attn_fwd", grid=(N_HEADS, T // TQ),
        in_specs=[pl.BlockSpec((None, TQ, QK_PAD), lambda h, i: (h, i, 0)),
                  pl.BlockSpec((T, QK_NOPE), lambda h, i: (0, h)),
                  pl.BlockSpec((T, 128), lambda h, i: (0, 0)),
                  pl.BlockSpec((T, V_HEAD), lambda h, i: (0, h))],
        out_specs=[pl.BlockSpec((TQ, V_HEAD), lambda h, i: (i, h)), pl.BlockSpec((None, 1, TQ), lambda h, i: (h, 0, i))],
        out_shape=[jax.ShapeDtypeStruct((T, N_HEADS * V_HEAD), BF16), jax.ShapeDtypeStruct((N_HEADS, 1, T), F32)],
        params=_params(("parallel", "parallel")))(q, kn, kr, v)


def _attn_bwd(q, kn, kr, v, o, do, lse_row, tables):
    nq = T // TQ
    cos, sa, sb = tables

    def body(q_ref, kn_ref, kr_ref, v_ref, o_ref, do_ref, lse_ref, c_ref, sa_ref, sb_ref,
             dq_ref, dkn_ref, dkr_ref, dv_ref, dq_acc, dl_ref):
        j = pl.program_id(1)

        @pl.when(j == 0)
        def _():
            dq_acc[...] = jnp.zeros_like(dq_acc)
            for i in range(nq):
                rows = pl.ds(i * TQ, TQ)
                prod = do_ref[rows, :].astype(F32) * o_ref[rows, :].astype(F32)
                dl_ref[:, rows] = _as_row(jnp.sum(prod, axis=-1, keepdims=True))

        kk = jnp.concatenate([kn_ref[...], kr_ref[...]], axis=1)
        vv = v_ref[...]

        def step(i, carry, masked):
            dk, dv = carry
            off = pl.multiple_of(i * TQ, TQ)
            qi = q_ref[pl.ds(off, TQ), :]
            doi = do_ref[pl.ds(off, TQ), :]
            st = lax.dot_general(kk, qi, NT, preferred_element_type=F32) * SCALE2
            if masked:
                st = jnp.where(_diag_mask(True), st, NEG_INF)
            pt = jnp.exp2(st - lse_ref[:, pl.ds(off, TQ)])
            dv = dv + lax.dot_general(pt.astype(BF16), doi, NN, preferred_element_type=F32)
            dpt = lax.dot_general(vv, doi, NT, preferred_element_type=F32)
            dst = ((pt * (dpt - dl_ref[:, pl.ds(off, TQ)])) * SCALE).astype(BF16)
            dk = dk + lax.dot_general(dst, qi, NN, preferred_element_type=F32)
            dq_acc[pl.ds(off, TQ), :] += lax.dot_general(dst, kk, TN, preferred_element_type=F32)
            return dk, dv

        carry = step(j, (jnp.zeros((TQ, QK_PAD), F32), jnp.zeros((TQ, V_HEAD), F32)), True)
        dk, dv = lax.fori_loop(j + 1, nq, lambda i, cr: step(i, cr, False), carry)
        dkn_ref[...] = dk[:, :QK_NOPE].astype(BF16)
        dkr_ref[...] = dk[:, QK_NOPE:]
        dv_ref[...] = dv.astype(BF16)

        @pl.when(j == nq - 1)
        def _():
            dq = dq_acc[...]
            dq_ref[:, :QK_NOPE] = dq[:, :QK_NOPE].astype(BF16)
            dq_ref[:, QK_NOPE:] = _rotate(dq[:, QK_NOPE:], c_ref[...], sa_ref[...], sb_ref[...], -1.0).astype(BF16)

    row = pl.BlockSpec((None, 1, T), lambda h, j: (h, 0, 0))
    head = pl.BlockSpec((TQ, 128), lambda h, j: (j, h))
    whole = pl.BlockSpec((None, T, QK_PAD), lambda h, j: (h, 0, 0))
    tab = pl.BlockSpec((T, 128), lambda h, j: (0, 0))
    heads = pl.BlockSpec((T, V_HEAD), lambda h, j: (0, h))
    return _pallas(
        body, name="attn_bwd", grid=(N_HEADS, nq),
        in_specs=[whole, head, pl.BlockSpec((TQ, 128), lambda h, j: (j, 0)), head, heads, heads, row, tab, tab, tab],
        out_specs=[whole, head, pl.BlockSpec((None, TQ, 128), lambda h, j: (h, j, 0)), head],
        out_shape=[jax.ShapeDtypeStruct((N_HEADS, T, QK_PAD), BF16), jax.ShapeDtypeStruct((T, N_HEADS * QK_NOPE), BF16),
                   jax.ShapeDtypeStruct((N_HEADS, T, 128), F32), jax.ShapeDtypeStruct((T, N_HEADS * V_HEAD), BF16)],
        scratch_shapes=[pltpu.VMEM((T, QK_PAD), F32), pltpu.VMEM((1, T), F32)],
        params=_params(("parallel", "arbitrary")))(q, kn, kr, v, o, do, lse_row, cos, sa, sb)


def _ffn_gup(name, dg, dv, hf):
    def body(dg_ref, dv_ref, hf_ref, o_ref):
        j = pl.program_id(0)

        @pl.when(j < N_FF_BLK)
        def _():
            o_ref[...] = lax.dot_general(dg_ref[...], hf_ref[...], TN, preferred_element_type=F32).astype(BF16)

        @pl.when(j >= N_FF_BLK)
        def _():
            o_ref[...] = lax.dot_general(dv_ref[...], hf_ref[...], TN, preferred_element_type=F32).astype(BF16)

    return _pallas(
        body, name=name, grid=(N_DEV,),
        in_specs=[pl.BlockSpec((None, T, FF_BLK), lambda j: (jnp.minimum(j, N_FF_BLK - 1), 0, 0)),
                  pl.BlockSpec((None, T, FF_BLK), lambda j: (jnp.maximum(j - N_FF_BLK, 0), 0, 0)),
                  pl.BlockSpec((T, D), lambda j: (0, 0))],
        out_specs=pl.BlockSpec((None, FF_BLK, D), lambda j: (j, 0, 0)),
        out_shape=jax.ShapeDtypeStruct((N_DEV, FF_BLK, D), BF16), params=_params(("parallel",)))(dg, dv, hf)


def _ffn_layer_fwd(tag, h, gain, ex):
    hf = _rms_fwd(f"{tag}_norm", h, gain)
    g, v, act = _ffn_up_act(f"{tag}_up", hf, ex.need(f"ffn_w_up{tag[1]}", hf), ex.need(f"ffn_cw{tag[1]}", hf),
                            ex.need(f"ffn_cb{tag[1]}", hf))
    ex.at(f"{tag}_up", act)
    rows = pl.BlockSpec((TS, D), lambda i: (i, 0))
    out = _mm_sum(f"{tag}_down",
                  [(act, pl.BlockSpec((N_FF_BLK, TS, FF_BLK), lambda i: (0, i, 0)), ex.need(f"ffn_w_down{tag[1]}", act),
                    pl.BlockSpec((None, N_FF_BLK, FF_BLK, D), lambda i: (0, 0, 0, 0)), NN, 0)],
                  grid=(T // TS,), o_spec=rows, o_shape=(T, D), o_dtype=F32, add=h)
    ex.at(f"{tag}_down", out)
    return out, (hf, g, v, act)


def _ffn_layer_bwd(tag, h, gain, ex, saved, dh, dh_bf):
    hf, g, v, act = saved
    layer = tag[1]
    w_up, w_down4 = ex.need(f"ffn_w_up{layer}", dh_bf), ex.need(f"ffn_w_down{layer}", dh_bf)
    dg, dv, dcw, dcb = _ffn_dact(f"{tag}_dact", dh_bf, w_down4, g, v, ex.need(f"ffn_cw{layer}", dh_bf),
                                 ex.need(f"ffn_cb{layer}", dh_bf))
    ex.at(f"{tag}_dact", dg)
    g_down = _mm(f"{tag}_gdown", act, dh_bf, grid=(N_FF_BLK,),
                 a_spec=pl.BlockSpec((None, T, FF_BLK), lambda j: (j, 0, 0)),
                 b_spec=pl.BlockSpec((T, D), lambda j: (0, 0)),
                 o_spec=pl.BlockSpec((FF_BLK, D), lambda j: (j, 0)),
                 o_shape=(D_FF, D), o_dtype=BF16, dims=TN)
    g_up = _ffn_gup(f"{tag}_gup", dg, dv, hf)
    ex.grad("ffn_w_up", int(layer), g_up.reshape(1, N_DEV, FF_BLK, D))
    ex.grad("ffn_w_down", int(layer), g_down.reshape(1, N_DEV, D_FF // N_DEV, D))
    ex.at(f"{tag}_gup", g_up)
    part = pl.BlockSpec((N_FF_BLK, TR, FF_BLK), lambda i: (0, i, 0))
    dh_in, dh_in_bf, dgain = _mm_sum(
        f"{tag}_dhf",
        [(dg, part, w_up, pl.BlockSpec((None, N_FF_BLK, FF_BLK, D), lambda i: (0, 0, 0, 0)), NN, 0),
         (dv, part, w_up, pl.BlockSpec((None, N_FF_BLK, FF_BLK, D), lambda i: (0, 1, 0, 0)), NN, 0)],
        grid=(T // TR,), o_spec=pl.BlockSpec((TR, D), lambda i: (i, 0)), o_shape=(T, D), o_dtype=F32,
        norm_bwd=(h, [gain], dh))
    ex.at(f"{tag}_dhf", dh_in)
    return dh_in, dh_in_bf, dgain[0], dcw, dcb


def _local_step(x, pos, tgt, rep, ex):
    attn_norm, ffn_norm, final_norm = rep["attn_norm"], rep["ffn_norm"], rep["final_norm"]
    half = QK_ROPE // 2
    inv = 1.0 / (ROPE_THETA ** (jnp.arange(half, dtype=F32) / half))
    inv_freq = jnp.concatenate([inv, inv, jnp.zeros((128 - 2 * half,), F32)]).reshape(1, 128)
    tables = _rope_tables(pos, inv_freq)

    hn0 = _rms_fwd("l0_norm", x, attn_norm[0:1])
    w_in = ex.need("sc_w_in", hn0)
    ex.at("mixer_ready", hn0)
    zb, zc, zu, y = _mixer_in(hn0, w_in, ex.need("sc_conv_w", hn0))
    ex.at("l0_in", y)
    h1 = _mm_rows("l0_out", y, ex.need("sc_w_out", y), NN, F32, D, tn=512, add=x)
    ex.at("l0_out", h1)
    h2, ffn0 = _ffn_layer_fwd("f0", h1, ffn_norm[0:1], ex)

    hk, hn1 = _rms_fwd2("h2_norms", h2, rep["kv_in_norm"], attn_norm[1:2])
    ckv_raw, ckv, kr = _kv_down(hk, ex.need("w_dkv", hk), ex.need("w_kr", hk), rep["kv_latent_norm"], tables)
    kn, vv = _kv_up(ckv, ex.need("w_uk", ckv), ex.need("w_uv", ckv))

    cq_raw, cq = _down_norm("q_down", hn1, ex.need("w_dq", hn1), rep["q_latent_norm"])
    w_uq = ex.need("w_uq", cq)
    q = _q_up(cq, w_uq, tables)
    o, lse = _attn_fwd(q, kn, kr, vv)
    ex.at("attn_fwd", o)
    w_o = ex.need("w_o", o)
    h3 = _mm_rows("attn_out", o, w_o, NN, F32, D, tn=512, add=h2)
    h4, ffn1 = _ffn_layer_fwd("f1", h3, ffn_norm[1:2], ex)

    loss, dh4, dh4_bf, d_final = _final(h4, final_norm.reshape(1, D), tgt)

    dh3, dh3_bf, d_fn1, dcw1, dcb1 = _ffn_layer_bwd("f1", h3, ffn_norm[1:2], ex, ffn1, dh4, dh4_bf)
    ex.at("f1_bwd", dh3)

    do = _mm_rows("d_attn_out", dh3_bf, w_o, NT, BF16, N_HEADS * V_HEAD)
    ex.grad("w_o", None, _mm_wgrad("g_w_o", o, dh3_bf).reshape(1, N_DEV, D // N_DEV, D))
    dq_pre, dkn, dkr, dvv = _attn_bwd(q, kn, kr, vv, o, do, lse, tables)
    def rows_of(a):
        return a[None], pl.BlockSpec((1, TS, a.shape[1]), lambda i: (0, i, 0))

    def whole(wt):
        return wt[None], pl.BlockSpec((1,) + wt.shape, lambda i: (0, 0, 0))

    def row_blocks(d):
        return dict(grid=(T // TS,), o_spec=pl.BlockSpec((TS, d), lambda i: (i, 0)), o_shape=(T, d), o_dtype=F32)

    _, dcq_raw_bf, (d_qln,) = _mm_sum(
        "d_q_up", [(dq_pre, pl.BlockSpec((N_HEADS, TS, QK_PAD), lambda i: (0, i, 0)),
                    w_uq, pl.BlockSpec((N_HEADS, QK_PAD, Q_LORA), lambda i: (0, 0, 0)), NN, 0)],
        norm_bwd=(cq_raw, [rep["q_latent_norm"]], None), **row_blocks(Q_LORA))
    g_uq = _mm("g_w_uq", dq_pre, cq, grid=(N_HEADS,),
               a_spec=pl.BlockSpec((None, T, QK_PAD), lambda h: (h, 0, 0)),
               b_spec=pl.BlockSpec((T, Q_LORA), lambda h: (0, 0)),
               o_spec=pl.BlockSpec((None, QK_PAD, Q_LORA), lambda h: (h, 0, 0)),
               o_shape=(N_HEADS, QK_PAD, Q_LORA), o_dtype=BF16, dims=TN)
    ex.grad("w_uq", None, g_uq[:, :QK_NOPE + QK_ROPE].reshape(1, N_DEV, QK_NOPE + QK_ROPE, Q_LORA))
    ex.grad("w_dq", None, _mm_wgrad("g_w_dq", hn1, dcq_raw_bf).reshape(1, N_DEV, D // N_DEV, Q_LORA))

    _, dckv_raw_bf, (d_kvln,) = _mm_sum(
        "d_kv_up", [(*rows_of(dkn), *whole(ex.need("w_uk", dkn)), NT, 0),
                    (*rows_of(dvv), *whole(ex.need("w_uv", dvv)), NT, 0)],
        norm_bwd=(ckv_raw, [rep["kv_latent_norm"]], None), **row_blocks(KV_LORA))
    ex.grad("w_uk", None, _mm_wgrad("g_w_uk", ckv, dkn))
    ex.grad("w_uv", None, _mm_wgrad("g_w_uv", ckv, dvv))
    dkr_raw_bf = _rope("dk_rope", dkr, tables, -1.0, BF16, reduce_groups=True).reshape(T, 128)
    ex.grad("w_dkv", None, _mm_wgrad("g_w_dkv", hk, dckv_raw_bf).reshape(1, N_DEV, D // N_DEV, KV_LORA))
    ex.grad("w_kr", None, _mm_wgrad("g_w_kr", dkr_raw_bf, hk)[:QK_ROPE])

    dh2, dh2_bf, (d_an1, d_kvin) = _mm_sum(
        "d_h2", [(*rows_of(dcq_raw_bf), *whole(ex.need("w_dq", dcq_raw_bf)), NT, 0),
                 (*rows_of(dckv_raw_bf), *whole(ex.need("w_dkv", dckv_raw_bf)), NT, 1),
                 (*rows_of(dkr_raw_bf), *whole(ex.need("w_kr", dkr_raw_bf)), NN, 1)],
        norm_bwd=(h2, [attn_norm[1:2], rep["kv_in_norm"]], dh3), **row_blocks(D))
    ex.at("kv_bwd", dh2)

    dh1, dh1_bf, d_fn0, dcw0, dcb0 = _ffn_layer_bwd("f0", h1, ffn_norm[0:1], ex, ffn0, dh2, dh2_bf)
    ex.at("f0_bwd", dh1)

    ex.grad("sc_w_out", None, _mm_wgrad("g_sc_w_out", y, dh1_bf).reshape(1, N_DEV, D // N_DEV, D))
    dz, d_scw = _mixer_out_bwd(dh1_bf, ex.need("sc_w_out", dh1_bf), zb, zc, zu, ex.need("sc_conv_w", dh1_bf))
    g_in = _mm_wgrad("g_sc_w_in", hn0, dz)
    ex.grad("sc_w_in", None, g_in)
    ex.at("sc_bwd", g_in)
    ex.at("d_l0_in", g_in)
    grad_x, _, (d_an0,) = _mm_sum(
        "d_l0_in", [(*rows_of(dz), *whole(ex.need("sc_w_in", dz)), NT, 0)],
        norm_bwd=(x, [attn_norm[0:1]], dh1), **row_blocks(D))

    small = {
        "attn_norm": jnp.concatenate([d_an0, d_an1], axis=0),
        "ffn_norm": jnp.concatenate([d_fn0, d_fn1], axis=0),
        "final_norm": d_final.reshape(D),
        "kv_in_norm": d_kvin.reshape(D),
        "kv_latent_norm": d_kvln.reshape(KV_LORA),
        "q_latent_norm": d_qln,
        "ffn_conv_b": jnp.stack([dcb0, dcb1]).transpose(0, 2, 1, 3).reshape(2, D_FF),
        "sc_conv_w": d_scw,
        "ffn_conv_w": jnp.stack([dcw0, dcw1]).transpose(0, 2, 1, 3).reshape(2, 3, D_FF),
    }
    return loss, grad_x, small


def _place():
    return lax.axis_index("x"), lax.axis_index("y"), lax.axis_index("c")


def _peers():
    x, y, c = _place()
    return (x, y, 1 - c), [(1 - x, y), (x, 1 - y), (1 - x, 1 - y)]


def _window(ref, kind, dev):
    if kind == "blocked":
        return ref.at[:, dev]
    width = ref.shape[-1] // N_DEV
    return ref.at[:, pl.ds(pl.multiple_of(dev * width, 128), width)]


HBM_SPEC = pl.BlockSpec(memory_space=pltpu.HBM)
SEM_SPEC = pl.BlockSpec(memory_space=pltpu.SEMAPHORE)
EFFECT = pltpu.SideEffectType.DATAFLOW_SIDE_EFFECTING
TOKEN = jax.ShapeDtypeStruct((8, 128), F32)


def _hbm(a):
    return pltpu.with_memory_space_constraint(a, pltpu.HBM)


def _copies_start(name, jobs):
    nj = len(jobs)
    counts = [(len(srcs), len(lands)) for srcs, lands, _, _ in jobs]
    n_arr = sum(ns + nl for ns, nl in counts)

    def body(*refs):
        sems, token = refs[n_arr:n_arr + 2 * nj], refs[-1]
        at = 0
        for j, ((ns, nl), (_, _, ncopy, plan)) in enumerate(zip(counts, jobs)):
            copies = plan(refs[at:at + ns], refs[at + ns:at + ns + nl])
            assert len(copies) == ncopy
            for k, (sent, dst, to, _) in enumerate(copies):
                pltpu.make_async_remote_copy(src_ref=sent, dst_ref=dst, send_sem=sems[2 * j].at[k],
                                             recv_sem=sems[2 * j + 1].at[k], device_id=to, device_id_type=MESH).start()
            at += ns + nl
        token[...] = jnp.zeros_like(token)

    arrays = [a for srcs, lands, _, _ in jobs for a in list(srcs) + list(lands)]
    sem_shapes = [pltpu.SemaphoreType.DMA((ncopy,)) for _, _, ncopy, _ in jobs for _ in range(2)]
    outs = pl.pallas_call(
        body, name=name, in_specs=[HBM_SPEC] * n_arr,
        out_specs=[SEM_SPEC] * (2 * nj) + [HBM_SPEC] * n_arr + [VMEM_SPEC],
        out_shape=sem_shapes + [pltpu.HBM(a.shape, a.dtype) for a in arrays] + [TOKEN],
        input_output_aliases={i: 2 * nj + i for i in range(n_arr)},
        compiler_params=pltpu.CompilerParams(has_side_effects=EFFECT))(*[_hbm(a) for a in arrays])
    _Chain.last = outs[-1]
    flights, at = [], 2 * nj
    for j, (ns, nl) in enumerate(counts):
        flights.append((outs[2 * j], outs[2 * j + 1], list(outs[at:at + ns]), list(outs[at + ns:at + ns + nl])))
        at += ns + nl
    return flights


def _copies_wait(name, started, ncopy, plan):
    send, recv, srcs, lands = started
    ns, nl = len(srcs), len(lands)

    def body(*refs):
        send_ref, recv_ref, token = refs[ns + nl], refs[ns + nl + 1], refs[-1]
        copies = plan(refs[:ns], refs[ns:ns + nl])
        assert len(copies) == ncopy
        for k, (sent, _, to, landed) in enumerate(copies):
            cp = pltpu.make_async_remote_copy(src_ref=sent, dst_ref=landed, send_sem=send_ref.at[k],
                                              recv_sem=recv_ref.at[k], device_id=to, device_id_type=MESH)
            cp.wait_send()
            cp.wait_recv()
        token[...] = jnp.zeros_like(token)

    arrays = list(srcs) + list(lands)
    outs = pl.pallas_call(
        body, name=name, in_specs=[HBM_SPEC] * (ns + nl) + [SEM_SPEC] * 2 + [ANY_SPEC],
        out_specs=[HBM_SPEC] * (ns + nl) + [VMEM_SPEC], out_shape=[pltpu.HBM(a.shape, a.dtype) for a in arrays] + [TOKEN],
        input_output_aliases={i: i for i in range(ns + nl)},
        compiler_params=pltpu.CompilerParams(has_side_effects=EFFECT))(*arrays, send, recv, _Chain.last)
    _Chain.last = outs[-1]
    return list(outs[:ns]), list(outs[ns:-1])


def _plan_gather_chips(kinds):
    def plan(srcs, lands):
        x, y, c = _place()
        sibling, chips = _peers()
        out = []
        for t, kind in enumerate(kinds):
            mine = _window(lands[t], kind, 4 * x + 2 * y + c)
            out.append((srcs[t], mine, (x, y, c), mine))
            out.append((srcs[t], mine, sibling, _window(lands[t], kind, 4 * x + 2 * y + 1 - c)))
            for px, py in chips:
                out.append((srcs[t], mine, (px, py, c), _window(lands[t], kind, 4 * px + 2 * py + c)))
        return out
    return plan, 5 * len(kinds)


def _plan_gather_all(n):
    def plan(srcs, lands):
        x, y, c = _place()
        out = []
        for t in range(n):
            mine = lands[t].at[:, 4 * x + 2 * y + c]
            for m in range(N_DEV):
                px, py, pc = (1 - x if m & 4 else x), (1 - y if m & 2 else y), (1 - c if m & 1 else c)
                out.append((srcs[t], mine, (px, py, pc), lands[t].at[:, 4 * px + 2 * py + pc]))
        return out
    return plan, N_DEV * n


def _plan_gather_sibling(kinds):
    def plan(srcs, lands):
        _, _, c = _place()
        sibling, chips = _peers()
        out = []
        for t, kind in enumerate(kinds):
            for px, py in chips:
                w = _window(lands[t], kind, 4 * px + 2 * py + c)
                out.append((w, w, sibling, _window(lands[t], kind, 4 * px + 2 * py + 1 - c)))
        return out
    return plan, 3 * len(kinds)


def _plan_scatter_sibling(kinds):
    def plan(srcs, lands):
        _, _, c = _place()
        sibling, _ = _peers()
        out = []
        for t, kind in enumerate(kinds):
            for k in range(N_CHIP):
                out.append((_window(srcs[t], kind, 2 * k + 1 - c), lands[t].at[k], sibling, lands[t].at[k]))
        return out
    return plan, N_CHIP * len(kinds)


def _plan_scatter_chips(n):
    def plan(srcs, lands):
        x, y, c = _place()
        _, chips = _peers()
        out = []
        for t in range(n):
            for px, py in chips:
                out.append((srcs[t].at[2 * px + py], lands[t].at[2 * x + y], (px, py, c), lands[t].at[2 * px + py]))
        return out
    return plan, 3 * n


def _landing(shard, kind):
    if kind == "blocked":
        return lax.empty((shard.shape[0], N_DEV) + shard.shape[1:], shard.dtype)
    return lax.empty((shard.shape[0], N_DEV * shard.shape[1]), shard.dtype)


def _chip_sums(name, grads, kinds, recvs, c):
    n = len(grads)
    in_specs, out_specs, out_shape, args = [], [], [], []
    for gr, kind, rv in zip(grads, kinds, recvs):
        if kind == "blocked":
            rows, w = gr.shape[2], gr.shape[3]
            in_specs.append(pl.BlockSpec((None, None, rows, w), lambda k, cref: (0, 2 * k + cref[0], 0, 0)))
        else:
            rows, w = gr.shape[0], gr.shape[1] // N_DEV
            in_specs.append(pl.BlockSpec((rows, w), lambda k, cref: (0, 2 * k + cref[0])))
        blk = pl.BlockSpec((None, rows, w), lambda k, cref: (k, 0, 0))
        in_specs.append(blk)
        out_specs.append(blk)
        out_shape.append(jax.ShapeDtypeStruct((N_CHIP, rows, w), BF16))
        args += [gr, rv.reshape(N_CHIP, rows, w)]

    def body(*refs):
        for t in range(n):
            g_ref, r_ref, o_ref = refs[1 + 2 * t], refs[2 + 2 * t], refs[1 + 2 * n + t]
            o_ref[...] = (g_ref[...].astype(F32) + r_ref[...].astype(F32)).astype(BF16)

    return _pallas(body, name=name, n_prefetch=1, grid=(N_CHIP,), in_specs=in_specs, out_specs=out_specs,
                   out_shape=out_shape, params=_params(("parallel",)))(c, *args)


def _adamw_math(g, wv, mv, vv):
    m = ADAM_B1 * mv + (1.0 - ADAM_B1) * g
    v = ADAM_B2 * vv + (1.0 - ADAM_B2) * (g * g)
    m_hat = m / (1.0 - ADAM_B1 ** ADAM_STEP)
    v_hat = v / (1.0 - ADAM_B2 ** ADAM_STEP)
    delta = -ADAM_LR * (m_hat / (jnp.sqrt(v_hat) + ADAM_EPS) + ADAM_WD * wv)
    return delta, m, v


ADAM_STEPS = 2


def _adamw_group(name, items, chip_ids):
    n = len(items)
    in_specs, out_specs, out_shape, args, prevs = [], [], [], [chip_ids], []
    for own, recv, w3, m3, v3, layer, _ in items:
        nl, rows, w = w3.shape
        tr = rows // ADAM_STEPS
        assert tr % 16 == 0, (name, rows)
        in_specs += [pl.BlockSpec((None, tr, w), lambda i, ids, slot=slot: (ids[slot], i, 0)) for slot in range(4)]
        slab = pl.BlockSpec((None, tr, w), lambda i, ids, layer=layer: (layer, i, 0))
        in_specs += [slab] * 3
        out_specs += [slab] * 4
        out_shape += [jax.ShapeDtypeStruct((nl, rows, w), F32)] * 4
        args += [own, recv, recv, recv, w3, m3, v3]
    aliases = {}
    for t, item in enumerate(items):
        if item[6] is not None:
            for k in range(4):
                aliases[len(args) + k] = 4 * t + k
            in_specs += [ANY_SPEC] * 4
            args += list(item[6])
            prevs.append(t)
    n_in = 1 + 7 * n + 4 * len(prevs)

    def body(*refs):
        for t in range(n):
            own_ref, r1_ref, r2_ref, r3_ref, w_ref, m_ref, v_ref = refs[1 + 7 * t:8 + 7 * t]
            g_ref, d_ref, nm_ref, nv_ref = refs[n_in + 4 * t:n_in + 4 * t + 4]
            g = ((own_ref[...].astype(F32) + r1_ref[...].astype(F32)) + r2_ref[...].astype(F32)) + r3_ref[...].astype(F32)
            g_ref[...] = g
            d_ref[...], nm_ref[...], nv_ref[...] = _adamw_math(g, w_ref[...], m_ref[...], v_ref[...])

    outs = _pallas(body, name=name, n_prefetch=1, grid=(ADAM_STEPS,), in_specs=in_specs, out_specs=out_specs,
                   out_shape=out_shape, aliases=aliases, params=_params(("parallel",)))(*args)
    return [list(outs[4 * t:4 * t + 4]) for t in range(n)]


def _adamw_small(gathered, ws, ms, vs):
    n = len(gathered)
    full = [w is not None for w in ws]
    args = list(gathered)
    out_shape = []
    for t in range(n):
        shape = jax.ShapeDtypeStruct(gathered[t].shape[2:], F32)
        if full[t]:
            args += [ws[t], ms[t], vs[t]]
            out_shape += [shape] * 4
        else:
            out_shape += [shape]

    def body(*refs):
        i_in, i_out = n, len(args)
        for t in range(n):
            p_ref = refs[t]
            g = p_ref[0, 0]
            for k in range(1, N_DEV):
                g = g + p_ref[0, k]
            refs[i_out][...] = g
            if full[t]:
                w_ref, m_ref, v_ref = refs[i_in:i_in + 3]
                refs[i_out + 1][...], refs[i_out + 2][...], refs[i_out + 3][...] = _adamw_math(
                    g, w_ref[...], m_ref[...], v_ref[...])
                i_in += 3
                i_out += 4
            else:
                i_out += 1

    outs = _pallas(body, name="adamw_small", in_specs=[VMEM_SPEC] * len(args), out_specs=[VMEM_SPEC] * len(out_shape),
                   out_shape=out_shape, params=pltpu.CompilerParams(vmem_limit_bytes=VMEM_LIMIT))(*args)
    result, i = [], 0
    for t in range(n):
        k = 4 if full[t] else 1
        result.append(list(outs[i:i + k]))
        i += k
    return result


def _adamw_plain(name, gs, ws, ms, vs):
    n = len(gs)

    def body(*refs):
        for t in range(n):
            g_ref, w_ref, m_ref, v_ref = refs[4 * t:4 * t + 4]
            outs = refs[4 * n + 3 * t:4 * n + 3 * t + 3]
            outs[0][...], outs[1][...], outs[2][...] = _adamw_math(g_ref[...], w_ref[...], m_ref[...], v_ref[...])

    args, out_shape = [], []
    for g, w, m, v in zip(gs, ws, ms, vs):
        args += [g, w, m, v]
        out_shape += [jax.ShapeDtypeStruct(w.shape, F32)] * 3
    outs = _pallas(body, name=name, in_specs=[VMEM_SPEC] * len(args), out_specs=[VMEM_SPEC] * len(out_shape),
                   out_shape=out_shape, params=pltpu.CompilerParams(vmem_limit_bytes=VMEM_LIMIT))(*args)
    return [list(outs[3 * t:3 * t + 3]) for t in range(n)]


KIND = {"sc_w_in": "cols", "sc_w_out": "blocked", "w_dkv": "blocked", "w_kr": "cols", "w_uk": "cols", "w_uv": "cols",
        "w_dq": "blocked", "w_uq": "blocked", "w_o": "blocked", "ffn_w_up": "blocked", "ffn_w_down": "blocked",
        "conv": "blocked"}
GATHER_GROUPS = (("mixer", ("sc_w_in", "sc_w_out", "conv")),
                 ("up0", ("ffn_w_up0",)),
                 ("down0", ("ffn_w_down0",)),
                 ("attn", ("w_dkv", "w_kr", "w_uk", "w_uv", "w_dq", "w_uq", "w_o")),
                 ("ffn1", ("ffn_w_up1", "ffn_w_down1")))
SCATTER_GROUPS = (("ffn1", (("ffn_w_up", 1), ("ffn_w_down", 1))),
                  ("attn", (("w_o", None), ("w_uq", None), ("w_dq", None), ("w_uk", None), ("w_uv", None),
                            ("w_dkv", None), ("w_kr", None))),
                  ("ffn0", (("ffn_w_up", 0), ("ffn_w_down", 0))),
                  ("mixer", (("sc_w_out", None), ("sc_w_in", None))))
SCHEDULE = {
    "begin": (("gather_start", "mixer"),),
    "mixer_ready": (("gather_start", "up0"),),
    "l0_out": (("gather_forward", "up0"), ("gather_start", "down0")),
    "f0_up": (("gather_forward", "down0"), ("gather_start", "attn")),
    "f0_down": (("gather_forward", "attn"), ("gather_start", "ffn1")),
    "attn_fwd": (("gather_forward", "ffn1"),),
    "f1_gup": (("scatter_sibling", "ffn1"),),
    "f1_dhf": (("scatter_chips", "ffn1"),),
    "kv_bwd": (("scatter_sibling", "attn"), ("scatter_done", "ffn1")),
    "f0_dact": (("scatter_chips", "attn"),),
    "f0_gup": (("scatter_sibling", "ffn0"),),
    "f0_dhf": (("scatter_chips", "ffn0"),),
    "f0_bwd": (("scatter_done", "attn"),),
    "sc_bwd": (("scatter_sibling", "mixer"),),
    "d_l0_in": (("scatter_chips", "mixer"),),
}
FINISH = (("scatter_done", "ffn0"), ("scatter_done", "mixer"))
STAGES = {"gather_start": 1, "gather_forward": 2, "gather_done": 3,
          "scatter_sibling": 1, "scatter_chips": 2, "scatter_done": 3}
SMALL_W_ROWS = 24


def _pack(arrays, rows):
    flat = jnp.concatenate([a.reshape(-1).astype(F32) for a in arrays])
    return jnp.pad(flat, (0, rows * 128 - flat.shape[0])).reshape(rows, 128)


STORED_TRANSPOSED = ("ffn_w_up", "w_uq", "w_kr")


def _stored(name, a):
    return jnp.swapaxes(a, -1, -2) if name in STORED_TRANSPOSED else a


def _base(name):
    if name.startswith("ffn_w_") and name[-1] in "01":
        return name[:-1], int(name[-1])
    return name, None


class _Exchange:
    def __init__(self, wts, mom, var, ffn_conv_b):
        self.wts, self.mom, self.var = wts, mom, var
        x, y, c = _place()
        self.c_arr = jnp.reshape(c, (1,)).astype(jnp.int32)
        chip = 2 * x + y
        self.chip_ids = jnp.stack([chip, chip ^ 1, chip ^ 2, chip ^ 3]).astype(jnp.int32)
        self.ready = {"ffn_cb0": ffn_conv_b.reshape(2, N_FF_BLK, 1, FF_BLK)[0],
                      "ffn_cb1": ffn_conv_b.reshape(2, N_FF_BLK, 1, FF_BLK)[1]}
        self.gathers, self.group_of = {}, {}
        self.grads, self.scatters, self.results, self.queue = {}, {}, {}, []
        for gname, names in GATHER_GROUPS:
            self.gathers[gname] = dict(stage=0, names=names, kinds=[KIND[_base(nm)[0]] for nm in names])
            for nm in names:
                self.group_of[nm] = gname
        for nm in ("sc_conv_w", "ffn_cw0", "ffn_cw1"):
            self.group_of[nm] = "mixer"
        self.at("begin", None)

    def _shard(self, name):
        if name == "conv":
            return _pack([self.wts["sc_conv_w"], self.wts["ffn_conv_w"]], SMALL_W_ROWS).reshape(1, SMALL_W_ROWS, 128)
        base, layer = _base(name)
        a = _stored(base, self.wts[base])
        if layer is not None:
            a = a[layer:layer + 1]
        if KIND[base] == "cols":
            return a.reshape(a.shape[-2], a.shape[-1]).astype(BF16)
        return a.reshape((-1,) + a.shape[-2:]).astype(BF16)

    def _start(self, name, srcs, lands, ncopy, plan, st):
        self.queue.append((name, (srcs, lands, ncopy, plan), st))

    def _flush(self):
        if self.queue:
            flights = _copies_start("__".join(name for name, _, _ in self.queue), [job for _, job, _ in self.queue])
            for (_, _, st), flight in zip(self.queue, flights):
                st["flight"] = flight
            self.queue = []

    def _flight(self, st):
        self._flush()
        return st["flight"]

    def _gather_to(self, gname, stage, after):
        st = self.gathers[gname]
        if st["stage"] < 1 <= stage:
            shards = [self._shard(nm) for nm in st["names"]]
            lands = [_landing(s, kind) for s, kind in zip(shards, st["kinds"])]
            plan, ncopy = _plan_gather_chips(st["kinds"])
            self._start(f"ag_{gname}_chips", shards, lands, ncopy, plan, st)
            st["stage"] = 1
        if st["stage"] < 2 <= stage:
            plan, ncopy = _plan_gather_chips(st["kinds"])
            _, lands = _copies_wait(f"ag_{gname}_chips_wait", self._flight(st), ncopy, plan)
            plan, ncopy = _plan_gather_sibling(st["kinds"])
            self._start(f"ag_{gname}_sibling", [], lands, ncopy, plan, st)
            st["stage"] = 2
        if st["stage"] < 3 <= stage:
            plan, ncopy = _plan_gather_sibling(st["kinds"])
            _, lands = _copies_wait(f"ag_{gname}_sibling_wait", self._flight(st), ncopy, plan)
            for nm, land in zip(st["names"], lands):
                self._arrived(nm, land)
            st["stage"] = 3

    def _arrived(self, name, land):
        if name == "conv":
            conv = land.reshape(N_DEV, SMALL_W_ROWS * 128)
            self.ready["sc_conv_w"] = conv[:, :3 * 128].reshape(N_DEV, 3, 128).transpose(1, 0, 2).reshape(3, D)
            fcw = conv[:, 3 * 128:3 * 128 + 6 * 352].reshape(N_DEV, 2, 3, 352).transpose(1, 2, 0, 3)
            fcw = fcw.reshape(2, 3, N_FF_BLK, FF_BLK).transpose(0, 2, 1, 3)
            self.ready["ffn_cw0"], self.ready["ffn_cw1"] = fcw[0], fcw[1]
        elif name in ("sc_w_in", "w_uk", "w_uv") or name.startswith("ffn_w_up"):
            self.ready[name] = land
        elif name.startswith("ffn_w_down"):
            self.ready[name] = land.reshape(1, N_FF_BLK, FF_BLK, D)
        elif name == "w_kr":
            self.ready[name] = jnp.pad(land, ((0, 128 - QK_ROPE), (0, 0)))
        elif name == "w_uq":
            self.ready[name] = jnp.pad(land.reshape(N_HEADS, QK_NOPE + QK_ROPE, Q_LORA),
                                       ((0, 0), (0, QK_PAD - QK_NOPE - QK_ROPE), (0, 0)))
        else:
            self.ready[name] = land.reshape(D, land.shape[-1])

    def need(self, name, after):
        if name not in self.ready:
            self._gather_to(self.group_of[name], 3, after)
            self._flush()
        return self.ready[name]

    def grad(self, name, layer, array):
        self.grads[(name, layer)] = array

    def _scatter_to(self, gname, stage, after):
        keys = dict(SCATTER_GROUPS)[gname]
        st = self.scatters.setdefault(gname, dict(stage=0))
        kinds = [KIND[nm] for nm, _ in keys]
        if st["stage"] < 1 <= stage:
            grads = [self.grads[key] for key in keys]
            lands = []
            for gr, kind in zip(grads, kinds):
                shard = (gr.shape[0],) + gr.shape[2:] if kind == "blocked" else (gr.shape[0], gr.shape[1] // N_DEV)
                lands.append(lax.empty((N_CHIP,) + shard, BF16))
            plan, ncopy = _plan_scatter_sibling(kinds)
            self._start(f"rs_{gname}_sibling", grads, lands, ncopy, plan, st)
            st["stage"] = 1
        if st["stage"] < 2 <= stage:
            plan, ncopy = _plan_scatter_sibling(kinds)
            grads, recvs = _copies_wait(f"rs_{gname}_sibling_wait", self._flight(st), ncopy, plan)
            sums = _chip_sums(f"rs_{gname}_sums", grads, kinds, recvs, self.c_arr)
            lands = [lax.empty(s.shape, BF16) for s in sums]
            plan, ncopy = _plan_scatter_chips(len(sums))
            self._start(f"rs_{gname}_chips", sums, lands, ncopy, plan, st)
            st["stage"] = 2
        if st["stage"] < 3 <= stage:
            plan, ncopy = _plan_scatter_chips(len(keys))
            sums, recvs = _copies_wait(f"rs_{gname}_chips_wait", self._flight(st), ncopy, plan)
            items = []
            for (nm, layer), own, rv in zip(keys, sums, recvs):
                nl = 1 if layer is None else 2
                rows, w = own.shape[1], own.shape[2]
                w3, m3, v3 = (_stored(nm, src[nm]).reshape(nl, rows, w) for src in (self.wts, self.mom, self.var))
                items.append((own, rv, w3, m3, v3, 0 if layer is None else layer, self.results.get(nm)))
            outs = _adamw_group(f"adamw_{gname}", items, self.chip_ids)
            for (nm, _), out in zip(keys, outs):
                self.results[nm] = out
            st["stage"] = 3

    def at(self, place, after):
        for action, gname in SCHEDULE.get(place, ()):
            self._advance(action, gname, after)
        self._flush()

    def _advance(self, action, gname, after):
        if action.startswith("gather"):
            self._gather_to(gname, STAGES[action], after)
        else:
            self._scatter_to(gname, STAGES[action], after)

    def finish(self, after):
        for action, gname in FINISH:
            self._advance(action, gname, after)
        for gname, _ in SCATTER_GROUPS:
            self._scatter_to(gname, 3, after)
        return {nm: [_stored(nm, o.reshape(_stored(nm, self.wts[nm]).shape)) for o in outs]
                for nm, outs in self.results.items()}


REPLICATED = ("attn_norm", "ffn_norm", "final_norm", "kv_in_norm", "kv_latent_norm", "q_latent_norm", "ffn_conv_b")
WEIGHTS = ("attn_norm", "ffn_norm", "final_norm", "sc_w_in", "sc_conv_w", "sc_w_out", "kv_in_norm", "w_dkv",
           "kv_latent_norm", "w_kr", "w_uk", "w_uv", "w_dq", "q_latent_norm", "w_uq", "w_o", "ffn_w_up", "ffn_conv_w",
           "ffn_conv_b", "ffn_w_down")


def kernel(x, positions, attn_norm, ffn_norm, final_norm, sc_w_in, sc_conv_w, sc_w_out, kv_in_norm, w_dkv, kv_latent_norm, w_kr, w_uk, w_uv, w_dq, q_latent_norm, w_uq, w_o, ffn_w_up, ffn_conv_w, ffn_conv_b, ffn_w_down, loss_target, m_attn_norm, m_ffn_norm, m_final_norm, m_sc_w_in, m_sc_conv_w, m_sc_w_out, m_kv_in_norm, m_w_dkv, m_kv_latent_norm, m_w_kr, m_w_uk, m_w_uv, m_w_dq, m_q_latent_norm, m_w_uq, m_w_o, m_ffn_w_up, m_ffn_conv_w, m_ffn_conv_b, m_ffn_w_down, v_attn_norm, v_ffn_norm, v_final_norm, v_sc_w_in, v_sc_conv_w, v_sc_w_out, v_kv_in_norm, v_w_dkv, v_kv_latent_norm, v_w_kr, v_w_uk, v_w_uv, v_w_dq, v_q_latent_norm, v_w_uq, v_w_o, v_ffn_w_up, v_ffn_conv_w, v_ffn_conv_b, v_ffn_w_down):
    wts = dict(attn_norm=attn_norm, ffn_norm=ffn_norm, final_norm=final_norm, sc_w_in=sc_w_in, sc_conv_w=sc_conv_w,
               sc_w_out=sc_w_out, kv_in_norm=kv_in_norm, w_dkv=w_dkv, kv_latent_norm=kv_latent_norm, w_kr=w_kr,
               w_uk=w_uk, w_uv=w_uv, w_dq=w_dq, q_latent_norm=q_latent_norm, w_uq=w_uq, w_o=w_o, ffn_w_up=ffn_w_up,
               ffn_conv_w=ffn_conv_w, ffn_conv_b=ffn_conv_b, ffn_w_down=ffn_w_down)
    mom = dict(attn_norm=m_attn_norm, ffn_norm=m_ffn_norm, final_norm=m_final_norm, sc_w_in=m_sc_w_in,
               sc_conv_w=m_sc_conv_w, sc_w_out=m_sc_w_out, kv_in_norm=m_kv_in_norm, w_dkv=m_w_dkv,
               kv_latent_norm=m_kv_latent_norm, w_kr=m_w_kr, w_uk=m_w_uk, w_uv=m_w_uv, w_dq=m_w_dq,
               q_latent_norm=m_q_latent_norm, w_uq=m_w_uq, w_o=m_w_o, ffn_w_up=m_ffn_w_up, ffn_conv_w=m_ffn_conv_w,
               ffn_conv_b=m_ffn_conv_b, ffn_w_down=m_ffn_w_down)
    var = dict(attn_norm=v_attn_norm, ffn_norm=v_ffn_norm, final_norm=v_final_norm, sc_w_in=v_sc_w_in,
               sc_conv_w=v_sc_conv_w, sc_w_out=v_sc_w_out, kv_in_norm=v_kv_in_norm, w_dkv=v_w_dkv,
               kv_latent_norm=v_kv_latent_norm, w_kr=v_w_kr, w_uk=v_w_uk, w_uv=v_w_uv, w_dq=v_w_dq,
               q_latent_norm=v_q_latent_norm, w_uq=v_w_uq, w_o=v_w_o, ffn_w_up=v_ffn_w_up, ffn_conv_w=v_ffn_conv_w,
               ffn_conv_b=v_ffn_conv_b, ffn_w_down=v_ffn_w_down)
    xi, yi, ci = _place()
    me = 4 * xi + 2 * yi + ci
    _Chain.last = None

    ex = _Exchange(wts, mom, var, ffn_conv_b)
    rep = {
        "attn_norm": attn_norm, "ffn_norm": ffn_norm, "final_norm": final_norm,
        "kv_in_norm": kv_in_norm.reshape(1, D), "kv_latent_norm": kv_latent_norm.reshape(1, KV_LORA),
        "q_latent_norm": q_latent_norm.reshape(1, Q_LORA),
    }
    loss, grad_x, small = _local_step(x.reshape(T, D), positions.reshape(T, 1), loss_target.reshape(T, D), rep, ex)

    def rows_of(a):
        return a.reshape(-1, a.shape[-1])

    small_order = list(REPLICATED) + ["sc_conv_w", "ffn_conv_w"]
    shards = [loss.reshape(1, 1, 128)] + [rows_of(small[nm])[None] for nm in small_order]
    plan, ncopy = _plan_gather_all(len(shards))
    flight, = _copies_start("ag_small", [(shards, [lax.empty((1, N_DEV) + s.shape[1:], F32) for s in shards], ncopy, plan)])
    results = ex.finish(grad_x)
    _, gathered = _copies_wait("ag_small_wait", flight, ncopy, plan)
    params = [[None] + [rows_of(src[nm]) for nm in REPLICATED] + [None, None] for src in (wts, mom, var)]
    summed = _adamw_small(gathered, *params)
    loss_total = summed[0][0][0, 0]
    for nm, vals in zip(REPLICATED, summed[1:1 + len(REPLICATED)]):
        results[nm] = [a.reshape(wts[nm].shape) for a in vals]
    g_scw = lax.dynamic_slice(summed[-2][0], (0, me * 128), (3, 128))
    g_fcw = lax.dynamic_slice(summed[-1][0], (0, me * 352), (6, 352))
    conv = _adamw_plain("adamw_conv", [g_scw, g_fcw], *[[rows_of(src["sc_conv_w"]), rows_of(src["ffn_conv_w"])]
                                                        for src in (wts, mom, var)])
    for nm, g_own, vals in zip(("sc_conv_w", "ffn_conv_w"), (g_scw, g_fcw), conv):
        results[nm] = [a.reshape(wts[nm].shape) for a in [g_own] + vals]

    outs = [loss_total, grad_x.reshape(1, T, D)]
    for slot in range(4):
        outs.extend(results[nm][slot] for nm in WEIGHTS)
    return tuple(outs)
```

```python
import jax
import jax.numpy as jnp
from jax import lax
from jax.experimental import pallas as pl
from jax.experimental.pallas import tpu as pltpu

F32 = jnp.float32
BF16 = jnp.bfloat16

T = 2048
D = 1024
N_HEADS = 8
QK_NOPE = 128
QK_ROPE = 64
V_HEAD = 128
Q_LORA = 384
KV_LORA = 256
D_FF = 2816
CHUNK = 64
ROPE_THETA = 10000.0
EPS = 1e-6
NEG_INF = -1e30
ADAM_LR = 0.001
ADAM_B1 = 0.9
ADAM_B2 = 0.999
ADAM_EPS = 1e-08
ADAM_WD = 0.01
ADAM_STEP = 10

N_DEV = 8
N_CHIP = 4
FF_BLK = D_FF * 2 // N_DEV
N_FF_BLK = D_FF // FF_BLK
QK_PAD = 256
HALO = 16

TM = 1024
TS = 512
TR = 256
TQ = 512
VMEM_LIMIT = 56 * 1024 * 1024

NN = (((1,), (0,)), ((), ()))
NT = (((1,), (1,)), ((), ()))
TN = (((0,), (0,)), ((), ()))
MESH = pl.DeviceIdType.MESH


def _params(sem):
    return pltpu.CompilerParams(dimension_semantics=sem, vmem_limit_bytes=VMEM_LIMIT)


ANY_SPEC = pl.BlockSpec(memory_space=pl.ANY)
VMEM_SPEC = pl.BlockSpec(memory_space=pltpu.VMEM)


class _Chain:
    last = None


def _pallas(body, *, name, in_specs, out_specs, out_shape, grid=(), scratch_shapes=(), n_prefetch=0, aliases=None,
            params=None):
    def run(*args):
        after = _Chain.last
        n_lead = len(args)
        specs, operands, fn = list(in_specs), list(args), body
        if after is not None:
            def fn(*refs):
                return body(*refs[:n_lead], *refs[n_lead + 1:])
            specs.append(ANY_SPEC)
            operands.append(after)
        kw = dict(name=name, out_shape=out_shape, input_output_aliases=aliases or {})
        if params is not None:
            kw["compiler_params"] = params
        if n_prefetch:
            kw["grid_spec"] = pltpu.PrefetchScalarGridSpec(
                num_scalar_prefetch=n_prefetch, grid=grid, in_specs=specs, out_specs=out_specs,
                scratch_shapes=scratch_shapes)
        else:
            kw.update(grid=grid, in_specs=specs, out_specs=out_specs, scratch_shapes=scratch_shapes)
        outs = pl.pallas_call(fn, **kw)(*operands)
        _Chain.last = outs[0] if isinstance(outs, (list, tuple)) else outs
        return outs
    return run


def _mm(name, a, b, *, grid, a_spec, b_spec, o_spec, o_shape, o_dtype, dims, k_axis=None, acc_shape=None,
        add=None, add_spec=None):
    nk = grid[k_axis] if k_axis is not None else 1
    has_add = add is not None

    def body(*refs):
        a_ref, b_ref = refs[0], refs[1]
        p = 2
        add_ref = None
        if has_add:
            add_ref = refs[p]
            p += 1
        o_ref = refs[p]
        p += 1
        r = lax.dot_general(a_ref[...].astype(BF16), b_ref[...].astype(BF16), dims, preferred_element_type=F32)
        if k_axis is None:
            if has_add:
                r = r + add_ref[...].astype(F32)
            o_ref[...] = r.astype(o_dtype)
        else:
            acc = refs[p]
            k = pl.program_id(k_axis)

            @pl.when(k == 0)
            def _():
                acc[...] = r

            @pl.when(k > 0)
            def _():
                acc[...] += r

            @pl.when(k == nk - 1)
            def _():
                t = acc[...]
                if has_add:
                    t = t + add_ref[...].astype(F32)
                o_ref[...] = t.astype(o_dtype)

    in_specs = [a_spec, b_spec]
    args = [a, b]
    if has_add:
        in_specs.append(add_spec if add_spec is not None else o_spec)
        args.append(add)
    sem = tuple("arbitrary" if ax == k_axis else "parallel" for ax in range(len(grid)))
    scratch = [pltpu.VMEM(acc_shape, F32)] if k_axis is not None else []
    return _pallas(body, name=name, grid=grid, in_specs=in_specs, out_specs=o_spec,
                   out_shape=jax.ShapeDtypeStruct(o_shape, o_dtype), scratch_shapes=scratch, params=_params(sem))(*args)


def _mm_sum(name, parts, *, grid, o_spec, o_shape, o_dtype, add=None, norm_bwd=None):
    has_add = add is not None
    np_ = len(parts)
    nn = 1 if norm_bwd is None else len(norm_bwd[1])
    has_res = norm_bwd is not None and norm_bwd[2] is not None

    def body(*refs):
        accs = [None] * nn
        for p, (_, _, _, _, dims, n) in enumerate(parts):
            a_ref, b_ref = refs[2 * p], refs[2 * p + 1]
            for k in range(a_ref.shape[0]):
                r = lax.dot_general(a_ref[k], b_ref[k], dims, preferred_element_type=F32)
                accs[n] = r if accs[n] is None else accs[n] + r
        if norm_bwd is None:
            acc = accs[0]
            if has_add:
                acc = acc + refs[2 * np_][...]
            refs[-1][...] = acc.astype(o_dtype)
            return
        x_ref, g_refs = refs[2 * np_], refs[2 * np_ + 1:2 * np_ + 1 + nn]
        dx_ref, dxb_ref, dg_refs = refs[-2 - nn], refs[-1 - nn], refs[-nn:]
        xv = x_ref[...]
        r = lax.rsqrt(jnp.mean(xv * xv, axis=-1, keepdims=True) + EPS)
        xn = xv * r
        dx = refs[2 * np_ + 1 + nn][...] if has_res else None
        sums = []
        for acc, g_ref in zip(accs, g_refs):
            gdy = acc * g_ref[...]
            t = r * (gdy - xn * jnp.mean(gdy * xn, axis=-1, keepdims=True))
            dx = t if dx is None else dx + t
            sums.append(jnp.sum(acc * xn, axis=0, keepdims=True))
        dx_ref[...] = dx
        dxb_ref[...] = dx.astype(BF16)

        @pl.when(pl.program_id(0) == 0)
        def _():
            for dg_ref, part in zip(dg_refs, sums):
                dg_ref[...] = part

        @pl.when(pl.program_id(0) > 0)
        def _():
            for dg_ref, part in zip(dg_refs, sums):
                dg_ref[...] += part

    in_specs, args = [], []
    for a, a_spec, b, b_spec, _, _ in parts:
        in_specs += [a_spec, b_spec]
        args += [a, b]
    if norm_bwd is None:
        if has_add:
            in_specs.append(o_spec)
            args.append(add)
        return _pallas(body, name=name, grid=grid, in_specs=in_specs, out_specs=o_spec,
                       out_shape=jax.ShapeDtypeStruct(o_shape, o_dtype),
                       params=_params(("parallel",) * len(grid)))(*args)
    x, gains, dres = norm_bwd
    vec = pl.BlockSpec((1, o_shape[1]), lambda i: (0, 0))
    in_specs += [o_spec] + [vec] * nn + ([o_spec] if has_res else [])
    args += [x] + list(gains) + ([dres] if has_res else [])
    outs = _pallas(body, name=name, grid=grid, in_specs=in_specs, out_specs=[o_spec, o_spec] + [vec] * nn,
                   out_shape=[jax.ShapeDtypeStruct(o_shape, F32), jax.ShapeDtypeStruct(o_shape, BF16)]
                   + [jax.ShapeDtypeStruct((1, o_shape[1]), F32)] * nn,
                   params=_params(("arbitrary",)))(*args)
    return outs[0], outs[1], list(outs[2:])


def _mm_rows(name, a, b, dims, o_dtype, n_out, *, tn=None, add=None):
    k = a.shape[1]
    tn = n_out if tn is None else tn
    if dims == NN:
        b_spec = pl.BlockSpec((k, tn), lambda n, i: (0, n))
    else:
        b_spec = pl.BlockSpec((tn, k), lambda n, i: (n, 0))
    return _mm(name, a, b, grid=(n_out // tn, T // TM),
               a_spec=pl.BlockSpec((TM, k), lambda n, i: (i, 0)), b_spec=b_spec,
               o_spec=pl.BlockSpec((TM, tn), lambda n, i: (i, n)), o_shape=(T, n_out), o_dtype=o_dtype,
               dims=dims, add=add)


def _mm_wgrad(name, a, b, *, tn=512):
    k, n = a.shape[1], b.shape[1]
    tn = min(tn, n)
    return _mm(name, a, b, grid=(n // tn,),
               a_spec=pl.BlockSpec((T, k), lambda j: (0, 0)), b_spec=pl.BlockSpec((T, tn), lambda j: (0, j)),
               o_spec=pl.BlockSpec((k, tn), lambda j: (0, j)), o_shape=(k, n), o_dtype=BF16, dims=TN)


def _rms_fwd(name, x, g):
    d = x.shape[1]

    def body(x_ref, g_ref, o_ref):
        xv = x_ref[...]
        r = lax.rsqrt(jnp.mean(xv * xv, axis=-1, keepdims=True) + EPS)
        o_ref[...] = ((xv * r) * g_ref[...]).astype(BF16)

    return _pallas(
        body, name=name, grid=(T // TM,),
        in_specs=[pl.BlockSpec((TM, d), lambda i: (i, 0)), pl.BlockSpec((1, d), lambda i: (0, 0))],
        out_specs=pl.BlockSpec((TM, d), lambda i: (i, 0)),
        out_shape=jax.ShapeDtypeStruct((T, d), BF16), params=_params(("parallel",)))(x, g)


def _rows_call(name, body, row_ins, whole_ins, outs):
    in_specs = [pl.BlockSpec((TM, a.shape[1]), lambda i: (i, 0)) for a in row_ins]
    in_specs += [pl.BlockSpec(a.shape, lambda i: (0, 0)) for a in whole_ins]
    return _pallas(
        body, name=name, grid=(T // TM,), in_specs=in_specs,
        out_specs=[pl.BlockSpec((TM, d), lambda i: (i, 0)) for d, _ in outs],
        out_shape=[jax.ShapeDtypeStruct((T, d), dt) for d, dt in outs],
        params=_params(("parallel",)))(*row_ins, *whole_ins)


def _rms(xv, g):
    return (xv * lax.rsqrt(jnp.mean(xv * xv, axis=-1, keepdims=True) + EPS)) * g


def _rms_fwd2(name, x, g1, g2):
    d = x.shape[1]

    def body(x_ref, g1_ref, g2_ref, o1_ref, o2_ref):
        xv = x_ref[...]
        xn = xv * lax.rsqrt(jnp.mean(xv * xv, axis=-1, keepdims=True) + EPS)
        o1_ref[...] = (xn * g1_ref[...]).astype(BF16)
        o2_ref[...] = (xn * g2_ref[...]).astype(BF16)

    return _rows_call(name, body, [x], [g1, g2], [(d, BF16), (d, BF16)])


def _down_norm(name, a, w, g):
    n = w.shape[1]

    def body(a_ref, w_ref, g_ref, raw_ref, o_ref):
        raw = lax.dot_general(a_ref[...], w_ref[...], NN, preferred_element_type=F32)
        raw_ref[...] = raw
        o_ref[...] = _rms(raw, g_ref[...]).astype(BF16)

    return _rows_call(name, body, [a], [w, g], [(n, F32), (n, BF16)])


def _kv_down(hk, w_dkv, w_kr, g, tables):
    def body(a_ref, c_ref, sa_ref, sb_ref, wd_ref, wr_ref, g_ref, raw_ref, ckv_ref, kr_ref):
        av = a_ref[...]
        raw = lax.dot_general(av, wd_ref[...], NN, preferred_element_type=F32)
        raw_ref[...] = raw
        ckv_ref[...] = _rms(raw, g_ref[...]).astype(BF16)
        kr = lax.dot_general(av, wr_ref[...], NT, preferred_element_type=F32)
        kr_ref[...] = _rotate(kr, c_ref[...], sa_ref[...], sb_ref[...], 1.0).astype(BF16)

    return _rows_call("kv_down", body, [hk, *tables], [w_dkv, w_kr, g], [(KV_LORA, F32), (KV_LORA, BF16), (128, BF16)])


def _kv_up(ckv, w_uk, w_uv):
    def body(a_ref, wk_ref, wv_ref, k_ref, v_ref):
        av = a_ref[...]
        k_ref[...] = lax.dot_general(av, wk_ref[...], NN, preferred_element_type=F32).astype(BF16)
        v_ref[...] = lax.dot_general(av, wv_ref[...], NN, preferred_element_type=F32).astype(BF16)

    return _rows_call("kv_up", body, [ckv], [w_uk, w_uv], [(N_HEADS * QK_NOPE, BF16), (N_HEADS * V_HEAD, BF16)])


def _rms_bwd(name, x, gains, dys, dres=None):
    d = x.shape[1]
    n = len(gains)
    has_res = dres is not None

    def body(*refs):
        x_ref, g_refs, dy_refs = refs[0], refs[1:1 + n], refs[1 + n:1 + 2 * n]
        dx_ref, dxb_ref = refs[-2 - n], refs[-1 - n]
        dg_refs = refs[-n:]
        xv = x_ref[...]
        r = lax.rsqrt(jnp.mean(xv * xv, axis=-1, keepdims=True) + EPS)
        xn = xv * r
        dx = refs[1 + 2 * n][...] if has_res else None
        parts = []
        for g_ref, dy_ref in zip(g_refs, dy_refs):
            dyv = dy_ref[...].astype(F32)
            gdy = dyv * g_ref[...]
            t = r * (gdy - xn * jnp.mean(gdy * xn, axis=-1, keepdims=True))
            dx = t if dx is None else dx + t
            parts.append(jnp.sum(dyv * xn, axis=0, keepdims=True))
        dx_ref[...] = dx
        dxb_ref[...] = dx.astype(BF16)

        @pl.when(pl.program_id(0) == 0)
        def _():
            for dg_ref, part in zip(dg_refs, parts):
                dg_ref[...] = part

        @pl.when(pl.program_id(0) > 0)
        def _():
            for dg_ref, part in zip(dg_refs, parts):
                dg_ref[...] += part

    row = pl.BlockSpec((TR, d), lambda i: (i, 0))
    vec = pl.BlockSpec((1, d), lambda i: (0, 0))
    args = [x] + list(gains) + list(dys) + ([dres] if has_res else [])
    in_specs = [row] + [vec] * n + [row] * n + ([row] if has_res else [])
    outs = _pallas(
        body, name=name, grid=(T // TR,), in_specs=in_specs, out_specs=[row, row] + [vec] * n,
        out_shape=[jax.ShapeDtypeStruct((T, d), F32), jax.ShapeDtypeStruct((T, d), BF16)]
        + [jax.ShapeDtypeStruct((1, d), F32)] * n,
        params=_params(("arbitrary",)))(*args)
    return outs[0], outs[1], list(outs[2:])


def _final(h, g, tgt):
    def body(h_ref, g_ref, t_ref, loss_ref, dh_ref, dhb_ref, dg_ref):
        hv = h_ref[...]
        r = lax.rsqrt(jnp.mean(hv * hv, axis=-1, keepdims=True) + EPS)
        xn = hv * r
        gv = g_ref[...]
        err = xn * gv - t_ref[...]
        part_loss = 0.5 * jnp.sum(jnp.mean(err * err, axis=-1, keepdims=True), axis=0, keepdims=True)
        dy = err * (1.0 / D)
        gdy = dy * gv
        dh = r * (gdy - xn * jnp.mean(gdy * xn, axis=-1, keepdims=True))
        dh_ref[...] = dh
        dhb_ref[...] = dh.astype(BF16)
        part = jnp.sum(dy * xn, axis=0, keepdims=True)
        first = pl.program_id(0) == 0

        @pl.when(first)
        def _():
            dg_ref[...] = part
            loss_ref[...] = jnp.broadcast_to(part_loss, (1, 128))

        @pl.when(jnp.logical_not(first))
        def _():
            dg_ref[...] += part
            loss_ref[...] += jnp.broadcast_to(part_loss, (1, 128))

    row = pl.BlockSpec((TR, D), lambda i: (i, 0))
    vec = pl.BlockSpec((1, D), lambda i: (0, 0))
    return _pallas(
        body, name="final_loss", grid=(T // TR,), in_specs=[row, vec, row],
        out_specs=[pl.BlockSpec((1, 128), lambda i: (0, 0)), row, row, vec],
        out_shape=[jax.ShapeDtypeStruct((1, 128), F32), jax.ShapeDtypeStruct((T, D), F32),
                   jax.ShapeDtypeStruct((T, D), BF16), jax.ShapeDtypeStruct((1, D), F32)],
        params=_params(("arbitrary",)))(h, g, tgt)


def _prev_idx(i, rows=TR):
    return jnp.maximum(i * (rows // HALO) - 1, 0)


def _next_idx(i, rows=TR):
    return jnp.minimum((i + 1) * (rows // HALO), T // HALO - 1)


def _causal_taps(ext):
    return pltpu.roll(ext, 2, 0)[HALO:], pltpu.roll(ext, 1, 0)[HALO:], ext[HALO:]


def _anticausal_taps(ext, n):
    rows = ext.shape[0]
    return pltpu.roll(ext, rows - 1, 0)[:n], pltpu.roll(ext, rows - 2, 0)[:n]


MIX_COLS = 512


def _mixer_in(hn, w_in, w):
    nc = D // MIX_COLS

    def body(h_ref, hh_ref, wb_ref, wc_ref, wu_ref, w_ref, b_ref, c_ref, u_ref, y_ref):
        i = pl.program_id(1)
        hv = h_ref[...]
        he = jnp.concatenate([hh_ref[...], hv], axis=0)
        ce = lax.dot_general(he, wc_ref[...], NN, preferred_element_type=F32).astype(BF16)
        ue = lax.dot_general(he, wu_ref[...], NN, preferred_element_type=F32).astype(BF16)
        bv = lax.dot_general(hv, wb_ref[...], NN, preferred_element_type=F32).astype(BF16)
        b_ref[...] = bv
        c_ref[...] = ce[HALO:]
        u_ref[...] = ue[HALO:]
        row = lax.broadcasted_iota(jnp.int32, (HALO + TS, 1), 0)
        cu = jnp.where(jnp.logical_or(i > 0, row >= HALO), ce.astype(F32) * ue.astype(F32), 0.0)
        x2, x1, x0 = _causal_taps(cu)
        wv = w_ref[...]
        cv = (x2 * wv[0:1] + x1 * wv[1:2]) + x0 * wv[2:3]
        y_ref[...] = (bv.astype(F32) * cv).astype(BF16)

    def cols(part):
        return pl.BlockSpec((D, MIX_COLS), lambda j, i: (0, part * nc + j))

    blk = pl.BlockSpec((TS, MIX_COLS), lambda j, i: (i, j))
    out = jax.ShapeDtypeStruct((T, D), BF16)
    return _pallas(
        body, name="l0_in", grid=(nc, T // TS),
        in_specs=[pl.BlockSpec((TS, D), lambda j, i: (i, 0)), pl.BlockSpec((HALO, D), lambda j, i: (_prev_idx(i, TS), 0)),
                  cols(0), cols(1), cols(2), pl.BlockSpec((3, MIX_COLS), lambda j, i: (0, j))],
        out_specs=[blk] * 4, out_shape=[out] * 4,
        params=_params(("parallel", "parallel")))(hn, hn, w_in, w_in, w_in, w)


def _mixer_out_bwd(dh, w_out, zb, zc, zu, w):
    last = T // TR - 1

    def body(dh_ref, dhn_ref, wo_ref, b_ref, bn_ref, c_ref, ch_ref, u_ref, uh_ref, w_ref, dz_ref, dw_ref):
        i = pl.program_id(0)
        dye = lax.dot_general(jnp.concatenate([dh_ref[...], dhn_ref[...]], axis=0), wo_ref[...], NT,
                              preferred_element_type=F32)
        cv_ = c_ref[...].astype(F32)
        uv = u_ref[...].astype(F32)
        cu = cv_ * uv
        cuh = jnp.where(i > 0, ch_ref[...].astype(F32) * uh_ref[...].astype(F32), 0.0)
        x2, x1, x0 = _causal_taps(jnp.concatenate([cuh, cu], axis=0))
        wv = w_ref[...]
        conv = (x2 * wv[0:1] + x1 * wv[1:2]) + x0 * wv[2:3]
        dyv = dye[:TR]
        dz_ref[:, 0:D] = (dyv * conv).astype(BF16)
        dconv = dyv * b_ref[...].astype(F32)
        dconv_n = jnp.where(i < last, dye[TR:] * bn_ref[...].astype(F32), 0.0)
        n1, n2 = _anticausal_taps(jnp.concatenate([dconv, dconv_n], axis=0), TR)
        dcu = (dconv * wv[2:3] + n1 * wv[1:2]) + n2 * wv[0:1]
        dz_ref[:, D:2 * D] = (dcu * uv).astype(BF16)
        dz_ref[:, 2 * D:3 * D] = (dcu * cv_).astype(BF16)
        part = jnp.concatenate([jnp.sum(dconv * x2, axis=0, keepdims=True),
                                jnp.sum(dconv * x1, axis=0, keepdims=True),
                                jnp.sum(dconv * x0, axis=0, keepdims=True)], axis=0)

        @pl.when(i == 0)
        def _():
            dw_ref[...] = part

        @pl.when(i > 0)
        def _():
            dw_ref[...] += part

    main = pl.BlockSpec((TR, D), lambda i: (i, 0))
    prev = pl.BlockSpec((HALO, D), lambda i: (_prev_idx(i), 0))
    nxt = pl.BlockSpec((HALO, D), lambda i: (_next_idx(i), 0))
    wspec = pl.BlockSpec((3, D), lambda i: (0, 0))
    return _pallas(
        body, name="d_l0_out", grid=(T // TR,),
        in_specs=[main, nxt, pl.BlockSpec((D, D), lambda i: (0, 0)), main, nxt, main, prev, main, prev, wspec],
        out_specs=[pl.BlockSpec((TR, 3 * D), lambda i: (i, 0)), wspec],
        out_shape=[jax.ShapeDtypeStruct((T, 3 * D), BF16), jax.ShapeDtypeStruct((3, D), F32)],
        params=_params(("arbitrary",)))(dh, dh, w_out, zb, zb, zc, zc, zu, zu, w)


def _sigmoid(x):
    return 0.5 * jnp.tanh(0.5 * x) + 0.5


def _ffn_up_act(name, hf, w_up, w, b):
    def body(h_ref, hh_ref, wg_ref, wv_ref, w_ref, b_ref, g_ref, v_ref, a_ref):
        i = pl.program_id(1)
        hv = h_ref[...]
        ge = lax.dot_general(jnp.concatenate([hh_ref[...], hv], axis=0), wg_ref[...], NT,
                             preferred_element_type=F32).astype(BF16)
        v = lax.dot_general(hv, wv_ref[...], NT, preferred_element_type=F32).astype(BF16)
        g_ref[...] = ge[HALO:]
        v_ref[...] = v
        ext = ge.astype(F32)
        row = lax.broadcasted_iota(jnp.int32, (HALO + TM, 1), 0)
        ext = jnp.where(jnp.logical_or(i > 0, row >= HALO), ext, 0.0)
        x2, x1, x0 = _causal_taps(ext)
        wv = w_ref[...]
        gc = ((x2 * wv[0:1] + x1 * wv[1:2]) + x0 * wv[2:3]) + b_ref[...]
        a_ref[...] = ((gc * _sigmoid(gc)) * v.astype(F32)).astype(BF16)

    blk = pl.BlockSpec((None, TM, FF_BLK), lambda j, i: (j, i, 0))
    out = jax.ShapeDtypeStruct((N_FF_BLK, T, FF_BLK), BF16)
    return _pallas(
        body, name=name, grid=(N_FF_BLK, T // TM),
        in_specs=[pl.BlockSpec((TM, D), lambda j, i: (i, 0)),
                  pl.BlockSpec((HALO, D), lambda j, i: (_prev_idx(i, TM), 0)),
                  pl.BlockSpec((None, None, FF_BLK, D), lambda j, i: (0, j, 0, 0)),
                  pl.BlockSpec((None, None, FF_BLK, D), lambda j, i: (0, j + N_FF_BLK, 0, 0)),
                  pl.BlockSpec((None, 3, FF_BLK), lambda j, i: (j, 0, 0)),
                  pl.BlockSpec((None, 1, FF_BLK), lambda j, i: (j, 0, 0))],
        out_specs=[blk, blk, blk], out_shape=[out, out, out],
        params=_params(("parallel", "parallel")))(hf, hf, w_up, w_up, w, b)


def _ffn_dact(name, dh, w_down4, g, v, w, b):
    last = T // TS - 1

    def body(dh_ref, dhn_ref, wd_ref, g_ref, gp_ref, gn_ref, v_ref, vn_ref, w_ref, b_ref, dg_ref, dv_ref, dw_ref, db_ref):
        i = pl.program_id(1)
        da = lax.dot_general(jnp.concatenate([dh_ref[...], dhn_ref[...]], axis=0), wd_ref[...], NT,
                             preferred_element_type=F32)
        row = lax.broadcasted_iota(jnp.int32, (TS + HALO, 1), 0)
        da = jnp.where(jnp.logical_or(i < last, row < TS), da, 0.0)
        gp = jnp.where(i > 0, gp_ref[...].astype(F32), 0.0)
        ext = jnp.concatenate([gp, g_ref[...].astype(F32), gn_ref[...].astype(F32)], axis=0)
        x2, x1, x0 = _causal_taps(ext)
        wv = w_ref[...]
        gc = ((x2 * wv[0:1] + x1 * wv[1:2]) + x0 * wv[2:3]) + b_ref[...]
        sg = _sigmoid(gc)
        vv = jnp.concatenate([v_ref[...].astype(F32), vn_ref[...].astype(F32)], axis=0)
        dv_ref[...] = (da[:TS] * (gc[:TS] * sg[:TS])).astype(BF16)
        dgc = (da * vv) * (sg * (1.0 + gc * (1.0 - sg)))
        n1, n2 = _anticausal_taps(dgc, TS)
        d0 = dgc[:TS]
        dg_ref[...] = ((d0 * wv[2:3] + n1 * wv[1:2]) + n2 * wv[0:1]).astype(BF16)
        part_w = jnp.concatenate([jnp.sum(d0 * x2[:TS], axis=0, keepdims=True),
                                  jnp.sum(d0 * x1[:TS], axis=0, keepdims=True),
                                  jnp.sum(d0 * x0[:TS], axis=0, keepdims=True)], axis=0)
        part_b = jnp.sum(d0, axis=0, keepdims=True)

        @pl.when(i == 0)
        def _():
            dw_ref[...] = part_w
            db_ref[...] = part_b

        @pl.when(i > 0)
        def _():
            dw_ref[...] += part_w
            db_ref[...] += part_b

    blk = pl.BlockSpec((None, TS, FF_BLK), lambda j, i: (j, i, 0))
    prev = pl.BlockSpec((None, HALO, FF_BLK), lambda j, i: (j, _prev_idx(i, TS), 0))
    nxt = pl.BlockSpec((None, HALO, FF_BLK), lambda j, i: (j, _next_idx(i, TS), 0))
    wspec = pl.BlockSpec((None, 3, FF_BLK), lambda j, i: (j, 0, 0))
    bspec = pl.BlockSpec((None, 1, FF_BLK), lambda j, i: (j, 0, 0))
    return _pallas(
        body, name=name, grid=(N_FF_BLK, T // TS),
        in_specs=[pl.BlockSpec((TS, D), lambda j, i: (i, 0)),
                  pl.BlockSpec((HALO, D), lambda j, i: (_next_idx(i, TS), 0)),
                  pl.BlockSpec((None, None, FF_BLK, D), lambda j, i: (0, j, 0, 0)),
                  blk, prev, nxt, blk, nxt, wspec, bspec],
        out_specs=[blk, blk, wspec, bspec],
        out_shape=[jax.ShapeDtypeStruct((N_FF_BLK, T, FF_BLK), BF16), jax.ShapeDtypeStruct((N_FF_BLK, T, FF_BLK), BF16),
                   jax.ShapeDtypeStruct((N_FF_BLK, 3, FF_BLK), F32), jax.ShapeDtypeStruct((N_FF_BLK, 1, FF_BLK), F32)],
        params=_params(("parallel", "arbitrary")))(dh, dh, w_down4, g, g, g, v, v, w, b)


def _rope_tables(pos, inv_freq):
    half = QK_ROPE // 2

    def body(p_ref, f_ref, c_ref, sa_ref, sb_ref):
        ang = p_ref[...].astype(F32) * f_ref[...]
        lane = lax.broadcasted_iota(jnp.int32, (T, 128), 1)
        c = jnp.cos(ang)
        s = jnp.sin(ang)
        c_ref[...] = jnp.where(lane < 2 * half, c, 0.0)
        sa_ref[...] = jnp.where(lane < half, -s, 0.0)
        sb_ref[...] = jnp.where(jnp.logical_and(lane >= half, lane < 2 * half), s, 0.0)

    return _pallas(
        body, name="rope_tables", in_specs=[VMEM_SPEC] * 2, out_specs=[VMEM_SPEC] * 3,
        out_shape=[jax.ShapeDtypeStruct((T, 128), F32)] * 3,
        params=pltpu.CompilerParams(vmem_limit_bytes=VMEM_LIMIT))(pos, inv_freq)


def _rotate(r, c, sa, sb, sign):
    return r * c + sign * (pltpu.roll(r, 96, 1) * sa + pltpu.roll(r, 32, 1) * sb)


def _q_up(cq, w_uq, tables):
    cos, sa, sb = tables

    def body(a_ref, b_ref, c_ref, sa_ref, sb_ref, o_ref):
        for h in range(N_HEADS):
            r = lax.dot_general(a_ref[...], b_ref[h], NT, preferred_element_type=F32)
            o_ref[h, :, :QK_NOPE] = r[:, :QK_NOPE].astype(BF16)
            o_ref[h, :, QK_NOPE:] = _rotate(r[:, QK_NOPE:], c_ref[...], sa_ref[...], sb_ref[...], 1.0).astype(BF16)

    tab = pl.BlockSpec((TS, 128), lambda i: (i, 0))
    return _pallas(
        body, name="q_up", grid=(T // TS,),
        in_specs=[pl.BlockSpec((TS, Q_LORA), lambda i: (i, 0)),
                  pl.BlockSpec((N_HEADS, QK_PAD, Q_LORA), lambda i: (0, 0, 0)), tab, tab, tab],
        out_specs=pl.BlockSpec((N_HEADS, TS, QK_PAD), lambda i: (0, i, 0)),
        out_shape=jax.ShapeDtypeStruct((N_HEADS, T, QK_PAD), BF16),
        params=_params(("parallel",)))(cq, w_uq, cos, sa, sb)


def _rope(name, x, tables, sign, out_dtype, reduce_groups=False):
    g, _, w = x.shape
    cos, sa, sb = tables

    def body(x_ref, c_ref, sa_ref, sb_ref, o_ref):
        xv = x_ref[...].astype(F32)
        if reduce_groups:
            acc = xv[0]
            for k in range(1, g):
                acc = acc + xv[k]
            xv = acc
        out = _rotate(xv[:, w - 128:], c_ref[...], sa_ref[...], sb_ref[...], sign)
        if w > 128:
            o_ref[:, :w - 128] = xv[:, :w - 128].astype(out_dtype)
        o_ref[:, w - 128:] = out.astype(out_dtype)

    tab = pl.BlockSpec((TM, 128), lambda h, i: (i, 0))
    if reduce_groups:
        x_spec = pl.BlockSpec((g, TM, w), lambda h, i: (0, i, 0))
        groups = 1
    else:
        x_spec = pl.BlockSpec((None, TM, w), lambda h, i: (h, i, 0))
        groups = g
    return _pallas(
        body, name=name, grid=(groups, T // TM), in_specs=[x_spec, tab, tab, tab],
        out_specs=pl.BlockSpec((None, TM, w), lambda h, i: (h, i, 0)),
        out_shape=jax.ShapeDtypeStruct((groups, T, w), out_dtype),
        params=_params(("parallel", "parallel")))(x, cos, sa, sb)


SCALE = (QK_NOPE + QK_ROPE) ** -0.5
LOG2E = 1.4426950408889634
SCALE2 = SCALE * LOG2E


def _diag_mask(transposed):
    shift = CHUNK.bit_length() - 1
    a = lax.broadcasted_iota(jnp.int32, (TQ, TQ), 0) >> shift
    b = lax.broadcasted_iota(jnp.int32, (TQ, TQ), 1) >> shift
    return (a <= b) if transposed else (b <= a)


def _as_row(col):
    return jnp.transpose(jnp.broadcast_to(col, (col.shape[0], 128)), (1, 0))[0:1]


def _keys(kn_ref, kr_ref, off):
    return jnp.concatenate([kn_ref[pl.ds(off, TQ), :], kr_ref[pl.ds(off, TQ), :]], axis=1)


def _attn_fwd(q, kn, kr, v):
    hp = 2

    def body(q_ref, kn_ref, kr_ref, v_ref, o_ref, lse_ref):
        i = pl.program_id(1)
        qs = [q_ref[a] for a in range(hp)]

        def step(j, carry, masked):
            off = pl.multiple_of(j * TQ, TQ)
            krv = kr_ref[pl.ds(off, TQ), :]
            ss = []
            for a in range(hp):
                kk = jnp.concatenate([kn_ref[pl.ds(off, TQ), a * QK_NOPE:(a + 1) * QK_NOPE], krv], axis=1)
                ss.append(lax.dot_general(qs[a], kk, NT, preferred_element_type=F32))
            out = []
            for a in range(hp):
                m, l, acc = carry[a]
                s = ss[a] * SCALE2
                if masked:
                    s = jnp.where(_diag_mask(False), s, NEG_INF)
                m_new = jnp.maximum(m, jnp.max(s, axis=-1, keepdims=True))
                p = jnp.exp2(s - m_new)
                alpha = jnp.exp2(m - m_new)
                l = alpha * l + jnp.sum(p, axis=-1, keepdims=True)
                pv = lax.dot_general(p.astype(BF16), v_ref[pl.ds(off, TQ), a * V_HEAD:(a + 1) * V_HEAD], NN,
                                     preferred_element_type=F32)
                out.append((m_new, l, alpha * acc + pv))
            return tuple(out)

        one = (jnp.full((TQ, 1), NEG_INF, F32), jnp.zeros((TQ, 1), F32), jnp.zeros((TQ, V_HEAD), F32))
        carry = lax.fori_loop(0, i, lambda j, cr: step(j, cr, False), (one,) * hp)
        carry = step(i, carry, True)
        for a, (m, l, acc) in enumerate(carry):
            o_ref[:, a * V_HEAD:(a + 1) * V_HEAD] = (acc / l).astype(BF16)
            lse_ref[a] = _as_row(m + jnp.log(l) * LOG2E)

    return _pallas(
        body, name="attn_fwd", grid=(N_HEADS // hp, T // TQ),
        in_specs=[pl.BlockSpec((hp, TQ, QK_PAD), lambda h, i: (h, i, 0)),
                  pl.BlockSpec((T, hp * QK_NOPE), lambda h, i: (0, h)),
                  pl.BlockSpec((T, 128), lambda h, i: (0, 0)),
                  pl.BlockSpec((T, hp * V_HEAD), lambda h, i: (0, h))],
        out_specs=[pl.BlockSpec((TQ, hp * V_HEAD), lambda h, i: (i, h)), pl.BlockSpec((hp, 1, TQ), lambda h, i: (h, 0, i))],
        out_shape=[jax.ShapeDtypeStruct((T, N_HEADS * V_HEAD), BF16), jax.ShapeDtypeStruct((N_HEADS, 1, T), F32)],
        params=_params(("parallel", "parallel")))(q, kn, kr, v)


def _attn_bwd(q, kn, kr, v, o, do, lse_row, tables):
    nq = T // TQ
    cos, sa, sb = tables

    def body(q_ref, kn_ref, kr_ref, v_ref, o_ref, do_ref, lse_ref, c_ref, sa_ref, sb_ref,
             dq_ref, dkn_ref, dkr_ref, dv_ref, dq_acc, dl_ref):
        j = pl.program_id(1)

        @pl.when(j == 0)
        def _():
            dq_acc[...] = jnp.zeros_like(dq_acc)
            for i in range(nq):
                rows = pl.ds(i * TQ, TQ)
                prod = do_ref[rows, :].astype(F32) * o_ref[rows, :].astype(F32)
                dl_ref[:, rows] = _as_row(jnp.sum(prod, axis=-1, keepdims=True))

        kk = jnp.concatenate([kn_ref[...], kr_ref[...]], axis=1)
        vv = v_ref[...]

        def step(i, carry, masked):
            dk, dv = carry
            off = pl.multiple_of(i * TQ, TQ)
            qi = q_ref[pl.ds(off, TQ), :]
            doi = do_ref[pl.ds(off, TQ), :]
            st = lax.dot_general(kk, qi, NT, preferred_element_type=F32) * SCALE2
            if masked:
                st = jnp.where(_diag_mask(True), st, NEG_INF)
            pt = jnp.exp2(st - lse_ref[:, pl.ds(off, TQ)])
            dv = dv + lax.dot_general(pt.astype(BF16), doi, NN, preferred_element_type=F32)
            dpt = lax.dot_general(vv, doi, NT, preferred_element_type=F32)
            dst = ((pt * (dpt - dl_ref[:, pl.ds(off, TQ)])) * SCALE).astype(BF16)
            dk = dk + lax.dot_general(dst, qi, NN, preferred_element_type=F32)
            dq_acc[pl.ds(off, TQ), :] += lax.dot_general(dst, kk, TN, preferred_element_type=F32)
            return dk, dv

        carry = step(j, (jnp.zeros((TQ, QK_PAD), F32), jnp.zeros((TQ, V_HEAD), F32)), True)
        dk, dv = lax.fori_loop(j + 1, nq, lambda i, cr: step(i, cr, False), carry)
        dkn_ref[...] = dk[:, :QK_NOPE].astype(BF16)
        dkr_ref[...] = dk[:, QK_NOPE:]
        dv_ref[...] = dv.astype(BF16)

        @pl.when(j == nq - 1)
        def _():
            dq = dq_acc[...]
            dq_ref[:, :QK_NOPE] = dq[:, :QK_NOPE].astype(BF16)
            dq_ref[:, QK_NOPE:] = _rotate(dq[:, QK_NOPE:], c_ref[...], sa_ref[...], sb_ref[...], -1.0).astype(BF16)

    row = pl.BlockSpec((None, 1, T), lambda h, j: (h, 0, 0))
    head = pl.BlockSpec((TQ, 128), lambda h, j: (j, h))
    whole = pl.BlockSpec((None, T, QK_PAD), lambda h, j: (h, 0, 0))
    tab = pl.BlockSpec((T, 128), lambda h, j: (0, 0))
    heads = pl.BlockSpec((T, V_HEAD), lambda h, j: (0, h))
    return _pallas(
        body, name="attn_bwd", grid=(N_HEADS, nq),
        in_specs=[whole, head, pl.BlockSpec((TQ, 128), lambda h, j: (j, 0)), head, heads, heads, row, tab, tab, tab],
        out_specs=[whole, head, pl.BlockSpec((None, TQ, 128), lambda h, j: (h, j, 0)), head],
        out_shape=[jax.ShapeDtypeStruct((N_HEADS, T, QK_PAD), BF16), jax.ShapeDtypeStruct((T, N_HEADS * QK_NOPE), BF16),
                   jax.ShapeDtypeStruct((N_HEADS, T, 128), F32), jax.ShapeDtypeStruct((T, N_HEADS * V_HEAD), BF16)],
        scratch_shapes=[pltpu.VMEM((T, QK_PAD), F32), pltpu.VMEM((1, T), F32)],
        params=_params(("parallel", "arbitrary")))(q, kn, kr, v, o, do, lse_row, cos, sa, sb)


def _ffn_gup(name, dg, dv, hf):
    def body(dg_ref, dv_ref, hf_ref, o_ref):
        j = pl.program_id(0)

        @pl.when(j < N_FF_BLK)
        def _():
            o_ref[...] = lax.dot_general(dg_ref[...], hf_ref[...], TN, preferred_element_type=F32).astype(BF16)

        @pl.when(j >= N_FF_BLK)
        def _():
            o_ref[...] = lax.dot_general(dv_ref[...], hf_ref[...], TN, preferred_element_type=F32).astype(BF16)

    return _pallas(
        body, name=name, grid=(N_DEV,),
        in_specs=[pl.BlockSpec((None, T, FF_BLK), lambda j: (jnp.minimum(j, N_FF_BLK - 1), 0, 0)),
                  pl.BlockSpec((None, T, FF_BLK), lambda j: (jnp.maximum(j - N_FF_BLK, 0), 0, 0)),
                  pl.BlockSpec((T, D), lambda j: (0, 0))],
        out_specs=pl.BlockSpec((None, FF_BLK, D), lambda j: (j, 0, 0)),
        out_shape=jax.ShapeDtypeStruct((N_DEV, FF_BLK, D), BF16), params=_params(("parallel",)))(dg, dv, hf)


def _ffn_layer_fwd(tag, h, gain, ex):
    hf = _rms_fwd(f"{tag}_norm", h, gain)
    g, v, act = _ffn_up_act(f"{tag}_up", hf, ex.need(f"ffn_w_up{tag[1]}", hf), ex.need(f"ffn_cw{tag[1]}", hf),
                            ex.need(f"ffn_cb{tag[1]}", hf))
    ex.at(f"{tag}_up", act)
    rows = pl.BlockSpec((TS, D), lambda i: (i, 0))
    out = _mm_sum(f"{tag}_down",
                  [(act, pl.BlockSpec((N_FF_BLK, TS, FF_BLK), lambda i: (0, i, 0)), ex.need(f"ffn_w_down{tag[1]}", act),
                    pl.BlockSpec((None, N_FF_BLK, FF_BLK, D), lambda i: (0, 0, 0, 0)), NN, 0)],
                  grid=(T // TS,), o_spec=rows, o_shape=(T, D), o_dtype=F32, add=h)
    ex.at(f"{tag}_down", out)
    return out, (hf, g, v, act)


def _ffn_layer_bwd(tag, h, gain, ex, saved, dh, dh_bf):
    hf, g, v, act = saved
    layer = tag[1]
    w_up, w_down4 = ex.need(f"ffn_w_up{layer}", dh_bf), ex.need(f"ffn_w_down{layer}", dh_bf)
    dg, dv, dcw, dcb = _ffn_dact(f"{tag}_dact", dh_bf, w_down4, g, v, ex.need(f"ffn_cw{layer}", dh_bf),
                                 ex.need(f"ffn_cb{layer}", dh_bf))
    ex.at(f"{tag}_dact", dg)
    g_down = _mm(f"{tag}_gdown", act, dh_bf, grid=(N_FF_BLK,),
                 a_spec=pl.BlockSpec((None, T, FF_BLK), lambda j: (j, 0, 0)),
                 b_spec=pl.BlockSpec((T, D), lambda j: (0, 0)),
                 o_spec=pl.BlockSpec((FF_BLK, D), lambda j: (j, 0)),
                 o_shape=(D_FF, D), o_dtype=BF16, dims=TN)
    g_up = _ffn_gup(f"{tag}_gup", dg, dv, hf)
    ex.grad("ffn_w_up", int(layer), g_up.reshape(1, N_DEV, FF_BLK, D))
    ex.grad("ffn_w_down", int(layer), g_down.reshape(1, N_DEV, D_FF // N_DEV, D))
    ex.at(f"{tag}_gup", g_up)
    part = pl.BlockSpec((N_FF_BLK, TR, FF_BLK), lambda i: (0, i, 0))
    dh_in, dh_in_bf, dgain = _mm_sum(
        f"{tag}_dhf",
        [(dg, part, w_up, pl.BlockSpec((None, N_FF_BLK, FF_BLK, D), lambda i: (0, 0, 0, 0)), NN, 0),
         (dv, part, w_up, pl.BlockSpec((None, N_FF_BLK, FF_BLK, D), lambda i: (0, 1, 0, 0)), NN, 0)],
        grid=(T // TR,), o_spec=pl.BlockSpec((TR, D), lambda i: (i, 0)), o_shape=(T, D), o_dtype=F32,
        norm_bwd=(h, [gain], dh))
    ex.at(f"{tag}_dhf", dh_in)
    return dh_in, dh_in_bf, dgain[0], dcw, dcb


def _local_step(x, pos, tgt, rep, ex):
    attn_norm, ffn_norm, final_norm = rep["attn_norm"], rep["ffn_norm"], rep["final_norm"]
    half = QK_ROPE // 2
    inv = 1.0 / (ROPE_THETA ** (jnp.arange(half, dtype=F32) / half))
    inv_freq = jnp.concatenate([inv, inv, jnp.zeros((128 - 2 * half,), F32)]).reshape(1, 128)
    tables = _rope_tables(pos, inv_freq)

    hn0 = _rms_fwd("l0_norm", x, attn_norm[0:1])
    w_in = ex.need("sc_w_in", hn0)
    ex.at("mixer_ready", hn0)
    zb, zc, zu, y = _mixer_in(hn0, w_in, ex.need("sc_conv_w", hn0))
    ex.at("l0_in", y)
    h1 = _mm_rows("l0_out", y, ex.need("sc_w_out", y), NN, F32, D, tn=512, add=x)
    ex.at("l0_out", h1)
    h2, ffn0 = _ffn_layer_fwd("f0", h1, ffn_norm[0:1], ex)

    hk, hn1 = _rms_fwd2("h2_norms", h2, rep["kv_in_norm"], attn_norm[1:2])
    ckv_raw, ckv, kr = _kv_down(hk, ex.need("w_dkv", hk), ex.need("w_kr", hk), rep["kv_latent_norm"], tables)
    kn, vv = _kv_up(ckv, ex.need("w_uk", ckv), ex.need("w_uv", ckv))

    cq_raw, cq = _down_norm("q_down", hn1, ex.need("w_dq", hn1), rep["q_latent_norm"])
    w_uq = ex.need("w_uq", cq)
    q = _q_up(cq, w_uq, tables)
    o, lse = _attn_fwd(q, kn, kr, vv)
    ex.at("attn_fwd", o)
    w_o = ex.need("w_o", o)
    h3 = _mm_rows("attn_out", o, w_o, NN, F32, D, tn=512, add=h2)
    h4, ffn1 = _ffn_layer_fwd("f1", h3, ffn_norm[1:2], ex)

    loss, dh4, dh4_bf, d_final = _final(h4, final_norm.reshape(1, D), tgt)

    dh3, dh3_bf, d_fn1, dcw1, dcb1 = _ffn_layer_bwd("f1", h3, ffn_norm[1:2], ex, ffn1, dh4, dh4_bf)
    ex.at("f1_bwd", dh3)

    do = _mm_rows("d_attn_out", dh3_bf, w_o, NT, BF16, N_HEADS * V_HEAD)
    ex.grad("w_o", None, _mm_wgrad("g_w_o", o, dh3_bf).reshape(1, N_DEV, D // N_DEV, D))
    dq_pre, dkn, dkr, dvv = _attn_bwd(q, kn, kr, vv, o, do, lse, tables)
    def rows_of(a):
        return a[None], pl.BlockSpec((1, TS, a.shape[1]), lambda i: (0, i, 0))

    def whole(wt):
        return wt[None], pl.BlockSpec((1,) + wt.shape, lambda i: (0, 0, 0))

    def row_blocks(d):
        return dict(grid=(T // TS,), o_spec=pl.BlockSpec((TS, d), lambda i: (i, 0)), o_shape=(T, d), o_dtype=F32)

    _, dcq_raw_bf, (d_qln,) = _mm_sum(
        "d_q_up", [(dq_pre, pl.BlockSpec((N_HEADS, TS, QK_PAD), lambda i: (0, i, 0)),
                    w_uq, pl.BlockSpec((N_HEADS, QK_PAD, Q_LORA), lambda i: (0, 0, 0)), NN, 0)],
        norm_bwd=(cq_raw, [rep["q_latent_norm"]], None), **row_blocks(Q_LORA))
    g_uq = _mm("g_w_uq", dq_pre, cq, grid=(N_HEADS,),
               a_spec=pl.BlockSpec((None, T, QK_PAD), lambda h: (h, 0, 0)),
               b_spec=pl.BlockSpec((T, Q_LORA), lambda h: (0, 0)),
               o_spec=pl.BlockSpec((None, QK_PAD, Q_LORA), lambda h: (h, 0, 0)),
               o_shape=(N_HEADS, QK_PAD, Q_LORA), o_dtype=BF16, dims=TN)
    ex.grad("w_uq", None, g_uq[:, :QK_NOPE + QK_ROPE].reshape(1, N_DEV, QK_NOPE + QK_ROPE, Q_LORA))
    ex.grad("w_dq", None, _mm_wgrad("g_w_dq", hn1, dcq_raw_bf).reshape(1, N_DEV, D // N_DEV, Q_LORA))

    _, dckv_raw_bf, (d_kvln,) = _mm_sum(
        "d_kv_up", [(*rows_of(dkn), *whole(ex.need("w_uk", dkn)), NT, 0),
                    (*rows_of(dvv), *whole(ex.need("w_uv", dvv)), NT, 0)],
        norm_bwd=(ckv_raw, [rep["kv_latent_norm"]], None), **row_blocks(KV_LORA))
    ex.grad("w_uk", None, _mm_wgrad("g_w_uk", ckv, dkn))
    ex.grad("w_uv", None, _mm_wgrad("g_w_uv", ckv, dvv))
    dkr_raw_bf = _rope("dk_rope", dkr, tables, -1.0, BF16, reduce_groups=True).reshape(T, 128)
    ex.grad("w_dkv", None, _mm_wgrad("g_w_dkv", hk, dckv_raw_bf).reshape(1, N_DEV, D // N_DEV, KV_LORA))
    ex.grad("w_kr", None, _mm_wgrad("g_w_kr", dkr_raw_bf, hk)[:QK_ROPE])

    dh2, dh2_bf, (d_an1, d_kvin) = _mm_sum(
        "d_h2", [(*rows_of(dcq_raw_bf), *whole(ex.need("w_dq", dcq_raw_bf)), NT, 0),
                 (*rows_of(dckv_raw_bf), *whole(ex.need("w_dkv", dckv_raw_bf)), NT, 1),
                 (*rows_of(dkr_raw_bf), *whole(ex.need("w_kr", dkr_raw_bf)), NN, 1)],
        norm_bwd=(h2, [attn_norm[1:2], rep["kv_in_norm"]], dh3), **row_blocks(D))
    ex.at("kv_bwd", dh2)

    dh1, dh1_bf, d_fn0, dcw0, dcb0 = _ffn_layer_bwd("f0", h1, ffn_norm[0:1], ex, ffn0, dh2, dh2_bf)
    ex.at("f0_bwd", dh1)

    ex.grad("sc_w_out", None, _mm_wgrad("g_sc_w_out", y, dh1_bf).reshape(1, N_DEV, D // N_DEV, D))
    dz, d_scw = _mixer_out_bwd(dh1_bf, ex.need("sc_w_out", dh1_bf), zb, zc, zu, ex.need("sc_conv_w", dh1_bf))
    g_in = _mm_wgrad("g_sc_w_in", hn0, dz)
    ex.grad("sc_w_in", None, g_in)
    ex.at("sc_bwd", g_in)
    ex.at("d_l0_in", g_in)
    grad_x, _, (d_an0,) = _mm_sum(
        "d_l0_in", [(*rows_of(dz), *whole(ex.need("sc_w_in", dz)), NT, 0)],
        norm_bwd=(x, [attn_norm[0:1]], dh1), **row_blocks(D))

    small = {
        "attn_norm": jnp.concatenate([d_an0, d_an1], axis=0),
        "ffn_norm": jnp.concatenate([d_fn0, d_fn1], axis=0),
        "final_norm": d_final.reshape(D),
        "kv_in_norm": d_kvin.reshape(D),
        "kv_latent_norm": d_kvln.reshape(KV_LORA),
        "q_latent_norm": d_qln,
        "ffn_conv_b": jnp.stack([dcb0, dcb1]).transpose(0, 2, 1, 3).reshape(2, D_FF),
        "sc_conv_w": d_scw,
        "ffn_conv_w": jnp.stack([dcw0, dcw1]).transpose(0, 2, 1, 3).reshape(2, 3, D_FF),
    }
    return loss, grad_x, small


def _place():
    return lax.axis_index("x"), lax.axis_index("y"), lax.axis_index("c")


def _peers():
    x, y, c = _place()
    return (x, y, 1 - c), [(1 - x, y), (x, 1 - y), (1 - x, 1 - y)]


def _window(ref, kind, dev):
    if kind == "blocked":
        return ref.at[:, dev]
    width = ref.shape[-1] // N_DEV
    return ref.at[:, pl.ds(pl.multiple_of(dev * width, 128), width)]


HBM_SPEC = pl.BlockSpec(memory_space=pltpu.HBM)
SEM_SPEC = pl.BlockSpec(memory_space=pltpu.SEMAPHORE)
EFFECT = pltpu.SideEffectType.DATAFLOW_SIDE_EFFECTING
TOKEN = jax.ShapeDtypeStruct((8, 128), F32)


def _hbm(a):
    return pltpu.with_memory_space_constraint(a, pltpu.HBM)


def _copies_start(name, jobs):
    nj = len(jobs)
    counts = [(len(srcs), len(lands)) for srcs, lands, _, _ in jobs]
    n_arr = sum(ns + nl for ns, nl in counts)

    def body(*refs):
        sems, token = refs[n_arr:n_arr + 2 * nj], refs[-1]
        at = 0
        for j, ((ns, nl), (_, _, ncopy, plan)) in enumerate(zip(counts, jobs)):
            copies = plan(refs[at:at + ns], refs[at + ns:at + ns + nl])
            assert len(copies) == ncopy
            for k, (sent, dst, to, _) in enumerate(copies):
                pltpu.make_async_remote_copy(src_ref=sent, dst_ref=dst, send_sem=sems[2 * j].at[k],
                                             recv_sem=sems[2 * j + 1].at[k], device_id=to, device_id_type=MESH).start()
            at += ns + nl
        token[...] = jnp.zeros_like(token)

    arrays = [a for srcs, lands, _, _ in jobs for a in list(srcs) + list(lands)]
    sem_shapes = [pltpu.SemaphoreType.DMA((ncopy,)) for _, _, ncopy, _ in jobs for _ in range(2)]
    outs = pl.pallas_call(
        body, name=name, in_specs=[HBM_SPEC] * n_arr,
        out_specs=[SEM_SPEC] * (2 * nj) + [HBM_SPEC] * n_arr + [VMEM_SPEC],
        out_shape=sem_shapes + [pltpu.HBM(a.shape, a.dtype) for a in arrays] + [TOKEN],
        input_output_aliases={i: 2 * nj + i for i in range(n_arr)},
        compiler_params=pltpu.CompilerParams(has_side_effects=EFFECT))(*[_hbm(a) for a in arrays])
    _Chain.last = outs[-1]
    flights, at = [], 2 * nj
    for j, (ns, nl) in enumerate(counts):
        flights.append((outs[2 * j], outs[2 * j + 1], list(outs[at:at + ns]), list(outs[at + ns:at + ns + nl])))
        at += ns + nl
    return flights


def _copies_wait(name, started, ncopy, plan):
    send, recv, srcs, lands = started
    ns, nl = len(srcs), len(lands)

    def body(*refs):
        send_ref, recv_ref, token = refs[ns + nl], refs[ns + nl + 1], refs[-1]
        copies = plan(refs[:ns], refs[ns:ns + nl])
        assert len(copies) == ncopy
        for k, (sent, _, to, landed) in enumerate(copies):
            cp = pltpu.make_async_remote_copy(src_ref=sent, dst_ref=landed, send_sem=send_ref.at[k],
                                              recv_sem=recv_ref.at[k], device_id=to, device_id_type=MESH)
            cp.wait_send()
            cp.wait_recv()
        token[...] = jnp.zeros_like(token)

    arrays = list(srcs) + list(lands)
    outs = pl.pallas_call(
        body, name=name, in_specs=[HBM_SPEC] * (ns + nl) + [SEM_SPEC] * 2 + [ANY_SPEC],
        out_specs=[HBM_SPEC] * (ns + nl) + [VMEM_SPEC], out_shape=[pltpu.HBM(a.shape, a.dtype) for a in arrays] + [TOKEN],
        input_output_aliases={i: i for i in range(ns + nl)},
        compiler_params=pltpu.CompilerParams(has_side_effects=EFFECT))(*arrays, send, recv, _Chain.last)
    _Chain.last = outs[-1]
    return list(outs[:ns]), list(outs[ns:-1])


def _plan_gather_chips(kinds):
    def plan(srcs, lands):
        x, y, c = _place()
        sibling, chips = _peers()
        out = []
        for t, kind in enumerate(kinds):
            mine = _window(lands[t], kind, 4 * x + 2 * y + c)
            out.append((srcs[t], mine, (x, y, c), mine))
            out.append((srcs[t], mine, sibling, _window(lands[t], kind, 4 * x + 2 * y + 1 - c)))
            for px, py in chips:
                out.append((srcs[t], mine, (px, py, c), _window(lands[t], kind, 4 * px + 2 * py + c)))
        return out
    return plan, 5 * len(kinds)


def _plan_gather_all(n):
    def plan(srcs, lands):
        x, y, c = _place()
        out = []
        for t in range(n):
            mine = lands[t].at[:, 4 * x + 2 * y + c]
            for m in range(N_DEV):
                px, py, pc = (1 - x if m & 4 else x), (1 - y if m & 2 else y), (1 - c if m & 1 else c)
                out.append((srcs[t], mine, (px, py, pc), lands[t].at[:, 4 * px + 2 * py + pc]))
        return out
    return plan, N_DEV * n


def _plan_gather_sibling(kinds):
    def plan(srcs, lands):
        _, _, c = _place()
        sibling, chips = _peers()
        out = []
        for t, kind in enumerate(kinds):
            for px, py in chips:
                w = _window(lands[t], kind, 4 * px + 2 * py + c)
                out.append((w, w, sibling, _window(lands[t], kind, 4 * px + 2 * py + 1 - c)))
        return out
    return plan, 3 * len(kinds)


def _plan_scatter_sibling(kinds):
    def plan(srcs, lands):
        _, _, c = _place()
        sibling, _ = _peers()
        out = []
        for t, kind in enumerate(kinds):
            for k in range(N_CHIP):
                out.append((_window(srcs[t], kind, 2 * k + 1 - c), lands[t].at[k], sibling, lands[t].at[k]))
        return out
    return plan, N_CHIP * len(kinds)


def _plan_scatter_chips(n):
    def plan(srcs, lands):
        x, y, c = _place()
        _, chips = _peers()
        out = []
        for t in range(n):
            for px, py in chips:
                out.append((srcs[t].at[2 * px + py], lands[t].at[2 * x + y], (px, py, c), lands[t].at[2 * px + py]))
        return out
    return plan, 3 * n


def _landing(shard, kind):
    if kind == "blocked":
        return lax.empty((shard.shape[0], N_DEV) + shard.shape[1:], shard.dtype)
    return lax.empty((shard.shape[0], N_DEV * shard.shape[1]), shard.dtype)


def _chip_sums(name, grads, kinds, recvs, c):
    n = len(grads)
    in_specs, out_specs, out_shape, args = [], [], [], []
    for gr, kind, rv in zip(grads, kinds, recvs):
        if kind == "blocked":
            rows, w = gr.shape[2], gr.shape[3]
            in_specs.append(pl.BlockSpec((None, None, rows, w), lambda k, cref: (0, 2 * k + cref[0], 0, 0)))
        else:
            rows, w = gr.shape[0], gr.shape[1] // N_DEV
            in_specs.append(pl.BlockSpec((rows, w), lambda k, cref: (0, 2 * k + cref[0])))
        blk = pl.BlockSpec((None, rows, w), lambda k, cref: (k, 0, 0))
        in_specs.append(blk)
        out_specs.append(blk)
        out_shape.append(jax.ShapeDtypeStruct((N_CHIP, rows, w), BF16))
        args += [gr, rv.reshape(N_CHIP, rows, w)]

    def body(*refs):
        for t in range(n):
            g_ref, r_ref, o_ref = refs[1 + 2 * t], refs[2 + 2 * t], refs[1 + 2 * n + t]
            o_ref[...] = (g_ref[...].astype(F32) + r_ref[...].astype(F32)).astype(BF16)

    return _pallas(body, name=name, n_prefetch=1, grid=(N_CHIP,), in_specs=in_specs, out_specs=out_specs,
                   out_shape=out_shape, params=_params(("parallel",)))(c, *args)


def _adamw_math(g, wv, mv, vv):
    m = ADAM_B1 * mv + (1.0 - ADAM_B1) * g
    v = ADAM_B2 * vv + (1.0 - ADAM_B2) * (g * g)
    m_hat = m / (1.0 - ADAM_B1 ** ADAM_STEP)
    v_hat = v / (1.0 - ADAM_B2 ** ADAM_STEP)
    delta = -ADAM_LR * (m_hat / (jnp.sqrt(v_hat) + ADAM_EPS) + ADAM_WD * wv)
    return delta, m, v


ADAM_STEPS = 2


def _adamw_group(name, items, chip_ids):
    n = len(items)
    in_specs, out_specs, out_shape, args, prevs = [], [], [], [chip_ids], []
    for own, recv, w3, m3, v3, layer, _ in items:
        nl, rows, w = w3.shape
        tr = rows // ADAM_STEPS
        assert tr % 16 == 0, (name, rows)
        in_specs += [pl.BlockSpec((None, tr, w), lambda i, ids, slot=slot: (ids[slot], i, 0)) for slot in range(4)]
        slab = pl.BlockSpec((None, tr, w), lambda i, ids, layer=layer: (layer, i, 0))
        in_specs += [slab] * 3
        out_specs += [slab] * 4
        out_shape += [jax.ShapeDtypeStruct((nl, rows, w), F32)] * 4
        args += [own, recv, recv, recv, w3, m3, v3]
    aliases = {}
    for t, item in enumerate(items):
        if item[6] is not None:
            for k in range(4):
                aliases[len(args) + k] = 4 * t + k
            in_specs += [ANY_SPEC] * 4
            args += list(item[6])
            prevs.append(t)
    n_in = 1 + 7 * n + 4 * len(prevs)

    def body(*refs):
        for t in range(n):
            own_ref, r1_ref, r2_ref, r3_ref, w_ref, m_ref, v_ref = refs[1 + 7 * t:8 + 7 * t]
            g_ref, d_ref, nm_ref, nv_ref = refs[n_in + 4 * t:n_in + 4 * t + 4]
            g = ((own_ref[...].astype(F32) + r1_ref[...].astype(F32)) + r2_ref[...].astype(F32)) + r3_ref[...].astype(F32)
            g_ref[...] = g
            d_ref[...], nm_ref[...], nv_ref[...] = _adamw_math(g, w_ref[...], m_ref[...], v_ref[...])

    outs = _pallas(body, name=name, n_prefetch=1, grid=(ADAM_STEPS,), in_specs=in_specs, out_specs=out_specs,
                   out_shape=out_shape, aliases=aliases, params=_params(("parallel",)))(*args)
    return [list(outs[4 * t:4 * t + 4]) for t in range(n)]


def _adamw_small(gathered, ws, ms, vs):
    n = len(gathered)
    full = [w is not None for w in ws]
    args = list(gathered)
    out_shape = []
    for t in range(n):
        shape = jax.ShapeDtypeStruct(gathered[t].shape[2:], F32)
        if full[t]:
            args += [ws[t], ms[t], vs[t]]
            out_shape += [shape] * 4
        else:
            out_shape += [shape]

    def body(*refs):
        i_in, i_out = n, len(args)
        for t in range(n):
            p_ref = refs[t]
            g = p_ref[0, 0]
            for k in range(1, N_DEV):
                g = g + p_ref[0, k]
            refs[i_out][...] = g
            if full[t]:
                w_ref, m_ref, v_ref = refs[i_in:i_in + 3]
                refs[i_out + 1][...], refs[i_out + 2][...], refs[i_out + 3][...] = _adamw_math(
                    g, w_ref[...], m_ref[...], v_ref[...])
                i_in += 3
                i_out += 4
            else:
                i_out += 1

    outs = _pallas(body, name="adamw_small", in_specs=[VMEM_SPEC] * len(args), out_specs=[VMEM_SPEC] * len(out_shape),
                   out_shape=out_shape, params=pltpu.CompilerParams(vmem_limit_bytes=VMEM_LIMIT))(*args)
    result, i = [], 0
    for t in range(n):
        k = 4 if full[t] else 1
        result.append(list(outs[i:i + k]))
        i += k
    return result


def _adamw_plain(name, gs, ws, ms, vs):
    n = len(gs)

    def body(*refs):
        for t in range(n):
            g_ref, w_ref, m_ref, v_ref = refs[4 * t:4 * t + 4]
            outs = refs[4 * n + 3 * t:4 * n + 3 * t + 3]
            outs[0][...], outs[1][...], outs[2][...] = _adamw_math(g_ref[...], w_ref[...], m_ref[...], v_ref[...])

    args, out_shape = [], []
    for g, w, m, v in zip(gs, ws, ms, vs):
        args += [g, w, m, v]
        out_shape += [jax.ShapeDtypeStruct(w.shape, F32)] * 3
    outs = _pallas(body, name=name, in_specs=[VMEM_SPEC] * len(args), out_specs=[VMEM_SPEC] * len(out_shape),
                   out_shape=out_shape, params=pltpu.CompilerParams(vmem_limit_bytes=VMEM_LIMIT))(*args)
    return [list(outs[3 * t:3 * t + 3]) for t in range(n)]


KIND = {"sc_w_in": "cols", "sc_w_out": "blocked", "w_dkv": "blocked", "w_kr": "cols", "w_uk": "cols", "w_uv": "cols",
        "w_dq": "blocked", "w_uq": "blocked", "w_o": "blocked", "ffn_w_up": "blocked", "ffn_w_down": "blocked",
        "conv": "blocked"}
GATHER_GROUPS = (("mixer", ("sc_w_in", "sc_w_out", "conv")),
                 ("up0", ("ffn_w_up0",)),
                 ("down0", ("ffn_w_down0",)),
                 ("attn", ("w_dkv", "w_kr", "w_uk", "w_uv", "w_dq", "w_uq", "w_o")),
                 ("ffn1", ("ffn_w_up1", "ffn_w_down1")))
SCATTER_GROUPS = (("ffn1", (("ffn_w_up", 1), ("ffn_w_down", 1))),
                  ("attn", (("w_o", None), ("w_uq", None), ("w_dq", None), ("w_uk", None), ("w_uv", None),
                            ("w_dkv", None), ("w_kr", None))),
                  ("ffn0", (("ffn_w_up", 0), ("ffn_w_down", 0))),
                  ("mixer", (("sc_w_out", None), ("sc_w_in", None))))
SCHEDULE = {
    "begin": (("gather_start", "mixer"),),
    "mixer_ready": (("gather_start", "up0"),),
    "l0_out": (("gather_forward", "up0"), ("gather_start", "down0")),
    "f0_up": (("gather_forward", "down0"), ("gather_start", "attn")),
    "f0_down": (("gather_forward", "attn"), ("gather_start", "ffn1")),
    "attn_fwd": (("gather_forward", "ffn1"),),
    "f1_gup": (("scatter_sibling", "ffn1"),),
    "f1_dhf": (("scatter_chips", "ffn1"),),
    "kv_bwd": (("scatter_sibling", "attn"), ("scatter_done", "ffn1")),
    "f0_dact": (("scatter_chips", "attn"),),
    "f0_gup": (("scatter_sibling", "ffn0"),),
    "f0_dhf": (("scatter_chips", "ffn0"),),
    "f0_bwd": (("scatter_done", "attn"),),
    "sc_bwd": (("scatter_sibling", "mixer"),),
    "d_l0_in": (("scatter_chips", "mixer"),),
}
FINISH = (("scatter_done", "ffn0"), ("scatter_done", "mixer"))
STAGES = {"gather_start": 1, "gather_forward": 2, "gather_done": 3,
          "scatter_sibling": 1, "scatter_chips": 2, "scatter_done": 3}
SMALL_W_ROWS = 24


def _pack(arrays, rows):
    flat = jnp.concatenate([a.reshape(-1).astype(F32) for a in arrays])
    return jnp.pad(flat, (0, rows * 128 - flat.shape[0])).reshape(rows, 128)


STORED_TRANSPOSED = ("ffn_w_up", "w_uq", "w_kr")


def _stored(name, a):
    return jnp.swapaxes(a, -1, -2) if name in STORED_TRANSPOSED else a


def _base(name):
    if name.startswith("ffn_w_") and name[-1] in "01":
        return name[:-1], int(name[-1])
    return name, None


class _Exchange:
    def __init__(self, wts, mom, var, ffn_conv_b):
        self.wts, self.mom, self.var = wts, mom, var
        x, y, c = _place()
        self.c_arr = jnp.reshape(c, (1,)).astype(jnp.int32)
        chip = 2 * x + y
        self.chip_ids = jnp.stack([chip, chip ^ 1, chip ^ 2, chip ^ 3]).astype(jnp.int32)
        self.ready = {"ffn_cb0": ffn_conv_b.reshape(2, N_FF_BLK, 1, FF_BLK)[0],
                      "ffn_cb1": ffn_conv_b.reshape(2, N_FF_BLK, 1, FF_BLK)[1]}
        self.gathers, self.group_of = {}, {}
        self.grads, self.scatters, self.results, self.queue = {}, {}, {}, []
        for gname, names in GATHER_GROUPS:
            self.gathers[gname] = dict(stage=0, names=names, kinds=[KIND[_base(nm)[0]] for nm in names])
            for nm in names:
                self.group_of[nm] = gname
        for nm in ("sc_conv_w", "ffn_cw0", "ffn_cw1"):
            self.group_of[nm] = "mixer"
        self.at("begin", None)

    def _shard(self, name):
        if name == "conv":
            return _pack([self.wts["sc_conv_w"], self.wts["ffn_conv_w"]], SMALL_W_ROWS).reshape(1, SMALL_W_ROWS, 128)
        base, layer = _base(name)
        a = _stored(base, self.wts[base])
        if layer is not None:
            a = a[layer:layer + 1]
        if KIND[base] == "cols":
            return a.reshape(a.shape[-2], a.shape[-1]).astype(BF16)
        return a.reshape((-1,) + a.shape[-2:]).astype(BF16)

    def _start(self, name, srcs, lands, ncopy, plan, st):
        self.queue.append((name, (srcs, lands, ncopy, plan), st))

    def _flush(self):
        if self.queue:
            flights = _copies_start("__".join(name for name, _, _ in self.queue), [job for _, job, _ in self.queue])
            for (_, _, st), flight in zip(self.queue, flights):
                st["flight"] = flight
            self.queue = []

    def _flight(self, st):
        self._flush()
        return st["flight"]

    def _gather_to(self, gname, stage, after):
        st = self.gathers[gname]
        if st["stage"] < 1 <= stage:
            shards = [self._shard(nm) for nm in st["names"]]
            lands = [_landing(s, kind) for s, kind in zip(shards, st["kinds"])]
            plan, ncopy = _plan_gather_chips(st["kinds"])
            self._start(f"ag_{gname}_chips", shards, lands, ncopy, plan, st)
            st["stage"] = 1
        if st["stage"] < 2 <= stage:
            plan, ncopy = _plan_gather_chips(st["kinds"])
            _, lands = _copies_wait(f"ag_{gname}_chips_wait", self._flight(st), ncopy, plan)
            plan, ncopy = _plan_gather_sibling(st["kinds"])
            self._start(f"ag_{gname}_sibling", [], lands, ncopy, plan, st)
            st["stage"] = 2
        if st["stage"] < 3 <= stage:
            plan, ncopy = _plan_gather_sibling(st["kinds"])
            _, lands = _copies_wait(f"ag_{gname}_sibling_wait", self._flight(st), ncopy, plan)
            for nm, land in zip(st["names"], lands):
                self._arrived(nm, land)
            st["stage"] = 3

    def _arrived(self, name, land):
        if name == "conv":
            conv = land.reshape(N_DEV, SMALL_W_ROWS * 128)
            self.ready["sc_conv_w"] = conv[:, :3 * 128].reshape(N_DEV, 3, 128).transpose(1, 0, 2).reshape(3, D)
            fcw = conv[:, 3 * 128:3 * 128 + 6 * 352].reshape(N_DEV, 2, 3, 352).transpose(1, 2, 0, 3)
            fcw = fcw.reshape(2, 3, N_FF_BLK, FF_BLK).transpose(0, 2, 1, 3)
            self.ready["ffn_cw0"], self.ready["ffn_cw1"] = fcw[0], fcw[1]
        elif name in ("sc_w_in", "w_uk", "w_uv") or name.startswith("ffn_w_up"):
            self.ready[name] = land
        elif name.startswith("ffn_w_down"):
            self.ready[name] = land.reshape(1, N_FF_BLK, FF_BLK, D)
        elif name == "w_kr":
            self.ready[name] = jnp.pad(land, ((0, 128 - QK_ROPE), (0, 0)))
        elif name == "w_uq":
            self.ready[name] = jnp.pad(land.reshape(N_HEADS, QK_NOPE + QK_ROPE, Q_LORA),
                                       ((0, 0), (0, QK_PAD - QK_NOPE - QK_ROPE), (0, 0)))
        else:
            self.ready[name] = land.reshape(D, land.shape[-1])

    def need(self, name, after):
        if name not in self.ready:
            self._gather_to(self.group_of[name], 3, after)
            self._flush()
        return self.ready[name]

    def grad(self, name, layer, array):
        self.grads[(name, layer)] = array

    def _scatter_to(self, gname, stage, after):
        keys = dict(SCATTER_GROUPS)[gname]
        st = self.scatters.setdefault(gname, dict(stage=0))
        kinds = [KIND[nm] for nm, _ in keys]
        if st["stage"] < 1 <= stage:
            grads = [self.grads[key] for key in keys]
            lands = []
            for gr, kind in zip(grads, kinds):
                shard = (gr.shape[0],) + gr.shape[2:] if kind == "blocked" else (gr.shape[0], gr.shape[1] // N_DEV)
                lands.append(lax.empty((N_CHIP,) + shard, BF16))
            plan, ncopy = _plan_scatter_sibling(kinds)
            self._start(f"rs_{gname}_sibling", grads, lands, ncopy, plan, st)
            st["stage"] = 1
        if st["stage"] < 2 <= stage:
            plan, ncopy = _plan_scatter_sibling(kinds)
            grads, recvs = _copies_wait(f"rs_{gname}_sibling_wait", self._flight(st), ncopy, plan)
            sums = _chip_sums(f"rs_{gname}_sums", grads, kinds, recvs, self.c_arr)
            lands = [lax.empty(s.shape, BF16) for s in sums]
            plan, ncopy = _plan_scatter_chips(len(sums))
            self._start(f"rs_{gname}_chips", sums, lands, ncopy, plan, st)
            st["stage"] = 2
        if st["stage"] < 3 <= stage:
            plan, ncopy = _plan_scatter_chips(len(keys))
            sums, recvs = _copies_wait(f"rs_{gname}_chips_wait", self._flight(st), ncopy, plan)
            items = []
            for (nm, layer), own, rv in zip(keys, sums, recvs):
                nl = 1 if layer is None else 2
                rows, w = own.shape[1], own.shape[2]
                w3, m3, v3 = (_stored(nm, src[nm]).reshape(nl, rows, w) for src in (self.wts, self.mom, self.var))
                items.append((own, rv, w3, m3, v3, 0 if layer is None else layer, self.results.get(nm)))
            outs = _adamw_group(f"adamw_{gname}", items, self.chip_ids)
            for (nm, _), out in zip(keys, outs):
                self.results[nm] = out
            st["stage"] = 3

    def at(self, place, after):
        for action, gname in SCHEDULE.get(place, ()):
            self._advance(action, gname, after)
        self._flush()

    def _advance(self, action, gname, after):
        if action.startswith("gather"):
            self._gather_to(gname, STAGES[action], after)
        else:
            self._scatter_to(gname, STAGES[action], after)

    def finish(self, after):
        for action, gname in FINISH:
            self._advance(action, gname, after)
        for gname, _ in SCATTER_GROUPS:
            self._scatter_to(gname, 3, after)
        return {nm: [_stored(nm, o.reshape(_stored(nm, self.wts[nm]).shape)) for o in outs]
                for nm, outs in self.results.items()}


REPLICATED = ("attn_norm", "ffn_norm", "final_norm", "kv_in_norm", "kv_latent_norm", "q_latent_norm", "ffn_conv_b")
WEIGHTS = ("attn_norm", "ffn_norm", "final_norm", "sc_w_in", "sc_conv_w", "sc_w_out", "kv_in_norm", "w_dkv",
           "kv_latent_norm", "w_kr", "w_uk", "w_uv", "w_dq", "q_latent_norm", "w_uq", "w_o", "ffn_w_up", "ffn_conv_w",
           "ffn_conv_b", "ffn_w_down")


def kernel(x, positions, attn_norm, ffn_norm, final_norm, sc_w_in, sc_conv_w, sc_w_out, kv_in_norm, w_dkv, kv_latent_norm, w_kr, w_uk, w_uv, w_dq, q_latent_norm, w_uq, w_o, ffn_w_up, ffn_conv_w, ffn_conv_b, ffn_w_down, loss_target, m_attn_norm, m_ffn_norm, m_final_norm, m_sc_w_in, m_sc_conv_w, m_sc_w_out, m_kv_in_norm, m_w_dkv, m_kv_latent_norm, m_w_kr, m_w_uk, m_w_uv, m_w_dq, m_q_latent_norm, m_w_uq, m_w_o, m_ffn_w_up, m_ffn_conv_w, m_ffn_conv_b, m_ffn_w_down, v_attn_norm, v_ffn_norm, v_final_norm, v_sc_w_in, v_sc_conv_w, v_sc_w_out, v_kv_in_norm, v_w_dkv, v_kv_latent_norm, v_w_kr, v_w_uk, v_w_uv, v_w_dq, v_q_latent_norm, v_w_uq, v_w_o, v_ffn_w_up, v_ffn_conv_w, v_ffn_conv_b, v_ffn_w_down):
    wts = dict(attn_norm=attn_norm, ffn_norm=ffn_norm, final_norm=final_norm, sc_w_in=sc_w_in, sc_conv_w=sc_conv_w,
               sc_w_out=sc_w_out, kv_in_norm=kv_in_norm, w_dkv=w_dkv, kv_latent_norm=kv_latent_norm, w_kr=w_kr,
               w_uk=w_uk, w_uv=w_uv, w_dq=w_dq, q_latent_norm=q_latent_norm, w_uq=w_uq, w_o=w_o, ffn_w_up=ffn_w_up,
               ffn_conv_w=ffn_conv_w, ffn_conv_b=ffn_conv_b, ffn_w_down=ffn_w_down)
    mom = dict(attn_norm=m_attn_norm, ffn_norm=m_ffn_norm, final_norm=m_final_norm, sc_w_in=m_sc_w_in,
               sc_conv_w=m_sc_conv_w, sc_w_out=m_sc_w_out, kv_in_norm=m_kv_in_norm, w_dkv=m_w_dkv,
               kv_latent_norm=m_kv_latent_norm, w_kr=m_w_kr, w_uk=m_w_uk, w_uv=m_w_uv, w_dq=m_w_dq,
               q_latent_norm=m_q_latent_norm, w_uq=m_w_uq, w_o=m_w_o, ffn_w_up=m_ffn_w_up, ffn_conv_w=m_ffn_conv_w,
               ffn_conv_b=m_ffn_conv_b, ffn_w_down=m_ffn_w_down)
    var = dict(attn_norm=v_attn_norm, ffn_norm=v_ffn_norm, final_norm=v_final_norm, sc_w_in=v_sc_w_in,
               sc_conv_w=v_sc_conv_w, sc_w_out=v_sc_w_out, kv_in_norm=v_kv_in_norm, w_dkv=v_w_dkv,
               kv_latent_norm=v_kv_latent_norm, w_kr=v_w_kr, w_uk=v_w_uk, w_uv=v_w_uv, w_dq=v_w_dq,
               q_latent_norm=v_q_latent_norm, w_uq=v_w_uq, w_o=v_w_o, ffn_w_up=v_ffn_w_up, ffn_conv_w=v_ffn_conv_w,
               ffn_conv_b=v_ffn_conv_b, ffn_w_down=v_ffn_w_down)
    xi, yi, ci = _place()
    me = 4 * xi + 2 * yi + ci
    _Chain.last = None

    ex = _Exchange(wts, mom, var, ffn_conv_b)
    rep = {
        "attn_norm": attn_norm, "ffn_norm": ffn_norm, "final_norm": final_norm,
        "kv_in_norm": kv_in_norm.reshape(1, D), "kv_latent_norm": kv_latent_norm.reshape(1, KV_LORA),
        "q_latent_norm": q_latent_norm.reshape(1, Q_LORA),
    }
    loss, grad_x, small = _local_step(x.reshape(T, D), positions.reshape(T, 1), loss_target.reshape(T, D), rep, ex)

    def rows_of(a):
        return a.reshape(-1, a.shape[-1])

    small_order = list(REPLICATED) + ["sc_conv_w", "ffn_conv_w"]
    shards = [loss.reshape(1, 1, 128)] + [rows_of(small[nm])[None] for nm in small_order]
    plan, ncopy = _plan_gather_all(len(shards))
    flight, = _copies_start("ag_small", [(shards, [lax.empty((1, N_DEV) + s.shape[1:], F32) for s in shards], ncopy, plan)])
    results = ex.finish(grad_x)
    _, gathered = _copies_wait("ag_small_wait", flight, ncopy, plan)
    params = [[None] + [rows_of(src[nm]) for nm in REPLICATED] + [None, None] for src in (wts, mom, var)]
    summed = _adamw_small(gathered, *params)
    loss_total = summed[0][0][0, 0]
    for nm, vals in zip(REPLICATED, summed[1:1 + len(REPLICATED)]):
        results[nm] = [a.reshape(wts[nm].shape) for a in vals]
    g_scw = lax.dynamic_slice(summed[-2][0], (0, me * 128), (3, 128))
    g_fcw = lax.dynamic_slice(summed[-1][0], (0, me * 352), (6, 352))
    conv = _adamw_plain("adamw_conv", [g_scw, g_fcw], *[[rows_of(src["sc_conv_w"]), rows_of(src["ffn_conv_w"])]
                                                        for src in (wts, mom, var)])
    for nm, g_own, vals in zip(("sc_conv_w", "ffn_conv_w"), (g_scw, g_fcw), conv):
        results[nm] = [a.reshape(wts[nm].shape) for a in [g_own] + vals]

    outs = [loss_total, grad_x.reshape(1, T, D)]
    for slot in range(4):
        outs.extend(results[nm][slot] for nm in WEIGHTS)
    return tuple(outs)
```

```python
import jax
import jax.numpy as jnp
from jax import lax
from jax.experimental import pallas as pl
from jax.experimental.pallas import tpu as pltpu

F32 = jnp.float32
BF16 = jnp.bfloat16

T = 2048
D = 1024
N_HEADS = 8
QK_NOPE = 128
QK_ROPE = 64
V_HEAD = 128
Q_LORA = 384
KV_LORA = 256
D_FF = 2816
CHUNK = 64
ROPE_THETA = 10000.0
EPS = 1e-6
NEG_INF = -1e30
ADAM_LR = 0.001
ADAM_B1 = 0.9
ADAM_B2 = 0.999
ADAM_EPS = 1e-08
ADAM_WD = 0.01
ADAM_STEP = 10

N_DEV = 8
N_CHIP = 4
FF_BLK = D_FF * 2 // N_DEV
N_FF_BLK = D_FF // FF_BLK
QK_PAD = 256
HALO = 16

TM = 1024
TS = 512
TR = 256
TQ = 512
VMEM_LIMIT = 56 * 1024 * 1024

NN = (((1,), (0,)), ((), ()))
NT = (((1,), (1,)), ((), ()))
TN = (((0,), (0,)), ((), ()))
MESH = pl.DeviceIdType.MESH


def _params(sem):
    return pltpu.CompilerParams(dimension_semantics=sem, vmem_limit_bytes=VMEM_LIMIT)


ANY_SPEC = pl.BlockSpec(memory_space=pl.ANY)
VMEM_SPEC = pl.BlockSpec(memory_space=pltpu.VMEM)


class _Chain:
    last = None


def _pallas(body, *, name, in_specs, out_specs, out_shape, grid=(), scratch_shapes=(), n_prefetch=0, aliases=None,
            params=None):
    def run(*args):
        after = _Chain.last
        n_lead = len(args)
        specs, operands, fn = list(in_specs), list(args), body
        if after is not None:
            def fn(*refs):
                return body(*refs[:n_lead], *refs[n_lead + 1:])
            specs.append(ANY_SPEC)
            operands.append(after)
        kw = dict(name=name, out_shape=out_shape, input_output_aliases=aliases or {})
        if params is not None:
            kw["compiler_params"] = params
        if n_prefetch:
            kw["grid_spec"] = pltpu.PrefetchScalarGridSpec(
                num_scalar_prefetch=n_prefetch, grid=grid, in_specs=specs, out_specs=out_specs,
                scratch_shapes=scratch_shapes)
        else:
            kw.update(grid=grid, in_specs=specs, out_specs=out_specs, scratch_shapes=scratch_shapes)
        outs = pl.pallas_call(fn, **kw)(*operands)
        _Chain.last = outs[0] if isinstance(outs, (list, tuple)) else outs
        return outs
    return run


def _mm(name, a, b, *, grid, a_spec, b_spec, o_spec, o_shape, o_dtype, dims, k_axis=None, acc_shape=None,
        add=None, add_spec=None):
    nk = grid[k_axis] if k_axis is not None else 1
    has_add = add is not None

    def body(*refs):
        a_ref, b_ref = refs[0], refs[1]
        p = 2
        add_ref = None
        if has_add:
            add_ref = refs[p]
            p += 1
        o_ref = refs[p]
        p += 1
        r = lax.dot_general(a_ref[...].astype(BF16), b_ref[...].astype(BF16), dims, preferred_element_type=F32)
        if k_axis is None:
            if has_add:
                r = r + add_ref[...].astype(F32)
            o_ref[...] = r.astype(o_dtype)
        else:
            acc = refs[p]
            k = pl.program_id(k_axis)

            @pl.when(k == 0)
            def _():
                acc[...] = r

            @pl.when(k > 0)
            def _():
                acc[...] += r

            @pl.when(k == nk - 1)
            def _():
                t = acc[...]
                if has_add:
                    t = t + add_ref[...].astype(F32)
                o_ref[...] = t.astype(o_dtype)

    in_specs = [a_spec, b_spec]
    args = [a, b]
    if has_add:
        in_specs.append(add_spec if add_spec is not None else o_spec)
        args.append(add)
    sem = tuple("arbitrary" if ax == k_axis else "parallel" for ax in range(len(grid)))
    scratch = [pltpu.VMEM(acc_shape, F32)] if k_axis is not None else []
    return _pallas(body, name=name, grid=grid, in_specs=in_specs, out_specs=o_spec,
                   out_shape=jax.ShapeDtypeStruct(o_shape, o_dtype), scratch_shapes=scratch, params=_params(sem))(*args)


def _mm_sum(name, parts, *, grid, o_spec, o_shape, o_dtype, add=None, norm_bwd=None):
    has_add = add is not None
    np_ = len(parts)
    nn = 1 if norm_bwd is None else len(norm_bwd[1])
    has_res = norm_bwd is not None and norm_bwd[2] is not None

    def body(*refs):
        accs = [None] * nn
        for p, (_, _, _, _, dims, n) in enumerate(parts):
            a_ref, b_ref = refs[2 * p], refs[2 * p + 1]
            for k in range(a_ref.shape[0]):
                r = lax.dot_general(a_ref[k], b_ref[k], dims, preferred_element_type=F32)
                accs[n] = r if accs[n] is None else accs[n] + r
        if norm_bwd is None:
            acc = accs[0]
            if has_add:
                acc = acc + refs[2 * np_][...]
            refs[-1][...] = acc.astype(o_dtype)
            return
        x_ref, g_refs = refs[2 * np_], refs[2 * np_ + 1:2 * np_ + 1 + nn]
        dx_ref, dxb_ref, dg_refs = refs[-2 - nn], refs[-1 - nn], refs[-nn:]
        xv = x_ref[...]
        r = lax.rsqrt(jnp.mean(xv * xv, axis=-1, keepdims=True) + EPS)
        xn = xv * r
        dx = refs[2 * np_ + 1 + nn][...] if has_res else None
        sums = []
        for acc, g_ref in zip(accs, g_refs):
            gdy = acc * g_ref[...]
            t = r * (gdy - xn * jnp.mean(gdy * xn, axis=-1, keepdims=True))
            dx = t if dx is None else dx + t
            sums.append(jnp.sum(acc * xn, axis=0, keepdims=True))
        dx_ref[...] = dx
        dxb_ref[...] = dx.astype(BF16)

        @pl.when(pl.program_id(0) == 0)
        def _():
            for dg_ref, part in zip(dg_refs, sums):
                dg_ref[...] = part

        @pl.when(pl.program_id(0) > 0)
        def _():
            for dg_ref, part in zip(dg_refs, sums):
                dg_ref[...] += part

    in_specs, args = [], []
    for a, a_spec, b, b_spec, _, _ in parts:
        in_specs += [a_spec, b_spec]
        args += [a, b]
    if norm_bwd is None:
        if has_add:
            in_specs.append(o_spec)
            args.append(add)
        return _pallas(body, name=name, grid=grid, in_specs=in_specs, out_specs=o_spec,
                       out_shape=jax.ShapeDtypeStruct(o_shape, o_dtype),
                       params=_params(("parallel",) * len(grid)))(*args)
    x, gains, dres = norm_bwd
    vec = pl.BlockSpec((1, o_shape[1]), lambda i: (0, 0))
    in_specs += [o_spec] + [vec] * nn + ([o_spec] if has_res else [])
    args += [x] + list(gains) + ([dres] if has_res else [])
    outs = _pallas(body, name=name, grid=grid, in_specs=in_specs, out_specs=[o_spec, o_spec] + [vec] * nn,
                   out_shape=[jax.ShapeDtypeStruct(o_shape, F32), jax.ShapeDtypeStruct(o_shape, BF16)]
                   + [jax.ShapeDtypeStruct((1, o_shape[1]), F32)] * nn,
                   params=_params(("arbitrary",)))(*args)
    return outs[0], outs[1], list(outs[2:])


def _mm_rows(name, a, b, dims, o_dtype, n_out, *, tn=None, add=None):
    k = a.shape[1]
    tn = n_out if tn is None else tn
    if dims == NN:
        b_spec = pl.BlockSpec((k, tn), lambda n, i: (0, n))
    else:
        b_spec = pl.BlockSpec((tn, k), lambda n, i: (n, 0))
    return _mm(name, a, b, grid=(n_out // tn, T // TM),
               a_spec=pl.BlockSpec((TM, k), lambda n, i: (i, 0)), b_spec=b_spec,
               o_spec=pl.BlockSpec((TM, tn), lambda n, i: (i, n)), o_shape=(T, n_out), o_dtype=o_dtype,
               dims=dims, add=add)


def _mm_wgrad(name, a, b, *, tn=512):
    k, n = a.shape[1], b.shape[1]
    tn = min(tn, n)
    return _mm(name, a, b, grid=(n // tn,),
               a_spec=pl.BlockSpec((T, k), lambda j: (0, 0)), b_spec=pl.BlockSpec((T, tn), lambda j: (0, j)),
               o_spec=pl.BlockSpec((k, tn), lambda j: (0, j)), o_shape=(k, n), o_dtype=BF16, dims=TN)


def _rms_fwd(name, x, g):
    d = x.shape[1]

    def body(x_ref, g_ref, o_ref):
        xv = x_ref[...]
        r = lax.rsqrt(jnp.mean(xv * xv, axis=-1, keepdims=True) + EPS)
        o_ref[...] = ((xv * r) * g_ref[...]).astype(BF16)

    return _pallas(
        body, name=name, grid=(T // TM,),
        in_specs=[pl.BlockSpec((TM, d), lambda i: (i, 0)), pl.BlockSpec((1, d), lambda i: (0, 0))],
        out_specs=pl.BlockSpec((TM, d), lambda i: (i, 0)),
        out_shape=jax.ShapeDtypeStruct((T, d), BF16), params=_params(("parallel",)))(x, g)


def _rows_call(name, body, row_ins, whole_ins, outs):
    in_specs = [pl.BlockSpec((TM, a.shape[1]), lambda i: (i, 0)) for a in row_ins]
    in_specs += [pl.BlockSpec(a.shape, lambda i: (0, 0)) for a in whole_ins]
    return _pallas(
        body, name=name, grid=(T // TM,), in_specs=in_specs,
        out_specs=[pl.BlockSpec((TM, d), lambda i: (i, 0)) for d, _ in outs],
        out_shape=[jax.ShapeDtypeStruct((T, d), dt) for d, dt in outs],
        params=_params(("parallel",)))(*row_ins, *whole_ins)


def _rms(xv, g):
    return (xv * lax.rsqrt(jnp.mean(xv * xv, axis=-1, keepdims=True) + EPS)) * g


def _rms_fwd2(name, x, g1, g2):
    d = x.shape[1]

    def body(x_ref, g1_ref, g2_ref, o1_ref, o2_ref):
        xv = x_ref[...]
        xn = xv * lax.rsqrt(jnp.mean(xv * xv, axis=-1, keepdims=True) + EPS)
        o1_ref[...] = (xn * g1_ref[...]).astype(BF16)
        o2_ref[...] = (xn * g2_ref[...]).astype(BF16)

    return _rows_call(name, body, [x], [g1, g2], [(d, BF16), (d, BF16)])


def _down_norm(name, a, w, g):
    n = w.shape[1]

    def body(a_ref, w_ref, g_ref, raw_ref, o_ref):
        raw = lax.dot_general(a_ref[...], w_ref[...], NN, preferred_element_type=F32)
        raw_ref[...] = raw
        o_ref[...] = _rms(raw, g_ref[...]).astype(BF16)

    return _rows_call(name, body, [a], [w, g], [(n, F32), (n, BF16)])


def _kv_down(hk, w_dkv, w_kr, g, tables):
    def body(a_ref, c_ref, sa_ref, sb_ref, wd_ref, wr_ref, g_ref, raw_ref, ckv_ref, kr_ref):
        av = a_ref[...]
        raw = lax.dot_general(av, wd_ref[...], NN, preferred_element_type=F32)
        raw_ref[...] = raw
        ckv_ref[...] = _rms(raw, g_ref[...]).astype(BF16)
        kr = lax.dot_general(av, wr_ref[...], NT, preferred_element_type=F32)
        kr_ref[...] = _rotate(kr, c_ref[...], sa_ref[...], sb_ref[...], 1.0).astype(BF16)

    return _rows_call("kv_down", body, [hk, *tables], [w_dkv, w_kr, g], [(KV_LORA, F32), (KV_LORA, BF16), (128, BF16)])


def _kv_up(ckv, w_uk, w_uv):
    def body(a_ref, wk_ref, wv_ref, k_ref, v_ref):
        av = a_ref[...]
        k_ref[...] = lax.dot_general(av, wk_ref[...], NN, preferred_element_type=F32).astype(BF16)
        v_ref[...] = lax.dot_general(av, wv_ref[...], NN, preferred_element_type=F32).astype(BF16)

    return _rows_call("kv_up", body, [ckv], [w_uk, w_uv], [(N_HEADS * QK_NOPE, BF16), (N_HEADS * V_HEAD, BF16)])


def _rms_bwd(name, x, gains, dys, dres=None):
    d = x.shape[1]
    n = len(gains)
    has_res = dres is not None

    def body(*refs):
        x_ref, g_refs, dy_refs = refs[0], refs[1:1 + n], refs[1 + n:1 + 2 * n]
        dx_ref, dxb_ref = refs[-2 - n], refs[-1 - n]
        dg_refs = refs[-n:]
        xv = x_ref[...]
        r = lax.rsqrt(jnp.mean(xv * xv, axis=-1, keepdims=True) + EPS)
        xn = xv * r
        dx = refs[1 + 2 * n][...] if has_res else None
        parts = []
        for g_ref, dy_ref in zip(g_refs, dy_refs):
            dyv = dy_ref[...].astype(F32)
            gdy = dyv * g_ref[...]
            t = r * (gdy - xn * jnp.mean(gdy * xn, axis=-1, keepdims=True))
            dx = t if dx is None else dx + t
            parts.append(jnp.sum(dyv * xn, axis=0, keepdims=True))
        dx_ref[...] = dx
        dxb_ref[...] = dx.astype(BF16)

        @pl.when(pl.program_id(0) == 0)
        def _():
            for dg_ref, part in zip(dg_refs, parts):
                dg_ref[...] = part

        @pl.when(pl.program_id(0) > 0)
        def _():
            for dg_ref, part in zip(dg_refs, parts):
                dg_ref[...] += part

    row = pl.BlockSpec((TR, d), lambda i: (i, 0))
    vec = pl.BlockSpec((1, d), lambda i: (0, 0))
    args = [x] + list(gains) + list(dys) + ([dres] if has_res else [])
    in_specs = [row] + [vec] * n + [row] * n + ([row] if has_res else [])
    outs = _pallas(
        body, name=name, grid=(T // TR,), in_specs=in_specs, out_specs=[row, row] + [vec] * n,
        out_shape=[jax.ShapeDtypeStruct((T, d), F32), jax.ShapeDtypeStruct((T, d), BF16)]
        + [jax.ShapeDtypeStruct((1, d), F32)] * n,
        params=_params(("arbitrary",)))(*args)
    return outs[0], outs[1], list(outs[2:])


def _final(h, g, tgt):
    def body(h_ref, g_ref, t_ref, loss_ref, dh_ref, dhb_ref, dg_ref):
        hv = h_ref[...]
        r = lax.rsqrt(jnp.mean(hv * hv, axis=-1, keepdims=True) + EPS)
        xn = hv * r
        gv = g_ref[...]
        err = xn * gv - t_ref[...]
        part_loss = 0.5 * jnp.sum(jnp.mean(err * err, axis=-1, keepdims=True), axis=0, keepdims=True)
        dy = err * (1.0 / D)
        gdy = dy * gv
        dh = r * (gdy - xn * jnp.mean(gdy * xn, axis=-1, keepdims=True))
        dh_ref[...] = dh
        dhb_ref[...] = dh.astype(BF16)
        part = jnp.sum(dy * xn, axis=0, keepdims=True)
        first = pl.program_id(0) == 0

        @pl.when(first)
        def _():
            dg_ref[...] = part
            loss_ref[...] = jnp.broadcast_to(part_loss, (1, 128))

        @pl.when(jnp.logical_not(first))
        def _():
            dg_ref[...] += part
            loss_ref[...] += jnp.broadcast_to(part_loss, (1, 128))

    row = pl.BlockSpec((TR, D), lambda i: (i, 0))
    vec = pl.BlockSpec((1, D), lambda i: (0, 0))
    return _pallas(
        body, name="final_loss", grid=(T // TR,), in_specs=[row, vec, row],
        out_specs=[pl.BlockSpec((1, 128), lambda i: (0, 0)), row, row, vec],
        out_shape=[jax.ShapeDtypeStruct((1, 128), F32), jax.ShapeDtypeStruct((T, D), F32),
                   jax.ShapeDtypeStruct((T, D), BF16), jax.ShapeDtypeStruct((1, D), F32)],
        params=_params(("arbitrary",)))(h, g, tgt)


def _prev_idx(i, rows=TR):
    return jnp.maximum(i * (rows // HALO) - 1, 0)


def _next_idx(i, rows=TR):
    return jnp.minimum((i + 1) * (rows // HALO), T // HALO - 1)


def _causal_taps(ext):
    return pltpu.roll(ext, 2, 0)[HALO:], pltpu.roll(ext, 1, 0)[HALO:], ext[HALO:]


def _anticausal_taps(ext, n):
    rows = ext.shape[0]
    return pltpu.roll(ext, rows - 1, 0)[:n], pltpu.roll(ext, rows - 2, 0)[:n]


MIX_COLS = 512


def _mixer_in(hn, w_in, w):
    nc = D // MIX_COLS

    def body(h_ref, hh_ref, wb_ref, wc_ref, wu_ref, w_ref, b_ref, c_ref, u_ref, y_ref):
        i = pl.program_id(1)
        hv = h_ref[...]
        he = jnp.concatenate([hh_ref[...], hv], axis=0)
        ce = lax.dot_general(he, wc_ref[...], NN, preferred_element_type=F32).astype(BF16)
        ue = lax.dot_general(he, wu_ref[...], NN, preferred_element_type=F32).astype(BF16)
        bv = lax.dot_general(hv, wb_ref[...], NN, preferred_element_type=F32).astype(BF16)
        b_ref[...] = bv
        c_ref[...] = ce[HALO:]
        u_ref[...] = ue[HALO:]
        row = lax.broadcasted_iota(jnp.int32, (HALO + TS, 1), 0)
        cu = jnp.where(jnp.logical_or(i > 0, row >= HALO), ce.astype(F32) * ue.astype(F32), 0.0)
        x2, x1, x0 = _causal_taps(cu)
        wv = w_ref[...]
        cv = (x2 * wv[0:1] + x1 * wv[1:2]) + x0 * wv[2:3]
        y_ref[...] = (bv.astype(F32) * cv).astype(BF16)

    def cols(part):
        return pl.BlockSpec((D, MIX_COLS), lambda j, i: (0, part * nc + j))

    blk = pl.BlockSpec((TS, MIX_COLS), lambda j, i: (i, j))
    out = jax.ShapeDtypeStruct((T, D), BF16)
    return _pallas(
        body, name="l0_in", grid=(nc, T // TS),
        in_specs=[pl.BlockSpec((TS, D), lambda j, i: (i, 0)), pl.BlockSpec((HALO, D), lambda j, i: (_prev_idx(i, TS), 0)),
                  cols(0), cols(1), cols(2), pl.BlockSpec((3, MIX_COLS), lambda j, i: (0, j))],
        out_specs=[blk] * 4, out_shape=[out] * 4,
        params=_params(("parallel", "parallel")))(hn, hn, w_in, w_in, w_in, w)


def _mixer_out_bwd(dh, w_out, zb, zc, zu, w):
    last = T // TR - 1

    def body(dh_ref, dhn_ref, wo_ref, b_ref, bn_ref, c_ref, ch_ref, u_ref, uh_ref, w_ref, dz_ref, dw_ref):
        i = pl.program_id(0)
        dye = lax.dot_general(jnp.concatenate([dh_ref[...], dhn_ref[...]], axis=0), wo_ref[...], NT,
                              preferred_element_type=F32)
        cv_ = c_ref[...].astype(F32)
        uv = u_ref[...].astype(F32)
        cu = cv_ * uv
        cuh = jnp.where(i > 0, ch_ref[...].astype(F32) * uh_ref[...].astype(F32), 0.0)
        x2, x1, x0 = _causal_taps(jnp.concatenate([cuh, cu], axis=0))
        wv = w_ref[...]
        conv = (x2 * wv[0:1] + x1 * wv[1:2]) + x0 * wv[2:3]
        dyv = dye[:TR]
        dz_ref[:, 0:D] = (dyv * conv).astype(BF16)
        dconv = dyv * b_ref[...].astype(F32)
        dconv_n = jnp.where(i < last, dye[TR:] * bn_ref[...].astype(F32), 0.0)
        n1, n2 = _anticausal_taps(jnp.concatenate([dconv, dconv_n], axis=0), TR)
        dcu = (dconv * wv[2:3] + n1 * wv[1:2]) + n2 * wv[0:1]
        dz_ref[:, D:2 * D] = (dcu * uv).astype(BF16)
        dz_ref[:, 2 * D:3 * D] = (dcu * cv_).astype(BF16)
        part = jnp.concatenate([jnp.sum(dconv * x2, axis=0, keepdims=True),
                                jnp.sum(dconv * x1, axis=0, keepdims=True),
                                jnp.sum(dconv * x0, axis=0, keepdims=True)], axis=0)

        @pl.when(i == 0)
        def _():
            dw_ref[...] = part

        @pl.when(i > 0)
        def _():
            dw_ref[...] += part

    main = pl.BlockSpec((TR, D), lambda i: (i, 0))
    prev = pl.BlockSpec((HALO, D), lambda i: (_prev_idx(i), 0))
    nxt = pl.BlockSpec((HALO, D), lambda i: (_next_idx(i), 0))
    wspec = pl.BlockSpec((3, D), lambda i: (0, 0))
    return _pallas(
        body, name="d_l0_out", grid=(T // TR,),
        in_specs=[main, nxt, pl.BlockSpec((D, D), lambda i: (0, 0)), main, nxt, main, prev, main, prev, wspec],
        out_specs=[pl.BlockSpec((TR, 3 * D), lambda i: (i, 0)), wspec],
        out_shape=[jax.ShapeDtypeStruct((T, 3 * D), BF16), jax.ShapeDtypeStruct((3, D), F32)],
        params=_params(("arbitrary",)))(dh, dh, w_out, zb, zb, zc, zc, zu, zu, w)


def _sigmoid(x):
    return 0.5 * jnp.tanh(0.5 * x) + 0.5


def _ffn_up_act(name, hf, w_up, w, b):
    def body(h_ref, hh_ref, wg_ref, wv_ref, w_ref, b_ref, g_ref, v_ref, a_ref):
        i = pl.program_id(1)
        hv = h_ref[...]
        ge = lax.dot_general(jnp.concatenate([hh_ref[...], hv], axis=0), wg_ref[...], NT,
                             preferred_element_type=F32).astype(BF16)
        v = lax.dot_general(hv, wv_ref[...], NT, preferred_element_type=F32).astype(BF16)
        g_ref[...] = ge[HALO:]
        v_ref[...] = v
        ext = ge.astype(F32)
        row = lax.broadcasted_iota(jnp.int32, (HALO + TM, 1), 0)
        ext = jnp.where(jnp.logical_or(i > 0, row >= HALO), ext, 0.0)
        x2, x1, x0 = _causal_taps(ext)
        wv = w_ref[...]
        gc = ((x2 * wv[0:1] + x1 * wv[1:2]) + x0 * wv[2:3]) + b_ref[...]
        a_ref[...] = ((gc * _sigmoid(gc)) * v.astype(F32)).astype(BF16)

    blk = pl.BlockSpec((None, TM, FF_BLK), lambda j, i: (j, i, 0))
    out = jax.ShapeDtypeStruct((N_FF_BLK, T, FF_BLK), BF16)
    return _pallas(
        body, name=name, grid=(N_FF_BLK, T // TM),
        in_specs=[pl.BlockSpec((TM, D), lambda j, i: (i, 0)),
                  pl.BlockSpec((HALO, D), lambda j, i: (_prev_idx(i, TM), 0)),
                  pl.BlockSpec((None, None, FF_BLK, D), lambda j, i: (0, j, 0, 0)),
                  pl.BlockSpec((None, None, FF_BLK, D), lambda j, i: (0, j + N_FF_BLK, 0, 0)),
                  pl.BlockSpec((None, 3, FF_BLK), lambda j, i: (j, 0, 0)),
                  pl.BlockSpec((None, 1, FF_BLK), lambda j, i: (j, 0, 0))],
        out_specs=[blk, blk, blk], out_shape=[out, out, out],
        params=_params(("parallel", "parallel")))(hf, hf, w_up, w_up, w, b)


def _ffn_dact(name, dh, w_down4, g, v, w, b):
    last = T // TS - 1

    def body(dh_ref, dhn_ref, wd_ref, g_ref, gp_ref, gn_ref, v_ref, vn_ref, w_ref, b_ref, dg_ref, dv_ref, dw_ref, db_ref):
        i = pl.program_id(1)
        da = lax.dot_general(jnp.concatenate([dh_ref[...], dhn_ref[...]], axis=0), wd_ref[...], NT,
                             preferred_element_type=F32)
        row = lax.broadcasted_iota(jnp.int32, (TS + HALO, 1), 0)
        da = jnp.where(jnp.logical_or(i < last, row < TS), da, 0.0)
        gp = jnp.where(i > 0, gp_ref[...].astype(F32), 0.0)
        ext = jnp.concatenate([gp, g_ref[...].astype(F32), gn_ref[...].astype(F32)], axis=0)
        x2, x1, x0 = _causal_taps(ext)
        wv = w_ref[...]
        gc = ((x2 * wv[0:1] + x1 * wv[1:2]) + x0 * wv[2:3]) + b_ref[...]
        sg = _sigmoid(gc)
        vv = jnp.concatenate([v_ref[...].astype(F32), vn_ref[...].astype(F32)], axis=0)
        dv_ref[...] = (da[:TS] * (gc[:TS] * sg[:TS])).astype(BF16)
        dgc = (da * vv) * (sg * (1.0 + gc * (1.0 - sg)))
        n1, n2 = _anticausal_taps(dgc, TS)
        d0 = dgc[:TS]
        dg_ref[...] = ((d0 * wv[2:3] + n1 * wv[1:2]) + n2 * wv[0:1]).astype(BF16)
        part_w = jnp.concatenate([jnp.sum(d0 * x2[:TS], axis=0, keepdims=True),
                                  jnp.sum(d0 * x1[:TS], axis=0, keepdims=True),
                                  jnp.sum(d0 * x0[:TS], axis=0, keepdims=True)], axis=0)
        part_b = jnp.sum(d0, axis=0, keepdims=True)

        @pl.when(i == 0)
        def _():
            dw_ref[...] = part_w
            db_ref[...] = part_b

        @pl.when(i > 0)
        def _():
            dw_ref[...] += part_w
            db_ref[...] += part_b

    blk = pl.BlockSpec((None, TS, FF_BLK), lambda j, i: (j, i, 0))
    prev = pl.BlockSpec((None, HALO, FF_BLK), lambda j, i: (j, _prev_idx(i, TS), 0))
    nxt = pl.BlockSpec((None, HALO, FF_BLK), lambda j, i: (j, _next_idx(i, TS), 0))
    wspec = pl.BlockSpec((None, 3, FF_BLK), lambda j, i: (j, 0, 0))
    bspec = pl.BlockSpec((None, 1, FF_BLK), lambda j, i: (j, 0, 0))
    return _pallas(
        body, name=name, grid=(N_FF_BLK, T // TS),
        in_specs=[pl.BlockSpec((TS, D), lambda j, i: (i, 0)),
                  pl.BlockSpec((HALO, D), lambda j, i: (_next_idx(i, TS), 0)),
                  pl.BlockSpec((None, None, FF_BLK, D), lambda j, i: (0, j, 0, 0)),
                  blk, prev, nxt, blk, nxt, wspec, bspec],
        out_specs=[blk, blk, wspec, bspec],
        out_shape=[jax.ShapeDtypeStruct((N_FF_BLK, T, FF_BLK), BF16), jax.ShapeDtypeStruct((N_FF_BLK, T, FF_BLK), BF16),
                   jax.ShapeDtypeStruct((N_FF_BLK, 3, FF_BLK), F32), jax.ShapeDtypeStruct((N_FF_BLK, 1, FF_BLK), F32)],
        params=_params(("parallel", "arbitrary")))(dh, dh, w_down4, g, g, g, v, v, w, b)


def _rope_tables(pos, inv_freq):
    half = QK_ROPE // 2

    def body(p_ref, f_ref, c_ref, sa_ref, sb_ref):
        ang = p_ref[...].astype(F32) * f_ref[...]
        lane = lax.broadcasted_iota(jnp.int32, (T, 128), 1)
        c = jnp.cos(ang)
        s = jnp.sin(ang)
        c_ref[...] = jnp.where(lane < 2 * half, c, 0.0)
        sa_ref[...] = jnp.where(lane < half, -s, 0.0)
        sb_ref[...] = jnp.where(jnp.logical_and(lane >= half, lane < 2 * half), s, 0.0)

    return _pallas(
        body, name="rope_tables", in_specs=[VMEM_SPEC] * 2, out_specs=[VMEM_SPEC] * 3,
        out_shape=[jax.ShapeDtypeStruct((T, 128), F32)] * 3,
        params=pltpu.CompilerParams(vmem_limit_bytes=VMEM_LIMIT))(pos, inv_freq)


def _rotate(r, c, sa, sb, sign):
    return r * c + sign * (pltpu.roll(r, 96, 1) * sa + pltpu.roll(r, 32, 1) * sb)


def _q_up(cq, w_uq, tables):
    cos, sa, sb = tables

    def body(a_ref, b_ref, c_ref, sa_ref, sb_ref, o_ref):
        for h in range(N_HEADS):
            r = lax.dot_general(a_ref[...], b_ref[h], NT, preferred_element_type=F32)
            o_ref[h, :, :QK_NOPE] = r[:, :QK_NOPE].astype(BF16)
            o_ref[h, :, QK_NOPE:] = _rotate(r[:, QK_NOPE:], c_ref[...], sa_ref[...], sb_ref[...], 1.0).astype(BF16)

    tab = pl.BlockSpec((TS, 128), lambda i: (i, 0))
    return _pallas(
        body, name="q_up", grid=(T // TS,),
        in_specs=[pl.BlockSpec((TS, Q_LORA), lambda i: (i, 0)),
                  pl.BlockSpec((N_HEADS, QK_PAD, Q_LORA), lambda i: (0, 0, 0)), tab, tab, tab],
        out_specs=pl.BlockSpec((N_HEADS, TS, QK_PAD), lambda i: (0, i, 0)),
        out_shape=jax.ShapeDtypeStruct((N_HEADS, T, QK_PAD), BF16),
        params=_params(("parallel",)))(cq, w_uq, cos, sa, sb)


def _rope(name, x, tables, sign, out_dtype, reduce_groups=False):
    g, _, w = x.shape
    cos, sa, sb = tables

    def body(x_ref, c_ref, sa_ref, sb_ref, o_ref):
        xv = x_ref[...].astype(F32)
        if reduce_groups:
            acc = xv[0]
            for k in range(1, g):
                acc = acc + xv[k]
            xv = acc
        out = _rotate(xv[:, w - 128:], c_ref[...], sa_ref[...], sb_ref[...], sign)
        if w > 128:
            o_ref[:, :w - 128] = xv[:, :w - 128].astype(out_dtype)
        o_ref[:, w - 128:] = out.astype(out_dtype)

    tab = pl.BlockSpec((TM, 128), lambda h, i: (i, 0))
    if reduce_groups:
        x_spec = pl.BlockSpec((g, TM, w), lambda h, i: (0, i, 0))
        groups = 1
    else:
        x_spec = pl.BlockSpec((None, TM, w), lambda h, i: (h, i, 0))
        groups = g
    return _pallas(
        body, name=name, grid=(groups, T // TM), in_specs=[x_spec, tab, tab, tab],
        out_specs=pl.BlockSpec((None, TM, w), lambda h, i: (h, i, 0)),
        out_shape=jax.ShapeDtypeStruct((groups, T, w), out_dtype),
        params=_params(("parallel", "parallel")))(x, cos, sa, sb)


SCALE = (QK_NOPE + QK_ROPE) ** -0.5
LOG2E = 1.4426950408889634
SCALE2 = SCALE * LOG2E


def _diag_mask(transposed):
    shift = CHUNK.bit_length() - 1
    a = lax.broadcasted_iota(jnp.int32, (TQ, TQ), 0) >> shift
    b = lax.broadcasted_iota(jnp.int32, (TQ, TQ), 1) >> shift
    return (a <= b) if transposed else (b <= a)


def _as_row(col):
    return jnp.transpose(jnp.broadcast_to(col, (col.shape[0], 128)), (1, 0))[0:1]


def _keys(kn_ref, kr_ref, off):
    return jnp.concatenate([kn_ref[pl.ds(off, TQ), :], kr_ref[pl.ds(off, TQ), :]], axis=1)


def _attn_fwd(q, kn, kr, v):
    hp = 2

    def body(q_ref, kn_ref, kr_ref, v_ref, o_ref, lse_ref):
        i = pl.program_id(1)
        qs = [q_ref[a] for a in range(hp)]

        def step(j, carry, masked):
            off = pl.multiple_of(j * TQ, TQ)
            krv = kr_ref[pl.ds(off, TQ), :]
            ss = []
            for a in range(hp):
                kk = jnp.concatenate([kn_ref[pl.ds(off, TQ), a * QK_NOPE:(a + 1) * QK_NOPE], krv], axis=1)
                ss.append(lax.dot_general(qs[a], kk, NT, preferred_element_type=F32))
            out = []
            for a in range(hp):
                m, l, acc = carry[a]
                s = ss[a] * SCALE2
                if masked:
                    s = jnp.where(_diag_mask(False), s, NEG_INF)
                m_new = jnp.maximum(m, jnp.max(s, axis=-1, keepdims=True))
                p = jnp.exp2(s - m_new)
                alpha = jnp.exp2(m - m_new)
                l = alpha * l + jnp.sum(p, axis=-1, keepdims=True)
                pv = lax.dot_general(p.astype(BF16), v_ref[pl.ds(off, TQ), a * V_HEAD:(a + 1) * V_HEAD], NN,
                                     preferred_element_type=F32)
                out.append((m_new, l, alpha * acc + pv))
            return tuple(out)

        one = (jnp.full((TQ, 1), NEG_INF, F32), jnp.zeros((TQ, 1), F32), jnp.zeros((TQ, V_HEAD), F32))
        carry = lax.fori_loop(0, i, lambda j, cr: step(j, cr, False), (one,) * hp)
        carry = step(i, carry, True)
        for a, (m, l, acc) in enumerate(carry):
            o_ref[:, a * V_HEAD:(a + 1) * V_HEAD] = (acc / l).astype(BF16)
            lse_ref[a] = _as_row(m + jnp.log(l) * LOG2E)

    return _pallas(
        body, name="attn_fwd", grid=(N_HEADS // hp, T // TQ),
        in_specs=[pl.BlockSpec((hp, TQ, QK_PAD), lambda h, i: (h, i, 0)),
                  pl.BlockSpec((T, hp * QK_NOPE), lambda h, i: (0, h)),
                  pl.BlockSpec((T, 128), lambda h, i: (0, 0)),
                  pl.BlockSpec((T, hp * V_HEAD), lambda h, i: (0, h))],
        out_specs=[pl.BlockSpec((TQ, hp * V_HEAD), lambda h, i: (i, h)), pl.BlockSpec((hp, 1, TQ), lambda h, i: (h, 0, i))],
        out_shape=[jax.ShapeDtypeStruct((T, N_HEADS * V_HEAD), BF16), jax.ShapeDtypeStruct((N_HEADS, 1, T), F32)],
        params=_params(("parallel", "parallel")))(q, kn, kr, v)


def _attn_bwd(q, kn, kr, v, o, do, lse_row, tables):
    nq = T // TQ
    hp = 2
    cos, sa, sb = tables

    def body(q_ref, kn_ref, kr_ref, v_ref, o_ref, do_ref, lse_ref, c_ref, sa_ref, sb_ref,
             dq_ref, dkn_ref, dkr_ref, dv_ref, dq_acc, dl_ref):
        j = pl.program_id(1)

        def cols(a):
            return slice(a * 128, (a + 1) * 128)

        @pl.when(j == 0)
        def _():
            dq_acc[...] = jnp.zeros_like(dq_acc)
            for a in range(hp):
                for i in range(nq):
                    rows = pl.ds(i * TQ, TQ)
                    prod = do_ref[rows, cols(a)].astype(F32) * o_ref[rows, cols(a)].astype(F32)
                    dl_ref[a, :, rows] = _as_row(jnp.sum(prod, axis=-1, keepdims=True))

        krv = kr_ref[...]
        kks = [jnp.concatenate([kn_ref[:, cols(a)], krv], axis=1) for a in range(hp)]
        vvs = [v_ref[:, cols(a)] for a in range(hp)]

        def step(i, carry, masked):
            off = pl.multiple_of(i * TQ, TQ)
            rows = pl.ds(off, TQ)
            qis = [q_ref[a, rows, :] for a in range(hp)]
            dois = [do_ref[rows, cols(a)] for a in range(hp)]
            sts = [lax.dot_general(kks[a], qis[a], NT, preferred_element_type=F32) for a in range(hp)]
            dpts = [lax.dot_general(vvs[a], dois[a], NT, preferred_element_type=F32) for a in range(hp)]
            out = []
            for a in range(hp):
                dk, dv = carry[a]
                st = sts[a] * SCALE2
                if masked:
                    st = jnp.where(_diag_mask(True), st, NEG_INF)
                pt = jnp.exp2(st - lse_ref[a, :, rows])
                dv = dv + lax.dot_general(pt.astype(BF16), dois[a], NN, preferred_element_type=F32)
                dst = ((pt * (dpts[a] - dl_ref[a, :, rows])) * SCALE).astype(BF16)
                dk = dk + lax.dot_general(dst, qis[a], NN, preferred_element_type=F32)
                dq_acc[a, rows, :] += lax.dot_general(dst, kks[a], TN, preferred_element_type=F32)
                out.append((dk, dv))
            return tuple(out)

        zero = (jnp.zeros((TQ, QK_PAD), F32), jnp.zeros((TQ, V_HEAD), F32))
        carry = step(j, (zero,) * hp, True)
        carry = lax.fori_loop(j + 1, nq, lambda i, cr: step(i, cr, False), carry)
        for a, (dk, dv) in enumerate(carry):
            dkn_ref[:, cols(a)] = dk[:, :QK_NOPE].astype(BF16)
            dkr_ref[a] = dk[:, QK_NOPE:]
            dv_ref[:, cols(a)] = dv.astype(BF16)

        @pl.when(j == nq - 1)
        def _():
            for a in range(hp):
                dq = dq_acc[a]
                dq_ref[a, :, :QK_NOPE] = dq[:, :QK_NOPE].astype(BF16)
                dq_ref[a, :, QK_NOPE:] = _rotate(dq[:, QK_NOPE:], c_ref[...], sa_ref[...], sb_ref[...], -1.0).astype(BF16)

    row = pl.BlockSpec((hp, 1, T), lambda h, j: (h, 0, 0))
    head = pl.BlockSpec((TQ, hp * 128), lambda h, j: (j, h))
    whole = pl.BlockSpec((hp, T, QK_PAD), lambda h, j: (h, 0, 0))
    tab = pl.BlockSpec((T, 128), lambda h, j: (0, 0))
    heads = pl.BlockSpec((T, hp * V_HEAD), lambda h, j: (0, h))
    return _pallas(
        body, name="attn_bwd", grid=(N_HEADS // hp, nq),
        in_specs=[whole, head, pl.BlockSpec((TQ, 128), lambda h, j: (j, 0)), head, heads, heads, row, tab, tab, tab],
        out_specs=[whole, head, pl.BlockSpec((hp, TQ, 128), lambda h, j: (h, j, 0)), head],
        out_shape=[jax.ShapeDtypeStruct((N_HEADS, T, QK_PAD), BF16), jax.ShapeDtypeStruct((T, N_HEADS * QK_NOPE), BF16),
                   jax.ShapeDtypeStruct((N_HEADS, T, 128), F32), jax.ShapeDtypeStruct((T, N_HEADS * V_HEAD), BF16)],
        scratch_shapes=[pltpu.VMEM((hp, T, QK_PAD), F32), pltpu.VMEM((hp, 1, T), F32)],
        params=_params(("parallel", "arbitrary")))(q, kn, kr, v, o, do, lse_row, cos, sa, sb)


def _ffn_gup(name, dg, dv, hf):
    def body(dg_ref, dv_ref, hf_ref, o_ref):
        j = pl.program_id(0)

        @pl.when(j < N_FF_BLK)
        def _():
            o_ref[...] = lax.dot_general(dg_ref[...], hf_ref[...], TN, preferred_element_type=F32).astype(BF16)

        @pl.when(j >= N_FF_BLK)
        def _():
            o_ref[...] = lax.dot_general(dv_ref[...], hf_ref[...], TN, preferred_element_type=F32).astype(BF16)

    return _pallas(
        body, name=name, grid=(N_DEV,),
        in_specs=[pl.BlockSpec((None, T, FF_BLK), lambda j: (jnp.minimum(j, N_FF_BLK - 1), 0, 0)),
                  pl.BlockSpec((None, T, FF_BLK), lambda j: (jnp.maximum(j - N_FF_BLK, 0), 0, 0)),
                  pl.BlockSpec((T, D), lambda j: (0, 0))],
        out_specs=pl.BlockSpec((None, FF_BLK, D), lambda j: (j, 0, 0)),
        out_shape=jax.ShapeDtypeStruct((N_DEV, FF_BLK, D), BF16), params=_params(("parallel",)))(dg, dv, hf)


def _ffn_layer_fwd(tag, h, gain, ex):
    hf = _rms_fwd(f"{tag}_norm", h, gain)
    g, v, act = _ffn_up_act(f"{tag}_up", hf, ex.need(f"ffn_w_up{tag[1]}", hf), ex.need(f"ffn_cw{tag[1]}", hf),
                            ex.need(f"ffn_cb{tag[1]}", hf))
    ex.at(f"{tag}_up", act)
    rows = pl.BlockSpec((TS, D), lambda i: (i, 0))
    out = _mm_sum(f"{tag}_down",
                  [(act, pl.BlockSpec((N_FF_BLK, TS, FF_BLK), lambda i: (0, i, 0)), ex.need(f"ffn_w_down{tag[1]}", act),
                    pl.BlockSpec((None, N_FF_BLK, FF_BLK, D), lambda i: (0, 0, 0, 0)), NN, 0)],
                  grid=(T // TS,), o_spec=rows, o_shape=(T, D), o_dtype=F32, add=h)
    ex.at(f"{tag}_down", out)
    return out, (hf, g, v, act)


def _ffn_layer_bwd(tag, h, gain, ex, saved, dh, dh_bf):
    hf, g, v, act = saved
    layer = tag[1]
    w_up, w_down4 = ex.need(f"ffn_w_up{layer}", dh_bf), ex.need(f"ffn_w_down{layer}", dh_bf)
    dg, dv, dcw, dcb = _ffn_dact(f"{tag}_dact", dh_bf, w_down4, g, v, ex.need(f"ffn_cw{layer}", dh_bf),
                                 ex.need(f"ffn_cb{layer}", dh_bf))
    ex.at(f"{tag}_dact", dg)
    g_down = _mm(f"{tag}_gdown", act, dh_bf, grid=(N_FF_BLK,),
                 a_spec=pl.BlockSpec((None, T, FF_BLK), lambda j: (j, 0, 0)),
                 b_spec=pl.BlockSpec((T, D), lambda j: (0, 0)),
                 o_spec=pl.BlockSpec((FF_BLK, D), lambda j: (j, 0)),
                 o_shape=(D_FF, D), o_dtype=BF16, dims=TN)
    g_up = _ffn_gup(f"{tag}_gup", dg, dv, hf)
    ex.grad("ffn_w_up", int(layer), g_up.reshape(1, N_DEV, FF_BLK, D))
    ex.grad("ffn_w_down", int(layer), g_down.reshape(1, N_DEV, D_FF // N_DEV, D))
    ex.at(f"{tag}_gup", g_up)
    part = pl.BlockSpec((N_FF_BLK, TR, FF_BLK), lambda i: (0, i, 0))
    dh_in, dh_in_bf, dgain = _mm_sum(
        f"{tag}_dhf",
        [(dg, part, w_up, pl.BlockSpec((None, N_FF_BLK, FF_BLK, D), lambda i: (0, 0, 0, 0)), NN, 0),
         (dv, part, w_up, pl.BlockSpec((None, N_FF_BLK, FF_BLK, D), lambda i: (0, 1, 0, 0)), NN, 0)],
        grid=(T // TR,), o_spec=pl.BlockSpec((TR, D), lambda i: (i, 0)), o_shape=(T, D), o_dtype=F32,
        norm_bwd=(h, [gain], dh))
    ex.at(f"{tag}_dhf", dh_in)
    return dh_in, dh_in_bf, dgain[0], dcw, dcb


def _local_step(x, pos, tgt, rep, ex):
    attn_norm, ffn_norm, final_norm = rep["attn_norm"], rep["ffn_norm"], rep["final_norm"]
    half = QK_ROPE // 2
    inv = 1.0 / (ROPE_THETA ** (jnp.arange(half, dtype=F32) / half))
    inv_freq = jnp.concatenate([inv, inv, jnp.zeros((128 - 2 * half,), F32)]).reshape(1, 128)
    tables = _rope_tables(pos, inv_freq)

    hn0 = _rms_fwd("l0_norm", x, attn_norm[0:1])
    w_in = ex.need("sc_w_in", hn0)
    ex.at("mixer_ready", hn0)
    zb, zc, zu, y = _mixer_in(hn0, w_in, ex.need("sc_conv_w", hn0))
    ex.at("l0_in", y)
    h1 = _mm_rows("l0_out", y, ex.need("sc_w_out", y), NN, F32, D, tn=512, add=x)
    ex.at("l0_out", h1)
    h2, ffn0 = _ffn_layer_fwd("f0", h1, ffn_norm[0:1], ex)

    hk, hn1 = _rms_fwd2("h2_norms", h2, rep["kv_in_norm"], attn_norm[1:2])
    ckv_raw, ckv, kr = _kv_down(hk, ex.need("w_dkv", hk), ex.need("w_kr", hk), rep["kv_latent_norm"], tables)
    kn, vv = _kv_up(ckv, ex.need("w_uk", ckv), ex.need("w_uv", ckv))

    cq_raw, cq = _down_norm("q_down", hn1, ex.need("w_dq", hn1), rep["q_latent_norm"])
    w_uq = ex.need("w_uq", cq)
    q = _q_up(cq, w_uq, tables)
    o, lse = _attn_fwd(q, kn, kr, vv)
    ex.at("attn_fwd", o)
    w_o = ex.need("w_o", o)
    h3 = _mm_rows("attn_out", o, w_o, NN, F32, D, tn=512, add=h2)
    h4, ffn1 = _ffn_layer_fwd("f1", h3, ffn_norm[1:2], ex)

    loss, dh4, dh4_bf, d_final = _final(h4, final_norm.reshape(1, D), tgt)

    dh3, dh3_bf, d_fn1, dcw1, dcb1 = _ffn_layer_bwd("f1", h3, ffn_norm[1:2], ex, ffn1, dh4, dh4_bf)
    ex.at("f1_bwd", dh3)

    do = _mm_rows("d_attn_out", dh3_bf, w_o, NT, BF16, N_HEADS * V_HEAD)
    ex.grad("w_o", None, _mm_wgrad("g_w_o", o, dh3_bf).reshape(1, N_DEV, D // N_DEV, D))
    dq_pre, dkn, dkr, dvv = _attn_bwd(q, kn, kr, vv, o, do, lse, tables)
    def rows_of(a):
        return a[None], pl.BlockSpec((1, TS, a.shape[1]), lambda i: (0, i, 0))

    def whole(wt):
        return wt[None], pl.BlockSpec((1,) + wt.shape, lambda i: (0, 0, 0))

    def row_blocks(d):
        return dict(grid=(T // TS,), o_spec=pl.BlockSpec((TS, d), lambda i: (i, 0)), o_shape=(T, d), o_dtype=F32)

    _, dcq_raw_bf, (d_qln,) = _mm_sum(
        "d_q_up", [(dq_pre, pl.BlockSpec((N_HEADS, TS, QK_PAD), lambda i: (0, i, 0)),
                    w_uq, pl.BlockSpec((N_HEADS, QK_PAD, Q_LORA), lambda i: (0, 0, 0)), NN, 0)],
        norm_bwd=(cq_raw, [rep["q_latent_norm"]], None), **row_blocks(Q_LORA))
    g_uq = _mm("g_w_uq", dq_pre, cq, grid=(N_HEADS,),
               a_spec=pl.BlockSpec((None, T, QK_PAD), lambda h: (h, 0, 0)),
               b_spec=pl.BlockSpec((T, Q_LORA), lambda h: (0, 0)),
               o_spec=pl.BlockSpec((None, QK_PAD, Q_LORA), lambda h: (h, 0, 0)),
               o_shape=(N_HEADS, QK_PAD, Q_LORA), o_dtype=BF16, dims=TN)
    ex.grad("w_uq", None, g_uq[:, :QK_NOPE + QK_ROPE].reshape(1, N_DEV, QK_NOPE + QK_ROPE, Q_LORA))
    ex.grad("w_dq", None, _mm_wgrad("g_w_dq", hn1, dcq_raw_bf).reshape(1, N_DEV, D // N_DEV, Q_LORA))

    _, dckv_raw_bf, (d_kvln,) = _mm_sum(
        "d_kv_up", [(*rows_of(dkn), *whole(ex.need("w_uk", dkn)), NT, 0),
                    (*rows_of(dvv), *whole(ex.need("w_uv", dvv)), NT, 0)],
        norm_bwd=(ckv_raw, [rep["kv_latent_norm"]], None), **row_blocks(KV_LORA))
    ex.grad("w_uk", None, _mm_wgrad("g_w_uk", ckv, dkn))
    ex.grad("w_uv", None, _mm_wgrad("g_w_uv", ckv, dvv))
    dkr_raw_bf = _rope("dk_rope", dkr, tables, -1.0, BF16, reduce_groups=True).reshape(T, 128)
    ex.grad("w_dkv", None, _mm_wgrad("g_w_dkv", hk, dckv_raw_bf).reshape(1, N_DEV, D // N_DEV, KV_LORA))
    ex.grad("w_kr", None, _mm_wgrad("g_w_kr", dkr_raw_bf, hk)[:QK_ROPE])

    dh2, dh2_bf, (d_an1, d_kvin) = _mm_sum(
        "d_h2", [(*rows_of(dcq_raw_bf), *whole(ex.need("w_dq", dcq_raw_bf)), NT, 0),
                 (*rows_of(dckv_raw_bf), *whole(ex.need("w_dkv", dckv_raw_bf)), NT, 1),
                 (*rows_of(dkr_raw_bf), *whole(ex.need("w_kr", dkr_raw_bf)), NN, 1)],
        norm_bwd=(h2, [attn_norm[1:2], rep["kv_in_norm"]], dh3), **row_blocks(D))
    ex.at("kv_bwd", dh2)

    dh1, dh1_bf, d_fn0, dcw0, dcb0 = _ffn_layer_bwd("f0", h1, ffn_norm[0:1], ex, ffn0, dh2, dh2_bf)
    ex.at("f0_bwd", dh1)

    ex.grad("sc_w_out", None, _mm_wgrad("g_sc_w_out", y, dh1_bf).reshape(1, N_DEV, D // N_DEV, D))
    dz, d_scw = _mixer_out_bwd(dh1_bf, ex.need("sc_w_out", dh1_bf), zb, zc, zu, ex.need("sc_conv_w", dh1_bf))
    g_in = _mm_wgrad("g_sc_w_in", hn0, dz)
    ex.grad("sc_w_in", None, g_in)
    ex.at("sc_bwd", g_in)
    ex.at("d_l0_in", g_in)
    grad_x, _, (d_an0,) = _mm_sum(
        "d_l0_in", [(*rows_of(dz), *whole(ex.need("sc_w_in", dz)), NT, 0)],
        norm_bwd=(x, [attn_norm[0:1]], dh1), **row_blocks(D))

    small = {
        "attn_norm": jnp.concatenate([d_an0, d_an1], axis=0),
        "ffn_norm": jnp.concatenate([d_fn0, d_fn1], axis=0),
        "final_norm": d_final.reshape(D),
        "kv_in_norm": d_kvin.reshape(D),
        "kv_latent_norm": d_kvln.reshape(KV_LORA),
        "q_latent_norm": d_qln,
        "ffn_conv_b": jnp.stack([dcb0, dcb1]).transpose(0, 2, 1, 3).reshape(2, D_FF),
        "sc_conv_w": d_scw,
        "ffn_conv_w": jnp.stack([dcw0, dcw1]).transpose(0, 2, 1, 3).reshape(2, 3, D_FF),
    }
    return loss, grad_x, small


def _place():
    return lax.axis_index("x"), lax.axis_index("y"), lax.axis_index("c")


def _peers():
    x, y, c = _place()
    return (x, y, 1 - c), [(1 - x, y), (x, 1 - y), (1 - x, 1 - y)]


def _window(ref, kind, dev):
    if kind == "blocked":
        return ref.at[:, dev]
    width = ref.shape[-1] // N_DEV
    return ref.at[:, pl.ds(pl.multiple_of(dev * width, 128), width)]


HBM_SPEC = pl.BlockSpec(memory_space=pltpu.HBM)
SEM_SPEC = pl.BlockSpec(memory_space=pltpu.SEMAPHORE)
EFFECT = pltpu.SideEffectType.DATAFLOW_SIDE_EFFECTING
TOKEN = jax.ShapeDtypeStruct((8, 128), F32)


def _hbm(a):
    return pltpu.with_memory_space_constraint(a, pltpu.HBM)


def _copies_start(name, jobs):
    nj = len(jobs)
    counts = [(len(srcs), len(lands)) for srcs, lands, _, _ in jobs]
    n_arr = sum(ns + nl for ns, nl in counts)

    def body(*refs):
        sems, token = refs[n_arr:n_arr + 2 * nj], refs[-1]
        at = 0
        for j, ((ns, nl), (_, _, ncopy, plan)) in enumerate(zip(counts, jobs)):
            copies = plan(refs[at:at + ns], refs[at + ns:at + ns + nl])
            assert len(copies) == ncopy
            for k, (sent, dst, to, _) in enumerate(copies):
                pltpu.make_async_remote_copy(src_ref=sent, dst_ref=dst, send_sem=sems[2 * j].at[k],
                                             recv_sem=sems[2 * j + 1].at[k], device_id=to, device_id_type=MESH).start()
            at += ns + nl
        token[...] = jnp.zeros_like(token)

    arrays = [a for srcs, lands, _, _ in jobs for a in list(srcs) + list(lands)]
    sem_shapes = [pltpu.SemaphoreType.DMA((ncopy,)) for _, _, ncopy, _ in jobs for _ in range(2)]
    outs = pl.pallas_call(
        body, name=name, in_specs=[HBM_SPEC] * n_arr,
        out_specs=[SEM_SPEC] * (2 * nj) + [HBM_SPEC] * n_arr + [VMEM_SPEC],
        out_shape=sem_shapes + [pltpu.HBM(a.shape, a.dtype) for a in arrays] + [TOKEN],
        input_output_aliases={i: 2 * nj + i for i in range(n_arr)},
        compiler_params=pltpu.CompilerParams(has_side_effects=EFFECT))(*[_hbm(a) for a in arrays])
    _Chain.last = outs[-1]
    flights, at = [], 2 * nj
    for j, (ns, nl) in enumerate(counts):
        flights.append((outs[2 * j], outs[2 * j + 1], list(outs[at:at + ns]), list(outs[at + ns:at + ns + nl])))
        at += ns + nl
    return flights


def _copies_wait(name, started, ncopy, plan):
    send, recv, srcs, lands = started
    ns, nl = len(srcs), len(lands)

    def body(*refs):
        send_ref, recv_ref, token = refs[ns + nl], refs[ns + nl + 1], refs[-1]
        copies = plan(refs[:ns], refs[ns:ns + nl])
        assert len(copies) == ncopy
        for k, (sent, _, to, landed) in enumerate(copies):
            cp = pltpu.make_async_remote_copy(src_ref=sent, dst_ref=landed, send_sem=send_ref.at[k],
                                              recv_sem=recv_ref.at[k], device_id=to, device_id_type=MESH)
            cp.wait_send()
            cp.wait_recv()
        token[...] = jnp.zeros_like(token)

    arrays = list(srcs) + list(lands)
    outs = pl.pallas_call(
        body, name=name, in_specs=[HBM_SPEC] * (ns + nl) + [SEM_SPEC] * 2 + [ANY_SPEC],
        out_specs=[HBM_SPEC] * (ns + nl) + [VMEM_SPEC], out_shape=[pltpu.HBM(a.shape, a.dtype) for a in arrays] + [TOKEN],
        input_output_aliases={i: i for i in range(ns + nl)},
        compiler_params=pltpu.CompilerParams(has_side_effects=EFFECT))(*arrays, send, recv, _Chain.last)
    _Chain.last = outs[-1]
    return list(outs[:ns]), list(outs[ns:-1])


def _plan_gather_chips(kinds):
    def plan(srcs, lands):
        x, y, c = _place()
        sibling, chips = _peers()
        out = []
        for t, kind in enumerate(kinds):
            mine = _window(lands[t], kind, 4 * x + 2 * y + c)
            out.append((srcs[t], mine, (x, y, c), mine))
            out.append((srcs[t], mine, sibling, _window(lands[t], kind, 4 * x + 2 * y + 1 - c)))
            for px, py in chips:
                out.append((srcs[t], mine, (px, py, c), _window(lands[t], kind, 4 * px + 2 * py + c)))
        return out
    return plan, 5 * len(kinds)


def _plan_gather_all(n):
    def plan(srcs, lands):
        x, y, c = _place()
        out = []
        for t in range(n):
            mine = lands[t].at[:, 4 * x + 2 * y + c]
            for m in range(N_DEV):
                px, py, pc = (1 - x if m & 4 else x), (1 - y if m & 2 else y), (1 - c if m & 1 else c)
                out.append((srcs[t], mine, (px, py, pc), lands[t].at[:, 4 * px + 2 * py + pc]))
        return out
    return plan, N_DEV * n


def _plan_gather_sibling(kinds):
    def plan(srcs, lands):
        _, _, c = _place()
        sibling, chips = _peers()
        out = []
        for t, kind in enumerate(kinds):
            for px, py in chips:
                w = _window(lands[t], kind, 4 * px + 2 * py + c)
                out.append((w, w, sibling, _window(lands[t], kind, 4 * px + 2 * py + 1 - c)))
        return out
    return plan, 3 * len(kinds)


def _plan_scatter_sibling(kinds):
    def plan(srcs, lands):
        _, _, c = _place()
        sibling, _ = _peers()
        out = []
        for t, kind in enumerate(kinds):
            for k in range(N_CHIP):
                out.append((_window(srcs[t], kind, 2 * k + 1 - c), lands[t].at[k], sibling, lands[t].at[k]))
        return out
    return plan, N_CHIP * len(kinds)


def _plan_scatter_chips(n):
    def plan(srcs, lands):
        x, y, c = _place()
        _, chips = _peers()
        out = []
        for t in range(n):
            for px, py in chips:
                out.append((srcs[t].at[2 * px + py], lands[t].at[2 * x + y], (px, py, c), lands[t].at[2 * px + py]))
        return out
    return plan, 3 * n


def _landing(shard, kind):
    if kind == "blocked":
        return lax.empty((shard.shape[0], N_DEV) + shard.shape[1:], shard.dtype)
    return lax.empty((shard.shape[0], N_DEV * shard.shape[1]), shard.dtype)


def _chip_sums(name, grads, kinds, recvs, c):
    n = len(grads)
    in_specs, out_specs, out_shape, args = [], [], [], []
    for gr, kind, rv in zip(grads, kinds, recvs):
        if kind == "blocked":
            rows, w = gr.shape[2], gr.shape[3]
            in_specs.append(pl.BlockSpec((None, None, rows, w), lambda k, cref: (0, 2 * k + cref[0], 0, 0)))
        else:
            rows, w = gr.shape[0], gr.shape[1] // N_DEV
            in_specs.append(pl.BlockSpec((rows, w), lambda k, cref: (0, 2 * k + cref[0])))
        blk = pl.BlockSpec((None, rows, w), lambda k, cref: (k, 0, 0))
        in_specs.append(blk)
        out_specs.append(blk)
        out_shape.append(jax.ShapeDtypeStruct((N_CHIP, rows, w), BF16))
        args += [gr, rv.reshape(N_CHIP, rows, w)]

    def body(*refs):
        for t in range(n):
            g_ref, r_ref, o_ref = refs[1 + 2 * t], refs[2 + 2 * t], refs[1 + 2 * n + t]
            o_ref[...] = (g_ref[...].astype(F32) + r_ref[...].astype(F32)).astype(BF16)

    return _pallas(body, name=name, n_prefetch=1, grid=(N_CHIP,), in_specs=in_specs, out_specs=out_specs,
                   out_shape=out_shape, params=_params(("parallel",)))(c, *args)


def _adamw_math(g, wv, mv, vv):
    m = ADAM_B1 * mv + (1.0 - ADAM_B1) * g
    v = ADAM_B2 * vv + (1.0 - ADAM_B2) * (g * g)
    m_hat = m / (1.0 - ADAM_B1 ** ADAM_STEP)
    v_hat = v / (1.0 - ADAM_B2 ** ADAM_STEP)
    delta = -ADAM_LR * (m_hat / (jnp.sqrt(v_hat) + ADAM_EPS) + ADAM_WD * wv)
    return delta, m, v


ADAM_STEPS = 2


def _adamw_group(name, items, chip_ids):
    n = len(items)
    in_specs, out_specs, out_shape, args, prevs = [], [], [], [chip_ids], []
    for own, recv, w3, m3, v3, layer, _ in items:
        nl, rows, w = w3.shape
        tr = rows // ADAM_STEPS
        assert tr % 16 == 0, (name, rows)
        in_specs += [pl.BlockSpec((None, tr, w), lambda i, ids, slot=slot: (ids[slot], i, 0)) for slot in range(4)]
        slab = pl.BlockSpec((None, tr, w), lambda i, ids, layer=layer: (layer, i, 0))
        in_specs += [slab] * 3
        out_specs += [slab] * 4
        out_shape += [jax.ShapeDtypeStruct((nl, rows, w), F32)] * 4
        args += [own, recv, recv, recv, w3, m3, v3]
    aliases = {}
    for t, item in enumerate(items):
        if item[6] is not None:
            for k in range(4):
                aliases[len(args) + k] = 4 * t + k
            in_specs += [ANY_SPEC] * 4
            args += list(item[6])
            prevs.append(t)
    n_in = 1 + 7 * n + 4 * len(prevs)

    def body(*refs):
        for t in range(n):
            own_ref, r1_ref, r2_ref, r3_ref, w_ref, m_ref, v_ref = refs[1 + 7 * t:8 + 7 * t]
            g_ref, d_ref, nm_ref, nv_ref = refs[n_in + 4 * t:n_in + 4 * t + 4]
            g = ((own_ref[...].astype(F32) + r1_ref[...].astype(F32)) + r2_ref[...].astype(F32)) + r3_ref[...].astype(F32)
            g_ref[...] = g
            d_ref[...], nm_ref[...], nv_ref[...] = _adamw_math(g, w_ref[...], m_ref[...], v_ref[...])

    outs = _pallas(body, name=name, n_prefetch=1, grid=(ADAM_STEPS,), in_specs=in_specs, out_specs=out_specs,
                   out_shape=out_shape, aliases=aliases, params=_params(("parallel",)))(*args)
    return [list(outs[4 * t:4 * t + 4]) for t in range(n)]


def _adamw_small(gathered, ws, ms, vs):
    n = len(gathered)
    full = [w is not None for w in ws]
    args = list(gathered)
    out_shape = []
    for t in range(n):
        shape = jax.ShapeDtypeStruct(gathered[t].shape[2:], F32)
        if full[t]:
            args += [ws[t], ms[t], vs[t]]
            out_shape += [shape] * 4
        else:
            out_shape += [shape]

    def body(*refs):
        i_in, i_out = n, len(args)
        for t in range(n):
            p_ref = refs[t]
            g = p_ref[0, 0]
            for k in range(1, N_DEV):
                g = g + p_ref[0, k]
            refs[i_out][...] = g
            if full[t]:
                w_ref, m_ref, v_ref = refs[i_in:i_in + 3]
                refs[i_out + 1][...], refs[i_out + 2][...], refs[i_out + 3][...] = _adamw_math(
                    g, w_ref[...], m_ref[...], v_ref[...])
                i_in += 3
                i_out += 4
            else:
                i_out += 1

    outs = _pallas(body, name="adamw_small", in_specs=[VMEM_SPEC] * len(args), out_specs=[VMEM_SPEC] * len(out_shape),
                   out_shape=out_shape, params=pltpu.CompilerParams(vmem_limit_bytes=VMEM_LIMIT))(*args)
    result, i = [], 0
    for t in range(n):
        k = 4 if full[t] else 1
        result.append(list(outs[i:i + k]))
        i += k
    return result


def _adamw_plain(name, gs, ws, ms, vs):
    n = len(gs)

    def body(*refs):
        for t in range(n):
            g_ref, w_ref, m_ref, v_ref = refs[4 * t:4 * t + 4]
            outs = refs[4 * n + 3 * t:4 * n + 3 * t + 3]
            outs[0][...], outs[1][...], outs[2][...] = _adamw_math(g_ref[...], w_ref[...], m_ref[...], v_ref[...])

    args, out_shape = [], []
    for g, w, m, v in zip(gs, ws, ms, vs):
        args += [g, w, m, v]
        out_shape += [jax.ShapeDtypeStruct(w.shape, F32)] * 3
    outs = _pallas(body, name=name, in_specs=[VMEM_SPEC] * len(args), out_specs=[VMEM_SPEC] * len(out_shape),
                   out_shape=out_shape, params=pltpu.CompilerParams(vmem_limit_bytes=VMEM_LIMIT))(*args)
    return [list(outs[3 * t:3 * t + 3]) for t in range(n)]


KIND = {"sc_w_in": "cols", "sc_w_out": "blocked", "w_dkv": "blocked", "w_kr": "cols", "w_uk": "cols", "w_uv": "cols",
        "w_dq": "blocked", "w_uq": "blocked", "w_o": "blocked", "ffn_w_up": "blocked", "ffn_w_down": "blocked",
        "conv": "blocked"}
GATHER_GROUPS = (("mixer", ("sc_w_in", "sc_w_out", "conv")),
                 ("up0", ("ffn_w_up0",)),
                 ("down0", ("ffn_w_down0",)),
                 ("attn", ("w_dkv", "w_kr", "w_uk", "w_uv", "w_dq", "w_uq", "w_o")),
                 ("ffn1", ("ffn_w_up1", "ffn_w_down1")))
SCATTER_GROUPS = (("ffn1", (("ffn_w_up", 1), ("ffn_w_down", 1))),
                  ("attn", (("w_o", None), ("w_uq", None), ("w_dq", None), ("w_uk", None), ("w_uv", None),
                            ("w_dkv", None), ("w_kr", None))),
                  ("ffn0", (("ffn_w_up", 0), ("ffn_w_down", 0))),
                  ("mixer", (("sc_w_out", None), ("sc_w_in", None))))
SCHEDULE = {
    "begin": (("gather_start", "mixer"),),
    "mixer_ready": (("gather_start", "up0"),),
    "l0_out": (("gather_forward", "up0"), ("gather_start", "down0")),
    "f0_up": (("gather_forward", "down0"), ("gather_start", "attn")),
    "f0_down": (("gather_forward", "attn"), ("gather_start", "ffn1")),
    "attn_fwd": (("gather_forward", "ffn1"),),
    "f1_gup": (("scatter_sibling", "ffn1"),),
    "f1_dhf": (("scatter_chips", "ffn1"),),
    "kv_bwd": (("scatter_sibling", "attn"), ("scatter_done", "ffn1")),
    "f0_dact": (("scatter_chips", "attn"),),
    "f0_gup": (("scatter_sibling", "ffn0"),),
    "f0_dhf": (("scatter_chips", "ffn0"),),
    "f0_bwd": (("scatter_done", "attn"),),
    "sc_bwd": (("scatter_sibling", "mixer"),),
    "d_l0_in": (("scatter_chips", "mixer"),),
}
FINISH = (("scatter_done", "ffn0"), ("scatter_done", "mixer"))
STAGES = {"gather_start": 1, "gather_forward": 2, "gather_done": 3,
          "scatter_sibling": 1, "scatter_chips": 2, "scatter_done": 3}
SMALL_W_ROWS = 24


def _pack(arrays, rows):
    flat = jnp.concatenate([a.reshape(-1).astype(F32) for a in arrays])
    return jnp.pad(flat, (0, rows * 128 - flat.shape[0])).reshape(rows, 128)


STORED_TRANSPOSED = ("ffn_w_up", "w_uq", "w_kr")


def _stored(name, a):
    return jnp.swapaxes(a, -1, -2) if name in STORED_TRANSPOSED else a


def _base(name):
    if name.startswith("ffn_w_") and name[-1] in "01":
        return name[:-1], int(name[-1])
    return name, None


class _Exchange:
    def __init__(self, wts, mom, var, ffn_conv_b):
        self.wts, self.mom, self.var = wts, mom, var
        x, y, c = _place()
        self.c_arr = jnp.reshape(c, (1,)).astype(jnp.int32)
        chip = 2 * x + y
        self.chip_ids = jnp.stack([chip, chip ^ 1, chip ^ 2, chip ^ 3]).astype(jnp.int32)
        self.ready = {"ffn_cb0": ffn_conv_b.reshape(2, N_FF_BLK, 1, FF_BLK)[0],
                      "ffn_cb1": ffn_conv_b.reshape(2, N_FF_BLK, 1, FF_BLK)[1]}
        self.gathers, self.group_of = {}, {}
        self.grads, self.scatters, self.results, self.queue = {}, {}, {}, []
        for gname, names in GATHER_GROUPS:
            self.gathers[gname] = dict(stage=0, names=names, kinds=[KIND[_base(nm)[0]] for nm in names])
            for nm in names:
                self.group_of[nm] = gname
        for nm in ("sc_conv_w", "ffn_cw0", "ffn_cw1"):
            self.group_of[nm] = "mixer"
        self.at("begin", None)

    def _shard(self, name):
        if name == "conv":
            return _pack([self.wts["sc_conv_w"], self.wts["ffn_conv_w"]], SMALL_W_ROWS).reshape(1, SMALL_W_ROWS, 128)
        base, layer = _base(name)
        a = _stored(base, self.wts[base])
        if layer is not None:
            a = a[layer:layer + 1]
        if KIND[base] == "cols":
            return a.reshape(a.shape[-2], a.shape[-1]).astype(BF16)
        return a.reshape((-1,) + a.shape[-2:]).astype(BF16)

    def _start(self, name, srcs, lands, ncopy, plan, st):
        self.queue.append((name, (srcs, lands, ncopy, plan), st))

    def _flush(self):
        if self.queue:
            flights = _copies_start("__".join(name for name, _, _ in self.queue), [job for _, job, _ in self.queue])
            for (_, _, st), flight in zip(self.queue, flights):
                st["flight"] = flight
            self.queue = []

    def _flight(self, st):
        self._flush()
        return st["flight"]

    def _gather_to(self, gname, stage, after):
        st = self.gathers[gname]
        if st["stage"] < 1 <= stage:
            shards = [self._shard(nm) for nm in st["names"]]
            lands = [_landing(s, kind) for s, kind in zip(shards, st["kinds"])]
            plan, ncopy = _plan_gather_chips(st["kinds"])
            self._start(f"ag_{gname}_chips", shards, lands, ncopy, plan, st)
            st["stage"] = 1
        if st["stage"] < 2 <= stage:
            plan, ncopy = _plan_gather_chips(st["kinds"])
            _, lands = _copies_wait(f"ag_{gname}_chips_wait", self._flight(st), ncopy, plan)
            plan, ncopy = _plan_gather_sibling(st["kinds"])
            self._start(f"ag_{gname}_sibling", [], lands, ncopy, plan, st)
            st["stage"] = 2
        if st["stage"] < 3 <= stage:
            plan, ncopy = _plan_gather_sibling(st["kinds"])
            _, lands = _copies_wait(f"ag_{gname}_sibling_wait", self._flight(st), ncopy, plan)
            for nm, land in zip(st["names"], lands):
                self._arrived(nm, land)
            st["stage"] = 3

    def _arrived(self, name, land):
        if name == "conv":
            conv = land.reshape(N_DEV, SMALL_W_ROWS * 128)
            self.ready["sc_conv_w"] = conv[:, :3 * 128].reshape(N_DEV, 3, 128).transpose(1, 0, 2).reshape(3, D)
            fcw = conv[:, 3 * 128:3 * 128 + 6 * 352].reshape(N_DEV, 2, 3, 352).transpose(1, 2, 0, 3)
            fcw = fcw.reshape(2, 3, N_FF_BLK, FF_BLK).transpose(0, 2, 1, 3)
            self.ready["ffn_cw0"], self.ready["ffn_cw1"] = fcw[0], fcw[1]
        elif name in ("sc_w_in", "w_uk", "w_uv") or name.startswith("ffn_w_up"):
            self.ready[name] = land
        elif name.startswith("ffn_w_down"):
            self.ready[name] = land.reshape(1, N_FF_BLK, FF_BLK, D)
        elif name == "w_kr":
            self.ready[name] = jnp.pad(land, ((0, 128 - QK_ROPE), (0, 0)))
        elif name == "w_uq":
            self.ready[name] = jnp.pad(land.reshape(N_HEADS, QK_NOPE + QK_ROPE, Q_LORA),
                                       ((0, 0), (0, QK_PAD - QK_NOPE - QK_ROPE), (0, 0)))
        else:
            self.ready[name] = land.reshape(D, land.shape[-1])

    def need(self, name, after):
        if name not in self.ready:
            self._gather_to(self.group_of[name], 3, after)
            self._flush()
        return self.ready[name]

    def grad(self, name, layer, array):
        self.grads[(name, layer)] = array

    def _scatter_to(self, gname, stage, after):
        keys = dict(SCATTER_GROUPS)[gname]
        st = self.scatters.setdefault(gname, dict(stage=0))
        kinds = [KIND[nm] for nm, _ in keys]
        if st["stage"] < 1 <= stage:
            grads = [self.grads[key] for key in keys]
            lands = []
            for gr, kind in zip(grads, kinds):
                shard = (gr.shape[0],) + gr.shape[2:] if kind == "blocked" else (gr.shape[0], gr.shape[1] // N_DEV)
                lands.append(lax.empty((N_CHIP,) + shard, BF16))
            plan, ncopy = _plan_scatter_sibling(kinds)
            self._start(f"rs_{gname}_sibling", grads, lands, ncopy, plan, st)
            st["stage"] = 1
        if st["stage"] < 2 <= stage:
            plan, ncopy = _plan_scatter_sibling(kinds)
            grads, recvs = _copies_wait(f"rs_{gname}_sibling_wait", self._flight(st), ncopy, plan)
            sums = _chip_sums(f"rs_{gname}_sums", grads, kinds, recvs, self.c_arr)
            lands = [lax.empty(s.shape, BF16) for s in sums]
            plan, ncopy = _plan_scatter_chips(len(sums))
            self._start(f"rs_{gname}_chips", sums, lands, ncopy, plan, st)
            st["stage"] = 2
        if st["stage"] < 3 <= stage:
            plan, ncopy = _plan_scatter_chips(len(keys))
            sums, recvs = _copies_wait(f"rs_{gname}_chips_wait", self._flight(st), ncopy, plan)
            items = []
            for (nm, layer), own, rv in zip(keys, sums, recvs):
                nl = 1 if layer is None else 2
                rows, w = own.shape[1], own.shape[2]
                w3, m3, v3 = (_stored(nm, src[nm]).reshape(nl, rows, w) for src in (self.wts, self.mom, self.var))
                items.append((own, rv, w3, m3, v3, 0 if layer is None else layer, self.results.get(nm)))
            outs = _adamw_group(f"adamw_{gname}", items, self.chip_ids)
            for (nm, _), out in zip(keys, outs):
                self.results[nm] = out
            st["stage"] = 3

    def at(self, place, after):
        for action, gname in SCHEDULE.get(place, ()):
            self._advance(action, gname, after)
        self._flush()

    def _advance(self, action, gname, after):
        if action.startswith("gather"):
            self._gather_to(gname, STAGES[action], after)
        else:
            self._scatter_to(gname, STAGES[action], after)

    def finish(self, after):
        for action, gname in FINISH:
            self._advance(action, gname, after)
        for gname, _ in SCATTER_GROUPS:
            self._scatter_to(gname, 3, after)
        return {nm: [_stored(nm, o.reshape(_stored(nm, self.wts[nm]).shape)) for o in outs]
                for nm, outs in self.results.items()}


REPLICATED = ("attn_norm", "ffn_norm", "final_norm", "kv_in_norm", "kv_latent_norm", "q_latent_norm", "ffn_conv_b")
WEIGHTS = ("attn_norm", "ffn_norm", "final_norm", "sc_w_in", "sc_conv_w", "sc_w_out", "kv_in_norm", "w_dkv",
           "kv_latent_norm", "w_kr", "w_uk", "w_uv", "w_dq", "q_latent_norm", "w_uq", "w_o", "ffn_w_up", "ffn_conv_w",
           "ffn_conv_b", "ffn_w_down")


def kernel(x, positions, attn_norm, ffn_norm, final_norm, sc_w_in, sc_conv_w, sc_w_out, kv_in_norm, w_dkv, kv_latent_norm, w_kr, w_uk, w_uv, w_dq, q_latent_norm, w_uq, w_o, ffn_w_up, ffn_conv_w, ffn_conv_b, ffn_w_down, loss_target, m_attn_norm, m_ffn_norm, m_final_norm, m_sc_w_in, m_sc_conv_w, m_sc_w_out, m_kv_in_norm, m_w_dkv, m_kv_latent_norm, m_w_kr, m_w_uk, m_w_uv, m_w_dq, m_q_latent_norm, m_w_uq, m_w_o, m_ffn_w_up, m_ffn_conv_w, m_ffn_conv_b, m_ffn_w_down, v_attn_norm, v_ffn_norm, v_final_norm, v_sc_w_in, v_sc_conv_w, v_sc_w_out, v_kv_in_norm, v_w_dkv, v_kv_latent_norm, v_w_kr, v_w_uk, v_w_uv, v_w_dq, v_q_latent_norm, v_w_uq, v_w_o, v_ffn_w_up, v_ffn_conv_w, v_ffn_conv_b, v_ffn_w_down):
    wts = dict(attn_norm=attn_norm, ffn_norm=ffn_norm, final_norm=final_norm, sc_w_in=sc_w_in, sc_conv_w=sc_conv_w,
               sc_w_out=sc_w_out, kv_in_norm=kv_in_norm, w_dkv=w_dkv, kv_latent_norm=kv_latent_norm, w_kr=w_kr,
               w_uk=w_uk, w_uv=w_uv, w_dq=w_dq, q_latent_norm=q_latent_norm, w_uq=w_uq, w_o=w_o, ffn_w_up=ffn_w_up,
               ffn_conv_w=ffn_conv_w, ffn_conv_b=ffn_conv_b, ffn_w_down=ffn_w_down)
    mom = dict(attn_norm=m_attn_norm, ffn_norm=m_ffn_norm, final_norm=m_final_norm, sc_w_in=m_sc_w_in,
               sc_conv_w=m_sc_conv_w, sc_w_out=m_sc_w_out, kv_in_norm=m_kv_in_norm, w_dkv=m_w_dkv,
               kv_latent_norm=m_kv_latent_norm, w_kr=m_w_kr, w_uk=m_w_uk, w_uv=m_w_uv, w_dq=m_w_dq,
               q_latent_norm=m_q_latent_norm, w_uq=m_w_uq, w_o=m_w_o, ffn_w_up=m_ffn_w_up, ffn_conv_w=m_ffn_conv_w,
               ffn_conv_b=m_ffn_conv_b, ffn_w_down=m_ffn_w_down)
    var = dict(attn_norm=v_attn_norm, ffn_norm=v_ffn_norm, final_norm=v_final_norm, sc_w_in=v_sc_w_in,
               sc_conv_w=v_sc_conv_w, sc_w_out=v_sc_w_out, kv_in_norm=v_kv_in_norm, w_dkv=v_w_dkv,
               kv_latent_norm=v_kv_latent_norm, w_kr=v_w_kr, w_uk=v_w_uk, w_uv=v_w_uv, w_dq=v_w_dq,
               q_latent_norm=v_q_latent_norm, w_uq=v_w_uq, w_o=v_w_o, ffn_w_up=v_ffn_w_up, ffn_conv_w=v_ffn_conv_w,
               ffn_conv_b=v_ffn_conv_b, ffn_w_down=v_ffn_w_down)
    xi, yi, ci = _place()
    me = 4 * xi + 2 * yi + ci
    _Chain.last = None

    ex = _Exchange(wts, mom, var, ffn_conv_b)
    rep = {
        "attn_norm": attn_norm, "ffn_norm": ffn_norm, "final_norm": final_norm,
        "kv_in_norm": kv_in_norm.reshape(1, D), "kv_latent_norm": kv_latent_norm.reshape(1, KV_LORA),
        "q_latent_norm": q_latent_norm.reshape(1, Q_LORA),
    }
    loss, grad_x, small = _local_step(x.reshape(T, D), positions.reshape(T, 1), loss_target.reshape(T, D), rep, ex)

    def rows_of(a):
        return a.reshape(-1, a.shape[-1])

    small_order = list(REPLICATED) + ["sc_conv_w", "ffn_conv_w"]
    shards = [loss.reshape(1, 1, 128)] + [rows_of(small[nm])[None] for nm in small_order]
    plan, ncopy = _plan_gather_all(len(shards))
    flight, = _copies_start("ag_small", [(shards, [lax.empty((1, N_DEV) + s.shape[1:], F32) for s in shards], ncopy, plan)])
    results = ex.finish(grad_x)
    _, gathered = _copies_wait("ag_small_wait", flight, ncopy, plan)
    params = [[None] + [rows_of(src[nm]) for nm in REPLICATED] + [None, None] for src in (wts, mom, var)]
    summed = _adamw_small(gathered, *params)
    loss_total = summed[0][0][0, 0]
    for nm, vals in zip(REPLICATED, summed[1:1 + len(REPLICATED)]):
        results[nm] = [a.reshape(wts[nm].shape) for a in vals]
    g_scw = lax.dynamic_slice(summed[-2][0], (0, me * 128), (3, 128))
    g_fcw = lax.dynamic_slice(summed[-1][0], (0, me * 352), (6, 352))
    conv = _adamw_plain("adamw_conv", [g_scw, g_fcw], *[[rows_of(src["sc_conv_w"]), rows_of(src["ffn_conv_w"])]
                                                        for src in (wts, mom, var)])
    for nm, g_own, vals in zip(("sc_conv_w", "ffn_conv_w"), (g_scw, g_fcw), conv):
        results[nm] = [a.reshape(wts[nm].shape) for a in [g_own] + vals]

    outs = [loss_total, grad_x.reshape(1, T, D)]
    for slot in range(4):
        outs.extend(results[nm][slot] for nm in WEIGHTS)
    return tuple(outs)
```

```python
import jax
import jax.numpy as jnp
from jax import lax
from jax.experimental import pallas as pl
from jax.experimental.pallas import tpu as pltpu

F32 = jnp.float32
BF16 = jnp.bfloat16

T = 2048
D = 1024
N_HEADS = 8
QK_NOPE = 128
QK_ROPE = 64
V_HEAD = 128
Q_LORA = 384
KV_LORA = 256
D_FF = 2816
CHUNK = 64
ROPE_THETA = 10000.0
EPS = 1e-6
NEG_INF = -1e30
ADAM_LR = 0.001
ADAM_B1 = 0.9
ADAM_B2 = 0.999
ADAM_EPS = 1e-08
ADAM_WD = 0.01
ADAM_STEP = 10

N_DEV = 8
N_CHIP = 4
FF_BLK = D_FF * 2 // N_DEV
N_FF_BLK = D_FF // FF_BLK
QK_PAD = 256
HALO = 16

TM = 1024
TS = 512
TR = 256
TQ = 512
VMEM_LIMIT = 56 * 1024 * 1024

NN = (((1,), (0,)), ((), ()))
NT = (((1,), (1,)), ((), ()))
TN = (((0,), (0,)), ((), ()))
MESH = pl.DeviceIdType.MESH


def _params(sem):
    return pltpu.CompilerParams(dimension_semantics=sem, vmem_limit_bytes=VMEM_LIMIT)


ANY_SPEC = pl.BlockSpec(memory_space=pl.ANY)
VMEM_SPEC = pl.BlockSpec(memory_space=pltpu.VMEM)


class _Chain:
    last = None


def _pallas(body, *, name, in_specs, out_specs, out_shape, grid=(), scratch_shapes=(), n_prefetch=0, aliases=None,
            params=None):
    def run(*args):
        after = _Chain.last
        n_lead = len(args)
        specs, operands, fn = list(in_specs), list(args), body
        if after is not None:
            def fn(*refs):
                return body(*refs[:n_lead], *refs[n_lead + 1:])
            specs.append(ANY_SPEC)
            operands.append(after)
        kw = dict(name=name, out_shape=out_shape, input_output_aliases=aliases or {})
        if params is not None:
            kw["compiler_params"] = params
        if n_prefetch:
            kw["grid_spec"] = pltpu.PrefetchScalarGridSpec(
                num_scalar_prefetch=n_prefetch, grid=grid, in_specs=specs, out_specs=out_specs,
                scratch_shapes=scratch_shapes)
        else:
            kw.update(grid=grid, in_specs=specs, out_specs=out_specs, scratch_shapes=scratch_shapes)
        outs = pl.pallas_call(fn, **kw)(*operands)
        _Chain.last = outs[0] if isinstance(outs, (list, tuple)) else outs
        return outs
    return run


def _mm(name, a, b, *, grid, a_spec, b_spec, o_spec, o_shape, o_dtype, dims, k_axis=None, acc_shape=None,
        add=None, add_spec=None):
    nk = grid[k_axis] if k_axis is not None else 1
    has_add = add is not None

    def body(*refs):
        a_ref, b_ref = refs[0], refs[1]
        p = 2
        add_ref = None
        if has_add:
            add_ref = refs[p]
            p += 1
        o_ref = refs[p]
        p += 1
        r = lax.dot_general(a_ref[...].astype(BF16), b_ref[...].astype(BF16), dims, preferred_element_type=F32)
        if k_axis is None:
            if has_add:
                r = r + add_ref[...].astype(F32)
            o_ref[...] = r.astype(o_dtype)
        else:
            acc = refs[p]
            k = pl.program_id(k_axis)

            @pl.when(k == 0)
            def _():
                acc[...] = r

            @pl.when(k > 0)
            def _():
                acc[...] += r

            @pl.when(k == nk - 1)
            def _():
                t = acc[...]
                if has_add:
                    t = t + add_ref[...].astype(F32)
                o_ref[...] = t.astype(o_dtype)

    in_specs = [a_spec, b_spec]
    args = [a, b]
    if has_add:
        in_specs.append(add_spec if add_spec is not None else o_spec)
        args.append(add)
    sem = tuple("arbitrary" if ax == k_axis else "parallel" for ax in range(len(grid)))
    scratch = [pltpu.VMEM(acc_shape, F32)] if k_axis is not None else []
    return _pallas(body, name=name, grid=grid, in_specs=in_specs, out_specs=o_spec,
                   out_shape=jax.ShapeDtypeStruct(o_shape, o_dtype), scratch_shapes=scratch, params=_params(sem))(*args)


def _mm_sum(name, parts, *, grid, o_spec, o_shape, o_dtype, add=None, norm_bwd=None):
    has_add = add is not None
    np_ = len(parts)
    nn = 1 if norm_bwd is None else len(norm_bwd[1])
    has_res = norm_bwd is not None and norm_bwd[2] is not None

    def body(*refs):
        accs = [None] * nn
        for p, (_, _, _, _, dims, n) in enumerate(parts):
            a_ref, b_ref = refs[2 * p], refs[2 * p + 1]
            for k in range(a_ref.shape[0]):
                r = lax.dot_general(a_ref[k], b_ref[k], dims, preferred_element_type=F32)
                accs[n] = r if accs[n] is None else accs[n] + r
        if norm_bwd is None:
            acc = accs[0]
            if has_add:
                acc = acc + refs[2 * np_][...]
            refs[-1][...] = acc.astype(o_dtype)
            return
        x_ref, g_refs = refs[2 * np_], refs[2 * np_ + 1:2 * np_ + 1 + nn]
        dx_ref, dxb_ref, dg_refs = refs[-2 - nn], refs[-1 - nn], refs[-nn:]
        xv = x_ref[...]
        r = lax.rsqrt(jnp.mean(xv * xv, axis=-1, keepdims=True) + EPS)
        xn = xv * r
        dx = refs[2 * np_ + 1 + nn][...] if has_res else None
        sums = []
        for acc, g_ref in zip(accs, g_refs):
            gdy = acc * g_ref[...]
            t = r * (gdy - xn * jnp.mean(gdy * xn, axis=-1, keepdims=True))
            dx = t if dx is None else dx + t
            sums.append(jnp.sum(acc * xn, axis=0, keepdims=True))
        dx_ref[...] = dx
        dxb_ref[...] = dx.astype(BF16)

        @pl.when(pl.program_id(0) == 0)
        def _():
            for dg_ref, part in zip(dg_refs, sums):
                dg_ref[...] = part

        @pl.when(pl.program_id(0) > 0)
        def _():
            for dg_ref, part in zip(dg_refs, sums):
                dg_ref[...] += part

    in_specs, args = [], []
    for a, a_spec, b, b_spec, _, _ in parts:
        in_specs += [a_spec, b_spec]
        args += [a, b]
    if norm_bwd is None:
        if has_add:
            in_specs.append(o_spec)
            args.append(add)
        return _pallas(body, name=name, grid=grid, in_specs=in_specs, out_specs=o_spec,
                       out_shape=jax.ShapeDtypeStruct(o_shape, o_dtype),
                       params=_params(("parallel",) * len(grid)))(*args)
    x, gains, dres = norm_bwd
    vec = pl.BlockSpec((1, o_shape[1]), lambda i: (0, 0))
    in_specs += [o_spec] + [vec] * nn + ([o_spec] if has_res else [])
    args += [x] + list(gains) + ([dres] if has_res else [])
    outs = _pallas(body, name=name, grid=grid, in_specs=in_specs, out_specs=[o_spec, o_spec] + [vec] * nn,
                   out_shape=[jax.ShapeDtypeStruct(o_shape, F32), jax.ShapeDtypeStruct(o_shape, BF16)]
                   + [jax.ShapeDtypeStruct((1, o_shape[1]), F32)] * nn,
                   params=_params(("arbitrary",)))(*args)
    return outs[0], outs[1], list(outs[2:])


def _mm_rows(name, a, b, dims, o_dtype, n_out, *, tn=None, add=None):
    k = a.shape[1]
    tn = n_out if tn is None else tn
    if dims == NN:
        b_spec = pl.BlockSpec((k, tn), lambda n, i: (0, n))
    else:
        b_spec = pl.BlockSpec((tn, k), lambda n, i: (n, 0))
    return _mm(name, a, b, grid=(n_out // tn, T // TM),
               a_spec=pl.BlockSpec((TM, k), lambda n, i: (i, 0)), b_spec=b_spec,
               o_spec=pl.BlockSpec((TM, tn), lambda n, i: (i, n)), o_shape=(T, n_out), o_dtype=o_dtype,
               dims=dims, add=add)


def _wgrads(name, jobs):
    arrays, index = [], {}
    for a, b in jobs:
        for arr in (a, b):
            if id(arr) not in index:
                index[id(arr)] = len(arrays)
                arrays.append(arr)
    n_in = len(arrays)

    def body(*refs):
        for t, (a, b) in enumerate(jobs):
            a_ref, b_ref, o_ref = refs[index[id(a)]], refs[index[id(b)]], refs[n_in + t]
            if a.ndim == 3:
                for h in range(a.shape[0]):
                    o_ref[h] = lax.dot_general(a_ref[h], b_ref[...], TN, preferred_element_type=F32).astype(BF16)
            else:
                o_ref[...] = lax.dot_general(a_ref[...], b_ref[...], TN, preferred_element_type=F32).astype(BF16)

    out_shape = [jax.ShapeDtypeStruct(a.shape[:-2] + (a.shape[-1], b.shape[-1]), BF16) for a, b in jobs]
    return _pallas(body, name=name, in_specs=[VMEM_SPEC] * n_in, out_specs=[VMEM_SPEC] * len(jobs), out_shape=out_shape,
                   params=pltpu.CompilerParams(vmem_limit_bytes=VMEM_LIMIT))(*arrays)


def _mm_wgrad(name, a, b, *, tn=512):
    k, n = a.shape[1], b.shape[1]
    tn = min(tn, n)
    return _mm(name, a, b, grid=(n // tn,),
               a_spec=pl.BlockSpec((T, k), lambda j: (0, 0)), b_spec=pl.BlockSpec((T, tn), lambda j: (0, j)),
               o_spec=pl.BlockSpec((k, tn), lambda j: (0, j)), o_shape=(k, n), o_dtype=BF16, dims=TN)


def _rms_fwd(name, x, g):
    d = x.shape[1]

    def body(x_ref, g_ref, o_ref):
        xv = x_ref[...]
        r = lax.rsqrt(jnp.mean(xv * xv, axis=-1, keepdims=True) + EPS)
        o_ref[...] = ((xv * r) * g_ref[...]).astype(BF16)

    return _pallas(
        body, name=name, grid=(T // TM,),
        in_specs=[pl.BlockSpec((TM, d), lambda i: (i, 0)), pl.BlockSpec((1, d), lambda i: (0, 0))],
        out_specs=pl.BlockSpec((TM, d), lambda i: (i, 0)),
        out_shape=jax.ShapeDtypeStruct((T, d), BF16), params=_params(("parallel",)))(x, g)


def _rows_call(name, body, row_ins, whole_ins, outs):
    in_specs = [pl.BlockSpec((TM, a.shape[1]), lambda i: (i, 0)) for a in row_ins]
    in_specs += [pl.BlockSpec(a.shape, lambda i: (0, 0)) for a in whole_ins]
    return _pallas(
        body, name=name, grid=(T // TM,), in_specs=in_specs,
        out_specs=[pl.BlockSpec((TM, d), lambda i: (i, 0)) for d, _ in outs],
        out_shape=[jax.ShapeDtypeStruct((T, d), dt) for d, dt in outs],
        params=_params(("parallel",)))(*row_ins, *whole_ins)


def _rms(xv, g):
    return (xv * lax.rsqrt(jnp.mean(xv * xv, axis=-1, keepdims=True) + EPS)) * g


def _rms_fwd2(name, x, g1, g2):
    d = x.shape[1]

    def body(x_ref, g1_ref, g2_ref, o1_ref, o2_ref):
        xv = x_ref[...]
        xn = xv * lax.rsqrt(jnp.mean(xv * xv, axis=-1, keepdims=True) + EPS)
        o1_ref[...] = (xn * g1_ref[...]).astype(BF16)
        o2_ref[...] = (xn * g2_ref[...]).astype(BF16)

    return _rows_call(name, body, [x], [g1, g2], [(d, BF16), (d, BF16)])


def _down_norm(name, a, w, g):
    n = w.shape[1]

    def body(a_ref, w_ref, g_ref, raw_ref, o_ref):
        raw = lax.dot_general(a_ref[...], w_ref[...], NN, preferred_element_type=F32)
        raw_ref[...] = raw
        o_ref[...] = _rms(raw, g_ref[...]).astype(BF16)

    return _rows_call(name, body, [a], [w, g], [(n, F32), (n, BF16)])


def _kv_down(hk, w_dkv, w_kr, g, tables):
    def body(a_ref, c_ref, sa_ref, sb_ref, wd_ref, wr_ref, g_ref, raw_ref, ckv_ref, kr_ref):
        av = a_ref[...]
        raw = lax.dot_general(av, wd_ref[...], NN, preferred_element_type=F32)
        raw_ref[...] = raw
        ckv_ref[...] = _rms(raw, g_ref[...]).astype(BF16)
        kr = lax.dot_general(av, wr_ref[...], NT, preferred_element_type=F32)
        kr_ref[...] = _rotate(kr, c_ref[...], sa_ref[...], sb_ref[...], 1.0).astype(BF16)

    return _rows_call("kv_down", body, [hk, *tables], [w_dkv, w_kr, g], [(KV_LORA, F32), (KV_LORA, BF16), (128, BF16)])


def _kv_up(ckv, w_uk, w_uv):
    def body(a_ref, wk_ref, wv_ref, k_ref, v_ref):
        av = a_ref[...]
        k_ref[...] = lax.dot_general(av, wk_ref[...], NN, preferred_element_type=F32).astype(BF16)
        v_ref[...] = lax.dot_general(av, wv_ref[...], NN, preferred_element_type=F32).astype(BF16)

    return _rows_call("kv_up", body, [ckv], [w_uk, w_uv], [(N_HEADS * QK_NOPE, BF16), (N_HEADS * V_HEAD, BF16)])


def _rms_bwd(name, x, gains, dys, dres=None):
    d = x.shape[1]
    n = len(gains)
    has_res = dres is not None

    def body(*refs):
        x_ref, g_refs, dy_refs = refs[0], refs[1:1 + n], refs[1 + n:1 + 2 * n]
        dx_ref, dxb_ref = refs[-2 - n], refs[-1 - n]
        dg_refs = refs[-n:]
        xv = x_ref[...]
        r = lax.rsqrt(jnp.mean(xv * xv, axis=-1, keepdims=True) + EPS)
        xn = xv * r
        dx = refs[1 + 2 * n][...] if has_res else None
        parts = []
        for g_ref, dy_ref in zip(g_refs, dy_refs):
            dyv = dy_ref[...].astype(F32)
            gdy = dyv * g_ref[...]
            t = r * (gdy - xn * jnp.mean(gdy * xn, axis=-1, keepdims=True))
            dx = t if dx is None else dx + t
            parts.append(jnp.sum(dyv * xn, axis=0, keepdims=True))
        dx_ref[...] = dx
        dxb_ref[...] = dx.astype(BF16)

        @pl.when(pl.program_id(0) == 0)
        def _():
            for dg_ref, part in zip(dg_refs, parts):
                dg_ref[...] = part

        @pl.when(pl.program_id(0) > 0)
        def _():
            for dg_ref, part in zip(dg_refs, parts):
                dg_ref[...] += part

    row = pl.BlockSpec((TR, d), lambda i: (i, 0))
    vec = pl.BlockSpec((1, d), lambda i: (0, 0))
    args = [x] + list(gains) + list(dys) + ([dres] if has_res else [])
    in_specs = [row] + [vec] * n + [row] * n + ([row] if has_res else [])
    outs = _pallas(
        body, name=name, grid=(T // TR,), in_specs=in_specs, out_specs=[row, row] + [vec] * n,
        out_shape=[jax.ShapeDtypeStruct((T, d), F32), jax.ShapeDtypeStruct((T, d), BF16)]
        + [jax.ShapeDtypeStruct((1, d), F32)] * n,
        params=_params(("arbitrary",)))(*args)
    return outs[0], outs[1], list(outs[2:])


def _final(h, g, tgt):
    def body(h_ref, g_ref, t_ref, loss_ref, dh_ref, dhb_ref, dg_ref):
        hv = h_ref[...]
        r = lax.rsqrt(jnp.mean(hv * hv, axis=-1, keepdims=True) + EPS)
        xn = hv * r
        gv = g_ref[...]
        err = xn * gv - t_ref[...]
        part_loss = 0.5 * jnp.sum(jnp.mean(err * err, axis=-1, keepdims=True), axis=0, keepdims=True)
        dy = err * (1.0 / D)
        gdy = dy * gv
        dh = r * (gdy - xn * jnp.mean(gdy * xn, axis=-1, keepdims=True))
        dh_ref[...] = dh
        dhb_ref[...] = dh.astype(BF16)
        part = jnp.sum(dy * xn, axis=0, keepdims=True)
        first = pl.program_id(0) == 0

        @pl.when(first)
        def _():
            dg_ref[...] = part
            loss_ref[...] = jnp.broadcast_to(part_loss, (1, 128))

        @pl.when(jnp.logical_not(first))
        def _():
            dg_ref[...] += part
            loss_ref[...] += jnp.broadcast_to(part_loss, (1, 128))

    row = pl.BlockSpec((TR, D), lambda i: (i, 0))
    vec = pl.BlockSpec((1, D), lambda i: (0, 0))
    return _pallas(
        body, name="final_loss", grid=(T // TR,), in_specs=[row, vec, row],
        out_specs=[pl.BlockSpec((1, 128), lambda i: (0, 0)), row, row, vec],
        out_shape=[jax.ShapeDtypeStruct((1, 128), F32), jax.ShapeDtypeStruct((T, D), F32),
                   jax.ShapeDtypeStruct((T, D), BF16), jax.ShapeDtypeStruct((1, D), F32)],
        params=_params(("arbitrary",)))(h, g, tgt)


def _prev_idx(i, rows=TR):
    return jnp.maximum(i * (rows // HALO) - 1, 0)


def _next_idx(i, rows=TR):
    return jnp.minimum((i + 1) * (rows // HALO), T // HALO - 1)


def _causal_taps(ext):
    return pltpu.roll(ext, 2, 0)[HALO:], pltpu.roll(ext, 1, 0)[HALO:], ext[HALO:]


def _anticausal_taps(ext, n):
    rows = ext.shape[0]
    return pltpu.roll(ext, rows - 1, 0)[:n], pltpu.roll(ext, rows - 2, 0)[:n]


MIX_COLS = 512


def _mixer_in(hn, w_in, w):
    nc = D // MIX_COLS

    def body(h_ref, hh_ref, wb_ref, wc_ref, wu_ref, w_ref, b_ref, c_ref, u_ref, y_ref):
        i = pl.program_id(1)
        hv = h_ref[...]
        he = jnp.concatenate([hh_ref[...], hv], axis=0)
        ce = lax.dot_general(he, wc_ref[...], NN, preferred_element_type=F32).astype(BF16)
        ue = lax.dot_general(he, wu_ref[...], NN, preferred_element_type=F32).astype(BF16)
        bv = lax.dot_general(hv, wb_ref[...], NN, preferred_element_type=F32).astype(BF16)
        b_ref[...] = bv
        c_ref[...] = ce[HALO:]
        u_ref[...] = ue[HALO:]
        row = lax.broadcasted_iota(jnp.int32, (HALO + TS, 1), 0)
        cu = jnp.where(jnp.logical_or(i > 0, row >= HALO), ce.astype(F32) * ue.astype(F32), 0.0)
        x2, x1, x0 = _causal_taps(cu)
        wv = w_ref[...]
        cv = (x2 * wv[0:1] + x1 * wv[1:2]) + x0 * wv[2:3]
        y_ref[...] = (bv.astype(F32) * cv).astype(BF16)

    def cols(part):
        return pl.BlockSpec((D, MIX_COLS), lambda j, i: (0, part * nc + j))

    blk = pl.BlockSpec((TS, MIX_COLS), lambda j, i: (i, j))
    out = jax.ShapeDtypeStruct((T, D), BF16)
    return _pallas(
        body, name="l0_in", grid=(nc, T // TS),
        in_specs=[pl.BlockSpec((TS, D), lambda j, i: (i, 0)), pl.BlockSpec((HALO, D), lambda j, i: (_prev_idx(i, TS), 0)),
                  cols(0), cols(1), cols(2), pl.BlockSpec((3, MIX_COLS), lambda j, i: (0, j))],
        out_specs=[blk] * 4, out_shape=[out] * 4,
        params=_params(("parallel", "parallel")))(hn, hn, w_in, w_in, w_in, w)


def _mixer_out_bwd(dh, w_out, zb, zc, zu, w):
    last = T // TR - 1

    def body(dh_ref, dhn_ref, wo_ref, b_ref, bn_ref, c_ref, ch_ref, u_ref, uh_ref, w_ref, dz_ref, dw_ref):
        i = pl.program_id(0)
        dye = lax.dot_general(jnp.concatenate([dh_ref[...], dhn_ref[...]], axis=0), wo_ref[...], NT,
                              preferred_element_type=F32)
        cv_ = c_ref[...].astype(F32)
        uv = u_ref[...].astype(F32)
        cu = cv_ * uv
        cuh = jnp.where(i > 0, ch_ref[...].astype(F32) * uh_ref[...].astype(F32), 0.0)
        x2, x1, x0 = _causal_taps(jnp.concatenate([cuh, cu], axis=0))
        wv = w_ref[...]
        conv = (x2 * wv[0:1] + x1 * wv[1:2]) + x0 * wv[2:3]
        dyv = dye[:TR]
        dz_ref[:, 0:D] = (dyv * conv).astype(BF16)
        dconv = dyv * b_ref[...].astype(F32)
        dconv_n = jnp.where(i < last, dye[TR:] * bn_ref[...].astype(F32), 0.0)
        n1, n2 = _anticausal_taps(jnp.concatenate([dconv, dconv_n], axis=0), TR)
        dcu = (dconv * wv[2:3] + n1 * wv[1:2]) + n2 * wv[0:1]
        dz_ref[:, D:2 * D] = (dcu * uv).astype(BF16)
        dz_ref[:, 2 * D:3 * D] = (dcu * cv_).astype(BF16)
        part = jnp.concatenate([jnp.sum(dconv * x2, axis=0, keepdims=True),
                                jnp.sum(dconv * x1, axis=0, keepdims=True),
                                jnp.sum(dconv * x0, axis=0, keepdims=True)], axis=0)

        @pl.when(i == 0)
        def _():
            dw_ref[...] = part

        @pl.when(i > 0)
        def _():
            dw_ref[...] += part

    main = pl.BlockSpec((TR, D), lambda i: (i, 0))
    prev = pl.BlockSpec((HALO, D), lambda i: (_prev_idx(i), 0))
    nxt = pl.BlockSpec((HALO, D), lambda i: (_next_idx(i), 0))
    wspec = pl.BlockSpec((3, D), lambda i: (0, 0))
    return _pallas(
        body, name="d_l0_out", grid=(T // TR,),
        in_specs=[main, nxt, pl.BlockSpec((D, D), lambda i: (0, 0)), main, nxt, main, prev, main, prev, wspec],
        out_specs=[pl.BlockSpec((TR, 3 * D), lambda i: (i, 0)), wspec],
        out_shape=[jax.ShapeDtypeStruct((T, 3 * D), BF16), jax.ShapeDtypeStruct((3, D), F32)],
        params=_params(("arbitrary",)))(dh, dh, w_out, zb, zb, zc, zc, zu, zu, w)


def _sigmoid(x):
    return 0.5 * jnp.tanh(0.5 * x) + 0.5


def _ffn_up_act(name, hf, w_up, w, b):
    def body(h_ref, hh_ref, wg_ref, wv_ref, w_ref, b_ref, g_ref, v_ref, a_ref):
        i = pl.program_id(1)
        hv = h_ref[...]
        ge = lax.dot_general(jnp.concatenate([hh_ref[...], hv], axis=0), wg_ref[...], NT,
                             preferred_element_type=F32).astype(BF16)
        v = lax.dot_general(hv, wv_ref[...], NT, preferred_element_type=F32).astype(BF16)
        g_ref[...] = ge[HALO:]
        v_ref[...] = v
        ext = ge.astype(F32)
        row = lax.broadcasted_iota(jnp.int32, (HALO + TM, 1), 0)
        ext = jnp.where(jnp.logical_or(i > 0, row >= HALO), ext, 0.0)
        x2, x1, x0 = _causal_taps(ext)
        wv = w_ref[...]
        gc = ((x2 * wv[0:1] + x1 * wv[1:2]) + x0 * wv[2:3]) + b_ref[...]
        a_ref[...] = ((gc * _sigmoid(gc)) * v.astype(F32)).astype(BF16)

    blk = pl.BlockSpec((None, TM, FF_BLK), lambda j, i: (j, i, 0))
    out = jax.ShapeDtypeStruct((N_FF_BLK, T, FF_BLK), BF16)
    return _pallas(
        body, name=name, grid=(N_FF_BLK, T // TM),
        in_specs=[pl.BlockSpec((TM, D), lambda j, i: (i, 0)),
                  pl.BlockSpec((HALO, D), lambda j, i: (_prev_idx(i, TM), 0)),
                  pl.BlockSpec((None, None, FF_BLK, D), lambda j, i: (0, j, 0, 0)),
                  pl.BlockSpec((None, None, FF_BLK, D), lambda j, i: (0, j + N_FF_BLK, 0, 0)),
                  pl.BlockSpec((None, 3, FF_BLK), lambda j, i: (j, 0, 0)),
                  pl.BlockSpec((None, 1, FF_BLK), lambda j, i: (j, 0, 0))],
        out_specs=[blk, blk, blk], out_shape=[out, out, out],
        params=_params(("parallel", "parallel")))(hf, hf, w_up, w_up, w, b)


def _ffn_dact(name, dh, w_down4, g, v, w, b):
    last = T // TS - 1

    def body(dh_ref, dhn_ref, wd_ref, g_ref, gp_ref, gn_ref, v_ref, vn_ref, w_ref, b_ref, dg_ref, dv_ref, dw_ref, db_ref):
        i = pl.program_id(1)
        da = lax.dot_general(jnp.concatenate([dh_ref[...], dhn_ref[...]], axis=0), wd_ref[...], NT,
                             preferred_element_type=F32)
        row = lax.broadcasted_iota(jnp.int32, (TS + HALO, 1), 0)
        da = jnp.where(jnp.logical_or(i < last, row < TS), da, 0.0)
        gp = jnp.where(i > 0, gp_ref[...].astype(F32), 0.0)
        ext = jnp.concatenate([gp, g_ref[...].astype(F32), gn_ref[...].astype(F32)], axis=0)
        x2, x1, x0 = _causal_taps(ext)
        wv = w_ref[...]
        gc = ((x2 * wv[0:1] + x1 * wv[1:2]) + x0 * wv[2:3]) + b_ref[...]
        sg = _sigmoid(gc)
        vv = jnp.concatenate([v_ref[...].astype(F32), vn_ref[...].astype(F32)], axis=0)
        dv_ref[...] = (da[:TS] * (gc[:TS] * sg[:TS])).astype(BF16)
        dgc = (da * vv) * (sg * (1.0 + gc * (1.0 - sg)))
        n1, n2 = _anticausal_taps(dgc, TS)
        d0 = dgc[:TS]
        dg_ref[...] = ((d0 * wv[2:3] + n1 * wv[1:2]) + n2 * wv[0:1]).astype(BF16)
        part_w = jnp.concatenate([jnp.sum(d0 * x2[:TS], axis=0, keepdims=True),
                                  jnp.sum(d0 * x1[:TS], axis=0, keepdims=True),
                                  jnp.sum(d0 * x0[:TS], axis=0, keepdims=True)], axis=0)
        part_b = jnp.sum(d0, axis=0, keepdims=True)

        @pl.when(i == 0)
        def _():
            dw_ref[...] = part_w
            db_ref[...] = part_b

        @pl.when(i > 0)
        def _():
            dw_ref[...] += part_w
            db_ref[...] += part_b

    blk = pl.BlockSpec((None, TS, FF_BLK), lambda j, i: (j, i, 0))
    prev = pl.BlockSpec((None, HALO, FF_BLK), lambda j, i: (j, _prev_idx(i, TS), 0))
    nxt = pl.BlockSpec((None, HALO, FF_BLK), lambda j, i: (j, _next_idx(i, TS), 0))
    wspec = pl.BlockSpec((None, 3, FF_BLK), lambda j, i: (j, 0, 0))
    bspec = pl.BlockSpec((None, 1, FF_BLK), lambda j, i: (j, 0, 0))
    return _pallas(
        body, name=name, grid=(N_FF_BLK, T // TS),
        in_specs=[pl.BlockSpec((TS, D), lambda j, i: (i, 0)),
                  pl.BlockSpec((HALO, D), lambda j, i: (_next_idx(i, TS), 0)),
                  pl.BlockSpec((None, None, FF_BLK, D), lambda j, i: (0, j, 0, 0)),
                  blk, prev, nxt, blk, nxt, wspec, bspec],
        out_specs=[blk, blk, wspec, bspec],
        out_shape=[jax.ShapeDtypeStruct((N_FF_BLK, T, FF_BLK), BF16), jax.ShapeDtypeStruct((N_FF_BLK, T, FF_BLK), BF16),
                   jax.ShapeDtypeStruct((N_FF_BLK, 3, FF_BLK), F32), jax.ShapeDtypeStruct((N_FF_BLK, 1, FF_BLK), F32)],
        params=_params(("parallel", "arbitrary")))(dh, dh, w_down4, g, g, g, v, v, w, b)


def _rope_tables(pos, inv_freq):
    half = QK_ROPE // 2

    def body(p_ref, f_ref, c_ref, sa_ref, sb_ref):
        ang = p_ref[...].astype(F32) * f_ref[...]
        lane = lax.broadcasted_iota(jnp.int32, (T, 128), 1)
        c = jnp.cos(ang)
        s = jnp.sin(ang)
        c_ref[...] = jnp.where(lane < 2 * half, c, 0.0)
        sa_ref[...] = jnp.where(lane < half, -s, 0.0)
        sb_ref[...] = jnp.where(jnp.logical_and(lane >= half, lane < 2 * half), s, 0.0)

    return _pallas(
        body, name="rope_tables", in_specs=[VMEM_SPEC] * 2, out_specs=[VMEM_SPEC] * 3,
        out_shape=[jax.ShapeDtypeStruct((T, 128), F32)] * 3,
        params=pltpu.CompilerParams(vmem_limit_bytes=VMEM_LIMIT))(pos, inv_freq)


def _rotate(r, c, sa, sb, sign):
    return r * c + sign * (pltpu.roll(r, 96, 1) * sa + pltpu.roll(r, 32, 1) * sb)


def _q_up(cq, w_uq, tables):
    cos, sa, sb = tables

    def body(a_ref, b_ref, c_ref, sa_ref, sb_ref, o_ref):
        for h in range(N_HEADS):
            r = lax.dot_general(a_ref[...], b_ref[h], NT, preferred_element_type=F32)
            o_ref[h, :, :QK_NOPE] = r[:, :QK_NOPE].astype(BF16)
            o_ref[h, :, QK_NOPE:] = _rotate(r[:, QK_NOPE:], c_ref[...], sa_ref[...], sb_ref[...], 1.0).astype(BF16)

    tab = pl.BlockSpec((TS, 128), lambda i: (i, 0))
    return _pallas(
        body, name="q_up", grid=(T // TS,),
        in_specs=[pl.BlockSpec((TS, Q_LORA), lambda i: (i, 0)),
                  pl.BlockSpec((N_HEADS, QK_PAD, Q_LORA), lambda i: (0, 0, 0)), tab, tab, tab],
        out_specs=pl.BlockSpec((N_HEADS, TS, QK_PAD), lambda i: (0, i, 0)),
        out_shape=jax.ShapeDtypeStruct((N_HEADS, T, QK_PAD), BF16),
        params=_params(("parallel",)))(cq, w_uq, cos, sa, sb)


def _rope(name, x, tables, sign, out_dtype, reduce_groups=False):
    g, _, w = x.shape
    cos, sa, sb = tables

    def body(x_ref, c_ref, sa_ref, sb_ref, o_ref):
        xv = x_ref[...].astype(F32)
        if reduce_groups:
            acc = xv[0]
            for k in range(1, g):
                acc = acc + xv[k]
            xv = acc
        out = _rotate(xv[:, w - 128:], c_ref[...], sa_ref[...], sb_ref[...], sign)
        if w > 128:
            o_ref[:, :w - 128] = xv[:, :w - 128].astype(out_dtype)
        o_ref[:, w - 128:] = out.astype(out_dtype)

    tab = pl.BlockSpec((TM, 128), lambda h, i: (i, 0))
    if reduce_groups:
        x_spec = pl.BlockSpec((g, TM, w), lambda h, i: (0, i, 0))
        groups = 1
    else:
        x_spec = pl.BlockSpec((None, TM, w), lambda h, i: (h, i, 0))
        groups = g
    return _pallas(
        body, name=name, grid=(groups, T // TM), in_specs=[x_spec, tab, tab, tab],
        out_specs=pl.BlockSpec((None, TM, w), lambda h, i: (h, i, 0)),
        out_shape=jax.ShapeDtypeStruct((groups, T, w), out_dtype),
        params=_params(("parallel", "parallel")))(x, cos, sa, sb)


SCALE = (QK_NOPE + QK_ROPE) ** -0.5
LOG2E = 1.4426950408889634
SCALE2 = SCALE * LOG2E


def _diag_mask(transposed):
    shift = CHUNK.bit_length() - 1
    a = lax.broadcasted_iota(jnp.int32, (TQ, TQ), 0) >> shift
    b = lax.broadcasted_iota(jnp.int32, (TQ, TQ), 1) >> shift
    return (a <= b) if transposed else (b <= a)


def _as_row(col):
    return jnp.transpose(jnp.broadcast_to(col, (col.shape[0], 128)), (1, 0))[0:1]


def _keys(kn_ref, kr_ref, off):
    return jnp.concatenate([kn_ref[pl.ds(off, TQ), :], kr_ref[pl.ds(off, TQ), :]], axis=1)


def _attn_fwd(q, kn, kr, v):
    hp = 2

    def body(q_ref, kn_ref, kr_ref, v_ref, o_ref, lse_ref):
        i = pl.program_id(1)
        qs = [q_ref[a] for a in range(hp)]

        def step(j, carry, masked):
            off = pl.multiple_of(j * TQ, TQ)
            krv = kr_ref[pl.ds(off, TQ), :]
            ss = []
            for a in range(hp):
                kk = jnp.concatenate([kn_ref[pl.ds(off, TQ), a * QK_NOPE:(a + 1) * QK_NOPE], krv], axis=1)
                ss.append(lax.dot_general(qs[a], kk, NT, preferred_element_type=F32))
            out = []
            for a in range(hp):
                m, l, acc = carry[a]
                s = ss[a] * SCALE2
                if masked:
                    s = jnp.where(_diag_mask(False), s, NEG_INF)
                m_new = jnp.maximum(m, jnp.max(s, axis=-1, keepdims=True))
                p = jnp.exp2(s - m_new)
                alpha = jnp.exp2(m - m_new)
                l = alpha * l + jnp.sum(p, axis=-1, keepdims=True)
                pv = lax.dot_general(p.astype(BF16), v_ref[pl.ds(off, TQ), a * V_HEAD:(a + 1) * V_HEAD], NN,
                                     preferred_element_type=F32)
                out.append((m_new, l, alpha * acc + pv))
            return tuple(out)

        one = (jnp.full((TQ, 1), NEG_INF, F32), jnp.zeros((TQ, 1), F32), jnp.zeros((TQ, V_HEAD), F32))
        carry = lax.fori_loop(0, i, lambda j, cr: step(j, cr, False), (one,) * hp)
        carry = step(i, carry, True)
        for a, (m, l, acc) in enumerate(carry):
            o_ref[:, a * V_HEAD:(a + 1) * V_HEAD] = (acc / l).astype(BF16)
            lse_ref[a] = _as_row(m + jnp.log(l) * LOG2E)

    return _pallas(
        body, name="attn_fwd", grid=(N_HEADS // hp, T // TQ),
        in_specs=[pl.BlockSpec((hp, TQ, QK_PAD), lambda h, i: (h, i, 0)),
                  pl.BlockSpec((T, hp * QK_NOPE), lambda h, i: (0, h)),
                  pl.BlockSpec((T, 128), lambda h, i: (0, 0)),
                  pl.BlockSpec((T, hp * V_HEAD), lambda h, i: (0, h))],
        out_specs=[pl.BlockSpec((TQ, hp * V_HEAD), lambda h, i: (i, h)), pl.BlockSpec((hp, 1, TQ), lambda h, i: (h, 0, i))],
        out_shape=[jax.ShapeDtypeStruct((T, N_HEADS * V_HEAD), BF16), jax.ShapeDtypeStruct((N_HEADS, 1, T), F32)],
        params=_params(("parallel", "parallel")))(q, kn, kr, v)


def _attn_bwd(q, kn, kr, v, o, do, lse_row, tables):
    nq = T // TQ
    hp = 2
    cos, sa, sb = tables

    def body(q_ref, kn_ref, kr_ref, v_ref, o_ref, do_ref, lse_ref, c_ref, sa_ref, sb_ref,
             dq_ref, dkn_ref, dkr_ref, dv_ref, dq_acc, dl_ref):
        j = pl.program_id(1)

        def cols(a):
            return slice(a * 128, (a + 1) * 128)

        @pl.when(j == 0)
        def _():
            dq_acc[...] = jnp.zeros_like(dq_acc)
            for a in range(hp):
                for i in range(nq):
                    rows = pl.ds(i * TQ, TQ)
                    prod = do_ref[rows, cols(a)].astype(F32) * o_ref[rows, cols(a)].astype(F32)
                    dl_ref[a, :, rows] = _as_row(jnp.sum(prod, axis=-1, keepdims=True))

        krv = kr_ref[...]
        kks = [jnp.concatenate([kn_ref[:, cols(a)], krv], axis=1) for a in range(hp)]
        vvs = [v_ref[:, cols(a)] for a in range(hp)]

        def step(i, carry, masked):
            off = pl.multiple_of(i * TQ, TQ)
            rows = pl.ds(off, TQ)
            qis = [q_ref[a, rows, :] for a in range(hp)]
            dois = [do_ref[rows, cols(a)] for a in range(hp)]
            sts = [lax.dot_general(kks[a], qis[a], NT, preferred_element_type=F32) for a in range(hp)]
            dpts = [lax.dot_general(vvs[a], dois[a], NT, preferred_element_type=F32) for a in range(hp)]
            out = []
            for a in range(hp):
                dk, dv = carry[a]
                st = sts[a] * SCALE2
                if masked:
                    st = jnp.where(_diag_mask(True), st, NEG_INF)
                pt = jnp.exp2(st - lse_ref[a, :, rows])
                dv = dv + lax.dot_general(pt.astype(BF16), dois[a], NN, preferred_element_type=F32)
                dst = ((pt * (dpts[a] - dl_ref[a, :, rows])) * SCALE).astype(BF16)
                dk = dk + lax.dot_general(dst, qis[a], NN, preferred_element_type=F32)
                dq_acc[a, rows, :] += lax.dot_general(dst, kks[a], TN, preferred_element_type=F32)
                out.append((dk, dv))
            return tuple(out)

        zero = (jnp.zeros((TQ, QK_PAD), F32), jnp.zeros((TQ, V_HEAD), F32))
        carry = step(j, (zero,) * hp, True)
        carry = lax.fori_loop(j + 1, nq, lambda i, cr: step(i, cr, False), carry)
        for a, (dk, dv) in enumerate(carry):
            dkn_ref[:, cols(a)] = dk[:, :QK_NOPE].astype(BF16)
            dkr_ref[a] = dk[:, QK_NOPE:]
            dv_ref[:, cols(a)] = dv.astype(BF16)

        @pl.when(j == nq - 1)
        def _():
            for a in range(hp):
                dq = dq_acc[a]
                dq_ref[a, :, :QK_NOPE] = dq[:, :QK_NOPE].astype(BF16)
                dq_ref[a, :, QK_NOPE:] = _rotate(dq[:, QK_NOPE:], c_ref[...], sa_ref[...], sb_ref[...], -1.0).astype(BF16)

    row = pl.BlockSpec((hp, 1, T), lambda h, j: (h, 0, 0))
    head = pl.BlockSpec((TQ, hp * 128), lambda h, j: (j, h))
    whole = pl.BlockSpec((hp, T, QK_PAD), lambda h, j: (h, 0, 0))
    tab = pl.BlockSpec((T, 128), lambda h, j: (0, 0))
    heads = pl.BlockSpec((T, hp * V_HEAD), lambda h, j: (0, h))
    return _pallas(
        body, name="attn_bwd", grid=(N_HEADS // hp, nq),
        in_specs=[whole, head, pl.BlockSpec((TQ, 128), lambda h, j: (j, 0)), head, heads, heads, row, tab, tab, tab],
        out_specs=[whole, head, pl.BlockSpec((hp, TQ, 128), lambda h, j: (h, j, 0)), head],
        out_shape=[jax.ShapeDtypeStruct((N_HEADS, T, QK_PAD), BF16), jax.ShapeDtypeStruct((T, N_HEADS * QK_NOPE), BF16),
                   jax.ShapeDtypeStruct((N_HEADS, T, 128), F32), jax.ShapeDtypeStruct((T, N_HEADS * V_HEAD), BF16)],
        scratch_shapes=[pltpu.VMEM((hp, T, QK_PAD), F32), pltpu.VMEM((hp, 1, T), F32)],
        params=_params(("parallel", "arbitrary")))(q, kn, kr, v, o, do, lse_row, cos, sa, sb)


def _ffn_gup(name, dg, dv, hf):
    def body(dg_ref, dv_ref, hf_ref, o_ref):
        j = pl.program_id(0)

        @pl.when(j < N_FF_BLK)
        def _():
            o_ref[...] = lax.dot_general(dg_ref[...], hf_ref[...], TN, preferred_element_type=F32).astype(BF16)

        @pl.when(j >= N_FF_BLK)
        def _():
            o_ref[...] = lax.dot_general(dv_ref[...], hf_ref[...], TN, preferred_element_type=F32).astype(BF16)

    return _pallas(
        body, name=name, grid=(N_DEV,),
        in_specs=[pl.BlockSpec((None, T, FF_BLK), lambda j: (jnp.minimum(j, N_FF_BLK - 1), 0, 0)),
                  pl.BlockSpec((None, T, FF_BLK), lambda j: (jnp.maximum(j - N_FF_BLK, 0), 0, 0)),
                  pl.BlockSpec((T, D), lambda j: (0, 0))],
        out_specs=pl.BlockSpec((None, FF_BLK, D), lambda j: (j, 0, 0)),
        out_shape=jax.ShapeDtypeStruct((N_DEV, FF_BLK, D), BF16), params=_params(("parallel",)))(dg, dv, hf)


def _ffn_layer_fwd(tag, h, gain, ex):
    hf = _rms_fwd(f"{tag}_norm", h, gain)
    g, v, act = _ffn_up_act(f"{tag}_up", hf, ex.need(f"ffn_w_up{tag[1]}", hf), ex.need(f"ffn_cw{tag[1]}", hf),
                            ex.need(f"ffn_cb{tag[1]}", hf))
    ex.at(f"{tag}_up", act)
    rows = pl.BlockSpec((TS, D), lambda i: (i, 0))
    out = _mm_sum(f"{tag}_down",
                  [(act, pl.BlockSpec((N_FF_BLK, TS, FF_BLK), lambda i: (0, i, 0)), ex.need(f"ffn_w_down{tag[1]}", act),
                    pl.BlockSpec((None, N_FF_BLK, FF_BLK, D), lambda i: (0, 0, 0, 0)), NN, 0)],
                  grid=(T // TS,), o_spec=rows, o_shape=(T, D), o_dtype=F32, add=h)
    ex.at(f"{tag}_down", out)
    return out, (hf, g, v, act)


def _ffn_layer_bwd(tag, h, gain, ex, saved, dh, dh_bf):
    hf, g, v, act = saved
    layer = tag[1]
    w_up, w_down4 = ex.need(f"ffn_w_up{layer}", dh_bf), ex.need(f"ffn_w_down{layer}", dh_bf)
    dg, dv, dcw, dcb = _ffn_dact(f"{tag}_dact", dh_bf, w_down4, g, v, ex.need(f"ffn_cw{layer}", dh_bf),
                                 ex.need(f"ffn_cb{layer}", dh_bf))
    ex.at(f"{tag}_dact", dg)
    g_down = _mm(f"{tag}_gdown", act, dh_bf, grid=(N_FF_BLK,),
                 a_spec=pl.BlockSpec((None, T, FF_BLK), lambda j: (j, 0, 0)),
                 b_spec=pl.BlockSpec((T, D), lambda j: (0, 0)),
                 o_spec=pl.BlockSpec((FF_BLK, D), lambda j: (j, 0)),
                 o_shape=(D_FF, D), o_dtype=BF16, dims=TN)
    g_up = _ffn_gup(f"{tag}_gup", dg, dv, hf)
    ex.grad("ffn_w_up", int(layer), g_up.reshape(1, N_DEV, FF_BLK, D))
    ex.grad("ffn_w_down", int(layer), g_down.reshape(1, N_DEV, D_FF // N_DEV, D))
    ex.at(f"{tag}_gup", g_up)
    part = pl.BlockSpec((N_FF_BLK, TR, FF_BLK), lambda i: (0, i, 0))
    dh_in, dh_in_bf, dgain = _mm_sum(
        f"{tag}_dhf",
        [(dg, part, w_up, pl.BlockSpec((None, N_FF_BLK, FF_BLK, D), lambda i: (0, 0, 0, 0)), NN, 0),
         (dv, part, w_up, pl.BlockSpec((None, N_FF_BLK, FF_BLK, D), lambda i: (0, 1, 0, 0)), NN, 0)],
        grid=(T // TR,), o_spec=pl.BlockSpec((TR, D), lambda i: (i, 0)), o_shape=(T, D), o_dtype=F32,
        norm_bwd=(h, [gain], dh))
    ex.at(f"{tag}_dhf", dh_in)
    return dh_in, dh_in_bf, dgain[0], dcw, dcb


def _local_step(x, pos, tgt, rep, ex):
    attn_norm, ffn_norm, final_norm = rep["attn_norm"], rep["ffn_norm"], rep["final_norm"]
    half = QK_ROPE // 2
    inv = 1.0 / (ROPE_THETA ** (jnp.arange(half, dtype=F32) / half))
    inv_freq = jnp.concatenate([inv, inv, jnp.zeros((128 - 2 * half,), F32)]).reshape(1, 128)
    tables = _rope_tables(pos, inv_freq)

    hn0 = _rms_fwd("l0_norm", x, attn_norm[0:1])
    w_in = ex.need("sc_w_in", hn0)
    ex.at("mixer_ready", hn0)
    zb, zc, zu, y = _mixer_in(hn0, w_in, ex.need("sc_conv_w", hn0))
    ex.at("l0_in", y)
    h1 = _mm_rows("l0_out", y, ex.need("sc_w_out", y), NN, F32, D, tn=512, add=x)
    ex.at("l0_out", h1)
    h2, ffn0 = _ffn_layer_fwd("f0", h1, ffn_norm[0:1], ex)

    hk, hn1 = _rms_fwd2("h2_norms", h2, rep["kv_in_norm"], attn_norm[1:2])
    ckv_raw, ckv, kr = _kv_down(hk, ex.need("w_dkv", hk), ex.need("w_kr", hk), rep["kv_latent_norm"], tables)
    kn, vv = _kv_up(ckv, ex.need("w_uk", ckv), ex.need("w_uv", ckv))

    cq_raw, cq = _down_norm("q_down", hn1, ex.need("w_dq", hn1), rep["q_latent_norm"])
    w_uq = ex.need("w_uq", cq)
    q = _q_up(cq, w_uq, tables)
    o, lse = _attn_fwd(q, kn, kr, vv)
    ex.at("attn_fwd", o)
    w_o = ex.need("w_o", o)
    h3 = _mm_rows("attn_out", o, w_o, NN, F32, D, tn=512, add=h2)
    h4, ffn1 = _ffn_layer_fwd("f1", h3, ffn_norm[1:2], ex)

    loss, dh4, dh4_bf, d_final = _final(h4, final_norm.reshape(1, D), tgt)

    dh3, dh3_bf, d_fn1, dcw1, dcb1 = _ffn_layer_bwd("f1", h3, ffn_norm[1:2], ex, ffn1, dh4, dh4_bf)
    ex.at("f1_bwd", dh3)

    do = _mm_rows("d_attn_out", dh3_bf, w_o, NT, BF16, N_HEADS * V_HEAD)
    dq_pre, dkn, dkr, dvv = _attn_bwd(q, kn, kr, vv, o, do, lse, tables)

    def rows_of(a):
        return a[None], pl.BlockSpec((1, TS, a.shape[1]), lambda i: (0, i, 0))

    def whole(wt):
        return wt[None], pl.BlockSpec((1,) + wt.shape, lambda i: (0, 0, 0))

    def row_blocks(d):
        return dict(grid=(T // TS,), o_spec=pl.BlockSpec((TS, d), lambda i: (i, 0)), o_shape=(T, d), o_dtype=F32)

    _, dcq_raw_bf, (d_qln,) = _mm_sum(
        "d_q_up", [(dq_pre, pl.BlockSpec((N_HEADS, TS, QK_PAD), lambda i: (0, i, 0)),
                    w_uq, pl.BlockSpec((N_HEADS, QK_PAD, Q_LORA), lambda i: (0, 0, 0)), NN, 0)],
        norm_bwd=(cq_raw, [rep["q_latent_norm"]], None), **row_blocks(Q_LORA))
    g_uq, g_dq, g_o = _wgrads("g_q", [(dq_pre, cq), (hn1, dcq_raw_bf), (o, dh3_bf)])
    ex.grad("w_uq", None, g_uq[:, :QK_NOPE + QK_ROPE].reshape(1, N_DEV, QK_NOPE + QK_ROPE, Q_LORA))
    ex.grad("w_dq", None, g_dq.reshape(1, N_DEV, D // N_DEV, Q_LORA))
    ex.grad("w_o", None, g_o.reshape(1, N_DEV, D // N_DEV, D))

    _, dckv_raw_bf, (d_kvln,) = _mm_sum(
        "d_kv_up", [(*rows_of(dkn), *whole(ex.need("w_uk", dkn)), NT, 0),
                    (*rows_of(dvv), *whole(ex.need("w_uv", dvv)), NT, 0)],
        norm_bwd=(ckv_raw, [rep["kv_latent_norm"]], None), **row_blocks(KV_LORA))
    dkr_raw_bf = _rope("dk_rope", dkr, tables, -1.0, BF16, reduce_groups=True).reshape(T, 128)
    g_uk, g_uv, g_dkv, g_kr = _wgrads("g_kv", [(ckv, dkn), (ckv, dvv), (hk, dckv_raw_bf), (dkr_raw_bf, hk)])
    ex.grad("w_uk", None, g_uk)
    ex.grad("w_uv", None, g_uv)
    ex.grad("w_dkv", None, g_dkv.reshape(1, N_DEV, D // N_DEV, KV_LORA))
    ex.grad("w_kr", None, g_kr[:QK_ROPE])

    dh2, dh2_bf, (d_an1, d_kvin) = _mm_sum(
        "d_h2", [(*rows_of(dcq_raw_bf), *whole(ex.need("w_dq", dcq_raw_bf)), NT, 0),
                 (*rows_of(dckv_raw_bf), *whole(ex.need("w_dkv", dckv_raw_bf)), NT, 1),
                 (*rows_of(dkr_raw_bf), *whole(ex.need("w_kr", dkr_raw_bf)), NN, 1)],
        norm_bwd=(h2, [attn_norm[1:2], rep["kv_in_norm"]], dh3), **row_blocks(D))
    ex.at("kv_bwd", dh2)

    dh1, dh1_bf, d_fn0, dcw0, dcb0 = _ffn_layer_bwd("f0", h1, ffn_norm[0:1], ex, ffn0, dh2, dh2_bf)
    ex.at("f0_bwd", dh1)

    ex.grad("sc_w_out", None, _mm_wgrad("g_sc_w_out", y, dh1_bf).reshape(1, N_DEV, D // N_DEV, D))
    dz, d_scw = _mixer_out_bwd(dh1_bf, ex.need("sc_w_out", dh1_bf), zb, zc, zu, ex.need("sc_conv_w", dh1_bf))
    g_in = _mm_wgrad("g_sc_w_in", hn0, dz)
    ex.grad("sc_w_in", None, g_in)
    ex.at("sc_bwd", g_in)
    ex.at("d_l0_in", g_in)
    grad_x, _, (d_an0,) = _mm_sum(
        "d_l0_in", [(*rows_of(dz), *whole(ex.need("sc_w_in", dz)), NT, 0)],
        norm_bwd=(x, [attn_norm[0:1]], dh1), **row_blocks(D))

    small = {
        "attn_norm": jnp.concatenate([d_an0, d_an1], axis=0),
        "ffn_norm": jnp.concatenate([d_fn0, d_fn1], axis=0),
        "final_norm": d_final.reshape(D),
        "kv_in_norm": d_kvin.reshape(D),
        "kv_latent_norm": d_kvln.reshape(KV_LORA),
        "q_latent_norm": d_qln,
        "ffn_conv_b": jnp.stack([dcb0, dcb1]).transpose(0, 2, 1, 3).reshape(2, D_FF),
        "sc_conv_w": d_scw,
        "ffn_conv_w": jnp.stack([dcw0, dcw1]).transpose(0, 2, 1, 3).reshape(2, 3, D_FF),
    }
    return loss, grad_x, small


def _place():
    return lax.axis_index("x"), lax.axis_index("y"), lax.axis_index("c")


def _peers():
    x, y, c = _place()
    return (x, y, 1 - c), [(1 - x, y), (x, 1 - y), (1 - x, 1 - y)]


def _window(ref, kind, dev):
    if kind == "blocked":
        return ref.at[:, dev]
    width = ref.shape[-1] // N_DEV
    return ref.at[:, pl.ds(pl.multiple_of(dev * width, 128), width)]


HBM_SPEC = pl.BlockSpec(memory_space=pltpu.HBM)
SEM_SPEC = pl.BlockSpec(memory_space=pltpu.SEMAPHORE)
EFFECT = pltpu.SideEffectType.DATAFLOW_SIDE_EFFECTING
TOKEN = jax.ShapeDtypeStruct((8, 128), F32)


def _hbm(a):
    return pltpu.with_memory_space_constraint(a, pltpu.HBM)


def _copies_start(name, jobs):
    nj = len(jobs)
    counts = [(len(srcs), len(lands)) for srcs, lands, _, _ in jobs]
    n_arr = sum(ns + nl for ns, nl in counts)

    def body(*refs):
        sems, token = refs[n_arr:n_arr + 2 * nj], refs[-1]
        at = 0
        for j, ((ns, nl), (_, _, ncopy, plan)) in enumerate(zip(counts, jobs)):
            copies = plan(refs[at:at + ns], refs[at + ns:at + ns + nl])
            assert len(copies) == ncopy
            for k, (sent, dst, to, _) in enumerate(copies):
                pltpu.make_async_remote_copy(src_ref=sent, dst_ref=dst, send_sem=sems[2 * j].at[k],
                                             recv_sem=sems[2 * j + 1].at[k], device_id=to, device_id_type=MESH).start()
            at += ns + nl
        token[...] = jnp.zeros_like(token)

    arrays = [a for srcs, lands, _, _ in jobs for a in list(srcs) + list(lands)]
    sem_shapes = [pltpu.SemaphoreType.DMA((ncopy,)) for _, _, ncopy, _ in jobs for _ in range(2)]
    outs = pl.pallas_call(
        body, name=name, in_specs=[HBM_SPEC] * n_arr,
        out_specs=[SEM_SPEC] * (2 * nj) + [HBM_SPEC] * n_arr + [VMEM_SPEC],
        out_shape=sem_shapes + [pltpu.HBM(a.shape, a.dtype) for a in arrays] + [TOKEN],
        input_output_aliases={i: 2 * nj + i for i in range(n_arr)},
        compiler_params=pltpu.CompilerParams(has_side_effects=EFFECT))(*[_hbm(a) for a in arrays])
    _Chain.last = outs[-1]
    flights, at = [], 2 * nj
    for j, (ns, nl) in enumerate(counts):
        flights.append((outs[2 * j], outs[2 * j + 1], list(outs[at:at + ns]), list(outs[at + ns:at + ns + nl])))
        at += ns + nl
    return flights


def _copies_wait(name, started, ncopy, plan):
    send, recv, srcs, lands = started
    ns, nl = len(srcs), len(lands)

    def body(*refs):
        send_ref, recv_ref, token = refs[ns + nl], refs[ns + nl + 1], refs[-1]
        copies = plan(refs[:ns], refs[ns:ns + nl])
        assert len(copies) == ncopy
        for k, (sent, _, to, landed) in enumerate(copies):
            cp = pltpu.make_async_remote_copy(src_ref=sent, dst_ref=landed, send_sem=send_ref.at[k],
                                              recv_sem=recv_ref.at[k], device_id=to, device_id_type=MESH)
            cp.wait_send()
            cp.wait_recv()
        token[...] = jnp.zeros_like(token)

    arrays = list(srcs) + list(lands)
    outs = pl.pallas_call(
        body, name=name, in_specs=[HBM_SPEC] * (ns + nl) + [SEM_SPEC] * 2 + [ANY_SPEC],
        out_specs=[HBM_SPEC] * (ns + nl) + [VMEM_SPEC], out_shape=[pltpu.HBM(a.shape, a.dtype) for a in arrays] + [TOKEN],
        input_output_aliases={i: i for i in range(ns + nl)},
        compiler_params=pltpu.CompilerParams(has_side_effects=EFFECT))(*arrays, send, recv, _Chain.last)
    _Chain.last = outs[-1]
    return list(outs[:ns]), list(outs[ns:-1])


def _plan_gather_chips(kinds):
    def plan(srcs, lands):
        x, y, c = _place()
        sibling, chips = _peers()
        out = []
        for t, kind in enumerate(kinds):
            mine = _window(lands[t], kind, 4 * x + 2 * y + c)
            out.append((srcs[t], mine, (x, y, c), mine))
            out.append((srcs[t], mine, sibling, _window(lands[t], kind, 4 * x + 2 * y + 1 - c)))
            for px, py in chips:
                out.append((srcs[t], mine, (px, py, c), _window(lands[t], kind, 4 * px + 2 * py + c)))
        return out
    return plan, 5 * len(kinds)


def _plan_gather_all(n):
    def plan(srcs, lands):
        x, y, c = _place()
        out = []
        for t in range(n):
            mine = lands[t].at[:, 4 * x + 2 * y + c]
            for m in range(N_DEV):
                px, py, pc = (1 - x if m & 4 else x), (1 - y if m & 2 else y), (1 - c if m & 1 else c)
                out.append((srcs[t], mine, (px, py, pc), lands[t].at[:, 4 * px + 2 * py + pc]))
        return out
    return plan, N_DEV * n


def _plan_gather_sibling(kinds):
    def plan(srcs, lands):
        _, _, c = _place()
        sibling, chips = _peers()
        out = []
        for t, kind in enumerate(kinds):
            for px, py in chips:
                w = _window(lands[t], kind, 4 * px + 2 * py + c)
                out.append((w, w, sibling, _window(lands[t], kind, 4 * px + 2 * py + 1 - c)))
        return out
    return plan, 3 * len(kinds)


def _plan_scatter_sibling(kinds):
    def plan(srcs, lands):
        _, _, c = _place()
        sibling, _ = _peers()
        out = []
        for t, kind in enumerate(kinds):
            for k in range(N_CHIP):
                out.append((_window(srcs[t], kind, 2 * k + 1 - c), lands[t].at[k], sibling, lands[t].at[k]))
        return out
    return plan, N_CHIP * len(kinds)


def _plan_scatter_chips(n):
    def plan(srcs, lands):
        x, y, c = _place()
        _, chips = _peers()
        out = []
        for t in range(n):
            for px, py in chips:
                out.append((srcs[t].at[2 * px + py], lands[t].at[2 * x + y], (px, py, c), lands[t].at[2 * px + py]))
        return out
    return plan, 3 * n


def _landing(shard, kind):
    if kind == "blocked":
        return lax.empty((shard.shape[0], N_DEV) + shard.shape[1:], shard.dtype)
    return lax.empty((shard.shape[0], N_DEV * shard.shape[1]), shard.dtype)


def _chip_sums(name, grads, kinds, recvs, c):
    n = len(grads)
    in_specs, out_specs, out_shape, args = [], [], [], []
    for gr, kind, rv in zip(grads, kinds, recvs):
        if kind == "blocked":
            rows, w = gr.shape[2], gr.shape[3]
            in_specs.append(pl.BlockSpec((None, None, rows, w), lambda k, cref: (0, 2 * k + cref[0], 0, 0)))
        else:
            rows, w = gr.shape[0], gr.shape[1] // N_DEV
            in_specs.append(pl.BlockSpec((rows, w), lambda k, cref: (0, 2 * k + cref[0])))
        blk = pl.BlockSpec((None, rows, w), lambda k, cref: (k, 0, 0))
        in_specs.append(blk)
        out_specs.append(blk)
        out_shape.append(jax.ShapeDtypeStruct((N_CHIP, rows, w), BF16))
        args += [gr, rv.reshape(N_CHIP, rows, w)]

    def body(*refs):
        for t in range(n):
            g_ref, r_ref, o_ref = refs[1 + 2 * t], refs[2 + 2 * t], refs[1 + 2 * n + t]
            o_ref[...] = (g_ref[...].astype(F32) + r_ref[...].astype(F32)).astype(BF16)

    return _pallas(body, name=name, n_prefetch=1, grid=(N_CHIP,), in_specs=in_specs, out_specs=out_specs,
                   out_shape=out_shape, params=_params(("parallel",)))(c, *args)


def _adamw_math(g, wv, mv, vv):
    m = ADAM_B1 * mv + (1.0 - ADAM_B1) * g
    v = ADAM_B2 * vv + (1.0 - ADAM_B2) * (g * g)
    m_hat = m / (1.0 - ADAM_B1 ** ADAM_STEP)
    v_hat = v / (1.0 - ADAM_B2 ** ADAM_STEP)
    delta = -ADAM_LR * (m_hat / (jnp.sqrt(v_hat) + ADAM_EPS) + ADAM_WD * wv)
    return delta, m, v


ADAM_STEPS = 2


def _adamw_group(name, items, chip_ids):
    n = len(items)
    in_specs, out_specs, out_shape, args, prevs = [], [], [], [chip_ids], []
    for own, recv, w3, m3, v3, layer, _ in items:
        nl, rows, w = w3.shape
        tr = rows // ADAM_STEPS
        assert tr % 16 == 0, (name, rows)
        in_specs += [pl.BlockSpec((None, tr, w), lambda i, ids, slot=slot: (ids[slot], i, 0)) for slot in range(4)]
        slab = pl.BlockSpec((None, tr, w), lambda i, ids, layer=layer: (layer, i, 0))
        in_specs += [slab] * 3
        out_specs += [slab] * 4
        out_shape += [jax.ShapeDtypeStruct((nl, rows, w), F32)] * 4
        args += [own, recv, recv, recv, w3, m3, v3]
    aliases = {}
    for t, item in enumerate(items):
        if item[6] is not None:
            for k in range(4):
                aliases[len(args) + k] = 4 * t + k
            in_specs += [ANY_SPEC] * 4
            args += list(item[6])
            prevs.append(t)
    n_in = 1 + 7 * n + 4 * len(prevs)

    def body(*refs):
        for t in range(n):
            own_ref, r1_ref, r2_ref, r3_ref, w_ref, m_ref, v_ref = refs[1 + 7 * t:8 + 7 * t]
            g_ref, d_ref, nm_ref, nv_ref = refs[n_in + 4 * t:n_in + 4 * t + 4]
            g = ((own_ref[...].astype(F32) + r1_ref[...].astype(F32)) + r2_ref[...].astype(F32)) + r3_ref[...].astype(F32)
            g_ref[...] = g
            d_ref[...], nm_ref[...], nv_ref[...] = _adamw_math(g, w_ref[...], m_ref[...], v_ref[...])

    outs = _pallas(body, name=name, n_prefetch=1, grid=(ADAM_STEPS,), in_specs=in_specs, out_specs=out_specs,
                   out_shape=out_shape, aliases=aliases, params=_params(("parallel",)))(*args)
    return [list(outs[4 * t:4 * t + 4]) for t in range(n)]


def _adamw_small(gathered, ws, ms, vs):
    n = len(gathered)
    full = [w is not None for w in ws]
    args = list(gathered)
    out_shape = []
    for t in range(n):
        shape = jax.ShapeDtypeStruct(gathered[t].shape[2:], F32)
        if full[t]:
            args += [ws[t], ms[t], vs[t]]
            out_shape += [shape] * 4
        else:
            out_shape += [shape]

    def body(*refs):
        i_in, i_out = n, len(args)
        for t in range(n):
            p_ref = refs[t]
            g = p_ref[0, 0]
            for k in range(1, N_DEV):
                g = g + p_ref[0, k]
            refs[i_out][...] = g
            if full[t]:
                w_ref, m_ref, v_ref = refs[i_in:i_in + 3]
                refs[i_out + 1][...], refs[i_out + 2][...], refs[i_out + 3][...] = _adamw_math(
                    g, w_ref[...], m_ref[...], v_ref[...])
                i_in += 3
                i_out += 4
            else:
                i_out += 1

    outs = _pallas(body, name="adamw_small", in_specs=[VMEM_SPEC] * len(args), out_specs=[VMEM_SPEC] * len(out_shape),
                   out_shape=out_shape, params=pltpu.CompilerParams(vmem_limit_bytes=VMEM_LIMIT))(*args)
    result, i = [], 0
    for t in range(n):
        k = 4 if full[t] else 1
        result.append(list(outs[i:i + k]))
        i += k
    return result


def _adamw_plain(name, gs, ws, ms, vs):
    n = len(gs)

    def body(*refs):
        for t in range(n):
            g_ref, w_ref, m_ref, v_ref = refs[4 * t:4 * t + 4]
            outs = refs[4 * n + 3 * t:4 * n + 3 * t + 3]
            outs[0][...], outs[1][...], outs[2][...] = _adamw_math(g_ref[...], w_ref[...], m_ref[...], v_ref[...])

    args, out_shape = [], []
    for g, w, m, v in zip(gs, ws, ms, vs):
        args += [g, w, m, v]
        out_shape += [jax.ShapeDtypeStruct(w.shape, F32)] * 3
    outs = _pallas(body, name=name, in_specs=[VMEM_SPEC] * len(args), out_specs=[VMEM_SPEC] * len(out_shape),
                   out_shape=out_shape, params=pltpu.CompilerParams(vmem_limit_bytes=VMEM_LIMIT))(*args)
    return [list(outs[3 * t:3 * t + 3]) for t in range(n)]


KIND = {"sc_w_in": "cols", "sc_w_out": "blocked", "w_dkv": "blocked", "w_kr": "cols", "w_uk": "cols", "w_uv": "cols",
        "w_dq": "blocked", "w_uq": "blocked", "w_o": "blocked", "ffn_w_up": "blocked", "ffn_w_down": "blocked",
        "conv": "blocked"}
GATHER_GROUPS = (("mixer", ("sc_w_in", "sc_w_out", "conv")),
                 ("up0", ("ffn_w_up0",)),
                 ("down0", ("ffn_w_down0",)),
                 ("attn", ("w_dkv", "w_kr", "w_uk", "w_uv", "w_dq", "w_uq", "w_o")),
                 ("ffn1", ("ffn_w_up1", "ffn_w_down1")))
SCATTER_GROUPS = (("ffn1", (("ffn_w_up", 1), ("ffn_w_down", 1))),
                  ("attn", (("w_o", None), ("w_uq", None), ("w_dq", None), ("w_uk", None), ("w_uv", None),
                            ("w_dkv", None), ("w_kr", None))),
                  ("ffn0", (("ffn_w_up", 0), ("ffn_w_down", 0))),
                  ("mixer", (("sc_w_out", None), ("sc_w_in", None))))
SCHEDULE = {
    "begin": (("gather_start", "mixer"),),
    "mixer_ready": (("gather_start", "up0"),),
    "l0_out": (("gather_forward", "up0"), ("gather_start", "down0")),
    "f0_up": (("gather_forward", "down0"), ("gather_start", "attn")),
    "f0_down": (("gather_forward", "attn"), ("gather_start", "ffn1")),
    "attn_fwd": (("gather_forward", "ffn1"),),
    "f1_gup": (("scatter_sibling", "ffn1"),),
    "f1_dhf": (("scatter_chips", "ffn1"),),
    "kv_bwd": (("scatter_sibling", "attn"), ("scatter_done", "ffn1")),
    "f0_dact": (("scatter_chips", "attn"),),
    "f0_gup": (("scatter_sibling", "ffn0"),),
    "f0_dhf": (("scatter_chips", "ffn0"),),
    "f0_bwd": (("scatter_done", "attn"),),
    "sc_bwd": (("scatter_sibling", "mixer"),),
    "d_l0_in": (("scatter_chips", "mixer"),),
}
FINISH = (("scatter_done", "ffn0"), ("scatter_done", "mixer"))
STAGES = {"gather_start": 1, "gather_forward": 2, "gather_done": 3,
          "scatter_sibling": 1, "scatter_chips": 2, "scatter_done": 3}
SMALL_W_ROWS = 24


def _pack(arrays, rows):
    flat = jnp.concatenate([a.reshape(-1).astype(F32) for a in arrays])
    return jnp.pad(flat, (0, rows * 128 - flat.shape[0])).reshape(rows, 128)


STORED_TRANSPOSED = ("ffn_w_up", "w_uq", "w_kr")


def _stored(name, a):
    return jnp.swapaxes(a, -1, -2) if name in STORED_TRANSPOSED else a


def _base(name):
    if name.startswith("ffn_w_") and name[-1] in "01":
        return name[:-1], int(name[-1])
    return name, None


class _Exchange:
    def __init__(self, wts, mom, var, ffn_conv_b):
        self.wts, self.mom, self.var = wts, mom, var
        x, y, c = _place()
        self.c_arr = jnp.reshape(c, (1,)).astype(jnp.int32)
        chip = 2 * x + y
        self.chip_ids = jnp.stack([chip, chip ^ 1, chip ^ 2, chip ^ 3]).astype(jnp.int32)
        self.ready = {"ffn_cb0": ffn_conv_b.reshape(2, N_FF_BLK, 1, FF_BLK)[0],
                      "ffn_cb1": ffn_conv_b.reshape(2, N_FF_BLK, 1, FF_BLK)[1]}
        self.gathers, self.group_of = {}, {}
        self.grads, self.scatters, self.results, self.queue = {}, {}, {}, []
        for gname, names in GATHER_GROUPS:
            self.gathers[gname] = dict(stage=0, names=names, kinds=[KIND[_base(nm)[0]] for nm in names])
            for nm in names:
                self.group_of[nm] = gname
        for nm in ("sc_conv_w", "ffn_cw0", "ffn_cw1"):
            self.group_of[nm] = "mixer"
        self.at("begin", None)

    def _shard(self, name):
        if name == "conv":
            return _pack([self.wts["sc_conv_w"], self.wts["ffn_conv_w"]], SMALL_W_ROWS).reshape(1, SMALL_W_ROWS, 128)
        base, layer = _base(name)
        a = _stored(base, self.wts[base])
        if layer is not None:
            a = a[layer:layer + 1]
        if KIND[base] == "cols":
            return a.reshape(a.shape[-2], a.shape[-1]).astype(BF16)
        return a.reshape((-1,) + a.shape[-2:]).astype(BF16)

    def _start(self, name, srcs, lands, ncopy, plan, st):
        self.queue.append((name, (srcs, lands, ncopy, plan), st))

    def _flush(self):
        if self.queue:
            flights = _copies_start("__".join(name for name, _, _ in self.queue), [job for _, job, _ in self.queue])
            for (_, _, st), flight in zip(self.queue, flights):
                st["flight"] = flight
            self.queue = []

    def _flight(self, st):
        self._flush()
        return st["flight"]

    def _gather_to(self, gname, stage, after):
        st = self.gathers[gname]
        if st["stage"] < 1 <= stage:
            shards = [self._shard(nm) for nm in st["names"]]
            lands = [_landing(s, kind) for s, kind in zip(shards, st["kinds"])]
            plan, ncopy = _plan_gather_chips(st["kinds"])
            self._start(f"ag_{gname}_chips", shards, lands, ncopy, plan, st)
            st["stage"] = 1
        if st["stage"] < 2 <= stage:
            plan, ncopy = _plan_gather_chips(st["kinds"])
            _, lands = _copies_wait(f"ag_{gname}_chips_wait", self._flight(st), ncopy, plan)
            plan, ncopy = _plan_gather_sibling(st["kinds"])
            self._start(f"ag_{gname}_sibling", [], lands, ncopy, plan, st)
            st["stage"] = 2
        if st["stage"] < 3 <= stage:
            plan, ncopy = _plan_gather_sibling(st["kinds"])
            _, lands = _copies_wait(f"ag_{gname}_sibling_wait", self._flight(st), ncopy, plan)
            for nm, land in zip(st["names"], lands):
                self._arrived(nm, land)
            st["stage"] = 3

    def _arrived(self, name, land):
        if name == "conv":
            conv = land.reshape(N_DEV, SMALL_W_ROWS * 128)
            self.ready["sc_conv_w"] = conv[:, :3 * 128].reshape(N_DEV, 3, 128).transpose(1, 0, 2).reshape(3, D)
            fcw = conv[:, 3 * 128:3 * 128 + 6 * 352].reshape(N_DEV, 2, 3, 352).transpose(1, 2, 0, 3)
            fcw = fcw.reshape(2, 3, N_FF_BLK, FF_BLK).transpose(0, 2, 1, 3)
            self.ready["ffn_cw0"], self.ready["ffn_cw1"] = fcw[0], fcw[1]
        elif name in ("sc_w_in", "w_uk", "w_uv") or name.startswith("ffn_w_up"):
            self.ready[name] = land
        elif name.startswith("ffn_w_down"):
            self.ready[name] = land.reshape(1, N_FF_BLK, FF_BLK, D)
        elif name == "w_kr":
            self.ready[name] = jnp.pad(land, ((0, 128 - QK_ROPE), (0, 0)))
        elif name == "w_uq":
            self.ready[name] = jnp.pad(land.reshape(N_HEADS, QK_NOPE + QK_ROPE, Q_LORA),
                                       ((0, 0), (0, QK_PAD - QK_NOPE - QK_ROPE), (0, 0)))
        else:
            self.ready[name] = land.reshape(D, land.shape[-1])

    def need(self, name, after):
        if name not in self.ready:
            self._gather_to(self.group_of[name], 3, after)
            self._flush()
        return self.ready[name]

    def grad(self, name, layer, array):
        self.grads[(name, layer)] = array

    def _scatter_to(self, gname, stage, after):
        keys = dict(SCATTER_GROUPS)[gname]
        st = self.scatters.setdefault(gname, dict(stage=0))
        kinds = [KIND[nm] for nm, _ in keys]
        if st["stage"] < 1 <= stage:
            grads = [self.grads[key] for key in keys]
            lands = []
            for gr, kind in zip(grads, kinds):
                shard = (gr.shape[0],) + gr.shape[2:] if kind == "blocked" else (gr.shape[0], gr.shape[1] // N_DEV)
                lands.append(lax.empty((N_CHIP,) + shard, BF16))
            plan, ncopy = _plan_scatter_sibling(kinds)
            self._start(f"rs_{gname}_sibling", grads, lands, ncopy, plan, st)
            st["stage"] = 1
        if st["stage"] < 2 <= stage:
            plan, ncopy = _plan_scatter_sibling(kinds)
            grads, recvs = _copies_wait(f"rs_{gname}_sibling_wait", self._flight(st), ncopy, plan)
            sums = _chip_sums(f"rs_{gname}_sums", grads, kinds, recvs, self.c_arr)
            lands = [lax.empty(s.shape, BF16) for s in sums]
            plan, ncopy = _plan_scatter_chips(len(sums))
            self._start(f"rs_{gname}_chips", sums, lands, ncopy, plan, st)
            st["stage"] = 2
        if st["stage"] < 3 <= stage:
            plan, ncopy = _plan_scatter_chips(len(keys))
            sums, recvs = _copies_wait(f"rs_{gname}_chips_wait", self._flight(st), ncopy, plan)
            items = []
            for (nm, layer), own, rv in zip(keys, sums, recvs):
                nl = 1 if layer is None else 2
                rows, w = own.shape[1], own.shape[2]
                w3, m3, v3 = (_stored(nm, src[nm]).reshape(nl, rows, w) for src in (self.wts, self.mom, self.var))
                items.append((own, rv, w3, m3, v3, 0 if layer is None else layer, self.results.get(nm)))
            outs = _adamw_group(f"adamw_{gname}", items, self.chip_ids)
            for (nm, _), out in zip(keys, outs):
                self.results[nm] = out
            st["stage"] = 3

    def at(self, place, after):
        for action, gname in SCHEDULE.get(place, ()):
            self._advance(action, gname, after)
        self._flush()

    def _advance(self, action, gname, after):
        if action.startswith("gather"):
            self._gather_to(gname, STAGES[action], after)
        else:
            self._scatter_to(gname, STAGES[action], after)

    def finish(self, after):
        for action, gname in FINISH:
            self._advance(action, gname, after)
        for gname, _ in SCATTER_GROUPS:
            self._scatter_to(gname, 3, after)
        return {nm: [_stored(nm, o.reshape(_stored(nm, self.wts[nm]).shape)) for o in outs]
                for nm, outs in self.results.items()}


REPLICATED = ("attn_norm", "ffn_norm", "final_norm", "kv_in_norm", "kv_latent_norm", "q_latent_norm", "ffn_conv_b")
WEIGHTS = ("attn_norm", "ffn_norm", "final_norm", "sc_w_in", "sc_conv_w", "sc_w_out", "kv_in_norm", "w_dkv",
           "kv_latent_norm", "w_kr", "w_uk", "w_uv", "w_dq", "q_latent_norm", "w_uq", "w_o", "ffn_w_up", "ffn_conv_w",
           "ffn_conv_b", "ffn_w_down")


def kernel(x, positions, attn_norm, ffn_norm, final_norm, sc_w_in, sc_conv_w, sc_w_out, kv_in_norm, w_dkv, kv_latent_norm, w_kr, w_uk, w_uv, w_dq, q_latent_norm, w_uq, w_o, ffn_w_up, ffn_conv_w, ffn_conv_b, ffn_w_down, loss_target, m_attn_norm, m_ffn_norm, m_final_norm, m_sc_w_in, m_sc_conv_w, m_sc_w_out, m_kv_in_norm, m_w_dkv, m_kv_latent_norm, m_w_kr, m_w_uk, m_w_uv, m_w_dq, m_q_latent_norm, m_w_uq, m_w_o, m_ffn_w_up, m_ffn_conv_w, m_ffn_conv_b, m_ffn_w_down, v_attn_norm, v_ffn_norm, v_final_norm, v_sc_w_in, v_sc_conv_w, v_sc_w_out, v_kv_in_norm, v_w_dkv, v_kv_latent_norm, v_w_kr, v_w_uk, v_w_uv, v_w_dq, v_q_latent_norm, v_w_uq, v_w_o, v_ffn_w_up, v_ffn_conv_w, v_ffn_conv_b, v_ffn_w_down):
    wts = dict(attn_norm=attn_norm, ffn_norm=ffn_norm, final_norm=final_norm, sc_w_in=sc_w_in, sc_conv_w=sc_conv_w,
               sc_w_out=sc_w_out, kv_in_norm=kv_in_norm, w_dkv=w_dkv, kv_latent_norm=kv_latent_norm, w_kr=w_kr,
               w_uk=w_uk, w_uv=w_uv, w_dq=w_dq, q_latent_norm=q_latent_norm, w_uq=w_uq, w_o=w_o, ffn_w_up=ffn_w_up,
               ffn_conv_w=ffn_conv_w, ffn_conv_b=ffn_conv_b, ffn_w_down=ffn_w_down)
    mom = dict(attn_norm=m_attn_norm, ffn_norm=m_ffn_norm, final_norm=m_final_norm, sc_w_in=m_sc_w_in,
               sc_conv_w=m_sc_conv_w, sc_w_out=m_sc_w_out, kv_in_norm=m_kv_in_norm, w_dkv=m_w_dkv,
               kv_latent_norm=m_kv_latent_norm, w_kr=m_w_kr, w_uk=m_w_uk, w_uv=m_w_uv, w_dq=m_w_dq,
               q_latent_norm=m_q_latent_norm, w_uq=m_w_uq, w_o=m_w_o, ffn_w_up=m_ffn_w_up, ffn_conv_w=m_ffn_conv_w,
               ffn_conv_b=m_ffn_conv_b, ffn_w_down=m_ffn_w_down)
    var = dict(attn_norm=v_attn_norm, ffn_norm=v_ffn_norm, final_norm=v_final_norm, sc_w_in=v_sc_w_in,
               sc_conv_w=v_sc_conv_w, sc_w_out=v_sc_w_out, kv_in_norm=v_kv_in_norm, w_dkv=v_w_dkv,
               kv_latent_norm=v_kv_latent_norm, w_kr=v_w_kr, w_uk=v_w_uk, w_uv=v_w_uv, w_dq=v_w_dq,
               q_latent_norm=v_q_latent_norm, w_uq=v_w_uq, w_o=v_w_o, ffn_w_up=v_ffn_w_up, ffn_conv_w=v_ffn_conv_w,
               ffn_conv_b=v_ffn_conv_b, ffn_w_down=v_ffn_w_down)
    xi, yi, ci = _place()
    me = 4 * xi + 2 * yi + ci
    _Chain.last = None

    ex = _Exchange(wts, mom, var, ffn_conv_b)
    rep = {
        "attn_norm": attn_norm, "ffn_norm": ffn_norm, "final_norm": final_norm,
        "kv_in_norm": kv_in_norm.reshape(1, D), "kv_latent_norm": kv_latent_norm.reshape(1, KV_LORA),
        "q_latent_norm": q_latent_norm.reshape(1, Q_LORA),
    }
    loss, grad_x, small = _local_step(x.reshape(T, D), positions.reshape(T, 1), loss_target.reshape(T, D), rep, ex)

    def rows_of(a):
        return a.reshape(-1, a.shape[-1])

    small_order = list(REPLICATED) + ["sc_conv_w", "ffn_conv_w"]
    shards = [loss.reshape(1, 1, 128)] + [rows_of(small[nm])[None] for nm in small_order]
    plan, ncopy = _plan_gather_all(len(shards))
    flight, = _copies_start("ag_small", [(shards, [lax.empty((1, N_DEV) + s.shape[1:], F32) for s in shards], ncopy, plan)])
    results = ex.finish(grad_x)
    _, gathered = _copies_wait("ag_small_wait", flight, ncopy, plan)
    params = [[None] + [rows_of(src[nm]) for nm in REPLICATED] + [None, None] for src in (wts, mom, var)]
    summed = _adamw_small(gathered, *params)
    loss_total = summed[0][0][0, 0]
    for nm, vals in zip(REPLICATED, summed[1:1 + len(REPLICATED)]):
        results[nm] = [a.reshape(wts[nm].shape) for a in vals]
    g_scw = lax.dynamic_slice(summed[-2][0], (0, me * 128), (3, 128))
    g_fcw = lax.dynamic_slice(summed[-1][0], (0, me * 352), (6, 352))
    conv = _adamw_plain("adamw_conv", [g_scw, g_fcw], *[[rows_of(src["sc_conv_w"]), rows_of(src["ffn_conv_w"])]
                                                        for src in (wts, mom, var)])
    for nm, g_own, vals in zip(("sc_conv_w", "ffn_conv_w"), (g_scw, g_fcw), conv):
        results[nm] = [a.reshape(wts[nm].shape) for a in [g_own] + vals]

    outs = [loss_total, grad_x.reshape(1, T, D)]
    for slot in range(4):
        outs.extend(results[nm][slot] for nm in WEIGHTS)
    return tuple(outs)
```

```python
import jax
import jax.numpy as jnp
from jax import lax
from jax.experimental import pallas as pl
from jax.experimental.pallas import tpu as pltpu

F32 = jnp.float32
BF16 = jnp.bfloat16

T = 2048
D = 1024
N_HEADS = 8
QK_NOPE = 128
QK_ROPE = 64
V_HEAD = 128
Q_LORA = 384
KV_LORA = 256
D_FF = 2816
CHUNK = 64
ROPE_THETA = 10000.0
EPS = 1e-6
NEG_INF = -1e30
ADAM_LR = 0.001
ADAM_B1 = 0.9
ADAM_B2 = 0.999
ADAM_EPS = 1e-08
ADAM_WD = 0.01
ADAM_STEP = 10

N_DEV = 8
N_CHIP = 4
FF_BLK = D_FF * 2 // N_DEV
N_FF_BLK = D_FF // FF_BLK
QK_PAD = 256
HALO = 16

TM = 1024
TS = 512
TR = 256
TQ = 512
VMEM_LIMIT = 56 * 1024 * 1024

NN = (((1,), (0,)), ((), ()))
NT = (((1,), (1,)), ((), ()))
TN = (((0,), (0,)), ((), ()))
MESH = pl.DeviceIdType.MESH


def _params(sem):
    return pltpu.CompilerParams(dimension_semantics=sem, vmem_limit_bytes=VMEM_LIMIT)


ANY_SPEC = pl.BlockSpec(memory_space=pl.ANY)
VMEM_SPEC = pl.BlockSpec(memory_space=pltpu.VMEM)


class _Chain:
    last = None


def _pallas(body, *, name, in_specs, out_specs, out_shape, grid=(), scratch_shapes=(), n_prefetch=0, aliases=None,
            params=None):
    def run(*args):
        after = _Chain.last
        n_lead = len(args)
        specs, operands, fn = list(in_specs), list(args), body
        if after is not None:
            def fn(*refs):
                return body(*refs[:n_lead], *refs[n_lead + 1:])
            specs.append(ANY_SPEC)
            operands.append(after)
        kw = dict(name=name, out_shape=out_shape, input_output_aliases=aliases or {})
        if params is not None:
            kw["compiler_params"] = params
        if n_prefetch:
            kw["grid_spec"] = pltpu.PrefetchScalarGridSpec(
                num_scalar_prefetch=n_prefetch, grid=grid, in_specs=specs, out_specs=out_specs,
                scratch_shapes=scratch_shapes)
        else:
            kw.update(grid=grid, in_specs=specs, out_specs=out_specs, scratch_shapes=scratch_shapes)
        outs = pl.pallas_call(fn, **kw)(*operands)
        _Chain.last = outs[0] if isinstance(outs, (list, tuple)) else outs
        return outs
    return run


def _mm(name, a, b, *, grid, a_spec, b_spec, o_spec, o_shape, o_dtype, dims, k_axis=None, acc_shape=None,
        add=None, add_spec=None):
    nk = grid[k_axis] if k_axis is not None else 1
    has_add = add is not None

    def body(*refs):
        a_ref, b_ref = refs[0], refs[1]
        p = 2
        add_ref = None
        if has_add:
            add_ref = refs[p]
            p += 1
        o_ref = refs[p]
        p += 1
        r = lax.dot_general(a_ref[...].astype(BF16), b_ref[...].astype(BF16), dims, preferred_element_type=F32)
        if k_axis is None:
            if has_add:
                r = r + add_ref[...].astype(F32)
            o_ref[...] = r.astype(o_dtype)
        else:
            acc = refs[p]
            k = pl.program_id(k_axis)

            @pl.when(k == 0)
            def _():
                acc[...] = r

            @pl.when(k > 0)
            def _():
                acc[...] += r

            @pl.when(k == nk - 1)
            def _():
                t = acc[...]
                if has_add:
                    t = t + add_ref[...].astype(F32)
                o_ref[...] = t.astype(o_dtype)

    in_specs = [a_spec, b_spec]
    args = [a, b]
    if has_add:
        in_specs.append(add_spec if add_spec is not None else o_spec)
        args.append(add)
    sem = tuple("arbitrary" if ax == k_axis else "parallel" for ax in range(len(grid)))
    scratch = [pltpu.VMEM(acc_shape, F32)] if k_axis is not None else []
    return _pallas(body, name=name, grid=grid, in_specs=in_specs, out_specs=o_spec,
                   out_shape=jax.ShapeDtypeStruct(o_shape, o_dtype), scratch_shapes=scratch, params=_params(sem))(*args)


def _mm_sum(name, parts, *, grid, o_spec, o_shape, o_dtype, add=None, norm_bwd=None):
    has_add = add is not None
    np_ = len(parts)
    nn = 1 if norm_bwd is None else len(norm_bwd[1])
    has_res = norm_bwd is not None and norm_bwd[2] is not None

    def body(*refs):
        accs = [None] * nn
        for p, (_, _, _, _, dims, n) in enumerate(parts):
            a_ref, b_ref = refs[2 * p], refs[2 * p + 1]
            for k in range(a_ref.shape[0]):
                r = lax.dot_general(a_ref[k], b_ref[k], dims, preferred_element_type=F32)
                accs[n] = r if accs[n] is None else accs[n] + r
        if norm_bwd is None:
            acc = accs[0]
            if has_add:
                acc = acc + refs[2 * np_][...]
            refs[-1][...] = acc.astype(o_dtype)
            return
        x_ref, g_refs = refs[2 * np_], refs[2 * np_ + 1:2 * np_ + 1 + nn]
        dx_ref, dxb_ref, dg_refs = refs[-2 - nn], refs[-1 - nn], refs[-nn:]
        xv = x_ref[...]
        r = lax.rsqrt(jnp.mean(xv * xv, axis=-1, keepdims=True) + EPS)
        xn = xv * r
        dx = refs[2 * np_ + 1 + nn][...] if has_res else None
        sums = []
        for acc, g_ref in zip(accs, g_refs):
            gdy = acc * g_ref[...]
            t = r * (gdy - xn * jnp.mean(gdy * xn, axis=-1, keepdims=True))
            dx = t if dx is None else dx + t
            sums.append(jnp.sum(acc * xn, axis=0, keepdims=True))
        dx_ref[...] = dx
        dxb_ref[...] = dx.astype(BF16)

        @pl.when(pl.program_id(0) == 0)
        def _():
            for dg_ref, part in zip(dg_refs, sums):
                dg_ref[...] = part

        @pl.when(pl.program_id(0) > 0)
        def _():
            for dg_ref, part in zip(dg_refs, sums):
                dg_ref[...] += part

    in_specs, args = [], []
    for a, a_spec, b, b_spec, _, _ in parts:
        in_specs += [a_spec, b_spec]
        args += [a, b]
    if norm_bwd is None:
        if has_add:
            in_specs.append(o_spec)
            args.append(add)
        return _pallas(body, name=name, grid=grid, in_specs=in_specs, out_specs=o_spec,
                       out_shape=jax.ShapeDtypeStruct(o_shape, o_dtype),
                       params=_params(("parallel",) * len(grid)))(*args)
    x, gains, dres = norm_bwd
    vec = pl.BlockSpec((1, o_shape[1]), lambda i: (0, 0))
    in_specs += [o_spec] + [vec] * nn + ([o_spec] if has_res else [])
    args += [x] + list(gains) + ([dres] if has_res else [])
    outs = _pallas(body, name=name, grid=grid, in_specs=in_specs, out_specs=[o_spec, o_spec] + [vec] * nn,
                   out_shape=[jax.ShapeDtypeStruct(o_shape, F32), jax.ShapeDtypeStruct(o_shape, BF16)]
                   + [jax.ShapeDtypeStruct((1, o_shape[1]), F32)] * nn,
                   params=_params(("arbitrary",)))(*args)
    return outs[0], outs[1], list(outs[2:])


def _mm_rows(name, a, b, dims, o_dtype, n_out, *, tn=None, add=None):
    k = a.shape[1]
    tn = n_out if tn is None else tn
    if dims == NN:
        b_spec = pl.BlockSpec((k, tn), lambda n, i: (0, n))
    else:
        b_spec = pl.BlockSpec((tn, k), lambda n, i: (n, 0))
    return _mm(name, a, b, grid=(n_out // tn, T // TM),
               a_spec=pl.BlockSpec((TM, k), lambda n, i: (i, 0)), b_spec=b_spec,
               o_spec=pl.BlockSpec((TM, tn), lambda n, i: (i, n)), o_shape=(T, n_out), o_dtype=o_dtype,
               dims=dims, add=add)


def _wgrads(name, jobs):
    arrays, index = [], {}
    for a, b in jobs:
        for arr in (a, b):
            if id(arr) not in index:
                index[id(arr)] = len(arrays)
                arrays.append(arr)
    n_in = len(arrays)

    def body(*refs):
        for t, (a, b) in enumerate(jobs):
            a_ref, b_ref, o_ref = refs[index[id(a)]], refs[index[id(b)]], refs[n_in + t]
            if a.ndim == 3:
                for h in range(a.shape[0]):
                    o_ref[h] = lax.dot_general(a_ref[h], b_ref[...], TN, preferred_element_type=F32).astype(BF16)
            else:
                o_ref[...] = lax.dot_general(a_ref[...], b_ref[...], TN, preferred_element_type=F32).astype(BF16)

    out_shape = [jax.ShapeDtypeStruct(a.shape[:-2] + (a.shape[-1], b.shape[-1]), BF16) for a, b in jobs]
    return _pallas(body, name=name, in_specs=[VMEM_SPEC] * n_in, out_specs=[VMEM_SPEC] * len(jobs), out_shape=out_shape,
                   params=pltpu.CompilerParams(vmem_limit_bytes=VMEM_LIMIT))(*arrays)


def _mm_wgrad(name, a, b, *, tn=512):
    k, n = a.shape[1], b.shape[1]
    tn = min(tn, n)
    return _mm(name, a, b, grid=(n // tn,),
               a_spec=pl.BlockSpec((T, k), lambda j: (0, 0)), b_spec=pl.BlockSpec((T, tn), lambda j: (0, j)),
               o_spec=pl.BlockSpec((k, tn), lambda j: (0, j)), o_shape=(k, n), o_dtype=BF16, dims=TN)


def _rms_fwd(name, x, g):
    d = x.shape[1]

    def body(x_ref, g_ref, o_ref):
        xv = x_ref[...]
        r = lax.rsqrt(jnp.mean(xv * xv, axis=-1, keepdims=True) + EPS)
        o_ref[...] = ((xv * r) * g_ref[...]).astype(BF16)

    return _pallas(
        body, name=name, grid=(T // TM,),
        in_specs=[pl.BlockSpec((TM, d), lambda i: (i, 0)), pl.BlockSpec((1, d), lambda i: (0, 0))],
        out_specs=pl.BlockSpec((TM, d), lambda i: (i, 0)),
        out_shape=jax.ShapeDtypeStruct((T, d), BF16), params=_params(("parallel",)))(x, g)


def _rms(xv, g):
    return (xv * lax.rsqrt(jnp.mean(xv * xv, axis=-1, keepdims=True) + EPS)) * g


def _rms_bwd(name, x, gains, dys, dres=None):
    d = x.shape[1]
    n = len(gains)
    has_res = dres is not None

    def body(*refs):
        x_ref, g_refs, dy_refs = refs[0], refs[1:1 + n], refs[1 + n:1 + 2 * n]
        dx_ref, dxb_ref = refs[-2 - n], refs[-1 - n]
        dg_refs = refs[-n:]
        xv = x_ref[...]
        r = lax.rsqrt(jnp.mean(xv * xv, axis=-1, keepdims=True) + EPS)
        xn = xv * r
        dx = refs[1 + 2 * n][...] if has_res else None
        parts = []
        for g_ref, dy_ref in zip(g_refs, dy_refs):
            dyv = dy_ref[...].astype(F32)
            gdy = dyv * g_ref[...]
            t = r * (gdy - xn * jnp.mean(gdy * xn, axis=-1, keepdims=True))
            dx = t if dx is None else dx + t
            parts.append(jnp.sum(dyv * xn, axis=0, keepdims=True))
        dx_ref[...] = dx
        dxb_ref[...] = dx.astype(BF16)

        @pl.when(pl.program_id(0) == 0)
        def _():
            for dg_ref, part in zip(dg_refs, parts):
                dg_ref[...] = part

        @pl.when(pl.program_id(0) > 0)
        def _():
            for dg_ref, part in zip(dg_refs, parts):
                dg_ref[...] += part

    row = pl.BlockSpec((TR, d), lambda i: (i, 0))
    vec = pl.BlockSpec((1, d), lambda i: (0, 0))
    args = [x] + list(gains) + list(dys) + ([dres] if has_res else [])
    in_specs = [row] + [vec] * n + [row] * n + ([row] if has_res else [])
    outs = _pallas(
        body, name=name, grid=(T // TR,), in_specs=in_specs, out_specs=[row, row] + [vec] * n,
        out_shape=[jax.ShapeDtypeStruct((T, d), F32), jax.ShapeDtypeStruct((T, d), BF16)]
        + [jax.ShapeDtypeStruct((1, d), F32)] * n,
        params=_params(("arbitrary",)))(*args)
    return outs[0], outs[1], list(outs[2:])


def _final(h, g, tgt):
    def body(h_ref, g_ref, t_ref, loss_ref, dh_ref, dhb_ref, dg_ref):
        hv = h_ref[...]
        r = lax.rsqrt(jnp.mean(hv * hv, axis=-1, keepdims=True) + EPS)
        xn = hv * r
        gv = g_ref[...]
        err = xn * gv - t_ref[...]
        part_loss = 0.5 * jnp.sum(jnp.mean(err * err, axis=-1, keepdims=True), axis=0, keepdims=True)
        dy = err * (1.0 / D)
        gdy = dy * gv
        dh = r * (gdy - xn * jnp.mean(gdy * xn, axis=-1, keepdims=True))
        dh_ref[...] = dh
        dhb_ref[...] = dh.astype(BF16)
        part = jnp.sum(dy * xn, axis=0, keepdims=True)
        first = pl.program_id(0) == 0

        @pl.when(first)
        def _():
            dg_ref[...] = part
            loss_ref[...] = jnp.broadcast_to(part_loss, (1, 128))

        @pl.when(jnp.logical_not(first))
        def _():
            dg_ref[...] += part
            loss_ref[...] += jnp.broadcast_to(part_loss, (1, 128))

    row = pl.BlockSpec((TR, D), lambda i: (i, 0))
    vec = pl.BlockSpec((1, D), lambda i: (0, 0))
    return _pallas(
        body, name="final_loss", grid=(T // TR,), in_specs=[row, vec, row],
        out_specs=[pl.BlockSpec((1, 128), lambda i: (0, 0)), row, row, vec],
        out_shape=[jax.ShapeDtypeStruct((1, 128), F32), jax.ShapeDtypeStruct((T, D), F32),
                   jax.ShapeDtypeStruct((T, D), BF16), jax.ShapeDtypeStruct((1, D), F32)],
        params=_params(("arbitrary",)))(h, g, tgt)


def _prev_idx(i, rows=TR):
    return jnp.maximum(i * (rows // HALO) - 1, 0)


def _next_idx(i, rows=TR):
    return jnp.minimum((i + 1) * (rows // HALO), T // HALO - 1)


def _causal_taps(ext):
    return pltpu.roll(ext, 2, 0)[HALO:], pltpu.roll(ext, 1, 0)[HALO:], ext[HALO:]


def _anticausal_taps(ext, n):
    rows = ext.shape[0]
    return pltpu.roll(ext, rows - 1, 0)[:n], pltpu.roll(ext, rows - 2, 0)[:n]


MIX_COLS = 512


def _mixer_in(hn, w_in, w):
    nc = D // MIX_COLS

    def body(h_ref, hh_ref, wb_ref, wc_ref, wu_ref, w_ref, b_ref, c_ref, u_ref, y_ref):
        i = pl.program_id(1)
        hv = h_ref[...]
        he = jnp.concatenate([hh_ref[...], hv], axis=0)
        ce = lax.dot_general(he, wc_ref[...], NN, preferred_element_type=F32).astype(BF16)
        ue = lax.dot_general(he, wu_ref[...], NN, preferred_element_type=F32).astype(BF16)
        bv = lax.dot_general(hv, wb_ref[...], NN, preferred_element_type=F32).astype(BF16)
        b_ref[...] = bv
        c_ref[...] = ce[HALO:]
        u_ref[...] = ue[HALO:]
        row = lax.broadcasted_iota(jnp.int32, (HALO + TS, 1), 0)
        cu = jnp.where(jnp.logical_or(i > 0, row >= HALO), ce.astype(F32) * ue.astype(F32), 0.0)
        x2, x1, x0 = _causal_taps(cu)
        wv = w_ref[...]
        cv = (x2 * wv[0:1] + x1 * wv[1:2]) + x0 * wv[2:3]
        y_ref[...] = (bv.astype(F32) * cv).astype(BF16)

    def cols(part):
        return pl.BlockSpec((D, MIX_COLS), lambda j, i: (0, part * nc + j))

    blk = pl.BlockSpec((TS, MIX_COLS), lambda j, i: (i, j))
    out = jax.ShapeDtypeStruct((T, D), BF16)
    return _pallas(
        body, name="l0_in", grid=(nc, T // TS),
        in_specs=[pl.BlockSpec((TS, D), lambda j, i: (i, 0)), pl.BlockSpec((HALO, D), lambda j, i: (_prev_idx(i, TS), 0)),
                  cols(0), cols(1), cols(2), pl.BlockSpec((3, MIX_COLS), lambda j, i: (0, j))],
        out_specs=[blk] * 4, out_shape=[out] * 4,
        params=_params(("parallel", "parallel")))(hn, hn, w_in, w_in, w_in, w)


def _mixer_out_bwd(dh, w_out, zb, zc, zu, w):
    last = T // TR - 1

    def body(dh_ref, dhn_ref, wo_ref, b_ref, bn_ref, c_ref, ch_ref, u_ref, uh_ref, w_ref, dz_ref, dw_ref):
        i = pl.program_id(0)
        dye = lax.dot_general(jnp.concatenate([dh_ref[...], dhn_ref[...]], axis=0), wo_ref[...], NT,
                              preferred_element_type=F32)
        cv_ = c_ref[...].astype(F32)
        uv = u_ref[...].astype(F32)
        cu = cv_ * uv
        cuh = jnp.where(i > 0, ch_ref[...].astype(F32) * uh_ref[...].astype(F32), 0.0)
        x2, x1, x0 = _causal_taps(jnp.concatenate([cuh, cu], axis=0))
        wv = w_ref[...]
        conv = (x2 * wv[0:1] + x1 * wv[1:2]) + x0 * wv[2:3]
        dyv = dye[:TR]
        dz_ref[:, 0:D] = (dyv * conv).astype(BF16)
        dconv = dyv * b_ref[...].astype(F32)
        dconv_n = jnp.where(i < last, dye[TR:] * bn_ref[...].astype(F32), 0.0)
        n1, n2 = _anticausal_taps(jnp.concatenate([dconv, dconv_n], axis=0), TR)
        dcu = (dconv * wv[2:3] + n1 * wv[1:2]) + n2 * wv[0:1]
        dz_ref[:, D:2 * D] = (dcu * uv).astype(BF16)
        dz_ref[:, 2 * D:3 * D] = (dcu * cv_).astype(BF16)
        part = jnp.concatenate([jnp.sum(dconv * x2, axis=0, keepdims=True),
                                jnp.sum(dconv * x1, axis=0, keepdims=True),
                                jnp.sum(dconv * x0, axis=0, keepdims=True)], axis=0)

        @pl.when(i == 0)
        def _():
            dw_ref[...] = part

        @pl.when(i > 0)
        def _():
            dw_ref[...] += part

    main = pl.BlockSpec((TR, D), lambda i: (i, 0))
    prev = pl.BlockSpec((HALO, D), lambda i: (_prev_idx(i), 0))
    nxt = pl.BlockSpec((HALO, D), lambda i: (_next_idx(i), 0))
    wspec = pl.BlockSpec((3, D), lambda i: (0, 0))
    return _pallas(
        body, name="d_l0_out", grid=(T // TR,),
        in_specs=[main, nxt, pl.BlockSpec((D, D), lambda i: (0, 0)), main, nxt, main, prev, main, prev, wspec],
        out_specs=[pl.BlockSpec((TR, 3 * D), lambda i: (i, 0)), wspec],
        out_shape=[jax.ShapeDtypeStruct((T, 3 * D), BF16), jax.ShapeDtypeStruct((3, D), F32)],
        params=_params(("arbitrary",)))(dh, dh, w_out, zb, zb, zc, zc, zu, zu, w)


def _sigmoid(x):
    return 0.5 * jnp.tanh(0.5 * x) + 0.5


def _ffn_up_act(name, hf, w_up, w, b):
    def body(h_ref, hh_ref, wg_ref, wv_ref, w_ref, b_ref, g_ref, v_ref, a_ref):
        i = pl.program_id(1)
        hv = h_ref[...]
        ge = lax.dot_general(jnp.concatenate([hh_ref[...], hv], axis=0), wg_ref[...], NT,
                             preferred_element_type=F32).astype(BF16)
        v = lax.dot_general(hv, wv_ref[...], NT, preferred_element_type=F32).astype(BF16)
        g_ref[...] = ge[HALO:]
        v_ref[...] = v
        ext = ge.astype(F32)
        row = lax.broadcasted_iota(jnp.int32, (HALO + TM, 1), 0)
        ext = jnp.where(jnp.logical_or(i > 0, row >= HALO), ext, 0.0)
        x2, x1, x0 = _causal_taps(ext)
        wv = w_ref[...]
        gc = ((x2 * wv[0:1] + x1 * wv[1:2]) + x0 * wv[2:3]) + b_ref[...]
        a_ref[...] = ((gc * _sigmoid(gc)) * v.astype(F32)).astype(BF16)

    blk = pl.BlockSpec((None, TM, FF_BLK), lambda j, i: (j, i, 0))
    out = jax.ShapeDtypeStruct((N_FF_BLK, T, FF_BLK), BF16)
    return _pallas(
        body, name=name, grid=(N_FF_BLK, T // TM),
        in_specs=[pl.BlockSpec((TM, D), lambda j, i: (i, 0)),
                  pl.BlockSpec((HALO, D), lambda j, i: (_prev_idx(i, TM), 0)),
                  pl.BlockSpec((None, None, FF_BLK, D), lambda j, i: (0, j, 0, 0)),
                  pl.BlockSpec((None, None, FF_BLK, D), lambda j, i: (0, j + N_FF_BLK, 0, 0)),
                  pl.BlockSpec((None, 3, FF_BLK), lambda j, i: (j, 0, 0)),
                  pl.BlockSpec((None, 1, FF_BLK), lambda j, i: (j, 0, 0))],
        out_specs=[blk, blk, blk], out_shape=[out, out, out],
        params=_params(("parallel", "parallel")))(hf, hf, w_up, w_up, w, b)


def _ffn_dact(name, dh, w_down4, g, v, w, b):
    last = T // TS - 1

    def body(dh_ref, dhn_ref, wd_ref, g_ref, gp_ref, gn_ref, v_ref, vn_ref, w_ref, b_ref, dg_ref, dv_ref, dw_ref, db_ref):
        i = pl.program_id(1)
        da = lax.dot_general(jnp.concatenate([dh_ref[...], dhn_ref[...]], axis=0), wd_ref[...], NT,
                             preferred_element_type=F32)
        row = lax.broadcasted_iota(jnp.int32, (TS + HALO, 1), 0)
        da = jnp.where(jnp.logical_or(i < last, row < TS), da, 0.0)
        gp = jnp.where(i > 0, gp_ref[...].astype(F32), 0.0)
        ext = jnp.concatenate([gp, g_ref[...].astype(F32), gn_ref[...].astype(F32)], axis=0)
        x2, x1, x0 = _causal_taps(ext)
        wv = w_ref[...]
        gc = ((x2 * wv[0:1] + x1 * wv[1:2]) + x0 * wv[2:3]) + b_ref[...]
        sg = _sigmoid(gc)
        vv = jnp.concatenate([v_ref[...].astype(F32), vn_ref[...].astype(F32)], axis=0)
        dv_ref[...] = (da[:TS] * (gc[:TS] * sg[:TS])).astype(BF16)
        dgc = (da * vv) * (sg * (1.0 + gc * (1.0 - sg)))
        n1, n2 = _anticausal_taps(dgc, TS)
        d0 = dgc[:TS]
        dg_ref[...] = ((d0 * wv[2:3] + n1 * wv[1:2]) + n2 * wv[0:1]).astype(BF16)
        part_w = jnp.concatenate([jnp.sum(d0 * x2[:TS], axis=0, keepdims=True),
                                  jnp.sum(d0 * x1[:TS], axis=0, keepdims=True),
                                  jnp.sum(d0 * x0[:TS], axis=0, keepdims=True)], axis=0)
        part_b = jnp.sum(d0, axis=0, keepdims=True)

        @pl.when(i == 0)
        def _():
            dw_ref[...] = part_w
            db_ref[...] = part_b

        @pl.when(i > 0)
        def _():
            dw_ref[...] += part_w
            db_ref[...] += part_b

    blk = pl.BlockSpec((None, TS, FF_BLK), lambda j, i: (j, i, 0))
    prev = pl.BlockSpec((None, HALO, FF_BLK), lambda j, i: (j, _prev_idx(i, TS), 0))
    nxt = pl.BlockSpec((None, HALO, FF_BLK), lambda j, i: (j, _next_idx(i, TS), 0))
    wspec = pl.BlockSpec((None, 3, FF_BLK), lambda j, i: (j, 0, 0))
    bspec = pl.BlockSpec((None, 1, FF_BLK), lambda j, i: (j, 0, 0))
    return _pallas(
        body, name=name, grid=(N_FF_BLK, T // TS),
        in_specs=[pl.BlockSpec((TS, D), lambda j, i: (i, 0)),
                  pl.BlockSpec((HALO, D), lambda j, i: (_next_idx(i, TS), 0)),
                  pl.BlockSpec((None, None, FF_BLK, D), lambda j, i: (0, j, 0, 0)),
                  blk, prev, nxt, blk, nxt, wspec, bspec],
        out_specs=[blk, blk, wspec, bspec],
        out_shape=[jax.ShapeDtypeStruct((N_FF_BLK, T, FF_BLK), BF16), jax.ShapeDtypeStruct((N_FF_BLK, T, FF_BLK), BF16),
                   jax.ShapeDtypeStruct((N_FF_BLK, 3, FF_BLK), F32), jax.ShapeDtypeStruct((N_FF_BLK, 1, FF_BLK), F32)],
        params=_params(("parallel", "arbitrary")))(dh, dh, w_down4, g, g, g, v, v, w, b)


def _rope_tables(pos, inv_freq):
    half = QK_ROPE // 2

    def body(p_ref, f_ref, c_ref, sa_ref, sb_ref):
        ang = p_ref[...].astype(F32) * f_ref[...]
        lane = lax.broadcasted_iota(jnp.int32, (T, 128), 1)
        c = jnp.cos(ang)
        s = jnp.sin(ang)
        c_ref[...] = jnp.where(lane < 2 * half, c, 0.0)
        sa_ref[...] = jnp.where(lane < half, -s, 0.0)
        sb_ref[...] = jnp.where(jnp.logical_and(lane >= half, lane < 2 * half), s, 0.0)

    return _pallas(
        body, name="rope_tables", in_specs=[VMEM_SPEC] * 2, out_specs=[VMEM_SPEC] * 3,
        out_shape=[jax.ShapeDtypeStruct((T, 128), F32)] * 3,
        params=pltpu.CompilerParams(vmem_limit_bytes=VMEM_LIMIT))(pos, inv_freq)


def _rotate(r, c, sa, sb, sign):
    return r * c + sign * (pltpu.roll(r, 96, 1) * sa + pltpu.roll(r, 32, 1) * sb)


def _attn_pre(h2, g_kv, g_l1, g_kvl, g_ql, w_dkv, w_kr, w_uk, w_uv, w_dq, w_uq, tables):
    def body(h_ref, c_ref, sa_ref, sb_ref, gkv_ref, gl1_ref, gkvl_ref, gql_ref, wdkv_ref, wkr_ref, wuk_ref, wuv_ref,
             wdq_ref, wuq_ref, hk_ref, hn_ref, ckvr_ref, ckv_ref, kr_ref, kn_ref, v_ref, cqr_ref, cq_ref, q_ref):
        xv = h_ref[...]
        xn = xv * lax.rsqrt(jnp.mean(xv * xv, axis=-1, keepdims=True) + EPS)
        hk = (xn * gkv_ref[...]).astype(BF16)
        hn = (xn * gl1_ref[...]).astype(BF16)
        hk_ref[...] = hk
        hn_ref[...] = hn
        cv, sav, sbv = c_ref[...], sa_ref[...], sb_ref[...]
        raw = lax.dot_general(hk, wdkv_ref[...], NN, preferred_element_type=F32)
        ckvr_ref[...] = raw
        ckv = _rms(raw, gkvl_ref[...]).astype(BF16)
        ckv_ref[...] = ckv
        kr = lax.dot_general(hk, wkr_ref[...], NT, preferred_element_type=F32)
        kr_ref[...] = _rotate(kr, cv, sav, sbv, 1.0).astype(BF16)
        kn_ref[...] = lax.dot_general(ckv, wuk_ref[...], NN, preferred_element_type=F32).astype(BF16)
        v_ref[...] = lax.dot_general(ckv, wuv_ref[...], NN, preferred_element_type=F32).astype(BF16)
        cqr = lax.dot_general(hn, wdq_ref[...], NN, preferred_element_type=F32)
        cqr_ref[...] = cqr
        cq = _rms(cqr, gql_ref[...]).astype(BF16)
        cq_ref[...] = cq
        for h in range(N_HEADS):
            r = lax.dot_general(cq, wuq_ref[h], NT, preferred_element_type=F32)
            q_ref[h, :, :QK_NOPE] = r[:, :QK_NOPE].astype(BF16)
            q_ref[h, :, QK_NOPE:] = _rotate(r[:, QK_NOPE:], cv, sav, sbv, 1.0).astype(BF16)

    def rows(d):
        return pl.BlockSpec((TS, d), lambda i: (i, 0))

    def whole(a):
        return pl.BlockSpec(a.shape, lambda i: (0,) * a.ndim)

    wholes = [g_kv, g_l1, g_kvl, g_ql, w_dkv, w_kr, w_uk, w_uv, w_dq, w_uq]
    outs = [(D, BF16), (D, BF16), (KV_LORA, F32), (KV_LORA, BF16), (128, BF16), (N_HEADS * QK_NOPE, BF16),
            (N_HEADS * V_HEAD, BF16), (Q_LORA, F32), (Q_LORA, BF16)]
    return _pallas(
        body, name="attn_pre", grid=(T // TS,),
        in_specs=[rows(D), rows(128), rows(128), rows(128)] + [whole(a) for a in wholes],
        out_specs=[rows(d) for d, _ in outs] + [pl.BlockSpec((N_HEADS, TS, QK_PAD), lambda i: (0, i, 0))],
        out_shape=[jax.ShapeDtypeStruct((T, d), dt) for d, dt in outs]
        + [jax.ShapeDtypeStruct((N_HEADS, T, QK_PAD), BF16)],
        params=_params(("parallel",)))(h2, *tables, *wholes)


def _rope(name, x, tables, sign, out_dtype, reduce_groups=False):
    g, _, w = x.shape
    cos, sa, sb = tables

    def body(x_ref, c_ref, sa_ref, sb_ref, o_ref):
        xv = x_ref[...].astype(F32)
        if reduce_groups:
            acc = xv[0]
            for k in range(1, g):
                acc = acc + xv[k]
            xv = acc
        out = _rotate(xv[:, w - 128:], c_ref[...], sa_ref[...], sb_ref[...], sign)
        if w > 128:
            o_ref[:, :w - 128] = xv[:, :w - 128].astype(out_dtype)
        o_ref[:, w - 128:] = out.astype(out_dtype)

    tab = pl.BlockSpec((TM, 128), lambda h, i: (i, 0))
    if reduce_groups:
        x_spec = pl.BlockSpec((g, TM, w), lambda h, i: (0, i, 0))
        groups = 1
    else:
        x_spec = pl.BlockSpec((None, TM, w), lambda h, i: (h, i, 0))
        groups = g
    return _pallas(
        body, name=name, grid=(groups, T // TM), in_specs=[x_spec, tab, tab, tab],
        out_specs=pl.BlockSpec((None, TM, w), lambda h, i: (h, i, 0)),
        out_shape=jax.ShapeDtypeStruct((groups, T, w), out_dtype),
        params=_params(("parallel", "parallel")))(x, cos, sa, sb)


SCALE = (QK_NOPE + QK_ROPE) ** -0.5
LOG2E = 1.4426950408889634
SCALE2 = SCALE * LOG2E


def _diag_mask(transposed):
    shift = CHUNK.bit_length() - 1
    a = lax.broadcasted_iota(jnp.int32, (TQ, TQ), 0) >> shift
    b = lax.broadcasted_iota(jnp.int32, (TQ, TQ), 1) >> shift
    return (a <= b) if transposed else (b <= a)


def _as_row(col):
    return jnp.transpose(jnp.broadcast_to(col, (col.shape[0], 128)), (1, 0))[0:1]


def _keys(kn_ref, kr_ref, off):
    return jnp.concatenate([kn_ref[pl.ds(off, TQ), :], kr_ref[pl.ds(off, TQ), :]], axis=1)


def _attn_fwd(q, kn, kr, v):
    hp = 2

    def body(q_ref, kn_ref, kr_ref, v_ref, o_ref, lse_ref):
        i = pl.program_id(1)
        qs = [q_ref[a] for a in range(hp)]

        def step(j, carry, masked):
            off = pl.multiple_of(j * TQ, TQ)
            krv = kr_ref[pl.ds(off, TQ), :]
            ss = []
            for a in range(hp):
                kk = jnp.concatenate([kn_ref[pl.ds(off, TQ), a * QK_NOPE:(a + 1) * QK_NOPE], krv], axis=1)
                ss.append(lax.dot_general(qs[a], kk, NT, preferred_element_type=F32))
            out = []
            for a in range(hp):
                m, l, acc = carry[a]
                s = ss[a] * SCALE2
                if masked:
                    s = jnp.where(_diag_mask(False), s, NEG_INF)
                m_new = jnp.maximum(m, jnp.max(s, axis=-1, keepdims=True))
                p = jnp.exp2(s - m_new)
                alpha = jnp.exp2(m - m_new)
                l = alpha * l + jnp.sum(p, axis=-1, keepdims=True)
                pv = lax.dot_general(p.astype(BF16), v_ref[pl.ds(off, TQ), a * V_HEAD:(a + 1) * V_HEAD], NN,
                                     preferred_element_type=F32)
                out.append((m_new, l, alpha * acc + pv))
            return tuple(out)

        one = (jnp.full((TQ, 1), NEG_INF, F32), jnp.zeros((TQ, 1), F32), jnp.zeros((TQ, V_HEAD), F32))
        carry = lax.fori_loop(0, i, lambda j, cr: step(j, cr, False), (one,) * hp)
        carry = step(i, carry, True)
        for a, (m, l, acc) in enumerate(carry):
            o_ref[:, a * V_HEAD:(a + 1) * V_HEAD] = (acc / l).astype(BF16)
            lse_ref[a] = _as_row(m + jnp.log(l) * LOG2E)

    return _pallas(
        body, name="attn_fwd", grid=(N_HEADS // hp, T // TQ),
        in_specs=[pl.BlockSpec((hp, TQ, QK_PAD), lambda h, i: (h, i, 0)),
                  pl.BlockSpec((T, hp * QK_NOPE), lambda h, i: (0, h)),
                  pl.BlockSpec((T, 128), lambda h, i: (0, 0)),
                  pl.BlockSpec((T, hp * V_HEAD), lambda h, i: (0, h))],
        out_specs=[pl.BlockSpec((TQ, hp * V_HEAD), lambda h, i: (i, h)), pl.BlockSpec((hp, 1, TQ), lambda h, i: (h, 0, i))],
        out_shape=[jax.ShapeDtypeStruct((T, N_HEADS * V_HEAD), BF16), jax.ShapeDtypeStruct((N_HEADS, 1, T), F32)],
        params=_params(("parallel", "parallel")))(q, kn, kr, v)


def _attn_bwd(q, kn, kr, v, o, do, lse_row, tables):
    nq = T // TQ
    hp = 2
    cos, sa, sb = tables

    def body(q_ref, kn_ref, kr_ref, v_ref, o_ref, do_ref, lse_ref, c_ref, sa_ref, sb_ref,
             dq_ref, dkn_ref, dkr_ref, dv_ref, dq_acc, dl_ref):
        j = pl.program_id(1)

        def cols(a):
            return slice(a * 128, (a + 1) * 128)

        @pl.when(j == 0)
        def _():
            dq_acc[...] = jnp.zeros_like(dq_acc)
            for a in range(hp):
                for i in range(nq):
                    rows = pl.ds(i * TQ, TQ)
                    prod = do_ref[rows, cols(a)].astype(F32) * o_ref[rows, cols(a)].astype(F32)
                    dl_ref[a, :, rows] = _as_row(jnp.sum(prod, axis=-1, keepdims=True))

        krv = kr_ref[...]
        kks = [jnp.concatenate([kn_ref[:, cols(a)], krv], axis=1) for a in range(hp)]
        vvs = [v_ref[:, cols(a)] for a in range(hp)]

        def step(i, carry, masked):
            off = pl.multiple_of(i * TQ, TQ)
            rows = pl.ds(off, TQ)
            qis = [q_ref[a, rows, :] for a in range(hp)]
            dois = [do_ref[rows, cols(a)] for a in range(hp)]
            sts = [lax.dot_general(kks[a], qis[a], NT, preferred_element_type=F32) for a in range(hp)]
            dpts = [lax.dot_general(vvs[a], dois[a], NT, preferred_element_type=F32) for a in range(hp)]
            out = []
            for a in range(hp):
                dk, dv = carry[a]
                st = sts[a] * SCALE2
                if masked:
                    st = jnp.where(_diag_mask(True), st, NEG_INF)
                pt = jnp.exp2(st - lse_ref[a, :, rows])
                dv = dv + lax.dot_general(pt.astype(BF16), dois[a], NN, preferred_element_type=F32)
                dst = ((pt * (dpts[a] - dl_ref[a, :, rows])) * SCALE).astype(BF16)
                dk = dk + lax.dot_general(dst, qis[a], NN, preferred_element_type=F32)
                dq_acc[a, rows, :] += lax.dot_general(dst, kks[a], TN, preferred_element_type=F32)
                out.append((dk, dv))
            return tuple(out)

        zero = (jnp.zeros((TQ, QK_PAD), F32), jnp.zeros((TQ, V_HEAD), F32))
        carry = step(j, (zero,) * hp, True)
        carry = lax.fori_loop(j + 1, nq, lambda i, cr: step(i, cr, False), carry)
        for a, (dk, dv) in enumerate(carry):
            dkn_ref[:, cols(a)] = dk[:, :QK_NOPE].astype(BF16)
            dkr_ref[a] = dk[:, QK_NOPE:]
            dv_ref[:, cols(a)] = dv.astype(BF16)

        @pl.when(j == nq - 1)
        def _():
            for a in range(hp):
                dq = dq_acc[a]
                dq_ref[a, :, :QK_NOPE] = dq[:, :QK_NOPE].astype(BF16)
                dq_ref[a, :, QK_NOPE:] = _rotate(dq[:, QK_NOPE:], c_ref[...], sa_ref[...], sb_ref[...], -1.0).astype(BF16)

    row = pl.BlockSpec((hp, 1, T), lambda h, j: (h, 0, 0))
    head = pl.BlockSpec((TQ, hp * 128), lambda h, j: (j, h))
    whole = pl.BlockSpec((hp, T, QK_PAD), lambda h, j: (h, 0, 0))
    tab = pl.BlockSpec((T, 128), lambda h, j: (0, 0))
    heads = pl.BlockSpec((T, hp * V_HEAD), lambda h, j: (0, h))
    return _pallas(
        body, name="attn_bwd", grid=(N_HEADS // hp, nq),
        in_specs=[whole, head, pl.BlockSpec((TQ, 128), lambda h, j: (j, 0)), head, heads, heads, row, tab, tab, tab],
        out_specs=[whole, head, pl.BlockSpec((hp, TQ, 128), lambda h, j: (h, j, 0)), head],
        out_shape=[jax.ShapeDtypeStruct((N_HEADS, T, QK_PAD), BF16), jax.ShapeDtypeStruct((T, N_HEADS * QK_NOPE), BF16),
                   jax.ShapeDtypeStruct((N_HEADS, T, 128), F32), jax.ShapeDtypeStruct((T, N_HEADS * V_HEAD), BF16)],
        scratch_shapes=[pltpu.VMEM((hp, T, QK_PAD), F32), pltpu.VMEM((hp, 1, T), F32)],
        params=_params(("parallel", "arbitrary")))(q, kn, kr, v, o, do, lse_row, cos, sa, sb)


def _ffn_gup(name, dg, dv, hf):
    def body(dg_ref, dv_ref, hf_ref, o_ref):
        j = pl.program_id(0)

        @pl.when(j < N_FF_BLK)
        def _():
            o_ref[...] = lax.dot_general(dg_ref[...], hf_ref[...], TN, preferred_element_type=F32).astype(BF16)

        @pl.when(j >= N_FF_BLK)
        def _():
            o_ref[...] = lax.dot_general(dv_ref[...], hf_ref[...], TN, preferred_element_type=F32).astype(BF16)

    return _pallas(
        body, name=name, grid=(N_DEV,),
        in_specs=[pl.BlockSpec((None, T, FF_BLK), lambda j: (jnp.minimum(j, N_FF_BLK - 1), 0, 0)),
                  pl.BlockSpec((None, T, FF_BLK), lambda j: (jnp.maximum(j - N_FF_BLK, 0), 0, 0)),
                  pl.BlockSpec((T, D), lambda j: (0, 0))],
        out_specs=pl.BlockSpec((None, FF_BLK, D), lambda j: (j, 0, 0)),
        out_shape=jax.ShapeDtypeStruct((N_DEV, FF_BLK, D), BF16), params=_params(("parallel",)))(dg, dv, hf)


def _ffn_layer_fwd(tag, h, gain, ex):
    hf = _rms_fwd(f"{tag}_norm", h, gain)
    g, v, act = _ffn_up_act(f"{tag}_up", hf, ex.need(f"ffn_w_up{tag[1]}", hf), ex.need(f"ffn_cw{tag[1]}", hf),
                            ex.need(f"ffn_cb{tag[1]}", hf))
    ex.at(f"{tag}_up", act)
    rows = pl.BlockSpec((TS, D), lambda i: (i, 0))
    out = _mm_sum(f"{tag}_down",
                  [(act, pl.BlockSpec((N_FF_BLK, TS, FF_BLK), lambda i: (0, i, 0)), ex.need(f"ffn_w_down{tag[1]}", act),
                    pl.BlockSpec((None, N_FF_BLK, FF_BLK, D), lambda i: (0, 0, 0, 0)), NN, 0)],
                  grid=(T // TS,), o_spec=rows, o_shape=(T, D), o_dtype=F32, add=h)
    ex.at(f"{tag}_down", out)
    return out, (hf, g, v, act)


def _ffn_layer_bwd(tag, h, gain, ex, saved, dh, dh_bf):
    hf, g, v, act = saved
    layer = tag[1]
    w_up, w_down4 = ex.need(f"ffn_w_up{layer}", dh_bf), ex.need(f"ffn_w_down{layer}", dh_bf)
    dg, dv, dcw, dcb = _ffn_dact(f"{tag}_dact", dh_bf, w_down4, g, v, ex.need(f"ffn_cw{layer}", dh_bf),
                                 ex.need(f"ffn_cb{layer}", dh_bf))
    ex.at(f"{tag}_dact", dg)
    g_down = _mm(f"{tag}_gdown", act, dh_bf, grid=(N_FF_BLK,),
                 a_spec=pl.BlockSpec((None, T, FF_BLK), lambda j: (j, 0, 0)),
                 b_spec=pl.BlockSpec((T, D), lambda j: (0, 0)),
                 o_spec=pl.BlockSpec((FF_BLK, D), lambda j: (j, 0)),
                 o_shape=(D_FF, D), o_dtype=BF16, dims=TN)
    g_up = _ffn_gup(f"{tag}_gup", dg, dv, hf)
    ex.grad("ffn_w_up", int(layer), g_up.reshape(1, N_DEV, FF_BLK, D))
    ex.grad("ffn_w_down", int(layer), g_down.reshape(1, N_DEV, D_FF // N_DEV, D))
    ex.at(f"{tag}_gup", g_up)
    part = pl.BlockSpec((N_FF_BLK, TR, FF_BLK), lambda i: (0, i, 0))
    dh_in, dh_in_bf, dgain = _mm_sum(
        f"{tag}_dhf",
        [(dg, part, w_up, pl.BlockSpec((None, N_FF_BLK, FF_BLK, D), lambda i: (0, 0, 0, 0)), NN, 0),
         (dv, part, w_up, pl.BlockSpec((None, N_FF_BLK, FF_BLK, D), lambda i: (0, 1, 0, 0)), NN, 0)],
        grid=(T // TR,), o_spec=pl.BlockSpec((TR, D), lambda i: (i, 0)), o_shape=(T, D), o_dtype=F32,
        norm_bwd=(h, [gain], dh))
    ex.at(f"{tag}_dhf", dh_in)
    return dh_in, dh_in_bf, dgain[0], dcw, dcb


def _local_step(x, pos, tgt, rep, ex):
    attn_norm, ffn_norm, final_norm = rep["attn_norm"], rep["ffn_norm"], rep["final_norm"]
    half = QK_ROPE // 2
    inv = 1.0 / (ROPE_THETA ** (jnp.arange(half, dtype=F32) / half))
    inv_freq = jnp.concatenate([inv, inv, jnp.zeros((128 - 2 * half,), F32)]).reshape(1, 128)
    tables = _rope_tables(pos, inv_freq)

    hn0 = _rms_fwd("l0_norm", x, attn_norm[0:1])
    w_in = ex.need("sc_w_in", hn0)
    ex.at("mixer_ready", hn0)
    zb, zc, zu, y = _mixer_in(hn0, w_in, ex.need("sc_conv_w", hn0))
    ex.at("l0_in", y)
    h1 = _mm_rows("l0_out", y, ex.need("sc_w_out", y), NN, F32, D, tn=512, add=x)
    ex.at("l0_out", h1)
    h2, ffn0 = _ffn_layer_fwd("f0", h1, ffn_norm[0:1], ex)

    w_uq = ex.need("w_uq", h2)
    hk, hn1, ckv_raw, ckv, kr, kn, vv, cq_raw, cq, q = _attn_pre(
        h2, rep["kv_in_norm"], attn_norm[1:2], rep["kv_latent_norm"], rep["q_latent_norm"], ex.need("w_dkv", h2),
        ex.need("w_kr", h2), ex.need("w_uk", h2), ex.need("w_uv", h2), ex.need("w_dq", h2), w_uq, tables)

    o, lse = _attn_fwd(q, kn, kr, vv)
    ex.at("attn_fwd", o)
    w_o = ex.need("w_o", o)
    h3 = _mm_rows("attn_out", o, w_o, NN, F32, D, tn=512, add=h2)
    h4, ffn1 = _ffn_layer_fwd("f1", h3, ffn_norm[1:2], ex)

    loss, dh4, dh4_bf, d_final = _final(h4, final_norm.reshape(1, D), tgt)

    dh3, dh3_bf, d_fn1, dcw1, dcb1 = _ffn_layer_bwd("f1", h3, ffn_norm[1:2], ex, ffn1, dh4, dh4_bf)
    ex.at("f1_bwd", dh3)

    do = _mm_rows("d_attn_out", dh3_bf, w_o, NT, BF16, N_HEADS * V_HEAD)
    dq_pre, dkn, dkr, dvv = _attn_bwd(q, kn, kr, vv, o, do, lse, tables)

    def rows_of(a):
        return a[None], pl.BlockSpec((1, TS, a.shape[1]), lambda i: (0, i, 0))

    def whole(wt):
        return wt[None], pl.BlockSpec((1,) + wt.shape, lambda i: (0, 0, 0))

    def row_blocks(d):
        return dict(grid=(T // TS,), o_spec=pl.BlockSpec((TS, d), lambda i: (i, 0)), o_shape=(T, d), o_dtype=F32)

    _, dcq_raw_bf, (d_qln,) = _mm_sum(
        "d_q_up", [(dq_pre, pl.BlockSpec((N_HEADS, TS, QK_PAD), lambda i: (0, i, 0)),
                    w_uq, pl.BlockSpec((N_HEADS, QK_PAD, Q_LORA), lambda i: (0, 0, 0)), NN, 0)],
        norm_bwd=(cq_raw, [rep["q_latent_norm"]], None), **row_blocks(Q_LORA))
    g_uq, g_dq, g_o = _wgrads("g_q", [(dq_pre, cq), (hn1, dcq_raw_bf), (o, dh3_bf)])
    ex.grad("w_uq", None, g_uq[:, :QK_NOPE + QK_ROPE].reshape(1, N_DEV, QK_NOPE + QK_ROPE, Q_LORA))
    ex.grad("w_dq", None, g_dq.reshape(1, N_DEV, D // N_DEV, Q_LORA))
    ex.grad("w_o", None, g_o.reshape(1, N_DEV, D // N_DEV, D))

    _, dckv_raw_bf, (d_kvln,) = _mm_sum(
        "d_kv_up", [(*rows_of(dkn), *whole(ex.need("w_uk", dkn)), NT, 0),
                    (*rows_of(dvv), *whole(ex.need("w_uv", dvv)), NT, 0)],
        norm_bwd=(ckv_raw, [rep["kv_latent_norm"]], None), **row_blocks(KV_LORA))
    dkr_raw_bf = _rope("dk_rope", dkr, tables, -1.0, BF16, reduce_groups=True).reshape(T, 128)
    g_uk, g_uv, g_dkv, g_kr = _wgrads("g_kv", [(ckv, dkn), (ckv, dvv), (hk, dckv_raw_bf), (dkr_raw_bf, hk)])
    ex.grad("w_uk", None, g_uk)
    ex.grad("w_uv", None, g_uv)
    ex.grad("w_dkv", None, g_dkv.reshape(1, N_DEV, D // N_DEV, KV_LORA))
    ex.grad("w_kr", None, g_kr[:QK_ROPE])

    dh2, dh2_bf, (d_an1, d_kvin) = _mm_sum(
        "d_h2", [(*rows_of(dcq_raw_bf), *whole(ex.need("w_dq", dcq_raw_bf)), NT, 0),
                 (*rows_of(dckv_raw_bf), *whole(ex.need("w_dkv", dckv_raw_bf)), NT, 1),
                 (*rows_of(dkr_raw_bf), *whole(ex.need("w_kr", dkr_raw_bf)), NN, 1)],
        norm_bwd=(h2, [attn_norm[1:2], rep["kv_in_norm"]], dh3), **row_blocks(D))
    ex.at("kv_bwd", dh2)

    dh1, dh1_bf, d_fn0, dcw0, dcb0 = _ffn_layer_bwd("f0", h1, ffn_norm[0:1], ex, ffn0, dh2, dh2_bf)
    ex.at("f0_bwd", dh1)

    ex.grad("sc_w_out", None, _mm_wgrad("g_sc_w_out", y, dh1_bf).reshape(1, N_DEV, D // N_DEV, D))
    dz, d_scw = _mixer_out_bwd(dh1_bf, ex.need("sc_w_out", dh1_bf), zb, zc, zu, ex.need("sc_conv_w", dh1_bf))
    g_in = _mm_wgrad("g_sc_w_in", hn0, dz)
    ex.grad("sc_w_in", None, g_in)
    ex.at("sc_bwd", g_in)
    ex.at("d_l0_in", g_in)
    grad_x, _, (d_an0,) = _mm_sum(
        "d_l0_in", [(*rows_of(dz), *whole(ex.need("sc_w_in", dz)), NT, 0)],
        norm_bwd=(x, [attn_norm[0:1]], dh1), **row_blocks(D))

    small = {
        "attn_norm": jnp.concatenate([d_an0, d_an1], axis=0),
        "ffn_norm": jnp.concatenate([d_fn0, d_fn1], axis=0),
        "final_norm": d_final.reshape(D),
        "kv_in_norm": d_kvin.reshape(D),
        "kv_latent_norm": d_kvln.reshape(KV_LORA),
        "q_latent_norm": d_qln,
        "ffn_conv_b": jnp.stack([dcb0, dcb1]).transpose(0, 2, 1, 3).reshape(2, D_FF),
        "sc_conv_w": d_scw,
        "ffn_conv_w": jnp.stack([dcw0, dcw1]).transpose(0, 2, 1, 3).reshape(2, 3, D_FF),
    }
    return loss, grad_x, small


def _place():
    return lax.axis_index("x"), lax.axis_index("y"), lax.axis_index("c")


def _peers():
    x, y, c = _place()
    return (x, y, 1 - c), [(1 - x, y), (x, 1 - y), (1 - x, 1 - y)]


def _window(ref, kind, dev):
    if kind == "blocked":
        return ref.at[:, dev]
    width = ref.shape[-1] // N_DEV
    return ref.at[:, pl.ds(pl.multiple_of(dev * width, 128), width)]


HBM_SPEC = pl.BlockSpec(memory_space=pltpu.HBM)
SEM_SPEC = pl.BlockSpec(memory_space=pltpu.SEMAPHORE)
EFFECT = pltpu.SideEffectType.DATAFLOW_SIDE_EFFECTING
TOKEN = jax.ShapeDtypeStruct((8, 128), F32)


def _hbm(a):
    return pltpu.with_memory_space_constraint(a, pltpu.HBM)


def _copies_start(name, jobs):
    nj = len(jobs)
    counts = [(len(srcs), len(lands)) for srcs, lands, _, _ in jobs]
    n_arr = sum(ns + nl for ns, nl in counts)

    def body(*refs):
        sems, token = refs[n_arr:n_arr + 2 * nj], refs[-1]
        at = 0
        for j, ((ns, nl), (_, _, ncopy, plan)) in enumerate(zip(counts, jobs)):
            copies = plan(refs[at:at + ns], refs[at + ns:at + ns + nl])
            assert len(copies) == ncopy
            for k, (sent, dst, to, _) in enumerate(copies):
                pltpu.make_async_remote_copy(src_ref=sent, dst_ref=dst, send_sem=sems[2 * j].at[k],
                                             recv_sem=sems[2 * j + 1].at[k], device_id=to, device_id_type=MESH).start()
            at += ns + nl
        token[...] = jnp.zeros_like(token)

    arrays = [a for srcs, lands, _, _ in jobs for a in list(srcs) + list(lands)]
    sem_shapes = [pltpu.SemaphoreType.DMA((ncopy,)) for _, _, ncopy, _ in jobs for _ in range(2)]
    outs = pl.pallas_call(
        body, name=name, in_specs=[HBM_SPEC] * n_arr,
        out_specs=[SEM_SPEC] * (2 * nj) + [HBM_SPEC] * n_arr + [VMEM_SPEC],
        out_shape=sem_shapes + [pltpu.HBM(a.shape, a.dtype) for a in arrays] + [TOKEN],
        input_output_aliases={i: 2 * nj + i for i in range(n_arr)},
        compiler_params=pltpu.CompilerParams(has_side_effects=EFFECT))(*[_hbm(a) for a in arrays])
    _Chain.last = outs[-1]
    flights, at = [], 2 * nj
    for j, (ns, nl) in enumerate(counts):
        flights.append((outs[2 * j], outs[2 * j + 1], list(outs[at:at + ns]), list(outs[at + ns:at + ns + nl])))
        at += ns + nl
    return flights


def _copies_wait(name, started, ncopy, plan):
    send, recv, srcs, lands = started
    ns, nl = len(srcs), len(lands)

    def body(*refs):
        send_ref, recv_ref, token = refs[ns + nl], refs[ns + nl + 1], refs[-1]
        copies = plan(refs[:ns], refs[ns:ns + nl])
        assert len(copies) == ncopy
        for k, (sent, _, to, landed) in enumerate(copies):
            cp = pltpu.make_async_remote_copy(src_ref=sent, dst_ref=landed, send_sem=send_ref.at[k],
                                              recv_sem=recv_ref.at[k], device_id=to, device_id_type=MESH)
            cp.wait_send()
            cp.wait_recv()
        token[...] = jnp.zeros_like(token)

    arrays = list(srcs) + list(lands)
    outs = pl.pallas_call(
        body, name=name, in_specs=[HBM_SPEC] * (ns + nl) + [SEM_SPEC] * 2 + [ANY_SPEC],
        out_specs=[HBM_SPEC] * (ns + nl) + [VMEM_SPEC], out_shape=[pltpu.HBM(a.shape, a.dtype) for a in arrays] + [TOKEN],
        input_output_aliases={i: i for i in range(ns + nl)},
        compiler_params=pltpu.CompilerParams(has_side_effects=EFFECT))(*arrays, send, recv, _Chain.last)
    _Chain.last = outs[-1]
    return list(outs[:ns]), list(outs[ns:-1])


def _plan_gather_chips(kinds):
    def plan(srcs, lands):
        x, y, c = _place()
        sibling, chips = _peers()
        out = []
        for t, kind in enumerate(kinds):
            mine = _window(lands[t], kind, 4 * x + 2 * y + c)
            out.append((srcs[t], mine, (x, y, c), mine))
            out.append((srcs[t], mine, sibling, _window(lands[t], kind, 4 * x + 2 * y + 1 - c)))
            for px, py in chips:
                out.append((srcs[t], mine, (px, py, c), _window(lands[t], kind, 4 * px + 2 * py + c)))
        return out
    return plan, 5 * len(kinds)


def _plan_gather_all(n):
    def plan(srcs, lands):
        x, y, c = _place()
        out = []
        for t in range(n):
            mine = lands[t].at[:, 4 * x + 2 * y + c]
            for m in range(N_DEV):
                px, py, pc = (1 - x if m & 4 else x), (1 - y if m & 2 else y), (1 - c if m & 1 else c)
                out.append((srcs[t], mine, (px, py, pc), lands[t].at[:, 4 * px + 2 * py + pc]))
        return out
    return plan, N_DEV * n


def _plan_gather_sibling(kinds):
    def plan(srcs, lands):
        _, _, c = _place()
        sibling, chips = _peers()
        out = []
        for t, kind in enumerate(kinds):
            for px, py in chips:
                w = _window(lands[t], kind, 4 * px + 2 * py + c)
                out.append((w, w, sibling, _window(lands[t], kind, 4 * px + 2 * py + 1 - c)))
        return out
    return plan, 3 * len(kinds)


def _plan_scatter_sibling(kinds):
    def plan(srcs, lands):
        _, _, c = _place()
        sibling, _ = _peers()
        out = []
        for t, kind in enumerate(kinds):
            for k in range(N_CHIP):
                out.append((_window(srcs[t], kind, 2 * k + 1 - c), lands[t].at[k], sibling, lands[t].at[k]))
        return out
    return plan, N_CHIP * len(kinds)


def _plan_scatter_chips(n):
    def plan(srcs, lands):
        x, y, c = _place()
        _, chips = _peers()
        out = []
        for t in range(n):
            for px, py in chips:
                out.append((srcs[t].at[2 * px + py], lands[t].at[2 * x + y], (px, py, c), lands[t].at[2 * px + py]))
        return out
    return plan, 3 * n


def _landing(shard, kind):
    if kind == "blocked":
        return lax.empty((shard.shape[0], N_DEV) + shard.shape[1:], shard.dtype)
    return lax.empty((shard.shape[0], N_DEV * shard.shape[1]), shard.dtype)


def _chip_sums(name, grads, kinds, recvs, c):
    n = len(grads)
    in_specs, out_specs, out_shape, args = [], [], [], []
    for gr, kind, rv in zip(grads, kinds, recvs):
        if kind == "blocked":
            rows, w = gr.shape[2], gr.shape[3]
            in_specs.append(pl.BlockSpec((None, None, rows, w), lambda k, cref: (0, 2 * k + cref[0], 0, 0)))
        else:
            rows, w = gr.shape[0], gr.shape[1] // N_DEV
            in_specs.append(pl.BlockSpec((rows, w), lambda k, cref: (0, 2 * k + cref[0])))
        blk = pl.BlockSpec((None, rows, w), lambda k, cref: (k, 0, 0))
        in_specs.append(blk)
        out_specs.append(blk)
        out_shape.append(jax.ShapeDtypeStruct((N_CHIP, rows, w), BF16))
        args += [gr, rv.reshape(N_CHIP, rows, w)]

    def body(*refs):
        for t in range(n):
            g_ref, r_ref, o_ref = refs[1 + 2 * t], refs[2 + 2 * t], refs[1 + 2 * n + t]
            o_ref[...] = (g_ref[...].astype(F32) + r_ref[...].astype(F32)).astype(BF16)

    return _pallas(body, name=name, n_prefetch=1, grid=(N_CHIP,), in_specs=in_specs, out_specs=out_specs,
                   out_shape=out_shape, params=_params(("parallel",)))(c, *args)


def _adamw_math(g, wv, mv, vv):
    m = ADAM_B1 * mv + (1.0 - ADAM_B1) * g
    v = ADAM_B2 * vv + (1.0 - ADAM_B2) * (g * g)
    m_hat = m / (1.0 - ADAM_B1 ** ADAM_STEP)
    v_hat = v / (1.0 - ADAM_B2 ** ADAM_STEP)
    delta = -ADAM_LR * (m_hat / (jnp.sqrt(v_hat) + ADAM_EPS) + ADAM_WD * wv)
    return delta, m, v


ADAM_STEPS = 2


def _adamw_group(name, items, chip_ids):
    n = len(items)
    in_specs, out_specs, out_shape, args, prevs = [], [], [], [chip_ids], []
    for own, recv, w3, m3, v3, layer, _ in items:
        nl, rows, w = w3.shape
        tr = rows // ADAM_STEPS
        assert tr % 16 == 0, (name, rows)
        in_specs += [pl.BlockSpec((None, tr, w), lambda i, ids, slot=slot: (ids[slot], i, 0)) for slot in range(4)]
        slab = pl.BlockSpec((None, tr, w), lambda i, ids, layer=layer: (layer, i, 0))
        in_specs += [slab] * 3
        out_specs += [slab] * 4
        out_shape += [jax.ShapeDtypeStruct((nl, rows, w), F32)] * 4
        args += [own, recv, recv, recv, w3, m3, v3]
    aliases = {}
    for t, item in enumerate(items):
        if item[6] is not None:
            for k in range(4):
                aliases[len(args) + k] = 4 * t + k
            in_specs += [ANY_SPEC] * 4
            args += list(item[6])
            prevs.append(t)
    n_in = 1 + 7 * n + 4 * len(prevs)

    def body(*refs):
        for t in range(n):
            own_ref, r1_ref, r2_ref, r3_ref, w_ref, m_ref, v_ref = refs[1 + 7 * t:8 + 7 * t]
            g_ref, d_ref, nm_ref, nv_ref = refs[n_in + 4 * t:n_in + 4 * t + 4]
            g = ((own_ref[...].astype(F32) + r1_ref[...].astype(F32)) + r2_ref[...].astype(F32)) + r3_ref[...].astype(F32)
            g_ref[...] = g
            d_ref[...], nm_ref[...], nv_ref[...] = _adamw_math(g, w_ref[...], m_ref[...], v_ref[...])

    outs = _pallas(body, name=name, n_prefetch=1, grid=(ADAM_STEPS,), in_specs=in_specs, out_specs=out_specs,
                   out_shape=out_shape, aliases=aliases, params=_params(("parallel",)))(*args)
    return [list(outs[4 * t:4 * t + 4]) for t in range(n)]


def _adamw_small(gathered, ws, ms, vs):
    n = len(gathered)
    full = [w is not None for w in ws]
    args = list(gathered)
    out_shape = []
    for t in range(n):
        shape = jax.ShapeDtypeStruct(gathered[t].shape[2:], F32)
        if full[t]:
            args += [ws[t], ms[t], vs[t]]
            out_shape += [shape] * 4
        else:
            out_shape += [shape]

    def body(*refs):
        i_in, i_out = n, len(args)
        for t in range(n):
            p_ref = refs[t]
            g = p_ref[0, 0]
            for k in range(1, N_DEV):
                g = g + p_ref[0, k]
            refs[i_out][...] = g
            if full[t]:
                w_ref, m_ref, v_ref = refs[i_in:i_in + 3]
                refs[i_out + 1][...], refs[i_out + 2][...], refs[i_out + 3][...] = _adamw_math(
                    g, w_ref[...], m_ref[...], v_ref[...])
                i_in += 3
                i_out += 4
            else:
                i_out += 1

    outs = _pallas(body, name="adamw_small", in_specs=[VMEM_SPEC] * len(args), out_specs=[VMEM_SPEC] * len(out_shape),
                   out_shape=out_shape, params=pltpu.CompilerParams(vmem_limit_bytes=VMEM_LIMIT))(*args)
    result, i = [], 0
    for t in range(n):
        k = 4 if full[t] else 1
        result.append(list(outs[i:i + k]))
        i += k
    return result


def _adamw_plain(name, gs, ws, ms, vs):
    n = len(gs)

    def body(*refs):
        for t in range(n):
            g_ref, w_ref, m_ref, v_ref = refs[4 * t:4 * t + 4]
            outs = refs[4 * n + 3 * t:4 * n + 3 * t + 3]
            outs[0][...], outs[1][...], outs[2][...] = _adamw_math(g_ref[...], w_ref[...], m_ref[...], v_ref[...])

    args, out_shape = [], []
    for g, w, m, v in zip(gs, ws, ms, vs):
        args += [g, w, m, v]
        out_shape += [jax.ShapeDtypeStruct(w.shape, F32)] * 3
    outs = _pallas(body, name=name, in_specs=[VMEM_SPEC] * len(args), out_specs=[VMEM_SPEC] * len(out_shape),
                   out_shape=out_shape, params=pltpu.CompilerParams(vmem_limit_bytes=VMEM_LIMIT))(*args)
    return [list(outs[3 * t:3 * t + 3]) for t in range(n)]


KIND = {"sc_w_in": "cols", "sc_w_out": "blocked", "w_dkv": "blocked", "w_kr": "cols", "w_uk": "cols", "w_uv": "cols",
        "w_dq": "blocked", "w_uq": "blocked", "w_o": "blocked", "ffn_w_up": "blocked", "ffn_w_down": "blocked",
        "conv": "blocked"}
GATHER_GROUPS = (("mixer", ("sc_w_in", "sc_w_out", "conv")),
                 ("up0", ("ffn_w_up0",)),
                 ("down0", ("ffn_w_down0",)),
                 ("attn", ("w_dkv", "w_kr", "w_uk", "w_uv", "w_dq", "w_uq", "w_o")),
                 ("ffn1", ("ffn_w_up1", "ffn_w_down1")))
SCATTER_GROUPS = (("ffn1", (("ffn_w_up", 1), ("ffn_w_down", 1))),
                  ("attn", (("w_o", None), ("w_uq", None), ("w_dq", None), ("w_uk", None), ("w_uv", None),
                            ("w_dkv", None), ("w_kr", None))),
                  ("ffn0", (("ffn_w_up", 0), ("ffn_w_down", 0))),
                  ("mixer", (("sc_w_out", None), ("sc_w_in", None))))
SCHEDULE = {
    "begin": (("gather_start", "mixer"),),
    "mixer_ready": (("gather_start", "up0"),),
    "l0_out": (("gather_forward", "up0"), ("gather_start", "down0")),
    "f0_up": (("gather_forward", "down0"), ("gather_start", "attn")),
    "f0_down": (("gather_forward", "attn"), ("gather_start", "ffn1")),
    "attn_fwd": (("gather_forward", "ffn1"),),
    "f1_gup": (("scatter_sibling", "ffn1"),),
    "f1_dhf": (("scatter_chips", "ffn1"),),
    "kv_bwd": (("scatter_sibling", "attn"), ("scatter_done", "ffn1")),
    "f0_dact": (("scatter_chips", "attn"),),
    "f0_gup": (("scatter_sibling", "ffn0"),),
    "f0_dhf": (("scatter_chips", "ffn0"),),
    "f0_bwd": (("scatter_done", "attn"),),
    "sc_bwd": (("scatter_sibling", "mixer"),),
    "d_l0_in": (("scatter_chips", "mixer"),),
}
FINISH = (("scatter_done", "ffn0"), ("scatter_done", "mixer"))
STAGES = {"gather_start": 1, "gather_forward": 2, "gather_done": 3,
          "scatter_sibling": 1, "scatter_chips": 2, "scatter_done": 3}
SMALL_W_ROWS = 24


def _pack(arrays, rows):
    flat = jnp.concatenate([a.reshape(-1).astype(F32) for a in arrays])
    return jnp.pad(flat, (0, rows * 128 - flat.shape[0])).reshape(rows, 128)


STORED_TRANSPOSED = ("ffn_w_up", "w_uq", "w_kr")


def _stored(name, a):
    return jnp.swapaxes(a, -1, -2) if name in STORED_TRANSPOSED else a


def _base(name):
    if name.startswith("ffn_w_") and name[-1] in "01":
        return name[:-1], int(name[-1])
    return name, None


class _Exchange:
    def __init__(self, wts, mom, var, ffn_conv_b):
        self.wts, self.mom, self.var = wts, mom, var
        x, y, c = _place()
        self.c_arr = jnp.reshape(c, (1,)).astype(jnp.int32)
        chip = 2 * x + y
        self.chip_ids = jnp.stack([chip, chip ^ 1, chip ^ 2, chip ^ 3]).astype(jnp.int32)
        self.ready = {"ffn_cb0": ffn_conv_b.reshape(2, N_FF_BLK, 1, FF_BLK)[0],
                      "ffn_cb1": ffn_conv_b.reshape(2, N_FF_BLK, 1, FF_BLK)[1]}
        self.gathers, self.group_of = {}, {}
        self.grads, self.scatters, self.results, self.queue = {}, {}, {}, []
        for gname, names in GATHER_GROUPS:
            self.gathers[gname] = dict(stage=0, names=names, kinds=[KIND[_base(nm)[0]] for nm in names])
            for nm in names:
                self.group_of[nm] = gname
        for nm in ("sc_conv_w", "ffn_cw0", "ffn_cw1"):
            self.group_of[nm] = "mixer"
        self.at("begin", None)

    def _shard(self, name):
        if name == "conv":
            return _pack([self.wts["sc_conv_w"], self.wts["ffn_conv_w"]], SMALL_W_ROWS).reshape(1, SMALL_W_ROWS, 128)
        base, layer = _base(name)
        a = _stored(base, self.wts[base])
        if layer is not None:
            a = a[layer:layer + 1]
        if KIND[base] == "cols":
            return a.reshape(a.shape[-2], a.shape[-1]).astype(BF16)
        return a.reshape((-1,) + a.shape[-2:]).astype(BF16)

    def _start(self, name, srcs, lands, ncopy, plan, st):
        self.queue.append((name, (srcs, lands, ncopy, plan), st))

    def _flush(self):
        if self.queue:
            flights = _copies_start("__".join(name for name, _, _ in self.queue), [job for _, job, _ in self.queue])
            for (_, _, st), flight in zip(self.queue, flights):
                st["flight"] = flight
            self.queue = []

    def _flight(self, st):
        self._flush()
        return st["flight"]

    def _gather_to(self, gname, stage, after):
        st = self.gathers[gname]
        if st["stage"] < 1 <= stage:
            shards = [self._shard(nm) for nm in st["names"]]
            lands = [_landing(s, kind) for s, kind in zip(shards, st["kinds"])]
            plan, ncopy = _plan_gather_chips(st["kinds"])
            self._start(f"ag_{gname}_chips", shards, lands, ncopy, plan, st)
            st["stage"] = 1
        if st["stage"] < 2 <= stage:
            plan, ncopy = _plan_gather_chips(st["kinds"])
            _, lands = _copies_wait(f"ag_{gname}_chips_wait", self._flight(st), ncopy, plan)
            plan, ncopy = _plan_gather_sibling(st["kinds"])
            self._start(f"ag_{gname}_sibling", [], lands, ncopy, plan, st)
            st["stage"] = 2
        if st["stage"] < 3 <= stage:
            plan, ncopy = _plan_gather_sibling(st["kinds"])
            _, lands = _copies_wait(f"ag_{gname}_sibling_wait", self._flight(st), ncopy, plan)
            for nm, land in zip(st["names"], lands):
                self._arrived(nm, land)
            st["stage"] = 3

    def _arrived(self, name, land):
        if name == "conv":
            conv = land.reshape(N_DEV, SMALL_W_ROWS * 128)
            self.ready["sc_conv_w"] = conv[:, :3 * 128].reshape(N_DEV, 3, 128).transpose(1, 0, 2).reshape(3, D)
            fcw = conv[:, 3 * 128:3 * 128 + 6 * 352].reshape(N_DEV, 2, 3, 352).transpose(1, 2, 0, 3)
            fcw = fcw.reshape(2, 3, N_FF_BLK, FF_BLK).transpose(0, 2, 1, 3)
            self.ready["ffn_cw0"], self.ready["ffn_cw1"] = fcw[0], fcw[1]
        elif name in ("sc_w_in", "w_uk", "w_uv") or name.startswith("ffn_w_up"):
            self.ready[name] = land
        elif name.startswith("ffn_w_down"):
            self.ready[name] = land.reshape(1, N_FF_BLK, FF_BLK, D)
        elif name == "w_kr":
            self.ready[name] = jnp.pad(land, ((0, 128 - QK_ROPE), (0, 0)))
        elif name == "w_uq":
            self.ready[name] = jnp.pad(land.reshape(N_HEADS, QK_NOPE + QK_ROPE, Q_LORA),
                                       ((0, 0), (0, QK_PAD - QK_NOPE - QK_ROPE), (0, 0)))
        else:
            self.ready[name] = land.reshape(D, land.shape[-1])

    def need(self, name, after):
        if name not in self.ready:
            self._gather_to(self.group_of[name], 3, after)
            self._flush()
        return self.ready[name]

    def grad(self, name, layer, array):
        self.grads[(name, layer)] = array

    def _scatter_to(self, gname, stage, after):
        keys = dict(SCATTER_GROUPS)[gname]
        st = self.scatters.setdefault(gname, dict(stage=0))
        kinds = [KIND[nm] for nm, _ in keys]
        if st["stage"] < 1 <= stage:
            grads = [self.grads[key] for key in keys]
            lands = []
            for gr, kind in zip(grads, kinds):
                shard = (gr.shape[0],) + gr.shape[2:] if kind == "blocked" else (gr.shape[0], gr.shape[1] // N_DEV)
                lands.append(lax.empty((N_CHIP,) + shard, BF16))
            plan, ncopy = _plan_scatter_sibling(kinds)
            self._start(f"rs_{gname}_sibling", grads, lands, ncopy, plan, st)
            st["stage"] = 1
        if st["stage"] < 2 <= stage:
            plan, ncopy = _plan_scatter_sibling(kinds)
            grads, recvs = _copies_wait(f"rs_{gname}_sibling_wait", self._flight(st), ncopy, plan)
            sums = _chip_sums(f"rs_{gname}_sums", grads, kinds, recvs, self.c_arr)
            lands = [lax.empty(s.shape, BF16) for s in sums]
            plan, ncopy = _plan_scatter_chips(len(sums))
            self._start(f"rs_{gname}_chips", sums, lands, ncopy, plan, st)
            st["stage"] = 2
        if st["stage"] < 3 <= stage:
            plan, ncopy = _plan_scatter_chips(len(keys))
            sums, recvs = _copies_wait(f"rs_{gname}_chips_wait", self._flight(st), ncopy, plan)
            items = []
            for (nm, layer), own, rv in zip(keys, sums, recvs):
                nl = 1 if layer is None else 2
                rows, w = own.shape[1], own.shape[2]
                w3, m3, v3 = (_stored(nm, src[nm]).reshape(nl, rows, w) for src in (self.wts, self.mom, self.var))
                items.append((own, rv, w3, m3, v3, 0 if layer is None else layer, self.results.get(nm)))
            outs = _adamw_group(f"adamw_{gname}", items, self.chip_ids)
            for (nm, _), out in zip(keys, outs):
                self.results[nm] = out
            st["stage"] = 3

    def at(self, place, after):
        for action, gname in SCHEDULE.get(place, ()):
            self._advance(action, gname, after)
        self._flush()

    def _advance(self, action, gname, after):
        if action.startswith("gather"):
            self._gather_to(gname, STAGES[action], after)
        else:
            self._scatter_to(gname, STAGES[action], after)

    def finish(self, after):
        for action, gname in FINISH:
            self._advance(action, gname, after)
        for gname, _ in SCATTER_GROUPS:
            self._scatter_to(gname, 3, after)
        return {nm: [_stored(nm, o.reshape(_stored(nm, self.wts[nm]).shape)) for o in outs]
                for nm, outs in self.results.items()}


REPLICATED = ("attn_norm", "ffn_norm", "final_norm", "kv_in_norm", "kv_latent_norm", "q_latent_norm", "ffn_conv_b")
WEIGHTS = ("attn_norm", "ffn_norm", "final_norm", "sc_w_in", "sc_conv_w", "sc_w_out", "kv_in_norm", "w_dkv",
           "kv_latent_norm", "w_kr", "w_uk", "w_uv", "w_dq", "q_latent_norm", "w_uq", "w_o", "ffn_w_up", "ffn_conv_w",
           "ffn_conv_b", "ffn_w_down")


def kernel(x, positions, attn_norm, ffn_norm, final_norm, sc_w_in, sc_conv_w, sc_w_out, kv_in_norm, w_dkv, kv_latent_norm, w_kr, w_uk, w_uv, w_dq, q_latent_norm, w_uq, w_o, ffn_w_up, ffn_conv_w, ffn_conv_b, ffn_w_down, loss_target, m_attn_norm, m_ffn_norm, m_final_norm, m_sc_w_in, m_sc_conv_w, m_sc_w_out, m_kv_in_norm, m_w_dkv, m_kv_latent_norm, m_w_kr, m_w_uk, m_w_uv, m_w_dq, m_q_latent_norm, m_w_uq, m_w_o, m_ffn_w_up, m_ffn_conv_w, m_ffn_conv_b, m_ffn_w_down, v_attn_norm, v_ffn_norm, v_final_norm, v_sc_w_in, v_sc_conv_w, v_sc_w_out, v_kv_in_norm, v_w_dkv, v_kv_latent_norm, v_w_kr, v_w_uk, v_w_uv, v_w_dq, v_q_latent_norm, v_w_uq, v_w_o, v_ffn_w_up, v_ffn_conv_w, v_ffn_conv_b, v_ffn_w_down):
    wts = dict(attn_norm=attn_norm, ffn_norm=ffn_norm, final_norm=final_norm, sc_w_in=sc_w_in, sc_conv_w=sc_conv_w,
               sc_w_out=sc_w_out, kv_in_norm=kv_in_norm, w_dkv=w_dkv, kv_latent_norm=kv_latent_norm, w_kr=w_kr,
               w_uk=w_uk, w_uv=w_uv, w_dq=w_dq, q_latent_norm=q_latent_norm, w_uq=w_uq, w_o=w_o, ffn_w_up=ffn_w_up,
               ffn_conv_w=ffn_conv_w, ffn_conv_b=ffn_conv_b, ffn_w_down=ffn_w_down)
    mom = dict(attn_norm=m_attn_norm, ffn_norm=m_ffn_norm, final_norm=m_final_norm, sc_w_in=m_sc_w_in,
               sc_conv_w=m_sc_conv_w, sc_w_out=m_sc_w_out, kv_in_norm=m_kv_in_norm, w_dkv=m_w_dkv,
               kv_latent_norm=m_kv_latent_norm, w_kr=m_w_kr, w_uk=m_w_uk, w_uv=m_w_uv, w_dq=m_w_dq,
               q_latent_norm=m_q_latent_norm, w_uq=m_w_uq, w_o=m_w_o, ffn_w_up=m_ffn_w_up, ffn_conv_w=m_ffn_conv_w,
               ffn_conv_b=m_ffn_conv_b, ffn_w_down=m_ffn_w_down)
    var = dict(attn_norm=v_attn_norm, ffn_norm=v_ffn_norm, final_norm=v_final_norm, sc_w_in=v_sc_w_in,
               sc_conv_w=v_sc_conv_w, sc_w_out=v_sc_w_out, kv_in_norm=v_kv_in_norm, w_dkv=v_w_dkv,
               kv_latent_norm=v_kv_latent_norm, w_kr=v_w_kr, w_uk=v_w_uk, w_uv=v_w_uv, w_dq=v_w_dq,
               q_latent_norm=v_q_latent_norm, w_uq=v_w_uq, w_o=v_w_o, ffn_w_up=v_ffn_w_up, ffn_conv_w=v_ffn_conv_w,
               ffn_conv_b=v_ffn_conv_b, ffn_w_down=v_ffn_w_down)
    xi, yi, ci = _place()
    me = 4 * xi + 2 * yi + ci
    _Chain.last = None

    ex = _Exchange(wts, mom, var, ffn_conv_b)
    rep = {
        "attn_norm": attn_norm, "ffn_norm": ffn_norm, "final_norm": final_norm,
        "kv_in_norm": kv_in_norm.reshape(1, D), "kv_latent_norm": kv_latent_norm.reshape(1, KV_LORA),
        "q_latent_norm": q_latent_norm.reshape(1, Q_LORA),
    }
    loss, grad_x, small = _local_step(x.reshape(T, D), positions.reshape(T, 1), loss_target.reshape(T, D), rep, ex)

    def rows_of(a):
        return a.reshape(-1, a.shape[-1])

    small_order = list(REPLICATED) + ["sc_conv_w", "ffn_conv_w"]
    shards = [loss.reshape(1, 1, 128)] + [rows_of(small[nm])[None] for nm in small_order]
    plan, ncopy = _plan_gather_all(len(shards))
    flight, = _copies_start("ag_small", [(shards, [lax.empty((1, N_DEV) + s.shape[1:], F32) for s in shards], ncopy, plan)])
    results = ex.finish(grad_x)
    _, gathered = _copies_wait("ag_small_wait", flight, ncopy, plan)
    params = [[None] + [rows_of(src[nm]) for nm in REPLICATED] + [None, None] for src in (wts, mom, var)]
    summed = _adamw_small(gathered, *params)
    loss_total = summed[0][0][0, 0]
    for nm, vals in zip(REPLICATED, summed[1:1 + len(REPLICATED)]):
        results[nm] = [a.reshape(wts[nm].shape) for a in vals]
    g_scw = lax.dynamic_slice(summed[-2][0], (0, me * 128), (3, 128))
    g_fcw = lax.dynamic_slice(summed[-1][0], (0, me * 352), (6, 352))
    conv = _adamw_plain("adamw_conv", [g_scw, g_fcw], *[[rows_of(src["sc_conv_w"]), rows_of(src["ffn_conv_w"])]
                                                        for src in (wts, mom, var)])
    for nm, g_own, vals in zip(("sc_conv_w", "ffn_conv_w"), (g_scw, g_fcw), conv):
        results[nm] = [a.reshape(wts[nm].shape) for a in [g_own] + vals]

    outs = [loss_total, grad_x.reshape(1, T, D)]
    for slot in range(4):
        outs.extend(results[nm][slot] for nm in WEIGHTS)
    return tuple(outs)
```

```python
import jax
import jax.numpy as jnp
from jax import lax
from jax.experimental import pallas as pl
from jax.experimental.pallas import tpu as pltpu

F32 = jnp.float32
BF16 = jnp.bfloat16

T = 2048
D = 1024
N_HEADS = 8
QK_NOPE = 128
QK_ROPE = 64
V_HEAD = 128
Q_LORA = 384
KV_LORA = 256
D_FF = 2816
CHUNK = 64
ROPE_THETA = 10000.0
EPS = 1e-6
NEG_INF = -1e30
ADAM_LR = 0.001
ADAM_B1 = 0.9
ADAM_B2 = 0.999
ADAM_EPS = 1e-08
ADAM_WD = 0.01
ADAM_STEP = 10

N_DEV = 8
N_CHIP = 4
FF_BLK = D_FF * 2 // N_DEV
N_FF_BLK = D_FF // FF_BLK
QK_PAD = 256
HALO = 16

TM = 1024
TS = 512
TR = 256
TQ = 512
VMEM_LIMIT = 56 * 1024 * 1024

NN = (((1,), (0,)), ((), ()))
NT = (((1,), (1,)), ((), ()))
TN = (((0,), (0,)), ((), ()))
MESH = pl.DeviceIdType.MESH


def _params(sem):
    return pltpu.CompilerParams(dimension_semantics=sem, vmem_limit_bytes=VMEM_LIMIT)


ANY_SPEC = pl.BlockSpec(memory_space=pl.ANY)
VMEM_SPEC = pl.BlockSpec(memory_space=pltpu.VMEM)


class _Chain:
    last = None


def _pallas(body, *, name, in_specs, out_specs, out_shape, grid=(), scratch_shapes=(), n_prefetch=0, aliases=None,
            params=None):
    def run(*args):
        after = _Chain.last
        n_lead = len(args)
        specs, operands, fn = list(in_specs), list(args), body
        if after is not None:
            def fn(*refs):
                return body(*refs[:n_lead], *refs[n_lead + 1:])
            specs.append(ANY_SPEC)
            operands.append(after)
        kw = dict(name=name, out_shape=out_shape, input_output_aliases=aliases or {})
        if params is not None:
            kw["compiler_params"] = params
        if n_prefetch:
            kw["grid_spec"] = pltpu.PrefetchScalarGridSpec(
                num_scalar_prefetch=n_prefetch, grid=grid, in_specs=specs, out_specs=out_specs,
                scratch_shapes=scratch_shapes)
        else:
            kw.update(grid=grid, in_specs=specs, out_specs=out_specs, scratch_shapes=scratch_shapes)
        outs = pl.pallas_call(fn, **kw)(*operands)
        _Chain.last = outs[0] if isinstance(outs, (list, tuple)) else outs
        return outs
    return run


def _mm(name, a, b, *, grid, a_spec, b_spec, o_spec, o_shape, o_dtype, dims, k_axis=None, acc_shape=None,
        add=None, add_spec=None):
    nk = grid[k_axis] if k_axis is not None else 1
    has_add = add is not None

    def body(*refs):
        a_ref, b_ref = refs[0], refs[1]
        p = 2
        add_ref = None
        if has_add:
            add_ref = refs[p]
            p += 1
        o_ref = refs[p]
        p += 1
        r = lax.dot_general(a_ref[...].astype(BF16), b_ref[...].astype(BF16), dims, preferred_element_type=F32)
        if k_axis is None:
            if has_add:
                r = r + add_ref[...].astype(F32)
            o_ref[...] = r.astype(o_dtype)
        else:
            acc = refs[p]
            k = pl.program_id(k_axis)

            @pl.when(k == 0)
            def _():
                acc[...] = r

            @pl.when(k > 0)
            def _():
                acc[...] += r

            @pl.when(k == nk - 1)
            def _():
                t = acc[...]
                if has_add:
                    t = t + add_ref[...].astype(F32)
                o_ref[...] = t.astype(o_dtype)

    in_specs = [a_spec, b_spec]
    args = [a, b]
    if has_add:
        in_specs.append(add_spec if add_spec is not None else o_spec)
        args.append(add)
    sem = tuple("arbitrary" if ax == k_axis else "parallel" for ax in range(len(grid)))
    scratch = [pltpu.VMEM(acc_shape, F32)] if k_axis is not None else []
    return _pallas(body, name=name, grid=grid, in_specs=in_specs, out_specs=o_spec,
                   out_shape=jax.ShapeDtypeStruct(o_shape, o_dtype), scratch_shapes=scratch, params=_params(sem))(*args)


def _mm_sum(name, parts, *, grid, o_spec, o_shape, o_dtype, add=None, norm_bwd=None):
    has_add = add is not None
    np_ = len(parts)
    nn = 1 if norm_bwd is None else len(norm_bwd[1])
    has_res = norm_bwd is not None and norm_bwd[2] is not None

    def body(*refs):
        accs = [None] * nn
        for p, (_, _, _, _, dims, n) in enumerate(parts):
            a_ref, b_ref = refs[2 * p], refs[2 * p + 1]
            for k in range(a_ref.shape[0]):
                r = lax.dot_general(a_ref[k], b_ref[k], dims, preferred_element_type=F32)
                accs[n] = r if accs[n] is None else accs[n] + r
        if norm_bwd is None:
            acc = accs[0]
            if has_add:
                acc = acc + refs[2 * np_][...]
            refs[-1][...] = acc.astype(o_dtype)
            return
        x_ref, g_refs = refs[2 * np_], refs[2 * np_ + 1:2 * np_ + 1 + nn]
        dx_ref, dxb_ref, dg_refs = refs[-2 - nn], refs[-1 - nn], refs[-nn:]
        xv = x_ref[...]
        r = lax.rsqrt(jnp.mean(xv * xv, axis=-1, keepdims=True) + EPS)
        xn = xv * r
        dx = refs[2 * np_ + 1 + nn][...] if has_res else None
        sums = []
        for acc, g_ref in zip(accs, g_refs):
            gdy = acc * g_ref[...]
            t = r * (gdy - xn * jnp.mean(gdy * xn, axis=-1, keepdims=True))
            dx = t if dx is None else dx + t
            sums.append(jnp.sum(acc * xn, axis=0, keepdims=True))
        dx_ref[...] = dx
        dxb_ref[...] = dx.astype(BF16)

        @pl.when(pl.program_id(0) == 0)
        def _():
            for dg_ref, part in zip(dg_refs, sums):
                dg_ref[...] = part

        @pl.when(pl.program_id(0) > 0)
        def _():
            for dg_ref, part in zip(dg_refs, sums):
                dg_ref[...] += part

    in_specs, args = [], []
    for a, a_spec, b, b_spec, _, _ in parts:
        in_specs += [a_spec, b_spec]
        args += [a, b]
    if norm_bwd is None:
        if has_add:
            in_specs.append(o_spec)
            args.append(add)
        return _pallas(body, name=name, grid=grid, in_specs=in_specs, out_specs=o_spec,
                       out_shape=jax.ShapeDtypeStruct(o_shape, o_dtype),
                       params=_params(("parallel",) * len(grid)))(*args)
    x, gains, dres = norm_bwd
    vec = pl.BlockSpec((1, o_shape[1]), lambda i: (0, 0))
    in_specs += [o_spec] + [vec] * nn + ([o_spec] if has_res else [])
    args += [x] + list(gains) + ([dres] if has_res else [])
    outs = _pallas(body, name=name, grid=grid, in_specs=in_specs, out_specs=[o_spec, o_spec] + [vec] * nn,
                   out_shape=[jax.ShapeDtypeStruct(o_shape, F32), jax.ShapeDtypeStruct(o_shape, BF16)]
                   + [jax.ShapeDtypeStruct((1, o_shape[1]), F32)] * nn,
                   params=_params(("arbitrary",)))(*args)
    return outs[0], outs[1], list(outs[2:])


def _mm_rows(name, a, b, dims, o_dtype, n_out, *, tn=None, add=None):
    k = a.shape[1]
    tn = n_out if tn is None else tn
    if dims == NN:
        b_spec = pl.BlockSpec((k, tn), lambda n, i: (0, n))
    else:
        b_spec = pl.BlockSpec((tn, k), lambda n, i: (n, 0))
    return _mm(name, a, b, grid=(n_out // tn, T // TM),
               a_spec=pl.BlockSpec((TM, k), lambda n, i: (i, 0)), b_spec=b_spec,
               o_spec=pl.BlockSpec((TM, tn), lambda n, i: (i, n)), o_shape=(T, n_out), o_dtype=o_dtype,
               dims=dims, add=add)


def _wgrads(name, jobs):
    arrays, index = [], {}
    for a, b in jobs:
        for arr in (a, b):
            if id(arr) not in index:
                index[id(arr)] = len(arrays)
                arrays.append(arr)
    n_in = len(arrays)

    def body(*refs):
        for t, (a, b) in enumerate(jobs):
            a_ref, b_ref, o_ref = refs[index[id(a)]], refs[index[id(b)]], refs[n_in + t]
            if a.ndim == 3:
                for h in range(a.shape[0]):
                    o_ref[h] = lax.dot_general(a_ref[h], b_ref[...], TN, preferred_element_type=F32).astype(BF16)
            else:
                o_ref[...] = lax.dot_general(a_ref[...], b_ref[...], TN, preferred_element_type=F32).astype(BF16)

    out_shape = [jax.ShapeDtypeStruct(a.shape[:-2] + (a.shape[-1], b.shape[-1]), BF16) for a, b in jobs]
    return _pallas(body, name=name, in_specs=[VMEM_SPEC] * n_in, out_specs=[VMEM_SPEC] * len(jobs), out_shape=out_shape,
                   params=pltpu.CompilerParams(vmem_limit_bytes=VMEM_LIMIT))(*arrays)


def _mm_wgrad(name, a, b, *, tn=512):
    k, n = a.shape[1], b.shape[1]
    tn = min(tn, n)
    return _mm(name, a, b, grid=(n // tn,),
               a_spec=pl.BlockSpec((T, k), lambda j: (0, 0)), b_spec=pl.BlockSpec((T, tn), lambda j: (0, j)),
               o_spec=pl.BlockSpec((k, tn), lambda j: (0, j)), o_shape=(k, n), o_dtype=BF16, dims=TN)


def _rms_fwd(name, x, g):
    d = x.shape[1]

    def body(x_ref, g_ref, o_ref):
        xv = x_ref[...]
        r = lax.rsqrt(jnp.mean(xv * xv, axis=-1, keepdims=True) + EPS)
        o_ref[...] = ((xv * r) * g_ref[...]).astype(BF16)

    return _pallas(
        body, name=name, grid=(T // TM,),
        in_specs=[pl.BlockSpec((TM, d), lambda i: (i, 0)), pl.BlockSpec((1, d), lambda i: (0, 0))],
        out_specs=pl.BlockSpec((TM, d), lambda i: (i, 0)),
        out_shape=jax.ShapeDtypeStruct((T, d), BF16), params=_params(("parallel",)))(x, g)


def _rms(xv, g):
    return (xv * lax.rsqrt(jnp.mean(xv * xv, axis=-1, keepdims=True) + EPS)) * g


def _rms_bwd(name, x, gains, dys, dres=None):
    d = x.shape[1]
    n = len(gains)
    has_res = dres is not None

    def body(*refs):
        x_ref, g_refs, dy_refs = refs[0], refs[1:1 + n], refs[1 + n:1 + 2 * n]
        dx_ref, dxb_ref = refs[-2 - n], refs[-1 - n]
        dg_refs = refs[-n:]
        xv = x_ref[...]
        r = lax.rsqrt(jnp.mean(xv * xv, axis=-1, keepdims=True) + EPS)
        xn = xv * r
        dx = refs[1 + 2 * n][...] if has_res else None
        parts = []
        for g_ref, dy_ref in zip(g_refs, dy_refs):
            dyv = dy_ref[...].astype(F32)
            gdy = dyv * g_ref[...]
            t = r * (gdy - xn * jnp.mean(gdy * xn, axis=-1, keepdims=True))
            dx = t if dx is None else dx + t
            parts.append(jnp.sum(dyv * xn, axis=0, keepdims=True))
        dx_ref[...] = dx
        dxb_ref[...] = dx.astype(BF16)

        @pl.when(pl.program_id(0) == 0)
        def _():
            for dg_ref, part in zip(dg_refs, parts):
                dg_ref[...] = part

        @pl.when(pl.program_id(0) > 0)
        def _():
            for dg_ref, part in zip(dg_refs, parts):
                dg_ref[...] += part

    row = pl.BlockSpec((TR, d), lambda i: (i, 0))
    vec = pl.BlockSpec((1, d), lambda i: (0, 0))
    args = [x] + list(gains) + list(dys) + ([dres] if has_res else [])
    in_specs = [row] + [vec] * n + [row] * n + ([row] if has_res else [])
    outs = _pallas(
        body, name=name, grid=(T // TR,), in_specs=in_specs, out_specs=[row, row] + [vec] * n,
        out_shape=[jax.ShapeDtypeStruct((T, d), F32), jax.ShapeDtypeStruct((T, d), BF16)]
        + [jax.ShapeDtypeStruct((1, d), F32)] * n,
        params=_params(("arbitrary",)))(*args)
    return outs[0], outs[1], list(outs[2:])


def _final(h, g, tgt):
    def body(h_ref, g_ref, t_ref, loss_ref, dh_ref, dhb_ref, dg_ref):
        hv = h_ref[...]
        r = lax.rsqrt(jnp.mean(hv * hv, axis=-1, keepdims=True) + EPS)
        xn = hv * r
        gv = g_ref[...]
        err = xn * gv - t_ref[...]
        part_loss = 0.5 * jnp.sum(jnp.mean(err * err, axis=-1, keepdims=True), axis=0, keepdims=True)
        dy = err * (1.0 / D)
        gdy = dy * gv
        dh = r * (gdy - xn * jnp.mean(gdy * xn, axis=-1, keepdims=True))
        dh_ref[...] = dh
        dhb_ref[...] = dh.astype(BF16)
        part = jnp.sum(dy * xn, axis=0, keepdims=True)
        first = pl.program_id(0) == 0

        @pl.when(first)
        def _():
            dg_ref[...] = part
            loss_ref[...] = jnp.broadcast_to(part_loss, (1, 128))

        @pl.when(jnp.logical_not(first))
        def _():
            dg_ref[...] += part
            loss_ref[...] += jnp.broadcast_to(part_loss, (1, 128))

    row = pl.BlockSpec((TR, D), lambda i: (i, 0))
    vec = pl.BlockSpec((1, D), lambda i: (0, 0))
    return _pallas(
        body, name="final_loss", grid=(T // TR,), in_specs=[row, vec, row],
        out_specs=[pl.BlockSpec((1, 128), lambda i: (0, 0)), row, row, vec],
        out_shape=[jax.ShapeDtypeStruct((1, 128), F32), jax.ShapeDtypeStruct((T, D), F32),
                   jax.ShapeDtypeStruct((T, D), BF16), jax.ShapeDtypeStruct((1, D), F32)],
        params=_params(("arbitrary",)))(h, g, tgt)


def _prev_idx(i, rows=TR):
    return jnp.maximum(i * (rows // HALO) - 1, 0)


def _next_idx(i, rows=TR):
    return jnp.minimum((i + 1) * (rows // HALO), T // HALO - 1)


def _causal_taps(ext):
    return pltpu.roll(ext, 2, 0)[HALO:], pltpu.roll(ext, 1, 0)[HALO:], ext[HALO:]


def _anticausal_taps(ext, n):
    rows = ext.shape[0]
    return pltpu.roll(ext, rows - 1, 0)[:n], pltpu.roll(ext, rows - 2, 0)[:n]


MIX_COLS = 512


def _mixer_in(hn, w_in, w):
    nc = D // MIX_COLS

    def body(h_ref, hh_ref, wb_ref, wc_ref, wu_ref, w_ref, b_ref, c_ref, u_ref, y_ref):
        i = pl.program_id(1)
        hv = h_ref[...]
        he = jnp.concatenate([hh_ref[...], hv], axis=0)
        ce = lax.dot_general(he, wc_ref[...], NN, preferred_element_type=F32).astype(BF16)
        ue = lax.dot_general(he, wu_ref[...], NN, preferred_element_type=F32).astype(BF16)
        bv = lax.dot_general(hv, wb_ref[...], NN, preferred_element_type=F32).astype(BF16)
        b_ref[...] = bv
        c_ref[...] = ce[HALO:]
        u_ref[...] = ue[HALO:]
        row = lax.broadcasted_iota(jnp.int32, (HALO + TS, 1), 0)
        cu = jnp.where(jnp.logical_or(i > 0, row >= HALO), ce.astype(F32) * ue.astype(F32), 0.0)
        x2, x1, x0 = _causal_taps(cu)
        wv = w_ref[...]
        cv = (x2 * wv[0:1] + x1 * wv[1:2]) + x0 * wv[2:3]
        y_ref[...] = (bv.astype(F32) * cv).astype(BF16)

    def cols(part):
        return pl.BlockSpec((D, MIX_COLS), lambda j, i: (0, part * nc + j))

    blk = pl.BlockSpec((TS, MIX_COLS), lambda j, i: (i, j))
    out = jax.ShapeDtypeStruct((T, D), BF16)
    return _pallas(
        body, name="l0_in", grid=(nc, T // TS),
        in_specs=[pl.BlockSpec((TS, D), lambda j, i: (i, 0)), pl.BlockSpec((HALO, D), lambda j, i: (_prev_idx(i, TS), 0)),
                  cols(0), cols(1), cols(2), pl.BlockSpec((3, MIX_COLS), lambda j, i: (0, j))],
        out_specs=[blk] * 4, out_shape=[out] * 4,
        params=_params(("parallel", "parallel")))(hn, hn, w_in, w_in, w_in, w)


def _mixer_out_bwd(dh, w_out, zb, zc, zu, w):
    last = T // TR - 1

    def body(dh_ref, dhn_ref, wo_ref, b_ref, bn_ref, c_ref, ch_ref, u_ref, uh_ref, w_ref, dz_ref, dw_ref):
        i = pl.program_id(0)
        dye = lax.dot_general(jnp.concatenate([dh_ref[...], dhn_ref[...]], axis=0), wo_ref[...], NT,
                              preferred_element_type=F32)
        cv_ = c_ref[...].astype(F32)
        uv = u_ref[...].astype(F32)
        cu = cv_ * uv
        cuh = jnp.where(i > 0, ch_ref[...].astype(F32) * uh_ref[...].astype(F32), 0.0)
        x2, x1, x0 = _causal_taps(jnp.concatenate([cuh, cu], axis=0))
        wv = w_ref[...]
        conv = (x2 * wv[0:1] + x1 * wv[1:2]) + x0 * wv[2:3]
        dyv = dye[:TR]
        dz_ref[:, 0:D] = (dyv * conv).astype(BF16)
        dconv = dyv * b_ref[...].astype(F32)
        dconv_n = jnp.where(i < last, dye[TR:] * bn_ref[...].astype(F32), 0.0)
        n1, n2 = _anticausal_taps(jnp.concatenate([dconv, dconv_n], axis=0), TR)
        dcu = (dconv * wv[2:3] + n1 * wv[1:2]) + n2 * wv[0:1]
        dz_ref[:, D:2 * D] = (dcu * uv).astype(BF16)
        dz_ref[:, 2 * D:3 * D] = (dcu * cv_).astype(BF16)
        part = jnp.concatenate([jnp.sum(dconv * x2, axis=0, keepdims=True),
                                jnp.sum(dconv * x1, axis=0, keepdims=True),
                                jnp.sum(dconv * x0, axis=0, keepdims=True)], axis=0)

        @pl.when(i == 0)
        def _():
            dw_ref[...] = part

        @pl.when(i > 0)
        def _():
            dw_ref[...] += part

    main = pl.BlockSpec((TR, D), lambda i: (i, 0))
    prev = pl.BlockSpec((HALO, D), lambda i: (_prev_idx(i), 0))
    nxt = pl.BlockSpec((HALO, D), lambda i: (_next_idx(i), 0))
    wspec = pl.BlockSpec((3, D), lambda i: (0, 0))
    return _pallas(
        body, name="d_l0_out", grid=(T // TR,),
        in_specs=[main, nxt, pl.BlockSpec((D, D), lambda i: (0, 0)), main, nxt, main, prev, main, prev, wspec],
        out_specs=[pl.BlockSpec((TR, 3 * D), lambda i: (i, 0)), wspec],
        out_shape=[jax.ShapeDtypeStruct((T, 3 * D), BF16), jax.ShapeDtypeStruct((3, D), F32)],
        params=_params(("arbitrary",)))(dh, dh, w_out, zb, zb, zc, zc, zu, zu, w)


def _sigmoid(x):
    return 0.5 * jnp.tanh(0.5 * x) + 0.5


def _ffn_up_act(name, hf, w_up, w, b):
    def body(h_ref, hh_ref, wg_ref, wv_ref, w_ref, b_ref, g_ref, v_ref, a_ref):
        i = pl.program_id(1)
        hv = h_ref[...]
        ge = lax.dot_general(jnp.concatenate([hh_ref[...], hv], axis=0), wg_ref[...], NT,
                             preferred_element_type=F32).astype(BF16)
        v = lax.dot_general(hv, wv_ref[...], NT, preferred_element_type=F32).astype(BF16)
        g_ref[...] = ge[HALO:]
        v_ref[...] = v
        ext = ge.astype(F32)
        row = lax.broadcasted_iota(jnp.int32, (HALO + TM, 1), 0)
        ext = jnp.where(jnp.logical_or(i > 0, row >= HALO), ext, 0.0)
        x2, x1, x0 = _causal_taps(ext)
        wv = w_ref[...]
        gc = ((x2 * wv[0:1] + x1 * wv[1:2]) + x0 * wv[2:3]) + b_ref[...]
        a_ref[...] = ((gc * _sigmoid(gc)) * v.astype(F32)).astype(BF16)

    blk = pl.BlockSpec((None, TM, FF_BLK), lambda j, i: (j, i, 0))
    out = jax.ShapeDtypeStruct((N_FF_BLK, T, FF_BLK), BF16)
    return _pallas(
        body, name=name, grid=(N_FF_BLK, T // TM),
        in_specs=[pl.BlockSpec((TM, D), lambda j, i: (i, 0)),
                  pl.BlockSpec((HALO, D), lambda j, i: (_prev_idx(i, TM), 0)),
                  pl.BlockSpec((None, None, FF_BLK, D), lambda j, i: (0, j, 0, 0)),
                  pl.BlockSpec((None, None, FF_BLK, D), lambda j, i: (0, j + N_FF_BLK, 0, 0)),
                  pl.BlockSpec((None, 3, FF_BLK), lambda j, i: (j, 0, 0)),
                  pl.BlockSpec((None, 1, FF_BLK), lambda j, i: (j, 0, 0))],
        out_specs=[blk, blk, blk], out_shape=[out, out, out],
        params=_params(("parallel", "parallel")))(hf, hf, w_up, w_up, w, b)


def _ffn_dact(name, dh, w_down4, g, v, w, b):
    last = T // TS - 1

    def body(dh_ref, dhn_ref, wd_ref, g_ref, gp_ref, gn_ref, v_ref, vn_ref, w_ref, b_ref, dg_ref, dv_ref, dw_ref, db_ref):
        i = pl.program_id(1)
        da = lax.dot_general(jnp.concatenate([dh_ref[...], dhn_ref[...]], axis=0), wd_ref[...], NT,
                             preferred_element_type=F32)
        row = lax.broadcasted_iota(jnp.int32, (TS + HALO, 1), 0)
        da = jnp.where(jnp.logical_or(i < last, row < TS), da, 0.0)
        gp = jnp.where(i > 0, gp_ref[...].astype(F32), 0.0)
        ext = jnp.concatenate([gp, g_ref[...].astype(F32), gn_ref[...].astype(F32)], axis=0)
        x2, x1, x0 = _causal_taps(ext)
        wv = w_ref[...]
        gc = ((x2 * wv[0:1] + x1 * wv[1:2]) + x0 * wv[2:3]) + b_ref[...]
        sg = _sigmoid(gc)
        vv = jnp.concatenate([v_ref[...].astype(F32), vn_ref[...].astype(F32)], axis=0)
        dv_ref[...] = (da[:TS] * (gc[:TS] * sg[:TS])).astype(BF16)
        dgc = (da * vv) * (sg * (1.0 + gc * (1.0 - sg)))
        n1, n2 = _anticausal_taps(dgc, TS)
        d0 = dgc[:TS]
        dg_ref[...] = ((d0 * wv[2:3] + n1 * wv[1:2]) + n2 * wv[0:1]).astype(BF16)
        part_w = jnp.concatenate([jnp.sum(d0 * x2[:TS], axis=0, keepdims=True),
                                  jnp.sum(d0 * x1[:TS], axis=0, keepdims=True),
                                  jnp.sum(d0 * x0[:TS], axis=0, keepdims=True)], axis=0)
        part_b = jnp.sum(d0, axis=0, keepdims=True)

        @pl.when(i == 0)
        def _():
            dw_ref[...] = part_w
            db_ref[...] = part_b

        @pl.when(i > 0)
        def _():
            dw_ref[...] += part_w
            db_ref[...] += part_b

    blk = pl.BlockSpec((None, TS, FF_BLK), lambda j, i: (j, i, 0))
    prev = pl.BlockSpec((None, HALO, FF_BLK), lambda j, i: (j, _prev_idx(i, TS), 0))
    nxt = pl.BlockSpec((None, HALO, FF_BLK), lambda j, i: (j, _next_idx(i, TS), 0))
    wspec = pl.BlockSpec((None, 3, FF_BLK), lambda j, i: (j, 0, 0))
    bspec = pl.BlockSpec((None, 1, FF_BLK), lambda j, i: (j, 0, 0))
    return _pallas(
        body, name=name, grid=(N_FF_BLK, T // TS),
        in_specs=[pl.BlockSpec((TS, D), lambda j, i: (i, 0)),
                  pl.BlockSpec((HALO, D), lambda j, i: (_next_idx(i, TS), 0)),
                  pl.BlockSpec((None, None, FF_BLK, D), lambda j, i: (0, j, 0, 0)),
                  blk, prev, nxt, blk, nxt, wspec, bspec],
        out_specs=[blk, blk, wspec, bspec],
        out_shape=[jax.ShapeDtypeStruct((N_FF_BLK, T, FF_BLK), BF16), jax.ShapeDtypeStruct((N_FF_BLK, T, FF_BLK), BF16),
                   jax.ShapeDtypeStruct((N_FF_BLK, 3, FF_BLK), F32), jax.ShapeDtypeStruct((N_FF_BLK, 1, FF_BLK), F32)],
        params=_params(("parallel", "arbitrary")))(dh, dh, w_down4, g, g, g, v, v, w, b)


def _rope_tables(pos, inv_freq):
    half = QK_ROPE // 2

    def body(p_ref, f_ref, c_ref, sa_ref, sb_ref):
        ang = p_ref[...].astype(F32) * f_ref[...]
        lane = lax.broadcasted_iota(jnp.int32, (T, 128), 1)
        c = jnp.cos(ang)
        s = jnp.sin(ang)
        c_ref[...] = jnp.where(lane < 2 * half, c, 0.0)
        sa_ref[...] = jnp.where(lane < half, -s, 0.0)
        sb_ref[...] = jnp.where(jnp.logical_and(lane >= half, lane < 2 * half), s, 0.0)

    return _pallas(
        body, name="rope_tables", in_specs=[VMEM_SPEC] * 2, out_specs=[VMEM_SPEC] * 3,
        out_shape=[jax.ShapeDtypeStruct((T, 128), F32)] * 3,
        params=pltpu.CompilerParams(vmem_limit_bytes=VMEM_LIMIT))(pos, inv_freq)


def _rotate(r, c, sa, sb, sign):
    return r * c + sign * (pltpu.roll(r, 96, 1) * sa + pltpu.roll(r, 32, 1) * sb)


def _attn_pre(h2, g_kv, g_l1, g_kvl, g_ql, w_dkv, w_kr, w_uk, w_uv, w_dq, w_uq, tables):
    def body(h_ref, c_ref, sa_ref, sb_ref, gkv_ref, gl1_ref, gkvl_ref, gql_ref, wdkv_ref, wkr_ref, wuk_ref, wuv_ref,
             wdq_ref, wuq_ref, hk_ref, hn_ref, ckvr_ref, ckv_ref, kr_ref, kn_ref, v_ref, cqr_ref, cq_ref, q_ref):
        xv = h_ref[...]
        xn = xv * lax.rsqrt(jnp.mean(xv * xv, axis=-1, keepdims=True) + EPS)
        hk = (xn * gkv_ref[...]).astype(BF16)
        hn = (xn * gl1_ref[...]).astype(BF16)
        hk_ref[...] = hk
        hn_ref[...] = hn
        cv, sav, sbv = c_ref[...], sa_ref[...], sb_ref[...]
        raw = lax.dot_general(hk, wdkv_ref[...], NN, preferred_element_type=F32)
        ckvr_ref[...] = raw
        ckv = _rms(raw, gkvl_ref[...]).astype(BF16)
        ckv_ref[...] = ckv
        kr = lax.dot_general(hk, wkr_ref[...], NT, preferred_element_type=F32)
        kr_ref[...] = _rotate(kr, cv, sav, sbv, 1.0).astype(BF16)
        kn_ref[...] = lax.dot_general(ckv, wuk_ref[...], NN, preferred_element_type=F32).astype(BF16)
        v_ref[...] = lax.dot_general(ckv, wuv_ref[...], NN, preferred_element_type=F32).astype(BF16)
        cqr = lax.dot_general(hn, wdq_ref[...], NN, preferred_element_type=F32)
        cqr_ref[...] = cqr
        cq = _rms(cqr, gql_ref[...]).astype(BF16)
        cq_ref[...] = cq
        for h in range(N_HEADS):
            r = lax.dot_general(cq, wuq_ref[h], NT, preferred_element_type=F32)
            q_ref[h, :, :QK_NOPE] = r[:, :QK_NOPE].astype(BF16)
            q_ref[h, :, QK_NOPE:] = _rotate(r[:, QK_NOPE:], cv, sav, sbv, 1.0).astype(BF16)

    def rows(d):
        return pl.BlockSpec((TS, d), lambda i: (i, 0))

    def whole(a):
        return pl.BlockSpec(a.shape, lambda i: (0,) * a.ndim)

    wholes = [g_kv, g_l1, g_kvl, g_ql, w_dkv, w_kr, w_uk, w_uv, w_dq, w_uq]
    outs = [(D, BF16), (D, BF16), (KV_LORA, F32), (KV_LORA, BF16), (128, BF16), (N_HEADS * QK_NOPE, BF16),
            (N_HEADS * V_HEAD, BF16), (Q_LORA, F32), (Q_LORA, BF16)]
    return _pallas(
        body, name="attn_pre", grid=(T // TS,),
        in_specs=[rows(D), rows(128), rows(128), rows(128)] + [whole(a) for a in wholes],
        out_specs=[rows(d) for d, _ in outs] + [pl.BlockSpec((N_HEADS, TS, QK_PAD), lambda i: (0, i, 0))],
        out_shape=[jax.ShapeDtypeStruct((T, d), dt) for d, dt in outs]
        + [jax.ShapeDtypeStruct((N_HEADS, T, QK_PAD), BF16)],
        params=_params(("parallel",)))(h2, *tables, *wholes)


SCALE = (QK_NOPE + QK_ROPE) ** -0.5
LOG2E = 1.4426950408889634
SCALE2 = SCALE * LOG2E


def _diag_mask(transposed):
    shift = CHUNK.bit_length() - 1
    a = lax.broadcasted_iota(jnp.int32, (TQ, TQ), 0) >> shift
    b = lax.broadcasted_iota(jnp.int32, (TQ, TQ), 1) >> shift
    return (a <= b) if transposed else (b <= a)


def _as_row(col):
    return jnp.transpose(jnp.broadcast_to(col, (col.shape[0], 128)), (1, 0))[0:1]


def _keys(kn_ref, kr_ref, off):
    return jnp.concatenate([kn_ref[pl.ds(off, TQ), :], kr_ref[pl.ds(off, TQ), :]], axis=1)


def _attn_fwd(q, kn, kr, v):
    hp = 2

    def body(q_ref, kn_ref, kr_ref, v_ref, o_ref, lse_ref):
        i = pl.program_id(1)
        qs = [q_ref[a] for a in range(hp)]

        def step(j, carry, masked):
            off = pl.multiple_of(j * TQ, TQ)
            krv = kr_ref[pl.ds(off, TQ), :]
            ss = []
            for a in range(hp):
                kk = jnp.concatenate([kn_ref[pl.ds(off, TQ), a * QK_NOPE:(a + 1) * QK_NOPE], krv], axis=1)
                ss.append(lax.dot_general(qs[a], kk, NT, preferred_element_type=F32))
            out = []
            for a in range(hp):
                m, l, acc = carry[a]
                s = ss[a] * SCALE2
                if masked:
                    s = jnp.where(_diag_mask(False), s, NEG_INF)
                m_new = jnp.maximum(m, jnp.max(s, axis=-1, keepdims=True))
                p = jnp.exp2(s - m_new)
                alpha = jnp.exp2(m - m_new)
                l = alpha * l + jnp.sum(p, axis=-1, keepdims=True)
                pv = lax.dot_general(p.astype(BF16), v_ref[pl.ds(off, TQ), a * V_HEAD:(a + 1) * V_HEAD], NN,
                                     preferred_element_type=F32)
                out.append((m_new, l, alpha * acc + pv))
            return tuple(out)

        one = (jnp.full((TQ, 1), NEG_INF, F32), jnp.zeros((TQ, 1), F32), jnp.zeros((TQ, V_HEAD), F32))
        carry = lax.fori_loop(0, i, lambda j, cr: step(j, cr, False), (one,) * hp)
        carry = step(i, carry, True)
        for a, (m, l, acc) in enumerate(carry):
            o_ref[:, a * V_HEAD:(a + 1) * V_HEAD] = (acc / l).astype(BF16)
            lse_ref[a] = _as_row(m + jnp.log(l) * LOG2E)

    return _pallas(
        body, name="attn_fwd", grid=(N_HEADS // hp, T // TQ),
        in_specs=[pl.BlockSpec((hp, TQ, QK_PAD), lambda h, i: (h, i, 0)),
                  pl.BlockSpec((T, hp * QK_NOPE), lambda h, i: (0, h)),
                  pl.BlockSpec((T, 128), lambda h, i: (0, 0)),
                  pl.BlockSpec((T, hp * V_HEAD), lambda h, i: (0, h))],
        out_specs=[pl.BlockSpec((TQ, hp * V_HEAD), lambda h, i: (i, h)), pl.BlockSpec((hp, 1, TQ), lambda h, i: (h, 0, i))],
        out_shape=[jax.ShapeDtypeStruct((T, N_HEADS * V_HEAD), BF16), jax.ShapeDtypeStruct((N_HEADS, 1, T), F32)],
        params=_params(("parallel", "parallel")))(q, kn, kr, v)


def _attn_bwd(q, kn, kr, v, o, do, lse_row, tables):
    nq = T // TQ
    hp = 2
    cos, sa, sb = tables

    def body(q_ref, kn_ref, kr_ref, v_ref, o_ref, do_ref, lse_ref, c_ref, sa_ref, sb_ref,
             dq_ref, dkn_ref, dkr_ref, dv_ref, dq_acc, dl_ref):
        j = pl.program_id(1)

        def cols(a):
            return slice(a * 128, (a + 1) * 128)

        @pl.when(j == 0)
        def _():
            dq_acc[...] = jnp.zeros_like(dq_acc)
            for a in range(hp):
                for i in range(nq):
                    rows = pl.ds(i * TQ, TQ)
                    prod = do_ref[rows, cols(a)].astype(F32) * o_ref[rows, cols(a)].astype(F32)
                    dl_ref[a, :, rows] = _as_row(jnp.sum(prod, axis=-1, keepdims=True))

        krv = kr_ref[...]
        kks = [jnp.concatenate([kn_ref[:, cols(a)], krv], axis=1) for a in range(hp)]
        vvs = [v_ref[:, cols(a)] for a in range(hp)]

        def step(i, carry, masked):
            off = pl.multiple_of(i * TQ, TQ)
            rows = pl.ds(off, TQ)
            qis = [q_ref[a, rows, :] for a in range(hp)]
            dois = [do_ref[rows, cols(a)] for a in range(hp)]
            sts = [lax.dot_general(kks[a], qis[a], NT, preferred_element_type=F32) for a in range(hp)]
            dpts = [lax.dot_general(vvs[a], dois[a], NT, preferred_element_type=F32) for a in range(hp)]
            out = []
            for a in range(hp):
                dk, dv = carry[a]
                st = sts[a] * SCALE2
                if masked:
                    st = jnp.where(_diag_mask(True), st, NEG_INF)
                pt = jnp.exp2(st - lse_ref[a, :, rows])
                dv = dv + lax.dot_general(pt.astype(BF16), dois[a], NN, preferred_element_type=F32)
                dst = ((pt * (dpts[a] - dl_ref[a, :, rows])) * SCALE).astype(BF16)
                dk = dk + lax.dot_general(dst, qis[a], NN, preferred_element_type=F32)
                dq_acc[a, rows, :] += lax.dot_general(dst, kks[a], TN, preferred_element_type=F32)
                out.append((dk, dv))
            return tuple(out)

        zero = (jnp.zeros((TQ, QK_PAD), F32), jnp.zeros((TQ, V_HEAD), F32))
        carry = step(j, (zero,) * hp, True)
        carry = lax.fori_loop(j + 1, nq, lambda i, cr: step(i, cr, False), carry)
        for a, (dk, dv) in enumerate(carry):
            dkn_ref[:, cols(a)] = dk[:, :QK_NOPE].astype(BF16)
            dkr_ref[a] = dk[:, QK_NOPE:]
            dv_ref[:, cols(a)] = dv.astype(BF16)

        @pl.when(j == nq - 1)
        def _():
            for a in range(hp):
                dq = dq_acc[a]
                dq_ref[a, :, :QK_NOPE] = dq[:, :QK_NOPE].astype(BF16)
                dq_ref[a, :, QK_NOPE:] = _rotate(dq[:, QK_NOPE:], c_ref[...], sa_ref[...], sb_ref[...], -1.0).astype(BF16)

    row = pl.BlockSpec((hp, 1, T), lambda h, j: (h, 0, 0))
    head = pl.BlockSpec((TQ, hp * 128), lambda h, j: (j, h))
    whole = pl.BlockSpec((hp, T, QK_PAD), lambda h, j: (h, 0, 0))
    tab = pl.BlockSpec((T, 128), lambda h, j: (0, 0))
    heads = pl.BlockSpec((T, hp * V_HEAD), lambda h, j: (0, h))
    return _pallas(
        body, name="attn_bwd", grid=(N_HEADS // hp, nq),
        in_specs=[whole, head, pl.BlockSpec((TQ, 128), lambda h, j: (j, 0)), head, heads, heads, row, tab, tab, tab],
        out_specs=[whole, head, pl.BlockSpec((hp, TQ, 128), lambda h, j: (h, j, 0)), head],
        out_shape=[jax.ShapeDtypeStruct((N_HEADS, T, QK_PAD), BF16), jax.ShapeDtypeStruct((T, N_HEADS * QK_NOPE), BF16),
                   jax.ShapeDtypeStruct((N_HEADS, T, 128), F32), jax.ShapeDtypeStruct((T, N_HEADS * V_HEAD), BF16)],
        scratch_shapes=[pltpu.VMEM((hp, T, QK_PAD), F32), pltpu.VMEM((hp, 1, T), F32)],
        params=_params(("parallel", "arbitrary")))(q, kn, kr, v, o, do, lse_row, cos, sa, sb)


def _rms_bwd_math(xv, g, dy):
    r = lax.rsqrt(jnp.mean(xv * xv, axis=-1, keepdims=True) + EPS)
    xn = xv * r
    gdy = dy * g
    return r * (gdy - xn * jnp.mean(gdy * xn, axis=-1, keepdims=True)), jnp.sum(dy * xn, axis=0, keepdims=True)


def _attn_post(dq, dkn, dv, dkr, cq_raw, ckv_raw, h2, dres, g_ql, g_kvl, g_l1, g_kv, w_uq, w_uk, w_uv, w_dq, w_dkv,
               w_kr, tables):
    def body(dq_ref, dkn_ref, dv_ref, dkr_ref, cqr_ref, ckvr_ref, h_ref, res_ref, c_ref, sa_ref, sb_ref,
             gql_ref, gkvl_ref, gl1_ref, gkv_ref, wuq_ref, wuk_ref, wuv_ref, wdq_ref, wdkv_ref, wkr_ref,
             dcq_ref, dckv_ref, dkrr_ref, dh_ref, dhb_ref, dgql_ref, dgkvl_ref, dgl1_ref, dgkv_ref):
        dcq = lax.dot_general(dq_ref[0], wuq_ref[0], NN, preferred_element_type=F32)
        for h in range(1, N_HEADS):
            dcq = dcq + lax.dot_general(dq_ref[h], wuq_ref[h], NN, preferred_element_type=F32)
        dcq_raw, s_ql = _rms_bwd_math(cqr_ref[...], gql_ref[...], dcq)
        dcq_raw = dcq_raw.astype(BF16)
        dcq_ref[...] = dcq_raw
        dckv = (lax.dot_general(dkn_ref[...], wuk_ref[...], NT, preferred_element_type=F32)
                + lax.dot_general(dv_ref[...], wuv_ref[...], NT, preferred_element_type=F32))
        dckv_raw, s_kvl = _rms_bwd_math(ckvr_ref[...], gkvl_ref[...], dckv)
        dckv_raw = dckv_raw.astype(BF16)
        dckv_ref[...] = dckv_raw
        dkr = dkr_ref[0]
        for h in range(1, N_HEADS):
            dkr = dkr + dkr_ref[h]
        dkr_raw = _rotate(dkr, c_ref[...], sa_ref[...], sb_ref[...], -1.0).astype(BF16)
        dkrr_ref[...] = dkr_raw
        d_hn = lax.dot_general(dcq_raw, wdq_ref[...], NT, preferred_element_type=F32)
        d_hk = (lax.dot_general(dckv_raw, wdkv_ref[...], NT, preferred_element_type=F32)
                + lax.dot_general(dkr_raw, wkr_ref[...], NN, preferred_element_type=F32))
        xv = h_ref[...]
        r = lax.rsqrt(jnp.mean(xv * xv, axis=-1, keepdims=True) + EPS)
        xn = xv * r
        dx = res_ref[...]
        sums = [s_ql, s_kvl]
        for dy, g_ref in ((d_hn, gl1_ref), (d_hk, gkv_ref)):
            gdy = dy * g_ref[...]
            dx = dx + r * (gdy - xn * jnp.mean(gdy * xn, axis=-1, keepdims=True))
            sums.append(jnp.sum(dy * xn, axis=0, keepdims=True))
        dh_ref[...] = dx
        dhb_ref[...] = dx.astype(BF16)
        dg_refs = (dgql_ref, dgkvl_ref, dgl1_ref, dgkv_ref)

        @pl.when(pl.program_id(0) == 0)
        def _():
            for dg_ref, part in zip(dg_refs, sums):
                dg_ref[...] = part

        @pl.when(pl.program_id(0) > 0)
        def _():
            for dg_ref, part in zip(dg_refs, sums):
                dg_ref[...] += part

    def rows(d):
        return pl.BlockSpec((TS, d), lambda i: (i, 0))

    def heads(d):
        return pl.BlockSpec((N_HEADS, TS, d), lambda i: (0, i, 0))

    def whole(a):
        return pl.BlockSpec(a.shape, lambda i: (0,) * a.ndim)

    wholes = [g_ql, g_kvl, g_l1, g_kv, w_uq, w_uk, w_uv, w_dq, w_dkv, w_kr]
    vecs = [Q_LORA, KV_LORA, D, D]
    return _pallas(
        body, name="attn_post", grid=(T // TS,),
        in_specs=[heads(QK_PAD), rows(N_HEADS * QK_NOPE), rows(N_HEADS * V_HEAD), heads(128), rows(Q_LORA),
                  rows(KV_LORA), rows(D), rows(D), rows(128), rows(128), rows(128)] + [whole(a) for a in wholes],
        out_specs=[rows(Q_LORA), rows(KV_LORA), rows(128), rows(D), rows(D)]
        + [pl.BlockSpec((1, d), lambda i: (0, 0)) for d in vecs],
        out_shape=[jax.ShapeDtypeStruct((T, Q_LORA), BF16), jax.ShapeDtypeStruct((T, KV_LORA), BF16),
                   jax.ShapeDtypeStruct((T, 128), BF16), jax.ShapeDtypeStruct((T, D), F32),
                   jax.ShapeDtypeStruct((T, D), BF16)] + [jax.ShapeDtypeStruct((1, d), F32) for d in vecs],
        params=_params(("arbitrary",)))(dq, dkn, dv, dkr, cq_raw, ckv_raw, h2, dres, *tables, *wholes)


def _ffn_gup(name, dg, dv, hf):
    def body(dg_ref, dv_ref, hf_ref, o_ref):
        j = pl.program_id(0)

        @pl.when(j < N_FF_BLK)
        def _():
            o_ref[...] = lax.dot_general(dg_ref[...], hf_ref[...], TN, preferred_element_type=F32).astype(BF16)

        @pl.when(j >= N_FF_BLK)
        def _():
            o_ref[...] = lax.dot_general(dv_ref[...], hf_ref[...], TN, preferred_element_type=F32).astype(BF16)

    return _pallas(
        body, name=name, grid=(N_DEV,),
        in_specs=[pl.BlockSpec((None, T, FF_BLK), lambda j: (jnp.minimum(j, N_FF_BLK - 1), 0, 0)),
                  pl.BlockSpec((None, T, FF_BLK), lambda j: (jnp.maximum(j - N_FF_BLK, 0), 0, 0)),
                  pl.BlockSpec((T, D), lambda j: (0, 0))],
        out_specs=pl.BlockSpec((None, FF_BLK, D), lambda j: (j, 0, 0)),
        out_shape=jax.ShapeDtypeStruct((N_DEV, FF_BLK, D), BF16), params=_params(("parallel",)))(dg, dv, hf)


def _ffn_layer_fwd(tag, h, gain, ex):
    hf = _rms_fwd(f"{tag}_norm", h, gain)
    g, v, act = _ffn_up_act(f"{tag}_up", hf, ex.need(f"ffn_w_up{tag[1]}", hf), ex.need(f"ffn_cw{tag[1]}", hf),
                            ex.need(f"ffn_cb{tag[1]}", hf))
    ex.at(f"{tag}_up", act)
    rows = pl.BlockSpec((TS, D), lambda i: (i, 0))
    out = _mm_sum(f"{tag}_down",
                  [(act, pl.BlockSpec((N_FF_BLK, TS, FF_BLK), lambda i: (0, i, 0)), ex.need(f"ffn_w_down{tag[1]}", act),
                    pl.BlockSpec((None, N_FF_BLK, FF_BLK, D), lambda i: (0, 0, 0, 0)), NN, 0)],
                  grid=(T // TS,), o_spec=rows, o_shape=(T, D), o_dtype=F32, add=h)
    ex.at(f"{tag}_down", out)
    return out, (hf, g, v, act)


def _ffn_layer_bwd(tag, h, gain, ex, saved, dh, dh_bf):
    hf, g, v, act = saved
    layer = tag[1]
    w_up, w_down4 = ex.need(f"ffn_w_up{layer}", dh_bf), ex.need(f"ffn_w_down{layer}", dh_bf)
    dg, dv, dcw, dcb = _ffn_dact(f"{tag}_dact", dh_bf, w_down4, g, v, ex.need(f"ffn_cw{layer}", dh_bf),
                                 ex.need(f"ffn_cb{layer}", dh_bf))
    ex.at(f"{tag}_dact", dg)
    g_down = _mm(f"{tag}_gdown", act, dh_bf, grid=(N_FF_BLK,),
                 a_spec=pl.BlockSpec((None, T, FF_BLK), lambda j: (j, 0, 0)),
                 b_spec=pl.BlockSpec((T, D), lambda j: (0, 0)),
                 o_spec=pl.BlockSpec((FF_BLK, D), lambda j: (j, 0)),
                 o_shape=(D_FF, D), o_dtype=BF16, dims=TN)
    g_up = _ffn_gup(f"{tag}_gup", dg, dv, hf)
    ex.grad("ffn_w_up", int(layer), g_up.reshape(1, N_DEV, FF_BLK, D))
    ex.grad("ffn_w_down", int(layer), g_down.reshape(1, N_DEV, D_FF // N_DEV, D))
    ex.at(f"{tag}_gup", g_up)
    part = pl.BlockSpec((N_FF_BLK, TR, FF_BLK), lambda i: (0, i, 0))
    dh_in, dh_in_bf, dgain = _mm_sum(
        f"{tag}_dhf",
        [(dg, part, w_up, pl.BlockSpec((None, N_FF_BLK, FF_BLK, D), lambda i: (0, 0, 0, 0)), NN, 0),
         (dv, part, w_up, pl.BlockSpec((None, N_FF_BLK, FF_BLK, D), lambda i: (0, 1, 0, 0)), NN, 0)],
        grid=(T // TR,), o_spec=pl.BlockSpec((TR, D), lambda i: (i, 0)), o_shape=(T, D), o_dtype=F32,
        norm_bwd=(h, [gain], dh))
    ex.at(f"{tag}_dhf", dh_in)
    return dh_in, dh_in_bf, dgain[0], dcw, dcb


def _local_step(x, pos, tgt, rep, ex):
    attn_norm, ffn_norm, final_norm = rep["attn_norm"], rep["ffn_norm"], rep["final_norm"]
    half = QK_ROPE // 2
    inv = 1.0 / (ROPE_THETA ** (jnp.arange(half, dtype=F32) / half))
    inv_freq = jnp.concatenate([inv, inv, jnp.zeros((128 - 2 * half,), F32)]).reshape(1, 128)
    tables = _rope_tables(pos, inv_freq)

    hn0 = _rms_fwd("l0_norm", x, attn_norm[0:1])
    w_in = ex.need("sc_w_in", hn0)
    ex.at("mixer_ready", hn0)
    zb, zc, zu, y = _mixer_in(hn0, w_in, ex.need("sc_conv_w", hn0))
    ex.at("l0_in", y)
    h1 = _mm_rows("l0_out", y, ex.need("sc_w_out", y), NN, F32, D, tn=512, add=x)
    ex.at("l0_out", h1)
    h2, ffn0 = _ffn_layer_fwd("f0", h1, ffn_norm[0:1], ex)

    w_uq = ex.need("w_uq", h2)
    hk, hn1, ckv_raw, ckv, kr, kn, vv, cq_raw, cq, q = _attn_pre(
        h2, rep["kv_in_norm"], attn_norm[1:2], rep["kv_latent_norm"], rep["q_latent_norm"], ex.need("w_dkv", h2),
        ex.need("w_kr", h2), ex.need("w_uk", h2), ex.need("w_uv", h2), ex.need("w_dq", h2), w_uq, tables)

    o, lse = _attn_fwd(q, kn, kr, vv)
    ex.at("attn_fwd", o)
    w_o = ex.need("w_o", o)
    h3 = _mm_rows("attn_out", o, w_o, NN, F32, D, tn=512, add=h2)
    h4, ffn1 = _ffn_layer_fwd("f1", h3, ffn_norm[1:2], ex)

    loss, dh4, dh4_bf, d_final = _final(h4, final_norm.reshape(1, D), tgt)

    dh3, dh3_bf, d_fn1, dcw1, dcb1 = _ffn_layer_bwd("f1", h3, ffn_norm[1:2], ex, ffn1, dh4, dh4_bf)
    ex.at("f1_bwd", dh3)

    do = _mm_rows("d_attn_out", dh3_bf, w_o, NT, BF16, N_HEADS * V_HEAD)
    dq_pre, dkn, dkr, dvv = _attn_bwd(q, kn, kr, vv, o, do, lse, tables)

    dcq_raw_bf, dckv_raw_bf, dkr_raw_bf, dh2, dh2_bf, d_qln, d_kvln, d_an1, d_kvin = _attn_post(
        dq_pre, dkn, dvv, dkr, cq_raw, ckv_raw, h2, dh3, rep["q_latent_norm"], rep["kv_latent_norm"], attn_norm[1:2],
        rep["kv_in_norm"], w_uq, ex.need("w_uk", dkn), ex.need("w_uv", dvv), ex.need("w_dq", dq_pre),
        ex.need("w_dkv", dkn), ex.need("w_kr", dkr), tables)
    g_uq, g_dq, g_o = _wgrads("g_q", [(dq_pre, cq), (hn1, dcq_raw_bf), (o, dh3_bf)])
    ex.grad("w_uq", None, g_uq[:, :QK_NOPE + QK_ROPE].reshape(1, N_DEV, QK_NOPE + QK_ROPE, Q_LORA))
    ex.grad("w_dq", None, g_dq.reshape(1, N_DEV, D // N_DEV, Q_LORA))
    ex.grad("w_o", None, g_o.reshape(1, N_DEV, D // N_DEV, D))

    g_uk, g_uv, g_dkv, g_kr = _wgrads("g_kv", [(ckv, dkn), (ckv, dvv), (hk, dckv_raw_bf), (dkr_raw_bf, hk)])
    ex.grad("w_uk", None, g_uk)
    ex.grad("w_uv", None, g_uv)
    ex.grad("w_dkv", None, g_dkv.reshape(1, N_DEV, D // N_DEV, KV_LORA))
    ex.grad("w_kr", None, g_kr[:QK_ROPE])
    ex.at("kv_bwd", dh2)

    dh1, dh1_bf, d_fn0, dcw0, dcb0 = _ffn_layer_bwd("f0", h1, ffn_norm[0:1], ex, ffn0, dh2, dh2_bf)
    ex.at("f0_bwd", dh1)

    ex.grad("sc_w_out", None, _mm_wgrad("g_sc_w_out", y, dh1_bf).reshape(1, N_DEV, D // N_DEV, D))
    dz, d_scw = _mixer_out_bwd(dh1_bf, ex.need("sc_w_out", dh1_bf), zb, zc, zu, ex.need("sc_conv_w", dh1_bf))
    g_in = _mm_wgrad("g_sc_w_in", hn0, dz)
    ex.grad("sc_w_in", None, g_in)
    ex.at("sc_bwd", g_in)
    ex.at("d_l0_in", g_in)
    w_in = ex.need("sc_w_in", dz)
    grad_x, _, (d_an0,) = _mm_sum(
        "d_l0_in", [(dz[None], pl.BlockSpec((1, TS, dz.shape[1]), lambda i: (0, i, 0)),
                     w_in[None], pl.BlockSpec((1,) + w_in.shape, lambda i: (0, 0, 0)), NT, 0)],
        norm_bwd=(x, [attn_norm[0:1]], dh1),
        grid=(T // TS,), o_spec=pl.BlockSpec((TS, D), lambda i: (i, 0)), o_shape=(T, D), o_dtype=F32)

    small = {
        "attn_norm": jnp.concatenate([d_an0, d_an1], axis=0),
        "ffn_norm": jnp.concatenate([d_fn0, d_fn1], axis=0),
        "final_norm": d_final.reshape(D),
        "kv_in_norm": d_kvin.reshape(D),
        "kv_latent_norm": d_kvln.reshape(KV_LORA),
        "q_latent_norm": d_qln,
        "ffn_conv_b": jnp.stack([dcb0, dcb1]).transpose(0, 2, 1, 3).reshape(2, D_FF),
        "sc_conv_w": d_scw,
        "ffn_conv_w": jnp.stack([dcw0, dcw1]).transpose(0, 2, 1, 3).reshape(2, 3, D_FF),
    }
    return loss, grad_x, small


def _place():
    return lax.axis_index("x"), lax.axis_index("y"), lax.axis_index("c")


def _peers():
    x, y, c = _place()
    return (x, y, 1 - c), [(1 - x, y), (x, 1 - y), (1 - x, 1 - y)]


def _window(ref, kind, dev):
    if kind == "blocked":
        return ref.at[:, dev]
    width = ref.shape[-1] // N_DEV
    return ref.at[:, pl.ds(pl.multiple_of(dev * width, 128), width)]


HBM_SPEC = pl.BlockSpec(memory_space=pltpu.HBM)
SEM_SPEC = pl.BlockSpec(memory_space=pltpu.SEMAPHORE)
EFFECT = pltpu.SideEffectType.DATAFLOW_SIDE_EFFECTING
TOKEN = jax.ShapeDtypeStruct((8, 128), F32)


def _hbm(a):
    return pltpu.with_memory_space_constraint(a, pltpu.HBM)


def _copies_start(name, jobs):
    nj = len(jobs)
    counts = [(len(srcs), len(lands)) for srcs, lands, _, _ in jobs]
    n_arr = sum(ns + nl for ns, nl in counts)

    def body(*refs):
        sems, token = refs[n_arr:n_arr + 2 * nj], refs[-1]
        at = 0
        for j, ((ns, nl), (_, _, ncopy, plan)) in enumerate(zip(counts, jobs)):
            copies = plan(refs[at:at + ns], refs[at + ns:at + ns + nl])
            assert len(copies) == ncopy
            for k, (sent, dst, to, _) in enumerate(copies):
                pltpu.make_async_remote_copy(src_ref=sent, dst_ref=dst, send_sem=sems[2 * j].at[k],
                                             recv_sem=sems[2 * j + 1].at[k], device_id=to, device_id_type=MESH).start()
            at += ns + nl
        token[...] = jnp.zeros_like(token)

    arrays = [a for srcs, lands, _, _ in jobs for a in list(srcs) + list(lands)]
    sem_shapes = [pltpu.SemaphoreType.DMA((ncopy,)) for _, _, ncopy, _ in jobs for _ in range(2)]
    outs = pl.pallas_call(
        body, name=name, in_specs=[HBM_SPEC] * n_arr,
        out_specs=[SEM_SPEC] * (2 * nj) + [HBM_SPEC] * n_arr + [VMEM_SPEC],
        out_shape=sem_shapes + [pltpu.HBM(a.shape, a.dtype) for a in arrays] + [TOKEN],
        input_output_aliases={i: 2 * nj + i for i in range(n_arr)},
        compiler_params=pltpu.CompilerParams(has_side_effects=EFFECT))(*[_hbm(a) for a in arrays])
    _Chain.last = outs[-1]
    flights, at = [], 2 * nj
    for j, (ns, nl) in enumerate(counts):
        flights.append((outs[2 * j], outs[2 * j + 1], list(outs[at:at + ns]), list(outs[at + ns:at + ns + nl])))
        at += ns + nl
    return flights


def _copies_wait(name, started, ncopy, plan):
    send, recv, srcs, lands = started
    ns, nl = len(srcs), len(lands)

    def body(*refs):
        send_ref, recv_ref, token = refs[ns + nl], refs[ns + nl + 1], refs[-1]
        copies = plan(refs[:ns], refs[ns:ns + nl])
        assert len(copies) == ncopy
        for k, (sent, _, to, landed) in enumerate(copies):
            cp = pltpu.make_async_remote_copy(src_ref=sent, dst_ref=landed, send_sem=send_ref.at[k],
                                              recv_sem=recv_ref.at[k], device_id=to, device_id_type=MESH)
            cp.wait_send()
            cp.wait_recv()
        token[...] = jnp.zeros_like(token)

    arrays = list(srcs) + list(lands)
    outs = pl.pallas_call(
        body, name=name, in_specs=[HBM_SPEC] * (ns + nl) + [SEM_SPEC] * 2 + [ANY_SPEC],
        out_specs=[HBM_SPEC] * (ns + nl) + [VMEM_SPEC], out_shape=[pltpu.HBM(a.shape, a.dtype) for a in arrays] + [TOKEN],
        input_output_aliases={i: i for i in range(ns + nl)},
        compiler_params=pltpu.CompilerParams(has_side_effects=EFFECT))(*arrays, send, recv, _Chain.last)
    _Chain.last = outs[-1]
    return list(outs[:ns]), list(outs[ns:-1])


def _plan_gather_chips(kinds):
    def plan(srcs, lands):
        x, y, c = _place()
        sibling, chips = _peers()
        out = []
        for t, kind in enumerate(kinds):
            mine = _window(lands[t], kind, 4 * x + 2 * y + c)
            out.append((srcs[t], mine, (x, y, c), mine))
            out.append((srcs[t], mine, sibling, _window(lands[t], kind, 4 * x + 2 * y + 1 - c)))
            for px, py in chips:
                out.append((srcs[t], mine, (px, py, c), _window(lands[t], kind, 4 * px + 2 * py + c)))
        return out
    return plan, 5 * len(kinds)


def _plan_gather_all(n):
    def plan(srcs, lands):
        x, y, c = _place()
        out = []
        for t in range(n):
            mine = lands[t].at[:, 4 * x + 2 * y + c]
            for m in range(N_DEV):
                px, py, pc = (1 - x if m & 4 else x), (1 - y if m & 2 else y), (1 - c if m & 1 else c)
                out.append((srcs[t], mine, (px, py, pc), lands[t].at[:, 4 * px + 2 * py + pc]))
        return out
    return plan, N_DEV * n


def _plan_gather_sibling(kinds):
    def plan(srcs, lands):
        _, _, c = _place()
        sibling, chips = _peers()
        out = []
        for t, kind in enumerate(kinds):
            for px, py in chips:
                w = _window(lands[t], kind, 4 * px + 2 * py + c)
                out.append((w, w, sibling, _window(lands[t], kind, 4 * px + 2 * py + 1 - c)))
        return out
    return plan, 3 * len(kinds)


def _plan_scatter_sibling(kinds):
    def plan(srcs, lands):
        _, _, c = _place()
        sibling, _ = _peers()
        out = []
        for t, kind in enumerate(kinds):
            for k in range(N_CHIP):
                out.append((_window(srcs[t], kind, 2 * k + 1 - c), lands[t].at[k], sibling, lands[t].at[k]))
        return out
    return plan, N_CHIP * len(kinds)


def _plan_scatter_chips(n):
    def plan(srcs, lands):
        x, y, c = _place()
        _, chips = _peers()
        out = []
        for t in range(n):
            for px, py in chips:
                out.append((srcs[t].at[2 * px + py], lands[t].at[2 * x + y], (px, py, c), lands[t].at[2 * px + py]))
        return out
    return plan, 3 * n


def _landing(shard, kind):
    if kind == "blocked":
        return lax.empty((shard.shape[0], N_DEV) + shard.shape[1:], shard.dtype)
    return lax.empty((shard.shape[0], N_DEV * shard.shape[1]), shard.dtype)


def _chip_sums(name, grads, kinds, recvs, c):
    n = len(grads)
    in_specs, out_specs, out_shape, args = [], [], [], []
    for gr, kind, rv in zip(grads, kinds, recvs):
        if kind == "blocked":
            rows, w = gr.shape[2], gr.shape[3]
            in_specs.append(pl.BlockSpec((None, None, rows, w), lambda k, cref: (0, 2 * k + cref[0], 0, 0)))
        else:
            rows, w = gr.shape[0], gr.shape[1] // N_DEV
            in_specs.append(pl.BlockSpec((rows, w), lambda k, cref: (0, 2 * k + cref[0])))
        blk = pl.BlockSpec((None, rows, w), lambda k, cref: (k, 0, 0))
        in_specs.append(blk)
        out_specs.append(blk)
        out_shape.append(jax.ShapeDtypeStruct((N_CHIP, rows, w), BF16))
        args += [gr, rv.reshape(N_CHIP, rows, w)]

    def body(*refs):
        for t in range(n):
            g_ref, r_ref, o_ref = refs[1 + 2 * t], refs[2 + 2 * t], refs[1 + 2 * n + t]
            o_ref[...] = (g_ref[...].astype(F32) + r_ref[...].astype(F32)).astype(BF16)

    return _pallas(body, name=name, n_prefetch=1, grid=(N_CHIP,), in_specs=in_specs, out_specs=out_specs,
                   out_shape=out_shape, params=_params(("parallel",)))(c, *args)


def _adamw_math(g, wv, mv, vv):
    m = ADAM_B1 * mv + (1.0 - ADAM_B1) * g
    v = ADAM_B2 * vv + (1.0 - ADAM_B2) * (g * g)
    m_hat = m / (1.0 - ADAM_B1 ** ADAM_STEP)
    v_hat = v / (1.0 - ADAM_B2 ** ADAM_STEP)
    delta = -ADAM_LR * (m_hat / (jnp.sqrt(v_hat) + ADAM_EPS) + ADAM_WD * wv)
    return delta, m, v


ADAM_STEPS = 2


def _adamw_group(name, items, chip_ids):
    n = len(items)
    in_specs, out_specs, out_shape, args, prevs = [], [], [], [chip_ids], []
    for own, recv, w3, m3, v3, layer, _ in items:
        nl, rows, w = w3.shape
        tr = rows // ADAM_STEPS
        assert tr % 16 == 0, (name, rows)
        in_specs += [pl.BlockSpec((None, tr, w), lambda i, ids, slot=slot: (ids[slot], i, 0)) for slot in range(4)]
        slab = pl.BlockSpec((None, tr, w), lambda i, ids, layer=layer: (layer, i, 0))
        in_specs += [slab] * 3
        out_specs += [slab] * 4
        out_shape += [jax.ShapeDtypeStruct((nl, rows, w), F32)] * 4
        args += [own, recv, recv, recv, w3, m3, v3]
    aliases = {}
    for t, item in enumerate(items):
        if item[6] is not None:
            for k in range(4):
                aliases[len(args) + k] = 4 * t + k
            in_specs += [ANY_SPEC] * 4
            args += list(item[6])
            prevs.append(t)
    n_in = 1 + 7 * n + 4 * len(prevs)

    def body(*refs):
        for t in range(n):
            own_ref, r1_ref, r2_ref, r3_ref, w_ref, m_ref, v_ref = refs[1 + 7 * t:8 + 7 * t]
            g_ref, d_ref, nm_ref, nv_ref = refs[n_in + 4 * t:n_in + 4 * t + 4]
            g = ((own_ref[...].astype(F32) + r1_ref[...].astype(F32)) + r2_ref[...].astype(F32)) + r3_ref[...].astype(F32)
            g_ref[...] = g
            d_ref[...], nm_ref[...], nv_ref[...] = _adamw_math(g, w_ref[...], m_ref[...], v_ref[...])

    outs = _pallas(body, name=name, n_prefetch=1, grid=(ADAM_STEPS,), in_specs=in_specs, out_specs=out_specs,
                   out_shape=out_shape, aliases=aliases, params=_params(("parallel",)))(*args)
    return [list(outs[4 * t:4 * t + 4]) for t in range(n)]


def _adamw_small(gathered, ws, ms, vs):
    n = len(gathered)
    full = [w is not None for w in ws]
    args = list(gathered)
    out_shape = []
    for t in range(n):
        shape = jax.ShapeDtypeStruct(gathered[t].shape[2:], F32)
        if full[t]:
            args += [ws[t], ms[t], vs[t]]
            out_shape += [shape] * 4
        else:
            out_shape += [shape]

    def body(*refs):
        i_in, i_out = n, len(args)
        for t in range(n):
            p_ref = refs[t]
            g = p_ref[0, 0]
            for k in range(1, N_DEV):
                g = g + p_ref[0, k]
            refs[i_out][...] = g
            if full[t]:
                w_ref, m_ref, v_ref = refs[i_in:i_in + 3]
                refs[i_out + 1][...], refs[i_out + 2][...], refs[i_out + 3][...] = _adamw_math(
                    g, w_ref[...], m_ref[...], v_ref[...])
                i_in += 3
                i_out += 4
            else:
                i_out += 1

    outs = _pallas(body, name="adamw_small", in_specs=[VMEM_SPEC] * len(args), out_specs=[VMEM_SPEC] * len(out_shape),
                   out_shape=out_shape, params=pltpu.CompilerParams(vmem_limit_bytes=VMEM_LIMIT))(*args)
    result, i = [], 0
    for t in range(n):
        k = 4 if full[t] else 1
        result.append(list(outs[i:i + k]))
        i += k
    return result


def _adamw_plain(name, gs, ws, ms, vs):
    n = len(gs)

    def body(*refs):
        for t in range(n):
            g_ref, w_ref, m_ref, v_ref = refs[4 * t:4 * t + 4]
            outs = refs[4 * n + 3 * t:4 * n + 3 * t + 3]
            outs[0][...], outs[1][...], outs[2][...] = _adamw_math(g_ref[...], w_ref[...], m_ref[...], v_ref[...])

    args, out_shape = [], []
    for g, w, m, v in zip(gs, ws, ms, vs):
        args += [g, w, m, v]
        out_shape += [jax.ShapeDtypeStruct(w.shape, F32)] * 3
    outs = _pallas(body, name=name, in_specs=[VMEM_SPEC] * len(args), out_specs=[VMEM_SPEC] * len(out_shape),
                   out_shape=out_shape, params=pltpu.CompilerParams(vmem_limit_bytes=VMEM_LIMIT))(*args)
    return [list(outs[3 * t:3 * t + 3]) for t in range(n)]


KIND = {"sc_w_in": "cols", "sc_w_out": "blocked", "w_dkv": "blocked", "w_kr": "cols", "w_uk": "cols", "w_uv": "cols",
        "w_dq": "blocked", "w_uq": "blocked", "w_o": "blocked", "ffn_w_up": "blocked", "ffn_w_down": "blocked",
        "conv": "blocked"}
GATHER_GROUPS = (("mixer", ("sc_w_in", "sc_w_out", "conv")),
                 ("up0", ("ffn_w_up0",)),
                 ("down0", ("ffn_w_down0",)),
                 ("attn", ("w_dkv", "w_kr", "w_uk", "w_uv", "w_dq", "w_uq", "w_o")),
                 ("ffn1", ("ffn_w_up1", "ffn_w_down1")))
SCATTER_GROUPS = (("ffn1", (("ffn_w_up", 1), ("ffn_w_down", 1))),
                  ("attn", (("w_o", None), ("w_uq", None), ("w_dq", None), ("w_uk", None), ("w_uv", None),
                            ("w_dkv", None), ("w_kr", None))),
                  ("ffn0", (("ffn_w_up", 0), ("ffn_w_down", 0))),
                  ("mixer", (("sc_w_out", None), ("sc_w_in", None))))
SCHEDULE = {
    "begin": (("gather_start", "mixer"),),
    "mixer_ready": (("gather_start", "up0"),),
    "l0_out": (("gather_forward", "up0"), ("gather_start", "down0")),
    "f0_up": (("gather_forward", "down0"), ("gather_start", "attn")),
    "f0_down": (("gather_forward", "attn"), ("gather_start", "ffn1")),
    "attn_fwd": (("gather_forward", "ffn1"),),
    "f1_gup": (("scatter_sibling", "ffn1"),),
    "f1_dhf": (("scatter_chips", "ffn1"),),
    "kv_bwd": (("scatter_sibling", "attn"), ("scatter_done", "ffn1")),
    "f0_dact": (("scatter_chips", "attn"),),
    "f0_gup": (("scatter_sibling", "ffn0"),),
    "f0_dhf": (("scatter_chips", "ffn0"),),
    "f0_bwd": (("scatter_done", "attn"),),
    "sc_bwd": (("scatter_sibling", "mixer"),),
    "d_l0_in": (("scatter_chips", "mixer"),),
}
FINISH = (("scatter_done", "ffn0"), ("scatter_done", "mixer"))
STAGES = {"gather_start": 1, "gather_forward": 2, "gather_done": 3,
          "scatter_sibling": 1, "scatter_chips": 2, "scatter_done": 3}
SMALL_W_ROWS = 24


def _pack(arrays, rows):
    flat = jnp.concatenate([a.reshape(-1).astype(F32) for a in arrays])
    return jnp.pad(flat, (0, rows * 128 - flat.shape[0])).reshape(rows, 128)


STORED_TRANSPOSED = ("ffn_w_up", "w_uq", "w_kr")


def _stored(name, a):
    return jnp.swapaxes(a, -1, -2) if name in STORED_TRANSPOSED else a


def _base(name):
    if name.startswith("ffn_w_") and name[-1] in "01":
        return name[:-1], int(name[-1])
    return name, None


class _Exchange:
    def __init__(self, wts, mom, var, ffn_conv_b):
        self.wts, self.mom, self.var = wts, mom, var
        x, y, c = _place()
        self.c_arr = jnp.reshape(c, (1,)).astype(jnp.int32)
        chip = 2 * x + y
        self.chip_ids = jnp.stack([chip, chip ^ 1, chip ^ 2, chip ^ 3]).astype(jnp.int32)
        self.ready = {"ffn_cb0": ffn_conv_b.reshape(2, N_FF_BLK, 1, FF_BLK)[0],
                      "ffn_cb1": ffn_conv_b.reshape(2, N_FF_BLK, 1, FF_BLK)[1]}
        self.gathers, self.group_of = {}, {}
        self.grads, self.scatters, self.results, self.queue = {}, {}, {}, []
        for gname, names in GATHER_GROUPS:
            self.gathers[gname] = dict(stage=0, names=names, kinds=[KIND[_base(nm)[0]] for nm in names])
            for nm in names:
                self.group_of[nm] = gname
        for nm in ("sc_conv_w", "ffn_cw0", "ffn_cw1"):
            self.group_of[nm] = "mixer"
        self.at("begin", None)

    def _shard(self, name):
        if name == "conv":
            return _pack([self.wts["sc_conv_w"], self.wts["ffn_conv_w"]], SMALL_W_ROWS).reshape(1, SMALL_W_ROWS, 128)
        base, layer = _base(name)
        a = _stored(base, self.wts[base])
        if layer is not None:
            a = a[layer:layer + 1]
        if KIND[base] == "cols":
            return a.reshape(a.shape[-2], a.shape[-1]).astype(BF16)
        return a.reshape((-1,) + a.shape[-2:]).astype(BF16)

    def _start(self, name, srcs, lands, ncopy, plan, st):
        self.queue.append((name, (srcs, lands, ncopy, plan), st))

    def _flush(self):
        if self.queue:
            flights = _copies_start("__".join(name for name, _, _ in self.queue), [job for _, job, _ in self.queue])
            for (_, _, st), flight in zip(self.queue, flights):
                st["flight"] = flight
            self.queue = []

    def _flight(self, st):
        self._flush()
        return st["flight"]

    def _gather_to(self, gname, stage, after):
        st = self.gathers[gname]
        if st["stage"] < 1 <= stage:
            shards = [self._shard(nm) for nm in st["names"]]
            lands = [_landing(s, kind) for s, kind in zip(shards, st["kinds"])]
            plan, ncopy = _plan_gather_chips(st["kinds"])
            self._start(f"ag_{gname}_chips", shards, lands, ncopy, plan, st)
            st["stage"] = 1
        if st["stage"] < 2 <= stage:
            plan, ncopy = _plan_gather_chips(st["kinds"])
            _, lands = _copies_wait(f"ag_{gname}_chips_wait", self._flight(st), ncopy, plan)
            plan, ncopy = _plan_gather_sibling(st["kinds"])
            self._start(f"ag_{gname}_sibling", [], lands, ncopy, plan, st)
            st["stage"] = 2
        if st["stage"] < 3 <= stage:
            plan, ncopy = _plan_gather_sibling(st["kinds"])
            _, lands = _copies_wait(f"ag_{gname}_sibling_wait", self._flight(st), ncopy, plan)
            for nm, land in zip(st["names"], lands):
                self._arrived(nm, land)
            st["stage"] = 3

    def _arrived(self, name, land):
        if name == "conv":
            conv = land.reshape(N_DEV, SMALL_W_ROWS * 128)
            self.ready["sc_conv_w"] = conv[:, :3 * 128].reshape(N_DEV, 3, 128).transpose(1, 0, 2).reshape(3, D)
            fcw = conv[:, 3 * 128:3 * 128 + 6 * 352].reshape(N_DEV, 2, 3, 352).transpose(1, 2, 0, 3)
            fcw = fcw.reshape(2, 3, N_FF_BLK, FF_BLK).transpose(0, 2, 1, 3)
            self.ready["ffn_cw0"], self.ready["ffn_cw1"] = fcw[0], fcw[1]
        elif name in ("sc_w_in", "w_uk", "w_uv") or name.startswith("ffn_w_up"):
            self.ready[name] = land
        elif name.startswith("ffn_w_down"):
            self.ready[name] = land.reshape(1, N_FF_BLK, FF_BLK, D)
        elif name == "w_kr":
            self.ready[name] = jnp.pad(land, ((0, 128 - QK_ROPE), (0, 0)))
        elif name == "w_uq":
            self.ready[name] = jnp.pad(land.reshape(N_HEADS, QK_NOPE + QK_ROPE, Q_LORA),
                                       ((0, 0), (0, QK_PAD - QK_NOPE - QK_ROPE), (0, 0)))
        else:
            self.ready[name] = land.reshape(D, land.shape[-1])

    def need(self, name, after):
        if name not in self.ready:
            self._gather_to(self.group_of[name], 3, after)
            self._flush()
        return self.ready[name]

    def grad(self, name, layer, array):
        self.grads[(name, layer)] = array

    def _scatter_to(self, gname, stage, after):
        keys = dict(SCATTER_GROUPS)[gname]
        st = self.scatters.setdefault(gname, dict(stage=0))
        kinds = [KIND[nm] for nm, _ in keys]
        if st["stage"] < 1 <= stage:
            grads = [self.grads[key] for key in keys]
            lands = []
            for gr, kind in zip(grads, kinds):
                shard = (gr.shape[0],) + gr.shape[2:] if kind == "blocked" else (gr.shape[0], gr.shape[1] // N_DEV)
                lands.append(lax.empty((N_CHIP,) + shard, BF16))
            plan, ncopy = _plan_scatter_sibling(kinds)
            self._start(f"rs_{gname}_sibling", grads, lands, ncopy, plan, st)
            st["stage"] = 1
        if st["stage"] < 2 <= stage:
            plan, ncopy = _plan_scatter_sibling(kinds)
            grads, recvs = _copies_wait(f"rs_{gname}_sibling_wait", self._flight(st), ncopy, plan)
            sums = _chip_sums(f"rs_{gname}_sums", grads, kinds, recvs, self.c_arr)
            lands = [lax.empty(s.shape, BF16) for s in sums]
            plan, ncopy = _plan_scatter_chips(len(sums))
            self._start(f"rs_{gname}_chips", sums, lands, ncopy, plan, st)
            st["stage"] = 2
        if st["stage"] < 3 <= stage:
            plan, ncopy = _plan_scatter_chips(len(keys))
            sums, recvs = _copies_wait(f"rs_{gname}_chips_wait", self._flight(st), ncopy, plan)
            items = []
            for (nm, layer), own, rv in zip(keys, sums, recvs):
                nl = 1 if layer is None else 2
                rows, w = own.shape[1], own.shape[2]
                w3, m3, v3 = (_stored(nm, src[nm]).reshape(nl, rows, w) for src in (self.wts, self.mom, self.var))
                items.append((own, rv, w3, m3, v3, 0 if layer is None else layer, self.results.get(nm)))
            outs = _adamw_group(f"adamw_{gname}", items, self.chip_ids)
            for (nm, _), out in zip(keys, outs):
                self.results[nm] = out
            st["stage"] = 3

    def at(self, place, after):
        for action, gname in SCHEDULE.get(place, ()):
            self._advance(action, gname, after)
        self._flush()

    def _advance(self, action, gname, after):
        if action.startswith("gather"):
            self._gather_to(gname, STAGES[action], after)
        else:
            self._scatter_to(gname, STAGES[action], after)

    def finish(self, after):
        for action, gname in FINISH:
            self._advance(action, gname, after)
        for gname, _ in SCATTER_GROUPS:
            self._scatter_to(gname, 3, after)
        return {nm: [_stored(nm, o.reshape(_stored(nm, self.wts[nm]).shape)) for o in outs]
                for nm, outs in self.results.items()}


REPLICATED = ("attn_norm", "ffn_norm", "final_norm", "kv_in_norm", "kv_latent_norm", "q_latent_norm", "ffn_conv_b")
WEIGHTS = ("attn_norm", "ffn_norm", "final_norm", "sc_w_in", "sc_conv_w", "sc_w_out", "kv_in_norm", "w_dkv",
           "kv_latent_norm", "w_kr", "w_uk", "w_uv", "w_dq", "q_latent_norm", "w_uq", "w_o", "ffn_w_up", "ffn_conv_w",
           "ffn_conv_b", "ffn_w_down")


def kernel(x, positions, attn_norm, ffn_norm, final_norm, sc_w_in, sc_conv_w, sc_w_out, kv_in_norm, w_dkv, kv_latent_norm, w_kr, w_uk, w_uv, w_dq, q_latent_norm, w_uq, w_o, ffn_w_up, ffn_conv_w, ffn_conv_b, ffn_w_down, loss_target, m_attn_norm, m_ffn_norm, m_final_norm, m_sc_w_in, m_sc_conv_w, m_sc_w_out, m_kv_in_norm, m_w_dkv, m_kv_latent_norm, m_w_kr, m_w_uk, m_w_uv, m_w_dq, m_q_latent_norm, m_w_uq, m_w_o, m_ffn_w_up, m_ffn_conv_w, m_ffn_conv_b, m_ffn_w_down, v_attn_norm, v_ffn_norm, v_final_norm, v_sc_w_in, v_sc_conv_w, v_sc_w_out, v_kv_in_norm, v_w_dkv, v_kv_latent_norm, v_w_kr, v_w_uk, v_w_uv, v_w_dq, v_q_latent_norm, v_w_uq, v_w_o, v_ffn_w_up, v_ffn_conv_w, v_ffn_conv_b, v_ffn_w_down):
    wts = dict(attn_norm=attn_norm, ffn_norm=ffn_norm, final_norm=final_norm, sc_w_in=sc_w_in, sc_conv_w=sc_conv_w,
               sc_w_out=sc_w_out, kv_in_norm=kv_in_norm, w_dkv=w_dkv, kv_latent_norm=kv_latent_norm, w_kr=w_kr,
               w_uk=w_uk, w_uv=w_uv, w_dq=w_dq, q_latent_norm=q_latent_norm, w_uq=w_uq, w_o=w_o, ffn_w_up=ffn_w_up,
               ffn_conv_w=ffn_conv_w, ffn_conv_b=ffn_conv_b, ffn_w_down=ffn_w_down)
    mom = dict(attn_norm=m_attn_norm, ffn_norm=m_ffn_norm, final_norm=m_final_norm, sc_w_in=m_sc_w_in,
               sc_conv_w=m_sc_conv_w, sc_w_out=m_sc_w_out, kv_in_norm=m_kv_in_norm, w_dkv=m_w_dkv,
               kv_latent_norm=m_kv_latent_norm, w_kr=m_w_kr, w_uk=m_w_uk, w_uv=m_w_uv, w_dq=m_w_dq,
               q_latent_norm=m_q_latent_norm, w_uq=m_w_uq, w_o=m_w_o, ffn_w_up=m_ffn_w_up, ffn_conv_w=m_ffn_conv_w,
               ffn_conv_b=m_ffn_conv_b, ffn_w_down=m_ffn_w_down)
    var = dict(attn_norm=v_attn_norm, ffn_norm=v_ffn_norm, final_norm=v_final_norm, sc_w_in=v_sc_w_in,
               sc_conv_w=v_sc_conv_w, sc_w_out=v_sc_w_out, kv_in_norm=v_kv_in_norm, w_dkv=v_w_dkv,
               kv_latent_norm=v_kv_latent_norm, w_kr=v_w_kr, w_uk=v_w_uk, w_uv=v_w_uv, w_dq=v_w_dq,
               q_latent_norm=v_q_latent_norm, w_uq=v_w_uq, w_o=v_w_o, ffn_w_up=v_ffn_w_up, ffn_conv_w=v_ffn_conv_w,
               ffn_conv_b=v_ffn_conv_b, ffn_w_down=v_ffn_w_down)
    xi, yi, ci = _place()
    me = 4 * xi + 2 * yi + ci
    _Chain.last = None

    ex = _Exchange(wts, mom, var, ffn_conv_b)
    rep = {
        "attn_norm": attn_norm, "ffn_norm": ffn_norm, "final_norm": final_norm,
        "kv_in_norm": kv_in_norm.reshape(1, D), "kv_latent_norm": kv_latent_norm.reshape(1, KV_LORA),
        "q_latent_norm": q_latent_norm.reshape(1, Q_LORA),
    }
    loss, grad_x, small = _local_step(x.reshape(T, D), positions.reshape(T, 1), loss_target.reshape(T, D), rep, ex)

    def rows_of(a):
        return a.reshape(-1, a.shape[-1])

    small_order = list(REPLICATED) + ["sc_conv_w", "ffn_conv_w"]
    shards = [loss.reshape(1, 1, 128)] + [rows_of(small[nm])[None] for nm in small_order]
    plan, ncopy = _plan_gather_all(len(shards))
    flight, = _copies_start("ag_small", [(shards, [lax.empty((1, N_DEV) + s.shape[1:], F32) for s in shards], ncopy, plan)])
    results = ex.finish(grad_x)
    _, gathered = _copies_wait("ag_small_wait", flight, ncopy, plan)
    params = [[None] + [rows_of(src[nm]) for nm in REPLICATED] + [None, None] for src in (wts, mom, var)]
    summed = _adamw_small(gathered, *params)
    loss_total = summed[0][0][0, 0]
    for nm, vals in zip(REPLICATED, summed[1:1 + len(REPLICATED)]):
        results[nm] = [a.reshape(wts[nm].shape) for a in vals]
    g_scw = lax.dynamic_slice(summed[-2][0], (0, me * 128), (3, 128))
    g_fcw = lax.dynamic_slice(summed[-1][0], (0, me * 352), (6, 352))
    conv = _adamw_plain("adamw_conv", [g_scw, g_fcw], *[[rows_of(src["sc_conv_w"]), rows_of(src["ffn_conv_w"])]
                                                        for src in (wts, mom, var)])
    for nm, g_own, vals in zip(("sc_conv_w", "ffn_conv_w"), (g_scw, g_fcw), conv):
        results[nm] = [a.reshape(wts[nm].shape) for a in [g_own] + vals]

    outs = [loss_total, grad_x.reshape(1, T, D)]
    for slot in range(4):
        outs.extend(results[nm][slot] for nm in WEIGHTS)
    return tuple(outs)
```

```python
import jax
import jax.numpy as jnp
from jax import lax
from jax.experimental import pallas as pl
from jax.experimental.pallas import tpu as pltpu

F32 = jnp.float32
BF16 = jnp.bfloat16

T = 2048
D = 1024
N_HEADS = 8
QK_NOPE = 128
QK_ROPE = 64
V_HEAD = 128
Q_LORA = 384
KV_LORA = 256
D_FF = 2816
CHUNK = 64
ROPE_THETA = 10000.0
EPS = 1e-6
NEG_INF = -1e30
ADAM_LR = 0.001
ADAM_B1 = 0.9
ADAM_B2 = 0.999
ADAM_EPS = 1e-08
ADAM_WD = 0.01
ADAM_STEP = 10

N_DEV = 8
N_CHIP = 4
FF_BLK = D_FF * 2 // N_DEV
N_FF_BLK = D_FF // FF_BLK
QK_PAD = 256
HALO = 16

TM = 1024
TS = 512
TR = 256
TQ = 512
VMEM_LIMIT = 56 * 1024 * 1024

NN = (((1,), (0,)), ((), ()))
NT = (((1,), (1,)), ((), ()))
TN = (((0,), (0,)), ((), ()))
MESH = pl.DeviceIdType.MESH


def _params(sem):
    return pltpu.CompilerParams(dimension_semantics=sem, vmem_limit_bytes=VMEM_LIMIT)


ANY_SPEC = pl.BlockSpec(memory_space=pl.ANY)
VMEM_SPEC = pl.BlockSpec(memory_space=pltpu.VMEM)


class _Chain:
    last = None


def _pallas(body, *, name, in_specs, out_specs, out_shape, grid=(), scratch_shapes=(), n_prefetch=0, aliases=None,
            params=None):
    def run(*args):
        after = _Chain.last
        n_lead = len(args)
        specs, operands, fn = list(in_specs), list(args), body
        if after is not None:
            def fn(*refs):
                return body(*refs[:n_lead], *refs[n_lead + 1:])
            specs.append(ANY_SPEC)
            operands.append(after)
        kw = dict(name=name, out_shape=out_shape, input_output_aliases=aliases or {})
        if params is not None:
            kw["compiler_params"] = params
        if n_prefetch:
            kw["grid_spec"] = pltpu.PrefetchScalarGridSpec(
                num_scalar_prefetch=n_prefetch, grid=grid, in_specs=specs, out_specs=out_specs,
                scratch_shapes=scratch_shapes)
        else:
            kw.update(grid=grid, in_specs=specs, out_specs=out_specs, scratch_shapes=scratch_shapes)
        outs = pl.pallas_call(fn, **kw)(*operands)
        _Chain.last = outs[0] if isinstance(outs, (list, tuple)) else outs
        return outs
    return run


def _mm(name, a, b, *, grid, a_spec, b_spec, o_spec, o_shape, o_dtype, dims, k_axis=None, acc_shape=None,
        add=None, add_spec=None):
    nk = grid[k_axis] if k_axis is not None else 1
    has_add = add is not None

    def body(*refs):
        a_ref, b_ref = refs[0], refs[1]
        p = 2
        add_ref = None
        if has_add:
            add_ref = refs[p]
            p += 1
        o_ref = refs[p]
        p += 1
        r = lax.dot_general(a_ref[...].astype(BF16), b_ref[...].astype(BF16), dims, preferred_element_type=F32)
        if k_axis is None:
            if has_add:
                r = r + add_ref[...].astype(F32)
            o_ref[...] = r.astype(o_dtype)
        else:
            acc = refs[p]
            k = pl.program_id(k_axis)

            @pl.when(k == 0)
            def _():
                acc[...] = r

            @pl.when(k > 0)
            def _():
                acc[...] += r

            @pl.when(k == nk - 1)
            def _():
                t = acc[...]
                if has_add:
                    t = t + add_ref[...].astype(F32)
                o_ref[...] = t.astype(o_dtype)

    in_specs = [a_spec, b_spec]
    args = [a, b]
    if has_add:
        in_specs.append(add_spec if add_spec is not None else o_spec)
        args.append(add)
    sem = tuple("arbitrary" if ax == k_axis else "parallel" for ax in range(len(grid)))
    scratch = [pltpu.VMEM(acc_shape, F32)] if k_axis is not None else []
    return _pallas(body, name=name, grid=grid, in_specs=in_specs, out_specs=o_spec,
                   out_shape=jax.ShapeDtypeStruct(o_shape, o_dtype), scratch_shapes=scratch, params=_params(sem))(*args)


def _mm_sum(name, parts, *, grid, o_spec, o_shape, o_dtype, add=None, norm_bwd=None):
    has_add = add is not None
    np_ = len(parts)
    nn = 1 if norm_bwd is None else len(norm_bwd[1])
    has_res = norm_bwd is not None and norm_bwd[2] is not None

    def body(*refs):
        accs = [None] * nn
        for p, (_, _, _, _, dims, n) in enumerate(parts):
            a_ref, b_ref = refs[2 * p], refs[2 * p + 1]
            for k in range(a_ref.shape[0]):
                r = lax.dot_general(a_ref[k], b_ref[k], dims, preferred_element_type=F32)
                accs[n] = r if accs[n] is None else accs[n] + r
        if norm_bwd is None:
            acc = accs[0]
            if has_add:
                acc = acc + refs[2 * np_][...]
            refs[-1][...] = acc.astype(o_dtype)
            return
        x_ref, g_refs = refs[2 * np_], refs[2 * np_ + 1:2 * np_ + 1 + nn]
        dx_ref, dxb_ref, dg_refs = refs[-2 - nn], refs[-1 - nn], refs[-nn:]
        xv = x_ref[...]
        r = lax.rsqrt(jnp.mean(xv * xv, axis=-1, keepdims=True) + EPS)
        xn = xv * r
        dx = refs[2 * np_ + 1 + nn][...] if has_res else None
        sums = []
        for acc, g_ref in zip(accs, g_refs):
            gdy = acc * g_ref[...]
            t = r * (gdy - xn * jnp.mean(gdy * xn, axis=-1, keepdims=True))
            dx = t if dx is None else dx + t
            sums.append(jnp.sum(acc * xn, axis=0, keepdims=True))
        dx_ref[...] = dx
        dxb_ref[...] = dx.astype(BF16)

        @pl.when(pl.program_id(0) == 0)
        def _():
            for dg_ref, part in zip(dg_refs, sums):
                dg_ref[...] = part

        @pl.when(pl.program_id(0) > 0)
        def _():
            for dg_ref, part in zip(dg_refs, sums):
                dg_ref[...] += part

    in_specs, args = [], []
    for a, a_spec, b, b_spec, _, _ in parts:
        in_specs += [a_spec, b_spec]
        args += [a, b]
    if norm_bwd is None:
        if has_add:
            in_specs.append(o_spec)
            args.append(add)
        return _pallas(body, name=name, grid=grid, in_specs=in_specs, out_specs=o_spec,
                       out_shape=jax.ShapeDtypeStruct(o_shape, o_dtype),
                       params=_params(("parallel",) * len(grid)))(*args)
    x, gains, dres = norm_bwd
    vec = pl.BlockSpec((1, o_shape[1]), lambda i: (0, 0))
    in_specs += [o_spec] + [vec] * nn + ([o_spec] if has_res else [])
    args += [x] + list(gains) + ([dres] if has_res else [])
    outs = _pallas(body, name=name, grid=grid, in_specs=in_specs, out_specs=[o_spec, o_spec] + [vec] * nn,
                   out_shape=[jax.ShapeDtypeStruct(o_shape, F32), jax.ShapeDtypeStruct(o_shape, BF16)]
                   + [jax.ShapeDtypeStruct((1, o_shape[1]), F32)] * nn,
                   params=_params(("arbitrary",)))(*args)
    return outs[0], outs[1], list(outs[2:])


def _mm_rows(name, a, b, dims, o_dtype, n_out, *, tn=None, add=None):
    k = a.shape[1]
    tn = n_out if tn is None else tn
    if dims == NN:
        b_spec = pl.BlockSpec((k, tn), lambda n, i: (0, n))
    else:
        b_spec = pl.BlockSpec((tn, k), lambda n, i: (n, 0))
    return _mm(name, a, b, grid=(n_out // tn, T // TM),
               a_spec=pl.BlockSpec((TM, k), lambda n, i: (i, 0)), b_spec=b_spec,
               o_spec=pl.BlockSpec((TM, tn), lambda n, i: (i, n)), o_shape=(T, n_out), o_dtype=o_dtype,
               dims=dims, add=add)


def _wgrads(name, jobs):
    arrays, index = [], {}
    for a, b in jobs:
        for arr in (a, b):
            if id(arr) not in index:
                index[id(arr)] = len(arrays)
                arrays.append(arr)
    n_in = len(arrays)

    def body(*refs):
        for t, (a, b) in enumerate(jobs):
            a_ref, b_ref, o_ref = refs[index[id(a)]], refs[index[id(b)]], refs[n_in + t]
            if a.ndim == 3:
                for h in range(a.shape[0]):
                    o_ref[h] = lax.dot_general(a_ref[h], b_ref[...], TN, preferred_element_type=F32).astype(BF16)
            else:
                o_ref[...] = lax.dot_general(a_ref[...], b_ref[...], TN, preferred_element_type=F32).astype(BF16)

    out_shape = [jax.ShapeDtypeStruct(a.shape[:-2] + (a.shape[-1], b.shape[-1]), BF16) for a, b in jobs]
    return _pallas(body, name=name, in_specs=[VMEM_SPEC] * n_in, out_specs=[VMEM_SPEC] * len(jobs), out_shape=out_shape,
                   params=pltpu.CompilerParams(vmem_limit_bytes=VMEM_LIMIT))(*arrays)


def _mm_wgrad(name, a, b, *, tn=512):
    k, n = a.shape[1], b.shape[1]
    tn = min(tn, n)
    return _mm(name, a, b, grid=(n // tn,),
               a_spec=pl.BlockSpec((T, k), lambda j: (0, 0)), b_spec=pl.BlockSpec((T, tn), lambda j: (0, j)),
               o_spec=pl.BlockSpec((k, tn), lambda j: (0, j)), o_shape=(k, n), o_dtype=BF16, dims=TN)


def _rms_fwd(name, x, g):
    d = x.shape[1]

    def body(x_ref, g_ref, o_ref):
        xv = x_ref[...]
        r = lax.rsqrt(jnp.mean(xv * xv, axis=-1, keepdims=True) + EPS)
        o_ref[...] = ((xv * r) * g_ref[...]).astype(BF16)

    return _pallas(
        body, name=name, grid=(T // TM,),
        in_specs=[pl.BlockSpec((TM, d), lambda i: (i, 0)), pl.BlockSpec((1, d), lambda i: (0, 0))],
        out_specs=pl.BlockSpec((TM, d), lambda i: (i, 0)),
        out_shape=jax.ShapeDtypeStruct((T, d), BF16), params=_params(("parallel",)))(x, g)


def _rms(xv, g):
    return (xv * lax.rsqrt(jnp.mean(xv * xv, axis=-1, keepdims=True) + EPS)) * g


def _rms_bwd(name, x, gains, dys, dres=None):
    d = x.shape[1]
    n = len(gains)
    has_res = dres is not None

    def body(*refs):
        x_ref, g_refs, dy_refs = refs[0], refs[1:1 + n], refs[1 + n:1 + 2 * n]
        dx_ref, dxb_ref = refs[-2 - n], refs[-1 - n]
        dg_refs = refs[-n:]
        xv = x_ref[...]
        r = lax.rsqrt(jnp.mean(xv * xv, axis=-1, keepdims=True) + EPS)
        xn = xv * r
        dx = refs[1 + 2 * n][...] if has_res else None
        parts = []
        for g_ref, dy_ref in zip(g_refs, dy_refs):
            dyv = dy_ref[...].astype(F32)
            gdy = dyv * g_ref[...]
            t = r * (gdy - xn * jnp.mean(gdy * xn, axis=-1, keepdims=True))
            dx = t if dx is None else dx + t
            parts.append(jnp.sum(dyv * xn, axis=0, keepdims=True))
        dx_ref[...] = dx
        dxb_ref[...] = dx.astype(BF16)

        @pl.when(pl.program_id(0) == 0)
        def _():
            for dg_ref, part in zip(dg_refs, parts):
                dg_ref[...] = part

        @pl.when(pl.program_id(0) > 0)
        def _():
            for dg_ref, part in zip(dg_refs, parts):
                dg_ref[...] += part

    row = pl.BlockSpec((TR, d), lambda i: (i, 0))
    vec = pl.BlockSpec((1, d), lambda i: (0, 0))
    args = [x] + list(gains) + list(dys) + ([dres] if has_res else [])
    in_specs = [row] + [vec] * n + [row] * n + ([row] if has_res else [])
    outs = _pallas(
        body, name=name, grid=(T // TR,), in_specs=in_specs, out_specs=[row, row] + [vec] * n,
        out_shape=[jax.ShapeDtypeStruct((T, d), F32), jax.ShapeDtypeStruct((T, d), BF16)]
        + [jax.ShapeDtypeStruct((1, d), F32)] * n,
        params=_params(("arbitrary",)))(*args)
    return outs[0], outs[1], list(outs[2:])


def _final(h, g, tgt):
    def body(h_ref, g_ref, t_ref, loss_ref, dh_ref, dhb_ref, dg_ref):
        hv = h_ref[...]
        r = lax.rsqrt(jnp.mean(hv * hv, axis=-1, keepdims=True) + EPS)
        xn = hv * r
        gv = g_ref[...]
        err = xn * gv - t_ref[...]
        part_loss = 0.5 * jnp.sum(jnp.mean(err * err, axis=-1, keepdims=True), axis=0, keepdims=True)
        dy = err * (1.0 / D)
        gdy = dy * gv
        dh = r * (gdy - xn * jnp.mean(gdy * xn, axis=-1, keepdims=True))
        dh_ref[...] = dh
        dhb_ref[...] = dh.astype(BF16)
        part = jnp.sum(dy * xn, axis=0, keepdims=True)
        first = pl.program_id(0) == 0

        @pl.when(first)
        def _():
            dg_ref[...] = part
            loss_ref[...] = jnp.broadcast_to(part_loss, (1, 128))

        @pl.when(jnp.logical_not(first))
        def _():
            dg_ref[...] += part
            loss_ref[...] += jnp.broadcast_to(part_loss, (1, 128))

    row = pl.BlockSpec((TR, D), lambda i: (i, 0))
    vec = pl.BlockSpec((1, D), lambda i: (0, 0))
    return _pallas(
        body, name="final_loss", grid=(T // TR,), in_specs=[row, vec, row],
        out_specs=[pl.BlockSpec((1, 128), lambda i: (0, 0)), row, row, vec],
        out_shape=[jax.ShapeDtypeStruct((1, 128), F32), jax.ShapeDtypeStruct((T, D), F32),
                   jax.ShapeDtypeStruct((T, D), BF16), jax.ShapeDtypeStruct((1, D), F32)],
        params=_params(("arbitrary",)))(h, g, tgt)


def _prev_idx(i, rows=TR):
    return jnp.maximum(i * (rows // HALO) - 1, 0)


def _next_idx(i, rows=TR):
    return jnp.minimum((i + 1) * (rows // HALO), T // HALO - 1)


def _causal_taps(ext):
    return pltpu.roll(ext, 2, 0)[HALO:], pltpu.roll(ext, 1, 0)[HALO:], ext[HALO:]


def _anticausal_taps(ext, n):
    rows = ext.shape[0]
    return pltpu.roll(ext, rows - 1, 0)[:n], pltpu.roll(ext, rows - 2, 0)[:n]


MIX_COLS = 512


def _mixer_in(hn, w_in, w):
    nc = D // MIX_COLS

    def body(h_ref, hh_ref, wb_ref, wc_ref, wu_ref, w_ref, b_ref, c_ref, u_ref, y_ref):
        i = pl.program_id(1)
        hv = h_ref[...]
        he = jnp.concatenate([hh_ref[...], hv], axis=0)
        ce = lax.dot_general(he, wc_ref[...], NN, preferred_element_type=F32).astype(BF16)
        ue = lax.dot_general(he, wu_ref[...], NN, preferred_element_type=F32).astype(BF16)
        bv = lax.dot_general(hv, wb_ref[...], NN, preferred_element_type=F32).astype(BF16)
        b_ref[...] = bv
        c_ref[...] = ce[HALO:]
        u_ref[...] = ue[HALO:]
        row = lax.broadcasted_iota(jnp.int32, (HALO + TS, 1), 0)
        cu = jnp.where(jnp.logical_or(i > 0, row >= HALO), ce.astype(F32) * ue.astype(F32), 0.0)
        x2, x1, x0 = _causal_taps(cu)
        wv = w_ref[...]
        cv = (x2 * wv[0:1] + x1 * wv[1:2]) + x0 * wv[2:3]
        y_ref[...] = (bv.astype(F32) * cv).astype(BF16)

    def cols(part):
        return pl.BlockSpec((D, MIX_COLS), lambda j, i: (0, part * nc + j))

    blk = pl.BlockSpec((TS, MIX_COLS), lambda j, i: (i, j))
    out = jax.ShapeDtypeStruct((T, D), BF16)
    return _pallas(
        body, name="l0_in", grid=(nc, T // TS),
        in_specs=[pl.BlockSpec((TS, D), lambda j, i: (i, 0)), pl.BlockSpec((HALO, D), lambda j, i: (_prev_idx(i, TS), 0)),
                  cols(0), cols(1), cols(2), pl.BlockSpec((3, MIX_COLS), lambda j, i: (0, j))],
        out_specs=[blk] * 4, out_shape=[out] * 4,
        params=_params(("parallel", "parallel")))(hn, hn, w_in, w_in, w_in, w)


def _mixer_out_bwd(dh, w_out, zb, zc, zu, w):
    last = T // TR - 1

    def body(dh_ref, dhn_ref, wo_ref, b_ref, bn_ref, c_ref, ch_ref, u_ref, uh_ref, w_ref, dz_ref, dw_ref):
        i = pl.program_id(0)
        dye = lax.dot_general(jnp.concatenate([dh_ref[...], dhn_ref[...]], axis=0), wo_ref[...], NT,
                              preferred_element_type=F32)
        cv_ = c_ref[...].astype(F32)
        uv = u_ref[...].astype(F32)
        cu = cv_ * uv
        cuh = jnp.where(i > 0, ch_ref[...].astype(F32) * uh_ref[...].astype(F32), 0.0)
        x2, x1, x0 = _causal_taps(jnp.concatenate([cuh, cu], axis=0))
        wv = w_ref[...]
        conv = (x2 * wv[0:1] + x1 * wv[1:2]) + x0 * wv[2:3]
        dyv = dye[:TR]
        dz_ref[:, 0:D] = (dyv * conv).astype(BF16)
        dconv = dyv * b_ref[...].astype(F32)
        dconv_n = jnp.where(i < last, dye[TR:] * bn_ref[...].astype(F32), 0.0)
        n1, n2 = _anticausal_taps(jnp.concatenate([dconv, dconv_n], axis=0), TR)
        dcu = (dconv * wv[2:3] + n1 * wv[1:2]) + n2 * wv[0:1]
        dz_ref[:, D:2 * D] = (dcu * uv).astype(BF16)
        dz_ref[:, 2 * D:3 * D] = (dcu * cv_).astype(BF16)
        part = jnp.concatenate([jnp.sum(dconv * x2, axis=0, keepdims=True),
                                jnp.sum(dconv * x1, axis=0, keepdims=True),
                                jnp.sum(dconv * x0, axis=0, keepdims=True)], axis=0)

        @pl.when(i == 0)
        def _():
            dw_ref[...] = part

        @pl.when(i > 0)
        def _():
            dw_ref[...] += part

    main = pl.BlockSpec((TR, D), lambda i: (i, 0))
    prev = pl.BlockSpec((HALO, D), lambda i: (_prev_idx(i), 0))
    nxt = pl.BlockSpec((HALO, D), lambda i: (_next_idx(i), 0))
    wspec = pl.BlockSpec((3, D), lambda i: (0, 0))
    return _pallas(
        body, name="d_l0_out", grid=(T // TR,),
        in_specs=[main, nxt, pl.BlockSpec((D, D), lambda i: (0, 0)), main, nxt, main, prev, main, prev, wspec],
        out_specs=[pl.BlockSpec((TR, 3 * D), lambda i: (i, 0)), wspec],
        out_shape=[jax.ShapeDtypeStruct((T, 3 * D), BF16), jax.ShapeDtypeStruct((3, D), F32)],
        params=_params(("arbitrary",)))(dh, dh, w_out, zb, zb, zc, zc, zu, zu, w)


def _sigmoid(x):
    return 0.5 * jnp.tanh(0.5 * x) + 0.5


def _ffn_up_act(name, hf, w_up, w, b):
    def body(h_ref, hh_ref, wg_ref, wv_ref, w_ref, b_ref, g_ref, v_ref, a_ref):
        i = pl.program_id(1)
        hv = h_ref[...]
        ge = lax.dot_general(jnp.concatenate([hh_ref[...], hv], axis=0), wg_ref[...], NT,
                             preferred_element_type=F32).astype(BF16)
        v = lax.dot_general(hv, wv_ref[...], NT, preferred_element_type=F32).astype(BF16)
        g_ref[...] = ge[HALO:]
        v_ref[...] = v
        ext = ge.astype(F32)
        row = lax.broadcasted_iota(jnp.int32, (HALO + TM, 1), 0)
        ext = jnp.where(jnp.logical_or(i > 0, row >= HALO), ext, 0.0)
        x2, x1, x0 = _causal_taps(ext)
        wv = w_ref[...]
        gc = ((x2 * wv[0:1] + x1 * wv[1:2]) + x0 * wv[2:3]) + b_ref[...]
        a_ref[...] = ((gc * _sigmoid(gc)) * v.astype(F32)).astype(BF16)

    blk = pl.BlockSpec((None, TM, FF_BLK), lambda j, i: (j, i, 0))
    out = jax.ShapeDtypeStruct((N_FF_BLK, T, FF_BLK), BF16)
    return _pallas(
        body, name=name, grid=(N_FF_BLK, T // TM),
        in_specs=[pl.BlockSpec((TM, D), lambda j, i: (i, 0)),
                  pl.BlockSpec((HALO, D), lambda j, i: (_prev_idx(i, TM), 0)),
                  pl.BlockSpec((None, None, FF_BLK, D), lambda j, i: (0, j, 0, 0)),
                  pl.BlockSpec((None, None, FF_BLK, D), lambda j, i: (0, j + N_FF_BLK, 0, 0)),
                  pl.BlockSpec((None, 3, FF_BLK), lambda j, i: (j, 0, 0)),
                  pl.BlockSpec((None, 1, FF_BLK), lambda j, i: (j, 0, 0))],
        out_specs=[blk, blk, blk], out_shape=[out, out, out],
        params=_params(("parallel", "parallel")))(hf, hf, w_up, w_up, w, b)


def _ffn_dact(name, dh, w_down4, g, v, w, b):
    last = T // TS - 1

    def body(dh_ref, dhn_ref, wd_ref, g_ref, gp_ref, gn_ref, v_ref, vn_ref, w_ref, b_ref, dg_ref, dv_ref, dw_ref, db_ref):
        i = pl.program_id(1)
        da = lax.dot_general(jnp.concatenate([dh_ref[...], dhn_ref[...]], axis=0), wd_ref[...], NT,
                             preferred_element_type=F32)
        row = lax.broadcasted_iota(jnp.int32, (TS + HALO, 1), 0)
        da = jnp.where(jnp.logical_or(i < last, row < TS), da, 0.0)
        gp = jnp.where(i > 0, gp_ref[...].astype(F32), 0.0)
        ext = jnp.concatenate([gp, g_ref[...].astype(F32), gn_ref[...].astype(F32)], axis=0)
        x2, x1, x0 = _causal_taps(ext)
        wv = w_ref[...]
        gc = ((x2 * wv[0:1] + x1 * wv[1:2]) + x0 * wv[2:3]) + b_ref[...]
        sg = _sigmoid(gc)
        vv = jnp.concatenate([v_ref[...].astype(F32), vn_ref[...].astype(F32)], axis=0)
        dv_ref[...] = (da[:TS] * (gc[:TS] * sg[:TS])).astype(BF16)
        dgc = (da * vv) * (sg * (1.0 + gc * (1.0 - sg)))
        n1, n2 = _anticausal_taps(dgc, TS)
        d0 = dgc[:TS]
        dg_ref[...] = ((d0 * wv[2:3] + n1 * wv[1:2]) + n2 * wv[0:1]).astype(BF16)
        part_w = jnp.concatenate([jnp.sum(d0 * x2[:TS], axis=0, keepdims=True),
                                  jnp.sum(d0 * x1[:TS], axis=0, keepdims=True),
                                  jnp.sum(d0 * x0[:TS], axis=0, keepdims=True)], axis=0)
        part_b = jnp.sum(d0, axis=0, keepdims=True)

        @pl.when(i == 0)
        def _():
            dw_ref[...] = part_w
            db_ref[...] = part_b

        @pl.when(i > 0)
        def _():
            dw_ref[...] += part_w
            db_ref[...] += part_b

    blk = pl.BlockSpec((None, TS, FF_BLK), lambda j, i: (j, i, 0))
    prev = pl.BlockSpec((None, HALO, FF_BLK), lambda j, i: (j, _prev_idx(i, TS), 0))
    nxt = pl.BlockSpec((None, HALO, FF_BLK), lambda j, i: (j, _next_idx(i, TS), 0))
    wspec = pl.BlockSpec((None, 3, FF_BLK), lambda j, i: (j, 0, 0))
    bspec = pl.BlockSpec((None, 1, FF_BLK), lambda j, i: (j, 0, 0))
    return _pallas(
        body, name=name, grid=(N_FF_BLK, T // TS),
        in_specs=[pl.BlockSpec((TS, D), lambda j, i: (i, 0)),
                  pl.BlockSpec((HALO, D), lambda j, i: (_next_idx(i, TS), 0)),
                  pl.BlockSpec((None, None, FF_BLK, D), lambda j, i: (0, j, 0, 0)),
                  blk, prev, nxt, blk, nxt, wspec, bspec],
        out_specs=[blk, blk, wspec, bspec],
        out_shape=[jax.ShapeDtypeStruct((N_FF_BLK, T, FF_BLK), BF16), jax.ShapeDtypeStruct((N_FF_BLK, T, FF_BLK), BF16),
                   jax.ShapeDtypeStruct((N_FF_BLK, 3, FF_BLK), F32), jax.ShapeDtypeStruct((N_FF_BLK, 1, FF_BLK), F32)],
        params=_params(("parallel", "arbitrary")))(dh, dh, w_down4, g, g, g, v, v, w, b)


def _rope_tables(pos, inv_freq):
    half = QK_ROPE // 2

    def body(p_ref, f_ref, c_ref, sa_ref, sb_ref):
        ang = p_ref[...].astype(F32) * f_ref[...]
        lane = lax.broadcasted_iota(jnp.int32, (T, 128), 1)
        c = jnp.cos(ang)
        s = jnp.sin(ang)
        c_ref[...] = jnp.where(lane < 2 * half, c, 0.0)
        sa_ref[...] = jnp.where(lane < half, -s, 0.0)
        sb_ref[...] = jnp.where(jnp.logical_and(lane >= half, lane < 2 * half), s, 0.0)

    return _pallas(
        body, name="rope_tables", in_specs=[VMEM_SPEC] * 2, out_specs=[VMEM_SPEC] * 3,
        out_shape=[jax.ShapeDtypeStruct((T, 128), F32)] * 3,
        params=pltpu.CompilerParams(vmem_limit_bytes=VMEM_LIMIT))(pos, inv_freq)


def _rotate(r, c, sa, sb, sign):
    return r * c + sign * (pltpu.roll(r, 96, 1) * sa + pltpu.roll(r, 32, 1) * sb)


def _attn_pre(h2, g_kv, g_l1, g_kvl, g_ql, w_dkv, w_kr, w_uk, w_uv, w_dq, w_uq, tables):
    def body(h_ref, c_ref, sa_ref, sb_ref, gkv_ref, gl1_ref, gkvl_ref, gql_ref, wdkv_ref, wkr_ref, wuk_ref, wuv_ref,
             wdq_ref, wuq_ref, hk_ref, hn_ref, ckvr_ref, ckv_ref, kr_ref, kn_ref, v_ref, cqr_ref, cq_ref, q_ref):
        xv = h_ref[...]
        xn = xv * lax.rsqrt(jnp.mean(xv * xv, axis=-1, keepdims=True) + EPS)
        hk = (xn * gkv_ref[...]).astype(BF16)
        hn = (xn * gl1_ref[...]).astype(BF16)
        hk_ref[...] = hk
        hn_ref[...] = hn
        cv, sav, sbv = c_ref[...], sa_ref[...], sb_ref[...]
        raw = lax.dot_general(hk, wdkv_ref[...], NN, preferred_element_type=F32)
        ckvr_ref[...] = raw
        ckv = _rms(raw, gkvl_ref[...]).astype(BF16)
        ckv_ref[...] = ckv
        kr = lax.dot_general(hk, wkr_ref[...], NT, preferred_element_type=F32)
        kr_ref[...] = _rotate(kr, cv, sav, sbv, 1.0).astype(BF16)
        kn_ref[...] = lax.dot_general(ckv, wuk_ref[...], NN, preferred_element_type=F32).astype(BF16)
        v_ref[...] = lax.dot_general(ckv, wuv_ref[...], NN, preferred_element_type=F32).astype(BF16)
        cqr = lax.dot_general(hn, wdq_ref[...], NN, preferred_element_type=F32)
        cqr_ref[...] = cqr
        cq = _rms(cqr, gql_ref[...]).astype(BF16)
        cq_ref[...] = cq
        for h in range(N_HEADS):
            r = lax.dot_general(cq, wuq_ref[h], NT, preferred_element_type=F32)
            q_ref[h, :, :QK_NOPE] = r[:, :QK_NOPE].astype(BF16)
            q_ref[h, :, QK_NOPE:] = _rotate(r[:, QK_NOPE:], cv, sav, sbv, 1.0).astype(BF16)

    def rows(d):
        return pl.BlockSpec((TS, d), lambda i: (i, 0))

    def whole(a):
        return pl.BlockSpec(a.shape, lambda i: (0,) * a.ndim)

    wholes = [g_kv, g_l1, g_kvl, g_ql, w_dkv, w_kr, w_uk, w_uv, w_dq, w_uq]
    outs = [(D, BF16), (D, BF16), (KV_LORA, F32), (KV_LORA, BF16), (128, BF16), (N_HEADS * QK_NOPE, BF16),
            (N_HEADS * V_HEAD, BF16), (Q_LORA, F32), (Q_LORA, BF16)]
    return _pallas(
        body, name="attn_pre", grid=(T // TS,),
        in_specs=[rows(D), rows(128), rows(128), rows(128)] + [whole(a) for a in wholes],
        out_specs=[rows(d) for d, _ in outs] + [pl.BlockSpec((N_HEADS, TS, QK_PAD), lambda i: (0, i, 0))],
        out_shape=[jax.ShapeDtypeStruct((T, d), dt) for d, dt in outs]
        + [jax.ShapeDtypeStruct((N_HEADS, T, QK_PAD), BF16)],
        params=_params(("parallel",)))(h2, *tables, *wholes)


SCALE = (QK_NOPE + QK_ROPE) ** -0.5
LOG2E = 1.4426950408889634
SCALE2 = SCALE * LOG2E


def _diag_mask(transposed):
    shift = CHUNK.bit_length() - 1
    a = lax.broadcasted_iota(jnp.int32, (TQ, TQ), 0) >> shift
    b = lax.broadcasted_iota(jnp.int32, (TQ, TQ), 1) >> shift
    return (a <= b) if transposed else (b <= a)


def _as_row(col):
    return jnp.transpose(jnp.broadcast_to(col, (col.shape[0], 128)), (1, 0))[0:1]


def _keys(kn_ref, kr_ref, off):
    return jnp.concatenate([kn_ref[pl.ds(off, TQ), :], kr_ref[pl.ds(off, TQ), :]], axis=1)


def _attn_fwd(q, kn, kr, v):
    hp = 2

    def body(q_ref, kn_ref, kr_ref, v_ref, o_ref, lse_ref):
        i = pl.program_id(1)
        qs = [q_ref[a] for a in range(hp)]

        def step(j, carry, masked):
            off = pl.multiple_of(j * TQ, TQ)
            krv = kr_ref[pl.ds(off, TQ), :]
            ss = []
            for a in range(hp):
                kk = jnp.concatenate([kn_ref[pl.ds(off, TQ), a * QK_NOPE:(a + 1) * QK_NOPE], krv], axis=1)
                ss.append(lax.dot_general(qs[a], kk, NT, preferred_element_type=F32))
            out = []
            for a in range(hp):
                m, l, acc = carry[a]
                s = ss[a] * SCALE2
                if masked:
                    s = jnp.where(_diag_mask(False), s, NEG_INF)
                m_new = jnp.maximum(m, jnp.max(s, axis=-1, keepdims=True))
                p = jnp.exp2(s - m_new)
                alpha = jnp.exp2(m - m_new)
                l = alpha * l + jnp.sum(p, axis=-1, keepdims=True)
                pv = lax.dot_general(p.astype(BF16), v_ref[pl.ds(off, TQ), a * V_HEAD:(a + 1) * V_HEAD], NN,
                                     preferred_element_type=F32)
                out.append((m_new, l, alpha * acc + pv))
            return tuple(out)

        one = (jnp.full((TQ, 1), NEG_INF, F32), jnp.zeros((TQ, 1), F32), jnp.zeros((TQ, V_HEAD), F32))
        carry = lax.fori_loop(0, i, lambda j, cr: step(j, cr, False), (one,) * hp)
        carry = step(i, carry, True)
        for a, (m, l, acc) in enumerate(carry):
            o_ref[:, a * V_HEAD:(a + 1) * V_HEAD] = (acc / l).astype(BF16)
            lse_ref[a] = _as_row(m + jnp.log(l) * LOG2E)

    return _pallas(
        body, name="attn_fwd", grid=(N_HEADS // hp, T // TQ),
        in_specs=[pl.BlockSpec((hp, TQ, QK_PAD), lambda h, i: (h, i, 0)),
                  pl.BlockSpec((T, hp * QK_NOPE), lambda h, i: (0, h)),
                  pl.BlockSpec((T, 128), lambda h, i: (0, 0)),
                  pl.BlockSpec((T, hp * V_HEAD), lambda h, i: (0, h))],
        out_specs=[pl.BlockSpec((TQ, hp * V_HEAD), lambda h, i: (i, h)), pl.BlockSpec((hp, 1, TQ), lambda h, i: (h, 0, i))],
        out_shape=[jax.ShapeDtypeStruct((T, N_HEADS * V_HEAD), BF16), jax.ShapeDtypeStruct((N_HEADS, 1, T), F32)],
        params=_params(("parallel", "parallel")))(q, kn, kr, v)


def _attn_bwd(q, kn, kr, v, o, do, lse_row, tables):
    nq = T // TQ
    hp = 2
    cos, sa, sb = tables

    def body(q_ref, kn_ref, kr_ref, v_ref, o_ref, do_ref, lse_ref, c_ref, sa_ref, sb_ref,
             dq_ref, dkn_ref, dkr_ref, dv_ref, dq_acc, dl_ref):
        j = pl.program_id(1)

        def cols(a):
            return slice(a * 128, (a + 1) * 128)

        @pl.when(j == 0)
        def _():
            dq_acc[...] = jnp.zeros_like(dq_acc)
            for a in range(hp):
                for i in range(nq):
                    rows = pl.ds(i * TQ, TQ)
                    prod = do_ref[rows, cols(a)].astype(F32) * o_ref[rows, cols(a)].astype(F32)
                    dl_ref[a, :, rows] = _as_row(jnp.sum(prod, axis=-1, keepdims=True))

        krv = kr_ref[...]
        kks = [jnp.concatenate([kn_ref[:, cols(a)], krv], axis=1) for a in range(hp)]
        vvs = [v_ref[:, cols(a)] for a in range(hp)]

        def step(i, carry, masked):
            off = pl.multiple_of(i * TQ, TQ)
            rows = pl.ds(off, TQ)
            qis = [q_ref[a, rows, :] for a in range(hp)]
            dois = [do_ref[rows, cols(a)] for a in range(hp)]
            sts = [lax.dot_general(kks[a], qis[a], NT, preferred_element_type=F32) for a in range(hp)]
            dpts = [lax.dot_general(vvs[a], dois[a], NT, preferred_element_type=F32) for a in range(hp)]
            out = []
            for a in range(hp):
                dk, dv = carry[a]
                st = sts[a] * SCALE2
                if masked:
                    st = jnp.where(_diag_mask(True), st, NEG_INF)
                pt = jnp.exp2(st - lse_ref[a, :, rows])
                dv = dv + lax.dot_general(pt.astype(BF16), dois[a], NN, preferred_element_type=F32)
                dst = ((pt * (dpts[a] - dl_ref[a, :, rows])) * SCALE).astype(BF16)
                dk = dk + lax.dot_general(dst, qis[a], NN, preferred_element_type=F32)
                dq_acc[a, rows, :] += lax.dot_general(dst, kks[a], TN, preferred_element_type=F32)
                out.append((dk, dv))
            return tuple(out)

        zero = (jnp.zeros((TQ, QK_PAD), F32), jnp.zeros((TQ, V_HEAD), F32))
        carry = step(j, (zero,) * hp, True)
        carry = lax.fori_loop(j + 1, nq, lambda i, cr: step(i, cr, False), carry)
        for a, (dk, dv) in enumerate(carry):
            dkn_ref[:, cols(a)] = dk[:, :QK_NOPE].astype(BF16)
            dkr_ref[a] = dk[:, QK_NOPE:]
            dv_ref[:, cols(a)] = dv.astype(BF16)

        @pl.when(j == nq - 1)
        def _():
            for a in range(hp):
                dq = dq_acc[a]
                dq_ref[a, :, :QK_NOPE] = dq[:, :QK_NOPE].astype(BF16)
                dq_ref[a, :, QK_NOPE:] = _rotate(dq[:, QK_NOPE:], c_ref[...], sa_ref[...], sb_ref[...], -1.0).astype(BF16)

    row = pl.BlockSpec((hp, 1, T), lambda h, j: (h, 0, 0))
    head = pl.BlockSpec((TQ, hp * 128), lambda h, j: (j, h))
    whole = pl.BlockSpec((hp, T, QK_PAD), lambda h, j: (h, 0, 0))
    tab = pl.BlockSpec((T, 128), lambda h, j: (0, 0))
    heads = pl.BlockSpec((T, hp * V_HEAD), lambda h, j: (0, h))
    return _pallas(
        body, name="attn_bwd", grid=(N_HEADS // hp, nq),
        in_specs=[whole, head, pl.BlockSpec((TQ, 128), lambda h, j: (j, 0)), head, heads, heads, row, tab, tab, tab],
        out_specs=[whole, head, pl.BlockSpec((hp, TQ, 128), lambda h, j: (h, j, 0)), head],
        out_shape=[jax.ShapeDtypeStruct((N_HEADS, T, QK_PAD), BF16), jax.ShapeDtypeStruct((T, N_HEADS * QK_NOPE), BF16),
                   jax.ShapeDtypeStruct((N_HEADS, T, 128), F32), jax.ShapeDtypeStruct((T, N_HEADS * V_HEAD), BF16)],
        scratch_shapes=[pltpu.VMEM((hp, T, QK_PAD), F32), pltpu.VMEM((hp, 1, T), F32)],
        params=_params(("parallel", "arbitrary")))(q, kn, kr, v, o, do, lse_row, cos, sa, sb)


def _rms_bwd_math(xv, g, dy):
    r = lax.rsqrt(jnp.mean(xv * xv, axis=-1, keepdims=True) + EPS)
    xn = xv * r
    gdy = dy * g
    return r * (gdy - xn * jnp.mean(gdy * xn, axis=-1, keepdims=True)), jnp.sum(dy * xn, axis=0, keepdims=True)


def _attn_post(dq, dkn, dv, dkr, cq_raw, ckv_raw, h2, dres, g_ql, g_kvl, g_l1, g_kv, w_uq, w_uk, w_uv, w_dq, w_dkv,
               w_kr, tables):
    def body(dq_ref, dkn_ref, dv_ref, dkr_ref, cqr_ref, ckvr_ref, h_ref, res_ref, c_ref, sa_ref, sb_ref,
             gql_ref, gkvl_ref, gl1_ref, gkv_ref, wuq_ref, wuk_ref, wuv_ref, wdq_ref, wdkv_ref, wkr_ref,
             dcq_ref, dckv_ref, dkrr_ref, dh_ref, dhb_ref, dgql_ref, dgkvl_ref, dgl1_ref, dgkv_ref):
        dcq = lax.dot_general(dq_ref[0], wuq_ref[0], NN, preferred_element_type=F32)
        for h in range(1, N_HEADS):
            dcq = dcq + lax.dot_general(dq_ref[h], wuq_ref[h], NN, preferred_element_type=F32)
        dcq_raw, s_ql = _rms_bwd_math(cqr_ref[...], gql_ref[...], dcq)
        dcq_raw = dcq_raw.astype(BF16)
        dcq_ref[...] = dcq_raw
        dckv = (lax.dot_general(dkn_ref[...], wuk_ref[...], NT, preferred_element_type=F32)
                + lax.dot_general(dv_ref[...], wuv_ref[...], NT, preferred_element_type=F32))
        dckv_raw, s_kvl = _rms_bwd_math(ckvr_ref[...], gkvl_ref[...], dckv)
        dckv_raw = dckv_raw.astype(BF16)
        dckv_ref[...] = dckv_raw
        dkr = dkr_ref[0]
        for h in range(1, N_HEADS):
            dkr = dkr + dkr_ref[h]
        dkr_raw = _rotate(dkr, c_ref[...], sa_ref[...], sb_ref[...], -1.0).astype(BF16)
        dkrr_ref[...] = dkr_raw
        d_hn = lax.dot_general(dcq_raw, wdq_ref[...], NT, preferred_element_type=F32)
        d_hk = (lax.dot_general(dckv_raw, wdkv_ref[...], NT, preferred_element_type=F32)
                + lax.dot_general(dkr_raw, wkr_ref[...], NN, preferred_element_type=F32))
        xv = h_ref[...]
        r = lax.rsqrt(jnp.mean(xv * xv, axis=-1, keepdims=True) + EPS)
        xn = xv * r
        dx = res_ref[...]
        sums = [s_ql, s_kvl]
        for dy, g_ref in ((d_hn, gl1_ref), (d_hk, gkv_ref)):
            gdy = dy * g_ref[...]
            dx = dx + r * (gdy - xn * jnp.mean(gdy * xn, axis=-1, keepdims=True))
            sums.append(jnp.sum(dy * xn, axis=0, keepdims=True))
        dh_ref[...] = dx
        dhb_ref[...] = dx.astype(BF16)
        dg_refs = (dgql_ref, dgkvl_ref, dgl1_ref, dgkv_ref)

        @pl.when(pl.program_id(0) == 0)
        def _():
            for dg_ref, part in zip(dg_refs, sums):
                dg_ref[...] = part

        @pl.when(pl.program_id(0) > 0)
        def _():
            for dg_ref, part in zip(dg_refs, sums):
                dg_ref[...] += part

    def rows(d):
        return pl.BlockSpec((TS, d), lambda i: (i, 0))

    def heads(d):
        return pl.BlockSpec((N_HEADS, TS, d), lambda i: (0, i, 0))

    def whole(a):
        return pl.BlockSpec(a.shape, lambda i: (0,) * a.ndim)

    wholes = [g_ql, g_kvl, g_l1, g_kv, w_uq, w_uk, w_uv, w_dq, w_dkv, w_kr]
    vecs = [Q_LORA, KV_LORA, D, D]
    return _pallas(
        body, name="attn_post", grid=(T // TS,),
        in_specs=[heads(QK_PAD), rows(N_HEADS * QK_NOPE), rows(N_HEADS * V_HEAD), heads(128), rows(Q_LORA),
                  rows(KV_LORA), rows(D), rows(D), rows(128), rows(128), rows(128)] + [whole(a) for a in wholes],
        out_specs=[rows(Q_LORA), rows(KV_LORA), rows(128), rows(D), rows(D)]
        + [pl.BlockSpec((1, d), lambda i: (0, 0)) for d in vecs],
        out_shape=[jax.ShapeDtypeStruct((T, Q_LORA), BF16), jax.ShapeDtypeStruct((T, KV_LORA), BF16),
                   jax.ShapeDtypeStruct((T, 128), BF16), jax.ShapeDtypeStruct((T, D), F32),
                   jax.ShapeDtypeStruct((T, D), BF16)] + [jax.ShapeDtypeStruct((1, d), F32) for d in vecs],
        params=_params(("arbitrary",)))(dq, dkn, dv, dkr, cq_raw, ckv_raw, h2, dres, *tables, *wholes)


def _ffn_gup(name, dg, dv, hf):
    def body(dg_ref, dv_ref, hf_ref, o_ref):
        j = pl.program_id(0)

        @pl.when(j < N_FF_BLK)
        def _():
            o_ref[...] = lax.dot_general(dg_ref[...], hf_ref[...], TN, preferred_element_type=F32).astype(BF16)

        @pl.when(j >= N_FF_BLK)
        def _():
            o_ref[...] = lax.dot_general(dv_ref[...], hf_ref[...], TN, preferred_element_type=F32).astype(BF16)

    return _pallas(
        body, name=name, grid=(N_DEV,),
        in_specs=[pl.BlockSpec((None, T, FF_BLK), lambda j: (jnp.minimum(j, N_FF_BLK - 1), 0, 0)),
                  pl.BlockSpec((None, T, FF_BLK), lambda j: (jnp.maximum(j - N_FF_BLK, 0), 0, 0)),
                  pl.BlockSpec((T, D), lambda j: (0, 0))],
        out_specs=pl.BlockSpec((None, FF_BLK, D), lambda j: (j, 0, 0)),
        out_shape=jax.ShapeDtypeStruct((N_DEV, FF_BLK, D), BF16), params=_params(("parallel",)))(dg, dv, hf)


def _ffn_layer_fwd(tag, h, gain, ex):
    hf = _rms_fwd(f"{tag}_norm", h, gain)
    g, v, act = _ffn_up_act(f"{tag}_up", hf, ex.need(f"ffn_w_up{tag[1]}", hf), ex.need(f"ffn_cw{tag[1]}", hf),
                            ex.need(f"ffn_cb{tag[1]}", hf))
    ex.at(f"{tag}_up", act)
    rows = pl.BlockSpec((TS, D), lambda i: (i, 0))
    out = _mm_sum(f"{tag}_down",
                  [(act, pl.BlockSpec((N_FF_BLK, TS, FF_BLK), lambda i: (0, i, 0)), ex.need(f"ffn_w_down{tag[1]}", act),
                    pl.BlockSpec((None, N_FF_BLK, FF_BLK, D), lambda i: (0, 0, 0, 0)), NN, 0)],
                  grid=(T // TS,), o_spec=rows, o_shape=(T, D), o_dtype=F32, add=h)
    ex.at(f"{tag}_down", out)
    return out, (hf, g, v, act)


def _ffn_layer_bwd(tag, h, gain, ex, saved, dh, dh_bf):
    hf, g, v, act = saved
    layer = tag[1]
    w_up, w_down4 = ex.need(f"ffn_w_up{layer}", dh_bf), ex.need(f"ffn_w_down{layer}", dh_bf)
    dg, dv, dcw, dcb = _ffn_dact(f"{tag}_dact", dh_bf, w_down4, g, v, ex.need(f"ffn_cw{layer}", dh_bf),
                                 ex.need(f"ffn_cb{layer}", dh_bf))
    ex.at(f"{tag}_dact", dg)
    g_down = _mm(f"{tag}_gdown", act, dh_bf, grid=(N_FF_BLK,),
                 a_spec=pl.BlockSpec((None, T, FF_BLK), lambda j: (j, 0, 0)),
                 b_spec=pl.BlockSpec((T, D), lambda j: (0, 0)),
                 o_spec=pl.BlockSpec((FF_BLK, D), lambda j: (j, 0)),
                 o_shape=(D_FF, D), o_dtype=BF16, dims=TN)
    g_up = _ffn_gup(f"{tag}_gup", dg, dv, hf)
    ex.grad("ffn_w_up", int(layer), g_up.reshape(1, N_DEV, FF_BLK, D))
    ex.grad("ffn_w_down", int(layer), g_down.reshape(1, N_DEV, D_FF // N_DEV, D))
    ex.at(f"{tag}_gup", g_up)
    part = pl.BlockSpec((N_FF_BLK, TR, FF_BLK), lambda i: (0, i, 0))
    dh_in, dh_in_bf, dgain = _mm_sum(
        f"{tag}_dhf",
        [(dg, part, w_up, pl.BlockSpec((None, N_FF_BLK, FF_BLK, D), lambda i: (0, 0, 0, 0)), NN, 0),
         (dv, part, w_up, pl.BlockSpec((None, N_FF_BLK, FF_BLK, D), lambda i: (0, 1, 0, 0)), NN, 0)],
        grid=(T // TR,), o_spec=pl.BlockSpec((TR, D), lambda i: (i, 0)), o_shape=(T, D), o_dtype=F32,
        norm_bwd=(h, [gain], dh))
    ex.at(f"{tag}_dhf", dh_in)
    return dh_in, dh_in_bf, dgain[0], dcw, dcb


def _local_step(x, pos, tgt, rep, ex):
    attn_norm, ffn_norm, final_norm = rep["attn_norm"], rep["ffn_norm"], rep["final_norm"]
    half = QK_ROPE // 2
    inv = 1.0 / (ROPE_THETA ** (jnp.arange(half, dtype=F32) / half))
    inv_freq = jnp.concatenate([inv, inv, jnp.zeros((128 - 2 * half,), F32)]).reshape(1, 128)
    tables = _rope_tables(pos, inv_freq)

    hn0 = _rms_fwd("l0_norm", x, attn_norm[0:1])
    w_in = ex.need("sc_w_in", hn0)
    ex.at("mixer_ready", hn0)
    zb, zc, zu, y = _mixer_in(hn0, w_in, ex.need("sc_conv_w", hn0))
    ex.at("l0_in", y)
    h1 = _mm_rows("l0_out", y, ex.need("sc_w_out", y), NN, F32, D, tn=512, add=x)
    ex.at("l0_out", h1)
    h2, ffn0 = _ffn_layer_fwd("f0", h1, ffn_norm[0:1], ex)

    w_uq = ex.need("w_uq", h2)
    hk, hn1, ckv_raw, ckv, kr, kn, vv, cq_raw, cq, q = _attn_pre(
        h2, rep["kv_in_norm"], attn_norm[1:2], rep["kv_latent_norm"], rep["q_latent_norm"], ex.need("w_dkv", h2),
        ex.need("w_kr", h2), ex.need("w_uk", h2), ex.need("w_uv", h2), ex.need("w_dq", h2), w_uq, tables)

    o, lse = _attn_fwd(q, kn, kr, vv)
    ex.at("attn_fwd", o)
    w_o = ex.need("w_o", o)
    h3 = _mm_rows("attn_out", o, w_o, NN, F32, D, tn=512, add=h2)
    h4, ffn1 = _ffn_layer_fwd("f1", h3, ffn_norm[1:2], ex)

    loss, dh4, dh4_bf, d_final = _final(h4, final_norm.reshape(1, D), tgt)

    dh3, dh3_bf, d_fn1, dcw1, dcb1 = _ffn_layer_bwd("f1", h3, ffn_norm[1:2], ex, ffn1, dh4, dh4_bf)
    ex.at("f1_bwd", dh3)

    do = _mm_rows("d_attn_out", dh3_bf, w_o, NT, BF16, N_HEADS * V_HEAD)
    dq_pre, dkn, dkr, dvv = _attn_bwd(q, kn, kr, vv, o, do, lse, tables)

    dcq_raw_bf, dckv_raw_bf, dkr_raw_bf, dh2, dh2_bf, d_qln, d_kvln, d_an1, d_kvin = _attn_post(
        dq_pre, dkn, dvv, dkr, cq_raw, ckv_raw, h2, dh3, rep["q_latent_norm"], rep["kv_latent_norm"], attn_norm[1:2],
        rep["kv_in_norm"], w_uq, ex.need("w_uk", dkn), ex.need("w_uv", dvv), ex.need("w_dq", dq_pre),
        ex.need("w_dkv", dkn), ex.need("w_kr", dkr), tables)
    g_uq, g_dq, g_o = _wgrads("g_q", [(dq_pre, cq), (hn1, dcq_raw_bf), (o, dh3_bf)])
    ex.grad("w_uq", None, g_uq[:, :QK_NOPE + QK_ROPE].reshape(1, N_DEV, QK_NOPE + QK_ROPE, Q_LORA))
    ex.grad("w_dq", None, g_dq.reshape(1, N_DEV, D // N_DEV, Q_LORA))
    ex.grad("w_o", None, g_o.reshape(1, N_DEV, D // N_DEV, D))

    g_uk, g_uv, g_dkv, g_kr = _wgrads("g_kv", [(ckv, dkn), (ckv, dvv), (hk, dckv_raw_bf), (dkr_raw_bf, hk)])
    ex.grad("w_uk", None, g_uk)
    ex.grad("w_uv", None, g_uv)
    ex.grad("w_dkv", None, g_dkv.reshape(1, N_DEV, D // N_DEV, KV_LORA))
    ex.grad("w_kr", None, g_kr[:QK_ROPE])
    ex.at("kv_bwd", dh2)

    dh1, dh1_bf, d_fn0, dcw0, dcb0 = _ffn_layer_bwd("f0", h1, ffn_norm[0:1], ex, ffn0, dh2, dh2_bf)
    ex.at("f0_bwd", dh1)

    ex.grad("sc_w_out", None, _mm_wgrad("g_sc_w_out", y, dh1_bf).reshape(1, N_DEV, D // N_DEV, D))
    dz, d_scw = _mixer_out_bwd(dh1_bf, ex.need("sc_w_out", dh1_bf), zb, zc, zu, ex.need("sc_conv_w", dh1_bf))
    g_in = _mm_wgrad("g_sc_w_in", hn0, dz)
    ex.grad("sc_w_in", None, g_in)
    ex.at("sc_bwd", g_in)
    ex.at("d_l0_in", g_in)
    w_in = ex.need("sc_w_in", dz)
    grad_x, _, (d_an0,) = _mm_sum(
        "d_l0_in", [(dz[None], pl.BlockSpec((1, TS, dz.shape[1]), lambda i: (0, i, 0)),
                     w_in[None], pl.BlockSpec((1,) + w_in.shape, lambda i: (0, 0, 0)), NT, 0)],
        norm_bwd=(x, [attn_norm[0:1]], dh1),
        grid=(T // TS,), o_spec=pl.BlockSpec((TS, D), lambda i: (i, 0)), o_shape=(T, D), o_dtype=F32)

    small = {
        "attn_norm": jnp.concatenate([d_an0, d_an1], axis=0),
        "ffn_norm": jnp.concatenate([d_fn0, d_fn1], axis=0),
        "final_norm": d_final.reshape(D),
        "kv_in_norm": d_kvin.reshape(D),
        "kv_latent_norm": d_kvln.reshape(KV_LORA),
        "q_latent_norm": d_qln,
        "ffn_conv_b": jnp.stack([dcb0, dcb1]).transpose(0, 2, 1, 3).reshape(2, D_FF),
        "sc_conv_w": d_scw,
        "ffn_conv_w": jnp.stack([dcw0, dcw1]).transpose(0, 2, 1, 3).reshape(2, 3, D_FF),
    }
    return loss, grad_x, small


def _place():
    return lax.axis_index("x"), lax.axis_index("y"), lax.axis_index("c")


def _peers():
    x, y, c = _place()
    return (x, y, 1 - c), [(1 - x, y), (x, 1 - y), (1 - x, 1 - y)]


def _window(ref, kind, dev):
    if kind == "blocked":
        return ref.at[:, dev]
    width = ref.shape[-1] // N_DEV
    return ref.at[:, pl.ds(pl.multiple_of(dev * width, 128), width)]


HBM_SPEC = pl.BlockSpec(memory_space=pltpu.HBM)
SEM_SPEC = pl.BlockSpec(memory_space=pltpu.SEMAPHORE)
EFFECT = pltpu.SideEffectType.DATAFLOW_SIDE_EFFECTING
TOKEN = jax.ShapeDtypeStruct((8, 128), F32)


def _hbm(a):
    return pltpu.with_memory_space_constraint(a, pltpu.HBM)


def _copies_start(name, jobs):
    nj = len(jobs)
    counts = [(len(srcs), len(lands)) for srcs, lands, _, _ in jobs]
    n_arr = sum(ns + nl for ns, nl in counts)

    def body(*refs):
        sems, token = refs[n_arr:n_arr + 2 * nj], refs[-1]
        at = 0
        for j, ((ns, nl), (_, _, ncopy, plan)) in enumerate(zip(counts, jobs)):
            copies = plan(refs[at:at + ns], refs[at + ns:at + ns + nl])
            assert len(copies) == ncopy
            for k, (sent, dst, to, _) in enumerate(copies):
                pltpu.make_async_remote_copy(src_ref=sent, dst_ref=dst, send_sem=sems[2 * j].at[k],
                                             recv_sem=sems[2 * j + 1].at[k], device_id=to, device_id_type=MESH).start()
            at += ns + nl
        token[...] = jnp.zeros_like(token)

    arrays = [a for srcs, lands, _, _ in jobs for a in list(srcs) + list(lands)]
    sem_shapes = [pltpu.SemaphoreType.DMA((ncopy,)) for _, _, ncopy, _ in jobs for _ in range(2)]
    outs = pl.pallas_call(
        body, name=name, in_specs=[HBM_SPEC] * n_arr,
        out_specs=[SEM_SPEC] * (2 * nj) + [HBM_SPEC] * n_arr + [VMEM_SPEC],
        out_shape=sem_shapes + [pltpu.HBM(a.shape, a.dtype) for a in arrays] + [TOKEN],
        input_output_aliases={i: 2 * nj + i for i in range(n_arr)},
        compiler_params=pltpu.CompilerParams(has_side_effects=EFFECT))(*[_hbm(a) for a in arrays])
    _Chain.last = outs[-1]
    flights, at = [], 2 * nj
    for j, (ns, nl) in enumerate(counts):
        flights.append((outs[2 * j], outs[2 * j + 1], list(outs[at:at + ns]), list(outs[at + ns:at + ns + nl])))
        at += ns + nl
    return flights


def _copies_wait(name, started, ncopy, plan):
    send, recv, srcs, lands = started
    ns, nl = len(srcs), len(lands)

    def body(*refs):
        send_ref, recv_ref, token = refs[ns + nl], refs[ns + nl + 1], refs[-1]
        copies = plan(refs[:ns], refs[ns:ns + nl])
        assert len(copies) == ncopy
        for k, (sent, _, to, landed) in enumerate(copies):
            cp = pltpu.make_async_remote_copy(src_ref=sent, dst_ref=landed, send_sem=send_ref.at[k],
                                              recv_sem=recv_ref.at[k], device_id=to, device_id_type=MESH)
            cp.wait_send()
            cp.wait_recv()
        token[...] = jnp.zeros_like(token)

    arrays = list(srcs) + list(lands)
    outs = pl.pallas_call(
        body, name=name, in_specs=[HBM_SPEC] * (ns + nl) + [SEM_SPEC] * 2 + [ANY_SPEC],
        out_specs=[HBM_SPEC] * (ns + nl) + [VMEM_SPEC], out_shape=[pltpu.HBM(a.shape, a.dtype) for a in arrays] + [TOKEN],
        input_output_aliases={i: i for i in range(ns + nl)},
        compiler_params=pltpu.CompilerParams(has_side_effects=EFFECT))(*arrays, send, recv, _Chain.last)
    _Chain.last = outs[-1]
    return list(outs[:ns]), list(outs[ns:-1])


def _plan_gather_chips(kinds):
    def plan(srcs, lands):
        x, y, c = _place()
        sibling, chips = _peers()
        out = []
        for t, kind in enumerate(kinds):
            mine = _window(lands[t], kind, 4 * x + 2 * y + c)
            out.append((srcs[t], mine, (x, y, c), mine))
            out.append((srcs[t], mine, sibling, _window(lands[t], kind, 4 * x + 2 * y + 1 - c)))
            for px, py in chips:
                out.append((srcs[t], mine, (px, py, c), _window(lands[t], kind, 4 * px + 2 * py + c)))
        return out
    return plan, 5 * len(kinds)


def _plan_gather_all(n):
    def plan(srcs, lands):
        x, y, c = _place()
        out = []
        for t in range(n):
            mine = lands[t].at[:, 4 * x + 2 * y + c]
            for m in range(N_DEV):
                px, py, pc = (1 - x if m & 4 else x), (1 - y if m & 2 else y), (1 - c if m & 1 else c)
                out.append((srcs[t], mine, (px, py, pc), lands[t].at[:, 4 * px + 2 * py + pc]))
        return out
    return plan, N_DEV * n


def _plan_gather_sibling(kinds):
    def plan(srcs, lands):
        _, _, c = _place()
        sibling, chips = _peers()
        out = []
        for t, kind in enumerate(kinds):
            for px, py in chips:
                w = _window(lands[t], kind, 4 * px + 2 * py + c)
                out.append((w, w, sibling, _window(lands[t], kind, 4 * px + 2 * py + 1 - c)))
        return out
    return plan, 3 * len(kinds)


def _plan_scatter_sibling(kinds):
    def plan(srcs, lands):
        _, _, c = _place()
        sibling, _ = _peers()
        out = []
        for t, kind in enumerate(kinds):
            for k in range(N_CHIP):
                out.append((_window(srcs[t], kind, 2 * k + 1 - c), lands[t].at[k], sibling, lands[t].at[k]))
        return out
    return plan, N_CHIP * len(kinds)


def _plan_scatter_chips(n):
    def plan(srcs, lands):
        x, y, c = _place()
        _, chips = _peers()
        out = []
        for t in range(n):
            for px, py in chips:
                out.append((srcs[t].at[2 * px + py], lands[t].at[2 * x + y], (px, py, c), lands[t].at[2 * px + py]))
        return out
    return plan, 3 * n


def _landing(shard, kind):
    if kind == "blocked":
        return lax.empty((shard.shape[0], N_DEV) + shard.shape[1:], shard.dtype)
    return lax.empty((shard.shape[0], N_DEV * shard.shape[1]), shard.dtype)


def _chip_sums(name, grads, kinds, recvs, c):
    n = len(grads)
    in_specs, out_specs, out_shape, args = [], [], [], []
    for gr, kind, rv in zip(grads, kinds, recvs):
        if kind == "blocked":
            rows, w = gr.shape[2], gr.shape[3]
            in_specs.append(pl.BlockSpec((None, None, rows, w), lambda k, cref: (0, 2 * k + cref[0], 0, 0)))
        else:
            rows, w = gr.shape[0], gr.shape[1] // N_DEV
            in_specs.append(pl.BlockSpec((rows, w), lambda k, cref: (0, 2 * k + cref[0])))
        blk = pl.BlockSpec((None, rows, w), lambda k, cref: (k, 0, 0))
        in_specs.append(blk)
        out_specs.append(blk)
        out_shape.append(jax.ShapeDtypeStruct((N_CHIP, rows, w), BF16))
        args += [gr, rv.reshape(N_CHIP, rows, w)]

    def body(*refs):
        for t in range(n):
            g_ref, r_ref, o_ref = refs[1 + 2 * t], refs[2 + 2 * t], refs[1 + 2 * n + t]
            o_ref[...] = (g_ref[...].astype(F32) + r_ref[...].astype(F32)).astype(BF16)

    return _pallas(body, name=name, n_prefetch=1, grid=(N_CHIP,), in_specs=in_specs, out_specs=out_specs,
                   out_shape=out_shape, params=_params(("parallel",)))(c, *args)


def _adamw_math(g, wv, mv, vv):
    m = ADAM_B1 * mv + (1.0 - ADAM_B1) * g
    v = ADAM_B2 * vv + (1.0 - ADAM_B2) * (g * g)
    m_hat = m / (1.0 - ADAM_B1 ** ADAM_STEP)
    v_hat = v / (1.0 - ADAM_B2 ** ADAM_STEP)
    delta = -ADAM_LR * (m_hat / (jnp.sqrt(v_hat) + ADAM_EPS) + ADAM_WD * wv)
    return delta, m, v


ADAM_STEPS = 2


def _adamw_group(name, items, chip_ids):
    n = len(items)
    in_specs, out_specs, out_shape, args, prevs = [], [], [], [chip_ids], []
    for own, recv, w3, m3, v3, layer, _ in items:
        nl, rows, w = w3.shape
        tr = rows // ADAM_STEPS
        assert tr % 16 == 0, (name, rows)
        in_specs += [pl.BlockSpec((None, tr, w), lambda i, ids, slot=slot: (ids[slot], i, 0)) for slot in range(4)]
        slab = pl.BlockSpec((None, tr, w), lambda i, ids, layer=layer: (layer, i, 0))
        in_specs += [slab] * 3
        out_specs += [slab] * 4
        out_shape += [jax.ShapeDtypeStruct((nl, rows, w), F32)] * 4
        args += [own, recv, recv, recv, w3, m3, v3]
    aliases = {}
    for t, item in enumerate(items):
        if item[6] is not None:
            for k in range(4):
                aliases[len(args) + k] = 4 * t + k
            in_specs += [ANY_SPEC] * 4
            args += list(item[6])
            prevs.append(t)
    n_in = 1 + 7 * n + 4 * len(prevs)

    def body(*refs):
        for t in range(n):
            own_ref, r1_ref, r2_ref, r3_ref, w_ref, m_ref, v_ref = refs[1 + 7 * t:8 + 7 * t]
            g_ref, d_ref, nm_ref, nv_ref = refs[n_in + 4 * t:n_in + 4 * t + 4]
            g = ((own_ref[...].astype(F32) + r1_ref[...].astype(F32)) + r2_ref[...].astype(F32)) + r3_ref[...].astype(F32)
            g_ref[...] = g
            d_ref[...], nm_ref[...], nv_ref[...] = _adamw_math(g, w_ref[...], m_ref[...], v_ref[...])

    outs = _pallas(body, name=name, n_prefetch=1, grid=(ADAM_STEPS,), in_specs=in_specs, out_specs=out_specs,
                   out_shape=out_shape, aliases=aliases, params=_params(("parallel",)))(*args)
    return [list(outs[4 * t:4 * t + 4]) for t in range(n)]


SHARD_ROWS = 8


def _adamw_small(gathered, ws, ms, vs, me):
    n = len(gathered)
    full = [w is not None for w in ws]
    sharded = [w is not None and w.ndim == 3 for w in ws]
    args = list(gathered)
    out_shape = []
    for t in range(n):
        shape = jax.ShapeDtypeStruct(ws[t].shape if sharded[t] else gathered[t].shape[2:], F32)
        if full[t]:
            args += [ws[t], ms[t], vs[t]]
            out_shape += [shape] * 4
        else:
            out_shape += [shape]

    def body(*refs):
        i_in, i_out = n, len(args) + 1
        me_ref = refs[len(args)]
        for t in range(n):
            p_ref = refs[t]
            if sharded[t]:
                w_ref, m_ref, v_ref = refs[i_in:i_in + 3]
                taps, layers, _ = w_ref.shape
                mine = pl.ds(pl.multiple_of(me_ref[0] * SHARD_ROWS, SHARD_ROWS), SHARD_ROWS)
                g = p_ref[0, 0, mine, :]
                for k in range(1, N_DEV):
                    g = g + p_ref[0, k, mine, :]
                for l in range(layers):
                    for k in range(taps):
                        at = (k, slice(l, l + 1), slice(None))
                        row = g[l * taps + k:l * taps + k + 1]
                        refs[i_out][at] = row
                        refs[i_out + 1][at], refs[i_out + 2][at], refs[i_out + 3][at] = _adamw_math(
                            row, w_ref[at], m_ref[at], v_ref[at])
                i_in += 3
                i_out += 4
                continue
            g = p_ref[0, 0]
            for k in range(1, N_DEV):
                g = g + p_ref[0, k]
            refs[i_out][...] = g
            if full[t]:
                w_ref, m_ref, v_ref = refs[i_in:i_in + 3]
                refs[i_out + 1][...], refs[i_out + 2][...], refs[i_out + 3][...] = _adamw_math(
                    g, w_ref[...], m_ref[...], v_ref[...])
                i_in += 3
                i_out += 4
            else:
                i_out += 1

    outs = _pallas(body, name="adamw_small",
                   in_specs=[VMEM_SPEC] * len(args) + [pl.BlockSpec(memory_space=pltpu.SMEM)],
                   out_specs=[VMEM_SPEC] * len(out_shape), out_shape=out_shape,
                   params=pltpu.CompilerParams(vmem_limit_bytes=VMEM_LIMIT))(*args, me)
    result, i = [], 0
    for t in range(n):
        k = 4 if full[t] else 1
        result.append(list(outs[i:i + k]))
        i += k
    return result


KIND = {"sc_w_in": "cols", "sc_w_out": "blocked", "w_dkv": "blocked", "w_kr": "cols", "w_uk": "cols", "w_uv": "cols",
        "w_dq": "blocked", "w_uq": "blocked", "w_o": "blocked", "ffn_w_up": "blocked", "ffn_w_down": "blocked",
        "conv": "blocked"}
GATHER_GROUPS = (("mixer", ("sc_w_in", "sc_w_out", "conv")),
                 ("up0", ("ffn_w_up0",)),
                 ("down0", ("ffn_w_down0",)),
                 ("attn", ("w_dkv", "w_kr", "w_uk", "w_uv", "w_dq", "w_uq", "w_o")),
                 ("ffn1", ("ffn_w_up1", "ffn_w_down1")))
SCATTER_GROUPS = (("ffn1", (("ffn_w_up", 1), ("ffn_w_down", 1))),
                  ("attn", (("w_o", None), ("w_uq", None), ("w_dq", None), ("w_uk", None), ("w_uv", None),
                            ("w_dkv", None), ("w_kr", None))),
                  ("ffn0", (("ffn_w_up", 0), ("ffn_w_down", 0))),
                  ("mixer", (("sc_w_out", None), ("sc_w_in", None))))
SCHEDULE = {
    "begin": (("gather_start", "mixer"),),
    "mixer_ready": (("gather_start", "up0"),),
    "l0_out": (("gather_forward", "up0"), ("gather_start", "down0")),
    "f0_up": (("gather_forward", "down0"), ("gather_start", "attn")),
    "f0_down": (("gather_forward", "attn"), ("gather_start", "ffn1")),
    "attn_fwd": (("gather_forward", "ffn1"),),
    "f1_gup": (("scatter_sibling", "ffn1"),),
    "f1_dhf": (("scatter_chips", "ffn1"),),
    "kv_bwd": (("scatter_sibling", "attn"), ("scatter_done", "ffn1")),
    "f0_dact": (("scatter_chips", "attn"),),
    "f0_gup": (("scatter_sibling", "ffn0"),),
    "f0_dhf": (("scatter_chips", "ffn0"),),
    "f0_bwd": (("scatter_done", "attn"),),
    "sc_bwd": (("scatter_sibling", "mixer"),),
    "d_l0_in": (("scatter_chips", "mixer"),),
}
FINISH = (("scatter_done", "ffn0"), ("scatter_done", "mixer"))
STAGES = {"gather_start": 1, "gather_forward": 2, "gather_done": 3,
          "scatter_sibling": 1, "scatter_chips": 2, "scatter_done": 3}
SMALL_W_ROWS = 24


def _pack(arrays, rows):
    flat = jnp.concatenate([a.reshape(-1).astype(F32) for a in arrays])
    return jnp.pad(flat, (0, rows * 128 - flat.shape[0])).reshape(rows, 128)


STORED_TRANSPOSED = ("ffn_w_up", "w_uq", "w_kr")


def _stored(name, a):
    return jnp.swapaxes(a, -1, -2) if name in STORED_TRANSPOSED else a


def _base(name):
    if name.startswith("ffn_w_") and name[-1] in "01":
        return name[:-1], int(name[-1])
    return name, None


class _Exchange:
    def __init__(self, wts, mom, var, ffn_conv_b):
        self.wts, self.mom, self.var = wts, mom, var
        x, y, c = _place()
        self.c_arr = jnp.reshape(c, (1,)).astype(jnp.int32)
        chip = 2 * x + y
        self.chip_ids = jnp.stack([chip, chip ^ 1, chip ^ 2, chip ^ 3]).astype(jnp.int32)
        self.ready = {"ffn_cb0": ffn_conv_b.reshape(2, N_FF_BLK, 1, FF_BLK)[0],
                      "ffn_cb1": ffn_conv_b.reshape(2, N_FF_BLK, 1, FF_BLK)[1]}
        self.gathers, self.group_of = {}, {}
        self.grads, self.scatters, self.results, self.queue = {}, {}, {}, []
        for gname, names in GATHER_GROUPS:
            self.gathers[gname] = dict(stage=0, names=names, kinds=[KIND[_base(nm)[0]] for nm in names])
            for nm in names:
                self.group_of[nm] = gname
        for nm in ("sc_conv_w", "ffn_cw0", "ffn_cw1"):
            self.group_of[nm] = "mixer"
        self.at("begin", None)

    def _shard(self, name):
        if name == "conv":
            return _pack([self.wts["sc_conv_w"], self.wts["ffn_conv_w"]], SMALL_W_ROWS).reshape(1, SMALL_W_ROWS, 128)
        base, layer = _base(name)
        a = _stored(base, self.wts[base])
        if layer is not None:
            a = a[layer:layer + 1]
        if KIND[base] == "cols":
            return a.reshape(a.shape[-2], a.shape[-1]).astype(BF16)
        return a.reshape((-1,) + a.shape[-2:]).astype(BF16)

    def _start(self, name, srcs, lands, ncopy, plan, st):
        self.queue.append((name, (srcs, lands, ncopy, plan), st))

    def _flush(self):
        if self.queue:
            flights = _copies_start("__".join(name for name, _, _ in self.queue), [job for _, job, _ in self.queue])
            for (_, _, st), flight in zip(self.queue, flights):
                st["flight"] = flight
            self.queue = []

    def _flight(self, st):
        self._flush()
        return st["flight"]

    def _gather_to(self, gname, stage, after):
        st = self.gathers[gname]
        if st["stage"] < 1 <= stage:
            shards = [self._shard(nm) for nm in st["names"]]
            lands = [_landing(s, kind) for s, kind in zip(shards, st["kinds"])]
            plan, ncopy = _plan_gather_chips(st["kinds"])
            self._start(f"ag_{gname}_chips", shards, lands, ncopy, plan, st)
            st["stage"] = 1
        if st["stage"] < 2 <= stage:
            plan, ncopy = _plan_gather_chips(st["kinds"])
            _, lands = _copies_wait(f"ag_{gname}_chips_wait", self._flight(st), ncopy, plan)
            plan, ncopy = _plan_gather_sibling(st["kinds"])
            self._start(f"ag_{gname}_sibling", [], lands, ncopy, plan, st)
            st["stage"] = 2
        if st["stage"] < 3 <= stage:
            plan, ncopy = _plan_gather_sibling(st["kinds"])
            _, lands = _copies_wait(f"ag_{gname}_sibling_wait", self._flight(st), ncopy, plan)
            for nm, land in zip(st["names"], lands):
                self._arrived(nm, land)
            st["stage"] = 3

    def _arrived(self, name, land):
        if name == "conv":
            conv = land.reshape(N_DEV, SMALL_W_ROWS * 128)
            self.ready["sc_conv_w"] = conv[:, :3 * 128].reshape(N_DEV, 3, 128).transpose(1, 0, 2).reshape(3, D)
            fcw = conv[:, 3 * 128:3 * 128 + 6 * 352].reshape(N_DEV, 2, 3, 352).transpose(1, 2, 0, 3)
            fcw = fcw.reshape(2, 3, N_FF_BLK, FF_BLK).transpose(0, 2, 1, 3)
            self.ready["ffn_cw0"], self.ready["ffn_cw1"] = fcw[0], fcw[1]
        elif name in ("sc_w_in", "w_uk", "w_uv") or name.startswith("ffn_w_up"):
            self.ready[name] = land
        elif name.startswith("ffn_w_down"):
            self.ready[name] = land.reshape(1, N_FF_BLK, FF_BLK, D)
        elif name == "w_kr":
            self.ready[name] = jnp.pad(land, ((0, 128 - QK_ROPE), (0, 0)))
        elif name == "w_uq":
            self.ready[name] = jnp.pad(land.reshape(N_HEADS, QK_NOPE + QK_ROPE, Q_LORA),
                                       ((0, 0), (0, QK_PAD - QK_NOPE - QK_ROPE), (0, 0)))
        else:
            self.ready[name] = land.reshape(D, land.shape[-1])

    def need(self, name, after):
        if name not in self.ready:
            self._gather_to(self.group_of[name], 3, after)
            self._flush()
        return self.ready[name]

    def grad(self, name, layer, array):
        self.grads[(name, layer)] = array

    def _scatter_to(self, gname, stage, after):
        keys = dict(SCATTER_GROUPS)[gname]
        st = self.scatters.setdefault(gname, dict(stage=0))
        kinds = [KIND[nm] for nm, _ in keys]
        if st["stage"] < 1 <= stage:
            grads = [self.grads[key] for key in keys]
            lands = []
            for gr, kind in zip(grads, kinds):
                shard = (gr.shape[0],) + gr.shape[2:] if kind == "blocked" else (gr.shape[0], gr.shape[1] // N_DEV)
                lands.append(lax.empty((N_CHIP,) + shard, BF16))
            plan, ncopy = _plan_scatter_sibling(kinds)
            self._start(f"rs_{gname}_sibling", grads, lands, ncopy, plan, st)
            st["stage"] = 1
        if st["stage"] < 2 <= stage:
            plan, ncopy = _plan_scatter_sibling(kinds)
            grads, recvs = _copies_wait(f"rs_{gname}_sibling_wait", self._flight(st), ncopy, plan)
            sums = _chip_sums(f"rs_{gname}_sums", grads, kinds, recvs, self.c_arr)
            lands = [lax.empty(s.shape, BF16) for s in sums]
            plan, ncopy = _plan_scatter_chips(len(sums))
            self._start(f"rs_{gname}_chips", sums, lands, ncopy, plan, st)
            st["stage"] = 2
        if st["stage"] < 3 <= stage:
            plan, ncopy = _plan_scatter_chips(len(keys))
            sums, recvs = _copies_wait(f"rs_{gname}_chips_wait", self._flight(st), ncopy, plan)
            items = []
            for (nm, layer), own, rv in zip(keys, sums, recvs):
                nl = 1 if layer is None else 2
                rows, w = own.shape[1], own.shape[2]
                w3, m3, v3 = (_stored(nm, src[nm]).reshape(nl, rows, w) for src in (self.wts, self.mom, self.var))
                items.append((own, rv, w3, m3, v3, 0 if layer is None else layer, self.results.get(nm)))
            outs = _adamw_group(f"adamw_{gname}", items, self.chip_ids)
            for (nm, _), out in zip(keys, outs):
                self.results[nm] = out
            st["stage"] = 3

    def at(self, place, after):
        for action, gname in SCHEDULE.get(place, ()):
            self._advance(action, gname, after)
        self._flush()

    def _advance(self, action, gname, after):
        if action.startswith("gather"):
            self._gather_to(gname, STAGES[action], after)
        else:
            self._scatter_to(gname, STAGES[action], after)

    def finish(self, after):
        for action, gname in FINISH:
            self._advance(action, gname, after)
        for gname, _ in SCATTER_GROUPS:
            self._scatter_to(gname, 3, after)
        return {nm: [_stored(nm, o.reshape(_stored(nm, self.wts[nm]).shape)) for o in outs]
                for nm, outs in self.results.items()}


REPLICATED = ("attn_norm", "ffn_norm", "final_norm", "kv_in_norm", "kv_latent_norm", "q_latent_norm", "ffn_conv_b")
WEIGHTS = ("attn_norm", "ffn_norm", "final_norm", "sc_w_in", "sc_conv_w", "sc_w_out", "kv_in_norm", "w_dkv",
           "kv_latent_norm", "w_kr", "w_uk", "w_uv", "w_dq", "q_latent_norm", "w_uq", "w_o", "ffn_w_up", "ffn_conv_w",
           "ffn_conv_b", "ffn_w_down")


def kernel(x, positions, attn_norm, ffn_norm, final_norm, sc_w_in, sc_conv_w, sc_w_out, kv_in_norm, w_dkv, kv_latent_norm, w_kr, w_uk, w_uv, w_dq, q_latent_norm, w_uq, w_o, ffn_w_up, ffn_conv_w, ffn_conv_b, ffn_w_down, loss_target, m_attn_norm, m_ffn_norm, m_final_norm, m_sc_w_in, m_sc_conv_w, m_sc_w_out, m_kv_in_norm, m_w_dkv, m_kv_latent_norm, m_w_kr, m_w_uk, m_w_uv, m_w_dq, m_q_latent_norm, m_w_uq, m_w_o, m_ffn_w_up, m_ffn_conv_w, m_ffn_conv_b, m_ffn_w_down, v_attn_norm, v_ffn_norm, v_final_norm, v_sc_w_in, v_sc_conv_w, v_sc_w_out, v_kv_in_norm, v_w_dkv, v_kv_latent_norm, v_w_kr, v_w_uk, v_w_uv, v_w_dq, v_q_latent_norm, v_w_uq, v_w_o, v_ffn_w_up, v_ffn_conv_w, v_ffn_conv_b, v_ffn_w_down):
    wts = dict(attn_norm=attn_norm, ffn_norm=ffn_norm, final_norm=final_norm, sc_w_in=sc_w_in, sc_conv_w=sc_conv_w,
               sc_w_out=sc_w_out, kv_in_norm=kv_in_norm, w_dkv=w_dkv, kv_latent_norm=kv_latent_norm, w_kr=w_kr,
               w_uk=w_uk, w_uv=w_uv, w_dq=w_dq, q_latent_norm=q_latent_norm, w_uq=w_uq, w_o=w_o, ffn_w_up=ffn_w_up,
               ffn_conv_w=ffn_conv_w, ffn_conv_b=ffn_conv_b, ffn_w_down=ffn_w_down)
    mom = dict(attn_norm=m_attn_norm, ffn_norm=m_ffn_norm, final_norm=m_final_norm, sc_w_in=m_sc_w_in,
               sc_conv_w=m_sc_conv_w, sc_w_out=m_sc_w_out, kv_in_norm=m_kv_in_norm, w_dkv=m_w_dkv,
               kv_latent_norm=m_kv_latent_norm, w_kr=m_w_kr, w_uk=m_w_uk, w_uv=m_w_uv, w_dq=m_w_dq,
               q_latent_norm=m_q_latent_norm, w_uq=m_w_uq, w_o=m_w_o, ffn_w_up=m_ffn_w_up, ffn_conv_w=m_ffn_conv_w,
               ffn_conv_b=m_ffn_conv_b, ffn_w_down=m_ffn_w_down)
    var = dict(attn_norm=v_attn_norm, ffn_norm=v_ffn_norm, final_norm=v_final_norm, sc_w_in=v_sc_w_in,
               sc_conv_w=v_sc_conv_w, sc_w_out=v_sc_w_out, kv_in_norm=v_kv_in_norm, w_dkv=v_w_dkv,
               kv_latent_norm=v_kv_latent_norm, w_kr=v_w_kr, w_uk=v_w_uk, w_uv=v_w_uv, w_dq=v_w_dq,
               q_latent_norm=v_q_latent_norm, w_uq=v_w_uq, w_o=v_w_o, ffn_w_up=v_ffn_w_up, ffn_conv_w=v_ffn_conv_w,
               ffn_conv_b=v_ffn_conv_b, ffn_w_down=v_ffn_w_down)
    xi, yi, ci = _place()
    me = 4 * xi + 2 * yi + ci
    _Chain.last = None

    ex = _Exchange(wts, mom, var, ffn_conv_b)
    rep = {
        "attn_norm": attn_norm, "ffn_norm": ffn_norm, "final_norm": final_norm,
        "kv_in_norm": kv_in_norm.reshape(1, D), "kv_latent_norm": kv_latent_norm.reshape(1, KV_LORA),
        "q_latent_norm": q_latent_norm.reshape(1, Q_LORA),
    }
    loss, grad_x, small = _local_step(x.reshape(T, D), positions.reshape(T, 1), loss_target.reshape(T, D), rep, ex)

    def rows_of(a):
        return a.reshape(-1, a.shape[-1])

    def device_rows(a):
        taps, c = a.shape[-2], a.shape[-1] // N_DEV
        rows = a.reshape(-1, taps, N_DEV, c).transpose(2, 0, 1, 3).reshape(N_DEV, -1, c)
        return jnp.pad(rows, ((0, 0), (0, SHARD_ROWS - rows.shape[1]), (0, 0))).reshape(N_DEV * SHARD_ROWS, c)

    def taps_first(a):
        return jnp.transpose(a, (1, 0, 2))

    sharded = ("sc_conv_w", "ffn_conv_w")
    shards = ([loss.reshape(1, 1, 128)] + [rows_of(small[nm])[None] for nm in REPLICATED]
              + [device_rows(small[nm])[None] for nm in sharded])
    plan, ncopy = _plan_gather_all(len(shards))
    flight, = _copies_start("ag_small", [(shards, [lax.empty((1, N_DEV) + s.shape[1:], F32) for s in shards], ncopy, plan)])
    results = ex.finish(grad_x)
    _, gathered = _copies_wait("ag_small_wait", flight, ncopy, plan)
    params = [[None] + [rows_of(src[nm]) for nm in REPLICATED] + [taps_first(src[nm]) for nm in sharded]
              for src in (wts, mom, var)]
    summed = _adamw_small(gathered, *params, me.astype(jnp.int32).reshape(1))
    loss_total = summed[0][0][0, 0]
    for nm, vals in zip(REPLICATED, summed[1:1 + len(REPLICATED)]):
        results[nm] = [a.reshape(wts[nm].shape) for a in vals]
    for nm, vals in zip(sharded, summed[1 + len(REPLICATED):]):
        results[nm] = [taps_first(a) for a in vals]

    outs = [loss_total, grad_x.reshape(1, T, D)]
    for slot in range(4):
        outs.extend(results[nm][slot] for nm in WEIGHTS)
    return tuple(outs)
```

```python
import jax
import jax.numpy as jnp
from jax import lax
from jax.experimental import pallas as pl
from jax.experimental.pallas import tpu as pltpu

F32 = jnp.float32
BF16 = jnp.bfloat16

T = 2048
D = 1024
N_HEADS = 8
QK_NOPE = 128
QK_ROPE = 64
V_HEAD = 128
Q_LORA = 384
KV_LORA = 256
D_FF = 2816
CHUNK = 64
ROPE_THETA = 10000.0
EPS = 1e-6
NEG_INF = -1e30
ADAM_LR = 0.001
ADAM_B1 = 0.9
ADAM_B2 = 0.999
ADAM_EPS = 1e-08
ADAM_WD = 0.01
ADAM_STEP = 10

N_DEV = 8
N_CHIP = 4
FF_BLK = D_FF * 2 // N_DEV
N_FF_BLK = D_FF // FF_BLK
QK_PAD = 256
HALO = 16

TM = 1024
TS = 512
TR = 256
TQ = 512
VMEM_LIMIT = 56 * 1024 * 1024

NN = (((1,), (0,)), ((), ()))
NT = (((1,), (1,)), ((), ()))
TN = (((0,), (0,)), ((), ()))
MESH = pl.DeviceIdType.MESH


def _params(sem):
    return pltpu.CompilerParams(dimension_semantics=sem, vmem_limit_bytes=VMEM_LIMIT)


ANY_SPEC = pl.BlockSpec(memory_space=pl.ANY)
VMEM_SPEC = pl.BlockSpec(memory_space=pltpu.VMEM)


class _Chain:
    last = None


def _pallas(body, *, name, in_specs, out_specs, out_shape, grid=(), scratch_shapes=(), n_prefetch=0, aliases=None,
            params=None):
    def run(*args):
        after = _Chain.last
        n_lead = len(args)
        specs, operands, fn = list(in_specs), list(args), body
        if after is not None:
            def fn(*refs):
                return body(*refs[:n_lead], *refs[n_lead + 1:])
            specs.append(ANY_SPEC)
            operands.append(after)
        kw = dict(name=name, out_shape=out_shape, input_output_aliases=aliases or {})
        if params is not None:
            kw["compiler_params"] = params
        if n_prefetch:
            kw["grid_spec"] = pltpu.PrefetchScalarGridSpec(
                num_scalar_prefetch=n_prefetch, grid=grid, in_specs=specs, out_specs=out_specs,
                scratch_shapes=scratch_shapes)
        else:
            kw.update(grid=grid, in_specs=specs, out_specs=out_specs, scratch_shapes=scratch_shapes)
        outs = pl.pallas_call(fn, **kw)(*operands)
        _Chain.last = outs[0] if isinstance(outs, (list, tuple)) else outs
        return outs
    return run


def _mm(name, a, b, *, grid, a_spec, b_spec, o_spec, o_shape, o_dtype, dims, k_axis=None, acc_shape=None,
        add=None, add_spec=None):
    nk = grid[k_axis] if k_axis is not None else 1
    has_add = add is not None

    def body(*refs):
        a_ref, b_ref = refs[0], refs[1]
        p = 2
        add_ref = None
        if has_add:
            add_ref = refs[p]
            p += 1
        o_ref = refs[p]
        p += 1
        r = lax.dot_general(a_ref[...].astype(BF16), b_ref[...].astype(BF16), dims, preferred_element_type=F32)
        if k_axis is None:
            if has_add:
                r = r + add_ref[...].astype(F32)
            o_ref[...] = r.astype(o_dtype)
        else:
            acc = refs[p]
            k = pl.program_id(k_axis)

            @pl.when(k == 0)
            def _():
                acc[...] = r

            @pl.when(k > 0)
            def _():
                acc[...] += r

            @pl.when(k == nk - 1)
            def _():
                t = acc[...]
                if has_add:
                    t = t + add_ref[...].astype(F32)
                o_ref[...] = t.astype(o_dtype)

    in_specs = [a_spec, b_spec]
    args = [a, b]
    if has_add:
        in_specs.append(add_spec if add_spec is not None else o_spec)
        args.append(add)
    sem = tuple("arbitrary" if ax == k_axis else "parallel" for ax in range(len(grid)))
    scratch = [pltpu.VMEM(acc_shape, F32)] if k_axis is not None else []
    return _pallas(body, name=name, grid=grid, in_specs=in_specs, out_specs=o_spec,
                   out_shape=jax.ShapeDtypeStruct(o_shape, o_dtype), scratch_shapes=scratch, params=_params(sem))(*args)


def _mm_sum(name, parts, *, grid, o_spec, o_shape, o_dtype, add=None, norm_bwd=None):
    has_add = add is not None
    np_ = len(parts)
    nn = 1 if norm_bwd is None else len(norm_bwd[1])
    has_res = norm_bwd is not None and norm_bwd[2] is not None

    def body(*refs):
        accs = [None] * nn
        for p, (_, _, _, _, dims, n) in enumerate(parts):
            a_ref, b_ref = refs[2 * p], refs[2 * p + 1]
            for k in range(a_ref.shape[0]):
                r = lax.dot_general(a_ref[k], b_ref[k], dims, preferred_element_type=F32)
                accs[n] = r if accs[n] is None else accs[n] + r
        if norm_bwd is None:
            acc = accs[0]
            if has_add:
                acc = acc + refs[2 * np_][...]
            refs[-1][...] = acc.astype(o_dtype)
            return
        x_ref, g_refs = refs[2 * np_], refs[2 * np_ + 1:2 * np_ + 1 + nn]
        dx_ref, dxb_ref, dg_refs = refs[-2 - nn], refs[-1 - nn], refs[-nn:]
        xv = x_ref[...]
        r = lax.rsqrt(jnp.mean(xv * xv, axis=-1, keepdims=True) + EPS)
        xn = xv * r
        dx = refs[2 * np_ + 1 + nn][...] if has_res else None
        sums = []
        for acc, g_ref in zip(accs, g_refs):
            gdy = acc * g_ref[...]
            t = r * (gdy - xn * jnp.mean(gdy * xn, axis=-1, keepdims=True))
            dx = t if dx is None else dx + t
            sums.append(jnp.sum(acc * xn, axis=0, keepdims=True))
        dx_ref[...] = dx
        dxb_ref[...] = dx.astype(BF16)

        @pl.when(pl.program_id(0) == 0)
        def _():
            for dg_ref, part in zip(dg_refs, sums):
                dg_ref[...] = part

        @pl.when(pl.program_id(0) > 0)
        def _():
            for dg_ref, part in zip(dg_refs, sums):
                dg_ref[...] += part

    in_specs, args = [], []
    for a, a_spec, b, b_spec, _, _ in parts:
        in_specs += [a_spec, b_spec]
        args += [a, b]
    if norm_bwd is None:
        if has_add:
            in_specs.append(o_spec)
            args.append(add)
        return _pallas(body, name=name, grid=grid, in_specs=in_specs, out_specs=o_spec,
                       out_shape=jax.ShapeDtypeStruct(o_shape, o_dtype),
                       params=_params(("parallel",) * len(grid)))(*args)
    x, gains, dres = norm_bwd
    vec = pl.BlockSpec((1, o_shape[1]), lambda i: (0, 0))
    in_specs += [o_spec] + [vec] * nn + ([o_spec] if has_res else [])
    args += [x] + list(gains) + ([dres] if has_res else [])
    outs = _pallas(body, name=name, grid=grid, in_specs=in_specs, out_specs=[o_spec, o_spec] + [vec] * nn,
                   out_shape=[jax.ShapeDtypeStruct(o_shape, F32), jax.ShapeDtypeStruct(o_shape, BF16)]
                   + [jax.ShapeDtypeStruct((1, o_shape[1]), F32)] * nn,
                   params=_params(("arbitrary",)))(*args)
    return outs[0], outs[1], list(outs[2:])


def _mm_rows(name, a, b, dims, o_dtype, n_out, *, tn=None, add=None):
    k = a.shape[1]
    tn = n_out if tn is None else tn
    if dims == NN:
        b_spec = pl.BlockSpec((k, tn), lambda n, i: (0, n))
    else:
        b_spec = pl.BlockSpec((tn, k), lambda n, i: (n, 0))
    return _mm(name, a, b, grid=(n_out // tn, T // TM),
               a_spec=pl.BlockSpec((TM, k), lambda n, i: (i, 0)), b_spec=b_spec,
               o_spec=pl.BlockSpec((TM, tn), lambda n, i: (i, n)), o_shape=(T, n_out), o_dtype=o_dtype,
               dims=dims, add=add)


def _wgrads(name, jobs):
    arrays, index = [], {}
    for a, b in jobs:
        for arr in (a, b):
            if id(arr) not in index:
                index[id(arr)] = len(arrays)
                arrays.append(arr)
    n_in = len(arrays)

    def body(*refs):
        for t, (a, b) in enumerate(jobs):
            a_ref, b_ref, o_ref = refs[index[id(a)]], refs[index[id(b)]], refs[n_in + t]
            if a.ndim == 3:
                for h in range(a.shape[0]):
                    o_ref[h] = lax.dot_general(a_ref[h], b_ref[...], TN, preferred_element_type=F32).astype(BF16)
            else:
                o_ref[...] = lax.dot_general(a_ref[...], b_ref[...], TN, preferred_element_type=F32).astype(BF16)

    out_shape = [jax.ShapeDtypeStruct(a.shape[:-2] + (a.shape[-1], b.shape[-1]), BF16) for a, b in jobs]
    return _pallas(body, name=name, in_specs=[VMEM_SPEC] * n_in, out_specs=[VMEM_SPEC] * len(jobs), out_shape=out_shape,
                   params=pltpu.CompilerParams(vmem_limit_bytes=VMEM_LIMIT))(*arrays)


def _mm_wgrad(name, a, b, *, tn=512):
    k, n = a.shape[1], b.shape[1]
    tn = min(tn, n)
    return _mm(name, a, b, grid=(n // tn,),
               a_spec=pl.BlockSpec((T, k), lambda j: (0, 0)), b_spec=pl.BlockSpec((T, tn), lambda j: (0, j)),
               o_spec=pl.BlockSpec((k, tn), lambda j: (0, j)), o_shape=(k, n), o_dtype=BF16, dims=TN)


def _rms_fwd(name, x, g):
    d = x.shape[1]

    def body(x_ref, g_ref, o_ref):
        xv = x_ref[...]
        r = lax.rsqrt(jnp.mean(xv * xv, axis=-1, keepdims=True) + EPS)
        o_ref[...] = ((xv * r) * g_ref[...]).astype(BF16)

    return _pallas(
        body, name=name, grid=(T // TM,),
        in_specs=[pl.BlockSpec((TM, d), lambda i: (i, 0)), pl.BlockSpec((1, d), lambda i: (0, 0))],
        out_specs=pl.BlockSpec((TM, d), lambda i: (i, 0)),
        out_shape=jax.ShapeDtypeStruct((T, d), BF16), params=_params(("parallel",)))(x, g)


def _rms(xv, g):
    return (xv * lax.rsqrt(jnp.mean(xv * xv, axis=-1, keepdims=True) + EPS)) * g


def _rms_bwd(name, x, gains, dys, dres=None):
    d = x.shape[1]
    n = len(gains)
    has_res = dres is not None

    def body(*refs):
        x_ref, g_refs, dy_refs = refs[0], refs[1:1 + n], refs[1 + n:1 + 2 * n]
        dx_ref, dxb_ref = refs[-2 - n], refs[-1 - n]
        dg_refs = refs[-n:]
        xv = x_ref[...]
        r = lax.rsqrt(jnp.mean(xv * xv, axis=-1, keepdims=True) + EPS)
        xn = xv * r
        dx = refs[1 + 2 * n][...] if has_res else None
        parts = []
        for g_ref, dy_ref in zip(g_refs, dy_refs):
            dyv = dy_ref[...].astype(F32)
            gdy = dyv * g_ref[...]
            t = r * (gdy - xn * jnp.mean(gdy * xn, axis=-1, keepdims=True))
            dx = t if dx is None else dx + t
            parts.append(jnp.sum(dyv * xn, axis=0, keepdims=True))
        dx_ref[...] = dx
        dxb_ref[...] = dx.astype(BF16)

        @pl.when(pl.program_id(0) == 0)
        def _():
            for dg_ref, part in zip(dg_refs, parts):
                dg_ref[...] = part

        @pl.when(pl.program_id(0) > 0)
        def _():
            for dg_ref, part in zip(dg_refs, parts):
                dg_ref[...] += part

    row = pl.BlockSpec((TR, d), lambda i: (i, 0))
    vec = pl.BlockSpec((1, d), lambda i: (0, 0))
    args = [x] + list(gains) + list(dys) + ([dres] if has_res else [])
    in_specs = [row] + [vec] * n + [row] * n + ([row] if has_res else [])
    outs = _pallas(
        body, name=name, grid=(T // TR,), in_specs=in_specs, out_specs=[row, row] + [vec] * n,
        out_shape=[jax.ShapeDtypeStruct((T, d), F32), jax.ShapeDtypeStruct((T, d), BF16)]
        + [jax.ShapeDtypeStruct((1, d), F32)] * n,
        params=_params(("arbitrary",)))(*args)
    return outs[0], outs[1], list(outs[2:])


def _final(h, g, tgt):
    def body(h_ref, g_ref, t_ref, loss_ref, dh_ref, dhb_ref, dg_ref):
        hv = h_ref[...]
        r = lax.rsqrt(jnp.mean(hv * hv, axis=-1, keepdims=True) + EPS)
        xn = hv * r
        gv = g_ref[...]
        err = xn * gv - t_ref[...]
        part_loss = 0.5 * jnp.sum(jnp.mean(err * err, axis=-1, keepdims=True), axis=0, keepdims=True)
        dy = err * (1.0 / D)
        gdy = dy * gv
        dh = r * (gdy - xn * jnp.mean(gdy * xn, axis=-1, keepdims=True))
        dh_ref[...] = dh
        dhb_ref[...] = dh.astype(BF16)
        part = jnp.sum(dy * xn, axis=0, keepdims=True)
        first = pl.program_id(0) == 0

        @pl.when(first)
        def _():
            dg_ref[...] = part
            loss_ref[...] = jnp.broadcast_to(part_loss, (1, 128))

        @pl.when(jnp.logical_not(first))
        def _():
            dg_ref[...] += part
            loss_ref[...] += jnp.broadcast_to(part_loss, (1, 128))

    row = pl.BlockSpec((TR, D), lambda i: (i, 0))
    vec = pl.BlockSpec((1, D), lambda i: (0, 0))
    return _pallas(
        body, name="final_loss", grid=(T // TR,), in_specs=[row, vec, row],
        out_specs=[pl.BlockSpec((1, 128), lambda i: (0, 0)), row, row, vec],
        out_shape=[jax.ShapeDtypeStruct((1, 128), F32), jax.ShapeDtypeStruct((T, D), F32),
                   jax.ShapeDtypeStruct((T, D), BF16), jax.ShapeDtypeStruct((1, D), F32)],
        params=_params(("arbitrary",)))(h, g, tgt)


def _prev_idx(i, rows=TR):
    return jnp.maximum(i * (rows // HALO) - 1, 0)


def _next_idx(i, rows=TR):
    return jnp.minimum((i + 1) * (rows // HALO), T // HALO - 1)


def _causal_taps(ext):
    return pltpu.roll(ext, 2, 0)[HALO:], pltpu.roll(ext, 1, 0)[HALO:], ext[HALO:]


def _anticausal_taps(ext, n):
    rows = ext.shape[0]
    return pltpu.roll(ext, rows - 1, 0)[:n], pltpu.roll(ext, rows - 2, 0)[:n]


MIX_COLS = 512


def _mixer_in(hn, w_in, w):
    nc = D // MIX_COLS

    def body(h_ref, hh_ref, wb_ref, wc_ref, wu_ref, w_ref, b_ref, c_ref, u_ref, y_ref):
        i = pl.program_id(1)
        hv = h_ref[...]
        he = jnp.concatenate([hh_ref[...], hv], axis=0)
        ce = lax.dot_general(he, wc_ref[...], NN, preferred_element_type=F32).astype(BF16)
        ue = lax.dot_general(he, wu_ref[...], NN, preferred_element_type=F32).astype(BF16)
        bv = lax.dot_general(hv, wb_ref[...], NN, preferred_element_type=F32).astype(BF16)
        b_ref[...] = bv
        c_ref[...] = ce[HALO:]
        u_ref[...] = ue[HALO:]
        row = lax.broadcasted_iota(jnp.int32, (HALO + TS, 1), 0)
        cu = jnp.where(jnp.logical_or(i > 0, row >= HALO), ce.astype(F32) * ue.astype(F32), 0.0)
        x2, x1, x0 = _causal_taps(cu)
        wv = w_ref[...]
        cv = (x2 * wv[0:1] + x1 * wv[1:2]) + x0 * wv[2:3]
        y_ref[...] = (bv.astype(F32) * cv).astype(BF16)

    def cols(part):
        return pl.BlockSpec((D, MIX_COLS), lambda j, i: (0, part * nc + j))

    blk = pl.BlockSpec((TS, MIX_COLS), lambda j, i: (i, j))
    out = jax.ShapeDtypeStruct((T, D), BF16)
    return _pallas(
        body, name="l0_in", grid=(nc, T // TS),
        in_specs=[pl.BlockSpec((TS, D), lambda j, i: (i, 0)), pl.BlockSpec((HALO, D), lambda j, i: (_prev_idx(i, TS), 0)),
                  cols(0), cols(1), cols(2), pl.BlockSpec((3, MIX_COLS), lambda j, i: (0, j))],
        out_specs=[blk] * 4, out_shape=[out] * 4,
        params=_params(("parallel", "parallel")))(hn, hn, w_in, w_in, w_in, w)


def _mixer_out_bwd(dh, w_out, zb, zc, zu, w):
    last = T // TR - 1

    def body(dh_ref, dhn_ref, wo_ref, b_ref, bn_ref, c_ref, ch_ref, u_ref, uh_ref, w_ref, dz_ref, dw_ref):
        i = pl.program_id(0)
        dye = lax.dot_general(jnp.concatenate([dh_ref[...], dhn_ref[...]], axis=0), wo_ref[...], NT,
                              preferred_element_type=F32)
        cv_ = c_ref[...].astype(F32)
        uv = u_ref[...].astype(F32)
        cu = cv_ * uv
        cuh = jnp.where(i > 0, ch_ref[...].astype(F32) * uh_ref[...].astype(F32), 0.0)
        x2, x1, x0 = _causal_taps(jnp.concatenate([cuh, cu], axis=0))
        wv = w_ref[...]
        conv = (x2 * wv[0:1] + x1 * wv[1:2]) + x0 * wv[2:3]
        dyv = dye[:TR]
        dz_ref[:, 0:D] = (dyv * conv).astype(BF16)
        dconv = dyv * b_ref[...].astype(F32)
        dconv_n = jnp.where(i < last, dye[TR:] * bn_ref[...].astype(F32), 0.0)
        n1, n2 = _anticausal_taps(jnp.concatenate([dconv, dconv_n], axis=0), TR)
        dcu = (dconv * wv[2:3] + n1 * wv[1:2]) + n2 * wv[0:1]
        dz_ref[:, D:2 * D] = (dcu * uv).astype(BF16)
        dz_ref[:, 2 * D:3 * D] = (dcu * cv_).astype(BF16)
        part = jnp.concatenate([jnp.sum(dconv * x2, axis=0, keepdims=True),
                                jnp.sum(dconv * x1, axis=0, keepdims=True),
                                jnp.sum(dconv * x0, axis=0, keepdims=True)], axis=0)

        @pl.when(i == 0)
        def _():
            dw_ref[...] = part

        @pl.when(i > 0)
        def _():
            dw_ref[...] += part

    main = pl.BlockSpec((TR, D), lambda i: (i, 0))
    prev = pl.BlockSpec((HALO, D), lambda i: (_prev_idx(i), 0))
    nxt = pl.BlockSpec((HALO, D), lambda i: (_next_idx(i), 0))
    wspec = pl.BlockSpec((3, D), lambda i: (0, 0))
    return _pallas(
        body, name="d_l0_out", grid=(T // TR,),
        in_specs=[main, nxt, pl.BlockSpec((D, D), lambda i: (0, 0)), main, nxt, main, prev, main, prev, wspec],
        out_specs=[pl.BlockSpec((TR, 3 * D), lambda i: (i, 0)), wspec],
        out_shape=[jax.ShapeDtypeStruct((T, 3 * D), BF16), jax.ShapeDtypeStruct((3, D), F32)],
        params=_params(("arbitrary",)))(dh, dh, w_out, zb, zb, zc, zc, zu, zu, w)


def _sigmoid(x):
    return 0.5 * jnp.tanh(0.5 * x) + 0.5


def _ffn_up_act(name, hf, w_up, w, b):
    def body(h_ref, hh_ref, wg_ref, wv_ref, w_ref, b_ref, g_ref, v_ref, a_ref):
        i = pl.program_id(1)
        hv = h_ref[...]
        ge = lax.dot_general(jnp.concatenate([hh_ref[...], hv], axis=0), wg_ref[...], NT,
                             preferred_element_type=F32).astype(BF16)
        v = lax.dot_general(hv, wv_ref[...], NT, preferred_element_type=F32).astype(BF16)
        g_ref[...] = ge[HALO:]
        v_ref[...] = v
        ext = ge.astype(F32)
        row = lax.broadcasted_iota(jnp.int32, (HALO + TM, 1), 0)
        ext = jnp.where(jnp.logical_or(i > 0, row >= HALO), ext, 0.0)
        x2, x1, x0 = _causal_taps(ext)
        wv = w_ref[...]
        gc = ((x2 * wv[0:1] + x1 * wv[1:2]) + x0 * wv[2:3]) + b_ref[...]
        a_ref[...] = ((gc * _sigmoid(gc)) * v.astype(F32)).astype(BF16)

    blk = pl.BlockSpec((None, TM, FF_BLK), lambda j, i: (j, i, 0))
    out = jax.ShapeDtypeStruct((N_FF_BLK, T, FF_BLK), BF16)
    return _pallas(
        body, name=name, grid=(N_FF_BLK, T // TM),
        in_specs=[pl.BlockSpec((TM, D), lambda j, i: (i, 0)),
                  pl.BlockSpec((HALO, D), lambda j, i: (_prev_idx(i, TM), 0)),
                  pl.BlockSpec((None, None, FF_BLK, D), lambda j, i: (0, j, 0, 0)),
                  pl.BlockSpec((None, None, FF_BLK, D), lambda j, i: (0, j + N_FF_BLK, 0, 0)),
                  pl.BlockSpec((None, 3, FF_BLK), lambda j, i: (j, 0, 0)),
                  pl.BlockSpec((None, 1, FF_BLK), lambda j, i: (j, 0, 0))],
        out_specs=[blk, blk, blk], out_shape=[out, out, out],
        params=_params(("parallel", "parallel")))(hf, hf, w_up, w_up, w, b)


def _ffn_dact(name, dh, w_down4, g, v, w, b):
    last = T // TS - 1

    def body(dh_ref, dhn_ref, wd_ref, g_ref, gp_ref, gn_ref, v_ref, vn_ref, w_ref, b_ref, dg_ref, dv_ref, dw_ref, db_ref):
        i = pl.program_id(1)
        da = lax.dot_general(jnp.concatenate([dh_ref[...], dhn_ref[...]], axis=0), wd_ref[...], NT,
                             preferred_element_type=F32)
        row = lax.broadcasted_iota(jnp.int32, (TS + HALO, 1), 0)
        da = jnp.where(jnp.logical_or(i < last, row < TS), da, 0.0)
        gp = jnp.where(i > 0, gp_ref[...].astype(F32), 0.0)
        ext = jnp.concatenate([gp, g_ref[...].astype(F32), gn_ref[...].astype(F32)], axis=0)
        x2, x1, x0 = _causal_taps(ext)
        wv = w_ref[...]
        gc = ((x2 * wv[0:1] + x1 * wv[1:2]) + x0 * wv[2:3]) + b_ref[...]
        sg = _sigmoid(gc)
        vv = jnp.concatenate([v_ref[...].astype(F32), vn_ref[...].astype(F32)], axis=0)
        dv_ref[...] = (da[:TS] * (gc[:TS] * sg[:TS])).astype(BF16)
        dgc = (da * vv) * (sg * (1.0 + gc * (1.0 - sg)))
        n1, n2 = _anticausal_taps(dgc, TS)
        d0 = dgc[:TS]
        dg_ref[...] = ((d0 * wv[2:3] + n1 * wv[1:2]) + n2 * wv[0:1]).astype(BF16)
        part_w = jnp.concatenate([jnp.sum(d0 * x2[:TS], axis=0, keepdims=True),
                                  jnp.sum(d0 * x1[:TS], axis=0, keepdims=True),
                                  jnp.sum(d0 * x0[:TS], axis=0, keepdims=True)], axis=0)
        part_b = jnp.sum(d0, axis=0, keepdims=True)

        @pl.when(i == 0)
        def _():
            dw_ref[...] = part_w
            db_ref[...] = part_b

        @pl.when(i > 0)
        def _():
            dw_ref[...] += part_w
            db_ref[...] += part_b

    blk = pl.BlockSpec((None, TS, FF_BLK), lambda j, i: (j, i, 0))
    prev = pl.BlockSpec((None, HALO, FF_BLK), lambda j, i: (j, _prev_idx(i, TS), 0))
    nxt = pl.BlockSpec((None, HALO, FF_BLK), lambda j, i: (j, _next_idx(i, TS), 0))
    wspec = pl.BlockSpec((None, 3, FF_BLK), lambda j, i: (j, 0, 0))
    bspec = pl.BlockSpec((None, 1, FF_BLK), lambda j, i: (j, 0, 0))
    return _pallas(
        body, name=name, grid=(N_FF_BLK, T // TS),
        in_specs=[pl.BlockSpec((TS, D), lambda j, i: (i, 0)),
                  pl.BlockSpec((HALO, D), lambda j, i: (_next_idx(i, TS), 0)),
                  pl.BlockSpec((None, None, FF_BLK, D), lambda j, i: (0, j, 0, 0)),
                  blk, prev, nxt, blk, nxt, wspec, bspec],
        out_specs=[blk, blk, wspec, bspec],
        out_shape=[jax.ShapeDtypeStruct((N_FF_BLK, T, FF_BLK), BF16), jax.ShapeDtypeStruct((N_FF_BLK, T, FF_BLK), BF16),
                   jax.ShapeDtypeStruct((N_FF_BLK, 3, FF_BLK), F32), jax.ShapeDtypeStruct((N_FF_BLK, 1, FF_BLK), F32)],
        params=_params(("parallel", "arbitrary")))(dh, dh, w_down4, g, g, g, v, v, w, b)


def _rope_tables(pos, inv_freq):
    half = QK_ROPE // 2

    def body(p_ref, f_ref, c_ref, sa_ref, sb_ref):
        ang = p_ref[...].astype(F32) * f_ref[...]
        lane = lax.broadcasted_iota(jnp.int32, (T, 128), 1)
        c = jnp.cos(ang)
        s = jnp.sin(ang)
        c_ref[...] = jnp.where(lane < 2 * half, c, 0.0)
        sa_ref[...] = jnp.where(lane < half, -s, 0.0)
        sb_ref[...] = jnp.where(jnp.logical_and(lane >= half, lane < 2 * half), s, 0.0)

    return _pallas(
        body, name="rope_tables", in_specs=[VMEM_SPEC] * 2, out_specs=[VMEM_SPEC] * 3,
        out_shape=[jax.ShapeDtypeStruct((T, 128), F32)] * 3,
        params=pltpu.CompilerParams(vmem_limit_bytes=VMEM_LIMIT))(pos, inv_freq)


def _rotate(r, c, sa, sb, sign):
    return r * c + sign * (pltpu.roll(r, 96, 1) * sa + pltpu.roll(r, 32, 1) * sb)


def _attn_pre(h2, g_kv, g_l1, g_kvl, g_ql, w_dkv, w_kr, w_uk, w_uv, w_dq, w_uq, tables):
    def body(h_ref, c_ref, sa_ref, sb_ref, gkv_ref, gl1_ref, gkvl_ref, gql_ref, wdkv_ref, wkr_ref, wuk_ref, wuv_ref,
             wdq_ref, wuq_ref, hk_ref, hn_ref, ckvr_ref, ckv_ref, kr_ref, kn_ref, v_ref, cqr_ref, cq_ref, q_ref):
        xv = h_ref[...]
        xn = xv * lax.rsqrt(jnp.mean(xv * xv, axis=-1, keepdims=True) + EPS)
        hk = (xn * gkv_ref[...]).astype(BF16)
        hn = (xn * gl1_ref[...]).astype(BF16)
        hk_ref[...] = hk
        hn_ref[...] = hn
        cv, sav, sbv = c_ref[...], sa_ref[...], sb_ref[...]
        raw = lax.dot_general(hk, wdkv_ref[...], NN, preferred_element_type=F32)
        ckvr_ref[...] = raw
        ckv = _rms(raw, gkvl_ref[...]).astype(BF16)
        ckv_ref[...] = ckv
        kr = lax.dot_general(hk, wkr_ref[...], NT, preferred_element_type=F32)
        kr_ref[...] = _rotate(kr, cv, sav, sbv, 1.0).astype(BF16)
        kn_ref[...] = lax.dot_general(ckv, wuk_ref[...], NN, preferred_element_type=F32).astype(BF16)
        v_ref[...] = lax.dot_general(ckv, wuv_ref[...], NN, preferred_element_type=F32).astype(BF16)
        cqr = lax.dot_general(hn, wdq_ref[...], NN, preferred_element_type=F32)
        cqr_ref[...] = cqr
        cq = _rms(cqr, gql_ref[...]).astype(BF16)
        cq_ref[...] = cq
        for h in range(N_HEADS):
            r = lax.dot_general(cq, wuq_ref[h], NT, preferred_element_type=F32)
            q_ref[h, :, :QK_NOPE] = r[:, :QK_NOPE].astype(BF16)
            q_ref[h, :, QK_NOPE:] = _rotate(r[:, QK_NOPE:], cv, sav, sbv, 1.0).astype(BF16)

    def rows(d):
        return pl.BlockSpec((TS, d), lambda i: (i, 0))

    def whole(a):
        return pl.BlockSpec(a.shape, lambda i: (0,) * a.ndim)

    wholes = [g_kv, g_l1, g_kvl, g_ql, w_dkv, w_kr, w_uk, w_uv, w_dq, w_uq]
    outs = [(D, BF16), (D, BF16), (KV_LORA, F32), (KV_LORA, BF16), (128, BF16), (N_HEADS * QK_NOPE, BF16),
            (N_HEADS * V_HEAD, BF16), (Q_LORA, F32), (Q_LORA, BF16)]
    return _pallas(
        body, name="attn_pre", grid=(T // TS,),
        in_specs=[rows(D), rows(128), rows(128), rows(128)] + [whole(a) for a in wholes],
        out_specs=[rows(d) for d, _ in outs] + [pl.BlockSpec((N_HEADS, TS, QK_PAD), lambda i: (0, i, 0))],
        out_shape=[jax.ShapeDtypeStruct((T, d), dt) for d, dt in outs]
        + [jax.ShapeDtypeStruct((N_HEADS, T, QK_PAD), BF16)],
        params=_params(("parallel",)))(h2, *tables, *wholes)


SCALE = (QK_NOPE + QK_ROPE) ** -0.5
LOG2E = 1.4426950408889634
SCALE2 = SCALE * LOG2E


def _diag_mask(transposed):
    shift = CHUNK.bit_length() - 1
    a = lax.broadcasted_iota(jnp.int32, (TQ, TQ), 0) >> shift
    b = lax.broadcasted_iota(jnp.int32, (TQ, TQ), 1) >> shift
    return (a <= b) if transposed else (b <= a)


def _as_row(col):
    return jnp.transpose(jnp.broadcast_to(col, (col.shape[0], 128)), (1, 0))[0:1]


def _keys(kn_ref, kr_ref, off):
    return jnp.concatenate([kn_ref[pl.ds(off, TQ), :], kr_ref[pl.ds(off, TQ), :]], axis=1)


def _attn_fwd(q, kn, kr, v):
    hp = 2

    def body(q_ref, kn_ref, kr_ref, v_ref, o_ref, lse_ref):
        i = pl.program_id(1)
        qs = [q_ref[a] for a in range(hp)]

        def step(j, carry, masked):
            off = pl.multiple_of(j * TQ, TQ)
            krv = kr_ref[pl.ds(off, TQ), :]
            ss = []
            for a in range(hp):
                kk = jnp.concatenate([kn_ref[pl.ds(off, TQ), a * QK_NOPE:(a + 1) * QK_NOPE], krv], axis=1)
                ss.append(lax.dot_general(qs[a], kk, NT, preferred_element_type=F32))
            out = []
            for a in range(hp):
                m, l, acc = carry[a]
                s = ss[a] * SCALE2
                if masked:
                    s = jnp.where(_diag_mask(False), s, NEG_INF)
                m_new = jnp.maximum(m, jnp.max(s, axis=-1, keepdims=True))
                p = jnp.exp2(s - m_new)
                alpha = jnp.exp2(m - m_new)
                l = alpha * l + jnp.sum(p, axis=-1, keepdims=True)
                pv = lax.dot_general(p.astype(BF16), v_ref[pl.ds(off, TQ), a * V_HEAD:(a + 1) * V_HEAD], NN,
                                     preferred_element_type=F32)
                out.append((m_new, l, alpha * acc + pv))
            return tuple(out)

        one = (jnp.full((TQ, 1), NEG_INF, F32), jnp.zeros((TQ, 1), F32), jnp.zeros((TQ, V_HEAD), F32))
        carry = lax.fori_loop(0, i, lambda j, cr: step(j, cr, False), (one,) * hp)
        carry = step(i, carry, True)
        for a, (m, l, acc) in enumerate(carry):
            o_ref[:, a * V_HEAD:(a + 1) * V_HEAD] = (acc / l).astype(BF16)
            lse_ref[a] = _as_row(m + jnp.log(l) * LOG2E)

    return _pallas(
        body, name="attn_fwd", grid=(N_HEADS // hp, T // TQ),
        in_specs=[pl.BlockSpec((hp, TQ, QK_PAD), lambda h, i: (h, i, 0)),
                  pl.BlockSpec((T, hp * QK_NOPE), lambda h, i: (0, h)),
                  pl.BlockSpec((T, 128), lambda h, i: (0, 0)),
                  pl.BlockSpec((T, hp * V_HEAD), lambda h, i: (0, h))],
        out_specs=[pl.BlockSpec((TQ, hp * V_HEAD), lambda h, i: (i, h)), pl.BlockSpec((hp, 1, TQ), lambda h, i: (h, 0, i))],
        out_shape=[jax.ShapeDtypeStruct((T, N_HEADS * V_HEAD), BF16), jax.ShapeDtypeStruct((N_HEADS, 1, T), F32)],
        params=_params(("parallel", "parallel")))(q, kn, kr, v)


def _attn_bwd(q, kn, kr, v, o, do, lse_row, tables):
    nq = T // TQ
    hp = 2
    cos, sa, sb = tables

    def body(q_ref, kn_ref, kr_ref, v_ref, o_ref, do_ref, lse_ref, c_ref, sa_ref, sb_ref,
             dq_ref, dkn_ref, dkr_ref, dv_ref, dq_acc, dl_ref):
        j = pl.program_id(1)

        def cols(a):
            return slice(a * 128, (a + 1) * 128)

        @pl.when(j == 0)
        def _():
            dq_acc[...] = jnp.zeros_like(dq_acc)
            for a in range(hp):
                for i in range(nq):
                    rows = pl.ds(i * TQ, TQ)
                    prod = do_ref[rows, cols(a)].astype(F32) * o_ref[rows, cols(a)].astype(F32)
                    dl_ref[a, :, rows] = _as_row(jnp.sum(prod, axis=-1, keepdims=True))

        krv = kr_ref[...]
        kks = [jnp.concatenate([kn_ref[:, cols(a)], krv], axis=1) for a in range(hp)]
        vvs = [v_ref[:, cols(a)] for a in range(hp)]

        def step(i, carry, masked):
            off = pl.multiple_of(i * TQ, TQ)
            rows = pl.ds(off, TQ)
            qis = [q_ref[a, rows, :] for a in range(hp)]
            dois = [do_ref[rows, cols(a)] for a in range(hp)]
            sts = [lax.dot_general(kks[a], qis[a], NT, preferred_element_type=F32) for a in range(hp)]
            dpts = [lax.dot_general(vvs[a], dois[a], NT, preferred_element_type=F32) for a in range(hp)]
            out = []
            for a in range(hp):
                dk, dv = carry[a]
                st = sts[a] * SCALE2
                if masked:
                    st = jnp.where(_diag_mask(True), st, NEG_INF)
                pt = jnp.exp2(st - lse_ref[a, :, rows])
                dv = dv + lax.dot_general(pt.astype(BF16), dois[a], NN, preferred_element_type=F32)
                dst = ((pt * (dpts[a] - dl_ref[a, :, rows])) * SCALE).astype(BF16)
                dk = dk + lax.dot_general(dst, qis[a], NN, preferred_element_type=F32)
                dq_acc[a, rows, :] += lax.dot_general(dst, kks[a], TN, preferred_element_type=F32)
                out.append((dk, dv))
            return tuple(out)

        zero = (jnp.zeros((TQ, QK_PAD), F32), jnp.zeros((TQ, V_HEAD), F32))
        carry = step(j, (zero,) * hp, True)
        carry = lax.fori_loop(j + 1, nq, lambda i, cr: step(i, cr, False), carry)
        for a, (dk, dv) in enumerate(carry):
            dkn_ref[:, cols(a)] = dk[:, :QK_NOPE].astype(BF16)
            dkr_ref[a] = dk[:, QK_NOPE:]
            dv_ref[:, cols(a)] = dv.astype(BF16)

        @pl.when(j == nq - 1)
        def _():
            for a in range(hp):
                dq = dq_acc[a]
                dq_ref[a, :, :QK_NOPE] = dq[:, :QK_NOPE].astype(BF16)
                dq_ref[a, :, QK_NOPE:] = _rotate(dq[:, QK_NOPE:], c_ref[...], sa_ref[...], sb_ref[...], -1.0).astype(BF16)

    row = pl.BlockSpec((hp, 1, T), lambda h, j: (h, 0, 0))
    head = pl.BlockSpec((TQ, hp * 128), lambda h, j: (j, h))
    whole = pl.BlockSpec((hp, T, QK_PAD), lambda h, j: (h, 0, 0))
    tab = pl.BlockSpec((T, 128), lambda h, j: (0, 0))
    heads = pl.BlockSpec((T, hp * V_HEAD), lambda h, j: (0, h))
    return _pallas(
        body, name="attn_bwd", grid=(N_HEADS // hp, nq),
        in_specs=[whole, head, pl.BlockSpec((TQ, 128), lambda h, j: (j, 0)), head, heads, heads, row, tab, tab, tab],
        out_specs=[whole, head, pl.BlockSpec((hp, TQ, 128), lambda h, j: (h, j, 0)), head],
        out_shape=[jax.ShapeDtypeStruct((N_HEADS, T, QK_PAD), BF16), jax.ShapeDtypeStruct((T, N_HEADS * QK_NOPE), BF16),
                   jax.ShapeDtypeStruct((N_HEADS, T, 128), F32), jax.ShapeDtypeStruct((T, N_HEADS * V_HEAD), BF16)],
        scratch_shapes=[pltpu.VMEM((hp, T, QK_PAD), F32), pltpu.VMEM((hp, 1, T), F32)],
        params=_params(("parallel", "arbitrary")))(q, kn, kr, v, o, do, lse_row, cos, sa, sb)


def _rms_bwd_math(xv, g, dy):
    r = lax.rsqrt(jnp.mean(xv * xv, axis=-1, keepdims=True) + EPS)
    xn = xv * r
    gdy = dy * g
    return r * (gdy - xn * jnp.mean(gdy * xn, axis=-1, keepdims=True)), jnp.sum(dy * xn, axis=0, keepdims=True)


def _attn_post(dq, dkn, dv, dkr, cq_raw, ckv_raw, h2, dres, g_ql, g_kvl, g_l1, g_kv, w_uq, w_uk, w_uv, w_dq, w_dkv,
               w_kr, tables):
    def body(dq_ref, dkn_ref, dv_ref, dkr_ref, cqr_ref, ckvr_ref, h_ref, res_ref, c_ref, sa_ref, sb_ref,
             gql_ref, gkvl_ref, gl1_ref, gkv_ref, wuq_ref, wuk_ref, wuv_ref, wdq_ref, wdkv_ref, wkr_ref,
             dcq_ref, dckv_ref, dkrr_ref, dh_ref, dhb_ref, dgql_ref, dgkvl_ref, dgl1_ref, dgkv_ref):
        dcq = lax.dot_general(dq_ref[0], wuq_ref[0], NN, preferred_element_type=F32)
        for h in range(1, N_HEADS):
            dcq = dcq + lax.dot_general(dq_ref[h], wuq_ref[h], NN, preferred_element_type=F32)
        dcq_raw, s_ql = _rms_bwd_math(cqr_ref[...], gql_ref[...], dcq)
        dcq_raw = dcq_raw.astype(BF16)
        dcq_ref[...] = dcq_raw
        dckv = (lax.dot_general(dkn_ref[...], wuk_ref[...], NT, preferred_element_type=F32)
                + lax.dot_general(dv_ref[...], wuv_ref[...], NT, preferred_element_type=F32))
        dckv_raw, s_kvl = _rms_bwd_math(ckvr_ref[...], gkvl_ref[...], dckv)
        dckv_raw = dckv_raw.astype(BF16)
        dckv_ref[...] = dckv_raw
        dkr = dkr_ref[0]
        for h in range(1, N_HEADS):
            dkr = dkr + dkr_ref[h]
        dkr_raw = _rotate(dkr, c_ref[...], sa_ref[...], sb_ref[...], -1.0).astype(BF16)
        dkrr_ref[...] = dkr_raw
        d_hn = lax.dot_general(dcq_raw, wdq_ref[...], NT, preferred_element_type=F32)
        d_hk = (lax.dot_general(dckv_raw, wdkv_ref[...], NT, preferred_element_type=F32)
                + lax.dot_general(dkr_raw, wkr_ref[...], NN, preferred_element_type=F32))
        xv = h_ref[...]
        r = lax.rsqrt(jnp.mean(xv * xv, axis=-1, keepdims=True) + EPS)
        xn = xv * r
        dx = res_ref[...]
        sums = [s_ql, s_kvl]
        for dy, g_ref in ((d_hn, gl1_ref), (d_hk, gkv_ref)):
            gdy = dy * g_ref[...]
            dx = dx + r * (gdy - xn * jnp.mean(gdy * xn, axis=-1, keepdims=True))
            sums.append(jnp.sum(dy * xn, axis=0, keepdims=True))
        dh_ref[...] = dx
        dhb_ref[...] = dx.astype(BF16)
        dg_refs = (dgql_ref, dgkvl_ref, dgl1_ref, dgkv_ref)

        @pl.when(pl.program_id(0) == 0)
        def _():
            for dg_ref, part in zip(dg_refs, sums):
                dg_ref[...] = part

        @pl.when(pl.program_id(0) > 0)
        def _():
            for dg_ref, part in zip(dg_refs, sums):
                dg_ref[...] += part

    def rows(d):
        return pl.BlockSpec((TS, d), lambda i: (i, 0))

    def heads(d):
        return pl.BlockSpec((N_HEADS, TS, d), lambda i: (0, i, 0))

    def whole(a):
        return pl.BlockSpec(a.shape, lambda i: (0,) * a.ndim)

    wholes = [g_ql, g_kvl, g_l1, g_kv, w_uq, w_uk, w_uv, w_dq, w_dkv, w_kr]
    vecs = [Q_LORA, KV_LORA, D, D]
    return _pallas(
        body, name="attn_post", grid=(T // TS,),
        in_specs=[heads(QK_PAD), rows(N_HEADS * QK_NOPE), rows(N_HEADS * V_HEAD), heads(128), rows(Q_LORA),
                  rows(KV_LORA), rows(D), rows(D), rows(128), rows(128), rows(128)] + [whole(a) for a in wholes],
        out_specs=[rows(Q_LORA), rows(KV_LORA), rows(128), rows(D), rows(D)]
        + [pl.BlockSpec((1, d), lambda i: (0, 0)) for d in vecs],
        out_shape=[jax.ShapeDtypeStruct((T, Q_LORA), BF16), jax.ShapeDtypeStruct((T, KV_LORA), BF16),
                   jax.ShapeDtypeStruct((T, 128), BF16), jax.ShapeDtypeStruct((T, D), F32),
                   jax.ShapeDtypeStruct((T, D), BF16)] + [jax.ShapeDtypeStruct((1, d), F32) for d in vecs],
        params=_params(("arbitrary",)))(dq, dkn, dv, dkr, cq_raw, ckv_raw, h2, dres, *tables, *wholes)


def _ffn_gup(name, dg, dv, hf):
    def body(dg_ref, dv_ref, hf_ref, o_ref):
        j = pl.program_id(0)

        @pl.when(j < N_FF_BLK)
        def _():
            o_ref[...] = lax.dot_general(dg_ref[...], hf_ref[...], TN, preferred_element_type=F32).astype(BF16)

        @pl.when(j >= N_FF_BLK)
        def _():
            o_ref[...] = lax.dot_general(dv_ref[...], hf_ref[...], TN, preferred_element_type=F32).astype(BF16)

    return _pallas(
        body, name=name, grid=(N_DEV,),
        in_specs=[pl.BlockSpec((None, T, FF_BLK), lambda j: (jnp.minimum(j, N_FF_BLK - 1), 0, 0)),
                  pl.BlockSpec((None, T, FF_BLK), lambda j: (jnp.maximum(j - N_FF_BLK, 0), 0, 0)),
                  pl.BlockSpec((T, D), lambda j: (0, 0))],
        out_specs=pl.BlockSpec((None, FF_BLK, D), lambda j: (j, 0, 0)),
        out_shape=jax.ShapeDtypeStruct((N_DEV, FF_BLK, D), BF16), params=_params(("parallel",)))(dg, dv, hf)


def _ffn_layer_fwd(tag, h, gain, ex):
    hf = _rms_fwd(f"{tag}_norm", h, gain)
    g, v, act = _ffn_up_act(f"{tag}_up", hf, ex.need(f"ffn_w_up{tag[1]}", hf), ex.need(f"ffn_cw{tag[1]}", hf),
                            ex.need(f"ffn_cb{tag[1]}", hf))
    ex.at(f"{tag}_up", act)
    rows = pl.BlockSpec((TS, D), lambda i: (i, 0))
    out = _mm_sum(f"{tag}_down",
                  [(act, pl.BlockSpec((N_FF_BLK, TS, FF_BLK), lambda i: (0, i, 0)), ex.need(f"ffn_w_down{tag[1]}", act),
                    pl.BlockSpec((None, N_FF_BLK, FF_BLK, D), lambda i: (0, 0, 0, 0)), NN, 0)],
                  grid=(T // TS,), o_spec=rows, o_shape=(T, D), o_dtype=F32, add=h)
    ex.at(f"{tag}_down", out)
    return out, (hf, g, v, act)


def _ffn_layer_bwd(tag, h, gain, ex, saved, dh, dh_bf):
    hf, g, v, act = saved
    layer = tag[1]
    w_up, w_down4 = ex.need(f"ffn_w_up{layer}", dh_bf), ex.need(f"ffn_w_down{layer}", dh_bf)
    dg, dv, dcw, dcb = _ffn_dact(f"{tag}_dact", dh_bf, w_down4, g, v, ex.need(f"ffn_cw{layer}", dh_bf),
                                 ex.need(f"ffn_cb{layer}", dh_bf))
    ex.at(f"{tag}_dact", dg)
    g_down = _mm(f"{tag}_gdown", act, dh_bf, grid=(N_FF_BLK,),
                 a_spec=pl.BlockSpec((None, T, FF_BLK), lambda j: (j, 0, 0)),
                 b_spec=pl.BlockSpec((T, D), lambda j: (0, 0)),
                 o_spec=pl.BlockSpec((FF_BLK, D), lambda j: (j, 0)),
                 o_shape=(D_FF, D), o_dtype=BF16, dims=TN)
    g_up = _ffn_gup(f"{tag}_gup", dg, dv, hf)
    ex.grad("ffn_w_up", int(layer), g_up.reshape(1, N_DEV, FF_BLK, D))
    ex.grad("ffn_w_down", int(layer), g_down.reshape(1, N_DEV, D_FF // N_DEV, D))
    ex.at(f"{tag}_gup", g_up)
    part = pl.BlockSpec((N_FF_BLK, TR, FF_BLK), lambda i: (0, i, 0))
    dh_in, dh_in_bf, dgain = _mm_sum(
        f"{tag}_dhf",
        [(dg, part, w_up, pl.BlockSpec((None, N_FF_BLK, FF_BLK, D), lambda i: (0, 0, 0, 0)), NN, 0),
         (dv, part, w_up, pl.BlockSpec((None, N_FF_BLK, FF_BLK, D), lambda i: (0, 1, 0, 0)), NN, 0)],
        grid=(T // TR,), o_spec=pl.BlockSpec((TR, D), lambda i: (i, 0)), o_shape=(T, D), o_dtype=F32,
        norm_bwd=(h, [gain], dh))
    ex.at(f"{tag}_dhf", dh_in)
    return dh_in, dh_in_bf, dgain[0], dcw, dcb


def _local_step(x, pos, tgt, rep, ex):
    attn_norm, ffn_norm, final_norm = rep["attn_norm"], rep["ffn_norm"], rep["final_norm"]
    half = QK_ROPE // 2
    inv = 1.0 / (ROPE_THETA ** (jnp.arange(half, dtype=F32) / half))
    inv_freq = jnp.concatenate([inv, inv, jnp.zeros((128 - 2 * half,), F32)]).reshape(1, 128)
    tables = _rope_tables(pos, inv_freq)

    hn0 = _rms_fwd("l0_norm", x, attn_norm[0:1])
    ex.at("l0_norm", hn0)
    w_in = ex.need("sc_w_in", hn0)
    zb, zc, zu, y = _mixer_in(hn0, w_in, ex.need("sc_conv_w", hn0))
    ex.at("l0_in", y)
    h1 = _mm_rows("l0_out", y, ex.need("sc_w_out", y), NN, F32, D, tn=512, add=x)
    ex.at("l0_out", h1)
    h2, ffn0 = _ffn_layer_fwd("f0", h1, ffn_norm[0:1], ex)

    w_uq = ex.need("w_uq", h2)
    hk, hn1, ckv_raw, ckv, kr, kn, vv, cq_raw, cq, q = _attn_pre(
        h2, rep["kv_in_norm"], attn_norm[1:2], rep["kv_latent_norm"], rep["q_latent_norm"], ex.need("w_dkv", h2),
        ex.need("w_kr", h2), ex.need("w_uk", h2), ex.need("w_uv", h2), ex.need("w_dq", h2), w_uq, tables)

    o, lse = _attn_fwd(q, kn, kr, vv)
    ex.at("attn_fwd", o)
    w_o = ex.need("w_o", o)
    h3 = _mm_rows("attn_out", o, w_o, NN, F32, D, tn=512, add=h2)
    h4, ffn1 = _ffn_layer_fwd("f1", h3, ffn_norm[1:2], ex)

    loss, dh4, dh4_bf, d_final = _final(h4, final_norm.reshape(1, D), tgt)

    dh3, dh3_bf, d_fn1, dcw1, dcb1 = _ffn_layer_bwd("f1", h3, ffn_norm[1:2], ex, ffn1, dh4, dh4_bf)
    ex.at("f1_bwd", dh3)

    do = _mm_rows("d_attn_out", dh3_bf, w_o, NT, BF16, N_HEADS * V_HEAD)
    dq_pre, dkn, dkr, dvv = _attn_bwd(q, kn, kr, vv, o, do, lse, tables)

    dcq_raw_bf, dckv_raw_bf, dkr_raw_bf, dh2, dh2_bf, d_qln, d_kvln, d_an1, d_kvin = _attn_post(
        dq_pre, dkn, dvv, dkr, cq_raw, ckv_raw, h2, dh3, rep["q_latent_norm"], rep["kv_latent_norm"], attn_norm[1:2],
        rep["kv_in_norm"], w_uq, ex.need("w_uk", dkn), ex.need("w_uv", dvv), ex.need("w_dq", dq_pre),
        ex.need("w_dkv", dkn), ex.need("w_kr", dkr), tables)
    g_uq, g_dq, g_o = _wgrads("g_q", [(dq_pre, cq), (hn1, dcq_raw_bf), (o, dh3_bf)])
    ex.grad("w_uq", None, g_uq[:, :QK_NOPE + QK_ROPE].reshape(1, N_DEV, QK_NOPE + QK_ROPE, Q_LORA))
    ex.grad("w_dq", None, g_dq.reshape(1, N_DEV, D // N_DEV, Q_LORA))
    ex.grad("w_o", None, g_o.reshape(1, N_DEV, D // N_DEV, D))

    g_uk, g_uv, g_dkv, g_kr = _wgrads("g_kv", [(ckv, dkn), (ckv, dvv), (hk, dckv_raw_bf), (dkr_raw_bf, hk)])
    ex.grad("w_uk", None, g_uk)
    ex.grad("w_uv", None, g_uv)
    ex.grad("w_dkv", None, g_dkv.reshape(1, N_DEV, D // N_DEV, KV_LORA))
    ex.grad("w_kr", None, g_kr[:QK_ROPE])
    ex.at("kv_bwd", dh2)

    dh1, dh1_bf, d_fn0, dcw0, dcb0 = _ffn_layer_bwd("f0", h1, ffn_norm[0:1], ex, ffn0, dh2, dh2_bf)
    ex.at("f0_bwd", dh1)

    ex.grad("sc_w_out", None, _mm_wgrad("g_sc_w_out", y, dh1_bf).reshape(1, N_DEV, D // N_DEV, D))
    dz, d_scw = _mixer_out_bwd(dh1_bf, ex.need("sc_w_out", dh1_bf), zb, zc, zu, ex.need("sc_conv_w", dh1_bf))
    g_in = _mm_wgrad("g_sc_w_in", hn0, dz)
    ex.grad("sc_w_in", None, g_in)
    ex.at("sc_bwd", g_in)
    ex.at("d_l0_in", g_in)
    w_in = ex.need("sc_w_in", dz)
    grad_x, _, (d_an0,) = _mm_sum(
        "d_l0_in", [(dz[None], pl.BlockSpec((1, TS, dz.shape[1]), lambda i: (0, i, 0)),
                     w_in[None], pl.BlockSpec((1,) + w_in.shape, lambda i: (0, 0, 0)), NT, 0)],
        norm_bwd=(x, [attn_norm[0:1]], dh1),
        grid=(T // TS,), o_spec=pl.BlockSpec((TS, D), lambda i: (i, 0)), o_shape=(T, D), o_dtype=F32)

    small = {
        "attn_norm": jnp.concatenate([d_an0, d_an1], axis=0),
        "ffn_norm": jnp.concatenate([d_fn0, d_fn1], axis=0),
        "final_norm": d_final.reshape(D),
        "kv_in_norm": d_kvin.reshape(D),
        "kv_latent_norm": d_kvln.reshape(KV_LORA),
        "q_latent_norm": d_qln,
        "ffn_conv_b": jnp.stack([dcb0, dcb1]).transpose(0, 2, 1, 3).reshape(2, D_FF),
        "sc_conv_w": d_scw,
        "ffn_conv_w": jnp.stack([dcw0, dcw1]).transpose(0, 2, 1, 3).reshape(2, 3, D_FF),
    }
    return loss, grad_x, small


def _place():
    return lax.axis_index("x"), lax.axis_index("y"), lax.axis_index("c")


def _peers():
    x, y, c = _place()
    return (x, y, 1 - c), [(1 - x, y), (x, 1 - y), (1 - x, 1 - y)]


def _window(ref, kind, dev):
    if kind == "blocked":
        return ref.at[:, dev]
    width = ref.shape[-1] // N_DEV
    return ref.at[:, pl.ds(pl.multiple_of(dev * width, 128), width)]


HBM_SPEC = pl.BlockSpec(memory_space=pltpu.HBM)
SEM_SPEC = pl.BlockSpec(memory_space=pltpu.SEMAPHORE)
EFFECT = pltpu.SideEffectType.DATAFLOW_SIDE_EFFECTING
TOKEN = jax.ShapeDtypeStruct((8, 128), F32)


def _hbm(a):
    return pltpu.with_memory_space_constraint(a, pltpu.HBM)


def _copies_start(name, jobs):
    nj = len(jobs)
    counts = [(len(srcs), len(lands)) for srcs, lands, _, _ in jobs]
    n_arr = sum(ns + nl for ns, nl in counts)

    def body(*refs):
        sems, token = refs[n_arr:n_arr + 2 * nj], refs[-1]
        at = 0
        for j, ((ns, nl), (_, _, ncopy, plan)) in enumerate(zip(counts, jobs)):
            copies = plan(refs[at:at + ns], refs[at + ns:at + ns + nl])
            assert len(copies) == ncopy
            for k, (sent, dst, to, _) in enumerate(copies):
                pltpu.make_async_remote_copy(src_ref=sent, dst_ref=dst, send_sem=sems[2 * j].at[k],
                                             recv_sem=sems[2 * j + 1].at[k], device_id=to, device_id_type=MESH).start()
            at += ns + nl
        token[...] = jnp.zeros_like(token)

    arrays = [a for srcs, lands, _, _ in jobs for a in list(srcs) + list(lands)]
    sem_shapes = [pltpu.SemaphoreType.DMA((ncopy,)) for _, _, ncopy, _ in jobs for _ in range(2)]
    outs = pl.pallas_call(
        body, name=name, in_specs=[HBM_SPEC] * n_arr,
        out_specs=[SEM_SPEC] * (2 * nj) + [HBM_SPEC] * n_arr + [VMEM_SPEC],
        out_shape=sem_shapes + [pltpu.HBM(a.shape, a.dtype) for a in arrays] + [TOKEN],
        input_output_aliases={i: 2 * nj + i for i in range(n_arr)},
        compiler_params=pltpu.CompilerParams(has_side_effects=EFFECT))(*[_hbm(a) for a in arrays])
    _Chain.last = outs[-1]
    flights, at = [], 2 * nj
    for j, (ns, nl) in enumerate(counts):
        flights.append((outs[2 * j], outs[2 * j + 1], list(outs[at:at + ns]), list(outs[at + ns:at + ns + nl])))
        at += ns + nl
    return flights


def _copies_wait(name, started, ncopy, plan):
    send, recv, srcs, lands = started
    ns, nl = len(srcs), len(lands)

    def body(*refs):
        send_ref, recv_ref, token = refs[ns + nl], refs[ns + nl + 1], refs[-1]
        copies = plan(refs[:ns], refs[ns:ns + nl])
        assert len(copies) == ncopy
        for k, (sent, _, to, landed) in enumerate(copies):
            cp = pltpu.make_async_remote_copy(src_ref=sent, dst_ref=landed, send_sem=send_ref.at[k],
                                              recv_sem=recv_ref.at[k], device_id=to, device_id_type=MESH)
            cp.wait_send()
            cp.wait_recv()
        token[...] = jnp.zeros_like(token)

    arrays = list(srcs) + list(lands)
    outs = pl.pallas_call(
        body, name=name, in_specs=[HBM_SPEC] * (ns + nl) + [SEM_SPEC] * 2 + [ANY_SPEC],
        out_specs=[HBM_SPEC] * (ns + nl) + [VMEM_SPEC], out_shape=[pltpu.HBM(a.shape, a.dtype) for a in arrays] + [TOKEN],
        input_output_aliases={i: i for i in range(ns + nl)},
        compiler_params=pltpu.CompilerParams(has_side_effects=EFFECT))(*arrays, send, recv, _Chain.last)
    _Chain.last = outs[-1]
    return list(outs[:ns]), list(outs[ns:-1])


def _plan_gather_chips(kinds):
    def plan(srcs, lands):
        x, y, c = _place()
        sibling, chips = _peers()
        out = []
        for t, kind in enumerate(kinds):
            mine = _window(lands[t], kind, 4 * x + 2 * y + c)
            out.append((srcs[t], mine, (x, y, c), mine))
            out.append((srcs[t], mine, sibling, _window(lands[t], kind, 4 * x + 2 * y + 1 - c)))
            for px, py in chips:
                out.append((srcs[t], mine, (px, py, c), _window(lands[t], kind, 4 * px + 2 * py + c)))
        return out
    return plan, 5 * len(kinds)


def _plan_gather_all(n):
    def plan(srcs, lands):
        x, y, c = _place()
        out = []
        for t in range(n):
            mine = lands[t].at[:, 4 * x + 2 * y + c]
            for m in range(N_DEV):
                px, py, pc = (1 - x if m & 4 else x), (1 - y if m & 2 else y), (1 - c if m & 1 else c)
                out.append((srcs[t], mine, (px, py, pc), lands[t].at[:, 4 * px + 2 * py + pc]))
        return out
    return plan, N_DEV * n


def _plan_gather_sibling(kinds):
    def plan(srcs, lands):
        _, _, c = _place()
        sibling, chips = _peers()
        out = []
        for t, kind in enumerate(kinds):
            for px, py in chips:
                w = _window(lands[t], kind, 4 * px + 2 * py + c)
                out.append((w, w, sibling, _window(lands[t], kind, 4 * px + 2 * py + 1 - c)))
        return out
    return plan, 3 * len(kinds)


def _plan_scatter_sibling(kinds):
    def plan(srcs, lands):
        _, _, c = _place()
        sibling, _ = _peers()
        out = []
        for t, kind in enumerate(kinds):
            for k in range(N_CHIP):
                out.append((_window(srcs[t], kind, 2 * k + 1 - c), lands[t].at[k], sibling, lands[t].at[k]))
        return out
    return plan, N_CHIP * len(kinds)


def _plan_scatter_chips(n):
    def plan(srcs, lands):
        x, y, c = _place()
        _, chips = _peers()
        out = []
        for t in range(n):
            for px, py in chips:
                out.append((srcs[t].at[2 * px + py], lands[t].at[2 * x + y], (px, py, c), lands[t].at[2 * px + py]))
        return out
    return plan, 3 * n


def _landing(shard, kind):
    if kind == "blocked":
        return lax.empty((shard.shape[0], N_DEV) + shard.shape[1:], shard.dtype)
    return lax.empty((shard.shape[0], N_DEV * shard.shape[1]), shard.dtype)


def _chip_sums(name, grads, kinds, recvs, c):
    n = len(grads)
    in_specs, out_specs, out_shape, args = [], [], [], []
    for gr, kind, rv in zip(grads, kinds, recvs):
        if kind == "blocked":
            rows, w = gr.shape[2], gr.shape[3]
            in_specs.append(pl.BlockSpec((None, None, rows, w), lambda k, cref: (0, 2 * k + cref[0], 0, 0)))
        else:
            rows, w = gr.shape[0], gr.shape[1] // N_DEV
            in_specs.append(pl.BlockSpec((rows, w), lambda k, cref: (0, 2 * k + cref[0])))
        blk = pl.BlockSpec((None, rows, w), lambda k, cref: (k, 0, 0))
        in_specs.append(blk)
        out_specs.append(blk)
        out_shape.append(jax.ShapeDtypeStruct((N_CHIP, rows, w), BF16))
        args += [gr, rv.reshape(N_CHIP, rows, w)]

    def body(*refs):
        for t in range(n):
            g_ref, r_ref, o_ref = refs[1 + 2 * t], refs[2 + 2 * t], refs[1 + 2 * n + t]
            o_ref[...] = (g_ref[...].astype(F32) + r_ref[...].astype(F32)).astype(BF16)

    return _pallas(body, name=name, n_prefetch=1, grid=(N_CHIP,), in_specs=in_specs, out_specs=out_specs,
                   out_shape=out_shape, params=_params(("parallel",)))(c, *args)


def _adamw_math(g, wv, mv, vv):
    m = ADAM_B1 * mv + (1.0 - ADAM_B1) * g
    v = ADAM_B2 * vv + (1.0 - ADAM_B2) * (g * g)
    m_hat = m / (1.0 - ADAM_B1 ** ADAM_STEP)
    v_hat = v / (1.0 - ADAM_B2 ** ADAM_STEP)
    delta = -ADAM_LR * (m_hat / (jnp.sqrt(v_hat) + ADAM_EPS) + ADAM_WD * wv)
    return delta, m, v


ADAM_STEPS = 2


def _adamw_group(name, items, chip_ids):
    n = len(items)
    in_specs, out_specs, out_shape, args, prevs = [], [], [], [chip_ids], []
    for own, recv, w3, m3, v3, layer, _ in items:
        nl, rows, w = w3.shape
        tr = rows // ADAM_STEPS
        assert tr % 16 == 0, (name, rows)
        in_specs += [pl.BlockSpec((None, tr, w), lambda i, ids, slot=slot: (ids[slot], i, 0)) for slot in range(4)]
        slab = pl.BlockSpec((None, tr, w), lambda i, ids, layer=layer: (layer, i, 0))
        in_specs += [slab] * 3
        out_specs += [slab] * 4
        out_shape += [jax.ShapeDtypeStruct((nl, rows, w), F32)] * 4
        args += [own, recv, recv, recv, w3, m3, v3]
    aliases = {}
    for t, item in enumerate(items):
        if item[6] is not None:
            for k in range(4):
                aliases[len(args) + k] = 4 * t + k
            in_specs += [ANY_SPEC] * 4
            args += list(item[6])
            prevs.append(t)
    n_in = 1 + 7 * n + 4 * len(prevs)

    def body(*refs):
        for t in range(n):
            own_ref, r1_ref, r2_ref, r3_ref, w_ref, m_ref, v_ref = refs[1 + 7 * t:8 + 7 * t]
            g_ref, d_ref, nm_ref, nv_ref = refs[n_in + 4 * t:n_in + 4 * t + 4]
            g = ((own_ref[...].astype(F32) + r1_ref[...].astype(F32)) + r2_ref[...].astype(F32)) + r3_ref[...].astype(F32)
            g_ref[...] = g
            d_ref[...], nm_ref[...], nv_ref[...] = _adamw_math(g, w_ref[...], m_ref[...], v_ref[...])

    outs = _pallas(body, name=name, n_prefetch=1, grid=(ADAM_STEPS,), in_specs=in_specs, out_specs=out_specs,
                   out_shape=out_shape, aliases=aliases, params=_params(("parallel",)))(*args)
    return [list(outs[4 * t:4 * t + 4]) for t in range(n)]


SHARD_ROWS = 8


def _adamw_small(gathered, ws, ms, vs, me):
    n = len(gathered)
    full = [w is not None for w in ws]
    sharded = [w is not None and w.ndim == 3 for w in ws]
    args = list(gathered)
    out_shape = []
    for t in range(n):
        shape = jax.ShapeDtypeStruct(ws[t].shape if sharded[t] else gathered[t].shape[2:], F32)
        if full[t]:
            args += [ws[t], ms[t], vs[t]]
            out_shape += [shape] * 4
        else:
            out_shape += [shape]

    def body(*refs):
        i_in, i_out = n, len(args) + 1
        me_ref = refs[len(args)]
        for t in range(n):
            p_ref = refs[t]
            if sharded[t]:
                w_ref, m_ref, v_ref = refs[i_in:i_in + 3]
                taps, layers, _ = w_ref.shape
                mine = pl.ds(pl.multiple_of(me_ref[0] * SHARD_ROWS, SHARD_ROWS), SHARD_ROWS)
                g = p_ref[0, 0, mine, :]
                for k in range(1, N_DEV):
                    g = g + p_ref[0, k, mine, :]
                for l in range(layers):
                    for k in range(taps):
                        at = (k, slice(l, l + 1), slice(None))
                        row = g[l * taps + k:l * taps + k + 1]
                        refs[i_out][at] = row
                        refs[i_out + 1][at], refs[i_out + 2][at], refs[i_out + 3][at] = _adamw_math(
                            row, w_ref[at], m_ref[at], v_ref[at])
                i_in += 3
                i_out += 4
                continue
            g = p_ref[0, 0]
            for k in range(1, N_DEV):
                g = g + p_ref[0, k]
            refs[i_out][...] = g
            if full[t]:
                w_ref, m_ref, v_ref = refs[i_in:i_in + 3]
                refs[i_out + 1][...], refs[i_out + 2][...], refs[i_out + 3][...] = _adamw_math(
                    g, w_ref[...], m_ref[...], v_ref[...])
                i_in += 3
                i_out += 4
            else:
                i_out += 1

    outs = _pallas(body, name="adamw_small",
                   in_specs=[VMEM_SPEC] * len(args) + [pl.BlockSpec(memory_space=pltpu.SMEM)],
                   out_specs=[VMEM_SPEC] * len(out_shape), out_shape=out_shape,
                   params=pltpu.CompilerParams(vmem_limit_bytes=VMEM_LIMIT))(*args, me)
    result, i = [], 0
    for t in range(n):
        k = 4 if full[t] else 1
        result.append(list(outs[i:i + k]))
        i += k
    return result


KIND = {"sc_w_in": "cols", "sc_w_out": "blocked", "w_dkv": "blocked", "w_kr": "cols", "w_uk": "cols", "w_uv": "cols",
        "w_dq": "blocked", "w_uq": "blocked", "w_o": "blocked", "ffn_w_up": "blocked", "ffn_w_down": "blocked",
        "conv": "blocked"}
GATHER_GROUPS = (("mixer", ("sc_w_in", "sc_w_out", "conv")),
                 ("up0", ("ffn_w_up0",)),
                 ("down0", ("ffn_w_down0",)),
                 ("attn", ("w_dkv", "w_kr", "w_uk", "w_uv", "w_dq", "w_uq", "w_o")),
                 ("ffn1", ("ffn_w_up1", "ffn_w_down1")))
SCATTER_GROUPS = (("ffn1", (("ffn_w_up", 1), ("ffn_w_down", 1))),
                  ("attn", (("w_o", None), ("w_uq", None), ("w_dq", None), ("w_uk", None), ("w_uv", None),
                            ("w_dkv", None), ("w_kr", None))),
                  ("ffn0", (("ffn_w_up", 0), ("ffn_w_down", 0))),
                  ("mixer", (("sc_w_out", None), ("sc_w_in", None))))
SCHEDULE = {
    "begin": (("gather_start", "mixer"),),
    "l0_norm": (("gather_forward", "mixer"), ("gather_start", "up0")),
    "l0_out": (("gather_forward", "up0"), ("gather_start", "down0")),
    "f0_up": (("gather_forward", "down0"), ("gather_start", "attn")),
    "f0_down": (("gather_forward", "attn"), ("gather_start", "ffn1")),
    "attn_fwd": (("gather_forward", "ffn1"),),
    "f1_gup": (("scatter_sibling", "ffn1"),),
    "f1_dhf": (("scatter_chips", "ffn1"),),
    "kv_bwd": (("scatter_sibling", "attn"), ("scatter_done", "ffn1")),
    "f0_dact": (("scatter_chips", "attn"),),
    "f0_gup": (("scatter_sibling", "ffn0"),),
    "f0_dhf": (("scatter_chips", "ffn0"),),
    "f0_bwd": (("scatter_done", "attn"),),
    "sc_bwd": (("scatter_sibling", "mixer"),),
    "d_l0_in": (("scatter_chips", "mixer"),),
}
FINISH = (("scatter_done", "ffn0"), ("scatter_done", "mixer"))
STAGES = {"gather_start": 1, "gather_forward": 2, "gather_done": 3,
          "scatter_sibling": 1, "scatter_chips": 2, "scatter_done": 3}
SMALL_W_ROWS = 24


def _pack(arrays, rows):
    flat = jnp.concatenate([a.reshape(-1).astype(F32) for a in arrays])
    return jnp.pad(flat, (0, rows * 128 - flat.shape[0])).reshape(rows, 128)


def _cast_shards(items):
    arrays = []
    for a, _, _ in items:
        if not any(a is b for b in arrays):
            arrays.append(a)
    slot = [next(i for i, b in enumerate(arrays) if b is a) for a, _, _ in items]

    def body(*refs):
        for t, (a, layer, rows) in enumerate(items):
            w_ref, o_ref = refs[slot[t]], refs[len(arrays) + t]
            r, c = a.shape[-2:]
            if a.ndim == 3:
                o_ref[:, :r] = w_ref[(layer or 0):(layer or 0) + 1].astype(BF16)
                if rows > r:
                    o_ref[:, r:] = jnp.zeros((1, rows - r, c), BF16)
            else:
                o_ref[:r] = w_ref[...].astype(BF16)
                if rows > r:
                    o_ref[r:] = jnp.zeros((rows - r, c), BF16)

    out_shape = [jax.ShapeDtypeStruct(((1,) if a.ndim == 3 else ()) + (rows, a.shape[-1]), BF16)
                 for a, _, rows in items]
    return _pallas(body, name="cast_shards", in_specs=[VMEM_SPEC] * len(arrays), out_specs=[VMEM_SPEC] * len(items),
                   out_shape=out_shape, params=pltpu.CompilerParams(vmem_limit_bytes=VMEM_LIMIT))(*arrays)


STORED_TRANSPOSED = ("ffn_w_up", "w_uq", "w_kr")


def _stored(name, a):
    return jnp.swapaxes(a, -1, -2) if name in STORED_TRANSPOSED else a


def _base(name):
    if name.startswith("ffn_w_") and name[-1] in "01":
        return name[:-1], int(name[-1])
    return name, None


class _Exchange:
    def __init__(self, wts, mom, var, ffn_conv_b):
        self.wts, self.mom, self.var = wts, mom, var
        x, y, c = _place()
        self.c_arr = jnp.reshape(c, (1,)).astype(jnp.int32)
        chip = 2 * x + y
        self.chip_ids = jnp.stack([chip, chip ^ 1, chip ^ 2, chip ^ 3]).astype(jnp.int32)
        self.ready = {"ffn_cb0": ffn_conv_b.reshape(2, N_FF_BLK, 1, FF_BLK)[0],
                      "ffn_cb1": ffn_conv_b.reshape(2, N_FF_BLK, 1, FF_BLK)[1]}
        self.gathers, self.group_of = {}, {}
        self.grads, self.scatters, self.results, self.queue = {}, {}, {}, []
        for gname, names in GATHER_GROUPS:
            self.gathers[gname] = dict(stage=0, names=names, kinds=[KIND[_base(nm)[0]] for nm in names])
            for nm in names:
                self.group_of[nm] = gname
        for nm in ("sc_conv_w", "ffn_cw0", "ffn_cw1"):
            self.group_of[nm] = "mixer"
        self.cast, self.f32 = {}, {}
        self.at("begin", None)
        later = [nm for gname, names in GATHER_GROUPS[1:] for nm in names]
        self.cast = dict(zip(later, _cast_shards([self._shard_f32(nm) for nm in later])))

    def _shard_f32(self, name):
        base, layer = _base(name)
        if base not in self.f32:
            a = _stored(base, self.wts[base])
            self.f32[base] = a.reshape(a.shape[-2:]) if KIND[base] == "cols" else a.reshape((-1,) + a.shape[-2:])
        a = self.f32[base]
        return a, layer, {"w_kr": 128, "w_uq": QK_PAD}.get(base, a.shape[-2])

    def _shard(self, name):
        if name in self.cast:
            return self.cast[name]
        if name == "conv":
            return _pack([self.wts["sc_conv_w"], self.wts["ffn_conv_w"]], SMALL_W_ROWS).reshape(1, SMALL_W_ROWS, 128)
        base, layer = _base(name)
        a = _stored(base, self.wts[base])
        if layer is not None:
            a = a[layer:layer + 1]
        if KIND[base] == "cols":
            return a.reshape(a.shape[-2], a.shape[-1]).astype(BF16)
        return a.reshape((-1,) + a.shape[-2:]).astype(BF16)

    def _start(self, name, srcs, lands, ncopy, plan, st):
        self.queue.append((name, (srcs, lands, ncopy, plan), st))

    def _flush(self):
        if self.queue:
            flights = _copies_start("__".join(name for name, _, _ in self.queue), [job for _, job, _ in self.queue])
            for (_, _, st), flight in zip(self.queue, flights):
                st["flight"] = flight
            self.queue = []

    def _flight(self, st):
        self._flush()
        return st["flight"]

    def _gather_to(self, gname, stage, after):
        st = self.gathers[gname]
        if st["stage"] < 1 <= stage:
            shards = [self._shard(nm) for nm in st["names"]]
            lands = [_landing(s, kind) for s, kind in zip(shards, st["kinds"])]
            plan, ncopy = _plan_gather_chips(st["kinds"])
            self._start(f"ag_{gname}_chips", shards, lands, ncopy, plan, st)
            st["stage"] = 1
        if st["stage"] < 2 <= stage:
            plan, ncopy = _plan_gather_chips(st["kinds"])
            _, lands = _copies_wait(f"ag_{gname}_chips_wait", self._flight(st), ncopy, plan)
            plan, ncopy = _plan_gather_sibling(st["kinds"])
            self._start(f"ag_{gname}_sibling", [], lands, ncopy, plan, st)
            st["stage"] = 2
        if st["stage"] < 3 <= stage:
            plan, ncopy = _plan_gather_sibling(st["kinds"])
            _, lands = _copies_wait(f"ag_{gname}_sibling_wait", self._flight(st), ncopy, plan)
            for nm, land in zip(st["names"], lands):
                self._arrived(nm, land)
            st["stage"] = 3

    def _arrived(self, name, land):
        if name == "conv":
            conv = land.reshape(N_DEV, SMALL_W_ROWS * 128)
            self.ready["sc_conv_w"] = conv[:, :3 * 128].reshape(N_DEV, 3, 128).transpose(1, 0, 2).reshape(3, D)
            fcw = conv[:, 3 * 128:3 * 128 + 6 * 352].reshape(N_DEV, 2, 3, 352).transpose(1, 2, 0, 3)
            fcw = fcw.reshape(2, 3, N_FF_BLK, FF_BLK).transpose(0, 2, 1, 3)
            self.ready["ffn_cw0"], self.ready["ffn_cw1"] = fcw[0], fcw[1]
        elif name in ("sc_w_in", "w_uk", "w_uv", "w_kr") or name.startswith("ffn_w_up"):
            self.ready[name] = land
        elif name.startswith("ffn_w_down"):
            self.ready[name] = land.reshape(1, N_FF_BLK, FF_BLK, D)
        elif name == "w_uq":
            self.ready[name] = land.reshape(N_HEADS, QK_PAD, Q_LORA)
        else:
            self.ready[name] = land.reshape(D, land.shape[-1])

    def need(self, name, after):
        if name not in self.ready:
            self._gather_to(self.group_of[name], 3, after)
            self._flush()
        return self.ready[name]

    def grad(self, name, layer, array):
        self.grads[(name, layer)] = array

    def _scatter_to(self, gname, stage, after):
        keys = dict(SCATTER_GROUPS)[gname]
        st = self.scatters.setdefault(gname, dict(stage=0))
        kinds = [KIND[nm] for nm, _ in keys]
        if st["stage"] < 1 <= stage:
            grads = [self.grads[key] for key in keys]
            lands = []
            for gr, kind in zip(grads, kinds):
                shard = (gr.shape[0],) + gr.shape[2:] if kind == "blocked" else (gr.shape[0], gr.shape[1] // N_DEV)
                lands.append(lax.empty((N_CHIP,) + shard, BF16))
            plan, ncopy = _plan_scatter_sibling(kinds)
            self._start(f"rs_{gname}_sibling", grads, lands, ncopy, plan, st)
            st["stage"] = 1
        if st["stage"] < 2 <= stage:
            plan, ncopy = _plan_scatter_sibling(kinds)
            grads, recvs = _copies_wait(f"rs_{gname}_sibling_wait", self._flight(st), ncopy, plan)
            sums = _chip_sums(f"rs_{gname}_sums", grads, kinds, recvs, self.c_arr)
            lands = [lax.empty(s.shape, BF16) for s in sums]
            plan, ncopy = _plan_scatter_chips(len(sums))
            self._start(f"rs_{gname}_chips", sums, lands, ncopy, plan, st)
            st["stage"] = 2
        if st["stage"] < 3 <= stage:
            plan, ncopy = _plan_scatter_chips(len(keys))
            sums, recvs = _copies_wait(f"rs_{gname}_chips_wait", self._flight(st), ncopy, plan)
            items = []
            for (nm, layer), own, rv in zip(keys, sums, recvs):
                nl = 1 if layer is None else 2
                rows, w = own.shape[1], own.shape[2]
                w3, m3, v3 = (_stored(nm, src[nm]).reshape(nl, rows, w) for src in (self.wts, self.mom, self.var))
                items.append((own, rv, w3, m3, v3, 0 if layer is None else layer, self.results.get(nm)))
            outs = _adamw_group(f"adamw_{gname}", items, self.chip_ids)
            for (nm, _), out in zip(keys, outs):
                self.results[nm] = out
            st["stage"] = 3

    def at(self, place, after):
        for action, gname in SCHEDULE.get(place, ()):
            self._advance(action, gname, after)
        self._flush()

    def _advance(self, action, gname, after):
        if action.startswith("gather"):
            self._gather_to(gname, STAGES[action], after)
        else:
            self._scatter_to(gname, STAGES[action], after)

    def finish(self, after):
        for action, gname in FINISH:
            self._advance(action, gname, after)
        for gname, _ in SCATTER_GROUPS:
            self._scatter_to(gname, 3, after)
        return {nm: [_stored(nm, o.reshape(_stored(nm, self.wts[nm]).shape)) for o in outs]
                for nm, outs in self.results.items()}


REPLICATED = ("attn_norm", "ffn_norm", "final_norm", "kv_in_norm", "kv_latent_norm", "q_latent_norm", "ffn_conv_b")
WEIGHTS = ("attn_norm", "ffn_norm", "final_norm", "sc_w_in", "sc_conv_w", "sc_w_out", "kv_in_norm", "w_dkv",
           "kv_latent_norm", "w_kr", "w_uk", "w_uv", "w_dq", "q_latent_norm", "w_uq", "w_o", "ffn_w_up", "ffn_conv_w",
           "ffn_conv_b", "ffn_w_down")


def kernel(x, positions, attn_norm, ffn_norm, final_norm, sc_w_in, sc_conv_w, sc_w_out, kv_in_norm, w_dkv, kv_latent_norm, w_kr, w_uk, w_uv, w_dq, q_latent_norm, w_uq, w_o, ffn_w_up, ffn_conv_w, ffn_conv_b, ffn_w_down, loss_target, m_attn_norm, m_ffn_norm, m_final_norm, m_sc_w_in, m_sc_conv_w, m_sc_w_out, m_kv_in_norm, m_w_dkv, m_kv_latent_norm, m_w_kr, m_w_uk, m_w_uv, m_w_dq, m_q_latent_norm, m_w_uq, m_w_o, m_ffn_w_up, m_ffn_conv_w, m_ffn_conv_b, m_ffn_w_down, v_attn_norm, v_ffn_norm, v_final_norm, v_sc_w_in, v_sc_conv_w, v_sc_w_out, v_kv_in_norm, v_w_dkv, v_kv_latent_norm, v_w_kr, v_w_uk, v_w_uv, v_w_dq, v_q_latent_norm, v_w_uq, v_w_o, v_ffn_w_up, v_ffn_conv_w, v_ffn_conv_b, v_ffn_w_down):
    wts = dict(attn_norm=attn_norm, ffn_norm=ffn_norm, final_norm=final_norm, sc_w_in=sc_w_in, sc_conv_w=sc_conv_w,
               sc_w_out=sc_w_out, kv_in_norm=kv_in_norm, w_dkv=w_dkv, kv_latent_norm=kv_latent_norm, w_kr=w_kr,
               w_uk=w_uk, w_uv=w_uv, w_dq=w_dq, q_latent_norm=q_latent_norm, w_uq=w_uq, w_o=w_o, ffn_w_up=ffn_w_up,
               ffn_conv_w=ffn_conv_w, ffn_conv_b=ffn_conv_b, ffn_w_down=ffn_w_down)
    mom = dict(attn_norm=m_attn_norm, ffn_norm=m_ffn_norm, final_norm=m_final_norm, sc_w_in=m_sc_w_in,
               sc_conv_w=m_sc_conv_w, sc_w_out=m_sc_w_out, kv_in_norm=m_kv_in_norm, w_dkv=m_w_dkv,
               kv_latent_norm=m_kv_latent_norm, w_kr=m_w_kr, w_uk=m_w_uk, w_uv=m_w_uv, w_dq=m_w_dq,
               q_latent_norm=m_q_latent_norm, w_uq=m_w_uq, w_o=m_w_o, ffn_w_up=m_ffn_w_up, ffn_conv_w=m_ffn_conv_w,
               ffn_conv_b=m_ffn_conv_b, ffn_w_down=m_ffn_w_down)
    var = dict(attn_norm=v_attn_norm, ffn_norm=v_ffn_norm, final_norm=v_final_norm, sc_w_in=v_sc_w_in,
               sc_conv_w=v_sc_conv_w, sc_w_out=v_sc_w_out, kv_in_norm=v_kv_in_norm, w_dkv=v_w_dkv,
               kv_latent_norm=v_kv_latent_norm, w_kr=v_w_kr, w_uk=v_w_uk, w_uv=v_w_uv, w_dq=v_w_dq,
               q_latent_norm=v_q_latent_norm, w_uq=v_w_uq, w_o=v_w_o, ffn_w_up=v_ffn_w_up, ffn_conv_w=v_ffn_conv_w,
               ffn_conv_b=v_ffn_conv_b, ffn_w_down=v_ffn_w_down)
    xi, yi, ci = _place()
    me = 4 * xi + 2 * yi + ci
    _Chain.last = None

    ex = _Exchange(wts, mom, var, ffn_conv_b)
    rep = {
        "attn_norm": attn_norm, "ffn_norm": ffn_norm, "final_norm": final_norm,
        "kv_in_norm": kv_in_norm.reshape(1, D), "kv_latent_norm": kv_latent_norm.reshape(1, KV_LORA),
        "q_latent_norm": q_latent_norm.reshape(1, Q_LORA),
    }
    loss, grad_x, small = _local_step(x.reshape(T, D), positions.reshape(T, 1), loss_target.reshape(T, D), rep, ex)

    def rows_of(a):
        return a.reshape(-1, a.shape[-1])

    def device_rows(a):
        taps, c = a.shape[-2], a.shape[-1] // N_DEV
        rows = a.reshape(-1, taps, N_DEV, c).transpose(2, 0, 1, 3).reshape(N_DEV, -1, c)
        return jnp.pad(rows, ((0, 0), (0, SHARD_ROWS - rows.shape[1]), (0, 0))).reshape(N_DEV * SHARD_ROWS, c)

    def taps_first(a):
        return jnp.transpose(a, (1, 0, 2))

    sharded = ("sc_conv_w", "ffn_conv_w")
    shards = ([loss.reshape(1, 1, 128)] + [rows_of(small[nm])[None] for nm in REPLICATED]
              + [device_rows(small[nm])[None] for nm in sharded])
    plan, ncopy = _plan_gather_all(len(shards))
    flight, = _copies_start("ag_small", [(shards, [lax.empty((1, N_DEV) + s.shape[1:], F32) for s in shards], ncopy, plan)])
    results = ex.finish(grad_x)
    _, gathered = _copies_wait("ag_small_wait", flight, ncopy, plan)
    params = [[None] + [rows_of(src[nm]) for nm in REPLICATED] + [taps_first(src[nm]) for nm in sharded]
              for src in (wts, mom, var)]
    summed = _adamw_small(gathered, *params, me.astype(jnp.int32).reshape(1))
    loss_total = summed[0][0][0, 0]
    for nm, vals in zip(REPLICATED, summed[1:1 + len(REPLICATED)]):
        results[nm] = [a.reshape(wts[nm].shape) for a in vals]
    for nm, vals in zip(sharded, summed[1 + len(REPLICATED):]):
        results[nm] = [taps_first(a) for a in vals]

    outs = [loss_total, grad_x.reshape(1, T, D)]
    for slot in range(4):
        outs.extend(results[nm][slot] for nm in WEIGHTS)
    return tuple(outs)
```

```python
import jax
import jax.numpy as jnp
from jax import lax
from jax.experimental import pallas as pl
from jax.experimental.pallas import tpu as pltpu

F32 = jnp.float32
BF16 = jnp.bfloat16

T = 2048
D = 1024
N_HEADS = 8
QK_NOPE = 128
QK_ROPE = 64
V_HEAD = 128
Q_LORA = 384
KV_LORA = 256
D_FF = 2816
CHUNK = 64
ROPE_THETA = 10000.0
EPS = 1e-6
NEG_INF = -1e30
ADAM_LR = 0.001
ADAM_B1 = 0.9
ADAM_B2 = 0.999
ADAM_EPS = 1e-08
ADAM_WD = 0.01
ADAM_STEP = 10

N_DEV = 8
N_CHIP = 4
FF_BLK = D_FF * 2 // N_DEV
N_FF_BLK = D_FF // FF_BLK
QK_PAD = 256
HALO = 16

TM = 1024
TS = 512
TR = 256
TQ = 512
VMEM_LIMIT = 56 * 1024 * 1024

NN = (((1,), (0,)), ((), ()))
NT = (((1,), (1,)), ((), ()))
TN = (((0,), (0,)), ((), ()))
MESH = pl.DeviceIdType.MESH


def _params(sem):
    return pltpu.CompilerParams(dimension_semantics=sem, vmem_limit_bytes=VMEM_LIMIT)


ANY_SPEC = pl.BlockSpec(memory_space=pl.ANY)
VMEM_SPEC = pl.BlockSpec(memory_space=pltpu.VMEM)


class _Chain:
    last = None


def _pallas(body, *, name, in_specs, out_specs, out_shape, grid=(), scratch_shapes=(), n_prefetch=0, aliases=None,
            params=None):
    def run(*args):
        after = _Chain.last
        n_lead = len(args)
        specs, operands, fn = list(in_specs), list(args), body
        if after is not None:
            def fn(*refs):
                return body(*refs[:n_lead], *refs[n_lead + 1:])
            specs.append(ANY_SPEC)
            operands.append(after)
        kw = dict(name=name, out_shape=out_shape, input_output_aliases=aliases or {})
        if params is not None:
            kw["compiler_params"] = params
        if n_prefetch:
            kw["grid_spec"] = pltpu.PrefetchScalarGridSpec(
                num_scalar_prefetch=n_prefetch, grid=grid, in_specs=specs, out_specs=out_specs,
                scratch_shapes=scratch_shapes)
        else:
            kw.update(grid=grid, in_specs=specs, out_specs=out_specs, scratch_shapes=scratch_shapes)
        outs = pl.pallas_call(fn, **kw)(*operands)
        _Chain.last = outs[0] if isinstance(outs, (list, tuple)) else outs
        return outs
    return run


def _mm(name, a, b, *, grid, a_spec, b_spec, o_spec, o_shape, o_dtype, dims, k_axis=None, acc_shape=None,
        add=None, add_spec=None):
    nk = grid[k_axis] if k_axis is not None else 1
    has_add = add is not None

    def body(*refs):
        a_ref, b_ref = refs[0], refs[1]
        p = 2
        add_ref = None
        if has_add:
            add_ref = refs[p]
            p += 1
        o_ref = refs[p]
        p += 1
        r = lax.dot_general(a_ref[...].astype(BF16), b_ref[...].astype(BF16), dims, preferred_element_type=F32)
        if k_axis is None:
            if has_add:
                r = r + add_ref[...].astype(F32)
            o_ref[...] = r.astype(o_dtype)
        else:
            acc = refs[p]
            k = pl.program_id(k_axis)

            @pl.when(k == 0)
            def _():
                acc[...] = r

            @pl.when(k > 0)
            def _():
                acc[...] += r

            @pl.when(k == nk - 1)
            def _():
                t = acc[...]
                if has_add:
                    t = t + add_ref[...].astype(F32)
                o_ref[...] = t.astype(o_dtype)

    in_specs = [a_spec, b_spec]
    args = [a, b]
    if has_add:
        in_specs.append(add_spec if add_spec is not None else o_spec)
        args.append(add)
    sem = tuple("arbitrary" if ax == k_axis else "parallel" for ax in range(len(grid)))
    scratch = [pltpu.VMEM(acc_shape, F32)] if k_axis is not None else []
    return _pallas(body, name=name, grid=grid, in_specs=in_specs, out_specs=o_spec,
                   out_shape=jax.ShapeDtypeStruct(o_shape, o_dtype), scratch_shapes=scratch, params=_params(sem))(*args)


def _mm_sum(name, parts, *, grid, o_spec, o_shape, o_dtype, add=None, norm_bwd=None):
    has_add = add is not None
    np_ = len(parts)
    nn = 1 if norm_bwd is None else len(norm_bwd[1])
    has_res = norm_bwd is not None and norm_bwd[2] is not None

    def body(*refs):
        accs = [None] * nn
        for p, (_, _, _, _, dims, n) in enumerate(parts):
            a_ref, b_ref = refs[2 * p], refs[2 * p + 1]
            for k in range(a_ref.shape[0]):
                r = lax.dot_general(a_ref[k], b_ref[k], dims, preferred_element_type=F32)
                accs[n] = r if accs[n] is None else accs[n] + r
        if norm_bwd is None:
            acc = accs[0]
            if has_add:
                acc = acc + refs[2 * np_][...]
            refs[-1][...] = acc.astype(o_dtype)
            return
        x_ref, g_refs = refs[2 * np_], refs[2 * np_ + 1:2 * np_ + 1 + nn]
        dx_ref, dxb_ref, dg_refs = refs[-2 - nn], refs[-1 - nn], refs[-nn:]
        xv = x_ref[...]
        r = lax.rsqrt(jnp.mean(xv * xv, axis=-1, keepdims=True) + EPS)
        xn = xv * r
        dx = refs[2 * np_ + 1 + nn][...] if has_res else None
        sums = []
        for acc, g_ref in zip(accs, g_refs):
            gdy = acc * g_ref[...]
            t = r * (gdy - xn * jnp.mean(gdy * xn, axis=-1, keepdims=True))
            dx = t if dx is None else dx + t
            sums.append(jnp.sum(acc * xn, axis=0, keepdims=True))
        dx_ref[...] = dx
        dxb_ref[...] = dx.astype(BF16)

        @pl.when(pl.program_id(0) == 0)
        def _():
            for dg_ref, part in zip(dg_refs, sums):
                dg_ref[...] = part

        @pl.when(pl.program_id(0) > 0)
        def _():
            for dg_ref, part in zip(dg_refs, sums):
                dg_ref[...] += part

    in_specs, args = [], []
    for a, a_spec, b, b_spec, _, _ in parts:
        in_specs += [a_spec, b_spec]
        args += [a, b]
    if norm_bwd is None:
        if has_add:
            in_specs.append(o_spec)
            args.append(add)
        return _pallas(body, name=name, grid=grid, in_specs=in_specs, out_specs=o_spec,
                       out_shape=jax.ShapeDtypeStruct(o_shape, o_dtype),
                       params=_params(("parallel",) * len(grid)))(*args)
    x, gains, dres = norm_bwd
    vec = pl.BlockSpec((1, o_shape[1]), lambda i: (0, 0))
    in_specs += [o_spec] + [vec] * nn + ([o_spec] if has_res else [])
    args += [x] + list(gains) + ([dres] if has_res else [])
    outs = _pallas(body, name=name, grid=grid, in_specs=in_specs, out_specs=[o_spec, o_spec] + [vec] * nn,
                   out_shape=[jax.ShapeDtypeStruct(o_shape, F32), jax.ShapeDtypeStruct(o_shape, BF16)]
                   + [jax.ShapeDtypeStruct((1, o_shape[1]), F32)] * nn,
                   params=_params(("arbitrary",)))(*args)
    return outs[0], outs[1], list(outs[2:])


def _mm_rows(name, a, b, dims, o_dtype, n_out, *, tn=None, add=None):
    k = a.shape[1]
    tn = n_out if tn is None else tn
    if dims == NN:
        b_spec = pl.BlockSpec((k, tn), lambda n, i: (0, n))
    else:
        b_spec = pl.BlockSpec((tn, k), lambda n, i: (n, 0))
    return _mm(name, a, b, grid=(n_out // tn, T // TM),
               a_spec=pl.BlockSpec((TM, k), lambda n, i: (i, 0)), b_spec=b_spec,
               o_spec=pl.BlockSpec((TM, tn), lambda n, i: (i, n)), o_shape=(T, n_out), o_dtype=o_dtype,
               dims=dims, add=add)


def _wgrads(name, jobs):
    arrays, index = [], {}
    for a, b in jobs:
        for arr in (a, b):
            if id(arr) not in index:
                index[id(arr)] = len(arrays)
                arrays.append(arr)
    n_in = len(arrays)

    def body(*refs):
        for t, (a, b) in enumerate(jobs):
            a_ref, b_ref, o_ref = refs[index[id(a)]], refs[index[id(b)]], refs[n_in + t]
            if a.ndim == 3:
                for h in range(a.shape[0]):
                    o_ref[h] = lax.dot_general(a_ref[h], b_ref[...], TN, preferred_element_type=F32).astype(BF16)
            else:
                o_ref[...] = lax.dot_general(a_ref[...], b_ref[...], TN, preferred_element_type=F32).astype(BF16)

    out_shape = [jax.ShapeDtypeStruct(a.shape[:-2] + (a.shape[-1], b.shape[-1]), BF16) for a, b in jobs]
    return _pallas(body, name=name, in_specs=[VMEM_SPEC] * n_in, out_specs=[VMEM_SPEC] * len(jobs), out_shape=out_shape,
                   params=pltpu.CompilerParams(vmem_limit_bytes=VMEM_LIMIT))(*arrays)


def _mm_wgrad(name, a, b, *, tn=512):
    k, n = a.shape[1], b.shape[1]
    tn = min(tn, n)
    return _mm(name, a, b, grid=(n // tn,),
               a_spec=pl.BlockSpec((T, k), lambda j: (0, 0)), b_spec=pl.BlockSpec((T, tn), lambda j: (0, j)),
               o_spec=pl.BlockSpec((k, tn), lambda j: (0, j)), o_shape=(k, n), o_dtype=BF16, dims=TN)


def _rms_fwd(name, x, g):
    d = x.shape[1]

    def body(x_ref, g_ref, o_ref):
        xv = x_ref[...]
        r = lax.rsqrt(jnp.mean(xv * xv, axis=-1, keepdims=True) + EPS)
        o_ref[...] = ((xv * r) * g_ref[...]).astype(BF16)

    return _pallas(
        body, name=name, grid=(T // TM,),
        in_specs=[pl.BlockSpec((TM, d), lambda i: (i, 0)), pl.BlockSpec((1, d), lambda i: (0, 0))],
        out_specs=pl.BlockSpec((TM, d), lambda i: (i, 0)),
        out_shape=jax.ShapeDtypeStruct((T, d), BF16), params=_params(("parallel",)))(x, g)


def _rms(xv, g):
    return (xv * lax.rsqrt(jnp.mean(xv * xv, axis=-1, keepdims=True) + EPS)) * g


def _rms_bwd(name, x, gains, dys, dres=None):
    d = x.shape[1]
    n = len(gains)
    has_res = dres is not None

    def body(*refs):
        x_ref, g_refs, dy_refs = refs[0], refs[1:1 + n], refs[1 + n:1 + 2 * n]
        dx_ref, dxb_ref = refs[-2 - n], refs[-1 - n]
        dg_refs = refs[-n:]
        xv = x_ref[...]
        r = lax.rsqrt(jnp.mean(xv * xv, axis=-1, keepdims=True) + EPS)
        xn = xv * r
        dx = refs[1 + 2 * n][...] if has_res else None
        parts = []
        for g_ref, dy_ref in zip(g_refs, dy_refs):
            dyv = dy_ref[...].astype(F32)
            gdy = dyv * g_ref[...]
            t = r * (gdy - xn * jnp.mean(gdy * xn, axis=-1, keepdims=True))
            dx = t if dx is None else dx + t
            parts.append(jnp.sum(dyv * xn, axis=0, keepdims=True))
        dx_ref[...] = dx
        dxb_ref[...] = dx.astype(BF16)

        @pl.when(pl.program_id(0) == 0)
        def _():
            for dg_ref, part in zip(dg_refs, parts):
                dg_ref[...] = part

        @pl.when(pl.program_id(0) > 0)
        def _():
            for dg_ref, part in zip(dg_refs, parts):
                dg_ref[...] += part

    row = pl.BlockSpec((TR, d), lambda i: (i, 0))
    vec = pl.BlockSpec((1, d), lambda i: (0, 0))
    args = [x] + list(gains) + list(dys) + ([dres] if has_res else [])
    in_specs = [row] + [vec] * n + [row] * n + ([row] if has_res else [])
    outs = _pallas(
        body, name=name, grid=(T // TR,), in_specs=in_specs, out_specs=[row, row] + [vec] * n,
        out_shape=[jax.ShapeDtypeStruct((T, d), F32), jax.ShapeDtypeStruct((T, d), BF16)]
        + [jax.ShapeDtypeStruct((1, d), F32)] * n,
        params=_params(("arbitrary",)))(*args)
    return outs[0], outs[1], list(outs[2:])


def _final(h, g, tgt):
    def body(h_ref, g_ref, t_ref, loss_ref, dh_ref, dhb_ref, dg_ref):
        hv = h_ref[...]
        r = lax.rsqrt(jnp.mean(hv * hv, axis=-1, keepdims=True) + EPS)
        xn = hv * r
        gv = g_ref[...]
        err = xn * gv - t_ref[...]
        part_loss = 0.5 * jnp.sum(jnp.mean(err * err, axis=-1, keepdims=True), axis=0, keepdims=True)
        dy = err * (1.0 / D)
        gdy = dy * gv
        dh = r * (gdy - xn * jnp.mean(gdy * xn, axis=-1, keepdims=True))
        dh_ref[...] = dh
        dhb_ref[...] = dh.astype(BF16)
        part = jnp.sum(dy * xn, axis=0, keepdims=True)
        first = pl.program_id(0) == 0

        @pl.when(first)
        def _():
            dg_ref[...] = part
            loss_ref[...] = jnp.broadcast_to(part_loss, (1, 128))

        @pl.when(jnp.logical_not(first))
        def _():
            dg_ref[...] += part
            loss_ref[...] += jnp.broadcast_to(part_loss, (1, 128))

    row = pl.BlockSpec((TR, D), lambda i: (i, 0))
    vec = pl.BlockSpec((1, D), lambda i: (0, 0))
    return _pallas(
        body, name="final_loss", grid=(T // TR,), in_specs=[row, vec, row],
        out_specs=[pl.BlockSpec((1, 128), lambda i: (0, 0)), row, row, vec],
        out_shape=[jax.ShapeDtypeStruct((1, 128), F32), jax.ShapeDtypeStruct((T, D), F32),
                   jax.ShapeDtypeStruct((T, D), BF16), jax.ShapeDtypeStruct((1, D), F32)],
        params=_params(("arbitrary",)))(h, g, tgt)


def _prev_idx(i, rows=TR):
    return jnp.maximum(i * (rows // HALO) - 1, 0)


def _next_idx(i, rows=TR):
    return jnp.minimum((i + 1) * (rows // HALO), T // HALO - 1)


def _causal_taps(ext):
    return pltpu.roll(ext, 2, 0)[HALO:], pltpu.roll(ext, 1, 0)[HALO:], ext[HALO:]


def _anticausal_taps(ext, n):
    rows = ext.shape[0]
    return pltpu.roll(ext, rows - 1, 0)[:n], pltpu.roll(ext, rows - 2, 0)[:n]


MIX_COLS = 512


def _mixer_in(hn, w_in, w):
    nc = D // MIX_COLS

    def body(h_ref, hh_ref, wb_ref, wc_ref, wu_ref, w_ref, b_ref, c_ref, u_ref, y_ref):
        i = pl.program_id(1)
        hv = h_ref[...]
        he = jnp.concatenate([hh_ref[...], hv], axis=0)
        ce = lax.dot_general(he, wc_ref[...], NN, preferred_element_type=F32).astype(BF16)
        ue = lax.dot_general(he, wu_ref[...], NN, preferred_element_type=F32).astype(BF16)
        bv = lax.dot_general(hv, wb_ref[...], NN, preferred_element_type=F32).astype(BF16)
        b_ref[...] = bv
        c_ref[...] = ce[HALO:]
        u_ref[...] = ue[HALO:]
        row = lax.broadcasted_iota(jnp.int32, (HALO + TS, 1), 0)
        cu = jnp.where(jnp.logical_or(i > 0, row >= HALO), ce.astype(F32) * ue.astype(F32), 0.0)
        x2, x1, x0 = _causal_taps(cu)
        wv = w_ref[...]
        cv = (x2 * wv[0:1] + x1 * wv[1:2]) + x0 * wv[2:3]
        y_ref[...] = (bv.astype(F32) * cv).astype(BF16)

    def cols(part):
        return pl.BlockSpec((D, MIX_COLS), lambda j, i: (0, part * nc + j))

    blk = pl.BlockSpec((TS, MIX_COLS), lambda j, i: (i, j))
    out = jax.ShapeDtypeStruct((T, D), BF16)
    return _pallas(
        body, name="l0_in", grid=(nc, T // TS),
        in_specs=[pl.BlockSpec((TS, D), lambda j, i: (i, 0)), pl.BlockSpec((HALO, D), lambda j, i: (_prev_idx(i, TS), 0)),
                  cols(0), cols(1), cols(2), pl.BlockSpec((3, MIX_COLS), lambda j, i: (0, j))],
        out_specs=[blk] * 4, out_shape=[out] * 4,
        params=_params(("parallel", "parallel")))(hn, hn, w_in, w_in, w_in, w)


def _mixer_out_bwd(dh, w_out, zb, zc, zu, w):
    last = T // TR - 1

    def body(dh_ref, dhn_ref, wo_ref, b_ref, bn_ref, c_ref, ch_ref, u_ref, uh_ref, w_ref, dz_ref, dw_ref):
        i = pl.program_id(0)
        dye = lax.dot_general(jnp.concatenate([dh_ref[...], dhn_ref[...]], axis=0), wo_ref[...], NT,
                              preferred_element_type=F32)
        cv_ = c_ref[...].astype(F32)
        uv = u_ref[...].astype(F32)
        cu = cv_ * uv
        cuh = jnp.where(i > 0, ch_ref[...].astype(F32) * uh_ref[...].astype(F32), 0.0)
        x2, x1, x0 = _causal_taps(jnp.concatenate([cuh, cu], axis=0))
        wv = w_ref[...]
        conv = (x2 * wv[0:1] + x1 * wv[1:2]) + x0 * wv[2:3]
        dyv = dye[:TR]
        dz_ref[:, 0:D] = (dyv * conv).astype(BF16)
        dconv = dyv * b_ref[...].astype(F32)
        dconv_n = jnp.where(i < last, dye[TR:] * bn_ref[...].astype(F32), 0.0)
        n1, n2 = _anticausal_taps(jnp.concatenate([dconv, dconv_n], axis=0), TR)
        dcu = (dconv * wv[2:3] + n1 * wv[1:2]) + n2 * wv[0:1]
        dz_ref[:, D:2 * D] = (dcu * uv).astype(BF16)
        dz_ref[:, 2 * D:3 * D] = (dcu * cv_).astype(BF16)
        part = jnp.concatenate([jnp.sum(dconv * x2, axis=0, keepdims=True),
                                jnp.sum(dconv * x1, axis=0, keepdims=True),
                                jnp.sum(dconv * x0, axis=0, keepdims=True)], axis=0)

        @pl.when(i == 0)
        def _():
            dw_ref[...] = part

        @pl.when(i > 0)
        def _():
            dw_ref[...] += part

    main = pl.BlockSpec((TR, D), lambda i: (i, 0))
    prev = pl.BlockSpec((HALO, D), lambda i: (_prev_idx(i), 0))
    nxt = pl.BlockSpec((HALO, D), lambda i: (_next_idx(i), 0))
    wspec = pl.BlockSpec((3, D), lambda i: (0, 0))
    return _pallas(
        body, name="d_l0_out", grid=(T // TR,),
        in_specs=[main, nxt, pl.BlockSpec((D, D), lambda i: (0, 0)), main, nxt, main, prev, main, prev, wspec],
        out_specs=[pl.BlockSpec((TR, 3 * D), lambda i: (i, 0)), wspec],
        out_shape=[jax.ShapeDtypeStruct((T, 3 * D), BF16), jax.ShapeDtypeStruct((3, D), F32)],
        params=_params(("arbitrary",)))(dh, dh, w_out, zb, zb, zc, zc, zu, zu, w)


def _sigmoid(x):
    return 0.5 * jnp.tanh(0.5 * x) + 0.5


def _ffn_up_act(name, hf, w_up, w, b):
    def body(h_ref, hh_ref, wg_ref, wv_ref, w_ref, b_ref, g_ref, v_ref, a_ref):
        i = pl.program_id(1)
        hv = h_ref[...]
        ge = lax.dot_general(jnp.concatenate([hh_ref[...], hv], axis=0), wg_ref[...], NT,
                             preferred_element_type=F32).astype(BF16)
        v = lax.dot_general(hv, wv_ref[...], NT, preferred_element_type=F32).astype(BF16)
        g_ref[...] = ge[HALO:]
        v_ref[...] = v
        ext = ge.astype(F32)
        row = lax.broadcasted_iota(jnp.int32, (HALO + TM, 1), 0)
        ext = jnp.where(jnp.logical_or(i > 0, row >= HALO), ext, 0.0)
        x2, x1, x0 = _causal_taps(ext)
        wv = w_ref[...]
        gc = ((x2 * wv[0:1] + x1 * wv[1:2]) + x0 * wv[2:3]) + b_ref[...]
        a_ref[...] = ((gc * _sigmoid(gc)) * v.astype(F32)).astype(BF16)

    blk = pl.BlockSpec((None, TM, FF_BLK), lambda j, i: (j, i, 0))
    out = jax.ShapeDtypeStruct((N_FF_BLK, T, FF_BLK), BF16)
    return _pallas(
        body, name=name, grid=(N_FF_BLK, T // TM),
        in_specs=[pl.BlockSpec((TM, D), lambda j, i: (i, 0)),
                  pl.BlockSpec((HALO, D), lambda j, i: (_prev_idx(i, TM), 0)),
                  pl.BlockSpec((None, None, FF_BLK, D), lambda j, i: (0, j, 0, 0)),
                  pl.BlockSpec((None, None, FF_BLK, D), lambda j, i: (0, j + N_FF_BLK, 0, 0)),
                  pl.BlockSpec((None, 3, FF_BLK), lambda j, i: (j, 0, 0)),
                  pl.BlockSpec((None, 1, FF_BLK), lambda j, i: (j, 0, 0))],
        out_specs=[blk, blk, blk], out_shape=[out, out, out],
        params=_params(("parallel", "parallel")))(hf, hf, w_up, w_up, w, b)


def _ffn_dact(name, dh, w_down4, g, v, w, b):
    last = T // TS - 1

    def body(dh_ref, dhn_ref, wd_ref, g_ref, gp_ref, gn_ref, v_ref, vn_ref, w_ref, b_ref, dg_ref, dv_ref, dw_ref, db_ref):
        i = pl.program_id(1)
        da = lax.dot_general(jnp.concatenate([dh_ref[...], dhn_ref[...]], axis=0), wd_ref[...], NT,
                             preferred_element_type=F32)
        row = lax.broadcasted_iota(jnp.int32, (TS + HALO, 1), 0)
        da = jnp.where(jnp.logical_or(i < last, row < TS), da, 0.0)
        gp = jnp.where(i > 0, gp_ref[...].astype(F32), 0.0)
        ext = jnp.concatenate([gp, g_ref[...].astype(F32), gn_ref[...].astype(F32)], axis=0)
        x2, x1, x0 = _causal_taps(ext)
        wv = w_ref[...]
        gc = ((x2 * wv[0:1] + x1 * wv[1:2]) + x0 * wv[2:3]) + b_ref[...]
        sg = _sigmoid(gc)
        vv = jnp.concatenate([v_ref[...].astype(F32), vn_ref[...].astype(F32)], axis=0)
        dv_ref[...] = (da[:TS] * (gc[:TS] * sg[:TS])).astype(BF16)
        dgc = (da * vv) * (sg * (1.0 + gc * (1.0 - sg)))
        n1, n2 = _anticausal_taps(dgc, TS)
        d0 = dgc[:TS]
        dg_ref[...] = ((d0 * wv[2:3] + n1 * wv[1:2]) + n2 * wv[0:1]).astype(BF16)
        part_w = jnp.concatenate([jnp.sum(d0 * x2[:TS], axis=0, keepdims=True),
                                  jnp.sum(d0 * x1[:TS], axis=0, keepdims=True),
                                  jnp.sum(d0 * x0[:TS], axis=0, keepdims=True)], axis=0)
        part_b = jnp.sum(d0, axis=0, keepdims=True)

        @pl.when(i == 0)
        def _():
            dw_ref[...] = part_w
            db_ref[...] = part_b

        @pl.when(i > 0)
        def _():
            dw_ref[...] += part_w
            db_ref[...] += part_b

    blk = pl.BlockSpec((None, TS, FF_BLK), lambda j, i: (j, i, 0))
    prev = pl.BlockSpec((None, HALO, FF_BLK), lambda j, i: (j, _prev_idx(i, TS), 0))
    nxt = pl.BlockSpec((None, HALO, FF_BLK), lambda j, i: (j, _next_idx(i, TS), 0))
    wspec = pl.BlockSpec((None, 3, FF_BLK), lambda j, i: (j, 0, 0))
    bspec = pl.BlockSpec((None, 1, FF_BLK), lambda j, i: (j, 0, 0))
    return _pallas(
        body, name=name, grid=(N_FF_BLK, T // TS),
        in_specs=[pl.BlockSpec((TS, D), lambda j, i: (i, 0)),
                  pl.BlockSpec((HALO, D), lambda j, i: (_next_idx(i, TS), 0)),
                  pl.BlockSpec((None, None, FF_BLK, D), lambda j, i: (0, j, 0, 0)),
                  blk, prev, nxt, blk, nxt, wspec, bspec],
        out_specs=[blk, blk, wspec, bspec],
        out_shape=[jax.ShapeDtypeStruct((N_FF_BLK, T, FF_BLK), BF16), jax.ShapeDtypeStruct((N_FF_BLK, T, FF_BLK), BF16),
                   jax.ShapeDtypeStruct((N_FF_BLK, 3, FF_BLK), F32), jax.ShapeDtypeStruct((N_FF_BLK, 1, FF_BLK), F32)],
        params=_params(("parallel", "arbitrary")))(dh, dh, w_down4, g, g, g, v, v, w, b)


def _rope_tables(pos, inv_freq):
    half = QK_ROPE // 2

    def body(p_ref, f_ref, c_ref, sa_ref, sb_ref):
        ang = p_ref[...].astype(F32) * f_ref[...]
        lane = lax.broadcasted_iota(jnp.int32, (T, 128), 1)
        c = jnp.cos(ang)
        s = jnp.sin(ang)
        c_ref[...] = jnp.where(lane < 2 * half, c, 0.0)
        sa_ref[...] = jnp.where(lane < half, -s, 0.0)
        sb_ref[...] = jnp.where(jnp.logical_and(lane >= half, lane < 2 * half), s, 0.0)

    return _pallas(
        body, name="rope_tables", in_specs=[VMEM_SPEC] * 2, out_specs=[VMEM_SPEC] * 3,
        out_shape=[jax.ShapeDtypeStruct((T, 128), F32)] * 3,
        params=pltpu.CompilerParams(vmem_limit_bytes=VMEM_LIMIT))(pos, inv_freq)


def _rotate(r, c, sa, sb, sign):
    return r * c + sign * (pltpu.roll(r, 96, 1) * sa + pltpu.roll(r, 32, 1) * sb)


def _attn_pre(h2, g_kv, g_l1, g_kvl, g_ql, w_dkv, w_kr, w_uk, w_uv, w_dq, w_uq, tables):
    def body(h_ref, c_ref, sa_ref, sb_ref, gkv_ref, gl1_ref, gkvl_ref, gql_ref, wdkv_ref, wkr_ref, wuk_ref, wuv_ref,
             wdq_ref, wuq_ref, hk_ref, hn_ref, ckvr_ref, ckv_ref, kr_ref, kn_ref, v_ref, cqr_ref, cq_ref, q_ref):
        xv = h_ref[...]
        xn = xv * lax.rsqrt(jnp.mean(xv * xv, axis=-1, keepdims=True) + EPS)
        hk = (xn * gkv_ref[...]).astype(BF16)
        hn = (xn * gl1_ref[...]).astype(BF16)
        hk_ref[...] = hk
        hn_ref[...] = hn
        cv, sav, sbv = c_ref[...], sa_ref[...], sb_ref[...]
        raw = lax.dot_general(hk, wdkv_ref[...], NN, preferred_element_type=F32)
        ckvr_ref[...] = raw
        ckv = _rms(raw, gkvl_ref[...]).astype(BF16)
        ckv_ref[...] = ckv
        kr = lax.dot_general(hk, wkr_ref[...], NT, preferred_element_type=F32)
        kr_ref[...] = _rotate(kr, cv, sav, sbv, 1.0).astype(BF16)
        kn_ref[...] = lax.dot_general(ckv, wuk_ref[...], NN, preferred_element_type=F32).astype(BF16)
        v_ref[...] = lax.dot_general(ckv, wuv_ref[...], NN, preferred_element_type=F32).astype(BF16)
        cqr = lax.dot_general(hn, wdq_ref[...], NN, preferred_element_type=F32)
        cqr_ref[...] = cqr
        cq = _rms(cqr, gql_ref[...]).astype(BF16)
        cq_ref[...] = cq
        for h in range(N_HEADS):
            r = lax.dot_general(cq, wuq_ref[h], NT, preferred_element_type=F32)
            q_ref[h, :, :QK_NOPE] = r[:, :QK_NOPE].astype(BF16)
            q_ref[h, :, QK_NOPE:] = _rotate(r[:, QK_NOPE:], cv, sav, sbv, 1.0).astype(BF16)

    def rows(d):
        return pl.BlockSpec((TS, d), lambda i: (i, 0))

    def whole(a):
        return pl.BlockSpec(a.shape, lambda i: (0,) * a.ndim)

    wholes = [g_kv, g_l1, g_kvl, g_ql, w_dkv, w_kr, w_uk, w_uv, w_dq, w_uq]
    outs = [(D, BF16), (D, BF16), (KV_LORA, F32), (KV_LORA, BF16), (128, BF16), (N_HEADS * QK_NOPE, BF16),
            (N_HEADS * V_HEAD, BF16), (Q_LORA, F32), (Q_LORA, BF16)]
    return _pallas(
        body, name="attn_pre", grid=(T // TS,),
        in_specs=[rows(D), rows(128), rows(128), rows(128)] + [whole(a) for a in wholes],
        out_specs=[rows(d) for d, _ in outs] + [pl.BlockSpec((N_HEADS, TS, QK_PAD), lambda i: (0, i, 0))],
        out_shape=[jax.ShapeDtypeStruct((T, d), dt) for d, dt in outs]
        + [jax.ShapeDtypeStruct((N_HEADS, T, QK_PAD), BF16)],
        params=_params(("parallel",)))(h2, *tables, *wholes)


SCALE = (QK_NOPE + QK_ROPE) ** -0.5
LOG2E = 1.4426950408889634
SCALE2 = SCALE * LOG2E


def _diag_mask(transposed):
    shift = CHUNK.bit_length() - 1
    a = lax.broadcasted_iota(jnp.int32, (TQ, TQ), 0) >> shift
    b = lax.broadcasted_iota(jnp.int32, (TQ, TQ), 1) >> shift
    return (a <= b) if transposed else (b <= a)


def _as_row(col):
    return jnp.transpose(jnp.broadcast_to(col, (col.shape[0], 128)), (1, 0))[0:1]


def _keys(kn_ref, kr_ref, off):
    return jnp.concatenate([kn_ref[pl.ds(off, TQ), :], kr_ref[pl.ds(off, TQ), :]], axis=1)


def _attn_fwd(q, kn, kr, v):
    hp = 2

    def body(q_ref, kn_ref, kr_ref, v_ref, o_ref, lse_ref):
        i = pl.program_id(1)
        qs = [q_ref[a] for a in range(hp)]

        def step(j, carry, masked):
            off = pl.multiple_of(j * TQ, TQ)
            krv = kr_ref[pl.ds(off, TQ), :]
            ss = []
            for a in range(hp):
                kk = jnp.concatenate([kn_ref[pl.ds(off, TQ), a * QK_NOPE:(a + 1) * QK_NOPE], krv], axis=1)
                ss.append(lax.dot_general(qs[a], kk, NT, preferred_element_type=F32))
            out = []
            for a in range(hp):
                m, l, acc = carry[a]
                s = ss[a] * SCALE2
                if masked:
                    s = jnp.where(_diag_mask(False), s, NEG_INF)
                m_new = jnp.maximum(m, jnp.max(s, axis=-1, keepdims=True))
                p = jnp.exp2(s - m_new)
                alpha = jnp.exp2(m - m_new)
                l = alpha * l + jnp.sum(p, axis=-1, keepdims=True)
                pv = lax.dot_general(p.astype(BF16), v_ref[pl.ds(off, TQ), a * V_HEAD:(a + 1) * V_HEAD], NN,
                                     preferred_element_type=F32)
                out.append((m_new, l, alpha * acc + pv))
            return tuple(out)

        one = (jnp.full((TQ, 1), NEG_INF, F32), jnp.zeros((TQ, 1), F32), jnp.zeros((TQ, V_HEAD), F32))
        carry = lax.fori_loop(0, i, lambda j, cr: step(j, cr, False), (one,) * hp)
        carry = step(i, carry, True)
        for a, (m, l, acc) in enumerate(carry):
            o_ref[:, a * V_HEAD:(a + 1) * V_HEAD] = (acc / l).astype(BF16)
            lse_ref[a] = _as_row(m + jnp.log(l) * LOG2E)

    return _pallas(
        body, name="attn_fwd", grid=(N_HEADS // hp, T // TQ),
        in_specs=[pl.BlockSpec((hp, TQ, QK_PAD), lambda h, i: (h, i, 0)),
                  pl.BlockSpec((T, hp * QK_NOPE), lambda h, i: (0, h)),
                  pl.BlockSpec((T, 128), lambda h, i: (0, 0)),
                  pl.BlockSpec((T, hp * V_HEAD), lambda h, i: (0, h))],
        out_specs=[pl.BlockSpec((TQ, hp * V_HEAD), lambda h, i: (i, h)), pl.BlockSpec((hp, 1, TQ), lambda h, i: (h, 0, i))],
        out_shape=[jax.ShapeDtypeStruct((T, N_HEADS * V_HEAD), BF16), jax.ShapeDtypeStruct((N_HEADS, 1, T), F32)],
        params=_params(("parallel", "parallel")))(q, kn, kr, v)


def _attn_bwd(q, kn, kr, v, o, do, lse_row, tables):
    nq = T // TQ
    hp = 2
    cos, sa, sb = tables

    def body(q_ref, kn_ref, kr_ref, v_ref, o_ref, do_ref, lse_ref, c_ref, sa_ref, sb_ref,
             dq_ref, dkn_ref, dkr_ref, dv_ref, dq_acc, dl_ref):
        j = pl.program_id(1)

        def cols(a):
            return slice(a * 128, (a + 1) * 128)

        @pl.when(j == 0)
        def _():
            dq_acc[...] = jnp.zeros_like(dq_acc)
            for a in range(hp):
                for i in range(nq):
                    rows = pl.ds(i * TQ, TQ)
                    prod = do_ref[rows, cols(a)].astype(F32) * o_ref[rows, cols(a)].astype(F32)
                    dl_ref[a, :, rows] = _as_row(jnp.sum(prod, axis=-1, keepdims=True))

        krv = kr_ref[...]
        kks = [jnp.concatenate([kn_ref[:, cols(a)], krv], axis=1) for a in range(hp)]
        vvs = [v_ref[:, cols(a)] for a in range(hp)]

        def step(i, carry, masked):
            off = pl.multiple_of(i * TQ, TQ)
            rows = pl.ds(off, TQ)
            qis = [q_ref[a, rows, :] for a in range(hp)]
            dois = [do_ref[rows, cols(a)] for a in range(hp)]
            sts = [lax.dot_general(kks[a], qis[a], NT, preferred_element_type=F32) for a in range(hp)]
            dpts = [lax.dot_general(vvs[a], dois[a], NT, preferred_element_type=F32) for a in range(hp)]
            out = []
            for a in range(hp):
                dk, dv = carry[a]
                st = sts[a] * SCALE2
                if masked:
                    st = jnp.where(_diag_mask(True), st, NEG_INF)
                pt = jnp.exp2(st - lse_ref[a, :, rows])
                dv = dv + lax.dot_general(pt.astype(BF16), dois[a], NN, preferred_element_type=F32)
                dst = ((pt * (dpts[a] - dl_ref[a, :, rows])) * SCALE).astype(BF16)
                dk = dk + lax.dot_general(dst, qis[a], NN, preferred_element_type=F32)
                dq_acc[a, rows, :] += lax.dot_general(dst, kks[a], TN, preferred_element_type=F32)
                out.append((dk, dv))
            return tuple(out)

        zero = (jnp.zeros((TQ, QK_PAD), F32), jnp.zeros((TQ, V_HEAD), F32))
        carry = step(j, (zero,) * hp, True)
        carry = lax.fori_loop(j + 1, nq, lambda i, cr: step(i, cr, False), carry)
        for a, (dk, dv) in enumerate(carry):
            dkn_ref[:, cols(a)] = dk[:, :QK_NOPE].astype(BF16)
            dkr_ref[a] = dk[:, QK_NOPE:]
            dv_ref[:, cols(a)] = dv.astype(BF16)

        @pl.when(j == nq - 1)
        def _():
            for a in range(hp):
                dq = dq_acc[a]
                dq_ref[a, :, :QK_NOPE] = dq[:, :QK_NOPE].astype(BF16)
                dq_ref[a, :, QK_NOPE:] = _rotate(dq[:, QK_NOPE:], c_ref[...], sa_ref[...], sb_ref[...], -1.0).astype(BF16)

    row = pl.BlockSpec((hp, 1, T), lambda h, j: (h, 0, 0))
    head = pl.BlockSpec((TQ, hp * 128), lambda h, j: (j, h))
    whole = pl.BlockSpec((hp, T, QK_PAD), lambda h, j: (h, 0, 0))
    tab = pl.BlockSpec((T, 128), lambda h, j: (0, 0))
    heads = pl.BlockSpec((T, hp * V_HEAD), lambda h, j: (0, h))
    return _pallas(
        body, name="attn_bwd", grid=(N_HEADS // hp, nq),
        in_specs=[whole, head, pl.BlockSpec((TQ, 128), lambda h, j: (j, 0)), head, heads, heads, row, tab, tab, tab],
        out_specs=[whole, head, pl.BlockSpec((hp, TQ, 128), lambda h, j: (h, j, 0)), head],
        out_shape=[jax.ShapeDtypeStruct((N_HEADS, T, QK_PAD), BF16), jax.ShapeDtypeStruct((T, N_HEADS * QK_NOPE), BF16),
                   jax.ShapeDtypeStruct((N_HEADS, T, 128), F32), jax.ShapeDtypeStruct((T, N_HEADS * V_HEAD), BF16)],
        scratch_shapes=[pltpu.VMEM((hp, T, QK_PAD), F32), pltpu.VMEM((hp, 1, T), F32)],
        params=_params(("parallel", "arbitrary")))(q, kn, kr, v, o, do, lse_row, cos, sa, sb)


def _rms_bwd_math(xv, g, dy):
    r = lax.rsqrt(jnp.mean(xv * xv, axis=-1, keepdims=True) + EPS)
    xn = xv * r
    gdy = dy * g
    return r * (gdy - xn * jnp.mean(gdy * xn, axis=-1, keepdims=True)), jnp.sum(dy * xn, axis=0, keepdims=True)


def _attn_post(dq, dkn, dv, dkr, cq_raw, ckv_raw, h2, dres, g_ql, g_kvl, g_l1, g_kv, w_uq, w_uk, w_uv, w_dq, w_dkv,
               w_kr, tables):
    def body(dq_ref, dkn_ref, dv_ref, dkr_ref, cqr_ref, ckvr_ref, h_ref, res_ref, c_ref, sa_ref, sb_ref,
             gql_ref, gkvl_ref, gl1_ref, gkv_ref, wuq_ref, wuk_ref, wuv_ref, wdq_ref, wdkv_ref, wkr_ref,
             dcq_ref, dckv_ref, dkrr_ref, dh_ref, dhb_ref, dgql_ref, dgkvl_ref, dgl1_ref, dgkv_ref):
        dcq = lax.dot_general(dq_ref[0], wuq_ref[0], NN, preferred_element_type=F32)
        for h in range(1, N_HEADS):
            dcq = dcq + lax.dot_general(dq_ref[h], wuq_ref[h], NN, preferred_element_type=F32)
        dcq_raw, s_ql = _rms_bwd_math(cqr_ref[...], gql_ref[...], dcq)
        dcq_raw = dcq_raw.astype(BF16)
        dcq_ref[...] = dcq_raw
        dckv = (lax.dot_general(dkn_ref[...], wuk_ref[...], NT, preferred_element_type=F32)
                + lax.dot_general(dv_ref[...], wuv_ref[...], NT, preferred_element_type=F32))
        dckv_raw, s_kvl = _rms_bwd_math(ckvr_ref[...], gkvl_ref[...], dckv)
        dckv_raw = dckv_raw.astype(BF16)
        dckv_ref[...] = dckv_raw
        dkr = dkr_ref[0]
        for h in range(1, N_HEADS):
            dkr = dkr + dkr_ref[h]
        dkr_raw = _rotate(dkr, c_ref[...], sa_ref[...], sb_ref[...], -1.0).astype(BF16)
        dkrr_ref[...] = dkr_raw
        d_hn = lax.dot_general(dcq_raw, wdq_ref[...], NT, preferred_element_type=F32)
        d_hk = (lax.dot_general(dckv_raw, wdkv_ref[...], NT, preferred_element_type=F32)
                + lax.dot_general(dkr_raw, wkr_ref[...], NN, preferred_element_type=F32))
        xv = h_ref[...]
        r = lax.rsqrt(jnp.mean(xv * xv, axis=-1, keepdims=True) + EPS)
        xn = xv * r
        dx = res_ref[...]
        sums = [s_ql, s_kvl]
        for dy, g_ref in ((d_hn, gl1_ref), (d_hk, gkv_ref)):
            gdy = dy * g_ref[...]
            dx = dx + r * (gdy - xn * jnp.mean(gdy * xn, axis=-1, keepdims=True))
            sums.append(jnp.sum(dy * xn, axis=0, keepdims=True))
        dh_ref[...] = dx
        dhb_ref[...] = dx.astype(BF16)
        dg_refs = (dgql_ref, dgkvl_ref, dgl1_ref, dgkv_ref)

        @pl.when(pl.program_id(0) == 0)
        def _():
            for dg_ref, part in zip(dg_refs, sums):
                dg_ref[...] = part

        @pl.when(pl.program_id(0) > 0)
        def _():
            for dg_ref, part in zip(dg_refs, sums):
                dg_ref[...] += part

    def rows(d):
        return pl.BlockSpec((TS, d), lambda i: (i, 0))

    def heads(d):
        return pl.BlockSpec((N_HEADS, TS, d), lambda i: (0, i, 0))

    def whole(a):
        return pl.BlockSpec(a.shape, lambda i: (0,) * a.ndim)

    wholes = [g_ql, g_kvl, g_l1, g_kv, w_uq, w_uk, w_uv, w_dq, w_dkv, w_kr]
    vecs = [Q_LORA, KV_LORA, D, D]
    return _pallas(
        body, name="attn_post", grid=(T // TS,),
        in_specs=[heads(QK_PAD), rows(N_HEADS * QK_NOPE), rows(N_HEADS * V_HEAD), heads(128), rows(Q_LORA),
                  rows(KV_LORA), rows(D), rows(D), rows(128), rows(128), rows(128)] + [whole(a) for a in wholes],
        out_specs=[rows(Q_LORA), rows(KV_LORA), rows(128), rows(D), rows(D)]
        + [pl.BlockSpec((1, d), lambda i: (0, 0)) for d in vecs],
        out_shape=[jax.ShapeDtypeStruct((T, Q_LORA), BF16), jax.ShapeDtypeStruct((T, KV_LORA), BF16),
                   jax.ShapeDtypeStruct((T, 128), BF16), jax.ShapeDtypeStruct((T, D), F32),
                   jax.ShapeDtypeStruct((T, D), BF16)] + [jax.ShapeDtypeStruct((1, d), F32) for d in vecs],
        params=_params(("arbitrary",)))(dq, dkn, dv, dkr, cq_raw, ckv_raw, h2, dres, *tables, *wholes)


def _ffn_gup(name, dg, dv, hf):
    def body(dg_ref, dv_ref, hf_ref, o_ref):
        j = pl.program_id(0)

        @pl.when(j < N_FF_BLK)
        def _():
            o_ref[...] = lax.dot_general(dg_ref[...], hf_ref[...], TN, preferred_element_type=F32).astype(BF16)

        @pl.when(j >= N_FF_BLK)
        def _():
            o_ref[...] = lax.dot_general(dv_ref[...], hf_ref[...], TN, preferred_element_type=F32).astype(BF16)

    return _pallas(
        body, name=name, grid=(N_DEV,),
        in_specs=[pl.BlockSpec((None, T, FF_BLK), lambda j: (jnp.minimum(j, N_FF_BLK - 1), 0, 0)),
                  pl.BlockSpec((None, T, FF_BLK), lambda j: (jnp.maximum(j - N_FF_BLK, 0), 0, 0)),
                  pl.BlockSpec((T, D), lambda j: (0, 0))],
        out_specs=pl.BlockSpec((None, FF_BLK, D), lambda j: (j, 0, 0)),
        out_shape=jax.ShapeDtypeStruct((N_DEV, FF_BLK, D), BF16), params=_params(("parallel",)))(dg, dv, hf)


def _ffn_layer_fwd(tag, h, gain, ex):
    hf = _rms_fwd(f"{tag}_norm", h, gain)
    g, v, act = _ffn_up_act(f"{tag}_up", hf, ex.need(f"ffn_w_up{tag[1]}", hf), ex.need(f"ffn_cw{tag[1]}", hf),
                            ex.need(f"ffn_cb{tag[1]}", hf))
    ex.at(f"{tag}_up", act)
    rows = pl.BlockSpec((TS, D), lambda i: (i, 0))
    out = _mm_sum(f"{tag}_down",
                  [(act, pl.BlockSpec((N_FF_BLK, TS, FF_BLK), lambda i: (0, i, 0)), ex.need(f"ffn_w_down{tag[1]}", act),
                    pl.BlockSpec((None, N_FF_BLK, FF_BLK, D), lambda i: (0, 0, 0, 0)), NN, 0)],
                  grid=(T // TS,), o_spec=rows, o_shape=(T, D), o_dtype=F32, add=h)
    ex.at(f"{tag}_down", out)
    return out, (hf, g, v, act)


def _ffn_layer_bwd(tag, h, gain, ex, saved, dh, dh_bf):
    hf, g, v, act = saved
    layer = tag[1]
    w_up, w_down4 = ex.need(f"ffn_w_up{layer}", dh_bf), ex.need(f"ffn_w_down{layer}", dh_bf)
    dg, dv, dcw, dcb = _ffn_dact(f"{tag}_dact", dh_bf, w_down4, g, v, ex.need(f"ffn_cw{layer}", dh_bf),
                                 ex.need(f"ffn_cb{layer}", dh_bf))
    ex.at(f"{tag}_dact", dg)
    g_down = _mm(f"{tag}_gdown", act, dh_bf, grid=(N_FF_BLK,),
                 a_spec=pl.BlockSpec((None, T, FF_BLK), lambda j: (j, 0, 0)),
                 b_spec=pl.BlockSpec((T, D), lambda j: (0, 0)),
                 o_spec=pl.BlockSpec((FF_BLK, D), lambda j: (j, 0)),
                 o_shape=(D_FF, D), o_dtype=BF16, dims=TN)
    g_up = _ffn_gup(f"{tag}_gup", dg, dv, hf)
    ex.grad("ffn_w_up", int(layer), g_up.reshape(1, N_DEV, FF_BLK, D))
    ex.grad("ffn_w_down", int(layer), g_down.reshape(1, N_DEV, D_FF // N_DEV, D))
    ex.at(f"{tag}_gup", g_up)
    part = pl.BlockSpec((N_FF_BLK, TR, FF_BLK), lambda i: (0, i, 0))
    dh_in, dh_in_bf, dgain = _mm_sum(
        f"{tag}_dhf",
        [(dg, part, w_up, pl.BlockSpec((None, N_FF_BLK, FF_BLK, D), lambda i: (0, 0, 0, 0)), NN, 0),
         (dv, part, w_up, pl.BlockSpec((None, N_FF_BLK, FF_BLK, D), lambda i: (0, 1, 0, 0)), NN, 0)],
        grid=(T // TR,), o_spec=pl.BlockSpec((TR, D), lambda i: (i, 0)), o_shape=(T, D), o_dtype=F32,
        norm_bwd=(h, [gain], dh))
    ex.at(f"{tag}_dhf", dh_in)
    return dh_in, dh_in_bf, dgain[0], dcw, dcb


def _local_step(x, pos, tgt, rep, ex):
    attn_norm, ffn_norm, final_norm = rep["attn_norm"], rep["ffn_norm"], rep["final_norm"]
    half = QK_ROPE // 2
    inv = 1.0 / (ROPE_THETA ** (jnp.arange(half, dtype=F32) / half))
    inv_freq = jnp.concatenate([inv, inv, jnp.zeros((128 - 2 * half,), F32)]).reshape(1, 128)
    tables = _rope_tables(pos, inv_freq)

    hn0 = _rms_fwd("l0_norm", x, attn_norm[0:1])
    ex.at("l0_norm", hn0)
    w_in = ex.need("sc_w_in", hn0)
    zb, zc, zu, y = _mixer_in(hn0, w_in, ex.need("sc_conv_w", hn0))
    ex.at("l0_in", y)
    h1 = _mm_rows("l0_out", y, ex.need("sc_w_out", y), NN, F32, D, tn=512, add=x)
    ex.at("l0_out", h1)
    h2, ffn0 = _ffn_layer_fwd("f0", h1, ffn_norm[0:1], ex)

    w_uq = ex.need("w_uq", h2)
    hk, hn1, ckv_raw, ckv, kr, kn, vv, cq_raw, cq, q = _attn_pre(
        h2, rep["kv_in_norm"], attn_norm[1:2], rep["kv_latent_norm"], rep["q_latent_norm"], ex.need("w_dkv", h2),
        ex.need("w_kr", h2), ex.need("w_uk", h2), ex.need("w_uv", h2), ex.need("w_dq", h2), w_uq, tables)

    o, lse = _attn_fwd(q, kn, kr, vv)
    ex.at("attn_fwd", o)
    w_o = ex.need("w_o", o)
    h3 = _mm_rows("attn_out", o, w_o, NN, F32, D, tn=512, add=h2)
    h4, ffn1 = _ffn_layer_fwd("f1", h3, ffn_norm[1:2], ex)

    loss, dh4, dh4_bf, d_final = _final(h4, final_norm.reshape(1, D), tgt)

    dh3, dh3_bf, d_fn1, dcw1, dcb1 = _ffn_layer_bwd("f1", h3, ffn_norm[1:2], ex, ffn1, dh4, dh4_bf)
    ex.at("f1_bwd", dh3)

    do = _mm_rows("d_attn_out", dh3_bf, w_o, NT, BF16, N_HEADS * V_HEAD)
    dq_pre, dkn, dkr, dvv = _attn_bwd(q, kn, kr, vv, o, do, lse, tables)

    dcq_raw_bf, dckv_raw_bf, dkr_raw_bf, dh2, dh2_bf, d_qln, d_kvln, d_an1, d_kvin = _attn_post(
        dq_pre, dkn, dvv, dkr, cq_raw, ckv_raw, h2, dh3, rep["q_latent_norm"], rep["kv_latent_norm"], attn_norm[1:2],
        rep["kv_in_norm"], w_uq, ex.need("w_uk", dkn), ex.need("w_uv", dvv), ex.need("w_dq", dq_pre),
        ex.need("w_dkv", dkn), ex.need("w_kr", dkr), tables)
    g_uq, g_dq, g_o = _wgrads("g_q", [(dq_pre, cq), (hn1, dcq_raw_bf), (o, dh3_bf)])
    ex.grad("w_uq", None, g_uq[:, :QK_NOPE + QK_ROPE].reshape(1, N_DEV, QK_NOPE + QK_ROPE, Q_LORA))
    ex.grad("w_dq", None, g_dq.reshape(1, N_DEV, D // N_DEV, Q_LORA))
    ex.grad("w_o", None, g_o.reshape(1, N_DEV, D // N_DEV, D))

    g_uk, g_uv, g_dkv, g_kr = _wgrads("g_kv", [(ckv, dkn), (ckv, dvv), (hk, dckv_raw_bf), (dkr_raw_bf, hk)])
    ex.grad("w_uk", None, g_uk)
    ex.grad("w_uv", None, g_uv)
    ex.grad("w_dkv", None, g_dkv.reshape(1, N_DEV, D // N_DEV, KV_LORA))
    ex.grad("w_kr", None, g_kr[:QK_ROPE])
    ex.at("kv_bwd", dh2)

    dh1, dh1_bf, d_fn0, dcw0, dcb0 = _ffn_layer_bwd("f0", h1, ffn_norm[0:1], ex, ffn0, dh2, dh2_bf)
    ex.at("f0_bwd", dh1)

    ex.grad("sc_w_out", None, _mm_wgrad("g_sc_w_out", y, dh1_bf).reshape(1, N_DEV, D // N_DEV, D))
    dz, d_scw = _mixer_out_bwd(dh1_bf, ex.need("sc_w_out", dh1_bf), zb, zc, zu, ex.need("sc_conv_w", dh1_bf))
    g_in = _mm_wgrad("g_sc_w_in", hn0, dz)
    ex.grad("sc_w_in", None, g_in)
    ex.at("sc_bwd", g_in)
    ex.at("d_l0_in", g_in)
    w_in = ex.need("sc_w_in", dz)
    grad_x, _, (d_an0,) = _mm_sum(
        "d_l0_in", [(dz[None], pl.BlockSpec((1, TS, dz.shape[1]), lambda i: (0, i, 0)),
                     w_in[None], pl.BlockSpec((1,) + w_in.shape, lambda i: (0, 0, 0)), NT, 0)],
        norm_bwd=(x, [attn_norm[0:1]], dh1),
        grid=(T // TS,), o_spec=pl.BlockSpec((TS, D), lambda i: (i, 0)), o_shape=(T, D), o_dtype=F32)

    small = {
        "attn_norm": jnp.concatenate([d_an0, d_an1], axis=0),
        "ffn_norm": jnp.concatenate([d_fn0, d_fn1], axis=0),
        "final_norm": d_final.reshape(D),
        "kv_in_norm": d_kvin.reshape(D),
        "kv_latent_norm": d_kvln.reshape(KV_LORA),
        "q_latent_norm": d_qln,
        "ffn_conv_b": jnp.stack([dcb0, dcb1]).transpose(0, 2, 1, 3).reshape(2, D_FF),
        "sc_conv_w": d_scw,
        "ffn_conv_w": jnp.stack([dcw0, dcw1]).transpose(0, 2, 1, 3).reshape(2, 3, D_FF),
    }
    return loss, grad_x, small


def _place():
    return lax.axis_index("x"), lax.axis_index("y"), lax.axis_index("c")


def _peers():
    x, y, c = _place()
    return (x, y, 1 - c), [(1 - x, y), (x, 1 - y), (1 - x, 1 - y)]


def _window(ref, kind, dev):
    if kind == "blocked":
        return ref.at[:, dev]
    width = ref.shape[-1] // N_DEV
    return ref.at[:, pl.ds(pl.multiple_of(dev * width, 128), width)]


HBM_SPEC = pl.BlockSpec(memory_space=pltpu.HBM)
SEM_SPEC = pl.BlockSpec(memory_space=pltpu.SEMAPHORE)
EFFECT = pltpu.SideEffectType.DATAFLOW_SIDE_EFFECTING
TOKEN = jax.ShapeDtypeStruct((8, 128), F32)


def _hbm(a):
    return pltpu.with_memory_space_constraint(a, pltpu.HBM)


def _copies_start(name, jobs):
    nj = len(jobs)
    counts = [(len(srcs), len(lands)) for srcs, lands, _, _ in jobs]
    n_arr = sum(ns + nl for ns, nl in counts)

    def body(*refs):
        sems, token = refs[n_arr:n_arr + 2 * nj], refs[-1]
        at = 0
        for j, ((ns, nl), (_, _, ncopy, plan)) in enumerate(zip(counts, jobs)):
            copies = plan(refs[at:at + ns], refs[at + ns:at + ns + nl])
            assert len(copies) == ncopy
            for k, (sent, dst, to, _) in enumerate(copies):
                pltpu.make_async_remote_copy(src_ref=sent, dst_ref=dst, send_sem=sems[2 * j].at[k],
                                             recv_sem=sems[2 * j + 1].at[k], device_id=to, device_id_type=MESH).start()
            at += ns + nl
        token[...] = jnp.zeros_like(token)

    arrays = [a for srcs, lands, _, _ in jobs for a in list(srcs) + list(lands)]
    sem_shapes = [pltpu.SemaphoreType.DMA((ncopy,)) for _, _, ncopy, _ in jobs for _ in range(2)]
    outs = pl.pallas_call(
        body, name=name, in_specs=[HBM_SPEC] * n_arr,
        out_specs=[SEM_SPEC] * (2 * nj) + [HBM_SPEC] * n_arr + [VMEM_SPEC],
        out_shape=sem_shapes + [pltpu.HBM(a.shape, a.dtype) for a in arrays] + [TOKEN],
        input_output_aliases={i: 2 * nj + i for i in range(n_arr)},
        compiler_params=pltpu.CompilerParams(has_side_effects=EFFECT))(*[_hbm(a) for a in arrays])
    _Chain.last = outs[-1]
    flights, at = [], 2 * nj
    for j, (ns, nl) in enumerate(counts):
        flights.append((outs[2 * j], outs[2 * j + 1], list(outs[at:at + ns]), list(outs[at + ns:at + ns + nl])))
        at += ns + nl
    return flights


def _copies_wait(name, started, ncopy, plan):
    send, recv, srcs, lands = started
    ns, nl = len(srcs), len(lands)

    def body(*refs):
        send_ref, recv_ref, token = refs[ns + nl], refs[ns + nl + 1], refs[-1]
        copies = plan(refs[:ns], refs[ns:ns + nl])
        assert len(copies) == ncopy
        for k, (sent, _, to, landed) in enumerate(copies):
            cp = pltpu.make_async_remote_copy(src_ref=sent, dst_ref=landed, send_sem=send_ref.at[k],
                                              recv_sem=recv_ref.at[k], device_id=to, device_id_type=MESH)
            cp.wait_send()
            cp.wait_recv()
        token[...] = jnp.zeros_like(token)

    arrays = list(srcs) + list(lands)
    outs = pl.pallas_call(
        body, name=name, in_specs=[HBM_SPEC] * (ns + nl) + [SEM_SPEC] * 2 + [ANY_SPEC],
        out_specs=[HBM_SPEC] * (ns + nl) + [VMEM_SPEC], out_shape=[pltpu.HBM(a.shape, a.dtype) for a in arrays] + [TOKEN],
        input_output_aliases={i: i for i in range(ns + nl)},
        compiler_params=pltpu.CompilerParams(has_side_effects=EFFECT))(*arrays, send, recv, _Chain.last)
    _Chain.last = outs[-1]
    return list(outs[:ns]), list(outs[ns:-1])


def _plan_gather_chips(kinds):
    def plan(srcs, lands):
        x, y, c = _place()
        sibling, chips = _peers()
        out = []
        for t, kind in enumerate(kinds):
            mine = _window(lands[t], kind, 4 * x + 2 * y + c)
            out.append((srcs[t], mine, (x, y, c), mine))
            out.append((srcs[t], mine, sibling, _window(lands[t], kind, 4 * x + 2 * y + 1 - c)))
            for px, py in chips:
                out.append((srcs[t], mine, (px, py, c), _window(lands[t], kind, 4 * px + 2 * py + c)))
        return out
    return plan, 5 * len(kinds)


def _plan_gather_all(n):
    def plan(srcs, lands):
        x, y, c = _place()
        out = []
        for t in range(n):
            mine = lands[t].at[:, 4 * x + 2 * y + c]
            for m in range(N_DEV):
                px, py, pc = (1 - x if m & 4 else x), (1 - y if m & 2 else y), (1 - c if m & 1 else c)
                out.append((srcs[t], mine, (px, py, pc), lands[t].at[:, 4 * px + 2 * py + pc]))
        return out
    return plan, N_DEV * n


def _plan_gather_sibling(kinds):
    def plan(srcs, lands):
        _, _, c = _place()
        sibling, chips = _peers()
        out = []
        for t, kind in enumerate(kinds):
            for px, py in chips:
                w = _window(lands[t], kind, 4 * px + 2 * py + c)
                out.append((w, w, sibling, _window(lands[t], kind, 4 * px + 2 * py + 1 - c)))
        return out
    return plan, 3 * len(kinds)


def _plan_scatter_sibling(kinds):
    def plan(srcs, lands):
        _, _, c = _place()
        sibling, _ = _peers()
        out = []
        for t, kind in enumerate(kinds):
            for k in range(N_CHIP):
                out.append((_window(srcs[t], kind, 2 * k + 1 - c), lands[t].at[k], sibling, lands[t].at[k]))
        return out
    return plan, N_CHIP * len(kinds)


def _plan_scatter_chips(n):
    def plan(srcs, lands):
        x, y, c = _place()
        _, chips = _peers()
        out = []
        for t in range(n):
            for px, py in chips:
                out.append((srcs[t].at[2 * px + py], lands[t].at[2 * x + y], (px, py, c), lands[t].at[2 * px + py]))
        return out
    return plan, 3 * n


def _landing(shard, kind):
    if kind == "blocked":
        return lax.empty((shard.shape[0], N_DEV) + shard.shape[1:], shard.dtype)
    return lax.empty((shard.shape[0], N_DEV * shard.shape[1]), shard.dtype)


def _chip_sums(name, grads, kinds, recvs, c):
    n = len(grads)
    in_specs, out_specs, out_shape, args = [], [], [], []
    for gr, kind, rv in zip(grads, kinds, recvs):
        if kind == "blocked":
            rows, w = gr.shape[2], gr.shape[3]
            in_specs.append(pl.BlockSpec((None, None, rows, w), lambda k, cref: (0, 2 * k + cref[0], 0, 0)))
        else:
            rows, w = gr.shape[0], gr.shape[1] // N_DEV
            in_specs.append(pl.BlockSpec((rows, w), lambda k, cref: (0, 2 * k + cref[0])))
        blk = pl.BlockSpec((None, rows, w), lambda k, cref: (k, 0, 0))
        in_specs.append(blk)
        out_specs.append(blk)
        out_shape.append(jax.ShapeDtypeStruct((N_CHIP, rows, w), BF16))
        args += [gr, rv.reshape(N_CHIP, rows, w)]

    def body(*refs):
        for t in range(n):
            g_ref, r_ref, o_ref = refs[1 + 2 * t], refs[2 + 2 * t], refs[1 + 2 * n + t]
            o_ref[...] = (g_ref[...].astype(F32) + r_ref[...].astype(F32)).astype(BF16)

    return _pallas(body, name=name, n_prefetch=1, grid=(N_CHIP,), in_specs=in_specs, out_specs=out_specs,
                   out_shape=out_shape, params=_params(("parallel",)))(c, *args)


def _adamw_math(g, wv, mv, vv):
    m = ADAM_B1 * mv + (1.0 - ADAM_B1) * g
    v = ADAM_B2 * vv + (1.0 - ADAM_B2) * (g * g)
    m_hat = m / (1.0 - ADAM_B1 ** ADAM_STEP)
    v_hat = v / (1.0 - ADAM_B2 ** ADAM_STEP)
    delta = -ADAM_LR * (m_hat / (jnp.sqrt(v_hat) + ADAM_EPS) + ADAM_WD * wv)
    return delta, m, v


ADAM_STEPS = 2


def _adamw_group(name, items, chip_ids):
    n = len(items)
    in_specs, out_specs, out_shape, args, prevs = [], [], [], [chip_ids], []
    for own, recv, w3, m3, v3, layer, _ in items:
        nl, rows, w = w3.shape
        tr = rows // ADAM_STEPS
        assert tr % 16 == 0, (name, rows)
        in_specs += [pl.BlockSpec((None, tr, w), lambda i, ids, slot=slot: (ids[slot], i, 0)) for slot in range(4)]
        slab = pl.BlockSpec((None, tr, w), lambda i, ids, layer=layer: (layer, i, 0))
        in_specs += [slab] * 3
        out_specs += [slab] * 4
        out_shape += [jax.ShapeDtypeStruct((nl, rows, w), F32)] * 4
        args += [own, recv, recv, recv, w3, m3, v3]
    aliases = {}
    for t, item in enumerate(items):
        if item[6] is not None:
            for k in range(4):
                aliases[len(args) + k] = 4 * t + k
            in_specs += [ANY_SPEC] * 4
            args += list(item[6])
            prevs.append(t)
    n_in = 1 + 7 * n + 4 * len(prevs)

    def body(*refs):
        for t in range(n):
            own_ref, r1_ref, r2_ref, r3_ref, w_ref, m_ref, v_ref = refs[1 + 7 * t:8 + 7 * t]
            g_ref, d_ref, nm_ref, nv_ref = refs[n_in + 4 * t:n_in + 4 * t + 4]
            g = ((own_ref[...].astype(F32) + r1_ref[...].astype(F32)) + r2_ref[...].astype(F32)) + r3_ref[...].astype(F32)
            g_ref[...] = g
            d_ref[...], nm_ref[...], nv_ref[...] = _adamw_math(g, w_ref[...], m_ref[...], v_ref[...])

    outs = _pallas(body, name=name, n_prefetch=1, grid=(ADAM_STEPS,), in_specs=in_specs, out_specs=out_specs,
                   out_shape=out_shape, aliases=aliases, params=_params(("parallel",)))(*args)
    return [list(outs[4 * t:4 * t + 4]) for t in range(n)]


SHARD_ROWS = 8


def _adamw_small(gathered, ws, ms, vs, me):
    n = len(gathered)
    full = [w is not None for w in ws]
    sharded = [w is not None and w.ndim == 3 for w in ws]
    args = list(gathered)
    out_shape = []
    for t in range(n):
        shape = jax.ShapeDtypeStruct(ws[t].shape if sharded[t] else gathered[t].shape[2:], F32)
        if full[t]:
            args += [ws[t], ms[t], vs[t]]
            out_shape += [shape] * 4
        else:
            out_shape += [shape]

    def body(*refs):
        i_in, i_out = n, len(args) + 1
        me_ref = refs[len(args)]
        for t in range(n):
            p_ref = refs[t]
            if sharded[t]:
                w_ref, m_ref, v_ref = refs[i_in:i_in + 3]
                taps, layers, _ = w_ref.shape
                mine = pl.ds(pl.multiple_of(me_ref[0] * SHARD_ROWS, SHARD_ROWS), SHARD_ROWS)
                g = p_ref[0, 0, mine, :]
                for k in range(1, N_DEV):
                    g = g + p_ref[0, k, mine, :]
                for l in range(layers):
                    for k in range(taps):
                        at = (k, slice(l, l + 1), slice(None))
                        row = g[l * taps + k:l * taps + k + 1]
                        refs[i_out][at] = row
                        refs[i_out + 1][at], refs[i_out + 2][at], refs[i_out + 3][at] = _adamw_math(
                            row, w_ref[at], m_ref[at], v_ref[at])
                i_in += 3
                i_out += 4
                continue
            g = p_ref[0, 0]
            for k in range(1, N_DEV):
                g = g + p_ref[0, k]
            refs[i_out][...] = g
            if full[t]:
                w_ref, m_ref, v_ref = refs[i_in:i_in + 3]
                refs[i_out + 1][...], refs[i_out + 2][...], refs[i_out + 3][...] = _adamw_math(
                    g, w_ref[...], m_ref[...], v_ref[...])
                i_in += 3
                i_out += 4
            else:
                i_out += 1

    outs = _pallas(body, name="adamw_small",
                   in_specs=[VMEM_SPEC] * len(args) + [pl.BlockSpec(memory_space=pltpu.SMEM)],
                   out_specs=[VMEM_SPEC] * len(out_shape), out_shape=out_shape,
                   params=pltpu.CompilerParams(vmem_limit_bytes=VMEM_LIMIT))(*args, me)
    result, i = [], 0
    for t in range(n):
        k = 4 if full[t] else 1
        result.append(list(outs[i:i + k]))
        i += k
    return result


KIND = {"sc_w_in": "cols", "sc_w_out": "blocked", "w_dkv": "blocked", "w_kr": "cols", "w_uk": "cols", "w_uv": "cols",
        "w_dq": "blocked", "w_uq": "blocked", "w_o": "blocked", "ffn_w_up": "blocked", "ffn_w_down": "blocked",
        "conv": "blocked"}
GATHER_GROUPS = (("mixer", ("sc_w_in", "sc_w_out", "conv")),
                 ("up0", ("ffn_w_up0",)),
                 ("down0", ("ffn_w_down0",)),
                 ("attn", ("w_dkv", "w_kr", "w_uk", "w_uv", "w_dq", "w_uq", "w_o")),
                 ("ffn1", ("ffn_w_up1", "ffn_w_down1")))
SCATTER_GROUPS = (("ffn1", (("ffn_w_up", 1), ("ffn_w_down", 1))),
                  ("attn", (("w_o", None), ("w_uq", None), ("w_dq", None), ("w_uk", None), ("w_uv", None),
                            ("w_dkv", None), ("w_kr", None))),
                  ("ffn0", (("ffn_w_up", 0), ("ffn_w_down", 0))),
                  ("mixer", (("sc_w_out", None), ("sc_w_in", None))))
SCHEDULE = {
    "begin": (("gather_start", "mixer"),),
    "l0_norm": (("gather_forward", "mixer"), ("gather_start", "up0")),
    "l0_out": (("gather_forward", "up0"), ("gather_start", "down0"), ("gather_start", "attn")),
    "f0_up": (("gather_forward", "down0"), ("gather_forward", "attn"), ("gather_start", "ffn1")),
    "attn_fwd": (("gather_forward", "ffn1"),),
    "f1_gup": (("scatter_sibling", "ffn1"),),
    "f1_dhf": (("scatter_chips", "ffn1"),),
    "kv_bwd": (("scatter_sibling", "attn"),),
    "f0_dact": (("scatter_chips", "attn"),),
    "f0_gup": (("scatter_sibling", "ffn0"),),
    "f0_dhf": (("scatter_chips", "ffn0"),),
    "sc_bwd": (("scatter_sibling", "mixer"), ("scatter_done", "attn")),
    "d_l0_in": (("scatter_chips", "mixer"),),
}
FINISH = (("scatter_done", "ffn1"), ("scatter_done", "ffn0"), ("scatter_done", "mixer"))
STAGES = {"gather_start": 1, "gather_forward": 2, "gather_done": 3,
          "scatter_sibling": 1, "scatter_chips": 2, "scatter_done": 3}
SMALL_W_ROWS = 24


def _pack(arrays, rows):
    flat = jnp.concatenate([a.reshape(-1).astype(F32) for a in arrays])
    return jnp.pad(flat, (0, rows * 128 - flat.shape[0])).reshape(rows, 128)


def _cast_shards(items):
    arrays = []
    for a, _, _ in items:
        if not any(a is b for b in arrays):
            arrays.append(a)
    slot = [next(i for i, b in enumerate(arrays) if b is a) for a, _, _ in items]

    def body(*refs):
        for t, (a, layer, rows) in enumerate(items):
            w_ref, o_ref = refs[slot[t]], refs[len(arrays) + t]
            r, c = a.shape[-2:]
            if a.ndim == 3:
                o_ref[:, :r] = w_ref[(layer or 0):(layer or 0) + 1].astype(BF16)
                if rows > r:
                    o_ref[:, r:] = jnp.zeros((1, rows - r, c), BF16)
            else:
                o_ref[:r] = w_ref[...].astype(BF16)
                if rows > r:
                    o_ref[r:] = jnp.zeros((rows - r, c), BF16)

    out_shape = [jax.ShapeDtypeStruct(((1,) if a.ndim == 3 else ()) + (rows, a.shape[-1]), BF16)
                 for a, _, rows in items]
    return _pallas(body, name="cast_shards", in_specs=[VMEM_SPEC] * len(arrays), out_specs=[VMEM_SPEC] * len(items),
                   out_shape=out_shape, params=pltpu.CompilerParams(vmem_limit_bytes=VMEM_LIMIT))(*arrays)


STORED_TRANSPOSED = ("ffn_w_up", "w_uq", "w_kr")


def _stored(name, a):
    return jnp.swapaxes(a, -1, -2) if name in STORED_TRANSPOSED else a


def _base(name):
    if name.startswith("ffn_w_") and name[-1] in "01":
        return name[:-1], int(name[-1])
    return name, None


class _Exchange:
    def __init__(self, wts, mom, var, ffn_conv_b):
        self.wts, self.mom, self.var = wts, mom, var
        x, y, c = _place()
        self.c_arr = jnp.reshape(c, (1,)).astype(jnp.int32)
        chip = 2 * x + y
        self.chip_ids = jnp.stack([chip, chip ^ 1, chip ^ 2, chip ^ 3]).astype(jnp.int32)
        self.ready = {"ffn_cb0": ffn_conv_b.reshape(2, N_FF_BLK, 1, FF_BLK)[0],
                      "ffn_cb1": ffn_conv_b.reshape(2, N_FF_BLK, 1, FF_BLK)[1]}
        self.gathers, self.group_of = {}, {}
        self.grads, self.scatters, self.results, self.queue = {}, {}, {}, []
        for gname, names in GATHER_GROUPS:
            self.gathers[gname] = dict(stage=0, names=names, kinds=[KIND[_base(nm)[0]] for nm in names])
            for nm in names:
                self.group_of[nm] = gname
        for nm in ("sc_conv_w", "ffn_cw0", "ffn_cw1"):
            self.group_of[nm] = "mixer"
        self.cast, self.f32 = {}, {}
        self.at("begin", None)
        later = [nm for gname, names in GATHER_GROUPS[1:] for nm in names]
        self.cast = dict(zip(later, _cast_shards([self._shard_f32(nm) for nm in later])))

    def _shard_f32(self, name):
        base, layer = _base(name)
        if base not in self.f32:
            a = _stored(base, self.wts[base])
            self.f32[base] = a.reshape(a.shape[-2:]) if KIND[base] == "cols" else a.reshape((-1,) + a.shape[-2:])
        a = self.f32[base]
        return a, layer, {"w_kr": 128, "w_uq": QK_PAD}.get(base, a.shape[-2])

    def _shard(self, name):
        if name in self.cast:
            return self.cast[name]
        if name == "conv":
            return _pack([self.wts["sc_conv_w"], self.wts["ffn_conv_w"]], SMALL_W_ROWS).reshape(1, SMALL_W_ROWS, 128)
        base, layer = _base(name)
        a = _stored(base, self.wts[base])
        if layer is not None:
            a = a[layer:layer + 1]
        if KIND[base] == "cols":
            return a.reshape(a.shape[-2], a.shape[-1]).astype(BF16)
        return a.reshape((-1,) + a.shape[-2:]).astype(BF16)

    def _start(self, name, srcs, lands, ncopy, plan, st):
        self.queue.append((name, (srcs, lands, ncopy, plan), st))

    def _flush(self):
        if self.queue:
            flights = _copies_start("__".join(name for name, _, _ in self.queue), [job for _, job, _ in self.queue])
            for (_, _, st), flight in zip(self.queue, flights):
                st["flight"] = flight
            self.queue = []

    def _flight(self, st):
        self._flush()
        return st["flight"]

    def _gather_to(self, gname, stage, after):
        st = self.gathers[gname]
        if st["stage"] < 1 <= stage:
            shards = [self._shard(nm) for nm in st["names"]]
            lands = [_landing(s, kind) for s, kind in zip(shards, st["kinds"])]
            plan, ncopy = _plan_gather_chips(st["kinds"])
            self._start(f"ag_{gname}_chips", shards, lands, ncopy, plan, st)
            st["stage"] = 1
        if st["stage"] < 2 <= stage:
            plan, ncopy = _plan_gather_chips(st["kinds"])
            _, lands = _copies_wait(f"ag_{gname}_chips_wait", self._flight(st), ncopy, plan)
            plan, ncopy = _plan_gather_sibling(st["kinds"])
            self._start(f"ag_{gname}_sibling", [], lands, ncopy, plan, st)
            st["stage"] = 2
        if st["stage"] < 3 <= stage:
            plan, ncopy = _plan_gather_sibling(st["kinds"])
            _, lands = _copies_wait(f"ag_{gname}_sibling_wait", self._flight(st), ncopy, plan)
            for nm, land in zip(st["names"], lands):
                self._arrived(nm, land)
            st["stage"] = 3

    def _arrived(self, name, land):
        if name == "conv":
            conv = land.reshape(N_DEV, SMALL_W_ROWS * 128)
            self.ready["sc_conv_w"] = conv[:, :3 * 128].reshape(N_DEV, 3, 128).transpose(1, 0, 2).reshape(3, D)
            fcw = conv[:, 3 * 128:3 * 128 + 6 * 352].reshape(N_DEV, 2, 3, 352).transpose(1, 2, 0, 3)
            fcw = fcw.reshape(2, 3, N_FF_BLK, FF_BLK).transpose(0, 2, 1, 3)
            self.ready["ffn_cw0"], self.ready["ffn_cw1"] = fcw[0], fcw[1]
        elif name in ("sc_w_in", "w_uk", "w_uv", "w_kr") or name.startswith("ffn_w_up"):
            self.ready[name] = land
        elif name.startswith("ffn_w_down"):
            self.ready[name] = land.reshape(1, N_FF_BLK, FF_BLK, D)
        elif name == "w_uq":
            self.ready[name] = land.reshape(N_HEADS, QK_PAD, Q_LORA)
        else:
            self.ready[name] = land.reshape(D, land.shape[-1])

    def need(self, name, after):
        if name not in self.ready:
            self._gather_to(self.group_of[name], 3, after)
            self._flush()
        return self.ready[name]

    def grad(self, name, layer, array):
        self.grads[(name, layer)] = array

    def _scatter_to(self, gname, stage, after):
        keys = dict(SCATTER_GROUPS)[gname]
        st = self.scatters.setdefault(gname, dict(stage=0))
        kinds = [KIND[nm] for nm, _ in keys]
        if st["stage"] < 1 <= stage:
            grads = [self.grads[key] for key in keys]
            lands = []
            for gr, kind in zip(grads, kinds):
                shard = (gr.shape[0],) + gr.shape[2:] if kind == "blocked" else (gr.shape[0], gr.shape[1] // N_DEV)
                lands.append(lax.empty((N_CHIP,) + shard, BF16))
            plan, ncopy = _plan_scatter_sibling(kinds)
            self._start(f"rs_{gname}_sibling", grads, lands, ncopy, plan, st)
            st["stage"] = 1
        if st["stage"] < 2 <= stage:
            plan, ncopy = _plan_scatter_sibling(kinds)
            grads, recvs = _copies_wait(f"rs_{gname}_sibling_wait", self._flight(st), ncopy, plan)
            sums = _chip_sums(f"rs_{gname}_sums", grads, kinds, recvs, self.c_arr)
            lands = [lax.empty(s.shape, BF16) for s in sums]
            plan, ncopy = _plan_scatter_chips(len(sums))
            self._start(f"rs_{gname}_chips", sums, lands, ncopy, plan, st)
            st["stage"] = 2
        if st["stage"] < 3 <= stage:
            plan, ncopy = _plan_scatter_chips(len(keys))
            sums, recvs = _copies_wait(f"rs_{gname}_chips_wait", self._flight(st), ncopy, plan)
            items = []
            for (nm, layer), own, rv in zip(keys, sums, recvs):
                nl = 1 if layer is None else 2
                rows, w = own.shape[1], own.shape[2]
                w3, m3, v3 = (_stored(nm, src[nm]).reshape(nl, rows, w) for src in (self.wts, self.mom, self.var))
                items.append((own, rv, w3, m3, v3, 0 if layer is None else layer, self.results.get(nm)))
            outs = _adamw_group(f"adamw_{gname}", items, self.chip_ids)
            for (nm, _), out in zip(keys, outs):
                self.results[nm] = out
            st["stage"] = 3

    def at(self, place, after):
        for action, gname in SCHEDULE.get(place, ()):
            self._advance(action, gname, after)
        self._flush()

    def _advance(self, action, gname, after):
        if action.startswith("gather"):
            self._gather_to(gname, STAGES[action], after)
        else:
            self._scatter_to(gname, STAGES[action], after)

    def finish(self, after):
        for action, gname in FINISH:
            self._advance(action, gname, after)
        for gname, _ in SCATTER_GROUPS:
            self._scatter_to(gname, 3, after)
        return {nm: [_stored(nm, o.reshape(_stored(nm, self.wts[nm]).shape)) for o in outs]
                for nm, outs in self.results.items()}


REPLICATED = ("attn_norm", "ffn_norm", "final_norm", "kv_in_norm", "kv_latent_norm", "q_latent_norm", "ffn_conv_b")
WEIGHTS = ("attn_norm", "ffn_norm", "final_norm", "sc_w_in", "sc_conv_w", "sc_w_out", "kv_in_norm", "w_dkv",
           "kv_latent_norm", "w_kr", "w_uk", "w_uv", "w_dq", "q_latent_norm", "w_uq", "w_o", "ffn_w_up", "ffn_conv_w",
           "ffn_conv_b", "ffn_w_down")


def kernel(x, positions, attn_norm, ffn_norm, final_norm, sc_w_in, sc_conv_w, sc_w_out, kv_in_norm, w_dkv, kv_latent_norm, w_kr, w_uk, w_uv, w_dq, q_latent_norm, w_uq, w_o, ffn_w_up, ffn_conv_w, ffn_conv_b, ffn_w_down, loss_target, m_attn_norm, m_ffn_norm, m_final_norm, m_sc_w_in, m_sc_conv_w, m_sc_w_out, m_kv_in_norm, m_w_dkv, m_kv_latent_norm, m_w_kr, m_w_uk, m_w_uv, m_w_dq, m_q_latent_norm, m_w_uq, m_w_o, m_ffn_w_up, m_ffn_conv_w, m_ffn_conv_b, m_ffn_w_down, v_attn_norm, v_ffn_norm, v_final_norm, v_sc_w_in, v_sc_conv_w, v_sc_w_out, v_kv_in_norm, v_w_dkv, v_kv_latent_norm, v_w_kr, v_w_uk, v_w_uv, v_w_dq, v_q_latent_norm, v_w_uq, v_w_o, v_ffn_w_up, v_ffn_conv_w, v_ffn_conv_b, v_ffn_w_down):
    wts = dict(attn_norm=attn_norm, ffn_norm=ffn_norm, final_norm=final_norm, sc_w_in=sc_w_in, sc_conv_w=sc_conv_w,
               sc_w_out=sc_w_out, kv_in_norm=kv_in_norm, w_dkv=w_dkv, kv_latent_norm=kv_latent_norm, w_kr=w_kr,
               w_uk=w_uk, w_uv=w_uv, w_dq=w_dq, q_latent_norm=q_latent_norm, w_uq=w_uq, w_o=w_o, ffn_w_up=ffn_w_up,
               ffn_conv_w=ffn_conv_w, ffn_conv_b=ffn_conv_b, ffn_w_down=ffn_w_down)
    mom = dict(attn_norm=m_attn_norm, ffn_norm=m_ffn_norm, final_norm=m_final_norm, sc_w_in=m_sc_w_in,
               sc_conv_w=m_sc_conv_w, sc_w_out=m_sc_w_out, kv_in_norm=m_kv_in_norm, w_dkv=m_w_dkv,
               kv_latent_norm=m_kv_latent_norm, w_kr=m_w_kr, w_uk=m_w_uk, w_uv=m_w_uv, w_dq=m_w_dq,
               q_latent_norm=m_q_latent_norm, w_uq=m_w_uq, w_o=m_w_o, ffn_w_up=m_ffn_w_up, ffn_conv_w=m_ffn_conv_w,
               ffn_conv_b=m_ffn_conv_b, ffn_w_down=m_ffn_w_down)
    var = dict(attn_norm=v_attn_norm, ffn_norm=v_ffn_norm, final_norm=v_final_norm, sc_w_in=v_sc_w_in,
               sc_conv_w=v_sc_conv_w, sc_w_out=v_sc_w_out, kv_in_norm=v_kv_in_norm, w_dkv=v_w_dkv,
               kv_latent_norm=v_kv_latent_norm, w_kr=v_w_kr, w_uk=v_w_uk, w_uv=v_w_uv, w_dq=v_w_dq,
               q_latent_norm=v_q_latent_norm, w_uq=v_w_uq, w_o=v_w_o, ffn_w_up=v_ffn_w_up, ffn_conv_w=v_ffn_conv_w,
               ffn_conv_b=v_ffn_conv_b, ffn_w_down=v_ffn_w_down)
    xi, yi, ci = _place()
    me = 4 * xi + 2 * yi + ci
    _Chain.last = None

    ex = _Exchange(wts, mom, var, ffn_conv_b)
    rep = {
        "attn_norm": attn_norm, "ffn_norm": ffn_norm, "final_norm": final_norm,
        "kv_in_norm": kv_in_norm.reshape(1, D), "kv_latent_norm": kv_latent_norm.reshape(1, KV_LORA),
        "q_latent_norm": q_latent_norm.reshape(1, Q_LORA),
    }
    loss, grad_x, small = _local_step(x.reshape(T, D), positions.reshape(T, 1), loss_target.reshape(T, D), rep, ex)

    def rows_of(a):
        return a.reshape(-1, a.shape[-1])

    def device_rows(a):
        taps, c = a.shape[-2], a.shape[-1] // N_DEV
        rows = a.reshape(-1, taps, N_DEV, c).transpose(2, 0, 1, 3).reshape(N_DEV, -1, c)
        return jnp.pad(rows, ((0, 0), (0, SHARD_ROWS - rows.shape[1]), (0, 0))).reshape(N_DEV * SHARD_ROWS, c)

    def taps_first(a):
        return jnp.transpose(a, (1, 0, 2))

    sharded = ("sc_conv_w", "ffn_conv_w")
    shards = ([loss.reshape(1, 1, 128)] + [rows_of(small[nm])[None] for nm in REPLICATED]
              + [device_rows(small[nm])[None] for nm in sharded])
    plan, ncopy = _plan_gather_all(len(shards))
    flight, = _copies_start("ag_small", [(shards, [lax.empty((1, N_DEV) + s.shape[1:], F32) for s in shards], ncopy, plan)])
    results = ex.finish(grad_x)
    _, gathered = _copies_wait("ag_small_wait", flight, ncopy, plan)
    params = [[None] + [rows_of(src[nm]) for nm in REPLICATED] + [taps_first(src[nm]) for nm in sharded]
              for src in (wts, mom, var)]
    summed = _adamw_small(gathered, *params, me.astype(jnp.int32).reshape(1))
    loss_total = summed[0][0][0, 0]
    for nm, vals in zip(REPLICATED, summed[1:1 + len(REPLICATED)]):
        results[nm] = [a.reshape(wts[nm].shape) for a in vals]
    for nm, vals in zip(sharded, summed[1 + len(REPLICATED):]):
        results[nm] = [taps_first(a) for a in vals]

    outs = [loss_total, grad_x.reshape(1, T, D)]
    for slot in range(4):
        outs.extend(results[nm][slot] for nm in WEIGHTS)
    return tuple(outs)
```

```python
import jax
import jax.numpy as jnp
from jax import lax
from jax.experimental import pallas as pl
from jax.experimental.pallas import tpu as pltpu

F32 = jnp.float32
BF16 = jnp.bfloat16

T = 2048
D = 1024
N_HEADS = 8
QK_NOPE = 128
QK_ROPE = 64
V_HEAD = 128
Q_LORA = 384
KV_LORA = 256
D_FF = 2816
CHUNK = 64
ROPE_THETA = 10000.0
EPS = 1e-6
NEG_INF = -1e30
ADAM_LR = 0.001
ADAM_B1 = 0.9
ADAM_B2 = 0.999
ADAM_EPS = 1e-08
ADAM_WD = 0.01
ADAM_STEP = 10

N_DEV = 8
N_CHIP = 4
FF_BLK = D_FF * 2 // N_DEV
N_FF_BLK = D_FF // FF_BLK
QK_PAD = 256
HALO = 16

TM = 1024
TS = 512
TR = 256
TQ = 512
VMEM_LIMIT = 56 * 1024 * 1024

NN = (((1,), (0,)), ((), ()))
NT = (((1,), (1,)), ((), ()))
TN = (((0,), (0,)), ((), ()))
MESH = pl.DeviceIdType.MESH


def _params(sem):
    return pltpu.CompilerParams(dimension_semantics=sem, vmem_limit_bytes=VMEM_LIMIT)


ANY_SPEC = pl.BlockSpec(memory_space=pl.ANY)
VMEM_SPEC = pl.BlockSpec(memory_space=pltpu.VMEM)


class _Chain:
    last = None


def _pallas(body, *, name, in_specs, out_specs, out_shape, grid=(), scratch_shapes=(), n_prefetch=0, aliases=None,
            params=None):
    def run(*args):
        after = _Chain.last
        n_lead = len(args)
        specs, operands, fn = list(in_specs), list(args), body
        if after is not None:
            def fn(*refs):
                return body(*refs[:n_lead], *refs[n_lead + 1:])
            specs.append(ANY_SPEC)
            operands.append(after)
        kw = dict(name=name, out_shape=out_shape, input_output_aliases=aliases or {})
        if params is not None:
            kw["compiler_params"] = params
        if n_prefetch:
            kw["grid_spec"] = pltpu.PrefetchScalarGridSpec(
                num_scalar_prefetch=n_prefetch, grid=grid, in_specs=specs, out_specs=out_specs,
                scratch_shapes=scratch_shapes)
        else:
            kw.update(grid=grid, in_specs=specs, out_specs=out_specs, scratch_shapes=scratch_shapes)
        outs = pl.pallas_call(fn, **kw)(*operands)
        _Chain.last = outs[0] if isinstance(outs, (list, tuple)) else outs
        return outs
    return run


def _mm(name, a, b, *, grid, a_spec, b_spec, o_spec, o_shape, o_dtype, dims, k_axis=None, acc_shape=None,
        add=None, add_spec=None):
    nk = grid[k_axis] if k_axis is not None else 1
    has_add = add is not None

    def body(*refs):
        a_ref, b_ref = refs[0], refs[1]
        p = 2
        add_ref = None
        if has_add:
            add_ref = refs[p]
            p += 1
        o_ref = refs[p]
        p += 1
        r = lax.dot_general(a_ref[...].astype(BF16), b_ref[...].astype(BF16), dims, preferred_element_type=F32)
        if k_axis is None:
            if has_add:
                r = r + add_ref[...].astype(F32)
            o_ref[...] = r.astype(o_dtype)
        else:
            acc = refs[p]
            k = pl.program_id(k_axis)

            @pl.when(k == 0)
            def _():
                acc[...] = r

            @pl.when(k > 0)
            def _():
                acc[...] += r

            @pl.when(k == nk - 1)
            def _():
                t = acc[...]
                if has_add:
                    t = t + add_ref[...].astype(F32)
                o_ref[...] = t.astype(o_dtype)

    in_specs = [a_spec, b_spec]
    args = [a, b]
    if has_add:
        in_specs.append(add_spec if add_spec is not None else o_spec)
        args.append(add)
    sem = tuple("arbitrary" if ax == k_axis else "parallel" for ax in range(len(grid)))
    scratch = [pltpu.VMEM(acc_shape, F32)] if k_axis is not None else []
    return _pallas(body, name=name, grid=grid, in_specs=in_specs, out_specs=o_spec,
                   out_shape=jax.ShapeDtypeStruct(o_shape, o_dtype), scratch_shapes=scratch, params=_params(sem))(*args)


def _mm_sum(name, parts, *, grid, o_spec, o_shape, o_dtype, add=None, norm_bwd=None):
    has_add = add is not None
    np_ = len(parts)
    nn = 1 if norm_bwd is None else len(norm_bwd[1])
    has_res = norm_bwd is not None and norm_bwd[2] is not None

    def body(*refs):
        accs = [None] * nn
        for p, (_, _, _, _, dims, n) in enumerate(parts):
            a_ref, b_ref = refs[2 * p], refs[2 * p + 1]
            for k in range(a_ref.shape[0]):
                r = lax.dot_general(a_ref[k], b_ref[k], dims, preferred_element_type=F32)
                accs[n] = r if accs[n] is None else accs[n] + r
        if norm_bwd is None:
            acc = accs[0]
            if has_add:
                acc = acc + refs[2 * np_][...]
            refs[-1][...] = acc.astype(o_dtype)
            return
        x_ref, g_refs = refs[2 * np_], refs[2 * np_ + 1:2 * np_ + 1 + nn]
        dx_ref, dxb_ref, dg_refs = refs[-2 - nn], refs[-1 - nn], refs[-nn:]
        xv = x_ref[...]
        r = lax.rsqrt(jnp.mean(xv * xv, axis=-1, keepdims=True) + EPS)
        xn = xv * r
        dx = refs[2 * np_ + 1 + nn][...] if has_res else None
        sums = []
        for acc, g_ref in zip(accs, g_refs):
            gdy = acc * g_ref[...]
            t = r * (gdy - xn * jnp.mean(gdy * xn, axis=-1, keepdims=True))
            dx = t if dx is None else dx + t
            sums.append(jnp.sum(acc * xn, axis=0, keepdims=True))
        dx_ref[...] = dx
        dxb_ref[...] = dx.astype(BF16)

        @pl.when(pl.program_id(0) == 0)
        def _():
            for dg_ref, part in zip(dg_refs, sums):
                dg_ref[...] = part

        @pl.when(pl.program_id(0) > 0)
        def _():
            for dg_ref, part in zip(dg_refs, sums):
                dg_ref[...] += part

    in_specs, args = [], []
    for a, a_spec, b, b_spec, _, _ in parts:
        in_specs += [a_spec, b_spec]
        args += [a, b]
    if norm_bwd is None:
        if has_add:
            in_specs.append(o_spec)
            args.append(add)
        return _pallas(body, name=name, grid=grid, in_specs=in_specs, out_specs=o_spec,
                       out_shape=jax.ShapeDtypeStruct(o_shape, o_dtype),
                       params=_params(("parallel",) * len(grid)))(*args)
    x, gains, dres = norm_bwd
    vec = pl.BlockSpec((1, o_shape[1]), lambda i: (0, 0))
    in_specs += [o_spec] + [vec] * nn + ([o_spec] if has_res else [])
    args += [x] + list(gains) + ([dres] if has_res else [])
    outs = _pallas(body, name=name, grid=grid, in_specs=in_specs, out_specs=[o_spec, o_spec] + [vec] * nn,
                   out_shape=[jax.ShapeDtypeStruct(o_shape, F32), jax.ShapeDtypeStruct(o_shape, BF16)]
                   + [jax.ShapeDtypeStruct((1, o_shape[1]), F32)] * nn,
                   params=_params(("arbitrary",)))(*args)
    return outs[0], outs[1], list(outs[2:])


def _mm_rows(name, a, b, dims, o_dtype, n_out, *, tn=None, add=None):
    k = a.shape[1]
    tn = n_out if tn is None else tn
    if dims == NN:
        b_spec = pl.BlockSpec((k, tn), lambda n, i: (0, n))
    else:
        b_spec = pl.BlockSpec((tn, k), lambda n, i: (n, 0))
    return _mm(name, a, b, grid=(n_out // tn, T // TM),
               a_spec=pl.BlockSpec((TM, k), lambda n, i: (i, 0)), b_spec=b_spec,
               o_spec=pl.BlockSpec((TM, tn), lambda n, i: (i, n)), o_shape=(T, n_out), o_dtype=o_dtype,
               dims=dims, add=add)


def _wgrads(name, jobs):
    arrays, index = [], {}
    for a, b in jobs:
        for arr in (a, b):
            if id(arr) not in index:
                index[id(arr)] = len(arrays)
                arrays.append(arr)
    n_in = len(arrays)

    def body(*refs):
        for t, (a, b) in enumerate(jobs):
            a_ref, b_ref, o_ref = refs[index[id(a)]], refs[index[id(b)]], refs[n_in + t]
            if a.ndim == 3:
                for h in range(a.shape[0]):
                    o_ref[h] = lax.dot_general(a_ref[h], b_ref[...], TN, preferred_element_type=F32).astype(BF16)
            else:
                o_ref[...] = lax.dot_general(a_ref[...], b_ref[...], TN, preferred_element_type=F32).astype(BF16)

    out_shape = [jax.ShapeDtypeStruct(a.shape[:-2] + (a.shape[-1], b.shape[-1]), BF16) for a, b in jobs]
    return _pallas(body, name=name, in_specs=[VMEM_SPEC] * n_in, out_specs=[VMEM_SPEC] * len(jobs), out_shape=out_shape,
                   params=pltpu.CompilerParams(vmem_limit_bytes=VMEM_LIMIT))(*arrays)


def _mm_wgrad(name, a, b, *, tn=512):
    k, n = a.shape[1], b.shape[1]
    tn = min(tn, n)
    return _mm(name, a, b, grid=(n // tn,),
               a_spec=pl.BlockSpec((T, k), lambda j: (0, 0)), b_spec=pl.BlockSpec((T, tn), lambda j: (0, j)),
               o_spec=pl.BlockSpec((k, tn), lambda j: (0, j)), o_shape=(k, n), o_dtype=BF16, dims=TN)


def _rms_fwd(name, x, g):
    d = x.shape[1]

    def body(x_ref, g_ref, o_ref):
        xv = x_ref[...]
        r = lax.rsqrt(jnp.mean(xv * xv, axis=-1, keepdims=True) + EPS)
        o_ref[...] = ((xv * r) * g_ref[...]).astype(BF16)

    return _pallas(
        body, name=name, grid=(T // TM,),
        in_specs=[pl.BlockSpec((TM, d), lambda i: (i, 0)), pl.BlockSpec((1, d), lambda i: (0, 0))],
        out_specs=pl.BlockSpec((TM, d), lambda i: (i, 0)),
        out_shape=jax.ShapeDtypeStruct((T, d), BF16), params=_params(("parallel",)))(x, g)


def _rms(xv, g):
    return (xv * lax.rsqrt(jnp.mean(xv * xv, axis=-1, keepdims=True) + EPS)) * g


def _rms_bwd(name, x, gains, dys, dres=None):
    d = x.shape[1]
    n = len(gains)
    has_res = dres is not None

    def body(*refs):
        x_ref, g_refs, dy_refs = refs[0], refs[1:1 + n], refs[1 + n:1 + 2 * n]
        dx_ref, dxb_ref = refs[-2 - n], refs[-1 - n]
        dg_refs = refs[-n:]
        xv = x_ref[...]
        r = lax.rsqrt(jnp.mean(xv * xv, axis=-1, keepdims=True) + EPS)
        xn = xv * r
        dx = refs[1 + 2 * n][...] if has_res else None
        parts = []
        for g_ref, dy_ref in zip(g_refs, dy_refs):
            dyv = dy_ref[...].astype(F32)
            gdy = dyv * g_ref[...]
            t = r * (gdy - xn * jnp.mean(gdy * xn, axis=-1, keepdims=True))
            dx = t if dx is None else dx + t
            parts.append(jnp.sum(dyv * xn, axis=0, keepdims=True))
        dx_ref[...] = dx
        dxb_ref[...] = dx.astype(BF16)

        @pl.when(pl.program_id(0) == 0)
        def _():
            for dg_ref, part in zip(dg_refs, parts):
                dg_ref[...] = part

        @pl.when(pl.program_id(0) > 0)
        def _():
            for dg_ref, part in zip(dg_refs, parts):
                dg_ref[...] += part

    row = pl.BlockSpec((TR, d), lambda i: (i, 0))
    vec = pl.BlockSpec((1, d), lambda i: (0, 0))
    args = [x] + list(gains) + list(dys) + ([dres] if has_res else [])
    in_specs = [row] + [vec] * n + [row] * n + ([row] if has_res else [])
    outs = _pallas(
        body, name=name, grid=(T // TR,), in_specs=in_specs, out_specs=[row, row] + [vec] * n,
        out_shape=[jax.ShapeDtypeStruct((T, d), F32), jax.ShapeDtypeStruct((T, d), BF16)]
        + [jax.ShapeDtypeStruct((1, d), F32)] * n,
        params=_params(("arbitrary",)))(*args)
    return outs[0], outs[1], list(outs[2:])


def _final(h, g, tgt):
    def body(h_ref, g_ref, t_ref, loss_ref, dh_ref, dhb_ref, dg_ref):
        hv = h_ref[...]
        r = lax.rsqrt(jnp.mean(hv * hv, axis=-1, keepdims=True) + EPS)
        xn = hv * r
        gv = g_ref[...]
        err = xn * gv - t_ref[...]
        part_loss = 0.5 * jnp.sum(jnp.mean(err * err, axis=-1, keepdims=True), axis=0, keepdims=True)
        dy = err * (1.0 / D)
        gdy = dy * gv
        dh = r * (gdy - xn * jnp.mean(gdy * xn, axis=-1, keepdims=True))
        dh_ref[...] = dh
        dhb_ref[...] = dh.astype(BF16)
        part = jnp.sum(dy * xn, axis=0, keepdims=True)
        first = pl.program_id(0) == 0

        @pl.when(first)
        def _():
            dg_ref[...] = part
            loss_ref[...] = jnp.broadcast_to(part_loss, (1, 128))

        @pl.when(jnp.logical_not(first))
        def _():
            dg_ref[...] += part
            loss_ref[...] += jnp.broadcast_to(part_loss, (1, 128))

    row = pl.BlockSpec((TR, D), lambda i: (i, 0))
    vec = pl.BlockSpec((1, D), lambda i: (0, 0))
    return _pallas(
        body, name="final_loss", grid=(T // TR,), in_specs=[row, vec, row],
        out_specs=[pl.BlockSpec((1, 128), lambda i: (0, 0)), row, row, vec],
        out_shape=[jax.ShapeDtypeStruct((1, 128), F32), jax.ShapeDtypeStruct((T, D), F32),
                   jax.ShapeDtypeStruct((T, D), BF16), jax.ShapeDtypeStruct((1, D), F32)],
        params=_params(("arbitrary",)))(h, g, tgt)


def _prev_idx(i, rows=TR):
    return jnp.maximum(i * (rows // HALO) - 1, 0)


def _next_idx(i, rows=TR):
    return jnp.minimum((i + 1) * (rows // HALO), T // HALO - 1)


def _causal_taps(ext):
    return pltpu.roll(ext, 2, 0)[HALO:], pltpu.roll(ext, 1, 0)[HALO:], ext[HALO:]


def _anticausal_taps(ext, n):
    rows = ext.shape[0]
    return pltpu.roll(ext, rows - 1, 0)[:n], pltpu.roll(ext, rows - 2, 0)[:n]


MIX_COLS = 512


def _mixer_in(hn, w_in, w):
    nc = D // MIX_COLS

    def body(h_ref, hh_ref, wb_ref, wc_ref, wu_ref, w_ref, b_ref, c_ref, u_ref, y_ref):
        i = pl.program_id(1)
        hv = h_ref[...]
        he = jnp.concatenate([hh_ref[...], hv], axis=0)
        ce = lax.dot_general(he, wc_ref[...], NN, preferred_element_type=F32).astype(BF16)
        ue = lax.dot_general(he, wu_ref[...], NN, preferred_element_type=F32).astype(BF16)
        bv = lax.dot_general(hv, wb_ref[...], NN, preferred_element_type=F32).astype(BF16)
        b_ref[...] = bv
        c_ref[...] = ce[HALO:]
        u_ref[...] = ue[HALO:]
        row = lax.broadcasted_iota(jnp.int32, (HALO + TS, 1), 0)
        cu = jnp.where(jnp.logical_or(i > 0, row >= HALO), ce.astype(F32) * ue.astype(F32), 0.0)
        x2, x1, x0 = _causal_taps(cu)
        wv = w_ref[...]
        cv = (x2 * wv[0:1] + x1 * wv[1:2]) + x0 * wv[2:3]
        y_ref[...] = (bv.astype(F32) * cv).astype(BF16)

    def cols(part):
        return pl.BlockSpec((D, MIX_COLS), lambda j, i: (0, part * nc + j))

    blk = pl.BlockSpec((TS, MIX_COLS), lambda j, i: (i, j))
    out = jax.ShapeDtypeStruct((T, D), BF16)
    return _pallas(
        body, name="l0_in", grid=(nc, T // TS),
        in_specs=[pl.BlockSpec((TS, D), lambda j, i: (i, 0)), pl.BlockSpec((HALO, D), lambda j, i: (_prev_idx(i, TS), 0)),
                  cols(0), cols(1), cols(2), pl.BlockSpec((3, MIX_COLS), lambda j, i: (0, j))],
        out_specs=[blk] * 4, out_shape=[out] * 4,
        params=_params(("parallel", "parallel")))(hn, hn, w_in, w_in, w_in, w)


def _mixer_out_bwd(dh, w_out, zb, zc, zu, w):
    last = T // TR - 1

    def body(dh_ref, dhn_ref, wo_ref, b_ref, bn_ref, c_ref, ch_ref, u_ref, uh_ref, w_ref, dz_ref, dw_ref):
        i = pl.program_id(0)
        dye = lax.dot_general(jnp.concatenate([dh_ref[...], dhn_ref[...]], axis=0), wo_ref[...], NT,
                              preferred_element_type=F32)
        cv_ = c_ref[...].astype(F32)
        uv = u_ref[...].astype(F32)
        cu = cv_ * uv
        cuh = jnp.where(i > 0, ch_ref[...].astype(F32) * uh_ref[...].astype(F32), 0.0)
        x2, x1, x0 = _causal_taps(jnp.concatenate([cuh, cu], axis=0))
        wv = w_ref[...]
        conv = (x2 * wv[0:1] + x1 * wv[1:2]) + x0 * wv[2:3]
        dyv = dye[:TR]
        dz_ref[:, 0:D] = (dyv * conv).astype(BF16)
        dconv = dyv * b_ref[...].astype(F32)
        dconv_n = jnp.where(i < last, dye[TR:] * bn_ref[...].astype(F32), 0.0)
        n1, n2 = _anticausal_taps(jnp.concatenate([dconv, dconv_n], axis=0), TR)
        dcu = (dconv * wv[2:3] + n1 * wv[1:2]) + n2 * wv[0:1]
        dz_ref[:, D:2 * D] = (dcu * uv).astype(BF16)
        dz_ref[:, 2 * D:3 * D] = (dcu * cv_).astype(BF16)
        part = jnp.concatenate([jnp.sum(dconv * x2, axis=0, keepdims=True),
                                jnp.sum(dconv * x1, axis=0, keepdims=True),
                                jnp.sum(dconv * x0, axis=0, keepdims=True)], axis=0)

        @pl.when(i == 0)
        def _():
            dw_ref[...] = part

        @pl.when(i > 0)
        def _():
            dw_ref[...] += part

    main = pl.BlockSpec((TR, D), lambda i: (i, 0))
    prev = pl.BlockSpec((HALO, D), lambda i: (_prev_idx(i), 0))
    nxt = pl.BlockSpec((HALO, D), lambda i: (_next_idx(i), 0))
    wspec = pl.BlockSpec((3, D), lambda i: (0, 0))
    return _pallas(
        body, name="d_l0_out", grid=(T // TR,),
        in_specs=[main, nxt, pl.BlockSpec((D, D), lambda i: (0, 0)), main, nxt, main, prev, main, prev, wspec],
        out_specs=[pl.BlockSpec((TR, 3 * D), lambda i: (i, 0)), wspec],
        out_shape=[jax.ShapeDtypeStruct((T, 3 * D), BF16), jax.ShapeDtypeStruct((3, D), F32)],
        params=_params(("arbitrary",)))(dh, dh, w_out, zb, zb, zc, zc, zu, zu, w)


def _sigmoid(x):
    return 0.5 * jnp.tanh(0.5 * x) + 0.5


def _ffn_up_act(name, hf, w_up, w, b):
    def body(h_ref, hh_ref, wg_ref, wv_ref, w_ref, b_ref, g_ref, v_ref, a_ref):
        i = pl.program_id(1)
        hv = h_ref[...]
        ge = lax.dot_general(jnp.concatenate([hh_ref[...], hv], axis=0), wg_ref[...], NT,
                             preferred_element_type=F32).astype(BF16)
        v = lax.dot_general(hv, wv_ref[...], NT, preferred_element_type=F32).astype(BF16)
        g_ref[...] = ge[HALO:]
        v_ref[...] = v
        ext = ge.astype(F32)
        row = lax.broadcasted_iota(jnp.int32, (HALO + TM, 1), 0)
        ext = jnp.where(jnp.logical_or(i > 0, row >= HALO), ext, 0.0)
        x2, x1, x0 = _causal_taps(ext)
        wv = w_ref[...]
        gc = ((x2 * wv[0:1] + x1 * wv[1:2]) + x0 * wv[2:3]) + b_ref[...]
        a_ref[...] = ((gc * _sigmoid(gc)) * v.astype(F32)).astype(BF16)

    blk = pl.BlockSpec((None, TM, FF_BLK), lambda j, i: (j, i, 0))
    out = jax.ShapeDtypeStruct((N_FF_BLK, T, FF_BLK), BF16)
    return _pallas(
        body, name=name, grid=(N_FF_BLK, T // TM),
        in_specs=[pl.BlockSpec((TM, D), lambda j, i: (i, 0)),
                  pl.BlockSpec((HALO, D), lambda j, i: (_prev_idx(i, TM), 0)),
                  pl.BlockSpec((None, None, FF_BLK, D), lambda j, i: (0, j, 0, 0)),
                  pl.BlockSpec((None, None, FF_BLK, D), lambda j, i: (0, j + N_FF_BLK, 0, 0)),
                  pl.BlockSpec((None, 3, FF_BLK), lambda j, i: (j, 0, 0)),
                  pl.BlockSpec((None, 1, FF_BLK), lambda j, i: (j, 0, 0))],
        out_specs=[blk, blk, blk], out_shape=[out, out, out],
        params=_params(("parallel", "parallel")))(hf, hf, w_up, w_up, w, b)


def _ffn_dact(name, dh, w_down4, g, v, w, b):
    last = T // TS - 1

    def body(dh_ref, dhn_ref, wd_ref, g_ref, gp_ref, gn_ref, v_ref, vn_ref, w_ref, b_ref, dg_ref, dv_ref, dw_ref, db_ref):
        i = pl.program_id(1)
        da = lax.dot_general(jnp.concatenate([dh_ref[...], dhn_ref[...]], axis=0), wd_ref[...], NT,
                             preferred_element_type=F32)
        row = lax.broadcasted_iota(jnp.int32, (TS + HALO, 1), 0)
        da = jnp.where(jnp.logical_or(i < last, row < TS), da, 0.0)
        gp = jnp.where(i > 0, gp_ref[...].astype(F32), 0.0)
        ext = jnp.concatenate([gp, g_ref[...].astype(F32), gn_ref[...].astype(F32)], axis=0)
        x2, x1, x0 = _causal_taps(ext)
        wv = w_ref[...]
        gc = ((x2 * wv[0:1] + x1 * wv[1:2]) + x0 * wv[2:3]) + b_ref[...]
        sg = _sigmoid(gc)
        vv = jnp.concatenate([v_ref[...].astype(F32), vn_ref[...].astype(F32)], axis=0)
        dv_ref[...] = (da[:TS] * (gc[:TS] * sg[:TS])).astype(BF16)
        dgc = (da * vv) * (sg * (1.0 + gc * (1.0 - sg)))
        n1, n2 = _anticausal_taps(dgc, TS)
        d0 = dgc[:TS]
        dg_ref[...] = ((d0 * wv[2:3] + n1 * wv[1:2]) + n2 * wv[0:1]).astype(BF16)
        part_w = jnp.concatenate([jnp.sum(d0 * x2[:TS], axis=0, keepdims=True),
                                  jnp.sum(d0 * x1[:TS], axis=0, keepdims=True),
                                  jnp.sum(d0 * x0[:TS], axis=0, keepdims=True)], axis=0)
        part_b = jnp.sum(d0, axis=0, keepdims=True)

        @pl.when(i == 0)
        def _():
            dw_ref[...] = part_w
            db_ref[...] = part_b

        @pl.when(i > 0)
        def _():
            dw_ref[...] += part_w
            db_ref[...] += part_b

    blk = pl.BlockSpec((None, TS, FF_BLK), lambda j, i: (j, i, 0))
    prev = pl.BlockSpec((None, HALO, FF_BLK), lambda j, i: (j, _prev_idx(i, TS), 0))
    nxt = pl.BlockSpec((None, HALO, FF_BLK), lambda j, i: (j, _next_idx(i, TS), 0))
    wspec = pl.BlockSpec((None, 3, FF_BLK), lambda j, i: (j, 0, 0))
    bspec = pl.BlockSpec((None, 1, FF_BLK), lambda j, i: (j, 0, 0))
    return _pallas(
        body, name=name, grid=(N_FF_BLK, T // TS),
        in_specs=[pl.BlockSpec((TS, D), lambda j, i: (i, 0)),
                  pl.BlockSpec((HALO, D), lambda j, i: (_next_idx(i, TS), 0)),
                  pl.BlockSpec((None, None, FF_BLK, D), lambda j, i: (0, j, 0, 0)),
                  blk, prev, nxt, blk, nxt, wspec, bspec],
        out_specs=[blk, blk, wspec, bspec],
        out_shape=[jax.ShapeDtypeStruct((N_FF_BLK, T, FF_BLK), BF16), jax.ShapeDtypeStruct((N_FF_BLK, T, FF_BLK), BF16),
                   jax.ShapeDtypeStruct((N_FF_BLK, 3, FF_BLK), F32), jax.ShapeDtypeStruct((N_FF_BLK, 1, FF_BLK), F32)],
        params=_params(("parallel", "arbitrary")))(dh, dh, w_down4, g, g, g, v, v, w, b)


def _rope_tables(pos, inv_freq):
    half = QK_ROPE // 2

    def body(p_ref, f_ref, c_ref, sa_ref, sb_ref):
        ang = p_ref[...].astype(F32) * f_ref[...]
        lane = lax.broadcasted_iota(jnp.int32, (T, 128), 1)
        c = jnp.cos(ang)
        s = jnp.sin(ang)
        c_ref[...] = jnp.where(lane < 2 * half, c, 0.0)
        sa_ref[...] = jnp.where(lane < half, -s, 0.0)
        sb_ref[...] = jnp.where(jnp.logical_and(lane >= half, lane < 2 * half), s, 0.0)

    return _pallas(
        body, name="rope_tables", in_specs=[VMEM_SPEC] * 2, out_specs=[VMEM_SPEC] * 3,
        out_shape=[jax.ShapeDtypeStruct((T, 128), F32)] * 3,
        params=pltpu.CompilerParams(vmem_limit_bytes=VMEM_LIMIT))(pos, inv_freq)


def _rotate(r, c, sa, sb, sign):
    return r * c + sign * (pltpu.roll(r, 96, 1) * sa + pltpu.roll(r, 32, 1) * sb)


def _attn_pre(h2, g_kv, g_l1, g_kvl, g_ql, w_dkv, w_kr, w_uk, w_uv, w_dq, w_uq, tables):
    def body(h_ref, c_ref, sa_ref, sb_ref, gkv_ref, gl1_ref, gkvl_ref, gql_ref, wdkv_ref, wkr_ref, wuk_ref, wuv_ref,
             wdq_ref, wuq_ref, hk_ref, hn_ref, ckvr_ref, ckv_ref, kr_ref, kn_ref, v_ref, cqr_ref, cq_ref, q_ref):
        xv = h_ref[...]
        xn = xv * lax.rsqrt(jnp.mean(xv * xv, axis=-1, keepdims=True) + EPS)
        hk = (xn * gkv_ref[...]).astype(BF16)
        hn = (xn * gl1_ref[...]).astype(BF16)
        hk_ref[...] = hk
        hn_ref[...] = hn
        cv, sav, sbv = c_ref[...], sa_ref[...], sb_ref[...]
        raw = lax.dot_general(hk, wdkv_ref[...], NN, preferred_element_type=F32)
        ckvr_ref[...] = raw
        ckv = _rms(raw, gkvl_ref[...]).astype(BF16)
        ckv_ref[...] = ckv
        kr = lax.dot_general(hk, wkr_ref[...], NT, preferred_element_type=F32)
        kr_ref[...] = _rotate(kr, cv, sav, sbv, 1.0).astype(BF16)
        kn_ref[...] = lax.dot_general(ckv, wuk_ref[...], NN, preferred_element_type=F32).astype(BF16)
        v_ref[...] = lax.dot_general(ckv, wuv_ref[...], NN, preferred_element_type=F32).astype(BF16)
        cqr = lax.dot_general(hn, wdq_ref[...], NN, preferred_element_type=F32)
        cqr_ref[...] = cqr
        cq = _rms(cqr, gql_ref[...]).astype(BF16)
        cq_ref[...] = cq
        for h in range(N_HEADS):
            r = lax.dot_general(cq, wuq_ref[h], NT, preferred_element_type=F32)
            q_ref[h, :, :QK_NOPE] = (r[:, :QK_NOPE] * SCALE2).astype(BF16)
            q_ref[h, :, QK_NOPE:] = (_rotate(r[:, QK_NOPE:], cv, sav, sbv, 1.0) * SCALE2).astype(BF16)

    def rows(d):
        return pl.BlockSpec((TS, d), lambda i: (i, 0))

    def whole(a):
        return pl.BlockSpec(a.shape, lambda i: (0,) * a.ndim)

    wholes = [g_kv, g_l1, g_kvl, g_ql, w_dkv, w_kr, w_uk, w_uv, w_dq, w_uq]
    outs = [(D, BF16), (D, BF16), (KV_LORA, F32), (KV_LORA, BF16), (128, BF16), (N_HEADS * QK_NOPE, BF16),
            (N_HEADS * V_HEAD, BF16), (Q_LORA, F32), (Q_LORA, BF16)]
    return _pallas(
        body, name="attn_pre", grid=(T // TS,),
        in_specs=[rows(D), rows(128), rows(128), rows(128)] + [whole(a) for a in wholes],
        out_specs=[rows(d) for d, _ in outs] + [pl.BlockSpec((N_HEADS, TS, QK_PAD), lambda i: (0, i, 0))],
        out_shape=[jax.ShapeDtypeStruct((T, d), dt) for d, dt in outs]
        + [jax.ShapeDtypeStruct((N_HEADS, T, QK_PAD), BF16)],
        params=_params(("parallel",)))(h2, *tables, *wholes)


SCALE = (QK_NOPE + QK_ROPE) ** -0.5
LOG2E = 1.4426950408889634
SCALE2 = SCALE * LOG2E


def _diag_mask(transposed):
    shift = CHUNK.bit_length() - 1
    a = lax.broadcasted_iota(jnp.int32, (TQ, TQ), 0) >> shift
    b = lax.broadcasted_iota(jnp.int32, (TQ, TQ), 1) >> shift
    return (a <= b) if transposed else (b <= a)


def _as_row(col):
    return jnp.transpose(jnp.broadcast_to(col, (col.shape[0], 128)), (1, 0))[0:1]


def _keys(kn_ref, kr_ref, off):
    return jnp.concatenate([kn_ref[pl.ds(off, TQ), :], kr_ref[pl.ds(off, TQ), :]], axis=1)


def _attn_fwd(q, kn, kr, v):
    hp = 2

    def body(q_ref, kn_ref, kr_ref, v_ref, o_ref, lse_ref):
        i = pl.program_id(1)
        qs = [q_ref[a] for a in range(hp)]

        def step(j, carry, masked):
            off = pl.multiple_of(j * TQ, TQ)
            krv = kr_ref[pl.ds(off, TQ), :]
            ss = []
            for a in range(hp):
                kk = jnp.concatenate([kn_ref[pl.ds(off, TQ), a * QK_NOPE:(a + 1) * QK_NOPE], krv], axis=1)
                ss.append(lax.dot_general(qs[a], kk, NT, preferred_element_type=F32))
            out = []
            for a in range(hp):
                m, l, acc = carry[a]
                s = ss[a]
                if masked:
                    s = jnp.where(_diag_mask(False), s, NEG_INF)
                m_new = jnp.maximum(m, jnp.max(s, axis=-1, keepdims=True))
                p = jnp.exp2(s - m_new)
                alpha = jnp.exp2(m - m_new)
                l = alpha * l + jnp.sum(p, axis=-1, keepdims=True)
                pv = lax.dot_general(p.astype(BF16), v_ref[pl.ds(off, TQ), a * V_HEAD:(a + 1) * V_HEAD], NN,
                                     preferred_element_type=F32)
                out.append((m_new, l, alpha * acc + pv))
            return tuple(out)

        one = (jnp.full((TQ, 1), NEG_INF, F32), jnp.zeros((TQ, 1), F32), jnp.zeros((TQ, V_HEAD), F32))
        carry = lax.fori_loop(0, i, lambda j, cr: step(j, cr, False), (one,) * hp)
        carry = step(i, carry, True)
        for a, (m, l, acc) in enumerate(carry):
            o_ref[:, a * V_HEAD:(a + 1) * V_HEAD] = (acc / l).astype(BF16)
            lse_ref[a] = _as_row(m + jnp.log(l) * LOG2E)

    return _pallas(
        body, name="attn_fwd", grid=(N_HEADS // hp, T // TQ),
        in_specs=[pl.BlockSpec((hp, TQ, QK_PAD), lambda h, i: (h, i, 0)),
                  pl.BlockSpec((T, hp * QK_NOPE), lambda h, i: (0, h)),
                  pl.BlockSpec((T, 128), lambda h, i: (0, 0)),
                  pl.BlockSpec((T, hp * V_HEAD), lambda h, i: (0, h))],
        out_specs=[pl.BlockSpec((TQ, hp * V_HEAD), lambda h, i: (i, h)), pl.BlockSpec((hp, 1, TQ), lambda h, i: (h, 0, i))],
        out_shape=[jax.ShapeDtypeStruct((T, N_HEADS * V_HEAD), BF16), jax.ShapeDtypeStruct((N_HEADS, 1, T), F32)],
        params=_params(("parallel", "parallel")))(q, kn, kr, v)


def _attn_bwd(q, kn, kr, v, o, do, lse_row, tables):
    nq = T // TQ
    hp = 2
    cos, sa, sb = tables

    def body(q_ref, kn_ref, kr_ref, v_ref, o_ref, do_ref, lse_ref, c_ref, sa_ref, sb_ref,
             dq_ref, dkn_ref, dkr_ref, dv_ref, dq_acc, dl_ref):
        j = pl.program_id(1)

        def cols(a):
            return slice(a * 128, (a + 1) * 128)

        @pl.when(j == 0)
        def _():
            dq_acc[...] = jnp.zeros_like(dq_acc)
            for a in range(hp):
                for i in range(nq):
                    rows = pl.ds(i * TQ, TQ)
                    prod = do_ref[rows, cols(a)].astype(F32) * o_ref[rows, cols(a)].astype(F32)
                    dl_ref[a, :, rows] = _as_row(jnp.sum(prod, axis=-1, keepdims=True))

        krv = kr_ref[...]
        kks = [jnp.concatenate([kn_ref[:, cols(a)], krv], axis=1) for a in range(hp)]
        vvs = [v_ref[:, cols(a)] for a in range(hp)]

        def step(i, carry, masked):
            off = pl.multiple_of(i * TQ, TQ)
            rows = pl.ds(off, TQ)
            qis = [q_ref[a, rows, :] for a in range(hp)]
            dois = [do_ref[rows, cols(a)] for a in range(hp)]
            sts = [lax.dot_general(kks[a], qis[a], NT, preferred_element_type=F32) for a in range(hp)]
            dpts = [lax.dot_general(vvs[a], dois[a], NT, preferred_element_type=F32) for a in range(hp)]
            out = []
            for a in range(hp):
                dk, dv = carry[a]
                st = sts[a]
                if masked:
                    st = jnp.where(_diag_mask(True), st, NEG_INF)
                pt = jnp.exp2(st - lse_ref[a, :, rows])
                dv = dv + lax.dot_general(pt.astype(BF16), dois[a], NN, preferred_element_type=F32)
                dst = (pt * (dpts[a] - dl_ref[a, :, rows])).astype(BF16)
                dk = dk + lax.dot_general(dst, qis[a], NN, preferred_element_type=F32)
                dq_acc[a, rows, :] += lax.dot_general(dst, kks[a], TN, preferred_element_type=F32)
                out.append((dk, dv))
            return tuple(out)

        zero = (jnp.zeros((TQ, QK_PAD), F32), jnp.zeros((TQ, V_HEAD), F32))
        carry = step(j, (zero,) * hp, True)
        carry = lax.fori_loop(j + 1, nq, lambda i, cr: step(i, cr, False), carry)
        for a, (dk, dv) in enumerate(carry):
            dk = dk * (SCALE / SCALE2)
            dkn_ref[:, cols(a)] = dk[:, :QK_NOPE].astype(BF16)
            dkr_ref[a] = dk[:, QK_NOPE:]
            dv_ref[:, cols(a)] = dv.astype(BF16)

        @pl.when(j == nq - 1)
        def _():
            for a in range(hp):
                dq = dq_acc[a] * SCALE
                dq_ref[a, :, :QK_NOPE] = dq[:, :QK_NOPE].astype(BF16)
                dq_ref[a, :, QK_NOPE:] = _rotate(dq[:, QK_NOPE:], c_ref[...], sa_ref[...], sb_ref[...], -1.0).astype(BF16)

    row = pl.BlockSpec((hp, 1, T), lambda h, j: (h, 0, 0))
    head = pl.BlockSpec((TQ, hp * 128), lambda h, j: (j, h))
    whole = pl.BlockSpec((hp, T, QK_PAD), lambda h, j: (h, 0, 0))
    tab = pl.BlockSpec((T, 128), lambda h, j: (0, 0))
    heads = pl.BlockSpec((T, hp * V_HEAD), lambda h, j: (0, h))
    return _pallas(
        body, name="attn_bwd", grid=(N_HEADS // hp, nq),
        in_specs=[whole, head, pl.BlockSpec((TQ, 128), lambda h, j: (j, 0)), head, heads, heads, row, tab, tab, tab],
        out_specs=[whole, head, pl.BlockSpec((hp, TQ, 128), lambda h, j: (h, j, 0)), head],
        out_shape=[jax.ShapeDtypeStruct((N_HEADS, T, QK_PAD), BF16), jax.ShapeDtypeStruct((T, N_HEADS * QK_NOPE), BF16),
                   jax.ShapeDtypeStruct((N_HEADS, T, 128), F32), jax.ShapeDtypeStruct((T, N_HEADS * V_HEAD), BF16)],
        scratch_shapes=[pltpu.VMEM((hp, T, QK_PAD), F32), pltpu.VMEM((hp, 1, T), F32)],
        params=_params(("parallel", "arbitrary")))(q, kn, kr, v, o, do, lse_row, cos, sa, sb)


def _rms_bwd_math(xv, g, dy):
    r = lax.rsqrt(jnp.mean(xv * xv, axis=-1, keepdims=True) + EPS)
    xn = xv * r
    gdy = dy * g
    return r * (gdy - xn * jnp.mean(gdy * xn, axis=-1, keepdims=True)), jnp.sum(dy * xn, axis=0, keepdims=True)


def _attn_post(dq, dkn, dv, dkr, cq_raw, ckv_raw, h2, dres, g_ql, g_kvl, g_l1, g_kv, w_uq, w_uk, w_uv, w_dq, w_dkv,
               w_kr, tables):
    def body(dq_ref, dkn_ref, dv_ref, dkr_ref, cqr_ref, ckvr_ref, h_ref, res_ref, c_ref, sa_ref, sb_ref,
             gql_ref, gkvl_ref, gl1_ref, gkv_ref, wuq_ref, wuk_ref, wuv_ref, wdq_ref, wdkv_ref, wkr_ref,
             dcq_ref, dckv_ref, dkrr_ref, dh_ref, dhb_ref, dgql_ref, dgkvl_ref, dgl1_ref, dgkv_ref):
        dcq = lax.dot_general(dq_ref[0], wuq_ref[0], NN, preferred_element_type=F32)
        for h in range(1, N_HEADS):
            dcq = dcq + lax.dot_general(dq_ref[h], wuq_ref[h], NN, preferred_element_type=F32)
        dcq_raw, s_ql = _rms_bwd_math(cqr_ref[...], gql_ref[...], dcq)
        dcq_raw = dcq_raw.astype(BF16)
        dcq_ref[...] = dcq_raw
        dckv = (lax.dot_general(dkn_ref[...], wuk_ref[...], NT, preferred_element_type=F32)
                + lax.dot_general(dv_ref[...], wuv_ref[...], NT, preferred_element_type=F32))
        dckv_raw, s_kvl = _rms_bwd_math(ckvr_ref[...], gkvl_ref[...], dckv)
        dckv_raw = dckv_raw.astype(BF16)
        dckv_ref[...] = dckv_raw
        dkr = dkr_ref[0]
        for h in range(1, N_HEADS):
            dkr = dkr + dkr_ref[h]
        dkr_raw = _rotate(dkr, c_ref[...], sa_ref[...], sb_ref[...], -1.0).astype(BF16)
        dkrr_ref[...] = dkr_raw
        d_hn = lax.dot_general(dcq_raw, wdq_ref[...], NT, preferred_element_type=F32)
        d_hk = (lax.dot_general(dckv_raw, wdkv_ref[...], NT, preferred_element_type=F32)
                + lax.dot_general(dkr_raw, wkr_ref[...], NN, preferred_element_type=F32))
        xv = h_ref[...]
        r = lax.rsqrt(jnp.mean(xv * xv, axis=-1, keepdims=True) + EPS)
        xn = xv * r
        dx = res_ref[...]
        sums = [s_ql, s_kvl]
        for dy, g_ref in ((d_hn, gl1_ref), (d_hk, gkv_ref)):
            gdy = dy * g_ref[...]
            dx = dx + r * (gdy - xn * jnp.mean(gdy * xn, axis=-1, keepdims=True))
            sums.append(jnp.sum(dy * xn, axis=0, keepdims=True))
        dh_ref[...] = dx
        dhb_ref[...] = dx.astype(BF16)
        dg_refs = (dgql_ref, dgkvl_ref, dgl1_ref, dgkv_ref)

        @pl.when(pl.program_id(0) == 0)
        def _():
            for dg_ref, part in zip(dg_refs, sums):
                dg_ref[...] = part

        @pl.when(pl.program_id(0) > 0)
        def _():
            for dg_ref, part in zip(dg_refs, sums):
                dg_ref[...] += part

    def rows(d):
        return pl.BlockSpec((TS, d), lambda i: (i, 0))

    def heads(d):
        return pl.BlockSpec((N_HEADS, TS, d), lambda i: (0, i, 0))

    def whole(a):
        return pl.BlockSpec(a.shape, lambda i: (0,) * a.ndim)

    wholes = [g_ql, g_kvl, g_l1, g_kv, w_uq, w_uk, w_uv, w_dq, w_dkv, w_kr]
    vecs = [Q_LORA, KV_LORA, D, D]
    return _pallas(
        body, name="attn_post", grid=(T // TS,),
        in_specs=[heads(QK_PAD), rows(N_HEADS * QK_NOPE), rows(N_HEADS * V_HEAD), heads(128), rows(Q_LORA),
                  rows(KV_LORA), rows(D), rows(D), rows(128), rows(128), rows(128)] + [whole(a) for a in wholes],
        out_specs=[rows(Q_LORA), rows(KV_LORA), rows(128), rows(D), rows(D)]
        + [pl.BlockSpec((1, d), lambda i: (0, 0)) for d in vecs],
        out_shape=[jax.ShapeDtypeStruct((T, Q_LORA), BF16), jax.ShapeDtypeStruct((T, KV_LORA), BF16),
                   jax.ShapeDtypeStruct((T, 128), BF16), jax.ShapeDtypeStruct((T, D), F32),
                   jax.ShapeDtypeStruct((T, D), BF16)] + [jax.ShapeDtypeStruct((1, d), F32) for d in vecs],
        params=_params(("arbitrary",)))(dq, dkn, dv, dkr, cq_raw, ckv_raw, h2, dres, *tables, *wholes)


def _ffn_gup(name, dg, dv, hf):
    def body(dg_ref, dv_ref, hf_ref, o_ref):
        j = pl.program_id(0)

        @pl.when(j < N_FF_BLK)
        def _():
            o_ref[...] = lax.dot_general(dg_ref[...], hf_ref[...], TN, preferred_element_type=F32).astype(BF16)

        @pl.when(j >= N_FF_BLK)
        def _():
            o_ref[...] = lax.dot_general(dv_ref[...], hf_ref[...], TN, preferred_element_type=F32).astype(BF16)

    return _pallas(
        body, name=name, grid=(N_DEV,),
        in_specs=[pl.BlockSpec((None, T, FF_BLK), lambda j: (jnp.minimum(j, N_FF_BLK - 1), 0, 0)),
                  pl.BlockSpec((None, T, FF_BLK), lambda j: (jnp.maximum(j - N_FF_BLK, 0), 0, 0)),
                  pl.BlockSpec((T, D), lambda j: (0, 0))],
        out_specs=pl.BlockSpec((None, FF_BLK, D), lambda j: (j, 0, 0)),
        out_shape=jax.ShapeDtypeStruct((N_DEV, FF_BLK, D), BF16), params=_params(("parallel",)))(dg, dv, hf)


def _ffn_layer_fwd(tag, h, gain, ex):
    hf = _rms_fwd(f"{tag}_norm", h, gain)
    g, v, act = _ffn_up_act(f"{tag}_up", hf, ex.need(f"ffn_w_up{tag[1]}", hf), ex.need(f"ffn_cw{tag[1]}", hf),
                            ex.need(f"ffn_cb{tag[1]}", hf))
    ex.at(f"{tag}_up", act)
    rows = pl.BlockSpec((TS, D), lambda i: (i, 0))
    out = _mm_sum(f"{tag}_down",
                  [(act, pl.BlockSpec((N_FF_BLK, TS, FF_BLK), lambda i: (0, i, 0)), ex.need(f"ffn_w_down{tag[1]}", act),
                    pl.BlockSpec((None, N_FF_BLK, FF_BLK, D), lambda i: (0, 0, 0, 0)), NN, 0)],
                  grid=(T // TS,), o_spec=rows, o_shape=(T, D), o_dtype=F32, add=h)
    ex.at(f"{tag}_down", out)
    return out, (hf, g, v, act)


def _ffn_layer_bwd(tag, h, gain, ex, saved, dh, dh_bf):
    hf, g, v, act = saved
    layer = tag[1]
    w_up, w_down4 = ex.need(f"ffn_w_up{layer}", dh_bf), ex.need(f"ffn_w_down{layer}", dh_bf)
    dg, dv, dcw, dcb = _ffn_dact(f"{tag}_dact", dh_bf, w_down4, g, v, ex.need(f"ffn_cw{layer}", dh_bf),
                                 ex.need(f"ffn_cb{layer}", dh_bf))
    ex.at(f"{tag}_dact", dg)
    g_down = _mm(f"{tag}_gdown", act, dh_bf, grid=(N_FF_BLK,),
                 a_spec=pl.BlockSpec((None, T, FF_BLK), lambda j: (j, 0, 0)),
                 b_spec=pl.BlockSpec((T, D), lambda j: (0, 0)),
                 o_spec=pl.BlockSpec((FF_BLK, D), lambda j: (j, 0)),
                 o_shape=(D_FF, D), o_dtype=BF16, dims=TN)
    g_up = _ffn_gup(f"{tag}_gup", dg, dv, hf)
    ex.grad("ffn_w_up", int(layer), g_up.reshape(1, N_DEV, FF_BLK, D))
    ex.grad("ffn_w_down", int(layer), g_down.reshape(1, N_DEV, D_FF // N_DEV, D))
    ex.at(f"{tag}_gup", g_up)
    part = pl.BlockSpec((N_FF_BLK, TR, FF_BLK), lambda i: (0, i, 0))
    dh_in, dh_in_bf, dgain = _mm_sum(
        f"{tag}_dhf",
        [(dg, part, w_up, pl.BlockSpec((None, N_FF_BLK, FF_BLK, D), lambda i: (0, 0, 0, 0)), NN, 0),
         (dv, part, w_up, pl.BlockSpec((None, N_FF_BLK, FF_BLK, D), lambda i: (0, 1, 0, 0)), NN, 0)],
        grid=(T // TR,), o_spec=pl.BlockSpec((TR, D), lambda i: (i, 0)), o_shape=(T, D), o_dtype=F32,
        norm_bwd=(h, [gain], dh))
    ex.at(f"{tag}_dhf", dh_in)
    return dh_in, dh_in_bf, dgain[0], dcw, dcb


def _local_step(x, pos, tgt, rep, ex):
    attn_norm, ffn_norm, final_norm = rep["attn_norm"], rep["ffn_norm"], rep["final_norm"]
    half = QK_ROPE // 2
    inv = 1.0 / (ROPE_THETA ** (jnp.arange(half, dtype=F32) / half))
    inv_freq = jnp.concatenate([inv, inv, jnp.zeros((128 - 2 * half,), F32)]).reshape(1, 128)
    tables = _rope_tables(pos, inv_freq)

    hn0 = _rms_fwd("l0_norm", x, attn_norm[0:1])
    ex.at("l0_norm", hn0)
    w_in = ex.need("sc_w_in", hn0)
    zb, zc, zu, y = _mixer_in(hn0, w_in, ex.need("sc_conv_w", hn0))
    ex.at("l0_in", y)
    h1 = _mm_rows("l0_out", y, ex.need("sc_w_out", y), NN, F32, D, tn=512, add=x)
    ex.at("l0_out", h1)
    h2, ffn0 = _ffn_layer_fwd("f0", h1, ffn_norm[0:1], ex)

    w_uq = ex.need("w_uq", h2)
    hk, hn1, ckv_raw, ckv, kr, kn, vv, cq_raw, cq, q = _attn_pre(
        h2, rep["kv_in_norm"], attn_norm[1:2], rep["kv_latent_norm"], rep["q_latent_norm"], ex.need("w_dkv", h2),
        ex.need("w_kr", h2), ex.need("w_uk", h2), ex.need("w_uv", h2), ex.need("w_dq", h2), w_uq, tables)

    o, lse = _attn_fwd(q, kn, kr, vv)
    ex.at("attn_fwd", o)
    w_o = ex.need("w_o", o)
    h3 = _mm_rows("attn_out", o, w_o, NN, F32, D, tn=512, add=h2)
    h4, ffn1 = _ffn_layer_fwd("f1", h3, ffn_norm[1:2], ex)

    loss, dh4, dh4_bf, d_final = _final(h4, final_norm.reshape(1, D), tgt)

    dh3, dh3_bf, d_fn1, dcw1, dcb1 = _ffn_layer_bwd("f1", h3, ffn_norm[1:2], ex, ffn1, dh4, dh4_bf)
    ex.at("f1_bwd", dh3)

    do = _mm_rows("d_attn_out", dh3_bf, w_o, NT, BF16, N_HEADS * V_HEAD)
    dq_pre, dkn, dkr, dvv = _attn_bwd(q, kn, kr, vv, o, do, lse, tables)

    dcq_raw_bf, dckv_raw_bf, dkr_raw_bf, dh2, dh2_bf, d_qln, d_kvln, d_an1, d_kvin = _attn_post(
        dq_pre, dkn, dvv, dkr, cq_raw, ckv_raw, h2, dh3, rep["q_latent_norm"], rep["kv_latent_norm"], attn_norm[1:2],
        rep["kv_in_norm"], w_uq, ex.need("w_uk", dkn), ex.need("w_uv", dvv), ex.need("w_dq", dq_pre),
        ex.need("w_dkv", dkn), ex.need("w_kr", dkr), tables)
    g_uq, g_dq, g_o = _wgrads("g_q", [(dq_pre, cq), (hn1, dcq_raw_bf), (o, dh3_bf)])
    ex.grad("w_uq", None, g_uq[:, :QK_NOPE + QK_ROPE].reshape(1, N_DEV, QK_NOPE + QK_ROPE, Q_LORA))
    ex.grad("w_dq", None, g_dq.reshape(1, N_DEV, D // N_DEV, Q_LORA))
    ex.grad("w_o", None, g_o.reshape(1, N_DEV, D // N_DEV, D))

    g_uk, g_uv, g_dkv, g_kr = _wgrads("g_kv", [(ckv, dkn), (ckv, dvv), (hk, dckv_raw_bf), (dkr_raw_bf, hk)])
    ex.grad("w_uk", None, g_uk)
    ex.grad("w_uv", None, g_uv)
    ex.grad("w_dkv", None, g_dkv.reshape(1, N_DEV, D // N_DEV, KV_LORA))
    ex.grad("w_kr", None, g_kr[:QK_ROPE])
    ex.at("kv_bwd", dh2)

    dh1, dh1_bf, d_fn0, dcw0, dcb0 = _ffn_layer_bwd("f0", h1, ffn_norm[0:1], ex, ffn0, dh2, dh2_bf)
    ex.at("f0_bwd", dh1)

    ex.grad("sc_w_out", None, _mm_wgrad("g_sc_w_out", y, dh1_bf).reshape(1, N_DEV, D // N_DEV, D))
    dz, d_scw = _mixer_out_bwd(dh1_bf, ex.need("sc_w_out", dh1_bf), zb, zc, zu, ex.need("sc_conv_w", dh1_bf))
    g_in = _mm_wgrad("g_sc_w_in", hn0, dz)
    ex.grad("sc_w_in", None, g_in)
    ex.at("sc_bwd", g_in)
    ex.at("d_l0_in", g_in)
    w_in = ex.need("sc_w_in", dz)
    grad_x, _, (d_an0,) = _mm_sum(
        "d_l0_in", [(dz[None], pl.BlockSpec((1, TS, dz.shape[1]), lambda i: (0, i, 0)),
                     w_in[None], pl.BlockSpec((1,) + w_in.shape, lambda i: (0, 0, 0)), NT, 0)],
        norm_bwd=(x, [attn_norm[0:1]], dh1),
        grid=(T // TS,), o_spec=pl.BlockSpec((TS, D), lambda i: (i, 0)), o_shape=(T, D), o_dtype=F32)

    small = {
        "attn_norm": jnp.concatenate([d_an0, d_an1], axis=0),
        "ffn_norm": jnp.concatenate([d_fn0, d_fn1], axis=0),
        "final_norm": d_final.reshape(D),
        "kv_in_norm": d_kvin.reshape(D),
        "kv_latent_norm": d_kvln.reshape(KV_LORA),
        "q_latent_norm": d_qln,
        "ffn_conv_b": jnp.stack([dcb0, dcb1]).transpose(0, 2, 1, 3).reshape(2, D_FF),
        "sc_conv_w": d_scw,
        "ffn_conv_w": jnp.stack([dcw0, dcw1]).transpose(0, 2, 1, 3).reshape(2, 3, D_FF),
    }
    return loss, grad_x, small


def _place():
    return lax.axis_index("x"), lax.axis_index("y"), lax.axis_index("c")


def _peers():
    x, y, c = _place()
    return (x, y, 1 - c), [(1 - x, y), (x, 1 - y), (1 - x, 1 - y)]


def _window(ref, kind, dev):
    if kind == "blocked":
        return ref.at[:, dev]
    width = ref.shape[-1] // N_DEV
    return ref.at[:, pl.ds(pl.multiple_of(dev * width, 128), width)]


HBM_SPEC = pl.BlockSpec(memory_space=pltpu.HBM)
SEM_SPEC = pl.BlockSpec(memory_space=pltpu.SEMAPHORE)
EFFECT = pltpu.SideEffectType.DATAFLOW_SIDE_EFFECTING
TOKEN = jax.ShapeDtypeStruct((8, 128), F32)


def _hbm(a):
    return pltpu.with_memory_space_constraint(a, pltpu.HBM)


def _copies_start(name, jobs):
    nj = len(jobs)
    counts = [(len(srcs), len(lands)) for srcs, lands, _, _ in jobs]
    n_arr = sum(ns + nl for ns, nl in counts)

    def body(*refs):
        sems, token = refs[n_arr:n_arr + 2 * nj], refs[-1]
        at = 0
        for j, ((ns, nl), (_, _, ncopy, plan)) in enumerate(zip(counts, jobs)):
            copies = plan(refs[at:at + ns], refs[at + ns:at + ns + nl])
            assert len(copies) == ncopy
            for k, (sent, dst, to, _) in enumerate(copies):
                pltpu.make_async_remote_copy(src_ref=sent, dst_ref=dst, send_sem=sems[2 * j].at[k],
                                             recv_sem=sems[2 * j + 1].at[k], device_id=to, device_id_type=MESH).start()
            at += ns + nl
        token[...] = jnp.zeros_like(token)

    arrays = [a for srcs, lands, _, _ in jobs for a in list(srcs) + list(lands)]
    sem_shapes = [pltpu.SemaphoreType.DMA((ncopy,)) for _, _, ncopy, _ in jobs for _ in range(2)]
    outs = pl.pallas_call(
        body, name=name, in_specs=[HBM_SPEC] * n_arr,
        out_specs=[SEM_SPEC] * (2 * nj) + [HBM_SPEC] * n_arr + [VMEM_SPEC],
        out_shape=sem_shapes + [pltpu.HBM(a.shape, a.dtype) for a in arrays] + [TOKEN],
        input_output_aliases={i: 2 * nj + i for i in range(n_arr)},
        compiler_params=pltpu.CompilerParams(has_side_effects=EFFECT))(*[_hbm(a) for a in arrays])
    _Chain.last = outs[-1]
    flights, at = [], 2 * nj
    for j, (ns, nl) in enumerate(counts):
        flights.append((outs[2 * j], outs[2 * j + 1], list(outs[at:at + ns]), list(outs[at + ns:at + ns + nl])))
        at += ns + nl
    return flights


def _copies_wait(name, started, ncopy, plan):
    send, recv, srcs, lands = started
    ns, nl = len(srcs), len(lands)

    def body(*refs):
        send_ref, recv_ref, token = refs[ns + nl], refs[ns + nl + 1], refs[-1]
        copies = plan(refs[:ns], refs[ns:ns + nl])
        assert len(copies) == ncopy
        for k, (sent, _, to, landed) in enumerate(copies):
            cp = pltpu.make_async_remote_copy(src_ref=sent, dst_ref=landed, send_sem=send_ref.at[k],
                                              recv_sem=recv_ref.at[k], device_id=to, device_id_type=MESH)
            cp.wait_send()
            cp.wait_recv()
        token[...] = jnp.zeros_like(token)

    arrays = list(srcs) + list(lands)
    outs = pl.pallas_call(
        body, name=name, in_specs=[HBM_SPEC] * (ns + nl) + [SEM_SPEC] * 2 + [ANY_SPEC],
        out_specs=[HBM_SPEC] * (ns + nl) + [VMEM_SPEC], out_shape=[pltpu.HBM(a.shape, a.dtype) for a in arrays] + [TOKEN],
        input_output_aliases={i: i for i in range(ns + nl)},
        compiler_params=pltpu.CompilerParams(has_side_effects=EFFECT))(*arrays, send, recv, _Chain.last)
    _Chain.last = outs[-1]
    return list(outs[:ns]), list(outs[ns:-1])


def _plan_gather_chips(kinds):
    def plan(srcs, lands):
        x, y, c = _place()
        sibling, chips = _peers()
        out = []
        for t, kind in enumerate(kinds):
            mine = _window(lands[t], kind, 4 * x + 2 * y + c)
            out.append((srcs[t], mine, (x, y, c), mine))
            out.append((srcs[t], mine, sibling, _window(lands[t], kind, 4 * x + 2 * y + 1 - c)))
            for px, py in chips:
                out.append((srcs[t], mine, (px, py, c), _window(lands[t], kind, 4 * px + 2 * py + c)))
        return out
    return plan, 5 * len(kinds)


def _plan_gather_all(n):
    def plan(srcs, lands):
        x, y, c = _place()
        out = []
        for t in range(n):
            mine = lands[t].at[:, 4 * x + 2 * y + c]
            for m in range(N_DEV):
                px, py, pc = (1 - x if m & 4 else x), (1 - y if m & 2 else y), (1 - c if m & 1 else c)
                out.append((srcs[t], mine, (px, py, pc), lands[t].at[:, 4 * px + 2 * py + pc]))
        return out
    return plan, N_DEV * n


def _plan_gather_sibling(kinds):
    def plan(srcs, lands):
        _, _, c = _place()
        sibling, chips = _peers()
        out = []
        for t, kind in enumerate(kinds):
            for px, py in chips:
                w = _window(lands[t], kind, 4 * px + 2 * py + c)
                out.append((w, w, sibling, _window(lands[t], kind, 4 * px + 2 * py + 1 - c)))
        return out
    return plan, 3 * len(kinds)


def _plan_scatter_sibling(kinds):
    def plan(srcs, lands):
        _, _, c = _place()
        sibling, _ = _peers()
        out = []
        for t, kind in enumerate(kinds):
            for k in range(N_CHIP):
                out.append((_window(srcs[t], kind, 2 * k + 1 - c), lands[t].at[k], sibling, lands[t].at[k]))
        return out
    return plan, N_CHIP * len(kinds)


def _plan_scatter_chips(n):
    def plan(srcs, lands):
        x, y, c = _place()
        _, chips = _peers()
        out = []
        for t in range(n):
            for px, py in chips:
                out.append((srcs[t].at[2 * px + py], lands[t].at[2 * x + y], (px, py, c), lands[t].at[2 * px + py]))
        return out
    return plan, 3 * n


def _landing(shard, kind):
    if kind == "blocked":
        return lax.empty((shard.shape[0], N_DEV) + shard.shape[1:], shard.dtype)
    return lax.empty((shard.shape[0], N_DEV * shard.shape[1]), shard.dtype)


def _chip_sums(name, grads, kinds, recvs, c):
    n = len(grads)
    in_specs, out_specs, out_shape, args = [], [], [], []
    for gr, kind, rv in zip(grads, kinds, recvs):
        if kind == "blocked":
            rows, w = gr.shape[2], gr.shape[3]
            in_specs.append(pl.BlockSpec((None, None, rows, w), lambda k, cref: (0, 2 * k + cref[0], 0, 0)))
        else:
            rows, w = gr.shape[0], gr.shape[1] // N_DEV
            in_specs.append(pl.BlockSpec((rows, w), lambda k, cref: (0, 2 * k + cref[0])))
        blk = pl.BlockSpec((None, rows, w), lambda k, cref: (k, 0, 0))
        in_specs.append(blk)
        out_specs.append(blk)
        out_shape.append(jax.ShapeDtypeStruct((N_CHIP, rows, w), BF16))
        args += [gr, rv.reshape(N_CHIP, rows, w)]

    def body(*refs):
        for t in range(n):
            g_ref, r_ref, o_ref = refs[1 + 2 * t], refs[2 + 2 * t], refs[1 + 2 * n + t]
            o_ref[...] = (g_ref[...].astype(F32) + r_ref[...].astype(F32)).astype(BF16)

    return _pallas(body, name=name, n_prefetch=1, grid=(N_CHIP,), in_specs=in_specs, out_specs=out_specs,
                   out_shape=out_shape, params=_params(("parallel",)))(c, *args)


def _adamw_math(g, wv, mv, vv):
    m = ADAM_B1 * mv + (1.0 - ADAM_B1) * g
    v = ADAM_B2 * vv + (1.0 - ADAM_B2) * (g * g)
    m_hat = m / (1.0 - ADAM_B1 ** ADAM_STEP)
    v_hat = v / (1.0 - ADAM_B2 ** ADAM_STEP)
    delta = -ADAM_LR * (m_hat / (jnp.sqrt(v_hat) + ADAM_EPS) + ADAM_WD * wv)
    return delta, m, v


ADAM_STEPS = 2


def _adamw_group(name, items, chip_ids):
    n = len(items)
    in_specs, out_specs, out_shape, args, prevs = [], [], [], [chip_ids], []
    for own, recv, w3, m3, v3, layer, _ in items:
        nl, rows, w = w3.shape
        tr = rows // ADAM_STEPS
        assert tr % 16 == 0, (name, rows)
        in_specs += [pl.BlockSpec((None, tr, w), lambda i, ids, slot=slot: (ids[slot], i, 0)) for slot in range(4)]
        slab = pl.BlockSpec((None, tr, w), lambda i, ids, layer=layer: (layer, i, 0))
        in_specs += [slab] * 3
        out_specs += [slab] * 4
        out_shape += [jax.ShapeDtypeStruct((nl, rows, w), F32)] * 4
        args += [own, recv, recv, recv, w3, m3, v3]
    aliases = {}
    for t, item in enumerate(items):
        if item[6] is not None:
            for k in range(4):
                aliases[len(args) + k] = 4 * t + k
            in_specs += [ANY_SPEC] * 4
            args += list(item[6])
            prevs.append(t)
    n_in = 1 + 7 * n + 4 * len(prevs)

    def body(*refs):
        for t in range(n):
            own_ref, r1_ref, r2_ref, r3_ref, w_ref, m_ref, v_ref = refs[1 + 7 * t:8 + 7 * t]
            g_ref, d_ref, nm_ref, nv_ref = refs[n_in + 4 * t:n_in + 4 * t + 4]
            g = ((own_ref[...].astype(F32) + r1_ref[...].astype(F32)) + r2_ref[...].astype(F32)) + r3_ref[...].astype(F32)
            g_ref[...] = g
            d_ref[...], nm_ref[...], nv_ref[...] = _adamw_math(g, w_ref[...], m_ref[...], v_ref[...])

    outs = _pallas(body, name=name, n_prefetch=1, grid=(ADAM_STEPS,), in_specs=in_specs, out_specs=out_specs,
                   out_shape=out_shape, aliases=aliases, params=_params(("parallel",)))(*args)
    return [list(outs[4 * t:4 * t + 4]) for t in range(n)]


SHARD_ROWS = 8


def _adamw_small(gathered, ws, ms, vs, me):
    n = len(gathered)
    full = [w is not None for w in ws]
    sharded = [w is not None and w.ndim == 3 for w in ws]
    args = list(gathered)
    out_shape = []
    for t in range(n):
        shape = jax.ShapeDtypeStruct(ws[t].shape if sharded[t] else gathered[t].shape[2:], F32)
        if full[t]:
            args += [ws[t], ms[t], vs[t]]
            out_shape += [shape] * 4
        else:
            out_shape += [shape]

    def body(*refs):
        i_in, i_out = n, len(args) + 1
        me_ref = refs[len(args)]
        for t in range(n):
            p_ref = refs[t]
            if sharded[t]:
                w_ref, m_ref, v_ref = refs[i_in:i_in + 3]
                taps, layers, _ = w_ref.shape
                mine = pl.ds(pl.multiple_of(me_ref[0] * SHARD_ROWS, SHARD_ROWS), SHARD_ROWS)
                g = p_ref[0, 0, mine, :]
                for k in range(1, N_DEV):
                    g = g + p_ref[0, k, mine, :]
                for l in range(layers):
                    for k in range(taps):
                        at = (k, slice(l, l + 1), slice(None))
                        row = g[l * taps + k:l * taps + k + 1]
                        refs[i_out][at] = row
                        refs[i_out + 1][at], refs[i_out + 2][at], refs[i_out + 3][at] = _adamw_math(
                            row, w_ref[at], m_ref[at], v_ref[at])
                i_in += 3
                i_out += 4
                continue
            g = p_ref[0, 0]
            for k in range(1, N_DEV):
                g = g + p_ref[0, k]
            refs[i_out][...] = g
            if full[t]:
                w_ref, m_ref, v_ref = refs[i_in:i_in + 3]
                refs[i_out + 1][...], refs[i_out + 2][...], refs[i_out + 3][...] = _adamw_math(
                    g, w_ref[...], m_ref[...], v_ref[...])
                i_in += 3
                i_out += 4
            else:
                i_out += 1

    outs = _pallas(body, name="adamw_small",
                   in_specs=[VMEM_SPEC] * len(args) + [pl.BlockSpec(memory_space=pltpu.SMEM)],
                   out_specs=[VMEM_SPEC] * len(out_shape), out_shape=out_shape,
                   params=pltpu.CompilerParams(vmem_limit_bytes=VMEM_LIMIT))(*args, me)
    result, i = [], 0
    for t in range(n):
        k = 4 if full[t] else 1
        result.append(list(outs[i:i + k]))
        i += k
    return result


KIND = {"sc_w_in": "cols", "sc_w_out": "blocked", "w_dkv": "blocked", "w_kr": "cols", "w_uk": "cols", "w_uv": "cols",
        "w_dq": "blocked", "w_uq": "blocked", "w_o": "blocked", "ffn_w_up": "blocked", "ffn_w_down": "blocked",
        "conv": "blocked"}
GATHER_GROUPS = (("mixer", ("sc_w_in", "sc_w_out", "conv")),
                 ("up0", ("ffn_w_up0",)),
                 ("down0", ("ffn_w_down0",)),
                 ("attn", ("w_dkv", "w_kr", "w_uk", "w_uv", "w_dq", "w_uq", "w_o")),
                 ("ffn1", ("ffn_w_up1", "ffn_w_down1")))
SCATTER_GROUPS = (("ffn1", (("ffn_w_up", 1), ("ffn_w_down", 1))),
                  ("attn", (("w_o", None), ("w_uq", None), ("w_dq", None), ("w_uk", None), ("w_uv", None),
                            ("w_dkv", None), ("w_kr", None))),
                  ("ffn0", (("ffn_w_up", 0), ("ffn_w_down", 0))),
                  ("mixer", (("sc_w_out", None), ("sc_w_in", None))))
SCHEDULE = {
    "begin": (("gather_start", "mixer"),),
    "l0_norm": (("gather_forward", "mixer"), ("gather_start", "up0")),
    "l0_out": (("gather_forward", "up0"), ("gather_start", "down0"), ("gather_start", "attn")),
    "f0_up": (("gather_forward", "down0"), ("gather_forward", "attn"), ("gather_start", "ffn1")),
    "attn_fwd": (("gather_forward", "ffn1"),),
    "f1_gup": (("scatter_sibling", "ffn1"),),
    "f1_dhf": (("scatter_chips", "ffn1"),),
    "kv_bwd": (("scatter_sibling", "attn"),),
    "f0_dact": (("scatter_chips", "attn"),),
    "f0_gup": (("scatter_sibling", "ffn0"),),
    "f0_dhf": (("scatter_chips", "ffn0"),),
    "sc_bwd": (("scatter_sibling", "mixer"), ("scatter_done", "attn")),
    "d_l0_in": (("scatter_chips", "mixer"),),
}
FINISH = (("scatter_done", "ffn1"), ("scatter_done", "ffn0"), ("scatter_done", "mixer"))
STAGES = {"gather_start": 1, "gather_forward": 2, "gather_done": 3,
          "scatter_sibling": 1, "scatter_chips": 2, "scatter_done": 3}
SMALL_W_ROWS = 24


def _pack(arrays, rows):
    flat = jnp.concatenate([a.reshape(-1).astype(F32) for a in arrays])
    return jnp.pad(flat, (0, rows * 128 - flat.shape[0])).reshape(rows, 128)


def _cast_shards(items):
    arrays = []
    for a, _, _ in items:
        if not any(a is b for b in arrays):
            arrays.append(a)
    slot = [next(i for i, b in enumerate(arrays) if b is a) for a, _, _ in items]

    def body(*refs):
        for t, (a, layer, rows) in enumerate(items):
            w_ref, o_ref = refs[slot[t]], refs[len(arrays) + t]
            r, c = a.shape[-2:]
            if a.ndim == 3:
                o_ref[:, :r] = w_ref[(layer or 0):(layer or 0) + 1].astype(BF16)
                if rows > r:
                    o_ref[:, r:] = jnp.zeros((1, rows - r, c), BF16)
            else:
                o_ref[:r] = w_ref[...].astype(BF16)
                if rows > r:
                    o_ref[r:] = jnp.zeros((rows - r, c), BF16)

    out_shape = [jax.ShapeDtypeStruct(((1,) if a.ndim == 3 else ()) + (rows, a.shape[-1]), BF16)
                 for a, _, rows in items]
    return _pallas(body, name="cast_shards", in_specs=[VMEM_SPEC] * len(arrays), out_specs=[VMEM_SPEC] * len(items),
                   out_shape=out_shape, params=pltpu.CompilerParams(vmem_limit_bytes=VMEM_LIMIT))(*arrays)


STORED_TRANSPOSED = ("ffn_w_up", "w_uq", "w_kr")


def _stored(name, a):
    return jnp.swapaxes(a, -1, -2) if name in STORED_TRANSPOSED else a


def _base(name):
    if name.startswith("ffn_w_") and name[-1] in "01":
        return name[:-1], int(name[-1])
    return name, None


class _Exchange:
    def __init__(self, wts, mom, var, ffn_conv_b):
        self.wts, self.mom, self.var = wts, mom, var
        x, y, c = _place()
        self.c_arr = jnp.reshape(c, (1,)).astype(jnp.int32)
        chip = 2 * x + y
        self.chip_ids = jnp.stack([chip, chip ^ 1, chip ^ 2, chip ^ 3]).astype(jnp.int32)
        self.ready = {"ffn_cb0": ffn_conv_b.reshape(2, N_FF_BLK, 1, FF_BLK)[0],
                      "ffn_cb1": ffn_conv_b.reshape(2, N_FF_BLK, 1, FF_BLK)[1]}
        self.gathers, self.group_of = {}, {}
        self.grads, self.scatters, self.results, self.queue = {}, {}, {}, []
        for gname, names in GATHER_GROUPS:
            self.gathers[gname] = dict(stage=0, names=names, kinds=[KIND[_base(nm)[0]] for nm in names])
            for nm in names:
                self.group_of[nm] = gname
        for nm in ("sc_conv_w", "ffn_cw0", "ffn_cw1"):
            self.group_of[nm] = "mixer"
        self.cast, self.f32 = {}, {}
        self.at("begin", None)
        later = [nm for gname, names in GATHER_GROUPS[1:] for nm in names]
        self.cast = dict(zip(later, _cast_shards([self._shard_f32(nm) for nm in later])))

    def _shard_f32(self, name):
        base, layer = _base(name)
        if base not in self.f32:
            a = _stored(base, self.wts[base])
            self.f32[base] = a.reshape(a.shape[-2:]) if KIND[base] == "cols" else a.reshape((-1,) + a.shape[-2:])
        a = self.f32[base]
        return a, layer, {"w_kr": 128, "w_uq": QK_PAD}.get(base, a.shape[-2])

    def _shard(self, name):
        if name in self.cast:
            return self.cast[name]
        if name == "conv":
            return _pack([self.wts["sc_conv_w"], self.wts["ffn_conv_w"]], SMALL_W_ROWS).reshape(1, SMALL_W_ROWS, 128)
        base, layer = _base(name)
        a = _stored(base, self.wts[base])
        if layer is not None:
            a = a[layer:layer + 1]
        if KIND[base] == "cols":
            return a.reshape(a.shape[-2], a.shape[-1]).astype(BF16)
        return a.reshape((-1,) + a.shape[-2:]).astype(BF16)

    def _start(self, name, srcs, lands, ncopy, plan, st):
        self.queue.append((name, (srcs, lands, ncopy, plan), st))

    def _flush(self):
        if self.queue:
            flights = _copies_start("__".join(name for name, _, _ in self.queue), [job for _, job, _ in self.queue])
            for (_, _, st), flight in zip(self.queue, flights):
                st["flight"] = flight
            self.queue = []

    def _flight(self, st):
        self._flush()
        return st["flight"]

    def _gather_to(self, gname, stage, after):
        st = self.gathers[gname]
        if st["stage"] < 1 <= stage:
            shards = [self._shard(nm) for nm in st["names"]]
            lands = [_landing(s, kind) for s, kind in zip(shards, st["kinds"])]
            plan, ncopy = _plan_gather_chips(st["kinds"])
            self._start(f"ag_{gname}_chips", shards, lands, ncopy, plan, st)
            st["stage"] = 1
        if st["stage"] < 2 <= stage:
            plan, ncopy = _plan_gather_chips(st["kinds"])
            _, lands = _copies_wait(f"ag_{gname}_chips_wait", self._flight(st), ncopy, plan)
            plan, ncopy = _plan_gather_sibling(st["kinds"])
            self._start(f"ag_{gname}_sibling", [], lands, ncopy, plan, st)
            st["stage"] = 2
        if st["stage"] < 3 <= stage:
            plan, ncopy = _plan_gather_sibling(st["kinds"])
            _, lands = _copies_wait(f"ag_{gname}_sibling_wait", self._flight(st), ncopy, plan)
            for nm, land in zip(st["names"], lands):
                self._arrived(nm, land)
            st["stage"] = 3

    def _arrived(self, name, land):
        if name == "conv":
            conv = land.reshape(N_DEV, SMALL_W_ROWS * 128)
            self.ready["sc_conv_w"] = conv[:, :3 * 128].reshape(N_DEV, 3, 128).transpose(1, 0, 2).reshape(3, D)
            fcw = conv[:, 3 * 128:3 * 128 + 6 * 352].reshape(N_DEV, 2, 3, 352).transpose(1, 2, 0, 3)
            fcw = fcw.reshape(2, 3, N_FF_BLK, FF_BLK).transpose(0, 2, 1, 3)
            self.ready["ffn_cw0"], self.ready["ffn_cw1"] = fcw[0], fcw[1]
        elif name in ("sc_w_in", "w_uk", "w_uv", "w_kr") or name.startswith("ffn_w_up"):
            self.ready[name] = land
        elif name.startswith("ffn_w_down"):
            self.ready[name] = land.reshape(1, N_FF_BLK, FF_BLK, D)
        elif name == "w_uq":
            self.ready[name] = land.reshape(N_HEADS, QK_PAD, Q_LORA)
        else:
            self.ready[name] = land.reshape(D, land.shape[-1])

    def need(self, name, after):
        if name not in self.ready:
            self._gather_to(self.group_of[name], 3, after)
            self._flush()
        return self.ready[name]

    def grad(self, name, layer, array):
        self.grads[(name, layer)] = array

    def _scatter_to(self, gname, stage, after):
        keys = dict(SCATTER_GROUPS)[gname]
        st = self.scatters.setdefault(gname, dict(stage=0))
        kinds = [KIND[nm] for nm, _ in keys]
        if st["stage"] < 1 <= stage:
            grads = [self.grads[key] for key in keys]
            lands = []
            for gr, kind in zip(grads, kinds):
                shard = (gr.shape[0],) + gr.shape[2:] if kind == "blocked" else (gr.shape[0], gr.shape[1] // N_DEV)
                lands.append(lax.empty((N_CHIP,) + shard, BF16))
            plan, ncopy = _plan_scatter_sibling(kinds)
            self._start(f"rs_{gname}_sibling", grads, lands, ncopy, plan, st)
            st["stage"] = 1
        if st["stage"] < 2 <= stage:
            plan, ncopy = _plan_scatter_sibling(kinds)
            grads, recvs = _copies_wait(f"rs_{gname}_sibling_wait", self._flight(st), ncopy, plan)
            sums = _chip_sums(f"rs_{gname}_sums", grads, kinds, recvs, self.c_arr)
            lands = [lax.empty(s.shape, BF16) for s in sums]
            plan, ncopy = _plan_scatter_chips(len(sums))
            self._start(f"rs_{gname}_chips", sums, lands, ncopy, plan, st)
            st["stage"] = 2
        if st["stage"] < 3 <= stage:
            plan, ncopy = _plan_scatter_chips(len(keys))
            sums, recvs = _copies_wait(f"rs_{gname}_chips_wait", self._flight(st), ncopy, plan)
            items = []
            for (nm, layer), own, rv in zip(keys, sums, recvs):
                nl = 1 if layer is None else 2
                rows, w = own.shape[1], own.shape[2]
                w3, m3, v3 = (_stored(nm, src[nm]).reshape(nl, rows, w) for src in (self.wts, self.mom, self.var))
                items.append((own, rv, w3, m3, v3, 0 if layer is None else layer, self.results.get(nm)))
            outs = _adamw_group(f"adamw_{gname}", items, self.chip_ids)
            for (nm, _), out in zip(keys, outs):
                self.results[nm] = out
            st["stage"] = 3

    def at(self, place, after):
        for action, gname in SCHEDULE.get(place, ()):
            self._advance(action, gname, after)
        self._flush()

    def _advance(self, action, gname, after):
        if action.startswith("gather"):
            self._gather_to(gname, STAGES[action], after)
        else:
            self._scatter_to(gname, STAGES[action], after)

    def finish(self, after):
        for action, gname in FINISH:
            self._advance(action, gname, after)
        for gname, _ in SCATTER_GROUPS:
            self._scatter_to(gname, 3, after)
        return {nm: [_stored(nm, o.reshape(_stored(nm, self.wts[nm]).shape)) for o in outs]
                for nm, outs in self.results.items()}


REPLICATED = ("attn_norm", "ffn_norm", "final_norm", "kv_in_norm", "kv_latent_norm", "q_latent_norm", "ffn_conv_b")
WEIGHTS = ("attn_norm", "ffn_norm", "final_norm", "sc_w_in", "sc_conv_w", "sc_w_out", "kv_in_norm", "w_dkv",
           "kv_latent_norm", "w_kr", "w_uk", "w_uv", "w_dq", "q_latent_norm", "w_uq", "w_o", "ffn_w_up", "ffn_conv_w",
           "ffn_conv_b", "ffn_w_down")


def kernel(x, positions, attn_norm, ffn_norm, final_norm, sc_w_in, sc_conv_w, sc_w_out, kv_in_norm, w_dkv, kv_latent_norm, w_kr, w_uk, w_uv, w_dq, q_latent_norm, w_uq, w_o, ffn_w_up, ffn_conv_w, ffn_conv_b, ffn_w_down, loss_target, m_attn_norm, m_ffn_norm, m_final_norm, m_sc_w_in, m_sc_conv_w, m_sc_w_out, m_kv_in_norm, m_w_dkv, m_kv_latent_norm, m_w_kr, m_w_uk, m_w_uv, m_w_dq, m_q_latent_norm, m_w_uq, m_w_o, m_ffn_w_up, m_ffn_conv_w, m_ffn_conv_b, m_ffn_w_down, v_attn_norm, v_ffn_norm, v_final_norm, v_sc_w_in, v_sc_conv_w, v_sc_w_out, v_kv_in_norm, v_w_dkv, v_kv_latent_norm, v_w_kr, v_w_uk, v_w_uv, v_w_dq, v_q_latent_norm, v_w_uq, v_w_o, v_ffn_w_up, v_ffn_conv_w, v_ffn_conv_b, v_ffn_w_down):
    wts = dict(attn_norm=attn_norm, ffn_norm=ffn_norm, final_norm=final_norm, sc_w_in=sc_w_in, sc_conv_w=sc_conv_w,
               sc_w_out=sc_w_out, kv_in_norm=kv_in_norm, w_dkv=w_dkv, kv_latent_norm=kv_latent_norm, w_kr=w_kr,
               w_uk=w_uk, w_uv=w_uv, w_dq=w_dq, q_latent_norm=q_latent_norm, w_uq=w_uq, w_o=w_o, ffn_w_up=ffn_w_up,
               ffn_conv_w=ffn_conv_w, ffn_conv_b=ffn_conv_b, ffn_w_down=ffn_w_down)
    mom = dict(attn_norm=m_attn_norm, ffn_norm=m_ffn_norm, final_norm=m_final_norm, sc_w_in=m_sc_w_in,
               sc_conv_w=m_sc_conv_w, sc_w_out=m_sc_w_out, kv_in_norm=m_kv_in_norm, w_dkv=m_w_dkv,
               kv_latent_norm=m_kv_latent_norm, w_kr=m_w_kr, w_uk=m_w_uk, w_uv=m_w_uv, w_dq=m_w_dq,
               q_latent_norm=m_q_latent_norm, w_uq=m_w_uq, w_o=m_w_o, ffn_w_up=m_ffn_w_up, ffn_conv_w=m_ffn_conv_w,
               ffn_conv_b=m_ffn_conv_b, ffn_w_down=m_ffn_w_down)
    var = dict(attn_norm=v_attn_norm, ffn_norm=v_ffn_norm, final_norm=v_final_norm, sc_w_in=v_sc_w_in,
               sc_conv_w=v_sc_conv_w, sc_w_out=v_sc_w_out, kv_in_norm=v_kv_in_norm, w_dkv=v_w_dkv,
               kv_latent_norm=v_kv_latent_norm, w_kr=v_w_kr, w_uk=v_w_uk, w_uv=v_w_uv, w_dq=v_w_dq,
               q_latent_norm=v_q_latent_norm, w_uq=v_w_uq, w_o=v_w_o, ffn_w_up=v_ffn_w_up, ffn_conv_w=v_ffn_conv_w,
               ffn_conv_b=v_ffn_conv_b, ffn_w_down=v_ffn_w_down)
    xi, yi, ci = _place()
    me = 4 * xi + 2 * yi + ci
    _Chain.last = None

    ex = _Exchange(wts, mom, var, ffn_conv_b)
    rep = {
        "attn_norm": attn_norm, "ffn_norm": ffn_norm, "final_norm": final_norm,
        "kv_in_norm": kv_in_norm.reshape(1, D), "kv_latent_norm": kv_latent_norm.reshape(1, KV_LORA),
        "q_latent_norm": q_latent_norm.reshape(1, Q_LORA),
    }
    loss, grad_x, small = _local_step(x.reshape(T, D), positions.reshape(T, 1), loss_target.reshape(T, D), rep, ex)

    def rows_of(a):
        return a.reshape(-1, a.shape[-1])

    def device_rows(a):
        taps, c = a.shape[-2], a.shape[-1] // N_DEV
        rows = a.reshape(-1, taps, N_DEV, c).transpose(2, 0, 1, 3).reshape(N_DEV, -1, c)
        return jnp.pad(rows, ((0, 0), (0, SHARD_ROWS - rows.shape[1]), (0, 0))).reshape(N_DEV * SHARD_ROWS, c)

    def taps_first(a):
        return jnp.transpose(a, (1, 0, 2))

    sharded = ("sc_conv_w", "ffn_conv_w")
    shards = ([loss.reshape(1, 1, 128)] + [rows_of(small[nm])[None] for nm in REPLICATED]
              + [device_rows(small[nm])[None] for nm in sharded])
    plan, ncopy = _plan_gather_all(len(shards))
    flight, = _copies_start("ag_small", [(shards, [lax.empty((1, N_DEV) + s.shape[1:], F32) for s in shards], ncopy, plan)])
    results = ex.finish(grad_x)
    _, gathered = _copies_wait("ag_small_wait", flight, ncopy, plan)
    params = [[None] + [rows_of(src[nm]) for nm in REPLICATED] + [taps_first(src[nm]) for nm in sharded]
              for src in (wts, mom, var)]
    summed = _adamw_small(gathered, *params, me.astype(jnp.int32).reshape(1))
    loss_total = summed[0][0][0, 0]
    for nm, vals in zip(REPLICATED, summed[1:1 + len(REPLICATED)]):
        results[nm] = [a.reshape(wts[nm].shape) for a in vals]
    for nm, vals in zip(sharded, summed[1 + len(REPLICATED):]):
        results[nm] = [taps_first(a) for a in vals]

    outs = [loss_total, grad_x.reshape(1, T, D)]
    for slot in range(4):
        outs.extend(results[nm][slot] for nm in WEIGHTS)
    return tuple(outs)
```

```python
import jax
import jax.numpy as jnp
from jax import lax
from jax.experimental import pallas as pl
from jax.experimental.pallas import tpu as pltpu

F32 = jnp.float32
BF16 = jnp.bfloat16

T = 2048
D = 1024
N_HEADS = 8
QK_NOPE = 128
QK_ROPE = 64
V_HEAD = 128
Q_LORA = 384
KV_LORA = 256
D_FF = 2816
CHUNK = 64
ROPE_THETA = 10000.0
EPS = 1e-6
NEG_INF = -1e30
ADAM_LR = 0.001
ADAM_B1 = 0.9
ADAM_B2 = 0.999
ADAM_EPS = 1e-08
ADAM_WD = 0.01
ADAM_STEP = 10

N_DEV = 8
N_CHIP = 4
FF_BLK = D_FF * 2 // N_DEV
N_FF_BLK = D_FF // FF_BLK
QK_PAD = 256
HALO = 16

TM = 1024
TS = 512
TR = 256
TQ = 512
VMEM_LIMIT = 56 * 1024 * 1024

NN = (((1,), (0,)), ((), ()))
NT = (((1,), (1,)), ((), ()))
TN = (((0,), (0,)), ((), ()))
MESH = pl.DeviceIdType.MESH


def _params(sem):
    return pltpu.CompilerParams(dimension_semantics=sem, vmem_limit_bytes=VMEM_LIMIT)


ANY_SPEC = pl.BlockSpec(memory_space=pl.ANY)
VMEM_SPEC = pl.BlockSpec(memory_space=pltpu.VMEM)


class _Chain:
    last = None


def _pallas(body, *, name, in_specs, out_specs, out_shape, grid=(), scratch_shapes=(), n_prefetch=0, aliases=None,
            params=None):
    def run(*args):
        after = _Chain.last
        n_lead = len(args)
        specs, operands, fn = list(in_specs), list(args), body
        if after is not None:
            def fn(*refs):
                return body(*refs[:n_lead], *refs[n_lead + 1:])
            specs.append(ANY_SPEC)
            operands.append(after)
        kw = dict(name=name, out_shape=out_shape, input_output_aliases=aliases or {})
        if params is not None:
            kw["compiler_params"] = params
        if n_prefetch:
            kw["grid_spec"] = pltpu.PrefetchScalarGridSpec(
                num_scalar_prefetch=n_prefetch, grid=grid, in_specs=specs, out_specs=out_specs,
                scratch_shapes=scratch_shapes)
        else:
            kw.update(grid=grid, in_specs=specs, out_specs=out_specs, scratch_shapes=scratch_shapes)
        outs = pl.pallas_call(fn, **kw)(*operands)
        _Chain.last = outs[0] if isinstance(outs, (list, tuple)) else outs
        return outs
    return run


def _mm(name, a, b, *, grid, a_spec, b_spec, o_spec, o_shape, o_dtype, dims, k_axis=None, acc_shape=None,
        add=None, add_spec=None):
    nk = grid[k_axis] if k_axis is not None else 1
    has_add = add is not None

    def body(*refs):
        a_ref, b_ref = refs[0], refs[1]
        p = 2
        add_ref = None
        if has_add:
            add_ref = refs[p]
            p += 1
        o_ref = refs[p]
        p += 1
        r = lax.dot_general(a_ref[...].astype(BF16), b_ref[...].astype(BF16), dims, preferred_element_type=F32)
        if k_axis is None:
            if has_add:
                r = r + add_ref[...].astype(F32)
            o_ref[...] = r.astype(o_dtype)
        else:
            acc = refs[p]
            k = pl.program_id(k_axis)

            @pl.when(k == 0)
            def _():
                acc[...] = r

            @pl.when(k > 0)
            def _():
                acc[...] += r

            @pl.when(k == nk - 1)
            def _():
                t = acc[...]
                if has_add:
                    t = t + add_ref[...].astype(F32)
                o_ref[...] = t.astype(o_dtype)

    in_specs = [a_spec, b_spec]
    args = [a, b]
    if has_add:
        in_specs.append(add_spec if add_spec is not None else o_spec)
        args.append(add)
    sem = tuple("arbitrary" if ax == k_axis else "parallel" for ax in range(len(grid)))
    scratch = [pltpu.VMEM(acc_shape, F32)] if k_axis is not None else []
    return _pallas(body, name=name, grid=grid, in_specs=in_specs, out_specs=o_spec,
                   out_shape=jax.ShapeDtypeStruct(o_shape, o_dtype), scratch_shapes=scratch, params=_params(sem))(*args)


def _mm_sum(name, parts, *, grid, o_spec, o_shape, o_dtype, add=None, norm_bwd=None):
    has_add = add is not None
    np_ = len(parts)
    nn = 1 if norm_bwd is None else len(norm_bwd[1])
    has_res = norm_bwd is not None and norm_bwd[2] is not None

    def body(*refs):
        accs = [None] * nn
        for p, (_, _, _, _, dims, n) in enumerate(parts):
            a_ref, b_ref = refs[2 * p], refs[2 * p + 1]
            for k in range(a_ref.shape[0]):
                r = lax.dot_general(a_ref[k], b_ref[k], dims, preferred_element_type=F32)
                accs[n] = r if accs[n] is None else accs[n] + r
        if norm_bwd is None:
            acc = accs[0]
            if has_add:
                acc = acc + refs[2 * np_][...]
            refs[-1][...] = acc.astype(o_dtype)
            return
        x_ref, g_refs = refs[2 * np_], refs[2 * np_ + 1:2 * np_ + 1 + nn]
        dx_ref, dxb_ref, dg_refs = refs[-2 - nn], refs[-1 - nn], refs[-nn:]
        xv = x_ref[...]
        r = lax.rsqrt(jnp.mean(xv * xv, axis=-1, keepdims=True) + EPS)
        xn = xv * r
        dx = refs[2 * np_ + 1 + nn][...] if has_res else None
        sums = []
        for acc, g_ref in zip(accs, g_refs):
            gdy = acc * g_ref[...]
            t = r * (gdy - xn * jnp.mean(gdy * xn, axis=-1, keepdims=True))
            dx = t if dx is None else dx + t
            sums.append(jnp.sum(acc * xn, axis=0, keepdims=True))
        dx_ref[...] = dx
        dxb_ref[...] = dx.astype(BF16)

        @pl.when(pl.program_id(0) == 0)
        def _():
            for dg_ref, part in zip(dg_refs, sums):
                dg_ref[...] = part

        @pl.when(pl.program_id(0) > 0)
        def _():
            for dg_ref, part in zip(dg_refs, sums):
                dg_ref[...] += part

    in_specs, args = [], []
    for a, a_spec, b, b_spec, _, _ in parts:
        in_specs += [a_spec, b_spec]
        args += [a, b]
    if norm_bwd is None:
        if has_add:
            in_specs.append(o_spec)
            args.append(add)
        return _pallas(body, name=name, grid=grid, in_specs=in_specs, out_specs=o_spec,
                       out_shape=jax.ShapeDtypeStruct(o_shape, o_dtype),
                       params=_params(("parallel",) * len(grid)))(*args)
    x, gains, dres = norm_bwd
    vec = pl.BlockSpec((1, o_shape[1]), lambda i: (0, 0))
    in_specs += [o_spec] + [vec] * nn + ([o_spec] if has_res else [])
    args += [x] + list(gains) + ([dres] if has_res else [])
    outs = _pallas(body, name=name, grid=grid, in_specs=in_specs, out_specs=[o_spec, o_spec] + [vec] * nn,
                   out_shape=[jax.ShapeDtypeStruct(o_shape, F32), jax.ShapeDtypeStruct(o_shape, BF16)]
                   + [jax.ShapeDtypeStruct((1, o_shape[1]), F32)] * nn,
                   params=_params(("arbitrary",)))(*args)
    return outs[0], outs[1], list(outs[2:])


def _mm_rows(name, a, b, dims, o_dtype, n_out, *, tn=None, add=None):
    k = a.shape[1]
    tn = n_out if tn is None else tn
    if dims == NN:
        b_spec = pl.BlockSpec((k, tn), lambda n, i: (0, n))
    else:
        b_spec = pl.BlockSpec((tn, k), lambda n, i: (n, 0))
    return _mm(name, a, b, grid=(n_out // tn, T // TM),
               a_spec=pl.BlockSpec((TM, k), lambda n, i: (i, 0)), b_spec=b_spec,
               o_spec=pl.BlockSpec((TM, tn), lambda n, i: (i, n)), o_shape=(T, n_out), o_dtype=o_dtype,
               dims=dims, add=add)


def _wgrads(name, jobs):
    arrays, index = [], {}
    for a, b in jobs:
        for arr in (a, b):
            if id(arr) not in index:
                index[id(arr)] = len(arrays)
                arrays.append(arr)
    n_in = len(arrays)

    def body(*refs):
        for t, (a, b) in enumerate(jobs):
            a_ref, b_ref, o_ref = refs[index[id(a)]], refs[index[id(b)]], refs[n_in + t]
            if a.ndim == 3:
                for h in range(a.shape[0]):
                    o_ref[h] = lax.dot_general(a_ref[h], b_ref[...], TN, preferred_element_type=F32).astype(BF16)
            else:
                o_ref[...] = lax.dot_general(a_ref[...], b_ref[...], TN, preferred_element_type=F32).astype(BF16)

    out_shape = [jax.ShapeDtypeStruct(a.shape[:-2] + (a.shape[-1], b.shape[-1]), BF16) for a, b in jobs]
    return _pallas(body, name=name, in_specs=[VMEM_SPEC] * n_in, out_specs=[VMEM_SPEC] * len(jobs), out_shape=out_shape,
                   params=pltpu.CompilerParams(vmem_limit_bytes=VMEM_LIMIT))(*arrays)


def _mm_wgrad(name, a, b, *, tn=512):
    k, n = a.shape[1], b.shape[1]
    tn = min(tn, n)
    return _mm(name, a, b, grid=(n // tn,),
               a_spec=pl.BlockSpec((T, k), lambda j: (0, 0)), b_spec=pl.BlockSpec((T, tn), lambda j: (0, j)),
               o_spec=pl.BlockSpec((k, tn), lambda j: (0, j)), o_shape=(k, n), o_dtype=BF16, dims=TN)


def _rms_fwd(name, x, g):
    d = x.shape[1]

    def body(x_ref, g_ref, o_ref):
        xv = x_ref[...]
        r = lax.rsqrt(jnp.mean(xv * xv, axis=-1, keepdims=True) + EPS)
        o_ref[...] = ((xv * r) * g_ref[...]).astype(BF16)

    return _pallas(
        body, name=name, grid=(T // TM,),
        in_specs=[pl.BlockSpec((TM, d), lambda i: (i, 0)), pl.BlockSpec((1, d), lambda i: (0, 0))],
        out_specs=pl.BlockSpec((TM, d), lambda i: (i, 0)),
        out_shape=jax.ShapeDtypeStruct((T, d), BF16), params=_params(("parallel",)))(x, g)


def _rms(xv, g):
    return (xv * lax.rsqrt(jnp.mean(xv * xv, axis=-1, keepdims=True) + EPS)) * g


def _rms_bwd(name, x, gains, dys, dres=None):
    d = x.shape[1]
    n = len(gains)
    has_res = dres is not None

    def body(*refs):
        x_ref, g_refs, dy_refs = refs[0], refs[1:1 + n], refs[1 + n:1 + 2 * n]
        dx_ref, dxb_ref = refs[-2 - n], refs[-1 - n]
        dg_refs = refs[-n:]
        xv = x_ref[...]
        r = lax.rsqrt(jnp.mean(xv * xv, axis=-1, keepdims=True) + EPS)
        xn = xv * r
        dx = refs[1 + 2 * n][...] if has_res else None
        parts = []
        for g_ref, dy_ref in zip(g_refs, dy_refs):
            dyv = dy_ref[...].astype(F32)
            gdy = dyv * g_ref[...]
            t = r * (gdy - xn * jnp.mean(gdy * xn, axis=-1, keepdims=True))
            dx = t if dx is None else dx + t
            parts.append(jnp.sum(dyv * xn, axis=0, keepdims=True))
        dx_ref[...] = dx
        dxb_ref[...] = dx.astype(BF16)

        @pl.when(pl.program_id(0) == 0)
        def _():
            for dg_ref, part in zip(dg_refs, parts):
                dg_ref[...] = part

        @pl.when(pl.program_id(0) > 0)
        def _():
            for dg_ref, part in zip(dg_refs, parts):
                dg_ref[...] += part

    row = pl.BlockSpec((TR, d), lambda i: (i, 0))
    vec = pl.BlockSpec((1, d), lambda i: (0, 0))
    args = [x] + list(gains) + list(dys) + ([dres] if has_res else [])
    in_specs = [row] + [vec] * n + [row] * n + ([row] if has_res else [])
    outs = _pallas(
        body, name=name, grid=(T // TR,), in_specs=in_specs, out_specs=[row, row] + [vec] * n,
        out_shape=[jax.ShapeDtypeStruct((T, d), F32), jax.ShapeDtypeStruct((T, d), BF16)]
        + [jax.ShapeDtypeStruct((1, d), F32)] * n,
        params=_params(("arbitrary",)))(*args)
    return outs[0], outs[1], list(outs[2:])


def _final(h, g, tgt):
    def body(h_ref, g_ref, t_ref, loss_ref, dh_ref, dhb_ref, dg_ref):
        hv = h_ref[...]
        r = lax.rsqrt(jnp.mean(hv * hv, axis=-1, keepdims=True) + EPS)
        xn = hv * r
        gv = g_ref[...]
        err = xn * gv - t_ref[...]
        part_loss = 0.5 * jnp.sum(jnp.mean(err * err, axis=-1, keepdims=True), axis=0, keepdims=True)
        dy = err * (1.0 / D)
        gdy = dy * gv
        dh = r * (gdy - xn * jnp.mean(gdy * xn, axis=-1, keepdims=True))
        dh_ref[...] = dh
        dhb_ref[...] = dh.astype(BF16)
        part = jnp.sum(dy * xn, axis=0, keepdims=True)
        first = pl.program_id(0) == 0

        @pl.when(first)
        def _():
            dg_ref[...] = part
            loss_ref[...] = jnp.broadcast_to(part_loss, (1, 128))

        @pl.when(jnp.logical_not(first))
        def _():
            dg_ref[...] += part
            loss_ref[...] += jnp.broadcast_to(part_loss, (1, 128))

    row = pl.BlockSpec((TR, D), lambda i: (i, 0))
    vec = pl.BlockSpec((1, D), lambda i: (0, 0))
    return _pallas(
        body, name="final_loss", grid=(T // TR,), in_specs=[row, vec, row],
        out_specs=[pl.BlockSpec((1, 128), lambda i: (0, 0)), row, row, vec],
        out_shape=[jax.ShapeDtypeStruct((1, 128), F32), jax.ShapeDtypeStruct((T, D), F32),
                   jax.ShapeDtypeStruct((T, D), BF16), jax.ShapeDtypeStruct((1, D), F32)],
        params=_params(("arbitrary",)))(h, g, tgt)


def _prev_idx(i, rows=TR):
    return jnp.maximum(i * (rows // HALO) - 1, 0)


def _next_idx(i, rows=TR):
    return jnp.minimum((i + 1) * (rows // HALO), T // HALO - 1)


def _causal_taps(ext):
    return pltpu.roll(ext, 2, 0)[HALO:], pltpu.roll(ext, 1, 0)[HALO:], ext[HALO:]


def _anticausal_taps(ext, n):
    rows = ext.shape[0]
    return pltpu.roll(ext, rows - 1, 0)[:n], pltpu.roll(ext, rows - 2, 0)[:n]


MIX_COLS = 512


def _mixer_in(hn, w_in, w):
    nc = D // MIX_COLS

    def body(h_ref, hh_ref, wb_ref, wc_ref, wu_ref, w_ref, b_ref, c_ref, u_ref, y_ref):
        i = pl.program_id(1)
        hv = h_ref[...]
        he = jnp.concatenate([hh_ref[...], hv], axis=0)
        ce = lax.dot_general(he, wc_ref[...], NN, preferred_element_type=F32).astype(BF16)
        ue = lax.dot_general(he, wu_ref[...], NN, preferred_element_type=F32).astype(BF16)
        bv = lax.dot_general(hv, wb_ref[...], NN, preferred_element_type=F32).astype(BF16)
        b_ref[...] = bv
        c_ref[...] = ce[HALO:]
        u_ref[...] = ue[HALO:]
        row = lax.broadcasted_iota(jnp.int32, (HALO + TS, 1), 0)
        cu = jnp.where(jnp.logical_or(i > 0, row >= HALO), ce.astype(F32) * ue.astype(F32), 0.0)
        x2, x1, x0 = _causal_taps(cu)
        wv = w_ref[...]
        cv = (x2 * wv[0:1] + x1 * wv[1:2]) + x0 * wv[2:3]
        y_ref[...] = (bv.astype(F32) * cv).astype(BF16)

    def cols(part):
        return pl.BlockSpec((D, MIX_COLS), lambda j, i: (0, part * nc + j))

    blk = pl.BlockSpec((TS, MIX_COLS), lambda j, i: (i, j))
    out = jax.ShapeDtypeStruct((T, D), BF16)
    return _pallas(
        body, name="l0_in", grid=(nc, T // TS),
        in_specs=[pl.BlockSpec((TS, D), lambda j, i: (i, 0)), pl.BlockSpec((HALO, D), lambda j, i: (_prev_idx(i, TS), 0)),
                  cols(0), cols(1), cols(2), pl.BlockSpec((3, MIX_COLS), lambda j, i: (0, j))],
        out_specs=[blk] * 4, out_shape=[out] * 4,
        params=_params(("parallel", "parallel")))(hn, hn, w_in, w_in, w_in, w)


def _mixer_out_bwd(dh, w_out, zb, zc, zu, w):
    last = T // TR - 1

    def body(dh_ref, dhn_ref, wo_ref, b_ref, bn_ref, c_ref, ch_ref, u_ref, uh_ref, w_ref, dz_ref, dw_ref):
        i = pl.program_id(0)
        dye = lax.dot_general(jnp.concatenate([dh_ref[...], dhn_ref[...]], axis=0), wo_ref[...], NT,
                              preferred_element_type=F32)
        cv_ = c_ref[...].astype(F32)
        uv = u_ref[...].astype(F32)
        cu = cv_ * uv
        cuh = jnp.where(i > 0, ch_ref[...].astype(F32) * uh_ref[...].astype(F32), 0.0)
        x2, x1, x0 = _causal_taps(jnp.concatenate([cuh, cu], axis=0))
        wv = w_ref[...]
        conv = (x2 * wv[0:1] + x1 * wv[1:2]) + x0 * wv[2:3]
        dyv = dye[:TR]
        dz_ref[:, 0:D] = (dyv * conv).astype(BF16)
        dconv = dyv * b_ref[...].astype(F32)
        dconv_n = jnp.where(i < last, dye[TR:] * bn_ref[...].astype(F32), 0.0)
        n1, n2 = _anticausal_taps(jnp.concatenate([dconv, dconv_n], axis=0), TR)
        dcu = (dconv * wv[2:3] + n1 * wv[1:2]) + n2 * wv[0:1]
        dz_ref[:, D:2 * D] = (dcu * uv).astype(BF16)
        dz_ref[:, 2 * D:3 * D] = (dcu * cv_).astype(BF16)
        part = jnp.concatenate([jnp.sum(dconv * x2, axis=0, keepdims=True),
                                jnp.sum(dconv * x1, axis=0, keepdims=True),
                                jnp.sum(dconv * x0, axis=0, keepdims=True)], axis=0)

        @pl.when(i == 0)
        def _():
            dw_ref[...] = part

        @pl.when(i > 0)
        def _():
            dw_ref[...] += part

    main = pl.BlockSpec((TR, D), lambda i: (i, 0))
    prev = pl.BlockSpec((HALO, D), lambda i: (_prev_idx(i), 0))
    nxt = pl.BlockSpec((HALO, D), lambda i: (_next_idx(i), 0))
    wspec = pl.BlockSpec((3, D), lambda i: (0, 0))
    return _pallas(
        body, name="d_l0_out", grid=(T // TR,),
        in_specs=[main, nxt, pl.BlockSpec((D, D), lambda i: (0, 0)), main, nxt, main, prev, main, prev, wspec],
        out_specs=[pl.BlockSpec((TR, 3 * D), lambda i: (i, 0)), wspec],
        out_shape=[jax.ShapeDtypeStruct((T, 3 * D), BF16), jax.ShapeDtypeStruct((3, D), F32)],
        params=_params(("arbitrary",)))(dh, dh, w_out, zb, zb, zc, zc, zu, zu, w)


def _sigmoid(x):
    return 0.5 * jnp.tanh(0.5 * x) + 0.5


def _ffn_up_act(name, hf, w_up, w, b):
    def body(h_ref, hh_ref, wg_ref, wv_ref, w_ref, b_ref, g_ref, v_ref, a_ref):
        i = pl.program_id(1)
        hv = h_ref[...]
        ge = lax.dot_general(jnp.concatenate([hh_ref[...], hv], axis=0), wg_ref[...], NT,
                             preferred_element_type=F32).astype(BF16)
        v = lax.dot_general(hv, wv_ref[...], NT, preferred_element_type=F32).astype(BF16)
        g_ref[...] = ge[HALO:]
        v_ref[...] = v
        ext = ge.astype(F32)
        row = lax.broadcasted_iota(jnp.int32, (HALO + TM, 1), 0)
        ext = jnp.where(jnp.logical_or(i > 0, row >= HALO), ext, 0.0)
        x2, x1, x0 = _causal_taps(ext)
        wv = w_ref[...]
        gc = ((x2 * wv[0:1] + x1 * wv[1:2]) + x0 * wv[2:3]) + b_ref[...]
        a_ref[...] = ((gc * _sigmoid(gc)) * v.astype(F32)).astype(BF16)

    blk = pl.BlockSpec((None, TM, FF_BLK), lambda j, i: (j, i, 0))
    out = jax.ShapeDtypeStruct((N_FF_BLK, T, FF_BLK), BF16)
    return _pallas(
        body, name=name, grid=(N_FF_BLK, T // TM),
        in_specs=[pl.BlockSpec((TM, D), lambda j, i: (i, 0)),
                  pl.BlockSpec((HALO, D), lambda j, i: (_prev_idx(i, TM), 0)),
                  pl.BlockSpec((None, None, FF_BLK, D), lambda j, i: (0, j, 0, 0)),
                  pl.BlockSpec((None, None, FF_BLK, D), lambda j, i: (0, j + N_FF_BLK, 0, 0)),
                  pl.BlockSpec((None, 3, FF_BLK), lambda j, i: (j, 0, 0)),
                  pl.BlockSpec((None, 1, FF_BLK), lambda j, i: (j, 0, 0))],
        out_specs=[blk, blk, blk], out_shape=[out, out, out],
        params=_params(("parallel", "parallel")))(hf, hf, w_up, w_up, w, b)


def _ffn_dact(name, dh, w_down4, g, v, w, b):
    last = T // TS - 1

    def body(dh_ref, dhn_ref, wd_ref, g_ref, gp_ref, gn_ref, v_ref, vn_ref, w_ref, b_ref, dg_ref, dv_ref, dw_ref, db_ref):
        i = pl.program_id(1)
        da = lax.dot_general(jnp.concatenate([dh_ref[...], dhn_ref[...]], axis=0), wd_ref[...], NT,
                             preferred_element_type=F32)
        row = lax.broadcasted_iota(jnp.int32, (TS + HALO, 1), 0)
        da = jnp.where(jnp.logical_or(i < last, row < TS), da, 0.0)
        gp = jnp.where(i > 0, gp_ref[...].astype(F32), 0.0)
        ext = jnp.concatenate([gp, g_ref[...].astype(F32), gn_ref[...].astype(F32)], axis=0)
        x2, x1, x0 = _causal_taps(ext)
        wv = w_ref[...]
        gc = ((x2 * wv[0:1] + x1 * wv[1:2]) + x0 * wv[2:3]) + b_ref[...]
        sg = _sigmoid(gc)
        vv = jnp.concatenate([v_ref[...].astype(F32), vn_ref[...].astype(F32)], axis=0)
        dv_ref[...] = (da[:TS] * (gc[:TS] * sg[:TS])).astype(BF16)
        dgc = (da * vv) * (sg * (1.0 + gc * (1.0 - sg)))
        n1, n2 = _anticausal_taps(dgc, TS)
        d0 = dgc[:TS]
        dg_ref[...] = ((d0 * wv[2:3] + n1 * wv[1:2]) + n2 * wv[0:1]).astype(BF16)
        part_w = jnp.concatenate([jnp.sum(d0 * x2[:TS], axis=0, keepdims=True),
                                  jnp.sum(d0 * x1[:TS], axis=0, keepdims=True),
                                  jnp.sum(d0 * x0[:TS], axis=0, keepdims=True)], axis=0)
        part_b = jnp.sum(d0, axis=0, keepdims=True)

        @pl.when(i == 0)
        def _():
            dw_ref[...] = part_w
            db_ref[...] = part_b

        @pl.when(i > 0)
        def _():
            dw_ref[...] += part_w
            db_ref[...] += part_b

    blk = pl.BlockSpec((None, TS, FF_BLK), lambda j, i: (j, i, 0))
    prev = pl.BlockSpec((None, HALO, FF_BLK), lambda j, i: (j, _prev_idx(i, TS), 0))
    nxt = pl.BlockSpec((None, HALO, FF_BLK), lambda j, i: (j, _next_idx(i, TS), 0))
    wspec = pl.BlockSpec((None, 3, FF_BLK), lambda j, i: (j, 0, 0))
    bspec = pl.BlockSpec((None, 1, FF_BLK), lambda j, i: (j, 0, 0))
    return _pallas(
        body, name=name, grid=(N_FF_BLK, T // TS),
        in_specs=[pl.BlockSpec((TS, D), lambda j, i: (i, 0)),
                  pl.BlockSpec((HALO, D), lambda j, i: (_next_idx(i, TS), 0)),
                  pl.BlockSpec((None, None, FF_BLK, D), lambda j, i: (0, j, 0, 0)),
                  blk, prev, nxt, blk, nxt, wspec, bspec],
        out_specs=[blk, blk, wspec, bspec],
        out_shape=[jax.ShapeDtypeStruct((N_FF_BLK, T, FF_BLK), BF16), jax.ShapeDtypeStruct((N_FF_BLK, T, FF_BLK), BF16),
                   jax.ShapeDtypeStruct((N_FF_BLK, 3, FF_BLK), F32), jax.ShapeDtypeStruct((N_FF_BLK, 1, FF_BLK), F32)],
        params=_params(("parallel", "arbitrary")))(dh, dh, w_down4, g, g, g, v, v, w, b)


def _rope_tables(pos, inv_freq):
    half = QK_ROPE // 2

    def body(p_ref, f_ref, c_ref, sa_ref, sb_ref):
        ang = p_ref[...].astype(F32) * f_ref[...]
        lane = lax.broadcasted_iota(jnp.int32, (T, 128), 1)
        c = jnp.cos(ang)
        s = jnp.sin(ang)
        c_ref[...] = jnp.where(lane < 2 * half, c, 0.0)
        sa_ref[...] = jnp.where(lane < half, -s, 0.0)
        sb_ref[...] = jnp.where(jnp.logical_and(lane >= half, lane < 2 * half), s, 0.0)

    return _pallas(
        body, name="rope_tables", in_specs=[VMEM_SPEC] * 2, out_specs=[VMEM_SPEC] * 3,
        out_shape=[jax.ShapeDtypeStruct((T, 128), F32)] * 3,
        params=pltpu.CompilerParams(vmem_limit_bytes=VMEM_LIMIT))(pos, inv_freq)


def _rotate(r, c, sa, sb, sign):
    return r * c + sign * (pltpu.roll(r, 96, 1) * sa + pltpu.roll(r, 32, 1) * sb)


def _attn_pre(h2, g_kv, g_l1, g_kvl, g_ql, w_dkv, w_kr, w_uk, w_uv, w_dq, w_uq, tables):
    def body(h_ref, c_ref, sa_ref, sb_ref, gkv_ref, gl1_ref, gkvl_ref, gql_ref, wdkv_ref, wkr_ref, wuk_ref, wuv_ref,
             wdq_ref, wuq_ref, hk_ref, hn_ref, ckvr_ref, ckv_ref, kr_ref, kn_ref, v_ref, cqr_ref, cq_ref, q_ref):
        xv = h_ref[...]
        xn = xv * lax.rsqrt(jnp.mean(xv * xv, axis=-1, keepdims=True) + EPS)
        hk = (xn * gkv_ref[...]).astype(BF16)
        hn = (xn * gl1_ref[...]).astype(BF16)
        hk_ref[...] = hk
        hn_ref[...] = hn
        cv, sav, sbv = c_ref[...], sa_ref[...], sb_ref[...]
        raw = lax.dot_general(hk, wdkv_ref[...], NN, preferred_element_type=F32)
        ckvr_ref[...] = raw
        ckv = _rms(raw, gkvl_ref[...]).astype(BF16)
        ckv_ref[...] = ckv
        kr = lax.dot_general(hk, wkr_ref[...], NT, preferred_element_type=F32)
        kr_ref[...] = _rotate(kr, cv, sav, sbv, 1.0).astype(BF16)
        kn_ref[...] = lax.dot_general(ckv, wuk_ref[...], NN, preferred_element_type=F32).astype(BF16)
        v_ref[...] = lax.dot_general(ckv, wuv_ref[...], NN, preferred_element_type=F32).astype(BF16)
        cqr = lax.dot_general(hn, wdq_ref[...], NN, preferred_element_type=F32)
        cqr_ref[...] = cqr
        cq = _rms(cqr, gql_ref[...]).astype(BF16)
        cq_ref[...] = cq
        for h in range(N_HEADS):
            r = lax.dot_general(cq, wuq_ref[h], NT, preferred_element_type=F32)
            q_ref[h, :, :QK_NOPE] = (r[:, :QK_NOPE] * SCALE2).astype(BF16)
            q_ref[h, :, QK_NOPE:] = (_rotate(r[:, QK_NOPE:], cv, sav, sbv, 1.0) * SCALE2).astype(BF16)

    def rows(d):
        return pl.BlockSpec((TS, d), lambda i: (i, 0))

    def whole(a):
        return pl.BlockSpec(a.shape, lambda i: (0,) * a.ndim)

    wholes = [g_kv, g_l1, g_kvl, g_ql, w_dkv, w_kr, w_uk, w_uv, w_dq, w_uq]
    outs = [(D, BF16), (D, BF16), (KV_LORA, F32), (KV_LORA, BF16), (128, BF16), (N_HEADS * QK_NOPE, BF16),
            (N_HEADS * V_HEAD, BF16), (Q_LORA, F32), (Q_LORA, BF16)]
    return _pallas(
        body, name="attn_pre", grid=(T // TS,),
        in_specs=[rows(D), rows(128), rows(128), rows(128)] + [whole(a) for a in wholes],
        out_specs=[rows(d) for d, _ in outs] + [pl.BlockSpec((N_HEADS, TS, QK_PAD), lambda i: (0, i, 0))],
        out_shape=[jax.ShapeDtypeStruct((T, d), dt) for d, dt in outs]
        + [jax.ShapeDtypeStruct((N_HEADS, T, QK_PAD), BF16)],
        params=_params(("parallel",)))(h2, *tables, *wholes)


SCALE = (QK_NOPE + QK_ROPE) ** -0.5
LOG2E = 1.4426950408889634
SCALE2 = SCALE * LOG2E


def _diag_mask(transposed):
    shift = CHUNK.bit_length() - 1
    a = lax.broadcasted_iota(jnp.int32, (TQ, TQ), 0) >> shift
    b = lax.broadcasted_iota(jnp.int32, (TQ, TQ), 1) >> shift
    return (a <= b) if transposed else (b <= a)


def _as_row(col):
    return jnp.transpose(jnp.broadcast_to(col, (col.shape[0], 128)), (1, 0))[0:1]


def _keys(kn_ref, kr_ref, off):
    return jnp.concatenate([kn_ref[pl.ds(off, TQ), :], kr_ref[pl.ds(off, TQ), :]], axis=1)


def _attn_fwd(q, kn, kr, v):
    hp = 2

    def body(q_ref, kn_ref, kr_ref, v_ref, o_ref, lse_ref):
        i = pl.program_id(1)
        qs = [q_ref[a] for a in range(hp)]

        def step(j, carry, masked):
            off = pl.multiple_of(j * TQ, TQ)
            krv = kr_ref[pl.ds(off, TQ), :]
            ss = []
            for a in range(hp):
                kk = jnp.concatenate([kn_ref[pl.ds(off, TQ), a * QK_NOPE:(a + 1) * QK_NOPE], krv], axis=1)
                ss.append(lax.dot_general(qs[a], kk, NT, preferred_element_type=F32))
            out = []
            for a in range(hp):
                m, l, acc = carry[a]
                s = ss[a]
                if masked:
                    s = jnp.where(_diag_mask(False), s, NEG_INF)
                m_new = jnp.maximum(m, jnp.max(s, axis=-1, keepdims=True))
                p = jnp.exp2(s - m_new)
                alpha = jnp.exp2(m - m_new)
                l = alpha * l + jnp.sum(p, axis=-1, keepdims=True)
                pv = lax.dot_general(p.astype(BF16), v_ref[pl.ds(off, TQ), a * V_HEAD:(a + 1) * V_HEAD], NN,
                                     preferred_element_type=F32)
                out.append((m_new, l, alpha * acc + pv))
            return tuple(out)

        one = (jnp.full((TQ, 1), NEG_INF, F32), jnp.zeros((TQ, 1), F32), jnp.zeros((TQ, V_HEAD), F32))
        carry = lax.fori_loop(0, i, lambda j, cr: step(j, cr, False), (one,) * hp)
        carry = step(i, carry, True)
        for a, (m, l, acc) in enumerate(carry):
            o_ref[:, a * V_HEAD:(a + 1) * V_HEAD] = (acc / l).astype(BF16)
            lse_ref[a] = _as_row(m + jnp.log(l) * LOG2E)

    return _pallas(
        body, name="attn_fwd", grid=(N_HEADS // hp, T // TQ),
        in_specs=[pl.BlockSpec((hp, TQ, QK_PAD), lambda h, i: (h, i, 0)),
                  pl.BlockSpec((T, hp * QK_NOPE), lambda h, i: (0, h)),
                  pl.BlockSpec((T, 128), lambda h, i: (0, 0)),
                  pl.BlockSpec((T, hp * V_HEAD), lambda h, i: (0, h))],
        out_specs=[pl.BlockSpec((TQ, hp * V_HEAD), lambda h, i: (i, h)), pl.BlockSpec((hp, 1, TQ), lambda h, i: (h, 0, i))],
        out_shape=[jax.ShapeDtypeStruct((T, N_HEADS * V_HEAD), BF16), jax.ShapeDtypeStruct((N_HEADS, 1, T), F32)],
        params=_params(("parallel", "parallel")))(q, kn, kr, v)


def _attn_bwd(q, kn, kr, v, o, do, lse_row, tables):
    nq = T // TQ
    hp = 2
    cos, sa, sb = tables

    def body(q_ref, kn_ref, kr_ref, v_ref, o_ref, do_ref, lse_ref, c_ref, sa_ref, sb_ref,
             dq_ref, dkn_ref, dkr_ref, dv_ref, dq_acc, dl_ref):
        j = pl.program_id(1)

        def cols(a):
            return slice(a * 128, (a + 1) * 128)

        @pl.when(j == 0)
        def _():
            dq_acc[...] = jnp.zeros_like(dq_acc)
            for a in range(hp):
                for i in range(nq):
                    rows = pl.ds(i * TQ, TQ)
                    prod = do_ref[rows, cols(a)].astype(F32) * o_ref[rows, cols(a)].astype(F32)
                    dl_ref[a, :, rows] = _as_row(jnp.sum(prod, axis=-1, keepdims=True))

        krv = kr_ref[...]
        kks = [jnp.concatenate([kn_ref[:, cols(a)], krv], axis=1) for a in range(hp)]
        vvs = [v_ref[:, cols(a)] for a in range(hp)]

        def step(i, carry, masked):
            off = pl.multiple_of(i * TQ, TQ)
            rows = pl.ds(off, TQ)
            qis = [q_ref[a, rows, :] for a in range(hp)]
            dois = [do_ref[rows, cols(a)] for a in range(hp)]
            sts = [lax.dot_general(kks[a], qis[a], NT, preferred_element_type=F32) for a in range(hp)]
            dpts = [lax.dot_general(vvs[a], dois[a], NT, preferred_element_type=F32) for a in range(hp)]
            out = []
            for a in range(hp):
                dk, dv = carry[a]
                st = sts[a]
                if masked:
                    st = jnp.where(_diag_mask(True), st, NEG_INF)
                pt = jnp.exp2(st - lse_ref[a, :, rows])
                dv = dv + lax.dot_general(pt.astype(BF16), dois[a], NN, preferred_element_type=F32)
                dst = (pt * (dpts[a] - dl_ref[a, :, rows])).astype(BF16)
                dk = dk + lax.dot_general(dst, qis[a], NN, preferred_element_type=F32)
                dq_acc[a, rows, :] += lax.dot_general(dst, kks[a], TN, preferred_element_type=F32)
                out.append((dk, dv))
            return tuple(out)

        zero = (jnp.zeros((TQ, QK_PAD), F32), jnp.zeros((TQ, V_HEAD), F32))
        carry = step(j, (zero,) * hp, True)
        carry = lax.fori_loop(j + 1, nq, lambda i, cr: step(i, cr, False), carry)
        for a, (dk, dv) in enumerate(carry):
            dk = dk * (SCALE / SCALE2)
            dkn_ref[:, cols(a)] = dk[:, :QK_NOPE].astype(BF16)
            dkr_ref[a] = dk[:, QK_NOPE:]
            dv_ref[:, cols(a)] = dv.astype(BF16)

        @pl.when(j == nq - 1)
        def _():
            for a in range(hp):
                dq = dq_acc[a] * SCALE
                dq_ref[a, :, :QK_NOPE] = dq[:, :QK_NOPE].astype(BF16)
                dq_ref[a, :, QK_NOPE:] = _rotate(dq[:, QK_NOPE:], c_ref[...], sa_ref[...], sb_ref[...], -1.0).astype(BF16)

    row = pl.BlockSpec((hp, 1, T), lambda h, j: (h, 0, 0))
    head = pl.BlockSpec((TQ, hp * 128), lambda h, j: (j, h))
    whole = pl.BlockSpec((hp, T, QK_PAD), lambda h, j: (h, 0, 0))
    tab = pl.BlockSpec((T, 128), lambda h, j: (0, 0))
    heads = pl.BlockSpec((T, hp * V_HEAD), lambda h, j: (0, h))
    return _pallas(
        body, name="attn_bwd", grid=(N_HEADS // hp, nq),
        in_specs=[whole, head, pl.BlockSpec((TQ, 128), lambda h, j: (j, 0)), head, heads, heads, row, tab, tab, tab],
        out_specs=[whole, head, pl.BlockSpec((hp, TQ, 128), lambda h, j: (h, j, 0)), head],
        out_shape=[jax.ShapeDtypeStruct((N_HEADS, T, QK_PAD), BF16), jax.ShapeDtypeStruct((T, N_HEADS * QK_NOPE), BF16),
                   jax.ShapeDtypeStruct((N_HEADS, T, 128), F32), jax.ShapeDtypeStruct((T, N_HEADS * V_HEAD), BF16)],
        scratch_shapes=[pltpu.VMEM((hp, T, QK_PAD), F32), pltpu.VMEM((hp, 1, T), F32)],
        params=_params(("parallel", "arbitrary")))(q, kn, kr, v, o, do, lse_row, cos, sa, sb)


def _rms_bwd_math(xv, g, dy):
    r = lax.rsqrt(jnp.mean(xv * xv, axis=-1, keepdims=True) + EPS)
    xn = xv * r
    gdy = dy * g
    return r * (gdy - xn * jnp.mean(gdy * xn, axis=-1, keepdims=True)), jnp.sum(dy * xn, axis=0, keepdims=True)


def _attn_post(dq, dkn, dv, dkr, cq_raw, ckv_raw, h2, dres, g_ql, g_kvl, g_l1, g_kv, w_uq, w_uk, w_uv, w_dq, w_dkv,
               w_kr, tables):
    def body(dq_ref, dkn_ref, dv_ref, dkr_ref, cqr_ref, ckvr_ref, h_ref, res_ref, c_ref, sa_ref, sb_ref,
             gql_ref, gkvl_ref, gl1_ref, gkv_ref, wuq_ref, wuk_ref, wuv_ref, wdq_ref, wdkv_ref, wkr_ref,
             dcq_ref, dckv_ref, dkrr_ref, dh_ref, dhb_ref, dgql_ref, dgkvl_ref, dgl1_ref, dgkv_ref):
        dcq = lax.dot_general(dq_ref[0], wuq_ref[0], NN, preferred_element_type=F32)
        for h in range(1, N_HEADS):
            dcq = dcq + lax.dot_general(dq_ref[h], wuq_ref[h], NN, preferred_element_type=F32)
        dcq_raw, s_ql = _rms_bwd_math(cqr_ref[...], gql_ref[...], dcq)
        dcq_raw = dcq_raw.astype(BF16)
        dcq_ref[...] = dcq_raw
        dckv = (lax.dot_general(dkn_ref[...], wuk_ref[...], NT, preferred_element_type=F32)
                + lax.dot_general(dv_ref[...], wuv_ref[...], NT, preferred_element_type=F32))
        dckv_raw, s_kvl = _rms_bwd_math(ckvr_ref[...], gkvl_ref[...], dckv)
        dckv_raw = dckv_raw.astype(BF16)
        dckv_ref[...] = dckv_raw
        dkr = dkr_ref[0]
        for h in range(1, N_HEADS):
            dkr = dkr + dkr_ref[h]
        dkr_raw = _rotate(dkr, c_ref[...], sa_ref[...], sb_ref[...], -1.0).astype(BF16)
        dkrr_ref[...] = dkr_raw
        d_hn = lax.dot_general(dcq_raw, wdq_ref[...], NT, preferred_element_type=F32)
        d_hk = (lax.dot_general(dckv_raw, wdkv_ref[...], NT, preferred_element_type=F32)
                + lax.dot_general(dkr_raw, wkr_ref[...], NN, preferred_element_type=F32))
        xv = h_ref[...]
        r = lax.rsqrt(jnp.mean(xv * xv, axis=-1, keepdims=True) + EPS)
        xn = xv * r
        dx = res_ref[...]
        sums = [s_ql, s_kvl]
        for dy, g_ref in ((d_hn, gl1_ref), (d_hk, gkv_ref)):
            gdy = dy * g_ref[...]
            dx = dx + r * (gdy - xn * jnp.mean(gdy * xn, axis=-1, keepdims=True))
            sums.append(jnp.sum(dy * xn, axis=0, keepdims=True))
        dh_ref[...] = dx
        dhb_ref[...] = dx.astype(BF16)
        dg_refs = (dgql_ref, dgkvl_ref, dgl1_ref, dgkv_ref)

        @pl.when(pl.program_id(0) == 0)
        def _():
            for dg_ref, part in zip(dg_refs, sums):
                dg_ref[...] = part

        @pl.when(pl.program_id(0) > 0)
        def _():
            for dg_ref, part in zip(dg_refs, sums):
                dg_ref[...] += part

    def rows(d):
        return pl.BlockSpec((TS, d), lambda i: (i, 0))

    def heads(d):
        return pl.BlockSpec((N_HEADS, TS, d), lambda i: (0, i, 0))

    def whole(a):
        return pl.BlockSpec(a.shape, lambda i: (0,) * a.ndim)

    wholes = [g_ql, g_kvl, g_l1, g_kv, w_uq, w_uk, w_uv, w_dq, w_dkv, w_kr]
    vecs = [Q_LORA, KV_LORA, D, D]
    return _pallas(
        body, name="attn_post", grid=(T // TS,),
        in_specs=[heads(QK_PAD), rows(N_HEADS * QK_NOPE), rows(N_HEADS * V_HEAD), heads(128), rows(Q_LORA),
                  rows(KV_LORA), rows(D), rows(D), rows(128), rows(128), rows(128)] + [whole(a) for a in wholes],
        out_specs=[rows(Q_LORA), rows(KV_LORA), rows(128), rows(D), rows(D)]
        + [pl.BlockSpec((1, d), lambda i: (0, 0)) for d in vecs],
        out_shape=[jax.ShapeDtypeStruct((T, Q_LORA), BF16), jax.ShapeDtypeStruct((T, KV_LORA), BF16),
                   jax.ShapeDtypeStruct((T, 128), BF16), jax.ShapeDtypeStruct((T, D), F32),
                   jax.ShapeDtypeStruct((T, D), BF16)] + [jax.ShapeDtypeStruct((1, d), F32) for d in vecs],
        params=_params(("arbitrary",)))(dq, dkn, dv, dkr, cq_raw, ckv_raw, h2, dres, *tables, *wholes)


def _ffn_gup(name, dg, dv, hf):
    def body(dg_ref, dv_ref, hf_ref, o_ref):
        j = pl.program_id(0)

        @pl.when(j < N_FF_BLK)
        def _():
            o_ref[...] = lax.dot_general(dg_ref[...], hf_ref[...], TN, preferred_element_type=F32).astype(BF16)

        @pl.when(j >= N_FF_BLK)
        def _():
            o_ref[...] = lax.dot_general(dv_ref[...], hf_ref[...], TN, preferred_element_type=F32).astype(BF16)

    return _pallas(
        body, name=name, grid=(N_DEV,),
        in_specs=[pl.BlockSpec((None, T, FF_BLK), lambda j: (jnp.minimum(j, N_FF_BLK - 1), 0, 0)),
                  pl.BlockSpec((None, T, FF_BLK), lambda j: (jnp.maximum(j - N_FF_BLK, 0), 0, 0)),
                  pl.BlockSpec((T, D), lambda j: (0, 0))],
        out_specs=pl.BlockSpec((None, FF_BLK, D), lambda j: (j, 0, 0)),
        out_shape=jax.ShapeDtypeStruct((N_DEV, FF_BLK, D), BF16), params=_params(("parallel",)))(dg, dv, hf)


def _ffn_layer_fwd(tag, h, gain, ex):
    hf = _rms_fwd(f"{tag}_norm", h, gain)
    g, v, act = _ffn_up_act(f"{tag}_up", hf, ex.need(f"ffn_w_up{tag[1]}", hf), ex.need(f"ffn_cw{tag[1]}", hf),
                            ex.need(f"ffn_cb{tag[1]}", hf))
    ex.at(f"{tag}_up", act)
    rows = pl.BlockSpec((TS, D), lambda i: (i, 0))
    out = _mm_sum(f"{tag}_down",
                  [(act, pl.BlockSpec((N_FF_BLK, TS, FF_BLK), lambda i: (0, i, 0)), ex.need(f"ffn_w_down{tag[1]}", act),
                    pl.BlockSpec((None, N_FF_BLK, FF_BLK, D), lambda i: (0, 0, 0, 0)), NN, 0)],
                  grid=(T // TS,), o_spec=rows, o_shape=(T, D), o_dtype=F32, add=h)
    ex.at(f"{tag}_down", out)
    return out, (hf, g, v, act)


def _ffn_layer_bwd(tag, h, gain, ex, saved, dh, dh_bf):
    hf, g, v, act = saved
    layer = tag[1]
    w_up, w_down4 = ex.need(f"ffn_w_up{layer}", dh_bf), ex.need(f"ffn_w_down{layer}", dh_bf)
    dg, dv, dcw, dcb = _ffn_dact(f"{tag}_dact", dh_bf, w_down4, g, v, ex.need(f"ffn_cw{layer}", dh_bf),
                                 ex.need(f"ffn_cb{layer}", dh_bf))
    ex.at(f"{tag}_dact", dg)
    g_down = _mm(f"{tag}_gdown", act, dh_bf, grid=(N_FF_BLK,),
                 a_spec=pl.BlockSpec((None, T, FF_BLK), lambda j: (j, 0, 0)),
                 b_spec=pl.BlockSpec((T, D), lambda j: (0, 0)),
                 o_spec=pl.BlockSpec((FF_BLK, D), lambda j: (j, 0)),
                 o_shape=(D_FF, D), o_dtype=BF16, dims=TN)
    g_up = _ffn_gup(f"{tag}_gup", dg, dv, hf)
    ex.grad("ffn_w_up", int(layer), g_up.reshape(1, N_DEV, FF_BLK, D))
    ex.grad("ffn_w_down", int(layer), g_down.reshape(1, N_DEV, D_FF // N_DEV, D))
    ex.at(f"{tag}_gup", g_up)
    part = pl.BlockSpec((N_FF_BLK, TR, FF_BLK), lambda i: (0, i, 0))
    dh_in, dh_in_bf, dgain = _mm_sum(
        f"{tag}_dhf",
        [(dg, part, w_up, pl.BlockSpec((None, N_FF_BLK, FF_BLK, D), lambda i: (0, 0, 0, 0)), NN, 0),
         (dv, part, w_up, pl.BlockSpec((None, N_FF_BLK, FF_BLK, D), lambda i: (0, 1, 0, 0)), NN, 0)],
        grid=(T // TR,), o_spec=pl.BlockSpec((TR, D), lambda i: (i, 0)), o_shape=(T, D), o_dtype=F32,
        norm_bwd=(h, [gain], dh))
    ex.at(f"{tag}_dhf", dh_in)
    return dh_in, dh_in_bf, dgain[0], dcw, dcb


def _local_step(x, pos, tgt, rep, ex):
    attn_norm, ffn_norm, final_norm = rep["attn_norm"], rep["ffn_norm"], rep["final_norm"]
    half = QK_ROPE // 2
    inv = 1.0 / (ROPE_THETA ** (jnp.arange(half, dtype=F32) / half))
    inv_freq = jnp.concatenate([inv, inv, jnp.zeros((128 - 2 * half,), F32)]).reshape(1, 128)
    tables = _rope_tables(pos, inv_freq)

    hn0 = _rms_fwd("l0_norm", x, attn_norm[0:1])
    ex.at("l0_norm", hn0)
    w_in = ex.need("sc_w_in", hn0)
    zb, zc, zu, y = _mixer_in(hn0, w_in, ex.need("sc_conv_w", hn0))
    ex.at("l0_in", y)
    h1 = _mm_rows("l0_out", y, ex.need("sc_w_out", y), NN, F32, D, tn=512, add=x)
    ex.at("l0_out", h1)
    h2, ffn0 = _ffn_layer_fwd("f0", h1, ffn_norm[0:1], ex)

    w_uq = ex.need("w_uq", h2)
    hk, hn1, ckv_raw, ckv, kr, kn, vv, cq_raw, cq, q = _attn_pre(
        h2, rep["kv_in_norm"], attn_norm[1:2], rep["kv_latent_norm"], rep["q_latent_norm"], ex.need("w_dkv", h2),
        ex.need("w_kr", h2), ex.need("w_uk", h2), ex.need("w_uv", h2), ex.need("w_dq", h2), w_uq, tables)

    o, lse = _attn_fwd(q, kn, kr, vv)
    ex.at("attn_fwd", o)
    w_o = ex.need("w_o", o)
    h3 = _mm_rows("attn_out", o, w_o, NN, F32, D, tn=512, add=h2)
    h4, ffn1 = _ffn_layer_fwd("f1", h3, ffn_norm[1:2], ex)

    loss, dh4, dh4_bf, d_final = _final(h4, final_norm.reshape(1, D), tgt)

    dh3, dh3_bf, d_fn1, dcw1, dcb1 = _ffn_layer_bwd("f1", h3, ffn_norm[1:2], ex, ffn1, dh4, dh4_bf)
    ex.at("f1_bwd", dh3)

    do = _mm_rows("d_attn_out", dh3_bf, w_o, NT, BF16, N_HEADS * V_HEAD)
    dq_pre, dkn, dkr, dvv = _attn_bwd(q, kn, kr, vv, o, do, lse, tables)

    dcq_raw_bf, dckv_raw_bf, dkr_raw_bf, dh2, dh2_bf, d_qln, d_kvln, d_an1, d_kvin = _attn_post(
        dq_pre, dkn, dvv, dkr, cq_raw, ckv_raw, h2, dh3, rep["q_latent_norm"], rep["kv_latent_norm"], attn_norm[1:2],
        rep["kv_in_norm"], w_uq, ex.need("w_uk", dkn), ex.need("w_uv", dvv), ex.need("w_dq", dq_pre),
        ex.need("w_dkv", dkn), ex.need("w_kr", dkr), tables)
    g_uq, g_dq, g_o = _wgrads("g_q", [(dq_pre, cq), (hn1, dcq_raw_bf), (o, dh3_bf)])
    ex.grad("w_uq", None, g_uq[:, :QK_NOPE + QK_ROPE].reshape(1, N_DEV, QK_NOPE + QK_ROPE, Q_LORA))
    ex.grad("w_dq", None, g_dq.reshape(1, N_DEV, D // N_DEV, Q_LORA))
    ex.grad("w_o", None, g_o.reshape(1, N_DEV, D // N_DEV, D))

    g_uk, g_uv, g_dkv, g_kr = _wgrads("g_kv", [(ckv, dkn), (ckv, dvv), (hk, dckv_raw_bf), (dkr_raw_bf, hk)])
    ex.grad("w_uk", None, g_uk)
    ex.grad("w_uv", None, g_uv)
    ex.grad("w_dkv", None, g_dkv.reshape(1, N_DEV, D // N_DEV, KV_LORA))
    ex.grad("w_kr", None, g_kr[:QK_ROPE])
    ex.at("kv_bwd", dh2)

    dh1, dh1_bf, d_fn0, dcw0, dcb0 = _ffn_layer_bwd("f0", h1, ffn_norm[0:1], ex, ffn0, dh2, dh2_bf)
    ex.at("f0_bwd", dh1)

    ex.grad("sc_w_out", None, _mm_wgrad("g_sc_w_out", y, dh1_bf).reshape(1, N_DEV, D // N_DEV, D))
    dz, d_scw = _mixer_out_bwd(dh1_bf, ex.need("sc_w_out", dh1_bf), zb, zc, zu, ex.need("sc_conv_w", dh1_bf))
    g_in = _mm_wgrad("g_sc_w_in", hn0, dz)
    ex.grad("sc_w_in", None, g_in)
    ex.at("sc_bwd", g_in)
    ex.at("d_l0_in", g_in)
    w_in = ex.need("sc_w_in", dz)
    grad_x, _, (d_an0,) = _mm_sum(
        "d_l0_in", [(dz[None], pl.BlockSpec((1, TS, dz.shape[1]), lambda i: (0, i, 0)),
                     w_in[None], pl.BlockSpec((1,) + w_in.shape, lambda i: (0, 0, 0)), NT, 0)],
        norm_bwd=(x, [attn_norm[0:1]], dh1),
        grid=(T // TS,), o_spec=pl.BlockSpec((TS, D), lambda i: (i, 0)), o_shape=(T, D), o_dtype=F32)

    small = {
        "attn_norm": jnp.concatenate([d_an0, d_an1], axis=0),
        "ffn_norm": jnp.concatenate([d_fn0, d_fn1], axis=0),
        "final_norm": d_final.reshape(D),
        "kv_in_norm": d_kvin.reshape(D),
        "kv_latent_norm": d_kvln.reshape(KV_LORA),
        "q_latent_norm": d_qln,
        "ffn_conv_b": jnp.stack([dcb0, dcb1]).transpose(0, 2, 1, 3).reshape(2, D_FF),
        "sc_conv_w": d_scw,
        "ffn_conv_w": jnp.stack([dcw0, dcw1]).transpose(0, 2, 1, 3).reshape(2, 3, D_FF),
    }
    return loss, grad_x, small


def _place():
    return lax.axis_index("x"), lax.axis_index("y"), lax.axis_index("c")


def _peers():
    x, y, c = _place()
    return (x, y, 1 - c), [(1 - x, y), (x, 1 - y), (1 - x, 1 - y)]


def _window(ref, kind, dev):
    if kind == "blocked":
        return ref.at[:, dev]
    width = ref.shape[-1] // N_DEV
    return ref.at[:, pl.ds(pl.multiple_of(dev * width, 128), width)]


HBM_SPEC = pl.BlockSpec(memory_space=pltpu.HBM)
SEM_SPEC = pl.BlockSpec(memory_space=pltpu.SEMAPHORE)
EFFECT = pltpu.SideEffectType.DATAFLOW_SIDE_EFFECTING
TOKEN = jax.ShapeDtypeStruct((8, 128), F32)


def _hbm(a):
    return pltpu.with_memory_space_constraint(a, pltpu.HBM)


def _copies_start(name, jobs):
    nj = len(jobs)
    counts = [(len(srcs), len(lands)) for srcs, lands, _, _ in jobs]
    n_arr = sum(ns + nl for ns, nl in counts)

    def body(*refs):
        sems, token = refs[n_arr:n_arr + 2 * nj], refs[-1]
        at = 0
        for j, ((ns, nl), (_, _, ncopy, plan)) in enumerate(zip(counts, jobs)):
            copies = plan(refs[at:at + ns], refs[at + ns:at + ns + nl])
            assert len(copies) == ncopy
            for k, (sent, dst, to, _) in enumerate(copies):
                pltpu.make_async_remote_copy(src_ref=sent, dst_ref=dst, send_sem=sems[2 * j].at[k],
                                             recv_sem=sems[2 * j + 1].at[k], device_id=to, device_id_type=MESH).start()
            at += ns + nl
        token[...] = jnp.zeros_like(token)

    arrays = [a for srcs, lands, _, _ in jobs for a in list(srcs) + list(lands)]
    sem_shapes = [pltpu.SemaphoreType.DMA((ncopy,)) for _, _, ncopy, _ in jobs for _ in range(2)]
    outs = pl.pallas_call(
        body, name=name, in_specs=[HBM_SPEC] * n_arr,
        out_specs=[SEM_SPEC] * (2 * nj) + [HBM_SPEC] * n_arr + [VMEM_SPEC],
        out_shape=sem_shapes + [pltpu.HBM(a.shape, a.dtype) for a in arrays] + [TOKEN],
        input_output_aliases={i: 2 * nj + i for i in range(n_arr)},
        compiler_params=pltpu.CompilerParams(has_side_effects=EFFECT))(*[_hbm(a) for a in arrays])
    _Chain.last = outs[-1]
    flights, at = [], 2 * nj
    for j, (ns, nl) in enumerate(counts):
        flights.append((outs[2 * j], outs[2 * j + 1], list(outs[at:at + ns]), list(outs[at + ns:at + ns + nl])))
        at += ns + nl
    return flights


def _copies_wait(name, started, ncopy, plan):
    send, recv, srcs, lands = started
    ns, nl = len(srcs), len(lands)

    def body(*refs):
        send_ref, recv_ref, token = refs[ns + nl], refs[ns + nl + 1], refs[-1]
        copies = plan(refs[:ns], refs[ns:ns + nl])
        assert len(copies) == ncopy
        for k, (sent, _, to, landed) in enumerate(copies):
            cp = pltpu.make_async_remote_copy(src_ref=sent, dst_ref=landed, send_sem=send_ref.at[k],
                                              recv_sem=recv_ref.at[k], device_id=to, device_id_type=MESH)
            cp.wait_send()
            cp.wait_recv()
        token[...] = jnp.zeros_like(token)

    arrays = list(srcs) + list(lands)
    outs = pl.pallas_call(
        body, name=name, in_specs=[HBM_SPEC] * (ns + nl) + [SEM_SPEC] * 2 + [ANY_SPEC],
        out_specs=[HBM_SPEC] * (ns + nl) + [VMEM_SPEC], out_shape=[pltpu.HBM(a.shape, a.dtype) for a in arrays] + [TOKEN],
        input_output_aliases={i: i for i in range(ns + nl)},
        compiler_params=pltpu.CompilerParams(has_side_effects=EFFECT))(*arrays, send, recv, _Chain.last)
    _Chain.last = outs[-1]
    return list(outs[:ns]), list(outs[ns:-1])


def _plan_gather_chips(kinds):
    def plan(srcs, lands):
        x, y, c = _place()
        sibling, chips = _peers()
        out = []
        for t, kind in enumerate(kinds):
            mine = _window(lands[t], kind, 4 * x + 2 * y + c)
            out.append((srcs[t], mine, (x, y, c), mine))
            out.append((srcs[t], mine, sibling, _window(lands[t], kind, 4 * x + 2 * y + 1 - c)))
            for px, py in chips:
                out.append((srcs[t], mine, (px, py, c), _window(lands[t], kind, 4 * px + 2 * py + c)))
        return out
    return plan, 5 * len(kinds)


def _plan_gather_all(n):
    def plan(srcs, lands):
        x, y, c = _place()
        out = []
        for t in range(n):
            mine = lands[t].at[:, 4 * x + 2 * y + c]
            for m in range(N_DEV):
                px, py, pc = (1 - x if m & 4 else x), (1 - y if m & 2 else y), (1 - c if m & 1 else c)
                out.append((srcs[t], mine, (px, py, pc), lands[t].at[:, 4 * px + 2 * py + pc]))
        return out
    return plan, N_DEV * n


def _plan_gather_sibling(kinds):
    def plan(srcs, lands):
        _, _, c = _place()
        sibling, chips = _peers()
        out = []
        for t, kind in enumerate(kinds):
            for px, py in chips:
                w = _window(lands[t], kind, 4 * px + 2 * py + c)
                out.append((w, w, sibling, _window(lands[t], kind, 4 * px + 2 * py + 1 - c)))
        return out
    return plan, 3 * len(kinds)


def _plan_scatter_sibling(kinds):
    def plan(srcs, lands):
        _, _, c = _place()
        sibling, _ = _peers()
        out = []
        for t, kind in enumerate(kinds):
            for k in range(N_CHIP):
                out.append((_window(srcs[t], kind, 2 * k + 1 - c), lands[t].at[k], sibling, lands[t].at[k]))
        return out
    return plan, N_CHIP * len(kinds)


def _plan_scatter_chips(n):
    def plan(srcs, lands):
        x, y, c = _place()
        _, chips = _peers()
        out = []
        for t in range(n):
            for px, py in chips:
                out.append((srcs[t].at[2 * px + py], lands[t].at[2 * x + y], (px, py, c), lands[t].at[2 * px + py]))
        return out
    return plan, 3 * n


def _landing(shard, kind):
    if kind == "blocked":
        return lax.empty((shard.shape[0], N_DEV) + shard.shape[1:], shard.dtype)
    return lax.empty((shard.shape[0], N_DEV * shard.shape[1]), shard.dtype)


def _chip_sums(name, grads, kinds, recvs, c):
    n = len(grads)
    in_specs, out_specs, out_shape, args = [], [], [], []
    for gr, kind, rv in zip(grads, kinds, recvs):
        if kind == "blocked":
            rows, w = gr.shape[2], gr.shape[3]
            in_specs.append(pl.BlockSpec((None, None, rows, w), lambda k, cref: (0, 2 * k + cref[0], 0, 0)))
        else:
            rows, w = gr.shape[0], gr.shape[1] // N_DEV
            in_specs.append(pl.BlockSpec((rows, w), lambda k, cref: (0, 2 * k + cref[0])))
        blk = pl.BlockSpec((None, rows, w), lambda k, cref: (k, 0, 0))
        in_specs.append(blk)
        out_specs.append(blk)
        out_shape.append(jax.ShapeDtypeStruct((N_CHIP, rows, w), BF16))
        args += [gr, rv.reshape(N_CHIP, rows, w)]

    def body(*refs):
        for t in range(n):
            g_ref, r_ref, o_ref = refs[1 + 2 * t], refs[2 + 2 * t], refs[1 + 2 * n + t]
            o_ref[...] = (g_ref[...].astype(F32) + r_ref[...].astype(F32)).astype(BF16)

    return _pallas(body, name=name, n_prefetch=1, grid=(N_CHIP,), in_specs=in_specs, out_specs=out_specs,
                   out_shape=out_shape, params=_params(("parallel",)))(c, *args)


def _adamw_math(g, wv, mv, vv):
    m = ADAM_B1 * mv + (1.0 - ADAM_B1) * g
    v = ADAM_B2 * vv + (1.0 - ADAM_B2) * (g * g)
    m_hat = m / (1.0 - ADAM_B1 ** ADAM_STEP)
    v_hat = v / (1.0 - ADAM_B2 ** ADAM_STEP)
    delta = -ADAM_LR * (m_hat / (jnp.sqrt(v_hat) + ADAM_EPS) + ADAM_WD * wv)
    return delta, m, v


ADAM_STEPS = 2


def _adamw_group(name, items, chip_ids):
    n = len(items)
    in_specs, out_specs, out_shape, args, prevs = [], [], [], [chip_ids], []
    for own, recv, w3, m3, v3, layer, _ in items:
        nl, rows, w = w3.shape
        tr = rows // ADAM_STEPS
        assert tr % 16 == 0, (name, rows)
        in_specs += [pl.BlockSpec((None, tr, w), lambda i, ids, slot=slot: (ids[slot], i, 0)) for slot in range(4)]
        slab = pl.BlockSpec((None, tr, w), lambda i, ids, layer=layer: (layer, i, 0))
        in_specs += [slab] * 3
        out_specs += [slab] * 4
        out_shape += [jax.ShapeDtypeStruct((nl, rows, w), F32)] * 4
        args += [own, recv, recv, recv, w3, m3, v3]
    aliases = {}
    for t, item in enumerate(items):
        if item[6] is not None:
            for k in range(4):
                aliases[len(args) + k] = 4 * t + k
            in_specs += [ANY_SPEC] * 4
            args += list(item[6])
            prevs.append(t)
    n_in = 1 + 7 * n + 4 * len(prevs)

    def body(*refs):
        for t in range(n):
            own_ref, r1_ref, r2_ref, r3_ref, w_ref, m_ref, v_ref = refs[1 + 7 * t:8 + 7 * t]
            g_ref, d_ref, nm_ref, nv_ref = refs[n_in + 4 * t:n_in + 4 * t + 4]
            g = ((own_ref[...].astype(F32) + r1_ref[...].astype(F32)) + r2_ref[...].astype(F32)) + r3_ref[...].astype(F32)
            g_ref[...] = g
            d_ref[...], nm_ref[...], nv_ref[...] = _adamw_math(g, w_ref[...], m_ref[...], v_ref[...])

    outs = _pallas(body, name=name, n_prefetch=1, grid=(ADAM_STEPS,), in_specs=in_specs, out_specs=out_specs,
                   out_shape=out_shape, aliases=aliases, params=_params(("parallel",)))(*args)
    return [list(outs[4 * t:4 * t + 4]) for t in range(n)]


SHARD_ROWS = 8


def _adamw_small(packs, views, ws, ms, vs, me):
    n = len(views)
    full = [w is not None for w in ws]
    sharded = [w is not None and w.ndim == 3 for w in ws]
    args = list(packs)
    out_shape = []
    for t in range(n):
        shape = jax.ShapeDtypeStruct(ws[t].shape if sharded[t] else views[t][3:], F32)
        if full[t]:
            args += [ws[t], ms[t], vs[t]]
            out_shape += [shape] * 4
        else:
            out_shape += [shape]

    def body(*refs):
        i_in, i_out = len(packs), len(args) + 1
        me_ref = refs[len(args)]
        for t in range(n):
            pack, r0, c0, r, c = views[t]
            p_ref = refs[pack]
            cols = slice(c0, c0 + c)
            if sharded[t]:
                w_ref, m_ref, v_ref = refs[i_in:i_in + 3]
                taps, layers, _ = w_ref.shape
                mine = pl.ds(pl.multiple_of(me_ref[0] * SHARD_ROWS, SHARD_ROWS), SHARD_ROWS)
                g = p_ref[0, 0, mine, cols]
                for k in range(1, N_DEV):
                    g = g + p_ref[0, k, mine, cols]
                for l in range(layers):
                    for k in range(taps):
                        at = (k, slice(l, l + 1), slice(None))
                        row = g[l * taps + k:l * taps + k + 1]
                        refs[i_out][at] = row
                        refs[i_out + 1][at], refs[i_out + 2][at], refs[i_out + 3][at] = _adamw_math(
                            row, w_ref[at], m_ref[at], v_ref[at])
                i_in += 3
                i_out += 4
                continue
            g = p_ref[0, 0, r0:r0 + r, cols]
            for k in range(1, N_DEV):
                g = g + p_ref[0, k, r0:r0 + r, cols]
            refs[i_out][...] = g
            if full[t]:
                w_ref, m_ref, v_ref = refs[i_in:i_in + 3]
                refs[i_out + 1][...], refs[i_out + 2][...], refs[i_out + 3][...] = _adamw_math(
                    g, w_ref[...], m_ref[...], v_ref[...])
                i_in += 3
                i_out += 4
            else:
                i_out += 1

    outs = _pallas(body, name="adamw_small",
                   in_specs=[VMEM_SPEC] * len(args) + [pl.BlockSpec(memory_space=pltpu.SMEM)],
                   out_specs=[VMEM_SPEC] * len(out_shape), out_shape=out_shape,
                   params=pltpu.CompilerParams(vmem_limit_bytes=VMEM_LIMIT))(*args, me)
    result, i = [], 0
    for t in range(n):
        k = 4 if full[t] else 1
        result.append(list(outs[i:i + k]))
        i += k
    return result


KIND = {"sc_w_in": "cols", "sc_w_out": "blocked", "w_dkv": "blocked", "w_kr": "cols", "w_uk": "cols", "w_uv": "cols",
        "w_dq": "blocked", "w_uq": "blocked", "w_o": "blocked", "ffn_w_up": "blocked", "ffn_w_down": "blocked",
        "conv": "blocked"}
GATHER_GROUPS = (("mixer", ("sc_w_in", "sc_w_out", "conv")),
                 ("up0", ("ffn_w_up0",)),
                 ("down0", ("ffn_w_down0",)),
                 ("attn", ("w_dkv", "w_kr", "w_uk", "w_uv", "w_dq", "w_uq", "w_o")),
                 ("ffn1", ("ffn_w_up1", "ffn_w_down1")))
SCATTER_GROUPS = (("ffn1", (("ffn_w_up", 1), ("ffn_w_down", 1))),
                  ("attn", (("w_o", None), ("w_uq", None), ("w_dq", None), ("w_uk", None), ("w_uv", None),
                            ("w_dkv", None), ("w_kr", None))),
                  ("ffn0", (("ffn_w_up", 0), ("ffn_w_down", 0))),
                  ("mixer", (("sc_w_out", None), ("sc_w_in", None))))
SCHEDULE = {
    "begin": (("gather_start", "mixer"),),
    "l0_norm": (("gather_forward", "mixer"), ("gather_start", "up0")),
    "l0_out": (("gather_forward", "up0"), ("gather_start", "down0"), ("gather_start", "attn")),
    "f0_up": (("gather_forward", "down0"), ("gather_forward", "attn"), ("gather_start", "ffn1")),
    "attn_fwd": (("gather_forward", "ffn1"),),
    "f1_gup": (("scatter_sibling", "ffn1"),),
    "f1_dhf": (("scatter_chips", "ffn1"),),
    "kv_bwd": (("scatter_sibling", "attn"),),
    "f0_dact": (("scatter_chips", "attn"),),
    "f0_gup": (("scatter_sibling", "ffn0"),),
    "f0_dhf": (("scatter_chips", "ffn0"),),
    "sc_bwd": (("scatter_sibling", "mixer"), ("scatter_done", "attn")),
    "d_l0_in": (("scatter_chips", "mixer"),),
}
FINISH = (("scatter_done", "ffn1"), ("scatter_done", "ffn0"), ("scatter_done", "mixer"))
STAGES = {"gather_start": 1, "gather_forward": 2, "gather_done": 3,
          "scatter_sibling": 1, "scatter_chips": 2, "scatter_done": 3}
SMALL_W_ROWS = 24


def _pack(arrays, rows):
    flat = jnp.concatenate([a.reshape(-1).astype(F32) for a in arrays])
    return jnp.pad(flat, (0, rows * 128 - flat.shape[0])).reshape(rows, 128)


def _cast_shards(items):
    arrays = []
    for a, _, _ in items:
        if not any(a is b for b in arrays):
            arrays.append(a)
    slot = [next(i for i, b in enumerate(arrays) if b is a) for a, _, _ in items]

    def body(*refs):
        for t, (a, layer, rows) in enumerate(items):
            w_ref, o_ref = refs[slot[t]], refs[len(arrays) + t]
            r, c = a.shape[-2:]
            if a.ndim == 3:
                o_ref[:, :r] = w_ref[(layer or 0):(layer or 0) + 1].astype(BF16)
                if rows > r:
                    o_ref[:, r:] = jnp.zeros((1, rows - r, c), BF16)
            else:
                o_ref[:r] = w_ref[...].astype(BF16)
                if rows > r:
                    o_ref[r:] = jnp.zeros((rows - r, c), BF16)

    out_shape = [jax.ShapeDtypeStruct(((1,) if a.ndim == 3 else ()) + (rows, a.shape[-1]), BF16)
                 for a, _, rows in items]
    return _pallas(body, name="cast_shards", in_specs=[VMEM_SPEC] * len(arrays), out_specs=[VMEM_SPEC] * len(items),
                   out_shape=out_shape, params=pltpu.CompilerParams(vmem_limit_bytes=VMEM_LIMIT))(*arrays)


STORED_TRANSPOSED = ("ffn_w_up", "w_uq", "w_kr")


def _stored(name, a):
    return jnp.swapaxes(a, -1, -2) if name in STORED_TRANSPOSED else a


def _base(name):
    if name.startswith("ffn_w_") and name[-1] in "01":
        return name[:-1], int(name[-1])
    return name, None


class _Exchange:
    def __init__(self, wts, mom, var, ffn_conv_b):
        self.wts, self.mom, self.var = wts, mom, var
        x, y, c = _place()
        self.c_arr = jnp.reshape(c, (1,)).astype(jnp.int32)
        chip = 2 * x + y
        self.chip_ids = jnp.stack([chip, chip ^ 1, chip ^ 2, chip ^ 3]).astype(jnp.int32)
        self.ready = {"ffn_cb0": ffn_conv_b.reshape(2, N_FF_BLK, 1, FF_BLK)[0],
                      "ffn_cb1": ffn_conv_b.reshape(2, N_FF_BLK, 1, FF_BLK)[1]}
        self.gathers, self.group_of = {}, {}
        self.grads, self.scatters, self.results, self.queue = {}, {}, {}, []
        for gname, names in GATHER_GROUPS:
            self.gathers[gname] = dict(stage=0, names=names, kinds=[KIND[_base(nm)[0]] for nm in names])
            for nm in names:
                self.group_of[nm] = gname
        for nm in ("sc_conv_w", "ffn_cw0", "ffn_cw1"):
            self.group_of[nm] = "mixer"
        self.cast, self.f32 = {}, {}
        self.at("begin", None)
        later = [nm for gname, names in GATHER_GROUPS[1:] for nm in names]
        self.cast = dict(zip(later, _cast_shards([self._shard_f32(nm) for nm in later])))

    def _shard_f32(self, name):
        base, layer = _base(name)
        if base not in self.f32:
            a = _stored(base, self.wts[base])
            self.f32[base] = a.reshape(a.shape[-2:]) if KIND[base] == "cols" else a.reshape((-1,) + a.shape[-2:])
        a = self.f32[base]
        return a, layer, {"w_kr": 128, "w_uq": QK_PAD}.get(base, a.shape[-2])

    def _shard(self, name):
        if name in self.cast:
            return self.cast[name]
        if name == "conv":
            return _pack([self.wts["sc_conv_w"], self.wts["ffn_conv_w"]], SMALL_W_ROWS).reshape(1, SMALL_W_ROWS, 128)
        base, layer = _base(name)
        a = _stored(base, self.wts[base])
        if layer is not None:
            a = a[layer:layer + 1]
        if KIND[base] == "cols":
            return a.reshape(a.shape[-2], a.shape[-1]).astype(BF16)
        return a.reshape((-1,) + a.shape[-2:]).astype(BF16)

    def _start(self, name, srcs, lands, ncopy, plan, st):
        self.queue.append((name, (srcs, lands, ncopy, plan), st))

    def _flush(self):
        if self.queue:
            flights = _copies_start("__".join(name for name, _, _ in self.queue), [job for _, job, _ in self.queue])
            for (_, _, st), flight in zip(self.queue, flights):
                st["flight"] = flight
            self.queue = []

    def _flight(self, st):
        self._flush()
        return st["flight"]

    def _gather_to(self, gname, stage, after):
        st = self.gathers[gname]
        if st["stage"] < 1 <= stage:
            shards = [self._shard(nm) for nm in st["names"]]
            lands = [_landing(s, kind) for s, kind in zip(shards, st["kinds"])]
            plan, ncopy = _plan_gather_chips(st["kinds"])
            self._start(f"ag_{gname}_chips", shards, lands, ncopy, plan, st)
            st["stage"] = 1
        if st["stage"] < 2 <= stage:
            plan, ncopy = _plan_gather_chips(st["kinds"])
            _, lands = _copies_wait(f"ag_{gname}_chips_wait", self._flight(st), ncopy, plan)
            plan, ncopy = _plan_gather_sibling(st["kinds"])
            self._start(f"ag_{gname}_sibling", [], lands, ncopy, plan, st)
            st["stage"] = 2
        if st["stage"] < 3 <= stage:
            plan, ncopy = _plan_gather_sibling(st["kinds"])
            _, lands = _copies_wait(f"ag_{gname}_sibling_wait", self._flight(st), ncopy, plan)
            for nm, land in zip(st["names"], lands):
                self._arrived(nm, land)
            st["stage"] = 3

    def _arrived(self, name, land):
        if name == "conv":
            conv = land.reshape(N_DEV, SMALL_W_ROWS * 128)
            self.ready["sc_conv_w"] = conv[:, :3 * 128].reshape(N_DEV, 3, 128).transpose(1, 0, 2).reshape(3, D)
            fcw = conv[:, 3 * 128:3 * 128 + 6 * 352].reshape(N_DEV, 2, 3, 352).transpose(1, 2, 0, 3)
            fcw = fcw.reshape(2, 3, N_FF_BLK, FF_BLK).transpose(0, 2, 1, 3)
            self.ready["ffn_cw0"], self.ready["ffn_cw1"] = fcw[0], fcw[1]
        elif name in ("sc_w_in", "w_uk", "w_uv", "w_kr") or name.startswith("ffn_w_up"):
            self.ready[name] = land
        elif name.startswith("ffn_w_down"):
            self.ready[name] = land.reshape(1, N_FF_BLK, FF_BLK, D)
        elif name == "w_uq":
            self.ready[name] = land.reshape(N_HEADS, QK_PAD, Q_LORA)
        else:
            self.ready[name] = land.reshape(D, land.shape[-1])

    def need(self, name, after):
        if name not in self.ready:
            self._gather_to(self.group_of[name], 3, after)
            self._flush()
        return self.ready[name]

    def grad(self, name, layer, array):
        self.grads[(name, layer)] = array

    def _scatter_to(self, gname, stage, after):
        keys = dict(SCATTER_GROUPS)[gname]
        st = self.scatters.setdefault(gname, dict(stage=0))
        kinds = [KIND[nm] for nm, _ in keys]
        if st["stage"] < 1 <= stage:
            grads = [self.grads[key] for key in keys]
            lands = []
            for gr, kind in zip(grads, kinds):
                shard = (gr.shape[0],) + gr.shape[2:] if kind == "blocked" else (gr.shape[0], gr.shape[1] // N_DEV)
                lands.append(lax.empty((N_CHIP,) + shard, BF16))
            plan, ncopy = _plan_scatter_sibling(kinds)
            self._start(f"rs_{gname}_sibling", grads, lands, ncopy, plan, st)
            st["stage"] = 1
        if st["stage"] < 2 <= stage:
            plan, ncopy = _plan_scatter_sibling(kinds)
            grads, recvs = _copies_wait(f"rs_{gname}_sibling_wait", self._flight(st), ncopy, plan)
            sums = _chip_sums(f"rs_{gname}_sums", grads, kinds, recvs, self.c_arr)
            lands = [lax.empty(s.shape, BF16) for s in sums]
            plan, ncopy = _plan_scatter_chips(len(sums))
            self._start(f"rs_{gname}_chips", sums, lands, ncopy, plan, st)
            st["stage"] = 2
        if st["stage"] < 3 <= stage:
            plan, ncopy = _plan_scatter_chips(len(keys))
            sums, recvs = _copies_wait(f"rs_{gname}_chips_wait", self._flight(st), ncopy, plan)
            items = []
            for (nm, layer), own, rv in zip(keys, sums, recvs):
                nl = 1 if layer is None else 2
                rows, w = own.shape[1], own.shape[2]
                w3, m3, v3 = (_stored(nm, src[nm]).reshape(nl, rows, w) for src in (self.wts, self.mom, self.var))
                items.append((own, rv, w3, m3, v3, 0 if layer is None else layer, self.results.get(nm)))
            outs = _adamw_group(f"adamw_{gname}", items, self.chip_ids)
            for (nm, _), out in zip(keys, outs):
                self.results[nm] = out
            st["stage"] = 3

    def at(self, place, after):
        for action, gname in SCHEDULE.get(place, ()):
            self._advance(action, gname, after)
        self._flush()

    def _advance(self, action, gname, after):
        if action.startswith("gather"):
            self._gather_to(gname, STAGES[action], after)
        else:
            self._scatter_to(gname, STAGES[action], after)

    def finish(self, after):
        for action, gname in FINISH:
            self._advance(action, gname, after)
        for gname, _ in SCATTER_GROUPS:
            self._scatter_to(gname, 3, after)
        return {nm: [_stored(nm, o.reshape(_stored(nm, self.wts[nm]).shape)) for o in outs]
                for nm, outs in self.results.items()}


REPLICATED = ("attn_norm", "ffn_norm", "final_norm", "kv_in_norm", "kv_latent_norm", "q_latent_norm", "ffn_conv_b")
WEIGHTS = ("attn_norm", "ffn_norm", "final_norm", "sc_w_in", "sc_conv_w", "sc_w_out", "kv_in_norm", "w_dkv",
           "kv_latent_norm", "w_kr", "w_uk", "w_uv", "w_dq", "q_latent_norm", "w_uq", "w_o", "ffn_w_up", "ffn_conv_w",
           "ffn_conv_b", "ffn_w_down")


def kernel(x, positions, attn_norm, ffn_norm, final_norm, sc_w_in, sc_conv_w, sc_w_out, kv_in_norm, w_dkv, kv_latent_norm, w_kr, w_uk, w_uv, w_dq, q_latent_norm, w_uq, w_o, ffn_w_up, ffn_conv_w, ffn_conv_b, ffn_w_down, loss_target, m_attn_norm, m_ffn_norm, m_final_norm, m_sc_w_in, m_sc_conv_w, m_sc_w_out, m_kv_in_norm, m_w_dkv, m_kv_latent_norm, m_w_kr, m_w_uk, m_w_uv, m_w_dq, m_q_latent_norm, m_w_uq, m_w_o, m_ffn_w_up, m_ffn_conv_w, m_ffn_conv_b, m_ffn_w_down, v_attn_norm, v_ffn_norm, v_final_norm, v_sc_w_in, v_sc_conv_w, v_sc_w_out, v_kv_in_norm, v_w_dkv, v_kv_latent_norm, v_w_kr, v_w_uk, v_w_uv, v_w_dq, v_q_latent_norm, v_w_uq, v_w_o, v_ffn_w_up, v_ffn_conv_w, v_ffn_conv_b, v_ffn_w_down):
    wts = dict(attn_norm=attn_norm, ffn_norm=ffn_norm, final_norm=final_norm, sc_w_in=sc_w_in, sc_conv_w=sc_conv_w,
               sc_w_out=sc_w_out, kv_in_norm=kv_in_norm, w_dkv=w_dkv, kv_latent_norm=kv_latent_norm, w_kr=w_kr,
               w_uk=w_uk, w_uv=w_uv, w_dq=w_dq, q_latent_norm=q_latent_norm, w_uq=w_uq, w_o=w_o, ffn_w_up=ffn_w_up,
               ffn_conv_w=ffn_conv_w, ffn_conv_b=ffn_conv_b, ffn_w_down=ffn_w_down)
    mom = dict(attn_norm=m_attn_norm, ffn_norm=m_ffn_norm, final_norm=m_final_norm, sc_w_in=m_sc_w_in,
               sc_conv_w=m_sc_conv_w, sc_w_out=m_sc_w_out, kv_in_norm=m_kv_in_norm, w_dkv=m_w_dkv,
               kv_latent_norm=m_kv_latent_norm, w_kr=m_w_kr, w_uk=m_w_uk, w_uv=m_w_uv, w_dq=m_w_dq,
               q_latent_norm=m_q_latent_norm, w_uq=m_w_uq, w_o=m_w_o, ffn_w_up=m_ffn_w_up, ffn_conv_w=m_ffn_conv_w,
               ffn_conv_b=m_ffn_conv_b, ffn_w_down=m_ffn_w_down)
    var = dict(attn_norm=v_attn_norm, ffn_norm=v_ffn_norm, final_norm=v_final_norm, sc_w_in=v_sc_w_in,
               sc_conv_w=v_sc_conv_w, sc_w_out=v_sc_w_out, kv_in_norm=v_kv_in_norm, w_dkv=v_w_dkv,
               kv_latent_norm=v_kv_latent_norm, w_kr=v_w_kr, w_uk=v_w_uk, w_uv=v_w_uv, w_dq=v_w_dq,
               q_latent_norm=v_q_latent_norm, w_uq=v_w_uq, w_o=v_w_o, ffn_w_up=v_ffn_w_up, ffn_conv_w=v_ffn_conv_w,
               ffn_conv_b=v_ffn_conv_b, ffn_w_down=v_ffn_w_down)
    xi, yi, ci = _place()
    me = 4 * xi + 2 * yi + ci
    _Chain.last = None

    ex = _Exchange(wts, mom, var, ffn_conv_b)
    rep = {
        "attn_norm": attn_norm, "ffn_norm": ffn_norm, "final_norm": final_norm,
        "kv_in_norm": kv_in_norm.reshape(1, D), "kv_latent_norm": kv_latent_norm.reshape(1, KV_LORA),
        "q_latent_norm": q_latent_norm.reshape(1, Q_LORA),
    }
    loss, grad_x, small = _local_step(x.reshape(T, D), positions.reshape(T, 1), loss_target.reshape(T, D), rep, ex)

    def rows_of(a):
        return a.reshape(-1, a.shape[-1])

    def device_rows(a):
        taps, c = a.shape[-2], a.shape[-1] // N_DEV
        rows = a.reshape(-1, taps, N_DEV, c).transpose(2, 0, 1, 3).reshape(N_DEV, -1, c)
        return jnp.pad(rows, ((0, 0), (0, SHARD_ROWS - rows.shape[1]), (0, 0))).reshape(N_DEV * SHARD_ROWS, c)

    def taps_first(a):
        return jnp.transpose(a, (1, 0, 2))

    sharded = ("sc_conv_w", "ffn_conv_w")
    narrow = [loss.reshape(1, 128)] + [rows_of(small[nm]) for nm in REPLICATED[:-1]]
    conv = [device_rows(small[nm]) for nm in sharded]
    shards = [jnp.concatenate([jnp.pad(a, ((0, 0), (0, D - a.shape[1]))) for a in narrow])[None],
              rows_of(small[REPLICATED[-1]])[None], jnp.concatenate(conv, axis=1)[None]]
    views, row = [], 0
    for a in narrow:
        views.append((0, row, 0) + a.shape)
        row += a.shape[0]
    views.append((1, 0, 0) + shards[1].shape[1:])
    views += [(2, 0, 0, SHARD_ROWS, conv[0].shape[1]), (2, 0, conv[0].shape[1], SHARD_ROWS, conv[1].shape[1])]
    plan, ncopy = _plan_gather_all(len(shards))
    flight, = _copies_start("ag_small", [(shards, [lax.empty((1, N_DEV) + s.shape[1:], F32) for s in shards], ncopy, plan)])
    results = ex.finish(grad_x)
    _, gathered = _copies_wait("ag_small_wait", flight, ncopy, plan)
    params = [[None] + [rows_of(src[nm]) for nm in REPLICATED] + [taps_first(src[nm]) for nm in sharded]
              for src in (wts, mom, var)]
    summed = _adamw_small(gathered, views, *params, me.astype(jnp.int32).reshape(1))
    loss_total = summed[0][0][0, 0]
    for nm, vals in zip(REPLICATED, summed[1:1 + len(REPLICATED)]):
        results[nm] = [a.reshape(wts[nm].shape) for a in vals]
    for nm, vals in zip(sharded, summed[1 + len(REPLICATED):]):
        results[nm] = [taps_first(a) for a in vals]

    outs = [loss_total, grad_x.reshape(1, T, D)]
    for slot in range(4):
        outs.extend(results[nm][slot] for nm in WEIGHTS)
    return tuple(outs)
```

```python
import jax
import jax.numpy as jnp
from jax import lax
from jax.experimental import pallas as pl
from jax.experimental.pallas import tpu as pltpu

F32 = jnp.float32
BF16 = jnp.bfloat16

T = 2048
D = 1024
N_HEADS = 8
QK_NOPE = 128
QK_ROPE = 64
V_HEAD = 128
Q_LORA = 384
KV_LORA = 256
D_FF = 2816
CHUNK = 64
ROPE_THETA = 10000.0
EPS = 1e-6
NEG_INF = -1e30
ADAM_LR = 0.001
ADAM_B1 = 0.9
ADAM_B2 = 0.999
ADAM_EPS = 1e-08
ADAM_WD = 0.01
ADAM_STEP = 10

N_DEV = 8
N_CHIP = 4
FF_BLK = D_FF * 2 // N_DEV
N_FF_BLK = D_FF // FF_BLK
QK_PAD = 256
HALO = 16

TM = 1024
TS = 512
TR = 256
TQ = 512
VMEM_LIMIT = 56 * 1024 * 1024

NN = (((1,), (0,)), ((), ()))
NT = (((1,), (1,)), ((), ()))
TN = (((0,), (0,)), ((), ()))
MESH = pl.DeviceIdType.MESH


def _params(sem):
    return pltpu.CompilerParams(dimension_semantics=sem, vmem_limit_bytes=VMEM_LIMIT)


ANY_SPEC = pl.BlockSpec(memory_space=pl.ANY)
VMEM_SPEC = pl.BlockSpec(memory_space=pltpu.VMEM)


class _Chain:
    last = None


def _pallas(body, *, name, in_specs, out_specs, out_shape, grid=(), scratch_shapes=(), n_prefetch=0, aliases=None,
            params=None):
    def run(*args):
        after = _Chain.last
        n_lead = len(args)
        specs, operands, fn = list(in_specs), list(args), body
        if after is not None:
            def fn(*refs):
                return body(*refs[:n_lead], *refs[n_lead + 1:])
            specs.append(ANY_SPEC)
            operands.append(after)
        kw = dict(name=name, out_shape=out_shape, input_output_aliases=aliases or {})
        if params is not None:
            kw["compiler_params"] = params
        if n_prefetch:
            kw["grid_spec"] = pltpu.PrefetchScalarGridSpec(
                num_scalar_prefetch=n_prefetch, grid=grid, in_specs=specs, out_specs=out_specs,
                scratch_shapes=scratch_shapes)
        else:
            kw.update(grid=grid, in_specs=specs, out_specs=out_specs, scratch_shapes=scratch_shapes)
        outs = pl.pallas_call(fn, **kw)(*operands)
        _Chain.last = outs[0] if isinstance(outs, (list, tuple)) else outs
        return outs
    return run


def _mm(name, a, b, *, grid, a_spec, b_spec, o_spec, o_shape, o_dtype, dims, k_axis=None, acc_shape=None,
        add=None, add_spec=None):
    nk = grid[k_axis] if k_axis is not None else 1
    has_add = add is not None

    def body(*refs):
        a_ref, b_ref = refs[0], refs[1]
        p = 2
        add_ref = None
        if has_add:
            add_ref = refs[p]
            p += 1
        o_ref = refs[p]
        p += 1
        r = lax.dot_general(a_ref[...].astype(BF16), b_ref[...].astype(BF16), dims, preferred_element_type=F32)
        if k_axis is None:
            if has_add:
                r = r + add_ref[...].astype(F32)
            o_ref[...] = r.astype(o_dtype)
        else:
            acc = refs[p]
            k = pl.program_id(k_axis)

            @pl.when(k == 0)
            def _():
                acc[...] = r

            @pl.when(k > 0)
            def _():
                acc[...] += r

            @pl.when(k == nk - 1)
            def _():
                t = acc[...]
                if has_add:
                    t = t + add_ref[...].astype(F32)
                o_ref[...] = t.astype(o_dtype)

    in_specs = [a_spec, b_spec]
    args = [a, b]
    if has_add:
        in_specs.append(add_spec if add_spec is not None else o_spec)
        args.append(add)
    sem = tuple("arbitrary" if ax == k_axis else "parallel" for ax in range(len(grid)))
    scratch = [pltpu.VMEM(acc_shape, F32)] if k_axis is not None else []
    return _pallas(body, name=name, grid=grid, in_specs=in_specs, out_specs=o_spec,
                   out_shape=jax.ShapeDtypeStruct(o_shape, o_dtype), scratch_shapes=scratch, params=_params(sem))(*args)


def _mm_sum(name, parts, *, grid, o_spec, o_shape, o_dtype, add=None, norm_bwd=None, post=None):
    has_add = add is not None
    np_ = len(parts)
    nn = 1 if norm_bwd is None else len(norm_bwd[1])
    has_res = norm_bwd is not None and norm_bwd[2] is not None
    has_post = post is not None

    def body(*refs):
        accs = [None] * nn
        for p, (_, _, _, _, dims, n) in enumerate(parts):
            a_ref, b_ref = refs[2 * p], refs[2 * p + 1]
            for k in range(a_ref.shape[0]):
                r = lax.dot_general(a_ref[k], b_ref[k], dims, preferred_element_type=F32)
                accs[n] = r if accs[n] is None else accs[n] + r
        if norm_bwd is None:
            acc = accs[0]
            if has_add:
                acc = acc + refs[2 * np_][...]
            refs[-1][...] = acc.astype(o_dtype)
            return
        x_ref, g_refs = refs[2 * np_], refs[2 * np_ + 1:2 * np_ + 1 + nn]
        n_in = 2 * np_ + 1 + nn + has_res + has_post
        dx_ref, dxb_ref, dg_refs = refs[n_in], refs[n_in + 1], refs[n_in + 2:n_in + 2 + nn]
        xv = x_ref[...]
        r = lax.rsqrt(jnp.mean(xv * xv, axis=-1, keepdims=True) + EPS)
        xn = xv * r
        dx = refs[2 * np_ + 1 + nn][...] if has_res else None
        sums = []
        for acc, g_ref in zip(accs, g_refs):
            gdy = acc * g_ref[...]
            t = r * (gdy - xn * jnp.mean(gdy * xn, axis=-1, keepdims=True))
            dx = t if dx is None else dx + t
            sums.append(jnp.sum(acc * xn, axis=0, keepdims=True))
        dx_ref[...] = dx
        dxb = dx.astype(BF16)
        dxb_ref[...] = dxb
        if has_post:
            refs[n_in + 2 + nn][...] = lax.dot_general(dxb, refs[n_in - 1][...], post[1],
                                                       preferred_element_type=F32).astype(BF16)

        @pl.when(pl.program_id(0) == 0)
        def _():
            for dg_ref, part in zip(dg_refs, sums):
                dg_ref[...] = part

        @pl.when(pl.program_id(0) > 0)
        def _():
            for dg_ref, part in zip(dg_refs, sums):
                dg_ref[...] += part

    in_specs, args = [], []
    for a, a_spec, b, b_spec, _, _ in parts:
        in_specs += [a_spec, b_spec]
        args += [a, b]
    if norm_bwd is None:
        if has_add:
            in_specs.append(o_spec)
            args.append(add)
        return _pallas(body, name=name, grid=grid, in_specs=in_specs, out_specs=o_spec,
                       out_shape=jax.ShapeDtypeStruct(o_shape, o_dtype),
                       params=_params(("parallel",) * len(grid)))(*args)
    x, gains, dres = norm_bwd
    vec = pl.BlockSpec((1, o_shape[1]), lambda i: (0, 0))
    in_specs += [o_spec] + [vec] * nn + ([o_spec] if has_res else [])
    args += [x] + list(gains) + ([dres] if has_res else [])
    out_specs = [o_spec, o_spec] + [vec] * nn
    out_shape = ([jax.ShapeDtypeStruct(o_shape, F32), jax.ShapeDtypeStruct(o_shape, BF16)]
                 + [jax.ShapeDtypeStruct((1, o_shape[1]), F32)] * nn)
    if has_post:
        in_specs.append(pl.BlockSpec(post[0].shape, lambda i: (0, 0)))
        args.append(post[0])
        out_specs.append(pl.BlockSpec((o_spec.block_shape[0], post[2]), lambda i: (i, 0)))
        out_shape.append(jax.ShapeDtypeStruct((o_shape[0], post[2]), BF16))
    outs = _pallas(body, name=name, grid=grid, in_specs=in_specs, out_specs=out_specs, out_shape=out_shape,
                   params=_params(("arbitrary",)))(*args)
    if has_post:
        return outs[0], outs[1], list(outs[2:2 + nn]), outs[2 + nn]
    return outs[0], outs[1], list(outs[2:])


def _mm_rows(name, a, b, dims, o_dtype, n_out, *, tn=None, add=None):
    k = a.shape[1]
    tn = n_out if tn is None else tn
    if dims == NN:
        b_spec = pl.BlockSpec((k, tn), lambda n, i: (0, n))
    else:
        b_spec = pl.BlockSpec((tn, k), lambda n, i: (n, 0))
    return _mm(name, a, b, grid=(n_out // tn, T // TM),
               a_spec=pl.BlockSpec((TM, k), lambda n, i: (i, 0)), b_spec=b_spec,
               o_spec=pl.BlockSpec((TM, tn), lambda n, i: (i, n)), o_shape=(T, n_out), o_dtype=o_dtype,
               dims=dims, add=add)


def _wgrads(name, jobs):
    arrays, index = [], {}
    for a, b in jobs:
        for arr in (a, b):
            if id(arr) not in index:
                index[id(arr)] = len(arrays)
                arrays.append(arr)
    n_in = len(arrays)

    def body(*refs):
        for t, (a, b) in enumerate(jobs):
            a_ref, b_ref, o_ref = refs[index[id(a)]], refs[index[id(b)]], refs[n_in + t]
            if a.ndim == 3:
                for h in range(a.shape[0]):
                    o_ref[h] = lax.dot_general(a_ref[h], b_ref[...], TN, preferred_element_type=F32).astype(BF16)
            else:
                o_ref[...] = lax.dot_general(a_ref[...], b_ref[...], TN, preferred_element_type=F32).astype(BF16)

    out_shape = [jax.ShapeDtypeStruct(a.shape[:-2] + (a.shape[-1], b.shape[-1]), BF16) for a, b in jobs]
    return _pallas(body, name=name, in_specs=[VMEM_SPEC] * n_in, out_specs=[VMEM_SPEC] * len(jobs), out_shape=out_shape,
                   params=pltpu.CompilerParams(vmem_limit_bytes=VMEM_LIMIT))(*arrays)


def _mm_wgrad(name, a, b, *, tn=512):
    k, n = a.shape[1], b.shape[1]
    tn = min(tn, n)
    return _mm(name, a, b, grid=(n // tn,),
               a_spec=pl.BlockSpec((T, k), lambda j: (0, 0)), b_spec=pl.BlockSpec((T, tn), lambda j: (0, j)),
               o_spec=pl.BlockSpec((k, tn), lambda j: (0, j)), o_shape=(k, n), o_dtype=BF16, dims=TN)


def _rms_fwd(name, x, g):
    d = x.shape[1]

    def body(x_ref, g_ref, o_ref):
        xv = x_ref[...]
        r = lax.rsqrt(jnp.mean(xv * xv, axis=-1, keepdims=True) + EPS)
        o_ref[...] = ((xv * r) * g_ref[...]).astype(BF16)

    return _pallas(
        body, name=name, grid=(T // TM,),
        in_specs=[pl.BlockSpec((TM, d), lambda i: (i, 0)), pl.BlockSpec((1, d), lambda i: (0, 0))],
        out_specs=pl.BlockSpec((TM, d), lambda i: (i, 0)),
        out_shape=jax.ShapeDtypeStruct((T, d), BF16), params=_params(("parallel",)))(x, g)


def _rms(xv, g):
    return (xv * lax.rsqrt(jnp.mean(xv * xv, axis=-1, keepdims=True) + EPS)) * g


def _rms_bwd(name, x, gains, dys, dres=None):
    d = x.shape[1]
    n = len(gains)
    has_res = dres is not None

    def body(*refs):
        x_ref, g_refs, dy_refs = refs[0], refs[1:1 + n], refs[1 + n:1 + 2 * n]
        dx_ref, dxb_ref = refs[-2 - n], refs[-1 - n]
        dg_refs = refs[-n:]
        xv = x_ref[...]
        r = lax.rsqrt(jnp.mean(xv * xv, axis=-1, keepdims=True) + EPS)
        xn = xv * r
        dx = refs[1 + 2 * n][...] if has_res else None
        parts = []
        for g_ref, dy_ref in zip(g_refs, dy_refs):
            dyv = dy_ref[...].astype(F32)
            gdy = dyv * g_ref[...]
            t = r * (gdy - xn * jnp.mean(gdy * xn, axis=-1, keepdims=True))
            dx = t if dx is None else dx + t
            parts.append(jnp.sum(dyv * xn, axis=0, keepdims=True))
        dx_ref[...] = dx
        dxb_ref[...] = dx.astype(BF16)

        @pl.when(pl.program_id(0) == 0)
        def _():
            for dg_ref, part in zip(dg_refs, parts):
                dg_ref[...] = part

        @pl.when(pl.program_id(0) > 0)
        def _():
            for dg_ref, part in zip(dg_refs, parts):
                dg_ref[...] += part

    row = pl.BlockSpec((TR, d), lambda i: (i, 0))
    vec = pl.BlockSpec((1, d), lambda i: (0, 0))
    args = [x] + list(gains) + list(dys) + ([dres] if has_res else [])
    in_specs = [row] + [vec] * n + [row] * n + ([row] if has_res else [])
    outs = _pallas(
        body, name=name, grid=(T // TR,), in_specs=in_specs, out_specs=[row, row] + [vec] * n,
        out_shape=[jax.ShapeDtypeStruct((T, d), F32), jax.ShapeDtypeStruct((T, d), BF16)]
        + [jax.ShapeDtypeStruct((1, d), F32)] * n,
        params=_params(("arbitrary",)))(*args)
    return outs[0], outs[1], list(outs[2:])


def _down_final(act, w_down4, h_in, g, tgt):
    def body(a_ref, w_ref, hin_ref, g_ref, t_ref, loss_ref, dh_ref, dhb_ref, dg_ref):
        hv = lax.dot_general(a_ref[0], w_ref[0], NN, preferred_element_type=F32)
        for j in range(1, N_FF_BLK):
            hv = hv + lax.dot_general(a_ref[j], w_ref[j], NN, preferred_element_type=F32)
        hv = hv + hin_ref[...]
        r = lax.rsqrt(jnp.mean(hv * hv, axis=-1, keepdims=True) + EPS)
        xn = hv * r
        gv = g_ref[...]
        err = xn * gv - t_ref[...]
        part_loss = 0.5 * jnp.sum(jnp.mean(err * err, axis=-1, keepdims=True), axis=0, keepdims=True)
        dy = err * (1.0 / D)
        gdy = dy * gv
        dh = r * (gdy - xn * jnp.mean(gdy * xn, axis=-1, keepdims=True))
        dh_ref[...] = dh
        dhb_ref[...] = dh.astype(BF16)
        part = jnp.sum(dy * xn, axis=0, keepdims=True)
        first = pl.program_id(0) == 0

        @pl.when(first)
        def _():
            dg_ref[...] = part
            loss_ref[...] = jnp.broadcast_to(part_loss, (1, 128))

        @pl.when(jnp.logical_not(first))
        def _():
            dg_ref[...] += part
            loss_ref[...] += jnp.broadcast_to(part_loss, (1, 128))

    row = pl.BlockSpec((TS, D), lambda i: (i, 0))
    vec = pl.BlockSpec((1, D), lambda i: (0, 0))
    return _pallas(
        body, name="f1_down_loss", grid=(T // TS,),
        in_specs=[pl.BlockSpec((N_FF_BLK, TS, FF_BLK), lambda i: (0, i, 0)),
                  pl.BlockSpec((None, N_FF_BLK, FF_BLK, D), lambda i: (0, 0, 0, 0)), row, vec, row],
        out_specs=[pl.BlockSpec((1, 128), lambda i: (0, 0)), row, row, vec],
        out_shape=[jax.ShapeDtypeStruct((1, 128), F32), jax.ShapeDtypeStruct((T, D), F32),
                   jax.ShapeDtypeStruct((T, D), BF16), jax.ShapeDtypeStruct((1, D), F32)],
        params=_params(("arbitrary",)))(act, w_down4, h_in, g, tgt)


def _prev_idx(i, rows=TR):
    return jnp.maximum(i * (rows // HALO) - 1, 0)


def _next_idx(i, rows=TR):
    return jnp.minimum((i + 1) * (rows // HALO), T // HALO - 1)


def _causal_taps(ext):
    return pltpu.roll(ext, 2, 0)[HALO:], pltpu.roll(ext, 1, 0)[HALO:], ext[HALO:]


def _anticausal_taps(ext, n):
    rows = ext.shape[0]
    return pltpu.roll(ext, rows - 1, 0)[:n], pltpu.roll(ext, rows - 2, 0)[:n]


MIX_COLS = 512


def _mixer_in(hn, w_in, w):
    nc = D // MIX_COLS

    def body(h_ref, hh_ref, wb_ref, wc_ref, wu_ref, w_ref, b_ref, c_ref, u_ref, y_ref):
        i = pl.program_id(1)
        hv = h_ref[...]
        he = jnp.concatenate([hh_ref[...], hv], axis=0)
        ce = lax.dot_general(he, wc_ref[...], NN, preferred_element_type=F32).astype(BF16)
        ue = lax.dot_general(he, wu_ref[...], NN, preferred_element_type=F32).astype(BF16)
        bv = lax.dot_general(hv, wb_ref[...], NN, preferred_element_type=F32).astype(BF16)
        b_ref[...] = bv
        c_ref[...] = ce[HALO:]
        u_ref[...] = ue[HALO:]
        row = lax.broadcasted_iota(jnp.int32, (HALO + TS, 1), 0)
        cu = jnp.where(jnp.logical_or(i > 0, row >= HALO), ce.astype(F32) * ue.astype(F32), 0.0)
        x2, x1, x0 = _causal_taps(cu)
        wv = w_ref[...]
        cv = (x2 * wv[0:1] + x1 * wv[1:2]) + x0 * wv[2:3]
        y_ref[...] = (bv.astype(F32) * cv).astype(BF16)

    def cols(part):
        return pl.BlockSpec((D, MIX_COLS), lambda j, i: (0, part * nc + j))

    blk = pl.BlockSpec((TS, MIX_COLS), lambda j, i: (i, j))
    out = jax.ShapeDtypeStruct((T, D), BF16)
    return _pallas(
        body, name="l0_in", grid=(nc, T // TS),
        in_specs=[pl.BlockSpec((TS, D), lambda j, i: (i, 0)), pl.BlockSpec((HALO, D), lambda j, i: (_prev_idx(i, TS), 0)),
                  cols(0), cols(1), cols(2), pl.BlockSpec((3, MIX_COLS), lambda j, i: (0, j))],
        out_specs=[blk] * 4, out_shape=[out] * 4,
        params=_params(("parallel", "parallel")))(hn, hn, w_in, w_in, w_in, w)


def _mixer_out_bwd(dh, w_out, zb, zc, zu, w):
    last = T // TR - 1

    def body(dh_ref, dhn_ref, wo_ref, b_ref, bn_ref, c_ref, ch_ref, u_ref, uh_ref, w_ref, dz_ref, dw_ref):
        i = pl.program_id(0)
        dye = lax.dot_general(jnp.concatenate([dh_ref[...], dhn_ref[...]], axis=0), wo_ref[...], NT,
                              preferred_element_type=F32)
        cv_ = c_ref[...].astype(F32)
        uv = u_ref[...].astype(F32)
        cu = cv_ * uv
        cuh = jnp.where(i > 0, ch_ref[...].astype(F32) * uh_ref[...].astype(F32), 0.0)
        x2, x1, x0 = _causal_taps(jnp.concatenate([cuh, cu], axis=0))
        wv = w_ref[...]
        conv = (x2 * wv[0:1] + x1 * wv[1:2]) + x0 * wv[2:3]
        dyv = dye[:TR]
        dz_ref[:, 0:D] = (dyv * conv).astype(BF16)
        dconv = dyv * b_ref[...].astype(F32)
        dconv_n = jnp.where(i < last, dye[TR:] * bn_ref[...].astype(F32), 0.0)
        n1, n2 = _anticausal_taps(jnp.concatenate([dconv, dconv_n], axis=0), TR)
        dcu = (dconv * wv[2:3] + n1 * wv[1:2]) + n2 * wv[0:1]
        dz_ref[:, D:2 * D] = (dcu * uv).astype(BF16)
        dz_ref[:, 2 * D:3 * D] = (dcu * cv_).astype(BF16)
        part = jnp.concatenate([jnp.sum(dconv * x2, axis=0, keepdims=True),
                                jnp.sum(dconv * x1, axis=0, keepdims=True),
                                jnp.sum(dconv * x0, axis=0, keepdims=True)], axis=0)

        @pl.when(i == 0)
        def _():
            dw_ref[...] = part

        @pl.when(i > 0)
        def _():
            dw_ref[...] += part

    main = pl.BlockSpec((TR, D), lambda i: (i, 0))
    prev = pl.BlockSpec((HALO, D), lambda i: (_prev_idx(i), 0))
    nxt = pl.BlockSpec((HALO, D), lambda i: (_next_idx(i), 0))
    wspec = pl.BlockSpec((3, D), lambda i: (0, 0))
    return _pallas(
        body, name="d_l0_out", grid=(T // TR,),
        in_specs=[main, nxt, pl.BlockSpec((D, D), lambda i: (0, 0)), main, nxt, main, prev, main, prev, wspec],
        out_specs=[pl.BlockSpec((TR, 3 * D), lambda i: (i, 0)), wspec],
        out_shape=[jax.ShapeDtypeStruct((T, 3 * D), BF16), jax.ShapeDtypeStruct((3, D), F32)],
        params=_params(("arbitrary",)))(dh, dh, w_out, zb, zb, zc, zc, zu, zu, w)


def _sigmoid(x):
    return 0.5 * jnp.tanh(0.5 * x) + 0.5


def _ffn_up_act(name, hf, w_up, w, b):
    def body(h_ref, hh_ref, wg_ref, wv_ref, w_ref, b_ref, g_ref, v_ref, a_ref):
        i = pl.program_id(1)
        hv = h_ref[...]
        ge = lax.dot_general(jnp.concatenate([hh_ref[...], hv], axis=0), wg_ref[...], NT,
                             preferred_element_type=F32).astype(BF16)
        v = lax.dot_general(hv, wv_ref[...], NT, preferred_element_type=F32).astype(BF16)
        g_ref[...] = ge[HALO:]
        v_ref[...] = v
        ext = ge.astype(F32)
        row = lax.broadcasted_iota(jnp.int32, (HALO + TM, 1), 0)
        ext = jnp.where(jnp.logical_or(i > 0, row >= HALO), ext, 0.0)
        x2, x1, x0 = _causal_taps(ext)
        wv = w_ref[...]
        gc = ((x2 * wv[0:1] + x1 * wv[1:2]) + x0 * wv[2:3]) + b_ref[...]
        a_ref[...] = ((gc * _sigmoid(gc)) * v.astype(F32)).astype(BF16)

    blk = pl.BlockSpec((None, TM, FF_BLK), lambda j, i: (j, i, 0))
    out = jax.ShapeDtypeStruct((N_FF_BLK, T, FF_BLK), BF16)
    return _pallas(
        body, name=name, grid=(N_FF_BLK, T // TM),
        in_specs=[pl.BlockSpec((TM, D), lambda j, i: (i, 0)),
                  pl.BlockSpec((HALO, D), lambda j, i: (_prev_idx(i, TM), 0)),
                  pl.BlockSpec((None, None, FF_BLK, D), lambda j, i: (0, j, 0, 0)),
                  pl.BlockSpec((None, None, FF_BLK, D), lambda j, i: (0, j + N_FF_BLK, 0, 0)),
                  pl.BlockSpec((None, 3, FF_BLK), lambda j, i: (j, 0, 0)),
                  pl.BlockSpec((None, 1, FF_BLK), lambda j, i: (j, 0, 0))],
        out_specs=[blk, blk, blk], out_shape=[out, out, out],
        params=_params(("parallel", "parallel")))(hf, hf, w_up, w_up, w, b)


def _ffn_dact(name, dh, w_down4, g, v, w, b):
    last = T // TS - 1

    def body(dh_ref, dhn_ref, wd_ref, g_ref, gp_ref, gn_ref, v_ref, vn_ref, w_ref, b_ref, dg_ref, dv_ref, dw_ref, db_ref):
        i = pl.program_id(1)
        da = lax.dot_general(jnp.concatenate([dh_ref[...], dhn_ref[...]], axis=0), wd_ref[...], NT,
                             preferred_element_type=F32)
        row = lax.broadcasted_iota(jnp.int32, (TS + HALO, 1), 0)
        da = jnp.where(jnp.logical_or(i < last, row < TS), da, 0.0)
        gp = jnp.where(i > 0, gp_ref[...].astype(F32), 0.0)
        ext = jnp.concatenate([gp, g_ref[...].astype(F32), gn_ref[...].astype(F32)], axis=0)
        x2, x1, x0 = _causal_taps(ext)
        wv = w_ref[...]
        gc = ((x2 * wv[0:1] + x1 * wv[1:2]) + x0 * wv[2:3]) + b_ref[...]
        sg = _sigmoid(gc)
        vv = jnp.concatenate([v_ref[...].astype(F32), vn_ref[...].astype(F32)], axis=0)
        dv_ref[...] = (da[:TS] * (gc[:TS] * sg[:TS])).astype(BF16)
        dgc = (da * vv) * (sg * (1.0 + gc * (1.0 - sg)))
        n1, n2 = _anticausal_taps(dgc, TS)
        d0 = dgc[:TS]
        dg_ref[...] = ((d0 * wv[2:3] + n1 * wv[1:2]) + n2 * wv[0:1]).astype(BF16)
        part_w = jnp.concatenate([jnp.sum(d0 * x2[:TS], axis=0, keepdims=True),
                                  jnp.sum(d0 * x1[:TS], axis=0, keepdims=True),
                                  jnp.sum(d0 * x0[:TS], axis=0, keepdims=True)], axis=0)
        part_b = jnp.sum(d0, axis=0, keepdims=True)

        @pl.when(i == 0)
        def _():
            dw_ref[...] = part_w
            db_ref[...] = part_b

        @pl.when(i > 0)
        def _():
            dw_ref[...] += part_w
            db_ref[...] += part_b

    blk = pl.BlockSpec((None, TS, FF_BLK), lambda j, i: (j, i, 0))
    prev = pl.BlockSpec((None, HALO, FF_BLK), lambda j, i: (j, _prev_idx(i, TS), 0))
    nxt = pl.BlockSpec((None, HALO, FF_BLK), lambda j, i: (j, _next_idx(i, TS), 0))
    wspec = pl.BlockSpec((None, 3, FF_BLK), lambda j, i: (j, 0, 0))
    bspec = pl.BlockSpec((None, 1, FF_BLK), lambda j, i: (j, 0, 0))
    return _pallas(
        body, name=name, grid=(N_FF_BLK, T // TS),
        in_specs=[pl.BlockSpec((TS, D), lambda j, i: (i, 0)),
                  pl.BlockSpec((HALO, D), lambda j, i: (_next_idx(i, TS), 0)),
                  pl.BlockSpec((None, None, FF_BLK, D), lambda j, i: (0, j, 0, 0)),
                  blk, prev, nxt, blk, nxt, wspec, bspec],
        out_specs=[blk, blk, wspec, bspec],
        out_shape=[jax.ShapeDtypeStruct((N_FF_BLK, T, FF_BLK), BF16), jax.ShapeDtypeStruct((N_FF_BLK, T, FF_BLK), BF16),
                   jax.ShapeDtypeStruct((N_FF_BLK, 3, FF_BLK), F32), jax.ShapeDtypeStruct((N_FF_BLK, 1, FF_BLK), F32)],
        params=_params(("parallel", "arbitrary")))(dh, dh, w_down4, g, g, g, v, v, w, b)


def _rope_tables(pos, inv_freq):
    half = QK_ROPE // 2

    def body(p_ref, f_ref, c_ref, sa_ref, sb_ref):
        ang = p_ref[...].astype(F32) * f_ref[...]
        lane = lax.broadcasted_iota(jnp.int32, (T, 128), 1)
        c = jnp.cos(ang)
        s = jnp.sin(ang)
        c_ref[...] = jnp.where(lane < 2 * half, c, 0.0)
        sa_ref[...] = jnp.where(lane < half, -s, 0.0)
        sb_ref[...] = jnp.where(jnp.logical_and(lane >= half, lane < 2 * half), s, 0.0)

    return _pallas(
        body, name="rope_tables", in_specs=[VMEM_SPEC] * 2, out_specs=[VMEM_SPEC] * 3,
        out_shape=[jax.ShapeDtypeStruct((T, 128), F32)] * 3,
        params=pltpu.CompilerParams(vmem_limit_bytes=VMEM_LIMIT))(pos, inv_freq)


def _rotate(r, c, sa, sb, sign):
    return r * c + sign * (pltpu.roll(r, 96, 1) * sa + pltpu.roll(r, 32, 1) * sb)


def _attn_pre(h2, g_kv, g_l1, g_kvl, g_ql, w_dkv, w_kr, w_uk, w_uv, w_dq, w_uq, tables):
    def body(h_ref, c_ref, sa_ref, sb_ref, gkv_ref, gl1_ref, gkvl_ref, gql_ref, wdkv_ref, wkr_ref, wuk_ref, wuv_ref,
             wdq_ref, wuq_ref, hk_ref, hn_ref, ckvr_ref, ckv_ref, kr_ref, kn_ref, v_ref, cqr_ref, cq_ref, q_ref):
        xv = h_ref[...]
        xn = xv * lax.rsqrt(jnp.mean(xv * xv, axis=-1, keepdims=True) + EPS)
        hk = (xn * gkv_ref[...]).astype(BF16)
        hn = (xn * gl1_ref[...]).astype(BF16)
        hk_ref[...] = hk
        hn_ref[...] = hn
        cv, sav, sbv = c_ref[...], sa_ref[...], sb_ref[...]
        raw = lax.dot_general(hk, wdkv_ref[...], NN, preferred_element_type=F32)
        ckvr_ref[...] = raw
        ckv = _rms(raw, gkvl_ref[...]).astype(BF16)
        ckv_ref[...] = ckv
        kr = lax.dot_general(hk, wkr_ref[...], NT, preferred_element_type=F32)
        kr_ref[...] = _rotate(kr, cv, sav, sbv, 1.0).astype(BF16)
        kn_ref[...] = lax.dot_general(ckv, wuk_ref[...], NN, preferred_element_type=F32).astype(BF16)
        v_ref[...] = lax.dot_general(ckv, wuv_ref[...], NN, preferred_element_type=F32).astype(BF16)
        cqr = lax.dot_general(hn, wdq_ref[...], NN, preferred_element_type=F32)
        cqr_ref[...] = cqr
        cq = _rms(cqr, gql_ref[...]).astype(BF16)
        cq_ref[...] = cq
        for h in range(N_HEADS):
            r = lax.dot_general(cq, wuq_ref[h], NT, preferred_element_type=F32)
            q_ref[h, :, :QK_NOPE] = (r[:, :QK_NOPE] * SCALE2).astype(BF16)
            q_ref[h, :, QK_NOPE:] = (_rotate(r[:, QK_NOPE:], cv, sav, sbv, 1.0) * SCALE2).astype(BF16)

    def rows(d):
        return pl.BlockSpec((TS, d), lambda i: (i, 0))

    def whole(a):
        return pl.BlockSpec(a.shape, lambda i: (0,) * a.ndim)

    wholes = [g_kv, g_l1, g_kvl, g_ql, w_dkv, w_kr, w_uk, w_uv, w_dq, w_uq]
    outs = [(D, BF16), (D, BF16), (KV_LORA, F32), (KV_LORA, BF16), (128, BF16), (N_HEADS * QK_NOPE, BF16),
            (N_HEADS * V_HEAD, BF16), (Q_LORA, F32), (Q_LORA, BF16)]
    return _pallas(
        body, name="attn_pre", grid=(T // TS,),
        in_specs=[rows(D), rows(128), rows(128), rows(128)] + [whole(a) for a in wholes],
        out_specs=[rows(d) for d, _ in outs] + [pl.BlockSpec((N_HEADS, TS, QK_PAD), lambda i: (0, i, 0))],
        out_shape=[jax.ShapeDtypeStruct((T, d), dt) for d, dt in outs]
        + [jax.ShapeDtypeStruct((N_HEADS, T, QK_PAD), BF16)],
        params=_params(("parallel",)))(h2, *tables, *wholes)


SCALE = (QK_NOPE + QK_ROPE) ** -0.5
LOG2E = 1.4426950408889634
SCALE2 = SCALE * LOG2E


def _diag_mask(transposed):
    shift = CHUNK.bit_length() - 1
    a = lax.broadcasted_iota(jnp.int32, (TQ, TQ), 0) >> shift
    b = lax.broadcasted_iota(jnp.int32, (TQ, TQ), 1) >> shift
    return (a <= b) if transposed else (b <= a)


def _as_row(col):
    return jnp.transpose(jnp.broadcast_to(col, (col.shape[0], 128)), (1, 0))[0:1]


def _keys(kn_ref, kr_ref, off):
    return jnp.concatenate([kn_ref[pl.ds(off, TQ), :], kr_ref[pl.ds(off, TQ), :]], axis=1)


def _attn_fwd(q, kn, kr, v):
    hp = 2

    def body(q_ref, kn_ref, kr_ref, v_ref, o_ref, lse_ref):
        i = pl.program_id(1)
        qs = [q_ref[a] for a in range(hp)]

        def step(j, carry, masked):
            off = pl.multiple_of(j * TQ, TQ)
            krv = kr_ref[pl.ds(off, TQ), :]
            ss = []
            for a in range(hp):
                kk = jnp.concatenate([kn_ref[pl.ds(off, TQ), a * QK_NOPE:(a + 1) * QK_NOPE], krv], axis=1)
                ss.append(lax.dot_general(qs[a], kk, NT, preferred_element_type=F32))
            out = []
            for a in range(hp):
                m, l, acc = carry[a]
                s = ss[a]
                if masked:
                    s = jnp.where(_diag_mask(False), s, NEG_INF)
                m_new = jnp.maximum(m, jnp.max(s, axis=-1, keepdims=True))
                p = jnp.exp2(s - m_new)
                alpha = jnp.exp2(m - m_new)
                l = alpha * l + jnp.sum(p, axis=-1, keepdims=True)
                pv = lax.dot_general(p.astype(BF16), v_ref[pl.ds(off, TQ), a * V_HEAD:(a + 1) * V_HEAD], NN,
                                     preferred_element_type=F32)
                out.append((m_new, l, alpha * acc + pv))
            return tuple(out)

        one = (jnp.full((TQ, 1), NEG_INF, F32), jnp.zeros((TQ, 1), F32), jnp.zeros((TQ, V_HEAD), F32))
        carry = lax.fori_loop(0, i, lambda j, cr: step(j, cr, False), (one,) * hp)
        carry = step(i, carry, True)
        for a, (m, l, acc) in enumerate(carry):
            o_ref[:, a * V_HEAD:(a + 1) * V_HEAD] = (acc / l).astype(BF16)
            lse_ref[a] = _as_row(m + jnp.log(l) * LOG2E)

    return _pallas(
        body, name="attn_fwd", grid=(N_HEADS // hp, T // TQ),
        in_specs=[pl.BlockSpec((hp, TQ, QK_PAD), lambda h, i: (h, i, 0)),
                  pl.BlockSpec((T, hp * QK_NOPE), lambda h, i: (0, h)),
                  pl.BlockSpec((T, 128), lambda h, i: (0, 0)),
                  pl.BlockSpec((T, hp * V_HEAD), lambda h, i: (0, h))],
        out_specs=[pl.BlockSpec((TQ, hp * V_HEAD), lambda h, i: (i, h)), pl.BlockSpec((hp, 1, TQ), lambda h, i: (h, 0, i))],
        out_shape=[jax.ShapeDtypeStruct((T, N_HEADS * V_HEAD), BF16), jax.ShapeDtypeStruct((N_HEADS, 1, T), F32)],
        params=_params(("parallel", "parallel")))(q, kn, kr, v)


def _attn_bwd(q, kn, kr, v, o, do, lse_row, tables):
    nq = T // TQ
    hp = 2
    cos, sa, sb = tables

    def body(q_ref, kn_ref, kr_ref, v_ref, o_ref, do_ref, lse_ref, c_ref, sa_ref, sb_ref,
             dq_ref, dkn_ref, dkr_ref, dv_ref, dq_acc, dl_ref):
        j = pl.program_id(1)

        def cols(a):
            return slice(a * 128, (a + 1) * 128)

        @pl.when(j == 0)
        def _():
            dq_acc[...] = jnp.zeros_like(dq_acc)
            for a in range(hp):
                for i in range(nq):
                    rows = pl.ds(i * TQ, TQ)
                    prod = do_ref[rows, cols(a)].astype(F32) * o_ref[rows, cols(a)].astype(F32)
                    dl_ref[a, :, rows] = _as_row(jnp.sum(prod, axis=-1, keepdims=True))

        krv = kr_ref[...]
        kks = [jnp.concatenate([kn_ref[:, cols(a)], krv], axis=1) for a in range(hp)]
        vvs = [v_ref[:, cols(a)] for a in range(hp)]

        def step(i, carry, masked):
            off = pl.multiple_of(i * TQ, TQ)
            rows = pl.ds(off, TQ)
            qis = [q_ref[a, rows, :] for a in range(hp)]
            dois = [do_ref[rows, cols(a)] for a in range(hp)]
            sts = [lax.dot_general(kks[a], qis[a], NT, preferred_element_type=F32) for a in range(hp)]
            dpts = [lax.dot_general(vvs[a], dois[a], NT, preferred_element_type=F32) for a in range(hp)]
            out = []
            for a in range(hp):
                dk, dv = carry[a]
                st = sts[a]
                if masked:
                    st = jnp.where(_diag_mask(True), st, NEG_INF)
                pt = jnp.exp2(st - lse_ref[a, :, rows])
                dv = dv + lax.dot_general(pt.astype(BF16), dois[a], NN, preferred_element_type=F32)
                dst = (pt * (dpts[a] - dl_ref[a, :, rows])).astype(BF16)
                dk = dk + lax.dot_general(dst, qis[a], NN, preferred_element_type=F32)
                dq_acc[a, rows, :] += lax.dot_general(dst, kks[a], TN, preferred_element_type=F32)
                out.append((dk, dv))
            return tuple(out)

        zero = (jnp.zeros((TQ, QK_PAD), F32), jnp.zeros((TQ, V_HEAD), F32))
        carry = step(j, (zero,) * hp, True)
        carry = lax.fori_loop(j + 1, nq, lambda i, cr: step(i, cr, False), carry)
        for a, (dk, dv) in enumerate(carry):
            dk = dk * (SCALE / SCALE2)
            dkn_ref[:, cols(a)] = dk[:, :QK_NOPE].astype(BF16)
            dkr_ref[a] = dk[:, QK_NOPE:]
            dv_ref[:, cols(a)] = dv.astype(BF16)

        @pl.when(j == nq - 1)
        def _():
            for a in range(hp):
                dq = dq_acc[a] * SCALE
                dq_ref[a, :, :QK_NOPE] = dq[:, :QK_NOPE].astype(BF16)
                dq_ref[a, :, QK_NOPE:] = _rotate(dq[:, QK_NOPE:], c_ref[...], sa_ref[...], sb_ref[...], -1.0).astype(BF16)

    row = pl.BlockSpec((hp, 1, T), lambda h, j: (h, 0, 0))
    head = pl.BlockSpec((TQ, hp * 128), lambda h, j: (j, h))
    whole = pl.BlockSpec((hp, T, QK_PAD), lambda h, j: (h, 0, 0))
    tab = pl.BlockSpec((T, 128), lambda h, j: (0, 0))
    heads = pl.BlockSpec((T, hp * V_HEAD), lambda h, j: (0, h))
    return _pallas(
        body, name="attn_bwd", grid=(N_HEADS // hp, nq),
        in_specs=[whole, head, pl.BlockSpec((TQ, 128), lambda h, j: (j, 0)), head, heads, heads, row, tab, tab, tab],
        out_specs=[whole, head, pl.BlockSpec((hp, TQ, 128), lambda h, j: (h, j, 0)), head],
        out_shape=[jax.ShapeDtypeStruct((N_HEADS, T, QK_PAD), BF16), jax.ShapeDtypeStruct((T, N_HEADS * QK_NOPE), BF16),
                   jax.ShapeDtypeStruct((N_HEADS, T, 128), F32), jax.ShapeDtypeStruct((T, N_HEADS * V_HEAD), BF16)],
        scratch_shapes=[pltpu.VMEM((hp, T, QK_PAD), F32), pltpu.VMEM((hp, 1, T), F32)],
        params=_params(("parallel", "arbitrary")))(q, kn, kr, v, o, do, lse_row, cos, sa, sb)


def _rms_bwd_math(xv, g, dy):
    r = lax.rsqrt(jnp.mean(xv * xv, axis=-1, keepdims=True) + EPS)
    xn = xv * r
    gdy = dy * g
    return r * (gdy - xn * jnp.mean(gdy * xn, axis=-1, keepdims=True)), jnp.sum(dy * xn, axis=0, keepdims=True)


def _attn_post(dq, dkn, dv, dkr, cq_raw, ckv_raw, h2, dres, g_ql, g_kvl, g_l1, g_kv, w_uq, w_uk, w_uv, w_dq, w_dkv,
               w_kr, tables):
    def body(dq_ref, dkn_ref, dv_ref, dkr_ref, cqr_ref, ckvr_ref, h_ref, res_ref, c_ref, sa_ref, sb_ref,
             gql_ref, gkvl_ref, gl1_ref, gkv_ref, wuq_ref, wuk_ref, wuv_ref, wdq_ref, wdkv_ref, wkr_ref,
             dcq_ref, dckv_ref, dkrr_ref, dh_ref, dhb_ref, dgql_ref, dgkvl_ref, dgl1_ref, dgkv_ref):
        dcq = lax.dot_general(dq_ref[0], wuq_ref[0], NN, preferred_element_type=F32)
        for h in range(1, N_HEADS):
            dcq = dcq + lax.dot_general(dq_ref[h], wuq_ref[h], NN, preferred_element_type=F32)
        dcq_raw, s_ql = _rms_bwd_math(cqr_ref[...], gql_ref[...], dcq)
        dcq_raw = dcq_raw.astype(BF16)
        dcq_ref[...] = dcq_raw
        dckv = (lax.dot_general(dkn_ref[...], wuk_ref[...], NT, preferred_element_type=F32)
                + lax.dot_general(dv_ref[...], wuv_ref[...], NT, preferred_element_type=F32))
        dckv_raw, s_kvl = _rms_bwd_math(ckvr_ref[...], gkvl_ref[...], dckv)
        dckv_raw = dckv_raw.astype(BF16)
        dckv_ref[...] = dckv_raw
        dkr = dkr_ref[0]
        for h in range(1, N_HEADS):
            dkr = dkr + dkr_ref[h]
        dkr_raw = _rotate(dkr, c_ref[...], sa_ref[...], sb_ref[...], -1.0).astype(BF16)
        dkrr_ref[...] = dkr_raw
        d_hn = lax.dot_general(dcq_raw, wdq_ref[...], NT, preferred_element_type=F32)
        d_hk = (lax.dot_general(dckv_raw, wdkv_ref[...], NT, preferred_element_type=F32)
                + lax.dot_general(dkr_raw, wkr_ref[...], NN, preferred_element_type=F32))
        xv = h_ref[...]
        r = lax.rsqrt(jnp.mean(xv * xv, axis=-1, keepdims=True) + EPS)
        xn = xv * r
        dx = res_ref[...]
        sums = [s_ql, s_kvl]
        for dy, g_ref in ((d_hn, gl1_ref), (d_hk, gkv_ref)):
            gdy = dy * g_ref[...]
            dx = dx + r * (gdy - xn * jnp.mean(gdy * xn, axis=-1, keepdims=True))
            sums.append(jnp.sum(dy * xn, axis=0, keepdims=True))
        dh_ref[...] = dx
        dhb_ref[...] = dx.astype(BF16)
        dg_refs = (dgql_ref, dgkvl_ref, dgl1_ref, dgkv_ref)

        @pl.when(pl.program_id(0) == 0)
        def _():
            for dg_ref, part in zip(dg_refs, sums):
                dg_ref[...] = part

        @pl.when(pl.program_id(0) > 0)
        def _():
            for dg_ref, part in zip(dg_refs, sums):
                dg_ref[...] += part

    def rows(d):
        return pl.BlockSpec((TS, d), lambda i: (i, 0))

    def heads(d):
        return pl.BlockSpec((N_HEADS, TS, d), lambda i: (0, i, 0))

    def whole(a):
        return pl.BlockSpec(a.shape, lambda i: (0,) * a.ndim)

    wholes = [g_ql, g_kvl, g_l1, g_kv, w_uq, w_uk, w_uv, w_dq, w_dkv, w_kr]
    vecs = [Q_LORA, KV_LORA, D, D]
    return _pallas(
        body, name="attn_post", grid=(T // TS,),
        in_specs=[heads(QK_PAD), rows(N_HEADS * QK_NOPE), rows(N_HEADS * V_HEAD), heads(128), rows(Q_LORA),
                  rows(KV_LORA), rows(D), rows(D), rows(128), rows(128), rows(128)] + [whole(a) for a in wholes],
        out_specs=[rows(Q_LORA), rows(KV_LORA), rows(128), rows(D), rows(D)]
        + [pl.BlockSpec((1, d), lambda i: (0, 0)) for d in vecs],
        out_shape=[jax.ShapeDtypeStruct((T, Q_LORA), BF16), jax.ShapeDtypeStruct((T, KV_LORA), BF16),
                   jax.ShapeDtypeStruct((T, 128), BF16), jax.ShapeDtypeStruct((T, D), F32),
                   jax.ShapeDtypeStruct((T, D), BF16)] + [jax.ShapeDtypeStruct((1, d), F32) for d in vecs],
        params=_params(("arbitrary",)))(dq, dkn, dv, dkr, cq_raw, ckv_raw, h2, dres, *tables, *wholes)


def _ffn_gup(name, dg, dv, hf):
    def body(dg_ref, dv_ref, hf_ref, o_ref):
        j = pl.program_id(0)

        @pl.when(j < N_FF_BLK)
        def _():
            o_ref[...] = lax.dot_general(dg_ref[...], hf_ref[...], TN, preferred_element_type=F32).astype(BF16)

        @pl.when(j >= N_FF_BLK)
        def _():
            o_ref[...] = lax.dot_general(dv_ref[...], hf_ref[...], TN, preferred_element_type=F32).astype(BF16)

    return _pallas(
        body, name=name, grid=(N_DEV,),
        in_specs=[pl.BlockSpec((None, T, FF_BLK), lambda j: (jnp.minimum(j, N_FF_BLK - 1), 0, 0)),
                  pl.BlockSpec((None, T, FF_BLK), lambda j: (jnp.maximum(j - N_FF_BLK, 0), 0, 0)),
                  pl.BlockSpec((T, D), lambda j: (0, 0))],
        out_specs=pl.BlockSpec((None, FF_BLK, D), lambda j: (j, 0, 0)),
        out_shape=jax.ShapeDtypeStruct((N_DEV, FF_BLK, D), BF16), params=_params(("parallel",)))(dg, dv, hf)


def _ffn_layer_fwd(tag, h, gain, ex, final=None):
    hf = _rms_fwd(f"{tag}_norm", h, gain)
    g, v, act = _ffn_up_act(f"{tag}_up", hf, ex.need(f"ffn_w_up{tag[1]}", hf), ex.need(f"ffn_cw{tag[1]}", hf),
                            ex.need(f"ffn_cb{tag[1]}", hf))
    ex.at(f"{tag}_up", act)
    if final is not None:
        return _down_final(act, ex.need(f"ffn_w_down{tag[1]}", act), h, *final), (hf, g, v, act)
    rows = pl.BlockSpec((TS, D), lambda i: (i, 0))
    out = _mm_sum(f"{tag}_down",
                  [(act, pl.BlockSpec((N_FF_BLK, TS, FF_BLK), lambda i: (0, i, 0)), ex.need(f"ffn_w_down{tag[1]}", act),
                    pl.BlockSpec((None, N_FF_BLK, FF_BLK, D), lambda i: (0, 0, 0, 0)), NN, 0)],
                  grid=(T // TS,), o_spec=rows, o_shape=(T, D), o_dtype=F32, add=h)
    ex.at(f"{tag}_down", out)
    return out, (hf, g, v, act)


def _ffn_layer_bwd(tag, h, gain, ex, saved, dh, dh_bf, post=None):
    hf, g, v, act = saved
    layer = tag[1]
    w_up, w_down4 = ex.need(f"ffn_w_up{layer}", dh_bf), ex.need(f"ffn_w_down{layer}", dh_bf)
    dg, dv, dcw, dcb = _ffn_dact(f"{tag}_dact", dh_bf, w_down4, g, v, ex.need(f"ffn_cw{layer}", dh_bf),
                                 ex.need(f"ffn_cb{layer}", dh_bf))
    ex.at(f"{tag}_dact", dg)
    g_down = _mm(f"{tag}_gdown", act, dh_bf, grid=(N_FF_BLK,),
                 a_spec=pl.BlockSpec((None, T, FF_BLK), lambda j: (j, 0, 0)),
                 b_spec=pl.BlockSpec((T, D), lambda j: (0, 0)),
                 o_spec=pl.BlockSpec((FF_BLK, D), lambda j: (j, 0)),
                 o_shape=(D_FF, D), o_dtype=BF16, dims=TN)
    g_up = _ffn_gup(f"{tag}_gup", dg, dv, hf)
    ex.grad("ffn_w_up", int(layer), g_up.reshape(1, N_DEV, FF_BLK, D))
    ex.grad("ffn_w_down", int(layer), g_down.reshape(1, N_DEV, D_FF // N_DEV, D))
    ex.at(f"{tag}_gup", g_up)
    part = pl.BlockSpec((N_FF_BLK, TR, FF_BLK), lambda i: (0, i, 0))
    dh_in, dh_in_bf, dgain, *onward = _mm_sum(
        f"{tag}_dhf",
        [(dg, part, w_up, pl.BlockSpec((None, N_FF_BLK, FF_BLK, D), lambda i: (0, 0, 0, 0)), NN, 0),
         (dv, part, w_up, pl.BlockSpec((None, N_FF_BLK, FF_BLK, D), lambda i: (0, 1, 0, 0)), NN, 0)],
        grid=(T // TR,), o_spec=pl.BlockSpec((TR, D), lambda i: (i, 0)), o_shape=(T, D), o_dtype=F32,
        norm_bwd=(h, [gain], dh), post=post)
    ex.at(f"{tag}_dhf", dh_in)
    return (dh_in, dh_in_bf, dgain[0], dcw, dcb, *onward)


def _local_step(x, pos, tgt, rep, ex):
    attn_norm, ffn_norm, final_norm = rep["attn_norm"], rep["ffn_norm"], rep["final_norm"]
    half = QK_ROPE // 2
    inv = 1.0 / (ROPE_THETA ** (jnp.arange(half, dtype=F32) / half))
    inv_freq = jnp.concatenate([inv, inv, jnp.zeros((128 - 2 * half,), F32)]).reshape(1, 128)
    tables = _rope_tables(pos, inv_freq)

    hn0 = _rms_fwd("l0_norm", x, attn_norm[0:1])
    ex.at("l0_norm", hn0)
    w_in = ex.need("sc_w_in", hn0)
    zb, zc, zu, y = _mixer_in(hn0, w_in, ex.need("sc_conv_w", hn0))
    ex.at("l0_in", y)
    h1 = _mm_rows("l0_out", y, ex.need("sc_w_out", y), NN, F32, D, tn=512, add=x)
    ex.at("l0_out", h1)
    h2, ffn0 = _ffn_layer_fwd("f0", h1, ffn_norm[0:1], ex)

    w_uq = ex.need("w_uq", h2)
    hk, hn1, ckv_raw, ckv, kr, kn, vv, cq_raw, cq, q = _attn_pre(
        h2, rep["kv_in_norm"], attn_norm[1:2], rep["kv_latent_norm"], rep["q_latent_norm"], ex.need("w_dkv", h2),
        ex.need("w_kr", h2), ex.need("w_uk", h2), ex.need("w_uv", h2), ex.need("w_dq", h2), w_uq, tables)

    o, lse = _attn_fwd(q, kn, kr, vv)
    ex.at("attn_fwd", o)
    w_o = ex.need("w_o", o)
    h3 = _mm_rows("attn_out", o, w_o, NN, F32, D, tn=512, add=h2)
    (loss, dh4, dh4_bf, d_final), ffn1 = _ffn_layer_fwd("f1", h3, ffn_norm[1:2], ex,
                                                        final=(final_norm.reshape(1, D), tgt))

    dh3, dh3_bf, d_fn1, dcw1, dcb1, do = _ffn_layer_bwd("f1", h3, ffn_norm[1:2], ex, ffn1, dh4, dh4_bf,
                                                        post=(w_o, NT, N_HEADS * V_HEAD))
    ex.at("f1_bwd", dh3)

    dq_pre, dkn, dkr, dvv = _attn_bwd(q, kn, kr, vv, o, do, lse, tables)

    dcq_raw_bf, dckv_raw_bf, dkr_raw_bf, dh2, dh2_bf, d_qln, d_kvln, d_an1, d_kvin = _attn_post(
        dq_pre, dkn, dvv, dkr, cq_raw, ckv_raw, h2, dh3, rep["q_latent_norm"], rep["kv_latent_norm"], attn_norm[1:2],
        rep["kv_in_norm"], w_uq, ex.need("w_uk", dkn), ex.need("w_uv", dvv), ex.need("w_dq", dq_pre),
        ex.need("w_dkv", dkn), ex.need("w_kr", dkr), tables)
    g_uq, g_dq, g_o = _wgrads("g_q", [(dq_pre, cq), (hn1, dcq_raw_bf), (o, dh3_bf)])
    ex.grad("w_uq", None, g_uq[:, :QK_NOPE + QK_ROPE].reshape(1, N_DEV, QK_NOPE + QK_ROPE, Q_LORA))
    ex.grad("w_dq", None, g_dq.reshape(1, N_DEV, D // N_DEV, Q_LORA))
    ex.grad("w_o", None, g_o.reshape(1, N_DEV, D // N_DEV, D))

    g_uk, g_uv, g_dkv, g_kr = _wgrads("g_kv", [(ckv, dkn), (ckv, dvv), (hk, dckv_raw_bf), (dkr_raw_bf, hk)])
    ex.grad("w_uk", None, g_uk)
    ex.grad("w_uv", None, g_uv)
    ex.grad("w_dkv", None, g_dkv.reshape(1, N_DEV, D // N_DEV, KV_LORA))
    ex.grad("w_kr", None, g_kr[:QK_ROPE])
    ex.at("kv_bwd", dh2)

    dh1, dh1_bf, d_fn0, dcw0, dcb0 = _ffn_layer_bwd("f0", h1, ffn_norm[0:1], ex, ffn0, dh2, dh2_bf)
    ex.at("f0_bwd", dh1)

    ex.grad("sc_w_out", None, _mm_wgrad("g_sc_w_out", y, dh1_bf).reshape(1, N_DEV, D // N_DEV, D))
    dz, d_scw = _mixer_out_bwd(dh1_bf, ex.need("sc_w_out", dh1_bf), zb, zc, zu, ex.need("sc_conv_w", dh1_bf))
    g_in = _mm_wgrad("g_sc_w_in", hn0, dz)
    ex.grad("sc_w_in", None, g_in)
    ex.at("sc_bwd", g_in)
    ex.at("d_l0_in", g_in)
    w_in = ex.need("sc_w_in", dz)
    grad_x, _, (d_an0,) = _mm_sum(
        "d_l0_in", [(dz[None], pl.BlockSpec((1, TS, dz.shape[1]), lambda i: (0, i, 0)),
                     w_in[None], pl.BlockSpec((1,) + w_in.shape, lambda i: (0, 0, 0)), NT, 0)],
        norm_bwd=(x, [attn_norm[0:1]], dh1),
        grid=(T // TS,), o_spec=pl.BlockSpec((TS, D), lambda i: (i, 0)), o_shape=(T, D), o_dtype=F32)

    small = {
        "attn_norm": jnp.concatenate([d_an0, d_an1], axis=0),
        "ffn_norm": jnp.concatenate([d_fn0, d_fn1], axis=0),
        "final_norm": d_final.reshape(D),
        "kv_in_norm": d_kvin.reshape(D),
        "kv_latent_norm": d_kvln.reshape(KV_LORA),
        "q_latent_norm": d_qln,
        "ffn_conv_b": jnp.stack([dcb0, dcb1]).transpose(0, 2, 1, 3).reshape(2, D_FF),
        "sc_conv_w": d_scw,
        "ffn_conv_w": jnp.stack([dcw0, dcw1]).transpose(0, 2, 1, 3).reshape(2, 3, D_FF),
    }
    return loss, grad_x, small


def _place():
    return lax.axis_index("x"), lax.axis_index("y"), lax.axis_index("c")


def _peers():
    x, y, c = _place()
    return (x, y, 1 - c), [(1 - x, y), (x, 1 - y), (1 - x, 1 - y)]


def _window(ref, kind, dev):
    if kind == "blocked":
        return ref.at[:, dev]
    width = ref.shape[-1] // N_DEV
    return ref.at[:, pl.ds(pl.multiple_of(dev * width, 128), width)]


HBM_SPEC = pl.BlockSpec(memory_space=pltpu.HBM)
SEM_SPEC = pl.BlockSpec(memory_space=pltpu.SEMAPHORE)
EFFECT = pltpu.SideEffectType.DATAFLOW_SIDE_EFFECTING
TOKEN = jax.ShapeDtypeStruct((8, 128), F32)


def _hbm(a):
    return pltpu.with_memory_space_constraint(a, pltpu.HBM)


def _copies_start(name, jobs):
    nj = len(jobs)
    counts = [(len(srcs), len(lands)) for srcs, lands, _, _ in jobs]
    n_arr = sum(ns + nl for ns, nl in counts)

    def body(*refs):
        sems, token = refs[n_arr:n_arr + 2 * nj], refs[-1]
        at = 0
        for j, ((ns, nl), (_, _, ncopy, plan)) in enumerate(zip(counts, jobs)):
            copies = plan(refs[at:at + ns], refs[at + ns:at + ns + nl])
            assert len(copies) == ncopy
            for k, (sent, dst, to, _) in enumerate(copies):
                pltpu.make_async_remote_copy(src_ref=sent, dst_ref=dst, send_sem=sems[2 * j].at[k],
                                             recv_sem=sems[2 * j + 1].at[k], device_id=to, device_id_type=MESH).start()
            at += ns + nl
        token[...] = jnp.zeros_like(token)

    arrays = [a for srcs, lands, _, _ in jobs for a in list(srcs) + list(lands)]
    sem_shapes = [pltpu.SemaphoreType.DMA((ncopy,)) for _, _, ncopy, _ in jobs for _ in range(2)]
    outs = pl.pallas_call(
        body, name=name, in_specs=[HBM_SPEC] * n_arr,
        out_specs=[SEM_SPEC] * (2 * nj) + [HBM_SPEC] * n_arr + [VMEM_SPEC],
        out_shape=sem_shapes + [pltpu.HBM(a.shape, a.dtype) for a in arrays] + [TOKEN],
        input_output_aliases={i: 2 * nj + i for i in range(n_arr)},
        compiler_params=pltpu.CompilerParams(has_side_effects=EFFECT))(*[_hbm(a) for a in arrays])
    _Chain.last = outs[-1]
    flights, at = [], 2 * nj
    for j, (ns, nl) in enumerate(counts):
        flights.append((outs[2 * j], outs[2 * j + 1], list(outs[at:at + ns]), list(outs[at + ns:at + ns + nl])))
        at += ns + nl
    return flights


def _copies_wait(name, started, ncopy, plan):
    send, recv, srcs, lands = started
    ns, nl = len(srcs), len(lands)

    def body(*refs):
        send_ref, recv_ref, token = refs[ns + nl], refs[ns + nl + 1], refs[-1]
        copies = plan(refs[:ns], refs[ns:ns + nl])
        assert len(copies) == ncopy
        for k, (sent, _, to, landed) in enumerate(copies):
            cp = pltpu.make_async_remote_copy(src_ref=sent, dst_ref=landed, send_sem=send_ref.at[k],
                                              recv_sem=recv_ref.at[k], device_id=to, device_id_type=MESH)
            cp.wait_send()
            cp.wait_recv()
        token[...] = jnp.zeros_like(token)

    arrays = list(srcs) + list(lands)
    outs = pl.pallas_call(
        body, name=name, in_specs=[HBM_SPEC] * (ns + nl) + [SEM_SPEC] * 2 + [ANY_SPEC],
        out_specs=[HBM_SPEC] * (ns + nl) + [VMEM_SPEC], out_shape=[pltpu.HBM(a.shape, a.dtype) for a in arrays] + [TOKEN],
        input_output_aliases={i: i for i in range(ns + nl)},
        compiler_params=pltpu.CompilerParams(has_side_effects=EFFECT))(*arrays, send, recv, _Chain.last)
    _Chain.last = outs[-1]
    return list(outs[:ns]), list(outs[ns:-1])


def _plan_gather_chips(kinds):
    def plan(srcs, lands):
        x, y, c = _place()
        sibling, chips = _peers()
        out = []
        for t, kind in enumerate(kinds):
            mine = _window(lands[t], kind, 4 * x + 2 * y + c)
            out.append((srcs[t], mine, (x, y, c), mine))
            out.append((srcs[t], mine, sibling, _window(lands[t], kind, 4 * x + 2 * y + 1 - c)))
            for px, py in chips:
                out.append((srcs[t], mine, (px, py, c), _window(lands[t], kind, 4 * px + 2 * py + c)))
        return out
    return plan, 5 * len(kinds)


def _plan_gather_all(n):
    def plan(srcs, lands):
        x, y, c = _place()
        out = []
        for t in range(n):
            mine = lands[t].at[:, 4 * x + 2 * y + c]
            for m in range(N_DEV):
                px, py, pc = (1 - x if m & 4 else x), (1 - y if m & 2 else y), (1 - c if m & 1 else c)
                out.append((srcs[t], mine, (px, py, pc), lands[t].at[:, 4 * px + 2 * py + pc]))
        return out
    return plan, N_DEV * n


def _plan_gather_sibling(kinds):
    def plan(srcs, lands):
        _, _, c = _place()
        sibling, chips = _peers()
        out = []
        for t, kind in enumerate(kinds):
            for px, py in chips:
                w = _window(lands[t], kind, 4 * px + 2 * py + c)
                out.append((w, w, sibling, _window(lands[t], kind, 4 * px + 2 * py + 1 - c)))
        return out
    return plan, 3 * len(kinds)


def _plan_scatter_sibling(kinds):
    def plan(srcs, lands):
        _, _, c = _place()
        sibling, _ = _peers()
        out = []
        for t, kind in enumerate(kinds):
            for k in range(N_CHIP):
                out.append((_window(srcs[t], kind, 2 * k + 1 - c), lands[t].at[k], sibling, lands[t].at[k]))
        return out
    return plan, N_CHIP * len(kinds)


def _plan_scatter_chips(n):
    def plan(srcs, lands):
        x, y, c = _place()
        _, chips = _peers()
        out = []
        for t in range(n):
            for px, py in chips:
                out.append((srcs[t].at[2 * px + py], lands[t].at[2 * x + y], (px, py, c), lands[t].at[2 * px + py]))
        return out
    return plan, 3 * n


def _landing(shard, kind):
    if kind == "blocked":
        return lax.empty((shard.shape[0], N_DEV) + shard.shape[1:], shard.dtype)
    return lax.empty((shard.shape[0], N_DEV * shard.shape[1]), shard.dtype)


def _chip_sums(name, grads, kinds, recvs, c):
    n = len(grads)
    in_specs, out_specs, out_shape, args = [], [], [], []
    for gr, kind, rv in zip(grads, kinds, recvs):
        if kind == "blocked":
            rows, w = gr.shape[2], gr.shape[3]
            in_specs.append(pl.BlockSpec((None, None, rows, w), lambda k, cref: (0, 2 * k + cref[0], 0, 0)))
        else:
            rows, w = gr.shape[0], gr.shape[1] // N_DEV
            in_specs.append(pl.BlockSpec((rows, w), lambda k, cref: (0, 2 * k + cref[0])))
        blk = pl.BlockSpec((None, rows, w), lambda k, cref: (k, 0, 0))
        in_specs.append(blk)
        out_specs.append(blk)
        out_shape.append(jax.ShapeDtypeStruct((N_CHIP, rows, w), BF16))
        args += [gr, rv.reshape(N_CHIP, rows, w)]

    def body(*refs):
        for t in range(n):
            g_ref, r_ref, o_ref = refs[1 + 2 * t], refs[2 + 2 * t], refs[1 + 2 * n + t]
            o_ref[...] = (g_ref[...].astype(F32) + r_ref[...].astype(F32)).astype(BF16)

    return _pallas(body, name=name, n_prefetch=1, grid=(N_CHIP,), in_specs=in_specs, out_specs=out_specs,
                   out_shape=out_shape, params=_params(("parallel",)))(c, *args)


def _adamw_math(g, wv, mv, vv):
    m = ADAM_B1 * mv + (1.0 - ADAM_B1) * g
    v = ADAM_B2 * vv + (1.0 - ADAM_B2) * (g * g)
    m_hat = m / (1.0 - ADAM_B1 ** ADAM_STEP)
    v_hat = v / (1.0 - ADAM_B2 ** ADAM_STEP)
    delta = -ADAM_LR * (m_hat / (jnp.sqrt(v_hat) + ADAM_EPS) + ADAM_WD * wv)
    return delta, m, v


ADAM_STEPS = 2


def _adamw_group(name, items, chip_ids):
    n = len(items)
    in_specs, out_specs, out_shape, args, prevs = [], [], [], [chip_ids], []
    for own, recv, w3, m3, v3, layer, _ in items:
        nl, rows, w = w3.shape
        tr = rows // ADAM_STEPS
        assert tr % 16 == 0, (name, rows)
        in_specs += [pl.BlockSpec((None, tr, w), lambda i, ids, slot=slot: (ids[slot], i, 0)) for slot in range(4)]
        slab = pl.BlockSpec((None, tr, w), lambda i, ids, layer=layer: (layer, i, 0))
        in_specs += [slab] * 3
        out_specs += [slab] * 4
        out_shape += [jax.ShapeDtypeStruct((nl, rows, w), F32)] * 4
        args += [own, recv, recv, recv, w3, m3, v3]
    aliases = {}
    for t, item in enumerate(items):
        if item[6] is not None:
            for k in range(4):
                aliases[len(args) + k] = 4 * t + k
            in_specs += [ANY_SPEC] * 4
            args += list(item[6])
            prevs.append(t)
    n_in = 1 + 7 * n + 4 * len(prevs)

    def body(*refs):
        for t in range(n):
            own_ref, r1_ref, r2_ref, r3_ref, w_ref, m_ref, v_ref = refs[1 + 7 * t:8 + 7 * t]
            g_ref, d_ref, nm_ref, nv_ref = refs[n_in + 4 * t:n_in + 4 * t + 4]
            g = ((own_ref[...].astype(F32) + r1_ref[...].astype(F32)) + r2_ref[...].astype(F32)) + r3_ref[...].astype(F32)
            g_ref[...] = g
            d_ref[...], nm_ref[...], nv_ref[...] = _adamw_math(g, w_ref[...], m_ref[...], v_ref[...])

    outs = _pallas(body, name=name, n_prefetch=1, grid=(ADAM_STEPS,), in_specs=in_specs, out_specs=out_specs,
                   out_shape=out_shape, aliases=aliases, params=_params(("parallel",)))(*args)
    return [list(outs[4 * t:4 * t + 4]) for t in range(n)]


SHARD_ROWS = 8


def _adamw_small(gathered, ws, ms, vs, me):
    n = len(gathered)
    full = [w is not None for w in ws]
    sharded = [w is not None and w.ndim == 3 for w in ws]
    args = list(gathered)
    out_shape = []
    for t in range(n):
        shape = jax.ShapeDtypeStruct(ws[t].shape if sharded[t] else gathered[t].shape[2:], F32)
        if full[t]:
            args += [ws[t], ms[t], vs[t]]
            out_shape += [shape] * 4
        else:
            out_shape += [shape]

    def body(*refs):
        i_in, i_out = n, len(args) + 1
        me_ref = refs[len(args)]
        for t in range(n):
            p_ref = refs[t]
            if sharded[t]:
                w_ref, m_ref, v_ref = refs[i_in:i_in + 3]
                taps, layers, _ = w_ref.shape
                mine = pl.ds(pl.multiple_of(me_ref[0] * SHARD_ROWS, SHARD_ROWS), SHARD_ROWS)
                g = p_ref[0, 0, mine, :]
                for k in range(1, N_DEV):
                    g = g + p_ref[0, k, mine, :]
                for l in range(layers):
                    for k in range(taps):
                        at = (k, slice(l, l + 1), slice(None))
                        row = g[l * taps + k:l * taps + k + 1]
                        refs[i_out][at] = row
                        refs[i_out + 1][at], refs[i_out + 2][at], refs[i_out + 3][at] = _adamw_math(
                            row, w_ref[at], m_ref[at], v_ref[at])
                i_in += 3
                i_out += 4
                continue
            g = p_ref[0, 0]
            for k in range(1, N_DEV):
                g = g + p_ref[0, k]
            refs[i_out][...] = g
            if full[t]:
                w_ref, m_ref, v_ref = refs[i_in:i_in + 3]
                refs[i_out + 1][...], refs[i_out + 2][...], refs[i_out + 3][...] = _adamw_math(
                    g, w_ref[...], m_ref[...], v_ref[...])
                i_in += 3
                i_out += 4
            else:
                i_out += 1

    outs = _pallas(body, name="adamw_small",
                   in_specs=[VMEM_SPEC] * len(args) + [pl.BlockSpec(memory_space=pltpu.SMEM)],
                   out_specs=[VMEM_SPEC] * len(out_shape), out_shape=out_shape,
                   params=pltpu.CompilerParams(vmem_limit_bytes=VMEM_LIMIT))(*args, me)
    result, i = [], 0
    for t in range(n):
        k = 4 if full[t] else 1
        result.append(list(outs[i:i + k]))
        i += k
    return result


KIND = {"sc_w_in": "cols", "sc_w_out": "blocked", "w_dkv": "blocked", "w_kr": "cols", "w_uk": "cols", "w_uv": "cols",
        "w_dq": "blocked", "w_uq": "blocked", "w_o": "blocked", "ffn_w_up": "blocked", "ffn_w_down": "blocked",
        "conv": "blocked"}
GATHER_GROUPS = (("mixer", ("sc_w_in", "sc_w_out", "conv")),
                 ("up0", ("ffn_w_up0",)),
                 ("down0", ("ffn_w_down0",)),
                 ("attn", ("w_dkv", "w_kr", "w_uk", "w_uv", "w_dq", "w_uq", "w_o")),
                 ("ffn1", ("ffn_w_up1", "ffn_w_down1")))
SCATTER_GROUPS = (("ffn1", (("ffn_w_up", 1), ("ffn_w_down", 1))),
                  ("attn", (("w_o", None), ("w_uq", None), ("w_dq", None), ("w_uk", None), ("w_uv", None),
                            ("w_dkv", None), ("w_kr", None))),
                  ("ffn0", (("ffn_w_up", 0), ("ffn_w_down", 0))),
                  ("mixer", (("sc_w_out", None), ("sc_w_in", None))))
SCHEDULE = {
    "begin": (("gather_start", "mixer"),),
    "l0_norm": (("gather_forward", "mixer"), ("gather_start", "up0")),
    "l0_out": (("gather_forward", "up0"), ("gather_start", "down0"), ("gather_start", "attn")),
    "f0_up": (("gather_forward", "down0"), ("gather_forward", "attn"), ("gather_start", "ffn1")),
    "attn_fwd": (("gather_forward", "ffn1"),),
    "f1_gup": (("scatter_sibling", "ffn1"),),
    "f1_dhf": (("scatter_chips", "ffn1"),),
    "kv_bwd": (("scatter_sibling", "attn"),),
    "f0_dact": (("scatter_chips", "attn"),),
    "f0_gup": (("scatter_sibling", "ffn0"),),
    "f0_dhf": (("scatter_chips", "ffn0"),),
    "sc_bwd": (("scatter_sibling", "mixer"), ("scatter_done", "attn")),
    "d_l0_in": (("scatter_chips", "mixer"),),
}
FINISH = (("scatter_done", "ffn1"), ("scatter_done", "ffn0"), ("scatter_done", "mixer"))
STAGES = {"gather_start": 1, "gather_forward": 2, "gather_done": 3,
          "scatter_sibling": 1, "scatter_chips": 2, "scatter_done": 3}
SMALL_W_ROWS = 24


def _pack(arrays, rows):
    flat = jnp.concatenate([a.reshape(-1).astype(F32) for a in arrays])
    return jnp.pad(flat, (0, rows * 128 - flat.shape[0])).reshape(rows, 128)


def _cast_shards(items):
    arrays = []
    for a, _, _ in items:
        if not any(a is b for b in arrays):
            arrays.append(a)
    slot = [next(i for i, b in enumerate(arrays) if b is a) for a, _, _ in items]

    def body(*refs):
        for t, (a, layer, rows) in enumerate(items):
            w_ref, o_ref = refs[slot[t]], refs[len(arrays) + t]
            r, c = a.shape[-2:]
            if a.ndim == 3:
                o_ref[:, :r] = w_ref[(layer or 0):(layer or 0) + 1].astype(BF16)
                if rows > r:
                    o_ref[:, r:] = jnp.zeros((1, rows - r, c), BF16)
            else:
                o_ref[:r] = w_ref[...].astype(BF16)
                if rows > r:
                    o_ref[r:] = jnp.zeros((rows - r, c), BF16)

    out_shape = [jax.ShapeDtypeStruct(((1,) if a.ndim == 3 else ()) + (rows, a.shape[-1]), BF16)
                 for a, _, rows in items]
    return _pallas(body, name="cast_shards", in_specs=[VMEM_SPEC] * len(arrays), out_specs=[VMEM_SPEC] * len(items),
                   out_shape=out_shape, params=pltpu.CompilerParams(vmem_limit_bytes=VMEM_LIMIT))(*arrays)


STORED_TRANSPOSED = ("ffn_w_up", "w_uq", "w_kr")


def _stored(name, a):
    return jnp.swapaxes(a, -1, -2) if name in STORED_TRANSPOSED else a


def _base(name):
    if name.startswith("ffn_w_") and name[-1] in "01":
        return name[:-1], int(name[-1])
    return name, None


class _Exchange:
    def __init__(self, wts, mom, var, ffn_conv_b):
        self.wts, self.mom, self.var = wts, mom, var
        x, y, c = _place()
        self.c_arr = jnp.reshape(c, (1,)).astype(jnp.int32)
        chip = 2 * x + y
        self.chip_ids = jnp.stack([chip, chip ^ 1, chip ^ 2, chip ^ 3]).astype(jnp.int32)
        self.ready = {"ffn_cb0": ffn_conv_b.reshape(2, N_FF_BLK, 1, FF_BLK)[0],
                      "ffn_cb1": ffn_conv_b.reshape(2, N_FF_BLK, 1, FF_BLK)[1]}
        self.gathers, self.group_of = {}, {}
        self.grads, self.scatters, self.results, self.queue = {}, {}, {}, []
        for gname, names in GATHER_GROUPS:
            self.gathers[gname] = dict(stage=0, names=names, kinds=[KIND[_base(nm)[0]] for nm in names])
            for nm in names:
                self.group_of[nm] = gname
        for nm in ("sc_conv_w", "ffn_cw0", "ffn_cw1"):
            self.group_of[nm] = "mixer"
        self.cast, self.f32 = {}, {}
        self.at("begin", None)
        later = [nm for gname, names in GATHER_GROUPS[1:] for nm in names]
        self.cast = dict(zip(later, _cast_shards([self._shard_f32(nm) for nm in later])))

    def _shard_f32(self, name):
        base, layer = _base(name)
        if base not in self.f32:
            a = _stored(base, self.wts[base])
            self.f32[base] = a.reshape(a.shape[-2:]) if KIND[base] == "cols" else a.reshape((-1,) + a.shape[-2:])
        a = self.f32[base]
        return a, layer, {"w_kr": 128, "w_uq": QK_PAD}.get(base, a.shape[-2])

    def _shard(self, name):
        if name in self.cast:
            return self.cast[name]
        if name == "conv":
            return _pack([self.wts["sc_conv_w"], self.wts["ffn_conv_w"]], SMALL_W_ROWS).reshape(1, SMALL_W_ROWS, 128)
        base, layer = _base(name)
        a = _stored(base, self.wts[base])
        if layer is not None:
            a = a[layer:layer + 1]
        if KIND[base] == "cols":
            return a.reshape(a.shape[-2], a.shape[-1]).astype(BF16)
        return a.reshape((-1,) + a.shape[-2:]).astype(BF16)

    def _start(self, name, srcs, lands, ncopy, plan, st):
        self.queue.append((name, (srcs, lands, ncopy, plan), st))

    def _flush(self):
        if self.queue:
            flights = _copies_start("__".join(name for name, _, _ in self.queue), [job for _, job, _ in self.queue])
            for (_, _, st), flight in zip(self.queue, flights):
                st["flight"] = flight
            self.queue = []

    def _flight(self, st):
        self._flush()
        return st["flight"]

    def _gather_to(self, gname, stage, after):
        st = self.gathers[gname]
        if st["stage"] < 1 <= stage:
            shards = [self._shard(nm) for nm in st["names"]]
            lands = [_landing(s, kind) for s, kind in zip(shards, st["kinds"])]
            plan, ncopy = _plan_gather_chips(st["kinds"])
            self._start(f"ag_{gname}_chips", shards, lands, ncopy, plan, st)
            st["stage"] = 1
        if st["stage"] < 2 <= stage:
            plan, ncopy = _plan_gather_chips(st["kinds"])
            _, lands = _copies_wait(f"ag_{gname}_chips_wait", self._flight(st), ncopy, plan)
            plan, ncopy = _plan_gather_sibling(st["kinds"])
            self._start(f"ag_{gname}_sibling", [], lands, ncopy, plan, st)
            st["stage"] = 2
        if st["stage"] < 3 <= stage:
            plan, ncopy = _plan_gather_sibling(st["kinds"])
            _, lands = _copies_wait(f"ag_{gname}_sibling_wait", self._flight(st), ncopy, plan)
            for nm, land in zip(st["names"], lands):
                self._arrived(nm, land)
            st["stage"] = 3

    def _arrived(self, name, land):
        if name == "conv":
            conv = land.reshape(N_DEV, SMALL_W_ROWS * 128)
            self.ready["sc_conv_w"] = conv[:, :3 * 128].reshape(N_DEV, 3, 128).transpose(1, 0, 2).reshape(3, D)
            fcw = conv[:, 3 * 128:3 * 128 + 6 * 352].reshape(N_DEV, 2, 3, 352).transpose(1, 2, 0, 3)
            fcw = fcw.reshape(2, 3, N_FF_BLK, FF_BLK).transpose(0, 2, 1, 3)
            self.ready["ffn_cw0"], self.ready["ffn_cw1"] = fcw[0], fcw[1]
        elif name in ("sc_w_in", "w_uk", "w_uv", "w_kr") or name.startswith("ffn_w_up"):
            self.ready[name] = land
        elif name.startswith("ffn_w_down"):
            self.ready[name] = land.reshape(1, N_FF_BLK, FF_BLK, D)
        elif name == "w_uq":
            self.ready[name] = land.reshape(N_HEADS, QK_PAD, Q_LORA)
        else:
            self.ready[name] = land.reshape(D, land.shape[-1])

    def need(self, name, after):
        if name not in self.ready:
            self._gather_to(self.group_of[name], 3, after)
            self._flush()
        return self.ready[name]

    def grad(self, name, layer, array):
        self.grads[(name, layer)] = array

    def _scatter_to(self, gname, stage, after):
        keys = dict(SCATTER_GROUPS)[gname]
        st = self.scatters.setdefault(gname, dict(stage=0))
        kinds = [KIND[nm] for nm, _ in keys]
        if st["stage"] < 1 <= stage:
            grads = [self.grads[key] for key in keys]
            lands = []
            for gr, kind in zip(grads, kinds):
                shard = (gr.shape[0],) + gr.shape[2:] if kind == "blocked" else (gr.shape[0], gr.shape[1] // N_DEV)
                lands.append(lax.empty((N_CHIP,) + shard, BF16))
            plan, ncopy = _plan_scatter_sibling(kinds)
            self._start(f"rs_{gname}_sibling", grads, lands, ncopy, plan, st)
            st["stage"] = 1
        if st["stage"] < 2 <= stage:
            plan, ncopy = _plan_scatter_sibling(kinds)
            grads, recvs = _copies_wait(f"rs_{gname}_sibling_wait", self._flight(st), ncopy, plan)
            sums = _chip_sums(f"rs_{gname}_sums", grads, kinds, recvs, self.c_arr)
            lands = [lax.empty(s.shape, BF16) for s in sums]
            plan, ncopy = _plan_scatter_chips(len(sums))
            self._start(f"rs_{gname}_chips", sums, lands, ncopy, plan, st)
            st["stage"] = 2
        if st["stage"] < 3 <= stage:
            plan, ncopy = _plan_scatter_chips(len(keys))
            sums, recvs = _copies_wait(f"rs_{gname}_chips_wait", self._flight(st), ncopy, plan)
            items = []
            for (nm, layer), own, rv in zip(keys, sums, recvs):
                nl = 1 if layer is None else 2
                rows, w = own.shape[1], own.shape[2]
                w3, m3, v3 = (_stored(nm, src[nm]).reshape(nl, rows, w) for src in (self.wts, self.mom, self.var))
                items.append((own, rv, w3, m3, v3, 0 if layer is None else layer, self.results.get(nm)))
            outs = _adamw_group(f"adamw_{gname}", items, self.chip_ids)
            for (nm, _), out in zip(keys, outs):
                self.results[nm] = out
            st["stage"] = 3

    def at(self, place, after):
        for action, gname in SCHEDULE.get(place, ()):
            self._advance(action, gname, after)
        self._flush()

    def _advance(self, action, gname, after):
        if action.startswith("gather"):
            self._gather_to(gname, STAGES[action], after)
        else:
            self._scatter_to(gname, STAGES[action], after)

    def finish(self, after):
        for action, gname in FINISH:
            self._advance(action, gname, after)
        for gname, _ in SCATTER_GROUPS:
            self._scatter_to(gname, 3, after)
        return {nm: [_stored(nm, o.reshape(_stored(nm, self.wts[nm]).shape)) for o in outs]
                for nm, outs in self.results.items()}


REPLICATED = ("attn_norm", "ffn_norm", "final_norm", "kv_in_norm", "kv_latent_norm", "q_latent_norm", "ffn_conv_b")
WEIGHTS = ("attn_norm", "ffn_norm", "final_norm", "sc_w_in", "sc_conv_w", "sc_w_out", "kv_in_norm", "w_dkv",
           "kv_latent_norm", "w_kr", "w_uk", "w_uv", "w_dq", "q_latent_norm", "w_uq", "w_o", "ffn_w_up", "ffn_conv_w",
           "ffn_conv_b", "ffn_w_down")


def kernel(x, positions, attn_norm, ffn_norm, final_norm, sc_w_in, sc_conv_w, sc_w_out, kv_in_norm, w_dkv, kv_latent_norm, w_kr, w_uk, w_uv, w_dq, q_latent_norm, w_uq, w_o, ffn_w_up, ffn_conv_w, ffn_conv_b, ffn_w_down, loss_target, m_attn_norm, m_ffn_norm, m_final_norm, m_sc_w_in, m_sc_conv_w, m_sc_w_out, m_kv_in_norm, m_w_dkv, m_kv_latent_norm, m_w_kr, m_w_uk, m_w_uv, m_w_dq, m_q_latent_norm, m_w_uq, m_w_o, m_ffn_w_up, m_ffn_conv_w, m_ffn_conv_b, m_ffn_w_down, v_attn_norm, v_ffn_norm, v_final_norm, v_sc_w_in, v_sc_conv_w, v_sc_w_out, v_kv_in_norm, v_w_dkv, v_kv_latent_norm, v_w_kr, v_w_uk, v_w_uv, v_w_dq, v_q_latent_norm, v_w_uq, v_w_o, v_ffn_w_up, v_ffn_conv_w, v_ffn_conv_b, v_ffn_w_down):
    wts = dict(attn_norm=attn_norm, ffn_norm=ffn_norm, final_norm=final_norm, sc_w_in=sc_w_in, sc_conv_w=sc_conv_w,
               sc_w_out=sc_w_out, kv_in_norm=kv_in_norm, w_dkv=w_dkv, kv_latent_norm=kv_latent_norm, w_kr=w_kr,
               w_uk=w_uk, w_uv=w_uv, w_dq=w_dq, q_latent_norm=q_latent_norm, w_uq=w_uq, w_o=w_o, ffn_w_up=ffn_w_up,
               ffn_conv_w=ffn_conv_w, ffn_conv_b=ffn_conv_b, ffn_w_down=ffn_w_down)
    mom = dict(attn_norm=m_attn_norm, ffn_norm=m_ffn_norm, final_norm=m_final_norm, sc_w_in=m_sc_w_in,
               sc_conv_w=m_sc_conv_w, sc_w_out=m_sc_w_out, kv_in_norm=m_kv_in_norm, w_dkv=m_w_dkv,
               kv_latent_norm=m_kv_latent_norm, w_kr=m_w_kr, w_uk=m_w_uk, w_uv=m_w_uv, w_dq=m_w_dq,
               q_latent_norm=m_q_latent_norm, w_uq=m_w_uq, w_o=m_w_o, ffn_w_up=m_ffn_w_up, ffn_conv_w=m_ffn_conv_w,
               ffn_conv_b=m_ffn_conv_b, ffn_w_down=m_ffn_w_down)
    var = dict(attn_norm=v_attn_norm, ffn_norm=v_ffn_norm, final_norm=v_final_norm, sc_w_in=v_sc_w_in,
               sc_conv_w=v_sc_conv_w, sc_w_out=v_sc_w_out, kv_in_norm=v_kv_in_norm, w_dkv=v_w_dkv,
               kv_latent_norm=v_kv_latent_norm, w_kr=v_w_kr, w_uk=v_w_uk, w_uv=v_w_uv, w_dq=v_w_dq,
               q_latent_norm=v_q_latent_norm, w_uq=v_w_uq, w_o=v_w_o, ffn_w_up=v_ffn_w_up, ffn_conv_w=v_ffn_conv_w,
               ffn_conv_b=v_ffn_conv_b, ffn_w_down=v_ffn_w_down)
    xi, yi, ci = _place()
    me = 4 * xi + 2 * yi + ci
    _Chain.last = None

    ex = _Exchange(wts, mom, var, ffn_conv_b)
    rep = {
        "attn_norm": attn_norm, "ffn_norm": ffn_norm, "final_norm": final_norm,
        "kv_in_norm": kv_in_norm.reshape(1, D), "kv_latent_norm": kv_latent_norm.reshape(1, KV_LORA),
        "q_latent_norm": q_latent_norm.reshape(1, Q_LORA),
    }
    loss, grad_x, small = _local_step(x.reshape(T, D), positions.reshape(T, 1), loss_target.reshape(T, D), rep, ex)

    def rows_of(a):
        return a.reshape(-1, a.shape[-1])

    def device_rows(a):
        taps, c = a.shape[-2], a.shape[-1] // N_DEV
        rows = a.reshape(-1, taps, N_DEV, c).transpose(2, 0, 1, 3).reshape(N_DEV, -1, c)
        return jnp.pad(rows, ((0, 0), (0, SHARD_ROWS - rows.shape[1]), (0, 0))).reshape(N_DEV * SHARD_ROWS, c)

    def taps_first(a):
        return jnp.transpose(a, (1, 0, 2))

    sharded = ("sc_conv_w", "ffn_conv_w")
    shards = ([loss.reshape(1, 1, 128)] + [rows_of(small[nm])[None] for nm in REPLICATED]
              + [device_rows(small[nm])[None] for nm in sharded])
    plan, ncopy = _plan_gather_all(len(shards))
    flight, = _copies_start("ag_small", [(shards, [lax.empty((1, N_DEV) + s.shape[1:], F32) for s in shards], ncopy, plan)])
    results = ex.finish(grad_x)
    _, gathered = _copies_wait("ag_small_wait", flight, ncopy, plan)
    params = [[None] + [rows_of(src[nm]) for nm in REPLICATED] + [taps_first(src[nm]) for nm in sharded]
              for src in (wts, mom, var)]
    summed = _adamw_small(gathered, *params, me.astype(jnp.int32).reshape(1))
    loss_total = summed[0][0][0, 0]
    for nm, vals in zip(REPLICATED, summed[1:1 + len(REPLICATED)]):
        results[nm] = [a.reshape(wts[nm].shape) for a in vals]
    for nm, vals in zip(sharded, summed[1 + len(REPLICATED):]):
        results[nm] = [taps_first(a) for a in vals]

    outs = [loss_total, grad_x.reshape(1, T, D)]
    for slot in range(4):
        outs.extend(results[nm][slot] for nm in WEIGHTS)
    return tuple(outs)
```

```python
import jax
import jax.numpy as jnp
from jax import lax
from jax.experimental import pallas as pl
from jax.experimental.pallas import tpu as pltpu

F32 = jnp.float32
BF16 = jnp.bfloat16

T = 2048
D = 1024
N_HEADS = 8
QK_NOPE = 128
QK_ROPE = 64
V_HEAD = 128
Q_LORA = 384
KV_LORA = 256
D_FF = 2816
CHUNK = 64
ROPE_THETA = 10000.0
EPS = 1e-6
NEG_INF = -1e30
ADAM_LR = 0.001
ADAM_B1 = 0.9
ADAM_B2 = 0.999
ADAM_EPS = 1e-08
ADAM_WD = 0.01
ADAM_STEP = 10

N_DEV = 8
N_CHIP = 4
FF_BLK = D_FF * 2 // N_DEV
N_FF_BLK = D_FF // FF_BLK
QK_PAD = 256
HALO = 16

TM = 1024
TS = 512
TR = 256
TQ = 512
VMEM_LIMIT = 56 * 1024 * 1024

NN = (((1,), (0,)), ((), ()))
NT = (((1,), (1,)), ((), ()))
TN = (((0,), (0,)), ((), ()))
MESH = pl.DeviceIdType.MESH


def _params(sem):
    return pltpu.CompilerParams(dimension_semantics=sem, vmem_limit_bytes=VMEM_LIMIT)


ANY_SPEC = pl.BlockSpec(memory_space=pl.ANY)
VMEM_SPEC = pl.BlockSpec(memory_space=pltpu.VMEM)


class _Chain:
    last = None


def _pallas(body, *, name, in_specs, out_specs, out_shape, grid=(), scratch_shapes=(), n_prefetch=0, aliases=None,
            params=None):
    def run(*args):
        after = _Chain.last
        n_lead = len(args)
        specs, operands, fn = list(in_specs), list(args), body
        if after is not None:
            def fn(*refs):
                return body(*refs[:n_lead], *refs[n_lead + 1:])
            specs.append(ANY_SPEC)
            operands.append(after)
        kw = dict(name=name, out_shape=out_shape, input_output_aliases=aliases or {})
        if params is not None:
            kw["compiler_params"] = params
        if n_prefetch:
            kw["grid_spec"] = pltpu.PrefetchScalarGridSpec(
                num_scalar_prefetch=n_prefetch, grid=grid, in_specs=specs, out_specs=out_specs,
                scratch_shapes=scratch_shapes)
        else:
            kw.update(grid=grid, in_specs=specs, out_specs=out_specs, scratch_shapes=scratch_shapes)
        outs = pl.pallas_call(fn, **kw)(*operands)
        _Chain.last = outs[0] if isinstance(outs, (list, tuple)) else outs
        return outs
    return run


def _mm(name, a, b, *, grid, a_spec, b_spec, o_spec, o_shape, o_dtype, dims, k_axis=None, acc_shape=None,
        add=None, add_spec=None):
    nk = grid[k_axis] if k_axis is not None else 1
    has_add = add is not None

    def body(*refs):
        a_ref, b_ref = refs[0], refs[1]
        p = 2
        add_ref = None
        if has_add:
            add_ref = refs[p]
            p += 1
        o_ref = refs[p]
        p += 1
        r = lax.dot_general(a_ref[...].astype(BF16), b_ref[...].astype(BF16), dims, preferred_element_type=F32)
        if k_axis is None:
            if has_add:
                r = r + add_ref[...].astype(F32)
            o_ref[...] = r.astype(o_dtype)
        else:
            acc = refs[p]
            k = pl.program_id(k_axis)

            @pl.when(k == 0)
            def _():
                acc[...] = r

            @pl.when(k > 0)
            def _():
                acc[...] += r

            @pl.when(k == nk - 1)
            def _():
                t = acc[...]
                if has_add:
                    t = t + add_ref[...].astype(F32)
                o_ref[...] = t.astype(o_dtype)

    in_specs = [a_spec, b_spec]
    args = [a, b]
    if has_add:
        in_specs.append(add_spec if add_spec is not None else o_spec)
        args.append(add)
    sem = tuple("arbitrary" if ax == k_axis else "parallel" for ax in range(len(grid)))
    scratch = [pltpu.VMEM(acc_shape, F32)] if k_axis is not None else []
    return _pallas(body, name=name, grid=grid, in_specs=in_specs, out_specs=o_spec,
                   out_shape=jax.ShapeDtypeStruct(o_shape, o_dtype), scratch_shapes=scratch, params=_params(sem))(*args)


def _mm_sum(name, parts, *, grid, o_spec, o_shape, o_dtype, add=None, norm_bwd=None, post=None):
    has_add = add is not None
    np_ = len(parts)
    nn = 1 if norm_bwd is None else len(norm_bwd[1])
    has_res = norm_bwd is not None and norm_bwd[2] is not None
    has_post = post is not None

    def body(*refs):
        accs = [None] * nn
        for p, (_, _, _, _, dims, n) in enumerate(parts):
            a_ref, b_ref = refs[2 * p], refs[2 * p + 1]
            for k in range(a_ref.shape[0]):
                r = lax.dot_general(a_ref[k], b_ref[k], dims, preferred_element_type=F32)
                accs[n] = r if accs[n] is None else accs[n] + r
        if norm_bwd is None:
            acc = accs[0]
            if has_add:
                acc = acc + refs[2 * np_][...]
            refs[-1][...] = acc.astype(o_dtype)
            return
        x_ref, g_refs = refs[2 * np_], refs[2 * np_ + 1:2 * np_ + 1 + nn]
        n_in = 2 * np_ + 1 + nn + has_res + has_post
        dx_ref, dxb_ref, dg_refs = refs[n_in], refs[n_in + 1], refs[n_in + 2:n_in + 2 + nn]
        xv = x_ref[...]
        r = lax.rsqrt(jnp.mean(xv * xv, axis=-1, keepdims=True) + EPS)
        xn = xv * r
        dx = refs[2 * np_ + 1 + nn][...] if has_res else None
        sums = []
        for acc, g_ref in zip(accs, g_refs):
            gdy = acc * g_ref[...]
            t = r * (gdy - xn * jnp.mean(gdy * xn, axis=-1, keepdims=True))
            dx = t if dx is None else dx + t
            sums.append(jnp.sum(acc * xn, axis=0, keepdims=True))
        dx_ref[...] = dx
        dxb = dx.astype(BF16)
        dxb_ref[...] = dxb
        if has_post:
            refs[n_in + 2 + nn][...] = lax.dot_general(dxb, refs[n_in - 1][...], post[1],
                                                       preferred_element_type=F32).astype(BF16)

        @pl.when(pl.program_id(0) == 0)
        def _():
            for dg_ref, part in zip(dg_refs, sums):
                dg_ref[...] = part

        @pl.when(pl.program_id(0) > 0)
        def _():
            for dg_ref, part in zip(dg_refs, sums):
                dg_ref[...] += part

    in_specs, args = [], []
    for a, a_spec, b, b_spec, _, _ in parts:
        in_specs += [a_spec, b_spec]
        args += [a, b]
    if norm_bwd is None:
        if has_add:
            in_specs.append(o_spec)
            args.append(add)
        return _pallas(body, name=name, grid=grid, in_specs=in_specs, out_specs=o_spec,
                       out_shape=jax.ShapeDtypeStruct(o_shape, o_dtype),
                       params=_params(("parallel",) * len(grid)))(*args)
    x, gains, dres = norm_bwd
    vec = pl.BlockSpec((1, o_shape[1]), lambda i: (0, 0))
    in_specs += [o_spec] + [vec] * nn + ([o_spec] if has_res else [])
    args += [x] + list(gains) + ([dres] if has_res else [])
    out_specs = [o_spec, o_spec] + [vec] * nn
    out_shape = ([jax.ShapeDtypeStruct(o_shape, F32), jax.ShapeDtypeStruct(o_shape, BF16)]
                 + [jax.ShapeDtypeStruct((1, o_shape[1]), F32)] * nn)
    if has_post:
        in_specs.append(pl.BlockSpec(post[0].shape, lambda i: (0, 0)))
        args.append(post[0])
        out_specs.append(pl.BlockSpec((o_spec.block_shape[0], post[2]), lambda i: (i, 0)))
        out_shape.append(jax.ShapeDtypeStruct((o_shape[0], post[2]), BF16))
    outs = _pallas(body, name=name, grid=grid, in_specs=in_specs, out_specs=out_specs, out_shape=out_shape,
                   params=_params(("arbitrary",)))(*args)
    if has_post:
        return outs[0], outs[1], list(outs[2:2 + nn]), outs[2 + nn]
    return outs[0], outs[1], list(outs[2:])


def _mm_rows(name, a, b, dims, o_dtype, n_out, *, tn=None, add=None):
    k = a.shape[1]
    tn = n_out if tn is None else tn
    if dims == NN:
        b_spec = pl.BlockSpec((k, tn), lambda n, i: (0, n))
    else:
        b_spec = pl.BlockSpec((tn, k), lambda n, i: (n, 0))
    return _mm(name, a, b, grid=(n_out // tn, T // TM),
               a_spec=pl.BlockSpec((TM, k), lambda n, i: (i, 0)), b_spec=b_spec,
               o_spec=pl.BlockSpec((TM, tn), lambda n, i: (i, n)), o_shape=(T, n_out), o_dtype=o_dtype,
               dims=dims, add=add)


def _wgrads(name, jobs):
    arrays, index = [], {}
    for a, b in jobs:
        for arr in (a, b):
            if id(arr) not in index:
                index[id(arr)] = len(arrays)
                arrays.append(arr)
    n_in = len(arrays)

    def body(*refs):
        for t, (a, b) in enumerate(jobs):
            a_ref, b_ref, o_ref = refs[index[id(a)]], refs[index[id(b)]], refs[n_in + t]
            if a.ndim == 3:
                for h in range(a.shape[0]):
                    o_ref[h] = lax.dot_general(a_ref[h], b_ref[...], TN, preferred_element_type=F32).astype(BF16)
            else:
                o_ref[...] = lax.dot_general(a_ref[...], b_ref[...], TN, preferred_element_type=F32).astype(BF16)

    out_shape = [jax.ShapeDtypeStruct(a.shape[:-2] + (a.shape[-1], b.shape[-1]), BF16) for a, b in jobs]
    return _pallas(body, name=name, in_specs=[VMEM_SPEC] * n_in, out_specs=[VMEM_SPEC] * len(jobs), out_shape=out_shape,
                   params=pltpu.CompilerParams(vmem_limit_bytes=VMEM_LIMIT))(*arrays)


def _mm_wgrad(name, a, b, *, tn=512):
    k, n = a.shape[1], b.shape[1]
    tn = min(tn, n)
    return _mm(name, a, b, grid=(n // tn,),
               a_spec=pl.BlockSpec((T, k), lambda j: (0, 0)), b_spec=pl.BlockSpec((T, tn), lambda j: (0, j)),
               o_spec=pl.BlockSpec((k, tn), lambda j: (0, j)), o_shape=(k, n), o_dtype=BF16, dims=TN)


def _rms_fwd(name, x, g):
    d = x.shape[1]

    def body(x_ref, g_ref, o_ref):
        xv = x_ref[...]
        r = lax.rsqrt(jnp.mean(xv * xv, axis=-1, keepdims=True) + EPS)
        o_ref[...] = ((xv * r) * g_ref[...]).astype(BF16)

    return _pallas(
        body, name=name, grid=(T // TM,),
        in_specs=[pl.BlockSpec((TM, d), lambda i: (i, 0)), pl.BlockSpec((1, d), lambda i: (0, 0))],
        out_specs=pl.BlockSpec((TM, d), lambda i: (i, 0)),
        out_shape=jax.ShapeDtypeStruct((T, d), BF16), params=_params(("parallel",)))(x, g)


def _rms(xv, g):
    return (xv * lax.rsqrt(jnp.mean(xv * xv, axis=-1, keepdims=True) + EPS)) * g


def _out_norm(name, a, w, add, g):
    def body(a_ref, w_ref, add_ref, g_ref, h_ref, hn_ref):
        hv = lax.dot_general(a_ref[...], w_ref[...], NN, preferred_element_type=F32) + add_ref[...]
        h_ref[...] = hv
        hn_ref[...] = _rms(hv, g_ref[...]).astype(BF16)

    rows = pl.BlockSpec((TS, D), lambda i: (i, 0))
    return _pallas(
        body, name=name, grid=(T // TS,),
        in_specs=[pl.BlockSpec((TS, a.shape[1]), lambda i: (i, 0)), pl.BlockSpec(w.shape, lambda i: (0, 0)), rows,
                  pl.BlockSpec((1, D), lambda i: (0, 0))],
        out_specs=[rows, rows], out_shape=[jax.ShapeDtypeStruct((T, D), F32), jax.ShapeDtypeStruct((T, D), BF16)],
        params=_params(("parallel",)))(a, w, add, g)


def _rms_bwd(name, x, gains, dys, dres=None):
    d = x.shape[1]
    n = len(gains)
    has_res = dres is not None

    def body(*refs):
        x_ref, g_refs, dy_refs = refs[0], refs[1:1 + n], refs[1 + n:1 + 2 * n]
        dx_ref, dxb_ref = refs[-2 - n], refs[-1 - n]
        dg_refs = refs[-n:]
        xv = x_ref[...]
        r = lax.rsqrt(jnp.mean(xv * xv, axis=-1, keepdims=True) + EPS)
        xn = xv * r
        dx = refs[1 + 2 * n][...] if has_res else None
        parts = []
        for g_ref, dy_ref in zip(g_refs, dy_refs):
            dyv = dy_ref[...].astype(F32)
            gdy = dyv * g_ref[...]
            t = r * (gdy - xn * jnp.mean(gdy * xn, axis=-1, keepdims=True))
            dx = t if dx is None else dx + t
            parts.append(jnp.sum(dyv * xn, axis=0, keepdims=True))
        dx_ref[...] = dx
        dxb_ref[...] = dx.astype(BF16)

        @pl.when(pl.program_id(0) == 0)
        def _():
            for dg_ref, part in zip(dg_refs, parts):
                dg_ref[...] = part

        @pl.when(pl.program_id(0) > 0)
        def _():
            for dg_ref, part in zip(dg_refs, parts):
                dg_ref[...] += part

    row = pl.BlockSpec((TR, d), lambda i: (i, 0))
    vec = pl.BlockSpec((1, d), lambda i: (0, 0))
    args = [x] + list(gains) + list(dys) + ([dres] if has_res else [])
    in_specs = [row] + [vec] * n + [row] * n + ([row] if has_res else [])
    outs = _pallas(
        body, name=name, grid=(T // TR,), in_specs=in_specs, out_specs=[row, row] + [vec] * n,
        out_shape=[jax.ShapeDtypeStruct((T, d), F32), jax.ShapeDtypeStruct((T, d), BF16)]
        + [jax.ShapeDtypeStruct((1, d), F32)] * n,
        params=_params(("arbitrary",)))(*args)
    return outs[0], outs[1], list(outs[2:])


def _down_final(act, w_down4, h_in, g, tgt):
    def body(a_ref, w_ref, hin_ref, g_ref, t_ref, loss_ref, dh_ref, dhb_ref, dg_ref):
        hv = lax.dot_general(a_ref[0], w_ref[0], NN, preferred_element_type=F32)
        for j in range(1, N_FF_BLK):
            hv = hv + lax.dot_general(a_ref[j], w_ref[j], NN, preferred_element_type=F32)
        hv = hv + hin_ref[...]
        r = lax.rsqrt(jnp.mean(hv * hv, axis=-1, keepdims=True) + EPS)
        xn = hv * r
        gv = g_ref[...]
        err = xn * gv - t_ref[...]
        part_loss = 0.5 * jnp.sum(jnp.mean(err * err, axis=-1, keepdims=True), axis=0, keepdims=True)
        dy = err * (1.0 / D)
        gdy = dy * gv
        dh = r * (gdy - xn * jnp.mean(gdy * xn, axis=-1, keepdims=True))
        dh_ref[...] = dh
        dhb_ref[...] = dh.astype(BF16)
        part = jnp.sum(dy * xn, axis=0, keepdims=True)
        first = pl.program_id(0) == 0

        @pl.when(first)
        def _():
            dg_ref[...] = part
            loss_ref[...] = jnp.broadcast_to(part_loss, (1, 128))

        @pl.when(jnp.logical_not(first))
        def _():
            dg_ref[...] += part
            loss_ref[...] += jnp.broadcast_to(part_loss, (1, 128))

    row = pl.BlockSpec((TS, D), lambda i: (i, 0))
    vec = pl.BlockSpec((1, D), lambda i: (0, 0))
    return _pallas(
        body, name="f1_down_loss", grid=(T // TS,),
        in_specs=[pl.BlockSpec((N_FF_BLK, TS, FF_BLK), lambda i: (0, i, 0)),
                  pl.BlockSpec((None, N_FF_BLK, FF_BLK, D), lambda i: (0, 0, 0, 0)), row, vec, row],
        out_specs=[pl.BlockSpec((1, 128), lambda i: (0, 0)), row, row, vec],
        out_shape=[jax.ShapeDtypeStruct((1, 128), F32), jax.ShapeDtypeStruct((T, D), F32),
                   jax.ShapeDtypeStruct((T, D), BF16), jax.ShapeDtypeStruct((1, D), F32)],
        params=_params(("arbitrary",)))(act, w_down4, h_in, g, tgt)


def _prev_idx(i, rows=TR):
    return jnp.maximum(i * (rows // HALO) - 1, 0)


def _next_idx(i, rows=TR):
    return jnp.minimum((i + 1) * (rows // HALO), T // HALO - 1)


def _causal_taps(ext):
    return pltpu.roll(ext, 2, 0)[HALO:], pltpu.roll(ext, 1, 0)[HALO:], ext[HALO:]


def _anticausal_taps(ext, n):
    rows = ext.shape[0]
    return pltpu.roll(ext, rows - 1, 0)[:n], pltpu.roll(ext, rows - 2, 0)[:n]


MIX_COLS = 512


def _mixer_in(hn, w_in, w):
    nc = D // MIX_COLS

    def body(h_ref, hh_ref, wb_ref, wc_ref, wu_ref, w_ref, b_ref, c_ref, u_ref, y_ref):
        i = pl.program_id(1)
        hv = h_ref[...]
        he = jnp.concatenate([hh_ref[...], hv], axis=0)
        ce = lax.dot_general(he, wc_ref[...], NN, preferred_element_type=F32).astype(BF16)
        ue = lax.dot_general(he, wu_ref[...], NN, preferred_element_type=F32).astype(BF16)
        bv = lax.dot_general(hv, wb_ref[...], NN, preferred_element_type=F32).astype(BF16)
        b_ref[...] = bv
        c_ref[...] = ce[HALO:]
        u_ref[...] = ue[HALO:]
        row = lax.broadcasted_iota(jnp.int32, (HALO + TS, 1), 0)
        cu = jnp.where(jnp.logical_or(i > 0, row >= HALO), ce.astype(F32) * ue.astype(F32), 0.0)
        x2, x1, x0 = _causal_taps(cu)
        wv = w_ref[...]
        cv = (x2 * wv[0:1] + x1 * wv[1:2]) + x0 * wv[2:3]
        y_ref[...] = (bv.astype(F32) * cv).astype(BF16)

    def cols(part):
        return pl.BlockSpec((D, MIX_COLS), lambda j, i: (0, part * nc + j))

    blk = pl.BlockSpec((TS, MIX_COLS), lambda j, i: (i, j))
    out = jax.ShapeDtypeStruct((T, D), BF16)
    return _pallas(
        body, name="l0_in", grid=(nc, T // TS),
        in_specs=[pl.BlockSpec((TS, D), lambda j, i: (i, 0)), pl.BlockSpec((HALO, D), lambda j, i: (_prev_idx(i, TS), 0)),
                  cols(0), cols(1), cols(2), pl.BlockSpec((3, MIX_COLS), lambda j, i: (0, j))],
        out_specs=[blk] * 4, out_shape=[out] * 4,
        params=_params(("parallel", "parallel")))(hn, hn, w_in, w_in, w_in, w)


def _mixer_out_bwd(dh, w_out, zb, zc, zu, w):
    last = T // TR - 1

    def body(dh_ref, dhn_ref, wo_ref, b_ref, bn_ref, c_ref, ch_ref, u_ref, uh_ref, w_ref, dz_ref, dw_ref):
        i = pl.program_id(0)
        dye = lax.dot_general(jnp.concatenate([dh_ref[...], dhn_ref[...]], axis=0), wo_ref[...], NT,
                              preferred_element_type=F32)
        cv_ = c_ref[...].astype(F32)
        uv = u_ref[...].astype(F32)
        cu = cv_ * uv
        cuh = jnp.where(i > 0, ch_ref[...].astype(F32) * uh_ref[...].astype(F32), 0.0)
        x2, x1, x0 = _causal_taps(jnp.concatenate([cuh, cu], axis=0))
        wv = w_ref[...]
        conv = (x2 * wv[0:1] + x1 * wv[1:2]) + x0 * wv[2:3]
        dyv = dye[:TR]
        dz_ref[:, 0:D] = (dyv * conv).astype(BF16)
        dconv = dyv * b_ref[...].astype(F32)
        dconv_n = jnp.where(i < last, dye[TR:] * bn_ref[...].astype(F32), 0.0)
        n1, n2 = _anticausal_taps(jnp.concatenate([dconv, dconv_n], axis=0), TR)
        dcu = (dconv * wv[2:3] + n1 * wv[1:2]) + n2 * wv[0:1]
        dz_ref[:, D:2 * D] = (dcu * uv).astype(BF16)
        dz_ref[:, 2 * D:3 * D] = (dcu * cv_).astype(BF16)
        part = jnp.concatenate([jnp.sum(dconv * x2, axis=0, keepdims=True),
                                jnp.sum(dconv * x1, axis=0, keepdims=True),
                                jnp.sum(dconv * x0, axis=0, keepdims=True)], axis=0)

        @pl.when(i == 0)
        def _():
            dw_ref[...] = part

        @pl.when(i > 0)
        def _():
            dw_ref[...] += part

    main = pl.BlockSpec((TR, D), lambda i: (i, 0))
    prev = pl.BlockSpec((HALO, D), lambda i: (_prev_idx(i), 0))
    nxt = pl.BlockSpec((HALO, D), lambda i: (_next_idx(i), 0))
    wspec = pl.BlockSpec((3, D), lambda i: (0, 0))
    return _pallas(
        body, name="d_l0_out", grid=(T // TR,),
        in_specs=[main, nxt, pl.BlockSpec((D, D), lambda i: (0, 0)), main, nxt, main, prev, main, prev, wspec],
        out_specs=[pl.BlockSpec((TR, 3 * D), lambda i: (i, 0)), wspec],
        out_shape=[jax.ShapeDtypeStruct((T, 3 * D), BF16), jax.ShapeDtypeStruct((3, D), F32)],
        params=_params(("arbitrary",)))(dh, dh, w_out, zb, zb, zc, zc, zu, zu, w)


def _sigmoid(x):
    return 0.5 * jnp.tanh(0.5 * x) + 0.5


def _ffn_up_act(name, hf, w_up, w, b):
    def body(h_ref, hh_ref, wg_ref, wv_ref, w_ref, b_ref, g_ref, v_ref, a_ref):
        i = pl.program_id(1)
        hv = h_ref[...]
        ge = lax.dot_general(jnp.concatenate([hh_ref[...], hv], axis=0), wg_ref[...], NT,
                             preferred_element_type=F32).astype(BF16)
        v = lax.dot_general(hv, wv_ref[...], NT, preferred_element_type=F32).astype(BF16)
        g_ref[...] = ge[HALO:]
        v_ref[...] = v
        ext = ge.astype(F32)
        row = lax.broadcasted_iota(jnp.int32, (HALO + TM, 1), 0)
        ext = jnp.where(jnp.logical_or(i > 0, row >= HALO), ext, 0.0)
        x2, x1, x0 = _causal_taps(ext)
        wv = w_ref[...]
        gc = ((x2 * wv[0:1] + x1 * wv[1:2]) + x0 * wv[2:3]) + b_ref[...]
        a_ref[...] = ((gc * _sigmoid(gc)) * v.astype(F32)).astype(BF16)

    blk = pl.BlockSpec((None, TM, FF_BLK), lambda j, i: (j, i, 0))
    out = jax.ShapeDtypeStruct((N_FF_BLK, T, FF_BLK), BF16)
    return _pallas(
        body, name=name, grid=(N_FF_BLK, T // TM),
        in_specs=[pl.BlockSpec((TM, D), lambda j, i: (i, 0)),
                  pl.BlockSpec((HALO, D), lambda j, i: (_prev_idx(i, TM), 0)),
                  pl.BlockSpec((None, None, FF_BLK, D), lambda j, i: (0, j, 0, 0)),
                  pl.BlockSpec((None, None, FF_BLK, D), lambda j, i: (0, j + N_FF_BLK, 0, 0)),
                  pl.BlockSpec((None, 3, FF_BLK), lambda j, i: (j, 0, 0)),
                  pl.BlockSpec((None, 1, FF_BLK), lambda j, i: (j, 0, 0))],
        out_specs=[blk, blk, blk], out_shape=[out, out, out],
        params=_params(("parallel", "parallel")))(hf, hf, w_up, w_up, w, b)


def _ffn_dact(name, dh, w_down4, g, v, w, b):
    last = T // TS - 1

    def body(dh_ref, dhn_ref, wd_ref, g_ref, gp_ref, gn_ref, v_ref, vn_ref, w_ref, b_ref, dg_ref, dv_ref, dw_ref, db_ref):
        i = pl.program_id(1)
        da = lax.dot_general(jnp.concatenate([dh_ref[...], dhn_ref[...]], axis=0), wd_ref[...], NT,
                             preferred_element_type=F32)
        row = lax.broadcasted_iota(jnp.int32, (TS + HALO, 1), 0)
        da = jnp.where(jnp.logical_or(i < last, row < TS), da, 0.0)
        gp = jnp.where(i > 0, gp_ref[...].astype(F32), 0.0)
        ext = jnp.concatenate([gp, g_ref[...].astype(F32), gn_ref[...].astype(F32)], axis=0)
        x2, x1, x0 = _causal_taps(ext)
        wv = w_ref[...]
        gc = ((x2 * wv[0:1] + x1 * wv[1:2]) + x0 * wv[2:3]) + b_ref[...]
        sg = _sigmoid(gc)
        vv = jnp.concatenate([v_ref[...].astype(F32), vn_ref[...].astype(F32)], axis=0)
        dv_ref[...] = (da[:TS] * (gc[:TS] * sg[:TS])).astype(BF16)
        dgc = (da * vv) * (sg * (1.0 + gc * (1.0 - sg)))
        n1, n2 = _anticausal_taps(dgc, TS)
        d0 = dgc[:TS]
        dg_ref[...] = ((d0 * wv[2:3] + n1 * wv[1:2]) + n2 * wv[0:1]).astype(BF16)
        part_w = jnp.concatenate([jnp.sum(d0 * x2[:TS], axis=0, keepdims=True),
                                  jnp.sum(d0 * x1[:TS], axis=0, keepdims=True),
                                  jnp.sum(d0 * x0[:TS], axis=0, keepdims=True)], axis=0)
        part_b = jnp.sum(d0, axis=0, keepdims=True)

        @pl.when(i == 0)
        def _():
            dw_ref[...] = part_w
            db_ref[...] = part_b

        @pl.when(i > 0)
        def _():
            dw_ref[...] += part_w
            db_ref[...] += part_b

    blk = pl.BlockSpec((None, TS, FF_BLK), lambda j, i: (j, i, 0))
    prev = pl.BlockSpec((None, HALO, FF_BLK), lambda j, i: (j, _prev_idx(i, TS), 0))
    nxt = pl.BlockSpec((None, HALO, FF_BLK), lambda j, i: (j, _next_idx(i, TS), 0))
    wspec = pl.BlockSpec((None, 3, FF_BLK), lambda j, i: (j, 0, 0))
    bspec = pl.BlockSpec((None, 1, FF_BLK), lambda j, i: (j, 0, 0))
    return _pallas(
        body, name=name, grid=(N_FF_BLK, T // TS),
        in_specs=[pl.BlockSpec((TS, D), lambda j, i: (i, 0)),
                  pl.BlockSpec((HALO, D), lambda j, i: (_next_idx(i, TS), 0)),
                  pl.BlockSpec((None, None, FF_BLK, D), lambda j, i: (0, j, 0, 0)),
                  blk, prev, nxt, blk, nxt, wspec, bspec],
        out_specs=[blk, blk, wspec, bspec],
        out_shape=[jax.ShapeDtypeStruct((N_FF_BLK, T, FF_BLK), BF16), jax.ShapeDtypeStruct((N_FF_BLK, T, FF_BLK), BF16),
                   jax.ShapeDtypeStruct((N_FF_BLK, 3, FF_BLK), F32), jax.ShapeDtypeStruct((N_FF_BLK, 1, FF_BLK), F32)],
        params=_params(("parallel", "arbitrary")))(dh, dh, w_down4, g, g, g, v, v, w, b)


def _rope_tables(pos, inv_freq):
    half = QK_ROPE // 2

    def body(p_ref, f_ref, c_ref, sa_ref, sb_ref):
        ang = p_ref[...].astype(F32) * f_ref[...]
        lane = lax.broadcasted_iota(jnp.int32, (T, 128), 1)
        c = jnp.cos(ang)
        s = jnp.sin(ang)
        c_ref[...] = jnp.where(lane < 2 * half, c, 0.0)
        sa_ref[...] = jnp.where(lane < half, -s, 0.0)
        sb_ref[...] = jnp.where(jnp.logical_and(lane >= half, lane < 2 * half), s, 0.0)

    return _pallas(
        body, name="rope_tables", in_specs=[VMEM_SPEC] * 2, out_specs=[VMEM_SPEC] * 3,
        out_shape=[jax.ShapeDtypeStruct((T, 128), F32)] * 3,
        params=pltpu.CompilerParams(vmem_limit_bytes=VMEM_LIMIT))(pos, inv_freq)


def _rotate(r, c, sa, sb, sign):
    return r * c + sign * (pltpu.roll(r, 96, 1) * sa + pltpu.roll(r, 32, 1) * sb)


def _attn_pre(h2, g_kv, g_l1, g_kvl, g_ql, w_dkv, w_kr, w_uk, w_uv, w_dq, w_uq, tables):
    def body(h_ref, c_ref, sa_ref, sb_ref, gkv_ref, gl1_ref, gkvl_ref, gql_ref, wdkv_ref, wkr_ref, wuk_ref, wuv_ref,
             wdq_ref, wuq_ref, hk_ref, hn_ref, ckvr_ref, ckv_ref, kr_ref, kn_ref, v_ref, cqr_ref, cq_ref, q_ref):
        xv = h_ref[...]
        xn = xv * lax.rsqrt(jnp.mean(xv * xv, axis=-1, keepdims=True) + EPS)
        hk = (xn * gkv_ref[...]).astype(BF16)
        hn = (xn * gl1_ref[...]).astype(BF16)
        hk_ref[...] = hk
        hn_ref[...] = hn
        cv, sav, sbv = c_ref[...], sa_ref[...], sb_ref[...]
        raw = lax.dot_general(hk, wdkv_ref[...], NN, preferred_element_type=F32)
        ckvr_ref[...] = raw
        ckv = _rms(raw, gkvl_ref[...]).astype(BF16)
        ckv_ref[...] = ckv
        kr = lax.dot_general(hk, wkr_ref[...], NT, preferred_element_type=F32)
        kr_ref[...] = _rotate(kr, cv, sav, sbv, 1.0).astype(BF16)
        kn_ref[...] = lax.dot_general(ckv, wuk_ref[...], NN, preferred_element_type=F32).astype(BF16)
        v_ref[...] = lax.dot_general(ckv, wuv_ref[...], NN, preferred_element_type=F32).astype(BF16)
        cqr = lax.dot_general(hn, wdq_ref[...], NN, preferred_element_type=F32)
        cqr_ref[...] = cqr
        cq = _rms(cqr, gql_ref[...]).astype(BF16)
        cq_ref[...] = cq
        for h in range(N_HEADS):
            r = lax.dot_general(cq, wuq_ref[h], NT, preferred_element_type=F32)
            q_ref[h, :, :QK_NOPE] = (r[:, :QK_NOPE] * SCALE2).astype(BF16)
            q_ref[h, :, QK_NOPE:] = (_rotate(r[:, QK_NOPE:], cv, sav, sbv, 1.0) * SCALE2).astype(BF16)

    def rows(d):
        return pl.BlockSpec((TS, d), lambda i: (i, 0))

    def whole(a):
        return pl.BlockSpec(a.shape, lambda i: (0,) * a.ndim)

    wholes = [g_kv, g_l1, g_kvl, g_ql, w_dkv, w_kr, w_uk, w_uv, w_dq, w_uq]
    outs = [(D, BF16), (D, BF16), (KV_LORA, F32), (KV_LORA, BF16), (128, BF16), (N_HEADS * QK_NOPE, BF16),
            (N_HEADS * V_HEAD, BF16), (Q_LORA, F32), (Q_LORA, BF16)]
    return _pallas(
        body, name="attn_pre", grid=(T // TS,),
        in_specs=[rows(D), rows(128), rows(128), rows(128)] + [whole(a) for a in wholes],
        out_specs=[rows(d) for d, _ in outs] + [pl.BlockSpec((N_HEADS, TS, QK_PAD), lambda i: (0, i, 0))],
        out_shape=[jax.ShapeDtypeStruct((T, d), dt) for d, dt in outs]
        + [jax.ShapeDtypeStruct((N_HEADS, T, QK_PAD), BF16)],
        params=_params(("parallel",)))(h2, *tables, *wholes)


SCALE = (QK_NOPE + QK_ROPE) ** -0.5
LOG2E = 1.4426950408889634
SCALE2 = SCALE * LOG2E


def _diag_mask(transposed):
    shift = CHUNK.bit_length() - 1
    a = lax.broadcasted_iota(jnp.int32, (TQ, TQ), 0) >> shift
    b = lax.broadcasted_iota(jnp.int32, (TQ, TQ), 1) >> shift
    return (a <= b) if transposed else (b <= a)


def _as_row(col):
    return jnp.transpose(jnp.broadcast_to(col, (col.shape[0], 128)), (1, 0))[0:1]


def _keys(kn_ref, kr_ref, off):
    return jnp.concatenate([kn_ref[pl.ds(off, TQ), :], kr_ref[pl.ds(off, TQ), :]], axis=1)


def _attn_fwd(q, kn, kr, v):
    hp = 2

    def body(q_ref, kn_ref, kr_ref, v_ref, o_ref, lse_ref):
        i = pl.program_id(1)
        qs = [q_ref[a] for a in range(hp)]

        def step(j, carry, masked):
            off = pl.multiple_of(j * TQ, TQ)
            krv = kr_ref[pl.ds(off, TQ), :]
            ss = []
            for a in range(hp):
                kk = jnp.concatenate([kn_ref[pl.ds(off, TQ), a * QK_NOPE:(a + 1) * QK_NOPE], krv], axis=1)
                ss.append(lax.dot_general(qs[a], kk, NT, preferred_element_type=F32))
            out = []
            for a in range(hp):
                m, l, acc = carry[a]
                s = ss[a]
                if masked:
                    s = jnp.where(_diag_mask(False), s, NEG_INF)
                m_new = jnp.maximum(m, jnp.max(s, axis=-1, keepdims=True))
                p = jnp.exp2(s - m_new)
                alpha = jnp.exp2(m - m_new)
                l = alpha * l + jnp.sum(p, axis=-1, keepdims=True)
                pv = lax.dot_general(p.astype(BF16), v_ref[pl.ds(off, TQ), a * V_HEAD:(a + 1) * V_HEAD], NN,
                                     preferred_element_type=F32)
                out.append((m_new, l, alpha * acc + pv))
            return tuple(out)

        one = (jnp.full((TQ, 1), NEG_INF, F32), jnp.zeros((TQ, 1), F32), jnp.zeros((TQ, V_HEAD), F32))
        carry = lax.fori_loop(0, i, lambda j, cr: step(j, cr, False), (one,) * hp)
        carry = step(i, carry, True)
        for a, (m, l, acc) in enumerate(carry):
            o_ref[:, a * V_HEAD:(a + 1) * V_HEAD] = (acc / l).astype(BF16)
            lse_ref[a] = _as_row(m + jnp.log(l) * LOG2E)

    return _pallas(
        body, name="attn_fwd", grid=(N_HEADS // hp, T // TQ),
        in_specs=[pl.BlockSpec((hp, TQ, QK_PAD), lambda h, i: (h, i, 0)),
                  pl.BlockSpec((T, hp * QK_NOPE), lambda h, i: (0, h)),
                  pl.BlockSpec((T, 128), lambda h, i: (0, 0)),
                  pl.BlockSpec((T, hp * V_HEAD), lambda h, i: (0, h))],
        out_specs=[pl.BlockSpec((TQ, hp * V_HEAD), lambda h, i: (i, h)), pl.BlockSpec((hp, 1, TQ), lambda h, i: (h, 0, i))],
        out_shape=[jax.ShapeDtypeStruct((T, N_HEADS * V_HEAD), BF16), jax.ShapeDtypeStruct((N_HEADS, 1, T), F32)],
        params=_params(("parallel", "parallel")))(q, kn, kr, v)


def _attn_bwd(q, kn, kr, v, o, do, lse_row, tables):
    nq = T // TQ
    hp = 2
    cos, sa, sb = tables

    def body(q_ref, kn_ref, kr_ref, v_ref, o_ref, do_ref, lse_ref, c_ref, sa_ref, sb_ref,
             dq_ref, dkn_ref, dkr_ref, dv_ref, dq_acc, dl_ref):
        j = pl.program_id(1)

        def cols(a):
            return slice(a * 128, (a + 1) * 128)

        @pl.when(j == 0)
        def _():
            dq_acc[...] = jnp.zeros_like(dq_acc)
            for a in range(hp):
                for i in range(nq):
                    rows = pl.ds(i * TQ, TQ)
                    prod = do_ref[rows, cols(a)].astype(F32) * o_ref[rows, cols(a)].astype(F32)
                    dl_ref[a, :, rows] = _as_row(jnp.sum(prod, axis=-1, keepdims=True))

        krv = kr_ref[...]
        kks = [jnp.concatenate([kn_ref[:, cols(a)], krv], axis=1) for a in range(hp)]
        vvs = [v_ref[:, cols(a)] for a in range(hp)]

        def step(i, carry, masked):
            off = pl.multiple_of(i * TQ, TQ)
            rows = pl.ds(off, TQ)
            qis = [q_ref[a, rows, :] for a in range(hp)]
            dois = [do_ref[rows, cols(a)] for a in range(hp)]
            sts = [lax.dot_general(kks[a], qis[a], NT, preferred_element_type=F32) for a in range(hp)]
            dpts = [lax.dot_general(vvs[a], dois[a], NT, preferred_element_type=F32) for a in range(hp)]
            out = []
            for a in range(hp):
                dk, dv = carry[a]
                st = sts[a]
                if masked:
                    st = jnp.where(_diag_mask(True), st, NEG_INF)
                pt = jnp.exp2(st - lse_ref[a, :, rows])
                dv = dv + lax.dot_general(pt.astype(BF16), dois[a], NN, preferred_element_type=F32)
                dst = (pt * (dpts[a] - dl_ref[a, :, rows])).astype(BF16)
                dk = dk + lax.dot_general(dst, qis[a], NN, preferred_element_type=F32)
                dq_acc[a, rows, :] += lax.dot_general(dst, kks[a], TN, preferred_element_type=F32)
                out.append((dk, dv))
            return tuple(out)

        zero = (jnp.zeros((TQ, QK_PAD), F32), jnp.zeros((TQ, V_HEAD), F32))
        carry = step(j, (zero,) * hp, True)
        carry = lax.fori_loop(j + 1, nq, lambda i, cr: step(i, cr, False), carry)
        for a, (dk, dv) in enumerate(carry):
            dk = dk * (SCALE / SCALE2)
            dkn_ref[:, cols(a)] = dk[:, :QK_NOPE].astype(BF16)
            dkr_ref[a] = dk[:, QK_NOPE:]
            dv_ref[:, cols(a)] = dv.astype(BF16)

        @pl.when(j == nq - 1)
        def _():
            for a in range(hp):
                dq = dq_acc[a] * SCALE
                dq_ref[a, :, :QK_NOPE] = dq[:, :QK_NOPE].astype(BF16)
                dq_ref[a, :, QK_NOPE:] = _rotate(dq[:, QK_NOPE:], c_ref[...], sa_ref[...], sb_ref[...], -1.0).astype(BF16)

    row = pl.BlockSpec((hp, 1, T), lambda h, j: (h, 0, 0))
    head = pl.BlockSpec((TQ, hp * 128), lambda h, j: (j, h))
    whole = pl.BlockSpec((hp, T, QK_PAD), lambda h, j: (h, 0, 0))
    tab = pl.BlockSpec((T, 128), lambda h, j: (0, 0))
    heads = pl.BlockSpec((T, hp * V_HEAD), lambda h, j: (0, h))
    return _pallas(
        body, name="attn_bwd", grid=(N_HEADS // hp, nq),
        in_specs=[whole, head, pl.BlockSpec((TQ, 128), lambda h, j: (j, 0)), head, heads, heads, row, tab, tab, tab],
        out_specs=[whole, head, pl.BlockSpec((hp, TQ, 128), lambda h, j: (h, j, 0)), head],
        out_shape=[jax.ShapeDtypeStruct((N_HEADS, T, QK_PAD), BF16), jax.ShapeDtypeStruct((T, N_HEADS * QK_NOPE), BF16),
                   jax.ShapeDtypeStruct((N_HEADS, T, 128), F32), jax.ShapeDtypeStruct((T, N_HEADS * V_HEAD), BF16)],
        scratch_shapes=[pltpu.VMEM((hp, T, QK_PAD), F32), pltpu.VMEM((hp, 1, T), F32)],
        params=_params(("parallel", "arbitrary")))(q, kn, kr, v, o, do, lse_row, cos, sa, sb)


def _rms_bwd_math(xv, g, dy):
    r = lax.rsqrt(jnp.mean(xv * xv, axis=-1, keepdims=True) + EPS)
    xn = xv * r
    gdy = dy * g
    return r * (gdy - xn * jnp.mean(gdy * xn, axis=-1, keepdims=True)), jnp.sum(dy * xn, axis=0, keepdims=True)


def _attn_post(dq, dkn, dv, dkr, cq_raw, ckv_raw, h2, dres, g_ql, g_kvl, g_l1, g_kv, w_uq, w_uk, w_uv, w_dq, w_dkv,
               w_kr, tables):
    def body(dq_ref, dkn_ref, dv_ref, dkr_ref, cqr_ref, ckvr_ref, h_ref, res_ref, c_ref, sa_ref, sb_ref,
             gql_ref, gkvl_ref, gl1_ref, gkv_ref, wuq_ref, wuk_ref, wuv_ref, wdq_ref, wdkv_ref, wkr_ref,
             dcq_ref, dckv_ref, dkrr_ref, dh_ref, dhb_ref, dgql_ref, dgkvl_ref, dgl1_ref, dgkv_ref):
        dcq = lax.dot_general(dq_ref[0], wuq_ref[0], NN, preferred_element_type=F32)
        for h in range(1, N_HEADS):
            dcq = dcq + lax.dot_general(dq_ref[h], wuq_ref[h], NN, preferred_element_type=F32)
        dcq_raw, s_ql = _rms_bwd_math(cqr_ref[...], gql_ref[...], dcq)
        dcq_raw = dcq_raw.astype(BF16)
        dcq_ref[...] = dcq_raw
        dckv = (lax.dot_general(dkn_ref[...], wuk_ref[...], NT, preferred_element_type=F32)
                + lax.dot_general(dv_ref[...], wuv_ref[...], NT, preferred_element_type=F32))
        dckv_raw, s_kvl = _rms_bwd_math(ckvr_ref[...], gkvl_ref[...], dckv)
        dckv_raw = dckv_raw.astype(BF16)
        dckv_ref[...] = dckv_raw
        dkr = dkr_ref[0]
        for h in range(1, N_HEADS):
            dkr = dkr + dkr_ref[h]
        dkr_raw = _rotate(dkr, c_ref[...], sa_ref[...], sb_ref[...], -1.0).astype(BF16)
        dkrr_ref[...] = dkr_raw
        d_hn = lax.dot_general(dcq_raw, wdq_ref[...], NT, preferred_element_type=F32)
        d_hk = (lax.dot_general(dckv_raw, wdkv_ref[...], NT, preferred_element_type=F32)
                + lax.dot_general(dkr_raw, wkr_ref[...], NN, preferred_element_type=F32))
        xv = h_ref[...]
        r = lax.rsqrt(jnp.mean(xv * xv, axis=-1, keepdims=True) + EPS)
        xn = xv * r
        dx = res_ref[...]
        sums = [s_ql, s_kvl]
        for dy, g_ref in ((d_hn, gl1_ref), (d_hk, gkv_ref)):
            gdy = dy * g_ref[...]
            dx = dx + r * (gdy - xn * jnp.mean(gdy * xn, axis=-1, keepdims=True))
            sums.append(jnp.sum(dy * xn, axis=0, keepdims=True))
        dh_ref[...] = dx
        dhb_ref[...] = dx.astype(BF16)
        dg_refs = (dgql_ref, dgkvl_ref, dgl1_ref, dgkv_ref)

        @pl.when(pl.program_id(0) == 0)
        def _():
            for dg_ref, part in zip(dg_refs, sums):
                dg_ref[...] = part

        @pl.when(pl.program_id(0) > 0)
        def _():
            for dg_ref, part in zip(dg_refs, sums):
                dg_ref[...] += part

    def rows(d):
        return pl.BlockSpec((TS, d), lambda i: (i, 0))

    def heads(d):
        return pl.BlockSpec((N_HEADS, TS, d), lambda i: (0, i, 0))

    def whole(a):
        return pl.BlockSpec(a.shape, lambda i: (0,) * a.ndim)

    wholes = [g_ql, g_kvl, g_l1, g_kv, w_uq, w_uk, w_uv, w_dq, w_dkv, w_kr]
    vecs = [Q_LORA, KV_LORA, D, D]
    return _pallas(
        body, name="attn_post", grid=(T // TS,),
        in_specs=[heads(QK_PAD), rows(N_HEADS * QK_NOPE), rows(N_HEADS * V_HEAD), heads(128), rows(Q_LORA),
                  rows(KV_LORA), rows(D), rows(D), rows(128), rows(128), rows(128)] + [whole(a) for a in wholes],
        out_specs=[rows(Q_LORA), rows(KV_LORA), rows(128), rows(D), rows(D)]
        + [pl.BlockSpec((1, d), lambda i: (0, 0)) for d in vecs],
        out_shape=[jax.ShapeDtypeStruct((T, Q_LORA), BF16), jax.ShapeDtypeStruct((T, KV_LORA), BF16),
                   jax.ShapeDtypeStruct((T, 128), BF16), jax.ShapeDtypeStruct((T, D), F32),
                   jax.ShapeDtypeStruct((T, D), BF16)] + [jax.ShapeDtypeStruct((1, d), F32) for d in vecs],
        params=_params(("arbitrary",)))(dq, dkn, dv, dkr, cq_raw, ckv_raw, h2, dres, *tables, *wholes)


def _ffn_gup(name, dg, dv, hf):
    def body(dg_ref, dv_ref, hf_ref, o_ref):
        j = pl.program_id(0)

        @pl.when(j < N_FF_BLK)
        def _():
            o_ref[...] = lax.dot_general(dg_ref[...], hf_ref[...], TN, preferred_element_type=F32).astype(BF16)

        @pl.when(j >= N_FF_BLK)
        def _():
            o_ref[...] = lax.dot_general(dv_ref[...], hf_ref[...], TN, preferred_element_type=F32).astype(BF16)

    return _pallas(
        body, name=name, grid=(N_DEV,),
        in_specs=[pl.BlockSpec((None, T, FF_BLK), lambda j: (jnp.minimum(j, N_FF_BLK - 1), 0, 0)),
                  pl.BlockSpec((None, T, FF_BLK), lambda j: (jnp.maximum(j - N_FF_BLK, 0), 0, 0)),
                  pl.BlockSpec((T, D), lambda j: (0, 0))],
        out_specs=pl.BlockSpec((None, FF_BLK, D), lambda j: (j, 0, 0)),
        out_shape=jax.ShapeDtypeStruct((N_DEV, FF_BLK, D), BF16), params=_params(("parallel",)))(dg, dv, hf)


def _ffn_layer_fwd(tag, h, hf, ex, final=None):
    g, v, act = _ffn_up_act(f"{tag}_up", hf, ex.need(f"ffn_w_up{tag[1]}", hf), ex.need(f"ffn_cw{tag[1]}", hf),
                            ex.need(f"ffn_cb{tag[1]}", hf))
    ex.at(f"{tag}_up", act)
    if final is not None:
        return _down_final(act, ex.need(f"ffn_w_down{tag[1]}", act), h, *final), (hf, g, v, act)
    rows = pl.BlockSpec((TS, D), lambda i: (i, 0))
    out = _mm_sum(f"{tag}_down",
                  [(act, pl.BlockSpec((N_FF_BLK, TS, FF_BLK), lambda i: (0, i, 0)), ex.need(f"ffn_w_down{tag[1]}", act),
                    pl.BlockSpec((None, N_FF_BLK, FF_BLK, D), lambda i: (0, 0, 0, 0)), NN, 0)],
                  grid=(T // TS,), o_spec=rows, o_shape=(T, D), o_dtype=F32, add=h)
    ex.at(f"{tag}_down", out)
    return out, (hf, g, v, act)


def _ffn_layer_bwd(tag, h, gain, ex, saved, dh, dh_bf, post=None):
    hf, g, v, act = saved
    layer = tag[1]
    w_up, w_down4 = ex.need(f"ffn_w_up{layer}", dh_bf), ex.need(f"ffn_w_down{layer}", dh_bf)
    dg, dv, dcw, dcb = _ffn_dact(f"{tag}_dact", dh_bf, w_down4, g, v, ex.need(f"ffn_cw{layer}", dh_bf),
                                 ex.need(f"ffn_cb{layer}", dh_bf))
    ex.at(f"{tag}_dact", dg)
    g_down = _mm(f"{tag}_gdown", act, dh_bf, grid=(N_FF_BLK,),
                 a_spec=pl.BlockSpec((None, T, FF_BLK), lambda j: (j, 0, 0)),
                 b_spec=pl.BlockSpec((T, D), lambda j: (0, 0)),
                 o_spec=pl.BlockSpec((FF_BLK, D), lambda j: (j, 0)),
                 o_shape=(D_FF, D), o_dtype=BF16, dims=TN)
    g_up = _ffn_gup(f"{tag}_gup", dg, dv, hf)
    ex.grad("ffn_w_up", int(layer), g_up.reshape(1, N_DEV, FF_BLK, D))
    ex.grad("ffn_w_down", int(layer), g_down.reshape(1, N_DEV, D_FF // N_DEV, D))
    ex.at(f"{tag}_gup", g_up)
    part = pl.BlockSpec((N_FF_BLK, TR, FF_BLK), lambda i: (0, i, 0))
    dh_in, dh_in_bf, dgain, *onward = _mm_sum(
        f"{tag}_dhf",
        [(dg, part, w_up, pl.BlockSpec((None, N_FF_BLK, FF_BLK, D), lambda i: (0, 0, 0, 0)), NN, 0),
         (dv, part, w_up, pl.BlockSpec((None, N_FF_BLK, FF_BLK, D), lambda i: (0, 1, 0, 0)), NN, 0)],
        grid=(T // TR,), o_spec=pl.BlockSpec((TR, D), lambda i: (i, 0)), o_shape=(T, D), o_dtype=F32,
        norm_bwd=(h, [gain], dh), post=post)
    ex.at(f"{tag}_dhf", dh_in)
    return (dh_in, dh_in_bf, dgain[0], dcw, dcb, *onward)


def _local_step(x, pos, tgt, rep, ex):
    attn_norm, ffn_norm, final_norm = rep["attn_norm"], rep["ffn_norm"], rep["final_norm"]
    half = QK_ROPE // 2
    inv = 1.0 / (ROPE_THETA ** (jnp.arange(half, dtype=F32) / half))
    inv_freq = jnp.concatenate([inv, inv, jnp.zeros((128 - 2 * half,), F32)]).reshape(1, 128)
    tables = _rope_tables(pos, inv_freq)

    hn0 = _rms_fwd("l0_norm", x, attn_norm[0:1])
    ex.at("l0_norm", hn0)
    w_in = ex.need("sc_w_in", hn0)
    zb, zc, zu, y = _mixer_in(hn0, w_in, ex.need("sc_conv_w", hn0))
    ex.at("l0_in", y)
    h1 = _mm_rows("l0_out", y, ex.need("sc_w_out", y), NN, F32, D, tn=512, add=x)
    ex.at("l0_out", h1)
    h2, ffn0 = _ffn_layer_fwd("f0", h1, _rms_fwd("f0_norm", h1, ffn_norm[0:1]), ex)

    w_uq = ex.need("w_uq", h2)
    hk, hn1, ckv_raw, ckv, kr, kn, vv, cq_raw, cq, q = _attn_pre(
        h2, rep["kv_in_norm"], attn_norm[1:2], rep["kv_latent_norm"], rep["q_latent_norm"], ex.need("w_dkv", h2),
        ex.need("w_kr", h2), ex.need("w_uk", h2), ex.need("w_uv", h2), ex.need("w_dq", h2), w_uq, tables)

    o, lse = _attn_fwd(q, kn, kr, vv)
    ex.at("attn_fwd", o)
    w_o = ex.need("w_o", o)
    h3, hf1 = _out_norm("attn_out", o, w_o, h2, ffn_norm[1:2])
    (loss, dh4, dh4_bf, d_final), ffn1 = _ffn_layer_fwd("f1", h3, hf1, ex, final=(final_norm.reshape(1, D), tgt))

    dh3, dh3_bf, d_fn1, dcw1, dcb1, do = _ffn_layer_bwd("f1", h3, ffn_norm[1:2], ex, ffn1, dh4, dh4_bf,
                                                        post=(w_o, NT, N_HEADS * V_HEAD))
    ex.at("f1_bwd", dh3)

    dq_pre, dkn, dkr, dvv = _attn_bwd(q, kn, kr, vv, o, do, lse, tables)

    dcq_raw_bf, dckv_raw_bf, dkr_raw_bf, dh2, dh2_bf, d_qln, d_kvln, d_an1, d_kvin = _attn_post(
        dq_pre, dkn, dvv, dkr, cq_raw, ckv_raw, h2, dh3, rep["q_latent_norm"], rep["kv_latent_norm"], attn_norm[1:2],
        rep["kv_in_norm"], w_uq, ex.need("w_uk", dkn), ex.need("w_uv", dvv), ex.need("w_dq", dq_pre),
        ex.need("w_dkv", dkn), ex.need("w_kr", dkr), tables)
    g_uq, g_dq, g_o = _wgrads("g_q", [(dq_pre, cq), (hn1, dcq_raw_bf), (o, dh3_bf)])
    ex.grad("w_uq", None, g_uq[:, :QK_NOPE + QK_ROPE].reshape(1, N_DEV, QK_NOPE + QK_ROPE, Q_LORA))
    ex.grad("w_dq", None, g_dq.reshape(1, N_DEV, D // N_DEV, Q_LORA))
    ex.grad("w_o", None, g_o.reshape(1, N_DEV, D // N_DEV, D))

    g_uk, g_uv, g_dkv, g_kr = _wgrads("g_kv", [(ckv, dkn), (ckv, dvv), (hk, dckv_raw_bf), (dkr_raw_bf, hk)])
    ex.grad("w_uk", None, g_uk)
    ex.grad("w_uv", None, g_uv)
    ex.grad("w_dkv", None, g_dkv.reshape(1, N_DEV, D // N_DEV, KV_LORA))
    ex.grad("w_kr", None, g_kr[:QK_ROPE])
    ex.at("kv_bwd", dh2)

    dh1, dh1_bf, d_fn0, dcw0, dcb0 = _ffn_layer_bwd("f0", h1, ffn_norm[0:1], ex, ffn0, dh2, dh2_bf)
    ex.at("f0_bwd", dh1)

    ex.grad("sc_w_out", None, _mm_wgrad("g_sc_w_out", y, dh1_bf).reshape(1, N_DEV, D // N_DEV, D))
    dz, d_scw = _mixer_out_bwd(dh1_bf, ex.need("sc_w_out", dh1_bf), zb, zc, zu, ex.need("sc_conv_w", dh1_bf))
    g_in = _mm_wgrad("g_sc_w_in", hn0, dz)
    ex.grad("sc_w_in", None, g_in)
    ex.at("sc_bwd", g_in)
    ex.at("d_l0_in", g_in)
    w_in = ex.need("sc_w_in", dz)
    grad_x, _, (d_an0,) = _mm_sum(
        "d_l0_in", [(dz[None], pl.BlockSpec((1, TS, dz.shape[1]), lambda i: (0, i, 0)),
                     w_in[None], pl.BlockSpec((1,) + w_in.shape, lambda i: (0, 0, 0)), NT, 0)],
        norm_bwd=(x, [attn_norm[0:1]], dh1),
        grid=(T // TS,), o_spec=pl.BlockSpec((TS, D), lambda i: (i, 0)), o_shape=(T, D), o_dtype=F32)

    small = {
        "attn_norm": jnp.concatenate([d_an0, d_an1], axis=0),
        "ffn_norm": jnp.concatenate([d_fn0, d_fn1], axis=0),
        "final_norm": d_final.reshape(D),
        "kv_in_norm": d_kvin.reshape(D),
        "kv_latent_norm": d_kvln.reshape(KV_LORA),
        "q_latent_norm": d_qln,
        "ffn_conv_b": jnp.stack([dcb0, dcb1]).transpose(0, 2, 1, 3).reshape(2, D_FF),
        "sc_conv_w": d_scw,
        "ffn_conv_w": jnp.stack([dcw0, dcw1]).transpose(0, 2, 1, 3).reshape(2, 3, D_FF),
    }
    return loss, grad_x, small


def _place():
    return lax.axis_index("x"), lax.axis_index("y"), lax.axis_index("c")


def _peers():
    x, y, c = _place()
    return (x, y, 1 - c), [(1 - x, y), (x, 1 - y), (1 - x, 1 - y)]


def _window(ref, kind, dev):
    if kind == "blocked":
        return ref.at[:, dev]
    width = ref.shape[-1] // N_DEV
    return ref.at[:, pl.ds(pl.multiple_of(dev * width, 128), width)]


HBM_SPEC = pl.BlockSpec(memory_space=pltpu.HBM)
SEM_SPEC = pl.BlockSpec(memory_space=pltpu.SEMAPHORE)
EFFECT = pltpu.SideEffectType.DATAFLOW_SIDE_EFFECTING
TOKEN = jax.ShapeDtypeStruct((8, 128), F32)


def _hbm(a):
    return pltpu.with_memory_space_constraint(a, pltpu.HBM)


def _copies_start(name, jobs):
    nj = len(jobs)
    counts = [(len(srcs), len(lands)) for srcs, lands, _, _ in jobs]
    n_arr = sum(ns + nl for ns, nl in counts)

    def body(*refs):
        sems, token = refs[n_arr:n_arr + 2 * nj], refs[-1]
        at = 0
        for j, ((ns, nl), (_, _, ncopy, plan)) in enumerate(zip(counts, jobs)):
            copies = plan(refs[at:at + ns], refs[at + ns:at + ns + nl])
            assert len(copies) == ncopy
            for k, (sent, dst, to, _) in enumerate(copies):
                pltpu.make_async_remote_copy(src_ref=sent, dst_ref=dst, send_sem=sems[2 * j].at[k],
                                             recv_sem=sems[2 * j + 1].at[k], device_id=to, device_id_type=MESH).start()
            at += ns + nl
        token[...] = jnp.zeros_like(token)

    arrays = [a for srcs, lands, _, _ in jobs for a in list(srcs) + list(lands)]
    sem_shapes = [pltpu.SemaphoreType.DMA((ncopy,)) for _, _, ncopy, _ in jobs for _ in range(2)]
    outs = pl.pallas_call(
        body, name=name, in_specs=[HBM_SPEC] * n_arr,
        out_specs=[SEM_SPEC] * (2 * nj) + [HBM_SPEC] * n_arr + [VMEM_SPEC],
        out_shape=sem_shapes + [pltpu.HBM(a.shape, a.dtype) for a in arrays] + [TOKEN],
        input_output_aliases={i: 2 * nj + i for i in range(n_arr)},
        compiler_params=pltpu.CompilerParams(has_side_effects=EFFECT))(*[_hbm(a) for a in arrays])
    _Chain.last = outs[-1]
    flights, at = [], 2 * nj
    for j, (ns, nl) in enumerate(counts):
        flights.append((outs[2 * j], outs[2 * j + 1], list(outs[at:at + ns]), list(outs[at + ns:at + ns + nl])))
        at += ns + nl
    return flights


def _copies_wait(name, started, ncopy, plan):
    send, recv, srcs, lands = started
    ns, nl = len(srcs), len(lands)

    def body(*refs):
        send_ref, recv_ref, token = refs[ns + nl], refs[ns + nl + 1], refs[-1]
        copies = plan(refs[:ns], refs[ns:ns + nl])
        assert len(copies) == ncopy
        for k, (sent, _, to, landed) in enumerate(copies):
            cp = pltpu.make_async_remote_copy(src_ref=sent, dst_ref=landed, send_sem=send_ref.at[k],
                                              recv_sem=recv_ref.at[k], device_id=to, device_id_type=MESH)
            cp.wait_send()
            cp.wait_recv()
        token[...] = jnp.zeros_like(token)

    arrays = list(srcs) + list(lands)
    outs = pl.pallas_call(
        body, name=name, in_specs=[HBM_SPEC] * (ns + nl) + [SEM_SPEC] * 2 + [ANY_SPEC],
        out_specs=[HBM_SPEC] * (ns + nl) + [VMEM_SPEC], out_shape=[pltpu.HBM(a.shape, a.dtype) for a in arrays] + [TOKEN],
        input_output_aliases={i: i for i in range(ns + nl)},
        compiler_params=pltpu.CompilerParams(has_side_effects=EFFECT))(*arrays, send, recv, _Chain.last)
    _Chain.last = outs[-1]
    return list(outs[:ns]), list(outs[ns:-1])


def _plan_gather_chips(kinds):
    def plan(srcs, lands):
        x, y, c = _place()
        sibling, chips = _peers()
        out = []
        for t, kind in enumerate(kinds):
            mine = _window(lands[t], kind, 4 * x + 2 * y + c)
            out.append((srcs[t], mine, (x, y, c), mine))
            out.append((srcs[t], mine, sibling, _window(lands[t], kind, 4 * x + 2 * y + 1 - c)))
            for px, py in chips:
                out.append((srcs[t], mine, (px, py, c), _window(lands[t], kind, 4 * px + 2 * py + c)))
        return out
    return plan, 5 * len(kinds)


def _plan_gather_all(n):
    def plan(srcs, lands):
        x, y, c = _place()
        out = []
        for t in range(n):
            mine = lands[t].at[:, 4 * x + 2 * y + c]
            for m in range(N_DEV):
                px, py, pc = (1 - x if m & 4 else x), (1 - y if m & 2 else y), (1 - c if m & 1 else c)
                out.append((srcs[t], mine, (px, py, pc), lands[t].at[:, 4 * px + 2 * py + pc]))
        return out
    return plan, N_DEV * n


def _plan_gather_sibling(kinds):
    def plan(srcs, lands):
        _, _, c = _place()
        sibling, chips = _peers()
        out = []
        for t, kind in enumerate(kinds):
            for px, py in chips:
                w = _window(lands[t], kind, 4 * px + 2 * py + c)
                out.append((w, w, sibling, _window(lands[t], kind, 4 * px + 2 * py + 1 - c)))
        return out
    return plan, 3 * len(kinds)


def _plan_scatter_sibling(kinds):
    def plan(srcs, lands):
        _, _, c = _place()
        sibling, _ = _peers()
        out = []
        for t, kind in enumerate(kinds):
            for k in range(N_CHIP):
                out.append((_window(srcs[t], kind, 2 * k + 1 - c), lands[t].at[k], sibling, lands[t].at[k]))
        return out
    return plan, N_CHIP * len(kinds)


def _plan_scatter_chips(n):
    def plan(srcs, lands):
        x, y, c = _place()
        _, chips = _peers()
        out = []
        for t in range(n):
            for px, py in chips:
                out.append((srcs[t].at[2 * px + py], lands[t].at[2 * x + y], (px, py, c), lands[t].at[2 * px + py]))
        return out
    return plan, 3 * n


def _landing(shard, kind):
    if kind == "blocked":
        return lax.empty((shard.shape[0], N_DEV) + shard.shape[1:], shard.dtype)
    return lax.empty((shard.shape[0], N_DEV * shard.shape[1]), shard.dtype)


def _chip_sums(name, grads, kinds, recvs, c):
    n = len(grads)
    in_specs, out_specs, out_shape, args = [], [], [], []
    for gr, kind, rv in zip(grads, kinds, recvs):
        if kind == "blocked":
            rows, w = gr.shape[2], gr.shape[3]
            in_specs.append(pl.BlockSpec((None, None, rows, w), lambda k, cref: (0, 2 * k + cref[0], 0, 0)))
        else:
            rows, w = gr.shape[0], gr.shape[1] // N_DEV
            in_specs.append(pl.BlockSpec((rows, w), lambda k, cref: (0, 2 * k + cref[0])))
        blk = pl.BlockSpec((None, rows, w), lambda k, cref: (k, 0, 0))
        in_specs.append(blk)
        out_specs.append(blk)
        out_shape.append(jax.ShapeDtypeStruct((N_CHIP, rows, w), BF16))
        args += [gr, rv.reshape(N_CHIP, rows, w)]

    def body(*refs):
        for t in range(n):
            g_ref, r_ref, o_ref = refs[1 + 2 * t], refs[2 + 2 * t], refs[1 + 2 * n + t]
            o_ref[...] = (g_ref[...].astype(F32) + r_ref[...].astype(F32)).astype(BF16)

    return _pallas(body, name=name, n_prefetch=1, grid=(N_CHIP,), in_specs=in_specs, out_specs=out_specs,
                   out_shape=out_shape, params=_params(("parallel",)))(c, *args)


def _adamw_math(g, wv, mv, vv):
    m = ADAM_B1 * mv + (1.0 - ADAM_B1) * g
    v = ADAM_B2 * vv + (1.0 - ADAM_B2) * (g * g)
    m_hat = m / (1.0 - ADAM_B1 ** ADAM_STEP)
    v_hat = v / (1.0 - ADAM_B2 ** ADAM_STEP)
    delta = -ADAM_LR * (m_hat / (jnp.sqrt(v_hat) + ADAM_EPS) + ADAM_WD * wv)
    return delta, m, v


ADAM_STEPS = 2


def _adamw_group(name, items, chip_ids):
    n = len(items)
    in_specs, out_specs, out_shape, args, prevs = [], [], [], [chip_ids], []
    for own, recv, w3, m3, v3, layer, _ in items:
        nl, rows, w = w3.shape
        tr = rows // ADAM_STEPS
        assert tr % 16 == 0, (name, rows)
        in_specs += [pl.BlockSpec((None, tr, w), lambda i, ids, slot=slot: (ids[slot], i, 0)) for slot in range(4)]
        slab = pl.BlockSpec((None, tr, w), lambda i, ids, layer=layer: (layer, i, 0))
        in_specs += [slab] * 3
        out_specs += [slab] * 4
        out_shape += [jax.ShapeDtypeStruct((nl, rows, w), F32)] * 4
        args += [own, recv, recv, recv, w3, m3, v3]
    aliases = {}
    for t, item in enumerate(items):
        if item[6] is not None:
            for k in range(4):
                aliases[len(args) + k] = 4 * t + k
            in_specs += [ANY_SPEC] * 4
            args += list(item[6])
            prevs.append(t)
    n_in = 1 + 7 * n + 4 * len(prevs)

    def body(*refs):
        for t in range(n):
            own_ref, r1_ref, r2_ref, r3_ref, w_ref, m_ref, v_ref = refs[1 + 7 * t:8 + 7 * t]
            g_ref, d_ref, nm_ref, nv_ref = refs[n_in + 4 * t:n_in + 4 * t + 4]
            g = ((own_ref[...].astype(F32) + r1_ref[...].astype(F32)) + r2_ref[...].astype(F32)) + r3_ref[...].astype(F32)
            g_ref[...] = g
            d_ref[...], nm_ref[...], nv_ref[...] = _adamw_math(g, w_ref[...], m_ref[...], v_ref[...])

    outs = _pallas(body, name=name, n_prefetch=1, grid=(ADAM_STEPS,), in_specs=in_specs, out_specs=out_specs,
                   out_shape=out_shape, aliases=aliases, params=_params(("parallel",)))(*args)
    return [list(outs[4 * t:4 * t + 4]) for t in range(n)]


SHARD_ROWS = 8


def _adamw_small(gathered, ws, ms, vs, me):
    n = len(gathered)
    full = [w is not None for w in ws]
    sharded = [w is not None and w.ndim == 3 for w in ws]
    args = list(gathered)
    out_shape = []
    for t in range(n):
        shape = jax.ShapeDtypeStruct(ws[t].shape if sharded[t] else gathered[t].shape[2:], F32)
        if full[t]:
            args += [ws[t], ms[t], vs[t]]
            out_shape += [shape] * 4
        else:
            out_shape += [shape]

    def body(*refs):
        i_in, i_out = n, len(args) + 1
        me_ref = refs[len(args)]
        for t in range(n):
            p_ref = refs[t]
            if sharded[t]:
                w_ref, m_ref, v_ref = refs[i_in:i_in + 3]
                taps, layers, _ = w_ref.shape
                mine = pl.ds(pl.multiple_of(me_ref[0] * SHARD_ROWS, SHARD_ROWS), SHARD_ROWS)
                g = p_ref[0, 0, mine, :]
                for k in range(1, N_DEV):
                    g = g + p_ref[0, k, mine, :]
                for l in range(layers):
                    for k in range(taps):
                        at = (k, slice(l, l + 1), slice(None))
                        row = g[l * taps + k:l * taps + k + 1]
                        refs[i_out][at] = row
                        refs[i_out + 1][at], refs[i_out + 2][at], refs[i_out + 3][at] = _adamw_math(
                            row, w_ref[at], m_ref[at], v_ref[at])
                i_in += 3
                i_out += 4
                continue
            g = p_ref[0, 0]
            for k in range(1, N_DEV):
                g = g + p_ref[0, k]
            refs[i_out][...] = g
            if full[t]:
                w_ref, m_ref, v_ref = refs[i_in:i_in + 3]
                refs[i_out + 1][...], refs[i_out + 2][...], refs[i_out + 3][...] = _adamw_math(
                    g, w_ref[...], m_ref[...], v_ref[...])
                i_in += 3
                i_out += 4
            else:
                i_out += 1

    outs = _pallas(body, name="adamw_small",
                   in_specs=[VMEM_SPEC] * len(args) + [pl.BlockSpec(memory_space=pltpu.SMEM)],
                   out_specs=[VMEM_SPEC] * len(out_shape), out_shape=out_shape,
                   params=pltpu.CompilerParams(vmem_limit_bytes=VMEM_LIMIT))(*args, me)
    result, i = [], 0
    for t in range(n):
        k = 4 if full[t] else 1
        result.append(list(outs[i:i + k]))
        i += k
    return result


KIND = {"sc_w_in": "cols", "sc_w_out": "blocked", "w_dkv": "blocked", "w_kr": "cols", "w_uk": "cols", "w_uv": "cols",
        "w_dq": "blocked", "w_uq": "blocked", "w_o": "blocked", "ffn_w_up": "blocked", "ffn_w_down": "blocked",
        "conv": "blocked"}
GATHER_GROUPS = (("mixer", ("sc_w_in", "sc_w_out", "conv")),
                 ("up0", ("ffn_w_up0",)),
                 ("down0", ("ffn_w_down0",)),
                 ("attn", ("w_dkv", "w_kr", "w_uk", "w_uv", "w_dq", "w_uq", "w_o")),
                 ("ffn1", ("ffn_w_up1", "ffn_w_down1")))
SCATTER_GROUPS = (("ffn1", (("ffn_w_up", 1), ("ffn_w_down", 1))),
                  ("attn", (("w_o", None), ("w_uq", None), ("w_dq", None), ("w_uk", None), ("w_uv", None),
                            ("w_dkv", None), ("w_kr", None))),
                  ("ffn0", (("ffn_w_up", 0), ("ffn_w_down", 0))),
                  ("mixer", (("sc_w_out", None), ("sc_w_in", None))))
SCHEDULE = {
    "begin": (("gather_start", "mixer"),),
    "l0_norm": (("gather_forward", "mixer"), ("gather_start", "up0")),
    "l0_out": (("gather_forward", "up0"), ("gather_start", "down0"), ("gather_start", "attn")),
    "f0_up": (("gather_forward", "down0"), ("gather_forward", "attn"), ("gather_start", "ffn1")),
    "attn_fwd": (("gather_forward", "ffn1"),),
    "f1_gup": (("scatter_sibling", "ffn1"),),
    "f1_dhf": (("scatter_chips", "ffn1"),),
    "kv_bwd": (("scatter_sibling", "attn"),),
    "f0_dact": (("scatter_chips", "attn"),),
    "f0_gup": (("scatter_sibling", "ffn0"),),
    "f0_dhf": (("scatter_chips", "ffn0"),),
    "sc_bwd": (("scatter_sibling", "mixer"), ("scatter_done", "attn")),
    "d_l0_in": (("scatter_chips", "mixer"),),
}
FINISH = (("scatter_done", "ffn1"), ("scatter_done", "ffn0"), ("scatter_done", "mixer"))
STAGES = {"gather_start": 1, "gather_forward": 2, "gather_done": 3,
          "scatter_sibling": 1, "scatter_chips": 2, "scatter_done": 3}
SMALL_W_ROWS = 24


def _pack(arrays, rows):
    flat = jnp.concatenate([a.reshape(-1).astype(F32) for a in arrays])
    return jnp.pad(flat, (0, rows * 128 - flat.shape[0])).reshape(rows, 128)


def _cast_shards(items):
    arrays = []
    for a, _, _ in items:
        if not any(a is b for b in arrays):
            arrays.append(a)
    slot = [next(i for i, b in enumerate(arrays) if b is a) for a, _, _ in items]

    def body(*refs):
        for t, (a, layer, rows) in enumerate(items):
            w_ref, o_ref = refs[slot[t]], refs[len(arrays) + t]
            r, c = a.shape[-2:]
            if a.ndim == 3:
                o_ref[:, :r] = w_ref[(layer or 0):(layer or 0) + 1].astype(BF16)
                if rows > r:
                    o_ref[:, r:] = jnp.zeros((1, rows - r, c), BF16)
            else:
                o_ref[:r] = w_ref[...].astype(BF16)
                if rows > r:
                    o_ref[r:] = jnp.zeros((rows - r, c), BF16)

    out_shape = [jax.ShapeDtypeStruct(((1,) if a.ndim == 3 else ()) + (rows, a.shape[-1]), BF16)
                 for a, _, rows in items]
    return _pallas(body, name="cast_shards", in_specs=[VMEM_SPEC] * len(arrays), out_specs=[VMEM_SPEC] * len(items),
                   out_shape=out_shape, params=pltpu.CompilerParams(vmem_limit_bytes=VMEM_LIMIT))(*arrays)


STORED_TRANSPOSED = ("ffn_w_up", "w_uq", "w_kr")


def _stored(name, a):
    return jnp.swapaxes(a, -1, -2) if name in STORED_TRANSPOSED else a


def _base(name):
    if name.startswith("ffn_w_") and name[-1] in "01":
        return name[:-1], int(name[-1])
    return name, None


class _Exchange:
    def __init__(self, wts, mom, var, ffn_conv_b):
        self.wts, self.mom, self.var = wts, mom, var
        x, y, c = _place()
        self.c_arr = jnp.reshape(c, (1,)).astype(jnp.int32)
        chip = 2 * x + y
        self.chip_ids = jnp.stack([chip, chip ^ 1, chip ^ 2, chip ^ 3]).astype(jnp.int32)
        self.ready = {"ffn_cb0": ffn_conv_b.reshape(2, N_FF_BLK, 1, FF_BLK)[0],
                      "ffn_cb1": ffn_conv_b.reshape(2, N_FF_BLK, 1, FF_BLK)[1]}
        self.gathers, self.group_of = {}, {}
        self.grads, self.scatters, self.results, self.queue = {}, {}, {}, []
        for gname, names in GATHER_GROUPS:
            self.gathers[gname] = dict(stage=0, names=names, kinds=[KIND[_base(nm)[0]] for nm in names])
            for nm in names:
                self.group_of[nm] = gname
        for nm in ("sc_conv_w", "ffn_cw0", "ffn_cw1"):
            self.group_of[nm] = "mixer"
        self.cast, self.f32 = {}, {}
        self.at("begin", None)
        later = [nm for gname, names in GATHER_GROUPS[1:] for nm in names]
        self.cast = dict(zip(later, _cast_shards([self._shard_f32(nm) for nm in later])))

    def _shard_f32(self, name):
        base, layer = _base(name)
        if base not in self.f32:
            a = _stored(base, self.wts[base])
            self.f32[base] = a.reshape(a.shape[-2:]) if KIND[base] == "cols" else a.reshape((-1,) + a.shape[-2:])
        a = self.f32[base]
        return a, layer, {"w_kr": 128, "w_uq": QK_PAD}.get(base, a.shape[-2])

    def _shard(self, name):
        if name in self.cast:
            return self.cast[name]
        if name == "conv":
            return _pack([self.wts["sc_conv_w"], self.wts["ffn_conv_w"]], SMALL_W_ROWS).reshape(1, SMALL_W_ROWS, 128)
        base, layer = _base(name)
        a = _stored(base, self.wts[base])
        if layer is not None:
            a = a[layer:layer + 1]
        if KIND[base] == "cols":
            return a.reshape(a.shape[-2], a.shape[-1]).astype(BF16)
        return a.reshape((-1,) + a.shape[-2:]).astype(BF16)

    def _start(self, name, srcs, lands, ncopy, plan, st):
        self.queue.append((name, (srcs, lands, ncopy, plan), st))

    def _flush(self):
        if self.queue:
            flights = _copies_start("__".join(name for name, _, _ in self.queue), [job for _, job, _ in self.queue])
            for (_, _, st), flight in zip(self.queue, flights):
                st["flight"] = flight
            self.queue = []

    def _flight(self, st):
        self._flush()
        return st["flight"]

    def _gather_to(self, gname, stage, after):
        st = self.gathers[gname]
        if st["stage"] < 1 <= stage:
            shards = [self._shard(nm) for nm in st["names"]]
            lands = [_landing(s, kind) for s, kind in zip(shards, st["kinds"])]
            plan, ncopy = _plan_gather_chips(st["kinds"])
            self._start(f"ag_{gname}_chips", shards, lands, ncopy, plan, st)
            st["stage"] = 1
        if st["stage"] < 2 <= stage:
            plan, ncopy = _plan_gather_chips(st["kinds"])
            _, lands = _copies_wait(f"ag_{gname}_chips_wait", self._flight(st), ncopy, plan)
            plan, ncopy = _plan_gather_sibling(st["kinds"])
            self._start(f"ag_{gname}_sibling", [], lands, ncopy, plan, st)
            st["stage"] = 2
        if st["stage"] < 3 <= stage:
            plan, ncopy = _plan_gather_sibling(st["kinds"])
            _, lands = _copies_wait(f"ag_{gname}_sibling_wait", self._flight(st), ncopy, plan)
            for nm, land in zip(st["names"], lands):
                self._arrived(nm, land)
            st["stage"] = 3

    def _arrived(self, name, land):
        if name == "conv":
            conv = land.reshape(N_DEV, SMALL_W_ROWS * 128)
            self.ready["sc_conv_w"] = conv[:, :3 * 128].reshape(N_DEV, 3, 128).transpose(1, 0, 2).reshape(3, D)
            fcw = conv[:, 3 * 128:3 * 128 + 6 * 352].reshape(N_DEV, 2, 3, 352).transpose(1, 2, 0, 3)
            fcw = fcw.reshape(2, 3, N_FF_BLK, FF_BLK).transpose(0, 2, 1, 3)
            self.ready["ffn_cw0"], self.ready["ffn_cw1"] = fcw[0], fcw[1]
        elif name in ("sc_w_in", "w_uk", "w_uv", "w_kr") or name.startswith("ffn_w_up"):
            self.ready[name] = land
        elif name.startswith("ffn_w_down"):
            self.ready[name] = land.reshape(1, N_FF_BLK, FF_BLK, D)
        elif name == "w_uq":
            self.ready[name] = land.reshape(N_HEADS, QK_PAD, Q_LORA)
        else:
            self.ready[name] = land.reshape(D, land.shape[-1])

    def need(self, name, after):
        if name not in self.ready:
            self._gather_to(self.group_of[name], 3, after)
            self._flush()
        return self.ready[name]

    def grad(self, name, layer, array):
        self.grads[(name, layer)] = array

    def _scatter_to(self, gname, stage, after):
        keys = dict(SCATTER_GROUPS)[gname]
        st = self.scatters.setdefault(gname, dict(stage=0))
        kinds = [KIND[nm] for nm, _ in keys]
        if st["stage"] < 1 <= stage:
            grads = [self.grads[key] for key in keys]
            lands = []
            for gr, kind in zip(grads, kinds):
                shard = (gr.shape[0],) + gr.shape[2:] if kind == "blocked" else (gr.shape[0], gr.shape[1] // N_DEV)
                lands.append(lax.empty((N_CHIP,) + shard, BF16))
            plan, ncopy = _plan_scatter_sibling(kinds)
            self._start(f"rs_{gname}_sibling", grads, lands, ncopy, plan, st)
            st["stage"] = 1
        if st["stage"] < 2 <= stage:
            plan, ncopy = _plan_scatter_sibling(kinds)
            grads, recvs = _copies_wait(f"rs_{gname}_sibling_wait", self._flight(st), ncopy, plan)
            sums = _chip_sums(f"rs_{gname}_sums", grads, kinds, recvs, self.c_arr)
            lands = [lax.empty(s.shape, BF16) for s in sums]
            plan, ncopy = _plan_scatter_chips(len(sums))
            self._start(f"rs_{gname}_chips", sums, lands, ncopy, plan, st)
            st["stage"] = 2
        if st["stage"] < 3 <= stage:
            plan, ncopy = _plan_scatter_chips(len(keys))
            sums, recvs = _copies_wait(f"rs_{gname}_chips_wait", self._flight(st), ncopy, plan)
            items = []
            for (nm, layer), own, rv in zip(keys, sums, recvs):
                nl = 1 if layer is None else 2
                rows, w = own.shape[1], own.shape[2]
                w3, m3, v3 = (_stored(nm, src[nm]).reshape(nl, rows, w) for src in (self.wts, self.mom, self.var))
                items.append((own, rv, w3, m3, v3, 0 if layer is None else layer, self.results.get(nm)))
            outs = _adamw_group(f"adamw_{gname}", items, self.chip_ids)
            for (nm, _), out in zip(keys, outs):
                self.results[nm] = out
            st["stage"] = 3

    def at(self, place, after):
        for action, gname in SCHEDULE.get(place, ()):
            self._advance(action, gname, after)
        self._flush()

    def _advance(self, action, gname, after):
        if action.startswith("gather"):
            self._gather_to(gname, STAGES[action], after)
        else:
            self._scatter_to(gname, STAGES[action], after)

    def finish(self, after):
        for action, gname in FINISH:
            self._advance(action, gname, after)
        for gname, _ in SCATTER_GROUPS:
            self._scatter_to(gname, 3, after)
        return {nm: [_stored(nm, o.reshape(_stored(nm, self.wts[nm]).shape)) for o in outs]
                for nm, outs in self.results.items()}


REPLICATED = ("attn_norm", "ffn_norm", "final_norm", "kv_in_norm", "kv_latent_norm", "q_latent_norm", "ffn_conv_b")
WEIGHTS = ("attn_norm", "ffn_norm", "final_norm", "sc_w_in", "sc_conv_w", "sc_w_out", "kv_in_norm", "w_dkv",
           "kv_latent_norm", "w_kr", "w_uk", "w_uv", "w_dq", "q_latent_norm", "w_uq", "w_o", "ffn_w_up", "ffn_conv_w",
           "ffn_conv_b", "ffn_w_down")


def kernel(x, positions, attn_norm, ffn_norm, final_norm, sc_w_in, sc_conv_w, sc_w_out, kv_in_norm, w_dkv, kv_latent_norm, w_kr, w_uk, w_uv, w_dq, q_latent_norm, w_uq, w_o, ffn_w_up, ffn_conv_w, ffn_conv_b, ffn_w_down, loss_target, m_attn_norm, m_ffn_norm, m_final_norm, m_sc_w_in, m_sc_conv_w, m_sc_w_out, m_kv_in_norm, m_w_dkv, m_kv_latent_norm, m_w_kr, m_w_uk, m_w_uv, m_w_dq, m_q_latent_norm, m_w_uq, m_w_o, m_ffn_w_up, m_ffn_conv_w, m_ffn_conv_b, m_ffn_w_down, v_attn_norm, v_ffn_norm, v_final_norm, v_sc_w_in, v_sc_conv_w, v_sc_w_out, v_kv_in_norm, v_w_dkv, v_kv_latent_norm, v_w_kr, v_w_uk, v_w_uv, v_w_dq, v_q_latent_norm, v_w_uq, v_w_o, v_ffn_w_up, v_ffn_conv_w, v_ffn_conv_b, v_ffn_w_down):
    wts = dict(attn_norm=attn_norm, ffn_norm=ffn_norm, final_norm=final_norm, sc_w_in=sc_w_in, sc_conv_w=sc_conv_w,
               sc_w_out=sc_w_out, kv_in_norm=kv_in_norm, w_dkv=w_dkv, kv_latent_norm=kv_latent_norm, w_kr=w_kr,
               w_uk=w_uk, w_uv=w_uv, w_dq=w_dq, q_latent_norm=q_latent_norm, w_uq=w_uq, w_o=w_o, ffn_w_up=ffn_w_up,
               ffn_conv_w=ffn_conv_w, ffn_conv_b=ffn_conv_b, ffn_w_down=ffn_w_down)
    mom = dict(attn_norm=m_attn_norm, ffn_norm=m_ffn_norm, final_norm=m_final_norm, sc_w_in=m_sc_w_in,
               sc_conv_w=m_sc_conv_w, sc_w_out=m_sc_w_out, kv_in_norm=m_kv_in_norm, w_dkv=m_w_dkv,
               kv_latent_norm=m_kv_latent_norm, w_kr=m_w_kr, w_uk=m_w_uk, w_uv=m_w_uv, w_dq=m_w_dq,
               q_latent_norm=m_q_latent_norm, w_uq=m_w_uq, w_o=m_w_o, ffn_w_up=m_ffn_w_up, ffn_conv_w=m_ffn_conv_w,
               ffn_conv_b=m_ffn_conv_b, ffn_w_down=m_ffn_w_down)
    var = dict(attn_norm=v_attn_norm, ffn_norm=v_ffn_norm, final_norm=v_final_norm, sc_w_in=v_sc_w_in,
               sc_conv_w=v_sc_conv_w, sc_w_out=v_sc_w_out, kv_in_norm=v_kv_in_norm, w_dkv=v_w_dkv,
               kv_latent_norm=v_kv_latent_norm, w_kr=v_w_kr, w_uk=v_w_uk, w_uv=v_w_uv, w_dq=v_w_dq,
               q_latent_norm=v_q_latent_norm, w_uq=v_w_uq, w_o=v_w_o, ffn_w_up=v_ffn_w_up, ffn_conv_w=v_ffn_conv_w,
               ffn_conv_b=v_ffn_conv_b, ffn_w_down=v_ffn_w_down)
    xi, yi, ci = _place()
    me = 4 * xi + 2 * yi + ci
    _Chain.last = None

    ex = _Exchange(wts, mom, var, ffn_conv_b)
    rep = {
        "attn_norm": attn_norm, "ffn_norm": ffn_norm, "final_norm": final_norm,
        "kv_in_norm": kv_in_norm.reshape(1, D), "kv_latent_norm": kv_latent_norm.reshape(1, KV_LORA),
        "q_latent_norm": q_latent_norm.reshape(1, Q_LORA),
    }
    loss, grad_x, small = _local_step(x.reshape(T, D), positions.reshape(T, 1), loss_target.reshape(T, D), rep, ex)

    def rows_of(a):
        return a.reshape(-1, a.shape[-1])

    def device_rows(a):
        taps, c = a.shape[-2], a.shape[-1] // N_DEV
        rows = a.reshape(-1, taps, N_DEV, c).transpose(2, 0, 1, 3).reshape(N_DEV, -1, c)
        return jnp.pad(rows, ((0, 0), (0, SHARD_ROWS - rows.shape[1]), (0, 0))).reshape(N_DEV * SHARD_ROWS, c)

    def taps_first(a):
        return jnp.transpose(a, (1, 0, 2))

    sharded = ("sc_conv_w", "ffn_conv_w")
    shards = ([loss.reshape(1, 1, 128)] + [rows_of(small[nm])[None] for nm in REPLICATED]
              + [device_rows(small[nm])[None] for nm in sharded])
    plan, ncopy = _plan_gather_all(len(shards))
    flight, = _copies_start("ag_small", [(shards, [lax.empty((1, N_DEV) + s.shape[1:], F32) for s in shards], ncopy, plan)])
    results = ex.finish(grad_x)
    _, gathered = _copies_wait("ag_small_wait", flight, ncopy, plan)
    params = [[None] + [rows_of(src[nm]) for nm in REPLICATED] + [taps_first(src[nm]) for nm in sharded]
              for src in (wts, mom, var)]
    summed = _adamw_small(gathered, *params, me.astype(jnp.int32).reshape(1))
    loss_total = summed[0][0][0, 0]
    for nm, vals in zip(REPLICATED, summed[1:1 + len(REPLICATED)]):
        results[nm] = [a.reshape(wts[nm].shape) for a in vals]
    for nm, vals in zip(sharded, summed[1 + len(REPLICATED):]):
        results[nm] = [taps_first(a) for a in vals]

    outs = [loss_total, grad_x.reshape(1, T, D)]
    for slot in range(4):
        outs.extend(results[nm][slot] for nm in WEIGHTS)
    return tuple(outs)
```

```python
import jax
import jax.numpy as jnp
from jax import lax
from jax.experimental import pallas as pl
from jax.experimental.pallas import tpu as pltpu

F32 = jnp.float32
BF16 = jnp.bfloat16

T = 2048
D = 1024
N_HEADS = 8
QK_NOPE = 128
QK_ROPE = 64
V_HEAD = 128
Q_LORA = 384
KV_LORA = 256
D_FF = 2816
CHUNK = 64
ROPE_THETA = 10000.0
EPS = 1e-6
NEG_INF = -1e30
ADAM_LR = 0.001
ADAM_B1 = 0.9
ADAM_B2 = 0.999
ADAM_EPS = 1e-08
ADAM_WD = 0.01
ADAM_STEP = 10

N_DEV = 8
N_CHIP = 4
FF_BLK = D_FF * 2 // N_DEV
N_FF_BLK = D_FF // FF_BLK
QK_PAD = 256
HALO = 16

TM = 1024
TS = 512
TR = 256
TQ = 512
VMEM_LIMIT = 56 * 1024 * 1024

NN = (((1,), (0,)), ((), ()))
NT = (((1,), (1,)), ((), ()))
TN = (((0,), (0,)), ((), ()))
MESH = pl.DeviceIdType.MESH


def _params(sem):
    return pltpu.CompilerParams(dimension_semantics=sem, vmem_limit_bytes=VMEM_LIMIT)


ANY_SPEC = pl.BlockSpec(memory_space=pl.ANY)
VMEM_SPEC = pl.BlockSpec(memory_space=pltpu.VMEM)


class _Chain:
    last = None


def _pallas(body, *, name, in_specs, out_specs, out_shape, grid=(), scratch_shapes=(), n_prefetch=0, aliases=None,
            params=None):
    def run(*args):
        after = _Chain.last
        n_lead = len(args)
        specs, operands, fn = list(in_specs), list(args), body
        if after is not None:
            def fn(*refs):
                return body(*refs[:n_lead], *refs[n_lead + 1:])
            specs.append(ANY_SPEC)
            operands.append(after)
        kw = dict(name=name, out_shape=out_shape, input_output_aliases=aliases or {})
        if params is not None:
            kw["compiler_params"] = params
        if n_prefetch:
            kw["grid_spec"] = pltpu.PrefetchScalarGridSpec(
                num_scalar_prefetch=n_prefetch, grid=grid, in_specs=specs, out_specs=out_specs,
                scratch_shapes=scratch_shapes)
        else:
            kw.update(grid=grid, in_specs=specs, out_specs=out_specs, scratch_shapes=scratch_shapes)
        outs = pl.pallas_call(fn, **kw)(*operands)
        _Chain.last = outs[0] if isinstance(outs, (list, tuple)) else outs
        return outs
    return run


def _mm(name, a, b, *, grid, a_spec, b_spec, o_spec, o_shape, o_dtype, dims, k_axis=None, acc_shape=None,
        add=None, add_spec=None):
    nk = grid[k_axis] if k_axis is not None else 1
    has_add = add is not None

    def body(*refs):
        a_ref, b_ref = refs[0], refs[1]
        p = 2
        add_ref = None
        if has_add:
            add_ref = refs[p]
            p += 1
        o_ref = refs[p]
        p += 1
        r = lax.dot_general(a_ref[...].astype(BF16), b_ref[...].astype(BF16), dims, preferred_element_type=F32)
        if k_axis is None:
            if has_add:
                r = r + add_ref[...].astype(F32)
            o_ref[...] = r.astype(o_dtype)
        else:
            acc = refs[p]
            k = pl.program_id(k_axis)

            @pl.when(k == 0)
            def _():
                acc[...] = r

            @pl.when(k > 0)
            def _():
                acc[...] += r

            @pl.when(k == nk - 1)
            def _():
                t = acc[...]
                if has_add:
                    t = t + add_ref[...].astype(F32)
                o_ref[...] = t.astype(o_dtype)

    in_specs = [a_spec, b_spec]
    args = [a, b]
    if has_add:
        in_specs.append(add_spec if add_spec is not None else o_spec)
        args.append(add)
    sem = tuple("arbitrary" if ax == k_axis else "parallel" for ax in range(len(grid)))
    scratch = [pltpu.VMEM(acc_shape, F32)] if k_axis is not None else []
    return _pallas(body, name=name, grid=grid, in_specs=in_specs, out_specs=o_spec,
                   out_shape=jax.ShapeDtypeStruct(o_shape, o_dtype), scratch_shapes=scratch, params=_params(sem))(*args)


def _mm_sum(name, parts, *, grid, o_spec, o_shape, o_dtype, add=None, norm_bwd=None, post=None):
    has_add = add is not None
    np_ = len(parts)
    nn = 1 if norm_bwd is None else len(norm_bwd[1])
    has_res = norm_bwd is not None and norm_bwd[2] is not None
    has_post = post is not None

    def body(*refs):
        accs = [None] * nn
        for p, (_, _, _, _, dims, n) in enumerate(parts):
            a_ref, b_ref = refs[2 * p], refs[2 * p + 1]
            for k in range(a_ref.shape[0]):
                r = lax.dot_general(a_ref[k], b_ref[k], dims, preferred_element_type=F32)
                accs[n] = r if accs[n] is None else accs[n] + r
        if norm_bwd is None:
            acc = accs[0]
            if has_add:
                acc = acc + refs[2 * np_][...]
            refs[-1][...] = acc.astype(o_dtype)
            return
        x_ref, g_refs = refs[2 * np_], refs[2 * np_ + 1:2 * np_ + 1 + nn]
        n_in = 2 * np_ + 1 + nn + has_res + has_post
        dx_ref, dxb_ref, dg_refs = refs[n_in], refs[n_in + 1], refs[n_in + 2:n_in + 2 + nn]
        xv = x_ref[...]
        r = lax.rsqrt(jnp.mean(xv * xv, axis=-1, keepdims=True) + EPS)
        xn = xv * r
        dx = refs[2 * np_ + 1 + nn][...] if has_res else None
        sums = []
        for acc, g_ref in zip(accs, g_refs):
            gdy = acc * g_ref[...]
            t = r * (gdy - xn * jnp.mean(gdy * xn, axis=-1, keepdims=True))
            dx = t if dx is None else dx + t
            sums.append(jnp.sum(acc * xn, axis=0, keepdims=True))
        dx_ref[...] = dx
        dxb = dx.astype(BF16)
        dxb_ref[...] = dxb
        if has_post:
            refs[n_in + 2 + nn][...] = lax.dot_general(dxb, refs[n_in - 1][...], post[1],
                                                       preferred_element_type=F32).astype(BF16)

        @pl.when(pl.program_id(0) == 0)
        def _():
            for dg_ref, part in zip(dg_refs, sums):
                dg_ref[...] = part

        @pl.when(pl.program_id(0) > 0)
        def _():
            for dg_ref, part in zip(dg_refs, sums):
                dg_ref[...] += part

    in_specs, args = [], []
    for a, a_spec, b, b_spec, _, _ in parts:
        in_specs += [a_spec, b_spec]
        args += [a, b]
    if norm_bwd is None:
        if has_add:
            in_specs.append(o_spec)
            args.append(add)
        return _pallas(body, name=name, grid=grid, in_specs=in_specs, out_specs=o_spec,
                       out_shape=jax.ShapeDtypeStruct(o_shape, o_dtype),
                       params=_params(("parallel",) * len(grid)))(*args)
    x, gains, dres = norm_bwd
    vec = pl.BlockSpec((1, o_shape[1]), lambda i: (0, 0))
    in_specs += [o_spec] + [vec] * nn + ([o_spec] if has_res else [])
    args += [x] + list(gains) + ([dres] if has_res else [])
    out_specs = [o_spec, o_spec] + [vec] * nn
    out_shape = ([jax.ShapeDtypeStruct(o_shape, F32), jax.ShapeDtypeStruct(o_shape, BF16)]
                 + [jax.ShapeDtypeStruct((1, o_shape[1]), F32)] * nn)
    if has_post:
        in_specs.append(pl.BlockSpec(post[0].shape, lambda i: (0, 0)))
        args.append(post[0])
        out_specs.append(pl.BlockSpec((o_spec.block_shape[0], post[2]), lambda i: (i, 0)))
        out_shape.append(jax.ShapeDtypeStruct((o_shape[0], post[2]), BF16))
    outs = _pallas(body, name=name, grid=grid, in_specs=in_specs, out_specs=out_specs, out_shape=out_shape,
                   params=_params(("arbitrary",)))(*args)
    if has_post:
        return outs[0], outs[1], list(outs[2:2 + nn]), outs[2 + nn]
    return outs[0], outs[1], list(outs[2:])


def _mm_rows(name, a, b, dims, o_dtype, n_out, *, tn=None, add=None):
    k = a.shape[1]
    tn = n_out if tn is None else tn
    if dims == NN:
        b_spec = pl.BlockSpec((k, tn), lambda n, i: (0, n))
    else:
        b_spec = pl.BlockSpec((tn, k), lambda n, i: (n, 0))
    return _mm(name, a, b, grid=(n_out // tn, T // TM),
               a_spec=pl.BlockSpec((TM, k), lambda n, i: (i, 0)), b_spec=b_spec,
               o_spec=pl.BlockSpec((TM, tn), lambda n, i: (i, n)), o_shape=(T, n_out), o_dtype=o_dtype,
               dims=dims, add=add)


def _wgrads(name, jobs):
    jobs = [job if len(job) == 3 else (*job, job[0].shape[-1]) for job in jobs]
    arrays, index = [], {}
    for a, b, _ in jobs:
        for arr in (a, b):
            if id(arr) not in index:
                index[id(arr)] = len(arrays)
                arrays.append(arr)
    n_in = len(arrays)

    def body(*refs):
        for t, (a, b, rows) in enumerate(jobs):
            a_ref, b_ref, o_ref = refs[index[id(a)]], refs[index[id(b)]], refs[n_in + t]
            if a.ndim == 3:
                for h in range(a.shape[0]):
                    o_ref[h] = lax.dot_general(a_ref[h], b_ref[...], TN, preferred_element_type=F32)[:rows].astype(BF16)
            else:
                o_ref[...] = lax.dot_general(a_ref[...], b_ref[...], TN, preferred_element_type=F32)[:rows].astype(BF16)

    out_shape = [jax.ShapeDtypeStruct(a.shape[:-2] + (rows, b.shape[-1]), BF16) for a, b, rows in jobs]
    return _pallas(body, name=name, in_specs=[VMEM_SPEC] * n_in, out_specs=[VMEM_SPEC] * len(jobs), out_shape=out_shape,
                   params=pltpu.CompilerParams(vmem_limit_bytes=VMEM_LIMIT))(*arrays)


def _mm_wgrad(name, a, b, *, tn=512):
    k, n = a.shape[1], b.shape[1]
    tn = min(tn, n)
    return _mm(name, a, b, grid=(n // tn,),
               a_spec=pl.BlockSpec((T, k), lambda j: (0, 0)), b_spec=pl.BlockSpec((T, tn), lambda j: (0, j)),
               o_spec=pl.BlockSpec((k, tn), lambda j: (0, j)), o_shape=(k, n), o_dtype=BF16, dims=TN)


def _rms_fwd(name, x, g):
    d = x.shape[1]

    def body(x_ref, g_ref, o_ref):
        xv = x_ref[...]
        r = lax.rsqrt(jnp.mean(xv * xv, axis=-1, keepdims=True) + EPS)
        o_ref[...] = ((xv * r) * g_ref[...]).astype(BF16)

    return _pallas(
        body, name=name, grid=(T // TM,),
        in_specs=[pl.BlockSpec((TM, d), lambda i: (i, 0)), pl.BlockSpec((1, d), lambda i: (0, 0))],
        out_specs=pl.BlockSpec((TM, d), lambda i: (i, 0)),
        out_shape=jax.ShapeDtypeStruct((T, d), BF16), params=_params(("parallel",)))(x, g)


def _rms(xv, g):
    return (xv * lax.rsqrt(jnp.mean(xv * xv, axis=-1, keepdims=True) + EPS)) * g


def _out_norm(name, a, w, add, g):
    def body(a_ref, w_ref, add_ref, g_ref, h_ref, hn_ref):
        hv = lax.dot_general(a_ref[...], w_ref[...], NN, preferred_element_type=F32) + add_ref[...]
        h_ref[...] = hv
        hn_ref[...] = _rms(hv, g_ref[...]).astype(BF16)

    rows = pl.BlockSpec((TS, D), lambda i: (i, 0))
    return _pallas(
        body, name=name, grid=(T // TS,),
        in_specs=[pl.BlockSpec((TS, a.shape[1]), lambda i: (i, 0)), pl.BlockSpec(w.shape, lambda i: (0, 0)), rows,
                  pl.BlockSpec((1, D), lambda i: (0, 0))],
        out_specs=[rows, rows], out_shape=[jax.ShapeDtypeStruct((T, D), F32), jax.ShapeDtypeStruct((T, D), BF16)],
        params=_params(("parallel",)))(a, w, add, g)


def _rms_bwd(name, x, gains, dys, dres=None):
    d = x.shape[1]
    n = len(gains)
    has_res = dres is not None

    def body(*refs):
        x_ref, g_refs, dy_refs = refs[0], refs[1:1 + n], refs[1 + n:1 + 2 * n]
        dx_ref, dxb_ref = refs[-2 - n], refs[-1 - n]
        dg_refs = refs[-n:]
        xv = x_ref[...]
        r = lax.rsqrt(jnp.mean(xv * xv, axis=-1, keepdims=True) + EPS)
        xn = xv * r
        dx = refs[1 + 2 * n][...] if has_res else None
        parts = []
        for g_ref, dy_ref in zip(g_refs, dy_refs):
            dyv = dy_ref[...].astype(F32)
            gdy = dyv * g_ref[...]
            t = r * (gdy - xn * jnp.mean(gdy * xn, axis=-1, keepdims=True))
            dx = t if dx is None else dx + t
            parts.append(jnp.sum(dyv * xn, axis=0, keepdims=True))
        dx_ref[...] = dx
        dxb_ref[...] = dx.astype(BF16)

        @pl.when(pl.program_id(0) == 0)
        def _():
            for dg_ref, part in zip(dg_refs, parts):
                dg_ref[...] = part

        @pl.when(pl.program_id(0) > 0)
        def _():
            for dg_ref, part in zip(dg_refs, parts):
                dg_ref[...] += part

    row = pl.BlockSpec((TR, d), lambda i: (i, 0))
    vec = pl.BlockSpec((1, d), lambda i: (0, 0))
    args = [x] + list(gains) + list(dys) + ([dres] if has_res else [])
    in_specs = [row] + [vec] * n + [row] * n + ([row] if has_res else [])
    outs = _pallas(
        body, name=name, grid=(T // TR,), in_specs=in_specs, out_specs=[row, row] + [vec] * n,
        out_shape=[jax.ShapeDtypeStruct((T, d), F32), jax.ShapeDtypeStruct((T, d), BF16)]
        + [jax.ShapeDtypeStruct((1, d), F32)] * n,
        params=_params(("arbitrary",)))(*args)
    return outs[0], outs[1], list(outs[2:])


def _down_final(act, w_down4, h_in, g, tgt):
    def body(a_ref, w_ref, hin_ref, g_ref, t_ref, loss_ref, dh_ref, dhb_ref, dg_ref):
        hv = lax.dot_general(a_ref[0], w_ref[0], NN, preferred_element_type=F32)
        for j in range(1, N_FF_BLK):
            hv = hv + lax.dot_general(a_ref[j], w_ref[j], NN, preferred_element_type=F32)
        hv = hv + hin_ref[...]
        r = lax.rsqrt(jnp.mean(hv * hv, axis=-1, keepdims=True) + EPS)
        xn = hv * r
        gv = g_ref[...]
        err = xn * gv - t_ref[...]
        part_loss = 0.5 * jnp.sum(jnp.mean(err * err, axis=-1, keepdims=True), axis=0, keepdims=True)
        dy = err * (1.0 / D)
        gdy = dy * gv
        dh = r * (gdy - xn * jnp.mean(gdy * xn, axis=-1, keepdims=True))
        dh_ref[...] = dh
        dhb_ref[...] = dh.astype(BF16)
        part = jnp.sum(dy * xn, axis=0, keepdims=True)
        first = pl.program_id(0) == 0

        @pl.when(first)
        def _():
            dg_ref[...] = part
            loss_ref[...] = jnp.broadcast_to(part_loss, (1, 128))

        @pl.when(jnp.logical_not(first))
        def _():
            dg_ref[...] += part
            loss_ref[...] += jnp.broadcast_to(part_loss, (1, 128))

    row = pl.BlockSpec((TS, D), lambda i: (i, 0))
    vec = pl.BlockSpec((1, D), lambda i: (0, 0))
    return _pallas(
        body, name="f1_down_loss", grid=(T // TS,),
        in_specs=[pl.BlockSpec((N_FF_BLK, TS, FF_BLK), lambda i: (0, i, 0)),
                  pl.BlockSpec((None, N_FF_BLK, FF_BLK, D), lambda i: (0, 0, 0, 0)), row, vec, row],
        out_specs=[pl.BlockSpec((1, 128), lambda i: (0, 0)), row, row, vec],
        out_shape=[jax.ShapeDtypeStruct((1, 128), F32), jax.ShapeDtypeStruct((T, D), F32),
                   jax.ShapeDtypeStruct((T, D), BF16), jax.ShapeDtypeStruct((1, D), F32)],
        params=_params(("arbitrary",)))(act, w_down4, h_in, g, tgt)


def _prev_idx(i, rows=TR):
    return jnp.maximum(i * (rows // HALO) - 1, 0)


def _next_idx(i, rows=TR):
    return jnp.minimum((i + 1) * (rows // HALO), T // HALO - 1)


def _causal_taps(ext):
    return pltpu.roll(ext, 2, 0)[HALO:], pltpu.roll(ext, 1, 0)[HALO:], ext[HALO:]


def _anticausal_taps(ext, n):
    rows = ext.shape[0]
    return pltpu.roll(ext, rows - 1, 0)[:n], pltpu.roll(ext, rows - 2, 0)[:n]


MIX_COLS = 512


def _mixer_in(hn, w_in, w):
    nc = D // MIX_COLS

    def body(h_ref, hh_ref, wb_ref, wc_ref, wu_ref, w_ref, b_ref, c_ref, u_ref, y_ref):
        i = pl.program_id(1)
        hv = h_ref[...]
        he = jnp.concatenate([hh_ref[...], hv], axis=0)
        ce = lax.dot_general(he, wc_ref[...], NN, preferred_element_type=F32).astype(BF16)
        ue = lax.dot_general(he, wu_ref[...], NN, preferred_element_type=F32).astype(BF16)
        bv = lax.dot_general(hv, wb_ref[...], NN, preferred_element_type=F32).astype(BF16)
        b_ref[...] = bv
        c_ref[...] = ce[HALO:]
        u_ref[...] = ue[HALO:]
        row = lax.broadcasted_iota(jnp.int32, (HALO + TS, 1), 0)
        cu = jnp.where(jnp.logical_or(i > 0, row >= HALO), ce.astype(F32) * ue.astype(F32), 0.0)
        x2, x1, x0 = _causal_taps(cu)
        wv = w_ref[...]
        cv = (x2 * wv[0:1] + x1 * wv[1:2]) + x0 * wv[2:3]
        y_ref[...] = (bv.astype(F32) * cv).astype(BF16)

    def cols(part):
        return pl.BlockSpec((D, MIX_COLS), lambda j, i: (0, part * nc + j))

    blk = pl.BlockSpec((TS, MIX_COLS), lambda j, i: (i, j))
    out = jax.ShapeDtypeStruct((T, D), BF16)
    return _pallas(
        body, name="l0_in", grid=(nc, T // TS),
        in_specs=[pl.BlockSpec((TS, D), lambda j, i: (i, 0)), pl.BlockSpec((HALO, D), lambda j, i: (_prev_idx(i, TS), 0)),
                  cols(0), cols(1), cols(2), pl.BlockSpec((3, MIX_COLS), lambda j, i: (0, j))],
        out_specs=[blk] * 4, out_shape=[out] * 4,
        params=_params(("parallel", "parallel")))(hn, hn, w_in, w_in, w_in, w)


def _mixer_out_bwd(dh, w_out, zb, zc, zu, w):
    last = T // TR - 1

    def body(dh_ref, dhn_ref, wo_ref, b_ref, bn_ref, c_ref, ch_ref, u_ref, uh_ref, w_ref, dz_ref, dw_ref):
        i = pl.program_id(0)
        dye = lax.dot_general(jnp.concatenate([dh_ref[...], dhn_ref[...]], axis=0), wo_ref[...], NT,
                              preferred_element_type=F32)
        cv_ = c_ref[...].astype(F32)
        uv = u_ref[...].astype(F32)
        cu = cv_ * uv
        cuh = jnp.where(i > 0, ch_ref[...].astype(F32) * uh_ref[...].astype(F32), 0.0)
        x2, x1, x0 = _causal_taps(jnp.concatenate([cuh, cu], axis=0))
        wv = w_ref[...]
        conv = (x2 * wv[0:1] + x1 * wv[1:2]) + x0 * wv[2:3]
        dyv = dye[:TR]
        dz_ref[:, 0:D] = (dyv * conv).astype(BF16)
        dconv = dyv * b_ref[...].astype(F32)
        dconv_n = jnp.where(i < last, dye[TR:] * bn_ref[...].astype(F32), 0.0)
        n1, n2 = _anticausal_taps(jnp.concatenate([dconv, dconv_n], axis=0), TR)
        dcu = (dconv * wv[2:3] + n1 * wv[1:2]) + n2 * wv[0:1]
        dz_ref[:, D:2 * D] = (dcu * uv).astype(BF16)
        dz_ref[:, 2 * D:3 * D] = (dcu * cv_).astype(BF16)
        part = jnp.concatenate([jnp.sum(dconv * x2, axis=0, keepdims=True),
                                jnp.sum(dconv * x1, axis=0, keepdims=True),
                                jnp.sum(dconv * x0, axis=0, keepdims=True)], axis=0)

        @pl.when(i == 0)
        def _():
            dw_ref[...] = part

        @pl.when(i > 0)
        def _():
            dw_ref[...] += part

    main = pl.BlockSpec((TR, D), lambda i: (i, 0))
    prev = pl.BlockSpec((HALO, D), lambda i: (_prev_idx(i), 0))
    nxt = pl.BlockSpec((HALO, D), lambda i: (_next_idx(i), 0))
    wspec = pl.BlockSpec((3, D), lambda i: (0, 0))
    return _pallas(
        body, name="d_l0_out", grid=(T // TR,),
        in_specs=[main, nxt, pl.BlockSpec((D, D), lambda i: (0, 0)), main, nxt, main, prev, main, prev, wspec],
        out_specs=[pl.BlockSpec((TR, 3 * D), lambda i: (i, 0)), wspec],
        out_shape=[jax.ShapeDtypeStruct((T, 3 * D), BF16), jax.ShapeDtypeStruct((3, D), F32)],
        params=_params(("arbitrary",)))(dh, dh, w_out, zb, zb, zc, zc, zu, zu, w)


def _sigmoid(x):
    return 0.5 * jnp.tanh(0.5 * x) + 0.5


def _ffn_up_act(name, hf, w_up, w, b):
    def body(h_ref, hh_ref, wg_ref, wv_ref, w_ref, b_ref, g_ref, v_ref, a_ref):
        i = pl.program_id(1)
        hv = h_ref[...]
        ge = lax.dot_general(jnp.concatenate([hh_ref[...], hv], axis=0), wg_ref[...], NT,
                             preferred_element_type=F32).astype(BF16)
        v = lax.dot_general(hv, wv_ref[...], NT, preferred_element_type=F32).astype(BF16)
        g_ref[...] = ge[HALO:]
        v_ref[...] = v
        ext = ge.astype(F32)
        row = lax.broadcasted_iota(jnp.int32, (HALO + TM, 1), 0)
        ext = jnp.where(jnp.logical_or(i > 0, row >= HALO), ext, 0.0)
        x2, x1, x0 = _causal_taps(ext)
        wv = w_ref[...]
        gc = ((x2 * wv[0:1] + x1 * wv[1:2]) + x0 * wv[2:3]) + b_ref[...]
        a_ref[...] = ((gc * _sigmoid(gc)) * v.astype(F32)).astype(BF16)

    blk = pl.BlockSpec((None, TM, FF_BLK), lambda j, i: (j, i, 0))
    out = jax.ShapeDtypeStruct((N_FF_BLK, T, FF_BLK), BF16)
    return _pallas(
        body, name=name, grid=(N_FF_BLK, T // TM),
        in_specs=[pl.BlockSpec((TM, D), lambda j, i: (i, 0)),
                  pl.BlockSpec((HALO, D), lambda j, i: (_prev_idx(i, TM), 0)),
                  pl.BlockSpec((None, None, FF_BLK, D), lambda j, i: (0, j, 0, 0)),
                  pl.BlockSpec((None, None, FF_BLK, D), lambda j, i: (0, j + N_FF_BLK, 0, 0)),
                  pl.BlockSpec((None, 3, FF_BLK), lambda j, i: (j, 0, 0)),
                  pl.BlockSpec((None, 1, FF_BLK), lambda j, i: (j, 0, 0))],
        out_specs=[blk, blk, blk], out_shape=[out, out, out],
        params=_params(("parallel", "parallel")))(hf, hf, w_up, w_up, w, b)


def _ffn_dact(name, dh, w_down4, g, v, w, b):
    last = T // TS - 1

    def body(dh_ref, dhn_ref, wd_ref, g_ref, gp_ref, gn_ref, v_ref, vn_ref, w_ref, b_ref, dg_ref, dv_ref, dw_ref, db_ref):
        i = pl.program_id(1)
        da = lax.dot_general(jnp.concatenate([dh_ref[...], dhn_ref[...]], axis=0), wd_ref[...], NT,
                             preferred_element_type=F32)
        row = lax.broadcasted_iota(jnp.int32, (TS + HALO, 1), 0)
        da = jnp.where(jnp.logical_or(i < last, row < TS), da, 0.0)
        gp = jnp.where(i > 0, gp_ref[...].astype(F32), 0.0)
        ext = jnp.concatenate([gp, g_ref[...].astype(F32), gn_ref[...].astype(F32)], axis=0)
        x2, x1, x0 = _causal_taps(ext)
        wv = w_ref[...]
        gc = ((x2 * wv[0:1] + x1 * wv[1:2]) + x0 * wv[2:3]) + b_ref[...]
        sg = _sigmoid(gc)
        vv = jnp.concatenate([v_ref[...].astype(F32), vn_ref[...].astype(F32)], axis=0)
        dv_ref[...] = (da[:TS] * (gc[:TS] * sg[:TS])).astype(BF16)
        dgc = (da * vv) * (sg * (1.0 + gc * (1.0 - sg)))
        n1, n2 = _anticausal_taps(dgc, TS)
        d0 = dgc[:TS]
        dg_ref[...] = ((d0 * wv[2:3] + n1 * wv[1:2]) + n2 * wv[0:1]).astype(BF16)
        part_w = jnp.concatenate([jnp.sum(d0 * x2[:TS], axis=0, keepdims=True),
                                  jnp.sum(d0 * x1[:TS], axis=0, keepdims=True),
                                  jnp.sum(d0 * x0[:TS], axis=0, keepdims=True)], axis=0)
        part_b = jnp.sum(d0, axis=0, keepdims=True)

        @pl.when(i == 0)
        def _():
            dw_ref[...] = part_w
            db_ref[...] = part_b

        @pl.when(i > 0)
        def _():
            dw_ref[...] += part_w
            db_ref[...] += part_b

    blk = pl.BlockSpec((None, TS, FF_BLK), lambda j, i: (j, i, 0))
    prev = pl.BlockSpec((None, HALO, FF_BLK), lambda j, i: (j, _prev_idx(i, TS), 0))
    nxt = pl.BlockSpec((None, HALO, FF_BLK), lambda j, i: (j, _next_idx(i, TS), 0))
    wspec = pl.BlockSpec((None, 3, FF_BLK), lambda j, i: (j, 0, 0))
    bspec = pl.BlockSpec((None, 1, FF_BLK), lambda j, i: (j, 0, 0))
    return _pallas(
        body, name=name, grid=(N_FF_BLK, T // TS),
        in_specs=[pl.BlockSpec((TS, D), lambda j, i: (i, 0)),
                  pl.BlockSpec((HALO, D), lambda j, i: (_next_idx(i, TS), 0)),
                  pl.BlockSpec((None, None, FF_BLK, D), lambda j, i: (0, j, 0, 0)),
                  blk, prev, nxt, blk, nxt, wspec, bspec],
        out_specs=[blk, blk, wspec, bspec],
        out_shape=[jax.ShapeDtypeStruct((N_FF_BLK, T, FF_BLK), BF16), jax.ShapeDtypeStruct((N_FF_BLK, T, FF_BLK), BF16),
                   jax.ShapeDtypeStruct((N_FF_BLK, 3, FF_BLK), F32), jax.ShapeDtypeStruct((N_FF_BLK, 1, FF_BLK), F32)],
        params=_params(("parallel", "arbitrary")))(dh, dh, w_down4, g, g, g, v, v, w, b)


def _rope_tables(pos, inv_freq):
    half = QK_ROPE // 2

    def body(p_ref, f_ref, c_ref, sa_ref, sb_ref):
        ang = p_ref[...].astype(F32) * f_ref[...]
        lane = lax.broadcasted_iota(jnp.int32, (T, 128), 1)
        c = jnp.cos(ang)
        s = jnp.sin(ang)
        c_ref[...] = jnp.where(lane < 2 * half, c, 0.0)
        sa_ref[...] = jnp.where(lane < half, -s, 0.0)
        sb_ref[...] = jnp.where(jnp.logical_and(lane >= half, lane < 2 * half), s, 0.0)

    return _pallas(
        body, name="rope_tables", in_specs=[VMEM_SPEC] * 2, out_specs=[VMEM_SPEC] * 3,
        out_shape=[jax.ShapeDtypeStruct((T, 128), F32)] * 3,
        params=pltpu.CompilerParams(vmem_limit_bytes=VMEM_LIMIT))(pos, inv_freq)


def _rotate(r, c, sa, sb, sign):
    return r * c + sign * (pltpu.roll(r, 96, 1) * sa + pltpu.roll(r, 32, 1) * sb)


def _attn_pre(h2, g_kv, g_l1, g_kvl, g_ql, w_dkv, w_kr, w_uk, w_uv, w_dq, w_uq, tables):
    def body(h_ref, c_ref, sa_ref, sb_ref, gkv_ref, gl1_ref, gkvl_ref, gql_ref, wdkv_ref, wkr_ref, wuk_ref, wuv_ref,
             wdq_ref, wuq_ref, hk_ref, hn_ref, ckvr_ref, ckv_ref, kr_ref, kn_ref, v_ref, cqr_ref, cq_ref, q_ref):
        xv = h_ref[...]
        xn = xv * lax.rsqrt(jnp.mean(xv * xv, axis=-1, keepdims=True) + EPS)
        hk = (xn * gkv_ref[...]).astype(BF16)
        hn = (xn * gl1_ref[...]).astype(BF16)
        hk_ref[...] = hk
        hn_ref[...] = hn
        cv, sav, sbv = c_ref[...], sa_ref[...], sb_ref[...]
        raw = lax.dot_general(hk, wdkv_ref[...], NN, preferred_element_type=F32)
        ckvr_ref[...] = raw
        ckv = _rms(raw, gkvl_ref[...]).astype(BF16)
        ckv_ref[...] = ckv
        kr = lax.dot_general(hk, wkr_ref[...], NT, preferred_element_type=F32)
        kr_ref[...] = _rotate(kr, cv, sav, sbv, 1.0).astype(BF16)
        kn_ref[...] = lax.dot_general(ckv, wuk_ref[...], NN, preferred_element_type=F32).astype(BF16)
        v_ref[...] = lax.dot_general(ckv, wuv_ref[...], NN, preferred_element_type=F32).astype(BF16)
        cqr = lax.dot_general(hn, wdq_ref[...], NN, preferred_element_type=F32)
        cqr_ref[...] = cqr
        cq = _rms(cqr, gql_ref[...]).astype(BF16)
        cq_ref[...] = cq
        for h in range(N_HEADS):
            r = lax.dot_general(cq, wuq_ref[h], NT, preferred_element_type=F32)
            q_ref[h, :, :QK_NOPE] = (r[:, :QK_NOPE] * SCALE2).astype(BF16)
            q_ref[h, :, QK_NOPE:] = (_rotate(r[:, QK_NOPE:], cv, sav, sbv, 1.0) * SCALE2).astype(BF16)

    def rows(d):
        return pl.BlockSpec((TS, d), lambda i: (i, 0))

    def whole(a):
        return pl.BlockSpec(a.shape, lambda i: (0,) * a.ndim)

    wholes = [g_kv, g_l1, g_kvl, g_ql, w_dkv, w_kr, w_uk, w_uv, w_dq, w_uq]
    outs = [(D, BF16), (D, BF16), (KV_LORA, F32), (KV_LORA, BF16), (128, BF16), (N_HEADS * QK_NOPE, BF16),
            (N_HEADS * V_HEAD, BF16), (Q_LORA, F32), (Q_LORA, BF16)]
    return _pallas(
        body, name="attn_pre", grid=(T // TS,),
        in_specs=[rows(D), rows(128), rows(128), rows(128)] + [whole(a) for a in wholes],
        out_specs=[rows(d) for d, _ in outs] + [pl.BlockSpec((N_HEADS, TS, QK_PAD), lambda i: (0, i, 0))],
        out_shape=[jax.ShapeDtypeStruct((T, d), dt) for d, dt in outs]
        + [jax.ShapeDtypeStruct((N_HEADS, T, QK_PAD), BF16)],
        params=_params(("parallel",)))(h2, *tables, *wholes)


SCALE = (QK_NOPE + QK_ROPE) ** -0.5
LOG2E = 1.4426950408889634
SCALE2 = SCALE * LOG2E


def _diag_mask(transposed):
    shift = CHUNK.bit_length() - 1
    a = lax.broadcasted_iota(jnp.int32, (TQ, TQ), 0) >> shift
    b = lax.broadcasted_iota(jnp.int32, (TQ, TQ), 1) >> shift
    return (a <= b) if transposed else (b <= a)


def _as_row(col):
    return jnp.transpose(jnp.broadcast_to(col, (col.shape[0], 128)), (1, 0))[0:1]


def _keys(kn_ref, kr_ref, off):
    return jnp.concatenate([kn_ref[pl.ds(off, TQ), :], kr_ref[pl.ds(off, TQ), :]], axis=1)


def _attn_fwd(q, kn, kr, v):
    hp = 2

    def body(q_ref, kn_ref, kr_ref, v_ref, o_ref, lse_ref):
        i = pl.program_id(1)
        qs = [q_ref[a] for a in range(hp)]

        def step(j, carry, masked):
            off = pl.multiple_of(j * TQ, TQ)
            krv = kr_ref[pl.ds(off, TQ), :]
            ss = []
            for a in range(hp):
                kk = jnp.concatenate([kn_ref[pl.ds(off, TQ), a * QK_NOPE:(a + 1) * QK_NOPE], krv], axis=1)
                ss.append(lax.dot_general(qs[a], kk, NT, preferred_element_type=F32))
            out = []
            for a in range(hp):
                m, l, acc = carry[a]
                s = ss[a]
                if masked:
                    s = jnp.where(_diag_mask(False), s, NEG_INF)
                m_new = jnp.maximum(m, jnp.max(s, axis=-1, keepdims=True))
                p = jnp.exp2(s - m_new)
                alpha = jnp.exp2(m - m_new)
                l = alpha * l + jnp.sum(p, axis=-1, keepdims=True)
                pv = lax.dot_general(p.astype(BF16), v_ref[pl.ds(off, TQ), a * V_HEAD:(a + 1) * V_HEAD], NN,
                                     preferred_element_type=F32)
                out.append((m_new, l, alpha * acc + pv))
            return tuple(out)

        one = (jnp.full((TQ, 1), NEG_INF, F32), jnp.zeros((TQ, 1), F32), jnp.zeros((TQ, V_HEAD), F32))
        carry = lax.fori_loop(0, i, lambda j, cr: step(j, cr, False), (one,) * hp)
        carry = step(i, carry, True)
        for a, (m, l, acc) in enumerate(carry):
            o_ref[:, a * V_HEAD:(a + 1) * V_HEAD] = (acc / l).astype(BF16)
            lse_ref[a] = _as_row(m + jnp.log(l) * LOG2E)

    return _pallas(
        body, name="attn_fwd", grid=(N_HEADS // hp, T // TQ),
        in_specs=[pl.BlockSpec((hp, TQ, QK_PAD), lambda h, i: (h, i, 0)),
                  pl.BlockSpec((T, hp * QK_NOPE), lambda h, i: (0, h)),
                  pl.BlockSpec((T, 128), lambda h, i: (0, 0)),
                  pl.BlockSpec((T, hp * V_HEAD), lambda h, i: (0, h))],
        out_specs=[pl.BlockSpec((TQ, hp * V_HEAD), lambda h, i: (i, h)), pl.BlockSpec((hp, 1, TQ), lambda h, i: (h, 0, i))],
        out_shape=[jax.ShapeDtypeStruct((T, N_HEADS * V_HEAD), BF16), jax.ShapeDtypeStruct((N_HEADS, 1, T), F32)],
        params=_params(("parallel", "parallel")))(q, kn, kr, v)


def _attn_bwd(q, kn, kr, v, o, do, lse_row, tables):
    nq = T // TQ
    hp = 2
    cos, sa, sb = tables

    def body(q_ref, kn_ref, kr_ref, v_ref, o_ref, do_ref, lse_ref, c_ref, sa_ref, sb_ref,
             dq_ref, dkn_ref, dkr_ref, dv_ref, dq_acc, dl_ref):
        j = pl.program_id(1)

        def cols(a):
            return slice(a * 128, (a + 1) * 128)

        @pl.when(j == 0)
        def _():
            dq_acc[...] = jnp.zeros_like(dq_acc)
            for a in range(hp):
                for i in range(nq):
                    rows = pl.ds(i * TQ, TQ)
                    prod = do_ref[rows, cols(a)].astype(F32) * o_ref[rows, cols(a)].astype(F32)
                    dl_ref[a, :, rows] = _as_row(jnp.sum(prod, axis=-1, keepdims=True))

        krv = kr_ref[...]
        kks = [jnp.concatenate([kn_ref[:, cols(a)], krv], axis=1) for a in range(hp)]
        vvs = [v_ref[:, cols(a)] for a in range(hp)]

        def step(i, carry, masked):
            off = pl.multiple_of(i * TQ, TQ)
            rows = pl.ds(off, TQ)
            qis = [q_ref[a, rows, :] for a in range(hp)]
            dois = [do_ref[rows, cols(a)] for a in range(hp)]
            sts = [lax.dot_general(kks[a], qis[a], NT, preferred_element_type=F32) for a in range(hp)]
            dpts = [lax.dot_general(vvs[a], dois[a], NT, preferred_element_type=F32) for a in range(hp)]
            out = []
            for a in range(hp):
                dk, dv = carry[a]
                st = sts[a]
                if masked:
                    st = jnp.where(_diag_mask(True), st, NEG_INF)
                pt = jnp.exp2(st - lse_ref[a, :, rows])
                dv = dv + lax.dot_general(pt.astype(BF16), dois[a], NN, preferred_element_type=F32)
                dst = (pt * (dpts[a] - dl_ref[a, :, rows])).astype(BF16)
                dk = dk + lax.dot_general(dst, qis[a], NN, preferred_element_type=F32)
                dq_acc[a, rows, :] += lax.dot_general(dst, kks[a], TN, preferred_element_type=F32)
                out.append((dk, dv))
            return tuple(out)

        zero = (jnp.zeros((TQ, QK_PAD), F32), jnp.zeros((TQ, V_HEAD), F32))
        carry = step(j, (zero,) * hp, True)
        carry = lax.fori_loop(j + 1, nq, lambda i, cr: step(i, cr, False), carry)
        for a, (dk, dv) in enumerate(carry):
            dk = dk * (SCALE / SCALE2)
            dkn_ref[:, cols(a)] = dk[:, :QK_NOPE].astype(BF16)
            dkr_ref[a] = dk[:, QK_NOPE:]
            dv_ref[:, cols(a)] = dv.astype(BF16)

        @pl.when(j == nq - 1)
        def _():
            for a in range(hp):
                dq = dq_acc[a] * SCALE
                dq_ref[a, :, :QK_NOPE] = dq[:, :QK_NOPE].astype(BF16)
                dq_ref[a, :, QK_NOPE:] = _rotate(dq[:, QK_NOPE:], c_ref[...], sa_ref[...], sb_ref[...], -1.0).astype(BF16)

    row = pl.BlockSpec((hp, 1, T), lambda h, j: (h, 0, 0))
    head = pl.BlockSpec((TQ, hp * 128), lambda h, j: (j, h))
    whole = pl.BlockSpec((hp, T, QK_PAD), lambda h, j: (h, 0, 0))
    tab = pl.BlockSpec((T, 128), lambda h, j: (0, 0))
    heads = pl.BlockSpec((T, hp * V_HEAD), lambda h, j: (0, h))
    return _pallas(
        body, name="attn_bwd", grid=(N_HEADS // hp, nq),
        in_specs=[whole, head, pl.BlockSpec((TQ, 128), lambda h, j: (j, 0)), head, heads, heads, row, tab, tab, tab],
        out_specs=[whole, head, pl.BlockSpec((hp, TQ, 128), lambda h, j: (h, j, 0)), head],
        out_shape=[jax.ShapeDtypeStruct((N_HEADS, T, QK_PAD), BF16), jax.ShapeDtypeStruct((T, N_HEADS * QK_NOPE), BF16),
                   jax.ShapeDtypeStruct((N_HEADS, T, 128), F32), jax.ShapeDtypeStruct((T, N_HEADS * V_HEAD), BF16)],
        scratch_shapes=[pltpu.VMEM((hp, T, QK_PAD), F32), pltpu.VMEM((hp, 1, T), F32)],
        params=_params(("parallel", "arbitrary")))(q, kn, kr, v, o, do, lse_row, cos, sa, sb)


def _rms_bwd_math(xv, g, dy):
    r = lax.rsqrt(jnp.mean(xv * xv, axis=-1, keepdims=True) + EPS)
    xn = xv * r
    gdy = dy * g
    return r * (gdy - xn * jnp.mean(gdy * xn, axis=-1, keepdims=True)), jnp.sum(dy * xn, axis=0, keepdims=True)


def _attn_post(dq, dkn, dv, dkr, cq_raw, ckv_raw, h2, dres, g_ql, g_kvl, g_l1, g_kv, w_uq, w_uk, w_uv, w_dq, w_dkv,
               w_kr, tables):
    def body(dq_ref, dkn_ref, dv_ref, dkr_ref, cqr_ref, ckvr_ref, h_ref, res_ref, c_ref, sa_ref, sb_ref,
             gql_ref, gkvl_ref, gl1_ref, gkv_ref, wuq_ref, wuk_ref, wuv_ref, wdq_ref, wdkv_ref, wkr_ref,
             dcq_ref, dckv_ref, dkrr_ref, dh_ref, dhb_ref, dgql_ref, dgkvl_ref, dgl1_ref, dgkv_ref):
        dcq = lax.dot_general(dq_ref[0], wuq_ref[0], NN, preferred_element_type=F32)
        for h in range(1, N_HEADS):
            dcq = dcq + lax.dot_general(dq_ref[h], wuq_ref[h], NN, preferred_element_type=F32)
        dcq_raw, s_ql = _rms_bwd_math(cqr_ref[...], gql_ref[...], dcq)
        dcq_raw = dcq_raw.astype(BF16)
        dcq_ref[...] = dcq_raw
        dckv = (lax.dot_general(dkn_ref[...], wuk_ref[...], NT, preferred_element_type=F32)
                + lax.dot_general(dv_ref[...], wuv_ref[...], NT, preferred_element_type=F32))
        dckv_raw, s_kvl = _rms_bwd_math(ckvr_ref[...], gkvl_ref[...], dckv)
        dckv_raw = dckv_raw.astype(BF16)
        dckv_ref[...] = dckv_raw
        dkr = dkr_ref[0]
        for h in range(1, N_HEADS):
            dkr = dkr + dkr_ref[h]
        dkr_raw = _rotate(dkr, c_ref[...], sa_ref[...], sb_ref[...], -1.0).astype(BF16)
        dkrr_ref[...] = dkr_raw
        d_hn = lax.dot_general(dcq_raw, wdq_ref[...], NT, preferred_element_type=F32)
        d_hk = (lax.dot_general(dckv_raw, wdkv_ref[...], NT, preferred_element_type=F32)
                + lax.dot_general(dkr_raw, wkr_ref[...], NN, preferred_element_type=F32))
        xv = h_ref[...]
        r = lax.rsqrt(jnp.mean(xv * xv, axis=-1, keepdims=True) + EPS)
        xn = xv * r
        dx = res_ref[...]
        sums = [s_ql, s_kvl]
        for dy, g_ref in ((d_hn, gl1_ref), (d_hk, gkv_ref)):
            gdy = dy * g_ref[...]
            dx = dx + r * (gdy - xn * jnp.mean(gdy * xn, axis=-1, keepdims=True))
            sums.append(jnp.sum(dy * xn, axis=0, keepdims=True))
        dh_ref[...] = dx
        dhb_ref[...] = dx.astype(BF16)
        dg_refs = (dgql_ref, dgkvl_ref, dgl1_ref, dgkv_ref)

        @pl.when(pl.program_id(0) == 0)
        def _():
            for dg_ref, part in zip(dg_refs, sums):
                dg_ref[...] = part

        @pl.when(pl.program_id(0) > 0)
        def _():
            for dg_ref, part in zip(dg_refs, sums):
                dg_ref[...] += part

    def rows(d):
        return pl.BlockSpec((TS, d), lambda i: (i, 0))

    def heads(d):
        return pl.BlockSpec((N_HEADS, TS, d), lambda i: (0, i, 0))

    def whole(a):
        return pl.BlockSpec(a.shape, lambda i: (0,) * a.ndim)

    wholes = [g_ql, g_kvl, g_l1, g_kv, w_uq, w_uk, w_uv, w_dq, w_dkv, w_kr]
    vecs = [Q_LORA, KV_LORA, D, D]
    return _pallas(
        body, name="attn_post", grid=(T // TS,),
        in_specs=[heads(QK_PAD), rows(N_HEADS * QK_NOPE), rows(N_HEADS * V_HEAD), heads(128), rows(Q_LORA),
                  rows(KV_LORA), rows(D), rows(D), rows(128), rows(128), rows(128)] + [whole(a) for a in wholes],
        out_specs=[rows(Q_LORA), rows(KV_LORA), rows(128), rows(D), rows(D)]
        + [pl.BlockSpec((1, d), lambda i: (0, 0)) for d in vecs],
        out_shape=[jax.ShapeDtypeStruct((T, Q_LORA), BF16), jax.ShapeDtypeStruct((T, KV_LORA), BF16),
                   jax.ShapeDtypeStruct((T, 128), BF16), jax.ShapeDtypeStruct((T, D), F32),
                   jax.ShapeDtypeStruct((T, D), BF16)] + [jax.ShapeDtypeStruct((1, d), F32) for d in vecs],
        params=_params(("arbitrary",)))(dq, dkn, dv, dkr, cq_raw, ckv_raw, h2, dres, *tables, *wholes)


def _ffn_gup(name, dg, dv, hf):
    def body(dg_ref, dv_ref, hf_ref, o_ref):
        j = pl.program_id(0)

        @pl.when(j < N_FF_BLK)
        def _():
            o_ref[...] = lax.dot_general(dg_ref[...], hf_ref[...], TN, preferred_element_type=F32).astype(BF16)

        @pl.when(j >= N_FF_BLK)
        def _():
            o_ref[...] = lax.dot_general(dv_ref[...], hf_ref[...], TN, preferred_element_type=F32).astype(BF16)

    return _pallas(
        body, name=name, grid=(N_DEV,),
        in_specs=[pl.BlockSpec((None, T, FF_BLK), lambda j: (jnp.minimum(j, N_FF_BLK - 1), 0, 0)),
                  pl.BlockSpec((None, T, FF_BLK), lambda j: (jnp.maximum(j - N_FF_BLK, 0), 0, 0)),
                  pl.BlockSpec((T, D), lambda j: (0, 0))],
        out_specs=pl.BlockSpec((None, FF_BLK, D), lambda j: (j, 0, 0)),
        out_shape=jax.ShapeDtypeStruct((N_DEV, FF_BLK, D), BF16), params=_params(("parallel",)))(dg, dv, hf)


def _ffn_layer_fwd(tag, h, hf, ex, final=None):
    g, v, act = _ffn_up_act(f"{tag}_up", hf, ex.need(f"ffn_w_up{tag[1]}", hf), ex.need(f"ffn_cw{tag[1]}", hf),
                            ex.need(f"ffn_cb{tag[1]}", hf))
    ex.at(f"{tag}_up", act)
    if final is not None:
        return _down_final(act, ex.need(f"ffn_w_down{tag[1]}", act), h, *final), (hf, g, v, act)
    rows = pl.BlockSpec((TS, D), lambda i: (i, 0))
    out = _mm_sum(f"{tag}_down",
                  [(act, pl.BlockSpec((N_FF_BLK, TS, FF_BLK), lambda i: (0, i, 0)), ex.need(f"ffn_w_down{tag[1]}", act),
                    pl.BlockSpec((None, N_FF_BLK, FF_BLK, D), lambda i: (0, 0, 0, 0)), NN, 0)],
                  grid=(T // TS,), o_spec=rows, o_shape=(T, D), o_dtype=F32, add=h)
    ex.at(f"{tag}_down", out)
    return out, (hf, g, v, act)


def _ffn_layer_bwd(tag, h, gain, ex, saved, dh, dh_bf, post=None):
    hf, g, v, act = saved
    layer = tag[1]
    w_up, w_down4 = ex.need(f"ffn_w_up{layer}", dh_bf), ex.need(f"ffn_w_down{layer}", dh_bf)
    dg, dv, dcw, dcb = _ffn_dact(f"{tag}_dact", dh_bf, w_down4, g, v, ex.need(f"ffn_cw{layer}", dh_bf),
                                 ex.need(f"ffn_cb{layer}", dh_bf))
    ex.at(f"{tag}_dact", dg)
    g_down = _mm(f"{tag}_gdown", act, dh_bf, grid=(N_FF_BLK,),
                 a_spec=pl.BlockSpec((None, T, FF_BLK), lambda j: (j, 0, 0)),
                 b_spec=pl.BlockSpec((T, D), lambda j: (0, 0)),
                 o_spec=pl.BlockSpec((FF_BLK, D), lambda j: (j, 0)),
                 o_shape=(D_FF, D), o_dtype=BF16, dims=TN)
    g_up = _ffn_gup(f"{tag}_gup", dg, dv, hf)
    ex.grad("ffn_w_up", int(layer), g_up.reshape(1, N_DEV, FF_BLK, D))
    ex.grad("ffn_w_down", int(layer), g_down.reshape(1, N_DEV, D_FF // N_DEV, D))
    ex.at(f"{tag}_gup", g_up)
    part = pl.BlockSpec((N_FF_BLK, TR, FF_BLK), lambda i: (0, i, 0))
    dh_in, dh_in_bf, dgain, *onward = _mm_sum(
        f"{tag}_dhf",
        [(dg, part, w_up, pl.BlockSpec((None, N_FF_BLK, FF_BLK, D), lambda i: (0, 0, 0, 0)), NN, 0),
         (dv, part, w_up, pl.BlockSpec((None, N_FF_BLK, FF_BLK, D), lambda i: (0, 1, 0, 0)), NN, 0)],
        grid=(T // TR,), o_spec=pl.BlockSpec((TR, D), lambda i: (i, 0)), o_shape=(T, D), o_dtype=F32,
        norm_bwd=(h, [gain], dh), post=post)
    ex.at(f"{tag}_dhf", dh_in)
    return (dh_in, dh_in_bf, dgain[0], dcw, dcb, *onward)


def _local_step(x, pos, tgt, rep, ex):
    attn_norm, ffn_norm, final_norm = rep["attn_norm"], rep["ffn_norm"], rep["final_norm"]
    half = QK_ROPE // 2
    inv = 1.0 / (ROPE_THETA ** (jnp.arange(half, dtype=F32) / half))
    inv_freq = jnp.concatenate([inv, inv, jnp.zeros((128 - 2 * half,), F32)]).reshape(1, 128)
    tables = _rope_tables(pos, inv_freq)

    hn0 = _rms_fwd("l0_norm", x, attn_norm[0:1])
    ex.at("l0_norm", hn0)
    w_in = ex.need("sc_w_in", hn0)
    zb, zc, zu, y = _mixer_in(hn0, w_in, ex.need("sc_conv_w", hn0))
    ex.at("l0_in", y)
    h1 = _mm_rows("l0_out", y, ex.need("sc_w_out", y), NN, F32, D, tn=512, add=x)
    ex.at("l0_out", h1)
    h2, ffn0 = _ffn_layer_fwd("f0", h1, _rms_fwd("f0_norm", h1, ffn_norm[0:1]), ex)

    w_uq = ex.need("w_uq", h2)
    hk, hn1, ckv_raw, ckv, kr, kn, vv, cq_raw, cq, q = _attn_pre(
        h2, rep["kv_in_norm"], attn_norm[1:2], rep["kv_latent_norm"], rep["q_latent_norm"], ex.need("w_dkv", h2),
        ex.need("w_kr", h2), ex.need("w_uk", h2), ex.need("w_uv", h2), ex.need("w_dq", h2), w_uq, tables)

    o, lse = _attn_fwd(q, kn, kr, vv)
    ex.at("attn_fwd", o)
    w_o = ex.need("w_o", o)
    h3, hf1 = _out_norm("attn_out", o, w_o, h2, ffn_norm[1:2])
    (loss, dh4, dh4_bf, d_final), ffn1 = _ffn_layer_fwd("f1", h3, hf1, ex, final=(final_norm.reshape(1, D), tgt))

    dh3, dh3_bf, d_fn1, dcw1, dcb1, do = _ffn_layer_bwd("f1", h3, ffn_norm[1:2], ex, ffn1, dh4, dh4_bf,
                                                        post=(w_o, NT, N_HEADS * V_HEAD))
    ex.at("f1_bwd", dh3)

    dq_pre, dkn, dkr, dvv = _attn_bwd(q, kn, kr, vv, o, do, lse, tables)

    dcq_raw_bf, dckv_raw_bf, dkr_raw_bf, dh2, dh2_bf, d_qln, d_kvln, d_an1, d_kvin = _attn_post(
        dq_pre, dkn, dvv, dkr, cq_raw, ckv_raw, h2, dh3, rep["q_latent_norm"], rep["kv_latent_norm"], attn_norm[1:2],
        rep["kv_in_norm"], w_uq, ex.need("w_uk", dkn), ex.need("w_uv", dvv), ex.need("w_dq", dq_pre),
        ex.need("w_dkv", dkn), ex.need("w_kr", dkr), tables)
    g_uq, g_dq, g_o = _wgrads("g_q", [(dq_pre, cq, QK_NOPE + QK_ROPE), (hn1, dcq_raw_bf), (o, dh3_bf)])
    ex.grad("w_uq", None, g_uq.reshape(1, N_DEV, QK_NOPE + QK_ROPE, Q_LORA))
    ex.grad("w_dq", None, g_dq.reshape(1, N_DEV, D // N_DEV, Q_LORA))
    ex.grad("w_o", None, g_o.reshape(1, N_DEV, D // N_DEV, D))

    g_uk, g_uv, g_dkv, g_kr = _wgrads("g_kv", [(ckv, dkn), (ckv, dvv), (hk, dckv_raw_bf), (dkr_raw_bf, hk, QK_ROPE)])
    ex.grad("w_uk", None, g_uk)
    ex.grad("w_uv", None, g_uv)
    ex.grad("w_dkv", None, g_dkv.reshape(1, N_DEV, D // N_DEV, KV_LORA))
    ex.grad("w_kr", None, g_kr)
    ex.at("kv_bwd", dh2)

    dh1, dh1_bf, d_fn0, dcw0, dcb0 = _ffn_layer_bwd("f0", h1, ffn_norm[0:1], ex, ffn0, dh2, dh2_bf)
    ex.at("f0_bwd", dh1)

    ex.grad("sc_w_out", None, _mm_wgrad("g_sc_w_out", y, dh1_bf).reshape(1, N_DEV, D // N_DEV, D))
    dz, d_scw = _mixer_out_bwd(dh1_bf, ex.need("sc_w_out", dh1_bf), zb, zc, zu, ex.need("sc_conv_w", dh1_bf))
    g_in = _mm_wgrad("g_sc_w_in", hn0, dz)
    ex.grad("sc_w_in", None, g_in)
    ex.at("sc_bwd", g_in)
    ex.at("d_l0_in", g_in)
    w_in = ex.need("sc_w_in", dz)
    grad_x, _, (d_an0,) = _mm_sum(
        "d_l0_in", [(dz[None], pl.BlockSpec((1, TS, dz.shape[1]), lambda i: (0, i, 0)),
                     w_in[None], pl.BlockSpec((1,) + w_in.shape, lambda i: (0, 0, 0)), NT, 0)],
        norm_bwd=(x, [attn_norm[0:1]], dh1),
        grid=(T // TS,), o_spec=pl.BlockSpec((TS, D), lambda i: (i, 0)), o_shape=(T, D), o_dtype=F32)

    small = {
        "attn_norm": jnp.concatenate([d_an0, d_an1], axis=0),
        "ffn_norm": jnp.concatenate([d_fn0, d_fn1], axis=0),
        "final_norm": d_final.reshape(D),
        "kv_in_norm": d_kvin.reshape(D),
        "kv_latent_norm": d_kvln.reshape(KV_LORA),
        "q_latent_norm": d_qln,
        "ffn_conv_b": jnp.stack([dcb0, dcb1]).transpose(0, 2, 1, 3).reshape(2, D_FF),
        "sc_conv_w": d_scw,
        "ffn_conv_w": jnp.stack([dcw0, dcw1]).transpose(0, 2, 1, 3).reshape(2, 3, D_FF),
    }
    return loss, grad_x, small


def _place():
    return lax.axis_index("x"), lax.axis_index("y"), lax.axis_index("c")


def _peers():
    x, y, c = _place()
    return (x, y, 1 - c), [(1 - x, y), (x, 1 - y), (1 - x, 1 - y)]


def _window(ref, kind, dev):
    if kind == "blocked":
        return ref.at[:, dev]
    width = ref.shape[-1] // N_DEV
    return ref.at[:, pl.ds(pl.multiple_of(dev * width, 128), width)]


HBM_SPEC = pl.BlockSpec(memory_space=pltpu.HBM)
SEM_SPEC = pl.BlockSpec(memory_space=pltpu.SEMAPHORE)
EFFECT = pltpu.SideEffectType.DATAFLOW_SIDE_EFFECTING
TOKEN = jax.ShapeDtypeStruct((8, 128), F32)


def _hbm(a):
    return pltpu.with_memory_space_constraint(a, pltpu.HBM)


def _copies_start(name, jobs):
    nj = len(jobs)
    counts = [(len(srcs), len(lands)) for srcs, lands, _, _ in jobs]
    n_arr = sum(ns + nl for ns, nl in counts)

    def body(*refs):
        sems, token = refs[n_arr:n_arr + 2 * nj], refs[-1]
        at = 0
        for j, ((ns, nl), (_, _, ncopy, plan)) in enumerate(zip(counts, jobs)):
            copies = plan(refs[at:at + ns], refs[at + ns:at + ns + nl])
            assert len(copies) == ncopy
            for k, (sent, dst, to, _) in enumerate(copies):
                pltpu.make_async_remote_copy(src_ref=sent, dst_ref=dst, send_sem=sems[2 * j].at[k],
                                             recv_sem=sems[2 * j + 1].at[k], device_id=to, device_id_type=MESH).start()
            at += ns + nl
        token[...] = jnp.zeros_like(token)

    arrays = [a for srcs, lands, _, _ in jobs for a in list(srcs) + list(lands)]
    sem_shapes = [pltpu.SemaphoreType.DMA((ncopy,)) for _, _, ncopy, _ in jobs for _ in range(2)]
    outs = pl.pallas_call(
        body, name=name, in_specs=[HBM_SPEC] * n_arr,
        out_specs=[SEM_SPEC] * (2 * nj) + [HBM_SPEC] * n_arr + [VMEM_SPEC],
        out_shape=sem_shapes + [pltpu.HBM(a.shape, a.dtype) for a in arrays] + [TOKEN],
        input_output_aliases={i: 2 * nj + i for i in range(n_arr)},
        compiler_params=pltpu.CompilerParams(has_side_effects=EFFECT))(*[_hbm(a) for a in arrays])
    _Chain.last = outs[-1]
    flights, at = [], 2 * nj
    for j, (ns, nl) in enumerate(counts):
        flights.append((outs[2 * j], outs[2 * j + 1], list(outs[at:at + ns]), list(outs[at + ns:at + ns + nl])))
        at += ns + nl
    return flights


def _copies_wait(name, started, ncopy, plan):
    send, recv, srcs, lands = started
    ns, nl = len(srcs), len(lands)

    def body(*refs):
        send_ref, recv_ref, token = refs[ns + nl], refs[ns + nl + 1], refs[-1]
        copies = plan(refs[:ns], refs[ns:ns + nl])
        assert len(copies) == ncopy
        for k, (sent, _, to, landed) in enumerate(copies):
            cp = pltpu.make_async_remote_copy(src_ref=sent, dst_ref=landed, send_sem=send_ref.at[k],
                                              recv_sem=recv_ref.at[k], device_id=to, device_id_type=MESH)
            cp.wait_send()
            cp.wait_recv()
        token[...] = jnp.zeros_like(token)

    arrays = list(srcs) + list(lands)
    outs = pl.pallas_call(
        body, name=name, in_specs=[HBM_SPEC] * (ns + nl) + [SEM_SPEC] * 2 + [ANY_SPEC],
        out_specs=[HBM_SPEC] * (ns + nl) + [VMEM_SPEC], out_shape=[pltpu.HBM(a.shape, a.dtype) for a in arrays] + [TOKEN],
        input_output_aliases={i: i for i in range(ns + nl)},
        compiler_params=pltpu.CompilerParams(has_side_effects=EFFECT))(*arrays, send, recv, _Chain.last)
    _Chain.last = outs[-1]
    return list(outs[:ns]), list(outs[ns:-1])


def _plan_gather_chips(kinds):
    def plan(srcs, lands):
        x, y, c = _place()
        sibling, chips = _peers()
        out = []
        for t, kind in enumerate(kinds):
            mine = _window(lands[t], kind, 4 * x + 2 * y + c)
            out.append((srcs[t], mine, (x, y, c), mine))
            out.append((srcs[t], mine, sibling, _window(lands[t], kind, 4 * x + 2 * y + 1 - c)))
            for px, py in chips:
                out.append((srcs[t], mine, (px, py, c), _window(lands[t], kind, 4 * px + 2 * py + c)))
        return out
    return plan, 5 * len(kinds)


def _plan_gather_all(n):
    def plan(srcs, lands):
        x, y, c = _place()
        out = []
        for t in range(n):
            mine = lands[t].at[:, 4 * x + 2 * y + c]
            for m in range(N_DEV):
                px, py, pc = (1 - x if m & 4 else x), (1 - y if m & 2 else y), (1 - c if m & 1 else c)
                out.append((srcs[t], mine, (px, py, pc), lands[t].at[:, 4 * px + 2 * py + pc]))
        return out
    return plan, N_DEV * n


def _plan_gather_sibling(kinds):
    def plan(srcs, lands):
        _, _, c = _place()
        sibling, chips = _peers()
        out = []
        for t, kind in enumerate(kinds):
            for px, py in chips:
                w = _window(lands[t], kind, 4 * px + 2 * py + c)
                out.append((w, w, sibling, _window(lands[t], kind, 4 * px + 2 * py + 1 - c)))
        return out
    return plan, 3 * len(kinds)


def _plan_scatter_sibling(kinds):
    def plan(srcs, lands):
        _, _, c = _place()
        sibling, _ = _peers()
        out = []
        for t, kind in enumerate(kinds):
            for k in range(N_CHIP):
                out.append((_window(srcs[t], kind, 2 * k + 1 - c), lands[t].at[k], sibling, lands[t].at[k]))
        return out
    return plan, N_CHIP * len(kinds)


def _plan_scatter_chips(n):
    def plan(srcs, lands):
        x, y, c = _place()
        _, chips = _peers()
        out = []
        for t in range(n):
            for px, py in chips:
                out.append((srcs[t].at[2 * px + py], lands[t].at[2 * x + y], (px, py, c), lands[t].at[2 * px + py]))
        return out
    return plan, 3 * n


def _landing(shard, kind):
    if kind == "blocked":
        return lax.empty((shard.shape[0], N_DEV) + shard.shape[1:], shard.dtype)
    return lax.empty((shard.shape[0], N_DEV * shard.shape[1]), shard.dtype)


def _chip_sums(name, grads, kinds, recvs, c):
    n = len(grads)
    in_specs, out_specs, out_shape, args = [], [], [], []
    for gr, kind, rv in zip(grads, kinds, recvs):
        if kind == "blocked":
            rows, w = gr.shape[2], gr.shape[3]
            in_specs.append(pl.BlockSpec((None, None, rows, w), lambda k, cref: (0, 2 * k + cref[0], 0, 0)))
        else:
            rows, w = gr.shape[0], gr.shape[1] // N_DEV
            in_specs.append(pl.BlockSpec((rows, w), lambda k, cref: (0, 2 * k + cref[0])))
        blk = pl.BlockSpec((None, rows, w), lambda k, cref: (k, 0, 0))
        in_specs.append(blk)
        out_specs.append(blk)
        out_shape.append(jax.ShapeDtypeStruct((N_CHIP, rows, w), BF16))
        args += [gr, rv.reshape(N_CHIP, rows, w)]

    def body(*refs):
        for t in range(n):
            g_ref, r_ref, o_ref = refs[1 + 2 * t], refs[2 + 2 * t], refs[1 + 2 * n + t]
            o_ref[...] = (g_ref[...].astype(F32) + r_ref[...].astype(F32)).astype(BF16)

    return _pallas(body, name=name, n_prefetch=1, grid=(N_CHIP,), in_specs=in_specs, out_specs=out_specs,
                   out_shape=out_shape, params=_params(("parallel",)))(c, *args)


def _adamw_math(g, wv, mv, vv):
    m = ADAM_B1 * mv + (1.0 - ADAM_B1) * g
    v = ADAM_B2 * vv + (1.0 - ADAM_B2) * (g * g)
    m_hat = m / (1.0 - ADAM_B1 ** ADAM_STEP)
    v_hat = v / (1.0 - ADAM_B2 ** ADAM_STEP)
    delta = -ADAM_LR * (m_hat / (jnp.sqrt(v_hat) + ADAM_EPS) + ADAM_WD * wv)
    return delta, m, v


ADAM_STEPS = 2


def _adamw_group(name, items, chip_ids):
    n = len(items)
    in_specs, out_specs, out_shape, args, prevs = [], [], [], [chip_ids], []
    for own, recv, w3, m3, v3, layer, _ in items:
        nl, rows, w = w3.shape
        tr = rows // ADAM_STEPS
        assert tr % 16 == 0, (name, rows)
        in_specs += [pl.BlockSpec((None, tr, w), lambda i, ids, slot=slot: (ids[slot], i, 0)) for slot in range(4)]
        slab = pl.BlockSpec((None, tr, w), lambda i, ids, layer=layer: (layer, i, 0))
        in_specs += [slab] * 3
        out_specs += [slab] * 4
        out_shape += [jax.ShapeDtypeStruct((nl, rows, w), F32)] * 4
        args += [own, recv, recv, recv, w3, m3, v3]
    aliases = {}
    for t, item in enumerate(items):
        if item[6] is not None:
            for k in range(4):
                aliases[len(args) + k] = 4 * t + k
            in_specs += [ANY_SPEC] * 4
            args += list(item[6])
            prevs.append(t)
    n_in = 1 + 7 * n + 4 * len(prevs)

    def body(*refs):
        for t in range(n):
            own_ref, r1_ref, r2_ref, r3_ref, w_ref, m_ref, v_ref = refs[1 + 7 * t:8 + 7 * t]
            g_ref, d_ref, nm_ref, nv_ref = refs[n_in + 4 * t:n_in + 4 * t + 4]
            g = ((own_ref[...].astype(F32) + r1_ref[...].astype(F32)) + r2_ref[...].astype(F32)) + r3_ref[...].astype(F32)
            g_ref[...] = g
            d_ref[...], nm_ref[...], nv_ref[...] = _adamw_math(g, w_ref[...], m_ref[...], v_ref[...])

    outs = _pallas(body, name=name, n_prefetch=1, grid=(ADAM_STEPS,), in_specs=in_specs, out_specs=out_specs,
                   out_shape=out_shape, aliases=aliases, params=_params(("parallel",)))(*args)
    return [list(outs[4 * t:4 * t + 4]) for t in range(n)]


SHARD_ROWS = 8


def _adamw_small(gathered, ws, ms, vs, me):
    n = len(gathered)
    full = [w is not None for w in ws]
    sharded = [w is not None and w.ndim == 3 for w in ws]
    args = list(gathered)
    out_shape = []
    for t in range(n):
        shape = jax.ShapeDtypeStruct(ws[t].shape if sharded[t] else gathered[t].shape[2:], F32)
        if full[t]:
            args += [ws[t], ms[t], vs[t]]
            out_shape += [shape] * 4
        else:
            out_shape += [shape]

    def body(*refs):
        i_in, i_out = n, len(args) + 1
        me_ref = refs[len(args)]
        for t in range(n):
            p_ref = refs[t]
            if sharded[t]:
                w_ref, m_ref, v_ref = refs[i_in:i_in + 3]
                taps, layers, _ = w_ref.shape
                mine = pl.ds(pl.multiple_of(me_ref[0] * SHARD_ROWS, SHARD_ROWS), SHARD_ROWS)
                g = p_ref[0, 0, mine, :]
                for k in range(1, N_DEV):
                    g = g + p_ref[0, k, mine, :]
                for l in range(layers):
                    for k in range(taps):
                        at = (k, slice(l, l + 1), slice(None))
                        row = g[l * taps + k:l * taps + k + 1]
                        refs[i_out][at] = row
                        refs[i_out + 1][at], refs[i_out + 2][at], refs[i_out + 3][at] = _adamw_math(
                            row, w_ref[at], m_ref[at], v_ref[at])
                i_in += 3
                i_out += 4
                continue
            g = p_ref[0, 0]
            for k in range(1, N_DEV):
                g = g + p_ref[0, k]
            refs[i_out][...] = g
            if full[t]:
                w_ref, m_ref, v_ref = refs[i_in:i_in + 3]
                refs[i_out + 1][...], refs[i_out + 2][...], refs[i_out + 3][...] = _adamw_math(
                    g, w_ref[...], m_ref[...], v_ref[...])
                i_in += 3
                i_out += 4
            else:
                i_out += 1

    outs = _pallas(body, name="adamw_small",
                   in_specs=[VMEM_SPEC] * len(args) + [pl.BlockSpec(memory_space=pltpu.SMEM)],
                   out_specs=[VMEM_SPEC] * len(out_shape), out_shape=out_shape,
                   params=pltpu.CompilerParams(vmem_limit_bytes=VMEM_LIMIT))(*args, me)
    result, i = [], 0
    for t in range(n):
        k = 4 if full[t] else 1
        result.append(list(outs[i:i + k]))
        i += k
    return result


KIND = {"sc_w_in": "cols", "sc_w_out": "blocked", "w_dkv": "blocked", "w_kr": "cols", "w_uk": "cols", "w_uv": "cols",
        "w_dq": "blocked", "w_uq": "blocked", "w_o": "blocked", "ffn_w_up": "blocked", "ffn_w_down": "blocked",
        "conv": "blocked"}
GATHER_GROUPS = (("mixer", ("sc_w_in", "sc_w_out", "conv")),
                 ("up0", ("ffn_w_up0",)),
                 ("down0", ("ffn_w_down0",)),
                 ("attn", ("w_dkv", "w_kr", "w_uk", "w_uv", "w_dq", "w_uq", "w_o")),
                 ("ffn1", ("ffn_w_up1", "ffn_w_down1")))
SCATTER_GROUPS = (("ffn1", (("ffn_w_up", 1), ("ffn_w_down", 1))),
                  ("attn", (("w_o", None), ("w_uq", None), ("w_dq", None), ("w_uk", None), ("w_uv", None),
                            ("w_dkv", None), ("w_kr", None))),
                  ("ffn0", (("ffn_w_up", 0), ("ffn_w_down", 0))),
                  ("mixer", (("sc_w_out", None), ("sc_w_in", None))))
SCHEDULE = {
    "begin": (("gather_start", "mixer"),),
    "l0_norm": (("gather_forward", "mixer"), ("gather_start", "up0")),
    "l0_out": (("gather_forward", "up0"), ("gather_start", "down0"), ("gather_start", "attn")),
    "f0_up": (("gather_forward", "down0"), ("gather_forward", "attn"), ("gather_start", "ffn1")),
    "attn_fwd": (("gather_forward", "ffn1"),),
    "f1_gup": (("scatter_sibling", "ffn1"),),
    "f1_dhf": (("scatter_chips", "ffn1"),),
    "kv_bwd": (("scatter_sibling", "attn"),),
    "f0_dact": (("scatter_chips", "attn"),),
    "f0_gup": (("scatter_sibling", "ffn0"),),
    "f0_dhf": (("scatter_chips", "ffn0"),),
    "sc_bwd": (("scatter_sibling", "mixer"), ("scatter_done", "attn")),
    "d_l0_in": (("scatter_chips", "mixer"),),
}
FINISH = (("scatter_done", "ffn1"), ("scatter_done", "ffn0"), ("scatter_done", "mixer"))
STAGES = {"gather_start": 1, "gather_forward": 2, "gather_done": 3,
          "scatter_sibling": 1, "scatter_chips": 2, "scatter_done": 3}
SMALL_W_ROWS = 24


def _pack(arrays, rows):
    flat = jnp.concatenate([a.reshape(-1).astype(F32) for a in arrays])
    return jnp.pad(flat, (0, rows * 128 - flat.shape[0])).reshape(rows, 128)


def _cast_shards(items):
    arrays = []
    for a, _, _ in items:
        if not any(a is b for b in arrays):
            arrays.append(a)
    slot = [next(i for i, b in enumerate(arrays) if b is a) for a, _, _ in items]

    def body(*refs):
        for t, (a, layer, rows) in enumerate(items):
            w_ref, o_ref = refs[slot[t]], refs[len(arrays) + t]
            r, c = a.shape[-2:]
            if a.ndim == 3:
                o_ref[:, :r] = w_ref[(layer or 0):(layer or 0) + 1].astype(BF16)
                if rows > r:
                    o_ref[:, r:] = jnp.zeros((1, rows - r, c), BF16)
            else:
                o_ref[:r] = w_ref[...].astype(BF16)
                if rows > r:
                    o_ref[r:] = jnp.zeros((rows - r, c), BF16)

    out_shape = [jax.ShapeDtypeStruct(((1,) if a.ndim == 3 else ()) + (rows, a.shape[-1]), BF16)
                 for a, _, rows in items]
    return _pallas(body, name="cast_shards", in_specs=[VMEM_SPEC] * len(arrays), out_specs=[VMEM_SPEC] * len(items),
                   out_shape=out_shape, params=pltpu.CompilerParams(vmem_limit_bytes=VMEM_LIMIT))(*arrays)


STORED_TRANSPOSED = ("ffn_w_up", "w_uq", "w_kr")


def _stored(name, a):
    return jnp.swapaxes(a, -1, -2) if name in STORED_TRANSPOSED else a


def _base(name):
    if name.startswith("ffn_w_") and name[-1] in "01":
        return name[:-1], int(name[-1])
    return name, None


class _Exchange:
    def __init__(self, wts, mom, var, ffn_conv_b):
        self.wts, self.mom, self.var = wts, mom, var
        x, y, c = _place()
        self.c_arr = jnp.reshape(c, (1,)).astype(jnp.int32)
        chip = 2 * x + y
        self.chip_ids = jnp.stack([chip, chip ^ 1, chip ^ 2, chip ^ 3]).astype(jnp.int32)
        self.ready = {"ffn_cb0": ffn_conv_b.reshape(2, N_FF_BLK, 1, FF_BLK)[0],
                      "ffn_cb1": ffn_conv_b.reshape(2, N_FF_BLK, 1, FF_BLK)[1]}
        self.gathers, self.group_of = {}, {}
        self.grads, self.scatters, self.results, self.queue = {}, {}, {}, []
        for gname, names in GATHER_GROUPS:
            self.gathers[gname] = dict(stage=0, names=names, kinds=[KIND[_base(nm)[0]] for nm in names])
            for nm in names:
                self.group_of[nm] = gname
        for nm in ("sc_conv_w", "ffn_cw0", "ffn_cw1"):
            self.group_of[nm] = "mixer"
        self.cast, self.f32 = {}, {}
        self.at("begin", None)
        later = [nm for gname, names in GATHER_GROUPS[1:] for nm in names]
        self.cast = dict(zip(later, _cast_shards([self._shard_f32(nm) for nm in later])))

    def _shard_f32(self, name):
        base, layer = _base(name)
        if base not in self.f32:
            a = _stored(base, self.wts[base])
            self.f32[base] = a.reshape(a.shape[-2:]) if KIND[base] == "cols" else a.reshape((-1,) + a.shape[-2:])
        a = self.f32[base]
        return a, layer, {"w_kr": 128, "w_uq": QK_PAD}.get(base, a.shape[-2])

    def _shard(self, name):
        if name in self.cast:
            return self.cast[name]
        if name == "conv":
            return _pack([self.wts["sc_conv_w"], self.wts["ffn_conv_w"]], SMALL_W_ROWS).reshape(1, SMALL_W_ROWS, 128)
        base, layer = _base(name)
        a = _stored(base, self.wts[base])
        if layer is not None:
            a = a[layer:layer + 1]
        if KIND[base] == "cols":
            return a.reshape(a.shape[-2], a.shape[-1]).astype(BF16)
        return a.reshape((-1,) + a.shape[-2:]).astype(BF16)

    def _start(self, name, srcs, lands, ncopy, plan, st):
        self.queue.append((name, (srcs, lands, ncopy, plan), st))

    def _flush(self):
        if self.queue:
            flights = _copies_start("__".join(name for name, _, _ in self.queue), [job for _, job, _ in self.queue])
            for (_, _, st), flight in zip(self.queue, flights):
                st["flight"] = flight
            self.queue = []

    def _flight(self, st):
        self._flush()
        return st["flight"]

    def _gather_to(self, gname, stage, after):
        st = self.gathers[gname]
        if st["stage"] < 1 <= stage:
            shards = [self._shard(nm) for nm in st["names"]]
            lands = [_landing(s, kind) for s, kind in zip(shards, st["kinds"])]
            plan, ncopy = _plan_gather_chips(st["kinds"])
            self._start(f"ag_{gname}_chips", shards, lands, ncopy, plan, st)
            st["stage"] = 1
        if st["stage"] < 2 <= stage:
            plan, ncopy = _plan_gather_chips(st["kinds"])
            _, lands = _copies_wait(f"ag_{gname}_chips_wait", self._flight(st), ncopy, plan)
            plan, ncopy = _plan_gather_sibling(st["kinds"])
            self._start(f"ag_{gname}_sibling", [], lands, ncopy, plan, st)
            st["stage"] = 2
        if st["stage"] < 3 <= stage:
            plan, ncopy = _plan_gather_sibling(st["kinds"])
            _, lands = _copies_wait(f"ag_{gname}_sibling_wait", self._flight(st), ncopy, plan)
            for nm, land in zip(st["names"], lands):
                self._arrived(nm, land)
            st["stage"] = 3

    def _arrived(self, name, land):
        if name == "conv":
            conv = land.reshape(N_DEV, SMALL_W_ROWS * 128)
            self.ready["sc_conv_w"] = conv[:, :3 * 128].reshape(N_DEV, 3, 128).transpose(1, 0, 2).reshape(3, D)
            fcw = conv[:, 3 * 128:3 * 128 + 6 * 352].reshape(N_DEV, 2, 3, 352).transpose(1, 2, 0, 3)
            fcw = fcw.reshape(2, 3, N_FF_BLK, FF_BLK).transpose(0, 2, 1, 3)
            self.ready["ffn_cw0"], self.ready["ffn_cw1"] = fcw[0], fcw[1]
        elif name in ("sc_w_in", "w_uk", "w_uv", "w_kr") or name.startswith("ffn_w_up"):
            self.ready[name] = land
        elif name.startswith("ffn_w_down"):
            self.ready[name] = land.reshape(1, N_FF_BLK, FF_BLK, D)
        elif name == "w_uq":
            self.ready[name] = land.reshape(N_HEADS, QK_PAD, Q_LORA)
        else:
            self.ready[name] = land.reshape(D, land.shape[-1])

    def need(self, name, after):
        if name not in self.ready:
            self._gather_to(self.group_of[name], 3, after)
            self._flush()
        return self.ready[name]

    def grad(self, name, layer, array):
        self.grads[(name, layer)] = array

    def _scatter_to(self, gname, stage, after):
        keys = dict(SCATTER_GROUPS)[gname]
        st = self.scatters.setdefault(gname, dict(stage=0))
        kinds = [KIND[nm] for nm, _ in keys]
        if st["stage"] < 1 <= stage:
            grads = [self.grads[key] for key in keys]
            lands = []
            for gr, kind in zip(grads, kinds):
                shard = (gr.shape[0],) + gr.shape[2:] if kind == "blocked" else (gr.shape[0], gr.shape[1] // N_DEV)
                lands.append(lax.empty((N_CHIP,) + shard, BF16))
            plan, ncopy = _plan_scatter_sibling(kinds)
            self._start(f"rs_{gname}_sibling", grads, lands, ncopy, plan, st)
            st["stage"] = 1
        if st["stage"] < 2 <= stage:
            plan, ncopy = _plan_scatter_sibling(kinds)
            grads, recvs = _copies_wait(f"rs_{gname}_sibling_wait", self._flight(st), ncopy, plan)
            sums = _chip_sums(f"rs_{gname}_sums", grads, kinds, recvs, self.c_arr)
            lands = [lax.empty(s.shape, BF16) for s in sums]
            plan, ncopy = _plan_scatter_chips(len(sums))
            self._start(f"rs_{gname}_chips", sums, lands, ncopy, plan, st)
            st["stage"] = 2
        if st["stage"] < 3 <= stage:
            plan, ncopy = _plan_scatter_chips(len(keys))
            sums, recvs = _copies_wait(f"rs_{gname}_chips_wait", self._flight(st), ncopy, plan)
            items = []
            for (nm, layer), own, rv in zip(keys, sums, recvs):
                nl = 1 if layer is None else 2
                rows, w = own.shape[1], own.shape[2]
                w3, m3, v3 = (_stored(nm, src[nm]).reshape(nl, rows, w) for src in (self.wts, self.mom, self.var))
                items.append((own, rv, w3, m3, v3, 0 if layer is None else layer, self.results.get(nm)))
            outs = _adamw_group(f"adamw_{gname}", items, self.chip_ids)
            for (nm, _), out in zip(keys, outs):
                self.results[nm] = out
            st["stage"] = 3

    def at(self, place, after):
        for action, gname in SCHEDULE.get(place, ()):
            self._advance(action, gname, after)
        self._flush()

    def _advance(self, action, gname, after):
        if action.startswith("gather"):
            self._gather_to(gname, STAGES[action], after)
        else:
            self._scatter_to(gname, STAGES[action], after)

    def finish(self, after):
        for action, gname in FINISH:
            self._advance(action, gname, after)
        for gname, _ in SCATTER_GROUPS:
            self._scatter_to(gname, 3, after)
        return {nm: [_stored(nm, o.reshape(_stored(nm, self.wts[nm]).shape)) for o in outs]
                for nm, outs in self.results.items()}


REPLICATED = ("attn_norm", "ffn_norm", "final_norm", "kv_in_norm", "kv_latent_norm", "q_latent_norm", "ffn_conv_b")
WEIGHTS = ("attn_norm", "ffn_norm", "final_norm", "sc_w_in", "sc_conv_w", "sc_w_out", "kv_in_norm", "w_dkv",
           "kv_latent_norm", "w_kr", "w_uk", "w_uv", "w_dq", "q_latent_norm", "w_uq", "w_o", "ffn_w_up", "ffn_conv_w",
           "ffn_conv_b", "ffn_w_down")


def kernel(x, positions, attn_norm, ffn_norm, final_norm, sc_w_in, sc_conv_w, sc_w_out, kv_in_norm, w_dkv, kv_latent_norm, w_kr, w_uk, w_uv, w_dq, q_latent_norm, w_uq, w_o, ffn_w_up, ffn_conv_w, ffn_conv_b, ffn_w_down, loss_target, m_attn_norm, m_ffn_norm, m_final_norm, m_sc_w_in, m_sc_conv_w, m_sc_w_out, m_kv_in_norm, m_w_dkv, m_kv_latent_norm, m_w_kr, m_w_uk, m_w_uv, m_w_dq, m_q_latent_norm, m_w_uq, m_w_o, m_ffn_w_up, m_ffn_conv_w, m_ffn_conv_b, m_ffn_w_down, v_attn_norm, v_ffn_norm, v_final_norm, v_sc_w_in, v_sc_conv_w, v_sc_w_out, v_kv_in_norm, v_w_dkv, v_kv_latent_norm, v_w_kr, v_w_uk, v_w_uv, v_w_dq, v_q_latent_norm, v_w_uq, v_w_o, v_ffn_w_up, v_ffn_conv_w, v_ffn_conv_b, v_ffn_w_down):
    wts = dict(attn_norm=attn_norm, ffn_norm=ffn_norm, final_norm=final_norm, sc_w_in=sc_w_in, sc_conv_w=sc_conv_w,
               sc_w_out=sc_w_out, kv_in_norm=kv_in_norm, w_dkv=w_dkv, kv_latent_norm=kv_latent_norm, w_kr=w_kr,
               w_uk=w_uk, w_uv=w_uv, w_dq=w_dq, q_latent_norm=q_latent_norm, w_uq=w_uq, w_o=w_o, ffn_w_up=ffn_w_up,
               ffn_conv_w=ffn_conv_w, ffn_conv_b=ffn_conv_b, ffn_w_down=ffn_w_down)
    mom = dict(attn_norm=m_attn_norm, ffn_norm=m_ffn_norm, final_norm=m_final_norm, sc_w_in=m_sc_w_in,
               sc_conv_w=m_sc_conv_w, sc_w_out=m_sc_w_out, kv_in_norm=m_kv_in_norm, w_dkv=m_w_dkv,
               kv_latent_norm=m_kv_latent_norm, w_kr=m_w_kr, w_uk=m_w_uk, w_uv=m_w_uv, w_dq=m_w_dq,
               q_latent_norm=m_q_latent_norm, w_uq=m_w_uq, w_o=m_w_o, ffn_w_up=m_ffn_w_up, ffn_conv_w=m_ffn_conv_w,
               ffn_conv_b=m_ffn_conv_b, ffn_w_down=m_ffn_w_down)
    var = dict(attn_norm=v_attn_norm, ffn_norm=v_ffn_norm, final_norm=v_final_norm, sc_w_in=v_sc_w_in,
               sc_conv_w=v_sc_conv_w, sc_w_out=v_sc_w_out, kv_in_norm=v_kv_in_norm, w_dkv=v_w_dkv,
               kv_latent_norm=v_kv_latent_norm, w_kr=v_w_kr, w_uk=v_w_uk, w_uv=v_w_uv, w_dq=v_w_dq,
               q_latent_norm=v_q_latent_norm, w_uq=v_w_uq, w_o=v_w_o, ffn_w_up=v_ffn_w_up, ffn_conv_w=v_ffn_conv_w,
               ffn_conv_b=v_ffn_conv_b, ffn_w_down=v_ffn_w_down)
    xi, yi, ci = _place()
    me = 4 * xi + 2 * yi + ci
    _Chain.last = None

    ex = _Exchange(wts, mom, var, ffn_conv_b)
    rep = {
        "attn_norm": attn_norm, "ffn_norm": ffn_norm, "final_norm": final_norm,
        "kv_in_norm": kv_in_norm.reshape(1, D), "kv_latent_norm": kv_latent_norm.reshape(1, KV_LORA),
        "q_latent_norm": q_latent_norm.reshape(1, Q_LORA),
    }
    loss, grad_x, small = _local_step(x.reshape(T, D), positions.reshape(T, 1), loss_target.reshape(T, D), rep, ex)

    def rows_of(a):
        return a.reshape(-1, a.shape[-1])

    def device_rows(a):
        taps, c = a.shape[-2], a.shape[-1] // N_DEV
        rows = a.reshape(-1, taps, N_DEV, c).transpose(2, 0, 1, 3).reshape(N_DEV, -1, c)
        return jnp.pad(rows, ((0, 0), (0, SHARD_ROWS - rows.shape[1]), (0, 0))).reshape(N_DEV * SHARD_ROWS, c)

    def taps_first(a):
        return jnp.transpose(a, (1, 0, 2))

    sharded = ("sc_conv_w", "ffn_conv_w")
    shards = ([loss.reshape(1, 1, 128)] + [rows_of(small[nm])[None] for nm in REPLICATED]
              + [device_rows(small[nm])[None] for nm in sharded])
    plan, ncopy = _plan_gather_all(len(shards))
    flight, = _copies_start("ag_small", [(shards, [lax.empty((1, N_DEV) + s.shape[1:], F32) for s in shards], ncopy, plan)])
    results = ex.finish(grad_x)
    _, gathered = _copies_wait("ag_small_wait", flight, ncopy, plan)
    params = [[None] + [rows_of(src[nm]) for nm in REPLICATED] + [taps_first(src[nm]) for nm in sharded]
              for src in (wts, mom, var)]
    summed = _adamw_small(gathered, *params, me.astype(jnp.int32).reshape(1))
    loss_total = summed[0][0][0, 0]
    for nm, vals in zip(REPLICATED, summed[1:1 + len(REPLICATED)]):
        results[nm] = [a.reshape(wts[nm].shape) for a in vals]
    for nm, vals in zip(sharded, summed[1 + len(REPLICATED):]):
        results[nm] = [taps_first(a) for a in vals]

    outs = [loss_total, grad_x.reshape(1, T, D)]
    for slot in range(4):
        outs.extend(results[nm][slot] for nm in WEIGHTS)
    return tuple(outs)
```

```python
import jax
import jax.numpy as jnp
from jax import lax
from jax.experimental import pallas as pl
from jax.experimental.pallas import tpu as pltpu

F32 = jnp.float32
BF16 = jnp.bfloat16

T = 2048
D = 1024
N_HEADS = 8
QK_NOPE = 128
QK_ROPE = 64
V_HEAD = 128
Q_LORA = 384
KV_LORA = 256
D_FF = 2816
CHUNK = 64
ROPE_THETA = 10000.0
EPS = 1e-6
NEG_INF = -1e30
ADAM_LR = 0.001
ADAM_B1 = 0.9
ADAM_B2 = 0.999
ADAM_EPS = 1e-08
ADAM_WD = 0.01
ADAM_STEP = 10

N_DEV = 8
N_CHIP = 4
FF_BLK = D_FF * 2 // N_DEV
N_FF_BLK = D_FF // FF_BLK
QK_PAD = 256
HALO = 16

TM = 1024
TS = 512
TR = 256
TQ = 512
VMEM_LIMIT = 56 * 1024 * 1024

NN = (((1,), (0,)), ((), ()))
NT = (((1,), (1,)), ((), ()))
TN = (((0,), (0,)), ((), ()))
MESH = pl.DeviceIdType.MESH


def _params(sem):
    return pltpu.CompilerParams(dimension_semantics=sem, vmem_limit_bytes=VMEM_LIMIT)


ANY_SPEC = pl.BlockSpec(memory_space=pl.ANY)
VMEM_SPEC = pl.BlockSpec(memory_space=pltpu.VMEM)


class _Chain:
    last = None


def _pallas(body, *, name, in_specs, out_specs, out_shape, grid=(), scratch_shapes=(), n_prefetch=0, aliases=None,
            params=None):
    def run(*args):
        after = _Chain.last
        n_lead = len(args)
        specs, operands, fn = list(in_specs), list(args), body
        if after is not None:
            def fn(*refs):
                return body(*refs[:n_lead], *refs[n_lead + 1:])
            specs.append(ANY_SPEC)
            operands.append(after)
        kw = dict(name=name, out_shape=out_shape, input_output_aliases=aliases or {})
        if params is not None:
            kw["compiler_params"] = params
        if n_prefetch:
            kw["grid_spec"] = pltpu.PrefetchScalarGridSpec(
                num_scalar_prefetch=n_prefetch, grid=grid, in_specs=specs, out_specs=out_specs,
                scratch_shapes=scratch_shapes)
        else:
            kw.update(grid=grid, in_specs=specs, out_specs=out_specs, scratch_shapes=scratch_shapes)
        outs = pl.pallas_call(fn, **kw)(*operands)
        _Chain.last = outs[0] if isinstance(outs, (list, tuple)) else outs
        return outs
    return run


def _mm(name, a, b, *, grid, a_spec, b_spec, o_spec, o_shape, o_dtype, dims, k_axis=None, acc_shape=None,
        add=None, add_spec=None):
    nk = grid[k_axis] if k_axis is not None else 1
    has_add = add is not None

    def body(*refs):
        a_ref, b_ref = refs[0], refs[1]
        p = 2
        add_ref = None
        if has_add:
            add_ref = refs[p]
            p += 1
        o_ref = refs[p]
        p += 1
        r = lax.dot_general(a_ref[...].astype(BF16), b_ref[...].astype(BF16), dims, preferred_element_type=F32)
        if k_axis is None:
            if has_add:
                r = r + add_ref[...].astype(F32)
            o_ref[...] = r.astype(o_dtype)
        else:
            acc = refs[p]
            k = pl.program_id(k_axis)

            @pl.when(k == 0)
            def _():
                acc[...] = r

            @pl.when(k > 0)
            def _():
                acc[...] += r

            @pl.when(k == nk - 1)
            def _():
                t = acc[...]
                if has_add:
                    t = t + add_ref[...].astype(F32)
                o_ref[...] = t.astype(o_dtype)

    in_specs = [a_spec, b_spec]
    args = [a, b]
    if has_add:
        in_specs.append(add_spec if add_spec is not None else o_spec)
        args.append(add)
    sem = tuple("arbitrary" if ax == k_axis else "parallel" for ax in range(len(grid)))
    scratch = [pltpu.VMEM(acc_shape, F32)] if k_axis is not None else []
    return _pallas(body, name=name, grid=grid, in_specs=in_specs, out_specs=o_spec,
                   out_shape=jax.ShapeDtypeStruct(o_shape, o_dtype), scratch_shapes=scratch, params=_params(sem))(*args)


def _mm_sum(name, parts, *, grid, o_spec, o_shape, o_dtype, add=None, norm_bwd=None, post=None):
    has_add = add is not None
    np_ = len(parts)
    nn = 1 if norm_bwd is None else len(norm_bwd[1])
    has_res = norm_bwd is not None and norm_bwd[2] is not None
    has_post = post is not None

    def body(*refs):
        accs = [None] * nn
        for p, (_, _, _, _, dims, n) in enumerate(parts):
            a_ref, b_ref = refs[2 * p], refs[2 * p + 1]
            for k in range(a_ref.shape[0]):
                r = lax.dot_general(a_ref[k], b_ref[k], dims, preferred_element_type=F32)
                accs[n] = r if accs[n] is None else accs[n] + r
        if norm_bwd is None:
            acc = accs[0]
            if has_add:
                acc = acc + refs[2 * np_][...]
            refs[-1][...] = acc.astype(o_dtype)
            return
        x_ref, g_refs = refs[2 * np_], refs[2 * np_ + 1:2 * np_ + 1 + nn]
        n_in = 2 * np_ + 1 + nn + has_res + has_post
        dx_ref, dxb_ref, dg_refs = refs[n_in], refs[n_in + 1], refs[n_in + 2:n_in + 2 + nn]
        xv = x_ref[...]
        r = lax.rsqrt(jnp.mean(xv * xv, axis=-1, keepdims=True) + EPS)
        xn = xv * r
        dx = refs[2 * np_ + 1 + nn][...] if has_res else None
        sums = []
        for acc, g_ref in zip(accs, g_refs):
            gdy = acc * g_ref[...]
            t = r * (gdy - xn * jnp.mean(gdy * xn, axis=-1, keepdims=True))
            dx = t if dx is None else dx + t
            sums.append(jnp.sum(acc * xn, axis=0, keepdims=True))
        dx_ref[...] = dx
        dxb = dx.astype(BF16)
        dxb_ref[...] = dxb
        if has_post:
            refs[n_in + 2 + nn][...] = lax.dot_general(dxb, refs[n_in - 1][...], post[1],
                                                       preferred_element_type=F32).astype(BF16)

        @pl.when(pl.program_id(0) == 0)
        def _():
            for dg_ref, part in zip(dg_refs, sums):
                dg_ref[...] = part

        @pl.when(pl.program_id(0) > 0)
        def _():
            for dg_ref, part in zip(dg_refs, sums):
                dg_ref[...] += part

    in_specs, args = [], []
    for a, a_spec, b, b_spec, _, _ in parts:
        in_specs += [a_spec, b_spec]
        args += [a, b]
    if norm_bwd is None:
        if has_add:
            in_specs.append(o_spec)
            args.append(add)
        return _pallas(body, name=name, grid=grid, in_specs=in_specs, out_specs=o_spec,
                       out_shape=jax.ShapeDtypeStruct(o_shape, o_dtype),
                       params=_params(("parallel",) * len(grid)))(*args)
    x, gains, dres = norm_bwd
    vec = pl.BlockSpec((1, o_shape[1]), lambda i: (0, 0))
    in_specs += [o_spec] + [vec] * nn + ([o_spec] if has_res else [])
    args += [x] + list(gains) + ([dres] if has_res else [])
    out_specs = [o_spec, o_spec] + [vec] * nn
    out_shape = ([jax.ShapeDtypeStruct(o_shape, F32), jax.ShapeDtypeStruct(o_shape, BF16)]
                 + [jax.ShapeDtypeStruct((1, o_shape[1]), F32)] * nn)
    if has_post:
        in_specs.append(pl.BlockSpec(post[0].shape, lambda i: (0, 0)))
        args.append(post[0])
        out_specs.append(pl.BlockSpec((o_spec.block_shape[0], post[2]), lambda i: (i, 0)))
        out_shape.append(jax.ShapeDtypeStruct((o_shape[0], post[2]), BF16))
    outs = _pallas(body, name=name, grid=grid, in_specs=in_specs, out_specs=out_specs, out_shape=out_shape,
                   params=_params(("arbitrary",)))(*args)
    if has_post:
        return outs[0], outs[1], list(outs[2:2 + nn]), outs[2 + nn]
    return outs[0], outs[1], list(outs[2:])


def _mm_rows(name, a, b, dims, o_dtype, n_out, *, tn=None, add=None):
    k = a.shape[1]
    tn = n_out if tn is None else tn
    if dims == NN:
        b_spec = pl.BlockSpec((k, tn), lambda n, i: (0, n))
    else:
        b_spec = pl.BlockSpec((tn, k), lambda n, i: (n, 0))
    return _mm(name, a, b, grid=(n_out // tn, T // TM),
               a_spec=pl.BlockSpec((TM, k), lambda n, i: (i, 0)), b_spec=b_spec,
               o_spec=pl.BlockSpec((TM, tn), lambda n, i: (i, n)), o_shape=(T, n_out), o_dtype=o_dtype,
               dims=dims, add=add)


def _wgrads(name, jobs):
    jobs = [job if len(job) == 3 else (*job, job[0].shape[-1]) for job in jobs]
    arrays, index = [], {}
    for a, b, _ in jobs:
        for arr in (a, b):
            if id(arr) not in index:
                index[id(arr)] = len(arrays)
                arrays.append(arr)
    n_in = len(arrays)

    def body(*refs):
        for t, (a, b, rows) in enumerate(jobs):
            a_ref, b_ref, o_ref = refs[index[id(a)]], refs[index[id(b)]], refs[n_in + t]
            if a.ndim == 3:
                for h in range(a.shape[0]):
                    o_ref[h] = lax.dot_general(a_ref[h], b_ref[...], TN, preferred_element_type=F32)[:rows].astype(BF16)
            else:
                o_ref[...] = lax.dot_general(a_ref[...], b_ref[...], TN, preferred_element_type=F32)[:rows].astype(BF16)

    out_shape = [jax.ShapeDtypeStruct(a.shape[:-2] + (rows, b.shape[-1]), BF16) for a, b, rows in jobs]
    return _pallas(body, name=name, in_specs=[VMEM_SPEC] * n_in, out_specs=[VMEM_SPEC] * len(jobs), out_shape=out_shape,
                   params=pltpu.CompilerParams(vmem_limit_bytes=VMEM_LIMIT))(*arrays)


def _mm_wgrad(name, a, b, *, tn=512):
    k, n = a.shape[1], b.shape[1]
    tn = min(tn, n)
    return _mm(name, a, b, grid=(n // tn,),
               a_spec=pl.BlockSpec((T, k), lambda j: (0, 0)), b_spec=pl.BlockSpec((T, tn), lambda j: (0, j)),
               o_spec=pl.BlockSpec((k, tn), lambda j: (0, j)), o_shape=(k, n), o_dtype=BF16, dims=TN)


def _rms_fwd(name, x, g):
    d = x.shape[1]

    def body(x_ref, g_ref, o_ref):
        xv = x_ref[...]
        r = lax.rsqrt(jnp.mean(xv * xv, axis=-1, keepdims=True) + EPS)
        o_ref[...] = ((xv * r) * g_ref[...]).astype(BF16)

    return _pallas(
        body, name=name, grid=(T // TM,),
        in_specs=[pl.BlockSpec((TM, d), lambda i: (i, 0)), pl.BlockSpec((1, d), lambda i: (0, 0))],
        out_specs=pl.BlockSpec((TM, d), lambda i: (i, 0)),
        out_shape=jax.ShapeDtypeStruct((T, d), BF16), params=_params(("parallel",)))(x, g)


def _rms(xv, g):
    return (xv * lax.rsqrt(jnp.mean(xv * xv, axis=-1, keepdims=True) + EPS)) * g


def _out_norm(name, a, w, add, g):
    def body(a_ref, w_ref, add_ref, g_ref, h_ref, hn_ref):
        hv = lax.dot_general(a_ref[...], w_ref[...], NN, preferred_element_type=F32) + add_ref[...]
        h_ref[...] = hv
        hn_ref[...] = _rms(hv, g_ref[...]).astype(BF16)

    rows = pl.BlockSpec((TS, D), lambda i: (i, 0))
    return _pallas(
        body, name=name, grid=(T // TS,),
        in_specs=[pl.BlockSpec((TS, a.shape[1]), lambda i: (i, 0)), pl.BlockSpec(w.shape, lambda i: (0, 0)), rows,
                  pl.BlockSpec((1, D), lambda i: (0, 0))],
        out_specs=[rows, rows], out_shape=[jax.ShapeDtypeStruct((T, D), F32), jax.ShapeDtypeStruct((T, D), BF16)],
        params=_params(("parallel",)))(a, w, add, g)


def _down_final(act, w_down4, h_in, g, tgt):
    def body(a_ref, w_ref, hin_ref, g_ref, t_ref, loss_ref, dh_ref, dhb_ref, dg_ref):
        hv = lax.dot_general(a_ref[0], w_ref[0], NN, preferred_element_type=F32)
        for j in range(1, N_FF_BLK):
            hv = hv + lax.dot_general(a_ref[j], w_ref[j], NN, preferred_element_type=F32)
        hv = hv + hin_ref[...]
        r = lax.rsqrt(jnp.mean(hv * hv, axis=-1, keepdims=True) + EPS)
        xn = hv * r
        gv = g_ref[...]
        err = xn * gv - t_ref[...]
        part_loss = 0.5 * jnp.sum(jnp.mean(err * err, axis=-1, keepdims=True), axis=0, keepdims=True)
        dy = err * (1.0 / D)
        gdy = dy * gv
        dh = r * (gdy - xn * jnp.mean(gdy * xn, axis=-1, keepdims=True))
        dh_ref[...] = dh
        dhb_ref[...] = dh.astype(BF16)
        part = jnp.sum(dy * xn, axis=0, keepdims=True)
        first = pl.program_id(0) == 0

        @pl.when(first)
        def _():
            dg_ref[...] = part
            loss_ref[...] = jnp.broadcast_to(part_loss, (1, 128))

        @pl.when(jnp.logical_not(first))
        def _():
            dg_ref[...] += part
            loss_ref[...] += jnp.broadcast_to(part_loss, (1, 128))

    row = pl.BlockSpec((TS, D), lambda i: (i, 0))
    vec = pl.BlockSpec((1, D), lambda i: (0, 0))
    return _pallas(
        body, name="f1_down_loss", grid=(T // TS,),
        in_specs=[pl.BlockSpec((N_FF_BLK, TS, FF_BLK), lambda i: (0, i, 0)),
                  pl.BlockSpec((None, N_FF_BLK, FF_BLK, D), lambda i: (0, 0, 0, 0)), row, vec, row],
        out_specs=[pl.BlockSpec((1, 128), lambda i: (0, 0)), row, row, vec],
        out_shape=[jax.ShapeDtypeStruct((1, 128), F32), jax.ShapeDtypeStruct((T, D), F32),
                   jax.ShapeDtypeStruct((T, D), BF16), jax.ShapeDtypeStruct((1, D), F32)],
        params=_params(("arbitrary",)))(act, w_down4, h_in, g, tgt)


def _prev_idx(i, rows=TR):
    return jnp.maximum(i * (rows // HALO) - 1, 0)


def _next_idx(i, rows=TR):
    return jnp.minimum((i + 1) * (rows // HALO), T // HALO - 1)


def _causal_taps(ext):
    return pltpu.roll(ext, 2, 0)[HALO:], pltpu.roll(ext, 1, 0)[HALO:], ext[HALO:]


def _anticausal_taps(ext, n):
    rows = ext.shape[0]
    return pltpu.roll(ext, rows - 1, 0)[:n], pltpu.roll(ext, rows - 2, 0)[:n]


MIX_COLS = 512


def _mixer_in(hn, w_in, w):
    nc = D // MIX_COLS

    def body(h_ref, hh_ref, wb_ref, wc_ref, wu_ref, w_ref, b_ref, c_ref, u_ref, y_ref):
        i = pl.program_id(1)
        hv = h_ref[...]
        he = jnp.concatenate([hh_ref[...], hv], axis=0)
        ce = lax.dot_general(he, wc_ref[...], NN, preferred_element_type=F32).astype(BF16)
        ue = lax.dot_general(he, wu_ref[...], NN, preferred_element_type=F32).astype(BF16)
        bv = lax.dot_general(hv, wb_ref[...], NN, preferred_element_type=F32).astype(BF16)
        b_ref[...] = bv
        c_ref[...] = ce[HALO:]
        u_ref[...] = ue[HALO:]
        row = lax.broadcasted_iota(jnp.int32, (HALO + TS, 1), 0)
        cu = jnp.where(jnp.logical_or(i > 0, row >= HALO), ce.astype(F32) * ue.astype(F32), 0.0)
        x2, x1, x0 = _causal_taps(cu)
        wv = w_ref[...]
        cv = (x2 * wv[0:1] + x1 * wv[1:2]) + x0 * wv[2:3]
        y_ref[...] = (bv.astype(F32) * cv).astype(BF16)

    def cols(part):
        return pl.BlockSpec((D, MIX_COLS), lambda j, i: (0, part * nc + j))

    blk = pl.BlockSpec((TS, MIX_COLS), lambda j, i: (i, j))
    out = jax.ShapeDtypeStruct((T, D), BF16)
    return _pallas(
        body, name="l0_in", grid=(nc, T // TS),
        in_specs=[pl.BlockSpec((TS, D), lambda j, i: (i, 0)), pl.BlockSpec((HALO, D), lambda j, i: (_prev_idx(i, TS), 0)),
                  cols(0), cols(1), cols(2), pl.BlockSpec((3, MIX_COLS), lambda j, i: (0, j))],
        out_specs=[blk] * 4, out_shape=[out] * 4,
        params=_params(("parallel", "parallel")))(hn, hn, w_in, w_in, w_in, w)


def _mixer_out_bwd(dh, w_out, zb, zc, zu, w):
    last = T // TR - 1

    def body(dh_ref, dhn_ref, wo_ref, b_ref, bn_ref, c_ref, ch_ref, u_ref, uh_ref, w_ref, dz_ref, dw_ref):
        i = pl.program_id(0)
        dye = lax.dot_general(jnp.concatenate([dh_ref[...], dhn_ref[...]], axis=0), wo_ref[...], NT,
                              preferred_element_type=F32)
        cv_ = c_ref[...].astype(F32)
        uv = u_ref[...].astype(F32)
        cu = cv_ * uv
        cuh = jnp.where(i > 0, ch_ref[...].astype(F32) * uh_ref[...].astype(F32), 0.0)
        x2, x1, x0 = _causal_taps(jnp.concatenate([cuh, cu], axis=0))
        wv = w_ref[...]
        conv = (x2 * wv[0:1] + x1 * wv[1:2]) + x0 * wv[2:3]
        dyv = dye[:TR]
        dz_ref[:, 0:D] = (dyv * conv).astype(BF16)
        dconv = dyv * b_ref[...].astype(F32)
        dconv_n = jnp.where(i < last, dye[TR:] * bn_ref[...].astype(F32), 0.0)
        n1, n2 = _anticausal_taps(jnp.concatenate([dconv, dconv_n], axis=0), TR)
        dcu = (dconv * wv[2:3] + n1 * wv[1:2]) + n2 * wv[0:1]
        dz_ref[:, D:2 * D] = (dcu * uv).astype(BF16)
        dz_ref[:, 2 * D:3 * D] = (dcu * cv_).astype(BF16)
        part = jnp.concatenate([jnp.sum(dconv * x2, axis=0, keepdims=True),
                                jnp.sum(dconv * x1, axis=0, keepdims=True),
                                jnp.sum(dconv * x0, axis=0, keepdims=True)], axis=0)

        @pl.when(i == 0)
        def _():
            dw_ref[...] = part

        @pl.when(i > 0)
        def _():
            dw_ref[...] += part

    main = pl.BlockSpec((TR, D), lambda i: (i, 0))
    prev = pl.BlockSpec((HALO, D), lambda i: (_prev_idx(i), 0))
    nxt = pl.BlockSpec((HALO, D), lambda i: (_next_idx(i), 0))
    wspec = pl.BlockSpec((3, D), lambda i: (0, 0))
    return _pallas(
        body, name="d_l0_out", grid=(T // TR,),
        in_specs=[main, nxt, pl.BlockSpec((D, D), lambda i: (0, 0)), main, nxt, main, prev, main, prev, wspec],
        out_specs=[pl.BlockSpec((TR, 3 * D), lambda i: (i, 0)), wspec],
        out_shape=[jax.ShapeDtypeStruct((T, 3 * D), BF16), jax.ShapeDtypeStruct((3, D), F32)],
        params=_params(("arbitrary",)))(dh, dh, w_out, zb, zb, zc, zc, zu, zu, w)


def _sigmoid(x):
    return 0.5 * jnp.tanh(0.5 * x) + 0.5


def _ffn_up_act(name, hf, w_up, w, b):
    def body(h_ref, hh_ref, wg_ref, wv_ref, w_ref, b_ref, g_ref, v_ref, a_ref):
        i = pl.program_id(1)
        hv = h_ref[...]
        ge = lax.dot_general(jnp.concatenate([hh_ref[...], hv], axis=0), wg_ref[...], NT,
                             preferred_element_type=F32).astype(BF16)
        v = lax.dot_general(hv, wv_ref[...], NT, preferred_element_type=F32).astype(BF16)
        g_ref[...] = ge[HALO:]
        v_ref[...] = v
        ext = ge.astype(F32)
        row = lax.broadcasted_iota(jnp.int32, (HALO + TM, 1), 0)
        ext = jnp.where(jnp.logical_or(i > 0, row >= HALO), ext, 0.0)
        x2, x1, x0 = _causal_taps(ext)
        wv = w_ref[...]
        gc = ((x2 * wv[0:1] + x1 * wv[1:2]) + x0 * wv[2:3]) + b_ref[...]
        a_ref[...] = ((gc * _sigmoid(gc)) * v.astype(F32)).astype(BF16)

    blk = pl.BlockSpec((None, TM, FF_BLK), lambda j, i: (j, i, 0))
    out = jax.ShapeDtypeStruct((N_FF_BLK, T, FF_BLK), BF16)
    return _pallas(
        body, name=name, grid=(N_FF_BLK, T // TM),
        in_specs=[pl.BlockSpec((TM, D), lambda j, i: (i, 0)),
                  pl.BlockSpec((HALO, D), lambda j, i: (_prev_idx(i, TM), 0)),
                  pl.BlockSpec((None, None, FF_BLK, D), lambda j, i: (0, j, 0, 0)),
                  pl.BlockSpec((None, None, FF_BLK, D), lambda j, i: (0, j + N_FF_BLK, 0, 0)),
                  pl.BlockSpec((None, 3, FF_BLK), lambda j, i: (j, 0, 0)),
                  pl.BlockSpec((None, 1, FF_BLK), lambda j, i: (j, 0, 0))],
        out_specs=[blk, blk, blk], out_shape=[out, out, out],
        params=_params(("parallel", "parallel")))(hf, hf, w_up, w_up, w, b)


def _ffn_dact(name, dh, w_down4, g, v, w, b):
    last = T // TS - 1

    def body(dh_ref, dhn_ref, wd_ref, g_ref, gp_ref, gn_ref, v_ref, vn_ref, w_ref, b_ref, dg_ref, dv_ref, dw_ref, db_ref):
        i = pl.program_id(1)
        da = lax.dot_general(jnp.concatenate([dh_ref[...], dhn_ref[...]], axis=0), wd_ref[...], NT,
                             preferred_element_type=F32)
        row = lax.broadcasted_iota(jnp.int32, (TS + HALO, 1), 0)
        da = jnp.where(jnp.logical_or(i < last, row < TS), da, 0.0)
        gp = jnp.where(i > 0, gp_ref[...].astype(F32), 0.0)
        ext = jnp.concatenate([gp, g_ref[...].astype(F32), gn_ref[...].astype(F32)], axis=0)
        x2, x1, x0 = _causal_taps(ext)
        wv = w_ref[...]
        gc = ((x2 * wv[0:1] + x1 * wv[1:2]) + x0 * wv[2:3]) + b_ref[...]
        sg = _sigmoid(gc)
        vv = jnp.concatenate([v_ref[...].astype(F32), vn_ref[...].astype(F32)], axis=0)
        silu = gc * sg
        dv_ref[...] = (da[:TS] * silu[:TS]).astype(BF16)
        dgc = (da * vv) * (sg + silu * (1.0 - sg))
        n1, n2 = _anticausal_taps(dgc, TS)
        d0 = dgc[:TS]
        dg_ref[...] = ((d0 * wv[2:3] + n1 * wv[1:2]) + n2 * wv[0:1]).astype(BF16)
        part_w = jnp.concatenate([jnp.sum(d0 * x2[:TS], axis=0, keepdims=True),
                                  jnp.sum(d0 * x1[:TS], axis=0, keepdims=True),
                                  jnp.sum(d0 * x0[:TS], axis=0, keepdims=True)], axis=0)
        part_b = jnp.sum(d0, axis=0, keepdims=True)

        @pl.when(i == 0)
        def _():
            dw_ref[...] = part_w
            db_ref[...] = part_b

        @pl.when(i > 0)
        def _():
            dw_ref[...] += part_w
            db_ref[...] += part_b

    blk = pl.BlockSpec((None, TS, FF_BLK), lambda j, i: (j, i, 0))
    prev = pl.BlockSpec((None, HALO, FF_BLK), lambda j, i: (j, _prev_idx(i, TS), 0))
    nxt = pl.BlockSpec((None, HALO, FF_BLK), lambda j, i: (j, _next_idx(i, TS), 0))
    wspec = pl.BlockSpec((None, 3, FF_BLK), lambda j, i: (j, 0, 0))
    bspec = pl.BlockSpec((None, 1, FF_BLK), lambda j, i: (j, 0, 0))
    return _pallas(
        body, name=name, grid=(N_FF_BLK, T // TS),
        in_specs=[pl.BlockSpec((TS, D), lambda j, i: (i, 0)),
                  pl.BlockSpec((HALO, D), lambda j, i: (_next_idx(i, TS), 0)),
                  pl.BlockSpec((None, None, FF_BLK, D), lambda j, i: (0, j, 0, 0)),
                  blk, prev, nxt, blk, nxt, wspec, bspec],
        out_specs=[blk, blk, wspec, bspec],
        out_shape=[jax.ShapeDtypeStruct((N_FF_BLK, T, FF_BLK), BF16), jax.ShapeDtypeStruct((N_FF_BLK, T, FF_BLK), BF16),
                   jax.ShapeDtypeStruct((N_FF_BLK, 3, FF_BLK), F32), jax.ShapeDtypeStruct((N_FF_BLK, 1, FF_BLK), F32)],
        params=_params(("parallel", "arbitrary")))(dh, dh, w_down4, g, g, g, v, v, w, b)


def _rope_tables(pos, inv_freq):
    half = QK_ROPE // 2

    def body(p_ref, f_ref, c_ref, sa_ref, sb_ref):
        ang = p_ref[...].astype(F32) * f_ref[...]
        lane = lax.broadcasted_iota(jnp.int32, (T, 128), 1)
        c = jnp.cos(ang)
        s = jnp.sin(ang)
        c_ref[...] = jnp.where(lane < 2 * half, c, 0.0)
        sa_ref[...] = jnp.where(lane < half, -s, 0.0)
        sb_ref[...] = jnp.where(jnp.logical_and(lane >= half, lane < 2 * half), s, 0.0)

    return _pallas(
        body, name="rope_tables", in_specs=[VMEM_SPEC] * 2, out_specs=[VMEM_SPEC] * 3,
        out_shape=[jax.ShapeDtypeStruct((T, 128), F32)] * 3,
        params=pltpu.CompilerParams(vmem_limit_bytes=VMEM_LIMIT))(pos, inv_freq)


def _rotate(r, c, sa, sb, sign):
    return r * c + sign * (pltpu.roll(r, 96, 1) * sa + pltpu.roll(r, 32, 1) * sb)


def _attn_pre(h2, g_kv, g_l1, g_kvl, g_ql, w_dkv, w_kr, w_uk, w_uv, w_dq, w_uq, tables):
    def body(h_ref, c_ref, sa_ref, sb_ref, gkv_ref, gl1_ref, gkvl_ref, gql_ref, wdkv_ref, wkr_ref, wuk_ref, wuv_ref,
             wdq_ref, wuq_ref, hk_ref, hn_ref, ckvr_ref, ckv_ref, kr_ref, kn_ref, v_ref, cqr_ref, cq_ref, q_ref):
        xv = h_ref[...]
        xn = xv * lax.rsqrt(jnp.mean(xv * xv, axis=-1, keepdims=True) + EPS)
        hk = (xn * gkv_ref[...]).astype(BF16)
        hn = (xn * gl1_ref[...]).astype(BF16)
        hk_ref[...] = hk
        hn_ref[...] = hn
        cv, sav, sbv = c_ref[...], sa_ref[...], sb_ref[...]
        raw = lax.dot_general(hk, wdkv_ref[...], NN, preferred_element_type=F32)
        ckvr_ref[...] = raw
        ckv = _rms(raw, gkvl_ref[...]).astype(BF16)
        ckv_ref[...] = ckv
        kr = lax.dot_general(hk, wkr_ref[...], NT, preferred_element_type=F32)
        kr_ref[...] = _rotate(kr, cv, sav, sbv, 1.0).astype(BF16)
        kn_ref[...] = lax.dot_general(ckv, wuk_ref[...], NN, preferred_element_type=F32).astype(BF16)
        v_ref[...] = lax.dot_general(ckv, wuv_ref[...], NN, preferred_element_type=F32).astype(BF16)
        cqr = lax.dot_general(hn, wdq_ref[...], NN, preferred_element_type=F32)
        cqr_ref[...] = cqr
        cq = _rms(cqr, gql_ref[...]).astype(BF16)
        cq_ref[...] = cq
        for h in range(N_HEADS):
            r = lax.dot_general(cq, wuq_ref[h], NT, preferred_element_type=F32)
            q_ref[h, :, :QK_NOPE] = (r[:, :QK_NOPE] * SCALE2).astype(BF16)
            q_ref[h, :, QK_NOPE:] = (_rotate(r[:, QK_NOPE:], cv, sav, sbv, 1.0) * SCALE2).astype(BF16)

    def rows(d):
        return pl.BlockSpec((TS, d), lambda i: (i, 0))

    def whole(a):
        return pl.BlockSpec(a.shape, lambda i: (0,) * a.ndim)

    wholes = [g_kv, g_l1, g_kvl, g_ql, w_dkv, w_kr, w_uk, w_uv, w_dq, w_uq]
    outs = [(D, BF16), (D, BF16), (KV_LORA, F32), (KV_LORA, BF16), (128, BF16), (N_HEADS * QK_NOPE, BF16),
            (N_HEADS * V_HEAD, BF16), (Q_LORA, F32), (Q_LORA, BF16)]
    return _pallas(
        body, name="attn_pre", grid=(T // TS,),
        in_specs=[rows(D), rows(128), rows(128), rows(128)] + [whole(a) for a in wholes],
        out_specs=[rows(d) for d, _ in outs] + [pl.BlockSpec((N_HEADS, TS, QK_PAD), lambda i: (0, i, 0))],
        out_shape=[jax.ShapeDtypeStruct((T, d), dt) for d, dt in outs]
        + [jax.ShapeDtypeStruct((N_HEADS, T, QK_PAD), BF16)],
        params=_params(("parallel",)))(h2, *tables, *wholes)


SCALE = (QK_NOPE + QK_ROPE) ** -0.5
LOG2E = 1.4426950408889634
SCALE2 = SCALE * LOG2E


def _diag_mask(transposed):
    shift = CHUNK.bit_length() - 1
    a = lax.broadcasted_iota(jnp.int32, (TQ, TQ), 0) >> shift
    b = lax.broadcasted_iota(jnp.int32, (TQ, TQ), 1) >> shift
    return (a <= b) if transposed else (b <= a)


def _as_row(col):
    return jnp.transpose(jnp.broadcast_to(col, (col.shape[0], 128)), (1, 0))[0:1]


def _attn_fwd(q, kn, kr, v):
    hp = 2

    def body(q_ref, kn_ref, kr_ref, v_ref, o_ref, lse_ref):
        i = pl.program_id(1)
        qs = [q_ref[a] for a in range(hp)]

        def step(j, carry, masked):
            off = pl.multiple_of(j * TQ, TQ)
            krv = kr_ref[pl.ds(off, TQ), :]
            ss = []
            for a in range(hp):
                kk = jnp.concatenate([kn_ref[pl.ds(off, TQ), a * QK_NOPE:(a + 1) * QK_NOPE], krv], axis=1)
                ss.append(lax.dot_general(qs[a], kk, NT, preferred_element_type=F32))
            out = []
            for a in range(hp):
                m, l, acc = carry[a]
                s = ss[a]
                if masked:
                    s = jnp.where(_diag_mask(False), s, NEG_INF)
                m_new = jnp.maximum(m, jnp.max(s, axis=-1, keepdims=True))
                p = jnp.exp2(s - m_new)
                alpha = jnp.exp2(m - m_new)
                l = alpha * l + jnp.sum(p, axis=-1, keepdims=True)
                pv = lax.dot_general(p.astype(BF16), v_ref[pl.ds(off, TQ), a * V_HEAD:(a + 1) * V_HEAD], NN,
                                     preferred_element_type=F32)
                out.append((m_new, l, alpha * acc + pv))
            return tuple(out)

        one = (jnp.full((TQ, 1), NEG_INF, F32), jnp.zeros((TQ, 1), F32), jnp.zeros((TQ, V_HEAD), F32))
        carry = lax.fori_loop(0, i, lambda j, cr: step(j, cr, False), (one,) * hp)
        carry = step(i, carry, True)
        for a, (m, l, acc) in enumerate(carry):
            o_ref[:, a * V_HEAD:(a + 1) * V_HEAD] = (acc / l).astype(BF16)
            lse_ref[a] = _as_row(m + jnp.log(l) * LOG2E)

    return _pallas(
        body, name="attn_fwd", grid=(N_HEADS // hp, T // TQ),
        in_specs=[pl.BlockSpec((hp, TQ, QK_PAD), lambda h, i: (h, i, 0)),
                  pl.BlockSpec((T, hp * QK_NOPE), lambda h, i: (0, h)),
                  pl.BlockSpec((T, 128), lambda h, i: (0, 0)),
                  pl.BlockSpec((T, hp * V_HEAD), lambda h, i: (0, h))],
        out_specs=[pl.BlockSpec((TQ, hp * V_HEAD), lambda h, i: (i, h)), pl.BlockSpec((hp, 1, TQ), lambda h, i: (h, 0, i))],
        out_shape=[jax.ShapeDtypeStruct((T, N_HEADS * V_HEAD), BF16), jax.ShapeDtypeStruct((N_HEADS, 1, T), F32)],
        params=_params(("parallel", "parallel")))(q, kn, kr, v)


def _attn_bwd(q, kn, kr, v, o, do, lse_row, tables):
    nq = T // TQ
    hp = 2
    cos, sa, sb = tables

    def body(q_ref, kn_ref, kr_ref, v_ref, o_ref, do_ref, lse_ref, c_ref, sa_ref, sb_ref,
             dq_ref, dkn_ref, dkr_ref, dv_ref, dq_acc, dl_ref):
        j = pl.program_id(1)

        def cols(a):
            return slice(a * 128, (a + 1) * 128)

        @pl.when(j == 0)
        def _():
            dq_acc[...] = jnp.zeros_like(dq_acc)
            for a in range(hp):
                for i in range(nq):
                    rows = pl.ds(i * TQ, TQ)
                    prod = do_ref[rows, cols(a)].astype(F32) * o_ref[rows, cols(a)].astype(F32)
                    dl_ref[a, :, rows] = _as_row(jnp.sum(prod, axis=-1, keepdims=True))

        krv = kr_ref[...]
        kks = [jnp.concatenate([kn_ref[:, cols(a)], krv], axis=1) for a in range(hp)]
        vvs = [v_ref[:, cols(a)] for a in range(hp)]

        def step(i, carry, masked):
            off = pl.multiple_of(i * TQ, TQ)
            rows = pl.ds(off, TQ)
            qis = [q_ref[a, rows, :] for a in range(hp)]
            dois = [do_ref[rows, cols(a)] for a in range(hp)]
            sts = [lax.dot_general(kks[a], qis[a], NT, preferred_element_type=F32) for a in range(hp)]
            dpts = [lax.dot_general(vvs[a], dois[a], NT, preferred_element_type=F32) for a in range(hp)]
            out = []
            for a in range(hp):
                dk, dv = carry[a]
                st = sts[a]
                if masked:
                    st = jnp.where(_diag_mask(True), st, NEG_INF)
                pt = jnp.exp2(st - lse_ref[a, :, rows])
                dv = dv + lax.dot_general(pt.astype(BF16), dois[a], NN, preferred_element_type=F32)
                dst = (pt * (dpts[a] - dl_ref[a, :, rows])).astype(BF16)
                dk = dk + lax.dot_general(dst, qis[a], NN, preferred_element_type=F32)
                dq_acc[a, rows, :] += lax.dot_general(dst, kks[a], TN, preferred_element_type=F32)
                out.append((dk, dv))
            return tuple(out)

        zero = (jnp.zeros((TQ, QK_PAD), F32), jnp.zeros((TQ, V_HEAD), F32))
        carry = step(j, (zero,) * hp, True)
        carry = lax.fori_loop(j + 1, nq, lambda i, cr: step(i, cr, False), carry)
        for a, (dk, dv) in enumerate(carry):
            dk = dk * (SCALE / SCALE2)
            dkn_ref[:, cols(a)] = dk[:, :QK_NOPE].astype(BF16)
            dkr_ref[a] = dk[:, QK_NOPE:]
            dv_ref[:, cols(a)] = dv.astype(BF16)

        @pl.when(j == nq - 1)
        def _():
            for a in range(hp):
                dq = dq_acc[a] * SCALE
                dq_ref[a, :, :QK_NOPE] = dq[:, :QK_NOPE].astype(BF16)
                dq_ref[a, :, QK_NOPE:] = _rotate(dq[:, QK_NOPE:], c_ref[...], sa_ref[...], sb_ref[...], -1.0).astype(BF16)

    row = pl.BlockSpec((hp, 1, T), lambda h, j: (h, 0, 0))
    head = pl.BlockSpec((TQ, hp * 128), lambda h, j: (j, h))
    whole = pl.BlockSpec((hp, T, QK_PAD), lambda h, j: (h, 0, 0))
    tab = pl.BlockSpec((T, 128), lambda h, j: (0, 0))
    heads = pl.BlockSpec((T, hp * V_HEAD), lambda h, j: (0, h))
    return _pallas(
        body, name="attn_bwd", grid=(N_HEADS // hp, nq),
        in_specs=[whole, head, pl.BlockSpec((TQ, 128), lambda h, j: (j, 0)), head, heads, heads, row, tab, tab, tab],
        out_specs=[whole, head, pl.BlockSpec((hp, TQ, 128), lambda h, j: (h, j, 0)), head],
        out_shape=[jax.ShapeDtypeStruct((N_HEADS, T, QK_PAD), BF16), jax.ShapeDtypeStruct((T, N_HEADS * QK_NOPE), BF16),
                   jax.ShapeDtypeStruct((N_HEADS, T, 128), F32), jax.ShapeDtypeStruct((T, N_HEADS * V_HEAD), BF16)],
        scratch_shapes=[pltpu.VMEM((hp, T, QK_PAD), F32), pltpu.VMEM((hp, 1, T), F32)],
        params=_params(("parallel", "arbitrary")))(q, kn, kr, v, o, do, lse_row, cos, sa, sb)


def _rms_bwd_math(xv, g, dy):
    r = lax.rsqrt(jnp.mean(xv * xv, axis=-1, keepdims=True) + EPS)
    xn = xv * r
    gdy = dy * g
    return r * (gdy - xn * jnp.mean(gdy * xn, axis=-1, keepdims=True)), jnp.sum(dy * xn, axis=0, keepdims=True)


def _attn_post(dq, dkn, dv, dkr, cq_raw, ckv_raw, h2, dres, g_ql, g_kvl, g_l1, g_kv, w_uq, w_uk, w_uv, w_dq, w_dkv,
               w_kr, tables):
    def body(dq_ref, dkn_ref, dv_ref, dkr_ref, cqr_ref, ckvr_ref, h_ref, res_ref, c_ref, sa_ref, sb_ref,
             gql_ref, gkvl_ref, gl1_ref, gkv_ref, wuq_ref, wuk_ref, wuv_ref, wdq_ref, wdkv_ref, wkr_ref,
             dcq_ref, dckv_ref, dkrr_ref, dh_ref, dhb_ref, dgql_ref, dgkvl_ref, dgl1_ref, dgkv_ref):
        dcq = lax.dot_general(dq_ref[0], wuq_ref[0], NN, preferred_element_type=F32)
        for h in range(1, N_HEADS):
            dcq = dcq + lax.dot_general(dq_ref[h], wuq_ref[h], NN, preferred_element_type=F32)
        dcq_raw, s_ql = _rms_bwd_math(cqr_ref[...], gql_ref[...], dcq)
        dcq_raw = dcq_raw.astype(BF16)
        dcq_ref[...] = dcq_raw
        dckv = (lax.dot_general(dkn_ref[...], wuk_ref[...], NT, preferred_element_type=F32)
                + lax.dot_general(dv_ref[...], wuv_ref[...], NT, preferred_element_type=F32))
        dckv_raw, s_kvl = _rms_bwd_math(ckvr_ref[...], gkvl_ref[...], dckv)
        dckv_raw = dckv_raw.astype(BF16)
        dckv_ref[...] = dckv_raw
        dkr = dkr_ref[0]
        for h in range(1, N_HEADS):
            dkr = dkr + dkr_ref[h]
        dkr_raw = _rotate(dkr, c_ref[...], sa_ref[...], sb_ref[...], -1.0).astype(BF16)
        dkrr_ref[...] = dkr_raw
        d_hn = lax.dot_general(dcq_raw, wdq_ref[...], NT, preferred_element_type=F32)
        d_hk = (lax.dot_general(dckv_raw, wdkv_ref[...], NT, preferred_element_type=F32)
                + lax.dot_general(dkr_raw, wkr_ref[...], NN, preferred_element_type=F32))
        xv = h_ref[...]
        r = lax.rsqrt(jnp.mean(xv * xv, axis=-1, keepdims=True) + EPS)
        xn = xv * r
        dx = res_ref[...]
        sums = [s_ql, s_kvl]
        for dy, g_ref in ((d_hn, gl1_ref), (d_hk, gkv_ref)):
            gdy = dy * g_ref[...]
            dx = dx + r * (gdy - xn * jnp.mean(gdy * xn, axis=-1, keepdims=True))
            sums.append(jnp.sum(dy * xn, axis=0, keepdims=True))
        dh_ref[...] = dx
        dhb_ref[...] = dx.astype(BF16)
        dg_refs = (dgql_ref, dgkvl_ref, dgl1_ref, dgkv_ref)

        @pl.when(pl.program_id(0) == 0)
        def _():
            for dg_ref, part in zip(dg_refs, sums):
                dg_ref[...] = part

        @pl.when(pl.program_id(0) > 0)
        def _():
            for dg_ref, part in zip(dg_refs, sums):
                dg_ref[...] += part

    def rows(d):
        return pl.BlockSpec((TS, d), lambda i: (i, 0))

    def heads(d):
        return pl.BlockSpec((N_HEADS, TS, d), lambda i: (0, i, 0))

    def whole(a):
        return pl.BlockSpec(a.shape, lambda i: (0,) * a.ndim)

    wholes = [g_ql, g_kvl, g_l1, g_kv, w_uq, w_uk, w_uv, w_dq, w_dkv, w_kr]
    vecs = [Q_LORA, KV_LORA, D, D]
    return _pallas(
        body, name="attn_post", grid=(T // TS,),
        in_specs=[heads(QK_PAD), rows(N_HEADS * QK_NOPE), rows(N_HEADS * V_HEAD), heads(128), rows(Q_LORA),
                  rows(KV_LORA), rows(D), rows(D), rows(128), rows(128), rows(128)] + [whole(a) for a in wholes],
        out_specs=[rows(Q_LORA), rows(KV_LORA), rows(128), rows(D), rows(D)]
        + [pl.BlockSpec((1, d), lambda i: (0, 0)) for d in vecs],
        out_shape=[jax.ShapeDtypeStruct((T, Q_LORA), BF16), jax.ShapeDtypeStruct((T, KV_LORA), BF16),
                   jax.ShapeDtypeStruct((T, 128), BF16), jax.ShapeDtypeStruct((T, D), F32),
                   jax.ShapeDtypeStruct((T, D), BF16)] + [jax.ShapeDtypeStruct((1, d), F32) for d in vecs],
        params=_params(("arbitrary",)))(dq, dkn, dv, dkr, cq_raw, ckv_raw, h2, dres, *tables, *wholes)


def _ffn_gup(name, dg, dv, hf):
    def body(dg_ref, dv_ref, hf_ref, o_ref):
        j = pl.program_id(0)

        @pl.when(j < N_FF_BLK)
        def _():
            o_ref[...] = lax.dot_general(dg_ref[...], hf_ref[...], TN, preferred_element_type=F32).astype(BF16)

        @pl.when(j >= N_FF_BLK)
        def _():
            o_ref[...] = lax.dot_general(dv_ref[...], hf_ref[...], TN, preferred_element_type=F32).astype(BF16)

    return _pallas(
        body, name=name, grid=(N_DEV,),
        in_specs=[pl.BlockSpec((None, T, FF_BLK), lambda j: (jnp.minimum(j, N_FF_BLK - 1), 0, 0)),
                  pl.BlockSpec((None, T, FF_BLK), lambda j: (jnp.maximum(j - N_FF_BLK, 0), 0, 0)),
                  pl.BlockSpec((T, D), lambda j: (0, 0))],
        out_specs=pl.BlockSpec((None, FF_BLK, D), lambda j: (j, 0, 0)),
        out_shape=jax.ShapeDtypeStruct((N_DEV, FF_BLK, D), BF16), params=_params(("parallel",)))(dg, dv, hf)


def _ffn_layer_fwd(tag, h, hf, ex, final=None):
    g, v, act = _ffn_up_act(f"{tag}_up", hf, ex.need(f"ffn_w_up{tag[1]}", hf), ex.need(f"ffn_cw{tag[1]}", hf),
                            ex.need(f"ffn_cb{tag[1]}", hf))
    ex.at(f"{tag}_up", act)
    if final is not None:
        return _down_final(act, ex.need(f"ffn_w_down{tag[1]}", act), h, *final), (hf, g, v, act)
    rows = pl.BlockSpec((TS, D), lambda i: (i, 0))
    out = _mm_sum(f"{tag}_down",
                  [(act, pl.BlockSpec((N_FF_BLK, TS, FF_BLK), lambda i: (0, i, 0)), ex.need(f"ffn_w_down{tag[1]}", act),
                    pl.BlockSpec((None, N_FF_BLK, FF_BLK, D), lambda i: (0, 0, 0, 0)), NN, 0)],
                  grid=(T // TS,), o_spec=rows, o_shape=(T, D), o_dtype=F32, add=h)
    ex.at(f"{tag}_down", out)
    return out, (hf, g, v, act)


def _ffn_layer_bwd(tag, h, gain, ex, saved, dh, dh_bf, post=None):
    hf, g, v, act = saved
    layer = tag[1]
    w_up, w_down4 = ex.need(f"ffn_w_up{layer}", dh_bf), ex.need(f"ffn_w_down{layer}", dh_bf)
    dg, dv, dcw, dcb = _ffn_dact(f"{tag}_dact", dh_bf, w_down4, g, v, ex.need(f"ffn_cw{layer}", dh_bf),
                                 ex.need(f"ffn_cb{layer}", dh_bf))
    ex.at(f"{tag}_dact", dg)
    g_down = _mm(f"{tag}_gdown", act, dh_bf, grid=(N_FF_BLK,),
                 a_spec=pl.BlockSpec((None, T, FF_BLK), lambda j: (j, 0, 0)),
                 b_spec=pl.BlockSpec((T, D), lambda j: (0, 0)),
                 o_spec=pl.BlockSpec((FF_BLK, D), lambda j: (j, 0)),
                 o_shape=(D_FF, D), o_dtype=BF16, dims=TN)
    g_up = _ffn_gup(f"{tag}_gup", dg, dv, hf)
    ex.grad("ffn_w_up", int(layer), g_up.reshape(1, N_DEV, FF_BLK, D))
    ex.grad("ffn_w_down", int(layer), g_down.reshape(1, N_DEV, D_FF // N_DEV, D))
    ex.at(f"{tag}_gup", g_up)
    part = pl.BlockSpec((N_FF_BLK, TR, FF_BLK), lambda i: (0, i, 0))
    dh_in, dh_in_bf, dgain, *onward = _mm_sum(
        f"{tag}_dhf",
        [(dg, part, w_up, pl.BlockSpec((None, N_FF_BLK, FF_BLK, D), lambda i: (0, 0, 0, 0)), NN, 0),
         (dv, part, w_up, pl.BlockSpec((None, N_FF_BLK, FF_BLK, D), lambda i: (0, 1, 0, 0)), NN, 0)],
        grid=(T // TR,), o_spec=pl.BlockSpec((TR, D), lambda i: (i, 0)), o_shape=(T, D), o_dtype=F32,
        norm_bwd=(h, [gain], dh), post=post)
    ex.at(f"{tag}_dhf", dh_in)
    return (dh_in, dh_in_bf, dgain[0], dcw, dcb, *onward)


def _local_step(x, pos, tgt, rep, ex):
    attn_norm, ffn_norm, final_norm = rep["attn_norm"], rep["ffn_norm"], rep["final_norm"]
    half = QK_ROPE // 2
    inv = 1.0 / (ROPE_THETA ** (jnp.arange(half, dtype=F32) / half))
    inv_freq = jnp.concatenate([inv, inv, jnp.zeros((128 - 2 * half,), F32)]).reshape(1, 128)
    tables = _rope_tables(pos, inv_freq)

    hn0 = _rms_fwd("l0_norm", x, attn_norm[0:1])
    ex.at("l0_norm", hn0)
    w_in = ex.need("sc_w_in", hn0)
    zb, zc, zu, y = _mixer_in(hn0, w_in, ex.need("sc_conv_w", hn0))
    ex.at("l0_in", y)
    h1 = _mm_rows("l0_out", y, ex.need("sc_w_out", y), NN, F32, D, tn=512, add=x)
    ex.at("l0_out", h1)
    h2, ffn0 = _ffn_layer_fwd("f0", h1, _rms_fwd("f0_norm", h1, ffn_norm[0:1]), ex)

    w_uq = ex.need("w_uq", h2)
    hk, hn1, ckv_raw, ckv, kr, kn, vv, cq_raw, cq, q = _attn_pre(
        h2, rep["kv_in_norm"], attn_norm[1:2], rep["kv_latent_norm"], rep["q_latent_norm"], ex.need("w_dkv", h2),
        ex.need("w_kr", h2), ex.need("w_uk", h2), ex.need("w_uv", h2), ex.need("w_dq", h2), w_uq, tables)

    o, lse = _attn_fwd(q, kn, kr, vv)
    ex.at("attn_fwd", o)
    w_o = ex.need("w_o", o)
    h3, hf1 = _out_norm("attn_out", o, w_o, h2, ffn_norm[1:2])
    (loss, dh4, dh4_bf, d_final), ffn1 = _ffn_layer_fwd("f1", h3, hf1, ex, final=(final_norm.reshape(1, D), tgt))

    dh3, dh3_bf, d_fn1, dcw1, dcb1, do = _ffn_layer_bwd("f1", h3, ffn_norm[1:2], ex, ffn1, dh4, dh4_bf,
                                                        post=(w_o, NT, N_HEADS * V_HEAD))
    ex.at("f1_bwd", dh3)

    dq_pre, dkn, dkr, dvv = _attn_bwd(q, kn, kr, vv, o, do, lse, tables)

    dcq_raw_bf, dckv_raw_bf, dkr_raw_bf, dh2, dh2_bf, d_qln, d_kvln, d_an1, d_kvin = _attn_post(
        dq_pre, dkn, dvv, dkr, cq_raw, ckv_raw, h2, dh3, rep["q_latent_norm"], rep["kv_latent_norm"], attn_norm[1:2],
        rep["kv_in_norm"], w_uq, ex.need("w_uk", dkn), ex.need("w_uv", dvv), ex.need("w_dq", dq_pre),
        ex.need("w_dkv", dkn), ex.need("w_kr", dkr), tables)
    g_uq, g_dq, g_o = _wgrads("g_q", [(dq_pre, cq, QK_NOPE + QK_ROPE), (hn1, dcq_raw_bf), (o, dh3_bf)])
    ex.grad("w_uq", None, g_uq.reshape(1, N_DEV, QK_NOPE + QK_ROPE, Q_LORA))
    ex.grad("w_dq", None, g_dq.reshape(1, N_DEV, D // N_DEV, Q_LORA))
    ex.grad("w_o", None, g_o.reshape(1, N_DEV, D // N_DEV, D))

    g_uk, g_uv, g_dkv, g_kr = _wgrads("g_kv", [(ckv, dkn), (ckv, dvv), (hk, dckv_raw_bf), (dkr_raw_bf, hk, QK_ROPE)])
    ex.grad("w_uk", None, g_uk)
    ex.grad("w_uv", None, g_uv)
    ex.grad("w_dkv", None, g_dkv.reshape(1, N_DEV, D // N_DEV, KV_LORA))
    ex.grad("w_kr", None, g_kr)
    ex.at("kv_bwd", dh2)

    dh1, dh1_bf, d_fn0, dcw0, dcb0 = _ffn_layer_bwd("f0", h1, ffn_norm[0:1], ex, ffn0, dh2, dh2_bf)
    ex.at("f0_bwd", dh1)

    ex.grad("sc_w_out", None, _mm_wgrad("g_sc_w_out", y, dh1_bf).reshape(1, N_DEV, D // N_DEV, D))
    dz, d_scw = _mixer_out_bwd(dh1_bf, ex.need("sc_w_out", dh1_bf), zb, zc, zu, ex.need("sc_conv_w", dh1_bf))
    g_in = _mm_wgrad("g_sc_w_in", hn0, dz)
    ex.grad("sc_w_in", None, g_in)
    ex.at("sc_bwd", g_in)
    ex.at("d_l0_in", g_in)
    w_in = ex.need("sc_w_in", dz)
    grad_x, _, (d_an0,) = _mm_sum(
        "d_l0_in", [(dz[None], pl.BlockSpec((1, TS, dz.shape[1]), lambda i: (0, i, 0)),
                     w_in[None], pl.BlockSpec((1,) + w_in.shape, lambda i: (0, 0, 0)), NT, 0)],
        norm_bwd=(x, [attn_norm[0:1]], dh1),
        grid=(T // TS,), o_spec=pl.BlockSpec((TS, D), lambda i: (i, 0)), o_shape=(T, D), o_dtype=F32)

    small = {
        "attn_norm": jnp.concatenate([d_an0, d_an1], axis=0),
        "ffn_norm": jnp.concatenate([d_fn0, d_fn1], axis=0),
        "final_norm": d_final.reshape(D),
        "kv_in_norm": d_kvin.reshape(D),
        "kv_latent_norm": d_kvln.reshape(KV_LORA),
        "q_latent_norm": d_qln,
        "ffn_conv_b": jnp.stack([dcb0, dcb1]).transpose(0, 2, 1, 3).reshape(2, D_FF),
        "sc_conv_w": d_scw,
        "ffn_conv_w": jnp.stack([dcw0, dcw1]).transpose(0, 2, 1, 3).reshape(2, 3, D_FF),
    }
    return loss, grad_x, small


def _place():
    return lax.axis_index("x"), lax.axis_index("y"), lax.axis_index("c")


def _peers():
    x, y, c = _place()
    return (x, y, 1 - c), [(1 - x, y), (x, 1 - y), (1 - x, 1 - y)]


def _window(ref, kind, dev):
    if kind == "blocked":
        return ref.at[:, dev]
    width = ref.shape[-1] // N_DEV
    return ref.at[:, pl.ds(pl.multiple_of(dev * width, 128), width)]


HBM_SPEC = pl.BlockSpec(memory_space=pltpu.HBM)
SEM_SPEC = pl.BlockSpec(memory_space=pltpu.SEMAPHORE)
EFFECT = pltpu.SideEffectType.DATAFLOW_SIDE_EFFECTING
TOKEN = jax.ShapeDtypeStruct((8, 128), F32)


def _hbm(a):
    return pltpu.with_memory_space_constraint(a, pltpu.HBM)


def _copies_start(name, jobs):
    nj = len(jobs)
    counts = [(len(srcs), len(lands)) for srcs, lands, _, _ in jobs]
    n_arr = sum(ns + nl for ns, nl in counts)

    def body(*refs):
        sems, token = refs[n_arr:n_arr + 2 * nj], refs[-1]
        at = 0
        for j, ((ns, nl), (_, _, ncopy, plan)) in enumerate(zip(counts, jobs)):
            copies = plan(refs[at:at + ns], refs[at + ns:at + ns + nl])
            assert len(copies) == ncopy
            for k, (sent, dst, to, _) in enumerate(copies):
                pltpu.make_async_remote_copy(src_ref=sent, dst_ref=dst, send_sem=sems[2 * j].at[k],
                                             recv_sem=sems[2 * j + 1].at[k], device_id=to, device_id_type=MESH).start()
            at += ns + nl
        token[...] = jnp.zeros_like(token)

    arrays = [a for srcs, lands, _, _ in jobs for a in list(srcs) + list(lands)]
    sem_shapes = [pltpu.SemaphoreType.DMA((ncopy,)) for _, _, ncopy, _ in jobs for _ in range(2)]
    outs = pl.pallas_call(
        body, name=name, in_specs=[HBM_SPEC] * n_arr,
        out_specs=[SEM_SPEC] * (2 * nj) + [HBM_SPEC] * n_arr + [VMEM_SPEC],
        out_shape=sem_shapes + [pltpu.HBM(a.shape, a.dtype) for a in arrays] + [TOKEN],
        input_output_aliases={i: 2 * nj + i for i in range(n_arr)},
        compiler_params=pltpu.CompilerParams(has_side_effects=EFFECT))(*[_hbm(a) for a in arrays])
    _Chain.last = outs[-1]
    flights, at = [], 2 * nj
    for j, (ns, nl) in enumerate(counts):
        flights.append((outs[2 * j], outs[2 * j + 1], list(outs[at:at + ns]), list(outs[at + ns:at + ns + nl])))
        at += ns + nl
    return flights


def _copies_wait(name, started, ncopy, plan):
    send, recv, srcs, lands = started
    ns, nl = len(srcs), len(lands)

    def body(*refs):
        send_ref, recv_ref, token = refs[ns + nl], refs[ns + nl + 1], refs[-1]
        copies = plan(refs[:ns], refs[ns:ns + nl])
        assert len(copies) == ncopy
        for k, (sent, _, to, landed) in enumerate(copies):
            cp = pltpu.make_async_remote_copy(src_ref=sent, dst_ref=landed, send_sem=send_ref.at[k],
                                              recv_sem=recv_ref.at[k], device_id=to, device_id_type=MESH)
            cp.wait_send()
            cp.wait_recv()
        token[...] = jnp.zeros_like(token)

    arrays = list(srcs) + list(lands)
    outs = pl.pallas_call(
        body, name=name, in_specs=[HBM_SPEC] * (ns + nl) + [SEM_SPEC] * 2 + [ANY_SPEC],
        out_specs=[HBM_SPEC] * (ns + nl) + [VMEM_SPEC], out_shape=[pltpu.HBM(a.shape, a.dtype) for a in arrays] + [TOKEN],
        input_output_aliases={i: i for i in range(ns + nl)},
        compiler_params=pltpu.CompilerParams(has_side_effects=EFFECT))(*arrays, send, recv, _Chain.last)
    _Chain.last = outs[-1]
    return list(outs[:ns]), list(outs[ns:-1])


def _plan_gather_chips(kinds):
    def plan(srcs, lands):
        x, y, c = _place()
        sibling, chips = _peers()
        out = []
        for t, kind in enumerate(kinds):
            mine = _window(lands[t], kind, 4 * x + 2 * y + c)
            out.append((srcs[t], mine, (x, y, c), mine))
            out.append((srcs[t], mine, sibling, _window(lands[t], kind, 4 * x + 2 * y + 1 - c)))
            for px, py in chips:
                out.append((srcs[t], mine, (px, py, c), _window(lands[t], kind, 4 * px + 2 * py + c)))
        return out
    return plan, 5 * len(kinds)


def _plan_gather_all(n):
    def plan(srcs, lands):
        x, y, c = _place()
        out = []
        for t in range(n):
            mine = lands[t].at[:, 4 * x + 2 * y + c]
            for m in range(N_DEV):
                px, py, pc = (1 - x if m & 4 else x), (1 - y if m & 2 else y), (1 - c if m & 1 else c)
                out.append((srcs[t], mine, (px, py, pc), lands[t].at[:, 4 * px + 2 * py + pc]))
        return out
    return plan, N_DEV * n


def _plan_gather_sibling(kinds):
    def plan(srcs, lands):
        _, _, c = _place()
        sibling, chips = _peers()
        out = []
        for t, kind in enumerate(kinds):
            for px, py in chips:
                w = _window(lands[t], kind, 4 * px + 2 * py + c)
                out.append((w, w, sibling, _window(lands[t], kind, 4 * px + 2 * py + 1 - c)))
        return out
    return plan, 3 * len(kinds)


def _plan_scatter_sibling(kinds):
    def plan(srcs, lands):
        _, _, c = _place()
        sibling, _ = _peers()
        out = []
        for t, kind in enumerate(kinds):
            for k in range(N_CHIP):
                out.append((_window(srcs[t], kind, 2 * k + 1 - c), lands[t].at[k], sibling, lands[t].at[k]))
        return out
    return plan, N_CHIP * len(kinds)


def _plan_scatter_chips(n):
    def plan(srcs, lands):
        x, y, c = _place()
        _, chips = _peers()
        out = []
        for t in range(n):
            for px, py in chips:
                out.append((srcs[t].at[2 * px + py], lands[t].at[2 * x + y], (px, py, c), lands[t].at[2 * px + py]))
        return out
    return plan, 3 * n


def _landing(shard, kind):
    if kind == "blocked":
        return lax.empty((shard.shape[0], N_DEV) + shard.shape[1:], shard.dtype)
    return lax.empty((shard.shape[0], N_DEV * shard.shape[1]), shard.dtype)


def _chip_sums(name, grads, kinds, recvs, c):
    n = len(grads)
    in_specs, out_specs, out_shape, args = [], [], [], []
    for gr, kind, rv in zip(grads, kinds, recvs):
        if kind == "blocked":
            rows, w = gr.shape[2], gr.shape[3]
            in_specs.append(pl.BlockSpec((None, None, rows, w), lambda k, cref: (0, 2 * k + cref[0], 0, 0)))
        else:
            rows, w = gr.shape[0], gr.shape[1] // N_DEV
            in_specs.append(pl.BlockSpec((rows, w), lambda k, cref: (0, 2 * k + cref[0])))
        blk = pl.BlockSpec((None, rows, w), lambda k, cref: (k, 0, 0))
        in_specs.append(blk)
        out_specs.append(blk)
        out_shape.append(jax.ShapeDtypeStruct((N_CHIP, rows, w), BF16))
        args += [gr, rv.reshape(N_CHIP, rows, w)]

    def body(*refs):
        for t in range(n):
            g_ref, r_ref, o_ref = refs[1 + 2 * t], refs[2 + 2 * t], refs[1 + 2 * n + t]
            o_ref[...] = (g_ref[...].astype(F32) + r_ref[...].astype(F32)).astype(BF16)

    return _pallas(body, name=name, n_prefetch=1, grid=(N_CHIP,), in_specs=in_specs, out_specs=out_specs,
                   out_shape=out_shape, params=_params(("parallel",)))(c, *args)


def _adamw_math(g, wv, mv, vv):
    m = ADAM_B1 * mv + (1.0 - ADAM_B1) * g
    v = ADAM_B2 * vv + (1.0 - ADAM_B2) * (g * g)
    m_hat = m / (1.0 - ADAM_B1 ** ADAM_STEP)
    v_hat = v / (1.0 - ADAM_B2 ** ADAM_STEP)
    delta = -ADAM_LR * (m_hat / (jnp.sqrt(v_hat) + ADAM_EPS) + ADAM_WD * wv)
    return delta, m, v


ADAM_STEPS = 2


def _adamw_group(name, items, chip_ids):
    n = len(items)
    in_specs, out_specs, out_shape, args, prevs = [], [], [], [chip_ids], []
    for own, recv, w3, m3, v3, layer, _ in items:
        nl, rows, w = w3.shape
        tr = rows // ADAM_STEPS
        assert tr % 16 == 0, (name, rows)
        in_specs += [pl.BlockSpec((None, tr, w), lambda i, ids, slot=slot: (ids[slot], i, 0)) for slot in range(4)]
        slab = pl.BlockSpec((None, tr, w), lambda i, ids, layer=layer: (layer, i, 0))
        in_specs += [slab] * 3
        out_specs += [slab] * 4
        out_shape += [jax.ShapeDtypeStruct((nl, rows, w), F32)] * 4
        args += [own, recv, recv, recv, w3, m3, v3]
    aliases = {}
    for t, item in enumerate(items):
        if item[6] is not None:
            for k in range(4):
                aliases[len(args) + k] = 4 * t + k
            in_specs += [ANY_SPEC] * 4
            args += list(item[6])
            prevs.append(t)
    n_in = 1 + 7 * n + 4 * len(prevs)

    def body(*refs):
        for t in range(n):
            own_ref, r1_ref, r2_ref, r3_ref, w_ref, m_ref, v_ref = refs[1 + 7 * t:8 + 7 * t]
            g_ref, d_ref, nm_ref, nv_ref = refs[n_in + 4 * t:n_in + 4 * t + 4]
            g = ((own_ref[...].astype(F32) + r1_ref[...].astype(F32)) + r2_ref[...].astype(F32)) + r3_ref[...].astype(F32)
            g_ref[...] = g
            d_ref[...], nm_ref[...], nv_ref[...] = _adamw_math(g, w_ref[...], m_ref[...], v_ref[...])

    outs = _pallas(body, name=name, n_prefetch=1, grid=(ADAM_STEPS,), in_specs=in_specs, out_specs=out_specs,
                   out_shape=out_shape, aliases=aliases, params=_params(("parallel",)))(*args)
    return [list(outs[4 * t:4 * t + 4]) for t in range(n)]


SHARD_ROWS = 8


def _adamw_small(gathered, ws, ms, vs, me):
    n = len(gathered)
    full = [w is not None for w in ws]
    sharded = [w is not None and w.ndim == 3 for w in ws]
    args = list(gathered)
    out_shape = []
    for t in range(n):
        shape = jax.ShapeDtypeStruct(ws[t].shape if sharded[t] else gathered[t].shape[2:], F32)
        if full[t]:
            args += [ws[t], ms[t], vs[t]]
            out_shape += [shape] * 4
        else:
            out_shape += [shape]

    def body(*refs):
        i_in, i_out = n, len(args) + 1
        me_ref = refs[len(args)]
        for t in range(n):
            p_ref = refs[t]
            if sharded[t]:
                w_ref, m_ref, v_ref = refs[i_in:i_in + 3]
                taps, layers, _ = w_ref.shape
                mine = pl.ds(pl.multiple_of(me_ref[0] * SHARD_ROWS, SHARD_ROWS), SHARD_ROWS)
                g = p_ref[0, 0, mine, :]
                for k in range(1, N_DEV):
                    g = g + p_ref[0, k, mine, :]
                for l in range(layers):
                    for k in range(taps):
                        at = (k, slice(l, l + 1), slice(None))
                        row = g[l * taps + k:l * taps + k + 1]
                        refs[i_out][at] = row
                        refs[i_out + 1][at], refs[i_out + 2][at], refs[i_out + 3][at] = _adamw_math(
                            row, w_ref[at], m_ref[at], v_ref[at])
                i_in += 3
                i_out += 4
                continue
            g = p_ref[0, 0]
            for k in range(1, N_DEV):
                g = g + p_ref[0, k]
            refs[i_out][...] = g
            if full[t]:
                w_ref, m_ref, v_ref = refs[i_in:i_in + 3]
                refs[i_out + 1][...], refs[i_out + 2][...], refs[i_out + 3][...] = _adamw_math(
                    g, w_ref[...], m_ref[...], v_ref[...])
                i_in += 3
                i_out += 4
            else:
                i_out += 1

    outs = _pallas(body, name="adamw_small",
                   in_specs=[VMEM_SPEC] * len(args) + [pl.BlockSpec(memory_space=pltpu.SMEM)],
                   out_specs=[VMEM_SPEC] * len(out_shape), out_shape=out_shape,
                   params=pltpu.CompilerParams(vmem_limit_bytes=VMEM_LIMIT))(*args, me)
    result, i = [], 0
    for t in range(n):
        k = 4 if full[t] else 1
        result.append(list(outs[i:i + k]))
        i += k
    return result


KIND = {"sc_w_in": "cols", "sc_w_out": "blocked", "w_dkv": "blocked", "w_kr": "cols", "w_uk": "cols", "w_uv": "cols",
        "w_dq": "blocked", "w_uq": "blocked", "w_o": "blocked", "ffn_w_up": "blocked", "ffn_w_down": "blocked",
        "conv": "blocked"}
GATHER_GROUPS = (("mixer", ("sc_w_in", "sc_w_out", "conv")),
                 ("up0", ("ffn_w_up0",)),
                 ("down0", ("ffn_w_down0",)),
                 ("attn", ("w_dkv", "w_kr", "w_uk", "w_uv", "w_dq", "w_uq", "w_o")),
                 ("ffn1", ("ffn_w_up1", "ffn_w_down1")))
SCATTER_GROUPS = (("ffn1", (("ffn_w_up", 1), ("ffn_w_down", 1))),
                  ("attn", (("w_o", None), ("w_uq", None), ("w_dq", None), ("w_uk", None), ("w_uv", None),
                            ("w_dkv", None), ("w_kr", None))),
                  ("ffn0", (("ffn_w_up", 0), ("ffn_w_down", 0))),
                  ("mixer", (("sc_w_out", None), ("sc_w_in", None))))
SCHEDULE = {
    "begin": (("gather_start", "mixer"),),
    "l0_norm": (("gather_forward", "mixer"), ("gather_start", "up0")),
    "l0_out": (("gather_forward", "up0"), ("gather_start", "down0"), ("gather_start", "attn")),
    "f0_up": (("gather_forward", "down0"), ("gather_forward", "attn"), ("gather_start", "ffn1")),
    "attn_fwd": (("gather_forward", "ffn1"),),
    "f1_gup": (("scatter_sibling", "ffn1"),),
    "f1_dhf": (("scatter_chips", "ffn1"),),
    "kv_bwd": (("scatter_sibling", "attn"),),
    "f0_dact": (("scatter_chips", "attn"),),
    "f0_gup": (("scatter_sibling", "ffn0"),),
    "f0_dhf": (("scatter_chips", "ffn0"),),
    "sc_bwd": (("scatter_sibling", "mixer"), ("scatter_done", "attn")),
    "d_l0_in": (("scatter_chips", "mixer"),),
}
FINISH = (("scatter_done", "ffn1"), ("scatter_done", "ffn0"), ("scatter_done", "mixer"))
STAGES = {"gather_start": 1, "gather_forward": 2, "gather_done": 3,
          "scatter_sibling": 1, "scatter_chips": 2, "scatter_done": 3}
SMALL_W_ROWS = 24


def _pack(arrays, rows):
    flat = jnp.concatenate([a.reshape(-1).astype(F32) for a in arrays])
    return jnp.pad(flat, (0, rows * 128 - flat.shape[0])).reshape(rows, 128)


def _cast_shards(items):
    arrays = []
    for a, _, _ in items:
        if not any(a is b for b in arrays):
            arrays.append(a)
    slot = [next(i for i, b in enumerate(arrays) if b is a) for a, _, _ in items]

    def body(*refs):
        for t, (a, layer, rows) in enumerate(items):
            w_ref, o_ref = refs[slot[t]], refs[len(arrays) + t]
            r, c = a.shape[-2:]
            if a.ndim == 3:
                o_ref[:, :r] = w_ref[(layer or 0):(layer or 0) + 1].astype(BF16)
                if rows > r:
                    o_ref[:, r:] = jnp.zeros((1, rows - r, c), BF16)
            else:
                o_ref[:r] = w_ref[...].astype(BF16)
                if rows > r:
                    o_ref[r:] = jnp.zeros((rows - r, c), BF16)

    out_shape = [jax.ShapeDtypeStruct(((1,) if a.ndim == 3 else ()) + (rows, a.shape[-1]), BF16)
                 for a, _, rows in items]
    return _pallas(body, name="cast_shards", in_specs=[VMEM_SPEC] * len(arrays), out_specs=[VMEM_SPEC] * len(items),
                   out_shape=out_shape, params=pltpu.CompilerParams(vmem_limit_bytes=VMEM_LIMIT))(*arrays)


STORED_TRANSPOSED = ("ffn_w_up", "w_uq", "w_kr")


def _stored(name, a):
    return jnp.swapaxes(a, -1, -2) if name in STORED_TRANSPOSED else a


def _base(name):
    if name.startswith("ffn_w_") and name[-1] in "01":
        return name[:-1], int(name[-1])
    return name, None


class _Exchange:
    def __init__(self, wts, mom, var, ffn_conv_b):
        self.wts, self.mom, self.var = wts, mom, var
        x, y, c = _place()
        self.c_arr = jnp.reshape(c, (1,)).astype(jnp.int32)
        chip = 2 * x + y
        self.chip_ids = jnp.stack([chip, chip ^ 1, chip ^ 2, chip ^ 3]).astype(jnp.int32)
        self.ready = {"ffn_cb0": ffn_conv_b.reshape(2, N_FF_BLK, 1, FF_BLK)[0],
                      "ffn_cb1": ffn_conv_b.reshape(2, N_FF_BLK, 1, FF_BLK)[1]}
        self.gathers, self.group_of = {}, {}
        self.grads, self.scatters, self.results, self.queue = {}, {}, {}, []
        for gname, names in GATHER_GROUPS:
            self.gathers[gname] = dict(stage=0, names=names, kinds=[KIND[_base(nm)[0]] for nm in names])
            for nm in names:
                self.group_of[nm] = gname
        for nm in ("sc_conv_w", "ffn_cw0", "ffn_cw1"):
            self.group_of[nm] = "mixer"
        self.cast, self.f32 = {}, {}
        self.at("begin", None)
        later = [nm for gname, names in GATHER_GROUPS[1:] for nm in names]
        self.cast = dict(zip(later, _cast_shards([self._shard_f32(nm) for nm in later])))

    def _shard_f32(self, name):
        base, layer = _base(name)
        if base not in self.f32:
            a = _stored(base, self.wts[base])
            self.f32[base] = a.reshape(a.shape[-2:]) if KIND[base] == "cols" else a.reshape((-1,) + a.shape[-2:])
        a = self.f32[base]
        return a, layer, {"w_kr": 128, "w_uq": QK_PAD}.get(base, a.shape[-2])

    def _shard(self, name):
        if name in self.cast:
            return self.cast[name]
        if name == "conv":
            return _pack([self.wts["sc_conv_w"], self.wts["ffn_conv_w"]], SMALL_W_ROWS).reshape(1, SMALL_W_ROWS, 128)
        base, layer = _base(name)
        a = _stored(base, self.wts[base])
        if layer is not None:
            a = a[layer:layer + 1]
        if KIND[base] == "cols":
            return a.reshape(a.shape[-2], a.shape[-1]).astype(BF16)
        return a.reshape((-1,) + a.shape[-2:]).astype(BF16)

    def _start(self, name, srcs, lands, ncopy, plan, st):
        self.queue.append((name, (srcs, lands, ncopy, plan), st))

    def _flush(self):
        if self.queue:
            flights = _copies_start("__".join(name for name, _, _ in self.queue), [job for _, job, _ in self.queue])
            for (_, _, st), flight in zip(self.queue, flights):
                st["flight"] = flight
            self.queue = []

    def _flight(self, st):
        self._flush()
        return st["flight"]

    def _gather_to(self, gname, stage, after):
        st = self.gathers[gname]
        if st["stage"] < 1 <= stage:
            shards = [self._shard(nm) for nm in st["names"]]
            lands = [_landing(s, kind) for s, kind in zip(shards, st["kinds"])]
            plan, ncopy = _plan_gather_chips(st["kinds"])
            self._start(f"ag_{gname}_chips", shards, lands, ncopy, plan, st)
            st["stage"] = 1
        if st["stage"] < 2 <= stage:
            plan, ncopy = _plan_gather_chips(st["kinds"])
            _, lands = _copies_wait(f"ag_{gname}_chips_wait", self._flight(st), ncopy, plan)
            plan, ncopy = _plan_gather_sibling(st["kinds"])
            self._start(f"ag_{gname}_sibling", [], lands, ncopy, plan, st)
            st["stage"] = 2
        if st["stage"] < 3 <= stage:
            plan, ncopy = _plan_gather_sibling(st["kinds"])
            _, lands = _copies_wait(f"ag_{gname}_sibling_wait", self._flight(st), ncopy, plan)
            for nm, land in zip(st["names"], lands):
                self._arrived(nm, land)
            st["stage"] = 3

    def _arrived(self, name, land):
        if name == "conv":
            conv = land.reshape(N_DEV, SMALL_W_ROWS * 128)
            self.ready["sc_conv_w"] = conv[:, :3 * 128].reshape(N_DEV, 3, 128).transpose(1, 0, 2).reshape(3, D)
            fcw = conv[:, 3 * 128:3 * 128 + 6 * 352].reshape(N_DEV, 2, 3, 352).transpose(1, 2, 0, 3)
            fcw = fcw.reshape(2, 3, N_FF_BLK, FF_BLK).transpose(0, 2, 1, 3)
            self.ready["ffn_cw0"], self.ready["ffn_cw1"] = fcw[0], fcw[1]
        elif name in ("sc_w_in", "w_uk", "w_uv", "w_kr") or name.startswith("ffn_w_up"):
            self.ready[name] = land
        elif name.startswith("ffn_w_down"):
            self.ready[name] = land.reshape(1, N_FF_BLK, FF_BLK, D)
        elif name == "w_uq":
            self.ready[name] = land.reshape(N_HEADS, QK_PAD, Q_LORA)
        else:
            self.ready[name] = land.reshape(D, land.shape[-1])

    def need(self, name, after):
        if name not in self.ready:
            self._gather_to(self.group_of[name], 3, after)
            self._flush()
        return self.ready[name]

    def grad(self, name, layer, array):
        self.grads[(name, layer)] = array

    def _scatter_to(self, gname, stage, after):
        keys = dict(SCATTER_GROUPS)[gname]
        st = self.scatters.setdefault(gname, dict(stage=0))
        kinds = [KIND[nm] for nm, _ in keys]
        if st["stage"] < 1 <= stage:
            grads = [self.grads[key] for key in keys]
            lands = []
            for gr, kind in zip(grads, kinds):
                shard = (gr.shape[0],) + gr.shape[2:] if kind == "blocked" else (gr.shape[0], gr.shape[1] // N_DEV)
                lands.append(lax.empty((N_CHIP,) + shard, BF16))
            plan, ncopy = _plan_scatter_sibling(kinds)
            self._start(f"rs_{gname}_sibling", grads, lands, ncopy, plan, st)
            st["stage"] = 1
        if st["stage"] < 2 <= stage:
            plan, ncopy = _plan_scatter_sibling(kinds)
            grads, recvs = _copies_wait(f"rs_{gname}_sibling_wait", self._flight(st), ncopy, plan)
            sums = _chip_sums(f"rs_{gname}_sums", grads, kinds, recvs, self.c_arr)
            lands = [lax.empty(s.shape, BF16) for s in sums]
            plan, ncopy = _plan_scatter_chips(len(sums))
            self._start(f"rs_{gname}_chips", sums, lands, ncopy, plan, st)
            st["stage"] = 2
        if st["stage"] < 3 <= stage:
            plan, ncopy = _plan_scatter_chips(len(keys))
            sums, recvs = _copies_wait(f"rs_{gname}_chips_wait", self._flight(st), ncopy, plan)
            items = []
            for (nm, layer), own, rv in zip(keys, sums, recvs):
                nl = 1 if layer is None else 2
                rows, w = own.shape[1], own.shape[2]
                w3, m3, v3 = (_stored(nm, src[nm]).reshape(nl, rows, w) for src in (self.wts, self.mom, self.var))
                items.append((own, rv, w3, m3, v3, 0 if layer is None else layer, self.results.get(nm)))
            outs = _adamw_group(f"adamw_{gname}", items, self.chip_ids)
            for (nm, _), out in zip(keys, outs):
                self.results[nm] = out
            st["stage"] = 3

    def at(self, place, after):
        for action, gname in SCHEDULE.get(place, ()):
            self._advance(action, gname, after)
        self._flush()

    def _advance(self, action, gname, after):
        if action.startswith("gather"):
            self._gather_to(gname, STAGES[action], after)
        else:
            self._scatter_to(gname, STAGES[action], after)

    def finish(self, after):
        for action, gname in FINISH:
            self._advance(action, gname, after)
        for gname, _ in SCATTER_GROUPS:
            self._scatter_to(gname, 3, after)
        return {nm: [_stored(nm, o.reshape(_stored(nm, self.wts[nm]).shape)) for o in outs]
                for nm, outs in self.results.items()}


REPLICATED = ("attn_norm", "ffn_norm", "final_norm", "kv_in_norm", "kv_latent_norm", "q_latent_norm", "ffn_conv_b")
WEIGHTS = ("attn_norm", "ffn_norm", "final_norm", "sc_w_in", "sc_conv_w", "sc_w_out", "kv_in_norm", "w_dkv",
           "kv_latent_norm", "w_kr", "w_uk", "w_uv", "w_dq", "q_latent_norm", "w_uq", "w_o", "ffn_w_up", "ffn_conv_w",
           "ffn_conv_b", "ffn_w_down")


def kernel(x, positions, attn_norm, ffn_norm, final_norm, sc_w_in, sc_conv_w, sc_w_out, kv_in_norm, w_dkv, kv_latent_norm, w_kr, w_uk, w_uv, w_dq, q_latent_norm, w_uq, w_o, ffn_w_up, ffn_conv_w, ffn_conv_b, ffn_w_down, loss_target, m_attn_norm, m_ffn_norm, m_final_norm, m_sc_w_in, m_sc_conv_w, m_sc_w_out, m_kv_in_norm, m_w_dkv, m_kv_latent_norm, m_w_kr, m_w_uk, m_w_uv, m_w_dq, m_q_latent_norm, m_w_uq, m_w_o, m_ffn_w_up, m_ffn_conv_w, m_ffn_conv_b, m_ffn_w_down, v_attn_norm, v_ffn_norm, v_final_norm, v_sc_w_in, v_sc_conv_w, v_sc_w_out, v_kv_in_norm, v_w_dkv, v_kv_latent_norm, v_w_kr, v_w_uk, v_w_uv, v_w_dq, v_q_latent_norm, v_w_uq, v_w_o, v_ffn_w_up, v_ffn_conv_w, v_ffn_conv_b, v_ffn_w_down):
    wts = dict(attn_norm=attn_norm, ffn_norm=ffn_norm, final_norm=final_norm, sc_w_in=sc_w_in, sc_conv_w=sc_conv_w,
               sc_w_out=sc_w_out, kv_in_norm=kv_in_norm, w_dkv=w_dkv, kv_latent_norm=kv_latent_norm, w_kr=w_kr,
               w_uk=w_uk, w_uv=w_uv, w_dq=w_dq, q_latent_norm=q_latent_norm, w_uq=w_uq, w_o=w_o, ffn_w_up=ffn_w_up,
               ffn_conv_w=ffn_conv_w, ffn_conv_b=ffn_conv_b, ffn_w_down=ffn_w_down)
    mom = dict(attn_norm=m_attn_norm, ffn_norm=m_ffn_norm, final_norm=m_final_norm, sc_w_in=m_sc_w_in,
               sc_conv_w=m_sc_conv_w, sc_w_out=m_sc_w_out, kv_in_norm=m_kv_in_norm, w_dkv=m_w_dkv,
               kv_latent_norm=m_kv_latent_norm, w_kr=m_w_kr, w_uk=m_w_uk, w_uv=m_w_uv, w_dq=m_w_dq,
               q_latent_norm=m_q_latent_norm, w_uq=m_w_uq, w_o=m_w_o, ffn_w_up=m_ffn_w_up, ffn_conv_w=m_ffn_conv_w,
               ffn_conv_b=m_ffn_conv_b, ffn_w_down=m_ffn_w_down)
    var = dict(attn_norm=v_attn_norm, ffn_norm=v_ffn_norm, final_norm=v_final_norm, sc_w_in=v_sc_w_in,
               sc_conv_w=v_sc_conv_w, sc_w_out=v_sc_w_out, kv_in_norm=v_kv_in_norm, w_dkv=v_w_dkv,
               kv_latent_norm=v_kv_latent_norm, w_kr=v_w_kr, w_uk=v_w_uk, w_uv=v_w_uv, w_dq=v_w_dq,
               q_latent_norm=v_q_latent_norm, w_uq=v_w_uq, w_o=v_w_o, ffn_w_up=v_ffn_w_up, ffn_conv_w=v_ffn_conv_w,
               ffn_conv_b=v_ffn_conv_b, ffn_w_down=v_ffn_w_down)
    xi, yi, ci = _place()
    me = 4 * xi + 2 * yi + ci
    _Chain.last = None

    ex = _Exchange(wts, mom, var, ffn_conv_b)
    rep = {
        "attn_norm": attn_norm, "ffn_norm": ffn_norm, "final_norm": final_norm,
        "kv_in_norm": kv_in_norm.reshape(1, D), "kv_latent_norm": kv_latent_norm.reshape(1, KV_LORA),
        "q_latent_norm": q_latent_norm.reshape(1, Q_LORA),
    }
    loss, grad_x, small = _local_step(x.reshape(T, D), positions.reshape(T, 1), loss_target.reshape(T, D), rep, ex)

    def rows_of(a):
        return a.reshape(-1, a.shape[-1])

    def device_rows(a):
        taps, c = a.shape[-2], a.shape[-1] // N_DEV
        rows = a.reshape(-1, taps, N_DEV, c).transpose(2, 0, 1, 3).reshape(N_DEV, -1, c)
        return jnp.pad(rows, ((0, 0), (0, SHARD_ROWS - rows.shape[1]), (0, 0))).reshape(N_DEV * SHARD_ROWS, c)

    def taps_first(a):
        return jnp.transpose(a, (1, 0, 2))

    sharded = ("sc_conv_w", "ffn_conv_w")
    shards = ([loss.reshape(1, 1, 128)] + [rows_of(small[nm])[None] for nm in REPLICATED]
              + [device_rows(small[nm])[None] for nm in sharded])
    plan, ncopy = _plan_gather_all(len(shards))
    flight, = _copies_start("ag_small", [(shards, [lax.empty((1, N_DEV) + s.shape[1:], F32) for s in shards], ncopy, plan)])
    results = ex.finish(grad_x)
    _, gathered = _copies_wait("ag_small_wait", flight, ncopy, plan)
    params = [[None] + [rows_of(src[nm]) for nm in REPLICATED] + [taps_first(src[nm]) for nm in sharded]
              for src in (wts, mom, var)]
    summed = _adamw_small(gathered, *params, me.astype(jnp.int32).reshape(1))
    loss_total = summed[0][0][0, 0]
    for nm, vals in zip(REPLICATED, summed[1:1 + len(REPLICATED)]):
        results[nm] = [a.reshape(wts[nm].shape) for a in vals]
    for nm, vals in zip(sharded, summed[1 + len(REPLICATED):]):
        results[nm] = [taps_first(a) for a in vals]

    outs = [loss_total, grad_x.reshape(1, T, D)]
    for slot in range(4):
        outs.extend(results[nm][slot] for nm in WEIGHTS)
    return tuple(outs)
```

```python
import jax
import jax.numpy as jnp
from jax import lax
from jax.experimental import pallas as pl
from jax.experimental.pallas import tpu as pltpu

F32 = jnp.float32
BF16 = jnp.bfloat16

T = 2048
D = 1024
N_HEADS = 8
QK_NOPE = 128
QK_ROPE = 64
V_HEAD = 128
Q_LORA = 384
KV_LORA = 256
D_FF = 2816
CHUNK = 64
ROPE_THETA = 10000.0
EPS = 1e-6
NEG_INF = -1e30
ADAM_LR = 0.001
ADAM_B1 = 0.9
ADAM_B2 = 0.999
ADAM_EPS = 1e-08
ADAM_WD = 0.01
ADAM_STEP = 10

N_DEV = 8
N_CHIP = 4
FF_BLK = D_FF * 2 // N_DEV
N_FF_BLK = D_FF // FF_BLK
QK_PAD = 256
HALO = 16

TM = 1024
TS = 512
TR = 256
TQ = 512
VMEM_LIMIT = 56 * 1024 * 1024

NN = (((1,), (0,)), ((), ()))
NT = (((1,), (1,)), ((), ()))
TN = (((0,), (0,)), ((), ()))
MESH = pl.DeviceIdType.MESH


def _params(sem):
    return pltpu.CompilerParams(dimension_semantics=sem, vmem_limit_bytes=VMEM_LIMIT)


ANY_SPEC = pl.BlockSpec(memory_space=pl.ANY)
VMEM_SPEC = pl.BlockSpec(memory_space=pltpu.VMEM)


class _Chain:
    last = None


def _pallas(body, *, name, in_specs, out_specs, out_shape, grid=(), scratch_shapes=(), n_prefetch=0, aliases=None,
            params=None):
    def run(*args):
        after = _Chain.last
        n_lead = len(args)
        specs, operands, fn = list(in_specs), list(args), body
        if after is not None:
            def fn(*refs):
                return body(*refs[:n_lead], *refs[n_lead + 1:])
            specs.append(ANY_SPEC)
            operands.append(after)
        kw = dict(name=name, out_shape=out_shape, input_output_aliases=aliases or {})
        if params is not None:
            kw["compiler_params"] = params
        if n_prefetch:
            kw["grid_spec"] = pltpu.PrefetchScalarGridSpec(
                num_scalar_prefetch=n_prefetch, grid=grid, in_specs=specs, out_specs=out_specs,
                scratch_shapes=scratch_shapes)
        else:
            kw.update(grid=grid, in_specs=specs, out_specs=out_specs, scratch_shapes=scratch_shapes)
        outs = pl.pallas_call(fn, **kw)(*operands)
        _Chain.last = outs[0] if isinstance(outs, (list, tuple)) else outs
        return outs
    return run


def _mm(name, a, b, *, grid, a_spec, b_spec, o_spec, o_shape, o_dtype, dims, k_axis=None, acc_shape=None,
        add=None, add_spec=None):
    nk = grid[k_axis] if k_axis is not None else 1
    has_add = add is not None

    def body(*refs):
        a_ref, b_ref = refs[0], refs[1]
        p = 2
        add_ref = None
        if has_add:
            add_ref = refs[p]
            p += 1
        o_ref = refs[p]
        p += 1
        r = lax.dot_general(a_ref[...].astype(BF16), b_ref[...].astype(BF16), dims, preferred_element_type=F32)
        if k_axis is None:
            if has_add:
                r = r + add_ref[...].astype(F32)
            o_ref[...] = r.astype(o_dtype)
        else:
            acc = refs[p]
            k = pl.program_id(k_axis)

            @pl.when(k == 0)
            def _():
                acc[...] = r

            @pl.when(k > 0)
            def _():
                acc[...] += r

            @pl.when(k == nk - 1)
            def _():
                t = acc[...]
                if has_add:
                    t = t + add_ref[...].astype(F32)
                o_ref[...] = t.astype(o_dtype)

    in_specs = [a_spec, b_spec]
    args = [a, b]
    if has_add:
        in_specs.append(add_spec if add_spec is not None else o_spec)
        args.append(add)
    sem = tuple("arbitrary" if ax == k_axis else "parallel" for ax in range(len(grid)))
    scratch = [pltpu.VMEM(acc_shape, F32)] if k_axis is not None else []
    return _pallas(body, name=name, grid=grid, in_specs=in_specs, out_specs=o_spec,
                   out_shape=jax.ShapeDtypeStruct(o_shape, o_dtype), scratch_shapes=scratch, params=_params(sem))(*args)


def _mm_sum(name, parts, *, grid, o_spec, o_shape, o_dtype, add=None, norm_bwd=None, post=None):
    has_add = add is not None
    np_ = len(parts)
    nn = 1 if norm_bwd is None else len(norm_bwd[1])
    has_res = norm_bwd is not None and norm_bwd[2] is not None
    has_post = post is not None

    def body(*refs):
        accs = [None] * nn
        for p, (_, _, _, _, dims, n) in enumerate(parts):
            a_ref, b_ref = refs[2 * p], refs[2 * p + 1]
            for k in range(a_ref.shape[0]):
                r = lax.dot_general(a_ref[k], b_ref[k], dims, preferred_element_type=F32)
                accs[n] = r if accs[n] is None else accs[n] + r
        if norm_bwd is None:
            acc = accs[0]
            if has_add:
                acc = acc + refs[2 * np_][...]
            refs[-1][...] = acc.astype(o_dtype)
            return
        x_ref, g_refs = refs[2 * np_], refs[2 * np_ + 1:2 * np_ + 1 + nn]
        n_in = 2 * np_ + 1 + nn + has_res + has_post
        dx_ref, dxb_ref, dg_refs = refs[n_in], refs[n_in + 1], refs[n_in + 2:n_in + 2 + nn]
        xv = x_ref[...]
        r = lax.rsqrt(jnp.mean(xv * xv, axis=-1, keepdims=True) + EPS)
        xn = xv * r
        dx = refs[2 * np_ + 1 + nn][...] if has_res else None
        sums = []
        for acc, g_ref in zip(accs, g_refs):
            gdy = acc * g_ref[...]
            t = r * (gdy - xn * jnp.mean(gdy * xn, axis=-1, keepdims=True))
            dx = t if dx is None else dx + t
            sums.append(jnp.sum(acc * xn, axis=0, keepdims=True))
        dx_ref[...] = dx
        dxb = dx.astype(BF16)
        dxb_ref[...] = dxb
        if has_post:
            refs[n_in + 2 + nn][...] = lax.dot_general(dxb, refs[n_in - 1][...], post[1],
                                                       preferred_element_type=F32).astype(BF16)

        @pl.when(pl.program_id(0) == 0)
        def _():
            for dg_ref, part in zip(dg_refs, sums):
                dg_ref[...] = part

        @pl.when(pl.program_id(0) > 0)
        def _():
            for dg_ref, part in zip(dg_refs, sums):
                dg_ref[...] += part

    in_specs, args = [], []
    for a, a_spec, b, b_spec, _, _ in parts:
        in_specs += [a_spec, b_spec]
        args += [a, b]
    if norm_bwd is None:
        if has_add:
            in_specs.append(o_spec)
            args.append(add)
        return _pallas(body, name=name, grid=grid, in_specs=in_specs, out_specs=o_spec,
                       out_shape=jax.ShapeDtypeStruct(o_shape, o_dtype),
                       params=_params(("parallel",) * len(grid)))(*args)
    x, gains, dres = norm_bwd
    vec = pl.BlockSpec((1, o_shape[1]), lambda i: (0, 0))
    in_specs += [o_spec] + [vec] * nn + ([o_spec] if has_res else [])
    args += [x] + list(gains) + ([dres] if has_res else [])
    out_specs = [o_spec, o_spec] + [vec] * nn
    out_shape = ([jax.ShapeDtypeStruct(o_shape, F32), jax.ShapeDtypeStruct(o_shape, BF16)]
                 + [jax.ShapeDtypeStruct((1, o_shape[1]), F32)] * nn)
    if has_post:
        in_specs.append(pl.BlockSpec(post[0].shape, lambda i: (0, 0)))
        args.append(post[0])
        out_specs.append(pl.BlockSpec((o_spec.block_shape[0], post[2]), lambda i: (i, 0)))
        out_shape.append(jax.ShapeDtypeStruct((o_shape[0], post[2]), BF16))
    outs = _pallas(body, name=name, grid=grid, in_specs=in_specs, out_specs=out_specs, out_shape=out_shape,
                   params=_params(("arbitrary",)))(*args)
    if has_post:
        return outs[0], outs[1], list(outs[2:2 + nn]), outs[2 + nn]
    return outs[0], outs[1], list(outs[2:])


def _mm_rows(name, a, b, dims, o_dtype, n_out, *, tn=None, add=None):
    k = a.shape[1]
    tn = n_out if tn is None else tn
    if dims == NN:
        b_spec = pl.BlockSpec((k, tn), lambda n, i: (0, n))
    else:
        b_spec = pl.BlockSpec((tn, k), lambda n, i: (n, 0))
    return _mm(name, a, b, grid=(n_out // tn, T // TM),
               a_spec=pl.BlockSpec((TM, k), lambda n, i: (i, 0)), b_spec=b_spec,
               o_spec=pl.BlockSpec((TM, tn), lambda n, i: (i, n)), o_shape=(T, n_out), o_dtype=o_dtype,
               dims=dims, add=add)


def _wgrads(name, jobs):
    jobs = [job if len(job) == 3 else (*job, job[0].shape[-1]) for job in jobs]
    arrays, index = [], {}
    for a, b, _ in jobs:
        for arr in (a, b):
            if id(arr) not in index:
                index[id(arr)] = len(arrays)
                arrays.append(arr)
    n_in = len(arrays)

    def body(*refs):
        for t, (a, b, rows) in enumerate(jobs):
            a_ref, b_ref, o_ref = refs[index[id(a)]], refs[index[id(b)]], refs[n_in + t]
            if a.ndim == 3:
                for h in range(a.shape[0]):
                    o_ref[h] = lax.dot_general(a_ref[h], b_ref[...], TN, preferred_element_type=F32)[:rows].astype(BF16)
            else:
                o_ref[...] = lax.dot_general(a_ref[...], b_ref[...], TN, preferred_element_type=F32)[:rows].astype(BF16)

    out_shape = [jax.ShapeDtypeStruct(a.shape[:-2] + (rows, b.shape[-1]), BF16) for a, b, rows in jobs]
    return _pallas(body, name=name, in_specs=[VMEM_SPEC] * n_in, out_specs=[VMEM_SPEC] * len(jobs), out_shape=out_shape,
                   params=pltpu.CompilerParams(vmem_limit_bytes=VMEM_LIMIT))(*arrays)


def _mm_wgrad(name, a, b, *, tn=512):
    k, n = a.shape[1], b.shape[1]
    tn = min(tn, n)
    return _mm(name, a, b, grid=(n // tn,),
               a_spec=pl.BlockSpec((T, k), lambda j: (0, 0)), b_spec=pl.BlockSpec((T, tn), lambda j: (0, j)),
               o_spec=pl.BlockSpec((k, tn), lambda j: (0, j)), o_shape=(k, n), o_dtype=BF16, dims=TN)


def _rms_fwd(name, x, g):
    d = x.shape[1]

    def body(x_ref, g_ref, o_ref):
        xv = x_ref[...]
        r = lax.rsqrt(jnp.mean(xv * xv, axis=-1, keepdims=True) + EPS)
        o_ref[...] = ((xv * r) * g_ref[...]).astype(BF16)

    return _pallas(
        body, name=name, grid=(T // TM,),
        in_specs=[pl.BlockSpec((TM, d), lambda i: (i, 0)), pl.BlockSpec((1, d), lambda i: (0, 0))],
        out_specs=pl.BlockSpec((TM, d), lambda i: (i, 0)),
        out_shape=jax.ShapeDtypeStruct((T, d), BF16), params=_params(("parallel",)))(x, g)


def _rms(xv, g):
    return (xv * lax.rsqrt(jnp.mean(xv * xv, axis=-1, keepdims=True) + EPS)) * g


def _out_norm(name, a, w, add, g):
    def body(a_ref, w_ref, add_ref, g_ref, h_ref, hn_ref):
        hv = lax.dot_general(a_ref[...], w_ref[...], NN, preferred_element_type=F32) + add_ref[...]
        h_ref[...] = hv
        hn_ref[...] = _rms(hv, g_ref[...]).astype(BF16)

    rows = pl.BlockSpec((TS, D), lambda i: (i, 0))
    return _pallas(
        body, name=name, grid=(T // TS,),
        in_specs=[pl.BlockSpec((TS, a.shape[1]), lambda i: (i, 0)), pl.BlockSpec(w.shape, lambda i: (0, 0)), rows,
                  pl.BlockSpec((1, D), lambda i: (0, 0))],
        out_specs=[rows, rows], out_shape=[jax.ShapeDtypeStruct((T, D), F32), jax.ShapeDtypeStruct((T, D), BF16)],
        params=_params(("parallel",)))(a, w, add, g)


def _down_final(act, w_down4, h_in, g, tgt):
    def body(a_ref, w_ref, hin_ref, g_ref, t_ref, loss_ref, dh_ref, dhb_ref, dg_ref):
        hv = lax.dot_general(a_ref[0], w_ref[0], NN, preferred_element_type=F32)
        for j in range(1, N_FF_BLK):
            hv = hv + lax.dot_general(a_ref[j], w_ref[j], NN, preferred_element_type=F32)
        hv = hv + hin_ref[...]
        r = lax.rsqrt(jnp.mean(hv * hv, axis=-1, keepdims=True) + EPS)
        xn = hv * r
        gv = g_ref[...]
        err = xn * gv - t_ref[...]
        part_loss = 0.5 * jnp.sum(jnp.mean(err * err, axis=-1, keepdims=True), axis=0, keepdims=True)
        dy = err * (1.0 / D)
        gdy = dy * gv
        dh = r * (gdy - xn * jnp.mean(gdy * xn, axis=-1, keepdims=True))
        dh_ref[...] = dh
        dhb_ref[...] = dh.astype(BF16)
        part = jnp.sum(dy * xn, axis=0, keepdims=True)
        first = pl.program_id(0) == 0

        @pl.when(first)
        def _():
            dg_ref[...] = part
            loss_ref[...] = jnp.broadcast_to(part_loss, (1, 128))

        @pl.when(jnp.logical_not(first))
        def _():
            dg_ref[...] += part
            loss_ref[...] += jnp.broadcast_to(part_loss, (1, 128))

    row = pl.BlockSpec((TS, D), lambda i: (i, 0))
    vec = pl.BlockSpec((1, D), lambda i: (0, 0))
    return _pallas(
        body, name="f1_down_loss", grid=(T // TS,),
        in_specs=[pl.BlockSpec((N_FF_BLK, TS, FF_BLK), lambda i: (0, i, 0)),
                  pl.BlockSpec((None, N_FF_BLK, FF_BLK, D), lambda i: (0, 0, 0, 0)), row, vec, row],
        out_specs=[pl.BlockSpec((1, 128), lambda i: (0, 0)), row, row, vec],
        out_shape=[jax.ShapeDtypeStruct((1, 128), F32), jax.ShapeDtypeStruct((T, D), F32),
                   jax.ShapeDtypeStruct((T, D), BF16), jax.ShapeDtypeStruct((1, D), F32)],
        params=_params(("arbitrary",)))(act, w_down4, h_in, g, tgt)


def _prev_idx(i, rows=TR):
    return jnp.maximum(i * (rows // HALO) - 1, 0)


def _next_idx(i, rows=TR):
    return jnp.minimum((i + 1) * (rows // HALO), T // HALO - 1)


def _causal_taps(ext):
    return pltpu.roll(ext, 2, 0)[HALO:], pltpu.roll(ext, 1, 0)[HALO:], ext[HALO:]


def _anticausal_taps(ext, n):
    rows = ext.shape[0]
    return pltpu.roll(ext, rows - 1, 0)[:n], pltpu.roll(ext, rows - 2, 0)[:n]


MIX_COLS = 512


def _mixer_in(hn, w_in, w):
    nc = D // MIX_COLS

    def body(h_ref, hh_ref, wb_ref, wc_ref, wu_ref, w_ref, b_ref, c_ref, u_ref, y_ref):
        i = pl.program_id(1)
        hv = h_ref[...]
        he = jnp.concatenate([hh_ref[...], hv], axis=0)
        ce = lax.dot_general(he, wc_ref[...], NN, preferred_element_type=F32).astype(BF16)
        ue = lax.dot_general(he, wu_ref[...], NN, preferred_element_type=F32).astype(BF16)
        bv = lax.dot_general(hv, wb_ref[...], NN, preferred_element_type=F32).astype(BF16)
        b_ref[...] = bv
        c_ref[...] = ce[HALO:]
        u_ref[...] = ue[HALO:]
        row = lax.broadcasted_iota(jnp.int32, (HALO + TS, 1), 0)
        cu = jnp.where(jnp.logical_or(i > 0, row >= HALO), ce.astype(F32) * ue.astype(F32), 0.0)
        x2, x1, x0 = _causal_taps(cu)
        wv = w_ref[...]
        cv = (x2 * wv[0:1] + x1 * wv[1:2]) + x0 * wv[2:3]
        y_ref[...] = (bv.astype(F32) * cv).astype(BF16)

    def cols(part):
        return pl.BlockSpec((D, MIX_COLS), lambda j, i: (0, part * nc + j))

    blk = pl.BlockSpec((TS, MIX_COLS), lambda j, i: (i, j))
    out = jax.ShapeDtypeStruct((T, D), BF16)
    return _pallas(
        body, name="l0_in", grid=(nc, T // TS),
        in_specs=[pl.BlockSpec((TS, D), lambda j, i: (i, 0)), pl.BlockSpec((HALO, D), lambda j, i: (_prev_idx(i, TS), 0)),
                  cols(0), cols(1), cols(2), pl.BlockSpec((3, MIX_COLS), lambda j, i: (0, j))],
        out_specs=[blk] * 4, out_shape=[out] * 4,
        params=_params(("parallel", "parallel")))(hn, hn, w_in, w_in, w_in, w)


def _mixer_out_bwd(dh, w_out, zb, zc, zu, w):
    last = T // TR - 1

    def body(dh_ref, dhn_ref, wo_ref, b_ref, bn_ref, c_ref, ch_ref, u_ref, uh_ref, w_ref, dz_ref, dw_ref):
        i = pl.program_id(0)
        dye = lax.dot_general(jnp.concatenate([dh_ref[...], dhn_ref[...]], axis=0), wo_ref[...], NT,
                              preferred_element_type=F32)
        cv_ = c_ref[...].astype(F32)
        uv = u_ref[...].astype(F32)
        cu = cv_ * uv
        cuh = jnp.where(i > 0, ch_ref[...].astype(F32) * uh_ref[...].astype(F32), 0.0)
        x2, x1, x0 = _causal_taps(jnp.concatenate([cuh, cu], axis=0))
        wv = w_ref[...]
        conv = (x2 * wv[0:1] + x1 * wv[1:2]) + x0 * wv[2:3]
        dyv = dye[:TR]
        dz_ref[:, 0:D] = (dyv * conv).astype(BF16)
        dconv = dyv * b_ref[...].astype(F32)
        dconv_n = jnp.where(i < last, dye[TR:] * bn_ref[...].astype(F32), 0.0)
        n1, n2 = _anticausal_taps(jnp.concatenate([dconv, dconv_n], axis=0), TR)
        dcu = (dconv * wv[2:3] + n1 * wv[1:2]) + n2 * wv[0:1]
        dz_ref[:, D:2 * D] = (dcu * uv).astype(BF16)
        dz_ref[:, 2 * D:3 * D] = (dcu * cv_).astype(BF16)
        part = jnp.concatenate([jnp.sum(dconv * x2, axis=0, keepdims=True),
                                jnp.sum(dconv * x1, axis=0, keepdims=True),
                                jnp.sum(dconv * x0, axis=0, keepdims=True)], axis=0)

        @pl.when(i == 0)
        def _():
            dw_ref[...] = part

        @pl.when(i > 0)
        def _():
            dw_ref[...] += part

    main = pl.BlockSpec((TR, D), lambda i: (i, 0))
    prev = pl.BlockSpec((HALO, D), lambda i: (_prev_idx(i), 0))
    nxt = pl.BlockSpec((HALO, D), lambda i: (_next_idx(i), 0))
    wspec = pl.BlockSpec((3, D), lambda i: (0, 0))
    return _pallas(
        body, name="d_l0_out", grid=(T // TR,),
        in_specs=[main, nxt, pl.BlockSpec((D, D), lambda i: (0, 0)), main, nxt, main, prev, main, prev, wspec],
        out_specs=[pl.BlockSpec((TR, 3 * D), lambda i: (i, 0)), wspec],
        out_shape=[jax.ShapeDtypeStruct((T, 3 * D), BF16), jax.ShapeDtypeStruct((3, D), F32)],
        params=_params(("arbitrary",)))(dh, dh, w_out, zb, zb, zc, zc, zu, zu, w)


def _sigmoid(x):
    return 0.5 * jnp.tanh(0.5 * x) + 0.5


def _ffn_up_act(name, hf, w_up, w, b):
    def body(h_ref, hh_ref, wg_ref, wv_ref, w_ref, b_ref, g_ref, v_ref, a_ref):
        i = pl.program_id(1)
        hv = h_ref[...]
        ge = lax.dot_general(jnp.concatenate([hh_ref[...], hv], axis=0), wg_ref[...], NT,
                             preferred_element_type=F32).astype(BF16)
        v = lax.dot_general(hv, wv_ref[...], NT, preferred_element_type=F32).astype(BF16)
        g_ref[...] = ge[HALO:]
        v_ref[...] = v
        ext = ge.astype(F32)
        row = lax.broadcasted_iota(jnp.int32, (HALO + TM, 1), 0)
        ext = jnp.where(jnp.logical_or(i > 0, row >= HALO), ext, 0.0)
        x2, x1, x0 = _causal_taps(ext)
        wv = w_ref[...]
        gc = ((x2 * wv[0:1] + x1 * wv[1:2]) + x0 * wv[2:3]) + b_ref[...]
        a_ref[...] = ((gc * _sigmoid(gc)) * v.astype(F32)).astype(BF16)

    blk = pl.BlockSpec((None, TM, FF_BLK), lambda j, i: (j, i, 0))
    out = jax.ShapeDtypeStruct((N_FF_BLK, T, FF_BLK), BF16)
    return _pallas(
        body, name=name, grid=(N_FF_BLK, T // TM),
        in_specs=[pl.BlockSpec((TM, D), lambda j, i: (i, 0)),
                  pl.BlockSpec((HALO, D), lambda j, i: (_prev_idx(i, TM), 0)),
                  pl.BlockSpec((None, None, FF_BLK, D), lambda j, i: (0, j, 0, 0)),
                  pl.BlockSpec((None, None, FF_BLK, D), lambda j, i: (0, j + N_FF_BLK, 0, 0)),
                  pl.BlockSpec((None, 3, FF_BLK), lambda j, i: (j, 0, 0)),
                  pl.BlockSpec((None, 1, FF_BLK), lambda j, i: (j, 0, 0))],
        out_specs=[blk, blk, blk], out_shape=[out, out, out],
        params=_params(("parallel", "parallel")))(hf, hf, w_up, w_up, w, b)


def _ffn_dact(name, dh, w_down4, g, v, w, b):
    last = T // TS - 1

    def body(dh_ref, dhn_ref, wd_ref, g_ref, gp_ref, gn_ref, v_ref, vn_ref, w_ref, b_ref, dg_ref, dv_ref, dw_ref, db_ref):
        i = pl.program_id(1)
        da = lax.dot_general(jnp.concatenate([dh_ref[...], dhn_ref[...]], axis=0), wd_ref[...], NT,
                             preferred_element_type=F32)
        row = lax.broadcasted_iota(jnp.int32, (TS + HALO, 1), 0)
        da = jnp.where(jnp.logical_or(i < last, row < TS), da, 0.0)
        gp = jnp.where(i > 0, gp_ref[...].astype(F32), 0.0)
        ext = jnp.concatenate([gp, g_ref[...].astype(F32), gn_ref[...].astype(F32)], axis=0)
        x2, x1, x0 = _causal_taps(ext)
        wv = w_ref[...]
        gc = ((x2 * wv[0:1] + x1 * wv[1:2]) + x0 * wv[2:3]) + b_ref[...]
        sg = _sigmoid(gc)
        vv = jnp.concatenate([v_ref[...].astype(F32), vn_ref[...].astype(F32)], axis=0)
        silu = gc * sg
        dv_ref[...] = (da[:TS] * silu[:TS]).astype(BF16)
        dgc = (da * vv) * (sg + silu * (1.0 - sg))
        n1, n2 = _anticausal_taps(dgc, TS)
        d0 = dgc[:TS]
        dg_ref[...] = ((d0 * wv[2:3] + n1 * wv[1:2]) + n2 * wv[0:1]).astype(BF16)
        part_w = jnp.concatenate([jnp.sum(d0 * x2[:TS], axis=0, keepdims=True),
                                  jnp.sum(d0 * x1[:TS], axis=0, keepdims=True),
                                  jnp.sum(d0 * x0[:TS], axis=0, keepdims=True)], axis=0)
        part_b = jnp.sum(d0, axis=0, keepdims=True)

        @pl.when(i == 0)
        def _():
            dw_ref[...] = part_w
            db_ref[...] = part_b

        @pl.when(i > 0)
        def _():
            dw_ref[...] += part_w
            db_ref[...] += part_b

    blk = pl.BlockSpec((None, TS, FF_BLK), lambda j, i: (j, i, 0))
    prev = pl.BlockSpec((None, HALO, FF_BLK), lambda j, i: (j, _prev_idx(i, TS), 0))
    nxt = pl.BlockSpec((None, HALO, FF_BLK), lambda j, i: (j, _next_idx(i, TS), 0))
    wspec = pl.BlockSpec((None, 3, FF_BLK), lambda j, i: (j, 0, 0))
    bspec = pl.BlockSpec((None, 1, FF_BLK), lambda j, i: (j, 0, 0))
    return _pallas(
        body, name=name, grid=(N_FF_BLK, T // TS),
        in_specs=[pl.BlockSpec((TS, D), lambda j, i: (i, 0)),
                  pl.BlockSpec((HALO, D), lambda j, i: (_next_idx(i, TS), 0)),
                  pl.BlockSpec((None, None, FF_BLK, D), lambda j, i: (0, j, 0, 0)),
                  blk, prev, nxt, blk, nxt, wspec, bspec],
        out_specs=[blk, blk, wspec, bspec],
        out_shape=[jax.ShapeDtypeStruct((N_FF_BLK, T, FF_BLK), BF16), jax.ShapeDtypeStruct((N_FF_BLK, T, FF_BLK), BF16),
                   jax.ShapeDtypeStruct((N_FF_BLK, 3, FF_BLK), F32), jax.ShapeDtypeStruct((N_FF_BLK, 1, FF_BLK), F32)],
        params=_params(("parallel", "arbitrary")))(dh, dh, w_down4, g, g, g, v, v, w, b)


def _rope_tables(pos, inv_freq):
    half = QK_ROPE // 2

    def body(p_ref, f_ref, c_ref, sa_ref, sb_ref):
        ang = p_ref[...].astype(F32) * f_ref[...]
        lane = lax.broadcasted_iota(jnp.int32, (T, 128), 1)
        c = jnp.cos(ang)
        s = jnp.sin(ang)
        c_ref[...] = jnp.where(lane < 2 * half, c, 0.0)
        sa_ref[...] = jnp.where(lane < half, -s, 0.0)
        sb_ref[...] = jnp.where(jnp.logical_and(lane >= half, lane < 2 * half), s, 0.0)

    return _pallas(
        body, name="rope_tables", in_specs=[VMEM_SPEC] * 2, out_specs=[VMEM_SPEC] * 3,
        out_shape=[jax.ShapeDtypeStruct((T, 128), F32)] * 3,
        params=pltpu.CompilerParams(vmem_limit_bytes=VMEM_LIMIT))(pos, inv_freq)


def _rotate(r, c, sa, sb, sign):
    return r * c + sign * (pltpu.roll(r, 96, 1) * sa + pltpu.roll(r, 32, 1) * sb)


def _attn_pre(h2, g_kv, g_l1, g_kvl, g_ql, w_dkv, w_kr, w_uk, w_uv, w_dq, w_uq, tables):
    def body(h_ref, c_ref, sa_ref, sb_ref, gkv_ref, gl1_ref, gkvl_ref, gql_ref, wdkv_ref, wkr_ref, wuk_ref, wuv_ref,
             wdq_ref, wuq_ref, hk_ref, hn_ref, ckvr_ref, ckv_ref, kr_ref, kn_ref, v_ref, cqr_ref, cq_ref, q_ref):
        xv = h_ref[...]
        xn = xv * lax.rsqrt(jnp.mean(xv * xv, axis=-1, keepdims=True) + EPS)
        hk = (xn * gkv_ref[...]).astype(BF16)
        hn = (xn * gl1_ref[...]).astype(BF16)
        hk_ref[...] = hk
        hn_ref[...] = hn
        cv, sav, sbv = c_ref[...], sa_ref[...], sb_ref[...]
        raw = lax.dot_general(hk, wdkv_ref[...], NN, preferred_element_type=F32)
        ckvr_ref[...] = raw
        ckv = _rms(raw, gkvl_ref[...]).astype(BF16)
        ckv_ref[...] = ckv
        kr = lax.dot_general(hk, wkr_ref[...], NT, preferred_element_type=F32)
        kr_ref[...] = _rotate(kr, cv, sav, sbv, 1.0).astype(BF16)
        kn_ref[...] = lax.dot_general(ckv, wuk_ref[...], NN, preferred_element_type=F32).astype(BF16)
        v_ref[...] = lax.dot_general(ckv, wuv_ref[...], NN, preferred_element_type=F32).astype(BF16)
        cqr = lax.dot_general(hn, wdq_ref[...], NN, preferred_element_type=F32)
        cqr_ref[...] = cqr
        cq = _rms(cqr, gql_ref[...]).astype(BF16)
        cq_ref[...] = cq
        for h in range(N_HEADS):
            r = lax.dot_general(cq, wuq_ref[h], NT, preferred_element_type=F32)
            q_ref[h, :, :QK_NOPE] = (r[:, :QK_NOPE] * SCALE2).astype(BF16)
            q_ref[h, :, QK_NOPE:] = (_rotate(r[:, QK_NOPE:], cv, sav, sbv, 1.0) * SCALE2).astype(BF16)

    def rows(d):
        return pl.BlockSpec((TS, d), lambda i: (i, 0))

    def whole(a):
        return pl.BlockSpec(a.shape, lambda i: (0,) * a.ndim)

    wholes = [g_kv, g_l1, g_kvl, g_ql, w_dkv, w_kr, w_uk, w_uv, w_dq, w_uq]
    outs = [(D, BF16), (D, BF16), (KV_LORA, F32), (KV_LORA, BF16), (128, BF16), (N_HEADS * QK_NOPE, BF16),
            (N_HEADS * V_HEAD, BF16), (Q_LORA, F32), (Q_LORA, BF16)]
    return _pallas(
        body, name="attn_pre", grid=(T // TS,),
        in_specs=[rows(D), rows(128), rows(128), rows(128)] + [whole(a) for a in wholes],
        out_specs=[rows(d) for d, _ in outs] + [pl.BlockSpec((N_HEADS, TS, QK_PAD), lambda i: (0, i, 0))],
        out_shape=[jax.ShapeDtypeStruct((T, d), dt) for d, dt in outs]
        + [jax.ShapeDtypeStruct((N_HEADS, T, QK_PAD), BF16)],
        params=_params(("parallel",)))(h2, *tables, *wholes)


SCALE = (QK_NOPE + QK_ROPE) ** -0.5
LOG2E = 1.4426950408889634
SCALE2 = SCALE * LOG2E


def _diag_mask(transposed):
    shift = CHUNK.bit_length() - 1
    a = lax.broadcasted_iota(jnp.int32, (TQ, TQ), 0) >> shift
    b = lax.broadcasted_iota(jnp.int32, (TQ, TQ), 1) >> shift
    return (a <= b) if transposed else (b <= a)


def _as_row(col):
    return jnp.transpose(jnp.broadcast_to(col, (col.shape[0], 128)), (1, 0))[0:1]


def _attn_fwd(q, kn, kr, v):
    hp = 2

    def body(q_ref, kn_ref, kr_ref, v_ref, o_ref, lse_ref):
        i = pl.program_id(1)
        qs = [q_ref[a] for a in range(hp)]

        def step(j, carry, masked):
            off = pl.multiple_of(j * TQ, TQ)
            krv = kr_ref[pl.ds(off, TQ), :]
            ss = []
            for a in range(hp):
                kk = jnp.concatenate([kn_ref[pl.ds(off, TQ), a * QK_NOPE:(a + 1) * QK_NOPE], krv], axis=1)
                ss.append(lax.dot_general(qs[a], kk, NT, preferred_element_type=F32))
            out = []
            for a in range(hp):
                m, l, acc = carry[a]
                s = ss[a]
                if masked:
                    s = jnp.where(_diag_mask(False), s, NEG_INF)
                m_new = jnp.maximum(m, jnp.max(s, axis=-1, keepdims=True))
                p = jnp.exp2(s - m_new)
                alpha = jnp.exp2(m - m_new)
                l = alpha * l + jnp.sum(p, axis=-1, keepdims=True)
                pv = lax.dot_general(p.astype(BF16), v_ref[pl.ds(off, TQ), a * V_HEAD:(a + 1) * V_HEAD], NN,
                                     preferred_element_type=F32)
                out.append((m_new, l, alpha * acc + pv))
            return tuple(out)

        one = (jnp.full((TQ, 1), NEG_INF, F32), jnp.zeros((TQ, 1), F32), jnp.zeros((TQ, V_HEAD), F32))
        carry = lax.fori_loop(0, i, lambda j, cr: step(j, cr, False), (one,) * hp)
        carry = step(i, carry, True)
        for a, (m, l, acc) in enumerate(carry):
            o_ref[:, a * V_HEAD:(a + 1) * V_HEAD] = (acc / l).astype(BF16)
            lse_ref[a] = _as_row(m + jnp.log(l) * LOG2E)

    return _pallas(
        body, name="attn_fwd", grid=(N_HEADS // hp, T // TQ),
        in_specs=[pl.BlockSpec((hp, TQ, QK_PAD), lambda h, i: (h, i, 0)),
                  pl.BlockSpec((T, hp * QK_NOPE), lambda h, i: (0, h)),
                  pl.BlockSpec((T, 128), lambda h, i: (0, 0)),
                  pl.BlockSpec((T, hp * V_HEAD), lambda h, i: (0, h))],
        out_specs=[pl.BlockSpec((TQ, hp * V_HEAD), lambda h, i: (i, h)), pl.BlockSpec((hp, 1, TQ), lambda h, i: (h, 0, i))],
        out_shape=[jax.ShapeDtypeStruct((T, N_HEADS * V_HEAD), BF16), jax.ShapeDtypeStruct((N_HEADS, 1, T), F32)],
        params=_params(("parallel", "parallel")))(q, kn, kr, v)


def _attn_bwd(q, kn, kr, v, o, do, lse_row, tables):
    nq = T // TQ
    hp = 2
    cos, sa, sb = tables

    def body(q_ref, kn_ref, kr_ref, v_ref, o_ref, do_ref, lse_ref, c_ref, sa_ref, sb_ref,
             dq_ref, dkn_ref, dkr_ref, dv_ref, dq_acc, dl_ref):
        j = pl.program_id(1)

        def cols(a):
            return slice(a * 128, (a + 1) * 128)

        @pl.when(j == 0)
        def _():
            dq_acc[...] = jnp.zeros_like(dq_acc)
            for a in range(hp):
                for i in range(nq):
                    rows = pl.ds(i * TQ, TQ)
                    prod = do_ref[rows, cols(a)].astype(F32) * o_ref[rows, cols(a)].astype(F32)
                    dl_ref[a, :, rows] = _as_row(jnp.sum(prod, axis=-1, keepdims=True))

        krv = kr_ref[...]
        kks = [jnp.concatenate([kn_ref[:, cols(a)], krv], axis=1) for a in range(hp)]
        vvs = [v_ref[:, cols(a)] for a in range(hp)]

        def step(i, carry, masked):
            off = pl.multiple_of(i * TQ, TQ)
            rows = pl.ds(off, TQ)
            qis = [q_ref[a, rows, :] for a in range(hp)]
            dois = [do_ref[rows, cols(a)] for a in range(hp)]
            sts = [lax.dot_general(kks[a], qis[a], NT, preferred_element_type=F32) for a in range(hp)]
            dpts = [lax.dot_general(vvs[a], dois[a], NT, preferred_element_type=F32) for a in range(hp)]
            out = []
            for a in range(hp):
                dk, dv = carry[a]
                st = sts[a]
                if masked:
                    st = jnp.where(_diag_mask(True), st, NEG_INF)
                pt = jnp.exp2(st - lse_ref[a, :, rows])
                dv = dv + lax.dot_general(pt.astype(BF16), dois[a], NN, preferred_element_type=F32)
                dst = (pt * (dpts[a] - dl_ref[a, :, rows])).astype(BF16)
                dk = dk + lax.dot_general(dst, qis[a], NN, preferred_element_type=F32)
                dq_acc[a, rows, :] += lax.dot_general(dst, kks[a], TN, preferred_element_type=F32)
                out.append((dk, dv))
            return tuple(out)

        zero = (jnp.zeros((TQ, QK_PAD), F32), jnp.zeros((TQ, V_HEAD), F32))
        carry = step(j, (zero,) * hp, True)
        carry = lax.fori_loop(j + 1, nq, lambda i, cr: step(i, cr, False), carry)
        for a, (dk, dv) in enumerate(carry):
            dk = dk * (SCALE / SCALE2)
            dkn_ref[:, cols(a)] = dk[:, :QK_NOPE].astype(BF16)
            dkr_ref[a] = dk[:, QK_NOPE:]
            dv_ref[:, cols(a)] = dv.astype(BF16)

        @pl.when(j == nq - 1)
        def _():
            for a in range(hp):
                dq = dq_acc[a] * SCALE
                dq_ref[a, :, :QK_NOPE] = dq[:, :QK_NOPE].astype(BF16)
                dq_ref[a, :, QK_NOPE:] = _rotate(dq[:, QK_NOPE:], c_ref[...], sa_ref[...], sb_ref[...], -1.0).astype(BF16)

    row = pl.BlockSpec((hp, 1, T), lambda h, j: (h, 0, 0))
    head = pl.BlockSpec((TQ, hp * 128), lambda h, j: (j, h))
    whole = pl.BlockSpec((hp, T, QK_PAD), lambda h, j: (h, 0, 0))
    tab = pl.BlockSpec((T, 128), lambda h, j: (0, 0))
    heads = pl.BlockSpec((T, hp * V_HEAD), lambda h, j: (0, h))
    return _pallas(
        body, name="attn_bwd", grid=(N_HEADS // hp, nq),
        in_specs=[whole, head, pl.BlockSpec((TQ, 128), lambda h, j: (j, 0)), head, heads, heads, row, tab, tab, tab],
        out_specs=[whole, head, pl.BlockSpec((hp, TQ, 128), lambda h, j: (h, j, 0)), head],
        out_shape=[jax.ShapeDtypeStruct((N_HEADS, T, QK_PAD), BF16), jax.ShapeDtypeStruct((T, N_HEADS * QK_NOPE), BF16),
                   jax.ShapeDtypeStruct((N_HEADS, T, 128), F32), jax.ShapeDtypeStruct((T, N_HEADS * V_HEAD), BF16)],
        scratch_shapes=[pltpu.VMEM((hp, T, QK_PAD), F32), pltpu.VMEM((hp, 1, T), F32)],
        params=_params(("parallel", "arbitrary")))(q, kn, kr, v, o, do, lse_row, cos, sa, sb)


def _rms_bwd_math(xv, g, dy):
    r = lax.rsqrt(jnp.mean(xv * xv, axis=-1, keepdims=True) + EPS)
    xn = xv * r
    gdy = dy * g
    return r * (gdy - xn * jnp.mean(gdy * xn, axis=-1, keepdims=True)), jnp.sum(dy * xn, axis=0, keepdims=True)


def _attn_post(dq, dkn, dv, dkr, cq_raw, ckv_raw, h2, dres, g_ql, g_kvl, g_l1, g_kv, w_uq, w_uk, w_uv, w_dq, w_dkv,
               w_kr, tables):
    def body(dq_ref, dkn_ref, dv_ref, dkr_ref, cqr_ref, ckvr_ref, h_ref, res_ref, c_ref, sa_ref, sb_ref,
             gql_ref, gkvl_ref, gl1_ref, gkv_ref, wuq_ref, wuk_ref, wuv_ref, wdq_ref, wdkv_ref, wkr_ref,
             dcq_ref, dckv_ref, dkrr_ref, dh_ref, dhb_ref, dgql_ref, dgkvl_ref, dgl1_ref, dgkv_ref):
        dcq = lax.dot_general(dq_ref[0], wuq_ref[0], NN, preferred_element_type=F32)
        for h in range(1, N_HEADS):
            dcq = dcq + lax.dot_general(dq_ref[h], wuq_ref[h], NN, preferred_element_type=F32)
        dcq_raw, s_ql = _rms_bwd_math(cqr_ref[...], gql_ref[...], dcq)
        dcq_raw = dcq_raw.astype(BF16)
        dcq_ref[...] = dcq_raw
        dckv = (lax.dot_general(dkn_ref[...], wuk_ref[...], NT, preferred_element_type=F32)
                + lax.dot_general(dv_ref[...], wuv_ref[...], NT, preferred_element_type=F32))
        dckv_raw, s_kvl = _rms_bwd_math(ckvr_ref[...], gkvl_ref[...], dckv)
        dckv_raw = dckv_raw.astype(BF16)
        dckv_ref[...] = dckv_raw
        dkr = dkr_ref[0]
        for h in range(1, N_HEADS):
            dkr = dkr + dkr_ref[h]
        dkr_raw = _rotate(dkr, c_ref[...], sa_ref[...], sb_ref[...], -1.0).astype(BF16)
        dkrr_ref[...] = dkr_raw
        d_hn = lax.dot_general(dcq_raw, wdq_ref[...], NT, preferred_element_type=F32)
        d_hk = (lax.dot_general(dckv_raw, wdkv_ref[...], NT, preferred_element_type=F32)
                + lax.dot_general(dkr_raw, wkr_ref[...], NN, preferred_element_type=F32))
        xv = h_ref[...]
        r = lax.rsqrt(jnp.mean(xv * xv, axis=-1, keepdims=True) + EPS)
        xn = xv * r
        dx = res_ref[...]
        sums = [s_ql, s_kvl]
        for dy, g_ref in ((d_hn, gl1_ref), (d_hk, gkv_ref)):
            gdy = dy * g_ref[...]
            dx = dx + r * (gdy - xn * jnp.mean(gdy * xn, axis=-1, keepdims=True))
            sums.append(jnp.sum(dy * xn, axis=0, keepdims=True))
        dh_ref[...] = dx
        dhb_ref[...] = dx.astype(BF16)
        dg_refs = (dgql_ref, dgkvl_ref, dgl1_ref, dgkv_ref)

        @pl.when(pl.program_id(0) == 0)
        def _():
            for dg_ref, part in zip(dg_refs, sums):
                dg_ref[...] = part

        @pl.when(pl.program_id(0) > 0)
        def _():
            for dg_ref, part in zip(dg_refs, sums):
                dg_ref[...] += part

    def rows(d):
        return pl.BlockSpec((TS, d), lambda i: (i, 0))

    def heads(d):
        return pl.BlockSpec((N_HEADS, TS, d), lambda i: (0, i, 0))

    def whole(a):
        return pl.BlockSpec(a.shape, lambda i: (0,) * a.ndim)

    wholes = [g_ql, g_kvl, g_l1, g_kv, w_uq, w_uk, w_uv, w_dq, w_dkv, w_kr]
    vecs = [Q_LORA, KV_LORA, D, D]
    return _pallas(
        body, name="attn_post", grid=(T // TS,),
        in_specs=[heads(QK_PAD), rows(N_HEADS * QK_NOPE), rows(N_HEADS * V_HEAD), heads(128), rows(Q_LORA),
                  rows(KV_LORA), rows(D), rows(D), rows(128), rows(128), rows(128)] + [whole(a) for a in wholes],
        out_specs=[rows(Q_LORA), rows(KV_LORA), rows(128), rows(D), rows(D)]
        + [pl.BlockSpec((1, d), lambda i: (0, 0)) for d in vecs],
        out_shape=[jax.ShapeDtypeStruct((T, Q_LORA), BF16), jax.ShapeDtypeStruct((T, KV_LORA), BF16),
                   jax.ShapeDtypeStruct((T, 128), BF16), jax.ShapeDtypeStruct((T, D), F32),
                   jax.ShapeDtypeStruct((T, D), BF16)] + [jax.ShapeDtypeStruct((1, d), F32) for d in vecs],
        params=_params(("arbitrary",)))(dq, dkn, dv, dkr, cq_raw, ckv_raw, h2, dres, *tables, *wholes)


def _ffn_gup(name, dg, dv, hf):
    def body(dg_ref, dv_ref, hf_ref, o_ref):
        j = pl.program_id(0)

        @pl.when(j < N_FF_BLK)
        def _():
            o_ref[...] = lax.dot_general(dg_ref[...], hf_ref[...], TN, preferred_element_type=F32).astype(BF16)

        @pl.when(j >= N_FF_BLK)
        def _():
            o_ref[...] = lax.dot_general(dv_ref[...], hf_ref[...], TN, preferred_element_type=F32).astype(BF16)

    return _pallas(
        body, name=name, grid=(N_DEV,),
        in_specs=[pl.BlockSpec((None, T, FF_BLK), lambda j: (jnp.minimum(j, N_FF_BLK - 1), 0, 0)),
                  pl.BlockSpec((None, T, FF_BLK), lambda j: (jnp.maximum(j - N_FF_BLK, 0), 0, 0)),
                  pl.BlockSpec((T, D), lambda j: (0, 0))],
        out_specs=pl.BlockSpec((None, FF_BLK, D), lambda j: (j, 0, 0)),
        out_shape=jax.ShapeDtypeStruct((N_DEV, FF_BLK, D), BF16), params=_params(("parallel",)))(dg, dv, hf)


def _ffn_layer_fwd(tag, h, hf, ex, final=None):
    g, v, act = _ffn_up_act(f"{tag}_up", hf, ex.need(f"ffn_w_up{tag[1]}", hf), ex.need(f"ffn_cw{tag[1]}", hf),
                            ex.need(f"ffn_cb{tag[1]}", hf))
    ex.at(f"{tag}_up", act)
    if final is not None:
        return _down_final(act, ex.need(f"ffn_w_down{tag[1]}", act), h, *final), (hf, g, v, act)
    rows = pl.BlockSpec((TS, D), lambda i: (i, 0))
    out = _mm_sum(f"{tag}_down",
                  [(act, pl.BlockSpec((N_FF_BLK, TS, FF_BLK), lambda i: (0, i, 0)), ex.need(f"ffn_w_down{tag[1]}", act),
                    pl.BlockSpec((None, N_FF_BLK, FF_BLK, D), lambda i: (0, 0, 0, 0)), NN, 0)],
                  grid=(T // TS,), o_spec=rows, o_shape=(T, D), o_dtype=F32, add=h)
    ex.at(f"{tag}_down", out)
    return out, (hf, g, v, act)


def _ffn_layer_bwd(tag, h, gain, ex, saved, dh, dh_bf, post=None):
    hf, g, v, act = saved
    layer = tag[1]
    w_up, w_down4 = ex.need(f"ffn_w_up{layer}", dh_bf), ex.need(f"ffn_w_down{layer}", dh_bf)
    dg, dv, dcw, dcb = _ffn_dact(f"{tag}_dact", dh_bf, w_down4, g, v, ex.need(f"ffn_cw{layer}", dh_bf),
                                 ex.need(f"ffn_cb{layer}", dh_bf))
    ex.at(f"{tag}_dact", dg)
    g_down = _mm(f"{tag}_gdown", act, dh_bf, grid=(N_FF_BLK,),
                 a_spec=pl.BlockSpec((None, T, FF_BLK), lambda j: (j, 0, 0)),
                 b_spec=pl.BlockSpec((T, D), lambda j: (0, 0)),
                 o_spec=pl.BlockSpec((FF_BLK, D), lambda j: (j, 0)),
                 o_shape=(D_FF, D), o_dtype=BF16, dims=TN)
    g_up = _ffn_gup(f"{tag}_gup", dg, dv, hf)
    ex.grad("ffn_w_up", int(layer), g_up.reshape(1, N_DEV, FF_BLK, D))
    ex.grad("ffn_w_down", int(layer), g_down.reshape(1, N_DEV, D_FF // N_DEV, D))
    ex.at(f"{tag}_gup", g_up)
    part = pl.BlockSpec((N_FF_BLK, TR, FF_BLK), lambda i: (0, i, 0))
    dh_in, dh_in_bf, dgain, *onward = _mm_sum(
        f"{tag}_dhf",
        [(dg, part, w_up, pl.BlockSpec((None, N_FF_BLK, FF_BLK, D), lambda i: (0, 0, 0, 0)), NN, 0),
         (dv, part, w_up, pl.BlockSpec((None, N_FF_BLK, FF_BLK, D), lambda i: (0, 1, 0, 0)), NN, 0)],
        grid=(T // TR,), o_spec=pl.BlockSpec((TR, D), lambda i: (i, 0)), o_shape=(T, D), o_dtype=F32,
        norm_bwd=(h, [gain], dh), post=post)
    ex.at(f"{tag}_dhf", dh_in)
    return (dh_in, dh_in_bf, dgain[0], dcw, dcb, *onward)


def _local_step(x, pos, tgt, rep, ex):
    attn_norm, ffn_norm, final_norm = rep["attn_norm"], rep["ffn_norm"], rep["final_norm"]
    half = QK_ROPE // 2
    inv = 1.0 / (ROPE_THETA ** (jnp.arange(half, dtype=F32) / half))
    inv_freq = jnp.concatenate([inv, inv, jnp.zeros((128 - 2 * half,), F32)]).reshape(1, 128)
    tables = _rope_tables(pos, inv_freq)

    hn0 = _rms_fwd("l0_norm", x, attn_norm[0:1])
    ex.at("l0_norm", hn0)
    w_in = ex.need("sc_w_in", hn0)
    zb, zc, zu, y = _mixer_in(hn0, w_in, ex.need("sc_conv_w", hn0))
    ex.at("l0_in", y)
    h1 = _mm_rows("l0_out", y, ex.need("sc_w_out", y), NN, F32, D, tn=512, add=x)
    ex.at("l0_out", h1)
    h2, ffn0 = _ffn_layer_fwd("f0", h1, _rms_fwd("f0_norm", h1, ffn_norm[0:1]), ex)

    w_uq = ex.need("w_uq", h2)
    hk, hn1, ckv_raw, ckv, kr, kn, vv, cq_raw, cq, q = _attn_pre(
        h2, rep["kv_in_norm"], attn_norm[1:2], rep["kv_latent_norm"], rep["q_latent_norm"], ex.need("w_dkv", h2),
        ex.need("w_kr", h2), ex.need("w_uk", h2), ex.need("w_uv", h2), ex.need("w_dq", h2), w_uq, tables)

    o, lse = _attn_fwd(q, kn, kr, vv)
    ex.at("attn_fwd", o)
    w_o = ex.need("w_o", o)
    h3, hf1 = _out_norm("attn_out", o, w_o, h2, ffn_norm[1:2])
    (loss, dh4, dh4_bf, d_final), ffn1 = _ffn_layer_fwd("f1", h3, hf1, ex, final=(final_norm.reshape(1, D), tgt))

    dh3, dh3_bf, d_fn1, dcw1, dcb1, do = _ffn_layer_bwd("f1", h3, ffn_norm[1:2], ex, ffn1, dh4, dh4_bf,
                                                        post=(w_o, NT, N_HEADS * V_HEAD))
    ex.at("f1_bwd", dh3)

    dq_pre, dkn, dkr, dvv = _attn_bwd(q, kn, kr, vv, o, do, lse, tables)

    dcq_raw_bf, dckv_raw_bf, dkr_raw_bf, dh2, dh2_bf, d_qln, d_kvln, d_an1, d_kvin = _attn_post(
        dq_pre, dkn, dvv, dkr, cq_raw, ckv_raw, h2, dh3, rep["q_latent_norm"], rep["kv_latent_norm"], attn_norm[1:2],
        rep["kv_in_norm"], w_uq, ex.need("w_uk", dkn), ex.need("w_uv", dvv), ex.need("w_dq", dq_pre),
        ex.need("w_dkv", dkn), ex.need("w_kr", dkr), tables)
    g_uq, g_dq, g_o = _wgrads("g_q", [(dq_pre, cq, QK_NOPE + QK_ROPE), (hn1, dcq_raw_bf), (o, dh3_bf)])
    ex.grad("w_uq", None, g_uq.reshape(1, N_DEV, QK_NOPE + QK_ROPE, Q_LORA))
    ex.grad("w_dq", None, g_dq.reshape(1, N_DEV, D // N_DEV, Q_LORA))
    ex.grad("w_o", None, g_o.reshape(1, N_DEV, D // N_DEV, D))

    g_uk, g_uv, g_dkv, g_kr = _wgrads("g_kv", [(ckv, dkn), (ckv, dvv), (hk, dckv_raw_bf), (dkr_raw_bf, hk, QK_ROPE)])
    ex.grad("w_uk", None, g_uk)
    ex.grad("w_uv", None, g_uv)
    ex.grad("w_dkv", None, g_dkv.reshape(1, N_DEV, D // N_DEV, KV_LORA))
    ex.grad("w_kr", None, g_kr)
    ex.at("kv_bwd", dh2)

    dh1, dh1_bf, d_fn0, dcw0, dcb0 = _ffn_layer_bwd("f0", h1, ffn_norm[0:1], ex, ffn0, dh2, dh2_bf)
    ex.at("f0_bwd", dh1)

    ex.grad("sc_w_out", None, _mm_wgrad("g_sc_w_out", y, dh1_bf).reshape(1, N_DEV, D // N_DEV, D))
    dz, d_scw = _mixer_out_bwd(dh1_bf, ex.need("sc_w_out", dh1_bf), zb, zc, zu, ex.need("sc_conv_w", dh1_bf))
    g_in = _mm_wgrad("g_sc_w_in", hn0, dz)
    ex.grad("sc_w_in", None, g_in)
    ex.at("sc_bwd", g_in)
    ex.at("d_l0_in", g_in)
    w_in = ex.need("sc_w_in", dz)
    grad_x, _, (d_an0,) = _mm_sum(
        "d_l0_in", [(dz[None], pl.BlockSpec((1, TS, dz.shape[1]), lambda i: (0, i, 0)),
                     w_in[None], pl.BlockSpec((1,) + w_in.shape, lambda i: (0, 0, 0)), NT, 0)],
        norm_bwd=(x, [attn_norm[0:1]], dh1),
        grid=(T // TS,), o_spec=pl.BlockSpec((TS, D), lambda i: (i, 0)), o_shape=(T, D), o_dtype=F32)

    small = {
        "attn_norm": jnp.concatenate([d_an0, d_an1], axis=0),
        "ffn_norm": jnp.concatenate([d_fn0, d_fn1], axis=0),
        "final_norm": d_final.reshape(D),
        "kv_in_norm": d_kvin.reshape(D),
        "kv_latent_norm": d_kvln.reshape(KV_LORA),
        "q_latent_norm": d_qln,
        "ffn_conv_b": jnp.stack([dcb0, dcb1]).transpose(0, 2, 1, 3).reshape(2, D_FF),
        "sc_conv_w": d_scw,
        "ffn_conv_w": jnp.stack([dcw0, dcw1]).transpose(0, 2, 1, 3).reshape(2, 3, D_FF),
    }
    return loss, grad_x, small


def _place():
    return lax.axis_index("x"), lax.axis_index("y"), lax.axis_index("c")


def _peers():
    x, y, c = _place()
    return (x, y, 1 - c), [(1 - x, y), (x, 1 - y), (1 - x, 1 - y)]


def _window(ref, kind, dev):
    if kind == "blocked":
        return ref.at[:, dev]
    width = ref.shape[-1] // N_DEV
    return ref.at[:, pl.ds(pl.multiple_of(dev * width, 128), width)]


HBM_SPEC = pl.BlockSpec(memory_space=pltpu.HBM)
SEM_SPEC = pl.BlockSpec(memory_space=pltpu.SEMAPHORE)
EFFECT = pltpu.SideEffectType.DATAFLOW_SIDE_EFFECTING
TOKEN = jax.ShapeDtypeStruct((8, 128), F32)


def _hbm(a):
    return pltpu.with_memory_space_constraint(a, pltpu.HBM)


def _copies_start(name, jobs):
    nj = len(jobs)
    counts = [(len(srcs), len(lands)) for srcs, lands, _, _ in jobs]
    n_arr = sum(ns + nl for ns, nl in counts)

    def body(*refs):
        sems, token = refs[n_arr:n_arr + 2 * nj], refs[-1]
        at = 0
        for j, ((ns, nl), (_, _, ncopy, plan)) in enumerate(zip(counts, jobs)):
            copies = plan(refs[at:at + ns], refs[at + ns:at + ns + nl])
            assert len(copies) == ncopy
            for k, (sent, dst, to, _) in enumerate(copies):
                pltpu.make_async_remote_copy(src_ref=sent, dst_ref=dst, send_sem=sems[2 * j].at[k],
                                             recv_sem=sems[2 * j + 1].at[k], device_id=to, device_id_type=MESH).start()
            at += ns + nl
        token[...] = jnp.zeros_like(token)

    arrays = [a for srcs, lands, _, _ in jobs for a in list(srcs) + list(lands)]
    sem_shapes = [pltpu.SemaphoreType.DMA((ncopy,)) for _, _, ncopy, _ in jobs for _ in range(2)]
    outs = pl.pallas_call(
        body, name=name, in_specs=[HBM_SPEC] * n_arr,
        out_specs=[SEM_SPEC] * (2 * nj) + [HBM_SPEC] * n_arr + [VMEM_SPEC],
        out_shape=sem_shapes + [pltpu.HBM(a.shape, a.dtype) for a in arrays] + [TOKEN],
        input_output_aliases={i: 2 * nj + i for i in range(n_arr)},
        compiler_params=pltpu.CompilerParams(has_side_effects=EFFECT))(*[_hbm(a) for a in arrays])
    _Chain.last = outs[-1]
    flights, at = [], 2 * nj
    for j, (ns, nl) in enumerate(counts):
        flights.append((outs[2 * j], outs[2 * j + 1], list(outs[at:at + ns]), list(outs[at + ns:at + ns + nl])))
        at += ns + nl
    return flights


def _copies_wait(name, started, ncopy, plan):
    send, recv, srcs, lands = started
    ns, nl = len(srcs), len(lands)

    def body(*refs):
        send_ref, recv_ref, token = refs[ns + nl], refs[ns + nl + 1], refs[-1]
        copies = plan(refs[:ns], refs[ns:ns + nl])
        assert len(copies) == ncopy
        for k, (sent, _, to, landed) in enumerate(copies):
            cp = pltpu.make_async_remote_copy(src_ref=sent, dst_ref=landed, send_sem=send_ref.at[k],
                                              recv_sem=recv_ref.at[k], device_id=to, device_id_type=MESH)
            cp.wait_send()
            cp.wait_recv()
        token[...] = jnp.zeros_like(token)

    arrays = list(srcs) + list(lands)
    outs = pl.pallas_call(
        body, name=name, in_specs=[HBM_SPEC] * (ns + nl) + [SEM_SPEC] * 2 + [ANY_SPEC],
        out_specs=[HBM_SPEC] * (ns + nl) + [VMEM_SPEC], out_shape=[pltpu.HBM(a.shape, a.dtype) for a in arrays] + [TOKEN],
        input_output_aliases={i: i for i in range(ns + nl)},
        compiler_params=pltpu.CompilerParams(has_side_effects=EFFECT))(*arrays, send, recv, _Chain.last)
    _Chain.last = outs[-1]
    return list(outs[:ns]), list(outs[ns:-1])


def _plan_gather_chips(kinds):
    def plan(srcs, lands):
        x, y, c = _place()
        sibling, chips = _peers()
        out = []
        for t, kind in enumerate(kinds):
            mine = _window(lands[t], kind, 4 * x + 2 * y + c)
            out.append((srcs[t], mine, (x, y, c), mine))
            out.append((srcs[t], mine, sibling, _window(lands[t], kind, 4 * x + 2 * y + 1 - c)))
            for px, py in chips:
                out.append((srcs[t], mine, (px, py, c), _window(lands[t], kind, 4 * px + 2 * py + c)))
        return out
    return plan, 5 * len(kinds)


def _plan_gather_all(n):
    def plan(srcs, lands):
        x, y, c = _place()
        out = []
        for t in range(n):
            mine = lands[t].at[:, 4 * x + 2 * y + c]
            for m in range(N_DEV):
                px, py, pc = (1 - x if m & 4 else x), (1 - y if m & 2 else y), (1 - c if m & 1 else c)
                out.append((srcs[t], mine, (px, py, pc), lands[t].at[:, 4 * px + 2 * py + pc]))
        return out
    return plan, N_DEV * n


def _plan_gather_sibling(kinds):
    def plan(srcs, lands):
        _, _, c = _place()
        sibling, chips = _peers()
        out = []
        for t, kind in enumerate(kinds):
            for px, py in chips:
                w = _window(lands[t], kind, 4 * px + 2 * py + c)
                out.append((w, w, sibling, _window(lands[t], kind, 4 * px + 2 * py + 1 - c)))
        return out
    return plan, 3 * len(kinds)


def _plan_scatter_sibling(kinds):
    def plan(srcs, lands):
        _, _, c = _place()
        sibling, _ = _peers()
        out = []
        for t, kind in enumerate(kinds):
            for k in range(N_CHIP):
                out.append((_window(srcs[t], kind, 2 * k + 1 - c), lands[t].at[k], sibling, lands[t].at[k]))
        return out
    return plan, N_CHIP * len(kinds)


def _plan_scatter_chips(n):
    def plan(srcs, lands):
        x, y, c = _place()
        _, chips = _peers()
        out = []
        for t in range(n):
            for px, py in chips:
                out.append((srcs[t].at[2 * px + py], lands[t].at[2 * x + y], (px, py, c), lands[t].at[2 * px + py]))
        return out
    return plan, 3 * n


def _landing(shard, kind):
    if kind == "blocked":
        return lax.empty((shard.shape[0], N_DEV) + shard.shape[1:], shard.dtype)
    return lax.empty((shard.shape[0], N_DEV * shard.shape[1]), shard.dtype)


def _chip_sums(name, grads, kinds, recvs, c):
    n = len(grads)
    in_specs, out_specs, out_shape, args = [], [], [], []
    for gr, kind, rv in zip(grads, kinds, recvs):
        if kind == "blocked":
            rows, w = gr.shape[2], gr.shape[3]
            in_specs.append(pl.BlockSpec((None, None, rows, w), lambda k, cref: (0, 2 * k + cref[0], 0, 0)))
        else:
            rows, w = gr.shape[0], gr.shape[1] // N_DEV
            in_specs.append(pl.BlockSpec((rows, w), lambda k, cref: (0, 2 * k + cref[0])))
        blk = pl.BlockSpec((None, rows, w), lambda k, cref: (k, 0, 0))
        in_specs.append(blk)
        out_specs.append(blk)
        out_shape.append(jax.ShapeDtypeStruct((N_CHIP, rows, w), BF16))
        args += [gr, rv.reshape(N_CHIP, rows, w)]

    def body(*refs):
        for t in range(n):
            g_ref, r_ref, o_ref = refs[1 + 2 * t], refs[2 + 2 * t], refs[1 + 2 * n + t]
            o_ref[...] = (g_ref[...].astype(F32) + r_ref[...].astype(F32)).astype(BF16)

    return _pallas(body, name=name, n_prefetch=1, grid=(N_CHIP,), in_specs=in_specs, out_specs=out_specs,
                   out_shape=out_shape, params=_params(("parallel",)))(c, *args)


def _adamw_math(g, wv, mv, vv):
    m = ADAM_B1 * mv + (1.0 - ADAM_B1) * g
    v = ADAM_B2 * vv + (1.0 - ADAM_B2) * (g * g)
    m_hat = m / (1.0 - ADAM_B1 ** ADAM_STEP)
    v_hat = v / (1.0 - ADAM_B2 ** ADAM_STEP)
    delta = -ADAM_LR * (m_hat / (jnp.sqrt(v_hat) + ADAM_EPS) + ADAM_WD * wv)
    return delta, m, v


ADAM_STEPS = 2


def _adamw_group(name, items, chip_ids):
    n = len(items)
    in_specs, out_specs, out_shape, args, prevs = [], [], [], [chip_ids], []
    for own, recv, w3, m3, v3, layer, _ in items:
        nl, rows, w = w3.shape
        tr = rows // ADAM_STEPS
        assert tr % 16 == 0, (name, rows)
        in_specs += [pl.BlockSpec((None, tr, w), lambda i, ids, slot=slot: (ids[slot], i, 0)) for slot in range(4)]
        slab = pl.BlockSpec((None, tr, w), lambda i, ids, layer=layer: (layer, i, 0))
        in_specs += [slab] * 3
        out_specs += [slab] * 4
        out_shape += [jax.ShapeDtypeStruct((nl, rows, w), F32)] * 4
        args += [own, recv, recv, recv, w3, m3, v3]
    aliases = {}
    for t, item in enumerate(items):
        if item[6] is not None:
            for k in range(4):
                aliases[len(args) + k] = 4 * t + k
            in_specs += [ANY_SPEC] * 4
            args += list(item[6])
            prevs.append(t)
    n_in = 1 + 7 * n + 4 * len(prevs)

    def body(*refs):
        for t in range(n):
            own_ref, r1_ref, r2_ref, r3_ref, w_ref, m_ref, v_ref = refs[1 + 7 * t:8 + 7 * t]
            g_ref, d_ref, nm_ref, nv_ref = refs[n_in + 4 * t:n_in + 4 * t + 4]
            g = ((own_ref[...].astype(F32) + r1_ref[...].astype(F32)) + r2_ref[...].astype(F32)) + r3_ref[...].astype(F32)
            g_ref[...] = g
            d_ref[...], nm_ref[...], nv_ref[...] = _adamw_math(g, w_ref[...], m_ref[...], v_ref[...])

    outs = _pallas(body, name=name, n_prefetch=1, grid=(ADAM_STEPS,), in_specs=in_specs, out_specs=out_specs,
                   out_shape=out_shape, aliases=aliases, params=_params(("parallel",)))(*args)
    return [list(outs[4 * t:4 * t + 4]) for t in range(n)]


SHARD_ROWS = 8


def _adamw_small(gathered, ws, ms, vs, me):
    n = len(gathered)
    full = [w is not None for w in ws]
    sharded = [w is not None and w.ndim == 3 for w in ws]
    args = list(gathered)
    out_shape = []
    for t in range(n):
        shape = jax.ShapeDtypeStruct(ws[t].shape if sharded[t] else gathered[t].shape[2:], F32)
        if full[t]:
            args += [ws[t], ms[t], vs[t]]
            out_shape += [shape] * 4
        else:
            out_shape += [shape]

    def body(*refs):
        i_in, i_out = n, len(args) + 1
        me_ref = refs[len(args)]
        for t in range(n):
            p_ref = refs[t]
            if sharded[t]:
                w_ref, m_ref, v_ref = refs[i_in:i_in + 3]
                taps, layers, _ = w_ref.shape
                mine = pl.ds(pl.multiple_of(me_ref[0] * SHARD_ROWS, SHARD_ROWS), SHARD_ROWS)
                g = p_ref[0, 0, mine, :]
                for k in range(1, N_DEV):
                    g = g + p_ref[0, k, mine, :]
                for l in range(layers):
                    for k in range(taps):
                        at = (k, slice(l, l + 1), slice(None))
                        row = g[l * taps + k:l * taps + k + 1]
                        refs[i_out][at] = row
                        refs[i_out + 1][at], refs[i_out + 2][at], refs[i_out + 3][at] = _adamw_math(
                            row, w_ref[at], m_ref[at], v_ref[at])
                i_in += 3
                i_out += 4
                continue
            g = p_ref[0, 0]
            for k in range(1, N_DEV):
                g = g + p_ref[0, k]
            refs[i_out][...] = g
            if full[t]:
                w_ref, m_ref, v_ref = refs[i_in:i_in + 3]
                refs[i_out + 1][...], refs[i_out + 2][...], refs[i_out + 3][...] = _adamw_math(
                    g, w_ref[...], m_ref[...], v_ref[...])
                i_in += 3
                i_out += 4
            else:
                i_out += 1

    outs = _pallas(body, name="adamw_small",
                   in_specs=[VMEM_SPEC] * len(args) + [pl.BlockSpec(memory_space=pltpu.SMEM)],
                   out_specs=[VMEM_SPEC] * len(out_shape), out_shape=out_shape,
                   params=pltpu.CompilerParams(vmem_limit_bytes=VMEM_LIMIT))(*args, me)
    result, i = [], 0
    for t in range(n):
        k = 4 if full[t] else 1
        result.append(list(outs[i:i + k]))
        i += k
    return result


KIND = {"sc_w_in": "cols", "sc_w_out": "blocked", "w_dkv": "blocked", "w_kr": "cols", "w_uk": "cols", "w_uv": "cols",
        "w_dq": "blocked", "w_uq": "blocked", "w_o": "blocked", "ffn_w_up": "blocked", "ffn_w_down": "blocked",
        "conv": "blocked"}
GATHER_GROUPS = (("mixer", ("sc_w_in",)),
                 ("mixer2", ("sc_w_out", "conv")),
                 ("up0", ("ffn_w_up0",)),
                 ("down0", ("ffn_w_down0",)),
                 ("attn", ("w_dkv", "w_kr", "w_uk", "w_uv", "w_dq", "w_uq", "w_o")),
                 ("ffn1", ("ffn_w_up1", "ffn_w_down1")))
SCATTER_GROUPS = (("ffn1", (("ffn_w_up", 1), ("ffn_w_down", 1))),
                  ("attn", (("w_o", None), ("w_uq", None), ("w_dq", None), ("w_uk", None), ("w_uv", None),
                            ("w_dkv", None), ("w_kr", None))),
                  ("ffn0", (("ffn_w_up", 0), ("ffn_w_down", 0))),
                  ("mixer", (("sc_w_out", None), ("sc_w_in", None))))
SCHEDULE = {
    "begin": (("gather_start", "mixer"),),
    "cast": (("gather_start", "mixer2"),),
    "l0_norm": (("gather_forward", "mixer"), ("gather_forward", "mixer2"), ("gather_start", "up0")),
    "l0_out": (("gather_forward", "up0"), ("gather_start", "down0"), ("gather_start", "attn")),
    "f0_up": (("gather_forward", "down0"), ("gather_forward", "attn"), ("gather_start", "ffn1")),
    "attn_fwd": (("gather_forward", "ffn1"),),
    "f1_gup": (("scatter_sibling", "ffn1"),),
    "f1_dhf": (("scatter_chips", "ffn1"),),
    "kv_bwd": (("scatter_sibling", "attn"),),
    "f0_dact": (("scatter_chips", "attn"),),
    "f0_gup": (("scatter_sibling", "ffn0"),),
    "f0_dhf": (("scatter_chips", "ffn0"),),
    "sc_bwd": (("scatter_sibling", "mixer"), ("scatter_done", "attn")),
    "d_l0_in": (("scatter_chips", "mixer"),),
}
FINISH = (("scatter_done", "ffn1"), ("scatter_done", "ffn0"), ("scatter_done", "mixer"))
STAGES = {"gather_start": 1, "gather_forward": 2, "gather_done": 3,
          "scatter_sibling": 1, "scatter_chips": 2, "scatter_done": 3}
SMALL_W_ROWS = 24


def _pack(arrays, rows):
    flat = jnp.concatenate([a.reshape(-1).astype(F32) for a in arrays])
    return jnp.pad(flat, (0, rows * 128 - flat.shape[0])).reshape(rows, 128)


def _cast_shards(items):
    arrays = []
    for a, _, _ in items:
        if not any(a is b for b in arrays):
            arrays.append(a)
    slot = [next(i for i, b in enumerate(arrays) if b is a) for a, _, _ in items]

    def body(*refs):
        for t, (a, layer, rows) in enumerate(items):
            w_ref, o_ref = refs[slot[t]], refs[len(arrays) + t]
            r, c = a.shape[-2:]
            if a.ndim == 3:
                o_ref[:, :r] = w_ref[(layer or 0):(layer or 0) + 1].astype(BF16)
                if rows > r:
                    o_ref[:, r:] = jnp.zeros((1, rows - r, c), BF16)
            else:
                o_ref[:r] = w_ref[...].astype(BF16)
                if rows > r:
                    o_ref[r:] = jnp.zeros((rows - r, c), BF16)

    out_shape = [jax.ShapeDtypeStruct(((1,) if a.ndim == 3 else ()) + (rows, a.shape[-1]), BF16)
                 for a, _, rows in items]
    return _pallas(body, name="cast_shards", in_specs=[VMEM_SPEC] * len(arrays), out_specs=[VMEM_SPEC] * len(items),
                   out_shape=out_shape, params=pltpu.CompilerParams(vmem_limit_bytes=VMEM_LIMIT))(*arrays)


STORED_TRANSPOSED = ("ffn_w_up", "w_uq", "w_kr")


def _stored(name, a):
    return jnp.swapaxes(a, -1, -2) if name in STORED_TRANSPOSED else a


def _base(name):
    if name.startswith("ffn_w_") and name[-1] in "01":
        return name[:-1], int(name[-1])
    return name, None


class _Exchange:
    def __init__(self, wts, mom, var, ffn_conv_b):
        self.wts, self.mom, self.var = wts, mom, var
        x, y, c = _place()
        self.c_arr = jnp.reshape(c, (1,)).astype(jnp.int32)
        chip = 2 * x + y
        self.chip_ids = jnp.stack([chip, chip ^ 1, chip ^ 2, chip ^ 3]).astype(jnp.int32)
        self.ready = {"ffn_cb0": ffn_conv_b.reshape(2, N_FF_BLK, 1, FF_BLK)[0],
                      "ffn_cb1": ffn_conv_b.reshape(2, N_FF_BLK, 1, FF_BLK)[1]}
        self.gathers, self.group_of = {}, {}
        self.grads, self.scatters, self.results, self.queue = {}, {}, {}, []
        for gname, names in GATHER_GROUPS:
            self.gathers[gname] = dict(stage=0, names=names, kinds=[KIND[_base(nm)[0]] for nm in names])
            for nm in names:
                self.group_of[nm] = gname
        for nm in ("sc_conv_w", "ffn_cw0", "ffn_cw1"):
            self.group_of[nm] = self.group_of["conv"]
        self.cast, self.f32 = {}, {}
        self.at("begin", None)
        later = [nm for gname, names in GATHER_GROUPS[1:] for nm in names if nm != "conv"]
        self.cast = dict(zip(later, _cast_shards([self._shard_f32(nm) for nm in later])))
        self.at("cast", None)

    def _shard_f32(self, name):
        base, layer = _base(name)
        if base not in self.f32:
            a = _stored(base, self.wts[base])
            self.f32[base] = a.reshape(a.shape[-2:]) if KIND[base] == "cols" else a.reshape((-1,) + a.shape[-2:])
        a = self.f32[base]
        return a, layer, {"w_kr": 128, "w_uq": QK_PAD}.get(base, a.shape[-2])

    def _shard(self, name):
        if name in self.cast:
            return self.cast[name]
        if name == "conv":
            return _pack([self.wts["sc_conv_w"], self.wts["ffn_conv_w"]], SMALL_W_ROWS).reshape(1, SMALL_W_ROWS, 128)
        base, layer = _base(name)
        a = _stored(base, self.wts[base])
        if layer is not None:
            a = a[layer:layer + 1]
        if KIND[base] == "cols":
            return a.reshape(a.shape[-2], a.shape[-1]).astype(BF16)
        return a.reshape((-1,) + a.shape[-2:]).astype(BF16)

    def _start(self, name, srcs, lands, ncopy, plan, st):
        self.queue.append((name, (srcs, lands, ncopy, plan), st))

    def _flush(self):
        if self.queue:
            flights = _copies_start("__".join(name for name, _, _ in self.queue), [job for _, job, _ in self.queue])
            for (_, _, st), flight in zip(self.queue, flights):
                st["flight"] = flight
            self.queue = []

    def _flight(self, st):
        self._flush()
        return st["flight"]

    def _gather_to(self, gname, stage, after):
        st = self.gathers[gname]
        if st["stage"] < 1 <= stage:
            shards = [self._shard(nm) for nm in st["names"]]
            lands = [_landing(s, kind) for s, kind in zip(shards, st["kinds"])]
            plan, ncopy = _plan_gather_chips(st["kinds"])
            self._start(f"ag_{gname}_chips", shards, lands, ncopy, plan, st)
            st["stage"] = 1
        if st["stage"] < 2 <= stage:
            plan, ncopy = _plan_gather_chips(st["kinds"])
            _, lands = _copies_wait(f"ag_{gname}_chips_wait", self._flight(st), ncopy, plan)
            plan, ncopy = _plan_gather_sibling(st["kinds"])
            self._start(f"ag_{gname}_sibling", [], lands, ncopy, plan, st)
            st["stage"] = 2
        if st["stage"] < 3 <= stage:
            plan, ncopy = _plan_gather_sibling(st["kinds"])
            _, lands = _copies_wait(f"ag_{gname}_sibling_wait", self._flight(st), ncopy, plan)
            for nm, land in zip(st["names"], lands):
                self._arrived(nm, land)
            st["stage"] = 3

    def _arrived(self, name, land):
        if name == "conv":
            conv = land.reshape(N_DEV, SMALL_W_ROWS * 128)
            self.ready["sc_conv_w"] = conv[:, :3 * 128].reshape(N_DEV, 3, 128).transpose(1, 0, 2).reshape(3, D)
            fcw = conv[:, 3 * 128:3 * 128 + 6 * 352].reshape(N_DEV, 2, 3, 352).transpose(1, 2, 0, 3)
            fcw = fcw.reshape(2, 3, N_FF_BLK, FF_BLK).transpose(0, 2, 1, 3)
            self.ready["ffn_cw0"], self.ready["ffn_cw1"] = fcw[0], fcw[1]
        elif name in ("sc_w_in", "w_uk", "w_uv", "w_kr") or name.startswith("ffn_w_up"):
            self.ready[name] = land
        elif name.startswith("ffn_w_down"):
            self.ready[name] = land.reshape(1, N_FF_BLK, FF_BLK, D)
        elif name == "w_uq":
            self.ready[name] = land.reshape(N_HEADS, QK_PAD, Q_LORA)
        else:
            self.ready[name] = land.reshape(D, land.shape[-1])

    def need(self, name, after):
        if name not in self.ready:
            self._gather_to(self.group_of[name], 3, after)
            self._flush()
        return self.ready[name]

    def grad(self, name, layer, array):
        self.grads[(name, layer)] = array

    def _scatter_to(self, gname, stage, after):
        keys = dict(SCATTER_GROUPS)[gname]
        st = self.scatters.setdefault(gname, dict(stage=0))
        kinds = [KIND[nm] for nm, _ in keys]
        if st["stage"] < 1 <= stage:
            grads = [self.grads[key] for key in keys]
            lands = []
            for gr, kind in zip(grads, kinds):
                shard = (gr.shape[0],) + gr.shape[2:] if kind == "blocked" else (gr.shape[0], gr.shape[1] // N_DEV)
                lands.append(lax.empty((N_CHIP,) + shard, BF16))
            plan, ncopy = _plan_scatter_sibling(kinds)
            self._start(f"rs_{gname}_sibling", grads, lands, ncopy, plan, st)
            st["stage"] = 1
        if st["stage"] < 2 <= stage:
            plan, ncopy = _plan_scatter_sibling(kinds)
            grads, recvs = _copies_wait(f"rs_{gname}_sibling_wait", self._flight(st), ncopy, plan)
            sums = _chip_sums(f"rs_{gname}_sums", grads, kinds, recvs, self.c_arr)
            lands = [lax.empty(s.shape, BF16) for s in sums]
            plan, ncopy = _plan_scatter_chips(len(sums))
            self._start(f"rs_{gname}_chips", sums, lands, ncopy, plan, st)
            st["stage"] = 2
        if st["stage"] < 3 <= stage:
            plan, ncopy = _plan_scatter_chips(len(keys))
            sums, recvs = _copies_wait(f"rs_{gname}_chips_wait", self._flight(st), ncopy, plan)
            items = []
            for (nm, layer), own, rv in zip(keys, sums, recvs):
                nl = 1 if layer is None else 2
                rows, w = own.shape[1], own.shape[2]
                w3, m3, v3 = (_stored(nm, src[nm]).reshape(nl, rows, w) for src in (self.wts, self.mom, self.var))
                items.append((own, rv, w3, m3, v3, 0 if layer is None else layer, self.results.get(nm)))
            outs = _adamw_group(f"adamw_{gname}", items, self.chip_ids)
            for (nm, _), out in zip(keys, outs):
                self.results[nm] = out
            st["stage"] = 3

    def at(self, place, after):
        for action, gname in SCHEDULE.get(place, ()):
            self._advance(action, gname, after)
        self._flush()

    def _advance(self, action, gname, after):
        if action.startswith("gather"):
            self._gather_to(gname, STAGES[action], after)
        else:
            self._scatter_to(gname, STAGES[action], after)

    def finish(self, after):
        for action, gname in FINISH:
            self._advance(action, gname, after)
        for gname, _ in SCATTER_GROUPS:
            self._scatter_to(gname, 3, after)
        return {nm: [_stored(nm, o.reshape(_stored(nm, self.wts[nm]).shape)) for o in outs]
                for nm, outs in self.results.items()}


REPLICATED = ("attn_norm", "ffn_norm", "final_norm", "kv_in_norm", "kv_latent_norm", "q_latent_norm", "ffn_conv_b")
WEIGHTS = ("attn_norm", "ffn_norm", "final_norm", "sc_w_in", "sc_conv_w", "sc_w_out", "kv_in_norm", "w_dkv",
           "kv_latent_norm", "w_kr", "w_uk", "w_uv", "w_dq", "q_latent_norm", "w_uq", "w_o", "ffn_w_up", "ffn_conv_w",
           "ffn_conv_b", "ffn_w_down")


def kernel(x, positions, attn_norm, ffn_norm, final_norm, sc_w_in, sc_conv_w, sc_w_out, kv_in_norm, w_dkv, kv_latent_norm, w_kr, w_uk, w_uv, w_dq, q_latent_norm, w_uq, w_o, ffn_w_up, ffn_conv_w, ffn_conv_b, ffn_w_down, loss_target, m_attn_norm, m_ffn_norm, m_final_norm, m_sc_w_in, m_sc_conv_w, m_sc_w_out, m_kv_in_norm, m_w_dkv, m_kv_latent_norm, m_w_kr, m_w_uk, m_w_uv, m_w_dq, m_q_latent_norm, m_w_uq, m_w_o, m_ffn_w_up, m_ffn_conv_w, m_ffn_conv_b, m_ffn_w_down, v_attn_norm, v_ffn_norm, v_final_norm, v_sc_w_in, v_sc_conv_w, v_sc_w_out, v_kv_in_norm, v_w_dkv, v_kv_latent_norm, v_w_kr, v_w_uk, v_w_uv, v_w_dq, v_q_latent_norm, v_w_uq, v_w_o, v_ffn_w_up, v_ffn_conv_w, v_ffn_conv_b, v_ffn_w_down):
    wts = dict(attn_norm=attn_norm, ffn_norm=ffn_norm, final_norm=final_norm, sc_w_in=sc_w_in, sc_conv_w=sc_conv_w,
               sc_w_out=sc_w_out, kv_in_norm=kv_in_norm, w_dkv=w_dkv, kv_latent_norm=kv_latent_norm, w_kr=w_kr,
               w_uk=w_uk, w_uv=w_uv, w_dq=w_dq, q_latent_norm=q_latent_norm, w_uq=w_uq, w_o=w_o, ffn_w_up=ffn_w_up,
               ffn_conv_w=ffn_conv_w, ffn_conv_b=ffn_conv_b, ffn_w_down=ffn_w_down)
    mom = dict(attn_norm=m_attn_norm, ffn_norm=m_ffn_norm, final_norm=m_final_norm, sc_w_in=m_sc_w_in,
               sc_conv_w=m_sc_conv_w, sc_w_out=m_sc_w_out, kv_in_norm=m_kv_in_norm, w_dkv=m_w_dkv,
               kv_latent_norm=m_kv_latent_norm, w_kr=m_w_kr, w_uk=m_w_uk, w_uv=m_w_uv, w_dq=m_w_dq,
               q_latent_norm=m_q_latent_norm, w_uq=m_w_uq, w_o=m_w_o, ffn_w_up=m_ffn_w_up, ffn_conv_w=m_ffn_conv_w,
               ffn_conv_b=m_ffn_conv_b, ffn_w_down=m_ffn_w_down)
    var = dict(attn_norm=v_attn_norm, ffn_norm=v_ffn_norm, final_norm=v_final_norm, sc_w_in=v_sc_w_in,
               sc_conv_w=v_sc_conv_w, sc_w_out=v_sc_w_out, kv_in_norm=v_kv_in_norm, w_dkv=v_w_dkv,
               kv_latent_norm=v_kv_latent_norm, w_kr=v_w_kr, w_uk=v_w_uk, w_uv=v_w_uv, w_dq=v_w_dq,
               q_latent_norm=v_q_latent_norm, w_uq=v_w_uq, w_o=v_w_o, ffn_w_up=v_ffn_w_up, ffn_conv_w=v_ffn_conv_w,
               ffn_conv_b=v_ffn_conv_b, ffn_w_down=v_ffn_w_down)
    xi, yi, ci = _place()
    me = 4 * xi + 2 * yi + ci
    _Chain.last = None

    ex = _Exchange(wts, mom, var, ffn_conv_b)
    rep = {
        "attn_norm": attn_norm, "ffn_norm": ffn_norm, "final_norm": final_norm,
        "kv_in_norm": kv_in_norm.reshape(1, D), "kv_latent_norm": kv_latent_norm.reshape(1, KV_LORA),
        "q_latent_norm": q_latent_norm.reshape(1, Q_LORA),
    }
    loss, grad_x, small = _local_step(x.reshape(T, D), positions.reshape(T, 1), loss_target.reshape(T, D), rep, ex)

    def rows_of(a):
        return a.reshape(-1, a.shape[-1])

    def device_rows(a):
        taps, c = a.shape[-2], a.shape[-1] // N_DEV
        rows = a.reshape(-1, taps, N_DEV, c).transpose(2, 0, 1, 3).reshape(N_DEV, -1, c)
        return jnp.pad(rows, ((0, 0), (0, SHARD_ROWS - rows.shape[1]), (0, 0))).reshape(N_DEV * SHARD_ROWS, c)

    def taps_first(a):
        return jnp.transpose(a, (1, 0, 2))

    sharded = ("sc_conv_w", "ffn_conv_w")
    shards = ([loss.reshape(1, 1, 128)] + [rows_of(small[nm])[None] for nm in REPLICATED]
              + [device_rows(small[nm])[None] for nm in sharded])
    plan, ncopy = _plan_gather_all(len(shards))
    flight, = _copies_start("ag_small", [(shards, [lax.empty((1, N_DEV) + s.shape[1:], F32) for s in shards], ncopy, plan)])
    results = ex.finish(grad_x)
    _, gathered = _copies_wait("ag_small_wait", flight, ncopy, plan)
    params = [[None] + [rows_of(src[nm]) for nm in REPLICATED] + [taps_first(src[nm]) for nm in sharded]
              for src in (wts, mom, var)]
    summed = _adamw_small(gathered, *params, me.astype(jnp.int32).reshape(1))
    loss_total = summed[0][0][0, 0]
    for nm, vals in zip(REPLICATED, summed[1:1 + len(REPLICATED)]):
        results[nm] = [a.reshape(wts[nm].shape) for a in vals]
    for nm, vals in zip(sharded, summed[1 + len(REPLICATED):]):
        results[nm] = [taps_first(a) for a in vals]

    outs = [loss_total, grad_x.reshape(1, T, D)]
    for slot in range(4):
        outs.extend(results[nm][slot] for nm in WEIGHTS)
    return tuple(outs)
```

```python
import jax
import jax.numpy as jnp
from jax import lax
from jax.experimental import pallas as pl
from jax.experimental.pallas import tpu as pltpu

F32 = jnp.float32
BF16 = jnp.bfloat16

T = 2048
D = 1024
N_HEADS = 8
QK_NOPE = 128
QK_ROPE = 64
V_HEAD = 128
Q_LORA = 384
KV_LORA = 256
D_FF = 2816
CHUNK = 64
ROPE_THETA = 10000.0
EPS = 1e-6
NEG_INF = -1e30
ADAM_LR = 0.001
ADAM_B1 = 0.9
ADAM_B2 = 0.999
ADAM_EPS = 1e-08
ADAM_WD = 0.01
ADAM_STEP = 10

N_DEV = 8
N_CHIP = 4
FF_BLK = D_FF * 2 // N_DEV
N_FF_BLK = D_FF // FF_BLK
QK_PAD = 256
HALO = 16

TM = 1024
TS = 512
TR = 256
TQ = 512
VMEM_LIMIT = 56 * 1024 * 1024

NN = (((1,), (0,)), ((), ()))
NT = (((1,), (1,)), ((), ()))
TN = (((0,), (0,)), ((), ()))
MESH = pl.DeviceIdType.MESH


def _params(sem):
    return pltpu.CompilerParams(dimension_semantics=sem, vmem_limit_bytes=VMEM_LIMIT)


ANY_SPEC = pl.BlockSpec(memory_space=pl.ANY)
VMEM_SPEC = pl.BlockSpec(memory_space=pltpu.VMEM)


class _Chain:
    last = None


def _pallas(body, *, name, in_specs, out_specs, out_shape, grid=(), scratch_shapes=(), n_prefetch=0, aliases=None,
            params=None):
    def run(*args):
        after = _Chain.last
        n_lead = len(args)
        specs, operands, fn = list(in_specs), list(args), body
        if after is not None:
            def fn(*refs):
                return body(*refs[:n_lead], *refs[n_lead + 1:])
            specs.append(ANY_SPEC)
            operands.append(after)
        kw = dict(name=name, out_shape=out_shape, input_output_aliases=aliases or {})
        if params is not None:
            kw["compiler_params"] = params
        if n_prefetch:
            kw["grid_spec"] = pltpu.PrefetchScalarGridSpec(
                num_scalar_prefetch=n_prefetch, grid=grid, in_specs=specs, out_specs=out_specs,
                scratch_shapes=scratch_shapes)
        else:
            kw.update(grid=grid, in_specs=specs, out_specs=out_specs, scratch_shapes=scratch_shapes)
        outs = pl.pallas_call(fn, **kw)(*operands)
        _Chain.last = outs[0] if isinstance(outs, (list, tuple)) else outs
        return outs
    return run


def _mm(name, a, b, *, grid, a_spec, b_spec, o_spec, o_shape, o_dtype, dims, k_axis=None, acc_shape=None,
        add=None, add_spec=None):
    nk = grid[k_axis] if k_axis is not None else 1
    has_add = add is not None

    def body(*refs):
        a_ref, b_ref = refs[0], refs[1]
        p = 2
        add_ref = None
        if has_add:
            add_ref = refs[p]
            p += 1
        o_ref = refs[p]
        p += 1
        r = lax.dot_general(a_ref[...].astype(BF16), b_ref[...].astype(BF16), dims, preferred_element_type=F32)
        if k_axis is None:
            if has_add:
                r = r + add_ref[...].astype(F32)
            o_ref[...] = r.astype(o_dtype)
        else:
            acc = refs[p]
            k = pl.program_id(k_axis)

            @pl.when(k == 0)
            def _():
                acc[...] = r

            @pl.when(k > 0)
            def _():
                acc[...] += r

            @pl.when(k == nk - 1)
            def _():
                t = acc[...]
                if has_add:
                    t = t + add_ref[...].astype(F32)
                o_ref[...] = t.astype(o_dtype)

    in_specs = [a_spec, b_spec]
    args = [a, b]
    if has_add:
        in_specs.append(add_spec if add_spec is not None else o_spec)
        args.append(add)
    sem = tuple("arbitrary" if ax == k_axis else "parallel" for ax in range(len(grid)))
    scratch = [pltpu.VMEM(acc_shape, F32)] if k_axis is not None else []
    return _pallas(body, name=name, grid=grid, in_specs=in_specs, out_specs=o_spec,
                   out_shape=jax.ShapeDtypeStruct(o_shape, o_dtype), scratch_shapes=scratch, params=_params(sem))(*args)


def _mm_sum(name, parts, *, grid, o_spec, o_shape, o_dtype, add=None, norm_bwd=None, post=None):
    has_add = add is not None
    np_ = len(parts)
    nn = 1 if norm_bwd is None else len(norm_bwd[1])
    has_res = norm_bwd is not None and norm_bwd[2] is not None
    has_post = post is not None

    def body(*refs):
        accs = [None] * nn
        for p, (_, _, _, _, dims, n) in enumerate(parts):
            a_ref, b_ref = refs[2 * p], refs[2 * p + 1]
            for k in range(a_ref.shape[0]):
                r = lax.dot_general(a_ref[k], b_ref[k], dims, preferred_element_type=F32)
                accs[n] = r if accs[n] is None else accs[n] + r
        if norm_bwd is None:
            acc = accs[0]
            if has_add:
                acc = acc + refs[2 * np_][...]
            refs[-1][...] = acc.astype(o_dtype)
            return
        x_ref, g_refs = refs[2 * np_], refs[2 * np_ + 1:2 * np_ + 1 + nn]
        n_in = 2 * np_ + 1 + nn + has_res + has_post
        dx_ref, dxb_ref, dg_refs = refs[n_in], refs[n_in + 1], refs[n_in + 2:n_in + 2 + nn]
        xv = x_ref[...]
        r = lax.rsqrt(jnp.mean(xv * xv, axis=-1, keepdims=True) + EPS)
        xn = xv * r
        dx = refs[2 * np_ + 1 + nn][...] if has_res else None
        sums = []
        for acc, g_ref in zip(accs, g_refs):
            gdy = acc * g_ref[...]
            t = r * (gdy - xn * jnp.mean(gdy * xn, axis=-1, keepdims=True))
            dx = t if dx is None else dx + t
            sums.append(jnp.sum(acc * xn, axis=0, keepdims=True))
        dx_ref[...] = dx
        dxb = dx.astype(BF16)
        dxb_ref[...] = dxb
        if has_post:
            refs[n_in + 2 + nn][...] = lax.dot_general(dxb, refs[n_in - 1][...], post[1],
                                                       preferred_element_type=F32).astype(BF16)

        @pl.when(pl.program_id(0) == 0)
        def _():
            for dg_ref, part in zip(dg_refs, sums):
                dg_ref[...] = part

        @pl.when(pl.program_id(0) > 0)
        def _():
            for dg_ref, part in zip(dg_refs, sums):
                dg_ref[...] += part

    in_specs, args = [], []
    for a, a_spec, b, b_spec, _, _ in parts:
        in_specs += [a_spec, b_spec]
        args += [a, b]
    if norm_bwd is None:
        if has_add:
            in_specs.append(o_spec)
            args.append(add)
        return _pallas(body, name=name, grid=grid, in_specs=in_specs, out_specs=o_spec,
                       out_shape=jax.ShapeDtypeStruct(o_shape, o_dtype),
                       params=_params(("parallel",) * len(grid)))(*args)
    x, gains, dres = norm_bwd
    vec = pl.BlockSpec((1, o_shape[1]), lambda i: (0, 0))
    in_specs += [o_spec] + [vec] * nn + ([o_spec] if has_res else [])
    args += [x] + list(gains) + ([dres] if has_res else [])
    out_specs = [o_spec, o_spec] + [vec] * nn
    out_shape = ([jax.ShapeDtypeStruct(o_shape, F32), jax.ShapeDtypeStruct(o_shape, BF16)]
                 + [jax.ShapeDtypeStruct((1, o_shape[1]), F32)] * nn)
    if has_post:
        in_specs.append(pl.BlockSpec(post[0].shape, lambda i: (0, 0)))
        args.append(post[0])
        out_specs.append(pl.BlockSpec((o_spec.block_shape[0], post[2]), lambda i: (i, 0)))
        out_shape.append(jax.ShapeDtypeStruct((o_shape[0], post[2]), BF16))
    outs = _pallas(body, name=name, grid=grid, in_specs=in_specs, out_specs=out_specs, out_shape=out_shape,
                   params=_params(("arbitrary",)))(*args)
    if has_post:
        return outs[0], outs[1], list(outs[2:2 + nn]), outs[2 + nn]
    return outs[0], outs[1], list(outs[2:])


def _mm_rows(name, a, b, dims, o_dtype, n_out, *, tn=None, add=None):
    k = a.shape[1]
    tn = n_out if tn is None else tn
    if dims == NN:
        b_spec = pl.BlockSpec((k, tn), lambda n, i: (0, n))
    else:
        b_spec = pl.BlockSpec((tn, k), lambda n, i: (n, 0))
    return _mm(name, a, b, grid=(n_out // tn, T // TM),
               a_spec=pl.BlockSpec((TM, k), lambda n, i: (i, 0)), b_spec=b_spec,
               o_spec=pl.BlockSpec((TM, tn), lambda n, i: (i, n)), o_shape=(T, n_out), o_dtype=o_dtype,
               dims=dims, add=add)


def _wgrads(name, jobs):
    jobs = [job if len(job) == 3 else (*job, job[0].shape[-1]) for job in jobs]
    arrays, index = [], {}
    for a, b, _ in jobs:
        for arr in (a, b):
            if id(arr) not in index:
                index[id(arr)] = len(arrays)
                arrays.append(arr)
    n_in = len(arrays)

    def body(*refs):
        for t, (a, b, rows) in enumerate(jobs):
            a_ref, b_ref, o_ref = refs[index[id(a)]], refs[index[id(b)]], refs[n_in + t]
            if a.ndim == 3:
                for h in range(a.shape[0]):
                    o_ref[h] = lax.dot_general(a_ref[h], b_ref[...], TN, preferred_element_type=F32)[:rows].astype(BF16)
            else:
                o_ref[...] = lax.dot_general(a_ref[...], b_ref[...], TN, preferred_element_type=F32)[:rows].astype(BF16)

    out_shape = [jax.ShapeDtypeStruct(a.shape[:-2] + (rows, b.shape[-1]), BF16) for a, b, rows in jobs]
    return _pallas(body, name=name, in_specs=[VMEM_SPEC] * n_in, out_specs=[VMEM_SPEC] * len(jobs), out_shape=out_shape,
                   params=pltpu.CompilerParams(vmem_limit_bytes=VMEM_LIMIT))(*arrays)


def _mm_wgrad(name, a, b, *, tn=512):
    k, n = a.shape[1], b.shape[1]
    tn = min(tn, n)
    return _mm(name, a, b, grid=(n // tn,),
               a_spec=pl.BlockSpec((T, k), lambda j: (0, 0)), b_spec=pl.BlockSpec((T, tn), lambda j: (0, j)),
               o_spec=pl.BlockSpec((k, tn), lambda j: (0, j)), o_shape=(k, n), o_dtype=BF16, dims=TN)


def _rms_fwd(name, x, g):
    d = x.shape[1]

    def body(x_ref, g_ref, o_ref):
        xv = x_ref[...]
        r = lax.rsqrt(jnp.mean(xv * xv, axis=-1, keepdims=True) + EPS)
        o_ref[...] = ((xv * r) * g_ref[...]).astype(BF16)

    return _pallas(
        body, name=name, grid=(T // TM,),
        in_specs=[pl.BlockSpec((TM, d), lambda i: (i, 0)), pl.BlockSpec((1, d), lambda i: (0, 0))],
        out_specs=pl.BlockSpec((TM, d), lambda i: (i, 0)),
        out_shape=jax.ShapeDtypeStruct((T, d), BF16), params=_params(("parallel",)))(x, g)


def _rms(xv, g):
    return (xv * lax.rsqrt(jnp.mean(xv * xv, axis=-1, keepdims=True) + EPS)) * g


def _out_norm(name, a, w, add, g):
    def body(a_ref, w_ref, add_ref, g_ref, h_ref, hn_ref):
        hv = lax.dot_general(a_ref[...], w_ref[...], NN, preferred_element_type=F32) + add_ref[...]
        h_ref[...] = hv
        hn_ref[...] = _rms(hv, g_ref[...]).astype(BF16)

    rows = pl.BlockSpec((TS, D), lambda i: (i, 0))
    return _pallas(
        body, name=name, grid=(T // TS,),
        in_specs=[pl.BlockSpec((TS, a.shape[1]), lambda i: (i, 0)), pl.BlockSpec(w.shape, lambda i: (0, 0)), rows,
                  pl.BlockSpec((1, D), lambda i: (0, 0))],
        out_specs=[rows, rows], out_shape=[jax.ShapeDtypeStruct((T, D), F32), jax.ShapeDtypeStruct((T, D), BF16)],
        params=_params(("parallel",)))(a, w, add, g)


def _down_final(act, w_down4, h_in, g, tgt):
    def body(a_ref, w_ref, hin_ref, g_ref, t_ref, loss_ref, dh_ref, dhb_ref, dg_ref):
        hv = lax.dot_general(a_ref[0], w_ref[0], NN, preferred_element_type=F32)
        for j in range(1, N_FF_BLK):
            hv = hv + lax.dot_general(a_ref[j], w_ref[j], NN, preferred_element_type=F32)
        hv = hv + hin_ref[...]
        r = lax.rsqrt(jnp.mean(hv * hv, axis=-1, keepdims=True) + EPS)
        xn = hv * r
        gv = g_ref[...]
        err = xn * gv - t_ref[...]
        part_loss = 0.5 * jnp.sum(jnp.mean(err * err, axis=-1, keepdims=True), axis=0, keepdims=True)
        dy = err * (1.0 / D)
        gdy = dy * gv
        dh = r * (gdy - xn * jnp.mean(gdy * xn, axis=-1, keepdims=True))
        dh_ref[...] = dh
        dhb_ref[...] = dh.astype(BF16)
        part = jnp.sum(dy * xn, axis=0, keepdims=True)
        first = pl.program_id(0) == 0

        @pl.when(first)
        def _():
            dg_ref[...] = part
            loss_ref[...] = jnp.broadcast_to(part_loss, (1, 128))

        @pl.when(jnp.logical_not(first))
        def _():
            dg_ref[...] += part
            loss_ref[...] += jnp.broadcast_to(part_loss, (1, 128))

    row = pl.BlockSpec((TS, D), lambda i: (i, 0))
    vec = pl.BlockSpec((1, D), lambda i: (0, 0))
    return _pallas(
        body, name="f1_down_loss", grid=(T // TS,),
        in_specs=[pl.BlockSpec((N_FF_BLK, TS, FF_BLK), lambda i: (0, i, 0)),
                  pl.BlockSpec((None, N_FF_BLK, FF_BLK, D), lambda i: (0, 0, 0, 0)), row, vec, row],
        out_specs=[pl.BlockSpec((1, 128), lambda i: (0, 0)), row, row, vec],
        out_shape=[jax.ShapeDtypeStruct((1, 128), F32), jax.ShapeDtypeStruct((T, D), F32),
                   jax.ShapeDtypeStruct((T, D), BF16), jax.ShapeDtypeStruct((1, D), F32)],
        params=_params(("arbitrary",)))(act, w_down4, h_in, g, tgt)


def _prev_idx(i, rows=TR):
    return jnp.maximum(i * (rows // HALO) - 1, 0)


def _next_idx(i, rows=TR):
    return jnp.minimum((i + 1) * (rows // HALO), T // HALO - 1)


def _causal_taps(ext):
    return pltpu.roll(ext, 2, 0)[HALO:], pltpu.roll(ext, 1, 0)[HALO:], ext[HALO:]


def _anticausal_taps(ext, n):
    rows = ext.shape[0]
    return pltpu.roll(ext, rows - 1, 0)[:n], pltpu.roll(ext, rows - 2, 0)[:n]


MIX_COLS = 512


def _mixer_in(hn, w_in, w):
    nc = D // MIX_COLS

    def body(h_ref, hh_ref, wb_ref, wc_ref, wu_ref, w_ref, b_ref, c_ref, u_ref, y_ref):
        i = pl.program_id(1)
        hv = h_ref[...]
        he = jnp.concatenate([hh_ref[...], hv], axis=0)
        ce = lax.dot_general(he, wc_ref[...], NN, preferred_element_type=F32).astype(BF16)
        ue = lax.dot_general(he, wu_ref[...], NN, preferred_element_type=F32).astype(BF16)
        bv = lax.dot_general(hv, wb_ref[...], NN, preferred_element_type=F32).astype(BF16)
        b_ref[...] = bv
        c_ref[...] = ce[HALO:]
        u_ref[...] = ue[HALO:]
        row = lax.broadcasted_iota(jnp.int32, (HALO + TS, 1), 0)
        cu = jnp.where(jnp.logical_or(i > 0, row >= HALO), ce.astype(F32) * ue.astype(F32), 0.0)
        x2, x1, x0 = _causal_taps(cu)
        wv = w_ref[...]
        cv = (x2 * wv[0:1] + x1 * wv[1:2]) + x0 * wv[2:3]
        y_ref[...] = (bv.astype(F32) * cv).astype(BF16)

    def cols(part):
        return pl.BlockSpec((D, MIX_COLS), lambda j, i: (0, part * nc + j))

    blk = pl.BlockSpec((TS, MIX_COLS), lambda j, i: (i, j))
    out = jax.ShapeDtypeStruct((T, D), BF16)
    return _pallas(
        body, name="l0_in", grid=(nc, T // TS),
        in_specs=[pl.BlockSpec((TS, D), lambda j, i: (i, 0)), pl.BlockSpec((HALO, D), lambda j, i: (_prev_idx(i, TS), 0)),
                  cols(0), cols(1), cols(2), pl.BlockSpec((3, MIX_COLS), lambda j, i: (0, j))],
        out_specs=[blk] * 4, out_shape=[out] * 4,
        params=_params(("parallel", "parallel")))(hn, hn, w_in, w_in, w_in, w)


def _mixer_out_bwd(dh, w_out, zb, zc, zu, w):
    last = T // TR - 1

    def body(dh_ref, dhn_ref, wo_ref, b_ref, bn_ref, c_ref, ch_ref, u_ref, uh_ref, w_ref, dz_ref, dw_ref):
        i = pl.program_id(0)
        dye = lax.dot_general(jnp.concatenate([dh_ref[...], dhn_ref[...]], axis=0), wo_ref[...], NT,
                              preferred_element_type=F32)
        cv_ = c_ref[...].astype(F32)
        uv = u_ref[...].astype(F32)
        cu = cv_ * uv
        cuh = jnp.where(i > 0, ch_ref[...].astype(F32) * uh_ref[...].astype(F32), 0.0)
        x2, x1, x0 = _causal_taps(jnp.concatenate([cuh, cu], axis=0))
        wv = w_ref[...]
        conv = (x2 * wv[0:1] + x1 * wv[1:2]) + x0 * wv[2:3]
        dyv = dye[:TR]
        dz_ref[:, 0:D] = (dyv * conv).astype(BF16)
        dconv = dyv * b_ref[...].astype(F32)
        dconv_n = jnp.where(i < last, dye[TR:] * bn_ref[...].astype(F32), 0.0)
        n1, n2 = _anticausal_taps(jnp.concatenate([dconv, dconv_n], axis=0), TR)
        dcu = (dconv * wv[2:3] + n1 * wv[1:2]) + n2 * wv[0:1]
        dz_ref[:, D:2 * D] = (dcu * uv).astype(BF16)
        dz_ref[:, 2 * D:3 * D] = (dcu * cv_).astype(BF16)
        part = jnp.concatenate([jnp.sum(dconv * x2, axis=0, keepdims=True),
                                jnp.sum(dconv * x1, axis=0, keepdims=True),
                                jnp.sum(dconv * x0, axis=0, keepdims=True)], axis=0)

        @pl.when(i == 0)
        def _():
            dw_ref[...] = part

        @pl.when(i > 0)
        def _():
            dw_ref[...] += part

    main = pl.BlockSpec((TR, D), lambda i: (i, 0))
    prev = pl.BlockSpec((HALO, D), lambda i: (_prev_idx(i), 0))
    nxt = pl.BlockSpec((HALO, D), lambda i: (_next_idx(i), 0))
    wspec = pl.BlockSpec((3, D), lambda i: (0, 0))
    return _pallas(
        body, name="d_l0_out", grid=(T // TR,),
        in_specs=[main, nxt, pl.BlockSpec((D, D), lambda i: (0, 0)), main, nxt, main, prev, main, prev, wspec],
        out_specs=[pl.BlockSpec((TR, 3 * D), lambda i: (i, 0)), wspec],
        out_shape=[jax.ShapeDtypeStruct((T, 3 * D), BF16), jax.ShapeDtypeStruct((3, D), F32)],
        params=_params(("arbitrary",)))(dh, dh, w_out, zb, zb, zc, zc, zu, zu, w)


def _sigmoid(x):
    return 0.5 * jnp.tanh(0.5 * x) + 0.5


def _ffn_up_act(name, hf, w_up, w, b):
    def body(h_ref, hh_ref, wg_ref, wv_ref, w_ref, b_ref, g_ref, v_ref, a_ref):
        i = pl.program_id(1)
        hv = h_ref[...]
        ge = lax.dot_general(jnp.concatenate([hh_ref[...], hv], axis=0), wg_ref[...], NT,
                             preferred_element_type=F32).astype(BF16)
        v = lax.dot_general(hv, wv_ref[...], NT, preferred_element_type=F32).astype(BF16)
        g_ref[...] = ge[HALO:]
        v_ref[...] = v
        ext = ge.astype(F32)
        row = lax.broadcasted_iota(jnp.int32, (HALO + TM, 1), 0)
        ext = jnp.where(jnp.logical_or(i > 0, row >= HALO), ext, 0.0)
        x2, x1, x0 = _causal_taps(ext)
        wv = w_ref[...]
        gc = ((x2 * wv[0:1] + x1 * wv[1:2]) + x0 * wv[2:3]) + b_ref[...]
        a_ref[...] = ((gc * _sigmoid(gc)) * v.astype(F32)).astype(BF16)

    blk = pl.BlockSpec((None, TM, FF_BLK), lambda j, i: (j, i, 0))
    out = jax.ShapeDtypeStruct((N_FF_BLK, T, FF_BLK), BF16)
    return _pallas(
        body, name=name, grid=(N_FF_BLK, T // TM),
        in_specs=[pl.BlockSpec((TM, D), lambda j, i: (i, 0)),
                  pl.BlockSpec((HALO, D), lambda j, i: (_prev_idx(i, TM), 0)),
                  pl.BlockSpec((None, None, FF_BLK, D), lambda j, i: (0, j, 0, 0)),
                  pl.BlockSpec((None, None, FF_BLK, D), lambda j, i: (0, j + N_FF_BLK, 0, 0)),
                  pl.BlockSpec((None, 3, FF_BLK), lambda j, i: (j, 0, 0)),
                  pl.BlockSpec((None, 1, FF_BLK), lambda j, i: (j, 0, 0))],
        out_specs=[blk, blk, blk], out_shape=[out, out, out],
        params=_params(("parallel", "parallel")))(hf, hf, w_up, w_up, w, b)


def _ffn_dact(name, dh, w_down4, g, v, w, b):
    last = T // TS - 1

    def body(dh_ref, dhn_ref, wd_ref, g_ref, gp_ref, gn_ref, v_ref, vn_ref, w_ref, b_ref, dg_ref, dv_ref, dw_ref, db_ref):
        i = pl.program_id(1)
        da = lax.dot_general(jnp.concatenate([dh_ref[...], dhn_ref[...]], axis=0), wd_ref[...], NT,
                             preferred_element_type=F32)
        row = lax.broadcasted_iota(jnp.int32, (TS + HALO, 1), 0)
        da = jnp.where(jnp.logical_or(i < last, row < TS), da, 0.0)
        gp = jnp.where(i > 0, gp_ref[...].astype(F32), 0.0)
        ext = jnp.concatenate([gp, g_ref[...].astype(F32), gn_ref[...].astype(F32)], axis=0)
        x2, x1, x0 = _causal_taps(ext)
        wv = w_ref[...]
        gc = ((x2 * wv[0:1] + x1 * wv[1:2]) + x0 * wv[2:3]) + b_ref[...]
        sg = _sigmoid(gc)
        vv = jnp.concatenate([v_ref[...].astype(F32), vn_ref[...].astype(F32)], axis=0)
        silu = gc * sg
        dv_ref[...] = (da[:TS] * silu[:TS]).astype(BF16)
        dgc = (da * vv) * (sg + silu * (1.0 - sg))
        n1, n2 = _anticausal_taps(dgc, TS)
        d0 = dgc[:TS]
        dg_ref[...] = ((d0 * wv[2:3] + n1 * wv[1:2]) + n2 * wv[0:1]).astype(BF16)
        part_w = jnp.concatenate([jnp.sum(d0 * x2[:TS], axis=0, keepdims=True),
                                  jnp.sum(d0 * x1[:TS], axis=0, keepdims=True),
                                  jnp.sum(d0 * x0[:TS], axis=0, keepdims=True)], axis=0)
        part_b = jnp.sum(d0, axis=0, keepdims=True)

        @pl.when(i == 0)
        def _():
            dw_ref[...] = part_w
            db_ref[...] = part_b

        @pl.when(i > 0)
        def _():
            dw_ref[...] += part_w
            db_ref[...] += part_b

    blk = pl.BlockSpec((None, TS, FF_BLK), lambda j, i: (j, i, 0))
    prev = pl.BlockSpec((None, HALO, FF_BLK), lambda j, i: (j, _prev_idx(i, TS), 0))
    nxt = pl.BlockSpec((None, HALO, FF_BLK), lambda j, i: (j, _next_idx(i, TS), 0))
    wspec = pl.BlockSpec((None, 3, FF_BLK), lambda j, i: (j, 0, 0))
    bspec = pl.BlockSpec((None, 1, FF_BLK), lambda j, i: (j, 0, 0))
    return _pallas(
        body, name=name, grid=(N_FF_BLK, T // TS),
        in_specs=[pl.BlockSpec((TS, D), lambda j, i: (i, 0)),
                  pl.BlockSpec((HALO, D), lambda j, i: (_next_idx(i, TS), 0)),
                  pl.BlockSpec((None, None, FF_BLK, D), lambda j, i: (0, j, 0, 0)),
                  blk, prev, nxt, blk, nxt, wspec, bspec],
        out_specs=[blk, blk, wspec, bspec],
        out_shape=[jax.ShapeDtypeStruct((N_FF_BLK, T, FF_BLK), BF16), jax.ShapeDtypeStruct((N_FF_BLK, T, FF_BLK), BF16),
                   jax.ShapeDtypeStruct((N_FF_BLK, 3, FF_BLK), F32), jax.ShapeDtypeStruct((N_FF_BLK, 1, FF_BLK), F32)],
        params=_params(("parallel", "arbitrary")))(dh, dh, w_down4, g, g, g, v, v, w, b)


def _rope_tables(pos, inv_freq):
    half = QK_ROPE // 2

    def body(p_ref, f_ref, c_ref, sa_ref, sb_ref):
        ang = p_ref[...].astype(F32) * f_ref[...]
        lane = lax.broadcasted_iota(jnp.int32, (T, 128), 1)
        c = jnp.cos(ang)
        s = jnp.sin(ang)
        c_ref[...] = jnp.where(lane < 2 * half, c, 0.0)
        sa_ref[...] = jnp.where(lane < half, -s, 0.0)
        sb_ref[...] = jnp.where(jnp.logical_and(lane >= half, lane < 2 * half), s, 0.0)

    return _pallas(
        body, name="rope_tables", in_specs=[VMEM_SPEC] * 2, out_specs=[VMEM_SPEC] * 3,
        out_shape=[jax.ShapeDtypeStruct((T, 128), F32)] * 3,
        params=pltpu.CompilerParams(vmem_limit_bytes=VMEM_LIMIT))(pos, inv_freq)


def _rotate(r, c, sa, sb, sign):
    return r * c + sign * (pltpu.roll(r, 96, 1) * sa + pltpu.roll(r, 32, 1) * sb)


def _attn_pre(h2, g_kv, g_l1, g_kvl, g_ql, w_dkv, w_kr, w_uk, w_uv, w_dq, w_uq, tables):
    def body(h_ref, c_ref, sa_ref, sb_ref, gkv_ref, gl1_ref, gkvl_ref, gql_ref, wdkv_ref, wkr_ref, wuk_ref, wuv_ref,
             wdq_ref, wuq_ref, hk_ref, hn_ref, ckvr_ref, ckv_ref, kr_ref, kn_ref, v_ref, cqr_ref, cq_ref, q_ref):
        xv = h_ref[...]
        xn = xv * lax.rsqrt(jnp.mean(xv * xv, axis=-1, keepdims=True) + EPS)
        hk = (xn * gkv_ref[...]).astype(BF16)
        hn = (xn * gl1_ref[...]).astype(BF16)
        hk_ref[...] = hk
        hn_ref[...] = hn
        cv, sav, sbv = c_ref[...], sa_ref[...], sb_ref[...]
        raw = lax.dot_general(hk, wdkv_ref[...], NN, preferred_element_type=F32)
        ckvr_ref[...] = raw
        ckv = _rms(raw, gkvl_ref[...]).astype(BF16)
        ckv_ref[...] = ckv
        kr = lax.dot_general(hk, wkr_ref[...], NT, preferred_element_type=F32)
        kr_ref[...] = _rotate(kr, cv, sav, sbv, 1.0).astype(BF16)
        kn_ref[...] = lax.dot_general(ckv, wuk_ref[...], NN, preferred_element_type=F32).astype(BF16)
        v_ref[...] = lax.dot_general(ckv, wuv_ref[...], NN, preferred_element_type=F32).astype(BF16)
        cqr = lax.dot_general(hn, wdq_ref[...], NN, preferred_element_type=F32)
        cqr_ref[...] = cqr
        cq = _rms(cqr, gql_ref[...]).astype(BF16)
        cq_ref[...] = cq
        for h in range(N_HEADS):
            r = lax.dot_general(cq, wuq_ref[h], NT, preferred_element_type=F32)
            q_ref[h, :, :QK_NOPE] = (r[:, :QK_NOPE] * SCALE2).astype(BF16)
            q_ref[h, :, QK_NOPE:] = (_rotate(r[:, QK_NOPE:], cv, sav, sbv, 1.0) * SCALE2).astype(BF16)

    def rows(d):
        return pl.BlockSpec((TS, d), lambda i: (i, 0))

    def whole(a):
        return pl.BlockSpec(a.shape, lambda i: (0,) * a.ndim)

    wholes = [g_kv, g_l1, g_kvl, g_ql, w_dkv, w_kr, w_uk, w_uv, w_dq, w_uq]
    outs = [(D, BF16), (D, BF16), (KV_LORA, F32), (KV_LORA, BF16), (128, BF16), (N_HEADS * QK_NOPE, BF16),
            (N_HEADS * V_HEAD, BF16), (Q_LORA, F32), (Q_LORA, BF16)]
    return _pallas(
        body, name="attn_pre", grid=(T // TS,),
        in_specs=[rows(D), rows(128), rows(128), rows(128)] + [whole(a) for a in wholes],
        out_specs=[rows(d) for d, _ in outs] + [pl.BlockSpec((N_HEADS, TS, QK_PAD), lambda i: (0, i, 0))],
        out_shape=[jax.ShapeDtypeStruct((T, d), dt) for d, dt in outs]
        + [jax.ShapeDtypeStruct((N_HEADS, T, QK_PAD), BF16)],
        params=_params(("parallel",)))(h2, *tables, *wholes)


SCALE = (QK_NOPE + QK_ROPE) ** -0.5
LOG2E = 1.4426950408889634
SCALE2 = SCALE * LOG2E


def _diag_mask(transposed):
    shift = CHUNK.bit_length() - 1
    a = lax.broadcasted_iota(jnp.int32, (TQ, TQ), 0) >> shift
    b = lax.broadcasted_iota(jnp.int32, (TQ, TQ), 1) >> shift
    return (a <= b) if transposed else (b <= a)


def _as_row(col):
    return jnp.transpose(jnp.broadcast_to(col, (col.shape[0], 128)), (1, 0))[0:1]


def _attn_fwd(q, kn, kr, v):
    hp = 2

    def body(q_ref, kn_ref, kr_ref, v_ref, o_ref, lse_ref):
        i = pl.program_id(1)
        qs = [q_ref[a] for a in range(hp)]

        def step(j, carry, masked):
            off = pl.multiple_of(j * TQ, TQ)
            krv = kr_ref[pl.ds(off, TQ), :]
            ss = []
            for a in range(hp):
                kk = jnp.concatenate([kn_ref[pl.ds(off, TQ), a * QK_NOPE:(a + 1) * QK_NOPE], krv], axis=1)
                ss.append(lax.dot_general(qs[a], kk, NT, preferred_element_type=F32))
            out = []
            for a in range(hp):
                m, l, acc = carry[a]
                s = ss[a]
                if masked:
                    s = jnp.where(_diag_mask(False), s, NEG_INF)
                m_new = jnp.maximum(m, jnp.max(s, axis=-1, keepdims=True))
                p = jnp.exp2(s - m_new)
                alpha = jnp.exp2(m - m_new)
                l = alpha * l + jnp.sum(p, axis=-1, keepdims=True)
                pv = lax.dot_general(p.astype(BF16), v_ref[pl.ds(off, TQ), a * V_HEAD:(a + 1) * V_HEAD], NN,
                                     preferred_element_type=F32)
                out.append((m_new, l, alpha * acc + pv))
            return tuple(out)

        one = (jnp.full((TQ, 1), NEG_INF, F32), jnp.zeros((TQ, 1), F32), jnp.zeros((TQ, V_HEAD), F32))
        carry = lax.fori_loop(0, i, lambda j, cr: step(j, cr, False), (one,) * hp)
        carry = step(i, carry, True)
        for a, (m, l, acc) in enumerate(carry):
            o_ref[:, a * V_HEAD:(a + 1) * V_HEAD] = (acc / l).astype(BF16)
            lse_ref[a] = _as_row(m + jnp.log(l) * LOG2E)

    return _pallas(
        body, name="attn_fwd", grid=(N_HEADS // hp, T // TQ),
        in_specs=[pl.BlockSpec((hp, TQ, QK_PAD), lambda h, i: (h, i, 0)),
                  pl.BlockSpec((T, hp * QK_NOPE), lambda h, i: (0, h)),
                  pl.BlockSpec((T, 128), lambda h, i: (0, 0)),
                  pl.BlockSpec((T, hp * V_HEAD), lambda h, i: (0, h))],
        out_specs=[pl.BlockSpec((TQ, hp * V_HEAD), lambda h, i: (i, h)), pl.BlockSpec((hp, 1, TQ), lambda h, i: (h, 0, i))],
        out_shape=[jax.ShapeDtypeStruct((T, N_HEADS * V_HEAD), BF16), jax.ShapeDtypeStruct((N_HEADS, 1, T), F32)],
        params=_params(("parallel", "parallel")))(q, kn, kr, v)


def _attn_bwd(q, kn, kr, v, o, do, lse_row, tables):
    nq = T // TQ
    hp = 2
    cos, sa, sb = tables

    def body(q_ref, kn_ref, kr_ref, v_ref, o_ref, do_ref, lse_ref, c_ref, sa_ref, sb_ref,
             dq_ref, dkn_ref, dkr_ref, dv_ref, dq_acc, dl_ref):
        j = pl.program_id(1)

        def cols(a):
            return slice(a * 128, (a + 1) * 128)

        @pl.when(j == 0)
        def _():
            dq_acc[...] = jnp.zeros_like(dq_acc)
            for a in range(hp):
                for i in range(nq):
                    rows = pl.ds(i * TQ, TQ)
                    prod = do_ref[rows, cols(a)].astype(F32) * o_ref[rows, cols(a)].astype(F32)
                    dl_ref[a, :, rows] = _as_row(jnp.sum(prod, axis=-1, keepdims=True))

        krv = kr_ref[...]
        kks = [jnp.concatenate([kn_ref[:, cols(a)], krv], axis=1) for a in range(hp)]
        vvs = [v_ref[:, cols(a)] for a in range(hp)]

        def step(i, carry, masked):
            off = pl.multiple_of(i * TQ, TQ)
            rows = pl.ds(off, TQ)
            qis = [q_ref[a, rows, :] for a in range(hp)]
            dois = [do_ref[rows, cols(a)] for a in range(hp)]
            sts = [lax.dot_general(kks[a], qis[a], NT, preferred_element_type=F32) for a in range(hp)]
            dpts = [lax.dot_general(vvs[a], dois[a], NT, preferred_element_type=F32) for a in range(hp)]
            out = []
            for a in range(hp):
                dk, dv = carry[a]
                st = sts[a]
                if masked:
                    st = jnp.where(_diag_mask(True), st, NEG_INF)
                pt = jnp.exp2(st - lse_ref[a, :, rows])
                dv = dv + lax.dot_general(pt.astype(BF16), dois[a], NN, preferred_element_type=F32)
                dst = (pt * (dpts[a] - dl_ref[a, :, rows])).astype(BF16)
                dk = dk + lax.dot_general(dst, qis[a], NN, preferred_element_type=F32)
                dq_acc[a, rows, :] += lax.dot_general(dst, kks[a], TN, preferred_element_type=F32)
                out.append((dk, dv))
            return tuple(out)

        zero = (jnp.zeros((TQ, QK_PAD), F32), jnp.zeros((TQ, V_HEAD), F32))
        carry = step(j, (zero,) * hp, True)
        carry = lax.fori_loop(j + 1, nq, lambda i, cr: step(i, cr, False), carry)
        for a, (dk, dv) in enumerate(carry):
            dk = dk * (SCALE / SCALE2)
            dkn_ref[:, cols(a)] = dk[:, :QK_NOPE].astype(BF16)
            dkr_ref[a] = dk[:, QK_NOPE:]
            dv_ref[:, cols(a)] = dv.astype(BF16)

        @pl.when(j == nq - 1)
        def _():
            for a in range(hp):
                dq = dq_acc[a] * SCALE
                dq_ref[a, :, :QK_NOPE] = dq[:, :QK_NOPE].astype(BF16)
                dq_ref[a, :, QK_NOPE:] = _rotate(dq[:, QK_NOPE:], c_ref[...], sa_ref[...], sb_ref[...], -1.0).astype(BF16)

    row = pl.BlockSpec((hp, 1, T), lambda h, j: (h, 0, 0))
    head = pl.BlockSpec((TQ, hp * 128), lambda h, j: (j, h))
    whole = pl.BlockSpec((hp, T, QK_PAD), lambda h, j: (h, 0, 0))
    tab = pl.BlockSpec((T, 128), lambda h, j: (0, 0))
    heads = pl.BlockSpec((T, hp * V_HEAD), lambda h, j: (0, h))
    return _pallas(
        body, name="attn_bwd", grid=(N_HEADS // hp, nq),
        in_specs=[whole, head, pl.BlockSpec((TQ, 128), lambda h, j: (j, 0)), head, heads, heads, row, tab, tab, tab],
        out_specs=[whole, head, pl.BlockSpec((hp, TQ, 128), lambda h, j: (h, j, 0)), head],
        out_shape=[jax.ShapeDtypeStruct((N_HEADS, T, QK_PAD), BF16), jax.ShapeDtypeStruct((T, N_HEADS * QK_NOPE), BF16),
                   jax.ShapeDtypeStruct((N_HEADS, T, 128), F32), jax.ShapeDtypeStruct((T, N_HEADS * V_HEAD), BF16)],
        scratch_shapes=[pltpu.VMEM((hp, T, QK_PAD), F32), pltpu.VMEM((hp, 1, T), F32)],
        params=_params(("parallel", "arbitrary")))(q, kn, kr, v, o, do, lse_row, cos, sa, sb)


def _rms_bwd_math(xv, g, dy):
    r = lax.rsqrt(jnp.mean(xv * xv, axis=-1, keepdims=True) + EPS)
    xn = xv * r
    gdy = dy * g
    return r * (gdy - xn * jnp.mean(gdy * xn, axis=-1, keepdims=True)), jnp.sum(dy * xn, axis=0, keepdims=True)


def _attn_post(dq, dkn, dv, dkr, cq_raw, ckv_raw, h2, dres, g_ql, g_kvl, g_l1, g_kv, w_uq, w_uk, w_uv, w_dq, w_dkv,
               w_kr, tables):
    def body(dq_ref, dkn_ref, dv_ref, dkr_ref, cqr_ref, ckvr_ref, h_ref, res_ref, c_ref, sa_ref, sb_ref,
             gql_ref, gkvl_ref, gl1_ref, gkv_ref, wuq_ref, wuk_ref, wuv_ref, wdq_ref, wdkv_ref, wkr_ref,
             dcq_ref, dckv_ref, dkrr_ref, dh_ref, dhb_ref, dgql_ref, dgkvl_ref, dgl1_ref, dgkv_ref):
        dcq = lax.dot_general(dq_ref[0], wuq_ref[0], NN, preferred_element_type=F32)
        for h in range(1, N_HEADS):
            dcq = dcq + lax.dot_general(dq_ref[h], wuq_ref[h], NN, preferred_element_type=F32)
        dcq_raw, s_ql = _rms_bwd_math(cqr_ref[...], gql_ref[...], dcq)
        dcq_raw = dcq_raw.astype(BF16)
        dcq_ref[...] = dcq_raw
        dckv = (lax.dot_general(dkn_ref[...], wuk_ref[...], NT, preferred_element_type=F32)
                + lax.dot_general(dv_ref[...], wuv_ref[...], NT, preferred_element_type=F32))
        dckv_raw, s_kvl = _rms_bwd_math(ckvr_ref[...], gkvl_ref[...], dckv)
        dckv_raw = dckv_raw.astype(BF16)
        dckv_ref[...] = dckv_raw
        dkr = dkr_ref[0]
        for h in range(1, N_HEADS):
            dkr = dkr + dkr_ref[h]
        dkr_raw = _rotate(dkr, c_ref[...], sa_ref[...], sb_ref[...], -1.0).astype(BF16)
        dkrr_ref[...] = dkr_raw
        d_hn = lax.dot_general(dcq_raw, wdq_ref[...], NT, preferred_element_type=F32)
        d_hk = (lax.dot_general(dckv_raw, wdkv_ref[...], NT, preferred_element_type=F32)
                + lax.dot_general(dkr_raw, wkr_ref[...], NN, preferred_element_type=F32))
        xv = h_ref[...]
        r = lax.rsqrt(jnp.mean(xv * xv, axis=-1, keepdims=True) + EPS)
        xn = xv * r
        dx = res_ref[...]
        sums = [s_ql, s_kvl]
        for dy, g_ref in ((d_hn, gl1_ref), (d_hk, gkv_ref)):
            gdy = dy * g_ref[...]
            dx = dx + r * (gdy - xn * jnp.mean(gdy * xn, axis=-1, keepdims=True))
            sums.append(jnp.sum(dy * xn, axis=0, keepdims=True))
        dh_ref[...] = dx
        dhb_ref[...] = dx.astype(BF16)
        dg_refs = (dgql_ref, dgkvl_ref, dgl1_ref, dgkv_ref)

        @pl.when(pl.program_id(0) == 0)
        def _():
            for dg_ref, part in zip(dg_refs, sums):
                dg_ref[...] = part

        @pl.when(pl.program_id(0) > 0)
        def _():
            for dg_ref, part in zip(dg_refs, sums):
                dg_ref[...] += part

    def rows(d):
        return pl.BlockSpec((TS, d), lambda i: (i, 0))

    def heads(d):
        return pl.BlockSpec((N_HEADS, TS, d), lambda i: (0, i, 0))

    def whole(a):
        return pl.BlockSpec(a.shape, lambda i: (0,) * a.ndim)

    wholes = [g_ql, g_kvl, g_l1, g_kv, w_uq, w_uk, w_uv, w_dq, w_dkv, w_kr]
    vecs = [Q_LORA, KV_LORA, D, D]
    return _pallas(
        body, name="attn_post", grid=(T // TS,),
        in_specs=[heads(QK_PAD), rows(N_HEADS * QK_NOPE), rows(N_HEADS * V_HEAD), heads(128), rows(Q_LORA),
                  rows(KV_LORA), rows(D), rows(D), rows(128), rows(128), rows(128)] + [whole(a) for a in wholes],
        out_specs=[rows(Q_LORA), rows(KV_LORA), rows(128), rows(D), rows(D)]
        + [pl.BlockSpec((1, d), lambda i: (0, 0)) for d in vecs],
        out_shape=[jax.ShapeDtypeStruct((T, Q_LORA), BF16), jax.ShapeDtypeStruct((T, KV_LORA), BF16),
                   jax.ShapeDtypeStruct((T, 128), BF16), jax.ShapeDtypeStruct((T, D), F32),
                   jax.ShapeDtypeStruct((T, D), BF16)] + [jax.ShapeDtypeStruct((1, d), F32) for d in vecs],
        params=_params(("arbitrary",)))(dq, dkn, dv, dkr, cq_raw, ckv_raw, h2, dres, *tables, *wholes)


def _ffn_gup(name, dg, dv, hf):
    def body(dg_ref, dv_ref, hf_ref, o_ref):
        j = pl.program_id(0)

        @pl.when(j < N_FF_BLK)
        def _():
            o_ref[...] = lax.dot_general(dg_ref[...], hf_ref[...], TN, preferred_element_type=F32).astype(BF16)

        @pl.when(j >= N_FF_BLK)
        def _():
            o_ref[...] = lax.dot_general(dv_ref[...], hf_ref[...], TN, preferred_element_type=F32).astype(BF16)

    return _pallas(
        body, name=name, grid=(N_DEV,),
        in_specs=[pl.BlockSpec((None, T, FF_BLK), lambda j: (jnp.minimum(j, N_FF_BLK - 1), 0, 0)),
                  pl.BlockSpec((None, T, FF_BLK), lambda j: (jnp.maximum(j - N_FF_BLK, 0), 0, 0)),
                  pl.BlockSpec((T, D), lambda j: (0, 0))],
        out_specs=pl.BlockSpec((None, FF_BLK, D), lambda j: (j, 0, 0)),
        out_shape=jax.ShapeDtypeStruct((N_DEV, FF_BLK, D), BF16), params=_params(("parallel",)))(dg, dv, hf)


def _ffn_layer_fwd(tag, h, hf, ex, final=None):
    g, v, act = _ffn_up_act(f"{tag}_up", hf, ex.need(f"ffn_w_up{tag[1]}", hf), ex.need(f"ffn_cw{tag[1]}", hf),
                            ex.need(f"ffn_cb{tag[1]}", hf))
    ex.at(f"{tag}_up", act)
    if final is not None:
        return _down_final(act, ex.need(f"ffn_w_down{tag[1]}", act), h, *final), (hf, g, v, act)
    rows = pl.BlockSpec((TS, D), lambda i: (i, 0))
    out = _mm_sum(f"{tag}_down",
                  [(act, pl.BlockSpec((N_FF_BLK, TS, FF_BLK), lambda i: (0, i, 0)), ex.need(f"ffn_w_down{tag[1]}", act),
                    pl.BlockSpec((None, N_FF_BLK, FF_BLK, D), lambda i: (0, 0, 0, 0)), NN, 0)],
                  grid=(T // TS,), o_spec=rows, o_shape=(T, D), o_dtype=F32, add=h)
    ex.at(f"{tag}_down", out)
    return out, (hf, g, v, act)


def _ffn_layer_bwd(tag, h, gain, ex, saved, dh, dh_bf, post=None):
    hf, g, v, act = saved
    layer = tag[1]
    w_up, w_down4 = ex.need(f"ffn_w_up{layer}", dh_bf), ex.need(f"ffn_w_down{layer}", dh_bf)
    dg, dv, dcw, dcb = _ffn_dact(f"{tag}_dact", dh_bf, w_down4, g, v, ex.need(f"ffn_cw{layer}", dh_bf),
                                 ex.need(f"ffn_cb{layer}", dh_bf))
    ex.at(f"{tag}_dact", dg)
    g_down = _mm(f"{tag}_gdown", act, dh_bf, grid=(N_FF_BLK,),
                 a_spec=pl.BlockSpec((None, T, FF_BLK), lambda j: (j, 0, 0)),
                 b_spec=pl.BlockSpec((T, D), lambda j: (0, 0)),
                 o_spec=pl.BlockSpec((FF_BLK, D), lambda j: (j, 0)),
                 o_shape=(D_FF, D), o_dtype=BF16, dims=TN)
    g_up = _ffn_gup(f"{tag}_gup", dg, dv, hf)
    ex.grad("ffn_w_up", int(layer), g_up.reshape(1, N_DEV, FF_BLK, D))
    ex.grad("ffn_w_down", int(layer), g_down.reshape(1, N_DEV, D_FF // N_DEV, D))
    ex.at(f"{tag}_gup", g_up)
    part = pl.BlockSpec((N_FF_BLK, TR, FF_BLK), lambda i: (0, i, 0))
    dh_in, dh_in_bf, dgain, *onward = _mm_sum(
        f"{tag}_dhf",
        [(dg, part, w_up, pl.BlockSpec((None, N_FF_BLK, FF_BLK, D), lambda i: (0, 0, 0, 0)), NN, 0),
         (dv, part, w_up, pl.BlockSpec((None, N_FF_BLK, FF_BLK, D), lambda i: (0, 1, 0, 0)), NN, 0)],
        grid=(T // TR,), o_spec=pl.BlockSpec((TR, D), lambda i: (i, 0)), o_shape=(T, D), o_dtype=F32,
        norm_bwd=(h, [gain], dh), post=post)
    ex.at(f"{tag}_dhf", dh_in)
    return (dh_in, dh_in_bf, dgain[0], dcw, dcb, *onward)


def _local_step(x, pos, tgt, rep, ex):
    attn_norm, ffn_norm, final_norm = rep["attn_norm"], rep["ffn_norm"], rep["final_norm"]
    half = QK_ROPE // 2
    inv = 1.0 / (ROPE_THETA ** (jnp.arange(half, dtype=F32) / half))
    inv_freq = jnp.concatenate([inv, inv, jnp.zeros((128 - 2 * half,), F32)]).reshape(1, 128)
    tables = _rope_tables(pos, inv_freq)

    hn0 = _rms_fwd("l0_norm", x, attn_norm[0:1])
    ex.at("l0_norm", hn0)
    w_in = ex.need("sc_w_in", hn0)
    zb, zc, zu, y = _mixer_in(hn0, w_in, ex.need("sc_conv_w", hn0))
    ex.at("l0_in", y)
    h1 = _mm_rows("l0_out", y, ex.need("sc_w_out", y), NN, F32, D, tn=512, add=x)
    ex.at("l0_out", h1)
    h2, ffn0 = _ffn_layer_fwd("f0", h1, _rms_fwd("f0_norm", h1, ffn_norm[0:1]), ex)

    w_uq = ex.need("w_uq", h2)
    hk, hn1, ckv_raw, ckv, kr, kn, vv, cq_raw, cq, q = _attn_pre(
        h2, rep["kv_in_norm"], attn_norm[1:2], rep["kv_latent_norm"], rep["q_latent_norm"], ex.need("w_dkv", h2),
        ex.need("w_kr", h2), ex.need("w_uk", h2), ex.need("w_uv", h2), ex.need("w_dq", h2), w_uq, tables)

    o, lse = _attn_fwd(q, kn, kr, vv)
    ex.at("attn_fwd", o)
    w_o = ex.need("w_o", o)
    h3, hf1 = _out_norm("attn_out", o, w_o, h2, ffn_norm[1:2])
    (loss, dh4, dh4_bf, d_final), ffn1 = _ffn_layer_fwd("f1", h3, hf1, ex, final=(final_norm.reshape(1, D), tgt))

    dh3, dh3_bf, d_fn1, dcw1, dcb1, do = _ffn_layer_bwd("f1", h3, ffn_norm[1:2], ex, ffn1, dh4, dh4_bf,
                                                        post=(w_o, NT, N_HEADS * V_HEAD))
    ex.at("f1_bwd", dh3)

    dq_pre, dkn, dkr, dvv = _attn_bwd(q, kn, kr, vv, o, do, lse, tables)

    dcq_raw_bf, dckv_raw_bf, dkr_raw_bf, dh2, dh2_bf, d_qln, d_kvln, d_an1, d_kvin = _attn_post(
        dq_pre, dkn, dvv, dkr, cq_raw, ckv_raw, h2, dh3, rep["q_latent_norm"], rep["kv_latent_norm"], attn_norm[1:2],
        rep["kv_in_norm"], w_uq, ex.need("w_uk", dkn), ex.need("w_uv", dvv), ex.need("w_dq", dq_pre),
        ex.need("w_dkv", dkn), ex.need("w_kr", dkr), tables)
    g_uq, g_dq, g_o = _wgrads("g_q", [(dq_pre, cq, QK_NOPE + QK_ROPE), (hn1, dcq_raw_bf), (o, dh3_bf)])
    ex.grad("w_uq", None, g_uq.reshape(1, N_DEV, QK_NOPE + QK_ROPE, Q_LORA))
    ex.grad("w_dq", None, g_dq.reshape(1, N_DEV, D // N_DEV, Q_LORA))
    ex.grad("w_o", None, g_o.reshape(1, N_DEV, D // N_DEV, D))

    g_uk, g_uv, g_dkv, g_kr = _wgrads("g_kv", [(ckv, dkn), (ckv, dvv), (hk, dckv_raw_bf), (dkr_raw_bf, hk, QK_ROPE)])
    ex.grad("w_uk", None, g_uk)
    ex.grad("w_uv", None, g_uv)
    ex.grad("w_dkv", None, g_dkv.reshape(1, N_DEV, D // N_DEV, KV_LORA))
    ex.grad("w_kr", None, g_kr)
    ex.at("kv_bwd", dh2)

    dh1, dh1_bf, d_fn0, dcw0, dcb0 = _ffn_layer_bwd("f0", h1, ffn_norm[0:1], ex, ffn0, dh2, dh2_bf)
    ex.at("f0_bwd", dh1)

    ex.grad("sc_w_out", None, _mm_wgrad("g_sc_w_out", y, dh1_bf).reshape(1, N_DEV, D // N_DEV, D))
    dz, d_scw = _mixer_out_bwd(dh1_bf, ex.need("sc_w_out", dh1_bf), zb, zc, zu, ex.need("sc_conv_w", dh1_bf))
    g_in = _mm_wgrad("g_sc_w_in", hn0, dz)
    ex.grad("sc_w_in", None, g_in)
    ex.at("sc_bwd", g_in)
    ex.at("d_l0_in", g_in)
    w_in = ex.need("sc_w_in", dz)
    grad_x, _, (d_an0,) = _mm_sum(
        "d_l0_in", [(dz[None], pl.BlockSpec((1, TS, dz.shape[1]), lambda i: (0, i, 0)),
                     w_in[None], pl.BlockSpec((1,) + w_in.shape, lambda i: (0, 0, 0)), NT, 0)],
        norm_bwd=(x, [attn_norm[0:1]], dh1),
        grid=(T // TS,), o_spec=pl.BlockSpec((TS, D), lambda i: (i, 0)), o_shape=(T, D), o_dtype=F32)

    small = {
        "attn_norm": jnp.concatenate([d_an0, d_an1], axis=0),
        "ffn_norm": jnp.concatenate([d_fn0, d_fn1], axis=0),
        "final_norm": d_final.reshape(D),
        "kv_in_norm": d_kvin.reshape(D),
        "kv_latent_norm": d_kvln.reshape(KV_LORA),
        "q_latent_norm": d_qln,
        "ffn_conv_b": jnp.stack([dcb0, dcb1]).transpose(0, 2, 1, 3).reshape(2, D_FF),
        "sc_conv_w": d_scw,
        "ffn_conv_w": jnp.stack([dcw0, dcw1]).transpose(0, 2, 1, 3).reshape(2, 3, D_FF),
    }
    return loss, grad_x, small


def _place():
    return lax.axis_index("x"), lax.axis_index("y"), lax.axis_index("c")


def _peers():
    x, y, c = _place()
    return (x, y, 1 - c), [(1 - x, y), (x, 1 - y), (1 - x, 1 - y)]


def _window(ref, kind, dev):
    if kind == "blocked":
        return ref.at[:, dev]
    width = ref.shape[-1] // N_DEV
    return ref.at[:, pl.ds(pl.multiple_of(dev * width, 128), width)]


HBM_SPEC = pl.BlockSpec(memory_space=pltpu.HBM)
SEM_SPEC = pl.BlockSpec(memory_space=pltpu.SEMAPHORE)
EFFECT = pltpu.SideEffectType.DATAFLOW_SIDE_EFFECTING
TOKEN = jax.ShapeDtypeStruct((8, 128), F32)


def _hbm(a):
    return pltpu.with_memory_space_constraint(a, pltpu.HBM)


def _copies_start(name, jobs):
    nj = len(jobs)
    counts = [(len(srcs), len(lands)) for srcs, lands, _, _ in jobs]
    n_arr = sum(ns + nl for ns, nl in counts)

    def body(*refs):
        sems, token = refs[n_arr:n_arr + 2 * nj], refs[-1]
        at = 0
        for j, ((ns, nl), (_, _, ncopy, plan)) in enumerate(zip(counts, jobs)):
            copies = plan(refs[at:at + ns], refs[at + ns:at + ns + nl])
            assert len(copies) == ncopy
            for k, (sent, dst, to, _) in enumerate(copies):
                pltpu.make_async_remote_copy(src_ref=sent, dst_ref=dst, send_sem=sems[2 * j].at[k],
                                             recv_sem=sems[2 * j + 1].at[k], device_id=to, device_id_type=MESH).start()
            at += ns + nl
        token[...] = jnp.zeros_like(token)

    arrays = [a for srcs, lands, _, _ in jobs for a in list(srcs) + list(lands)]
    sem_shapes = [pltpu.SemaphoreType.DMA((ncopy,)) for _, _, ncopy, _ in jobs for _ in range(2)]
    outs = pl.pallas_call(
        body, name=name, in_specs=[HBM_SPEC] * n_arr,
        out_specs=[SEM_SPEC] * (2 * nj) + [HBM_SPEC] * n_arr + [VMEM_SPEC],
        out_shape=sem_shapes + [pltpu.HBM(a.shape, a.dtype) for a in arrays] + [TOKEN],
        input_output_aliases={i: 2 * nj + i for i in range(n_arr)},
        compiler_params=pltpu.CompilerParams(has_side_effects=EFFECT))(*[_hbm(a) for a in arrays])
    _Chain.last = outs[-1]
    flights, at = [], 2 * nj
    for j, (ns, nl) in enumerate(counts):
        flights.append((outs[2 * j], outs[2 * j + 1], list(outs[at:at + ns]), list(outs[at + ns:at + ns + nl])))
        at += ns + nl
    return flights


def _copies_wait(name, started, ncopy, plan):
    send, recv, srcs, lands = started
    ns, nl = len(srcs), len(lands)

    def body(*refs):
        send_ref, recv_ref, token = refs[ns + nl], refs[ns + nl + 1], refs[-1]
        copies = plan(refs[:ns], refs[ns:ns + nl])
        assert len(copies) == ncopy
        for k, (sent, _, to, landed) in enumerate(copies):
            cp = pltpu.make_async_remote_copy(src_ref=sent, dst_ref=landed, send_sem=send_ref.at[k],
                                              recv_sem=recv_ref.at[k], device_id=to, device_id_type=MESH)
            cp.wait_send()
            cp.wait_recv()
        token[...] = jnp.zeros_like(token)

    arrays = list(srcs) + list(lands)
    outs = pl.pallas_call(
        body, name=name, in_specs=[HBM_SPEC] * (ns + nl) + [SEM_SPEC] * 2 + [ANY_SPEC],
        out_specs=[HBM_SPEC] * (ns + nl) + [VMEM_SPEC], out_shape=[pltpu.HBM(a.shape, a.dtype) for a in arrays] + [TOKEN],
        input_output_aliases={i: i for i in range(ns + nl)},
        compiler_params=pltpu.CompilerParams(has_side_effects=EFFECT))(*arrays, send, recv, _Chain.last)
    _Chain.last = outs[-1]
    return list(outs[:ns]), list(outs[ns:-1])


def _plan_gather_chips(kinds):
    def plan(srcs, lands):
        x, y, c = _place()
        sibling, chips = _peers()
        out = []
        for t, kind in enumerate(kinds):
            mine = _window(lands[t], kind, 4 * x + 2 * y + c)
            out.append((srcs[t], mine, (x, y, c), mine))
            out.append((srcs[t], mine, sibling, _window(lands[t], kind, 4 * x + 2 * y + 1 - c)))
            for px, py in chips:
                out.append((srcs[t], mine, (px, py, c), _window(lands[t], kind, 4 * px + 2 * py + c)))
        return out
    return plan, 5 * len(kinds)


def _plan_gather_all(n):
    def plan(srcs, lands):
        x, y, c = _place()
        out = []
        for t in range(n):
            mine = lands[t].at[:, 4 * x + 2 * y + c]
            for m in range(N_DEV):
                px, py, pc = (1 - x if m & 4 else x), (1 - y if m & 2 else y), (1 - c if m & 1 else c)
                out.append((srcs[t], mine, (px, py, pc), lands[t].at[:, 4 * px + 2 * py + pc]))
        return out
    return plan, N_DEV * n


def _plan_gather_sibling(kinds):
    def plan(srcs, lands):
        _, _, c = _place()
        sibling, chips = _peers()
        out = []
        for t, kind in enumerate(kinds):
            for px, py in chips:
                w = _window(lands[t], kind, 4 * px + 2 * py + c)
                out.append((w, w, sibling, _window(lands[t], kind, 4 * px + 2 * py + 1 - c)))
        return out
    return plan, 3 * len(kinds)


def _plan_scatter_sibling(kinds):
    def plan(srcs, lands):
        _, _, c = _place()
        sibling, _ = _peers()
        out = []
        for t, kind in enumerate(kinds):
            for k in range(N_CHIP):
                out.append((_window(srcs[t], kind, 2 * k + 1 - c), lands[t].at[k], sibling, lands[t].at[k]))
        return out
    return plan, N_CHIP * len(kinds)


def _plan_scatter_chips(n):
    def plan(srcs, lands):
        x, y, c = _place()
        _, chips = _peers()
        out = []
        for t in range(n):
            for px, py in chips:
                out.append((srcs[t].at[2 * px + py], lands[t].at[2 * x + y], (px, py, c), lands[t].at[2 * px + py]))
        return out
    return plan, 3 * n


def _landing(shard, kind):
    if kind == "blocked":
        return lax.empty((shard.shape[0], N_DEV) + shard.shape[1:], shard.dtype)
    return lax.empty((shard.shape[0], N_DEV * shard.shape[1]), shard.dtype)


def _chip_sums(name, grads, kinds, recvs, c):
    n = len(grads)
    in_specs, out_specs, out_shape, args = [], [], [], []
    for gr, kind, rv in zip(grads, kinds, recvs):
        if kind == "blocked":
            rows, w = gr.shape[2], gr.shape[3]
            in_specs.append(pl.BlockSpec((None, None, rows, w), lambda k, cref: (0, 2 * k + cref[0], 0, 0)))
        else:
            rows, w = gr.shape[0], gr.shape[1] // N_DEV
            in_specs.append(pl.BlockSpec((rows, w), lambda k, cref: (0, 2 * k + cref[0])))
        blk = pl.BlockSpec((None, rows, w), lambda k, cref: (k, 0, 0))
        in_specs.append(blk)
        out_specs.append(blk)
        out_shape.append(jax.ShapeDtypeStruct((N_CHIP, rows, w), BF16))
        args += [gr, rv.reshape(N_CHIP, rows, w)]

    def body(*refs):
        for t in range(n):
            g_ref, r_ref, o_ref = refs[1 + 2 * t], refs[2 + 2 * t], refs[1 + 2 * n + t]
            o_ref[...] = (g_ref[...].astype(F32) + r_ref[...].astype(F32)).astype(BF16)

    return _pallas(body, name=name, n_prefetch=1, grid=(N_CHIP,), in_specs=in_specs, out_specs=out_specs,
                   out_shape=out_shape, params=_params(("parallel",)))(c, *args)


def _adamw_math(g, wv, mv, vv):
    m = ADAM_B1 * mv + (1.0 - ADAM_B1) * g
    v = ADAM_B2 * vv + (1.0 - ADAM_B2) * (g * g)
    m_hat = m / (1.0 - ADAM_B1 ** ADAM_STEP)
    v_hat = v / (1.0 - ADAM_B2 ** ADAM_STEP)
    delta = -ADAM_LR * (m_hat / (jnp.sqrt(v_hat) + ADAM_EPS) + ADAM_WD * wv)
    return delta, m, v


ADAM_STEPS = 2


def _adamw_group(name, items, chip_ids):
    n = len(items)
    in_specs, out_specs, out_shape, args, prevs = [], [], [], [chip_ids], []
    for own, recv, w3, m3, v3, layer, _ in items:
        nl, rows, w = w3.shape
        tr = rows // ADAM_STEPS
        assert tr % 16 == 0, (name, rows)
        in_specs += [pl.BlockSpec((None, tr, w), lambda i, ids, slot=slot: (ids[slot], i, 0)) for slot in range(4)]
        slab = pl.BlockSpec((None, tr, w), lambda i, ids, layer=layer: (layer, i, 0))
        in_specs += [slab] * 3
        out_specs += [slab] * 4
        out_shape += [jax.ShapeDtypeStruct((nl, rows, w), F32)] * 4
        args += [own, recv, recv, recv, w3, m3, v3]
    aliases = {}
    for t, item in enumerate(items):
        if item[6] is not None:
            for k in range(4):
                aliases[len(args) + k] = 4 * t + k
            in_specs += [ANY_SPEC] * 4
            args += list(item[6])
            prevs.append(t)
    n_in = 1 + 7 * n + 4 * len(prevs)

    def body(*refs):
        for t in range(n):
            own_ref, r1_ref, r2_ref, r3_ref, w_ref, m_ref, v_ref = refs[1 + 7 * t:8 + 7 * t]
            g_ref, d_ref, nm_ref, nv_ref = refs[n_in + 4 * t:n_in + 4 * t + 4]
            g = ((own_ref[...].astype(F32) + r1_ref[...].astype(F32)) + r2_ref[...].astype(F32)) + r3_ref[...].astype(F32)
            g_ref[...] = g
            d_ref[...], nm_ref[...], nv_ref[...] = _adamw_math(g, w_ref[...], m_ref[...], v_ref[...])

    outs = _pallas(body, name=name, n_prefetch=1, grid=(ADAM_STEPS,), in_specs=in_specs, out_specs=out_specs,
                   out_shape=out_shape, aliases=aliases, params=_params(("parallel",)))(*args)
    return [list(outs[4 * t:4 * t + 4]) for t in range(n)]


SHARD_ROWS = 8


def _adamw_small(gathered, ws, ms, vs, me):
    n = len(gathered)
    full = [w is not None for w in ws]
    sharded = [w is not None and w.ndim == 3 for w in ws]
    args = list(gathered)
    out_shape = []
    for t in range(n):
        shape = jax.ShapeDtypeStruct(ws[t].shape if sharded[t] else gathered[t].shape[2:], F32)
        if full[t]:
            args += [ws[t], ms[t], vs[t]]
            out_shape += [shape] * 4
        else:
            out_shape += [shape]

    def body(*refs):
        i_in, i_out = n, len(args) + 1
        me_ref = refs[len(args)]
        for t in range(n):
            p_ref = refs[t]
            if sharded[t]:
                w_ref, m_ref, v_ref = refs[i_in:i_in + 3]
                taps, layers, _ = w_ref.shape
                mine = pl.ds(pl.multiple_of(me_ref[0] * SHARD_ROWS, SHARD_ROWS), SHARD_ROWS)
                g = p_ref[0, 0, mine, :]
                for k in range(1, N_DEV):
                    g = g + p_ref[0, k, mine, :]
                for l in range(layers):
                    for k in range(taps):
                        at = (k, slice(l, l + 1), slice(None))
                        row = g[l * taps + k:l * taps + k + 1]
                        refs[i_out][at] = row
                        refs[i_out + 1][at], refs[i_out + 2][at], refs[i_out + 3][at] = _adamw_math(
                            row, w_ref[at], m_ref[at], v_ref[at])
                i_in += 3
                i_out += 4
                continue
            g = p_ref[0, 0]
            for k in range(1, N_DEV):
                g = g + p_ref[0, k]
            refs[i_out][...] = g
            if full[t]:
                w_ref, m_ref, v_ref = refs[i_in:i_in + 3]
                refs[i_out + 1][...], refs[i_out + 2][...], refs[i_out + 3][...] = _adamw_math(
                    g, w_ref[...], m_ref[...], v_ref[...])
                i_in += 3
                i_out += 4
            else:
                i_out += 1

    outs = _pallas(body, name="adamw_small",
                   in_specs=[VMEM_SPEC] * len(args) + [pl.BlockSpec(memory_space=pltpu.SMEM)],
                   out_specs=[VMEM_SPEC] * len(out_shape), out_shape=out_shape,
                   params=pltpu.CompilerParams(vmem_limit_bytes=VMEM_LIMIT))(*args, me)
    result, i = [], 0
    for t in range(n):
        k = 4 if full[t] else 1
        result.append(list(outs[i:i + k]))
        i += k
    return result


KIND = {"sc_w_in": "cols", "sc_w_out": "blocked", "w_dkv": "blocked", "w_kr": "cols", "w_uk": "cols", "w_uv": "cols",
        "w_dq": "blocked", "w_uq": "blocked", "w_o": "blocked", "ffn_w_up": "blocked", "ffn_w_down": "blocked",
        "conv": "blocked"}
GATHER_GROUPS = (("mixer", ("sc_w_in",)),
                 ("mixer2", ("sc_w_out", "conv")),
                 ("up0", ("ffn_w_up0",)),
                 ("down0", ("ffn_w_down0",)),
                 ("attn", ("w_dkv", "w_kr", "w_uk", "w_uv", "w_dq", "w_uq", "w_o")),
                 ("ffn1", ("ffn_w_up1", "ffn_w_down1")))
SCATTER_GROUPS = (("ffn1", (("ffn_w_up", 1), ("ffn_w_down", 1))),
                  ("attn", (("w_o", None), ("w_uq", None), ("w_dq", None), ("w_uk", None), ("w_uv", None),
                            ("w_dkv", None), ("w_kr", None))),
                  ("ffn0", (("ffn_w_up", 0), ("ffn_w_down", 0))),
                  ("mixer", (("sc_w_out", None), ("sc_w_in", None))))
SCHEDULE = {
    "begin": (("gather_start", "mixer"),),
    "cast": (("gather_start", "mixer2"),),
    "l0_norm": (("gather_forward", "mixer"), ("gather_forward", "mixer2"), ("gather_start", "up0")),
    "l0_out": (("gather_forward", "up0"), ("gather_start", "down0"), ("gather_start", "attn")),
    "f0_up": (("gather_forward", "down0"), ("gather_start", "ffn1")),
    "f0_down": (("gather_forward", "attn"),),
    "attn_fwd": (("gather_forward", "ffn1"),),
    "f1_gup": (("scatter_sibling", "ffn1"),),
    "f1_dhf": (("scatter_chips", "ffn1"),),
    "kv_bwd": (("scatter_sibling", "attn"),),
    "f0_dact": (("scatter_chips", "attn"),),
    "f0_gup": (("scatter_sibling", "ffn0"),),
    "f0_dhf": (("scatter_chips", "ffn0"),),
    "sc_bwd": (("scatter_sibling", "mixer"), ("scatter_done", "attn")),
    "d_l0_in": (("scatter_chips", "mixer"),),
}
FINISH = (("scatter_done", "ffn1"), ("scatter_done", "ffn0"), ("scatter_done", "mixer"))
STAGES = {"gather_start": 1, "gather_forward": 2, "gather_done": 3,
          "scatter_sibling": 1, "scatter_chips": 2, "scatter_done": 3}
SMALL_W_ROWS = 24


def _pack(arrays, rows):
    flat = jnp.concatenate([a.reshape(-1).astype(F32) for a in arrays])
    return jnp.pad(flat, (0, rows * 128 - flat.shape[0])).reshape(rows, 128)


def _cast_shards(items):
    arrays = []
    for a, _, _ in items:
        if not any(a is b for b in arrays):
            arrays.append(a)
    slot = [next(i for i, b in enumerate(arrays) if b is a) for a, _, _ in items]

    def body(*refs):
        for t, (a, layer, rows) in enumerate(items):
            w_ref, o_ref = refs[slot[t]], refs[len(arrays) + t]
            r, c = a.shape[-2:]
            if a.ndim == 3:
                o_ref[:, :r] = w_ref[(layer or 0):(layer or 0) + 1].astype(BF16)
                if rows > r:
                    o_ref[:, r:] = jnp.zeros((1, rows - r, c), BF16)
            else:
                o_ref[:r] = w_ref[...].astype(BF16)
                if rows > r:
                    o_ref[r:] = jnp.zeros((rows - r, c), BF16)

    out_shape = [jax.ShapeDtypeStruct(((1,) if a.ndim == 3 else ()) + (rows, a.shape[-1]), BF16)
                 for a, _, rows in items]
    return _pallas(body, name="cast_shards", in_specs=[VMEM_SPEC] * len(arrays), out_specs=[VMEM_SPEC] * len(items),
                   out_shape=out_shape, params=pltpu.CompilerParams(vmem_limit_bytes=VMEM_LIMIT))(*arrays)


STORED_TRANSPOSED = ("ffn_w_up", "w_uq", "w_kr")


def _stored(name, a):
    return jnp.swapaxes(a, -1, -2) if name in STORED_TRANSPOSED else a


def _base(name):
    if name.startswith("ffn_w_") and name[-1] in "01":
        return name[:-1], int(name[-1])
    return name, None


class _Exchange:
    def __init__(self, wts, mom, var, ffn_conv_b):
        self.wts, self.mom, self.var = wts, mom, var
        x, y, c = _place()
        self.c_arr = jnp.reshape(c, (1,)).astype(jnp.int32)
        chip = 2 * x + y
        self.chip_ids = jnp.stack([chip, chip ^ 1, chip ^ 2, chip ^ 3]).astype(jnp.int32)
        self.ready = {"ffn_cb0": ffn_conv_b.reshape(2, N_FF_BLK, 1, FF_BLK)[0],
                      "ffn_cb1": ffn_conv_b.reshape(2, N_FF_BLK, 1, FF_BLK)[1]}
        self.gathers, self.group_of = {}, {}
        self.grads, self.scatters, self.results, self.queue = {}, {}, {}, []
        for gname, names in GATHER_GROUPS:
            self.gathers[gname] = dict(stage=0, names=names, kinds=[KIND[_base(nm)[0]] for nm in names])
            for nm in names:
                self.group_of[nm] = gname
        for nm in ("sc_conv_w", "ffn_cw0", "ffn_cw1"):
            self.group_of[nm] = self.group_of["conv"]
        self.cast, self.f32 = {}, {}
        self.at("begin", None)
        later = [nm for gname, names in GATHER_GROUPS[1:] for nm in names if nm != "conv"]
        self.cast = dict(zip(later, _cast_shards([self._shard_f32(nm) for nm in later])))
        self.at("cast", None)

    def _shard_f32(self, name):
        base, layer = _base(name)
        if base not in self.f32:
            a = _stored(base, self.wts[base])
            self.f32[base] = a.reshape(a.shape[-2:]) if KIND[base] == "cols" else a.reshape((-1,) + a.shape[-2:])
        a = self.f32[base]
        return a, layer, {"w_kr": 128, "w_uq": QK_PAD}.get(base, a.shape[-2])

    def _shard(self, name):
        if name in self.cast:
            return self.cast[name]
        if name == "conv":
            return _pack([self.wts["sc_conv_w"], self.wts["ffn_conv_w"]], SMALL_W_ROWS).reshape(1, SMALL_W_ROWS, 128)
        base, layer = _base(name)
        a = _stored(base, self.wts[base])
        if layer is not None:
            a = a[layer:layer + 1]
        if KIND[base] == "cols":
            return a.reshape(a.shape[-2], a.shape[-1]).astype(BF16)
        return a.reshape((-1,) + a.shape[-2:]).astype(BF16)

    def _start(self, name, srcs, lands, ncopy, plan, st):
        self.queue.append((name, (srcs, lands, ncopy, plan), st))

    def _flush(self):
        if self.queue:
            flights = _copies_start("__".join(name for name, _, _ in self.queue), [job for _, job, _ in self.queue])
            for (_, _, st), flight in zip(self.queue, flights):
                st["flight"] = flight
            self.queue = []

    def _flight(self, st):
        self._flush()
        return st["flight"]

    def _gather_to(self, gname, stage, after):
        st = self.gathers[gname]
        if st["stage"] < 1 <= stage:
            shards = [self._shard(nm) for nm in st["names"]]
            lands = [_landing(s, kind) for s, kind in zip(shards, st["kinds"])]
            plan, ncopy = _plan_gather_chips(st["kinds"])
            self._start(f"ag_{gname}_chips", shards, lands, ncopy, plan, st)
            st["stage"] = 1
        if st["stage"] < 2 <= stage:
            plan, ncopy = _plan_gather_chips(st["kinds"])
            _, lands = _copies_wait(f"ag_{gname}_chips_wait", self._flight(st), ncopy, plan)
            plan, ncopy = _plan_gather_sibling(st["kinds"])
            self._start(f"ag_{gname}_sibling", [], lands, ncopy, plan, st)
            st["stage"] = 2
        if st["stage"] < 3 <= stage:
            plan, ncopy = _plan_gather_sibling(st["kinds"])
            _, lands = _copies_wait(f"ag_{gname}_sibling_wait", self._flight(st), ncopy, plan)
            for nm, land in zip(st["names"], lands):
                self._arrived(nm, land)
            st["stage"] = 3

    def _arrived(self, name, land):
        if name == "conv":
            conv = land.reshape(N_DEV, SMALL_W_ROWS * 128)
            self.ready["sc_conv_w"] = conv[:, :3 * 128].reshape(N_DEV, 3, 128).transpose(1, 0, 2).reshape(3, D)
            fcw = conv[:, 3 * 128:3 * 128 + 6 * 352].reshape(N_DEV, 2, 3, 352).transpose(1, 2, 0, 3)
            fcw = fcw.reshape(2, 3, N_FF_BLK, FF_BLK).transpose(0, 2, 1, 3)
            self.ready["ffn_cw0"], self.ready["ffn_cw1"] = fcw[0], fcw[1]
        elif name in ("sc_w_in", "w_uk", "w_uv", "w_kr") or name.startswith("ffn_w_up"):
            self.ready[name] = land
        elif name.startswith("ffn_w_down"):
            self.ready[name] = land.reshape(1, N_FF_BLK, FF_BLK, D)
        elif name == "w_uq":
            self.ready[name] = land.reshape(N_HEADS, QK_PAD, Q_LORA)
        else:
            self.ready[name] = land.reshape(D, land.shape[-1])

    def need(self, name, after):
        if name not in self.ready:
            self._gather_to(self.group_of[name], 3, after)
            self._flush()
        return self.ready[name]

    def grad(self, name, layer, array):
        self.grads[(name, layer)] = array

    def _scatter_to(self, gname, stage, after):
        keys = dict(SCATTER_GROUPS)[gname]
        st = self.scatters.setdefault(gname, dict(stage=0))
        kinds = [KIND[nm] for nm, _ in keys]
        if st["stage"] < 1 <= stage:
            grads = [self.grads[key] for key in keys]
            lands = []
            for gr, kind in zip(grads, kinds):
                shard = (gr.shape[0],) + gr.shape[2:] if kind == "blocked" else (gr.shape[0], gr.shape[1] // N_DEV)
                lands.append(lax.empty((N_CHIP,) + shard, BF16))
            plan, ncopy = _plan_scatter_sibling(kinds)
            self._start(f"rs_{gname}_sibling", grads, lands, ncopy, plan, st)
            st["stage"] = 1
        if st["stage"] < 2 <= stage:
            plan, ncopy = _plan_scatter_sibling(kinds)
            grads, recvs = _copies_wait(f"rs_{gname}_sibling_wait", self._flight(st), ncopy, plan)
            sums = _chip_sums(f"rs_{gname}_sums", grads, kinds, recvs, self.c_arr)
            lands = [lax.empty(s.shape, BF16) for s in sums]
            plan, ncopy = _plan_scatter_chips(len(sums))
            self._start(f"rs_{gname}_chips", sums, lands, ncopy, plan, st)
            st["stage"] = 2
        if st["stage"] < 3 <= stage:
            plan, ncopy = _plan_scatter_chips(len(keys))
            sums, recvs = _copies_wait(f"rs_{gname}_chips_wait", self._flight(st), ncopy, plan)
            items = []
            for (nm, layer), own, rv in zip(keys, sums, recvs):
                nl = 1 if layer is None else 2
                rows, w = own.shape[1], own.shape[2]
                w3, m3, v3 = (_stored(nm, src[nm]).reshape(nl, rows, w) for src in (self.wts, self.mom, self.var))
                items.append((own, rv, w3, m3, v3, 0 if layer is None else layer, self.results.get(nm)))
            outs = _adamw_group(f"adamw_{gname}", items, self.chip_ids)
            for (nm, _), out in zip(keys, outs):
                self.results[nm] = out
            st["stage"] = 3

    def at(self, place, after):
        for action, gname in SCHEDULE.get(place, ()):
            self._advance(action, gname, after)
        self._flush()

    def _advance(self, action, gname, after):
        if action.startswith("gather"):
            self._gather_to(gname, STAGES[action], after)
        else:
            self._scatter_to(gname, STAGES[action], after)

    def finish(self, after):
        for action, gname in FINISH:
            self._advance(action, gname, after)
        for gname, _ in SCATTER_GROUPS:
            self._scatter_to(gname, 3, after)
        return {nm: [_stored(nm, o.reshape(_stored(nm, self.wts[nm]).shape)) for o in outs]
                for nm, outs in self.results.items()}


REPLICATED = ("attn_norm", "ffn_norm", "final_norm", "kv_in_norm", "kv_latent_norm", "q_latent_norm", "ffn_conv_b")
WEIGHTS = ("attn_norm", "ffn_norm", "final_norm", "sc_w_in", "sc_conv_w", "sc_w_out", "kv_in_norm", "w_dkv",
           "kv_latent_norm", "w_kr", "w_uk", "w_uv", "w_dq", "q_latent_norm", "w_uq", "w_o", "ffn_w_up", "ffn_conv_w",
           "ffn_conv_b", "ffn_w_down")


def kernel(x, positions, attn_norm, ffn_norm, final_norm, sc_w_in, sc_conv_w, sc_w_out, kv_in_norm, w_dkv, kv_latent_norm, w_kr, w_uk, w_uv, w_dq, q_latent_norm, w_uq, w_o, ffn_w_up, ffn_conv_w, ffn_conv_b, ffn_w_down, loss_target, m_attn_norm, m_ffn_norm, m_final_norm, m_sc_w_in, m_sc_conv_w, m_sc_w_out, m_kv_in_norm, m_w_dkv, m_kv_latent_norm, m_w_kr, m_w_uk, m_w_uv, m_w_dq, m_q_latent_norm, m_w_uq, m_w_o, m_ffn_w_up, m_ffn_conv_w, m_ffn_conv_b, m_ffn_w_down, v_attn_norm, v_ffn_norm, v_final_norm, v_sc_w_in, v_sc_conv_w, v_sc_w_out, v_kv_in_norm, v_w_dkv, v_kv_latent_norm, v_w_kr, v_w_uk, v_w_uv, v_w_dq, v_q_latent_norm, v_w_uq, v_w_o, v_ffn_w_up, v_ffn_conv_w, v_ffn_conv_b, v_ffn_w_down):
    wts = dict(attn_norm=attn_norm, ffn_norm=ffn_norm, final_norm=final_norm, sc_w_in=sc_w_in, sc_conv_w=sc_conv_w,
               sc_w_out=sc_w_out, kv_in_norm=kv_in_norm, w_dkv=w_dkv, kv_latent_norm=kv_latent_norm, w_kr=w_kr,
               w_uk=w_uk, w_uv=w_uv, w_dq=w_dq, q_latent_norm=q_latent_norm, w_uq=w_uq, w_o=w_o, ffn_w_up=ffn_w_up,
               ffn_conv_w=ffn_conv_w, ffn_conv_b=ffn_conv_b, ffn_w_down=ffn_w_down)
    mom = dict(attn_norm=m_attn_norm, ffn_norm=m_ffn_norm, final_norm=m_final_norm, sc_w_in=m_sc_w_in,
               sc_conv_w=m_sc_conv_w, sc_w_out=m_sc_w_out, kv_in_norm=m_kv_in_norm, w_dkv=m_w_dkv,
               kv_latent_norm=m_kv_latent_norm, w_kr=m_w_kr, w_uk=m_w_uk, w_uv=m_w_uv, w_dq=m_w_dq,
               q_latent_norm=m_q_latent_norm, w_uq=m_w_uq, w_o=m_w_o, ffn_w_up=m_ffn_w_up, ffn_conv_w=m_ffn_conv_w,
               ffn_conv_b=m_ffn_conv_b, ffn_w_down=m_ffn_w_down)
    var = dict(attn_norm=v_attn_norm, ffn_norm=v_ffn_norm, final_norm=v_final_norm, sc_w_in=v_sc_w_in,
               sc_conv_w=v_sc_conv_w, sc_w_out=v_sc_w_out, kv_in_norm=v_kv_in_norm, w_dkv=v_w_dkv,
               kv_latent_norm=v_kv_latent_norm, w_kr=v_w_kr, w_uk=v_w_uk, w_uv=v_w_uv, w_dq=v_w_dq,
               q_latent_norm=v_q_latent_norm, w_uq=v_w_uq, w_o=v_w_o, ffn_w_up=v_ffn_w_up, ffn_conv_w=v_ffn_conv_w,
               ffn_conv_b=v_ffn_conv_b, ffn_w_down=v_ffn_w_down)
    xi, yi, ci = _place()
    me = 4 * xi + 2 * yi + ci
    _Chain.last = None

    ex = _Exchange(wts, mom, var, ffn_conv_b)
    rep = {
        "attn_norm": attn_norm, "ffn_norm": ffn_norm, "final_norm": final_norm,
        "kv_in_norm": kv_in_norm.reshape(1, D), "kv_latent_norm": kv_latent_norm.reshape(1, KV_LORA),
        "q_latent_norm": q_latent_norm.reshape(1, Q_LORA),
    }
    loss, grad_x, small = _local_step(x.reshape(T, D), positions.reshape(T, 1), loss_target.reshape(T, D), rep, ex)

    def rows_of(a):
        return a.reshape(-1, a.shape[-1])

    def device_rows(a):
        taps, c = a.shape[-2], a.shape[-1] // N_DEV
        rows = a.reshape(-1, taps, N_DEV, c).transpose(2, 0, 1, 3).reshape(N_DEV, -1, c)
        return jnp.pad(rows, ((0, 0), (0, SHARD_ROWS - rows.shape[1]), (0, 0))).reshape(N_DEV * SHARD_ROWS, c)

    def taps_first(a):
        return jnp.transpose(a, (1, 0, 2))

    sharded = ("sc_conv_w", "ffn_conv_w")
    shards = ([loss.reshape(1, 1, 128)] + [rows_of(small[nm])[None] for nm in REPLICATED]
              + [device_rows(small[nm])[None] for nm in sharded])
    plan, ncopy = _plan_gather_all(len(shards))
    flight, = _copies_start("ag_small", [(shards, [lax.empty((1, N_DEV) + s.shape[1:], F32) for s in shards], ncopy, plan)])
    results = ex.finish(grad_x)
    _, gathered = _copies_wait("ag_small_wait", flight, ncopy, plan)
    params = [[None] + [rows_of(src[nm]) for nm in REPLICATED] + [taps_first(src[nm]) for nm in sharded]
              for src in (wts, mom, var)]
    summed = _adamw_small(gathered, *params, me.astype(jnp.int32).reshape(1))
    loss_total = summed[0][0][0, 0]
    for nm, vals in zip(REPLICATED, summed[1:1 + len(REPLICATED)]):
        results[nm] = [a.reshape(wts[nm].shape) for a in vals]
    for nm, vals in zip(sharded, summed[1 + len(REPLICATED):]):
        results[nm] = [taps_first(a) for a in vals]

    outs = [loss_total, grad_x.reshape(1, T, D)]
    for slot in range(4):
        outs.extend(results[nm][slot] for nm in WEIGHTS)
    return tuple(outs)
```

```python
import jax
import jax.numpy as jnp
from jax import lax
from jax.experimental import pallas as pl
from jax.experimental.pallas import tpu as pltpu

F32 = jnp.float32
BF16 = jnp.bfloat16

T = 2048
D = 1024
N_HEADS = 8
QK_NOPE = 128
QK_ROPE = 64
V_HEAD = 128
Q_LORA = 384
KV_LORA = 256
D_FF = 2816
CHUNK = 64
ROPE_THETA = 10000.0
EPS = 1e-6
NEG_INF = -1e30
ADAM_LR = 0.001
ADAM_B1 = 0.9
ADAM_B2 = 0.999
ADAM_EPS = 1e-08
ADAM_WD = 0.01
ADAM_STEP = 10

N_DEV = 8
N_CHIP = 4
FF_BLK = D_FF * 2 // N_DEV
N_FF_BLK = D_FF // FF_BLK
QK_PAD = 256
HALO = 16

TM = 1024
TS = 512
TR = 256
TQ = 512
VMEM_LIMIT = 56 * 1024 * 1024

NN = (((1,), (0,)), ((), ()))
NT = (((1,), (1,)), ((), ()))
TN = (((0,), (0,)), ((), ()))
MESH = pl.DeviceIdType.MESH


def _params(sem):
    return pltpu.CompilerParams(dimension_semantics=sem, vmem_limit_bytes=VMEM_LIMIT)


ANY_SPEC = pl.BlockSpec(memory_space=pl.ANY)
VMEM_SPEC = pl.BlockSpec(memory_space=pltpu.VMEM)


class _Chain:
    last = None


def _pallas(body, *, name, in_specs, out_specs, out_shape, grid=(), scratch_shapes=(), n_prefetch=0, aliases=None,
            params=None):
    def run(*args):
        after = _Chain.last
        n_lead = len(args)
        specs, operands, fn = list(in_specs), list(args), body
        if after is not None:
            def fn(*refs):
                return body(*refs[:n_lead], *refs[n_lead + 1:])
            specs.append(ANY_SPEC)
            operands.append(after)
        kw = dict(name=name, out_shape=out_shape, input_output_aliases=aliases or {})
        if params is not None:
            kw["compiler_params"] = params
        if n_prefetch:
            kw["grid_spec"] = pltpu.PrefetchScalarGridSpec(
                num_scalar_prefetch=n_prefetch, grid=grid, in_specs=specs, out_specs=out_specs,
                scratch_shapes=scratch_shapes)
        else:
            kw.update(grid=grid, in_specs=specs, out_specs=out_specs, scratch_shapes=scratch_shapes)
        outs = pl.pallas_call(fn, **kw)(*operands)
        _Chain.last = outs[0] if isinstance(outs, (list, tuple)) else outs
        return outs
    return run


def _mm(name, a, b, *, grid, a_spec, b_spec, o_spec, o_shape, o_dtype, dims, k_axis=None, acc_shape=None,
        add=None, add_spec=None):
    nk = grid[k_axis] if k_axis is not None else 1
    has_add = add is not None

    def body(*refs):
        a_ref, b_ref = refs[0], refs[1]
        p = 2
        add_ref = None
        if has_add:
            add_ref = refs[p]
            p += 1
        o_ref = refs[p]
        p += 1
        r = lax.dot_general(a_ref[...].astype(BF16), b_ref[...].astype(BF16), dims, preferred_element_type=F32)
        if k_axis is None:
            if has_add:
                r = r + add_ref[...].astype(F32)
            o_ref[...] = r.astype(o_dtype)
        else:
            acc = refs[p]
            k = pl.program_id(k_axis)

            @pl.when(k == 0)
            def _():
                acc[...] = r

            @pl.when(k > 0)
            def _():
                acc[...] += r

            @pl.when(k == nk - 1)
            def _():
                t = acc[...]
                if has_add:
                    t = t + add_ref[...].astype(F32)
                o_ref[...] = t.astype(o_dtype)

    in_specs = [a_spec, b_spec]
    args = [a, b]
    if has_add:
        in_specs.append(add_spec if add_spec is not None else o_spec)
        args.append(add)
    sem = tuple("arbitrary" if ax == k_axis else "parallel" for ax in range(len(grid)))
    scratch = [pltpu.VMEM(acc_shape, F32)] if k_axis is not None else []
    return _pallas(body, name=name, grid=grid, in_specs=in_specs, out_specs=o_spec,
                   out_shape=jax.ShapeDtypeStruct(o_shape, o_dtype), scratch_shapes=scratch, params=_params(sem))(*args)


def _mm_sum(name, parts, *, grid, o_spec, o_shape, o_dtype, add=None, norm_bwd=None, post=None):
    has_add = add is not None
    np_ = len(parts)
    nn = 1 if norm_bwd is None else len(norm_bwd[1])
    has_res = norm_bwd is not None and norm_bwd[2] is not None
    has_post = post is not None

    def body(*refs):
        accs = [None] * nn
        for p, (_, _, _, _, dims, n) in enumerate(parts):
            a_ref, b_ref = refs[2 * p], refs[2 * p + 1]
            for k in range(a_ref.shape[0]):
                r = lax.dot_general(a_ref[k], b_ref[k], dims, preferred_element_type=F32)
                accs[n] = r if accs[n] is None else accs[n] + r
        if norm_bwd is None:
            acc = accs[0]
            if has_add:
                acc = acc + refs[2 * np_][...]
            refs[-1][...] = acc.astype(o_dtype)
            return
        x_ref, g_refs = refs[2 * np_], refs[2 * np_ + 1:2 * np_ + 1 + nn]
        n_in = 2 * np_ + 1 + nn + has_res + has_post
        dx_ref, dxb_ref, dg_refs = refs[n_in], refs[n_in + 1], refs[n_in + 2:n_in + 2 + nn]
        xv = x_ref[...]
        r = lax.rsqrt(jnp.mean(xv * xv, axis=-1, keepdims=True) + EPS)
        xn = xv * r
        dx = refs[2 * np_ + 1 + nn][...] if has_res else None
        sums = []
        for acc, g_ref in zip(accs, g_refs):
            gdy = acc * g_ref[...]
            t = r * (gdy - xn * jnp.mean(gdy * xn, axis=-1, keepdims=True))
            dx = t if dx is None else dx + t
            sums.append(jnp.sum(acc * xn, axis=0, keepdims=True))
        dx_ref[...] = dx
        dxb = dx.astype(BF16)
        dxb_ref[...] = dxb
        if has_post:
            refs[n_in + 2 + nn][...] = lax.dot_general(dxb, refs[n_in - 1][...], post[1],
                                                       preferred_element_type=F32).astype(BF16)

        @pl.when(pl.program_id(0) == 0)
        def _():
            for dg_ref, part in zip(dg_refs, sums):
                dg_ref[...] = part

        @pl.when(pl.program_id(0) > 0)
        def _():
            for dg_ref, part in zip(dg_refs, sums):
                dg_ref[...] += part

    in_specs, args = [], []
    for a, a_spec, b, b_spec, _, _ in parts:
        in_specs += [a_spec, b_spec]
        args += [a, b]
    if norm_bwd is None:
        if has_add:
            in_specs.append(o_spec)
            args.append(add)
        return _pallas(body, name=name, grid=grid, in_specs=in_specs, out_specs=o_spec,
                       out_shape=jax.ShapeDtypeStruct(o_shape, o_dtype),
                       params=_params(("parallel",) * len(grid)))(*args)
    x, gains, dres = norm_bwd
    vec = pl.BlockSpec((1, o_shape[1]), lambda i: (0, 0))
    in_specs += [o_spec] + [vec] * nn + ([o_spec] if has_res else [])
    args += [x] + list(gains) + ([dres] if has_res else [])
    out_specs = [o_spec, o_spec] + [vec] * nn
    out_shape = ([jax.ShapeDtypeStruct(o_shape, F32), jax.ShapeDtypeStruct(o_shape, BF16)]
                 + [jax.ShapeDtypeStruct((1, o_shape[1]), F32)] * nn)
    if has_post:
        in_specs.append(pl.BlockSpec(post[0].shape, lambda i: (0, 0)))
        args.append(post[0])
        out_specs.append(pl.BlockSpec((o_spec.block_shape[0], post[2]), lambda i: (i, 0)))
        out_shape.append(jax.ShapeDtypeStruct((o_shape[0], post[2]), BF16))
    outs = _pallas(body, name=name, grid=grid, in_specs=in_specs, out_specs=out_specs, out_shape=out_shape,
                   params=_params(("arbitrary",)))(*args)
    if has_post:
        return outs[0], outs[1], list(outs[2:2 + nn]), outs[2 + nn]
    return outs[0], outs[1], list(outs[2:])


def _mm_rows(name, a, b, dims, o_dtype, n_out, *, tn=None, add=None):
    k = a.shape[1]
    tn = n_out if tn is None else tn
    if dims == NN:
        b_spec = pl.BlockSpec((k, tn), lambda n, i: (0, n))
    else:
        b_spec = pl.BlockSpec((tn, k), lambda n, i: (n, 0))
    return _mm(name, a, b, grid=(n_out // tn, T // TM),
               a_spec=pl.BlockSpec((TM, k), lambda n, i: (i, 0)), b_spec=b_spec,
               o_spec=pl.BlockSpec((TM, tn), lambda n, i: (i, n)), o_shape=(T, n_out), o_dtype=o_dtype,
               dims=dims, add=add)


def _wgrads(name, jobs):
    jobs = [job if len(job) == 3 else (*job, job[0].shape[-1]) for job in jobs]
    arrays, index = [], {}
    for a, b, _ in jobs:
        for arr in (a, b):
            if id(arr) not in index:
                index[id(arr)] = len(arrays)
                arrays.append(arr)
    n_in = len(arrays)

    def body(*refs):
        for t, (a, b, rows) in enumerate(jobs):
            a_ref, b_ref, o_ref = refs[index[id(a)]], refs[index[id(b)]], refs[n_in + t]
            if a.ndim == 3:
                for h in range(a.shape[0]):
                    o_ref[h] = lax.dot_general(a_ref[h], b_ref[...], TN, preferred_element_type=F32)[:rows].astype(BF16)
            else:
                o_ref[...] = lax.dot_general(a_ref[...], b_ref[...], TN, preferred_element_type=F32)[:rows].astype(BF16)

    out_shape = [jax.ShapeDtypeStruct(a.shape[:-2] + (rows, b.shape[-1]), BF16) for a, b, rows in jobs]
    return _pallas(body, name=name, in_specs=[VMEM_SPEC] * n_in, out_specs=[VMEM_SPEC] * len(jobs), out_shape=out_shape,
                   params=pltpu.CompilerParams(vmem_limit_bytes=VMEM_LIMIT))(*arrays)


def _mm_wgrad(name, a, b, *, tn=512):
    k, n = a.shape[1], b.shape[1]
    tn = min(tn, n)
    return _mm(name, a, b, grid=(n // tn,),
               a_spec=pl.BlockSpec((T, k), lambda j: (0, 0)), b_spec=pl.BlockSpec((T, tn), lambda j: (0, j)),
               o_spec=pl.BlockSpec((k, tn), lambda j: (0, j)), o_shape=(k, n), o_dtype=BF16, dims=TN)


def _rms_fwd(name, x, g):
    d = x.shape[1]

    def body(x_ref, g_ref, o_ref):
        xv = x_ref[...]
        r = lax.rsqrt(jnp.mean(xv * xv, axis=-1, keepdims=True) + EPS)
        o_ref[...] = ((xv * r) * g_ref[...]).astype(BF16)

    return _pallas(
        body, name=name, grid=(T // TM,),
        in_specs=[pl.BlockSpec((TM, d), lambda i: (i, 0)), pl.BlockSpec((1, d), lambda i: (0, 0))],
        out_specs=pl.BlockSpec((TM, d), lambda i: (i, 0)),
        out_shape=jax.ShapeDtypeStruct((T, d), BF16), params=_params(("parallel",)))(x, g)


def _rms(xv, g):
    return (xv * lax.rsqrt(jnp.mean(xv * xv, axis=-1, keepdims=True) + EPS)) * g


def _out_norm(name, a, w, add, g):
    def body(a_ref, w_ref, add_ref, g_ref, h_ref, hn_ref):
        hv = lax.dot_general(a_ref[...], w_ref[...], NN, preferred_element_type=F32) + add_ref[...]
        h_ref[...] = hv
        hn_ref[...] = _rms(hv, g_ref[...]).astype(BF16)

    rows = pl.BlockSpec((TS, D), lambda i: (i, 0))
    return _pallas(
        body, name=name, grid=(T // TS,),
        in_specs=[pl.BlockSpec((TS, a.shape[1]), lambda i: (i, 0)), pl.BlockSpec(w.shape, lambda i: (0, 0)), rows,
                  pl.BlockSpec((1, D), lambda i: (0, 0))],
        out_specs=[rows, rows], out_shape=[jax.ShapeDtypeStruct((T, D), F32), jax.ShapeDtypeStruct((T, D), BF16)],
        params=_params(("parallel",)))(a, w, add, g)


def _down_final(act, w_down4, h_in, g, tgt):
    def body(a_ref, w_ref, hin_ref, g_ref, t_ref, loss_ref, dh_ref, dhb_ref, dg_ref):
        hv = lax.dot_general(a_ref[0], w_ref[0], NN, preferred_element_type=F32)
        for j in range(1, N_FF_BLK):
            hv = hv + lax.dot_general(a_ref[j], w_ref[j], NN, preferred_element_type=F32)
        hv = hv + hin_ref[...]
        r = lax.rsqrt(jnp.mean(hv * hv, axis=-1, keepdims=True) + EPS)
        xn = hv * r
        gv = g_ref[...]
        err = xn * gv - t_ref[...]
        part_loss = 0.5 * jnp.sum(jnp.mean(err * err, axis=-1, keepdims=True), axis=0, keepdims=True)
        dy = err * (1.0 / D)
        gdy = dy * gv
        dh = r * (gdy - xn * jnp.mean(gdy * xn, axis=-1, keepdims=True))
        dh_ref[...] = dh
        dhb_ref[...] = dh.astype(BF16)
        part = jnp.sum(dy * xn, axis=0, keepdims=True)
        first = pl.program_id(0) == 0

        @pl.when(first)
        def _():
            dg_ref[...] = part
            loss_ref[...] = jnp.broadcast_to(part_loss, (1, 128))

        @pl.when(jnp.logical_not(first))
        def _():
            dg_ref[...] += part
            loss_ref[...] += jnp.broadcast_to(part_loss, (1, 128))

    row = pl.BlockSpec((TS, D), lambda i: (i, 0))
    vec = pl.BlockSpec((1, D), lambda i: (0, 0))
    return _pallas(
        body, name="f1_down_loss", grid=(T // TS,),
        in_specs=[pl.BlockSpec((N_FF_BLK, TS, FF_BLK), lambda i: (0, i, 0)),
                  pl.BlockSpec((None, N_FF_BLK, FF_BLK, D), lambda i: (0, 0, 0, 0)), row, vec, row],
        out_specs=[pl.BlockSpec((1, 128), lambda i: (0, 0)), row, row, vec],
        out_shape=[jax.ShapeDtypeStruct((1, 128), F32), jax.ShapeDtypeStruct((T, D), F32),
                   jax.ShapeDtypeStruct((T, D), BF16), jax.ShapeDtypeStruct((1, D), F32)],
        params=_params(("arbitrary",)))(act, w_down4, h_in, g, tgt)


def _prev_idx(i, rows=TR):
    return jnp.maximum(i * (rows // HALO) - 1, 0)


def _next_idx(i, rows=TR):
    return jnp.minimum((i + 1) * (rows // HALO), T // HALO - 1)


def _causal_taps(ext):
    return pltpu.roll(ext, 2, 0)[HALO:], pltpu.roll(ext, 1, 0)[HALO:], ext[HALO:]


def _anticausal_taps(ext, n):
    rows = ext.shape[0]
    return pltpu.roll(ext, rows - 1, 0)[:n], pltpu.roll(ext, rows - 2, 0)[:n]


MIX_COLS = 512


def _mixer_in(hn, w_in, w):
    nc = D // MIX_COLS

    def body(h_ref, hh_ref, wb_ref, wc_ref, wu_ref, w_ref, b_ref, c_ref, u_ref, y_ref):
        i = pl.program_id(1)
        hv = h_ref[...]
        he = jnp.concatenate([hh_ref[...], hv], axis=0)
        ce = lax.dot_general(he, wc_ref[...], NN, preferred_element_type=F32).astype(BF16)
        ue = lax.dot_general(he, wu_ref[...], NN, preferred_element_type=F32).astype(BF16)
        bv = lax.dot_general(hv, wb_ref[...], NN, preferred_element_type=F32).astype(BF16)
        b_ref[...] = bv
        c_ref[...] = ce[HALO:]
        u_ref[...] = ue[HALO:]
        row = lax.broadcasted_iota(jnp.int32, (HALO + TS, 1), 0)
        cu = jnp.where(jnp.logical_or(i > 0, row >= HALO), ce.astype(F32) * ue.astype(F32), 0.0)
        x2, x1, x0 = _causal_taps(cu)
        wv = w_ref[...]
        cv = (x2 * wv[0:1] + x1 * wv[1:2]) + x0 * wv[2:3]
        y_ref[...] = (bv.astype(F32) * cv).astype(BF16)

    def cols(part):
        return pl.BlockSpec((D, MIX_COLS), lambda j, i: (0, part * nc + j))

    blk = pl.BlockSpec((TS, MIX_COLS), lambda j, i: (i, j))
    out = jax.ShapeDtypeStruct((T, D), BF16)
    return _pallas(
        body, name="l0_in", grid=(nc, T // TS),
        in_specs=[pl.BlockSpec((TS, D), lambda j, i: (i, 0)), pl.BlockSpec((HALO, D), lambda j, i: (_prev_idx(i, TS), 0)),
                  cols(0), cols(1), cols(2), pl.BlockSpec((3, MIX_COLS), lambda j, i: (0, j))],
        out_specs=[blk] * 4, out_shape=[out] * 4,
        params=_params(("parallel", "parallel")))(hn, hn, w_in, w_in, w_in, w)


def _mixer_out_bwd(dh, w_out, zb, zc, zu, w):
    last = T // TR - 1

    def body(dh_ref, dhn_ref, wo_ref, b_ref, bn_ref, c_ref, ch_ref, u_ref, uh_ref, w_ref, dz_ref, dw_ref):
        i = pl.program_id(0)
        dye = lax.dot_general(jnp.concatenate([dh_ref[...], dhn_ref[...]], axis=0), wo_ref[...], NT,
                              preferred_element_type=F32)
        cv_ = c_ref[...].astype(F32)
        uv = u_ref[...].astype(F32)
        cu = cv_ * uv
        cuh = jnp.where(i > 0, ch_ref[...].astype(F32) * uh_ref[...].astype(F32), 0.0)
        x2, x1, x0 = _causal_taps(jnp.concatenate([cuh, cu], axis=0))
        wv = w_ref[...]
        conv = (x2 * wv[0:1] + x1 * wv[1:2]) + x0 * wv[2:3]
        dyv = dye[:TR]
        dz_ref[:, 0:D] = (dyv * conv).astype(BF16)
        dconv = dyv * b_ref[...].astype(F32)
        dconv_n = jnp.where(i < last, dye[TR:] * bn_ref[...].astype(F32), 0.0)
        n1, n2 = _anticausal_taps(jnp.concatenate([dconv, dconv_n], axis=0), TR)
        dcu = (dconv * wv[2:3] + n1 * wv[1:2]) + n2 * wv[0:1]
        dz_ref[:, D:2 * D] = (dcu * uv).astype(BF16)
        dz_ref[:, 2 * D:3 * D] = (dcu * cv_).astype(BF16)
        part = jnp.concatenate([jnp.sum(dconv * x2, axis=0, keepdims=True),
                                jnp.sum(dconv * x1, axis=0, keepdims=True),
                                jnp.sum(dconv * x0, axis=0, keepdims=True)], axis=0)

        @pl.when(i == 0)
        def _():
            dw_ref[...] = part

        @pl.when(i > 0)
        def _():
            dw_ref[...] += part

    main = pl.BlockSpec((TR, D), lambda i: (i, 0))
    prev = pl.BlockSpec((HALO, D), lambda i: (_prev_idx(i), 0))
    nxt = pl.BlockSpec((HALO, D), lambda i: (_next_idx(i), 0))
    wspec = pl.BlockSpec((3, D), lambda i: (0, 0))
    return _pallas(
        body, name="d_l0_out", grid=(T // TR,),
        in_specs=[main, nxt, pl.BlockSpec((D, D), lambda i: (0, 0)), main, nxt, main, prev, main, prev, wspec],
        out_specs=[pl.BlockSpec((TR, 3 * D), lambda i: (i, 0)), wspec],
        out_shape=[jax.ShapeDtypeStruct((T, 3 * D), BF16), jax.ShapeDtypeStruct((3, D), F32)],
        params=_params(("arbitrary",)))(dh, dh, w_out, zb, zb, zc, zc, zu, zu, w)


def _sigmoid(x):
    return 0.5 * jnp.tanh(0.5 * x) + 0.5


def _ffn_up_act(name, hf, w_up, w, b):
    def body(h_ref, hh_ref, wg_ref, wv_ref, w_ref, b_ref, g_ref, v_ref, a_ref):
        i = pl.program_id(1)
        hv = h_ref[...]
        ge = lax.dot_general(jnp.concatenate([hh_ref[...], hv], axis=0), wg_ref[...], NT,
                             preferred_element_type=F32).astype(BF16)
        v = lax.dot_general(hv, wv_ref[...], NT, preferred_element_type=F32).astype(BF16)
        g_ref[...] = ge[HALO:]
        v_ref[...] = v
        ext = ge.astype(F32)
        row = lax.broadcasted_iota(jnp.int32, (HALO + TM, 1), 0)
        ext = jnp.where(jnp.logical_or(i > 0, row >= HALO), ext, 0.0)
        x2, x1, x0 = _causal_taps(ext)
        wv = w_ref[...]
        gc = ((x2 * wv[0:1] + x1 * wv[1:2]) + x0 * wv[2:3]) + b_ref[...]
        a_ref[...] = ((gc * _sigmoid(gc)) * v.astype(F32)).astype(BF16)

    blk = pl.BlockSpec((None, TM, FF_BLK), lambda j, i: (j, i, 0))
    out = jax.ShapeDtypeStruct((N_FF_BLK, T, FF_BLK), BF16)
    return _pallas(
        body, name=name, grid=(N_FF_BLK, T // TM),
        in_specs=[pl.BlockSpec((TM, D), lambda j, i: (i, 0)),
                  pl.BlockSpec((HALO, D), lambda j, i: (_prev_idx(i, TM), 0)),
                  pl.BlockSpec((None, None, FF_BLK, D), lambda j, i: (0, j, 0, 0)),
                  pl.BlockSpec((None, None, FF_BLK, D), lambda j, i: (0, j + N_FF_BLK, 0, 0)),
                  pl.BlockSpec((None, 3, FF_BLK), lambda j, i: (j, 0, 0)),
                  pl.BlockSpec((None, 1, FF_BLK), lambda j, i: (j, 0, 0))],
        out_specs=[blk, blk, blk], out_shape=[out, out, out],
        params=_params(("parallel", "parallel")))(hf, hf, w_up, w_up, w, b)


def _ffn_dact(name, dh, w_down4, g, v, w, b):
    last = T // TS - 1

    def body(dh_ref, dhn_ref, wd_ref, g_ref, gp_ref, gn_ref, v_ref, vn_ref, w_ref, b_ref, dg_ref, dv_ref, dw_ref, db_ref):
        i = pl.program_id(1)
        da = lax.dot_general(jnp.concatenate([dh_ref[...], dhn_ref[...]], axis=0), wd_ref[...], NT,
                             preferred_element_type=F32)
        row = lax.broadcasted_iota(jnp.int32, (TS + HALO, 1), 0)
        da = jnp.where(jnp.logical_or(i < last, row < TS), da, 0.0)
        gp = jnp.where(i > 0, gp_ref[...].astype(F32), 0.0)
        ext = jnp.concatenate([gp, g_ref[...].astype(F32), gn_ref[...].astype(F32)], axis=0)
        x2, x1, x0 = _causal_taps(ext)
        wv = w_ref[...]
        gc = ((x2 * wv[0:1] + x1 * wv[1:2]) + x0 * wv[2:3]) + b_ref[...]
        sg = _sigmoid(gc)
        vv = jnp.concatenate([v_ref[...].astype(F32), vn_ref[...].astype(F32)], axis=0)
        silu = gc * sg
        dv_ref[...] = (da[:TS] * silu[:TS]).astype(BF16)
        dgc = (da * vv) * (sg + silu * (1.0 - sg))
        n1, n2 = _anticausal_taps(dgc, TS)
        d0 = dgc[:TS]
        dg_ref[...] = ((d0 * wv[2:3] + n1 * wv[1:2]) + n2 * wv[0:1]).astype(BF16)
        part_w = jnp.concatenate([jnp.sum(d0 * x2[:TS], axis=0, keepdims=True),
                                  jnp.sum(d0 * x1[:TS], axis=0, keepdims=True),
                                  jnp.sum(d0 * x0[:TS], axis=0, keepdims=True)], axis=0)
        part_b = jnp.sum(d0, axis=0, keepdims=True)

        @pl.when(i == 0)
        def _():
            dw_ref[...] = part_w
            db_ref[...] = part_b

        @pl.when(i > 0)
        def _():
            dw_ref[...] += part_w
            db_ref[...] += part_b

    blk = pl.BlockSpec((None, TS, FF_BLK), lambda j, i: (j, i, 0))
    prev = pl.BlockSpec((None, HALO, FF_BLK), lambda j, i: (j, _prev_idx(i, TS), 0))
    nxt = pl.BlockSpec((None, HALO, FF_BLK), lambda j, i: (j, _next_idx(i, TS), 0))
    wspec = pl.BlockSpec((None, 3, FF_BLK), lambda j, i: (j, 0, 0))
    bspec = pl.BlockSpec((None, 1, FF_BLK), lambda j, i: (j, 0, 0))
    return _pallas(
        body, name=name, grid=(N_FF_BLK, T // TS),
        in_specs=[pl.BlockSpec((TS, D), lambda j, i: (i, 0)),
                  pl.BlockSpec((HALO, D), lambda j, i: (_next_idx(i, TS), 0)),
                  pl.BlockSpec((None, None, FF_BLK, D), lambda j, i: (0, j, 0, 0)),
                  blk, prev, nxt, blk, nxt, wspec, bspec],
        out_specs=[blk, blk, wspec, bspec],
        out_shape=[jax.ShapeDtypeStruct((N_FF_BLK, T, FF_BLK), BF16), jax.ShapeDtypeStruct((N_FF_BLK, T, FF_BLK), BF16),
                   jax.ShapeDtypeStruct((N_FF_BLK, 3, FF_BLK), F32), jax.ShapeDtypeStruct((N_FF_BLK, 1, FF_BLK), F32)],
        params=_params(("parallel", "arbitrary")))(dh, dh, w_down4, g, g, g, v, v, w, b)


def _rope_tables(pos, inv_freq):
    half = QK_ROPE // 2

    def body(p_ref, f_ref, c_ref, sa_ref, sb_ref):
        ang = p_ref[...].astype(F32) * f_ref[...]
        lane = lax.broadcasted_iota(jnp.int32, (T, 128), 1)
        c = jnp.cos(ang)
        s = jnp.sin(ang)
        c_ref[...] = jnp.where(lane < 2 * half, c, 0.0)
        sa_ref[...] = jnp.where(lane < half, -s, 0.0)
        sb_ref[...] = jnp.where(jnp.logical_and(lane >= half, lane < 2 * half), s, 0.0)

    return _pallas(
        body, name="rope_tables", in_specs=[VMEM_SPEC] * 2, out_specs=[VMEM_SPEC] * 3,
        out_shape=[jax.ShapeDtypeStruct((T, 128), F32)] * 3,
        params=pltpu.CompilerParams(vmem_limit_bytes=VMEM_LIMIT))(pos, inv_freq)


def _rotate(r, c, sa, sb, sign):
    return r * c + sign * (pltpu.roll(r, 96, 1) * sa + pltpu.roll(r, 32, 1) * sb)


def _attn_pre(h2, g_kv, g_l1, g_kvl, g_ql, w_dkv, w_kr, w_uk, w_uv, w_dq, w_uq, tables):
    def body(h_ref, c_ref, sa_ref, sb_ref, gkv_ref, gl1_ref, gkvl_ref, gql_ref, wdkv_ref, wkr_ref, wuk_ref, wuv_ref,
             wdq_ref, wuq_ref, hk_ref, hn_ref, ckvr_ref, ckv_ref, kr_ref, kn_ref, v_ref, cqr_ref, cq_ref, q_ref):
        xv = h_ref[...]
        xn = xv * lax.rsqrt(jnp.mean(xv * xv, axis=-1, keepdims=True) + EPS)
        hk = (xn * gkv_ref[...]).astype(BF16)
        hn = (xn * gl1_ref[...]).astype(BF16)
        hk_ref[...] = hk
        hn_ref[...] = hn
        cv, sav, sbv = c_ref[...], sa_ref[...], sb_ref[...]
        raw = lax.dot_general(hk, wdkv_ref[...], NN, preferred_element_type=F32)
        ckvr_ref[...] = raw
        ckv = _rms(raw, gkvl_ref[...]).astype(BF16)
        ckv_ref[...] = ckv
        kr = lax.dot_general(hk, wkr_ref[...], NT, preferred_element_type=F32)
        kr_ref[...] = _rotate(kr, cv, sav, sbv, 1.0).astype(BF16)
        kn_ref[...] = lax.dot_general(ckv, wuk_ref[...], NN, preferred_element_type=F32).astype(BF16)
        v_ref[...] = lax.dot_general(ckv, wuv_ref[...], NN, preferred_element_type=F32).astype(BF16)
        cqr = lax.dot_general(hn, wdq_ref[...], NN, preferred_element_type=F32)
        cqr_ref[...] = cqr
        cq = _rms(cqr, gql_ref[...]).astype(BF16)
        cq_ref[...] = cq
        for h in range(N_HEADS):
            r = lax.dot_general(cq, wuq_ref[h], NT, preferred_element_type=F32)
            q_ref[h, :, :QK_NOPE] = (r[:, :QK_NOPE] * SCALE2).astype(BF16)
            q_ref[h, :, QK_NOPE:] = (_rotate(r[:, QK_NOPE:], cv, sav, sbv, 1.0) * SCALE2).astype(BF16)

    def rows(d):
        return pl.BlockSpec((TS, d), lambda i: (i, 0))

    def whole(a):
        return pl.BlockSpec(a.shape, lambda i: (0,) * a.ndim)

    wholes = [g_kv, g_l1, g_kvl, g_ql, w_dkv, w_kr, w_uk, w_uv, w_dq, w_uq]
    outs = [(D, BF16), (D, BF16), (KV_LORA, F32), (KV_LORA, BF16), (128, BF16), (N_HEADS * QK_NOPE, BF16),
            (N_HEADS * V_HEAD, BF16), (Q_LORA, F32), (Q_LORA, BF16)]
    return _pallas(
        body, name="attn_pre", grid=(T // TS,),
        in_specs=[rows(D), rows(128), rows(128), rows(128)] + [whole(a) for a in wholes],
        out_specs=[rows(d) for d, _ in outs] + [pl.BlockSpec((N_HEADS, TS, QK_PAD), lambda i: (0, i, 0))],
        out_shape=[jax.ShapeDtypeStruct((T, d), dt) for d, dt in outs]
        + [jax.ShapeDtypeStruct((N_HEADS, T, QK_PAD), BF16)],
        params=_params(("parallel",)))(h2, *tables, *wholes)


SCALE = (QK_NOPE + QK_ROPE) ** -0.5
LOG2E = 1.4426950408889634
SCALE2 = SCALE * LOG2E


def _diag_mask(transposed):
    shift = CHUNK.bit_length() - 1
    a = lax.broadcasted_iota(jnp.int32, (TQ, TQ), 0) >> shift
    b = lax.broadcasted_iota(jnp.int32, (TQ, TQ), 1) >> shift
    return (a <= b) if transposed else (b <= a)


def _as_row(col):
    return jnp.transpose(jnp.broadcast_to(col, (col.shape[0], 128)), (1, 0))[0:1]


def _attn_fwd(q, kn, kr, v):
    hp = 4

    def body(q_ref, kn_ref, kr_ref, v_ref, o_ref, lse_ref):
        i = pl.program_id(1)
        qs = [q_ref[a] for a in range(hp)]

        def step(j, carry, masked):
            off = pl.multiple_of(j * TQ, TQ)
            krv = kr_ref[pl.ds(off, TQ), :]
            ss = []
            for a in range(hp):
                kk = jnp.concatenate([kn_ref[pl.ds(off, TQ), a * QK_NOPE:(a + 1) * QK_NOPE], krv], axis=1)
                ss.append(lax.dot_general(qs[a], kk, NT, preferred_element_type=F32))
            out = []
            for a in range(hp):
                m, l, acc = carry[a]
                s = ss[a]
                if masked:
                    s = jnp.where(_diag_mask(False), s, NEG_INF)
                m_new = jnp.maximum(m, jnp.max(s, axis=-1, keepdims=True))
                p = jnp.exp2(s - m_new)
                alpha = jnp.exp2(m - m_new)
                l = alpha * l + jnp.sum(p, axis=-1, keepdims=True)
                pv = lax.dot_general(p.astype(BF16), v_ref[pl.ds(off, TQ), a * V_HEAD:(a + 1) * V_HEAD], NN,
                                     preferred_element_type=F32)
                out.append((m_new, l, alpha * acc + pv))
            return tuple(out)

        one = (jnp.full((TQ, 1), NEG_INF, F32), jnp.zeros((TQ, 1), F32), jnp.zeros((TQ, V_HEAD), F32))
        carry = lax.fori_loop(0, i, lambda j, cr: step(j, cr, False), (one,) * hp)
        carry = step(i, carry, True)
        for a, (m, l, acc) in enumerate(carry):
            o_ref[:, a * V_HEAD:(a + 1) * V_HEAD] = (acc / l).astype(BF16)
            lse_ref[a] = _as_row(m + jnp.log(l) * LOG2E)

    return _pallas(
        body, name="attn_fwd", grid=(N_HEADS // hp, T // TQ),
        in_specs=[pl.BlockSpec((hp, TQ, QK_PAD), lambda h, i: (h, i, 0)),
                  pl.BlockSpec((T, hp * QK_NOPE), lambda h, i: (0, h)),
                  pl.BlockSpec((T, 128), lambda h, i: (0, 0)),
                  pl.BlockSpec((T, hp * V_HEAD), lambda h, i: (0, h))],
        out_specs=[pl.BlockSpec((TQ, hp * V_HEAD), lambda h, i: (i, h)), pl.BlockSpec((hp, 1, TQ), lambda h, i: (h, 0, i))],
        out_shape=[jax.ShapeDtypeStruct((T, N_HEADS * V_HEAD), BF16), jax.ShapeDtypeStruct((N_HEADS, 1, T), F32)],
        params=_params(("parallel", "parallel")))(q, kn, kr, v)


def _attn_bwd(q, kn, kr, v, o, do, lse_row, tables):
    nq = T // TQ
    hp = 2
    cos, sa, sb = tables

    def body(q_ref, kn_ref, kr_ref, v_ref, o_ref, do_ref, lse_ref, c_ref, sa_ref, sb_ref,
             dq_ref, dkn_ref, dkr_ref, dv_ref, dq_acc, dl_ref):
        j = pl.program_id(1)

        def cols(a):
            return slice(a * 128, (a + 1) * 128)

        @pl.when(j == 0)
        def _():
            dq_acc[...] = jnp.zeros_like(dq_acc)
            for a in range(hp):
                for i in range(nq):
                    rows = pl.ds(i * TQ, TQ)
                    prod = do_ref[rows, cols(a)].astype(F32) * o_ref[rows, cols(a)].astype(F32)
                    dl_ref[a, :, rows] = _as_row(jnp.sum(prod, axis=-1, keepdims=True))

        krv = kr_ref[...]
        kks = [jnp.concatenate([kn_ref[:, cols(a)], krv], axis=1) for a in range(hp)]
        vvs = [v_ref[:, cols(a)] for a in range(hp)]

        def step(i, carry, masked):
            off = pl.multiple_of(i * TQ, TQ)
            rows = pl.ds(off, TQ)
            qis = [q_ref[a, rows, :] for a in range(hp)]
            dois = [do_ref[rows, cols(a)] for a in range(hp)]
            sts = [lax.dot_general(kks[a], qis[a], NT, preferred_element_type=F32) for a in range(hp)]
            dpts = [lax.dot_general(vvs[a], dois[a], NT, preferred_element_type=F32) for a in range(hp)]
            out = []
            for a in range(hp):
                dk, dv = carry[a]
                st = sts[a]
                if masked:
                    st = jnp.where(_diag_mask(True), st, NEG_INF)
                pt = jnp.exp2(st - lse_ref[a, :, rows])
                dv = dv + lax.dot_general(pt.astype(BF16), dois[a], NN, preferred_element_type=F32)
                dst = (pt * (dpts[a] - dl_ref[a, :, rows])).astype(BF16)
                dk = dk + lax.dot_general(dst, qis[a], NN, preferred_element_type=F32)
                dq_acc[a, rows, :] += lax.dot_general(dst, kks[a], TN, preferred_element_type=F32)
                out.append((dk, dv))
            return tuple(out)

        zero = (jnp.zeros((TQ, QK_PAD), F32), jnp.zeros((TQ, V_HEAD), F32))
        carry = step(j, (zero,) * hp, True)
        carry = lax.fori_loop(j + 1, nq, lambda i, cr: step(i, cr, False), carry)
        for a, (dk, dv) in enumerate(carry):
            dk = dk * (SCALE / SCALE2)
            dkn_ref[:, cols(a)] = dk[:, :QK_NOPE].astype(BF16)
            dkr_ref[a] = dk[:, QK_NOPE:]
            dv_ref[:, cols(a)] = dv.astype(BF16)

        @pl.when(j == nq - 1)
        def _():
            for a in range(hp):
                dq = dq_acc[a] * SCALE
                dq_ref[a, :, :QK_NOPE] = dq[:, :QK_NOPE].astype(BF16)
                dq_ref[a, :, QK_NOPE:] = _rotate(dq[:, QK_NOPE:], c_ref[...], sa_ref[...], sb_ref[...], -1.0).astype(BF16)

    row = pl.BlockSpec((hp, 1, T), lambda h, j: (h, 0, 0))
    head = pl.BlockSpec((TQ, hp * 128), lambda h, j: (j, h))
    whole = pl.BlockSpec((hp, T, QK_PAD), lambda h, j: (h, 0, 0))
    tab = pl.BlockSpec((T, 128), lambda h, j: (0, 0))
    heads = pl.BlockSpec((T, hp * V_HEAD), lambda h, j: (0, h))
    return _pallas(
        body, name="attn_bwd", grid=(N_HEADS // hp, nq),
        in_specs=[whole, head, pl.BlockSpec((TQ, 128), lambda h, j: (j, 0)), head, heads, heads, row, tab, tab, tab],
        out_specs=[whole, head, pl.BlockSpec((hp, TQ, 128), lambda h, j: (h, j, 0)), head],
        out_shape=[jax.ShapeDtypeStruct((N_HEADS, T, QK_PAD), BF16), jax.ShapeDtypeStruct((T, N_HEADS * QK_NOPE), BF16),
                   jax.ShapeDtypeStruct((N_HEADS, T, 128), F32), jax.ShapeDtypeStruct((T, N_HEADS * V_HEAD), BF16)],
        scratch_shapes=[pltpu.VMEM((hp, T, QK_PAD), F32), pltpu.VMEM((hp, 1, T), F32)],
        params=_params(("parallel", "arbitrary")))(q, kn, kr, v, o, do, lse_row, cos, sa, sb)


def _rms_bwd_math(xv, g, dy):
    r = lax.rsqrt(jnp.mean(xv * xv, axis=-1, keepdims=True) + EPS)
    xn = xv * r
    gdy = dy * g
    return r * (gdy - xn * jnp.mean(gdy * xn, axis=-1, keepdims=True)), jnp.sum(dy * xn, axis=0, keepdims=True)


def _attn_post(dq, dkn, dv, dkr, cq_raw, ckv_raw, h2, dres, g_ql, g_kvl, g_l1, g_kv, w_uq, w_uk, w_uv, w_dq, w_dkv,
               w_kr, tables):
    def body(dq_ref, dkn_ref, dv_ref, dkr_ref, cqr_ref, ckvr_ref, h_ref, res_ref, c_ref, sa_ref, sb_ref,
             gql_ref, gkvl_ref, gl1_ref, gkv_ref, wuq_ref, wuk_ref, wuv_ref, wdq_ref, wdkv_ref, wkr_ref,
             dcq_ref, dckv_ref, dkrr_ref, dh_ref, dhb_ref, dgql_ref, dgkvl_ref, dgl1_ref, dgkv_ref):
        dcq = lax.dot_general(dq_ref[0], wuq_ref[0], NN, preferred_element_type=F32)
        for h in range(1, N_HEADS):
            dcq = dcq + lax.dot_general(dq_ref[h], wuq_ref[h], NN, preferred_element_type=F32)
        dcq_raw, s_ql = _rms_bwd_math(cqr_ref[...], gql_ref[...], dcq)
        dcq_raw = dcq_raw.astype(BF16)
        dcq_ref[...] = dcq_raw
        dckv = (lax.dot_general(dkn_ref[...], wuk_ref[...], NT, preferred_element_type=F32)
                + lax.dot_general(dv_ref[...], wuv_ref[...], NT, preferred_element_type=F32))
        dckv_raw, s_kvl = _rms_bwd_math(ckvr_ref[...], gkvl_ref[...], dckv)
        dckv_raw = dckv_raw.astype(BF16)
        dckv_ref[...] = dckv_raw
        dkr = dkr_ref[0]
        for h in range(1, N_HEADS):
            dkr = dkr + dkr_ref[h]
        dkr_raw = _rotate(dkr, c_ref[...], sa_ref[...], sb_ref[...], -1.0).astype(BF16)
        dkrr_ref[...] = dkr_raw
        d_hn = lax.dot_general(dcq_raw, wdq_ref[...], NT, preferred_element_type=F32)
        d_hk = (lax.dot_general(dckv_raw, wdkv_ref[...], NT, preferred_element_type=F32)
                + lax.dot_general(dkr_raw, wkr_ref[...], NN, preferred_element_type=F32))
        xv = h_ref[...]
        r = lax.rsqrt(jnp.mean(xv * xv, axis=-1, keepdims=True) + EPS)
        xn = xv * r
        dx = res_ref[...]
        sums = [s_ql, s_kvl]
        for dy, g_ref in ((d_hn, gl1_ref), (d_hk, gkv_ref)):
            gdy = dy * g_ref[...]
            dx = dx + r * (gdy - xn * jnp.mean(gdy * xn, axis=-1, keepdims=True))
            sums.append(jnp.sum(dy * xn, axis=0, keepdims=True))
        dh_ref[...] = dx
        dhb_ref[...] = dx.astype(BF16)
        dg_refs = (dgql_ref, dgkvl_ref, dgl1_ref, dgkv_ref)

        @pl.when(pl.program_id(0) == 0)
        def _():
            for dg_ref, part in zip(dg_refs, sums):
                dg_ref[...] = part

        @pl.when(pl.program_id(0) > 0)
        def _():
            for dg_ref, part in zip(dg_refs, sums):
                dg_ref[...] += part

    def rows(d):
        return pl.BlockSpec((TS, d), lambda i: (i, 0))

    def heads(d):
        return pl.BlockSpec((N_HEADS, TS, d), lambda i: (0, i, 0))

    def whole(a):
        return pl.BlockSpec(a.shape, lambda i: (0,) * a.ndim)

    wholes = [g_ql, g_kvl, g_l1, g_kv, w_uq, w_uk, w_uv, w_dq, w_dkv, w_kr]
    vecs = [Q_LORA, KV_LORA, D, D]
    return _pallas(
        body, name="attn_post", grid=(T // TS,),
        in_specs=[heads(QK_PAD), rows(N_HEADS * QK_NOPE), rows(N_HEADS * V_HEAD), heads(128), rows(Q_LORA),
                  rows(KV_LORA), rows(D), rows(D), rows(128), rows(128), rows(128)] + [whole(a) for a in wholes],
        out_specs=[rows(Q_LORA), rows(KV_LORA), rows(128), rows(D), rows(D)]
        + [pl.BlockSpec((1, d), lambda i: (0, 0)) for d in vecs],
        out_shape=[jax.ShapeDtypeStruct((T, Q_LORA), BF16), jax.ShapeDtypeStruct((T, KV_LORA), BF16),
                   jax.ShapeDtypeStruct((T, 128), BF16), jax.ShapeDtypeStruct((T, D), F32),
                   jax.ShapeDtypeStruct((T, D), BF16)] + [jax.ShapeDtypeStruct((1, d), F32) for d in vecs],
        params=_params(("arbitrary",)))(dq, dkn, dv, dkr, cq_raw, ckv_raw, h2, dres, *tables, *wholes)


def _ffn_gup(name, dg, dv, hf):
    def body(dg_ref, dv_ref, hf_ref, o_ref):
        j = pl.program_id(0)

        @pl.when(j < N_FF_BLK)
        def _():
            o_ref[...] = lax.dot_general(dg_ref[...], hf_ref[...], TN, preferred_element_type=F32).astype(BF16)

        @pl.when(j >= N_FF_BLK)
        def _():
            o_ref[...] = lax.dot_general(dv_ref[...], hf_ref[...], TN, preferred_element_type=F32).astype(BF16)

    return _pallas(
        body, name=name, grid=(N_DEV,),
        in_specs=[pl.BlockSpec((None, T, FF_BLK), lambda j: (jnp.minimum(j, N_FF_BLK - 1), 0, 0)),
                  pl.BlockSpec((None, T, FF_BLK), lambda j: (jnp.maximum(j - N_FF_BLK, 0), 0, 0)),
                  pl.BlockSpec((T, D), lambda j: (0, 0))],
        out_specs=pl.BlockSpec((None, FF_BLK, D), lambda j: (j, 0, 0)),
        out_shape=jax.ShapeDtypeStruct((N_DEV, FF_BLK, D), BF16), params=_params(("parallel",)))(dg, dv, hf)


def _ffn_layer_fwd(tag, h, hf, ex, final=None):
    g, v, act = _ffn_up_act(f"{tag}_up", hf, ex.need(f"ffn_w_up{tag[1]}", hf), ex.need(f"ffn_cw{tag[1]}", hf),
                            ex.need(f"ffn_cb{tag[1]}", hf))
    ex.at(f"{tag}_up", act)
    if final is not None:
        return _down_final(act, ex.need(f"ffn_w_down{tag[1]}", act), h, *final), (hf, g, v, act)
    rows = pl.BlockSpec((TS, D), lambda i: (i, 0))
    out = _mm_sum(f"{tag}_down",
                  [(act, pl.BlockSpec((N_FF_BLK, TS, FF_BLK), lambda i: (0, i, 0)), ex.need(f"ffn_w_down{tag[1]}", act),
                    pl.BlockSpec((None, N_FF_BLK, FF_BLK, D), lambda i: (0, 0, 0, 0)), NN, 0)],
                  grid=(T // TS,), o_spec=rows, o_shape=(T, D), o_dtype=F32, add=h)
    ex.at(f"{tag}_down", out)
    return out, (hf, g, v, act)


def _ffn_layer_bwd(tag, h, gain, ex, saved, dh, dh_bf, post=None):
    hf, g, v, act = saved
    layer = tag[1]
    w_up, w_down4 = ex.need(f"ffn_w_up{layer}", dh_bf), ex.need(f"ffn_w_down{layer}", dh_bf)
    dg, dv, dcw, dcb = _ffn_dact(f"{tag}_dact", dh_bf, w_down4, g, v, ex.need(f"ffn_cw{layer}", dh_bf),
                                 ex.need(f"ffn_cb{layer}", dh_bf))
    ex.at(f"{tag}_dact", dg)
    g_down = _mm(f"{tag}_gdown", act, dh_bf, grid=(N_FF_BLK,),
                 a_spec=pl.BlockSpec((None, T, FF_BLK), lambda j: (j, 0, 0)),
                 b_spec=pl.BlockSpec((T, D), lambda j: (0, 0)),
                 o_spec=pl.BlockSpec((FF_BLK, D), lambda j: (j, 0)),
                 o_shape=(D_FF, D), o_dtype=BF16, dims=TN)
    g_up = _ffn_gup(f"{tag}_gup", dg, dv, hf)
    ex.grad("ffn_w_up", int(layer), g_up.reshape(1, N_DEV, FF_BLK, D))
    ex.grad("ffn_w_down", int(layer), g_down.reshape(1, N_DEV, D_FF // N_DEV, D))
    ex.at(f"{tag}_gup", g_up)
    part = pl.BlockSpec((N_FF_BLK, TR, FF_BLK), lambda i: (0, i, 0))
    dh_in, dh_in_bf, dgain, *onward = _mm_sum(
        f"{tag}_dhf",
        [(dg, part, w_up, pl.BlockSpec((None, N_FF_BLK, FF_BLK, D), lambda i: (0, 0, 0, 0)), NN, 0),
         (dv, part, w_up, pl.BlockSpec((None, N_FF_BLK, FF_BLK, D), lambda i: (0, 1, 0, 0)), NN, 0)],
        grid=(T // TR,), o_spec=pl.BlockSpec((TR, D), lambda i: (i, 0)), o_shape=(T, D), o_dtype=F32,
        norm_bwd=(h, [gain], dh), post=post)
    ex.at(f"{tag}_dhf", dh_in)
    return (dh_in, dh_in_bf, dgain[0], dcw, dcb, *onward)


def _local_step(x, pos, tgt, rep, ex):
    attn_norm, ffn_norm, final_norm = rep["attn_norm"], rep["ffn_norm"], rep["final_norm"]
    half = QK_ROPE // 2
    inv = 1.0 / (ROPE_THETA ** (jnp.arange(half, dtype=F32) / half))
    inv_freq = jnp.concatenate([inv, inv, jnp.zeros((128 - 2 * half,), F32)]).reshape(1, 128)
    tables = _rope_tables(pos, inv_freq)

    hn0 = _rms_fwd("l0_norm", x, attn_norm[0:1])
    ex.at("l0_norm", hn0)
    w_in = ex.need("sc_w_in", hn0)
    zb, zc, zu, y = _mixer_in(hn0, w_in, ex.need("sc_conv_w", hn0))
    ex.at("l0_in", y)
    h1 = _mm_rows("l0_out", y, ex.need("sc_w_out", y), NN, F32, D, tn=512, add=x)
    ex.at("l0_out", h1)
    h2, ffn0 = _ffn_layer_fwd("f0", h1, _rms_fwd("f0_norm", h1, ffn_norm[0:1]), ex)

    w_uq = ex.need("w_uq", h2)
    hk, hn1, ckv_raw, ckv, kr, kn, vv, cq_raw, cq, q = _attn_pre(
        h2, rep["kv_in_norm"], attn_norm[1:2], rep["kv_latent_norm"], rep["q_latent_norm"], ex.need("w_dkv", h2),
        ex.need("w_kr", h2), ex.need("w_uk", h2), ex.need("w_uv", h2), ex.need("w_dq", h2), w_uq, tables)

    o, lse = _attn_fwd(q, kn, kr, vv)
    ex.at("attn_fwd", o)
    w_o = ex.need("w_o", o)
    h3, hf1 = _out_norm("attn_out", o, w_o, h2, ffn_norm[1:2])
    (loss, dh4, dh4_bf, d_final), ffn1 = _ffn_layer_fwd("f1", h3, hf1, ex, final=(final_norm.reshape(1, D), tgt))

    dh3, dh3_bf, d_fn1, dcw1, dcb1, do = _ffn_layer_bwd("f1", h3, ffn_norm[1:2], ex, ffn1, dh4, dh4_bf,
                                                        post=(w_o, NT, N_HEADS * V_HEAD))
    ex.at("f1_bwd", dh3)

    dq_pre, dkn, dkr, dvv = _attn_bwd(q, kn, kr, vv, o, do, lse, tables)

    dcq_raw_bf, dckv_raw_bf, dkr_raw_bf, dh2, dh2_bf, d_qln, d_kvln, d_an1, d_kvin = _attn_post(
        dq_pre, dkn, dvv, dkr, cq_raw, ckv_raw, h2, dh3, rep["q_latent_norm"], rep["kv_latent_norm"], attn_norm[1:2],
        rep["kv_in_norm"], w_uq, ex.need("w_uk", dkn), ex.need("w_uv", dvv), ex.need("w_dq", dq_pre),
        ex.need("w_dkv", dkn), ex.need("w_kr", dkr), tables)
    g_uq, g_dq, g_o = _wgrads("g_q", [(dq_pre, cq, QK_NOPE + QK_ROPE), (hn1, dcq_raw_bf), (o, dh3_bf)])
    ex.grad("w_uq", None, g_uq.reshape(1, N_DEV, QK_NOPE + QK_ROPE, Q_LORA))
    ex.grad("w_dq", None, g_dq.reshape(1, N_DEV, D // N_DEV, Q_LORA))
    ex.grad("w_o", None, g_o.reshape(1, N_DEV, D // N_DEV, D))

    g_uk, g_uv, g_dkv, g_kr = _wgrads("g_kv", [(ckv, dkn), (ckv, dvv), (hk, dckv_raw_bf), (dkr_raw_bf, hk, QK_ROPE)])
    ex.grad("w_uk", None, g_uk)
    ex.grad("w_uv", None, g_uv)
    ex.grad("w_dkv", None, g_dkv.reshape(1, N_DEV, D // N_DEV, KV_LORA))
    ex.grad("w_kr", None, g_kr)
    ex.at("kv_bwd", dh2)

    dh1, dh1_bf, d_fn0, dcw0, dcb0 = _ffn_layer_bwd("f0", h1, ffn_norm[0:1], ex, ffn0, dh2, dh2_bf)
    ex.at("f0_bwd", dh1)

    ex.grad("sc_w_out", None, _mm_wgrad("g_sc_w_out", y, dh1_bf).reshape(1, N_DEV, D // N_DEV, D))
    dz, d_scw = _mixer_out_bwd(dh1_bf, ex.need("sc_w_out", dh1_bf), zb, zc, zu, ex.need("sc_conv_w", dh1_bf))
    g_in = _mm_wgrad("g_sc_w_in", hn0, dz)
    ex.grad("sc_w_in", None, g_in)
    ex.at("sc_bwd", g_in)
    ex.at("d_l0_in", g_in)
    w_in = ex.need("sc_w_in", dz)
    grad_x, _, (d_an0,) = _mm_sum(
        "d_l0_in", [(dz[None], pl.BlockSpec((1, TS, dz.shape[1]), lambda i: (0, i, 0)),
                     w_in[None], pl.BlockSpec((1,) + w_in.shape, lambda i: (0, 0, 0)), NT, 0)],
        norm_bwd=(x, [attn_norm[0:1]], dh1),
        grid=(T // TS,), o_spec=pl.BlockSpec((TS, D), lambda i: (i, 0)), o_shape=(T, D), o_dtype=F32)

    small = {
        "attn_norm": jnp.concatenate([d_an0, d_an1], axis=0),
        "ffn_norm": jnp.concatenate([d_fn0, d_fn1], axis=0),
        "final_norm": d_final.reshape(D),
        "kv_in_norm": d_kvin.reshape(D),
        "kv_latent_norm": d_kvln.reshape(KV_LORA),
        "q_latent_norm": d_qln,
        "ffn_conv_b": jnp.stack([dcb0, dcb1]).transpose(0, 2, 1, 3).reshape(2, D_FF),
        "sc_conv_w": d_scw,
        "ffn_conv_w": jnp.stack([dcw0, dcw1]).transpose(0, 2, 1, 3).reshape(2, 3, D_FF),
    }
    return loss, grad_x, small


def _place():
    return lax.axis_index("x"), lax.axis_index("y"), lax.axis_index("c")


def _peers():
    x, y, c = _place()
    return (x, y, 1 - c), [(1 - x, y), (x, 1 - y), (1 - x, 1 - y)]


def _window(ref, kind, dev):
    if kind == "blocked":
        return ref.at[:, dev]
    width = ref.shape[-1] // N_DEV
    return ref.at[:, pl.ds(pl.multiple_of(dev * width, 128), width)]


HBM_SPEC = pl.BlockSpec(memory_space=pltpu.HBM)
SEM_SPEC = pl.BlockSpec(memory_space=pltpu.SEMAPHORE)
EFFECT = pltpu.SideEffectType.DATAFLOW_SIDE_EFFECTING
TOKEN = jax.ShapeDtypeStruct((8, 128), F32)


def _hbm(a):
    return pltpu.with_memory_space_constraint(a, pltpu.HBM)


def _copies_start(name, jobs):
    nj = len(jobs)
    counts = [(len(srcs), len(lands)) for srcs, lands, _, _ in jobs]
    n_arr = sum(ns + nl for ns, nl in counts)

    def body(*refs):
        sems, token = refs[n_arr:n_arr + 2 * nj], refs[-1]
        at = 0
        for j, ((ns, nl), (_, _, ncopy, plan)) in enumerate(zip(counts, jobs)):
            copies = plan(refs[at:at + ns], refs[at + ns:at + ns + nl])
            assert len(copies) == ncopy
            for k, (sent, dst, to, _) in enumerate(copies):
                pltpu.make_async_remote_copy(src_ref=sent, dst_ref=dst, send_sem=sems[2 * j].at[k],
                                             recv_sem=sems[2 * j + 1].at[k], device_id=to, device_id_type=MESH).start()
            at += ns + nl
        token[...] = jnp.zeros_like(token)

    arrays = [a for srcs, lands, _, _ in jobs for a in list(srcs) + list(lands)]
    sem_shapes = [pltpu.SemaphoreType.DMA((ncopy,)) for _, _, ncopy, _ in jobs for _ in range(2)]
    outs = pl.pallas_call(
        body, name=name, in_specs=[HBM_SPEC] * n_arr,
        out_specs=[SEM_SPEC] * (2 * nj) + [HBM_SPEC] * n_arr + [VMEM_SPEC],
        out_shape=sem_shapes + [pltpu.HBM(a.shape, a.dtype) for a in arrays] + [TOKEN],
        input_output_aliases={i: 2 * nj + i for i in range(n_arr)},
        compiler_params=pltpu.CompilerParams(has_side_effects=EFFECT))(*[_hbm(a) for a in arrays])
    _Chain.last = outs[-1]
    flights, at = [], 2 * nj
    for j, (ns, nl) in enumerate(counts):
        flights.append((outs[2 * j], outs[2 * j + 1], list(outs[at:at + ns]), list(outs[at + ns:at + ns + nl])))
        at += ns + nl
    return flights


def _copies_wait(name, started, ncopy, plan):
    send, recv, srcs, lands = started
    ns, nl = len(srcs), len(lands)

    def body(*refs):
        send_ref, recv_ref, token = refs[ns + nl], refs[ns + nl + 1], refs[-1]
        copies = plan(refs[:ns], refs[ns:ns + nl])
        assert len(copies) == ncopy
        for k, (sent, _, to, landed) in enumerate(copies):
            cp = pltpu.make_async_remote_copy(src_ref=sent, dst_ref=landed, send_sem=send_ref.at[k],
                                              recv_sem=recv_ref.at[k], device_id=to, device_id_type=MESH)
            cp.wait_send()
            cp.wait_recv()
        token[...] = jnp.zeros_like(token)

    arrays = list(srcs) + list(lands)
    outs = pl.pallas_call(
        body, name=name, in_specs=[HBM_SPEC] * (ns + nl) + [SEM_SPEC] * 2 + [ANY_SPEC],
        out_specs=[HBM_SPEC] * (ns + nl) + [VMEM_SPEC], out_shape=[pltpu.HBM(a.shape, a.dtype) for a in arrays] + [TOKEN],
        input_output_aliases={i: i for i in range(ns + nl)},
        compiler_params=pltpu.CompilerParams(has_side_effects=EFFECT))(*arrays, send, recv, _Chain.last)
    _Chain.last = outs[-1]
    return list(outs[:ns]), list(outs[ns:-1])


def _plan_gather_chips(kinds):
    def plan(srcs, lands):
        x, y, c = _place()
        sibling, chips = _peers()
        out = []
        for t, kind in enumerate(kinds):
            mine = _window(lands[t], kind, 4 * x + 2 * y + c)
            out.append((srcs[t], mine, (x, y, c), mine))
            out.append((srcs[t], mine, sibling, _window(lands[t], kind, 4 * x + 2 * y + 1 - c)))
            for px, py in chips:
                out.append((srcs[t], mine, (px, py, c), _window(lands[t], kind, 4 * px + 2 * py + c)))
        return out
    return plan, 5 * len(kinds)


def _plan_gather_all(n):
    def plan(srcs, lands):
        x, y, c = _place()
        out = []
        for t in range(n):
            mine = lands[t].at[:, 4 * x + 2 * y + c]
            for m in range(N_DEV):
                px, py, pc = (1 - x if m & 4 else x), (1 - y if m & 2 else y), (1 - c if m & 1 else c)
                out.append((srcs[t], mine, (px, py, pc), lands[t].at[:, 4 * px + 2 * py + pc]))
        return out
    return plan, N_DEV * n


def _plan_gather_sibling(kinds):
    def plan(srcs, lands):
        _, _, c = _place()
        sibling, chips = _peers()
        out = []
        for t, kind in enumerate(kinds):
            for px, py in chips:
                w = _window(lands[t], kind, 4 * px + 2 * py + c)
                out.append((w, w, sibling, _window(lands[t], kind, 4 * px + 2 * py + 1 - c)))
        return out
    return plan, 3 * len(kinds)


def _plan_scatter_sibling(kinds):
    def plan(srcs, lands):
        _, _, c = _place()
        sibling, _ = _peers()
        out = []
        for t, kind in enumerate(kinds):
            for k in range(N_CHIP):
                out.append((_window(srcs[t], kind, 2 * k + 1 - c), lands[t].at[k], sibling, lands[t].at[k]))
        return out
    return plan, N_CHIP * len(kinds)


def _plan_scatter_chips(n):
    def plan(srcs, lands):
        x, y, c = _place()
        _, chips = _peers()
        out = []
        for t in range(n):
            for px, py in chips:
                out.append((srcs[t].at[2 * px + py], lands[t].at[2 * x + y], (px, py, c), lands[t].at[2 * px + py]))
        return out
    return plan, 3 * n


def _landing(shard, kind):
    if kind == "blocked":
        return lax.empty((shard.shape[0], N_DEV) + shard.shape[1:], shard.dtype)
    return lax.empty((shard.shape[0], N_DEV * shard.shape[1]), shard.dtype)


def _chip_sums(name, grads, kinds, recvs, c):
    n = len(grads)
    in_specs, out_specs, out_shape, args = [], [], [], []
    for gr, kind, rv in zip(grads, kinds, recvs):
        if kind == "blocked":
            rows, w = gr.shape[2], gr.shape[3]
            in_specs.append(pl.BlockSpec((None, None, rows, w), lambda k, cref: (0, 2 * k + cref[0], 0, 0)))
        else:
            rows, w = gr.shape[0], gr.shape[1] // N_DEV
            in_specs.append(pl.BlockSpec((rows, w), lambda k, cref: (0, 2 * k + cref[0])))
        blk = pl.BlockSpec((None, rows, w), lambda k, cref: (k, 0, 0))
        in_specs.append(blk)
        out_specs.append(blk)
        out_shape.append(jax.ShapeDtypeStruct((N_CHIP, rows, w), BF16))
        args += [gr, rv.reshape(N_CHIP, rows, w)]

    def body(*refs):
        for t in range(n):
            g_ref, r_ref, o_ref = refs[1 + 2 * t], refs[2 + 2 * t], refs[1 + 2 * n + t]
            o_ref[...] = (g_ref[...].astype(F32) + r_ref[...].astype(F32)).astype(BF16)

    return _pallas(body, name=name, n_prefetch=1, grid=(N_CHIP,), in_specs=in_specs, out_specs=out_specs,
                   out_shape=out_shape, params=_params(("parallel",)))(c, *args)


def _adamw_math(g, wv, mv, vv):
    m = ADAM_B1 * mv + (1.0 - ADAM_B1) * g
    v = ADAM_B2 * vv + (1.0 - ADAM_B2) * (g * g)
    m_hat = m / (1.0 - ADAM_B1 ** ADAM_STEP)
    v_hat = v / (1.0 - ADAM_B2 ** ADAM_STEP)
    delta = -ADAM_LR * (m_hat / (jnp.sqrt(v_hat) + ADAM_EPS) + ADAM_WD * wv)
    return delta, m, v


ADAM_STEPS = 2


def _adamw_group(name, items, chip_ids):
    n = len(items)
    in_specs, out_specs, out_shape, args, prevs = [], [], [], [chip_ids], []
    for own, recv, w3, m3, v3, layer, _ in items:
        nl, rows, w = w3.shape
        tr = rows // ADAM_STEPS
        assert tr % 16 == 0, (name, rows)
        in_specs += [pl.BlockSpec((None, tr, w), lambda i, ids, slot=slot: (ids[slot], i, 0)) for slot in range(4)]
        slab = pl.BlockSpec((None, tr, w), lambda i, ids, layer=layer: (layer, i, 0))
        in_specs += [slab] * 3
        out_specs += [slab] * 4
        out_shape += [jax.ShapeDtypeStruct((nl, rows, w), F32)] * 4
        args += [own, recv, recv, recv, w3, m3, v3]
    aliases = {}
    for t, item in enumerate(items):
        if item[6] is not None:
            for k in range(4):
                aliases[len(args) + k] = 4 * t + k
            in_specs += [ANY_SPEC] * 4
            args += list(item[6])
            prevs.append(t)
    n_in = 1 + 7 * n + 4 * len(prevs)

    def body(*refs):
        for t in range(n):
            own_ref, r1_ref, r2_ref, r3_ref, w_ref, m_ref, v_ref = refs[1 + 7 * t:8 + 7 * t]
            g_ref, d_ref, nm_ref, nv_ref = refs[n_in + 4 * t:n_in + 4 * t + 4]
            g = ((own_ref[...].astype(F32) + r1_ref[...].astype(F32)) + r2_ref[...].astype(F32)) + r3_ref[...].astype(F32)
            g_ref[...] = g
            d_ref[...], nm_ref[...], nv_ref[...] = _adamw_math(g, w_ref[...], m_ref[...], v_ref[...])

    outs = _pallas(body, name=name, n_prefetch=1, grid=(ADAM_STEPS,), in_specs=in_specs, out_specs=out_specs,
                   out_shape=out_shape, aliases=aliases, params=_params(("parallel",)))(*args)
    return [list(outs[4 * t:4 * t + 4]) for t in range(n)]


SHARD_ROWS = 8


def _adamw_small(gathered, ws, ms, vs, me):
    n = len(gathered)
    full = [w is not None for w in ws]
    sharded = [w is not None and w.ndim == 3 for w in ws]
    args = list(gathered)
    out_shape = []
    for t in range(n):
        shape = jax.ShapeDtypeStruct(ws[t].shape if sharded[t] else gathered[t].shape[2:], F32)
        if full[t]:
            args += [ws[t], ms[t], vs[t]]
            out_shape += [shape] * 4
        else:
            out_shape += [shape]

    def body(*refs):
        i_in, i_out = n, len(args) + 1
        me_ref = refs[len(args)]
        for t in range(n):
            p_ref = refs[t]
            if sharded[t]:
                w_ref, m_ref, v_ref = refs[i_in:i_in + 3]
                taps, layers, _ = w_ref.shape
                mine = pl.ds(pl.multiple_of(me_ref[0] * SHARD_ROWS, SHARD_ROWS), SHARD_ROWS)
                g = p_ref[0, 0, mine, :]
                for k in range(1, N_DEV):
                    g = g + p_ref[0, k, mine, :]
                for l in range(layers):
                    for k in range(taps):
                        at = (k, slice(l, l + 1), slice(None))
                        row = g[l * taps + k:l * taps + k + 1]
                        refs[i_out][at] = row
                        refs[i_out + 1][at], refs[i_out + 2][at], refs[i_out + 3][at] = _adamw_math(
                            row, w_ref[at], m_ref[at], v_ref[at])
                i_in += 3
                i_out += 4
                continue
            g = p_ref[0, 0]
            for k in range(1, N_DEV):
                g = g + p_ref[0, k]
            refs[i_out][...] = g
            if full[t]:
                w_ref, m_ref, v_ref = refs[i_in:i_in + 3]
                refs[i_out + 1][...], refs[i_out + 2][...], refs[i_out + 3][...] = _adamw_math(
                    g, w_ref[...], m_ref[...], v_ref[...])
                i_in += 3
                i_out += 4
            else:
                i_out += 1

    outs = _pallas(body, name="adamw_small",
                   in_specs=[VMEM_SPEC] * len(args) + [pl.BlockSpec(memory_space=pltpu.SMEM)],
                   out_specs=[VMEM_SPEC] * len(out_shape), out_shape=out_shape,
                   params=pltpu.CompilerParams(vmem_limit_bytes=VMEM_LIMIT))(*args, me)
    result, i = [], 0
    for t in range(n):
        k = 4 if full[t] else 1
        result.append(list(outs[i:i + k]))
        i += k
    return result


KIND = {"sc_w_in": "cols", "sc_w_out": "blocked", "w_dkv": "blocked", "w_kr": "cols", "w_uk": "cols", "w_uv": "cols",
        "w_dq": "blocked", "w_uq": "blocked", "w_o": "blocked", "ffn_w_up": "blocked", "ffn_w_down": "blocked",
        "conv": "blocked"}
GATHER_GROUPS = (("mixer", ("sc_w_in",)),
                 ("mixer2", ("sc_w_out", "conv")),
                 ("up0", ("ffn_w_up0",)),
                 ("down0", ("ffn_w_down0",)),
                 ("attn", ("w_dkv", "w_kr", "w_uk", "w_uv", "w_dq", "w_uq", "w_o")),
                 ("ffn1", ("ffn_w_up1", "ffn_w_down1")))
SCATTER_GROUPS = (("ffn1", (("ffn_w_up", 1), ("ffn_w_down", 1))),
                  ("attn", (("w_o", None), ("w_uq", None), ("w_dq", None), ("w_uk", None), ("w_uv", None),
                            ("w_dkv", None), ("w_kr", None))),
                  ("ffn0", (("ffn_w_up", 0), ("ffn_w_down", 0))),
                  ("mixer", (("sc_w_out", None), ("sc_w_in", None))))
SCHEDULE = {
    "begin": (("gather_start", "mixer"),),
    "cast": (("gather_start", "mixer2"),),
    "l0_norm": (("gather_forward", "mixer"), ("gather_forward", "mixer2"), ("gather_start", "up0")),
    "l0_out": (("gather_forward", "up0"), ("gather_start", "down0"), ("gather_start", "attn")),
    "f0_up": (("gather_forward", "down0"), ("gather_start", "ffn1")),
    "f0_down": (("gather_forward", "attn"),),
    "attn_fwd": (("gather_forward", "ffn1"),),
    "f1_gup": (("scatter_sibling", "ffn1"),),
    "f1_dhf": (("scatter_chips", "ffn1"),),
    "kv_bwd": (("scatter_sibling", "attn"),),
    "f0_dact": (("scatter_chips", "attn"),),
    "f0_gup": (("scatter_sibling", "ffn0"),),
    "f0_dhf": (("scatter_chips", "ffn0"),),
    "sc_bwd": (("scatter_sibling", "mixer"), ("scatter_done", "attn")),
    "d_l0_in": (("scatter_chips", "mixer"),),
}
FINISH = (("scatter_done", "ffn1"), ("scatter_done", "ffn0"), ("scatter_done", "mixer"))
STAGES = {"gather_start": 1, "gather_forward": 2, "gather_done": 3,
          "scatter_sibling": 1, "scatter_chips": 2, "scatter_done": 3}
SMALL_W_ROWS = 24


def _pack(arrays, rows):
    flat = jnp.concatenate([a.reshape(-1).astype(F32) for a in arrays])
    return jnp.pad(flat, (0, rows * 128 - flat.shape[0])).reshape(rows, 128)


def _cast_shards(items):
    arrays = []
    for a, _, _ in items:
        if not any(a is b for b in arrays):
            arrays.append(a)
    slot = [next(i for i, b in enumerate(arrays) if b is a) for a, _, _ in items]

    def body(*refs):
        for t, (a, layer, rows) in enumerate(items):
            w_ref, o_ref = refs[slot[t]], refs[len(arrays) + t]
            r, c = a.shape[-2:]
            if a.ndim == 3:
                o_ref[:, :r] = w_ref[(layer or 0):(layer or 0) + 1].astype(BF16)
                if rows > r:
                    o_ref[:, r:] = jnp.zeros((1, rows - r, c), BF16)
            else:
                o_ref[:r] = w_ref[...].astype(BF16)
                if rows > r:
                    o_ref[r:] = jnp.zeros((rows - r, c), BF16)

    out_shape = [jax.ShapeDtypeStruct(((1,) if a.ndim == 3 else ()) + (rows, a.shape[-1]), BF16)
                 for a, _, rows in items]
    return _pallas(body, name="cast_shards", in_specs=[VMEM_SPEC] * len(arrays), out_specs=[VMEM_SPEC] * len(items),
                   out_shape=out_shape, params=pltpu.CompilerParams(vmem_limit_bytes=VMEM_LIMIT))(*arrays)


STORED_TRANSPOSED = ("ffn_w_up", "w_uq", "w_kr")


def _stored(name, a):
    return jnp.swapaxes(a, -1, -2) if name in STORED_TRANSPOSED else a


def _base(name):
    if name.startswith("ffn_w_") and name[-1] in "01":
        return name[:-1], int(name[-1])
    return name, None


class _Exchange:
    def __init__(self, wts, mom, var, ffn_conv_b):
        self.wts, self.mom, self.var = wts, mom, var
        x, y, c = _place()
        self.c_arr = jnp.reshape(c, (1,)).astype(jnp.int32)
        chip = 2 * x + y
        self.chip_ids = jnp.stack([chip, chip ^ 1, chip ^ 2, chip ^ 3]).astype(jnp.int32)
        self.ready = {"ffn_cb0": ffn_conv_b.reshape(2, N_FF_BLK, 1, FF_BLK)[0],
                      "ffn_cb1": ffn_conv_b.reshape(2, N_FF_BLK, 1, FF_BLK)[1]}
        self.gathers, self.group_of = {}, {}
        self.grads, self.scatters, self.results, self.queue = {}, {}, {}, []
        for gname, names in GATHER_GROUPS:
            self.gathers[gname] = dict(stage=0, names=names, kinds=[KIND[_base(nm)[0]] for nm in names])
            for nm in names:
                self.group_of[nm] = gname
        for nm in ("sc_conv_w", "ffn_cw0", "ffn_cw1"):
            self.group_of[nm] = self.group_of["conv"]
        self.cast, self.f32 = {}, {}
        self.at("begin", None)
        later = [nm for gname, names in GATHER_GROUPS[1:] for nm in names if nm != "conv"]
        self.cast = dict(zip(later, _cast_shards([self._shard_f32(nm) for nm in later])))
        self.at("cast", None)

    def _shard_f32(self, name):
        base, layer = _base(name)
        if base not in self.f32:
            a = _stored(base, self.wts[base])
            self.f32[base] = a.reshape(a.shape[-2:]) if KIND[base] == "cols" else a.reshape((-1,) + a.shape[-2:])
        a = self.f32[base]
        return a, layer, {"w_kr": 128, "w_uq": QK_PAD}.get(base, a.shape[-2])

    def _shard(self, name):
        if name in self.cast:
            return self.cast[name]
        if name == "conv":
            return _pack([self.wts["sc_conv_w"], self.wts["ffn_conv_w"]], SMALL_W_ROWS).reshape(1, SMALL_W_ROWS, 128)
        base, layer = _base(name)
        a = _stored(base, self.wts[base])
        if layer is not None:
            a = a[layer:layer + 1]
        if KIND[base] == "cols":
            return a.reshape(a.shape[-2], a.shape[-1]).astype(BF16)
        return a.reshape((-1,) + a.shape[-2:]).astype(BF16)

    def _start(self, name, srcs, lands, ncopy, plan, st):
        self.queue.append((name, (srcs, lands, ncopy, plan), st))

    def _flush(self):
        if self.queue:
            flights = _copies_start("__".join(name for name, _, _ in self.queue), [job for _, job, _ in self.queue])
            for (_, _, st), flight in zip(self.queue, flights):
                st["flight"] = flight
            self.queue = []

    def _flight(self, st):
        self._flush()
        return st["flight"]

    def _gather_to(self, gname, stage, after):
        st = self.gathers[gname]
        if st["stage"] < 1 <= stage:
            shards = [self._shard(nm) for nm in st["names"]]
            lands = [_landing(s, kind) for s, kind in zip(shards, st["kinds"])]
            plan, ncopy = _plan_gather_chips(st["kinds"])
            self._start(f"ag_{gname}_chips", shards, lands, ncopy, plan, st)
            st["stage"] = 1
        if st["stage"] < 2 <= stage:
            plan, ncopy = _plan_gather_chips(st["kinds"])
            _, lands = _copies_wait(f"ag_{gname}_chips_wait", self._flight(st), ncopy, plan)
            plan, ncopy = _plan_gather_sibling(st["kinds"])
            self._start(f"ag_{gname}_sibling", [], lands, ncopy, plan, st)
            st["stage"] = 2
        if st["stage"] < 3 <= stage:
            plan, ncopy = _plan_gather_sibling(st["kinds"])
            _, lands = _copies_wait(f"ag_{gname}_sibling_wait", self._flight(st), ncopy, plan)
            for nm, land in zip(st["names"], lands):
                self._arrived(nm, land)
            st["stage"] = 3

    def _arrived(self, name, land):
        if name == "conv":
            conv = land.reshape(N_DEV, SMALL_W_ROWS * 128)
            self.ready["sc_conv_w"] = conv[:, :3 * 128].reshape(N_DEV, 3, 128).transpose(1, 0, 2).reshape(3, D)
            fcw = conv[:, 3 * 128:3 * 128 + 6 * 352].reshape(N_DEV, 2, 3, 352).transpose(1, 2, 0, 3)
            fcw = fcw.reshape(2, 3, N_FF_BLK, FF_BLK).transpose(0, 2, 1, 3)
            self.ready["ffn_cw0"], self.ready["ffn_cw1"] = fcw[0], fcw[1]
        elif name in ("sc_w_in", "w_uk", "w_uv", "w_kr") or name.startswith("ffn_w_up"):
            self.ready[name] = land
        elif name.startswith("ffn_w_down"):
            self.ready[name] = land.reshape(1, N_FF_BLK, FF_BLK, D)
        elif name == "w_uq":
            self.ready[name] = land.reshape(N_HEADS, QK_PAD, Q_LORA)
        else:
            self.ready[name] = land.reshape(D, land.shape[-1])

    def need(self, name, after):
        if name not in self.ready:
            self._gather_to(self.group_of[name], 3, after)
            self._flush()
        return self.ready[name]

    def grad(self, name, layer, array):
        self.grads[(name, layer)] = array

    def _scatter_to(self, gname, stage, after):
        keys = dict(SCATTER_GROUPS)[gname]
        st = self.scatters.setdefault(gname, dict(stage=0))
        kinds = [KIND[nm] for nm, _ in keys]
        if st["stage"] < 1 <= stage:
            grads = [self.grads[key] for key in keys]
            lands = []
            for gr, kind in zip(grads, kinds):
                shard = (gr.shape[0],) + gr.shape[2:] if kind == "blocked" else (gr.shape[0], gr.shape[1] // N_DEV)
                lands.append(lax.empty((N_CHIP,) + shard, BF16))
            plan, ncopy = _plan_scatter_sibling(kinds)
            self._start(f"rs_{gname}_sibling", grads, lands, ncopy, plan, st)
            st["stage"] = 1
        if st["stage"] < 2 <= stage:
            plan, ncopy = _plan_scatter_sibling(kinds)
            grads, recvs = _copies_wait(f"rs_{gname}_sibling_wait", self._flight(st), ncopy, plan)
            sums = _chip_sums(f"rs_{gname}_sums", grads, kinds, recvs, self.c_arr)
            lands = [lax.empty(s.shape, BF16) for s in sums]
            plan, ncopy = _plan_scatter_chips(len(sums))
            self._start(f"rs_{gname}_chips", sums, lands, ncopy, plan, st)
            st["stage"] = 2
        if st["stage"] < 3 <= stage:
            plan, ncopy = _plan_scatter_chips(len(keys))
            sums, recvs = _copies_wait(f"rs_{gname}_chips_wait", self._flight(st), ncopy, plan)
            items = []
            for (nm, layer), own, rv in zip(keys, sums, recvs):
                nl = 1 if layer is None else 2
                rows, w = own.shape[1], own.shape[2]
                w3, m3, v3 = (_stored(nm, src[nm]).reshape(nl, rows, w) for src in (self.wts, self.mom, self.var))
                items.append((own, rv, w3, m3, v3, 0 if layer is None else layer, self.results.get(nm)))
            outs = _adamw_group(f"adamw_{gname}", items, self.chip_ids)
            for (nm, _), out in zip(keys, outs):
                self.results[nm] = out
            st["stage"] = 3

    def at(self, place, after):
        for action, gname in SCHEDULE.get(place, ()):
            self._advance(action, gname, after)
        self._flush()

    def _advance(self, action, gname, after):
        if action.startswith("gather"):
            self._gather_to(gname, STAGES[action], after)
        else:
            self._scatter_to(gname, STAGES[action], after)

    def finish(self, after):
        for action, gname in FINISH:
            self._advance(action, gname, after)
        for gname, _ in SCATTER_GROUPS:
            self._scatter_to(gname, 3, after)
        return {nm: [_stored(nm, o.reshape(_stored(nm, self.wts[nm]).shape)) for o in outs]
                for nm, outs in self.results.items()}


REPLICATED = ("attn_norm", "ffn_norm", "final_norm", "kv_in_norm", "kv_latent_norm", "q_latent_norm", "ffn_conv_b")
WEIGHTS = ("attn_norm", "ffn_norm", "final_norm", "sc_w_in", "sc_conv_w", "sc_w_out", "kv_in_norm", "w_dkv",
           "kv_latent_norm", "w_kr", "w_uk", "w_uv", "w_dq", "q_latent_norm", "w_uq", "w_o", "ffn_w_up", "ffn_conv_w",
           "ffn_conv_b", "ffn_w_down")


def kernel(x, positions, attn_norm, ffn_norm, final_norm, sc_w_in, sc_conv_w, sc_w_out, kv_in_norm, w_dkv, kv_latent_norm, w_kr, w_uk, w_uv, w_dq, q_latent_norm, w_uq, w_o, ffn_w_up, ffn_conv_w, ffn_conv_b, ffn_w_down, loss_target, m_attn_norm, m_ffn_norm, m_final_norm, m_sc_w_in, m_sc_conv_w, m_sc_w_out, m_kv_in_norm, m_w_dkv, m_kv_latent_norm, m_w_kr, m_w_uk, m_w_uv, m_w_dq, m_q_latent_norm, m_w_uq, m_w_o, m_ffn_w_up, m_ffn_conv_w, m_ffn_conv_b, m_ffn_w_down, v_attn_norm, v_ffn_norm, v_final_norm, v_sc_w_in, v_sc_conv_w, v_sc_w_out, v_kv_in_norm, v_w_dkv, v_kv_latent_norm, v_w_kr, v_w_uk, v_w_uv, v_w_dq, v_q_latent_norm, v_w_uq, v_w_o, v_ffn_w_up, v_ffn_conv_w, v_ffn_conv_b, v_ffn_w_down):
    wts = dict(attn_norm=attn_norm, ffn_norm=ffn_norm, final_norm=final_norm, sc_w_in=sc_w_in, sc_conv_w=sc_conv_w,
               sc_w_out=sc_w_out, kv_in_norm=kv_in_norm, w_dkv=w_dkv, kv_latent_norm=kv_latent_norm, w_kr=w_kr,
               w_uk=w_uk, w_uv=w_uv, w_dq=w_dq, q_latent_norm=q_latent_norm, w_uq=w_uq, w_o=w_o, ffn_w_up=ffn_w_up,
               ffn_conv_w=ffn_conv_w, ffn_conv_b=ffn_conv_b, ffn_w_down=ffn_w_down)
    mom = dict(attn_norm=m_attn_norm, ffn_norm=m_ffn_norm, final_norm=m_final_norm, sc_w_in=m_sc_w_in,
               sc_conv_w=m_sc_conv_w, sc_w_out=m_sc_w_out, kv_in_norm=m_kv_in_norm, w_dkv=m_w_dkv,
               kv_latent_norm=m_kv_latent_norm, w_kr=m_w_kr, w_uk=m_w_uk, w_uv=m_w_uv, w_dq=m_w_dq,
               q_latent_norm=m_q_latent_norm, w_uq=m_w_uq, w_o=m_w_o, ffn_w_up=m_ffn_w_up, ffn_conv_w=m_ffn_conv_w,
               ffn_conv_b=m_ffn_conv_b, ffn_w_down=m_ffn_w_down)
    var = dict(attn_norm=v_attn_norm, ffn_norm=v_ffn_norm, final_norm=v_final_norm, sc_w_in=v_sc_w_in,
               sc_conv_w=v_sc_conv_w, sc_w_out=v_sc_w_out, kv_in_norm=v_kv_in_norm, w_dkv=v_w_dkv,
               kv_latent_norm=v_kv_latent_norm, w_kr=v_w_kr, w_uk=v_w_uk, w_uv=v_w_uv, w_dq=v_w_dq,
               q_latent_norm=v_q_latent_norm, w_uq=v_w_uq, w_o=v_w_o, ffn_w_up=v_ffn_w_up, ffn_conv_w=v_ffn_conv_w,
               ffn_conv_b=v_ffn_conv_b, ffn_w_down=v_ffn_w_down)
    xi, yi, ci = _place()
    me = 4 * xi + 2 * yi + ci
    _Chain.last = None

    ex = _Exchange(wts, mom, var, ffn_conv_b)
    rep = {
        "attn_norm": attn_norm, "ffn_norm": ffn_norm, "final_norm": final_norm,
        "kv_in_norm": kv_in_norm.reshape(1, D), "kv_latent_norm": kv_latent_norm.reshape(1, KV_LORA),
        "q_latent_norm": q_latent_norm.reshape(1, Q_LORA),
    }
    loss, grad_x, small = _local_step(x.reshape(T, D), positions.reshape(T, 1), loss_target.reshape(T, D), rep, ex)

    def rows_of(a):
        return a.reshape(-1, a.shape[-1])

    def device_rows(a):
        taps, c = a.shape[-2], a.shape[-1] // N_DEV
        rows = a.reshape(-1, taps, N_DEV, c).transpose(2, 0, 1, 3).reshape(N_DEV, -1, c)
        return jnp.pad(rows, ((0, 0), (0, SHARD_ROWS - rows.shape[1]), (0, 0))).reshape(N_DEV * SHARD_ROWS, c)

    def taps_first(a):
        return jnp.transpose(a, (1, 0, 2))

    sharded = ("sc_conv_w", "ffn_conv_w")
    shards = ([loss.reshape(1, 1, 128)] + [rows_of(small[nm])[None] for nm in REPLICATED]
              + [device_rows(small[nm])[None] for nm in sharded])
    plan, ncopy = _plan_gather_all(len(shards))
    flight, = _copies_start("ag_small", [(shards, [lax.empty((1, N_DEV) + s.shape[1:], F32) for s in shards], ncopy, plan)])
    results = ex.finish(grad_x)
    _, gathered = _copies_wait("ag_small_wait", flight, ncopy, plan)
    params = [[None] + [rows_of(src[nm]) for nm in REPLICATED] + [taps_first(src[nm]) for nm in sharded]
              for src in (wts, mom, var)]
    summed = _adamw_small(gathered, *params, me.astype(jnp.int32).reshape(1))
    loss_total = summed[0][0][0, 0]
    for nm, vals in zip(REPLICATED, summed[1:1 + len(REPLICATED)]):
        results[nm] = [a.reshape(wts[nm].shape) for a in vals]
    for nm, vals in zip(sharded, summed[1 + len(REPLICATED):]):
        results[nm] = [taps_first(a) for a in vals]

    outs = [loss_total, grad_x.reshape(1, T, D)]
    for slot in range(4):
        outs.extend(results[nm][slot] for nm in WEIGHTS)
    return tuple(outs)
```

```python
import jax
import jax.numpy as jnp
from jax import lax
from jax.experimental import pallas as pl
from jax.experimental.pallas import tpu as pltpu

F32 = jnp.float32
BF16 = jnp.bfloat16

T = 2048
D = 1024
N_HEADS = 8
QK_NOPE = 128
QK_ROPE = 64
V_HEAD = 128
Q_LORA = 384
KV_LORA = 256
D_FF = 2816
CHUNK = 64
ROPE_THETA = 10000.0
EPS = 1e-6
NEG_INF = -1e30
ADAM_LR = 0.001
ADAM_B1 = 0.9
ADAM_B2 = 0.999
ADAM_EPS = 1e-08
ADAM_WD = 0.01
ADAM_STEP = 10

N_DEV = 8
N_CHIP = 4
FF_BLK = D_FF * 2 // N_DEV
N_FF_BLK = D_FF // FF_BLK
QK_PAD = 256
HALO = 16

TM = 1024
TS = 512
TR = 256
TQ = 512
VMEM_LIMIT = 56 * 1024 * 1024

NN = (((1,), (0,)), ((), ()))
NT = (((1,), (1,)), ((), ()))
TN = (((0,), (0,)), ((), ()))
MESH = pl.DeviceIdType.MESH


def _params(sem):
    return pltpu.CompilerParams(dimension_semantics=sem, vmem_limit_bytes=VMEM_LIMIT)


ANY_SPEC = pl.BlockSpec(memory_space=pl.ANY)
VMEM_SPEC = pl.BlockSpec(memory_space=pltpu.VMEM)


class _Chain:
    last = None


def _pallas(body, *, name, in_specs, out_specs, out_shape, grid=(), scratch_shapes=(), n_prefetch=0, aliases=None,
            params=None):
    def run(*args):
        after = _Chain.last
        n_lead = len(args)
        specs, operands, fn = list(in_specs), list(args), body
        if after is not None:
            def fn(*refs):
                return body(*refs[:n_lead], *refs[n_lead + 1:])
            specs.append(ANY_SPEC)
            operands.append(after)
        kw = dict(name=name, out_shape=out_shape, input_output_aliases=aliases or {})
        if params is not None:
            kw["compiler_params"] = params
        if n_prefetch:
            kw["grid_spec"] = pltpu.PrefetchScalarGridSpec(
                num_scalar_prefetch=n_prefetch, grid=grid, in_specs=specs, out_specs=out_specs,
                scratch_shapes=scratch_shapes)
        else:
            kw.update(grid=grid, in_specs=specs, out_specs=out_specs, scratch_shapes=scratch_shapes)
        outs = pl.pallas_call(fn, **kw)(*operands)
        _Chain.last = outs[0] if isinstance(outs, (list, tuple)) else outs
        return outs
    return run


def _mm(name, a, b, *, grid, a_spec, b_spec, o_spec, o_shape, o_dtype, dims, k_axis=None, acc_shape=None,
        add=None, add_spec=None):
    nk = grid[k_axis] if k_axis is not None else 1
    has_add = add is not None

    def body(*refs):
        a_ref, b_ref = refs[0], refs[1]
        p = 2
        add_ref = None
        if has_add:
            add_ref = refs[p]
            p += 1
        o_ref = refs[p]
        p += 1
        r = lax.dot_general(a_ref[...].astype(BF16), b_ref[...].astype(BF16), dims, preferred_element_type=F32)
        if k_axis is None:
            if has_add:
                r = r + add_ref[...].astype(F32)
            o_ref[...] = r.astype(o_dtype)
        else:
            acc = refs[p]
            k = pl.program_id(k_axis)

            @pl.when(k == 0)
            def _():
                acc[...] = r

            @pl.when(k > 0)
            def _():
                acc[...] += r

            @pl.when(k == nk - 1)
            def _():
                t = acc[...]
                if has_add:
                    t = t + add_ref[...].astype(F32)
                o_ref[...] = t.astype(o_dtype)

    in_specs = [a_spec, b_spec]
    args = [a, b]
    if has_add:
        in_specs.append(add_spec if add_spec is not None else o_spec)
        args.append(add)
    sem = tuple("arbitrary" if ax == k_axis else "parallel" for ax in range(len(grid)))
    scratch = [pltpu.VMEM(acc_shape, F32)] if k_axis is not None else []
    return _pallas(body, name=name, grid=grid, in_specs=in_specs, out_specs=o_spec,
                   out_shape=jax.ShapeDtypeStruct(o_shape, o_dtype), scratch_shapes=scratch, params=_params(sem))(*args)


def _mm_sum(name, parts, *, grid, o_spec, o_shape, o_dtype, add=None, norm_bwd=None, post=None):
    has_add = add is not None
    np_ = len(parts)
    nn = 1 if norm_bwd is None else len(norm_bwd[1])
    has_res = norm_bwd is not None and norm_bwd[2] is not None
    has_post = post is not None

    def body(*refs):
        accs = [None] * nn
        for p, (_, _, _, _, dims, n) in enumerate(parts):
            a_ref, b_ref = refs[2 * p], refs[2 * p + 1]
            for k in range(a_ref.shape[0]):
                r = lax.dot_general(a_ref[k], b_ref[k], dims, preferred_element_type=F32)
                accs[n] = r if accs[n] is None else accs[n] + r
        if norm_bwd is None:
            acc = accs[0]
            if has_add:
                acc = acc + refs[2 * np_][...]
            refs[-1][...] = acc.astype(o_dtype)
            return
        x_ref, g_refs = refs[2 * np_], refs[2 * np_ + 1:2 * np_ + 1 + nn]
        n_in = 2 * np_ + 1 + nn + has_res + has_post
        dx_ref, dxb_ref, dg_refs = refs[n_in], refs[n_in + 1], refs[n_in + 2:n_in + 2 + nn]
        xv = x_ref[...]
        r = lax.rsqrt(jnp.mean(xv * xv, axis=-1, keepdims=True) + EPS)
        xn = xv * r
        dx = refs[2 * np_ + 1 + nn][...] if has_res else None
        sums = []
        for acc, g_ref in zip(accs, g_refs):
            gdy = acc * g_ref[...]
            t = r * (gdy - xn * jnp.mean(gdy * xn, axis=-1, keepdims=True))
            dx = t if dx is None else dx + t
            sums.append(jnp.sum(acc * xn, axis=0, keepdims=True))
        dx_ref[...] = dx
        dxb = dx.astype(BF16)
        dxb_ref[...] = dxb
        if has_post:
            refs[n_in + 2 + nn][...] = lax.dot_general(dxb, refs[n_in - 1][...], post[1],
                                                       preferred_element_type=F32).astype(BF16)

        @pl.when(pl.program_id(0) == 0)
        def _():
            for dg_ref, part in zip(dg_refs, sums):
                dg_ref[...] = part

        @pl.when(pl.program_id(0) > 0)
        def _():
            for dg_ref, part in zip(dg_refs, sums):
                dg_ref[...] += part

    in_specs, args = [], []
    for a, a_spec, b, b_spec, _, _ in parts:
        in_specs += [a_spec, b_spec]
        args += [a, b]
    if norm_bwd is None:
        if has_add:
            in_specs.append(o_spec)
            args.append(add)
        return _pallas(body, name=name, grid=grid, in_specs=in_specs, out_specs=o_spec,
                       out_shape=jax.ShapeDtypeStruct(o_shape, o_dtype),
                       params=_params(("parallel",) * len(grid)))(*args)
    x, gains, dres = norm_bwd
    vec = pl.BlockSpec((1, o_shape[1]), lambda i: (0, 0))
    in_specs += [o_spec] + [vec] * nn + ([o_spec] if has_res else [])
    args += [x] + list(gains) + ([dres] if has_res else [])
    out_specs = [o_spec, o_spec] + [vec] * nn
    out_shape = ([jax.ShapeDtypeStruct(o_shape, F32), jax.ShapeDtypeStruct(o_shape, BF16)]
                 + [jax.ShapeDtypeStruct((1, o_shape[1]), F32)] * nn)
    if has_post:
        in_specs.append(pl.BlockSpec(post[0].shape, lambda i: (0, 0)))
        args.append(post[0])
        out_specs.append(pl.BlockSpec((o_spec.block_shape[0], post[2]), lambda i: (i, 0)))
        out_shape.append(jax.ShapeDtypeStruct((o_shape[0], post[2]), BF16))
    outs = _pallas(body, name=name, grid=grid, in_specs=in_specs, out_specs=out_specs, out_shape=out_shape,
                   params=_params(("arbitrary",)))(*args)
    if has_post:
        return outs[0], outs[1], list(outs[2:2 + nn]), outs[2 + nn]
    return outs[0], outs[1], list(outs[2:])


def _mm_rows(name, a, b, dims, o_dtype, n_out, *, tn=None, add=None):
    k = a.shape[1]
    tn = n_out if tn is None else tn
    if dims == NN:
        b_spec = pl.BlockSpec((k, tn), lambda n, i: (0, n))
    else:
        b_spec = pl.BlockSpec((tn, k), lambda n, i: (n, 0))
    return _mm(name, a, b, grid=(n_out // tn, T // TM),
               a_spec=pl.BlockSpec((TM, k), lambda n, i: (i, 0)), b_spec=b_spec,
               o_spec=pl.BlockSpec((TM, tn), lambda n, i: (i, n)), o_shape=(T, n_out), o_dtype=o_dtype,
               dims=dims, add=add)


def _wgrads(name, jobs):
    jobs = [job if len(job) == 3 else (*job, job[0].shape[-1]) for job in jobs]
    arrays, index = [], {}
    for a, b, _ in jobs:
        for arr in (a, b):
            if id(arr) not in index:
                index[id(arr)] = len(arrays)
                arrays.append(arr)
    n_in = len(arrays)

    def body(*refs):
        for t, (a, b, rows) in enumerate(jobs):
            a_ref, b_ref, o_ref = refs[index[id(a)]], refs[index[id(b)]], refs[n_in + t]
            if a.ndim == 3:
                for h in range(a.shape[0]):
                    o_ref[h] = lax.dot_general(a_ref[h], b_ref[...], TN, preferred_element_type=F32)[:rows].astype(BF16)
            else:
                o_ref[...] = lax.dot_general(a_ref[...], b_ref[...], TN, preferred_element_type=F32)[:rows].astype(BF16)

    out_shape = [jax.ShapeDtypeStruct(a.shape[:-2] + (rows, b.shape[-1]), BF16) for a, b, rows in jobs]
    return _pallas(body, name=name, in_specs=[VMEM_SPEC] * n_in, out_specs=[VMEM_SPEC] * len(jobs), out_shape=out_shape,
                   params=pltpu.CompilerParams(vmem_limit_bytes=VMEM_LIMIT))(*arrays)


def _mm_wgrad(name, a, b, *, tn=512):
    k, n = a.shape[1], b.shape[1]
    tn = min(tn, n)
    return _mm(name, a, b, grid=(n // tn,),
               a_spec=pl.BlockSpec((T, k), lambda j: (0, 0)), b_spec=pl.BlockSpec((T, tn), lambda j: (0, j)),
               o_spec=pl.BlockSpec((k, tn), lambda j: (0, j)), o_shape=(k, n), o_dtype=BF16, dims=TN)


def _rms_fwd(name, x, g):
    d = x.shape[1]

    def body(x_ref, g_ref, o_ref):
        xv = x_ref[...]
        r = lax.rsqrt(jnp.mean(xv * xv, axis=-1, keepdims=True) + EPS)
        o_ref[...] = ((xv * r) * g_ref[...]).astype(BF16)

    return _pallas(
        body, name=name, grid=(T // TM,),
        in_specs=[pl.BlockSpec((TM, d), lambda i: (i, 0)), pl.BlockSpec((1, d), lambda i: (0, 0))],
        out_specs=pl.BlockSpec((TM, d), lambda i: (i, 0)),
        out_shape=jax.ShapeDtypeStruct((T, d), BF16), params=_params(("parallel",)))(x, g)


def _rms(xv, g):
    return (xv * lax.rsqrt(jnp.mean(xv * xv, axis=-1, keepdims=True) + EPS)) * g


def _out_norm(name, a, w, add, g):
    def body(a_ref, w_ref, add_ref, g_ref, h_ref, hn_ref):
        hv = lax.dot_general(a_ref[...], w_ref[...], NN, preferred_element_type=F32) + add_ref[...]
        h_ref[...] = hv
        hn_ref[...] = _rms(hv, g_ref[...]).astype(BF16)

    rows = pl.BlockSpec((TS, D), lambda i: (i, 0))
    return _pallas(
        body, name=name, grid=(T // TS,),
        in_specs=[pl.BlockSpec((TS, a.shape[1]), lambda i: (i, 0)), pl.BlockSpec(w.shape, lambda i: (0, 0)), rows,
                  pl.BlockSpec((1, D), lambda i: (0, 0))],
        out_specs=[rows, rows], out_shape=[jax.ShapeDtypeStruct((T, D), F32), jax.ShapeDtypeStruct((T, D), BF16)],
        params=_params(("parallel",)))(a, w, add, g)


def _down_final(act, w_down4, h_in, g, tgt):
    def body(a_ref, w_ref, hin_ref, g_ref, t_ref, loss_ref, dh_ref, dhb_ref, dg_ref):
        hv = lax.dot_general(a_ref[0], w_ref[0], NN, preferred_element_type=F32)
        for j in range(1, N_FF_BLK):
            hv = hv + lax.dot_general(a_ref[j], w_ref[j], NN, preferred_element_type=F32)
        hv = hv + hin_ref[...]
        r = lax.rsqrt(jnp.mean(hv * hv, axis=-1, keepdims=True) + EPS)
        xn = hv * r
        gv = g_ref[...]
        err = xn * gv - t_ref[...]
        part_loss = 0.5 * jnp.sum(jnp.mean(err * err, axis=-1, keepdims=True), axis=0, keepdims=True)
        dy = err * (1.0 / D)
        gdy = dy * gv
        dh = r * (gdy - xn * jnp.mean(gdy * xn, axis=-1, keepdims=True))
        dh_ref[...] = dh
        dhb_ref[...] = dh.astype(BF16)
        part = jnp.sum(dy * xn, axis=0, keepdims=True)
        first = pl.program_id(0) == 0

        @pl.when(first)
        def _():
            dg_ref[...] = part
            loss_ref[...] = jnp.broadcast_to(part_loss, (1, 128))

        @pl.when(jnp.logical_not(first))
        def _():
            dg_ref[...] += part
            loss_ref[...] += jnp.broadcast_to(part_loss, (1, 128))

    row = pl.BlockSpec((TS, D), lambda i: (i, 0))
    vec = pl.BlockSpec((1, D), lambda i: (0, 0))
    return _pallas(
        body, name="f1_down_loss", grid=(T // TS,),
        in_specs=[pl.BlockSpec((N_FF_BLK, TS, FF_BLK), lambda i: (0, i, 0)),
                  pl.BlockSpec((None, N_FF_BLK, FF_BLK, D), lambda i: (0, 0, 0, 0)), row, vec, row],
        out_specs=[pl.BlockSpec((1, 128), lambda i: (0, 0)), row, row, vec],
        out_shape=[jax.ShapeDtypeStruct((1, 128), F32), jax.ShapeDtypeStruct((T, D), F32),
                   jax.ShapeDtypeStruct((T, D), BF16), jax.ShapeDtypeStruct((1, D), F32)],
        params=_params(("arbitrary",)))(act, w_down4, h_in, g, tgt)


def _prev_idx(i, rows=TR):
    return jnp.maximum(i * (rows // HALO) - 1, 0)


def _next_idx(i, rows=TR):
    return jnp.minimum((i + 1) * (rows // HALO), T // HALO - 1)


def _causal_taps(ext):
    return pltpu.roll(ext, 2, 0)[HALO:], pltpu.roll(ext, 1, 0)[HALO:], ext[HALO:]


def _anticausal_taps(ext, n):
    rows = ext.shape[0]
    return pltpu.roll(ext, rows - 1, 0)[:n], pltpu.roll(ext, rows - 2, 0)[:n]


MIX_COLS = 512


def _mixer_in(hn, w_in, w):
    nc = D // MIX_COLS

    def body(h_ref, hh_ref, wb_ref, wc_ref, wu_ref, w_ref, b_ref, c_ref, u_ref, y_ref):
        i = pl.program_id(1)
        hv = h_ref[...]
        he = jnp.concatenate([hh_ref[...], hv], axis=0)
        ce = lax.dot_general(he, wc_ref[...], NN, preferred_element_type=F32).astype(BF16)
        ue = lax.dot_general(he, wu_ref[...], NN, preferred_element_type=F32).astype(BF16)
        bv = lax.dot_general(hv, wb_ref[...], NN, preferred_element_type=F32).astype(BF16)
        b_ref[...] = bv
        c_ref[...] = ce[HALO:]
        u_ref[...] = ue[HALO:]
        row = lax.broadcasted_iota(jnp.int32, (HALO + TS, 1), 0)
        cu = jnp.where(jnp.logical_or(i > 0, row >= HALO), ce.astype(F32) * ue.astype(F32), 0.0)
        x2, x1, x0 = _causal_taps(cu)
        wv = w_ref[...]
        cv = (x2 * wv[0:1] + x1 * wv[1:2]) + x0 * wv[2:3]
        y_ref[...] = (bv.astype(F32) * cv).astype(BF16)

    def cols(part):
        return pl.BlockSpec((D, MIX_COLS), lambda j, i: (0, part * nc + j))

    blk = pl.BlockSpec((TS, MIX_COLS), lambda j, i: (i, j))
    out = jax.ShapeDtypeStruct((T, D), BF16)
    return _pallas(
        body, name="l0_in", grid=(nc, T // TS),
        in_specs=[pl.BlockSpec((TS, D), lambda j, i: (i, 0)), pl.BlockSpec((HALO, D), lambda j, i: (_prev_idx(i, TS), 0)),
                  cols(0), cols(1), cols(2), pl.BlockSpec((3, MIX_COLS), lambda j, i: (0, j))],
        out_specs=[blk] * 4, out_shape=[out] * 4,
        params=_params(("parallel", "parallel")))(hn, hn, w_in, w_in, w_in, w)


def _mixer_out_bwd(dh, w_out, zb, zc, zu, w):
    last = T // TR - 1

    def body(dh_ref, dhn_ref, wo_ref, b_ref, bn_ref, c_ref, ch_ref, u_ref, uh_ref, w_ref, dz_ref, dw_ref):
        i = pl.program_id(0)
        dye = lax.dot_general(jnp.concatenate([dh_ref[...], dhn_ref[...]], axis=0), wo_ref[...], NT,
                              preferred_element_type=F32)
        cv_ = c_ref[...].astype(F32)
        uv = u_ref[...].astype(F32)
        cu = cv_ * uv
        cuh = jnp.where(i > 0, ch_ref[...].astype(F32) * uh_ref[...].astype(F32), 0.0)
        x2, x1, x0 = _causal_taps(jnp.concatenate([cuh, cu], axis=0))
        wv = w_ref[...]
        conv = (x2 * wv[0:1] + x1 * wv[1:2]) + x0 * wv[2:3]
        dyv = dye[:TR]
        dz_ref[:, 0:D] = (dyv * conv).astype(BF16)
        dconv = dyv * b_ref[...].astype(F32)
        dconv_n = jnp.where(i < last, dye[TR:] * bn_ref[...].astype(F32), 0.0)
        n1, n2 = _anticausal_taps(jnp.concatenate([dconv, dconv_n], axis=0), TR)
        dcu = (dconv * wv[2:3] + n1 * wv[1:2]) + n2 * wv[0:1]
        dz_ref[:, D:2 * D] = (dcu * uv).astype(BF16)
        dz_ref[:, 2 * D:3 * D] = (dcu * cv_).astype(BF16)
        part = jnp.concatenate([jnp.sum(dconv * x2, axis=0, keepdims=True),
                                jnp.sum(dconv * x1, axis=0, keepdims=True),
                                jnp.sum(dconv * x0, axis=0, keepdims=True)], axis=0)

        @pl.when(i == 0)
        def _():
            dw_ref[...] = part

        @pl.when(i > 0)
        def _():
            dw_ref[...] += part

    main = pl.BlockSpec((TR, D), lambda i: (i, 0))
    prev = pl.BlockSpec((HALO, D), lambda i: (_prev_idx(i), 0))
    nxt = pl.BlockSpec((HALO, D), lambda i: (_next_idx(i), 0))
    wspec = pl.BlockSpec((3, D), lambda i: (0, 0))
    return _pallas(
        body, name="d_l0_out", grid=(T // TR,),
        in_specs=[main, nxt, pl.BlockSpec((D, D), lambda i: (0, 0)), main, nxt, main, prev, main, prev, wspec],
        out_specs=[pl.BlockSpec((TR, 3 * D), lambda i: (i, 0)), wspec],
        out_shape=[jax.ShapeDtypeStruct((T, 3 * D), BF16), jax.ShapeDtypeStruct((3, D), F32)],
        params=_params(("arbitrary",)))(dh, dh, w_out, zb, zb, zc, zc, zu, zu, w)


def _sigmoid(x):
    return 0.5 * jnp.tanh(0.5 * x) + 0.5


def _ffn_up_act(name, hf, w_up, w, b):
    def body(h_ref, hh_ref, wg_ref, wv_ref, w_ref, b_ref, g_ref, v_ref, a_ref):
        i = pl.program_id(1)
        hv = h_ref[...]
        ge = lax.dot_general(jnp.concatenate([hh_ref[...], hv], axis=0), wg_ref[...], NT,
                             preferred_element_type=F32).astype(BF16)
        v = lax.dot_general(hv, wv_ref[...], NT, preferred_element_type=F32).astype(BF16)
        g_ref[...] = ge[HALO:]
        v_ref[...] = v
        ext = ge.astype(F32)
        row = lax.broadcasted_iota(jnp.int32, (HALO + TM, 1), 0)
        ext = jnp.where(jnp.logical_or(i > 0, row >= HALO), ext, 0.0)
        x2, x1, x0 = _causal_taps(ext)
        wv = w_ref[...]
        gc = ((x2 * wv[0:1] + x1 * wv[1:2]) + x0 * wv[2:3]) + b_ref[...]
        a_ref[...] = ((gc * _sigmoid(gc)) * v.astype(F32)).astype(BF16)

    blk = pl.BlockSpec((None, TM, FF_BLK), lambda j, i: (j, i, 0))
    out = jax.ShapeDtypeStruct((N_FF_BLK, T, FF_BLK), BF16)
    return _pallas(
        body, name=name, grid=(N_FF_BLK, T // TM),
        in_specs=[pl.BlockSpec((TM, D), lambda j, i: (i, 0)),
                  pl.BlockSpec((HALO, D), lambda j, i: (_prev_idx(i, TM), 0)),
                  pl.BlockSpec((None, None, FF_BLK, D), lambda j, i: (0, j, 0, 0)),
                  pl.BlockSpec((None, None, FF_BLK, D), lambda j, i: (0, j + N_FF_BLK, 0, 0)),
                  pl.BlockSpec((None, 3, FF_BLK), lambda j, i: (j, 0, 0)),
                  pl.BlockSpec((None, 1, FF_BLK), lambda j, i: (j, 0, 0))],
        out_specs=[blk, blk, blk], out_shape=[out, out, out],
        params=_params(("parallel", "parallel")))(hf, hf, w_up, w_up, w, b)


def _ffn_dact(name, dh, w_down4, g, v, w, b):
    last = T // TS - 1

    def body(dh_ref, dhn_ref, wd_ref, g_ref, gp_ref, gn_ref, v_ref, vn_ref, w_ref, b_ref, dg_ref, dv_ref, dw_ref, db_ref):
        i = pl.program_id(1)
        da = lax.dot_general(jnp.concatenate([dh_ref[...], dhn_ref[...]], axis=0), wd_ref[...], NT,
                             preferred_element_type=F32)
        row = lax.broadcasted_iota(jnp.int32, (TS + HALO, 1), 0)
        da = jnp.where(jnp.logical_or(i < last, row < TS), da, 0.0)
        gp = jnp.where(i > 0, gp_ref[...].astype(F32), 0.0)
        ext = jnp.concatenate([gp, g_ref[...].astype(F32), gn_ref[...].astype(F32)], axis=0)
        x2, x1, x0 = _causal_taps(ext)
        wv = w_ref[...]
        gc = ((x2 * wv[0:1] + x1 * wv[1:2]) + x0 * wv[2:3]) + b_ref[...]
        sg = _sigmoid(gc)
        vv = jnp.concatenate([v_ref[...].astype(F32), vn_ref[...].astype(F32)], axis=0)
        silu = gc * sg
        dv_ref[...] = (da[:TS] * silu[:TS]).astype(BF16)
        dgc = (da * vv) * (sg + silu * (1.0 - sg))
        n1, n2 = _anticausal_taps(dgc, TS)
        d0 = dgc[:TS]
        dg_ref[...] = ((d0 * wv[2:3] + n1 * wv[1:2]) + n2 * wv[0:1]).astype(BF16)
        part_w = jnp.concatenate([jnp.sum(d0 * x2[:TS], axis=0, keepdims=True),
                                  jnp.sum(d0 * x1[:TS], axis=0, keepdims=True),
                                  jnp.sum(d0 * x0[:TS], axis=0, keepdims=True)], axis=0)
        part_b = jnp.sum(d0, axis=0, keepdims=True)

        @pl.when(i == 0)
        def _():
            dw_ref[...] = part_w
            db_ref[...] = part_b

        @pl.when(i > 0)
        def _():
            dw_ref[...] += part_w
            db_ref[...] += part_b

    blk = pl.BlockSpec((None, TS, FF_BLK), lambda j, i: (j, i, 0))
    prev = pl.BlockSpec((None, HALO, FF_BLK), lambda j, i: (j, _prev_idx(i, TS), 0))
    nxt = pl.BlockSpec((None, HALO, FF_BLK), lambda j, i: (j, _next_idx(i, TS), 0))
    wspec = pl.BlockSpec((None, 3, FF_BLK), lambda j, i: (j, 0, 0))
    bspec = pl.BlockSpec((None, 1, FF_BLK), lambda j, i: (j, 0, 0))
    return _pallas(
        body, name=name, grid=(N_FF_BLK, T // TS),
        in_specs=[pl.BlockSpec((TS, D), lambda j, i: (i, 0)),
                  pl.BlockSpec((HALO, D), lambda j, i: (_next_idx(i, TS), 0)),
                  pl.BlockSpec((None, None, FF_BLK, D), lambda j, i: (0, j, 0, 0)),
                  blk, prev, nxt, blk, nxt, wspec, bspec],
        out_specs=[blk, blk, wspec, bspec],
        out_shape=[jax.ShapeDtypeStruct((N_FF_BLK, T, FF_BLK), BF16), jax.ShapeDtypeStruct((N_FF_BLK, T, FF_BLK), BF16),
                   jax.ShapeDtypeStruct((N_FF_BLK, 3, FF_BLK), F32), jax.ShapeDtypeStruct((N_FF_BLK, 1, FF_BLK), F32)],
        params=_params(("parallel", "arbitrary")))(dh, dh, w_down4, g, g, g, v, v, w, b)


def _rope_tables(pos, inv_freq):
    half = QK_ROPE // 2

    def body(p_ref, f_ref, c_ref, sa_ref, sb_ref):
        ang = p_ref[...].astype(F32) * f_ref[...]
        lane = lax.broadcasted_iota(jnp.int32, (T, 128), 1)
        c = jnp.cos(ang)
        s = jnp.sin(ang)
        c_ref[...] = jnp.where(lane < 2 * half, c, 0.0)
        sa_ref[...] = jnp.where(lane < half, -s, 0.0)
        sb_ref[...] = jnp.where(jnp.logical_and(lane >= half, lane < 2 * half), s, 0.0)

    return _pallas(
        body, name="rope_tables", in_specs=[VMEM_SPEC] * 2, out_specs=[VMEM_SPEC] * 3,
        out_shape=[jax.ShapeDtypeStruct((T, 128), F32)] * 3,
        params=pltpu.CompilerParams(vmem_limit_bytes=VMEM_LIMIT))(pos, inv_freq)


def _rotate(r, c, sa, sb, sign):
    return r * c + sign * (pltpu.roll(r, 96, 1) * sa + pltpu.roll(r, 32, 1) * sb)


def _attn_pre(h2, g_kv, g_l1, g_kvl, g_ql, w_dkv, w_kr, w_uk, w_uv, w_dq, w_uq, tables):
    def body(h_ref, c_ref, sa_ref, sb_ref, gkv_ref, gl1_ref, gkvl_ref, gql_ref, wdkv_ref, wkr_ref, wuk_ref, wuv_ref,
             wdq_ref, wuq_ref, hk_ref, hn_ref, ckvr_ref, ckv_ref, kr_ref, kn_ref, v_ref, cqr_ref, cq_ref, q_ref):
        xv = h_ref[...]
        xn = xv * lax.rsqrt(jnp.mean(xv * xv, axis=-1, keepdims=True) + EPS)
        hk = (xn * gkv_ref[...]).astype(BF16)
        hn = (xn * gl1_ref[...]).astype(BF16)
        hk_ref[...] = hk
        hn_ref[...] = hn
        cv, sav, sbv = c_ref[...], sa_ref[...], sb_ref[...]
        raw = lax.dot_general(hk, wdkv_ref[...], NN, preferred_element_type=F32)
        ckvr_ref[...] = raw
        ckv = _rms(raw, gkvl_ref[...]).astype(BF16)
        ckv_ref[...] = ckv
        kr = lax.dot_general(hk, wkr_ref[...], NT, preferred_element_type=F32)
        kr_ref[...] = _rotate(kr, cv, sav, sbv, 1.0).astype(BF16)
        kn_ref[...] = lax.dot_general(ckv, wuk_ref[...], NN, preferred_element_type=F32).astype(BF16)
        v_ref[...] = lax.dot_general(ckv, wuv_ref[...], NN, preferred_element_type=F32).astype(BF16)
        cqr = lax.dot_general(hn, wdq_ref[...], NN, preferred_element_type=F32)
        cqr_ref[...] = cqr
        cq = _rms(cqr, gql_ref[...]).astype(BF16)
        cq_ref[...] = cq
        for h in range(N_HEADS):
            r = lax.dot_general(cq, wuq_ref[h], NT, preferred_element_type=F32)
            q_ref[h, :, :QK_NOPE] = (r[:, :QK_NOPE] * SCALE2).astype(BF16)
            q_ref[h, :, QK_NOPE:] = (_rotate(r[:, QK_NOPE:], cv, sav, sbv, 1.0) * SCALE2).astype(BF16)

    def rows(d):
        return pl.BlockSpec((TS, d), lambda i: (i, 0))

    def whole(a):
        return pl.BlockSpec(a.shape, lambda i: (0,) * a.ndim)

    wholes = [g_kv, g_l1, g_kvl, g_ql, w_dkv, w_kr, w_uk, w_uv, w_dq, w_uq]
    outs = [(D, BF16), (D, BF16), (KV_LORA, F32), (KV_LORA, BF16), (128, BF16), (N_HEADS * QK_NOPE, BF16),
            (N_HEADS * V_HEAD, BF16), (Q_LORA, F32), (Q_LORA, BF16)]
    return _pallas(
        body, name="attn_pre", grid=(T // TS,),
        in_specs=[rows(D), rows(128), rows(128), rows(128)] + [whole(a) for a in wholes],
        out_specs=[rows(d) for d, _ in outs] + [pl.BlockSpec((N_HEADS, TS, QK_PAD), lambda i: (0, i, 0))],
        out_shape=[jax.ShapeDtypeStruct((T, d), dt) for d, dt in outs]
        + [jax.ShapeDtypeStruct((N_HEADS, T, QK_PAD), BF16)],
        params=_params(("parallel",)))(h2, *tables, *wholes)


SCALE = (QK_NOPE + QK_ROPE) ** -0.5
LOG2E = 1.4426950408889634
SCALE2 = SCALE * LOG2E


def _diag_mask(transposed):
    shift = CHUNK.bit_length() - 1
    a = lax.broadcasted_iota(jnp.int32, (TQ, TQ), 0) >> shift
    b = lax.broadcasted_iota(jnp.int32, (TQ, TQ), 1) >> shift
    return (a <= b) if transposed else (b <= a)


def _as_row(col):
    return jnp.transpose(jnp.broadcast_to(col, (col.shape[0], 128)), (1, 0))[0:1]


def _attn_fwd(q, kn, kr, v):
    hp = 4

    def body(q_ref, kn_ref, kr_ref, v_ref, o_ref, lse_ref):
        i = pl.program_id(1)
        qs = [q_ref[a] for a in range(hp)]

        def step(j, carry, masked):
            off = pl.multiple_of(j * TQ, TQ)
            krv = kr_ref[pl.ds(off, TQ), :]
            ss = []
            for a in range(hp):
                kk = jnp.concatenate([kn_ref[pl.ds(off, TQ), a * QK_NOPE:(a + 1) * QK_NOPE], krv], axis=1)
                ss.append(lax.dot_general(qs[a], kk, NT, preferred_element_type=F32))
            out = []
            for a in range(hp):
                m, l, acc = carry[a]
                s = ss[a]
                if masked:
                    s = jnp.where(_diag_mask(False), s, NEG_INF)
                m_new = jnp.maximum(m, jnp.max(s, axis=-1, keepdims=True))
                p = jnp.exp2(s - m_new)
                alpha = jnp.exp2(m - m_new)
                l = alpha * l + jnp.sum(p, axis=-1, keepdims=True)
                pv = lax.dot_general(p.astype(BF16), v_ref[pl.ds(off, TQ), a * V_HEAD:(a + 1) * V_HEAD], NN,
                                     preferred_element_type=F32)
                out.append((m_new, l, alpha * acc + pv))
            return tuple(out)

        one = (jnp.full((TQ, 1), NEG_INF, F32), jnp.zeros((TQ, 1), F32), jnp.zeros((TQ, V_HEAD), F32))
        carry = lax.fori_loop(0, i, lambda j, cr: step(j, cr, False), (one,) * hp)
        carry = step(i, carry, True)
        for a, (m, l, acc) in enumerate(carry):
            o_ref[:, a * V_HEAD:(a + 1) * V_HEAD] = (acc / l).astype(BF16)
            lse_ref[a] = _as_row(m + jnp.log(l) * LOG2E)

    return _pallas(
        body, name="attn_fwd", grid=(N_HEADS // hp, T // TQ),
        in_specs=[pl.BlockSpec((hp, TQ, QK_PAD), lambda h, i: (h, i, 0)),
                  pl.BlockSpec((T, hp * QK_NOPE), lambda h, i: (0, h)),
                  pl.BlockSpec((T, 128), lambda h, i: (0, 0)),
                  pl.BlockSpec((T, hp * V_HEAD), lambda h, i: (0, h))],
        out_specs=[pl.BlockSpec((TQ, hp * V_HEAD), lambda h, i: (i, h)), pl.BlockSpec((hp, 1, TQ), lambda h, i: (h, 0, i))],
        out_shape=[jax.ShapeDtypeStruct((T, N_HEADS * V_HEAD), BF16), jax.ShapeDtypeStruct((N_HEADS, 1, T), F32)],
        params=_params(("parallel", "parallel")))(q, kn, kr, v)


def _attn_bwd(q, kn, kr, v, o, do, lse_row, tables):
    nq = T // TQ
    hp = 2
    cos, sa, sb = tables

    def body(q_ref, kn_ref, kr_ref, v_ref, o_ref, do_ref, lse_ref, c_ref, sa_ref, sb_ref,
             dq_ref, dkn_ref, dkr_ref, dv_ref, dq_acc, dl_ref):
        j = pl.program_id(1)

        def cols(a):
            return slice(a * 128, (a + 1) * 128)

        @pl.when(j == 0)
        def _():
            dq_acc[...] = jnp.zeros_like(dq_acc)
            for a in range(hp):
                for i in range(nq):
                    rows = pl.ds(i * TQ, TQ)
                    prod = do_ref[rows, cols(a)].astype(F32) * o_ref[rows, cols(a)].astype(F32)
                    dl_ref[a, :, rows] = _as_row(jnp.sum(prod, axis=-1, keepdims=True))

        krv = kr_ref[...]
        kks = [jnp.concatenate([kn_ref[:, cols(a)], krv], axis=1) for a in range(hp)]
        vvs = [v_ref[:, cols(a)] for a in range(hp)]

        def step(i, carry, masked):
            off = pl.multiple_of(i * TQ, TQ)
            rows = pl.ds(off, TQ)
            qis = [q_ref[a, rows, :] for a in range(hp)]
            dois = [do_ref[rows, cols(a)] for a in range(hp)]
            sts = [lax.dot_general(kks[a], qis[a], NT, preferred_element_type=F32) for a in range(hp)]
            dpts = [lax.dot_general(vvs[a], dois[a], NT, preferred_element_type=F32) for a in range(hp)]
            out = []
            for a in range(hp):
                dk, dv = carry[a]
                st = sts[a]
                if masked:
                    st = jnp.where(_diag_mask(True), st, NEG_INF)
                pt = jnp.exp2(st - lse_ref[a, :, rows])
                dv = dv + lax.dot_general(pt.astype(BF16), dois[a], NN, preferred_element_type=F32)
                dst = (pt * (dpts[a] - dl_ref[a, :, rows])).astype(BF16)
                dk = dk + lax.dot_general(dst, qis[a], NN, preferred_element_type=F32)
                dq_acc[a, rows, :] += lax.dot_general(dst, kks[a], TN, preferred_element_type=F32)
                out.append((dk, dv))
            return tuple(out)

        zero = (jnp.zeros((TQ, QK_PAD), F32), jnp.zeros((TQ, V_HEAD), F32))
        carry = step(j, (zero,) * hp, True)
        carry = lax.fori_loop(j + 1, nq, lambda i, cr: step(i, cr, False), carry)
        for a, (dk, dv) in enumerate(carry):
            dk = dk * (SCALE / SCALE2)
            dkn_ref[:, cols(a)] = dk[:, :QK_NOPE].astype(BF16)
            dkr_ref[a] = dk[:, QK_NOPE:]
            dv_ref[:, cols(a)] = dv.astype(BF16)

        @pl.when(j == nq - 1)
        def _():
            for a in range(hp):
                dq = dq_acc[a] * SCALE
                dq_ref[a, :, :QK_NOPE] = dq[:, :QK_NOPE].astype(BF16)
                dq_ref[a, :, QK_NOPE:] = _rotate(dq[:, QK_NOPE:], c_ref[...], sa_ref[...], sb_ref[...], -1.0).astype(BF16)

    row = pl.BlockSpec((hp, 1, T), lambda h, j: (h, 0, 0))
    head = pl.BlockSpec((TQ, hp * 128), lambda h, j: (j, h))
    whole = pl.BlockSpec((hp, T, QK_PAD), lambda h, j: (h, 0, 0))
    tab = pl.BlockSpec((T, 128), lambda h, j: (0, 0))
    heads = pl.BlockSpec((T, hp * V_HEAD), lambda h, j: (0, h))
    return _pallas(
        body, name="attn_bwd", grid=(N_HEADS // hp, nq),
        in_specs=[whole, head, pl.BlockSpec((TQ, 128), lambda h, j: (j, 0)), head, heads, heads, row, tab, tab, tab],
        out_specs=[whole, head, pl.BlockSpec((hp, TQ, 128), lambda h, j: (h, j, 0)), head],
        out_shape=[jax.ShapeDtypeStruct((N_HEADS, T, QK_PAD), BF16), jax.ShapeDtypeStruct((T, N_HEADS * QK_NOPE), BF16),
                   jax.ShapeDtypeStruct((N_HEADS, T, 128), F32), jax.ShapeDtypeStruct((T, N_HEADS * V_HEAD), BF16)],
        scratch_shapes=[pltpu.VMEM((hp, T, QK_PAD), F32), pltpu.VMEM((hp, 1, T), F32)],
        params=_params(("parallel", "arbitrary")))(q, kn, kr, v, o, do, lse_row, cos, sa, sb)


def _rms_bwd_math(xv, g, dy):
    r = lax.rsqrt(jnp.mean(xv * xv, axis=-1, keepdims=True) + EPS)
    xn = xv * r
    gdy = dy * g
    return r * (gdy - xn * jnp.mean(gdy * xn, axis=-1, keepdims=True)), jnp.sum(dy * xn, axis=0, keepdims=True)


def _attn_post(dq, dkn, dv, dkr, cq_raw, ckv_raw, h2, dres, g_ql, g_kvl, g_l1, g_kv, w_uq, w_uk, w_uv, w_dq, w_dkv,
               w_kr, tables):
    def body(dq_ref, dkn_ref, dv_ref, dkr_ref, cqr_ref, ckvr_ref, h_ref, res_ref, c_ref, sa_ref, sb_ref,
             gql_ref, gkvl_ref, gl1_ref, gkv_ref, wuq_ref, wuk_ref, wuv_ref, wdq_ref, wdkv_ref, wkr_ref,
             dcq_ref, dckv_ref, dkrr_ref, dh_ref, dhb_ref, dgql_ref, dgkvl_ref, dgl1_ref, dgkv_ref):
        dcq = lax.dot_general(dq_ref[0], wuq_ref[0], NN, preferred_element_type=F32)
        for h in range(1, N_HEADS):
            dcq = dcq + lax.dot_general(dq_ref[h], wuq_ref[h], NN, preferred_element_type=F32)
        dcq_raw, s_ql = _rms_bwd_math(cqr_ref[...], gql_ref[...], dcq)
        dcq_raw = dcq_raw.astype(BF16)
        dcq_ref[...] = dcq_raw
        dckv = (lax.dot_general(dkn_ref[...], wuk_ref[...], NT, preferred_element_type=F32)
                + lax.dot_general(dv_ref[...], wuv_ref[...], NT, preferred_element_type=F32))
        dckv_raw, s_kvl = _rms_bwd_math(ckvr_ref[...], gkvl_ref[...], dckv)
        dckv_raw = dckv_raw.astype(BF16)
        dckv_ref[...] = dckv_raw
        dkr = dkr_ref[0]
        for h in range(1, N_HEADS):
            dkr = dkr + dkr_ref[h]
        dkr_raw = _rotate(dkr, c_ref[...], sa_ref[...], sb_ref[...], -1.0).astype(BF16)
        dkrr_ref[...] = dkr_raw
        d_hn = lax.dot_general(dcq_raw, wdq_ref[...], NT, preferred_element_type=F32)
        d_hk = (lax.dot_general(dckv_raw, wdkv_ref[...], NT, preferred_element_type=F32)
                + lax.dot_general(dkr_raw, wkr_ref[...], NN, preferred_element_type=F32))
        xv = h_ref[...]
        r = lax.rsqrt(jnp.mean(xv * xv, axis=-1, keepdims=True) + EPS)
        xn = xv * r
        dx = res_ref[...]
        sums = [s_ql, s_kvl]
        for dy, g_ref in ((d_hn, gl1_ref), (d_hk, gkv_ref)):
            gdy = dy * g_ref[...]
            dx = dx + r * (gdy - xn * jnp.mean(gdy * xn, axis=-1, keepdims=True))
            sums.append(jnp.sum(dy * xn, axis=0, keepdims=True))
        dh_ref[...] = dx
        dhb_ref[...] = dx.astype(BF16)
        dg_refs = (dgql_ref, dgkvl_ref, dgl1_ref, dgkv_ref)

        @pl.when(pl.program_id(0) == 0)
        def _():
            for dg_ref, part in zip(dg_refs, sums):
                dg_ref[...] = part

        @pl.when(pl.program_id(0) > 0)
        def _():
            for dg_ref, part in zip(dg_refs, sums):
                dg_ref[...] += part

    def rows(d):
        return pl.BlockSpec((TS, d), lambda i: (i, 0))

    def heads(d):
        return pl.BlockSpec((N_HEADS, TS, d), lambda i: (0, i, 0))

    def whole(a):
        return pl.BlockSpec(a.shape, lambda i: (0,) * a.ndim)

    wholes = [g_ql, g_kvl, g_l1, g_kv, w_uq, w_uk, w_uv, w_dq, w_dkv, w_kr]
    vecs = [Q_LORA, KV_LORA, D, D]
    return _pallas(
        body, name="attn_post", grid=(T // TS,),
        in_specs=[heads(QK_PAD), rows(N_HEADS * QK_NOPE), rows(N_HEADS * V_HEAD), heads(128), rows(Q_LORA),
                  rows(KV_LORA), rows(D), rows(D), rows(128), rows(128), rows(128)] + [whole(a) for a in wholes],
        out_specs=[rows(Q_LORA), rows(KV_LORA), rows(128), rows(D), rows(D)]
        + [pl.BlockSpec((1, d), lambda i: (0, 0)) for d in vecs],
        out_shape=[jax.ShapeDtypeStruct((T, Q_LORA), BF16), jax.ShapeDtypeStruct((T, KV_LORA), BF16),
                   jax.ShapeDtypeStruct((T, 128), BF16), jax.ShapeDtypeStruct((T, D), F32),
                   jax.ShapeDtypeStruct((T, D), BF16)] + [jax.ShapeDtypeStruct((1, d), F32) for d in vecs],
        params=_params(("arbitrary",)))(dq, dkn, dv, dkr, cq_raw, ckv_raw, h2, dres, *tables, *wholes)


def _ffn_gup(name, dg, dv, hf):
    def body(dg_ref, dv_ref, hf_ref, o_ref):
        j = pl.program_id(0)

        @pl.when(j < N_FF_BLK)
        def _():
            o_ref[...] = lax.dot_general(dg_ref[...], hf_ref[...], TN, preferred_element_type=F32).astype(BF16)

        @pl.when(j >= N_FF_BLK)
        def _():
            o_ref[...] = lax.dot_general(dv_ref[...], hf_ref[...], TN, preferred_element_type=F32).astype(BF16)

    return _pallas(
        body, name=name, grid=(N_DEV,),
        in_specs=[pl.BlockSpec((None, T, FF_BLK), lambda j: (jnp.minimum(j, N_FF_BLK - 1), 0, 0)),
                  pl.BlockSpec((None, T, FF_BLK), lambda j: (jnp.maximum(j - N_FF_BLK, 0), 0, 0)),
                  pl.BlockSpec((T, D), lambda j: (0, 0))],
        out_specs=pl.BlockSpec((None, FF_BLK, D), lambda j: (j, 0, 0)),
        out_shape=jax.ShapeDtypeStruct((N_DEV, FF_BLK, D), BF16), params=_params(("parallel",)))(dg, dv, hf)


def _ffn_layer_fwd(tag, h, hf, ex, final=None):
    g, v, act = _ffn_up_act(f"{tag}_up", hf, ex.need(f"ffn_w_up{tag[1]}", hf), ex.need(f"ffn_cw{tag[1]}", hf),
                            ex.need(f"ffn_cb{tag[1]}", hf))
    ex.at(f"{tag}_up", act)
    if final is not None:
        return _down_final(act, ex.need(f"ffn_w_down{tag[1]}", act), h, *final), (hf, g, v, act)
    rows = pl.BlockSpec((TS, D), lambda i: (i, 0))
    out = _mm_sum(f"{tag}_down",
                  [(act, pl.BlockSpec((N_FF_BLK, TS, FF_BLK), lambda i: (0, i, 0)), ex.need(f"ffn_w_down{tag[1]}", act),
                    pl.BlockSpec((None, N_FF_BLK, FF_BLK, D), lambda i: (0, 0, 0, 0)), NN, 0)],
                  grid=(T // TS,), o_spec=rows, o_shape=(T, D), o_dtype=F32, add=h)
    ex.at(f"{tag}_down", out)
    return out, (hf, g, v, act)


def _ffn_layer_bwd(tag, h, gain, ex, saved, dh, dh_bf, post=None):
    hf, g, v, act = saved
    layer = tag[1]
    w_up, w_down4 = ex.need(f"ffn_w_up{layer}", dh_bf), ex.need(f"ffn_w_down{layer}", dh_bf)
    dg, dv, dcw, dcb = _ffn_dact(f"{tag}_dact", dh_bf, w_down4, g, v, ex.need(f"ffn_cw{layer}", dh_bf),
                                 ex.need(f"ffn_cb{layer}", dh_bf))
    ex.at(f"{tag}_dact", dg)
    g_down = _mm(f"{tag}_gdown", act, dh_bf, grid=(N_FF_BLK,),
                 a_spec=pl.BlockSpec((None, T, FF_BLK), lambda j: (j, 0, 0)),
                 b_spec=pl.BlockSpec((T, D), lambda j: (0, 0)),
                 o_spec=pl.BlockSpec((FF_BLK, D), lambda j: (j, 0)),
                 o_shape=(D_FF, D), o_dtype=BF16, dims=TN)
    g_up = _ffn_gup(f"{tag}_gup", dg, dv, hf)
    ex.grad("ffn_w_up", int(layer), g_up.reshape(1, N_DEV, FF_BLK, D))
    ex.grad("ffn_w_down", int(layer), g_down.reshape(1, N_DEV, D_FF // N_DEV, D))
    ex.at(f"{tag}_gup", g_up)
    part = pl.BlockSpec((N_FF_BLK, TR, FF_BLK), lambda i: (0, i, 0))
    dh_in, dh_in_bf, dgain, *onward = _mm_sum(
        f"{tag}_dhf",
        [(dg, part, w_up, pl.BlockSpec((None, N_FF_BLK, FF_BLK, D), lambda i: (0, 0, 0, 0)), NN, 0),
         (dv, part, w_up, pl.BlockSpec((None, N_FF_BLK, FF_BLK, D), lambda i: (0, 1, 0, 0)), NN, 0)],
        grid=(T // TR,), o_spec=pl.BlockSpec((TR, D), lambda i: (i, 0)), o_shape=(T, D), o_dtype=F32,
        norm_bwd=(h, [gain], dh), post=post)
    ex.at(f"{tag}_dhf", dh_in)
    return (dh_in, dh_in_bf, dgain[0], dcw, dcb, *onward)


def _local_step(x, pos, tgt, rep, ex):
    attn_norm, ffn_norm, final_norm = rep["attn_norm"], rep["ffn_norm"], rep["final_norm"]
    half = QK_ROPE // 2
    inv = 1.0 / (ROPE_THETA ** (jnp.arange(half, dtype=F32) / half))
    inv_freq = jnp.concatenate([inv, inv, jnp.zeros((128 - 2 * half,), F32)]).reshape(1, 128)
    tables = _rope_tables(pos, inv_freq)

    hn0 = _rms_fwd("l0_norm", x, attn_norm[0:1])
    ex.at("l0_norm", hn0)
    w_in = ex.need("sc_w_in", hn0)
    zb, zc, zu, y = _mixer_in(hn0, w_in, ex.need("sc_conv_w", hn0))
    ex.at("l0_in", y)
    h1 = _mm_rows("l0_out", y, ex.need("sc_w_out", y), NN, F32, D, tn=512, add=x)
    ex.at("l0_out", h1)
    h2, ffn0 = _ffn_layer_fwd("f0", h1, _rms_fwd("f0_norm", h1, ffn_norm[0:1]), ex)

    w_uq = ex.need("w_uq", h2)
    hk, hn1, ckv_raw, ckv, kr, kn, vv, cq_raw, cq, q = _attn_pre(
        h2, rep["kv_in_norm"], attn_norm[1:2], rep["kv_latent_norm"], rep["q_latent_norm"], ex.need("w_dkv", h2),
        ex.need("w_kr", h2), ex.need("w_uk", h2), ex.need("w_uv", h2), ex.need("w_dq", h2), w_uq, tables)

    o, lse = _attn_fwd(q, kn, kr, vv)
    ex.at("attn_fwd", o)
    w_o = ex.need("w_o", o)
    h3, hf1 = _out_norm("attn_out", o, w_o, h2, ffn_norm[1:2])
    (loss, dh4, dh4_bf, d_final), ffn1 = _ffn_layer_fwd("f1", h3, hf1, ex, final=(final_norm.reshape(1, D), tgt))

    dh3, dh3_bf, d_fn1, dcw1, dcb1, do = _ffn_layer_bwd("f1", h3, ffn_norm[1:2], ex, ffn1, dh4, dh4_bf,
                                                        post=(w_o, NT, N_HEADS * V_HEAD))
    ex.at("f1_bwd", dh3)

    dq_pre, dkn, dkr, dvv = _attn_bwd(q, kn, kr, vv, o, do, lse, tables)

    dcq_raw_bf, dckv_raw_bf, dkr_raw_bf, dh2, dh2_bf, d_qln, d_kvln, d_an1, d_kvin = _attn_post(
        dq_pre, dkn, dvv, dkr, cq_raw, ckv_raw, h2, dh3, rep["q_latent_norm"], rep["kv_latent_norm"], attn_norm[1:2],
        rep["kv_in_norm"], w_uq, ex.need("w_uk", dkn), ex.need("w_uv", dvv), ex.need("w_dq", dq_pre),
        ex.need("w_dkv", dkn), ex.need("w_kr", dkr), tables)
    g_uq, g_dq, g_o = _wgrads("g_q", [(dq_pre, cq, QK_NOPE + QK_ROPE), (hn1, dcq_raw_bf), (o, dh3_bf)])
    ex.grad("w_uq", None, g_uq.reshape(1, N_DEV, QK_NOPE + QK_ROPE, Q_LORA))
    ex.grad("w_dq", None, g_dq.reshape(1, N_DEV, D // N_DEV, Q_LORA))
    ex.grad("w_o", None, g_o.reshape(1, N_DEV, D // N_DEV, D))

    g_uk, g_uv, g_dkv, g_kr = _wgrads("g_kv", [(ckv, dkn), (ckv, dvv), (hk, dckv_raw_bf), (dkr_raw_bf, hk, QK_ROPE)])
    ex.grad("w_uk", None, g_uk)
    ex.grad("w_uv", None, g_uv)
    ex.grad("w_dkv", None, g_dkv.reshape(1, N_DEV, D // N_DEV, KV_LORA))
    ex.grad("w_kr", None, g_kr)
    ex.at("kv_bwd", dh2)

    dh1, dh1_bf, d_fn0, dcw0, dcb0 = _ffn_layer_bwd("f0", h1, ffn_norm[0:1], ex, ffn0, dh2, dh2_bf)
    ex.at("f0_bwd", dh1)

    ex.grad("sc_w_out", None, _mm_wgrad("g_sc_w_out", y, dh1_bf).reshape(1, N_DEV, D // N_DEV, D))
    dz, d_scw = _mixer_out_bwd(dh1_bf, ex.need("sc_w_out", dh1_bf), zb, zc, zu, ex.need("sc_conv_w", dh1_bf))
    g_in = _mm_wgrad("g_sc_w_in", hn0, dz)
    ex.grad("sc_w_in", None, g_in)
    ex.at("sc_bwd", g_in)
    ex.at("d_l0_in", g_in)
    w_in = ex.need("sc_w_in", dz)
    grad_x, _, (d_an0,) = _mm_sum(
        "d_l0_in", [(dz[None], pl.BlockSpec((1, TS, dz.shape[1]), lambda i: (0, i, 0)),
                     w_in[None], pl.BlockSpec((1,) + w_in.shape, lambda i: (0, 0, 0)), NT, 0)],
        norm_bwd=(x, [attn_norm[0:1]], dh1),
        grid=(T // TS,), o_spec=pl.BlockSpec((TS, D), lambda i: (i, 0)), o_shape=(T, D), o_dtype=F32)

    small = {
        "attn_norm": jnp.concatenate([d_an0, d_an1], axis=0),
        "ffn_norm": jnp.concatenate([d_fn0, d_fn1], axis=0),
        "final_norm": d_final.reshape(D),
        "kv_in_norm": d_kvin.reshape(D),
        "kv_latent_norm": d_kvln.reshape(KV_LORA),
        "q_latent_norm": d_qln,
        "ffn_conv_b": jnp.stack([dcb0, dcb1]).transpose(0, 2, 1, 3).reshape(2, D_FF),
        "sc_conv_w": d_scw,
        "ffn_conv_w": jnp.stack([dcw0, dcw1]).transpose(0, 2, 1, 3).reshape(2, 3, D_FF),
    }
    return loss, grad_x, small


def _place():
    return lax.axis_index("x"), lax.axis_index("y"), lax.axis_index("c")


def _peers():
    x, y, c = _place()
    return (x, y, 1 - c), [(1 - x, y), (x, 1 - y), (1 - x, 1 - y)]


def _window(ref, kind, dev):
    if kind == "blocked":
        return ref.at[:, dev]
    width = ref.shape[-1] // N_DEV
    return ref.at[:, pl.ds(pl.multiple_of(dev * width, 128), width)]


HBM_SPEC = pl.BlockSpec(memory_space=pltpu.HBM)
SEM_SPEC = pl.BlockSpec(memory_space=pltpu.SEMAPHORE)
EFFECT = pltpu.SideEffectType.DATAFLOW_SIDE_EFFECTING
TOKEN = jax.ShapeDtypeStruct((8, 128), F32)


def _hbm(a):
    return pltpu.with_memory_space_constraint(a, pltpu.HBM)


def _copies_start(name, jobs):
    nj = len(jobs)
    counts = [(len(srcs), len(lands)) for srcs, lands, _, _ in jobs]
    n_arr = sum(ns + nl for ns, nl in counts)

    def body(*refs):
        sems, token = refs[n_arr:n_arr + 2 * nj], refs[-1]
        at = 0
        for j, ((ns, nl), (_, _, ncopy, plan)) in enumerate(zip(counts, jobs)):
            copies = plan(refs[at:at + ns], refs[at + ns:at + ns + nl])
            assert len(copies) == ncopy
            for k, (sent, dst, to, _) in enumerate(copies):
                pltpu.make_async_remote_copy(src_ref=sent, dst_ref=dst, send_sem=sems[2 * j].at[k],
                                             recv_sem=sems[2 * j + 1].at[k], device_id=to, device_id_type=MESH).start()
            at += ns + nl
        token[...] = jnp.zeros_like(token)

    arrays = [a for srcs, lands, _, _ in jobs for a in list(srcs) + list(lands)]
    sem_shapes = [pltpu.SemaphoreType.DMA((ncopy,)) for _, _, ncopy, _ in jobs for _ in range(2)]
    outs = pl.pallas_call(
        body, name=name, in_specs=[HBM_SPEC] * n_arr,
        out_specs=[SEM_SPEC] * (2 * nj) + [HBM_SPEC] * n_arr + [VMEM_SPEC],
        out_shape=sem_shapes + [pltpu.HBM(a.shape, a.dtype) for a in arrays] + [TOKEN],
        input_output_aliases={i: 2 * nj + i for i in range(n_arr)},
        compiler_params=pltpu.CompilerParams(has_side_effects=EFFECT))(*[_hbm(a) for a in arrays])
    _Chain.last = outs[-1]
    flights, at = [], 2 * nj
    for j, (ns, nl) in enumerate(counts):
        flights.append((outs[2 * j], outs[2 * j + 1], list(outs[at:at + ns]), list(outs[at + ns:at + ns + nl])))
        at += ns + nl
    return flights


def _copies_wait(name, started, ncopy, plan):
    send, recv, srcs, lands = started
    ns, nl = len(srcs), len(lands)

    def body(*refs):
        send_ref, recv_ref, token = refs[ns + nl], refs[ns + nl + 1], refs[-1]
        copies = plan(refs[:ns], refs[ns:ns + nl])
        assert len(copies) == ncopy
        for k, (sent, _, to, landed) in enumerate(copies):
            cp = pltpu.make_async_remote_copy(src_ref=sent, dst_ref=landed, send_sem=send_ref.at[k],
                                              recv_sem=recv_ref.at[k], device_id=to, device_id_type=MESH)
            cp.wait_send()
            cp.wait_recv()
        token[...] = jnp.zeros_like(token)

    arrays = list(srcs) + list(lands)
    outs = pl.pallas_call(
        body, name=name, in_specs=[HBM_SPEC] * (ns + nl) + [SEM_SPEC] * 2 + [ANY_SPEC],
        out_specs=[HBM_SPEC] * (ns + nl) + [VMEM_SPEC], out_shape=[pltpu.HBM(a.shape, a.dtype) for a in arrays] + [TOKEN],
        input_output_aliases={i: i for i in range(ns + nl)},
        compiler_params=pltpu.CompilerParams(has_side_effects=EFFECT))(*arrays, send, recv, _Chain.last)
    _Chain.last = outs[-1]
    return list(outs[:ns]), list(outs[ns:-1])


def _plan_gather_chips(kinds):
    def plan(srcs, lands):
        x, y, c = _place()
        sibling, chips = _peers()
        out = []
        for t, kind in enumerate(kinds):
            mine = _window(lands[t], kind, 4 * x + 2 * y + c)
            out.append((srcs[t], mine, (x, y, c), mine))
            out.append((srcs[t], mine, sibling, _window(lands[t], kind, 4 * x + 2 * y + 1 - c)))
            for px, py in chips:
                out.append((srcs[t], mine, (px, py, c), _window(lands[t], kind, 4 * px + 2 * py + c)))
        return out
    return plan, 5 * len(kinds)


def _plan_gather_all(n):
    def plan(srcs, lands):
        x, y, c = _place()
        out = []
        for t in range(n):
            mine = lands[t].at[:, 4 * x + 2 * y + c]
            for m in range(N_DEV):
                px, py, pc = (1 - x if m & 4 else x), (1 - y if m & 2 else y), (1 - c if m & 1 else c)
                out.append((srcs[t], mine, (px, py, pc), lands[t].at[:, 4 * px + 2 * py + pc]))
        return out
    return plan, N_DEV * n


def _plan_gather_sibling(kinds):
    def plan(srcs, lands):
        _, _, c = _place()
        sibling, chips = _peers()
        out = []
        for t, kind in enumerate(kinds):
            for px, py in chips:
                w = _window(lands[t], kind, 4 * px + 2 * py + c)
                out.append((w, w, sibling, _window(lands[t], kind, 4 * px + 2 * py + 1 - c)))
        return out
    return plan, 3 * len(kinds)


def _plan_scatter_sibling(kinds):
    def plan(srcs, lands):
        _, _, c = _place()
        sibling, _ = _peers()
        out = []
        for t, kind in enumerate(kinds):
            for k in range(N_CHIP):
                out.append((_window(srcs[t], kind, 2 * k + 1 - c), lands[t].at[k], sibling, lands[t].at[k]))
        return out
    return plan, N_CHIP * len(kinds)


def _plan_scatter_chips(n):
    def plan(srcs, lands):
        x, y, c = _place()
        _, chips = _peers()
        out = []
        for t in range(n):
            for px, py in chips:
                out.append((srcs[t].at[2 * px + py], lands[t].at[2 * x + y], (px, py, c), lands[t].at[2 * px + py]))
        return out
    return plan, 3 * n


def _landing(shard, kind):
    if kind == "blocked":
        return lax.empty((shard.shape[0], N_DEV) + shard.shape[1:], shard.dtype)
    return lax.empty((shard.shape[0], N_DEV * shard.shape[1]), shard.dtype)


def _chip_sums(name, grads, kinds, recvs, c):
    n = len(grads)
    in_specs, out_specs, out_shape, args = [], [], [], []
    for gr, kind, rv in zip(grads, kinds, recvs):
        if kind == "blocked":
            rows, w = gr.shape[2], gr.shape[3]
            in_specs.append(pl.BlockSpec((None, None, rows, w), lambda k, cref: (0, 2 * k + cref[0], 0, 0)))
        else:
            rows, w = gr.shape[0], gr.shape[1] // N_DEV
            in_specs.append(pl.BlockSpec((rows, w), lambda k, cref: (0, 2 * k + cref[0])))
        blk = pl.BlockSpec((None, rows, w), lambda k, cref: (k, 0, 0))
        in_specs.append(blk)
        out_specs.append(blk)
        out_shape.append(jax.ShapeDtypeStruct((N_CHIP, rows, w), BF16))
        args += [gr, rv.reshape(N_CHIP, rows, w)]

    def body(*refs):
        for t in range(n):
            g_ref, r_ref, o_ref = refs[1 + 2 * t], refs[2 + 2 * t], refs[1 + 2 * n + t]
            o_ref[...] = (g_ref[...].astype(F32) + r_ref[...].astype(F32)).astype(BF16)

    return _pallas(body, name=name, n_prefetch=1, grid=(N_CHIP,), in_specs=in_specs, out_specs=out_specs,
                   out_shape=out_shape, params=_params(("parallel",)))(c, *args)


def _adamw_math(g, wv, mv, vv):
    m = ADAM_B1 * mv + (1.0 - ADAM_B1) * g
    v = ADAM_B2 * vv + (1.0 - ADAM_B2) * (g * g)
    m_hat = m / (1.0 - ADAM_B1 ** ADAM_STEP)
    v_hat = v / (1.0 - ADAM_B2 ** ADAM_STEP)
    delta = -ADAM_LR * (m_hat / (jnp.sqrt(v_hat) + ADAM_EPS) + ADAM_WD * wv)
    return delta, m, v


ADAM_STEPS = (4, 2)


def _adamw_group(name, items, chip_ids):
    n = len(items)
    in_specs, out_specs, out_shape, args, prevs = [], [], [], [chip_ids], []
    steps = next(s for s in ADAM_STEPS if all(item[2].shape[1] % (16 * s) == 0 for item in items))
    for own, recv, w3, m3, v3, layer, _ in items:
        nl, rows, w = w3.shape
        tr = rows // steps
        in_specs += [pl.BlockSpec((None, tr, w), lambda i, ids, slot=slot: (ids[slot], i, 0)) for slot in range(4)]
        slab = pl.BlockSpec((None, tr, w), lambda i, ids, layer=layer: (layer, i, 0))
        in_specs += [slab] * 3
        out_specs += [slab] * 4
        out_shape += [jax.ShapeDtypeStruct((nl, rows, w), F32)] * 4
        args += [own, recv, recv, recv, w3, m3, v3]
    aliases = {}
    for t, item in enumerate(items):
        if item[6] is not None:
            for k in range(4):
                aliases[len(args) + k] = 4 * t + k
            in_specs += [ANY_SPEC] * 4
            args += list(item[6])
            prevs.append(t)
    n_in = 1 + 7 * n + 4 * len(prevs)

    def body(*refs):
        for t in range(n):
            own_ref, r1_ref, r2_ref, r3_ref, w_ref, m_ref, v_ref = refs[1 + 7 * t:8 + 7 * t]
            g_ref, d_ref, nm_ref, nv_ref = refs[n_in + 4 * t:n_in + 4 * t + 4]
            g = ((own_ref[...].astype(F32) + r1_ref[...].astype(F32)) + r2_ref[...].astype(F32)) + r3_ref[...].astype(F32)
            g_ref[...] = g
            d_ref[...], nm_ref[...], nv_ref[...] = _adamw_math(g, w_ref[...], m_ref[...], v_ref[...])

    outs = _pallas(body, name=name, n_prefetch=1, grid=(steps,), in_specs=in_specs, out_specs=out_specs,
                   out_shape=out_shape, aliases=aliases, params=_params(("parallel",)))(*args)
    return [list(outs[4 * t:4 * t + 4]) for t in range(n)]


SHARD_ROWS = 8


def _adamw_small(gathered, ws, ms, vs, me):
    n = len(gathered)
    full = [w is not None for w in ws]
    sharded = [w is not None and w.ndim == 3 for w in ws]
    args = list(gathered)
    out_shape = []
    for t in range(n):
        shape = jax.ShapeDtypeStruct(ws[t].shape if sharded[t] else gathered[t].shape[2:], F32)
        if full[t]:
            args += [ws[t], ms[t], vs[t]]
            out_shape += [shape] * 4
        else:
            out_shape += [shape]

    def body(*refs):
        i_in, i_out = n, len(args) + 1
        me_ref = refs[len(args)]
        for t in range(n):
            p_ref = refs[t]
            if sharded[t]:
                w_ref, m_ref, v_ref = refs[i_in:i_in + 3]
                taps, layers, _ = w_ref.shape
                mine = pl.ds(pl.multiple_of(me_ref[0] * SHARD_ROWS, SHARD_ROWS), SHARD_ROWS)
                g = p_ref[0, 0, mine, :]
                for k in range(1, N_DEV):
                    g = g + p_ref[0, k, mine, :]
                for l in range(layers):
                    for k in range(taps):
                        at = (k, slice(l, l + 1), slice(None))
                        row = g[l * taps + k:l * taps + k + 1]
                        refs[i_out][at] = row
                        refs[i_out + 1][at], refs[i_out + 2][at], refs[i_out + 3][at] = _adamw_math(
                            row, w_ref[at], m_ref[at], v_ref[at])
                i_in += 3
                i_out += 4
                continue
            g = p_ref[0, 0]
            for k in range(1, N_DEV):
                g = g + p_ref[0, k]
            refs[i_out][...] = g
            if full[t]:
                w_ref, m_ref, v_ref = refs[i_in:i_in + 3]
                refs[i_out + 1][...], refs[i_out + 2][...], refs[i_out + 3][...] = _adamw_math(
                    g, w_ref[...], m_ref[...], v_ref[...])
                i_in += 3
                i_out += 4
            else:
                i_out += 1

    outs = _pallas(body, name="adamw_small",
                   in_specs=[VMEM_SPEC] * len(args) + [pl.BlockSpec(memory_space=pltpu.SMEM)],
                   out_specs=[VMEM_SPEC] * len(out_shape), out_shape=out_shape,
                   params=pltpu.CompilerParams(vmem_limit_bytes=VMEM_LIMIT))(*args, me)
    result, i = [], 0
    for t in range(n):
        k = 4 if full[t] else 1
        result.append(list(outs[i:i + k]))
        i += k
    return result


KIND = {"sc_w_in": "cols", "sc_w_out": "blocked", "w_dkv": "blocked", "w_kr": "cols", "w_uk": "cols", "w_uv": "cols",
        "w_dq": "blocked", "w_uq": "blocked", "w_o": "blocked", "ffn_w_up": "blocked", "ffn_w_down": "blocked",
        "conv": "blocked"}
GATHER_GROUPS = (("mixer", ("sc_w_in",)),
                 ("mixer2", ("sc_w_out", "conv")),
                 ("up0", ("ffn_w_up0",)),
                 ("down0", ("ffn_w_down0",)),
                 ("attn", ("w_dkv", "w_kr", "w_uk", "w_uv", "w_dq", "w_uq", "w_o")),
                 ("ffn1", ("ffn_w_up1", "ffn_w_down1")))
SCATTER_GROUPS = (("ffn1", (("ffn_w_up", 1), ("ffn_w_down", 1))),
                  ("attn", (("w_o", None), ("w_uq", None), ("w_dq", None), ("w_uk", None), ("w_uv", None),
                            ("w_dkv", None), ("w_kr", None))),
                  ("ffn0", (("ffn_w_up", 0), ("ffn_w_down", 0))),
                  ("mixer", (("sc_w_out", None), ("sc_w_in", None))))
SCHEDULE = {
    "begin": (("gather_start", "mixer"),),
    "cast": (("gather_start", "mixer2"),),
    "l0_norm": (("gather_forward", "mixer"), ("gather_forward", "mixer2"), ("gather_start", "up0")),
    "l0_out": (("gather_forward", "up0"), ("gather_start", "down0"), ("gather_start", "attn")),
    "f0_up": (("gather_forward", "down0"), ("gather_start", "ffn1")),
    "f0_down": (("gather_forward", "attn"),),
    "attn_fwd": (("gather_forward", "ffn1"),),
    "f1_gup": (("scatter_sibling", "ffn1"),),
    "f1_dhf": (("scatter_chips", "ffn1"),),
    "kv_bwd": (("scatter_sibling", "attn"),),
    "f0_dact": (("scatter_chips", "attn"),),
    "f0_gup": (("scatter_sibling", "ffn0"),),
    "f0_dhf": (("scatter_chips", "ffn0"),),
    "sc_bwd": (("scatter_sibling", "mixer"), ("scatter_done", "attn")),
    "d_l0_in": (("scatter_chips", "mixer"),),
}
FINISH = (("scatter_done", "ffn1"), ("scatter_done", "ffn0"), ("scatter_done", "mixer"))
STAGES = {"gather_start": 1, "gather_forward": 2, "gather_done": 3,
          "scatter_sibling": 1, "scatter_chips": 2, "scatter_done": 3}
SMALL_W_ROWS = 24


def _pack(arrays, rows):
    flat = jnp.concatenate([a.reshape(-1).astype(F32) for a in arrays])
    return jnp.pad(flat, (0, rows * 128 - flat.shape[0])).reshape(rows, 128)


def _cast_shards(items):
    arrays = []
    for a, _, _ in items:
        if not any(a is b for b in arrays):
            arrays.append(a)
    slot = [next(i for i, b in enumerate(arrays) if b is a) for a, _, _ in items]

    def body(*refs):
        for t, (a, layer, rows) in enumerate(items):
            w_ref, o_ref = refs[slot[t]], refs[len(arrays) + t]
            r, c = a.shape[-2:]
            if a.ndim == 3:
                o_ref[:, :r] = w_ref[(layer or 0):(layer or 0) + 1].astype(BF16)
                if rows > r:
                    o_ref[:, r:] = jnp.zeros((1, rows - r, c), BF16)
            else:
                o_ref[:r] = w_ref[...].astype(BF16)
                if rows > r:
                    o_ref[r:] = jnp.zeros((rows - r, c), BF16)

    out_shape = [jax.ShapeDtypeStruct(((1,) if a.ndim == 3 else ()) + (rows, a.shape[-1]), BF16)
                 for a, _, rows in items]
    return _pallas(body, name="cast_shards", in_specs=[VMEM_SPEC] * len(arrays), out_specs=[VMEM_SPEC] * len(items),
                   out_shape=out_shape, params=pltpu.CompilerParams(vmem_limit_bytes=VMEM_LIMIT))(*arrays)


STORED_TRANSPOSED = ("ffn_w_up", "w_uq", "w_kr")


def _stored(name, a):
    return jnp.swapaxes(a, -1, -2) if name in STORED_TRANSPOSED else a


def _base(name):
    if name.startswith("ffn_w_") and name[-1] in "01":
        return name[:-1], int(name[-1])
    return name, None


class _Exchange:
    def __init__(self, wts, mom, var, ffn_conv_b):
        self.wts, self.mom, self.var = wts, mom, var
        x, y, c = _place()
        self.c_arr = jnp.reshape(c, (1,)).astype(jnp.int32)
        chip = 2 * x + y
        self.chip_ids = jnp.stack([chip, chip ^ 1, chip ^ 2, chip ^ 3]).astype(jnp.int32)
        self.ready = {"ffn_cb0": ffn_conv_b.reshape(2, N_FF_BLK, 1, FF_BLK)[0],
                      "ffn_cb1": ffn_conv_b.reshape(2, N_FF_BLK, 1, FF_BLK)[1]}
        self.gathers, self.group_of = {}, {}
        self.grads, self.scatters, self.results, self.queue = {}, {}, {}, []
        for gname, names in GATHER_GROUPS:
            self.gathers[gname] = dict(stage=0, names=names, kinds=[KIND[_base(nm)[0]] for nm in names])
            for nm in names:
                self.group_of[nm] = gname
        for nm in ("sc_conv_w", "ffn_cw0", "ffn_cw1"):
            self.group_of[nm] = self.group_of["conv"]
        self.cast, self.f32 = {}, {}
        self.at("begin", None)
        later = [nm for gname, names in GATHER_GROUPS[1:] for nm in names if nm != "conv"]
        self.cast = dict(zip(later, _cast_shards([self._shard_f32(nm) for nm in later])))
        self.at("cast", None)

    def _shard_f32(self, name):
        base, layer = _base(name)
        if base not in self.f32:
            a = _stored(base, self.wts[base])
            self.f32[base] = a.reshape(a.shape[-2:]) if KIND[base] == "cols" else a.reshape((-1,) + a.shape[-2:])
        a = self.f32[base]
        return a, layer, {"w_kr": 128, "w_uq": QK_PAD}.get(base, a.shape[-2])

    def _shard(self, name):
        if name in self.cast:
            return self.cast[name]
        if name == "conv":
            return _pack([self.wts["sc_conv_w"], self.wts["ffn_conv_w"]], SMALL_W_ROWS).reshape(1, SMALL_W_ROWS, 128)
        base, layer = _base(name)
        a = _stored(base, self.wts[base])
        if layer is not None:
            a = a[layer:layer + 1]
        if KIND[base] == "cols":
            return a.reshape(a.shape[-2], a.shape[-1]).astype(BF16)
        return a.reshape((-1,) + a.shape[-2:]).astype(BF16)

    def _start(self, name, srcs, lands, ncopy, plan, st):
        self.queue.append((name, (srcs, lands, ncopy, plan), st))

    def _flush(self):
        if self.queue:
            flights = _copies_start("__".join(name for name, _, _ in self.queue), [job for _, job, _ in self.queue])
            for (_, _, st), flight in zip(self.queue, flights):
                st["flight"] = flight
            self.queue = []

    def _flight(self, st):
        self._flush()
        return st["flight"]

    def _gather_to(self, gname, stage, after):
        st = self.gathers[gname]
        if st["stage"] < 1 <= stage:
            shards = [self._shard(nm) for nm in st["names"]]
            lands = [_landing(s, kind) for s, kind in zip(shards, st["kinds"])]
            plan, ncopy = _plan_gather_chips(st["kinds"])
            self._start(f"ag_{gname}_chips", shards, lands, ncopy, plan, st)
            st["stage"] = 1
        if st["stage"] < 2 <= stage:
            plan, ncopy = _plan_gather_chips(st["kinds"])
            _, lands = _copies_wait(f"ag_{gname}_chips_wait", self._flight(st), ncopy, plan)
            plan, ncopy = _plan_gather_sibling(st["kinds"])
            self._start(f"ag_{gname}_sibling", [], lands, ncopy, plan, st)
            st["stage"] = 2
        if st["stage"] < 3 <= stage:
            plan, ncopy = _plan_gather_sibling(st["kinds"])
            _, lands = _copies_wait(f"ag_{gname}_sibling_wait", self._flight(st), ncopy, plan)
            for nm, land in zip(st["names"], lands):
                self._arrived(nm, land)
            st["stage"] = 3

    def _arrived(self, name, land):
        if name == "conv":
            conv = land.reshape(N_DEV, SMALL_W_ROWS * 128)
            self.ready["sc_conv_w"] = conv[:, :3 * 128].reshape(N_DEV, 3, 128).transpose(1, 0, 2).reshape(3, D)
            fcw = conv[:, 3 * 128:3 * 128 + 6 * 352].reshape(N_DEV, 2, 3, 352).transpose(1, 2, 0, 3)
            fcw = fcw.reshape(2, 3, N_FF_BLK, FF_BLK).transpose(0, 2, 1, 3)
            self.ready["ffn_cw0"], self.ready["ffn_cw1"] = fcw[0], fcw[1]
        elif name in ("sc_w_in", "w_uk", "w_uv", "w_kr") or name.startswith("ffn_w_up"):
            self.ready[name] = land
        elif name.startswith("ffn_w_down"):
            self.ready[name] = land.reshape(1, N_FF_BLK, FF_BLK, D)
        elif name == "w_uq":
            self.ready[name] = land.reshape(N_HEADS, QK_PAD, Q_LORA)
        else:
            self.ready[name] = land.reshape(D, land.shape[-1])

    def need(self, name, after):
        if name not in self.ready:
            self._gather_to(self.group_of[name], 3, after)
            self._flush()
        return self.ready[name]

    def grad(self, name, layer, array):
        self.grads[(name, layer)] = array

    def _scatter_to(self, gname, stage, after):
        keys = dict(SCATTER_GROUPS)[gname]
        st = self.scatters.setdefault(gname, dict(stage=0))
        kinds = [KIND[nm] for nm, _ in keys]
        if st["stage"] < 1 <= stage:
            grads = [self.grads[key] for key in keys]
            lands = []
            for gr, kind in zip(grads, kinds):
                shard = (gr.shape[0],) + gr.shape[2:] if kind == "blocked" else (gr.shape[0], gr.shape[1] // N_DEV)
                lands.append(lax.empty((N_CHIP,) + shard, BF16))
            plan, ncopy = _plan_scatter_sibling(kinds)
            self._start(f"rs_{gname}_sibling", grads, lands, ncopy, plan, st)
            st["stage"] = 1
        if st["stage"] < 2 <= stage:
            plan, ncopy = _plan_scatter_sibling(kinds)
            grads, recvs = _copies_wait(f"rs_{gname}_sibling_wait", self._flight(st), ncopy, plan)
            sums = _chip_sums(f"rs_{gname}_sums", grads, kinds, recvs, self.c_arr)
            lands = [lax.empty(s.shape, BF16) for s in sums]
            plan, ncopy = _plan_scatter_chips(len(sums))
            self._start(f"rs_{gname}_chips", sums, lands, ncopy, plan, st)
            st["stage"] = 2
        if st["stage"] < 3 <= stage:
            plan, ncopy = _plan_scatter_chips(len(keys))
            sums, recvs = _copies_wait(f"rs_{gname}_chips_wait", self._flight(st), ncopy, plan)
            items = []
            for (nm, layer), own, rv in zip(keys, sums, recvs):
                nl = 1 if layer is None else 2
                rows, w = own.shape[1], own.shape[2]
                w3, m3, v3 = (_stored(nm, src[nm]).reshape(nl, rows, w) for src in (self.wts, self.mom, self.var))
                items.append((own, rv, w3, m3, v3, 0 if layer is None else layer, self.results.get(nm)))
            outs = _adamw_group(f"adamw_{gname}", items, self.chip_ids)
            for (nm, _), out in zip(keys, outs):
                self.results[nm] = out
            st["stage"] = 3

    def at(self, place, after):
        for action, gname in SCHEDULE.get(place, ()):
            self._advance(action, gname, after)
        self._flush()

    def _advance(self, action, gname, after):
        if action.startswith("gather"):
            self._gather_to(gname, STAGES[action], after)
        else:
            self._scatter_to(gname, STAGES[action], after)

    def finish(self, after):
        for action, gname in FINISH:
            self._advance(action, gname, after)
        for gname, _ in SCATTER_GROUPS:
            self._scatter_to(gname, 3, after)
        return {nm: [_stored(nm, o.reshape(_stored(nm, self.wts[nm]).shape)) for o in outs]
                for nm, outs in self.results.items()}


REPLICATED = ("attn_norm", "ffn_norm", "final_norm", "kv_in_norm", "kv_latent_norm", "q_latent_norm", "ffn_conv_b")
WEIGHTS = ("attn_norm", "ffn_norm", "final_norm", "sc_w_in", "sc_conv_w", "sc_w_out", "kv_in_norm", "w_dkv",
           "kv_latent_norm", "w_kr", "w_uk", "w_uv", "w_dq", "q_latent_norm", "w_uq", "w_o", "ffn_w_up", "ffn_conv_w",
           "ffn_conv_b", "ffn_w_down")


def kernel(x, positions, attn_norm, ffn_norm, final_norm, sc_w_in, sc_conv_w, sc_w_out, kv_in_norm, w_dkv, kv_latent_norm, w_kr, w_uk, w_uv, w_dq, q_latent_norm, w_uq, w_o, ffn_w_up, ffn_conv_w, ffn_conv_b, ffn_w_down, loss_target, m_attn_norm, m_ffn_norm, m_final_norm, m_sc_w_in, m_sc_conv_w, m_sc_w_out, m_kv_in_norm, m_w_dkv, m_kv_latent_norm, m_w_kr, m_w_uk, m_w_uv, m_w_dq, m_q_latent_norm, m_w_uq, m_w_o, m_ffn_w_up, m_ffn_conv_w, m_ffn_conv_b, m_ffn_w_down, v_attn_norm, v_ffn_norm, v_final_norm, v_sc_w_in, v_sc_conv_w, v_sc_w_out, v_kv_in_norm, v_w_dkv, v_kv_latent_norm, v_w_kr, v_w_uk, v_w_uv, v_w_dq, v_q_latent_norm, v_w_uq, v_w_o, v_ffn_w_up, v_ffn_conv_w, v_ffn_conv_b, v_ffn_w_down):
    wts = dict(attn_norm=attn_norm, ffn_norm=ffn_norm, final_norm=final_norm, sc_w_in=sc_w_in, sc_conv_w=sc_conv_w,
               sc_w_out=sc_w_out, kv_in_norm=kv_in_norm, w_dkv=w_dkv, kv_latent_norm=kv_latent_norm, w_kr=w_kr,
               w_uk=w_uk, w_uv=w_uv, w_dq=w_dq, q_latent_norm=q_latent_norm, w_uq=w_uq, w_o=w_o, ffn_w_up=ffn_w_up,
               ffn_conv_w=ffn_conv_w, ffn_conv_b=ffn_conv_b, ffn_w_down=ffn_w_down)
    mom = dict(attn_norm=m_attn_norm, ffn_norm=m_ffn_norm, final_norm=m_final_norm, sc_w_in=m_sc_w_in,
               sc_conv_w=m_sc_conv_w, sc_w_out=m_sc_w_out, kv_in_norm=m_kv_in_norm, w_dkv=m_w_dkv,
               kv_latent_norm=m_kv_latent_norm, w_kr=m_w_kr, w_uk=m_w_uk, w_uv=m_w_uv, w_dq=m_w_dq,
               q_latent_norm=m_q_latent_norm, w_uq=m_w_uq, w_o=m_w_o, ffn_w_up=m_ffn_w_up, ffn_conv_w=m_ffn_conv_w,
               ffn_conv_b=m_ffn_conv_b, ffn_w_down=m_ffn_w_down)
    var = dict(attn_norm=v_attn_norm, ffn_norm=v_ffn_norm, final_norm=v_final_norm, sc_w_in=v_sc_w_in,
               sc_conv_w=v_sc_conv_w, sc_w_out=v_sc_w_out, kv_in_norm=v_kv_in_norm, w_dkv=v_w_dkv,
               kv_latent_norm=v_kv_latent_norm, w_kr=v_w_kr, w_uk=v_w_uk, w_uv=v_w_uv, w_dq=v_w_dq,
               q_latent_norm=v_q_latent_norm, w_uq=v_w_uq, w_o=v_w_o, ffn_w_up=v_ffn_w_up, ffn_conv_w=v_ffn_conv_w,
               ffn_conv_b=v_ffn_conv_b, ffn_w_down=v_ffn_w_down)
    xi, yi, ci = _place()
    me = 4 * xi + 2 * yi + ci
    _Chain.last = None

    ex = _Exchange(wts, mom, var, ffn_conv_b)
    rep = {
        "attn_norm": attn_norm, "ffn_norm": ffn_norm, "final_norm": final_norm,
        "kv_in_norm": kv_in_norm.reshape(1, D), "kv_latent_norm": kv_latent_norm.reshape(1, KV_LORA),
        "q_latent_norm": q_latent_norm.reshape(1, Q_LORA),
    }
    loss, grad_x, small = _local_step(x.reshape(T, D), positions.reshape(T, 1), loss_target.reshape(T, D), rep, ex)

    def rows_of(a):
        return a.reshape(-1, a.shape[-1])

    def device_rows(a):
        taps, c = a.shape[-2], a.shape[-1] // N_DEV
        rows = a.reshape(-1, taps, N_DEV, c).transpose(2, 0, 1, 3).reshape(N_DEV, -1, c)
        return jnp.pad(rows, ((0, 0), (0, SHARD_ROWS - rows.shape[1]), (0, 0))).reshape(N_DEV * SHARD_ROWS, c)

    def taps_first(a):
        return jnp.transpose(a, (1, 0, 2))

    sharded = ("sc_conv_w", "ffn_conv_w")
    shards = ([loss.reshape(1, 1, 128)] + [rows_of(small[nm])[None] for nm in REPLICATED]
              + [device_rows(small[nm])[None] for nm in sharded])
    plan, ncopy = _plan_gather_all(len(shards))
    flight, = _copies_start("ag_small", [(shards, [lax.empty((1, N_DEV) + s.shape[1:], F32) for s in shards], ncopy, plan)])
    results = ex.finish(grad_x)
    _, gathered = _copies_wait("ag_small_wait", flight, ncopy, plan)
    params = [[None] + [rows_of(src[nm]) for nm in REPLICATED] + [taps_first(src[nm]) for nm in sharded]
              for src in (wts, mom, var)]
    summed = _adamw_small(gathered, *params, me.astype(jnp.int32).reshape(1))
    loss_total = summed[0][0][0, 0]
    for nm, vals in zip(REPLICATED, summed[1:1 + len(REPLICATED)]):
        results[nm] = [a.reshape(wts[nm].shape) for a in vals]
    for nm, vals in zip(sharded, summed[1 + len(REPLICATED):]):
        results[nm] = [taps_first(a) for a in vals]

    outs = [loss_total, grad_x.reshape(1, T, D)]
    for slot in range(4):
        outs.extend(results[nm][slot] for nm in WEIGHTS)
    return tuple(outs)
```

```python
import jax
import jax.numpy as jnp
from jax import lax
from jax.experimental import pallas as pl
from jax.experimental.pallas import tpu as pltpu

F32 = jnp.float32
BF16 = jnp.bfloat16

T = 2048
D = 1024
N_HEADS = 8
QK_NOPE = 128
QK_ROPE = 64
V_HEAD = 128
Q_LORA = 384
KV_LORA = 256
D_FF = 2816
CHUNK = 64
ROPE_THETA = 10000.0
EPS = 1e-6
NEG_INF = -1e30
ADAM_LR = 0.001
ADAM_B1 = 0.9
ADAM_B2 = 0.999
ADAM_EPS = 1e-08
ADAM_WD = 0.01
ADAM_STEP = 10

N_DEV = 8
N_CHIP = 4
FF_BLK = D_FF * 2 // N_DEV
N_FF_BLK = D_FF // FF_BLK
QK_PAD = 256
HALO = 16

TM = 1024
TS = 512
TR = 256
TQ = 512
VMEM_LIMIT = 56 * 1024 * 1024

NN = (((1,), (0,)), ((), ()))
NT = (((1,), (1,)), ((), ()))
TN = (((0,), (0,)), ((), ()))
MESH = pl.DeviceIdType.MESH


def _params(sem):
    return pltpu.CompilerParams(dimension_semantics=sem, vmem_limit_bytes=VMEM_LIMIT)


ANY_SPEC = pl.BlockSpec(memory_space=pl.ANY)
VMEM_SPEC = pl.BlockSpec(memory_space=pltpu.VMEM)


class _Chain:
    last = None


def _pallas(body, *, name, in_specs, out_specs, out_shape, grid=(), scratch_shapes=(), n_prefetch=0, aliases=None,
            params=None):
    def run(*args):
        after = _Chain.last
        n_lead = len(args)
        specs, operands, fn = list(in_specs), list(args), body
        if after is not None:
            def fn(*refs):
                return body(*refs[:n_lead], *refs[n_lead + 1:])
            specs.append(ANY_SPEC)
            operands.append(after)
        kw = dict(name=name, out_shape=out_shape, input_output_aliases=aliases or {})
        if params is not None:
            kw["compiler_params"] = params
        if n_prefetch:
            kw["grid_spec"] = pltpu.PrefetchScalarGridSpec(
                num_scalar_prefetch=n_prefetch, grid=grid, in_specs=specs, out_specs=out_specs,
                scratch_shapes=scratch_shapes)
        else:
            kw.update(grid=grid, in_specs=specs, out_specs=out_specs, scratch_shapes=scratch_shapes)
        outs = pl.pallas_call(fn, **kw)(*operands)
        _Chain.last = outs[0] if isinstance(outs, (list, tuple)) else outs
        return outs
    return run


def _mm(name, a, b, *, grid, a_spec, b_spec, o_spec, o_shape, o_dtype, dims, k_axis=None, acc_shape=None,
        add=None, add_spec=None):
    nk = grid[k_axis] if k_axis is not None else 1
    has_add = add is not None

    def body(*refs):
        a_ref, b_ref = refs[0], refs[1]
        p = 2
        add_ref = None
        if has_add:
            add_ref = refs[p]
            p += 1
        o_ref = refs[p]
        p += 1
        r = lax.dot_general(a_ref[...].astype(BF16), b_ref[...].astype(BF16), dims, preferred_element_type=F32)
        if k_axis is None:
            if has_add:
                r = r + add_ref[...].astype(F32)
            o_ref[...] = r.astype(o_dtype)
        else:
            acc = refs[p]
            k = pl.program_id(k_axis)

            @pl.when(k == 0)
            def _():
                acc[...] = r

            @pl.when(k > 0)
            def _():
                acc[...] += r

            @pl.when(k == nk - 1)
            def _():
                t = acc[...]
                if has_add:
                    t = t + add_ref[...].astype(F32)
                o_ref[...] = t.astype(o_dtype)

    in_specs = [a_spec, b_spec]
    args = [a, b]
    if has_add:
        in_specs.append(add_spec if add_spec is not None else o_spec)
        args.append(add)
    sem = tuple("arbitrary" if ax == k_axis else "parallel" for ax in range(len(grid)))
    scratch = [pltpu.VMEM(acc_shape, F32)] if k_axis is not None else []
    return _pallas(body, name=name, grid=grid, in_specs=in_specs, out_specs=o_spec,
                   out_shape=jax.ShapeDtypeStruct(o_shape, o_dtype), scratch_shapes=scratch, params=_params(sem))(*args)


def _mm_sum(name, parts, *, grid, o_spec, o_shape, o_dtype, add=None, norm_bwd=None, post=None):
    has_add = add is not None
    np_ = len(parts)
    nn = 1 if norm_bwd is None else len(norm_bwd[1])
    has_res = norm_bwd is not None and norm_bwd[2] is not None
    has_post = post is not None

    def body(*refs):
        accs = [None] * nn
        for p, (_, _, _, _, dims, n) in enumerate(parts):
            a_ref, b_ref = refs[2 * p], refs[2 * p + 1]
            for k in range(a_ref.shape[0]):
                r = lax.dot_general(a_ref[k], b_ref[k], dims, preferred_element_type=F32)
                accs[n] = r if accs[n] is None else accs[n] + r
        if norm_bwd is None:
            acc = accs[0]
            if has_add:
                acc = acc + refs[2 * np_][...]
            refs[-1][...] = acc.astype(o_dtype)
            return
        x_ref, g_refs = refs[2 * np_], refs[2 * np_ + 1:2 * np_ + 1 + nn]
        n_in = 2 * np_ + 1 + nn + has_res + has_post
        dx_ref, dxb_ref, dg_refs = refs[n_in], refs[n_in + 1], refs[n_in + 2:n_in + 2 + nn]
        xv = x_ref[...]
        r = lax.rsqrt(jnp.mean(xv * xv, axis=-1, keepdims=True) + EPS)
        xn = xv * r
        dx = refs[2 * np_ + 1 + nn][...] if has_res else None
        sums = []
        for acc, g_ref in zip(accs, g_refs):
            gdy = acc * g_ref[...]
            t = r * (gdy - xn * jnp.mean(gdy * xn, axis=-1, keepdims=True))
            dx = t if dx is None else dx + t
            sums.append(jnp.sum(acc * xn, axis=0, keepdims=True))
        dx_ref[...] = dx
        dxb = dx.astype(BF16)
        dxb_ref[...] = dxb
        if has_post:
            refs[n_in + 2 + nn][...] = lax.dot_general(dxb, refs[n_in - 1][...], post[1],
                                                       preferred_element_type=F32).astype(BF16)

        @pl.when(pl.program_id(0) == 0)
        def _():
            for dg_ref, part in zip(dg_refs, sums):
                dg_ref[...] = part

        @pl.when(pl.program_id(0) > 0)
        def _():
            for dg_ref, part in zip(dg_refs, sums):
                dg_ref[...] += part

    in_specs, args = [], []
    for a, a_spec, b, b_spec, _, _ in parts:
        in_specs += [a_spec, b_spec]
        args += [a, b]
    if norm_bwd is None:
        if has_add:
            in_specs.append(o_spec)
            args.append(add)
        return _pallas(body, name=name, grid=grid, in_specs=in_specs, out_specs=o_spec,
                       out_shape=jax.ShapeDtypeStruct(o_shape, o_dtype),
                       params=_params(("parallel",) * len(grid)))(*args)
    x, gains, dres = norm_bwd
    vec = pl.BlockSpec((1, o_shape[1]), lambda i: (0, 0))
    in_specs += [o_spec] + [vec] * nn + ([o_spec] if has_res else [])
    args += [x] + list(gains) + ([dres] if has_res else [])
    out_specs = [o_spec, o_spec] + [vec] * nn
    out_shape = ([jax.ShapeDtypeStruct(o_shape, F32), jax.ShapeDtypeStruct(o_shape, BF16)]
                 + [jax.ShapeDtypeStruct((1, o_shape[1]), F32)] * nn)
    if has_post:
        in_specs.append(pl.BlockSpec(post[0].shape, lambda i: (0, 0)))
        args.append(post[0])
        out_specs.append(pl.BlockSpec((o_spec.block_shape[0], post[2]), lambda i: (i, 0)))
        out_shape.append(jax.ShapeDtypeStruct((o_shape[0], post[2]), BF16))
    outs = _pallas(body, name=name, grid=grid, in_specs=in_specs, out_specs=out_specs, out_shape=out_shape,
                   params=_params(("arbitrary",)))(*args)
    if has_post:
        return outs[0], outs[1], list(outs[2:2 + nn]), outs[2 + nn]
    return outs[0], outs[1], list(outs[2:])


def _mm_rows(name, a, b, dims, o_dtype, n_out, *, tn=None, add=None):
    k = a.shape[1]
    tn = n_out if tn is None else tn
    if dims == NN:
        b_spec = pl.BlockSpec((k, tn), lambda n, i: (0, n))
    else:
        b_spec = pl.BlockSpec((tn, k), lambda n, i: (n, 0))
    return _mm(name, a, b, grid=(n_out // tn, T // TM),
               a_spec=pl.BlockSpec((TM, k), lambda n, i: (i, 0)), b_spec=b_spec,
               o_spec=pl.BlockSpec((TM, tn), lambda n, i: (i, n)), o_shape=(T, n_out), o_dtype=o_dtype,
               dims=dims, add=add)


def _wgrads(name, jobs):
    jobs = [job if len(job) == 3 else (*job, job[0].shape[-1]) for job in jobs]
    arrays, index = [], {}
    for a, b, _ in jobs:
        for arr in (a, b):
            if id(arr) not in index:
                index[id(arr)] = len(arrays)
                arrays.append(arr)
    n_in = len(arrays)

    def body(*refs):
        for t, (a, b, rows) in enumerate(jobs):
            a_ref, b_ref, o_ref = refs[index[id(a)]], refs[index[id(b)]], refs[n_in + t]
            if a.ndim == 3:
                for h in range(a.shape[0]):
                    o_ref[h] = lax.dot_general(a_ref[h], b_ref[...], TN, preferred_element_type=F32)[:rows].astype(BF16)
            else:
                o_ref[...] = lax.dot_general(a_ref[...], b_ref[...], TN, preferred_element_type=F32)[:rows].astype(BF16)

    out_shape = [jax.ShapeDtypeStruct(a.shape[:-2] + (rows, b.shape[-1]), BF16) for a, b, rows in jobs]
    return _pallas(body, name=name, in_specs=[VMEM_SPEC] * n_in, out_specs=[VMEM_SPEC] * len(jobs), out_shape=out_shape,
                   params=pltpu.CompilerParams(vmem_limit_bytes=VMEM_LIMIT))(*arrays)


def _mm_wgrad(name, a, b, *, tn=512):
    k, n = a.shape[1], b.shape[1]
    tn = min(tn, n)
    return _mm(name, a, b, grid=(n // tn,),
               a_spec=pl.BlockSpec((T, k), lambda j: (0, 0)), b_spec=pl.BlockSpec((T, tn), lambda j: (0, j)),
               o_spec=pl.BlockSpec((k, tn), lambda j: (0, j)), o_shape=(k, n), o_dtype=BF16, dims=TN)


def _rms_fwd(name, x, g):
    d = x.shape[1]

    def body(x_ref, g_ref, o_ref):
        xv = x_ref[...]
        r = lax.rsqrt(jnp.mean(xv * xv, axis=-1, keepdims=True) + EPS)
        o_ref[...] = ((xv * r) * g_ref[...]).astype(BF16)

    return _pallas(
        body, name=name, grid=(T // TM,),
        in_specs=[pl.BlockSpec((TM, d), lambda i: (i, 0)), pl.BlockSpec((1, d), lambda i: (0, 0))],
        out_specs=pl.BlockSpec((TM, d), lambda i: (i, 0)),
        out_shape=jax.ShapeDtypeStruct((T, d), BF16), params=_params(("parallel",)))(x, g)


def _rms(xv, g):
    return (xv * lax.rsqrt(jnp.mean(xv * xv, axis=-1, keepdims=True) + EPS)) * g


def _out_norm(name, a, w, add, g):
    def body(a_ref, w_ref, add_ref, g_ref, h_ref, hn_ref):
        hv = lax.dot_general(a_ref[...], w_ref[...], NN, preferred_element_type=F32) + add_ref[...]
        h_ref[...] = hv
        hn_ref[...] = _rms(hv, g_ref[...]).astype(BF16)

    rows = pl.BlockSpec((TS, D), lambda i: (i, 0))
    return _pallas(
        body, name=name, grid=(T // TS,),
        in_specs=[pl.BlockSpec((TS, a.shape[1]), lambda i: (i, 0)), pl.BlockSpec(w.shape, lambda i: (0, 0)), rows,
                  pl.BlockSpec((1, D), lambda i: (0, 0))],
        out_specs=[rows, rows], out_shape=[jax.ShapeDtypeStruct((T, D), F32), jax.ShapeDtypeStruct((T, D), BF16)],
        params=_params(("parallel",)))(a, w, add, g)


def _down_final(act, w_down4, h_in, g, tgt):
    def body(a_ref, w_ref, hin_ref, g_ref, t_ref, loss_ref, dh_ref, dhb_ref, dg_ref):
        hv = lax.dot_general(a_ref[0], w_ref[0], NN, preferred_element_type=F32)
        for j in range(1, N_FF_BLK):
            hv = hv + lax.dot_general(a_ref[j], w_ref[j], NN, preferred_element_type=F32)
        hv = hv + hin_ref[...]
        r = lax.rsqrt(jnp.mean(hv * hv, axis=-1, keepdims=True) + EPS)
        xn = hv * r
        gv = g_ref[...]
        err = xn * gv - t_ref[...]
        part_loss = 0.5 * jnp.sum(jnp.mean(err * err, axis=-1, keepdims=True), axis=0, keepdims=True)
        dy = err * (1.0 / D)
        gdy = dy * gv
        dh = r * (gdy - xn * jnp.mean(gdy * xn, axis=-1, keepdims=True))
        dh_ref[...] = dh
        dhb_ref[...] = dh.astype(BF16)
        part = jnp.sum(dy * xn, axis=0, keepdims=True)
        first = pl.program_id(0) == 0

        @pl.when(first)
        def _():
            dg_ref[...] = part
            loss_ref[...] = jnp.broadcast_to(part_loss, (1, 128))

        @pl.when(jnp.logical_not(first))
        def _():
            dg_ref[...] += part
            loss_ref[...] += jnp.broadcast_to(part_loss, (1, 128))

    row = pl.BlockSpec((TS, D), lambda i: (i, 0))
    vec = pl.BlockSpec((1, D), lambda i: (0, 0))
    return _pallas(
        body, name="f1_down_loss", grid=(T // TS,),
        in_specs=[pl.BlockSpec((N_FF_BLK, TS, FF_BLK), lambda i: (0, i, 0)),
                  pl.BlockSpec((None, N_FF_BLK, FF_BLK, D), lambda i: (0, 0, 0, 0)), row, vec, row],
        out_specs=[pl.BlockSpec((1, 128), lambda i: (0, 0)), row, row, vec],
        out_shape=[jax.ShapeDtypeStruct((1, 128), F32), jax.ShapeDtypeStruct((T, D), F32),
                   jax.ShapeDtypeStruct((T, D), BF16), jax.ShapeDtypeStruct((1, D), F32)],
        params=_params(("arbitrary",)))(act, w_down4, h_in, g, tgt)


def _prev_idx(i, rows=TR):
    return jnp.maximum(i * (rows // HALO) - 1, 0)


def _next_idx(i, rows=TR):
    return jnp.minimum((i + 1) * (rows // HALO), T // HALO - 1)


def _causal_taps(ext):
    return pltpu.roll(ext, 2, 0)[HALO:], pltpu.roll(ext, 1, 0)[HALO:], ext[HALO:]


def _anticausal_taps(ext, n):
    rows = ext.shape[0]
    return pltpu.roll(ext, rows - 1, 0)[:n], pltpu.roll(ext, rows - 2, 0)[:n]


MIX_COLS = 512


def _mixer_in(hn, w_in, w):
    nc = D // MIX_COLS

    def body(h_ref, hh_ref, wb_ref, wc_ref, wu_ref, w_ref, b_ref, c_ref, u_ref, y_ref):
        i = pl.program_id(1)
        hv = h_ref[...]
        he = jnp.concatenate([hh_ref[...], hv], axis=0)
        ce = lax.dot_general(he, wc_ref[...], NN, preferred_element_type=F32).astype(BF16)
        ue = lax.dot_general(he, wu_ref[...], NN, preferred_element_type=F32).astype(BF16)
        bv = lax.dot_general(hv, wb_ref[...], NN, preferred_element_type=F32).astype(BF16)
        b_ref[...] = bv
        c_ref[...] = ce[HALO:]
        u_ref[...] = ue[HALO:]
        row = lax.broadcasted_iota(jnp.int32, (HALO + TS, 1), 0)
        cu = jnp.where(jnp.logical_or(i > 0, row >= HALO), ce.astype(F32) * ue.astype(F32), 0.0)
        x2, x1, x0 = _causal_taps(cu)
        wv = w_ref[...]
        cv = (x2 * wv[0:1] + x1 * wv[1:2]) + x0 * wv[2:3]
        y_ref[...] = (bv.astype(F32) * cv).astype(BF16)

    def cols(part):
        return pl.BlockSpec((D, MIX_COLS), lambda j, i: (0, part * nc + j))

    blk = pl.BlockSpec((TS, MIX_COLS), lambda j, i: (i, j))
    out = jax.ShapeDtypeStruct((T, D), BF16)
    return _pallas(
        body, name="l0_in", grid=(nc, T // TS),
        in_specs=[pl.BlockSpec((TS, D), lambda j, i: (i, 0)), pl.BlockSpec((HALO, D), lambda j, i: (_prev_idx(i, TS), 0)),
                  cols(0), cols(1), cols(2), pl.BlockSpec((3, MIX_COLS), lambda j, i: (0, j))],
        out_specs=[blk] * 4, out_shape=[out] * 4,
        params=_params(("parallel", "parallel")))(hn, hn, w_in, w_in, w_in, w)


def _mixer_out_bwd(dh, w_out, zb, zc, zu, w):
    last = T // TR - 1

    def body(dh_ref, dhn_ref, wo_ref, b_ref, bn_ref, c_ref, ch_ref, u_ref, uh_ref, w_ref, dz_ref, dw_ref):
        i = pl.program_id(0)
        dye = lax.dot_general(jnp.concatenate([dh_ref[...], dhn_ref[...]], axis=0), wo_ref[...], NT,
                              preferred_element_type=F32)
        cv_ = c_ref[...].astype(F32)
        uv = u_ref[...].astype(F32)
        cu = cv_ * uv
        cuh = jnp.where(i > 0, ch_ref[...].astype(F32) * uh_ref[...].astype(F32), 0.0)
        x2, x1, x0 = _causal_taps(jnp.concatenate([cuh, cu], axis=0))
        wv = w_ref[...]
        conv = (x2 * wv[0:1] + x1 * wv[1:2]) + x0 * wv[2:3]
        dyv = dye[:TR]
        dz_ref[:, 0:D] = (dyv * conv).astype(BF16)
        dconv = dyv * b_ref[...].astype(F32)
        dconv_n = jnp.where(i < last, dye[TR:] * bn_ref[...].astype(F32), 0.0)
        n1, n2 = _anticausal_taps(jnp.concatenate([dconv, dconv_n], axis=0), TR)
        dcu = (dconv * wv[2:3] + n1 * wv[1:2]) + n2 * wv[0:1]
        dz_ref[:, D:2 * D] = (dcu * uv).astype(BF16)
        dz_ref[:, 2 * D:3 * D] = (dcu * cv_).astype(BF16)
        part = jnp.concatenate([jnp.sum(dconv * x2, axis=0, keepdims=True),
                                jnp.sum(dconv * x1, axis=0, keepdims=True),
                                jnp.sum(dconv * x0, axis=0, keepdims=True)], axis=0)

        @pl.when(i == 0)
        def _():
            dw_ref[...] = part

        @pl.when(i > 0)
        def _():
            dw_ref[...] += part

    main = pl.BlockSpec((TR, D), lambda i: (i, 0))
    prev = pl.BlockSpec((HALO, D), lambda i: (_prev_idx(i), 0))
    nxt = pl.BlockSpec((HALO, D), lambda i: (_next_idx(i), 0))
    wspec = pl.BlockSpec((3, D), lambda i: (0, 0))
    return _pallas(
        body, name="d_l0_out", grid=(T // TR,),
        in_specs=[main, nxt, pl.BlockSpec((D, D), lambda i: (0, 0)), main, nxt, main, prev, main, prev, wspec],
        out_specs=[pl.BlockSpec((TR, 3 * D), lambda i: (i, 0)), wspec],
        out_shape=[jax.ShapeDtypeStruct((T, 3 * D), BF16), jax.ShapeDtypeStruct((3, D), F32)],
        params=_params(("arbitrary",)))(dh, dh, w_out, zb, zb, zc, zc, zu, zu, w)


def _sigmoid(x):
    return 0.5 * jnp.tanh(0.5 * x) + 0.5


def _ffn_up_act(name, hf, w_up, w, b):
    def body(h_ref, hh_ref, wg_ref, wv_ref, w_ref, b_ref, g_ref, v_ref, a_ref):
        i = pl.program_id(1)
        hv = h_ref[...]
        ge = lax.dot_general(jnp.concatenate([hh_ref[...], hv], axis=0), wg_ref[...], NT,
                             preferred_element_type=F32).astype(BF16)
        v = lax.dot_general(hv, wv_ref[...], NT, preferred_element_type=F32).astype(BF16)
        g_ref[...] = ge[HALO:]
        v_ref[...] = v
        ext = ge.astype(F32)
        row = lax.broadcasted_iota(jnp.int32, (HALO + TM, 1), 0)
        ext = jnp.where(jnp.logical_or(i > 0, row >= HALO), ext, 0.0)
        x2, x1, x0 = _causal_taps(ext)
        wv = w_ref[...]
        gc = ((x2 * wv[0:1] + x1 * wv[1:2]) + x0 * wv[2:3]) + b_ref[...]
        a_ref[...] = ((gc * _sigmoid(gc)) * v.astype(F32)).astype(BF16)

    blk = pl.BlockSpec((None, TM, FF_BLK), lambda j, i: (j, i, 0))
    out = jax.ShapeDtypeStruct((N_FF_BLK, T, FF_BLK), BF16)
    return _pallas(
        body, name=name, grid=(N_FF_BLK, T // TM),
        in_specs=[pl.BlockSpec((TM, D), lambda j, i: (i, 0)),
                  pl.BlockSpec((HALO, D), lambda j, i: (_prev_idx(i, TM), 0)),
                  pl.BlockSpec((None, None, FF_BLK, D), lambda j, i: (0, j, 0, 0)),
                  pl.BlockSpec((None, None, FF_BLK, D), lambda j, i: (0, j + N_FF_BLK, 0, 0)),
                  pl.BlockSpec((None, 3, FF_BLK), lambda j, i: (j, 0, 0)),
                  pl.BlockSpec((None, 1, FF_BLK), lambda j, i: (j, 0, 0))],
        out_specs=[blk, blk, blk], out_shape=[out, out, out],
        params=_params(("parallel", "parallel")))(hf, hf, w_up, w_up, w, b)


def _ffn_dact(name, dh, w_down4, g, v, w, b):
    last = T // TS - 1

    def body(dh_ref, dhn_ref, wd_ref, g_ref, gp_ref, gn_ref, v_ref, vn_ref, w_ref, b_ref, dg_ref, dv_ref, dw_ref, db_ref):
        i = pl.program_id(1)
        da = lax.dot_general(jnp.concatenate([dh_ref[...], dhn_ref[...]], axis=0), wd_ref[...], NT,
                             preferred_element_type=F32)
        row = lax.broadcasted_iota(jnp.int32, (TS + HALO, 1), 0)
        da = jnp.where(jnp.logical_or(i < last, row < TS), da, 0.0)
        gp = jnp.where(i > 0, gp_ref[...].astype(F32), 0.0)
        ext = jnp.concatenate([gp, g_ref[...].astype(F32), gn_ref[...].astype(F32)], axis=0)
        x2, x1, x0 = _causal_taps(ext)
        wv = w_ref[...]
        gc = ((x2 * wv[0:1] + x1 * wv[1:2]) + x0 * wv[2:3]) + b_ref[...]
        sg = _sigmoid(gc)
        vv = jnp.concatenate([v_ref[...].astype(F32), vn_ref[...].astype(F32)], axis=0)
        silu = gc * sg
        dv_ref[...] = (da[:TS] * silu[:TS]).astype(BF16)
        dgc = (da * vv) * (sg + silu * (1.0 - sg))
        n1, n2 = _anticausal_taps(dgc, TS)
        d0 = dgc[:TS]
        dg_ref[...] = ((d0 * wv[2:3] + n1 * wv[1:2]) + n2 * wv[0:1]).astype(BF16)
        part_w = jnp.concatenate([jnp.sum(d0 * x2[:TS], axis=0, keepdims=True),
                                  jnp.sum(d0 * x1[:TS], axis=0, keepdims=True),
                                  jnp.sum(d0 * x0[:TS], axis=0, keepdims=True)], axis=0)
        part_b = jnp.sum(d0, axis=0, keepdims=True)

        @pl.when(i == 0)
        def _():
            dw_ref[...] = part_w
            db_ref[...] = part_b

        @pl.when(i > 0)
        def _():
            dw_ref[...] += part_w
            db_ref[...] += part_b

    blk = pl.BlockSpec((None, TS, FF_BLK), lambda j, i: (j, i, 0))
    prev = pl.BlockSpec((None, HALO, FF_BLK), lambda j, i: (j, _prev_idx(i, TS), 0))
    nxt = pl.BlockSpec((None, HALO, FF_BLK), lambda j, i: (j, _next_idx(i, TS), 0))
    wspec = pl.BlockSpec((None, 3, FF_BLK), lambda j, i: (j, 0, 0))
    bspec = pl.BlockSpec((None, 1, FF_BLK), lambda j, i: (j, 0, 0))
    return _pallas(
        body, name=name, grid=(N_FF_BLK, T // TS),
        in_specs=[pl.BlockSpec((TS, D), lambda j, i: (i, 0)),
                  pl.BlockSpec((HALO, D), lambda j, i: (_next_idx(i, TS), 0)),
                  pl.BlockSpec((None, None, FF_BLK, D), lambda j, i: (0, j, 0, 0)),
                  blk, prev, nxt, blk, nxt, wspec, bspec],
        out_specs=[blk, blk, wspec, bspec],
        out_shape=[jax.ShapeDtypeStruct((N_FF_BLK, T, FF_BLK), BF16), jax.ShapeDtypeStruct((N_FF_BLK, T, FF_BLK), BF16),
                   jax.ShapeDtypeStruct((N_FF_BLK, 3, FF_BLK), F32), jax.ShapeDtypeStruct((N_FF_BLK, 1, FF_BLK), F32)],
        params=_params(("parallel", "arbitrary")))(dh, dh, w_down4, g, g, g, v, v, w, b)


def _rope_tables(pos, inv_freq):
    half = QK_ROPE // 2

    def body(p_ref, f_ref, c_ref, sa_ref, sb_ref):
        ang = p_ref[...].astype(F32) * f_ref[...]
        lane = lax.broadcasted_iota(jnp.int32, (T, 128), 1)
        c = jnp.cos(ang)
        s = jnp.sin(ang)
        c_ref[...] = jnp.where(lane < 2 * half, c, 0.0)
        sa_ref[...] = jnp.where(lane < half, -s, 0.0)
        sb_ref[...] = jnp.where(jnp.logical_and(lane >= half, lane < 2 * half), s, 0.0)

    return _pallas(
        body, name="rope_tables", in_specs=[VMEM_SPEC] * 2, out_specs=[VMEM_SPEC] * 3,
        out_shape=[jax.ShapeDtypeStruct((T, 128), F32)] * 3,
        params=pltpu.CompilerParams(vmem_limit_bytes=VMEM_LIMIT))(pos, inv_freq)


def _rotate(r, c, sa, sb, sign):
    return r * c + sign * (pltpu.roll(r, 96, 1) * sa + pltpu.roll(r, 32, 1) * sb)


def _attn_pre(h2, g_kv, g_l1, g_kvl, g_ql, w_dkv, w_kr, w_uk, w_uv, w_dq, w_uq, tables):
    def body(h_ref, c_ref, sa_ref, sb_ref, gkv_ref, gl1_ref, gkvl_ref, gql_ref, wdkv_ref, wkr_ref, wuk_ref, wuv_ref,
             wdq_ref, wuq_ref, hk_ref, hn_ref, ckvr_ref, ckv_ref, kr_ref, kn_ref, v_ref, cqr_ref, cq_ref, q_ref):
        xv = h_ref[...]
        xn = xv * lax.rsqrt(jnp.mean(xv * xv, axis=-1, keepdims=True) + EPS)
        hk = (xn * gkv_ref[...]).astype(BF16)
        hn = (xn * gl1_ref[...]).astype(BF16)
        hk_ref[...] = hk
        hn_ref[...] = hn
        cv, sav, sbv = c_ref[...], sa_ref[...], sb_ref[...]
        raw = lax.dot_general(hk, wdkv_ref[...], NN, preferred_element_type=F32)
        ckvr_ref[...] = raw
        ckv = _rms(raw, gkvl_ref[...]).astype(BF16)
        ckv_ref[...] = ckv
        kr = lax.dot_general(hk, wkr_ref[...], NT, preferred_element_type=F32)
        kr_ref[...] = _rotate(kr, cv, sav, sbv, 1.0).astype(BF16)
        kn_ref[...] = lax.dot_general(ckv, wuk_ref[...], NN, preferred_element_type=F32).astype(BF16)
        v_ref[...] = lax.dot_general(ckv, wuv_ref[...], NN, preferred_element_type=F32).astype(BF16)
        cqr = lax.dot_general(hn, wdq_ref[...], NN, preferred_element_type=F32)
        cqr_ref[...] = cqr
        cq = _rms(cqr, gql_ref[...]).astype(BF16)
        cq_ref[...] = cq
        for h in range(N_HEADS):
            r = lax.dot_general(cq, wuq_ref[h], NT, preferred_element_type=F32)
            q_ref[h, :, :QK_NOPE] = (r[:, :QK_NOPE] * SCALE2).astype(BF16)
            q_ref[h, :, QK_NOPE:] = (_rotate(r[:, QK_NOPE:], cv, sav, sbv, 1.0) * SCALE2).astype(BF16)

    def rows(d):
        return pl.BlockSpec((TS, d), lambda i: (i, 0))

    def whole(a):
        return pl.BlockSpec(a.shape, lambda i: (0,) * a.ndim)

    wholes = [g_kv, g_l1, g_kvl, g_ql, w_dkv, w_kr, w_uk, w_uv, w_dq, w_uq]
    outs = [(D, BF16), (D, BF16), (KV_LORA, F32), (KV_LORA, BF16), (128, BF16), (N_HEADS * QK_NOPE, BF16),
            (N_HEADS * V_HEAD, BF16), (Q_LORA, F32), (Q_LORA, BF16)]
    return _pallas(
        body, name="attn_pre", grid=(T // TS,),
        in_specs=[rows(D), rows(128), rows(128), rows(128)] + [whole(a) for a in wholes],
        out_specs=[rows(d) for d, _ in outs] + [pl.BlockSpec((N_HEADS, TS, QK_PAD), lambda i: (0, i, 0))],
        out_shape=[jax.ShapeDtypeStruct((T, d), dt) for d, dt in outs]
        + [jax.ShapeDtypeStruct((N_HEADS, T, QK_PAD), BF16)],
        params=_params(("parallel",)))(h2, *tables, *wholes)


SCALE = (QK_NOPE + QK_ROPE) ** -0.5
LOG2E = 1.4426950408889634
SCALE2 = SCALE * LOG2E


def _diag_mask(transposed):
    shift = CHUNK.bit_length() - 1
    a = lax.broadcasted_iota(jnp.int32, (TQ, TQ), 0) >> shift
    b = lax.broadcasted_iota(jnp.int32, (TQ, TQ), 1) >> shift
    return (a <= b) if transposed else (b <= a)


def _as_row(col):
    return jnp.transpose(jnp.broadcast_to(col, (col.shape[0], 128)), (1, 0))[0:1]


def _attn_fwd(q, kn, kr, v):
    hp = 4

    def body(q_ref, kn_ref, kr_ref, v_ref, o_ref, lse_ref):
        i = pl.program_id(1)
        qs = [q_ref[a] for a in range(hp)]

        def step(j, carry, masked):
            off = pl.multiple_of(j * TQ, TQ)
            krv = kr_ref[pl.ds(off, TQ), :]
            ss = []
            for a in range(hp):
                kk = jnp.concatenate([kn_ref[pl.ds(off, TQ), a * QK_NOPE:(a + 1) * QK_NOPE], krv], axis=1)
                ss.append(lax.dot_general(qs[a], kk, NT, preferred_element_type=F32))
            out = []
            for a in range(hp):
                m, l, acc = carry[a]
                s = ss[a]
                if masked:
                    s = jnp.where(_diag_mask(False), s, NEG_INF)
                m_new = jnp.maximum(m, jnp.max(s, axis=-1, keepdims=True))
                p = jnp.exp2(s - m_new)
                alpha = jnp.exp2(m - m_new)
                l = alpha * l + jnp.sum(p, axis=-1, keepdims=True)
                pv = lax.dot_general(p.astype(BF16), v_ref[pl.ds(off, TQ), a * V_HEAD:(a + 1) * V_HEAD], NN,
                                     preferred_element_type=F32)
                out.append((m_new, l, alpha * acc + pv))
            return tuple(out)

        one = (jnp.full((TQ, 1), NEG_INF, F32), jnp.zeros((TQ, 1), F32), jnp.zeros((TQ, V_HEAD), F32))
        carry = lax.fori_loop(0, i, lambda j, cr: step(j, cr, False), (one,) * hp)
        carry = step(i, carry, True)
        for a, (m, l, acc) in enumerate(carry):
            o_ref[:, a * V_HEAD:(a + 1) * V_HEAD] = (acc / l).astype(BF16)
            lse_ref[a] = _as_row(m + jnp.log(l) * LOG2E)

    return _pallas(
        body, name="attn_fwd", grid=(N_HEADS // hp, T // TQ),
        in_specs=[pl.BlockSpec((hp, TQ, QK_PAD), lambda h, i: (h, i, 0)),
                  pl.BlockSpec((T, hp * QK_NOPE), lambda h, i: (0, h)),
                  pl.BlockSpec((T, 128), lambda h, i: (0, 0)),
                  pl.BlockSpec((T, hp * V_HEAD), lambda h, i: (0, h))],
        out_specs=[pl.BlockSpec((TQ, hp * V_HEAD), lambda h, i: (i, h)), pl.BlockSpec((hp, 1, TQ), lambda h, i: (h, 0, i))],
        out_shape=[jax.ShapeDtypeStruct((T, N_HEADS * V_HEAD), BF16), jax.ShapeDtypeStruct((N_HEADS, 1, T), F32)],
        params=_params(("parallel", "parallel")))(q, kn, kr, v)


def _attn_bwd(q, kn, kr, v, o, do, lse_row, tables):
    nq = T // TQ
    hp = 2
    cos, sa, sb = tables

    def body(q_ref, kn_ref, kr_ref, v_ref, o_ref, do_ref, lse_ref, c_ref, sa_ref, sb_ref,
             dq_ref, dkn_ref, dkr_ref, dv_ref, dq_acc, dl_ref):
        j = pl.program_id(1)

        def cols(a):
            return slice(a * 128, (a + 1) * 128)

        @pl.when(j == 0)
        def _():
            dq_acc[...] = jnp.zeros_like(dq_acc)
            for a in range(hp):
                for i in range(nq):
                    rows = pl.ds(i * TQ, TQ)
                    prod = do_ref[rows, cols(a)].astype(F32) * o_ref[rows, cols(a)].astype(F32)
                    dl_ref[a, :, rows] = _as_row(jnp.sum(prod, axis=-1, keepdims=True))

        krv = kr_ref[...]
        kks = [jnp.concatenate([kn_ref[:, cols(a)], krv], axis=1) for a in range(hp)]
        vvs = [v_ref[:, cols(a)] for a in range(hp)]

        def step(i, carry, masked):
            off = pl.multiple_of(i * TQ, TQ)
            rows = pl.ds(off, TQ)
            qis = [q_ref[a, rows, :] for a in range(hp)]
            dois = [do_ref[rows, cols(a)] for a in range(hp)]
            sts = [lax.dot_general(kks[a], qis[a], NT, preferred_element_type=F32) for a in range(hp)]
            dpts = [lax.dot_general(vvs[a], dois[a], NT, preferred_element_type=F32) for a in range(hp)]
            out = []
            for a in range(hp):
                dk, dv = carry[a]
                st = sts[a]
                if masked:
                    st = jnp.where(_diag_mask(True), st, NEG_INF)
                pt = jnp.exp2(st - lse_ref[a, :, rows])
                dv = dv + lax.dot_general(pt.astype(BF16), dois[a], NN, preferred_element_type=F32)
                dst = (pt * (dpts[a] - dl_ref[a, :, rows])).astype(BF16)
                dk = dk + lax.dot_general(dst, qis[a], NN, preferred_element_type=F32)
                dq_acc[a, rows, :] += lax.dot_general(dst, kks[a], TN, preferred_element_type=F32)
                out.append((dk, dv))
            return tuple(out)

        zero = (jnp.zeros((TQ, QK_PAD), F32), jnp.zeros((TQ, V_HEAD), F32))
        carry = step(j, (zero,) * hp, True)
        carry = lax.fori_loop(j + 1, nq, lambda i, cr: step(i, cr, False), carry)
        for a, (dk, dv) in enumerate(carry):
            dk = dk * (SCALE / SCALE2)
            dkn_ref[:, cols(a)] = dk[:, :QK_NOPE].astype(BF16)
            dkr_ref[a] = dk[:, QK_NOPE:]
            dv_ref[:, cols(a)] = dv.astype(BF16)

        @pl.when(j == nq - 1)
        def _():
            for a in range(hp):
                dq = dq_acc[a] * SCALE
                dq_ref[a, :, :QK_NOPE] = dq[:, :QK_NOPE].astype(BF16)
                dq_ref[a, :, QK_NOPE:] = _rotate(dq[:, QK_NOPE:], c_ref[...], sa_ref[...], sb_ref[...], -1.0).astype(BF16)

    row = pl.BlockSpec((hp, 1, T), lambda h, j: (h, 0, 0))
    head = pl.BlockSpec((TQ, hp * 128), lambda h, j: (j, h))
    whole = pl.BlockSpec((hp, T, QK_PAD), lambda h, j: (h, 0, 0))
    tab = pl.BlockSpec((T, 128), lambda h, j: (0, 0))
    heads = pl.BlockSpec((T, hp * V_HEAD), lambda h, j: (0, h))
    return _pallas(
        body, name="attn_bwd", grid=(N_HEADS // hp, nq),
        in_specs=[whole, head, pl.BlockSpec((TQ, 128), lambda h, j: (j, 0)), head, heads, heads, row, tab, tab, tab],
        out_specs=[whole, head, pl.BlockSpec((hp, TQ, 128), lambda h, j: (h, j, 0)), head],
        out_shape=[jax.ShapeDtypeStruct((N_HEADS, T, QK_PAD), BF16), jax.ShapeDtypeStruct((T, N_HEADS * QK_NOPE), BF16),
                   jax.ShapeDtypeStruct((N_HEADS, T, 128), F32), jax.ShapeDtypeStruct((T, N_HEADS * V_HEAD), BF16)],
        scratch_shapes=[pltpu.VMEM((hp, T, QK_PAD), F32), pltpu.VMEM((hp, 1, T), F32)],
        params=_params(("parallel", "arbitrary")))(q, kn, kr, v, o, do, lse_row, cos, sa, sb)


def _rms_bwd_math(xv, g, dy):
    r = lax.rsqrt(jnp.mean(xv * xv, axis=-1, keepdims=True) + EPS)
    xn = xv * r
    gdy = dy * g
    return r * (gdy - xn * jnp.mean(gdy * xn, axis=-1, keepdims=True)), jnp.sum(dy * xn, axis=0, keepdims=True)


def _attn_post(dq, dkn, dv, dkr, cq_raw, ckv_raw, h2, dres, g_ql, g_kvl, g_l1, g_kv, w_uq, w_uk, w_uv, w_dq, w_dkv,
               w_kr, tables):
    def body(dq_ref, dkn_ref, dv_ref, dkr_ref, cqr_ref, ckvr_ref, h_ref, res_ref, c_ref, sa_ref, sb_ref,
             gql_ref, gkvl_ref, gl1_ref, gkv_ref, wuq_ref, wuk_ref, wuv_ref, wdq_ref, wdkv_ref, wkr_ref,
             dcq_ref, dckv_ref, dkrr_ref, dh_ref, dhb_ref, dgql_ref, dgkvl_ref, dgl1_ref, dgkv_ref):
        dcq = lax.dot_general(dq_ref[0], wuq_ref[0], NN, preferred_element_type=F32)
        for h in range(1, N_HEADS):
            dcq = dcq + lax.dot_general(dq_ref[h], wuq_ref[h], NN, preferred_element_type=F32)
        dcq_raw, s_ql = _rms_bwd_math(cqr_ref[...], gql_ref[...], dcq)
        dcq_raw = dcq_raw.astype(BF16)
        dcq_ref[...] = dcq_raw
        dckv = (lax.dot_general(dkn_ref[...], wuk_ref[...], NT, preferred_element_type=F32)
                + lax.dot_general(dv_ref[...], wuv_ref[...], NT, preferred_element_type=F32))
        dckv_raw, s_kvl = _rms_bwd_math(ckvr_ref[...], gkvl_ref[...], dckv)
        dckv_raw = dckv_raw.astype(BF16)
        dckv_ref[...] = dckv_raw
        dkr = dkr_ref[0]
        for h in range(1, N_HEADS):
            dkr = dkr + dkr_ref[h]
        dkr_raw = _rotate(dkr, c_ref[...], sa_ref[...], sb_ref[...], -1.0).astype(BF16)
        dkrr_ref[...] = dkr_raw
        d_hn = lax.dot_general(dcq_raw, wdq_ref[...], NT, preferred_element_type=F32)
        d_hk = (lax.dot_general(dckv_raw, wdkv_ref[...], NT, preferred_element_type=F32)
                + lax.dot_general(dkr_raw, wkr_ref[...], NN, preferred_element_type=F32))
        xv = h_ref[...]
        r = lax.rsqrt(jnp.mean(xv * xv, axis=-1, keepdims=True) + EPS)
        xn = xv * r
        dx = res_ref[...]
        sums = [s_ql, s_kvl]
        for dy, g_ref in ((d_hn, gl1_ref), (d_hk, gkv_ref)):
            gdy = dy * g_ref[...]
            dx = dx + r * (gdy - xn * jnp.mean(gdy * xn, axis=-1, keepdims=True))
            sums.append(jnp.sum(dy * xn, axis=0, keepdims=True))
        dh_ref[...] = dx
        dhb_ref[...] = dx.astype(BF16)
        dg_refs = (dgql_ref, dgkvl_ref, dgl1_ref, dgkv_ref)

        @pl.when(pl.program_id(0) == 0)
        def _():
            for dg_ref, part in zip(dg_refs, sums):
                dg_ref[...] = part

        @pl.when(pl.program_id(0) > 0)
        def _():
            for dg_ref, part in zip(dg_refs, sums):
                dg_ref[...] += part

    def rows(d):
        return pl.BlockSpec((TS, d), lambda i: (i, 0))

    def heads(d):
        return pl.BlockSpec((N_HEADS, TS, d), lambda i: (0, i, 0))

    def whole(a):
        return pl.BlockSpec(a.shape, lambda i: (0,) * a.ndim)

    wholes = [g_ql, g_kvl, g_l1, g_kv, w_uq, w_uk, w_uv, w_dq, w_dkv, w_kr]
    vecs = [Q_LORA, KV_LORA, D, D]
    return _pallas(
        body, name="attn_post", grid=(T // TS,),
        in_specs=[heads(QK_PAD), rows(N_HEADS * QK_NOPE), rows(N_HEADS * V_HEAD), heads(128), rows(Q_LORA),
                  rows(KV_LORA), rows(D), rows(D), rows(128), rows(128), rows(128)] + [whole(a) for a in wholes],
        out_specs=[rows(Q_LORA), rows(KV_LORA), rows(128), rows(D), rows(D)]
        + [pl.BlockSpec((1, d), lambda i: (0, 0)) for d in vecs],
        out_shape=[jax.ShapeDtypeStruct((T, Q_LORA), BF16), jax.ShapeDtypeStruct((T, KV_LORA), BF16),
                   jax.ShapeDtypeStruct((T, 128), BF16), jax.ShapeDtypeStruct((T, D), F32),
                   jax.ShapeDtypeStruct((T, D), BF16)] + [jax.ShapeDtypeStruct((1, d), F32) for d in vecs],
        params=_params(("arbitrary",)))(dq, dkn, dv, dkr, cq_raw, ckv_raw, h2, dres, *tables, *wholes)


def _ffn_gup(name, dg, dv, hf):
    def body(dg_ref, dv_ref, hf_ref, o_ref):
        j = pl.program_id(0)

        @pl.when(j < N_FF_BLK)
        def _():
            o_ref[...] = lax.dot_general(dg_ref[...], hf_ref[...], TN, preferred_element_type=F32).astype(BF16)

        @pl.when(j >= N_FF_BLK)
        def _():
            o_ref[...] = lax.dot_general(dv_ref[...], hf_ref[...], TN, preferred_element_type=F32).astype(BF16)

    return _pallas(
        body, name=name, grid=(N_DEV,),
        in_specs=[pl.BlockSpec((None, T, FF_BLK), lambda j: (jnp.minimum(j, N_FF_BLK - 1), 0, 0)),
                  pl.BlockSpec((None, T, FF_BLK), lambda j: (jnp.maximum(j - N_FF_BLK, 0), 0, 0)),
                  pl.BlockSpec((T, D), lambda j: (0, 0))],
        out_specs=pl.BlockSpec((None, FF_BLK, D), lambda j: (j, 0, 0)),
        out_shape=jax.ShapeDtypeStruct((N_DEV, FF_BLK, D), BF16), params=_params(("parallel",)))(dg, dv, hf)


def _ffn_layer_fwd(tag, h, hf, ex, final=None):
    g, v, act = _ffn_up_act(f"{tag}_up", hf, ex.need(f"ffn_w_up{tag[1]}", hf), ex.need(f"ffn_cw{tag[1]}", hf),
                            ex.need(f"ffn_cb{tag[1]}", hf))
    ex.at(f"{tag}_up", act)
    if final is not None:
        return _down_final(act, ex.need(f"ffn_w_down{tag[1]}", act), h, *final), (hf, g, v, act)
    rows = pl.BlockSpec((TS, D), lambda i: (i, 0))
    out = _mm_sum(f"{tag}_down",
                  [(act, pl.BlockSpec((N_FF_BLK, TS, FF_BLK), lambda i: (0, i, 0)), ex.need(f"ffn_w_down{tag[1]}", act),
                    pl.BlockSpec((None, N_FF_BLK, FF_BLK, D), lambda i: (0, 0, 0, 0)), NN, 0)],
                  grid=(T // TS,), o_spec=rows, o_shape=(T, D), o_dtype=F32, add=h)
    ex.at(f"{tag}_down", out)
    return out, (hf, g, v, act)


def _ffn_layer_bwd(tag, h, gain, ex, saved, dh, dh_bf, post=None):
    hf, g, v, act = saved
    layer = tag[1]
    w_up, w_down4 = ex.need(f"ffn_w_up{layer}", dh_bf), ex.need(f"ffn_w_down{layer}", dh_bf)
    dg, dv, dcw, dcb = _ffn_dact(f"{tag}_dact", dh_bf, w_down4, g, v, ex.need(f"ffn_cw{layer}", dh_bf),
                                 ex.need(f"ffn_cb{layer}", dh_bf))
    ex.at(f"{tag}_dact", dg)
    g_down = _mm(f"{tag}_gdown", act, dh_bf, grid=(N_FF_BLK,),
                 a_spec=pl.BlockSpec((None, T, FF_BLK), lambda j: (j, 0, 0)),
                 b_spec=pl.BlockSpec((T, D), lambda j: (0, 0)),
                 o_spec=pl.BlockSpec((FF_BLK, D), lambda j: (j, 0)),
                 o_shape=(D_FF, D), o_dtype=BF16, dims=TN)
    g_up = _ffn_gup(f"{tag}_gup", dg, dv, hf)
    ex.grad("ffn_w_up", int(layer), g_up.reshape(1, N_DEV, FF_BLK, D))
    ex.grad("ffn_w_down", int(layer), g_down.reshape(1, N_DEV, D_FF // N_DEV, D))
    ex.at(f"{tag}_gup", g_up)
    part = pl.BlockSpec((N_FF_BLK, TS, FF_BLK), lambda i: (0, i, 0))
    dh_in, dh_in_bf, dgain, *onward = _mm_sum(
        f"{tag}_dhf",
        [(dg, part, w_up, pl.BlockSpec((None, N_FF_BLK, FF_BLK, D), lambda i: (0, 0, 0, 0)), NN, 0),
         (dv, part, w_up, pl.BlockSpec((None, N_FF_BLK, FF_BLK, D), lambda i: (0, 1, 0, 0)), NN, 0)],
        grid=(T // TS,), o_spec=pl.BlockSpec((TS, D), lambda i: (i, 0)), o_shape=(T, D), o_dtype=F32,
        norm_bwd=(h, [gain], dh), post=post)
    ex.at(f"{tag}_dhf", dh_in)
    return (dh_in, dh_in_bf, dgain[0], dcw, dcb, *onward)


def _local_step(x, pos, tgt, rep, ex):
    attn_norm, ffn_norm, final_norm = rep["attn_norm"], rep["ffn_norm"], rep["final_norm"]
    half = QK_ROPE // 2
    inv = 1.0 / (ROPE_THETA ** (jnp.arange(half, dtype=F32) / half))
    inv_freq = jnp.concatenate([inv, inv, jnp.zeros((128 - 2 * half,), F32)]).reshape(1, 128)
    tables = _rope_tables(pos, inv_freq)

    hn0 = _rms_fwd("l0_norm", x, attn_norm[0:1])
    ex.at("l0_norm", hn0)
    w_in = ex.need("sc_w_in", hn0)
    zb, zc, zu, y = _mixer_in(hn0, w_in, ex.need("sc_conv_w", hn0))
    ex.at("l0_in", y)
    h1 = _mm_rows("l0_out", y, ex.need("sc_w_out", y), NN, F32, D, tn=512, add=x)
    ex.at("l0_out", h1)
    h2, ffn0 = _ffn_layer_fwd("f0", h1, _rms_fwd("f0_norm", h1, ffn_norm[0:1]), ex)

    w_uq = ex.need("w_uq", h2)
    hk, hn1, ckv_raw, ckv, kr, kn, vv, cq_raw, cq, q = _attn_pre(
        h2, rep["kv_in_norm"], attn_norm[1:2], rep["kv_latent_norm"], rep["q_latent_norm"], ex.need("w_dkv", h2),
        ex.need("w_kr", h2), ex.need("w_uk", h2), ex.need("w_uv", h2), ex.need("w_dq", h2), w_uq, tables)

    o, lse = _attn_fwd(q, kn, kr, vv)
    ex.at("attn_fwd", o)
    w_o = ex.need("w_o", o)
    h3, hf1 = _out_norm("attn_out", o, w_o, h2, ffn_norm[1:2])
    (loss, dh4, dh4_bf, d_final), ffn1 = _ffn_layer_fwd("f1", h3, hf1, ex, final=(final_norm.reshape(1, D), tgt))

    dh3, dh3_bf, d_fn1, dcw1, dcb1, do = _ffn_layer_bwd("f1", h3, ffn_norm[1:2], ex, ffn1, dh4, dh4_bf,
                                                        post=(w_o, NT, N_HEADS * V_HEAD))
    ex.at("f1_bwd", dh3)

    dq_pre, dkn, dkr, dvv = _attn_bwd(q, kn, kr, vv, o, do, lse, tables)

    dcq_raw_bf, dckv_raw_bf, dkr_raw_bf, dh2, dh2_bf, d_qln, d_kvln, d_an1, d_kvin = _attn_post(
        dq_pre, dkn, dvv, dkr, cq_raw, ckv_raw, h2, dh3, rep["q_latent_norm"], rep["kv_latent_norm"], attn_norm[1:2],
        rep["kv_in_norm"], w_uq, ex.need("w_uk", dkn), ex.need("w_uv", dvv), ex.need("w_dq", dq_pre),
        ex.need("w_dkv", dkn), ex.need("w_kr", dkr), tables)
    g_uq, g_dq, g_o = _wgrads("g_q", [(dq_pre, cq, QK_NOPE + QK_ROPE), (hn1, dcq_raw_bf), (o, dh3_bf)])
    ex.grad("w_uq", None, g_uq.reshape(1, N_DEV, QK_NOPE + QK_ROPE, Q_LORA))
    ex.grad("w_dq", None, g_dq.reshape(1, N_DEV, D // N_DEV, Q_LORA))
    ex.grad("w_o", None, g_o.reshape(1, N_DEV, D // N_DEV, D))

    g_uk, g_uv, g_dkv, g_kr = _wgrads("g_kv", [(ckv, dkn), (ckv, dvv), (hk, dckv_raw_bf), (dkr_raw_bf, hk, QK_ROPE)])
    ex.grad("w_uk", None, g_uk)
    ex.grad("w_uv", None, g_uv)
    ex.grad("w_dkv", None, g_dkv.reshape(1, N_DEV, D // N_DEV, KV_LORA))
    ex.grad("w_kr", None, g_kr)
    ex.at("kv_bwd", dh2)

    dh1, dh1_bf, d_fn0, dcw0, dcb0 = _ffn_layer_bwd("f0", h1, ffn_norm[0:1], ex, ffn0, dh2, dh2_bf)
    ex.at("f0_bwd", dh1)

    ex.grad("sc_w_out", None, _mm_wgrad("g_sc_w_out", y, dh1_bf).reshape(1, N_DEV, D // N_DEV, D))
    dz, d_scw = _mixer_out_bwd(dh1_bf, ex.need("sc_w_out", dh1_bf), zb, zc, zu, ex.need("sc_conv_w", dh1_bf))
    g_in = _mm_wgrad("g_sc_w_in", hn0, dz)
    ex.grad("sc_w_in", None, g_in)
    ex.at("sc_bwd", g_in)
    ex.at("d_l0_in", g_in)
    w_in = ex.need("sc_w_in", dz)
    grad_x, _, (d_an0,) = _mm_sum(
        "d_l0_in", [(dz[None], pl.BlockSpec((1, TS, dz.shape[1]), lambda i: (0, i, 0)),
                     w_in[None], pl.BlockSpec((1,) + w_in.shape, lambda i: (0, 0, 0)), NT, 0)],
        norm_bwd=(x, [attn_norm[0:1]], dh1),
        grid=(T // TS,), o_spec=pl.BlockSpec((TS, D), lambda i: (i, 0)), o_shape=(T, D), o_dtype=F32)

    small = {
        "attn_norm": jnp.concatenate([d_an0, d_an1], axis=0),
        "ffn_norm": jnp.concatenate([d_fn0, d_fn1], axis=0),
        "final_norm": d_final.reshape(D),
        "kv_in_norm": d_kvin.reshape(D),
        "kv_latent_norm": d_kvln.reshape(KV_LORA),
        "q_latent_norm": d_qln,
        "ffn_conv_b": jnp.stack([dcb0, dcb1]).transpose(0, 2, 1, 3).reshape(2, D_FF),
        "sc_conv_w": d_scw,
        "ffn_conv_w": jnp.stack([dcw0, dcw1]).transpose(0, 2, 1, 3).reshape(2, 3, D_FF),
    }
    return loss, grad_x, small


def _place():
    return lax.axis_index("x"), lax.axis_index("y"), lax.axis_index("c")


def _peers():
    x, y, c = _place()
    return (x, y, 1 - c), [(1 - x, y), (x, 1 - y), (1 - x, 1 - y)]


def _window(ref, kind, dev):
    if kind == "blocked":
        return ref.at[:, dev]
    width = ref.shape[-1] // N_DEV
    return ref.at[:, pl.ds(pl.multiple_of(dev * width, 128), width)]


HBM_SPEC = pl.BlockSpec(memory_space=pltpu.HBM)
SEM_SPEC = pl.BlockSpec(memory_space=pltpu.SEMAPHORE)
EFFECT = pltpu.SideEffectType.DATAFLOW_SIDE_EFFECTING
TOKEN = jax.ShapeDtypeStruct((8, 128), F32)


def _hbm(a):
    return pltpu.with_memory_space_constraint(a, pltpu.HBM)


def _copies_start(name, jobs):
    nj = len(jobs)
    counts = [(len(srcs), len(lands)) for srcs, lands, _, _ in jobs]
    n_arr = sum(ns + nl for ns, nl in counts)

    def body(*refs):
        sems, token = refs[n_arr:n_arr + 2 * nj], refs[-1]
        at = 0
        for j, ((ns, nl), (_, _, ncopy, plan)) in enumerate(zip(counts, jobs)):
            copies = plan(refs[at:at + ns], refs[at + ns:at + ns + nl])
            assert len(copies) == ncopy
            for k, (sent, dst, to, _) in enumerate(copies):
                pltpu.make_async_remote_copy(src_ref=sent, dst_ref=dst, send_sem=sems[2 * j].at[k],
                                             recv_sem=sems[2 * j + 1].at[k], device_id=to, device_id_type=MESH).start()
            at += ns + nl
        token[...] = jnp.zeros_like(token)

    arrays = [a for srcs, lands, _, _ in jobs for a in list(srcs) + list(lands)]
    sem_shapes = [pltpu.SemaphoreType.DMA((ncopy,)) for _, _, ncopy, _ in jobs for _ in range(2)]
    outs = pl.pallas_call(
        body, name=name, in_specs=[HBM_SPEC] * n_arr,
        out_specs=[SEM_SPEC] * (2 * nj) + [HBM_SPEC] * n_arr + [VMEM_SPEC],
        out_shape=sem_shapes + [pltpu.HBM(a.shape, a.dtype) for a in arrays] + [TOKEN],
        input_output_aliases={i: 2 * nj + i for i in range(n_arr)},
        compiler_params=pltpu.CompilerParams(has_side_effects=EFFECT))(*[_hbm(a) for a in arrays])
    _Chain.last = outs[-1]
    flights, at = [], 2 * nj
    for j, (ns, nl) in enumerate(counts):
        flights.append((outs[2 * j], outs[2 * j + 1], list(outs[at:at + ns]), list(outs[at + ns:at + ns + nl])))
        at += ns + nl
    return flights


def _copies_wait(name, started, ncopy, plan):
    send, recv, srcs, lands = started
    ns, nl = len(srcs), len(lands)

    def body(*refs):
        send_ref, recv_ref, token = refs[ns + nl], refs[ns + nl + 1], refs[-1]
        copies = plan(refs[:ns], refs[ns:ns + nl])
        assert len(copies) == ncopy
        for k, (sent, _, to, landed) in enumerate(copies):
            cp = pltpu.make_async_remote_copy(src_ref=sent, dst_ref=landed, send_sem=send_ref.at[k],
                                              recv_sem=recv_ref.at[k], device_id=to, device_id_type=MESH)
            cp.wait_send()
            cp.wait_recv()
        token[...] = jnp.zeros_like(token)

    arrays = list(srcs) + list(lands)
    outs = pl.pallas_call(
        body, name=name, in_specs=[HBM_SPEC] * (ns + nl) + [SEM_SPEC] * 2 + [ANY_SPEC],
        out_specs=[HBM_SPEC] * (ns + nl) + [VMEM_SPEC], out_shape=[pltpu.HBM(a.shape, a.dtype) for a in arrays] + [TOKEN],
        input_output_aliases={i: i for i in range(ns + nl)},
        compiler_params=pltpu.CompilerParams(has_side_effects=EFFECT))(*arrays, send, recv, _Chain.last)
    _Chain.last = outs[-1]
    return list(outs[:ns]), list(outs[ns:-1])


def _plan_gather_chips(kinds):
    def plan(srcs, lands):
        x, y, c = _place()
        sibling, chips = _peers()
        out = []
        for t, kind in enumerate(kinds):
            mine = _window(lands[t], kind, 4 * x + 2 * y + c)
            out.append((srcs[t], mine, (x, y, c), mine))
            out.append((srcs[t], mine, sibling, _window(lands[t], kind, 4 * x + 2 * y + 1 - c)))
            for px, py in chips:
                out.append((srcs[t], mine, (px, py, c), _window(lands[t], kind, 4 * px + 2 * py + c)))
        return out
    return plan, 5 * len(kinds)


def _plan_gather_all(n):
    def plan(srcs, lands):
        x, y, c = _place()
        out = []
        for t in range(n):
            mine = lands[t].at[:, 4 * x + 2 * y + c]
            for m in range(N_DEV):
                px, py, pc = (1 - x if m & 4 else x), (1 - y if m & 2 else y), (1 - c if m & 1 else c)
                out.append((srcs[t], mine, (px, py, pc), lands[t].at[:, 4 * px + 2 * py + pc]))
        return out
    return plan, N_DEV * n


def _plan_gather_sibling(kinds):
    def plan(srcs, lands):
        _, _, c = _place()
        sibling, chips = _peers()
        out = []
        for t, kind in enumerate(kinds):
            for px, py in chips:
                w = _window(lands[t], kind, 4 * px + 2 * py + c)
                out.append((w, w, sibling, _window(lands[t], kind, 4 * px + 2 * py + 1 - c)))
        return out
    return plan, 3 * len(kinds)


def _plan_scatter_sibling(kinds):
    def plan(srcs, lands):
        _, _, c = _place()
        sibling, _ = _peers()
        out = []
        for t, kind in enumerate(kinds):
            for k in range(N_CHIP):
                out.append((_window(srcs[t], kind, 2 * k + 1 - c), lands[t].at[k], sibling, lands[t].at[k]))
        return out
    return plan, N_CHIP * len(kinds)


def _plan_scatter_chips(n):
    def plan(srcs, lands):
        x, y, c = _place()
        _, chips = _peers()
        out = []
        for t in range(n):
            for px, py in chips:
                out.append((srcs[t].at[2 * px + py], lands[t].at[2 * x + y], (px, py, c), lands[t].at[2 * px + py]))
        return out
    return plan, 3 * n


def _landing(shard, kind):
    if kind == "blocked":
        return lax.empty((shard.shape[0], N_DEV) + shard.shape[1:], shard.dtype)
    return lax.empty((shard.shape[0], N_DEV * shard.shape[1]), shard.dtype)


def _chip_sums(name, grads, kinds, recvs, c):
    n = len(grads)
    in_specs, out_specs, out_shape, args = [], [], [], []
    for gr, kind, rv in zip(grads, kinds, recvs):
        if kind == "blocked":
            rows, w = gr.shape[2], gr.shape[3]
            in_specs.append(pl.BlockSpec((None, None, rows, w), lambda k, cref: (0, 2 * k + cref[0], 0, 0)))
        else:
            rows, w = gr.shape[0], gr.shape[1] // N_DEV
            in_specs.append(pl.BlockSpec((rows, w), lambda k, cref: (0, 2 * k + cref[0])))
        blk = pl.BlockSpec((None, rows, w), lambda k, cref: (k, 0, 0))
        in_specs.append(blk)
        out_specs.append(blk)
        out_shape.append(jax.ShapeDtypeStruct((N_CHIP, rows, w), BF16))
        args += [gr, rv.reshape(N_CHIP, rows, w)]

    def body(*refs):
        for t in range(n):
            g_ref, r_ref, o_ref = refs[1 + 2 * t], refs[2 + 2 * t], refs[1 + 2 * n + t]
            o_ref[...] = (g_ref[...].astype(F32) + r_ref[...].astype(F32)).astype(BF16)

    return _pallas(body, name=name, n_prefetch=1, grid=(N_CHIP,), in_specs=in_specs, out_specs=out_specs,
                   out_shape=out_shape, params=_params(("parallel",)))(c, *args)


def _adamw_math(g, wv, mv, vv):
    m = ADAM_B1 * mv + (1.0 - ADAM_B1) * g
    v = ADAM_B2 * vv + (1.0 - ADAM_B2) * (g * g)
    m_hat = m / (1.0 - ADAM_B1 ** ADAM_STEP)
    v_hat = v / (1.0 - ADAM_B2 ** ADAM_STEP)
    delta = -ADAM_LR * (m_hat / (jnp.sqrt(v_hat) + ADAM_EPS) + ADAM_WD * wv)
    return delta, m, v


ADAM_STEPS = (4, 2)


def _adamw_group(name, items, chip_ids):
    n = len(items)
    in_specs, out_specs, out_shape, args, prevs = [], [], [], [chip_ids], []
    steps = next(s for s in ADAM_STEPS if all(item[2].shape[1] % (16 * s) == 0 for item in items))
    for own, recv, w3, m3, v3, layer, _ in items:
        nl, rows, w = w3.shape
        tr = rows // steps
        in_specs += [pl.BlockSpec((None, tr, w), lambda i, ids, slot=slot: (ids[slot], i, 0)) for slot in range(4)]
        slab = pl.BlockSpec((None, tr, w), lambda i, ids, layer=layer: (layer, i, 0))
        in_specs += [slab] * 3
        out_specs += [slab] * 4
        out_shape += [jax.ShapeDtypeStruct((nl, rows, w), F32)] * 4
        args += [own, recv, recv, recv, w3, m3, v3]
    aliases = {}
    for t, item in enumerate(items):
        if item[6] is not None:
            for k in range(4):
                aliases[len(args) + k] = 4 * t + k
            in_specs += [ANY_SPEC] * 4
            args += list(item[6])
            prevs.append(t)
    n_in = 1 + 7 * n + 4 * len(prevs)

    def body(*refs):
        for t in range(n):
            own_ref, r1_ref, r2_ref, r3_ref, w_ref, m_ref, v_ref = refs[1 + 7 * t:8 + 7 * t]
            g_ref, d_ref, nm_ref, nv_ref = refs[n_in + 4 * t:n_in + 4 * t + 4]
            g = ((own_ref[...].astype(F32) + r1_ref[...].astype(F32)) + r2_ref[...].astype(F32)) + r3_ref[...].astype(F32)
            g_ref[...] = g
            d_ref[...], nm_ref[...], nv_ref[...] = _adamw_math(g, w_ref[...], m_ref[...], v_ref[...])

    outs = _pallas(body, name=name, n_prefetch=1, grid=(steps,), in_specs=in_specs, out_specs=out_specs,
                   out_shape=out_shape, aliases=aliases, params=_params(("parallel",)))(*args)
    return [list(outs[4 * t:4 * t + 4]) for t in range(n)]


SHARD_ROWS = 8


def _adamw_small(gathered, ws, ms, vs, me):
    n = len(gathered)
    full = [w is not None for w in ws]
    sharded = [w is not None and w.ndim == 3 for w in ws]
    args = list(gathered)
    out_shape = []
    for t in range(n):
        shape = jax.ShapeDtypeStruct(ws[t].shape if sharded[t] else gathered[t].shape[2:], F32)
        if full[t]:
            args += [ws[t], ms[t], vs[t]]
            out_shape += [shape] * 4
        else:
            out_shape += [shape]

    def body(*refs):
        i_in, i_out = n, len(args) + 1
        me_ref = refs[len(args)]
        for t in range(n):
            p_ref = refs[t]
            if sharded[t]:
                w_ref, m_ref, v_ref = refs[i_in:i_in + 3]
                taps, layers, _ = w_ref.shape
                mine = pl.ds(pl.multiple_of(me_ref[0] * SHARD_ROWS, SHARD_ROWS), SHARD_ROWS)
                g = p_ref[0, 0, mine, :]
                for k in range(1, N_DEV):
                    g = g + p_ref[0, k, mine, :]
                for l in range(layers):
                    for k in range(taps):
                        at = (k, slice(l, l + 1), slice(None))
                        row = g[l * taps + k:l * taps + k + 1]
                        refs[i_out][at] = row
                        refs[i_out + 1][at], refs[i_out + 2][at], refs[i_out + 3][at] = _adamw_math(
                            row, w_ref[at], m_ref[at], v_ref[at])
                i_in += 3
                i_out += 4
                continue
            g = p_ref[0, 0]
            for k in range(1, N_DEV):
                g = g + p_ref[0, k]
            refs[i_out][...] = g
            if full[t]:
                w_ref, m_ref, v_ref = refs[i_in:i_in + 3]
                refs[i_out + 1][...], refs[i_out + 2][...], refs[i_out + 3][...] = _adamw_math(
                    g, w_ref[...], m_ref[...], v_ref[...])
                i_in += 3
                i_out += 4
            else:
                i_out += 1

    outs = _pallas(body, name="adamw_small",
                   in_specs=[VMEM_SPEC] * len(args) + [pl.BlockSpec(memory_space=pltpu.SMEM)],
                   out_specs=[VMEM_SPEC] * len(out_shape), out_shape=out_shape,
                   params=pltpu.CompilerParams(vmem_limit_bytes=VMEM_LIMIT))(*args, me)
    result, i = [], 0
    for t in range(n):
        k = 4 if full[t] else 1
        result.append(list(outs[i:i + k]))
        i += k
    return result


KIND = {"sc_w_in": "cols", "sc_w_out": "blocked", "w_dkv": "blocked", "w_kr": "cols", "w_uk": "cols", "w_uv": "cols",
        "w_dq": "blocked", "w_uq": "blocked", "w_o": "blocked", "ffn_w_up": "blocked", "ffn_w_down": "blocked",
        "conv": "blocked"}
GATHER_GROUPS = (("mixer", ("sc_w_in",)),
                 ("mixer2", ("sc_w_out", "conv")),
                 ("up0", ("ffn_w_up0",)),
                 ("down0", ("ffn_w_down0",)),
                 ("attn", ("w_dkv", "w_kr", "w_uk", "w_uv", "w_dq", "w_uq", "w_o")),
                 ("ffn1", ("ffn_w_up1", "ffn_w_down1")))
SCATTER_GROUPS = (("ffn1", (("ffn_w_up", 1), ("ffn_w_down", 1))),
                  ("attn", (("w_o", None), ("w_uq", None), ("w_dq", None), ("w_uk", None), ("w_uv", None),
                            ("w_dkv", None), ("w_kr", None))),
                  ("ffn0", (("ffn_w_up", 0), ("ffn_w_down", 0))),
                  ("mixer", (("sc_w_out", None), ("sc_w_in", None))))
SCHEDULE = {
    "begin": (("gather_start", "mixer"),),
    "cast": (("gather_start", "mixer2"),),
    "l0_norm": (("gather_forward", "mixer"), ("gather_forward", "mixer2"), ("gather_start", "up0")),
    "l0_out": (("gather_forward", "up0"), ("gather_start", "down0"), ("gather_start", "attn")),
    "f0_up": (("gather_forward", "down0"), ("gather_start", "ffn1")),
    "f0_down": (("gather_forward", "attn"),),
    "attn_fwd": (("gather_forward", "ffn1"),),
    "f1_gup": (("scatter_sibling", "ffn1"),),
    "f1_dhf": (("scatter_chips", "ffn1"),),
    "kv_bwd": (("scatter_sibling", "attn"),),
    "f0_dact": (("scatter_chips", "attn"),),
    "f0_gup": (("scatter_sibling", "ffn0"),),
    "f0_dhf": (("scatter_chips", "ffn0"),),
    "sc_bwd": (("scatter_sibling", "mixer"), ("scatter_done", "attn")),
    "d_l0_in": (("scatter_chips", "mixer"),),
}
FINISH = (("scatter_done", "ffn1"), ("scatter_done", "ffn0"), ("scatter_done", "mixer"))
STAGES = {"gather_start": 1, "gather_forward": 2, "gather_done": 3,
          "scatter_sibling": 1, "scatter_chips": 2, "scatter_done": 3}
SMALL_W_ROWS = 24


def _pack(arrays, rows):
    flat = jnp.concatenate([a.reshape(-1).astype(F32) for a in arrays])
    return jnp.pad(flat, (0, rows * 128 - flat.shape[0])).reshape(rows, 128)


def _cast_shards(items):
    arrays = []
    for a, _, _ in items:
        if not any(a is b for b in arrays):
            arrays.append(a)
    slot = [next(i for i, b in enumerate(arrays) if b is a) for a, _, _ in items]

    def body(*refs):
        for t, (a, layer, rows) in enumerate(items):
            w_ref, o_ref = refs[slot[t]], refs[len(arrays) + t]
            r, c = a.shape[-2:]
            if a.ndim == 3:
                o_ref[:, :r] = w_ref[(layer or 0):(layer or 0) + 1].astype(BF16)
                if rows > r:
                    o_ref[:, r:] = jnp.zeros((1, rows - r, c), BF16)
            else:
                o_ref[:r] = w_ref[...].astype(BF16)
                if rows > r:
                    o_ref[r:] = jnp.zeros((rows - r, c), BF16)

    out_shape = [jax.ShapeDtypeStruct(((1,) if a.ndim == 3 else ()) + (rows, a.shape[-1]), BF16)
                 for a, _, rows in items]
    return _pallas(body, name="cast_shards", in_specs=[VMEM_SPEC] * len(arrays), out_specs=[VMEM_SPEC] * len(items),
                   out_shape=out_shape, params=pltpu.CompilerParams(vmem_limit_bytes=VMEM_LIMIT))(*arrays)


STORED_TRANSPOSED = ("ffn_w_up", "w_uq", "w_kr")


def _stored(name, a):
    return jnp.swapaxes(a, -1, -2) if name in STORED_TRANSPOSED else a


def _base(name):
    if name.startswith("ffn_w_") and name[-1] in "01":
        return name[:-1], int(name[-1])
    return name, None


class _Exchange:
    def __init__(self, wts, mom, var, ffn_conv_b):
        self.wts, self.mom, self.var = wts, mom, var
        x, y, c = _place()
        self.c_arr = jnp.reshape(c, (1,)).astype(jnp.int32)
        chip = 2 * x + y
        self.chip_ids = jnp.stack([chip, chip ^ 1, chip ^ 2, chip ^ 3]).astype(jnp.int32)
        self.ready = {"ffn_cb0": ffn_conv_b.reshape(2, N_FF_BLK, 1, FF_BLK)[0],
                      "ffn_cb1": ffn_conv_b.reshape(2, N_FF_BLK, 1, FF_BLK)[1]}
        self.gathers, self.group_of = {}, {}
        self.grads, self.scatters, self.results, self.queue = {}, {}, {}, []
        for gname, names in GATHER_GROUPS:
            self.gathers[gname] = dict(stage=0, names=names, kinds=[KIND[_base(nm)[0]] for nm in names])
            for nm in names:
                self.group_of[nm] = gname
        for nm in ("sc_conv_w", "ffn_cw0", "ffn_cw1"):
            self.group_of[nm] = self.group_of["conv"]
        self.cast, self.f32 = {}, {}
        self.at("begin", None)
        later = [nm for gname, names in GATHER_GROUPS[1:] for nm in names if nm != "conv"]
        self.cast = dict(zip(later, _cast_shards([self._shard_f32(nm) for nm in later])))
        self.at("cast", None)

    def _shard_f32(self, name):
        base, layer = _base(name)
        if base not in self.f32:
            a = _stored(base, self.wts[base])
            self.f32[base] = a.reshape(a.shape[-2:]) if KIND[base] == "cols" else a.reshape((-1,) + a.shape[-2:])
        a = self.f32[base]
        return a, layer, {"w_kr": 128, "w_uq": QK_PAD}.get(base, a.shape[-2])

    def _shard(self, name):
        if name in self.cast:
            return self.cast[name]
        if name == "conv":
            return _pack([self.wts["sc_conv_w"], self.wts["ffn_conv_w"]], SMALL_W_ROWS).reshape(1, SMALL_W_ROWS, 128)
        base, layer = _base(name)
        a = _stored(base, self.wts[base])
        if layer is not None:
            a = a[layer:layer + 1]
        if KIND[base] == "cols":
            return a.reshape(a.shape[-2], a.shape[-1]).astype(BF16)
        return a.reshape((-1,) + a.shape[-2:]).astype(BF16)

    def _start(self, name, srcs, lands, ncopy, plan, st):
        self.queue.append((name, (srcs, lands, ncopy, plan), st))

    def _flush(self):
        if self.queue:
            flights = _copies_start("__".join(name for name, _, _ in self.queue), [job for _, job, _ in self.queue])
            for (_, _, st), flight in zip(self.queue, flights):
                st["flight"] = flight
            self.queue = []

    def _flight(self, st):
        self._flush()
        return st["flight"]

    def _gather_to(self, gname, stage, after):
        st = self.gathers[gname]
        if st["stage"] < 1 <= stage:
            shards = [self._shard(nm) for nm in st["names"]]
            lands = [_landing(s, kind) for s, kind in zip(shards, st["kinds"])]
            plan, ncopy = _plan_gather_chips(st["kinds"])
            self._start(f"ag_{gname}_chips", shards, lands, ncopy, plan, st)
            st["stage"] = 1
        if st["stage"] < 2 <= stage:
            plan, ncopy = _plan_gather_chips(st["kinds"])
            _, lands = _copies_wait(f"ag_{gname}_chips_wait", self._flight(st), ncopy, plan)
            plan, ncopy = _plan_gather_sibling(st["kinds"])
            self._start(f"ag_{gname}_sibling", [], lands, ncopy, plan, st)
            st["stage"] = 2
        if st["stage"] < 3 <= stage:
            plan, ncopy = _plan_gather_sibling(st["kinds"])
            _, lands = _copies_wait(f"ag_{gname}_sibling_wait", self._flight(st), ncopy, plan)
            for nm, land in zip(st["names"], lands):
                self._arrived(nm, land)
            st["stage"] = 3

    def _arrived(self, name, land):
        if name == "conv":
            conv = land.reshape(N_DEV, SMALL_W_ROWS * 128)
            self.ready["sc_conv_w"] = conv[:, :3 * 128].reshape(N_DEV, 3, 128).transpose(1, 0, 2).reshape(3, D)
            fcw = conv[:, 3 * 128:3 * 128 + 6 * 352].reshape(N_DEV, 2, 3, 352).transpose(1, 2, 0, 3)
            fcw = fcw.reshape(2, 3, N_FF_BLK, FF_BLK).transpose(0, 2, 1, 3)
            self.ready["ffn_cw0"], self.ready["ffn_cw1"] = fcw[0], fcw[1]
        elif name in ("sc_w_in", "w_uk", "w_uv", "w_kr") or name.startswith("ffn_w_up"):
            self.ready[name] = land
        elif name.startswith("ffn_w_down"):
            self.ready[name] = land.reshape(1, N_FF_BLK, FF_BLK, D)
        elif name == "w_uq":
            self.ready[name] = land.reshape(N_HEADS, QK_PAD, Q_LORA)
        else:
            self.ready[name] = land.reshape(D, land.shape[-1])

    def need(self, name, after):
        if name not in self.ready:
            self._gather_to(self.group_of[name], 3, after)
            self._flush()
        return self.ready[name]

    def grad(self, name, layer, array):
        self.grads[(name, layer)] = array

    def _scatter_to(self, gname, stage, after):
        keys = dict(SCATTER_GROUPS)[gname]
        st = self.scatters.setdefault(gname, dict(stage=0))
        kinds = [KIND[nm] for nm, _ in keys]
        if st["stage"] < 1 <= stage:
            grads = [self.grads[key] for key in keys]
            lands = []
            for gr, kind in zip(grads, kinds):
                shard = (gr.shape[0],) + gr.shape[2:] if kind == "blocked" else (gr.shape[0], gr.shape[1] // N_DEV)
                lands.append(lax.empty((N_CHIP,) + shard, BF16))
            plan, ncopy = _plan_scatter_sibling(kinds)
            self._start(f"rs_{gname}_sibling", grads, lands, ncopy, plan, st)
            st["stage"] = 1
        if st["stage"] < 2 <= stage:
            plan, ncopy = _plan_scatter_sibling(kinds)
            grads, recvs = _copies_wait(f"rs_{gname}_sibling_wait", self._flight(st), ncopy, plan)
            sums = _chip_sums(f"rs_{gname}_sums", grads, kinds, recvs, self.c_arr)
            lands = [lax.empty(s.shape, BF16) for s in sums]
            plan, ncopy = _plan_scatter_chips(len(sums))
            self._start(f"rs_{gname}_chips", sums, lands, ncopy, plan, st)
            st["stage"] = 2
        if st["stage"] < 3 <= stage:
            plan, ncopy = _plan_scatter_chips(len(keys))
            sums, recvs = _copies_wait(f"rs_{gname}_chips_wait", self._flight(st), ncopy, plan)
            items = []
            for (nm, layer), own, rv in zip(keys, sums, recvs):
                nl = 1 if layer is None else 2
                rows, w = own.shape[1], own.shape[2]
                w3, m3, v3 = (_stored(nm, src[nm]).reshape(nl, rows, w) for src in (self.wts, self.mom, self.var))
                items.append((own, rv, w3, m3, v3, 0 if layer is None else layer, self.results.get(nm)))
            outs = _adamw_group(f"adamw_{gname}", items, self.chip_ids)
            for (nm, _), out in zip(keys, outs):
                self.results[nm] = out
            st["stage"] = 3

    def at(self, place, after):
        for action, gname in SCHEDULE.get(place, ()):
            self._advance(action, gname, after)
        self._flush()

    def _advance(self, action, gname, after):
        if action.startswith("gather"):
            self._gather_to(gname, STAGES[action], after)
        else:
            self._scatter_to(gname, STAGES[action], after)

    def finish(self, after):
        for action, gname in FINISH:
            self._advance(action, gname, after)
        for gname, _ in SCATTER_GROUPS:
            self._scatter_to(gname, 3, after)
        return {nm: [_stored(nm, o.reshape(_stored(nm, self.wts[nm]).shape)) for o in outs]
                for nm, outs in self.results.items()}


REPLICATED = ("attn_norm", "ffn_norm", "final_norm", "kv_in_norm", "kv_latent_norm", "q_latent_norm", "ffn_conv_b")
WEIGHTS = ("attn_norm", "ffn_norm", "final_norm", "sc_w_in", "sc_conv_w", "sc_w_out", "kv_in_norm", "w_dkv",
           "kv_latent_norm", "w_kr", "w_uk", "w_uv", "w_dq", "q_latent_norm", "w_uq", "w_o", "ffn_w_up", "ffn_conv_w",
           "ffn_conv_b", "ffn_w_down")


def kernel(x, positions, attn_norm, ffn_norm, final_norm, sc_w_in, sc_conv_w, sc_w_out, kv_in_norm, w_dkv, kv_latent_norm, w_kr, w_uk, w_uv, w_dq, q_latent_norm, w_uq, w_o, ffn_w_up, ffn_conv_w, ffn_conv_b, ffn_w_down, loss_target, m_attn_norm, m_ffn_norm, m_final_norm, m_sc_w_in, m_sc_conv_w, m_sc_w_out, m_kv_in_norm, m_w_dkv, m_kv_latent_norm, m_w_kr, m_w_uk, m_w_uv, m_w_dq, m_q_latent_norm, m_w_uq, m_w_o, m_ffn_w_up, m_ffn_conv_w, m_ffn_conv_b, m_ffn_w_down, v_attn_norm, v_ffn_norm, v_final_norm, v_sc_w_in, v_sc_conv_w, v_sc_w_out, v_kv_in_norm, v_w_dkv, v_kv_latent_norm, v_w_kr, v_w_uk, v_w_uv, v_w_dq, v_q_latent_norm, v_w_uq, v_w_o, v_ffn_w_up, v_ffn_conv_w, v_ffn_conv_b, v_ffn_w_down):
    wts = dict(attn_norm=attn_norm, ffn_norm=ffn_norm, final_norm=final_norm, sc_w_in=sc_w_in, sc_conv_w=sc_conv_w,
               sc_w_out=sc_w_out, kv_in_norm=kv_in_norm, w_dkv=w_dkv, kv_latent_norm=kv_latent_norm, w_kr=w_kr,
               w_uk=w_uk, w_uv=w_uv, w_dq=w_dq, q_latent_norm=q_latent_norm, w_uq=w_uq, w_o=w_o, ffn_w_up=ffn_w_up,
               ffn_conv_w=ffn_conv_w, ffn_conv_b=ffn_conv_b, ffn_w_down=ffn_w_down)
    mom = dict(attn_norm=m_attn_norm, ffn_norm=m_ffn_norm, final_norm=m_final_norm, sc_w_in=m_sc_w_in,
               sc_conv_w=m_sc_conv_w, sc_w_out=m_sc_w_out, kv_in_norm=m_kv_in_norm, w_dkv=m_w_dkv,
               kv_latent_norm=m_kv_latent_norm, w_kr=m_w_kr, w_uk=m_w_uk, w_uv=m_w_uv, w_dq=m_w_dq,
               q_latent_norm=m_q_latent_norm, w_uq=m_w_uq, w_o=m_w_o, ffn_w_up=m_ffn_w_up, ffn_conv_w=m_ffn_conv_w,
               ffn_conv_b=m_ffn_conv_b, ffn_w_down=m_ffn_w_down)
    var = dict(attn_norm=v_attn_norm, ffn_norm=v_ffn_norm, final_norm=v_final_norm, sc_w_in=v_sc_w_in,
               sc_conv_w=v_sc_conv_w, sc_w_out=v_sc_w_out, kv_in_norm=v_kv_in_norm, w_dkv=v_w_dkv,
               kv_latent_norm=v_kv_latent_norm, w_kr=v_w_kr, w_uk=v_w_uk, w_uv=v_w_uv, w_dq=v_w_dq,
               q_latent_norm=v_q_latent_norm, w_uq=v_w_uq, w_o=v_w_o, ffn_w_up=v_ffn_w_up, ffn_conv_w=v_ffn_conv_w,
               ffn_conv_b=v_ffn_conv_b, ffn_w_down=v_ffn_w_down)
    xi, yi, ci = _place()
    me = 4 * xi + 2 * yi + ci
    _Chain.last = None

    ex = _Exchange(wts, mom, var, ffn_conv_b)
    rep = {
        "attn_norm": attn_norm, "ffn_norm": ffn_norm, "final_norm": final_norm,
        "kv_in_norm": kv_in_norm.reshape(1, D), "kv_latent_norm": kv_latent_norm.reshape(1, KV_LORA),
        "q_latent_norm": q_latent_norm.reshape(1, Q_LORA),
    }
    loss, grad_x, small = _local_step(x.reshape(T, D), positions.reshape(T, 1), loss_target.reshape(T, D), rep, ex)

    def rows_of(a):
        return a.reshape(-1, a.shape[-1])

    def device_rows(a):
        taps, c = a.shape[-2], a.shape[-1] // N_DEV
        rows = a.reshape(-1, taps, N_DEV, c).transpose(2, 0, 1, 3).reshape(N_DEV, -1, c)
        return jnp.pad(rows, ((0, 0), (0, SHARD_ROWS - rows.shape[1]), (0, 0))).reshape(N_DEV * SHARD_ROWS, c)

    def taps_first(a):
        return jnp.transpose(a, (1, 0, 2))

    sharded = ("sc_conv_w", "ffn_conv_w")
    shards = ([loss.reshape(1, 1, 128)] + [rows_of(small[nm])[None] for nm in REPLICATED]
              + [device_rows(small[nm])[None] for nm in sharded])
    plan, ncopy = _plan_gather_all(len(shards))
    flight, = _copies_start("ag_small", [(shards, [lax.empty((1, N_DEV) + s.shape[1:], F32) for s in shards], ncopy, plan)])
    results = ex.finish(grad_x)
    _, gathered = _copies_wait("ag_small_wait", flight, ncopy, plan)
    params = [[None] + [rows_of(src[nm]) for nm in REPLICATED] + [taps_first(src[nm]) for nm in sharded]
              for src in (wts, mom, var)]
    summed = _adamw_small(gathered, *params, me.astype(jnp.int32).reshape(1))
    loss_total = summed[0][0][0, 0]
    for nm, vals in zip(REPLICATED, summed[1:1 + len(REPLICATED)]):
        results[nm] = [a.reshape(wts[nm].shape) for a in vals]
    for nm, vals in zip(sharded, summed[1 + len(REPLICATED):]):
        results[nm] = [taps_first(a) for a in vals]

    outs = [loss_total, grad_x.reshape(1, T, D)]
    for slot in range(4):
        outs.extend(results[nm][slot] for nm in WEIGHTS)
    return tuple(outs)
```
